```python
import math
import jax, jax.numpy as jnp
from jax import lax
import numpy as np

D_MODEL = 2048
BATCH = 8
SEQ = 4096
DEPTH = 1

GRID_W = 64
CTX_LEN = 256
HEAD_DIM = 128
N_HEADS = D_MODEL // HEAD_DIM
HEADS_A = N_HEADS // 2
HEADS_B = N_HEADS - HEADS_A
KV_A = 2
KV_B = 2
WINDOW = 128
BLOCK = 128
FFN_HIDDEN = -(-8 * D_MODEL // 768) * 256
ROPE_THETA = 10000.0
EPS = 1e-6
ATTN_SCALE = HEAD_DIM ** -0.5
DN_ALPHA = (2.0 * DEPTH) ** 0.25
DN_BETA = (8.0 * DEPTH) ** -0.25

QA_W = HEADS_A * HEAD_DIM
KA_W = KV_A * HEAD_DIM
QB_W = HEADS_B * HEAD_DIM
KB_W = KV_B * HEAD_DIM
IN_WIDTH = QA_W + 2 * KA_W + QB_W + 2 * KB_W
IN_SPLITS = [QA_W, QA_W + KA_W, QA_W + 2 * KA_W, QA_W + 2 * KA_W + QB_W, QA_W + 2 * KA_W + QB_W + KB_W]
MIX_WIDTH = QA_W + QB_W

kernel_name = 'hybrid_window_sink_global_qknorm_dit_layer'


def layer_norm(x, g, b):
    xf = x.astype(jnp.float32)
    mu = jnp.mean(xf, axis=-1, keepdims=True)
    var = jnp.mean(jnp.square(xf - mu), axis=-1, keepdims=True)
    return ((xf - mu) * lax.rsqrt(var + EPS) * g + b).astype(x.dtype)


def rms_norm(x, g):
    xf = x.astype(jnp.float32)
    return (xf * lax.rsqrt(jnp.mean(jnp.square(xf), axis=-1, keepdims=True) + EPS) * g).astype(x.dtype)


def axial_rope_tables(rows):
    row_ids = jnp.repeat(jnp.arange(rows, dtype=jnp.float32), GRID_W)
    col_ids = jnp.tile(jnp.arange(GRID_W, dtype=jnp.float32), rows)
    axis_dim = HEAD_DIM // 2
    inv_freq = jnp.power(ROPE_THETA, -jnp.arange(0, axis_dim, 2, dtype=jnp.float32) / axis_dim)
    ang_r = row_ids[:, None] * inv_freq
    ang_c = col_ids[:, None] * inv_freq
    ang = jnp.concatenate([ang_r, ang_r, ang_c, ang_c], axis=-1)
    return jnp.cos(ang)[:, None, :], jnp.sin(ang)[:, None, :]


def rotate_half(x):
    x1, x2 = jnp.split(x, 2, axis=-1)
    return jnp.concatenate([-x2, x1], axis=-1)


def apply_axial_rope(x, cos, sin):
    xf = x.astype(jnp.float32)
    x_row, x_col = jnp.split(xf, 2, axis=-1)
    rot = jnp.concatenate([rotate_half(x_row), rotate_half(x_col)], axis=-1)
    return (xf * cos + rot * sin).astype(x.dtype)


def ada_mods(cond, w_ada, b_ada):
    m = jnp.einsum('bd,de->be', jax.nn.silu(cond), w_ada) + b_ada
    return [t[:, None, :] for t in jnp.split(m, 6, axis=-1)]


def modulate(x, shift, scale):
    return x * (1.0 + scale) + shift


def mixer_qkv(u, w_in, q_norm_g, k_norm_g, rope):
    B, N, _ = u.shape
    h = jnp.einsum('bnd,de->bne', u, w_in)
    qa, ka, va, qb, kb, vb = jnp.split(h, IN_SPLITS, axis=-1)
    heads = lambda t, n: t.reshape(B, N, n, HEAD_DIM)
    qa, ka, va = heads(qa, HEADS_A), heads(ka, KV_A), heads(va, KV_A)
    qb = rms_norm(heads(qb, HEADS_B), q_norm_g)
    kb = rms_norm(heads(kb, KV_B), k_norm_g)
    vb = heads(vb, KV_B)
    if rope is not None:
        cos, sin = rope
        qa, ka, qb, kb = (apply_axial_rope(t, cos, sin) for t in (qa, ka, qb, kb))
    return qa, ka, va, qb, kb, vb


def window_sink_attention(q, k, v, k_ctx, v_ctx, sink_logit):
    B, N = q.shape[0], q.shape[1]
    nb = N // BLOCK
    G = HEADS_A // KV_A
    qb = q.reshape(B, nb, BLOCK, KV_A, G, HEAD_DIM)
    pad = ((0, 0), (BLOCK, BLOCK), (0, 0), (0, 0))
    kp = jnp.pad(k, pad).reshape(B, nb + 2, BLOCK, KV_A, HEAD_DIM)
    vp = jnp.pad(v, pad).reshape(B, nb + 2, BLOCK, KV_A, HEAD_DIM)
    band = lambda t: jnp.concatenate([t[:, :-2], t[:, 1:-1], t[:, 2:]], axis=2)
    kb, vb = band(kp), band(vp)
    s_loc = jnp.einsum('bnqkgd,bnskd->bnkgqs', qb, kb, preferred_element_type=jnp.float32) * ATTN_SCALE
    blk = jnp.arange(nb)[:, None] * BLOCK
    qpos = blk + jnp.arange(BLOCK)[None, :]
    kpos = blk - BLOCK + jnp.arange(3 * BLOCK)[None, :]
    valid = ((jnp.abs(qpos[:, :, None] - kpos[:, None, :]) <= WINDOW)
             & (kpos[:, None, :] >= 0) & (kpos[:, None, :] < N))
    s_loc = jnp.where(valid[None, :, None, None, :, :], s_loc, -jnp.inf)
    s_ctx = jnp.einsum('bnqkgd,bckd->bnkgqc', qb, k_ctx, preferred_element_type=jnp.float32) * ATTN_SCALE
    sink_col = jnp.broadcast_to(sink_logit.reshape(KV_A, G)[None, None, :, :, None, None].astype(jnp.float32),
                                s_loc.shape[:-1] + (1,))
    p = jax.nn.softmax(jnp.concatenate([s_loc, s_ctx, sink_col], axis=-1), axis=-1)
    p_loc = p[..., :3 * BLOCK].astype(v.dtype)
    p_ctx = p[..., 3 * BLOCK:-1].astype(v.dtype)
    o = (jnp.einsum('bnkgqs,bnskd->bnqkgd', p_loc, vb)
         + jnp.einsum('bnkgqc,bckd->bnqkgd', p_ctx, v_ctx))
    return o.reshape(B, N, HEADS_A * HEAD_DIM)


def global_attention(q, k, v, k_ctx, v_ctx):
    B, N = q.shape[0], q.shape[1]
    nb = N // BLOCK
    G = HEADS_B // KV_B
    keys = jnp.concatenate([k, k_ctx], axis=1)
    vals = jnp.concatenate([v, v_ctx], axis=1)
    qb = q.reshape(B, nb, BLOCK, KV_B, G, HEAD_DIM).transpose(1, 0, 2, 3, 4, 5)

    def one_block(q_blk):
        s = jnp.einsum('bqkgd,bskd->bkgqs', q_blk, keys, preferred_element_type=jnp.float32) * ATTN_SCALE
        p = jax.nn.softmax(s, axis=-1).astype(vals.dtype)
        return jnp.einsum('bkgqs,bskd->bqkgd', p, vals)

    o = lax.map(one_block, qb)
    return o.transpose(1, 0, 2, 3, 4, 5).reshape(B, N, HEADS_B * HEAD_DIM)


def context_attention(q, k, v, sink_logit=None):
    B, C, H = q.shape[0], q.shape[1], q.shape[2]
    KV = k.shape[2]
    G = H // KV
    qg = q.reshape(B, C, KV, G, HEAD_DIM)
    s = jnp.einsum('bqkgd,bskd->bkgqs', qg, k, preferred_element_type=jnp.float32) * ATTN_SCALE
    if sink_logit is not None:
        col = jnp.broadcast_to(sink_logit.reshape(KV, G)[None, :, :, None, None].astype(jnp.float32),
                               s.shape[:-1] + (1,))
        s = jnp.concatenate([s, col], axis=-1)
    p = jax.nn.softmax(s, axis=-1)[..., :C].astype(v.dtype)
    o = jnp.einsum('bkgqs,bskd->bqkgd', p, v)
    return o.reshape(B, C, H * HEAD_DIM)


def swiglu(u, w_gate, w_up, w_down):
    return (jax.nn.silu(u @ w_gate) * (u @ w_up)) @ w_down


def _fwd_setup_inputs(seed: int = 0) -> dict:
    key = jax.random.key(seed)
    ks = jax.random.split(key, 20)
    f32 = jnp.float32
    nrm = lambda k, shape, s: jax.random.normal(k, shape, f32) * s
    D, F, L = D_MODEL, FFN_HIDDEN, DEPTH
    return {
        'x': nrm(ks[0], (BATCH, SEQ, D), 1.0),
        'c': nrm(ks[1], (BATCH, D), 1.0),
        'ctx': nrm(ks[2], (BATCH, CTX_LEN, D), 1.0),
        'c_ctx': nrm(ks[3], (D,), 1.0),
        'w_ada': nrm(ks[4], (L, D, 6 * D), 0.3 * D ** -0.5),
        'b_ada': nrm(ks[5], (L, 6 * D), 0.02),
        'w_in': nrm(ks[6], (L, D, IN_WIDTH), D ** -0.5),
        'q_norm_g': 1.0 + nrm(ks[7], (L, HEAD_DIM), 0.02),
        'k_norm_g': 1.0 + nrm(ks[8], (L, HEAD_DIM), 0.02),
        'sink_logit': nrm(ks[9], (L, HEADS_A), 0.5),
        'w_out': nrm(ks[10], (L, MIX_WIDTH, D), DN_BETA * MIX_WIDTH ** -0.5),
        'ln1_g': 1.0 + nrm(ks[11], (L, D), 0.02),
        'ln1_b': nrm(ks[12], (L, D), 0.02),
        'w_gate': nrm(ks[13], (L, D, F), D ** -0.5),
        'w_up': nrm(ks[14], (L, D, F), D ** -0.5),
        'w_down': nrm(ks[15], (L, F, D), DN_BETA * F ** -0.5),
        'ln2_g': 1.0 + nrm(ks[16], (L, D), 0.02),
        'ln2_b': nrm(ks[17], (L, D), 0.02),
    }


def _fwd_reference(x, c, ctx, c_ctx, w_ada, b_ada, w_in, q_norm_g, k_norm_g, sink_logit,
              w_out, ln1_g, ln1_b, w_gate, w_up, w_down, ln2_g, ln2_b):
    ROWS = x.shape[1] // GRID_W
    rope = axial_rope_tables(ROWS)
    for layer in range(DEPTH):
        sh1, sc1, g1, sh2, sc2, g2 = ada_mods(c, w_ada[layer], b_ada[layer])
        csh1, csc1, cg1, csh2, csc2, cg2 = ada_mods(c_ctx[None, :], w_ada[layer], b_ada[layer])

        qa, ka, va, qb, kb, vb = mixer_qkv(modulate(x, sh1, sc1), w_in[layer],
                                           q_norm_g[layer], k_norm_g[layer], rope)
        qac, kac, vac, qbc, kbc, vbc = mixer_qkv(modulate(ctx, csh1, csc1), w_in[layer],
                                                 q_norm_g[layer], k_norm_g[layer], None)
        heads = jnp.concatenate([window_sink_attention(qa, ka, va, kac, vac, sink_logit[layer]),
                                 global_attention(qb, kb, vb, kbc, vbc)], axis=-1)
        x = layer_norm(DN_ALPHA * x + g1 * (heads @ w_out[layer]), ln1_g[layer], ln1_b[layer])

        x = layer_norm(DN_ALPHA * x + g2 * swiglu(modulate(x, sh2, sc2), w_gate[layer], w_up[layer], w_down[layer]),
                       ln2_g[layer], ln2_b[layer])

        if layer < DEPTH - 1:
            heads_c = jnp.concatenate([context_attention(qac, kac, vac, sink_logit[layer]),
                                       context_attention(qbc, kbc, vbc)], axis=-1)
            ctx = layer_norm(DN_ALPHA * ctx + cg1 * (heads_c @ w_out[layer]), ln1_g[layer], ln1_b[layer])
            ctx = layer_norm(DN_ALPHA * ctx + cg2 * swiglu(modulate(ctx, csh2, csc2), w_gate[layer], w_up[layer], w_down[layer]),
                             ln2_g[layer], ln2_b[layer])
    return x


import jax as _jax
import jax.numpy as _jnp

TWIN_FORMAT = 'train_step'
FWD_PARAMS = ['x', 'c', 'ctx', 'c_ctx', 'w_ada', 'b_ada', 'w_in', 'q_norm_g', 'k_norm_g', 'sink_logit', 'w_out', 'ln1_g', 'ln1_b', 'w_gate', 'w_up', 'w_down', 'ln2_g', 'ln2_b']
TWIN_WEIGHTS = ['c_ctx', 'w_ada', 'b_ada', 'w_in', 'q_norm_g', 'k_norm_g', 'sink_logit', 'w_out', 'ln1_g', 'ln1_b', 'w_gate', 'w_up', 'w_down', 'ln2_g', 'ln2_b']
TWIN_DIFF_INPUT = 'x'
TWIN_INPUTS = ['x', 'c', 'ctx', 'c_ctx', 'w_ada', 'b_ada', 'w_in', 'q_norm_g', 'k_norm_g', 'sink_logit', 'w_out', 'ln1_g', 'ln1_b', 'w_gate', 'w_up', 'w_down', 'ln2_g', 'ln2_b', 'loss_target', 'm_c_ctx', 'm_w_ada', 'm_b_ada', 'm_w_in', 'm_q_norm_g', 'm_k_norm_g', 'm_sink_logit', 'm_w_out', 'm_ln1_g', 'm_ln1_b', 'm_w_gate', 'm_w_up', 'm_w_down', 'm_ln2_g', 'm_ln2_b', 'v_c_ctx', 'v_w_ada', 'v_b_ada', 'v_w_in', 'v_q_norm_g', 'v_k_norm_g', 'v_sink_logit', 'v_w_out', 'v_ln1_g', 'v_ln1_b', 'v_w_gate', 'v_w_up', 'v_w_down', 'v_ln2_g', 'v_ln2_b']
TWIN_OUTPUTS = ['loss', 'grad_x', 'grad_c_ctx', 'grad_w_ada', 'grad_b_ada', 'grad_w_in', 'grad_q_norm_g', 'grad_k_norm_g', 'grad_sink_logit', 'grad_w_out', 'grad_ln1_g', 'grad_ln1_b', 'grad_w_gate', 'grad_w_up', 'grad_w_down', 'grad_ln2_g', 'grad_ln2_b', 'delta_c_ctx', 'delta_w_ada', 'delta_b_ada', 'delta_w_in', 'delta_q_norm_g', 'delta_k_norm_g', 'delta_sink_logit', 'delta_w_out', 'delta_ln1_g', 'delta_ln1_b', 'delta_w_gate', 'delta_w_up', 'delta_w_down', 'delta_ln2_g', 'delta_ln2_b', 'new_m_c_ctx', 'new_m_w_ada', 'new_m_b_ada', 'new_m_w_in', 'new_m_q_norm_g', 'new_m_k_norm_g', 'new_m_sink_logit', 'new_m_w_out', 'new_m_ln1_g', 'new_m_ln1_b', 'new_m_w_gate', 'new_m_w_up', 'new_m_w_down', 'new_m_ln2_g', 'new_m_ln2_b', 'new_v_c_ctx', 'new_v_w_ada', 'new_v_b_ada', 'new_v_w_in', 'new_v_q_norm_g', 'new_v_k_norm_g', 'new_v_sink_logit', 'new_v_w_out', 'new_v_ln1_g', 'new_v_ln1_b', 'new_v_w_gate', 'new_v_w_up', 'new_v_w_down', 'new_v_ln2_g', 'new_v_ln2_b']
TWIN_LEAF_KINDS = {'loss': 'loss', 'grad_x': 'grad_x', 'grad_c_ctx': 'grad_w', 'grad_w_ada': 'grad_w', 'grad_b_ada': 'grad_w', 'grad_w_in': 'grad_w', 'grad_q_norm_g': 'grad_w', 'grad_k_norm_g': 'grad_w', 'grad_sink_logit': 'grad_w', 'grad_w_out': 'grad_w', 'grad_ln1_g': 'grad_w', 'grad_ln1_b': 'grad_w', 'grad_w_gate': 'grad_w', 'grad_w_up': 'grad_w', 'grad_w_down': 'grad_w', 'grad_ln2_g': 'grad_w', 'grad_ln2_b': 'grad_w', 'delta_c_ctx': 'delta_w', 'delta_w_ada': 'delta_w', 'delta_b_ada': 'delta_w', 'delta_w_in': 'delta_w', 'delta_q_norm_g': 'delta_w', 'delta_k_norm_g': 'delta_w', 'delta_sink_logit': 'delta_w', 'delta_w_out': 'delta_w', 'delta_ln1_g': 'delta_w', 'delta_ln1_b': 'delta_w', 'delta_w_gate': 'delta_w', 'delta_w_up': 'delta_w', 'delta_w_down': 'delta_w', 'delta_ln2_g': 'delta_w', 'delta_ln2_b': 'delta_w', 'new_m_c_ctx': 'new_m', 'new_m_w_ada': 'new_m', 'new_m_b_ada': 'new_m', 'new_m_w_in': 'new_m', 'new_m_q_norm_g': 'new_m', 'new_m_k_norm_g': 'new_m', 'new_m_sink_logit': 'new_m', 'new_m_w_out': 'new_m', 'new_m_ln1_g': 'new_m', 'new_m_ln1_b': 'new_m', 'new_m_w_gate': 'new_m', 'new_m_w_up': 'new_m', 'new_m_w_down': 'new_m', 'new_m_ln2_g': 'new_m', 'new_m_ln2_b': 'new_m', 'new_v_c_ctx': 'new_v', 'new_v_w_ada': 'new_v', 'new_v_b_ada': 'new_v', 'new_v_w_in': 'new_v', 'new_v_q_norm_g': 'new_v', 'new_v_k_norm_g': 'new_v', 'new_v_sink_logit': 'new_v', 'new_v_w_out': 'new_v', 'new_v_ln1_g': 'new_v', 'new_v_ln1_b': 'new_v', 'new_v_w_gate': 'new_v', 'new_v_w_up': 'new_v', 'new_v_w_down': 'new_v', 'new_v_ln2_g': 'new_v', 'new_v_ln2_b': 'new_v'}


def _forward(args):
    return _fwd_reference(*[args[k] for k in FWD_PARAMS])


def _output_shape():
    def fwd():
        inp = _fwd_setup_inputs(0)
        return _fwd_reference(*[inp[k] for k in FWD_PARAMS])
    out = _jax.eval_shape(fwd)
    return out.shape, out.dtype

N_MICROBATCH = 1
ADAM_LR = 0.001
ADAM_B1 = 0.9
ADAM_B2 = 0.999
ADAM_EPS = 1e-08
ADAM_WD = 0.01
ADAM_STEP = 10
PER_EXAMPLE_BATCH_AXIS = {'x': 0, 'c': 0, 'ctx': 0, 'loss_target': 0}
SHARED_INPUTS = []
_WEIGHT_DTYPES = {'c_ctx': _jnp.float32, 'w_ada': _jnp.float32, 'b_ada': _jnp.float32, 'w_in': _jnp.float32, 'q_norm_g': _jnp.float32, 'k_norm_g': _jnp.float32, 'sink_logit': _jnp.float32, 'w_out': _jnp.float32, 'ln1_g': _jnp.float32, 'ln1_b': _jnp.float32, 'w_gate': _jnp.float32, 'w_up': _jnp.float32, 'w_down': _jnp.float32, 'ln2_g': _jnp.float32, 'ln2_b': _jnp.float32}
MOMENT_SCALE = {'c_ctx': 8.573894e-04, 'w_ada': 7.990202e-03, 'b_ada': 1.427927e-02, 'w_in': 1.451515e-03, 'q_norm_g': 1.560728e-03, 'k_norm_g': 1.500226e-03, 'sink_logit': 4.260807e-05, 'w_out': 2.882501e-03, 'ln1_g': 5.783016e-01, 'ln1_b': 2.747483e-01, 'w_gate': 3.172712e-03, 'w_up': 3.080433e-03, 'w_down': 8.586742e-03, 'ln2_g': 1.600352e+01, 'ln2_b': 3.936852e-01}


def _to_microbatches(a, axis):
    t = _jnp.moveaxis(a, axis, 0)
    t = t.reshape((N_MICROBATCH, t.shape[0] // N_MICROBATCH) + t.shape[1:])
    return _jnp.moveaxis(t, 1, axis + 1)


def setup_inputs(seed: int = 0) -> dict:
    inp = _fwd_setup_inputs(seed)
    key = _jax.random.fold_in(_jax.random.key(seed), 7919)
    shape, _ = _output_shape()
    out = dict(inp)
    out["loss_target"] = _jax.random.normal(_jax.random.fold_in(key, 0), shape, _jnp.float32)
    for i, name in enumerate(TWIN_WEIGHTS):
        w = inp[name].astype(_jnp.float32)
        if MOMENT_SCALE is None:
            s = _jnp.sqrt(_jnp.mean(_jnp.square(w)) + 1e-30)
        else:
            s = MOMENT_SCALE[name]
        km, kv = _jax.random.split(_jax.random.fold_in(key, i + 1))
        out[name] = w
        out["m_" + name] = s * _jax.random.normal(km, w.shape, _jnp.float32)
        out["v_" + name] = (s * s) * _jax.random.uniform(kv, w.shape, _jnp.float32, 0.5, 1.5)
    if N_MICROBATCH > 1:
        for name, axis in PER_EXAMPLE_BATCH_AXIS.items():
            out[name] = _to_microbatches(out[name], axis)
    return {'x': out['x'], 'c': out['c'], 'ctx': out['ctx'], 'c_ctx': out['c_ctx'], 'w_ada': out['w_ada'], 'b_ada': out['b_ada'], 'w_in': out['w_in'], 'q_norm_g': out['q_norm_g'], 'k_norm_g': out['k_norm_g'], 'sink_logit': out['sink_logit'], 'w_out': out['w_out'], 'ln1_g': out['ln1_g'], 'ln1_b': out['ln1_b'], 'w_gate': out['w_gate'], 'w_up': out['w_up'], 'w_down': out['w_down'], 'ln2_g': out['ln2_g'], 'ln2_b': out['ln2_b'], 'loss_target': out['loss_target'], 'm_c_ctx': out['m_c_ctx'], 'm_w_ada': out['m_w_ada'], 'm_b_ada': out['m_b_ada'], 'm_w_in': out['m_w_in'], 'm_q_norm_g': out['m_q_norm_g'], 'm_k_norm_g': out['m_k_norm_g'], 'm_sink_logit': out['m_sink_logit'], 'm_w_out': out['m_w_out'], 'm_ln1_g': out['m_ln1_g'], 'm_ln1_b': out['m_ln1_b'], 'm_w_gate': out['m_w_gate'], 'm_w_up': out['m_w_up'], 'm_w_down': out['m_w_down'], 'm_ln2_g': out['m_ln2_g'], 'm_ln2_b': out['m_ln2_b'], 'v_c_ctx': out['v_c_ctx'], 'v_w_ada': out['v_w_ada'], 'v_b_ada': out['v_b_ada'], 'v_w_in': out['v_w_in'], 'v_q_norm_g': out['v_q_norm_g'], 'v_k_norm_g': out['v_k_norm_g'], 'v_sink_logit': out['v_sink_logit'], 'v_w_out': out['v_w_out'], 'v_ln1_g': out['v_ln1_g'], 'v_ln1_b': out['v_ln1_b'], 'v_w_gate': out['v_w_gate'], 'v_w_up': out['v_w_up'], 'v_w_down': out['v_w_down'], 'v_ln2_g': out['v_ln2_g'], 'v_ln2_b': out['v_ln2_b']}


def _loss(weights, diff, rest, loss_target):
    with _jax.named_scope("forward"):
        args = {**rest, TWIN_DIFF_INPUT: diff, **{k: w.astype(_WEIGHT_DTYPES[k]) for k, w in weights.items()}}
        y = _forward(args)
    with _jax.named_scope("loss_head"):
        err = _jnp.square(y.astype(_jnp.float32) - loss_target)
        return 0.5 * _jnp.sum(_jnp.mean(err, axis=-1)) if err.ndim else 0.5 * err


def _adamw(w, g, m, v):
    m = ADAM_B1 * m + (1.0 - ADAM_B1) * g
    v = ADAM_B2 * v + (1.0 - ADAM_B2) * _jnp.square(g)
    m_hat = m / (1.0 - ADAM_B1 ** ADAM_STEP)
    v_hat = v / (1.0 - ADAM_B2 ** ADAM_STEP)
    delta = -ADAM_LR * (m_hat / (_jnp.sqrt(v_hat) + ADAM_EPS) + ADAM_WD * w)
    return delta, m, v


def reference(x, c, ctx, c_ctx, w_ada, b_ada, w_in, q_norm_g, k_norm_g, sink_logit, w_out, ln1_g, ln1_b, w_gate, w_up, w_down, ln2_g, ln2_b, loss_target, m_c_ctx, m_w_ada, m_b_ada, m_w_in, m_q_norm_g, m_k_norm_g, m_sink_logit, m_w_out, m_ln1_g, m_ln1_b, m_w_gate, m_w_up, m_w_down, m_ln2_g, m_ln2_b, v_c_ctx, v_w_ada, v_b_ada, v_w_in, v_q_norm_g, v_k_norm_g, v_sink_logit, v_w_out, v_ln1_g, v_ln1_b, v_w_gate, v_w_up, v_w_down, v_ln2_g, v_ln2_b):
    given = dict(x=x, c=c, ctx=ctx, c_ctx=c_ctx, w_ada=w_ada, b_ada=b_ada, w_in=w_in, q_norm_g=q_norm_g, k_norm_g=k_norm_g, sink_logit=sink_logit, w_out=w_out, ln1_g=ln1_g, ln1_b=ln1_b, w_gate=w_gate, w_up=w_up, w_down=w_down, ln2_g=ln2_g, ln2_b=ln2_b, loss_target=loss_target, m_c_ctx=m_c_ctx, m_w_ada=m_w_ada, m_b_ada=m_b_ada, m_w_in=m_w_in, m_q_norm_g=m_q_norm_g, m_k_norm_g=m_k_norm_g, m_sink_logit=m_sink_logit, m_w_out=m_w_out, m_ln1_g=m_ln1_g, m_ln1_b=m_ln1_b, m_w_gate=m_w_gate, m_w_up=m_w_up, m_w_down=m_w_down, m_ln2_g=m_ln2_g, m_ln2_b=m_ln2_b, v_c_ctx=v_c_ctx, v_w_ada=v_w_ada, v_b_ada=v_b_ada, v_w_in=v_w_in, v_q_norm_g=v_q_norm_g, v_k_norm_g=v_k_norm_g, v_sink_logit=v_sink_logit, v_w_out=v_w_out, v_ln1_g=v_ln1_g, v_ln1_b=v_ln1_b, v_w_gate=v_w_gate, v_w_up=v_w_up, v_w_down=v_w_down, v_ln2_g=v_ln2_g, v_ln2_b=v_ln2_b)
    weights = {n: given[n] for n in TWIN_WEIGHTS}
    shared = {n: given[n] for n in SHARED_INPUTS}
    per_example = {n: given[n] for n in ['x', 'c', 'ctx']}
    grad_fn = _jax.value_and_grad(_loss, argnums=(0, 1))

    def one_microbatch(ex, loss_target):
        ex = dict(ex)
        diff = ex.pop(TWIN_DIFF_INPUT)
        return grad_fn(weights, diff, {**shared, **ex}, loss_target)

    if N_MICROBATCH == 1:
        loss, (grad_w, grad_x) = one_microbatch(per_example, given["loss_target"])
    else:
        def body(carry, xs):
            loss_sum, grad_sum = carry
            l_k, (gw_k, gx_k) = one_microbatch(xs[0], xs[1])
            with _jax.named_scope("update"):
                return (loss_sum + l_k, _jax.tree.map(_jnp.add, grad_sum, gw_k)), gx_k

        init = (_jnp.zeros((), _jnp.float32), _jax.tree.map(_jnp.zeros_like, weights))
        (loss, grad_w), grad_x = _jax.lax.scan(body, init, (per_example, given["loss_target"]))
    with _jax.named_scope("update"):
        delta_w, new_m, new_v = {}, {}, {}
        for n in TWIN_WEIGHTS:
            delta_w[n], new_m[n], new_v[n] = _adamw(weights[n], grad_w[n], given["m_" + n], given["v_" + n])
    return (loss, grad_x, *[grad_w[n] for n in TWIN_WEIGHTS], *[delta_w[n] for n in TWIN_WEIGHTS],
            *[new_m[n] for n in TWIN_WEIGHTS], *[new_v[n] for n in TWIN_WEIGHTS])
```

```python
import functools
import math

import jax
import jax.numpy as jnp
from jax import lax
from jax.experimental import pallas as pl
from jax.experimental.pallas import tpu as pltpu

F32 = jnp.float32
BF16 = jnp.bfloat16
MESH = pl.DeviceIdType.MESH

D_MODEL = 2048
HEAD_DIM = 128
HEADS_A = 8
HEADS_B = 8
KV_A = 2
KV_B = 2
GROUP = 4
GRID_W = 64
WINDOW = 128
BLOCK = 128
FFN = 5632
IN_WIDTH = 3072
MIX_WIDTH = 2048
ROPE_THETA = 10000.0
EPS = 1e-6
ATTN_SCALE = HEAD_DIM ** -0.5
DN_ALPHA = 2.0 ** 0.25
N_SHARD = 4
N_DEV = 8

ADAM_LR = 0.001
ADAM_B1 = 0.9
ADAM_B2 = 0.999
ADAM_EPS = 1e-08
ADAM_WD = 0.01
ADAM_STEP = 10

QA0, KA0, VA0, QB0, KB0, VB0 = 0, 1024, 1280, 1536, 2560, 2816

VMEM_LIMIT = 56 * 1024 * 1024
ROW_TILE = 256
NN = (((1,), (0,)), ((), ()))
NT = (((1,), (1,)), ((), ()))
TN = (((0,), (0,)), ((), ()))


def _fit(total, pref):
    step = ROW_TILE // 4
    best = step
    for cand in range(step, pref + 1, step):
        if total % cand == 0:
            best = cand
    return best


def _params(sem=None):
    return pltpu.CompilerParams(dimension_semantics=sem, vmem_limit_bytes=VMEM_LIMIT)


def _matmul(a, b, *, name, ta=False, tb=False, tm, tn, tk, out_dtype):
    m = a.shape[1] if ta else a.shape[0]
    k = a.shape[0] if ta else a.shape[1]
    n = b.shape[0] if tb else b.shape[1]
    assert (b.shape[1] if tb else b.shape[0]) == k
    tm, tn, tk = min(tm, m), min(tn, n), min(tk, k)
    assert m % tm == 0 and n % tn == 0 and k % tk == 0, (name, m, n, k, tm, tn, tk)
    nk = k // tk
    dn = (((0 if ta else 1,), (1 if tb else 0,)), ((), ()))

    def body(a_ref, b_ref, o_ref, acc_ref):
        kk = pl.program_id(2)
        part = lax.dot_general(a_ref[...].astype(BF16), b_ref[...].astype(BF16), dn,
                               preferred_element_type=F32)

        @pl.when(kk == 0)
        def _():
            acc_ref[...] = part

        @pl.when(kk != 0)
        def _():
            acc_ref[...] += part

        @pl.when(kk == nk - 1)
        def _():
            o_ref[...] = acc_ref[...].astype(o_ref.dtype)

    a_spec = (pl.BlockSpec((tk, tm), lambda i, j, kk: (kk, i)) if ta
              else pl.BlockSpec((tm, tk), lambda i, j, kk: (i, kk)))
    b_spec = (pl.BlockSpec((tn, tk), lambda i, j, kk: (j, kk)) if tb
              else pl.BlockSpec((tk, tn), lambda i, j, kk: (kk, j)))
    return pl.pallas_call(
        body, name=name, grid=(m // tm, n // tn, nk),
        in_specs=[a_spec, b_spec],
        out_specs=pl.BlockSpec((tm, tn), lambda i, j, kk: (i, j)),
        out_shape=jax.ShapeDtypeStruct((m, n), out_dtype),
        scratch_shapes=[pltpu.VMEM((tm, tn), F32)],
        compiler_params=_params(("parallel", "parallel", "arbitrary")),
    )(a, b)


def _modulate_rows(x, ctx, mods):
    n, d = x.shape
    c = ctx.shape[0]
    nx = n // ROW_TILE
    assert c == ROW_TILE

    def body(x_ref, ctx_ref, mods_ref, o_ref):
        i = pl.program_id(0)

        @pl.when(i < nx)
        def _():
            o_ref[...] = (x_ref[...] * (1.0 + mods_ref[0:1, :]) + mods_ref[1:2, :]).astype(BF16)

        @pl.when(i >= nx)
        def _():
            o_ref[...] = (ctx_ref[...] * (1.0 + mods_ref[2:3, :]) + mods_ref[3:4, :]).astype(BF16)

    return pl.pallas_call(
        body, name="modulate_rows", grid=(nx + 1,),
        in_specs=[pl.BlockSpec((ROW_TILE, d), lambda i: (jnp.minimum(i, nx - 1), 0)),
                  pl.BlockSpec((ROW_TILE, d), lambda i: (0, 0)),
                  pl.BlockSpec((8, d), lambda i: (0, 0))],
        out_specs=pl.BlockSpec((ROW_TILE, d), lambda i: (i, 0)),
        out_shape=jax.ShapeDtypeStruct((n + c, d), BF16),
        compiler_params=_params(("parallel",)),
    )(x, ctx, mods)


def _rope_tables(n, c):
    rows = n // GRID_W
    row_ids = jnp.repeat(jnp.arange(rows, dtype=F32), GRID_W)
    col_ids = jnp.tile(jnp.arange(GRID_W, dtype=F32), rows)
    axis_dim = HEAD_DIM // 2
    inv_freq = jnp.power(ROPE_THETA, -jnp.arange(0, axis_dim, 2, dtype=F32) / axis_dim)
    ang_r = row_ids[:, None] * inv_freq
    ang_c = col_ids[:, None] * inv_freq
    ang = jnp.concatenate([ang_r, ang_r, ang_c, ang_c], axis=-1)
    cos, sin = jnp.cos(ang), jnp.sin(ang)
    quarter = (jnp.arange(HEAD_DIM) // (HEAD_DIM // 4)) % 2
    sin_a = jnp.where(quarter == 0, -sin, 0.0)
    sin_b = jnp.where(quarter == 1, sin, 0.0)
    pad = lambda t, v: jnp.concatenate([t, jnp.full((c, HEAD_DIM), v, F32)], axis=0)
    return pad(cos, 1.0), pad(sin_a, 0.0), pad(sin_b, 0.0)


def _rope(x, cos, sin_a, sin_b):
    return x * cos + pltpu.roll(x, 96, 1) * sin_a + pltpu.roll(x, 32, 1) * sin_b


def _rope_t(dy, cos, sin_a, sin_b):
    return dy * cos - pltpu.roll(dy, 96, 1) * sin_a - pltpu.roll(dy, 32, 1) * sin_b


def _rms(x):
    r = lax.rsqrt(jnp.mean(x * x, axis=-1, keepdims=True) + EPS)
    return x * r, r


def _qkv_post(h_all, cos, sin_a, sin_b, q_g, k_g):
    t = h_all.shape[0]
    nt = t // ROW_TILE

    def body(h_ref, cos_ref, sa_ref, sb_ref, qg_ref, kg_ref, qa_ref, ka_ref, va_ref, qb_ref, kb_ref, vb_ref):
        cos_, sa, sb = cos_ref[...], sa_ref[...], sb_ref[...]
        sl = lambda off, hh: h_ref[:, off + hh * HEAD_DIM: off + (hh + 1) * HEAD_DIM]
        for hh in range(HEADS_A):
            qa_ref[hh] = (_rope(sl(QA0, hh), cos_, sa, sb) * ATTN_SCALE).astype(BF16)
        for hh in range(KV_A):
            ka_ref[hh] = _rope(sl(KA0, hh), cos_, sa, sb).astype(BF16)
            va_ref[hh] = sl(VA0, hh).astype(BF16)
        for hh in range(HEADS_B):
            xn, _ = _rms(sl(QB0, hh))
            qb_ref[hh] = (_rope(xn * qg_ref[...], cos_, sa, sb) * ATTN_SCALE).astype(BF16)
        for hh in range(KV_B):
            xn, _ = _rms(sl(KB0, hh))
            kb_ref[hh] = _rope(xn * kg_ref[...], cos_, sa, sb).astype(BF16)
            vb_ref[hh] = sl(VB0, hh).astype(BF16)

    tab = pl.BlockSpec((ROW_TILE, HEAD_DIM), lambda i: (i, 0))
    gain = pl.BlockSpec((1, HEAD_DIM), lambda i: (0, 0))
    hs = lambda nh: pl.BlockSpec((nh, ROW_TILE, HEAD_DIM), lambda i: (0, i, 0))
    sh = lambda nh: jax.ShapeDtypeStruct((nh, t, HEAD_DIM), BF16)
    return pl.pallas_call(
        body, name="qkv_post", grid=(nt,),
        in_specs=[pl.BlockSpec((ROW_TILE, IN_WIDTH), lambda i: (i, 0)), tab, tab, tab, gain, gain],
        out_specs=[hs(HEADS_A), hs(KV_A), hs(KV_A), hs(HEADS_B), hs(KV_B), hs(KV_B)],
        out_shape=[sh(HEADS_A), sh(KV_A), sh(KV_A), sh(HEADS_B), sh(KV_B), sh(KV_B)],
        compiler_params=_params(("parallel",)),
    )(h_all, cos, sin_a, sin_b, q_g, k_g)


def _qkv_bwd_post(h_all, cos, sin_a, sin_b, q_g, k_g, dqa, dka, dva, dqb, dkb, dvb, n):
    t = h_all.shape[0]
    nt = t // ROW_TILE
    nx = n // ROW_TILE

    def body(h_ref, cos_ref, sa_ref, sb_ref, qg_ref, kg_ref,
             dqa_ref, dka_ref, dva_ref, dqb_ref, dkb_ref, dvb_ref, dh_ref, gs_ref):
        i = pl.program_id(0)
        cos_, sa, sb = cos_ref[...], sa_ref[...], sb_ref[...]
        latent = (i < nx).astype(F32)
        sl = lambda off, hh: h_ref[:, off + hh * HEAD_DIM: off + (hh + 1) * HEAD_DIM]

        def put(off, hh, val):
            dh_ref[:, off + hh * HEAD_DIM: off + (hh + 1) * HEAD_DIM] = val.astype(BF16)

        def norm_bwd(x, gain, dy):
            xn, r = _rms(x)
            dxh = dy * gain
            dx = r * (dxh - xn * jnp.mean(dxh * xn, axis=-1, keepdims=True))
            return dx, jnp.sum(dy * xn, axis=0, keepdims=True)

        for hh in range(HEADS_A):
            put(QA0, hh, _rope_t(dqa_ref[hh] * (ATTN_SCALE * latent), cos_, sa, sb))
        for hh in range(KV_A):
            put(KA0, hh, _rope_t(dka_ref[hh], cos_, sa, sb))
            put(VA0, hh, dva_ref[hh])
        gq = jnp.zeros((1, HEAD_DIM), F32)
        gk = jnp.zeros((1, HEAD_DIM), F32)
        for hh in range(HEADS_B):
            dy = _rope_t(dqb_ref[hh] * (ATTN_SCALE * latent), cos_, sa, sb)
            dx, g = norm_bwd(sl(QB0, hh), qg_ref[...], dy)
            put(QB0, hh, dx)
            gq = gq + g
        for hh in range(KV_B):
            dy = _rope_t(dkb_ref[hh], cos_, sa, sb)
            dx, g = norm_bwd(sl(KB0, hh), kg_ref[...], dy)
            put(KB0, hh, dx)
            gk = gk + g
            put(VB0, hh, dvb_ref[hh])
        upd = jnp.concatenate([gq, gk, jnp.zeros((6, HEAD_DIM), F32)], axis=0)

        @pl.when(i == 0)
        def _():
            gs_ref[...] = upd

        @pl.when(i != 0)
        def _():
            gs_ref[...] += upd

    tab = pl.BlockSpec((ROW_TILE, HEAD_DIM), lambda i: (i, 0))
    gain = pl.BlockSpec((1, HEAD_DIM), lambda i: (0, 0))
    lat = lambda nh: pl.BlockSpec((nh, ROW_TILE, HEAD_DIM), lambda i: (0, jnp.minimum(i, nx - 1), 0))
    full = lambda nh: pl.BlockSpec((nh, ROW_TILE, HEAD_DIM), lambda i: (0, i, 0))
    return pl.pallas_call(
        body, name="qkv_bwd_post", grid=(nt,),
        in_specs=[pl.BlockSpec((ROW_TILE, IN_WIDTH), lambda i: (i, 0)), tab, tab, tab, gain, gain,
                  lat(HEADS_A), full(KV_A), full(KV_A), lat(HEADS_B), full(KV_B), full(KV_B)],
        out_specs=[pl.BlockSpec((ROW_TILE, IN_WIDTH), lambda i: (i, 0)),
                   pl.BlockSpec((8, HEAD_DIM), lambda i: (0, 0))],
        out_shape=[jax.ShapeDtypeStruct((t, IN_WIDTH), BF16), jax.ShapeDtypeStruct((8, HEAD_DIM), F32)],
        compiler_params=_params(("arbitrary",)),
    )(h_all, cos, sin_a, sin_b, q_g, k_g, dqa, dka, dva, dqb, dkb, dvb)


GB_TQ = 256
GB_TK = 256


def _heads_rows(ref2d, tq):
    return jnp.concatenate([ref2d[:, hh * HEAD_DIM:(hh + 1) * HEAD_DIM] for hh in range(GROUP)], axis=0)


def _attn_b_fwd(qb, kb, vb, n):
    t = kb.shape[1]
    nk = t // GB_TK
    tq = GB_TQ
    rows = GROUP * tq

    def body(q_ref, k_ref, v_ref, o_ref, lse_ref, m_s, l_s, acc_s):
        q = q_ref[...].reshape(rows, HEAD_DIM)
        m_s[...] = jnp.full((rows, 1), -jnp.inf, F32)
        l_s[...] = jnp.zeros((rows, 1), F32)
        acc_s[...] = jnp.zeros((rows, HEAD_DIM), F32)

        def step(j, carry):
            start = pl.multiple_of(j * GB_TK, GB_TK)
            ks = k_ref[pl.ds(start, GB_TK), :]
            vs = v_ref[pl.ds(start, GB_TK), :]
            s = lax.dot_general(q, ks, NT, preferred_element_type=F32)
            m_prev = m_s[...]
            m_new = jnp.maximum(m_prev, jnp.max(s, axis=1, keepdims=True))
            p = jnp.exp(s - m_new)
            alpha = jnp.exp(m_prev - m_new)
            l_s[...] = alpha * l_s[...] + jnp.sum(p, axis=1, keepdims=True)
            acc_s[...] = alpha * acc_s[...] + lax.dot_general(p.astype(BF16), vs, NN, preferred_element_type=F32)
            m_s[...] = m_new
            return carry

        lax.fori_loop(0, nk, step, 0)
        inv = 1.0 / l_s[...]
        o = acc_s[...] * inv
        lse = m_s[...] + jnp.log(l_s[...])
        for hh in range(GROUP):
            o_ref[:, hh * HEAD_DIM:(hh + 1) * HEAD_DIM] = o[hh * tq:(hh + 1) * tq].astype(BF16)
            lse_ref[hh] = jnp.broadcast_to(lse[hh * tq:(hh + 1) * tq], (tq, HEAD_DIM))

    return pl.pallas_call(
        body, name="attn_b_fwd", grid=(KV_B, n // tq),
        in_specs=[pl.BlockSpec((GROUP, tq, HEAD_DIM), lambda g, i: (g, i, 0)),
                  pl.BlockSpec((None, t, HEAD_DIM), lambda g, i: (g, 0, 0)),
                  pl.BlockSpec((None, t, HEAD_DIM), lambda g, i: (g, 0, 0))],
        out_specs=[pl.BlockSpec((tq, GROUP * HEAD_DIM), lambda g, i: (i, g)),
                   pl.BlockSpec((GROUP, tq, HEAD_DIM), lambda g, i: (g, i, 0))],
        out_shape=[jax.ShapeDtypeStruct((n, HEADS_B * HEAD_DIM), BF16),
                   jax.ShapeDtypeStruct((HEADS_B, n, HEAD_DIM), F32)],
        scratch_shapes=[pltpu.VMEM((rows, 1), F32), pltpu.VMEM((rows, 1), F32), pltpu.VMEM((rows, HEAD_DIM), F32)],
        compiler_params=_params(("parallel", "parallel")),
    )(qb, kb, vb)


def _attn_b_bwd(qb, kb, vb, dheads, lse, delta, n):
    t = kb.shape[1]
    nk = t // GB_TK
    tq = GB_TQ
    nq = n // tq

    def body(q_ref, k_ref, v_ref, do_ref, lse_ref, dl_ref, dq_ref, dk_ref, dv_ref):
        j = pl.program_id(1)
        i = pl.program_id(2)
        q = q_ref[...].reshape(GROUP * tq, HEAD_DIM)
        do = _heads_rows(do_ref, tq)
        lse_ = lse_ref[...].reshape(GROUP * tq, HEAD_DIM)[:, 0:1]
        dl = dl_ref[...].reshape(GROUP * tq, HEAD_DIM)[:, 0:1]
        ks, vs = k_ref[...], v_ref[...]
        s = lax.dot_general(q, ks, NT, preferred_element_type=F32)
        p = jnp.exp(s - lse_)
        dp = lax.dot_general(do, vs, NT, preferred_element_type=F32)
        ds = (p * (dp - dl)).astype(BF16)
        dv_part = lax.dot_general(p.astype(BF16), do, TN, preferred_element_type=F32)
        dk_part = lax.dot_general(ds, q, TN, preferred_element_type=F32)
        dq_part = lax.dot_general(ds, ks, NN, preferred_element_type=F32)

        @pl.when(i == 0)
        def _():
            dk_ref[...] = dk_part
            dv_ref[...] = dv_part

        @pl.when(i != 0)
        def _():
            dk_ref[...] += dk_part
            dv_ref[...] += dv_part

        start = pl.multiple_of(i * tq, tq)
        for hh in range(GROUP):
            piece = dq_part[hh * tq:(hh + 1) * tq]

            @pl.when(j == 0)
            def _():
                dq_ref[hh, pl.ds(start, tq), :] = piece

            @pl.when(j != 0)
            def _():
                dq_ref[hh, pl.ds(start, tq), :] += piece

    grp = lambda base: pl.BlockSpec((GROUP, tq, HEAD_DIM), lambda g, j, i: (base + g, i, 0))
    kv = pl.BlockSpec((None, GB_TK, HEAD_DIM), lambda g, j, i: (g, j, 0))
    return pl.pallas_call(
        body, name="attn_b_bwd", grid=(KV_B, nk, nq),
        in_specs=[grp(0), kv, kv,
                  pl.BlockSpec((tq, GROUP * HEAD_DIM), lambda g, j, i: (i, KV_A + g)),
                  grp(0), grp(KV_A)],
        out_specs=[pl.BlockSpec((GROUP, n, HEAD_DIM), lambda g, j, i: (g, 0, 0)), kv, kv],
        out_shape=[jax.ShapeDtypeStruct((HEADS_B, n, HEAD_DIM), F32),
                   jax.ShapeDtypeStruct((KV_B, t, HEAD_DIM), F32),
                   jax.ShapeDtypeStruct((KV_B, t, HEAD_DIM), F32)],
        compiler_params=_params(("parallel", "arbitrary", "arbitrary")),
    )(qb, kb, vb, dheads, lse, delta)


def _row_dot(dheads, heads):
    n, w = heads.shape
    nh = w // HEAD_DIM

    def body(a_ref, b_ref, o_ref):
        for hh in range(nh):
            sl = slice(hh * HEAD_DIM, (hh + 1) * HEAD_DIM)
            prod = a_ref[:, sl].astype(F32) * b_ref[:, sl].astype(F32)
            o_ref[hh] = jnp.broadcast_to(jnp.sum(prod, axis=1, keepdims=True), (ROW_TILE, HEAD_DIM))

    return pl.pallas_call(
        body, name="row_dot", grid=(n // ROW_TILE,),
        in_specs=[pl.BlockSpec((ROW_TILE, w), lambda i: (i, 0)), pl.BlockSpec((ROW_TILE, w), lambda i: (i, 0))],
        out_specs=pl.BlockSpec((nh, ROW_TILE, HEAD_DIM), lambda i: (0, i, 0)),
        out_shape=jax.ShapeDtypeStruct((nh, n, HEAD_DIM), F32),
        compiler_params=_params(("parallel",)),
    )(dheads, heads)


KWIN = 3 * BLOCK


def _window_scores(q, k_ref, j, n, nb, sink_col):
    c = k_ref.shape[0] - n
    start = pl.multiple_of(jnp.clip(j - 1, 0, nb - 3) * BLOCK, BLOCK)
    kw = k_ref[pl.ds(start, KWIN), :]
    kc = k_ref[pl.ds(n, c), :]
    s_loc = lax.dot_general(q, kw, NT, preferred_element_type=F32)
    s_ctx = lax.dot_general(q, kc, NT, preferred_element_type=F32)
    rows = GROUP * BLOCK
    qpos = j * BLOCK + lax.broadcasted_iota(jnp.int32, (rows, KWIN), 0) % BLOCK
    kpos = start + lax.broadcasted_iota(jnp.int32, (rows, KWIN), 1)
    s_loc = jnp.where(jnp.abs(qpos - kpos) <= WINDOW, s_loc, -jnp.inf)
    m = jnp.maximum(jnp.maximum(jnp.max(s_loc, axis=1, keepdims=True), jnp.max(s_ctx, axis=1, keepdims=True)),
                    sink_col)
    e_loc, e_ctx, e_sink = jnp.exp(s_loc - m), jnp.exp(s_ctx - m), jnp.exp(sink_col - m)
    inv = 1.0 / (jnp.sum(e_loc, axis=1, keepdims=True) + jnp.sum(e_ctx, axis=1, keepdims=True) + e_sink)
    return e_loc * inv, e_ctx * inv, e_sink * inv, start


def _sink_column(sink_ref, g):
    cols = [jnp.broadcast_to(sink_ref[pl.ds(g * GROUP + hh, 1), 0:1], (BLOCK, 1)) for hh in range(GROUP)]
    return jnp.concatenate(cols, axis=0)


def _attn_a_fwd(qa, ka, va, sink_b, n):
    t = ka.shape[1]
    nb = n // BLOCK
    assert nb >= 3

    def body(q_ref, k_ref, v_ref, sink_ref, o_ref):
        g, j = pl.program_id(0), pl.program_id(1)
        q = q_ref[...].reshape(GROUP * BLOCK, HEAD_DIM)
        p_loc, p_ctx, _, start = _window_scores(q, k_ref, j, n, nb, _sink_column(sink_ref, g))
        vw = v_ref[pl.ds(start, KWIN), :]
        vc = v_ref[pl.ds(n, t - n), :]
        o = (lax.dot_general(p_loc.astype(BF16), vw, NN, preferred_element_type=F32)
             + lax.dot_general(p_ctx.astype(BF16), vc, NN, preferred_element_type=F32))
        for hh in range(GROUP):
            o_ref[:, hh * HEAD_DIM:(hh + 1) * HEAD_DIM] = o[hh * BLOCK:(hh + 1) * BLOCK].astype(BF16)

    return pl.pallas_call(
        body, name="attn_a_fwd", grid=(KV_A, nb),
        in_specs=[pl.BlockSpec((GROUP, BLOCK, HEAD_DIM), lambda g, j: (g, j, 0)),
                  pl.BlockSpec((None, t, HEAD_DIM), lambda g, j: (g, 0, 0)),
                  pl.BlockSpec((None, t, HEAD_DIM), lambda g, j: (g, 0, 0)),
                  pl.BlockSpec((HEADS_A, HEAD_DIM), lambda g, j: (0, 0))],
        out_specs=pl.BlockSpec((BLOCK, GROUP * HEAD_DIM), lambda g, j: (j, g)),
        out_shape=jax.ShapeDtypeStruct((n, HEADS_A * HEAD_DIM), BF16),
        compiler_params=_params(("parallel", "parallel")),
    )(qa, ka, va, sink_b)


def _attn_a_bwd(qa, ka, va, sink_b, dheads, n):
    t = ka.shape[1]
    c = t - n
    nb = n // BLOCK

    def body(q_ref, k_ref, v_ref, sink_ref, do_ref, dq_ref, dk_ref, dv_ref, dsink_ref):
        g, j = pl.program_id(0), pl.program_id(1)

        @pl.when(j == 0)
        def _():
            dk_ref[...] = jnp.zeros_like(dk_ref)
            dv_ref[...] = jnp.zeros_like(dv_ref)
            dsink_ref[...] = jnp.zeros_like(dsink_ref)

        q = q_ref[...].reshape(GROUP * BLOCK, HEAD_DIM)
        do = _heads_rows(do_ref, BLOCK)
        p_loc, p_ctx, p_sink, start = _window_scores(q, k_ref, j, n, nb, _sink_column(sink_ref, g))
        kw, vw = k_ref[pl.ds(start, KWIN), :], v_ref[pl.ds(start, KWIN), :]
        kc, vc = k_ref[pl.ds(n, c), :], v_ref[pl.ds(n, c), :]
        dp_loc = lax.dot_general(do, vw, NT, preferred_element_type=F32)
        dp_ctx = lax.dot_general(do, vc, NT, preferred_element_type=F32)
        dl = jnp.sum(p_loc * dp_loc, axis=1, keepdims=True) + jnp.sum(p_ctx * dp_ctx, axis=1, keepdims=True)
        ds_loc = (p_loc * (dp_loc - dl)).astype(BF16)
        ds_ctx = (p_ctx * (dp_ctx - dl)).astype(BF16)
        dq = (lax.dot_general(ds_loc, kw, NN, preferred_element_type=F32)
              + lax.dot_general(ds_ctx, kc, NN, preferred_element_type=F32))
        for hh in range(GROUP):
            dq_ref[hh] = dq[hh * BLOCK:(hh + 1) * BLOCK]
        dk_ref[pl.ds(start, KWIN), :] += lax.dot_general(ds_loc, q, TN, preferred_element_type=F32)
        dv_ref[pl.ds(start, KWIN), :] += lax.dot_general(p_loc.astype(BF16), do, TN, preferred_element_type=F32)
        dk_ref[pl.ds(n, c), :] += lax.dot_general(ds_ctx, q, TN, preferred_element_type=F32)
        dv_ref[pl.ds(n, c), :] += lax.dot_general(p_ctx.astype(BF16), do, TN, preferred_element_type=F32)
        dsk = -(p_sink * dl)
        upd = [jnp.broadcast_to(jnp.sum(dsk[hh * BLOCK:(hh + 1) * BLOCK], axis=0, keepdims=True), (1, HEAD_DIM))
               for hh in range(GROUP)]
        dsink_ref[...] += jnp.concatenate(upd + [jnp.zeros((8 - GROUP, HEAD_DIM), F32)], axis=0)

    res = pl.BlockSpec((None, t, HEAD_DIM), lambda g, j: (g, 0, 0))
    return pl.pallas_call(
        body, name="attn_a_bwd", grid=(KV_A, nb),
        in_specs=[pl.BlockSpec((GROUP, BLOCK, HEAD_DIM), lambda g, j: (g, j, 0)), res, res,
                  pl.BlockSpec((HEADS_A, HEAD_DIM), lambda g, j: (0, 0)),
                  pl.BlockSpec((BLOCK, GROUP * HEAD_DIM), lambda g, j: (j, g))],
        out_specs=[pl.BlockSpec((GROUP, BLOCK, HEAD_DIM), lambda g, j: (g, j, 0)), res, res,
                   pl.BlockSpec((None, 8, HEAD_DIM), lambda g, j: (g, 0, 0))],
        out_shape=[jax.ShapeDtypeStruct((HEADS_A, n, HEAD_DIM), F32),
                   jax.ShapeDtypeStruct((KV_A, t, HEAD_DIM), F32),
                   jax.ShapeDtypeStruct((KV_A, t, HEAD_DIM), F32),
                   jax.ShapeDtypeStruct((KV_A, 8, HEAD_DIM), F32)],
        compiler_params=_params(("parallel", "arbitrary")),
    )(qa, ka, va, sink_b, dheads)


def _ln_stats(r):
    mu = jnp.mean(r, axis=-1, keepdims=True)
    cen = r - mu
    rstd = lax.rsqrt(jnp.mean(cen * cen, axis=-1, keepdims=True) + EPS)
    return cen * rstd, rstd


def _ln_bwd(dy, xhat, rstd, gain):
    dxh = dy * gain
    return rstd * (dxh - jnp.mean(dxh, axis=-1, keepdims=True)
                   - xhat * jnp.mean(dxh * xhat, axis=-1, keepdims=True))


def _accumulate_rows(ref, rows, i):
    pad = [jnp.zeros_like(rows[0])] * (8 - len(rows))
    upd = jnp.concatenate(rows + pad, axis=0)

    @pl.when(i == 0)
    def _():
        ref[...] = upd

    @pl.when(i != 0)
    def _():
        ref[...] += upd


def _colsum(v):
    return jnp.sum(v, axis=0, keepdims=True)


LN_TILE = 128


def _res_ln1(x, a, vec):
    n, d = x.shape

    def body(x_ref, a_ref, v_ref, xh_ref, rs_ref, u_ref):
        r1 = DN_ALPHA * x_ref[...] + v_ref[0:1, :] * a_ref[...]
        xhat, rstd = _ln_stats(r1)
        xh_ref[...] = xhat
        rs_ref[...] = rstd
        x1 = xhat * v_ref[1:2, :] + v_ref[2:3, :]
        u_ref[...] = (x1 * (1.0 + v_ref[3:4, :]) + v_ref[4:5, :]).astype(BF16)

    row = pl.BlockSpec((LN_TILE, d), lambda i: (i, 0))
    return pl.pallas_call(
        body, name="res_ln1", grid=(n // LN_TILE,),
        in_specs=[row, row, pl.BlockSpec((8, d), lambda i: (0, 0))],
        out_specs=[row, pl.BlockSpec((LN_TILE, 1), lambda i: (i, 0)), row],
        out_shape=[jax.ShapeDtypeStruct((n, d), F32), jax.ShapeDtypeStruct((n, 1), F32),
                   jax.ShapeDtypeStruct((n, d), BF16)],
        compiler_params=_params(("parallel",)),
    )(x, a, vec)


def _res_ln2_loss(xhat1, f, target, vec):
    n, d = f.shape

    def body(xh_ref, f_ref, t_ref, v_ref, dr_ref, df_ref, s_ref):
        i = pl.program_id(0)
        x1 = xh_ref[...] * v_ref[1:2, :] + v_ref[2:3, :]
        fv = f_ref[...]
        xhat, rstd = _ln_stats(DN_ALPHA * x1 + v_ref[0:1, :] * fv)
        err = xhat * v_ref[3:4, :] + v_ref[4:5, :] - t_ref[...]
        dy = err * (1.0 / d)
        dr2 = _ln_bwd(dy, xhat, rstd, v_ref[3:4, :])
        dr_ref[...] = dr2
        df_ref[...] = (dr2 * v_ref[0:1, :]).astype(BF16)
        _accumulate_rows(s_ref, [_colsum(dy * xhat), _colsum(dy), _colsum(dr2 * fv),
                                 _colsum(err * err) * (0.5 / d)], i)

    row = pl.BlockSpec((LN_TILE, d), lambda i: (i, 0))
    return pl.pallas_call(
        body, name="res_ln2_loss", grid=(n // LN_TILE,),
        in_specs=[row, row, row, pl.BlockSpec((8, d), lambda i: (0, 0))],
        out_specs=[row, row, pl.BlockSpec((8, d), lambda i: (0, 0))],
        out_shape=[jax.ShapeDtypeStruct((n, d), F32), jax.ShapeDtypeStruct((n, d), BF16),
                   jax.ShapeDtypeStruct((8, d), F32)],
        compiler_params=_params(("arbitrary",)),
    )(xhat1, f, target, vec)


def _ln1_bwd(du2, dr2, xhat1, rstd1, a, vec):
    n, d = du2.shape

    def body(du_ref, dr2_ref, xh_ref, rs_ref, a_ref, v_ref, dxp_ref, da_ref, s_ref):
        i = pl.program_id(0)
        du, xhat = du_ref[...], xh_ref[...]
        x1 = xhat * v_ref[1:2, :] + v_ref[2:3, :]
        dx1 = DN_ALPHA * dr2_ref[...] + du * (1.0 + v_ref[0:1, :])
        dr1 = _ln_bwd(dx1, xhat, rs_ref[...], v_ref[1:2, :])
        dxp_ref[...] = DN_ALPHA * dr1
        da_ref[...] = (dr1 * v_ref[3:4, :]).astype(BF16)
        _accumulate_rows(s_ref, [_colsum(du * x1), _colsum(du), _colsum(dx1 * xhat), _colsum(dx1),
                                 _colsum(dr1 * a_ref[...])], i)

    row = pl.BlockSpec((LN_TILE, d), lambda i: (i, 0))
    return pl.pallas_call(
        body, name="ln1_bwd", grid=(n // LN_TILE,),
        in_specs=[row, row, row, pl.BlockSpec((LN_TILE, 1), lambda i: (i, 0)), row,
                  pl.BlockSpec((8, d), lambda i: (0, 0))],
        out_specs=[row, row, pl.BlockSpec((8, d), lambda i: (0, 0))],
        out_shape=[jax.ShapeDtypeStruct((n, d), F32), jax.ShapeDtypeStruct((n, d), BF16),
                   jax.ShapeDtypeStruct((8, d), F32)],
        compiler_params=_params(("arbitrary",)),
    )(du2, dr2, xhat1, rstd1, a, vec)


def _mod1_bwd(du_all, dxp, x, ctx, mods):
    n, d = x.shape
    nx = n // ROW_TILE

    def body(du_ref, dxp_ref, x_ref, ctx_ref, m_ref, gx_ref, s_ref):
        i = pl.program_id(0)
        du = du_ref[...]
        zero = jnp.zeros((1, d), F32)

        @pl.when(i == 0)
        def _():
            s_ref[...] = jnp.zeros_like(s_ref)

        @pl.when(i < nx)
        def _():
            gx_ref[...] = dxp_ref[...] + du * (1.0 + m_ref[0:1, :])
            s_ref[...] += jnp.concatenate([_colsum(du * x_ref[...]), _colsum(du)] + [zero] * 6, axis=0)

        @pl.when(i >= nx)
        def _():
            s_ref[...] += jnp.concatenate([zero, zero, _colsum(du * ctx_ref[...]), _colsum(du)] + [zero] * 4, axis=0)

    lat = pl.BlockSpec((ROW_TILE, d), lambda i: (jnp.minimum(i, nx - 1), 0))
    return pl.pallas_call(
        body, name="mod1_bwd", grid=(nx + 1,),
        in_specs=[pl.BlockSpec((ROW_TILE, d), lambda i: (i, 0)), lat, lat,
                  pl.BlockSpec((ROW_TILE, d), lambda i: (0, 0)), pl.BlockSpec((8, d), lambda i: (0, 0))],
        out_specs=[lat, pl.BlockSpec((8, d), lambda i: (0, 0))],
        out_shape=[jax.ShapeDtypeStruct((n, d), F32), jax.ShapeDtypeStruct((8, d), F32)],
        compiler_params=_params(("arbitrary",)),
    )(du_all, dxp, x, ctx, mods)


FFN_TM = 512
FFN_TN = 512


def _gate_up(u2, wg, wu):
    n, d = u2.shape
    f = wg.shape[1]

    def body(u_ref, wg_ref, wu_ref, g_ref, up_ref, h_ref):
        u = u_ref[...]
        g = lax.dot_general(u, wg_ref[...], NN, preferred_element_type=F32)
        up = lax.dot_general(u, wu_ref[...], NN, preferred_element_type=F32)
        g_ref[...] = g
        up_ref[...] = up
        h_ref[...] = (g * jax.nn.sigmoid(g) * up).astype(BF16)

    wspec = pl.BlockSpec((d, FFN_TN), lambda j, i: (0, j))
    ospec = pl.BlockSpec((FFN_TM, FFN_TN), lambda j, i: (i, j))
    return pl.pallas_call(
        body, name="gate_up", grid=(f // FFN_TN, n // FFN_TM),
        in_specs=[pl.BlockSpec((FFN_TM, d), lambda j, i: (i, 0)), wspec, wspec],
        out_specs=[ospec, ospec, ospec],
        out_shape=[jax.ShapeDtypeStruct((n, f), F32), jax.ShapeDtypeStruct((n, f), F32),
                   jax.ShapeDtypeStruct((n, f), BF16)],
        compiler_params=_params(("parallel", "parallel")),
    )(u2, wg, wu)


def _glu_bwd(df, wd, g, u):
    n, d = df.shape
    f = wd.shape[0]

    def body(df_ref, wd_ref, g_ref, u_ref, dg_ref, du_ref):
        dh = lax.dot_general(df_ref[...], wd_ref[...], NT, preferred_element_type=F32)
        gv = g_ref[...]
        sig = jax.nn.sigmoid(gv)
        du_ref[...] = (dh * (gv * sig)).astype(BF16)
        dg_ref[...] = (dh * u_ref[...] * (sig * (1.0 + gv * (1.0 - sig)))).astype(BF16)

    ospec = pl.BlockSpec((FFN_TM, FFN_TN), lambda j, i: (i, j))
    return pl.pallas_call(
        body, name="glu_bwd", grid=(f // FFN_TN, n // FFN_TM),
        in_specs=[pl.BlockSpec((FFN_TM, d), lambda j, i: (i, 0)),
                  pl.BlockSpec((FFN_TN, d), lambda j, i: (j, 0)), ospec, ospec],
        out_specs=[ospec, ospec],
        out_shape=[jax.ShapeDtypeStruct((n, f), BF16), jax.ShapeDtypeStruct((n, f), BF16)],
        compiler_params=_params(("parallel", "parallel")),
    )(df, wd, g, u)


def _du2(dg, du, wg, wu):
    n, f = dg.shape
    d = wg.shape[0]
    tm, tn, tk = min(1024, n), 1024, 512
    nk = f // tk

    def body(dg_ref, du_ref, wg_ref, wu_ref, o_ref, acc_ref):
        kk = pl.program_id(2)
        part = (lax.dot_general(dg_ref[...], wg_ref[...], NT, preferred_element_type=F32)
                + lax.dot_general(du_ref[...], wu_ref[...], NT, preferred_element_type=F32))

        @pl.when(kk == 0)
        def _():
            acc_ref[...] = part

        @pl.when(kk != 0)
        def _():
            acc_ref[...] += part

        @pl.when(kk == nk - 1)
        def _():
            o_ref[...] = acc_ref[...]

    aspec = pl.BlockSpec((tm, tk), lambda i, j, kk: (i, kk))
    wspec = pl.BlockSpec((tn, tk), lambda i, j, kk: (j, kk))
    return pl.pallas_call(
        body, name="du2", grid=(n // tm, d // tn, nk),
        in_specs=[aspec, aspec, wspec, wspec],
        out_specs=pl.BlockSpec((tm, tn), lambda i, j, kk: (i, j)),
        out_shape=jax.ShapeDtypeStruct((n, d), F32),
        scratch_shapes=[pltpu.VMEM((tm, tn), F32)],
        compiler_params=_params(("parallel", "parallel", "arbitrary")),
    )(dg, du, wg, wu)


def _rows8(rows, d=D_MODEL):
    rows = [r.reshape(1, d).astype(F32) for r in rows]
    return jnp.concatenate(rows + [jnp.zeros((8 - len(rows), d), F32)], axis=0)


def _layer_fwd_bwd(x, ctx, target, mod, mod_ctx, w_in, w_out, w_gate, w_up, w_down,
                   q_g, k_g, sink, ln1_g, ln1_b, ln2_g, ln2_b):
    n, d = x.shape
    c = ctx.shape[0]
    sh1, sc1, g1, sh2, sc2, g2 = [mod[:, k * d:(k + 1) * d] for k in range(6)]
    csh1, csc1 = mod_ctx[:, 0:d], mod_ctx[:, d:2 * d]
    cos, sin_a, sin_b = _rope_tables(n, c)
    sink_b = jnp.broadcast_to(sink.reshape(HEADS_A, 1), (HEADS_A, HEAD_DIM)).astype(F32)

    u_all = _modulate_rows(x, ctx, _rows8([sc1, sh1, csc1, csh1]))
    h_all = _matmul(u_all, w_in, name="qkv_proj", tm=_fit(n + c, 1088), tn=1024, tk=2048, out_dtype=F32)
    qa, ka, va, qb, kb, vb = _qkv_post(h_all, cos, sin_a, sin_b, q_g, k_g)
    o_a = _attn_a_fwd(qa, ka, va, sink_b, n)
    o_b, lse = _attn_b_fwd(qb, kb, vb, n)
    heads = jnp.concatenate([o_a, o_b], axis=1)
    a = _matmul(heads, w_out, name="out_proj", tm=1024, tn=1024, tk=2048, out_dtype=F32)
    xhat1, rstd1, u2 = _res_ln1(x, a, _rows8([g1, ln1_g, ln1_b, sc2, sh2]))
    gg, uu, hh = _gate_up(u2, w_gate, w_up)
    f = _matmul(hh, w_down, name="ffn_down", tm=1024, tn=1024, tk=512, out_dtype=F32)
    dr2, df, s_ln2 = _res_ln2_loss(xhat1, f, target, _rows8([g2, ln1_g, ln1_b, ln2_g, ln2_b]))

    dgg, duu = _glu_bwd(df, w_down, gg, uu)
    dw_down = _matmul(hh, df, name="dw_down", ta=True, tm=512, tn=1024, tk=512, out_dtype=BF16)
    dw_gate = _matmul(u2, dgg, name="dw_gate", ta=True, tm=1024, tn=512, tk=512, out_dtype=BF16)
    dw_up = _matmul(u2, duu, name="dw_up", ta=True, tm=1024, tn=512, tk=512, out_dtype=BF16)
    du2 = _du2(dgg, duu, w_gate, w_up)
    dxp, da, s_ln1 = _ln1_bwd(du2, dr2, xhat1, rstd1, a, _rows8([sc2, ln1_g, ln1_b, g1]))

    dheads = _matmul(da, w_out, name="d_heads", tb=True, tm=1024, tn=1024, tk=2048, out_dtype=BF16)
    dw_out = _matmul(heads, da, name="dw_out", ta=True, tm=1024, tn=1024, tk=512, out_dtype=BF16)
    delta = _row_dot(dheads, heads)
    dqa, dka, dva, dsink = _attn_a_bwd(qa, ka, va, sink_b, dheads, n)
    dqb, dkb, dvb = _attn_b_bwd(qb, kb, vb, dheads, lse, delta, n)
    dh_all, s_gain = _qkv_bwd_post(h_all, cos, sin_a, sin_b, q_g, k_g, dqa, dka, dva, dqb, dkb, dvb, n)
    dw_in = _matmul(u_all, dh_all, name="dw_in", ta=True, tm=1024, tn=1024, tk=_fit(n + c, 1088), out_dtype=BF16)
    du_all = _matmul(dh_all, w_in, name="d_u1", tb=True, tm=_fit(n + c, 1088), tn=1024, tk=1024, out_dtype=F32)
    grad_x, s_mod1 = _mod1_bwd(du_all, dxp, x, ctx, _rows8([sc1]))

    dsink_row = jnp.concatenate([dsink[0, 0:GROUP, 0], dsink[1, 0:GROUP, 0]]).reshape(1, HEADS_A)
    misc = jnp.concatenate([s_gain[0:1], s_gain[1:2], dsink_row,
                            jnp.zeros((1, d - 2 * HEAD_DIM - HEADS_A), F32)], axis=1)
    partial = jnp.concatenate([
        s_mod1[1:2], s_mod1[0:1], s_ln1[4:5],
        s_ln1[1:2], s_ln1[0:1], s_ln2[2:3],
        s_mod1[3:4], s_mod1[2:3],
        s_ln1[2:3], s_ln1[3:4], s_ln2[0:1], s_ln2[1:2],
        s_ln2[3:4], misc, jnp.zeros((2, d), F32)], axis=0)
    return grad_x, (dw_in, dw_out, dw_gate, dw_up, dw_down), partial


ANY = pl.BlockSpec(memory_space=pl.ANY)
VMEM_FULL = pl.BlockSpec(memory_space=pltpu.VMEM)
N_CHIP_PEERS = 3


def _me():
    return lax.axis_index("x"), lax.axis_index("y"), lax.axis_index("c")


def _other_chips(x, y):
    return [(1 - x, y), (x, 1 - y), (1 - x, 1 - y)]


def _shard_of(chip):
    return 2 * chip[0] + chip[1]


def _dev_index(x, y, c):
    return 4 * x + 2 * y + c


def _rcopy(src, dst, send_sems, recv_sems, k, dev):
    return pltpu.make_async_remote_copy(src_ref=src, dst_ref=dst, send_sem=send_sems.at[k], recv_sem=recv_sems.at[k],
                                        device_id=dev, device_id_type=MESH)


BIG = (("w_in", (D_MODEL, IN_WIDTH), 1), ("w_out", (MIX_WIDTH, D_MODEL), 0), ("w_gate", (D_MODEL, FFN), 1),
       ("w_up", (D_MODEL, FFN), 1), ("w_down", (FFN, D_MODEL), 0))


def _sub(ref, axis, idx, size):
    start = pl.multiple_of(idx * size, size)
    return ref.at[pl.ds(start, size), :] if axis == 0 else ref.at[:, pl.ds(start, size)]


def _shape_div(shape, axis, parts):
    return tuple(s // parts if a == axis else s for a, s in enumerate(shape))


def _allgather_weights(shards):
    n_arr = len(BIG)

    def body(*refs):
        ins, outs = refs[:n_arr], refs[n_arr:2 * n_arr]
        send_sems, recv_sems, local_sems = refs[2 * n_arr:]
        x, y, c = _me()
        sibling = (x, y, 1 - c)
        chips = _other_chips(x, y)
        s_me = _shard_of((x, y))

        def piece(a, ref, shard, half):
            _, full, axis = BIG[a]
            view = _sub(ref, axis, shard, full[axis] // N_SHARD)
            return _sub(view, 1 - axis, half, full[1 - axis] // 2)

        def my_half(a, half):
            _, full, axis = BIG[a]
            return _sub(ins[a], 1 - axis, half, full[1 - axis] // 2)

        local, first, passed = [], [], []
        for a in range(n_arr):
            _, full, axis = BIG[a]
            cp = pltpu.make_async_copy(ins[a], _sub(outs[a], axis, s_me, full[axis] // N_SHARD), local_sems.at[a])
            cp.start()
            local.append(cp)
        for a in range(n_arr):
            for j, chip in enumerate(chips):
                cp = _rcopy(my_half(a, c), piece(a, outs[a], s_me, c), send_sems, recv_sems, 6 * a + j, (*chip, c))
                cp.start()
                first.append(cp)
        for a in range(n_arr):
            for j, chip in enumerate(chips):
                landed = piece(a, outs[a], _shard_of(chip), c)
                _rcopy(landed, landed, send_sems, recv_sems, 6 * a + j, (*chip, c)).wait_recv()
                cp = _rcopy(landed, landed, send_sems, recv_sems, 6 * a + 3 + j, sibling)
                cp.start()
                passed.append(cp)
        for a in range(n_arr):
            for j, chip in enumerate(chips):
                other = piece(a, outs[a], _shard_of(chip), 1 - c)
                _rcopy(other, other, send_sems, recv_sems, 6 * a + 3 + j, sibling).wait_recv()
        for cp in first + passed:
            cp.wait_send()
        for cp in local:
            cp.wait()

    return pl.pallas_call(
        body, name="allgather_weights",
        in_specs=[ANY] * n_arr, out_specs=[ANY] * n_arr,
        out_shape=[jax.ShapeDtypeStruct(full, BF16) for _, full, _ in BIG],
        scratch_shapes=[pltpu.SemaphoreType.DMA((6 * n_arr,)), pltpu.SemaphoreType.DMA((6 * n_arr,)),
                        pltpu.SemaphoreType.DMA((n_arr,))],
    )(*shards)


def _exchange_halves(dws):
    n_arr = len(BIG)

    def body(*refs):
        ins, outs = refs[:n_arr], refs[n_arr:2 * n_arr]
        send_sems, recv_sems = refs[2 * n_arr:]
        x, y, c = _me()
        copies = []
        for a in range(n_arr):
            _, full, axis = BIG[a]
            src = _sub(ins[a], 1 - axis, 1 - c, full[1 - axis] // 2)
            cp = _rcopy(src, outs[a], send_sems, recv_sems, a, (x, y, 1 - c))
            cp.start()
            copies.append(cp)
        for cp in copies:
            cp.wait()

    return pl.pallas_call(
        body, name="grad_exchange_halves",
        in_specs=[ANY] * n_arr, out_specs=[ANY] * n_arr,
        out_shape=[jax.ShapeDtypeStruct(_shape_div(full, 1 - axis, 2), BF16) for _, full, axis in BIG],
        scratch_shapes=[pltpu.SemaphoreType.DMA((n_arr,)), pltpu.SemaphoreType.DMA((n_arr,))],
    )(*dws)


def _exchange_pieces(halves):
    n_arr = len(BIG)

    def body(*refs):
        ins, outs = refs[:n_arr], refs[n_arr:2 * n_arr]
        send_sems, recv_sems = refs[2 * n_arr:]
        x, y, c = _me()
        copies = []
        for a in range(n_arr):
            _, full, axis = BIG[a]
            for j, chip in enumerate(_other_chips(x, y)):
                src = _sub(ins[a], axis, _shard_of(chip), full[axis] // N_SHARD)
                cp = _rcopy(src, outs[a].at[j], send_sems, recv_sems, N_CHIP_PEERS * a + j, (*chip, c))
                cp.start()
                copies.append(cp)
        for cp in copies:
            cp.wait()

    def piece_shape(full, axis):
        return (N_CHIP_PEERS,) + _shape_div(_shape_div(full, 1 - axis, 2), axis, N_SHARD)

    return pl.pallas_call(
        body, name="grad_exchange_pieces",
        in_specs=[ANY] * n_arr, out_specs=[ANY] * n_arr,
        out_shape=[jax.ShapeDtypeStruct(piece_shape(full, axis), BF16) for _, full, axis in BIG],
        scratch_shapes=[pltpu.SemaphoreType.DMA((N_CHIP_PEERS * n_arr,)), pltpu.SemaphoreType.DMA((N_CHIP_PEERS * n_arr,))],
    )(*halves)


def _join_halves(g_halves):
    n_arr = len(BIG)

    def body(*refs):
        ins, outs = refs[:n_arr], refs[n_arr:2 * n_arr]
        send_sems, recv_sems, local_sems = refs[2 * n_arr:]
        x, y, c = _me()
        copies, local = [], []
        for a in range(n_arr):
            _, full, axis = BIG[a]
            mine = _sub(outs[a], 1 - axis, c, full[1 - axis] // 2)
            cp = pltpu.make_async_copy(ins[a], mine, local_sems.at[a])
            cp.start()
            local.append(cp)
            cp = _rcopy(ins[a], mine, send_sems, recv_sems, a, (x, y, 1 - c))
            cp.start()
            copies.append(cp)
        for a in range(n_arr):
            _, full, axis = BIG[a]
            theirs = _sub(outs[a], 1 - axis, 1 - c, full[1 - axis] // 2)
            _rcopy(theirs, theirs, send_sems, recv_sems, a, (x, y, 1 - c)).wait_recv()
        for cp in copies:
            cp.wait_send()
        for cp in local:
            cp.wait()

    return pl.pallas_call(
        body, name="grad_join_halves",
        in_specs=[ANY] * n_arr, out_specs=[ANY] * n_arr,
        out_shape=[jax.ShapeDtypeStruct(_shape_div(full, axis, N_SHARD), F32) for _, full, axis in BIG],
        scratch_shapes=[pltpu.SemaphoreType.DMA((n_arr,)), pltpu.SemaphoreType.DMA((n_arr,)),
                        pltpu.SemaphoreType.DMA((n_arr,))],
    )(*g_halves)


def _window_sum(name, big, widx, axis, nwin, extra, stacked, out_dtype):
    rows, cols = _shape_div(big.shape, axis, nwin)
    tr = _fit(rows, ROW_TILE)
    nbr = rows // tr

    def body(w_ref, big_ref, ex_ref, o_ref):
        acc = big_ref[...].astype(F32)
        if stacked:
            for j in range(N_CHIP_PEERS):
                acc = acc + ex_ref[j].astype(F32)
        else:
            acc = acc + ex_ref[...].astype(F32)
        o_ref[...] = acc.astype(o_ref.dtype)

    if axis == 0:
        big_spec = pl.BlockSpec((tr, cols), lambda i, w: (w[0] * nbr + i, 0))
    else:
        big_spec = pl.BlockSpec((tr, cols), lambda i, w: (i, w[0]))
    ex_spec = (pl.BlockSpec((N_CHIP_PEERS, tr, cols), lambda i, w: (0, i, 0)) if stacked
               else pl.BlockSpec((tr, cols), lambda i, w: (i, 0)))
    return pl.pallas_call(
        body, name=name,
        grid_spec=pltpu.PrefetchScalarGridSpec(
            num_scalar_prefetch=1, grid=(nbr,), in_specs=[big_spec, ex_spec],
            out_specs=pl.BlockSpec((tr, cols), lambda i, w: (i, 0))),
        out_shape=jax.ShapeDtypeStruct((rows, cols), out_dtype),
        compiler_params=_params(("parallel",)),
    )(widx.reshape(1).astype(jnp.int32), big, extra)


def _reduce_scatter_grads(dws):
    x, y, c = _me()
    s_me = _shard_of((x, y))
    from_sibling = _exchange_halves(dws)
    halves = [_window_sum("grad_sum_halves_" + BIG[a][0], dws[a], c, 1 - BIG[a][2], 2, from_sibling[a], False, BF16)
              for a in range(len(BIG))]
    from_chips = _exchange_pieces(halves)
    g_halves = [_window_sum("grad_sum_pieces_" + BIG[a][0], halves[a], s_me, BIG[a][2], N_SHARD, from_chips[a], True, F32)
                for a in range(len(BIG))]
    return _join_halves(g_halves)


def _gather_rows(block, name):
    r, d = block.shape

    def body(in_ref, out_ref, send_sems, recv_sems):
        x, y, c = _me()
        out_ref[_dev_index(x, y, c)] = in_ref[...]
        copies = []
        for mask in range(1, N_DEV):
            peer = (x ^ (mask >> 2), y ^ ((mask >> 1) & 1), c ^ (mask & 1))
            cp = _rcopy(in_ref, out_ref.at[_dev_index(x, y, c)], send_sems, recv_sems, mask - 1, peer)
            cp.start()
            copies.append((cp, peer))
        for mask in range(1, N_DEV):
            peer = (x ^ (mask >> 2), y ^ ((mask >> 1) & 1), c ^ (mask & 1))
            landed = out_ref.at[_dev_index(*peer)]
            _rcopy(landed, landed, send_sems, recv_sems, mask - 1, peer).wait_recv()
        for cp, _ in copies:
            cp.wait_send()

    return pl.pallas_call(
        body, name=name, in_specs=[VMEM_FULL], out_specs=VMEM_FULL,
        out_shape=jax.ShapeDtypeStruct((N_DEV, r, d), F32),
        scratch_shapes=[pltpu.SemaphoreType.DMA((N_DEV - 1,)), pltpu.SemaphoreType.DMA((N_DEV - 1,))],
    )(block)


ADA_ROWS = 80
ADA_W = 6 * D_MODEL // N_SHARD


def _ada_forward(c_block, cctx_block, w_ada, b_shard):
    d = c_block.shape[1]

    def body(c_ref, cc_ref, w_ref, b_ref, act_ref, mods_ref, raw, mloc, send_sems, recv_sems):
        x, y, c = _me()
        me = _dev_index(x, y, c)
        s_me = _shard_of((x, y))
        raw[72:ADA_ROWS, :] = jnp.zeros((ADA_ROWS - 72, d), F32)
        raw[pl.ds(pl.multiple_of(me * 8, 8), 8), :] = c_ref[...]
        raw[64:72, :] = cc_ref[...]
        sends = []
        for mask in range(1, N_DEV):
            peer = (x ^ (mask >> 2), y ^ ((mask >> 1) & 1), c ^ (mask & 1))
            cp = _rcopy(c_ref, raw.at[pl.ds(pl.multiple_of(me * 8, 8), 8), :], send_sems, recv_sems, mask - 1, peer)
            cp.start()
            sends.append(cp)
        for mask in range(1, N_DEV):
            peer = (x ^ (mask >> 2), y ^ ((mask >> 1) & 1), c ^ (mask & 1))
            landed = raw.at[pl.ds(pl.multiple_of(_dev_index(*peer) * 8, 8), 8), :]
            _rcopy(landed, landed, send_sems, recv_sems, mask - 1, peer).wait_recv()
        v = raw[...]
        act = v * jax.nn.sigmoid(v)
        act_ref[...] = act
        mloc[...] = lax.dot_general(act.astype(BF16), w_ref[...].astype(BF16), NN,
                                    preferred_element_type=F32) + b_ref[...]
        mods_ref[s_me, 0:8, :] = mloc[pl.ds(pl.multiple_of(me * 8, 8), 8), :]
        mods_ref[s_me, 8:16, :] = mloc[64:72, :]
        base = N_DEV - 1
        for j, chip in enumerate(_other_chips(x, y)):
            peer = (*chip, c)
            rows = mloc.at[pl.ds(pl.multiple_of(_dev_index(*peer) * 8, 8), 8), :]
            cp = _rcopy(rows, mods_ref.at[s_me, 0:8, :], send_sems, recv_sems, base + 2 * j, peer)
            cp.start()
            sends.append(cp)
            cp = _rcopy(mloc.at[64:72, :], mods_ref.at[s_me, 8:16, :], send_sems, recv_sems, base + 2 * j + 1, peer)
            cp.start()
            sends.append(cp)
        for j, chip in enumerate(_other_chips(x, y)):
            for part in range(2):
                landed = mods_ref.at[_shard_of(chip), 8 * part:8 * part + 8, :]
                _rcopy(landed, landed, send_sems, recv_sems, base + 2 * j + part, (*chip, c)).wait_recv()
        for cp in sends:
            cp.wait_send()

    n_sem = N_DEV - 1 + 2 * N_CHIP_PEERS
    return pl.pallas_call(
        body, name="ada_forward",
        in_specs=[VMEM_FULL] * 4, out_specs=[VMEM_FULL, VMEM_FULL],
        out_shape=[jax.ShapeDtypeStruct((ADA_ROWS, d), F32), jax.ShapeDtypeStruct((N_SHARD, 16, ADA_W), F32)],
        scratch_shapes=[pltpu.VMEM((ADA_ROWS, d), F32), pltpu.VMEM((ADA_ROWS, ADA_W), F32),
                        pltpu.SemaphoreType.DMA((n_sem,)), pltpu.SemaphoreType.DMA((n_sem,))],
        compiler_params=pltpu.CompilerParams(vmem_limit_bytes=VMEM_LIMIT),
    )(c_block, cctx_block, w_ada, b_shard)


def _small_reduce(gathered):
    d = gathered.shape[2]

    def body(g_ref, o_ref):
        tot = g_ref[0]
        for i in range(1, N_DEV):
            tot = tot + g_ref[i]
        o_ref[...] = tot
        o_ref[0:2, :] = tot[0:2] + tot[6:8]
        o_ref[12:13, :] = jnp.broadcast_to(jnp.sum(tot[12:13], axis=1, keepdims=True), (1, d))

    return pl.pallas_call(body, name="small_reduce", in_specs=[VMEM_FULL], out_specs=VMEM_FULL,
                          out_shape=jax.ShapeDtypeStruct((16, d), F32))(gathered)


def _cctx_grad(gathered, c_ctx):
    d = gathered.shape[2]

    def body(g_ref, c_ref, o_ref):
        tot = g_ref[0, 0:1, :]
        for chip in range(1, N_SHARD):
            tot = tot + g_ref[2 * chip, 0:1, :]
        v = c_ref[...]
        sig = jax.nn.sigmoid(v)
        o_ref[...] = tot * (sig * (1.0 + v * (1.0 - sig)))

    return pl.pallas_call(body, name="cctx_grad", in_specs=[VMEM_FULL, VMEM_FULL], out_specs=VMEM_FULL,
                          out_shape=jax.ShapeDtypeStruct((1, d), F32))(gathered, c_ctx.reshape(1, d))


def _cast_bf16(w, name):
    r, cdim = w.shape
    tr = _fit(r, ROW_TILE)

    def body(w_ref, o_ref):
        o_ref[...] = w_ref[...].astype(BF16)

    spec = pl.BlockSpec((tr, cdim), lambda i: (i, 0))
    return pl.pallas_call(body, name=name, grid=(r // tr,), in_specs=[spec], out_specs=spec,
                          out_shape=jax.ShapeDtypeStruct((r, cdim), BF16), compiler_params=_params(("parallel",)))(w)


def _adamw(w, g, m, v, name):
    r, cdim = w.shape
    tr = _fit(r, 128) if r % (ROW_TILE // 4) == 0 else r
    c1 = 1.0 - ADAM_B1 ** ADAM_STEP
    c2 = 1.0 - ADAM_B2 ** ADAM_STEP

    def body(w_ref, g_ref, m_ref, v_ref, d_ref, nm_ref, nv_ref):
        gv = g_ref[...]
        nm = ADAM_B1 * m_ref[...] + (1.0 - ADAM_B1) * gv
        nv = ADAM_B2 * v_ref[...] + (1.0 - ADAM_B2) * (gv * gv)
        nm_ref[...] = nm
        nv_ref[...] = nv
        d_ref[...] = -ADAM_LR * ((nm / c1) / (jnp.sqrt(nv / c2) + ADAM_EPS) + ADAM_WD * w_ref[...])

    spec = pl.BlockSpec((tr, cdim), lambda i: (i, 0))
    sh = jax.ShapeDtypeStruct((r, cdim), F32)
    return pl.pallas_call(body, name=name, grid=(r // tr,), in_specs=[spec] * 4, out_specs=[spec] * 3,
                          out_shape=[sh, sh, sh], compiler_params=_params(("parallel",)))(w, g, m, v)


SMALL = (("c_ctx", D_MODEL), ("b_ada", 6 * D_MODEL), ("q_norm_g", HEAD_DIM), ("k_norm_g", HEAD_DIM),
         ("sink_logit", HEADS_A), ("ln1_g", D_MODEL), ("ln1_b", D_MODEL), ("ln2_g", D_MODEL), ("ln2_b", D_MODEL))
WEIGHT_ORDER = ("c_ctx", "w_ada", "b_ada", "w_in", "q_norm_g", "k_norm_g", "sink_logit", "w_out", "ln1_g", "ln1_b",
                "w_gate", "w_up", "w_down", "ln2_g", "ln2_b")


def kernel(x, c, ctx, c_ctx, w_ada, b_ada, w_in, q_norm_g, k_norm_g, sink_logit, w_out, ln1_g, ln1_b, w_gate, w_up, w_down, ln2_g, ln2_b, loss_target, m_c_ctx, m_w_ada, m_b_ada, m_w_in, m_q_norm_g, m_k_norm_g, m_sink_logit, m_w_out, m_ln1_g, m_ln1_b, m_w_gate, m_w_up, m_w_down, m_ln2_g, m_ln2_b, v_c_ctx, v_w_ada, v_b_ada, v_w_in, v_q_norm_g, v_k_norm_g, v_sink_logit, v_w_out, v_ln1_g, v_ln1_b, v_w_gate, v_w_up, v_w_down, v_ln2_g, v_ln2_b):
    d = D_MODEL
    w = dict(c_ctx=c_ctx, w_ada=w_ada[0], b_ada=b_ada, w_in=w_in[0], q_norm_g=q_norm_g, k_norm_g=k_norm_g,
             sink_logit=sink_logit, w_out=w_out[0], ln1_g=ln1_g, ln1_b=ln1_b, w_gate=w_gate[0], w_up=w_up[0],
             w_down=w_down[0], ln2_g=ln2_g, ln2_b=ln2_b)
    m = dict(c_ctx=m_c_ctx, w_ada=m_w_ada[0], b_ada=m_b_ada, w_in=m_w_in[0], q_norm_g=m_q_norm_g, k_norm_g=m_k_norm_g,
             sink_logit=m_sink_logit, w_out=m_w_out[0], ln1_g=m_ln1_g, ln1_b=m_ln1_b, w_gate=m_w_gate[0],
             w_up=m_w_up[0], w_down=m_w_down[0], ln2_g=m_ln2_g, ln2_b=m_ln2_b)
    v = dict(c_ctx=v_c_ctx, w_ada=v_w_ada[0], b_ada=v_b_ada, w_in=v_w_in[0], q_norm_g=v_q_norm_g, k_norm_g=v_k_norm_g,
             sink_logit=v_sink_logit, w_out=v_w_out[0], ln1_g=v_ln1_g, ln1_b=v_ln1_b, w_gate=v_w_gate[0],
             w_up=v_w_up[0], w_down=v_w_down[0], ln2_g=v_ln2_g, ln2_b=v_ln2_b)
    mx, my, mc = _me()
    s_me = _shard_of((mx, my))
    me = _dev_index(mx, my, mc)
    pad8 = lambda row: jnp.concatenate([row.reshape(1, -1), jnp.zeros((7, row.size), F32)], axis=0)

    b_shard = lax.dynamic_slice(b_ada, (0, s_me * ADA_W), (1, ADA_W))
    act, mods4 = _ada_forward(pad8(c), pad8(c_ctx), w["w_ada"], b_shard)
    mod = jnp.transpose(mods4[:, 0:1, :], (1, 0, 2)).reshape(1, 6 * d)
    mod_ctx = jnp.transpose(mods4[:, 8:9, :], (1, 0, 2)).reshape(1, 6 * d)

    shards = [_cast_bf16(w[name], "cast_" + name) for name, _, _ in BIG]
    full = _allgather_weights(shards)

    grad_x, dws, partial = _layer_fwd_bwd(x[0], ctx[0], loss_target[0], mod, mod_ctx, *full,
                                          q_norm_g, k_norm_g, sink_logit, ln1_g, ln1_b, ln2_g, ln2_b)

    g_big = _reduce_scatter_grads(list(dws))
    grads = {name: g for (name, _, _), g in zip(BIG, g_big)}

    gathered = _gather_rows(partial, "gather_partials")
    tot = _small_reduce(gathered)
    grads["b_ada"] = tot[0:6].reshape(1, 6 * d)
    grads["ln1_g"], grads["ln1_b"], grads["ln2_g"], grads["ln2_b"] = tot[8:9], tot[9:10], tot[10:11], tot[11:12]
    grads["q_norm_g"] = tot[13:14, 0:HEAD_DIM]
    grads["k_norm_g"] = tot[13:14, HEAD_DIM:2 * HEAD_DIM]
    grads["sink_logit"] = tot[13:14, 2 * HEAD_DIM:2 * HEAD_DIM + HEADS_A]
    loss = tot[12, 0]

    dm_all = gathered[:, 0:6, :].reshape(N_DEV, 6 * d)
    dmc_tot = jnp.concatenate([tot[6:8].reshape(1, 2 * d), jnp.zeros((1, 4 * d), F32)], axis=1)
    dm_rows = jnp.concatenate([pad8(dm_all[i]) for i in range(N_DEV)] + [pad8(dmc_tot), jnp.zeros((8, 6 * d), F32)], axis=0)
    dm_shard = lax.dynamic_slice(dm_rows, (0, s_me * ADA_W), (ADA_ROWS, ADA_W))
    grads["w_ada"] = _matmul(act, dm_shard, name="dw_ada", ta=True, tm=1024, tn=1024, tk=ADA_ROWS, out_dtype=F32)
    dmc_shard = lax.dynamic_slice(pad8(dmc_tot), (0, s_me * ADA_W), (8, ADA_W))
    cc_part = _matmul(dmc_shard, w["w_ada"], name="d_cctx", tb=True, tm=8, tn=1024, tk=1536, out_dtype=F32)
    grads["c_ctx"] = _cctx_grad(_gather_rows(cc_part, "gather_cctx"), c_ctx).reshape(d)

    delta, new_m, new_v = {}, {}, {}
    for name in ("w_ada", "w_in", "w_out", "w_gate", "w_up", "w_down"):
        delta[name], new_m[name], new_v[name] = _adamw(w[name], grads[name], m[name], v[name], "adamw_" + name)
    pack = lambda t: jnp.concatenate([t[name].reshape(1, size) for name, size in SMALL], axis=1)
    pd, pm, pv = _adamw(pack(w), pack(grads), pack(m), pack(v), "adamw_small")
    off = 0
    for name, size in SMALL:
        delta[name], new_m[name], new_v[name] = [t[:, off:off + size].reshape(w[name].shape) for t in (pd, pm, pv)]
        grads[name] = grads[name].reshape(w[name].shape)
        off += size

    lead = lambda name, t: t[None] if name in ("w_ada", "w_in", "w_out", "w_gate", "w_up", "w_down") else t
    outs = [loss, grad_x[None]]
    for group in (grads, delta, new_m, new_v):
        outs += [lead(name, group[name]) for name in WEIGHT_ORDER]
    return tuple(outs)
```

```python
import functools
import math

import jax
import jax.numpy as jnp
from jax import lax
from jax.experimental import pallas as pl
from jax.experimental.pallas import tpu as pltpu

F32 = jnp.float32
BF16 = jnp.bfloat16
MESH = pl.DeviceIdType.MESH

D_MODEL = 2048
HEAD_DIM = 128
HEADS_A = 8
HEADS_B = 8
KV_A = 2
KV_B = 2
GROUP = 4
GRID_W = 64
WINDOW = 128
BLOCK = 128
FFN = 5632
IN_WIDTH = 3072
MIX_WIDTH = 2048
ROPE_THETA = 10000.0
EPS = 1e-6
ATTN_SCALE = HEAD_DIM ** -0.5
DN_ALPHA = 2.0 ** 0.25
N_SHARD = 4
N_DEV = 8

ADAM_LR = 0.001
ADAM_B1 = 0.9
ADAM_B2 = 0.999
ADAM_EPS = 1e-08
ADAM_WD = 0.01
ADAM_STEP = 10

QA0, KA0, VA0, QB0, KB0, VB0 = 0, 1024, 1280, 1536, 2560, 2816

VMEM_LIMIT = 56 * 1024 * 1024
ROW_TILE = 256
NN = (((1,), (0,)), ((), ()))
NT = (((1,), (1,)), ((), ()))
TN = (((0,), (0,)), ((), ()))


def _fit(total, pref):
    step = ROW_TILE // 4
    best = step
    for cand in range(step, pref + 1, step):
        if total % cand == 0:
            best = cand
    return best


def _params(sem=None):
    return pltpu.CompilerParams(dimension_semantics=sem, vmem_limit_bytes=VMEM_LIMIT)


def _matmul(a, b, *, name, ta=False, tb=False, tm, tn, tk, out_dtype):
    m = a.shape[1] if ta else a.shape[0]
    k = a.shape[0] if ta else a.shape[1]
    n = b.shape[0] if tb else b.shape[1]
    assert (b.shape[1] if tb else b.shape[0]) == k
    tm, tn, tk = min(tm, m), min(tn, n), min(tk, k)
    assert m % tm == 0 and n % tn == 0 and k % tk == 0, (name, m, n, k, tm, tn, tk)
    nk = k // tk
    dn = (((0 if ta else 1,), (1 if tb else 0,)), ((), ()))

    def body(a_ref, b_ref, o_ref, acc_ref):
        kk = pl.program_id(2)
        part = lax.dot_general(a_ref[...].astype(BF16), b_ref[...].astype(BF16), dn,
                               preferred_element_type=F32)

        @pl.when(kk == 0)
        def _():
            acc_ref[...] = part

        @pl.when(kk != 0)
        def _():
            acc_ref[...] += part

        @pl.when(kk == nk - 1)
        def _():
            o_ref[...] = acc_ref[...].astype(o_ref.dtype)

    a_spec = (pl.BlockSpec((tk, tm), lambda i, j, kk: (kk, i)) if ta
              else pl.BlockSpec((tm, tk), lambda i, j, kk: (i, kk)))
    b_spec = (pl.BlockSpec((tn, tk), lambda i, j, kk: (j, kk)) if tb
              else pl.BlockSpec((tk, tn), lambda i, j, kk: (kk, j)))
    return pl.pallas_call(
        body, name=name, grid=(m // tm, n // tn, nk),
        in_specs=[a_spec, b_spec],
        out_specs=pl.BlockSpec((tm, tn), lambda i, j, kk: (i, j)),
        out_shape=jax.ShapeDtypeStruct((m, n), out_dtype),
        scratch_shapes=[pltpu.VMEM((tm, tn), F32)],
        compiler_params=_params(("parallel", "parallel", "arbitrary")),
    )(a, b)


def _modulate_rows(x, ctx, mods):
    n, d = x.shape
    c = ctx.shape[0]
    nx = n // ROW_TILE
    assert c == ROW_TILE

    def body(x_ref, ctx_ref, mods_ref, o_ref):
        i = pl.program_id(0)

        @pl.when(i < nx)
        def _():
            o_ref[...] = (x_ref[...] * (1.0 + mods_ref[0:1, :]) + mods_ref[1:2, :]).astype(BF16)

        @pl.when(i >= nx)
        def _():
            o_ref[...] = (ctx_ref[...] * (1.0 + mods_ref[2:3, :]) + mods_ref[3:4, :]).astype(BF16)

    return pl.pallas_call(
        body, name="modulate_rows", grid=(nx + 1,),
        in_specs=[pl.BlockSpec((ROW_TILE, d), lambda i: (jnp.minimum(i, nx - 1), 0)),
                  pl.BlockSpec((ROW_TILE, d), lambda i: (0, 0)),
                  pl.BlockSpec((8, d), lambda i: (0, 0))],
        out_specs=pl.BlockSpec((ROW_TILE, d), lambda i: (i, 0)),
        out_shape=jax.ShapeDtypeStruct((n + c, d), BF16),
        compiler_params=_params(("parallel",)),
    )(x, ctx, mods)


def _rope_tables(n, c):
    rows = n // GRID_W
    row_ids = jnp.repeat(jnp.arange(rows, dtype=F32), GRID_W)
    col_ids = jnp.tile(jnp.arange(GRID_W, dtype=F32), rows)
    axis_dim = HEAD_DIM // 2
    inv_freq = jnp.power(ROPE_THETA, -jnp.arange(0, axis_dim, 2, dtype=F32) / axis_dim)
    ang_r = row_ids[:, None] * inv_freq
    ang_c = col_ids[:, None] * inv_freq
    ang = jnp.concatenate([ang_r, ang_r, ang_c, ang_c], axis=-1)
    cos, sin = jnp.cos(ang), jnp.sin(ang)
    quarter = (jnp.arange(HEAD_DIM) // (HEAD_DIM // 4)) % 2
    sin_a = jnp.where(quarter == 0, -sin, 0.0)
    sin_b = jnp.where(quarter == 1, sin, 0.0)
    pad = lambda t, v: jnp.concatenate([t, jnp.full((c, HEAD_DIM), v, F32)], axis=0)
    return pad(cos, 1.0), pad(sin_a, 0.0), pad(sin_b, 0.0)


def _rope(x, cos, sin_a, sin_b):
    return x * cos + pltpu.roll(x, 96, 1) * sin_a + pltpu.roll(x, 32, 1) * sin_b


def _rope_t(dy, cos, sin_a, sin_b):
    return dy * cos - pltpu.roll(dy, 96, 1) * sin_a - pltpu.roll(dy, 32, 1) * sin_b


def _rms(x):
    r = lax.rsqrt(jnp.mean(x * x, axis=-1, keepdims=True) + EPS)
    return x * r, r


def _qkv_post(h_all, cos, sin_a, sin_b, q_g, k_g):
    t = h_all.shape[0]
    nt = t // ROW_TILE

    def body(h_ref, cos_ref, sa_ref, sb_ref, qg_ref, kg_ref, qa_ref, ka_ref, va_ref, qb_ref, kb_ref, vb_ref):
        cos_, sa, sb = cos_ref[...], sa_ref[...], sb_ref[...]
        sl = lambda off, hh: h_ref[:, off + hh * HEAD_DIM: off + (hh + 1) * HEAD_DIM]
        for hh in range(HEADS_A):
            qa_ref[hh] = (_rope(sl(QA0, hh), cos_, sa, sb) * ATTN_SCALE).astype(BF16)
        for hh in range(KV_A):
            ka_ref[hh] = _rope(sl(KA0, hh), cos_, sa, sb).astype(BF16)
            va_ref[hh] = sl(VA0, hh).astype(BF16)
        for hh in range(HEADS_B):
            xn, _ = _rms(sl(QB0, hh))
            qb_ref[hh] = (_rope(xn * qg_ref[...], cos_, sa, sb) * ATTN_SCALE).astype(BF16)
        for hh in range(KV_B):
            xn, _ = _rms(sl(KB0, hh))
            kb_ref[hh] = _rope(xn * kg_ref[...], cos_, sa, sb).astype(BF16)
            vb_ref[hh] = sl(VB0, hh).astype(BF16)

    tab = pl.BlockSpec((ROW_TILE, HEAD_DIM), lambda i: (i, 0))
    gain = pl.BlockSpec((1, HEAD_DIM), lambda i: (0, 0))
    hs = lambda nh: pl.BlockSpec((nh, ROW_TILE, HEAD_DIM), lambda i: (0, i, 0))
    sh = lambda nh: jax.ShapeDtypeStruct((nh, t, HEAD_DIM), BF16)
    return pl.pallas_call(
        body, name="qkv_post", grid=(nt,),
        in_specs=[pl.BlockSpec((ROW_TILE, IN_WIDTH), lambda i: (i, 0)), tab, tab, tab, gain, gain],
        out_specs=[hs(HEADS_A), hs(KV_A), hs(KV_A), hs(HEADS_B), hs(KV_B), hs(KV_B)],
        out_shape=[sh(HEADS_A), sh(KV_A), sh(KV_A), sh(HEADS_B), sh(KV_B), sh(KV_B)],
        compiler_params=_params(("parallel",)),
    )(h_all, cos, sin_a, sin_b, q_g, k_g)


def _qkv_bwd_post(h_all, cos, sin_a, sin_b, q_g, k_g, dqa, dka, dva, dqb, dkb, dvb, n):
    t = h_all.shape[0]
    nt = t // ROW_TILE
    nx = n // ROW_TILE

    def body(h_ref, cos_ref, sa_ref, sb_ref, qg_ref, kg_ref,
             dqa_ref, dka_ref, dva_ref, dqb_ref, dkb_ref, dvb_ref, dh_ref, gs_ref):
        i = pl.program_id(0)
        cos_, sa, sb = cos_ref[...], sa_ref[...], sb_ref[...]
        latent = (i < nx).astype(F32)
        sl = lambda off, hh: h_ref[:, off + hh * HEAD_DIM: off + (hh + 1) * HEAD_DIM]

        def put(off, hh, val):
            dh_ref[:, off + hh * HEAD_DIM: off + (hh + 1) * HEAD_DIM] = val.astype(BF16)

        def norm_bwd(x, gain, dy):
            xn, r = _rms(x)
            dxh = dy * gain
            dx = r * (dxh - xn * jnp.mean(dxh * xn, axis=-1, keepdims=True))
            return dx, jnp.sum(dy * xn, axis=0, keepdims=True)

        for hh in range(HEADS_A):
            put(QA0, hh, _rope_t(dqa_ref[hh] * (ATTN_SCALE * latent), cos_, sa, sb))
        for hh in range(KV_A):
            put(KA0, hh, _rope_t(dka_ref[hh], cos_, sa, sb))
            put(VA0, hh, dva_ref[hh])
        gq = jnp.zeros((1, HEAD_DIM), F32)
        gk = jnp.zeros((1, HEAD_DIM), F32)
        for hh in range(HEADS_B):
            dq_t = dqb_ref[hh // GROUP, :, (hh % GROUP) * ROW_TILE:(hh % GROUP + 1) * ROW_TILE]
            dy = _rope_t(dq_t.T * (ATTN_SCALE * latent), cos_, sa, sb)
            dx, g = norm_bwd(sl(QB0, hh), qg_ref[...], dy)
            put(QB0, hh, dx)
            gq = gq + g
        for hh in range(KV_B):
            dy = _rope_t(dkb_ref[hh], cos_, sa, sb)
            dx, g = norm_bwd(sl(KB0, hh), kg_ref[...], dy)
            put(KB0, hh, dx)
            gk = gk + g
            put(VB0, hh, dvb_ref[hh])
        upd = jnp.concatenate([gq, gk, jnp.zeros((6, HEAD_DIM), F32)], axis=0)

        @pl.when(i == 0)
        def _():
            gs_ref[...] = upd

        @pl.when(i != 0)
        def _():
            gs_ref[...] += upd

    tab = pl.BlockSpec((ROW_TILE, HEAD_DIM), lambda i: (i, 0))
    gain = pl.BlockSpec((1, HEAD_DIM), lambda i: (0, 0))
    lat = lambda nh: pl.BlockSpec((nh, ROW_TILE, HEAD_DIM), lambda i: (0, jnp.minimum(i, nx - 1), 0))
    full = lambda nh: pl.BlockSpec((nh, ROW_TILE, HEAD_DIM), lambda i: (0, i, 0))
    return pl.pallas_call(
        body, name="qkv_bwd_post", grid=(nt,),
        in_specs=[pl.BlockSpec((ROW_TILE, IN_WIDTH), lambda i: (i, 0)), tab, tab, tab, gain, gain,
                  lat(HEADS_A), full(KV_A), full(KV_A),
                  pl.BlockSpec((KV_B, None, HEAD_DIM, GROUP * ROW_TILE), lambda i: (0, jnp.minimum(i, nx - 1), 0, 0)),
                  full(KV_B), full(KV_B)],
        out_specs=[pl.BlockSpec((ROW_TILE, IN_WIDTH), lambda i: (i, 0)),
                   pl.BlockSpec((8, HEAD_DIM), lambda i: (0, 0))],
        out_shape=[jax.ShapeDtypeStruct((t, IN_WIDTH), BF16), jax.ShapeDtypeStruct((8, HEAD_DIM), F32)],
        compiler_params=_params(("arbitrary",)),
    )(h_all, cos, sin_a, sin_b, q_g, k_g, dqa, dka, dva, dqb, dkb, dvb)


GB_TQ = 256
GB_TK = 256


def _heads_rows(ref2d, tq):
    return jnp.concatenate([ref2d[:, hh * HEAD_DIM:(hh + 1) * HEAD_DIM] for hh in range(GROUP)], axis=0)


def _attn_b_fwd(qb, kb, vb, n):
    t = kb.shape[1]
    nk = t // GB_TK
    tq = GB_TQ
    rows = GROUP * tq

    def body(q_ref, k_ref, v_ref, o_ref, lse_ref, m_s, l_s, acc_s):
        q = q_ref[...].reshape(rows, HEAD_DIM)
        m_s[...] = jnp.full((1, rows), -jnp.inf, F32)
        l_s[...] = jnp.zeros((1, rows), F32)
        acc_s[...] = jnp.zeros((HEAD_DIM, rows), F32)

        def step(j, carry):
            start = pl.multiple_of(j * GB_TK, GB_TK)
            ks = k_ref[pl.ds(start, GB_TK), :]
            vs = v_ref[pl.ds(start, GB_TK), :]
            st = lax.dot_general(ks, q, NT, preferred_element_type=F32)
            m_prev = m_s[...]
            m_new = jnp.maximum(m_prev, jnp.max(st, axis=0, keepdims=True))
            p = jnp.exp(st - m_new)
            alpha = jnp.exp(m_prev - m_new)
            l_s[...] = alpha * l_s[...] + jnp.sum(p, axis=0, keepdims=True)
            acc_s[...] = alpha * acc_s[...] + lax.dot_general(vs, p.astype(BF16), TN, preferred_element_type=F32)
            m_s[...] = m_new
            return carry

        lax.fori_loop(0, nk, step, 0)
        ot = acc_s[...] * (1.0 / l_s[...])
        lse_ref[...] = m_s[...] + jnp.log(l_s[...])
        for hh in range(GROUP):
            o_ref[:, hh * HEAD_DIM:(hh + 1) * HEAD_DIM] = ot[:, hh * tq:(hh + 1) * tq].T.astype(BF16)

    return pl.pallas_call(
        body, name="attn_b_fwd", grid=(KV_B, n // tq),
        in_specs=[pl.BlockSpec((GROUP, tq, HEAD_DIM), lambda g, i: (g, i, 0)),
                  pl.BlockSpec((None, t, HEAD_DIM), lambda g, i: (g, 0, 0)),
                  pl.BlockSpec((None, t, HEAD_DIM), lambda g, i: (g, 0, 0))],
        out_specs=[pl.BlockSpec((tq, GROUP * HEAD_DIM), lambda g, i: (i, g)),
                   pl.BlockSpec((None, None, 1, rows), lambda g, i: (g, i, 0, 0))],
        out_shape=[jax.ShapeDtypeStruct((n, HEADS_B * HEAD_DIM), BF16),
                   jax.ShapeDtypeStruct((KV_B, n // tq, 1, rows), F32)],
        scratch_shapes=[pltpu.VMEM((1, rows), F32), pltpu.VMEM((1, rows), F32), pltpu.VMEM((HEAD_DIM, rows), F32)],
        compiler_params=_params(("parallel", "parallel")),
    )(qb, kb, vb)


def _attn_b_bwd(qb, kb, vb, dheads, lse, delta, n):
    t = kb.shape[1]
    nk = t // GB_TK
    tq = GB_TQ
    nq = n // tq
    rows = GROUP * tq

    def body(q_ref, k_ref, v_ref, do_ref, lse_ref, dl_ref, dq_ref, dk_ref, dv_ref):
        j = pl.program_id(1)
        i = pl.program_id(2)
        q = q_ref[...].reshape(rows, HEAD_DIM)
        do = _heads_rows(do_ref, tq)
        ks, vs = k_ref[...], v_ref[...]
        st = lax.dot_general(ks, q, NT, preferred_element_type=F32)
        p = jnp.exp(st - lse_ref[...])
        dpt = lax.dot_general(vs, do, NT, preferred_element_type=F32)
        ds = (p * (dpt - dl_ref[...])).astype(BF16)
        dv_part = lax.dot_general(p.astype(BF16), do, NN, preferred_element_type=F32)
        dk_part = lax.dot_general(ds, q, NN, preferred_element_type=F32)
        dq_part = lax.dot_general(ks, ds, TN, preferred_element_type=F32)

        @pl.when(i == 0)
        def _():
            dk_ref[...] = dk_part
            dv_ref[...] = dv_part

        @pl.when(i != 0)
        def _():
            dk_ref[...] += dk_part
            dv_ref[...] += dv_part

        @pl.when(j == 0)
        def _():
            dq_ref[i] = dq_part

        @pl.when(j != 0)
        def _():
            dq_ref[i] += dq_part

    kv = pl.BlockSpec((None, GB_TK, HEAD_DIM), lambda g, j, i: (g, j, 0))
    row = pl.BlockSpec((None, None, 1, rows), lambda g, j, i: (g, i, 0, 0))
    return pl.pallas_call(
        body, name="attn_b_bwd", grid=(KV_B, nk, nq),
        in_specs=[pl.BlockSpec((GROUP, tq, HEAD_DIM), lambda g, j, i: (g, i, 0)), kv, kv,
                  pl.BlockSpec((tq, GROUP * HEAD_DIM), lambda g, j, i: (i, KV_A + g)), row, row],
        out_specs=[pl.BlockSpec((None, nq, HEAD_DIM, rows), lambda g, j, i: (g, 0, 0, 0)), kv, kv],
        out_shape=[jax.ShapeDtypeStruct((KV_B, nq, HEAD_DIM, rows), F32),
                   jax.ShapeDtypeStruct((KV_B, t, HEAD_DIM), F32),
                   jax.ShapeDtypeStruct((KV_B, t, HEAD_DIM), F32)],
        compiler_params=_params(("parallel", "arbitrary", "arbitrary")),
    )(qb, kb, vb, dheads, lse, delta)


def _delta_rows(dheads, heads):
    n = heads.shape[0]
    tq = GB_TQ
    w = GROUP * HEAD_DIM

    def body(a_ref, b_ref, o_ref):
        prod = a_ref[...].astype(F32) * b_ref[...].astype(F32)
        cols = [jnp.sum(prod[:, hh * HEAD_DIM:(hh + 1) * HEAD_DIM].T, axis=0, keepdims=True) for hh in range(GROUP)]
        o_ref[...] = jnp.concatenate(cols, axis=1)

    blk = pl.BlockSpec((tq, w), lambda g, i: (i, KV_A + g))
    return pl.pallas_call(
        body, name="delta_rows", grid=(KV_B, n // tq),
        in_specs=[blk, blk],
        out_specs=pl.BlockSpec((None, None, 1, GROUP * tq), lambda g, i: (g, i, 0, 0)),
        out_shape=jax.ShapeDtypeStruct((KV_B, n // tq, 1, GROUP * tq), F32),
        compiler_params=_params(("parallel", "parallel")),
    )(dheads, heads)


KWIN = 3 * BLOCK


def _window_scores(q, k_ref, j, n, nb, sink_col):
    c = k_ref.shape[0] - n
    start = pl.multiple_of(jnp.clip(j - 1, 0, nb - 3) * BLOCK, BLOCK)
    kw = k_ref[pl.ds(start, KWIN), :]
    kc = k_ref[pl.ds(n, c), :]
    s_loc = lax.dot_general(q, kw, NT, preferred_element_type=F32)
    s_ctx = lax.dot_general(q, kc, NT, preferred_element_type=F32)
    rows = GROUP * BLOCK
    qpos = j * BLOCK + lax.broadcasted_iota(jnp.int32, (rows, KWIN), 0) % BLOCK
    kpos = start + lax.broadcasted_iota(jnp.int32, (rows, KWIN), 1)
    s_loc = jnp.where(jnp.abs(qpos - kpos) <= WINDOW, s_loc, -jnp.inf)
    m = jnp.maximum(jnp.maximum(jnp.max(s_loc, axis=1, keepdims=True), jnp.max(s_ctx, axis=1, keepdims=True)),
                    sink_col)
    e_loc, e_ctx, e_sink = jnp.exp(s_loc - m), jnp.exp(s_ctx - m), jnp.exp(sink_col - m)
    inv = 1.0 / (jnp.sum(e_loc, axis=1, keepdims=True) + jnp.sum(e_ctx, axis=1, keepdims=True) + e_sink)
    return e_loc * inv, e_ctx * inv, e_sink * inv, start


def _sink_column(sink_ref, g):
    cols = [jnp.broadcast_to(sink_ref[pl.ds(g * GROUP + hh, 1), 0:1], (BLOCK, 1)) for hh in range(GROUP)]
    return jnp.concatenate(cols, axis=0)


def _attn_a_fwd(qa, ka, va, sink_b, n):
    t = ka.shape[1]
    nb = n // BLOCK
    assert nb >= 3

    def body(q_ref, k_ref, v_ref, sink_ref, o_ref):
        g, j = pl.program_id(0), pl.program_id(1)
        q = q_ref[...].reshape(GROUP * BLOCK, HEAD_DIM)
        p_loc, p_ctx, _, start = _window_scores(q, k_ref, j, n, nb, _sink_column(sink_ref, g))
        vw = v_ref[pl.ds(start, KWIN), :]
        vc = v_ref[pl.ds(n, t - n), :]
        o = (lax.dot_general(p_loc.astype(BF16), vw, NN, preferred_element_type=F32)
             + lax.dot_general(p_ctx.astype(BF16), vc, NN, preferred_element_type=F32))
        for hh in range(GROUP):
            o_ref[:, hh * HEAD_DIM:(hh + 1) * HEAD_DIM] = o[hh * BLOCK:(hh + 1) * BLOCK].astype(BF16)

    return pl.pallas_call(
        body, name="attn_a_fwd", grid=(KV_A, nb),
        in_specs=[pl.BlockSpec((GROUP, BLOCK, HEAD_DIM), lambda g, j: (g, j, 0)),
                  pl.BlockSpec((None, t, HEAD_DIM), lambda g, j: (g, 0, 0)),
                  pl.BlockSpec((None, t, HEAD_DIM), lambda g, j: (g, 0, 0)),
                  pl.BlockSpec((HEADS_A, HEAD_DIM), lambda g, j: (0, 0))],
        out_specs=pl.BlockSpec((BLOCK, GROUP * HEAD_DIM), lambda g, j: (j, g)),
        out_shape=jax.ShapeDtypeStruct((n, HEADS_A * HEAD_DIM), BF16),
        compiler_params=_params(("parallel", "parallel")),
    )(qa, ka, va, sink_b)


def _attn_a_bwd(qa, ka, va, sink_b, dheads, n):
    t = ka.shape[1]
    c = t - n
    nb = n // BLOCK

    def body(q_ref, k_ref, v_ref, sink_ref, do_ref, dq_ref, dk_ref, dv_ref, dsink_ref):
        g, j = pl.program_id(0), pl.program_id(1)

        @pl.when(j == 0)
        def _():
            dk_ref[...] = jnp.zeros_like(dk_ref)
            dv_ref[...] = jnp.zeros_like(dv_ref)
            dsink_ref[...] = jnp.zeros_like(dsink_ref)

        q = q_ref[...].reshape(GROUP * BLOCK, HEAD_DIM)
        do = _heads_rows(do_ref, BLOCK)
        p_loc, p_ctx, p_sink, start = _window_scores(q, k_ref, j, n, nb, _sink_column(sink_ref, g))
        kw, vw = k_ref[pl.ds(start, KWIN), :], v_ref[pl.ds(start, KWIN), :]
        kc, vc = k_ref[pl.ds(n, c), :], v_ref[pl.ds(n, c), :]
        dp_loc = lax.dot_general(do, vw, NT, preferred_element_type=F32)
        dp_ctx = lax.dot_general(do, vc, NT, preferred_element_type=F32)
        dl = jnp.sum(p_loc * dp_loc, axis=1, keepdims=True) + jnp.sum(p_ctx * dp_ctx, axis=1, keepdims=True)
        ds_loc = (p_loc * (dp_loc - dl)).astype(BF16)
        ds_ctx = (p_ctx * (dp_ctx - dl)).astype(BF16)
        dq = (lax.dot_general(ds_loc, kw, NN, preferred_element_type=F32)
              + lax.dot_general(ds_ctx, kc, NN, preferred_element_type=F32))
        for hh in range(GROUP):
            dq_ref[hh] = dq[hh * BLOCK:(hh + 1) * BLOCK]
        dk_ref[pl.ds(start, KWIN), :] += lax.dot_general(ds_loc, q, TN, preferred_element_type=F32)
        dv_ref[pl.ds(start, KWIN), :] += lax.dot_general(p_loc.astype(BF16), do, TN, preferred_element_type=F32)
        dk_ref[pl.ds(n, c), :] += lax.dot_general(ds_ctx, q, TN, preferred_element_type=F32)
        dv_ref[pl.ds(n, c), :] += lax.dot_general(p_ctx.astype(BF16), do, TN, preferred_element_type=F32)
        dsk = -(p_sink * dl)
        upd = [jnp.broadcast_to(jnp.sum(dsk[hh * BLOCK:(hh + 1) * BLOCK], axis=0, keepdims=True), (1, HEAD_DIM))
               for hh in range(GROUP)]
        dsink_ref[...] += jnp.concatenate(upd + [jnp.zeros((8 - GROUP, HEAD_DIM), F32)], axis=0)

    res = pl.BlockSpec((None, t, HEAD_DIM), lambda g, j: (g, 0, 0))
    return pl.pallas_call(
        body, name="attn_a_bwd", grid=(KV_A, nb),
        in_specs=[pl.BlockSpec((GROUP, BLOCK, HEAD_DIM), lambda g, j: (g, j, 0)), res, res,
                  pl.BlockSpec((HEADS_A, HEAD_DIM), lambda g, j: (0, 0)),
                  pl.BlockSpec((BLOCK, GROUP * HEAD_DIM), lambda g, j: (j, g))],
        out_specs=[pl.BlockSpec((GROUP, BLOCK, HEAD_DIM), lambda g, j: (g, j, 0)), res, res,
                   pl.BlockSpec((None, 8, HEAD_DIM), lambda g, j: (g, 0, 0))],
        out_shape=[jax.ShapeDtypeStruct((HEADS_A, n, HEAD_DIM), F32),
                   jax.ShapeDtypeStruct((KV_A, t, HEAD_DIM), F32),
                   jax.ShapeDtypeStruct((KV_A, t, HEAD_DIM), F32),
                   jax.ShapeDtypeStruct((KV_A, 8, HEAD_DIM), F32)],
        compiler_params=_params(("parallel", "arbitrary")),
    )(qa, ka, va, sink_b, dheads)


def _ln_stats(r):
    mu = jnp.mean(r, axis=-1, keepdims=True)
    cen = r - mu
    rstd = lax.rsqrt(jnp.mean(cen * cen, axis=-1, keepdims=True) + EPS)
    return cen * rstd, rstd


def _ln_bwd(dy, xhat, rstd, gain):
    dxh = dy * gain
    return rstd * (dxh - jnp.mean(dxh, axis=-1, keepdims=True)
                   - xhat * jnp.mean(dxh * xhat, axis=-1, keepdims=True))


def _accumulate_rows(ref, rows, i):
    pad = [jnp.zeros_like(rows[0])] * (8 - len(rows))
    upd = jnp.concatenate(rows + pad, axis=0)

    @pl.when(i == 0)
    def _():
        ref[...] = upd

    @pl.when(i != 0)
    def _():
        ref[...] += upd


def _colsum(v):
    return jnp.sum(v, axis=0, keepdims=True)


LN_TILE = 128


def _res_ln1(x, a, vec):
    n, d = x.shape

    def body(x_ref, a_ref, v_ref, xh_ref, rs_ref, u_ref):
        r1 = DN_ALPHA * x_ref[...] + v_ref[0:1, :] * a_ref[...]
        xhat, rstd = _ln_stats(r1)
        xh_ref[...] = xhat
        rs_ref[...] = rstd
        x1 = xhat * v_ref[1:2, :] + v_ref[2:3, :]
        u_ref[...] = (x1 * (1.0 + v_ref[3:4, :]) + v_ref[4:5, :]).astype(BF16)

    row = pl.BlockSpec((LN_TILE, d), lambda i: (i, 0))
    return pl.pallas_call(
        body, name="res_ln1", grid=(n // LN_TILE,),
        in_specs=[row, row, pl.BlockSpec((8, d), lambda i: (0, 0))],
        out_specs=[row, pl.BlockSpec((LN_TILE, 1), lambda i: (i, 0)), row],
        out_shape=[jax.ShapeDtypeStruct((n, d), F32), jax.ShapeDtypeStruct((n, 1), F32),
                   jax.ShapeDtypeStruct((n, d), BF16)],
        compiler_params=_params(("parallel",)),
    )(x, a, vec)


def _res_ln2_loss(xhat1, f, target, vec):
    n, d = f.shape

    def body(xh_ref, f_ref, t_ref, v_ref, dr_ref, df_ref, s_ref):
        i = pl.program_id(0)
        x1 = xh_ref[...] * v_ref[1:2, :] + v_ref[2:3, :]
        fv = f_ref[...]
        xhat, rstd = _ln_stats(DN_ALPHA * x1 + v_ref[0:1, :] * fv)
        err = xhat * v_ref[3:4, :] + v_ref[4:5, :] - t_ref[...]
        dy = err * (1.0 / d)
        dr2 = _ln_bwd(dy, xhat, rstd, v_ref[3:4, :])
        dr_ref[...] = dr2
        df_ref[...] = (dr2 * v_ref[0:1, :]).astype(BF16)
        _accumulate_rows(s_ref, [_colsum(dy * xhat), _colsum(dy), _colsum(dr2 * fv),
                                 _colsum(err * err) * (0.5 / d)], i)

    row = pl.BlockSpec((LN_TILE, d), lambda i: (i, 0))
    return pl.pallas_call(
        body, name="res_ln2_loss", grid=(n // LN_TILE,),
        in_specs=[row, row, row, pl.BlockSpec((8, d), lambda i: (0, 0))],
        out_specs=[row, row, pl.BlockSpec((8, d), lambda i: (0, 0))],
        out_shape=[jax.ShapeDtypeStruct((n, d), F32), jax.ShapeDtypeStruct((n, d), BF16),
                   jax.ShapeDtypeStruct((8, d), F32)],
        compiler_params=_params(("arbitrary",)),
    )(xhat1, f, target, vec)


def _ln1_bwd(du2, dr2, xhat1, rstd1, a, vec):
    n, d = du2.shape

    def body(du_ref, dr2_ref, xh_ref, rs_ref, a_ref, v_ref, dxp_ref, da_ref, s_ref):
        i = pl.program_id(0)
        du, xhat = du_ref[...], xh_ref[...]
        x1 = xhat * v_ref[1:2, :] + v_ref[2:3, :]
        dx1 = DN_ALPHA * dr2_ref[...] + du * (1.0 + v_ref[0:1, :])
        dr1 = _ln_bwd(dx1, xhat, rs_ref[...], v_ref[1:2, :])
        dxp_ref[...] = DN_ALPHA * dr1
        da_ref[...] = (dr1 * v_ref[3:4, :]).astype(BF16)
        _accumulate_rows(s_ref, [_colsum(du * x1), _colsum(du), _colsum(dx1 * xhat), _colsum(dx1),
                                 _colsum(dr1 * a_ref[...])], i)

    row = pl.BlockSpec((LN_TILE, d), lambda i: (i, 0))
    return pl.pallas_call(
        body, name="ln1_bwd", grid=(n // LN_TILE,),
        in_specs=[row, row, row, pl.BlockSpec((LN_TILE, 1), lambda i: (i, 0)), row,
                  pl.BlockSpec((8, d), lambda i: (0, 0))],
        out_specs=[row, row, pl.BlockSpec((8, d), lambda i: (0, 0))],
        out_shape=[jax.ShapeDtypeStruct((n, d), F32), jax.ShapeDtypeStruct((n, d), BF16),
                   jax.ShapeDtypeStruct((8, d), F32)],
        compiler_params=_params(("arbitrary",)),
    )(du2, dr2, xhat1, rstd1, a, vec)


def _mod1_bwd(du_all, dxp, x, ctx, mods):
    n, d = x.shape
    nx = n // ROW_TILE

    def body(du_ref, dxp_ref, x_ref, ctx_ref, m_ref, gx_ref, s_ref):
        i = pl.program_id(0)
        du = du_ref[...]
        zero = jnp.zeros((1, d), F32)

        @pl.when(i == 0)
        def _():
            s_ref[...] = jnp.zeros_like(s_ref)

        @pl.when(i < nx)
        def _():
            gx_ref[...] = dxp_ref[...] + du * (1.0 + m_ref[0:1, :])
            s_ref[...] += jnp.concatenate([_colsum(du * x_ref[...]), _colsum(du)] + [zero] * 6, axis=0)

        @pl.when(i >= nx)
        def _():
            s_ref[...] += jnp.concatenate([zero, zero, _colsum(du * ctx_ref[...]), _colsum(du)] + [zero] * 4, axis=0)

    lat = pl.BlockSpec((ROW_TILE, d), lambda i: (jnp.minimum(i, nx - 1), 0))
    return pl.pallas_call(
        body, name="mod1_bwd", grid=(nx + 1,),
        in_specs=[pl.BlockSpec((ROW_TILE, d), lambda i: (i, 0)), lat, lat,
                  pl.BlockSpec((ROW_TILE, d), lambda i: (0, 0)), pl.BlockSpec((8, d), lambda i: (0, 0))],
        out_specs=[lat, pl.BlockSpec((8, d), lambda i: (0, 0))],
        out_shape=[jax.ShapeDtypeStruct((n, d), F32), jax.ShapeDtypeStruct((8, d), F32)],
        compiler_params=_params(("arbitrary",)),
    )(du_all, dxp, x, ctx, mods)


FFN_TM = 512
FFN_TN = 512


def _gate_up(u2, wg, wu):
    n, d = u2.shape
    f = wg.shape[1]

    def body(u_ref, wg_ref, wu_ref, g_ref, up_ref, h_ref):
        u = u_ref[...]
        g = lax.dot_general(u, wg_ref[...], NN, preferred_element_type=F32)
        up = lax.dot_general(u, wu_ref[...], NN, preferred_element_type=F32)
        g_ref[...] = g
        up_ref[...] = up
        h_ref[...] = (g * jax.nn.sigmoid(g) * up).astype(BF16)

    wspec = pl.BlockSpec((d, FFN_TN), lambda j, i: (0, j))
    ospec = pl.BlockSpec((FFN_TM, FFN_TN), lambda j, i: (i, j))
    return pl.pallas_call(
        body, name="gate_up", grid=(f // FFN_TN, n // FFN_TM),
        in_specs=[pl.BlockSpec((FFN_TM, d), lambda j, i: (i, 0)), wspec, wspec],
        out_specs=[ospec, ospec, ospec],
        out_shape=[jax.ShapeDtypeStruct((n, f), F32), jax.ShapeDtypeStruct((n, f), F32),
                   jax.ShapeDtypeStruct((n, f), BF16)],
        compiler_params=_params(("parallel", "parallel")),
    )(u2, wg, wu)


def _glu_bwd(df, wd, g, u):
    n, d = df.shape
    f = wd.shape[0]

    def body(df_ref, wd_ref, g_ref, u_ref, dg_ref, du_ref):
        dh = lax.dot_general(df_ref[...], wd_ref[...], NT, preferred_element_type=F32)
        gv = g_ref[...]
        sig = jax.nn.sigmoid(gv)
        du_ref[...] = (dh * (gv * sig)).astype(BF16)
        dg_ref[...] = (dh * u_ref[...] * (sig * (1.0 + gv * (1.0 - sig)))).astype(BF16)

    ospec = pl.BlockSpec((FFN_TM, FFN_TN), lambda j, i: (i, j))
    return pl.pallas_call(
        body, name="glu_bwd", grid=(f // FFN_TN, n // FFN_TM),
        in_specs=[pl.BlockSpec((FFN_TM, d), lambda j, i: (i, 0)),
                  pl.BlockSpec((FFN_TN, d), lambda j, i: (j, 0)), ospec, ospec],
        out_specs=[ospec, ospec],
        out_shape=[jax.ShapeDtypeStruct((n, f), BF16), jax.ShapeDtypeStruct((n, f), BF16)],
        compiler_params=_params(("parallel", "parallel")),
    )(df, wd, g, u)


def _du2(dg, du, wg, wu):
    n, f = dg.shape
    d = wg.shape[0]
    tm, tn, tk = min(1024, n), 1024, 512
    nk = f // tk

    def body(dg_ref, du_ref, wg_ref, wu_ref, o_ref, acc_ref):
        kk = pl.program_id(2)
        part = (lax.dot_general(dg_ref[...], wg_ref[...], NT, preferred_element_type=F32)
                + lax.dot_general(du_ref[...], wu_ref[...], NT, preferred_element_type=F32))

        @pl.when(kk == 0)
        def _():
            acc_ref[...] = part

        @pl.when(kk != 0)
        def _():
            acc_ref[...] += part

        @pl.when(kk == nk - 1)
        def _():
            o_ref[...] = acc_ref[...]

    aspec = pl.BlockSpec((tm, tk), lambda i, j, kk: (i, kk))
    wspec = pl.BlockSpec((tn, tk), lambda i, j, kk: (j, kk))
    return pl.pallas_call(
        body, name="du2", grid=(n // tm, d // tn, nk),
        in_specs=[aspec, aspec, wspec, wspec],
        out_specs=pl.BlockSpec((tm, tn), lambda i, j, kk: (i, j)),
        out_shape=jax.ShapeDtypeStruct((n, d), F32),
        scratch_shapes=[pltpu.VMEM((tm, tn), F32)],
        compiler_params=_params(("parallel", "parallel", "arbitrary")),
    )(dg, du, wg, wu)


def _rows8(rows, d=D_MODEL):
    rows = [r.reshape(1, d).astype(F32) for r in rows]
    return jnp.concatenate(rows + [jnp.zeros((8 - len(rows), d), F32)], axis=0)


def _layer_fwd_bwd(x, ctx, target, mod, mod_ctx, w_in, w_out, w_gate, w_up, w_down,
                   q_g, k_g, sink, ln1_g, ln1_b, ln2_g, ln2_b):
    n, d = x.shape
    c = ctx.shape[0]
    sh1, sc1, g1, sh2, sc2, g2 = [mod[:, k * d:(k + 1) * d] for k in range(6)]
    csh1, csc1 = mod_ctx[:, 0:d], mod_ctx[:, d:2 * d]
    cos, sin_a, sin_b = _rope_tables(n, c)
    sink_b = jnp.broadcast_to(sink.reshape(HEADS_A, 1), (HEADS_A, HEAD_DIM)).astype(F32)

    u_all = _modulate_rows(x, ctx, _rows8([sc1, sh1, csc1, csh1]))
    h_all = _matmul(u_all, w_in, name="qkv_proj", tm=_fit(n + c, 1088), tn=1024, tk=2048, out_dtype=F32)
    qa, ka, va, qb, kb, vb = _qkv_post(h_all, cos, sin_a, sin_b, q_g, k_g)
    o_a = _attn_a_fwd(qa, ka, va, sink_b, n)
    o_b, lse = _attn_b_fwd(qb, kb, vb, n)
    heads = jnp.concatenate([o_a, o_b], axis=1)
    a = _matmul(heads, w_out, name="out_proj", tm=1024, tn=1024, tk=2048, out_dtype=F32)
    xhat1, rstd1, u2 = _res_ln1(x, a, _rows8([g1, ln1_g, ln1_b, sc2, sh2]))
    gg, uu, hh = _gate_up(u2, w_gate, w_up)
    f = _matmul(hh, w_down, name="ffn_down", tm=1024, tn=1024, tk=512, out_dtype=F32)
    dr2, df, s_ln2 = _res_ln2_loss(xhat1, f, target, _rows8([g2, ln1_g, ln1_b, ln2_g, ln2_b]))

    dgg, duu = _glu_bwd(df, w_down, gg, uu)
    dw_down = _matmul(hh, df, name="dw_down", ta=True, tm=512, tn=1024, tk=512, out_dtype=BF16)
    dw_gate = _matmul(u2, dgg, name="dw_gate", ta=True, tm=1024, tn=512, tk=512, out_dtype=BF16)
    dw_up = _matmul(u2, duu, name="dw_up", ta=True, tm=1024, tn=512, tk=512, out_dtype=BF16)
    du2 = _du2(dgg, duu, w_gate, w_up)
    dxp, da, s_ln1 = _ln1_bwd(du2, dr2, xhat1, rstd1, a, _rows8([sc2, ln1_g, ln1_b, g1]))

    dheads = _matmul(da, w_out, name="d_heads", tb=True, tm=1024, tn=1024, tk=2048, out_dtype=BF16)
    dw_out = _matmul(heads, da, name="dw_out", ta=True, tm=1024, tn=1024, tk=512, out_dtype=BF16)
    delta = _delta_rows(dheads, heads)
    dqa, dka, dva, dsink = _attn_a_bwd(qa, ka, va, sink_b, dheads, n)
    dqb, dkb, dvb = _attn_b_bwd(qb, kb, vb, dheads, lse, delta, n)
    dh_all, s_gain = _qkv_bwd_post(h_all, cos, sin_a, sin_b, q_g, k_g, dqa, dka, dva, dqb, dkb, dvb, n)
    dw_in = _matmul(u_all, dh_all, name="dw_in", ta=True, tm=1024, tn=1024, tk=_fit(n + c, 1088), out_dtype=BF16)
    du_all = _matmul(dh_all, w_in, name="d_u1", tb=True, tm=_fit(n + c, 1088), tn=1024, tk=1024, out_dtype=F32)
    grad_x, s_mod1 = _mod1_bwd(du_all, dxp, x, ctx, _rows8([sc1]))

    dsink_row = jnp.concatenate([dsink[0, 0:GROUP, 0], dsink[1, 0:GROUP, 0]]).reshape(1, HEADS_A)
    misc = jnp.concatenate([s_gain[0:1], s_gain[1:2], dsink_row,
                            jnp.zeros((1, d - 2 * HEAD_DIM - HEADS_A), F32)], axis=1)
    partial = jnp.concatenate([
        s_mod1[1:2], s_mod1[0:1], s_ln1[4:5],
        s_ln1[1:2], s_ln1[0:1], s_ln2[2:3],
        s_mod1[3:4], s_mod1[2:3],
        s_ln1[2:3], s_ln1[3:4], s_ln2[0:1], s_ln2[1:2],
        s_ln2[3:4], misc, jnp.zeros((2, d), F32)], axis=0)
    return grad_x, (dw_in, dw_out, dw_gate, dw_up, dw_down), partial


ANY = pl.BlockSpec(memory_space=pl.ANY)
VMEM_FULL = pl.BlockSpec(memory_space=pltpu.VMEM)
N_CHIP_PEERS = 3


def _me():
    return lax.axis_index("x"), lax.axis_index("y"), lax.axis_index("c")


def _other_chips(x, y):
    return [(1 - x, y), (x, 1 - y), (1 - x, 1 - y)]


def _shard_of(chip):
    return 2 * chip[0] + chip[1]


def _dev_index(x, y, c):
    return 4 * x + 2 * y + c


def _rcopy(src, dst, send_sems, recv_sems, k, dev):
    return pltpu.make_async_remote_copy(src_ref=src, dst_ref=dst, send_sem=send_sems.at[k], recv_sem=recv_sems.at[k],
                                        device_id=dev, device_id_type=MESH)


BIG = (("w_in", (D_MODEL, IN_WIDTH), 1), ("w_out", (MIX_WIDTH, D_MODEL), 0), ("w_gate", (D_MODEL, FFN), 1),
       ("w_up", (D_MODEL, FFN), 1), ("w_down", (FFN, D_MODEL), 0))


def _sub(ref, axis, idx, size):
    start = pl.multiple_of(idx * size, size)
    return ref.at[pl.ds(start, size), :] if axis == 0 else ref.at[:, pl.ds(start, size)]


def _shape_div(shape, axis, parts):
    return tuple(s // parts if a == axis else s for a, s in enumerate(shape))


def _allgather_weights(shards):
    n_arr = len(BIG)

    def body(*refs):
        ins, outs = refs[:n_arr], refs[n_arr:2 * n_arr]
        send_sems, recv_sems, local_sems = refs[2 * n_arr:]
        x, y, c = _me()
        sibling = (x, y, 1 - c)
        chips = _other_chips(x, y)
        s_me = _shard_of((x, y))

        def piece(a, ref, shard, half):
            _, full, axis = BIG[a]
            view = _sub(ref, axis, shard, full[axis] // N_SHARD)
            return _sub(view, 1 - axis, half, full[1 - axis] // 2)

        def my_half(a, half):
            _, full, axis = BIG[a]
            return _sub(ins[a], 1 - axis, half, full[1 - axis] // 2)

        local, first, passed = [], [], []
        for a in range(n_arr):
            _, full, axis = BIG[a]
            cp = pltpu.make_async_copy(ins[a], _sub(outs[a], axis, s_me, full[axis] // N_SHARD), local_sems.at[a])
            cp.start()
            local.append(cp)
        for a in range(n_arr):
            for j, chip in enumerate(chips):
                cp = _rcopy(my_half(a, c), piece(a, outs[a], s_me, c), send_sems, recv_sems, 6 * a + j, (*chip, c))
                cp.start()
                first.append(cp)
        for a in range(n_arr):
            for j, chip in enumerate(chips):
                landed = piece(a, outs[a], _shard_of(chip), c)
                _rcopy(landed, landed, send_sems, recv_sems, 6 * a + j, (*chip, c)).wait_recv()
                cp = _rcopy(landed, landed, send_sems, recv_sems, 6 * a + 3 + j, sibling)
                cp.start()
                passed.append(cp)
        for a in range(n_arr):
            for j, chip in enumerate(chips):
                other = piece(a, outs[a], _shard_of(chip), 1 - c)
                _rcopy(other, other, send_sems, recv_sems, 6 * a + 3 + j, sibling).wait_recv()
        for cp in first + passed:
            cp.wait_send()
        for cp in local:
            cp.wait()

    return pl.pallas_call(
        body, name="allgather_weights",
        in_specs=[ANY] * n_arr, out_specs=[ANY] * n_arr,
        out_shape=[jax.ShapeDtypeStruct(full, BF16) for _, full, _ in BIG],
        scratch_shapes=[pltpu.SemaphoreType.DMA((6 * n_arr,)), pltpu.SemaphoreType.DMA((6 * n_arr,)),
                        pltpu.SemaphoreType.DMA((n_arr,))],
    )(*shards)


def _exchange_halves(dws):
    n_arr = len(BIG)

    def body(*refs):
        ins, outs = refs[:n_arr], refs[n_arr:2 * n_arr]
        send_sems, recv_sems = refs[2 * n_arr:]
        x, y, c = _me()
        copies = []
        for a in range(n_arr):
            _, full, axis = BIG[a]
            src = _sub(ins[a], 1 - axis, 1 - c, full[1 - axis] // 2)
            cp = _rcopy(src, outs[a], send_sems, recv_sems, a, (x, y, 1 - c))
            cp.start()
            copies.append(cp)
        for cp in copies:
            cp.wait()

    return pl.pallas_call(
        body, name="grad_exchange_halves",
        in_specs=[ANY] * n_arr, out_specs=[ANY] * n_arr,
        out_shape=[jax.ShapeDtypeStruct(_shape_div(full, 1 - axis, 2), BF16) for _, full, axis in BIG],
        scratch_shapes=[pltpu.SemaphoreType.DMA((n_arr,)), pltpu.SemaphoreType.DMA((n_arr,))],
    )(*dws)


def _exchange_pieces(halves):
    n_arr = len(BIG)

    def body(*refs):
        ins, outs = refs[:n_arr], refs[n_arr:2 * n_arr]
        send_sems, recv_sems = refs[2 * n_arr:]
        x, y, c = _me()
        copies = []
        for a in range(n_arr):
            _, full, axis = BIG[a]
            for j, chip in enumerate(_other_chips(x, y)):
                src = _sub(ins[a], axis, _shard_of(chip), full[axis] // N_SHARD)
                cp = _rcopy(src, outs[a].at[j], send_sems, recv_sems, N_CHIP_PEERS * a + j, (*chip, c))
                cp.start()
                copies.append(cp)
        for cp in copies:
            cp.wait()

    def piece_shape(full, axis):
        return (N_CHIP_PEERS,) + _shape_div(_shape_div(full, 1 - axis, 2), axis, N_SHARD)

    return pl.pallas_call(
        body, name="grad_exchange_pieces",
        in_specs=[ANY] * n_arr, out_specs=[ANY] * n_arr,
        out_shape=[jax.ShapeDtypeStruct(piece_shape(full, axis), BF16) for _, full, axis in BIG],
        scratch_shapes=[pltpu.SemaphoreType.DMA((N_CHIP_PEERS * n_arr,)), pltpu.SemaphoreType.DMA((N_CHIP_PEERS * n_arr,))],
    )(*halves)


def _join_halves(g_halves):
    n_arr = len(BIG)

    def body(*refs):
        ins, outs = refs[:n_arr], refs[n_arr:2 * n_arr]
        send_sems, recv_sems, local_sems = refs[2 * n_arr:]
        x, y, c = _me()
        copies, local = [], []
        for a in range(n_arr):
            _, full, axis = BIG[a]
            mine = _sub(outs[a], 1 - axis, c, full[1 - axis] // 2)
            cp = pltpu.make_async_copy(ins[a], mine, local_sems.at[a])
            cp.start()
            local.append(cp)
            cp = _rcopy(ins[a], mine, send_sems, recv_sems, a, (x, y, 1 - c))
            cp.start()
            copies.append(cp)
        for a in range(n_arr):
            _, full, axis = BIG[a]
            theirs = _sub(outs[a], 1 - axis, 1 - c, full[1 - axis] // 2)
            _rcopy(theirs, theirs, send_sems, recv_sems, a, (x, y, 1 - c)).wait_recv()
        for cp in copies:
            cp.wait_send()
        for cp in local:
            cp.wait()

    return pl.pallas_call(
        body, name="grad_join_halves",
        in_specs=[ANY] * n_arr, out_specs=[ANY] * n_arr,
        out_shape=[jax.ShapeDtypeStruct(_shape_div(full, axis, N_SHARD), F32) for _, full, axis in BIG],
        scratch_shapes=[pltpu.SemaphoreType.DMA((n_arr,)), pltpu.SemaphoreType.DMA((n_arr,)),
                        pltpu.SemaphoreType.DMA((n_arr,))],
    )(*g_halves)


def _window_sum(name, big, widx, axis, nwin, extra, stacked, out_dtype):
    rows, cols = _shape_div(big.shape, axis, nwin)
    tr = _fit(rows, ROW_TILE)
    nbr = rows // tr

    def body(w_ref, big_ref, ex_ref, o_ref):
        acc = big_ref[...].astype(F32)
        if stacked:
            for j in range(N_CHIP_PEERS):
                acc = acc + ex_ref[j].astype(F32)
        else:
            acc = acc + ex_ref[...].astype(F32)
        o_ref[...] = acc.astype(o_ref.dtype)

    if axis == 0:
        big_spec = pl.BlockSpec((tr, cols), lambda i, w: (w[0] * nbr + i, 0))
    else:
        big_spec = pl.BlockSpec((tr, cols), lambda i, w: (i, w[0]))
    ex_spec = (pl.BlockSpec((N_CHIP_PEERS, tr, cols), lambda i, w: (0, i, 0)) if stacked
               else pl.BlockSpec((tr, cols), lambda i, w: (i, 0)))
    return pl.pallas_call(
        body, name=name,
        grid_spec=pltpu.PrefetchScalarGridSpec(
            num_scalar_prefetch=1, grid=(nbr,), in_specs=[big_spec, ex_spec],
            out_specs=pl.BlockSpec((tr, cols), lambda i, w: (i, 0))),
        out_shape=jax.ShapeDtypeStruct((rows, cols), out_dtype),
        compiler_params=_params(("parallel",)),
    )(widx.reshape(1).astype(jnp.int32), big, extra)


def _reduce_scatter_grads(dws):
    x, y, c = _me()
    s_me = _shard_of((x, y))
    from_sibling = _exchange_halves(dws)
    halves = [_window_sum("grad_sum_halves_" + BIG[a][0], dws[a], c, 1 - BIG[a][2], 2, from_sibling[a], False, BF16)
              for a in range(len(BIG))]
    from_chips = _exchange_pieces(halves)
    g_halves = [_window_sum("grad_sum_pieces_" + BIG[a][0], halves[a], s_me, BIG[a][2], N_SHARD, from_chips[a], True, F32)
                for a in range(len(BIG))]
    return _join_halves(g_halves)


def _gather_rows(block, name):
    r, d = block.shape

    def body(in_ref, out_ref, send_sems, recv_sems):
        x, y, c = _me()
        out_ref[_dev_index(x, y, c)] = in_ref[...]
        copies = []
        for mask in range(1, N_DEV):
            peer = (x ^ (mask >> 2), y ^ ((mask >> 1) & 1), c ^ (mask & 1))
            cp = _rcopy(in_ref, out_ref.at[_dev_index(x, y, c)], send_sems, recv_sems, mask - 1, peer)
            cp.start()
            copies.append((cp, peer))
        for mask in range(1, N_DEV):
            peer = (x ^ (mask >> 2), y ^ ((mask >> 1) & 1), c ^ (mask & 1))
            landed = out_ref.at[_dev_index(*peer)]
            _rcopy(landed, landed, send_sems, recv_sems, mask - 1, peer).wait_recv()
        for cp, _ in copies:
            cp.wait_send()

    return pl.pallas_call(
        body, name=name, in_specs=[VMEM_FULL], out_specs=VMEM_FULL,
        out_shape=jax.ShapeDtypeStruct((N_DEV, r, d), F32),
        scratch_shapes=[pltpu.SemaphoreType.DMA((N_DEV - 1,)), pltpu.SemaphoreType.DMA((N_DEV - 1,))],
    )(block)


ADA_ROWS = 80
ADA_W = 6 * D_MODEL // N_SHARD


def _ada_forward(c_block, cctx_block, w_ada, b_shard):
    d = c_block.shape[1]

    def body(c_ref, cc_ref, w_ref, b_ref, act_ref, mods_ref, raw, mloc, send_sems, recv_sems):
        x, y, c = _me()
        me = _dev_index(x, y, c)
        s_me = _shard_of((x, y))
        raw[72:ADA_ROWS, :] = jnp.zeros((ADA_ROWS - 72, d), F32)
        raw[pl.ds(pl.multiple_of(me * 8, 8), 8), :] = c_ref[...]
        raw[64:72, :] = cc_ref[...]
        sends = []
        for mask in range(1, N_DEV):
            peer = (x ^ (mask >> 2), y ^ ((mask >> 1) & 1), c ^ (mask & 1))
            cp = _rcopy(c_ref, raw.at[pl.ds(pl.multiple_of(me * 8, 8), 8), :], send_sems, recv_sems, mask - 1, peer)
            cp.start()
            sends.append(cp)
        for mask in range(1, N_DEV):
            peer = (x ^ (mask >> 2), y ^ ((mask >> 1) & 1), c ^ (mask & 1))
            landed = raw.at[pl.ds(pl.multiple_of(_dev_index(*peer) * 8, 8), 8), :]
            _rcopy(landed, landed, send_sems, recv_sems, mask - 1, peer).wait_recv()
        v = raw[...]
        act = v * jax.nn.sigmoid(v)
        act_ref[...] = act
        mloc[...] = lax.dot_general(act.astype(BF16), w_ref[...].astype(BF16), NN,
                                    preferred_element_type=F32) + b_ref[...]
        mods_ref[s_me, 0:8, :] = mloc[pl.ds(pl.multiple_of(me * 8, 8), 8), :]
        mods_ref[s_me, 8:16, :] = mloc[64:72, :]
        base = N_DEV - 1
        for j, chip in enumerate(_other_chips(x, y)):
            peer = (*chip, c)
            rows = mloc.at[pl.ds(pl.multiple_of(_dev_index(*peer) * 8, 8), 8), :]
            cp = _rcopy(rows, mods_ref.at[s_me, 0:8, :], send_sems, recv_sems, base + 2 * j, peer)
            cp.start()
            sends.append(cp)
            cp = _rcopy(mloc.at[64:72, :], mods_ref.at[s_me, 8:16, :], send_sems, recv_sems, base + 2 * j + 1, peer)
            cp.start()
            sends.append(cp)
        for j, chip in enumerate(_other_chips(x, y)):
            for part in range(2):
                landed = mods_ref.at[_shard_of(chip), 8 * part:8 * part + 8, :]
                _rcopy(landed, landed, send_sems, recv_sems, base + 2 * j + part, (*chip, c)).wait_recv()
        for cp in sends:
            cp.wait_send()

    n_sem = N_DEV - 1 + 2 * N_CHIP_PEERS
    return pl.pallas_call(
        body, name="ada_forward",
        in_specs=[VMEM_FULL] * 4, out_specs=[VMEM_FULL, VMEM_FULL],
        out_shape=[jax.ShapeDtypeStruct((ADA_ROWS, d), F32), jax.ShapeDtypeStruct((N_SHARD, 16, ADA_W), F32)],
        scratch_shapes=[pltpu.VMEM((ADA_ROWS, d), F32), pltpu.VMEM((ADA_ROWS, ADA_W), F32),
                        pltpu.SemaphoreType.DMA((n_sem,)), pltpu.SemaphoreType.DMA((n_sem,))],
        compiler_params=pltpu.CompilerParams(vmem_limit_bytes=VMEM_LIMIT),
    )(c_block, cctx_block, w_ada, b_shard)


def _small_reduce(gathered):
    d = gathered.shape[2]

    def body(g_ref, o_ref):
        tot = g_ref[0]
        for i in range(1, N_DEV):
            tot = tot + g_ref[i]
        o_ref[...] = tot
        o_ref[0:2, :] = tot[0:2] + tot[6:8]
        o_ref[12:13, :] = jnp.broadcast_to(jnp.sum(tot[12:13], axis=1, keepdims=True), (1, d))

    return pl.pallas_call(body, name="small_reduce", in_specs=[VMEM_FULL], out_specs=VMEM_FULL,
                          out_shape=jax.ShapeDtypeStruct((16, d), F32))(gathered)


def _cctx_grad(gathered, c_ctx):
    d = gathered.shape[2]

    def body(g_ref, c_ref, o_ref):
        tot = g_ref[0, 0:1, :]
        for chip in range(1, N_SHARD):
            tot = tot + g_ref[2 * chip, 0:1, :]
        v = c_ref[...]
        sig = jax.nn.sigmoid(v)
        o_ref[...] = tot * (sig * (1.0 + v * (1.0 - sig)))

    return pl.pallas_call(body, name="cctx_grad", in_specs=[VMEM_FULL, VMEM_FULL], out_specs=VMEM_FULL,
                          out_shape=jax.ShapeDtypeStruct((1, d), F32))(gathered, c_ctx.reshape(1, d))


def _cast_bf16(w, name):
    r, cdim = w.shape
    tr = _fit(r, ROW_TILE)

    def body(w_ref, o_ref):
        o_ref[...] = w_ref[...].astype(BF16)

    spec = pl.BlockSpec((tr, cdim), lambda i: (i, 0))
    return pl.pallas_call(body, name=name, grid=(r // tr,), in_specs=[spec], out_specs=spec,
                          out_shape=jax.ShapeDtypeStruct((r, cdim), BF16), compiler_params=_params(("parallel",)))(w)


def _adamw(w, g, m, v, name):
    r, cdim = w.shape
    tr = _fit(r, 128) if r % (ROW_TILE // 4) == 0 else r
    c1 = 1.0 - ADAM_B1 ** ADAM_STEP
    c2 = 1.0 - ADAM_B2 ** ADAM_STEP

    def body(w_ref, g_ref, m_ref, v_ref, d_ref, nm_ref, nv_ref):
        gv = g_ref[...]
        nm = ADAM_B1 * m_ref[...] + (1.0 - ADAM_B1) * gv
        nv = ADAM_B2 * v_ref[...] + (1.0 - ADAM_B2) * (gv * gv)
        nm_ref[...] = nm
        nv_ref[...] = nv
        d_ref[...] = -ADAM_LR * ((nm / c1) / (jnp.sqrt(nv / c2) + ADAM_EPS) + ADAM_WD * w_ref[...])

    spec = pl.BlockSpec((tr, cdim), lambda i: (i, 0))
    sh = jax.ShapeDtypeStruct((r, cdim), F32)
    return pl.pallas_call(body, name=name, grid=(r // tr,), in_specs=[spec] * 4, out_specs=[spec] * 3,
                          out_shape=[sh, sh, sh], compiler_params=_params(("parallel",)))(w, g, m, v)


SMALL = (("c_ctx", D_MODEL), ("b_ada", 6 * D_MODEL), ("q_norm_g", HEAD_DIM), ("k_norm_g", HEAD_DIM),
         ("sink_logit", HEADS_A), ("ln1_g", D_MODEL), ("ln1_b", D_MODEL), ("ln2_g", D_MODEL), ("ln2_b", D_MODEL))
WEIGHT_ORDER = ("c_ctx", "w_ada", "b_ada", "w_in", "q_norm_g", "k_norm_g", "sink_logit", "w_out", "ln1_g", "ln1_b",
                "w_gate", "w_up", "w_down", "ln2_g", "ln2_b")


def kernel(x, c, ctx, c_ctx, w_ada, b_ada, w_in, q_norm_g, k_norm_g, sink_logit, w_out, ln1_g, ln1_b, w_gate, w_up, w_down, ln2_g, ln2_b, loss_target, m_c_ctx, m_w_ada, m_b_ada, m_w_in, m_q_norm_g, m_k_norm_g, m_sink_logit, m_w_out, m_ln1_g, m_ln1_b, m_w_gate, m_w_up, m_w_down, m_ln2_g, m_ln2_b, v_c_ctx, v_w_ada, v_b_ada, v_w_in, v_q_norm_g, v_k_norm_g, v_sink_logit, v_w_out, v_ln1_g, v_ln1_b, v_w_gate, v_w_up, v_w_down, v_ln2_g, v_ln2_b):
    d = D_MODEL
    w = dict(c_ctx=c_ctx, w_ada=w_ada[0], b_ada=b_ada, w_in=w_in[0], q_norm_g=q_norm_g, k_norm_g=k_norm_g,
             sink_logit=sink_logit, w_out=w_out[0], ln1_g=ln1_g, ln1_b=ln1_b, w_gate=w_gate[0], w_up=w_up[0],
             w_down=w_down[0], ln2_g=ln2_g, ln2_b=ln2_b)
    m = dict(c_ctx=m_c_ctx, w_ada=m_w_ada[0], b_ada=m_b_ada, w_in=m_w_in[0], q_norm_g=m_q_norm_g, k_norm_g=m_k_norm_g,
             sink_logit=m_sink_logit, w_out=m_w_out[0], ln1_g=m_ln1_g, ln1_b=m_ln1_b, w_gate=m_w_gate[0],
             w_up=m_w_up[0], w_down=m_w_down[0], ln2_g=m_ln2_g, ln2_b=m_ln2_b)
    v = dict(c_ctx=v_c_ctx, w_ada=v_w_ada[0], b_ada=v_b_ada, w_in=v_w_in[0], q_norm_g=v_q_norm_g, k_norm_g=v_k_norm_g,
             sink_logit=v_sink_logit, w_out=v_w_out[0], ln1_g=v_ln1_g, ln1_b=v_ln1_b, w_gate=v_w_gate[0],
             w_up=v_w_up[0], w_down=v_w_down[0], ln2_g=v_ln2_g, ln2_b=v_ln2_b)
    mx, my, mc = _me()
    s_me = _shard_of((mx, my))
    me = _dev_index(mx, my, mc)
    pad8 = lambda row: jnp.concatenate([row.reshape(1, -1), jnp.zeros((7, row.size), F32)], axis=0)

    b_shard = lax.dynamic_slice(b_ada, (0, s_me * ADA_W), (1, ADA_W))
    act, mods4 = _ada_forward(pad8(c), pad8(c_ctx), w["w_ada"], b_shard)
    mod = jnp.transpose(mods4[:, 0:1, :], (1, 0, 2)).reshape(1, 6 * d)
    mod_ctx = jnp.transpose(mods4[:, 8:9, :], (1, 0, 2)).reshape(1, 6 * d)

    shards = [_cast_bf16(w[name], "cast_" + name) for name, _, _ in BIG]
    full = _allgather_weights(shards)

    grad_x, dws, partial = _layer_fwd_bwd(x[0], ctx[0], loss_target[0], mod, mod_ctx, *full,
                                          q_norm_g, k_norm_g, sink_logit, ln1_g, ln1_b, ln2_g, ln2_b)

    g_big = _reduce_scatter_grads(list(dws))
    grads = {name: g for (name, _, _), g in zip(BIG, g_big)}

    gathered = _gather_rows(partial, "gather_partials")
    tot = _small_reduce(gathered)
    grads["b_ada"] = tot[0:6].reshape(1, 6 * d)
    grads["ln1_g"], grads["ln1_b"], grads["ln2_g"], grads["ln2_b"] = tot[8:9], tot[9:10], tot[10:11], tot[11:12]
    grads["q_norm_g"] = tot[13:14, 0:HEAD_DIM]
    grads["k_norm_g"] = tot[13:14, HEAD_DIM:2 * HEAD_DIM]
    grads["sink_logit"] = tot[13:14, 2 * HEAD_DIM:2 * HEAD_DIM + HEADS_A]
    loss = tot[12, 0]

    dm_all = gathered[:, 0:6, :].reshape(N_DEV, 6 * d)
    dmc_tot = jnp.concatenate([tot[6:8].reshape(1, 2 * d), jnp.zeros((1, 4 * d), F32)], axis=1)
    dm_rows = jnp.concatenate([pad8(dm_all[i]) for i in range(N_DEV)] + [pad8(dmc_tot), jnp.zeros((8, 6 * d), F32)], axis=0)
    dm_shard = lax.dynamic_slice(dm_rows, (0, s_me * ADA_W), (ADA_ROWS, ADA_W))
    grads["w_ada"] = _matmul(act, dm_shard, name="dw_ada", ta=True, tm=1024, tn=1024, tk=ADA_ROWS, out_dtype=F32)
    dmc_shard = lax.dynamic_slice(pad8(dmc_tot), (0, s_me * ADA_W), (8, ADA_W))
    cc_part = _matmul(dmc_shard, w["w_ada"], name="d_cctx", tb=True, tm=8, tn=1024, tk=1536, out_dtype=F32)
    grads["c_ctx"] = _cctx_grad(_gather_rows(cc_part, "gather_cctx"), c_ctx).reshape(d)

    delta, new_m, new_v = {}, {}, {}
    for name in ("w_ada", "w_in", "w_out", "w_gate", "w_up", "w_down"):
        delta[name], new_m[name], new_v[name] = _adamw(w[name], grads[name], m[name], v[name], "adamw_" + name)
    pack = lambda t: jnp.concatenate([t[name].reshape(1, size) for name, size in SMALL], axis=1)
    pd, pm, pv = _adamw(pack(w), pack(grads), pack(m), pack(v), "adamw_small")
    off = 0
    for name, size in SMALL:
        delta[name], new_m[name], new_v[name] = [t[:, off:off + size].reshape(w[name].shape) for t in (pd, pm, pv)]
        grads[name] = grads[name].reshape(w[name].shape)
        off += size

    lead = lambda name, t: t[None] if name in ("w_ada", "w_in", "w_out", "w_gate", "w_up", "w_down") else t
    outs = [loss, grad_x[None]]
    for group in (grads, delta, new_m, new_v):
        outs += [lead(name, group[name]) for name in WEIGHT_ORDER]
    return tuple(outs)
```

```python
import functools
import math

import jax
import jax.numpy as jnp
from jax import lax
from jax.experimental import pallas as pl
from jax.experimental.pallas import tpu as pltpu

F32 = jnp.float32
BF16 = jnp.bfloat16
MESH = pl.DeviceIdType.MESH

D_MODEL = 2048
HEAD_DIM = 128
HEADS_A = 8
HEADS_B = 8
KV_A = 2
KV_B = 2
GROUP = 4
GRID_W = 64
WINDOW = 128
BLOCK = 128
FFN = 5632
IN_WIDTH = 3072
MIX_WIDTH = 2048
ROPE_THETA = 10000.0
EPS = 1e-6
ATTN_SCALE = HEAD_DIM ** -0.5
DN_ALPHA = 2.0 ** 0.25
N_SHARD = 4
N_DEV = 8

ADAM_LR = 0.001
ADAM_B1 = 0.9
ADAM_B2 = 0.999
ADAM_EPS = 1e-08
ADAM_WD = 0.01
ADAM_STEP = 10

QA0, KA0, VA0, QB0, KB0, VB0 = 0, 1024, 1280, 1536, 2560, 2816

VMEM_LIMIT = 56 * 1024 * 1024
ROW_TILE = 256
NN = (((1,), (0,)), ((), ()))
NT = (((1,), (1,)), ((), ()))
TN = (((0,), (0,)), ((), ()))


def _fit(total, pref):
    step = ROW_TILE // 4
    best = step
    for cand in range(step, pref + 1, step):
        if total % cand == 0:
            best = cand
    return best


def _params(sem=None):
    return pltpu.CompilerParams(dimension_semantics=sem, vmem_limit_bytes=VMEM_LIMIT)


def _matmul(a, b, *, name, ta=False, tb=False, tm, tn, tk, out_dtype):
    m = a.shape[1] if ta else a.shape[0]
    k = a.shape[0] if ta else a.shape[1]
    n = b.shape[0] if tb else b.shape[1]
    assert (b.shape[1] if tb else b.shape[0]) == k
    tm, tn, tk = min(tm, m), min(tn, n), min(tk, k)
    assert m % tm == 0 and n % tn == 0 and k % tk == 0, (name, m, n, k, tm, tn, tk)
    nk = k // tk
    dn = (((0 if ta else 1,), (1 if tb else 0,)), ((), ()))

    def body(a_ref, b_ref, o_ref, acc_ref):
        kk = pl.program_id(2)
        part = lax.dot_general(a_ref[...].astype(BF16), b_ref[...].astype(BF16), dn,
                               preferred_element_type=F32)

        @pl.when(kk == 0)
        def _():
            acc_ref[...] = part

        @pl.when(kk != 0)
        def _():
            acc_ref[...] += part

        @pl.when(kk == nk - 1)
        def _():
            o_ref[...] = acc_ref[...].astype(o_ref.dtype)

    a_spec = (pl.BlockSpec((tk, tm), lambda i, j, kk: (kk, i)) if ta
              else pl.BlockSpec((tm, tk), lambda i, j, kk: (i, kk)))
    b_spec = (pl.BlockSpec((tn, tk), lambda i, j, kk: (j, kk)) if tb
              else pl.BlockSpec((tk, tn), lambda i, j, kk: (kk, j)))
    return pl.pallas_call(
        body, name=name, grid=(m // tm, n // tn, nk),
        in_specs=[a_spec, b_spec],
        out_specs=pl.BlockSpec((tm, tn), lambda i, j, kk: (i, j)),
        out_shape=jax.ShapeDtypeStruct((m, n), out_dtype),
        scratch_shapes=[pltpu.VMEM((tm, tn), F32)],
        compiler_params=_params(("parallel", "parallel", "arbitrary")),
    )(a, b)


def _modulate_rows(x, ctx, mods):
    n, d = x.shape
    c = ctx.shape[0]
    nx = n // ROW_TILE
    assert c == ROW_TILE

    def body(x_ref, ctx_ref, mods_ref, o_ref):
        i = pl.program_id(0)

        @pl.when(i < nx)
        def _():
            o_ref[...] = (x_ref[...] * (1.0 + mods_ref[0:1, :]) + mods_ref[1:2, :]).astype(BF16)

        @pl.when(i >= nx)
        def _():
            o_ref[...] = (ctx_ref[...] * (1.0 + mods_ref[2:3, :]) + mods_ref[3:4, :]).astype(BF16)

    return pl.pallas_call(
        body, name="modulate_rows", grid=(nx + 1,),
        in_specs=[pl.BlockSpec((ROW_TILE, d), lambda i: (jnp.minimum(i, nx - 1), 0)),
                  pl.BlockSpec((ROW_TILE, d), lambda i: (0, 0)),
                  pl.BlockSpec((8, d), lambda i: (0, 0))],
        out_specs=pl.BlockSpec((ROW_TILE, d), lambda i: (i, 0)),
        out_shape=jax.ShapeDtypeStruct((n + c, d), BF16),
        compiler_params=_params(("parallel",)),
    )(x, ctx, mods)


def _rope_tables(n, c):
    rows = n // GRID_W
    row_ids = jnp.repeat(jnp.arange(rows, dtype=F32), GRID_W)
    col_ids = jnp.tile(jnp.arange(GRID_W, dtype=F32), rows)
    axis_dim = HEAD_DIM // 2
    inv_freq = jnp.power(ROPE_THETA, -jnp.arange(0, axis_dim, 2, dtype=F32) / axis_dim)
    ang_r = row_ids[:, None] * inv_freq
    ang_c = col_ids[:, None] * inv_freq
    ang = jnp.concatenate([ang_r, ang_r, ang_c, ang_c], axis=-1)
    cos, sin = jnp.cos(ang), jnp.sin(ang)
    quarter = (jnp.arange(HEAD_DIM) // (HEAD_DIM // 4)) % 2
    sin_a = jnp.where(quarter == 0, -sin, 0.0)
    sin_b = jnp.where(quarter == 1, sin, 0.0)
    pad = lambda t, v: jnp.concatenate([t, jnp.full((c, HEAD_DIM), v, F32)], axis=0)
    return pad(cos, 1.0), pad(sin_a, 0.0), pad(sin_b, 0.0)


def _rope(x, cos, sin_a, sin_b):
    return x * cos + pltpu.roll(x, 96, 1) * sin_a + pltpu.roll(x, 32, 1) * sin_b


def _rope_t(dy, cos, sin_a, sin_b):
    return dy * cos - pltpu.roll(dy, 96, 1) * sin_a - pltpu.roll(dy, 32, 1) * sin_b


def _rms(x):
    r = lax.rsqrt(jnp.mean(x * x, axis=-1, keepdims=True) + EPS)
    return x * r, r


def _qkv_post(h_all, cos, sin_a, sin_b, q_g, k_g):
    t = h_all.shape[0]
    nt = t // ROW_TILE

    def body(h_ref, cos_ref, sa_ref, sb_ref, qg_ref, kg_ref, qa_ref, ka_ref, va_ref, qb_ref, kb_ref, vb_ref):
        cos_, sa, sb = cos_ref[...], sa_ref[...], sb_ref[...]
        sl = lambda off, hh: h_ref[:, off + hh * HEAD_DIM: off + (hh + 1) * HEAD_DIM]
        for hh in range(HEADS_A):
            qa_ref[hh] = (_rope(sl(QA0, hh), cos_, sa, sb) * ATTN_SCALE).astype(BF16)
        for hh in range(KV_A):
            ka_ref[hh] = _rope(sl(KA0, hh), cos_, sa, sb).astype(BF16)
            va_ref[hh] = sl(VA0, hh).astype(BF16)
        for hh in range(HEADS_B):
            xn, _ = _rms(sl(QB0, hh))
            qb_ref[hh] = (_rope(xn * qg_ref[...], cos_, sa, sb) * ATTN_SCALE).astype(BF16)
        for hh in range(KV_B):
            xn, _ = _rms(sl(KB0, hh))
            kb_ref[hh] = _rope(xn * kg_ref[...], cos_, sa, sb).astype(BF16)
            vb_ref[hh] = sl(VB0, hh).astype(BF16)

    tab = pl.BlockSpec((ROW_TILE, HEAD_DIM), lambda i: (i, 0))
    gain = pl.BlockSpec((1, HEAD_DIM), lambda i: (0, 0))
    hs = lambda nh: pl.BlockSpec((nh, ROW_TILE, HEAD_DIM), lambda i: (0, i, 0))
    sh = lambda nh: jax.ShapeDtypeStruct((nh, t, HEAD_DIM), BF16)
    return pl.pallas_call(
        body, name="qkv_post", grid=(nt,),
        in_specs=[pl.BlockSpec((ROW_TILE, IN_WIDTH), lambda i: (i, 0)), tab, tab, tab, gain, gain],
        out_specs=[hs(HEADS_A), hs(KV_A), hs(KV_A), hs(HEADS_B), hs(KV_B), hs(KV_B)],
        out_shape=[sh(HEADS_A), sh(KV_A), sh(KV_A), sh(HEADS_B), sh(KV_B), sh(KV_B)],
        compiler_params=_params(("parallel",)),
    )(h_all, cos, sin_a, sin_b, q_g, k_g)


def _qkv_bwd_post(h_all, cos, sin_a, sin_b, q_g, k_g, dqa, dka, dva, dqb, dkb, dvb, n):
    t = h_all.shape[0]
    nt = t // ROW_TILE
    nx = n // ROW_TILE

    def body(h_ref, cos_ref, sa_ref, sb_ref, qg_ref, kg_ref,
             dqa_ref, dka_ref, dva_ref, dqb_ref, dkb_ref, dvb_ref, dh_ref, gs_ref):
        i = pl.program_id(0)
        cos_, sa, sb = cos_ref[...], sa_ref[...], sb_ref[...]
        latent = (i < nx).astype(F32)
        sl = lambda off, hh: h_ref[:, off + hh * HEAD_DIM: off + (hh + 1) * HEAD_DIM]

        def put(off, hh, val):
            dh_ref[:, off + hh * HEAD_DIM: off + (hh + 1) * HEAD_DIM] = val.astype(BF16)

        def norm_bwd(x, gain, dy):
            xn, r = _rms(x)
            dxh = dy * gain
            dx = r * (dxh - xn * jnp.mean(dxh * xn, axis=-1, keepdims=True))
            return dx, jnp.sum(dy * xn, axis=0, keepdims=True)

        for hh in range(HEADS_A):
            put(QA0, hh, _rope_t(dqa_ref[hh] * (ATTN_SCALE * latent), cos_, sa, sb))
        for hh in range(KV_A):
            put(KA0, hh, _rope_t(dka_ref[hh], cos_, sa, sb))
            put(VA0, hh, dva_ref[hh])
        gq = jnp.zeros((1, HEAD_DIM), F32)
        gk = jnp.zeros((1, HEAD_DIM), F32)
        for hh in range(HEADS_B):
            dq_t = dqb_ref[hh // GROUP, :, (hh % GROUP) * ROW_TILE:(hh % GROUP + 1) * ROW_TILE]
            dy = _rope_t(dq_t.T * (ATTN_SCALE * latent), cos_, sa, sb)
            dx, g = norm_bwd(sl(QB0, hh), qg_ref[...], dy)
            put(QB0, hh, dx)
            gq = gq + g
        for hh in range(KV_B):
            dy = _rope_t(dkb_ref[hh], cos_, sa, sb)
            dx, g = norm_bwd(sl(KB0, hh), kg_ref[...], dy)
            put(KB0, hh, dx)
            gk = gk + g
            put(VB0, hh, dvb_ref[hh])
        upd = jnp.concatenate([gq, gk, jnp.zeros((6, HEAD_DIM), F32)], axis=0)

        @pl.when(i == 0)
        def _():
            gs_ref[...] = upd

        @pl.when(i != 0)
        def _():
            gs_ref[...] += upd

    tab = pl.BlockSpec((ROW_TILE, HEAD_DIM), lambda i: (i, 0))
    gain = pl.BlockSpec((1, HEAD_DIM), lambda i: (0, 0))
    lat = lambda nh: pl.BlockSpec((nh, ROW_TILE, HEAD_DIM), lambda i: (0, jnp.minimum(i, nx - 1), 0))
    full = lambda nh: pl.BlockSpec((nh, ROW_TILE, HEAD_DIM), lambda i: (0, i, 0))
    return pl.pallas_call(
        body, name="qkv_bwd_post", grid=(nt,),
        in_specs=[pl.BlockSpec((ROW_TILE, IN_WIDTH), lambda i: (i, 0)), tab, tab, tab, gain, gain,
                  lat(HEADS_A), full(KV_A), full(KV_A),
                  pl.BlockSpec((KV_B, None, HEAD_DIM, GROUP * ROW_TILE), lambda i: (0, jnp.minimum(i, nx - 1), 0, 0)),
                  full(KV_B), full(KV_B)],
        out_specs=[pl.BlockSpec((ROW_TILE, IN_WIDTH), lambda i: (i, 0)),
                   pl.BlockSpec((8, HEAD_DIM), lambda i: (0, 0))],
        out_shape=[jax.ShapeDtypeStruct((t, IN_WIDTH), BF16), jax.ShapeDtypeStruct((8, HEAD_DIM), F32)],
        compiler_params=_params(("arbitrary",)),
    )(h_all, cos, sin_a, sin_b, q_g, k_g, dqa, dka, dva, dqb, dkb, dvb)


GB_TQ = 256
GB_TK = 256


def _heads_rows(ref2d, tq):
    return jnp.concatenate([ref2d[:, hh * HEAD_DIM:(hh + 1) * HEAD_DIM] for hh in range(GROUP)], axis=0)


def _attn_b_fwd(qb, kb, vb, n):
    t = kb.shape[1]
    nk = t // GB_TK
    tq = GB_TQ
    rows = GROUP * tq

    def body(q_ref, k_ref, v_ref, o_ref, lse_ref, m_s, l_s, acc_s):
        q = q_ref[...].reshape(rows, HEAD_DIM)
        m_s[...] = jnp.full((1, rows), -jnp.inf, F32)
        l_s[...] = jnp.zeros((1, rows), F32)
        acc_s[...] = jnp.zeros((HEAD_DIM, rows), F32)

        def step(j, carry):
            start = pl.multiple_of(j * GB_TK, GB_TK)
            ks = k_ref[pl.ds(start, GB_TK), :]
            vs = v_ref[pl.ds(start, GB_TK), :]
            st = lax.dot_general(ks, q, NT, preferred_element_type=F32)
            m_prev = m_s[...]
            m_new = jnp.maximum(m_prev, jnp.max(st, axis=0, keepdims=True))
            p = jnp.exp(st - m_new)
            alpha = jnp.exp(m_prev - m_new)
            l_s[...] = alpha * l_s[...] + jnp.sum(p, axis=0, keepdims=True)
            acc_s[...] = alpha * acc_s[...] + lax.dot_general(vs, p.astype(BF16), TN, preferred_element_type=F32)
            m_s[...] = m_new
            return carry

        lax.fori_loop(0, nk, step, 0)
        ot = acc_s[...] * (1.0 / l_s[...])
        lse_ref[...] = m_s[...] + jnp.log(l_s[...])
        for hh in range(GROUP):
            o_ref[:, hh * HEAD_DIM:(hh + 1) * HEAD_DIM] = ot[:, hh * tq:(hh + 1) * tq].T.astype(BF16)

    return pl.pallas_call(
        body, name="attn_b_fwd", grid=(KV_B, n // tq),
        in_specs=[pl.BlockSpec((GROUP, tq, HEAD_DIM), lambda g, i: (g, i, 0)),
                  pl.BlockSpec((None, t, HEAD_DIM), lambda g, i: (g, 0, 0)),
                  pl.BlockSpec((None, t, HEAD_DIM), lambda g, i: (g, 0, 0))],
        out_specs=[pl.BlockSpec((tq, GROUP * HEAD_DIM), lambda g, i: (i, g)),
                   pl.BlockSpec((None, None, 1, rows), lambda g, i: (g, i, 0, 0))],
        out_shape=[jax.ShapeDtypeStruct((n, HEADS_B * HEAD_DIM), BF16),
                   jax.ShapeDtypeStruct((KV_B, n // tq, 1, rows), F32)],
        scratch_shapes=[pltpu.VMEM((1, rows), F32), pltpu.VMEM((1, rows), F32), pltpu.VMEM((HEAD_DIM, rows), F32)],
        compiler_params=_params(("parallel", "parallel")),
    )(qb, kb, vb)


def _attn_b_bwd(qb, kb, vb, dheads, lse, delta, n):
    t = kb.shape[1]
    nk = t // GB_TK
    tq = GB_TQ
    nq = n // tq
    rows = GROUP * tq

    def body(q_ref, k_ref, v_ref, do_ref, lse_ref, dl_ref, dq_ref, dk_ref, dv_ref):
        j = pl.program_id(1)
        i = pl.program_id(2)
        q = q_ref[...].reshape(rows, HEAD_DIM)
        do = _heads_rows(do_ref, tq)
        ks, vs = k_ref[...], v_ref[...]
        st = lax.dot_general(ks, q, NT, preferred_element_type=F32)
        p = jnp.exp(st - lse_ref[...])
        dpt = lax.dot_general(vs, do, NT, preferred_element_type=F32)
        ds = (p * (dpt - dl_ref[...])).astype(BF16)
        dv_part = lax.dot_general(p.astype(BF16), do, NN, preferred_element_type=F32)
        dk_part = lax.dot_general(ds, q, NN, preferred_element_type=F32)
        dq_part = lax.dot_general(ks, ds, TN, preferred_element_type=F32)

        @pl.when(i == 0)
        def _():
            dk_ref[...] = dk_part
            dv_ref[...] = dv_part

        @pl.when(i != 0)
        def _():
            dk_ref[...] += dk_part
            dv_ref[...] += dv_part

        @pl.when(j == 0)
        def _():
            dq_ref[i] = dq_part

        @pl.when(j != 0)
        def _():
            dq_ref[i] += dq_part

    kv = pl.BlockSpec((None, GB_TK, HEAD_DIM), lambda g, j, i: (g, j, 0))
    row = pl.BlockSpec((None, None, 1, rows), lambda g, j, i: (g, i, 0, 0))
    return pl.pallas_call(
        body, name="attn_b_bwd", grid=(KV_B, nk, nq),
        in_specs=[pl.BlockSpec((GROUP, tq, HEAD_DIM), lambda g, j, i: (g, i, 0)), kv, kv,
                  pl.BlockSpec((tq, GROUP * HEAD_DIM), lambda g, j, i: (i, KV_A + g)), row, row],
        out_specs=[pl.BlockSpec((None, nq, HEAD_DIM, rows), lambda g, j, i: (g, 0, 0, 0)), kv, kv],
        out_shape=[jax.ShapeDtypeStruct((KV_B, nq, HEAD_DIM, rows), F32),
                   jax.ShapeDtypeStruct((KV_B, t, HEAD_DIM), F32),
                   jax.ShapeDtypeStruct((KV_B, t, HEAD_DIM), F32)],
        compiler_params=_params(("parallel", "arbitrary", "arbitrary")),
    )(qb, kb, vb, dheads, lse, delta)


def _delta_rows(dheads, heads):
    n = heads.shape[0]
    tq = GB_TQ
    w = GROUP * HEAD_DIM

    def body(a_ref, b_ref, o_ref):
        prod = a_ref[...].astype(F32) * b_ref[...].astype(F32)
        cols = [jnp.sum(prod[:, hh * HEAD_DIM:(hh + 1) * HEAD_DIM].T, axis=0, keepdims=True) for hh in range(GROUP)]
        o_ref[...] = jnp.concatenate(cols, axis=1)

    blk = pl.BlockSpec((tq, w), lambda g, i: (i, KV_A + g))
    return pl.pallas_call(
        body, name="delta_rows", grid=(KV_B, n // tq),
        in_specs=[blk, blk],
        out_specs=pl.BlockSpec((None, None, 1, GROUP * tq), lambda g, i: (g, i, 0, 0)),
        out_shape=jax.ShapeDtypeStruct((KV_B, n // tq, 1, GROUP * tq), F32),
        compiler_params=_params(("parallel", "parallel")),
    )(dheads, heads)


KWIN = 3 * BLOCK


def _window_scores(q, k_ref, j, n, nb, sink_col):
    c = k_ref.shape[0] - n
    start = pl.multiple_of(jnp.clip(j - 1, 0, nb - 3) * BLOCK, BLOCK)
    kw = k_ref[pl.ds(start, KWIN), :]
    kc = k_ref[pl.ds(n, c), :]
    s_loc = lax.dot_general(q, kw, NT, preferred_element_type=F32)
    s_ctx = lax.dot_general(q, kc, NT, preferred_element_type=F32)
    rows = GROUP * BLOCK
    qpos = j * BLOCK + lax.broadcasted_iota(jnp.int32, (rows, KWIN), 0) % BLOCK
    kpos = start + lax.broadcasted_iota(jnp.int32, (rows, KWIN), 1)
    s_loc = jnp.where(jnp.abs(qpos - kpos) <= WINDOW, s_loc, -jnp.inf)
    m = jnp.maximum(jnp.maximum(jnp.max(s_loc, axis=1, keepdims=True), jnp.max(s_ctx, axis=1, keepdims=True)),
                    sink_col)
    e_loc, e_ctx, e_sink = jnp.exp(s_loc - m), jnp.exp(s_ctx - m), jnp.exp(sink_col - m)
    inv = 1.0 / (jnp.sum(e_loc, axis=1, keepdims=True) + jnp.sum(e_ctx, axis=1, keepdims=True) + e_sink)
    return e_loc * inv, e_ctx * inv, e_sink * inv, start


def _sink_column(sink_ref, g):
    cols = [jnp.broadcast_to(sink_ref[pl.ds(g * GROUP + hh, 1), 0:1], (BLOCK, 1)) for hh in range(GROUP)]
    return jnp.concatenate(cols, axis=0)


def _attn_a_fwd(qa, ka, va, sink_b, n):
    t = ka.shape[1]
    nb = n // BLOCK
    assert nb >= 3

    def body(q_ref, k_ref, v_ref, sink_ref, o_ref):
        g, j = pl.program_id(0), pl.program_id(1)
        q = q_ref[...].reshape(GROUP * BLOCK, HEAD_DIM)
        p_loc, p_ctx, _, start = _window_scores(q, k_ref, j, n, nb, _sink_column(sink_ref, g))
        vw = v_ref[pl.ds(start, KWIN), :]
        vc = v_ref[pl.ds(n, t - n), :]
        o = (lax.dot_general(p_loc.astype(BF16), vw, NN, preferred_element_type=F32)
             + lax.dot_general(p_ctx.astype(BF16), vc, NN, preferred_element_type=F32))
        for hh in range(GROUP):
            o_ref[:, hh * HEAD_DIM:(hh + 1) * HEAD_DIM] = o[hh * BLOCK:(hh + 1) * BLOCK].astype(BF16)

    return pl.pallas_call(
        body, name="attn_a_fwd", grid=(KV_A, nb),
        in_specs=[pl.BlockSpec((GROUP, BLOCK, HEAD_DIM), lambda g, j: (g, j, 0)),
                  pl.BlockSpec((None, t, HEAD_DIM), lambda g, j: (g, 0, 0)),
                  pl.BlockSpec((None, t, HEAD_DIM), lambda g, j: (g, 0, 0)),
                  pl.BlockSpec((HEADS_A, HEAD_DIM), lambda g, j: (0, 0))],
        out_specs=pl.BlockSpec((BLOCK, GROUP * HEAD_DIM), lambda g, j: (j, g)),
        out_shape=jax.ShapeDtypeStruct((n, HEADS_A * HEAD_DIM), BF16),
        compiler_params=_params(("parallel", "parallel")),
    )(qa, ka, va, sink_b)


def _attn_a_bwd(qa, ka, va, sink_b, dheads, n):
    t = ka.shape[1]
    c = t - n
    nb = n // BLOCK

    def body(q_ref, k_ref, v_ref, sink_ref, do_ref, dq_ref, dk_ref, dv_ref, dsink_ref):
        g, j = pl.program_id(0), pl.program_id(1)

        @pl.when(j == 0)
        def _():
            dk_ref[...] = jnp.zeros_like(dk_ref)
            dv_ref[...] = jnp.zeros_like(dv_ref)
            dsink_ref[...] = jnp.zeros_like(dsink_ref)

        q = q_ref[...].reshape(GROUP * BLOCK, HEAD_DIM)
        do = _heads_rows(do_ref, BLOCK)
        p_loc, p_ctx, p_sink, start = _window_scores(q, k_ref, j, n, nb, _sink_column(sink_ref, g))
        kw, vw = k_ref[pl.ds(start, KWIN), :], v_ref[pl.ds(start, KWIN), :]
        kc, vc = k_ref[pl.ds(n, c), :], v_ref[pl.ds(n, c), :]
        dp_loc = lax.dot_general(do, vw, NT, preferred_element_type=F32)
        dp_ctx = lax.dot_general(do, vc, NT, preferred_element_type=F32)
        dl = jnp.sum(p_loc * dp_loc, axis=1, keepdims=True) + jnp.sum(p_ctx * dp_ctx, axis=1, keepdims=True)
        ds_loc = (p_loc * (dp_loc - dl)).astype(BF16)
        ds_ctx = (p_ctx * (dp_ctx - dl)).astype(BF16)
        dq = (lax.dot_general(ds_loc, kw, NN, preferred_element_type=F32)
              + lax.dot_general(ds_ctx, kc, NN, preferred_element_type=F32))
        for hh in range(GROUP):
            dq_ref[hh] = dq[hh * BLOCK:(hh + 1) * BLOCK]
        dk_ref[pl.ds(start, KWIN), :] += lax.dot_general(ds_loc, q, TN, preferred_element_type=F32)
        dv_ref[pl.ds(start, KWIN), :] += lax.dot_general(p_loc.astype(BF16), do, TN, preferred_element_type=F32)
        dk_ref[pl.ds(n, c), :] += lax.dot_general(ds_ctx, q, TN, preferred_element_type=F32)
        dv_ref[pl.ds(n, c), :] += lax.dot_general(p_ctx.astype(BF16), do, TN, preferred_element_type=F32)
        dsk = -(p_sink * dl)
        upd = [jnp.broadcast_to(jnp.sum(dsk[hh * BLOCK:(hh + 1) * BLOCK], axis=0, keepdims=True), (1, HEAD_DIM))
               for hh in range(GROUP)]
        dsink_ref[...] += jnp.concatenate(upd + [jnp.zeros((8 - GROUP, HEAD_DIM), F32)], axis=0)

    res = pl.BlockSpec((None, t, HEAD_DIM), lambda g, j: (g, 0, 0))
    return pl.pallas_call(
        body, name="attn_a_bwd", grid=(KV_A, nb),
        in_specs=[pl.BlockSpec((GROUP, BLOCK, HEAD_DIM), lambda g, j: (g, j, 0)), res, res,
                  pl.BlockSpec((HEADS_A, HEAD_DIM), lambda g, j: (0, 0)),
                  pl.BlockSpec((BLOCK, GROUP * HEAD_DIM), lambda g, j: (j, g))],
        out_specs=[pl.BlockSpec((GROUP, BLOCK, HEAD_DIM), lambda g, j: (g, j, 0)), res, res,
                   pl.BlockSpec((None, 8, HEAD_DIM), lambda g, j: (g, 0, 0))],
        out_shape=[jax.ShapeDtypeStruct((HEADS_A, n, HEAD_DIM), F32),
                   jax.ShapeDtypeStruct((KV_A, t, HEAD_DIM), F32),
                   jax.ShapeDtypeStruct((KV_A, t, HEAD_DIM), F32),
                   jax.ShapeDtypeStruct((KV_A, 8, HEAD_DIM), F32)],
        compiler_params=_params(("parallel", "arbitrary")),
    )(qa, ka, va, sink_b, dheads)


def _ln_stats(r):
    mu = jnp.mean(r, axis=-1, keepdims=True)
    cen = r - mu
    rstd = lax.rsqrt(jnp.mean(cen * cen, axis=-1, keepdims=True) + EPS)
    return cen * rstd, rstd


def _ln_bwd(dy, xhat, rstd, gain):
    dxh = dy * gain
    return rstd * (dxh - jnp.mean(dxh, axis=-1, keepdims=True)
                   - xhat * jnp.mean(dxh * xhat, axis=-1, keepdims=True))


def _accumulate_rows(ref, rows, i):
    pad = [jnp.zeros_like(rows[0])] * (8 - len(rows))
    upd = jnp.concatenate(rows + pad, axis=0)

    @pl.when(i == 0)
    def _():
        ref[...] = upd

    @pl.when(i != 0)
    def _():
        ref[...] += upd


def _colsum(v):
    return jnp.sum(v, axis=0, keepdims=True)


LN_TILE = 128


def _res_ln1(x, a, vec):
    n, d = x.shape

    def body(x_ref, a_ref, v_ref, xh_ref, rs_ref, u_ref):
        r1 = DN_ALPHA * x_ref[...] + v_ref[0:1, :] * a_ref[...]
        xhat, rstd = _ln_stats(r1)
        xh_ref[...] = xhat
        rs_ref[...] = rstd
        x1 = xhat * v_ref[1:2, :] + v_ref[2:3, :]
        u_ref[...] = (x1 * (1.0 + v_ref[3:4, :]) + v_ref[4:5, :]).astype(BF16)

    row = pl.BlockSpec((LN_TILE, d), lambda i: (i, 0))
    return pl.pallas_call(
        body, name="res_ln1", grid=(n // LN_TILE,),
        in_specs=[row, row, pl.BlockSpec((8, d), lambda i: (0, 0))],
        out_specs=[row, pl.BlockSpec((LN_TILE, 1), lambda i: (i, 0)), row],
        out_shape=[jax.ShapeDtypeStruct((n, d), F32), jax.ShapeDtypeStruct((n, 1), F32),
                   jax.ShapeDtypeStruct((n, d), BF16)],
        compiler_params=_params(("parallel",)),
    )(x, a, vec)


def _res_ln2_loss(xhat1, f, target, vec):
    n, d = f.shape

    def body(xh_ref, f_ref, t_ref, v_ref, dr_ref, df_ref, s_ref):
        i = pl.program_id(0)
        x1 = xh_ref[...] * v_ref[1:2, :] + v_ref[2:3, :]
        fv = f_ref[...]
        xhat, rstd = _ln_stats(DN_ALPHA * x1 + v_ref[0:1, :] * fv)
        err = xhat * v_ref[3:4, :] + v_ref[4:5, :] - t_ref[...]
        dy = err * (1.0 / d)
        dr2 = _ln_bwd(dy, xhat, rstd, v_ref[3:4, :])
        dr_ref[...] = dr2
        df_ref[...] = (dr2 * v_ref[0:1, :]).astype(BF16)
        _accumulate_rows(s_ref, [_colsum(dy * xhat), _colsum(dy), _colsum(dr2 * fv),
                                 _colsum(err * err) * (0.5 / d)], i)

    row = pl.BlockSpec((LN_TILE, d), lambda i: (i, 0))
    return pl.pallas_call(
        body, name="res_ln2_loss", grid=(n // LN_TILE,),
        in_specs=[row, row, row, pl.BlockSpec((8, d), lambda i: (0, 0))],
        out_specs=[row, row, pl.BlockSpec((8, d), lambda i: (0, 0))],
        out_shape=[jax.ShapeDtypeStruct((n, d), F32), jax.ShapeDtypeStruct((n, d), BF16),
                   jax.ShapeDtypeStruct((8, d), F32)],
        compiler_params=_params(("arbitrary",)),
    )(xhat1, f, target, vec)


def _ln1_bwd(du2, dr2, xhat1, rstd1, a, vec):
    n, d = du2.shape

    def body(du_ref, dr2_ref, xh_ref, rs_ref, a_ref, v_ref, dxp_ref, da_ref, s_ref):
        i = pl.program_id(0)
        du, xhat = du_ref[...], xh_ref[...]
        x1 = xhat * v_ref[1:2, :] + v_ref[2:3, :]
        dx1 = DN_ALPHA * dr2_ref[...] + du * (1.0 + v_ref[0:1, :])
        dr1 = _ln_bwd(dx1, xhat, rs_ref[...], v_ref[1:2, :])
        dxp_ref[...] = DN_ALPHA * dr1
        da_ref[...] = (dr1 * v_ref[3:4, :]).astype(BF16)
        _accumulate_rows(s_ref, [_colsum(du * x1), _colsum(du), _colsum(dx1 * xhat), _colsum(dx1),
                                 _colsum(dr1 * a_ref[...])], i)

    row = pl.BlockSpec((LN_TILE, d), lambda i: (i, 0))
    return pl.pallas_call(
        body, name="ln1_bwd", grid=(n // LN_TILE,),
        in_specs=[row, row, row, pl.BlockSpec((LN_TILE, 1), lambda i: (i, 0)), row,
                  pl.BlockSpec((8, d), lambda i: (0, 0))],
        out_specs=[row, row, pl.BlockSpec((8, d), lambda i: (0, 0))],
        out_shape=[jax.ShapeDtypeStruct((n, d), F32), jax.ShapeDtypeStruct((n, d), BF16),
                   jax.ShapeDtypeStruct((8, d), F32)],
        compiler_params=_params(("arbitrary",)),
    )(du2, dr2, xhat1, rstd1, a, vec)


def _mod1_bwd(du_all, dxp, x, ctx, mods):
    n, d = x.shape
    nx = n // ROW_TILE

    def body(du_ref, dxp_ref, x_ref, ctx_ref, m_ref, gx_ref, s_ref):
        i = pl.program_id(0)
        du = du_ref[...]
        zero = jnp.zeros((1, d), F32)

        @pl.when(i == 0)
        def _():
            s_ref[...] = jnp.zeros_like(s_ref)

        @pl.when(i < nx)
        def _():
            gx_ref[...] = dxp_ref[...] + du * (1.0 + m_ref[0:1, :])
            s_ref[...] += jnp.concatenate([_colsum(du * x_ref[...]), _colsum(du)] + [zero] * 6, axis=0)

        @pl.when(i >= nx)
        def _():
            s_ref[...] += jnp.concatenate([zero, zero, _colsum(du * ctx_ref[...]), _colsum(du)] + [zero] * 4, axis=0)

    lat = pl.BlockSpec((ROW_TILE, d), lambda i: (jnp.minimum(i, nx - 1), 0))
    return pl.pallas_call(
        body, name="mod1_bwd", grid=(nx + 1,),
        in_specs=[pl.BlockSpec((ROW_TILE, d), lambda i: (i, 0)), lat, lat,
                  pl.BlockSpec((ROW_TILE, d), lambda i: (0, 0)), pl.BlockSpec((8, d), lambda i: (0, 0))],
        out_specs=[lat, pl.BlockSpec((8, d), lambda i: (0, 0))],
        out_shape=[jax.ShapeDtypeStruct((n, d), F32), jax.ShapeDtypeStruct((8, d), F32)],
        compiler_params=_params(("arbitrary",)),
    )(du_all, dxp, x, ctx, mods)


FFN_TM = 512
FFN_TN = 512


def _gate_up(u2, wg, wu):
    n, d = u2.shape
    f = wg.shape[1]

    def body(u_ref, wg_ref, wu_ref, g_ref, up_ref, h_ref):
        u = u_ref[...]
        g = lax.dot_general(u, wg_ref[...], NN, preferred_element_type=F32)
        up = lax.dot_general(u, wu_ref[...], NN, preferred_element_type=F32)
        g_ref[...] = g
        up_ref[...] = up
        h_ref[...] = (g * jax.nn.sigmoid(g) * up).astype(BF16)

    wspec = pl.BlockSpec((d, FFN_TN), lambda j, i: (0, j))
    ospec = pl.BlockSpec((FFN_TM, FFN_TN), lambda j, i: (i, j))
    return pl.pallas_call(
        body, name="gate_up", grid=(f // FFN_TN, n // FFN_TM),
        in_specs=[pl.BlockSpec((FFN_TM, d), lambda j, i: (i, 0)), wspec, wspec],
        out_specs=[ospec, ospec, ospec],
        out_shape=[jax.ShapeDtypeStruct((n, f), F32), jax.ShapeDtypeStruct((n, f), F32),
                   jax.ShapeDtypeStruct((n, f), BF16)],
        compiler_params=_params(("parallel", "parallel")),
    )(u2, wg, wu)


def _glu_bwd(df, wd, g, u):
    n, d = df.shape
    f = wd.shape[0]

    def body(df_ref, wd_ref, g_ref, u_ref, dg_ref, du_ref):
        dh = lax.dot_general(df_ref[...], wd_ref[...], NT, preferred_element_type=F32)
        gv = g_ref[...]
        sig = jax.nn.sigmoid(gv)
        du_ref[...] = (dh * (gv * sig)).astype(BF16)
        dg_ref[...] = (dh * u_ref[...] * (sig * (1.0 + gv * (1.0 - sig)))).astype(BF16)

    ospec = pl.BlockSpec((FFN_TM, FFN_TN), lambda j, i: (i, j))
    return pl.pallas_call(
        body, name="glu_bwd", grid=(f // FFN_TN, n // FFN_TM),
        in_specs=[pl.BlockSpec((FFN_TM, d), lambda j, i: (i, 0)),
                  pl.BlockSpec((FFN_TN, d), lambda j, i: (j, 0)), ospec, ospec],
        out_specs=[ospec, ospec],
        out_shape=[jax.ShapeDtypeStruct((n, f), BF16), jax.ShapeDtypeStruct((n, f), BF16)],
        compiler_params=_params(("parallel", "parallel")),
    )(df, wd, g, u)


def _du2(dg, du, wg, wu):
    n, f = dg.shape
    d = wg.shape[0]
    tm, tn, tk = min(1024, n), 1024, 512
    nk = f // tk

    def body(dg_ref, du_ref, wg_ref, wu_ref, o_ref, acc_ref):
        kk = pl.program_id(2)
        part = (lax.dot_general(dg_ref[...], wg_ref[...], NT, preferred_element_type=F32)
                + lax.dot_general(du_ref[...], wu_ref[...], NT, preferred_element_type=F32))

        @pl.when(kk == 0)
        def _():
            acc_ref[...] = part

        @pl.when(kk != 0)
        def _():
            acc_ref[...] += part

        @pl.when(kk == nk - 1)
        def _():
            o_ref[...] = acc_ref[...]

    aspec = pl.BlockSpec((tm, tk), lambda i, j, kk: (i, kk))
    wspec = pl.BlockSpec((tn, tk), lambda i, j, kk: (j, kk))
    return pl.pallas_call(
        body, name="du2", grid=(n // tm, d // tn, nk),
        in_specs=[aspec, aspec, wspec, wspec],
        out_specs=pl.BlockSpec((tm, tn), lambda i, j, kk: (i, j)),
        out_shape=jax.ShapeDtypeStruct((n, d), F32),
        scratch_shapes=[pltpu.VMEM((tm, tn), F32)],
        compiler_params=_params(("parallel", "parallel", "arbitrary")),
    )(dg, du, wg, wu)


def _rows8(rows, d=D_MODEL):
    rows = [r.reshape(1, d).astype(F32) for r in rows]
    return jnp.concatenate(rows + [jnp.zeros((8 - len(rows), d), F32)], axis=0)


def _layer_fwd_bwd(x, ctx, target, mod, mod_ctx, w_in, w_out, w_gate, w_up, w_down,
                   q_g, k_g, sink, ln1_g, ln1_b, ln2_g, ln2_b):
    n, d = x.shape
    c = ctx.shape[0]
    sh1, sc1, g1, sh2, sc2, g2 = [mod[:, k * d:(k + 1) * d] for k in range(6)]
    csh1, csc1 = mod_ctx[:, 0:d], mod_ctx[:, d:2 * d]
    cos, sin_a, sin_b = _rope_tables(n, c)
    sink_b = jnp.broadcast_to(sink.reshape(HEADS_A, 1), (HEADS_A, HEAD_DIM)).astype(F32)

    u_all = _modulate_rows(x, ctx, _rows8([sc1, sh1, csc1, csh1]))
    h_all = _matmul(u_all, w_in, name="qkv_proj", tm=_fit(n + c, 1088), tn=1024, tk=2048, out_dtype=F32)
    qa, ka, va, qb, kb, vb = _qkv_post(h_all, cos, sin_a, sin_b, q_g, k_g)
    o_a = _attn_a_fwd(qa, ka, va, sink_b, n)
    o_b, lse = _attn_b_fwd(qb, kb, vb, n)
    heads = jnp.concatenate([o_a, o_b], axis=1)
    a = _matmul(heads, w_out, name="out_proj", tm=1024, tn=1024, tk=2048, out_dtype=F32)
    xhat1, rstd1, u2 = _res_ln1(x, a, _rows8([g1, ln1_g, ln1_b, sc2, sh2]))
    gg, uu, hh = _gate_up(u2, w_gate, w_up)
    f = _matmul(hh, w_down, name="ffn_down", tm=1024, tn=1024, tk=512, out_dtype=F32)
    dr2, df, s_ln2 = _res_ln2_loss(xhat1, f, target, _rows8([g2, ln1_g, ln1_b, ln2_g, ln2_b]))

    dgg, duu = _glu_bwd(df, w_down, gg, uu)
    dw_down = _matmul(hh, df, name="dw_down", ta=True, tm=512, tn=1024, tk=512, out_dtype=BF16)
    dw_gate = _matmul(u2, dgg, name="dw_gate", ta=True, tm=1024, tn=512, tk=512, out_dtype=BF16)
    dw_up = _matmul(u2, duu, name="dw_up", ta=True, tm=1024, tn=512, tk=512, out_dtype=BF16)
    du2 = _du2(dgg, duu, w_gate, w_up)
    dxp, da, s_ln1 = _ln1_bwd(du2, dr2, xhat1, rstd1, a, _rows8([sc2, ln1_g, ln1_b, g1]))

    dheads = _matmul(da, w_out, name="d_heads", tb=True, tm=1024, tn=1024, tk=2048, out_dtype=BF16)
    dw_out = _matmul(heads, da, name="dw_out", ta=True, tm=1024, tn=1024, tk=512, out_dtype=BF16)
    delta = _delta_rows(dheads, heads)
    dqa, dka, dva, dsink = _attn_a_bwd(qa, ka, va, sink_b, dheads, n)
    dqb, dkb, dvb = _attn_b_bwd(qb, kb, vb, dheads, lse, delta, n)
    dh_all, s_gain = _qkv_bwd_post(h_all, cos, sin_a, sin_b, q_g, k_g, dqa, dka, dva, dqb, dkb, dvb, n)
    dw_in = _matmul(u_all, dh_all, name="dw_in", ta=True, tm=1024, tn=1024, tk=_fit(n + c, 1088), out_dtype=BF16)
    du_all = _matmul(dh_all, w_in, name="d_u1", tb=True, tm=_fit(n + c, 1088), tn=1024, tk=1024, out_dtype=F32)
    grad_x, s_mod1 = _mod1_bwd(du_all, dxp, x, ctx, _rows8([sc1]))

    dsink_row = jnp.concatenate([dsink[0, 0:GROUP, 0], dsink[1, 0:GROUP, 0]]).reshape(1, HEADS_A)
    misc = jnp.concatenate([s_gain[0:1], s_gain[1:2], dsink_row,
                            jnp.zeros((1, d - 2 * HEAD_DIM - HEADS_A), F32)], axis=1)
    partial = jnp.concatenate([
        s_mod1[1:2], s_mod1[0:1], s_ln1[4:5],
        s_ln1[1:2], s_ln1[0:1], s_ln2[2:3],
        s_mod1[3:4], s_mod1[2:3],
        s_ln1[2:3], s_ln1[3:4], s_ln2[0:1], s_ln2[1:2],
        s_ln2[3:4], misc, jnp.zeros((2, d), F32)], axis=0)
    return grad_x, (dw_in, dw_out, dw_gate, dw_up, dw_down), partial


ANY = pl.BlockSpec(memory_space=pl.ANY)
VMEM_FULL = pl.BlockSpec(memory_space=pltpu.VMEM)
N_CHIP_PEERS = 3


def _me():
    return lax.axis_index("x"), lax.axis_index("y"), lax.axis_index("c")


def _other_chips(x, y):
    return [(1 - x, y), (x, 1 - y), (1 - x, 1 - y)]


def _shard_of(chip):
    return 2 * chip[0] + chip[1]


def _dev_index(x, y, c):
    return 4 * x + 2 * y + c


def _rcopy(src, dst, send_sems, recv_sems, k, dev):
    return pltpu.make_async_remote_copy(src_ref=src, dst_ref=dst, send_sem=send_sems.at[k], recv_sem=recv_sems.at[k],
                                        device_id=dev, device_id_type=MESH)


BIG = (("w_in", (D_MODEL, IN_WIDTH), 1), ("w_out", (MIX_WIDTH, D_MODEL), 0), ("w_gate", (D_MODEL, FFN), 1),
       ("w_up", (D_MODEL, FFN), 1), ("w_down", (FFN, D_MODEL), 0))


def _sub(ref, axis, idx, size):
    start = pl.multiple_of(idx * size, size)
    return ref.at[pl.ds(start, size), :] if axis == 0 else ref.at[:, pl.ds(start, size)]


def _shape_div(shape, axis, parts):
    return tuple(s // parts if a == axis else s for a, s in enumerate(shape))


def _piece(a, ref, shard, half):
    _, full, axis = BIG[a]
    view = _sub(ref, axis, shard, full[axis] // N_SHARD)
    return _sub(view, 1 - axis, half, full[1 - axis] // 2)


def _allgather_weights(bufs):
    n_arr = len(BIG)

    def body(*refs):
        outs = refs[n_arr:2 * n_arr]
        send_sems, recv_sems = refs[2 * n_arr:]
        x, y, c = _me()
        sibling = (x, y, 1 - c)
        chips = _other_chips(x, y)
        s_me = _shard_of((x, y))
        first, passed = [], []
        for a in range(n_arr):
            mine = _piece(a, outs[a], s_me, c)
            for j, chip in enumerate(chips):
                cp = _rcopy(mine, mine, send_sems, recv_sems, 6 * a + j, (*chip, c))
                cp.start()
                first.append(cp)
        for a in range(n_arr):
            for j, chip in enumerate(chips):
                landed = _piece(a, outs[a], _shard_of(chip), c)
                _rcopy(landed, landed, send_sems, recv_sems, 6 * a + j, (*chip, c)).wait_recv()
                cp = _rcopy(landed, landed, send_sems, recv_sems, 6 * a + 3 + j, sibling)
                cp.start()
                passed.append(cp)
        for a in range(n_arr):
            for j, chip in enumerate(chips):
                other = _piece(a, outs[a], _shard_of(chip), 1 - c)
                _rcopy(other, other, send_sems, recv_sems, 6 * a + 3 + j, sibling).wait_recv()
        for cp in first + passed:
            cp.wait_send()

    return pl.pallas_call(
        body, name="allgather_weights",
        in_specs=[ANY] * n_arr, out_specs=[ANY] * n_arr,
        out_shape=[jax.ShapeDtypeStruct(full, BF16) for _, full, _ in BIG],
        input_output_aliases={a: a for a in range(n_arr)},
        scratch_shapes=[pltpu.SemaphoreType.DMA((6 * n_arr,)), pltpu.SemaphoreType.DMA((6 * n_arr,))],
    )(*bufs)


def _exchange_halves(dws):
    n_arr = len(BIG)

    def body(*refs):
        ins, outs = refs[:n_arr], refs[n_arr:2 * n_arr]
        send_sems, recv_sems = refs[2 * n_arr:]
        x, y, c = _me()
        copies = []
        for a in range(n_arr):
            _, full, axis = BIG[a]
            src = _sub(ins[a], 1 - axis, 1 - c, full[1 - axis] // 2)
            cp = _rcopy(src, outs[a], send_sems, recv_sems, a, (x, y, 1 - c))
            cp.start()
            copies.append(cp)
        for cp in copies:
            cp.wait()

    return pl.pallas_call(
        body, name="grad_exchange_halves",
        in_specs=[ANY] * n_arr, out_specs=[ANY] * n_arr,
        out_shape=[jax.ShapeDtypeStruct(_shape_div(full, 1 - axis, 2), BF16) for _, full, axis in BIG],
        scratch_shapes=[pltpu.SemaphoreType.DMA((n_arr,)), pltpu.SemaphoreType.DMA((n_arr,))],
    )(*dws)


def _exchange_pieces(halves):
    n_arr = len(BIG)

    def body(*refs):
        ins, outs = refs[:n_arr], refs[n_arr:2 * n_arr]
        send_sems, recv_sems = refs[2 * n_arr:]
        x, y, c = _me()
        copies = []
        for a in range(n_arr):
            _, full, axis = BIG[a]
            for j, chip in enumerate(_other_chips(x, y)):
                src = _sub(ins[a], axis, _shard_of(chip), full[axis] // N_SHARD)
                cp = _rcopy(src, outs[a].at[j], send_sems, recv_sems, N_CHIP_PEERS * a + j, (*chip, c))
                cp.start()
                copies.append(cp)
        for cp in copies:
            cp.wait()

    def piece_shape(full, axis):
        return (N_CHIP_PEERS,) + _shape_div(_shape_div(full, 1 - axis, 2), axis, N_SHARD)

    return pl.pallas_call(
        body, name="grad_exchange_pieces",
        in_specs=[ANY] * n_arr, out_specs=[ANY] * n_arr,
        out_shape=[jax.ShapeDtypeStruct(piece_shape(full, axis), BF16) for _, full, axis in BIG],
        scratch_shapes=[pltpu.SemaphoreType.DMA((N_CHIP_PEERS * n_arr,)), pltpu.SemaphoreType.DMA((N_CHIP_PEERS * n_arr,))],
    )(*halves)


def _join_halves(g_halves):
    n_arr = len(BIG)

    def body(*refs):
        ins, outs = refs[:n_arr], refs[n_arr:2 * n_arr]
        send_sems, recv_sems = refs[2 * n_arr:]
        x, y, c = _me()
        copies = []
        for a in range(n_arr):
            cp = _rcopy(ins[a], outs[a], send_sems, recv_sems, a, (x, y, 1 - c))
            cp.start()
            copies.append(cp)
        for cp in copies:
            cp.wait()

    return pl.pallas_call(
        body, name="grad_join_halves",
        in_specs=[ANY] * n_arr, out_specs=[ANY] * n_arr,
        out_shape=[jax.ShapeDtypeStruct(g.shape, F32) for g in g_halves],
        scratch_shapes=[pltpu.SemaphoreType.DMA((n_arr,)), pltpu.SemaphoreType.DMA((n_arr,))],
    )(*g_halves)


def _window_sum(name, big, widx, axis, nwin, extra, stacked, out_dtype):
    rows, cols = _shape_div(big.shape, axis, nwin)
    tr = _fit(rows, ROW_TILE)
    nbr = rows // tr

    def body(w_ref, big_ref, ex_ref, o_ref):
        acc = big_ref[...].astype(F32)
        if stacked:
            for j in range(N_CHIP_PEERS):
                acc = acc + ex_ref[j].astype(F32)
        else:
            acc = acc + ex_ref[...].astype(F32)
        o_ref[...] = acc.astype(o_ref.dtype)

    if axis == 0:
        big_spec = pl.BlockSpec((tr, cols), lambda i, w: (w[0] * nbr + i, 0))
    else:
        big_spec = pl.BlockSpec((tr, cols), lambda i, w: (i, w[0]))
    ex_spec = (pl.BlockSpec((N_CHIP_PEERS, tr, cols), lambda i, w: (0, i, 0)) if stacked
               else pl.BlockSpec((tr, cols), lambda i, w: (i, 0)))
    return pl.pallas_call(
        body, name=name,
        grid_spec=pltpu.PrefetchScalarGridSpec(
            num_scalar_prefetch=1, grid=(nbr,), in_specs=[big_spec, ex_spec],
            out_specs=pl.BlockSpec((tr, cols), lambda i, w: (i, 0))),
        out_shape=jax.ShapeDtypeStruct((rows, cols), out_dtype),
        compiler_params=_params(("parallel",)),
    )(widx.reshape(1).astype(jnp.int32), big, extra)


def _reduce_scatter_grads(dws):
    x, y, c = _me()
    s_me = _shard_of((x, y))
    from_sibling = _exchange_halves(dws)
    halves = [_window_sum("grad_sum_halves_" + BIG[a][0], dws[a], c, 1 - BIG[a][2], 2, from_sibling[a], False, BF16)
              for a in range(len(BIG))]
    from_chips = _exchange_pieces(halves)
    g_halves = [_window_sum("grad_sum_pieces_" + BIG[a][0], halves[a], s_me, BIG[a][2], N_SHARD, from_chips[a], True, F32)
                for a in range(len(BIG))]
    return g_halves, _join_halves(g_halves)


def _gather_rows(block, name):
    r, d = block.shape

    def body(in_ref, out_ref, send_sems, recv_sems):
        x, y, c = _me()
        out_ref[_dev_index(x, y, c)] = in_ref[...]
        copies = []
        for mask in range(1, N_DEV):
            peer = (x ^ (mask >> 2), y ^ ((mask >> 1) & 1), c ^ (mask & 1))
            cp = _rcopy(in_ref, out_ref.at[_dev_index(x, y, c)], send_sems, recv_sems, mask - 1, peer)
            cp.start()
            copies.append((cp, peer))
        for mask in range(1, N_DEV):
            peer = (x ^ (mask >> 2), y ^ ((mask >> 1) & 1), c ^ (mask & 1))
            landed = out_ref.at[_dev_index(*peer)]
            _rcopy(landed, landed, send_sems, recv_sems, mask - 1, peer).wait_recv()
        for cp, _ in copies:
            cp.wait_send()

    return pl.pallas_call(
        body, name=name, in_specs=[VMEM_FULL], out_specs=VMEM_FULL,
        out_shape=jax.ShapeDtypeStruct((N_DEV, r, d), F32),
        scratch_shapes=[pltpu.SemaphoreType.DMA((N_DEV - 1,)), pltpu.SemaphoreType.DMA((N_DEV - 1,))],
    )(block)


ADA_ROWS = 80
ADA_W = 6 * D_MODEL // N_SHARD


def _ada_forward(c_block, cctx_block, w_ada, b_shard):
    d = c_block.shape[1]

    def body(c_ref, cc_ref, w_ref, b_ref, act_ref, mods_ref, raw, mloc, send_sems, recv_sems):
        x, y, c = _me()
        me = _dev_index(x, y, c)
        s_me = _shard_of((x, y))
        raw[72:ADA_ROWS, :] = jnp.zeros((ADA_ROWS - 72, d), F32)
        raw[pl.ds(pl.multiple_of(me * 8, 8), 8), :] = c_ref[...]
        raw[64:72, :] = cc_ref[...]
        sends = []
        for mask in range(1, N_DEV):
            peer = (x ^ (mask >> 2), y ^ ((mask >> 1) & 1), c ^ (mask & 1))
            cp = _rcopy(c_ref, raw.at[pl.ds(pl.multiple_of(me * 8, 8), 8), :], send_sems, recv_sems, mask - 1, peer)
            cp.start()
            sends.append(cp)
        for mask in range(1, N_DEV):
            peer = (x ^ (mask >> 2), y ^ ((mask >> 1) & 1), c ^ (mask & 1))
            landed = raw.at[pl.ds(pl.multiple_of(_dev_index(*peer) * 8, 8), 8), :]
            _rcopy(landed, landed, send_sems, recv_sems, mask - 1, peer).wait_recv()
        v = raw[...]
        act = v * jax.nn.sigmoid(v)
        act_ref[...] = act
        mloc[...] = lax.dot_general(act.astype(BF16), w_ref[...].astype(BF16), NN,
                                    preferred_element_type=F32) + b_ref[...]
        mods_ref[s_me, 0:8, :] = mloc[pl.ds(pl.multiple_of(me * 8, 8), 8), :]
        mods_ref[s_me, 8:16, :] = mloc[64:72, :]
        base = N_DEV - 1
        for j, chip in enumerate(_other_chips(x, y)):
            peer = (*chip, c)
            rows = mloc.at[pl.ds(pl.multiple_of(_dev_index(*peer) * 8, 8), 8), :]
            cp = _rcopy(rows, mods_ref.at[s_me, 0:8, :], send_sems, recv_sems, base + 2 * j, peer)
            cp.start()
            sends.append(cp)
            cp = _rcopy(mloc.at[64:72, :], mods_ref.at[s_me, 8:16, :], send_sems, recv_sems, base + 2 * j + 1, peer)
            cp.start()
            sends.append(cp)
        for j, chip in enumerate(_other_chips(x, y)):
            for part in range(2):
                landed = mods_ref.at[_shard_of(chip), 8 * part:8 * part + 8, :]
                _rcopy(landed, landed, send_sems, recv_sems, base + 2 * j + part, (*chip, c)).wait_recv()
        for cp in sends:
            cp.wait_send()

    n_sem = N_DEV - 1 + 2 * N_CHIP_PEERS
    return pl.pallas_call(
        body, name="ada_forward",
        in_specs=[VMEM_FULL] * 4, out_specs=[VMEM_FULL, VMEM_FULL],
        out_shape=[jax.ShapeDtypeStruct((ADA_ROWS, d), F32), jax.ShapeDtypeStruct((N_SHARD, 16, ADA_W), F32)],
        scratch_shapes=[pltpu.VMEM((ADA_ROWS, d), F32), pltpu.VMEM((ADA_ROWS, ADA_W), F32),
                        pltpu.SemaphoreType.DMA((n_sem,)), pltpu.SemaphoreType.DMA((n_sem,))],
        compiler_params=pltpu.CompilerParams(vmem_limit_bytes=VMEM_LIMIT),
    )(c_block, cctx_block, w_ada, b_shard)


def _small_reduce(gathered):
    d = gathered.shape[2]

    def body(g_ref, o_ref):
        tot = g_ref[0]
        for i in range(1, N_DEV):
            tot = tot + g_ref[i]
        o_ref[...] = tot
        o_ref[0:2, :] = tot[0:2] + tot[6:8]
        o_ref[12:13, :] = jnp.broadcast_to(jnp.sum(tot[12:13], axis=1, keepdims=True), (1, d))

    return pl.pallas_call(body, name="small_reduce", in_specs=[VMEM_FULL], out_specs=VMEM_FULL,
                          out_shape=jax.ShapeDtypeStruct((16, d), F32))(gathered)


def _cctx_grad(gathered, c_ctx):
    d = gathered.shape[2]

    def body(g_ref, c_ref, o_ref):
        tot = g_ref[0, 0:1, :]
        for chip in range(1, N_SHARD):
            tot = tot + g_ref[2 * chip, 0:1, :]
        v = c_ref[...]
        sig = jax.nn.sigmoid(v)
        o_ref[...] = tot * (sig * (1.0 + v * (1.0 - sig)))

    return pl.pallas_call(body, name="cctx_grad", in_specs=[VMEM_FULL, VMEM_FULL], out_specs=VMEM_FULL,
                          out_shape=jax.ShapeDtypeStruct((1, d), F32))(gathered, c_ctx.reshape(1, d))


def _cast_into_full(w, shard, full, axis, name):
    r, cdim = w.shape
    tr = _fit(r, ROW_TILE)
    nbr = r // tr

    def body(s_ref, w_ref, o_ref):
        o_ref[...] = w_ref[...].astype(BF16)

    if axis == 0:
        out_spec = pl.BlockSpec((tr, cdim), lambda i, s: (s[0] * nbr + i, 0))
    else:
        out_spec = pl.BlockSpec((tr, cdim), lambda i, s: (i, s[0]))
    return pl.pallas_call(
        body, name=name,
        grid_spec=pltpu.PrefetchScalarGridSpec(
            num_scalar_prefetch=1, grid=(nbr,), in_specs=[pl.BlockSpec((tr, cdim), lambda i, s: (i, 0))],
            out_specs=out_spec),
        out_shape=jax.ShapeDtypeStruct(full, BF16), compiler_params=_params(("parallel",)),
    )(shard.reshape(1).astype(jnp.int32), w)


def _adamw_halves(w, g_own, g_other, m, v, core, axis, name):
    r, cdim = w.shape
    hr, hc = (r // 2, cdim) if axis == 1 else (r, cdim // 2)
    assert g_own.shape == (hr, hc) and g_other.shape == (hr, hc)
    tr = _fit(hr, 128)
    nb = hr // tr
    c1 = 1.0 - ADAM_B1 ** ADAM_STEP
    c2 = 1.0 - ADAM_B2 ** ADAM_STEP

    def body(c_ref, w_ref, go_ref, gt_ref, m_ref, v_ref, g_ref, d_ref, nm_ref, nv_ref):
        gv = jnp.where(pl.program_id(0) == c_ref[0], go_ref[...], gt_ref[...])
        nm = ADAM_B1 * m_ref[...] + (1.0 - ADAM_B1) * gv
        nv = ADAM_B2 * v_ref[...] + (1.0 - ADAM_B2) * (gv * gv)
        g_ref[...] = gv
        nm_ref[...] = nm
        nv_ref[...] = nv
        d_ref[...] = -ADAM_LR * ((nm / c1) / (jnp.sqrt(nv / c2) + ADAM_EPS) + ADAM_WD * w_ref[...])

    if axis == 1:
        big = pl.BlockSpec((tr, hc), lambda p, i, c: (p * nb + i, 0))
    else:
        big = pl.BlockSpec((tr, hc), lambda p, i, c: (i, p))
    half = pl.BlockSpec((tr, hc), lambda p, i, c: (i, 0))
    sh = jax.ShapeDtypeStruct((r, cdim), F32)
    return pl.pallas_call(
        body, name=name,
        grid_spec=pltpu.PrefetchScalarGridSpec(
            num_scalar_prefetch=1, grid=(2, nb), in_specs=[big, half, half, big, big], out_specs=[big] * 4),
        out_shape=[sh] * 4, compiler_params=_params(("parallel", "parallel")),
    )(core.reshape(1).astype(jnp.int32), w, g_own, g_other, m, v)


def _adamw(w, g, m, v, name):
    r, cdim = w.shape
    tr = _fit(r, 128) if r % (ROW_TILE // 4) == 0 else r
    c1 = 1.0 - ADAM_B1 ** ADAM_STEP
    c2 = 1.0 - ADAM_B2 ** ADAM_STEP

    def body(w_ref, g_ref, m_ref, v_ref, d_ref, nm_ref, nv_ref):
        gv = g_ref[...]
        nm = ADAM_B1 * m_ref[...] + (1.0 - ADAM_B1) * gv
        nv = ADAM_B2 * v_ref[...] + (1.0 - ADAM_B2) * (gv * gv)
        nm_ref[...] = nm
        nv_ref[...] = nv
        d_ref[...] = -ADAM_LR * ((nm / c1) / (jnp.sqrt(nv / c2) + ADAM_EPS) + ADAM_WD * w_ref[...])

    spec = pl.BlockSpec((tr, cdim), lambda i: (i, 0))
    sh = jax.ShapeDtypeStruct((r, cdim), F32)
    return pl.pallas_call(body, name=name, grid=(r // tr,), in_specs=[spec] * 4, out_specs=[spec] * 3,
                          out_shape=[sh, sh, sh], compiler_params=_params(("parallel",)))(w, g, m, v)


SMALL = (("c_ctx", D_MODEL), ("b_ada", 6 * D_MODEL), ("q_norm_g", HEAD_DIM), ("k_norm_g", HEAD_DIM),
         ("sink_logit", HEADS_A), ("ln1_g", D_MODEL), ("ln1_b", D_MODEL), ("ln2_g", D_MODEL), ("ln2_b", D_MODEL))
WEIGHT_ORDER = ("c_ctx", "w_ada", "b_ada", "w_in", "q_norm_g", "k_norm_g", "sink_logit", "w_out", "ln1_g", "ln1_b",
                "w_gate", "w_up", "w_down", "ln2_g", "ln2_b")


def kernel(x, c, ctx, c_ctx, w_ada, b_ada, w_in, q_norm_g, k_norm_g, sink_logit, w_out, ln1_g, ln1_b, w_gate, w_up, w_down, ln2_g, ln2_b, loss_target, m_c_ctx, m_w_ada, m_b_ada, m_w_in, m_q_norm_g, m_k_norm_g, m_sink_logit, m_w_out, m_ln1_g, m_ln1_b, m_w_gate, m_w_up, m_w_down, m_ln2_g, m_ln2_b, v_c_ctx, v_w_ada, v_b_ada, v_w_in, v_q_norm_g, v_k_norm_g, v_sink_logit, v_w_out, v_ln1_g, v_ln1_b, v_w_gate, v_w_up, v_w_down, v_ln2_g, v_ln2_b):
    d = D_MODEL
    w = dict(c_ctx=c_ctx, w_ada=w_ada[0], b_ada=b_ada, w_in=w_in[0], q_norm_g=q_norm_g, k_norm_g=k_norm_g,
             sink_logit=sink_logit, w_out=w_out[0], ln1_g=ln1_g, ln1_b=ln1_b, w_gate=w_gate[0], w_up=w_up[0],
             w_down=w_down[0], ln2_g=ln2_g, ln2_b=ln2_b)
    m = dict(c_ctx=m_c_ctx, w_ada=m_w_ada[0], b_ada=m_b_ada, w_in=m_w_in[0], q_norm_g=m_q_norm_g, k_norm_g=m_k_norm_g,
             sink_logit=m_sink_logit, w_out=m_w_out[0], ln1_g=m_ln1_g, ln1_b=m_ln1_b, w_gate=m_w_gate[0],
             w_up=m_w_up[0], w_down=m_w_down[0], ln2_g=m_ln2_g, ln2_b=m_ln2_b)
    v = dict(c_ctx=v_c_ctx, w_ada=v_w_ada[0], b_ada=v_b_ada, w_in=v_w_in[0], q_norm_g=v_q_norm_g, k_norm_g=v_k_norm_g,
             sink_logit=v_sink_logit, w_out=v_w_out[0], ln1_g=v_ln1_g, ln1_b=v_ln1_b, w_gate=v_w_gate[0],
             w_up=v_w_up[0], w_down=v_w_down[0], ln2_g=v_ln2_g, ln2_b=v_ln2_b)
    mx, my, mc = _me()
    s_me = _shard_of((mx, my))
    me = _dev_index(mx, my, mc)
    pad8 = lambda row: jnp.concatenate([row.reshape(1, -1), jnp.zeros((7, row.size), F32)], axis=0)

    b_shard = lax.dynamic_slice(b_ada, (0, s_me * ADA_W), (1, ADA_W))
    act, mods4 = _ada_forward(pad8(c), pad8(c_ctx), w["w_ada"], b_shard)
    mod = jnp.transpose(mods4[:, 0:1, :], (1, 0, 2)).reshape(1, 6 * d)
    mod_ctx = jnp.transpose(mods4[:, 8:9, :], (1, 0, 2)).reshape(1, 6 * d)

    full = _allgather_weights([_cast_into_full(w[name], s_me, shape, axis, "cast_" + name) for name, shape, axis in BIG])

    grad_x, dws, partial = _layer_fwd_bwd(x[0], ctx[0], loss_target[0], mod, mod_ctx, *full,
                                          q_norm_g, k_norm_g, sink_logit, ln1_g, ln1_b, ln2_g, ln2_b)

    g_own, g_other = _reduce_scatter_grads(list(dws))
    grads, delta, new_m, new_v = {}, {}, {}, {}
    for a, (name, _, axis) in enumerate(BIG):
        grads[name], delta[name], new_m[name], new_v[name] = _adamw_halves(
            w[name], g_own[a], g_other[a], m[name], v[name], mc, axis, "adamw_" + name)

    gathered = _gather_rows(partial, "gather_partials")
    tot = _small_reduce(gathered)
    grads["b_ada"] = tot[0:6].reshape(1, 6 * d)
    grads["ln1_g"], grads["ln1_b"], grads["ln2_g"], grads["ln2_b"] = tot[8:9], tot[9:10], tot[10:11], tot[11:12]
    grads["q_norm_g"] = tot[13:14, 0:HEAD_DIM]
    grads["k_norm_g"] = tot[13:14, HEAD_DIM:2 * HEAD_DIM]
    grads["sink_logit"] = tot[13:14, 2 * HEAD_DIM:2 * HEAD_DIM + HEADS_A]
    loss = tot[12, 0]

    dm_all = gathered[:, 0:6, :].reshape(N_DEV, 6 * d)
    dmc_tot = jnp.concatenate([tot[6:8].reshape(1, 2 * d), jnp.zeros((1, 4 * d), F32)], axis=1)
    dm_rows = jnp.concatenate([pad8(dm_all[i]) for i in range(N_DEV)] + [pad8(dmc_tot), jnp.zeros((8, 6 * d), F32)], axis=0)
    dm_shard = lax.dynamic_slice(dm_rows, (0, s_me * ADA_W), (ADA_ROWS, ADA_W))
    grads["w_ada"] = _matmul(act, dm_shard, name="dw_ada", ta=True, tm=1024, tn=1024, tk=ADA_ROWS, out_dtype=F32)
    dmc_shard = lax.dynamic_slice(pad8(dmc_tot), (0, s_me * ADA_W), (8, ADA_W))
    cc_part = _matmul(dmc_shard, w["w_ada"], name="d_cctx", tb=True, tm=8, tn=1024, tk=1536, out_dtype=F32)
    grads["c_ctx"] = _cctx_grad(_gather_rows(cc_part, "gather_cctx"), c_ctx).reshape(d)

    delta["w_ada"], new_m["w_ada"], new_v["w_ada"] = _adamw(w["w_ada"], grads["w_ada"], m["w_ada"], v["w_ada"],
                                                            "adamw_w_ada")
    pack = lambda t: jnp.concatenate([t[name].reshape(1, size) for name, size in SMALL], axis=1)
    pd, pm, pv = _adamw(pack(w), pack(grads), pack(m), pack(v), "adamw_small")
    off = 0
    for name, size in SMALL:
        delta[name], new_m[name], new_v[name] = [t[:, off:off + size].reshape(w[name].shape) for t in (pd, pm, pv)]
        grads[name] = grads[name].reshape(w[name].shape)
        off += size

    lead = lambda name, t: t[None] if name in ("w_ada", "w_in", "w_out", "w_gate", "w_up", "w_down") else t
    outs = [loss, grad_x[None]]
    for group in (grads, delta, new_m, new_v):
        outs += [lead(name, group[name]) for name in WEIGHT_ORDER]
    return tuple(outs)
```

```python
import functools
import math

import jax
import jax.numpy as jnp
from jax import lax
from jax.experimental import pallas as pl
from jax.experimental.pallas import tpu as pltpu

F32 = jnp.float32
BF16 = jnp.bfloat16
MESH = pl.DeviceIdType.MESH

D_MODEL = 2048
HEAD_DIM = 128
HEADS_A = 8
HEADS_B = 8
KV_A = 2
KV_B = 2
GROUP = 4
GRID_W = 64
WINDOW = 128
BLOCK = 128
FFN = 5632
IN_WIDTH = 3072
MIX_WIDTH = 2048
ROPE_THETA = 10000.0
EPS = 1e-6
ATTN_SCALE = HEAD_DIM ** -0.5
DN_ALPHA = 2.0 ** 0.25
N_SHARD = 4
N_DEV = 8

ADAM_LR = 0.001
ADAM_B1 = 0.9
ADAM_B2 = 0.999
ADAM_EPS = 1e-08
ADAM_WD = 0.01
ADAM_STEP = 10

QA0, KA0, VA0, QB0, KB0, VB0 = 0, 1024, 1280, 1536, 2560, 2816

VMEM_LIMIT = 56 * 1024 * 1024
ROW_TILE = 256
NN = (((1,), (0,)), ((), ()))
NT = (((1,), (1,)), ((), ()))
TN = (((0,), (0,)), ((), ()))


def _fit(total, pref):
    step = ROW_TILE // 4
    best = step
    for cand in range(step, pref + 1, step):
        if total % cand == 0:
            best = cand
    return best


def _params(sem=None):
    return pltpu.CompilerParams(dimension_semantics=sem, vmem_limit_bytes=VMEM_LIMIT)


def _matmul(a, b, *, name, ta=False, tb=False, tm, tn, tk, out_dtype):
    m = a.shape[1] if ta else a.shape[0]
    k = a.shape[0] if ta else a.shape[1]
    n = b.shape[0] if tb else b.shape[1]
    assert (b.shape[1] if tb else b.shape[0]) == k
    tm, tn, tk = min(tm, m), min(tn, n), min(tk, k)
    assert m % tm == 0 and n % tn == 0 and k % tk == 0, (name, m, n, k, tm, tn, tk)
    nk = k // tk
    dn = (((0 if ta else 1,), (1 if tb else 0,)), ((), ()))

    def body(a_ref, b_ref, o_ref, acc_ref):
        kk = pl.program_id(2)
        part = lax.dot_general(a_ref[...].astype(BF16), b_ref[...].astype(BF16), dn,
                               preferred_element_type=F32)

        @pl.when(kk == 0)
        def _():
            acc_ref[...] = part

        @pl.when(kk != 0)
        def _():
            acc_ref[...] += part

        @pl.when(kk == nk - 1)
        def _():
            o_ref[...] = acc_ref[...].astype(o_ref.dtype)

    a_spec = (pl.BlockSpec((tk, tm), lambda i, j, kk: (kk, i)) if ta
              else pl.BlockSpec((tm, tk), lambda i, j, kk: (i, kk)))
    b_spec = (pl.BlockSpec((tn, tk), lambda i, j, kk: (j, kk)) if tb
              else pl.BlockSpec((tk, tn), lambda i, j, kk: (kk, j)))
    return pl.pallas_call(
        body, name=name, grid=(m // tm, n // tn, nk),
        in_specs=[a_spec, b_spec],
        out_specs=pl.BlockSpec((tm, tn), lambda i, j, kk: (i, j)),
        out_shape=jax.ShapeDtypeStruct((m, n), out_dtype),
        scratch_shapes=[pltpu.VMEM((tm, tn), F32)],
        compiler_params=_params(("parallel", "parallel", "arbitrary")),
    )(a, b)


def _modulate_rows(x, ctx, mods):
    n, d = x.shape
    c = ctx.shape[0]
    nx = n // ROW_TILE
    assert c == ROW_TILE

    def body(x_ref, ctx_ref, mods_ref, o_ref):
        i = pl.program_id(0)

        @pl.when(i < nx)
        def _():
            o_ref[...] = (x_ref[...] * (1.0 + mods_ref[0:1, :]) + mods_ref[1:2, :]).astype(BF16)

        @pl.when(i >= nx)
        def _():
            o_ref[...] = (ctx_ref[...] * (1.0 + mods_ref[2:3, :]) + mods_ref[3:4, :]).astype(BF16)

    return pl.pallas_call(
        body, name="modulate_rows", grid=(nx + 1,),
        in_specs=[pl.BlockSpec((ROW_TILE, d), lambda i: (jnp.minimum(i, nx - 1), 0)),
                  pl.BlockSpec((ROW_TILE, d), lambda i: (0, 0)),
                  pl.BlockSpec((8, d), lambda i: (0, 0))],
        out_specs=pl.BlockSpec((ROW_TILE, d), lambda i: (i, 0)),
        out_shape=jax.ShapeDtypeStruct((n + c, d), BF16),
        compiler_params=_params(("parallel",)),
    )(x, ctx, mods)


def _rope_tables(n, c):
    rows = n // GRID_W
    row_ids = jnp.repeat(jnp.arange(rows, dtype=F32), GRID_W)
    col_ids = jnp.tile(jnp.arange(GRID_W, dtype=F32), rows)
    axis_dim = HEAD_DIM // 2
    inv_freq = jnp.power(ROPE_THETA, -jnp.arange(0, axis_dim, 2, dtype=F32) / axis_dim)
    ang_r = row_ids[:, None] * inv_freq
    ang_c = col_ids[:, None] * inv_freq
    ang = jnp.concatenate([ang_r, ang_r, ang_c, ang_c], axis=-1)
    cos, sin = jnp.cos(ang), jnp.sin(ang)
    quarter = (jnp.arange(HEAD_DIM) // (HEAD_DIM // 4)) % 2
    sin_a = jnp.where(quarter == 0, -sin, 0.0)
    sin_b = jnp.where(quarter == 1, sin, 0.0)
    pad = lambda t, v: jnp.concatenate([t, jnp.full((c, HEAD_DIM), v, F32)], axis=0)
    return pad(cos, 1.0), pad(sin_a, 0.0), pad(sin_b, 0.0)


def _rope(x, cos, sin_a, sin_b):
    return x * cos + pltpu.roll(x, 96, 1) * sin_a + pltpu.roll(x, 32, 1) * sin_b


def _rope_t(dy, cos, sin_a, sin_b):
    return dy * cos - pltpu.roll(dy, 96, 1) * sin_a - pltpu.roll(dy, 32, 1) * sin_b


def _rms(x):
    r = lax.rsqrt(jnp.mean(x * x, axis=-1, keepdims=True) + EPS)
    return x * r, r


def _qkv_post(h_all, cos, sin_a, sin_b, q_g, k_g):
    t = h_all.shape[0]
    nt = t // ROW_TILE

    def body(h_ref, cos_ref, sa_ref, sb_ref, qg_ref, kg_ref, qa_ref, ka_ref, va_ref, qb_ref, kb_ref, vb_ref):
        cos_, sa, sb = cos_ref[...], sa_ref[...], sb_ref[...]
        sl = lambda off, hh: h_ref[:, off + hh * HEAD_DIM: off + (hh + 1) * HEAD_DIM]
        for hh in range(HEADS_A):
            qa_ref[hh] = (_rope(sl(QA0, hh), cos_, sa, sb) * ATTN_SCALE).astype(BF16)
        for hh in range(KV_A):
            ka_ref[hh] = _rope(sl(KA0, hh), cos_, sa, sb).astype(BF16)
            va_ref[hh] = sl(VA0, hh).astype(BF16)
        for hh in range(HEADS_B):
            xn, _ = _rms(sl(QB0, hh))
            qb_ref[hh] = (_rope(xn * qg_ref[...], cos_, sa, sb) * ATTN_SCALE).astype(BF16)
        for hh in range(KV_B):
            xn, _ = _rms(sl(KB0, hh))
            kb_ref[hh] = _rope(xn * kg_ref[...], cos_, sa, sb).astype(BF16)
            vb_ref[hh] = sl(VB0, hh).astype(BF16)

    tab = pl.BlockSpec((ROW_TILE, HEAD_DIM), lambda i: (i, 0))
    gain = pl.BlockSpec((1, HEAD_DIM), lambda i: (0, 0))
    hs = lambda nh: pl.BlockSpec((nh, ROW_TILE, HEAD_DIM), lambda i: (0, i, 0))
    sh = lambda nh: jax.ShapeDtypeStruct((nh, t, HEAD_DIM), BF16)
    return pl.pallas_call(
        body, name="qkv_post", grid=(nt,),
        in_specs=[pl.BlockSpec((ROW_TILE, IN_WIDTH), lambda i: (i, 0)), tab, tab, tab, gain, gain],
        out_specs=[hs(HEADS_A), hs(KV_A), hs(KV_A), hs(HEADS_B), hs(KV_B), hs(KV_B)],
        out_shape=[sh(HEADS_A), sh(KV_A), sh(KV_A), sh(HEADS_B), sh(KV_B), sh(KV_B)],
        compiler_params=_params(("parallel",)),
    )(h_all, cos, sin_a, sin_b, q_g, k_g)


def _qkv_bwd_post(h_all, cos, sin_a, sin_b, q_g, k_g, dqa, dka, dva, dqb, dkb, dvb, n):
    t = h_all.shape[0]
    nt = t // ROW_TILE
    nx = n // ROW_TILE

    def body(h_ref, cos_ref, sa_ref, sb_ref, qg_ref, kg_ref,
             dqa_ref, dka_ref, dva_ref, dqb_ref, dkb_ref, dvb_ref, dh_ref, gs_ref):
        i = pl.program_id(0)
        cos_, sa, sb = cos_ref[...], sa_ref[...], sb_ref[...]
        latent = (i < nx).astype(F32)
        sl = lambda off, hh: h_ref[:, off + hh * HEAD_DIM: off + (hh + 1) * HEAD_DIM]

        def put(off, hh, val):
            dh_ref[:, off + hh * HEAD_DIM: off + (hh + 1) * HEAD_DIM] = val.astype(BF16)

        def norm_bwd(x, gain, dy):
            xn, r = _rms(x)
            dxh = dy * gain
            dx = r * (dxh - xn * jnp.mean(dxh * xn, axis=-1, keepdims=True))
            return dx, jnp.sum(dy * xn, axis=0, keepdims=True)

        for hh in range(HEADS_A):
            put(QA0, hh, _rope_t(dqa_ref[hh] * (ATTN_SCALE * latent), cos_, sa, sb))
        for hh in range(KV_A):
            put(KA0, hh, _rope_t(dka_ref[hh], cos_, sa, sb))
            put(VA0, hh, dva_ref[hh])
        gq = jnp.zeros((1, HEAD_DIM), F32)
        gk = jnp.zeros((1, HEAD_DIM), F32)
        for hh in range(HEADS_B):
            dq_t = dqb_ref[hh // GROUP, :, (hh % GROUP) * ROW_TILE:(hh % GROUP + 1) * ROW_TILE]
            dy = _rope_t(dq_t.T * (ATTN_SCALE * latent), cos_, sa, sb)
            dx, g = norm_bwd(sl(QB0, hh), qg_ref[...], dy)
            put(QB0, hh, dx)
            gq = gq + g
        for hh in range(KV_B):
            dy = _rope_t(dkb_ref[hh], cos_, sa, sb)
            dx, g = norm_bwd(sl(KB0, hh), kg_ref[...], dy)
            put(KB0, hh, dx)
            gk = gk + g
            put(VB0, hh, dvb_ref[hh])
        upd = jnp.concatenate([gq, gk, jnp.zeros((6, HEAD_DIM), F32)], axis=0)

        @pl.when(i == 0)
        def _():
            gs_ref[...] = upd

        @pl.when(i != 0)
        def _():
            gs_ref[...] += upd

    tab = pl.BlockSpec((ROW_TILE, HEAD_DIM), lambda i: (i, 0))
    gain = pl.BlockSpec((1, HEAD_DIM), lambda i: (0, 0))
    lat = lambda nh: pl.BlockSpec((nh, ROW_TILE, HEAD_DIM), lambda i: (0, jnp.minimum(i, nx - 1), 0))
    full = lambda nh: pl.BlockSpec((nh, ROW_TILE, HEAD_DIM), lambda i: (0, i, 0))
    return pl.pallas_call(
        body, name="qkv_bwd_post", grid=(nt,),
        in_specs=[pl.BlockSpec((ROW_TILE, IN_WIDTH), lambda i: (i, 0)), tab, tab, tab, gain, gain,
                  lat(HEADS_A), full(KV_A), full(KV_A),
                  pl.BlockSpec((KV_B, None, HEAD_DIM, GROUP * ROW_TILE), lambda i: (0, jnp.minimum(i, nx - 1), 0, 0)),
                  full(KV_B), full(KV_B)],
        out_specs=[pl.BlockSpec((ROW_TILE, IN_WIDTH), lambda i: (i, 0)),
                   pl.BlockSpec((8, HEAD_DIM), lambda i: (0, 0))],
        out_shape=[jax.ShapeDtypeStruct((t, IN_WIDTH), BF16), jax.ShapeDtypeStruct((8, HEAD_DIM), F32)],
        compiler_params=_params(("arbitrary",)),
    )(h_all, cos, sin_a, sin_b, q_g, k_g, dqa, dka, dva, dqb, dkb, dvb)


GB_TQ = 256
GB_TK = 256


def _heads_rows(ref2d, tq):
    return jnp.concatenate([ref2d[:, hh * HEAD_DIM:(hh + 1) * HEAD_DIM] for hh in range(GROUP)], axis=0)


def _attn_b_fwd(qb, kb, vb, n):
    t = kb.shape[1]
    nk = t // GB_TK
    tq = GB_TQ
    rows = GROUP * tq

    def body(q_ref, k_ref, v_ref, o_ref, lse_ref, m_s, l_s, acc_s):
        q = q_ref[...].reshape(rows, HEAD_DIM)
        m_s[...] = jnp.full((1, rows), -jnp.inf, F32)
        l_s[...] = jnp.zeros((1, rows), F32)
        acc_s[...] = jnp.zeros((HEAD_DIM, rows), F32)

        def step(j, carry):
            start = pl.multiple_of(j * GB_TK, GB_TK)
            ks = k_ref[pl.ds(start, GB_TK), :]
            vs = v_ref[pl.ds(start, GB_TK), :]
            st = lax.dot_general(ks, q, NT, preferred_element_type=F32)
            m_prev = m_s[...]
            m_new = jnp.maximum(m_prev, jnp.max(st, axis=0, keepdims=True))
            p = jnp.exp(st - m_new)
            alpha = jnp.exp(m_prev - m_new)
            l_s[...] = alpha * l_s[...] + jnp.sum(p, axis=0, keepdims=True)
            acc_s[...] = alpha * acc_s[...] + lax.dot_general(vs, p.astype(BF16), TN, preferred_element_type=F32)
            m_s[...] = m_new
            return carry

        lax.fori_loop(0, nk, step, 0)
        ot = acc_s[...] * (1.0 / l_s[...])
        lse_ref[...] = m_s[...] + jnp.log(l_s[...])
        for hh in range(GROUP):
            o_ref[:, hh * HEAD_DIM:(hh + 1) * HEAD_DIM] = ot[:, hh * tq:(hh + 1) * tq].T.astype(BF16)

    return pl.pallas_call(
        body, name="attn_b_fwd", grid=(KV_B, n // tq),
        in_specs=[pl.BlockSpec((GROUP, tq, HEAD_DIM), lambda g, i: (g, i, 0)),
                  pl.BlockSpec((None, t, HEAD_DIM), lambda g, i: (g, 0, 0)),
                  pl.BlockSpec((None, t, HEAD_DIM), lambda g, i: (g, 0, 0))],
        out_specs=[pl.BlockSpec((tq, GROUP * HEAD_DIM), lambda g, i: (i, g)),
                   pl.BlockSpec((None, None, 1, rows), lambda g, i: (g, i, 0, 0))],
        out_shape=[jax.ShapeDtypeStruct((n, HEADS_B * HEAD_DIM), BF16),
                   jax.ShapeDtypeStruct((KV_B, n // tq, 1, rows), F32)],
        scratch_shapes=[pltpu.VMEM((1, rows), F32), pltpu.VMEM((1, rows), F32), pltpu.VMEM((HEAD_DIM, rows), F32)],
        compiler_params=_params(("parallel", "parallel")),
    )(qb, kb, vb)


def _attn_b_bwd(qb, kb, vb, dheads, lse, delta, n):
    t = kb.shape[1]
    nk = t // GB_TK
    tq = GB_TQ
    nq = n // tq
    rows = GROUP * tq

    def body(q_ref, k_ref, v_ref, do_ref, lse_ref, dl_ref, dq_ref, dk_ref, dv_ref):
        j = pl.program_id(1)
        i = pl.program_id(2)
        q = q_ref[...].reshape(rows, HEAD_DIM)
        do = _heads_rows(do_ref, tq)
        ks, vs = k_ref[...], v_ref[...]
        st = lax.dot_general(ks, q, NT, preferred_element_type=F32)
        p = jnp.exp(st - lse_ref[...])
        dpt = lax.dot_general(vs, do, NT, preferred_element_type=F32)
        ds = (p * (dpt - dl_ref[...])).astype(BF16)
        dv_part = lax.dot_general(p.astype(BF16), do, NN, preferred_element_type=F32)
        dk_part = lax.dot_general(ds, q, NN, preferred_element_type=F32)
        dq_part = lax.dot_general(ks, ds, TN, preferred_element_type=F32)

        @pl.when(i == 0)
        def _():
            dk_ref[...] = dk_part
            dv_ref[...] = dv_part

        @pl.when(i != 0)
        def _():
            dk_ref[...] += dk_part
            dv_ref[...] += dv_part

        @pl.when(j == 0)
        def _():
            dq_ref[i] = dq_part

        @pl.when(j != 0)
        def _():
            dq_ref[i] += dq_part

    kv = pl.BlockSpec((None, GB_TK, HEAD_DIM), lambda g, j, i: (g, j, 0))
    row = pl.BlockSpec((None, None, 1, rows), lambda g, j, i: (g, i, 0, 0))
    return pl.pallas_call(
        body, name="attn_b_bwd", grid=(KV_B, nk, nq),
        in_specs=[pl.BlockSpec((GROUP, tq, HEAD_DIM), lambda g, j, i: (g, i, 0)), kv, kv,
                  pl.BlockSpec((tq, GROUP * HEAD_DIM), lambda g, j, i: (i, KV_A + g)), row, row],
        out_specs=[pl.BlockSpec((None, nq, HEAD_DIM, rows), lambda g, j, i: (g, 0, 0, 0)), kv, kv],
        out_shape=[jax.ShapeDtypeStruct((KV_B, nq, HEAD_DIM, rows), F32),
                   jax.ShapeDtypeStruct((KV_B, t, HEAD_DIM), F32),
                   jax.ShapeDtypeStruct((KV_B, t, HEAD_DIM), F32)],
        compiler_params=_params(("parallel", "arbitrary", "arbitrary")),
    )(qb, kb, vb, dheads, lse, delta)


def _delta_rows(dheads, heads):
    n = heads.shape[0]
    tq = GB_TQ
    w = GROUP * HEAD_DIM

    def body(a_ref, b_ref, o_ref):
        prod = a_ref[...].astype(F32) * b_ref[...].astype(F32)
        cols = [jnp.sum(prod[:, hh * HEAD_DIM:(hh + 1) * HEAD_DIM].T, axis=0, keepdims=True) for hh in range(GROUP)]
        o_ref[...] = jnp.concatenate(cols, axis=1)

    blk = pl.BlockSpec((tq, w), lambda g, i: (i, KV_A + g))
    return pl.pallas_call(
        body, name="delta_rows", grid=(KV_B, n // tq),
        in_specs=[blk, blk],
        out_specs=pl.BlockSpec((None, None, 1, GROUP * tq), lambda g, i: (g, i, 0, 0)),
        out_shape=jax.ShapeDtypeStruct((KV_B, n // tq, 1, GROUP * tq), F32),
        compiler_params=_params(("parallel", "parallel")),
    )(dheads, heads)


KWIN = 3 * BLOCK


def _window_scores(q, k_ref, j, n, nb, sink_col):
    c = k_ref.shape[0] - n
    start = pl.multiple_of(jnp.clip(j - 1, 0, nb - 3) * BLOCK, BLOCK)
    kw = k_ref[pl.ds(start, KWIN), :]
    kc = k_ref[pl.ds(n, c), :]
    s_loc = lax.dot_general(q, kw, NT, preferred_element_type=F32)
    s_ctx = lax.dot_general(q, kc, NT, preferred_element_type=F32)
    rows = GROUP * BLOCK
    qpos = j * BLOCK + lax.broadcasted_iota(jnp.int32, (rows, KWIN), 0) % BLOCK
    kpos = start + lax.broadcasted_iota(jnp.int32, (rows, KWIN), 1)
    s_loc = jnp.where(jnp.abs(qpos - kpos) <= WINDOW, s_loc, -jnp.inf)
    m = jnp.maximum(jnp.maximum(jnp.max(s_loc, axis=1, keepdims=True), jnp.max(s_ctx, axis=1, keepdims=True)),
                    sink_col)
    e_loc, e_ctx, e_sink = jnp.exp(s_loc - m), jnp.exp(s_ctx - m), jnp.exp(sink_col - m)
    inv = 1.0 / (jnp.sum(e_loc, axis=1, keepdims=True) + jnp.sum(e_ctx, axis=1, keepdims=True) + e_sink)
    return e_loc * inv, e_ctx * inv, e_sink * inv, start


def _sink_column(sink_ref, g):
    cols = [jnp.broadcast_to(sink_ref[pl.ds(g * GROUP + hh, 1), 0:1], (BLOCK, 1)) for hh in range(GROUP)]
    return jnp.concatenate(cols, axis=0)


def _attn_a_fwd(qa, ka, va, sink_b, n):
    t = ka.shape[1]
    nb = n // BLOCK
    assert nb >= 3

    def body(q_ref, k_ref, v_ref, sink_ref, o_ref):
        g, j = pl.program_id(0), pl.program_id(1)
        q = q_ref[...].reshape(GROUP * BLOCK, HEAD_DIM)
        p_loc, p_ctx, _, start = _window_scores(q, k_ref, j, n, nb, _sink_column(sink_ref, g))
        vw = v_ref[pl.ds(start, KWIN), :]
        vc = v_ref[pl.ds(n, t - n), :]
        o = (lax.dot_general(p_loc.astype(BF16), vw, NN, preferred_element_type=F32)
             + lax.dot_general(p_ctx.astype(BF16), vc, NN, preferred_element_type=F32))
        for hh in range(GROUP):
            o_ref[:, hh * HEAD_DIM:(hh + 1) * HEAD_DIM] = o[hh * BLOCK:(hh + 1) * BLOCK].astype(BF16)

    return pl.pallas_call(
        body, name="attn_a_fwd", grid=(KV_A, nb),
        in_specs=[pl.BlockSpec((GROUP, BLOCK, HEAD_DIM), lambda g, j: (g, j, 0)),
                  pl.BlockSpec((None, t, HEAD_DIM), lambda g, j: (g, 0, 0)),
                  pl.BlockSpec((None, t, HEAD_DIM), lambda g, j: (g, 0, 0)),
                  pl.BlockSpec((HEADS_A, HEAD_DIM), lambda g, j: (0, 0))],
        out_specs=pl.BlockSpec((BLOCK, GROUP * HEAD_DIM), lambda g, j: (j, g)),
        out_shape=jax.ShapeDtypeStruct((n, HEADS_A * HEAD_DIM), BF16),
        compiler_params=_params(("parallel", "parallel")),
    )(qa, ka, va, sink_b)


def _attn_a_bwd(qa, ka, va, sink_b, dheads, n):
    t = ka.shape[1]
    c = t - n
    nb = n // BLOCK

    def body(q_ref, k_ref, v_ref, sink_ref, do_ref, dq_ref, dk_ref, dv_ref, dsink_ref):
        g, j = pl.program_id(0), pl.program_id(1)

        @pl.when(j == 0)
        def _():
            dk_ref[...] = jnp.zeros_like(dk_ref)
            dv_ref[...] = jnp.zeros_like(dv_ref)
            dsink_ref[...] = jnp.zeros_like(dsink_ref)

        q = q_ref[...].reshape(GROUP * BLOCK, HEAD_DIM)
        do = _heads_rows(do_ref, BLOCK)
        p_loc, p_ctx, p_sink, start = _window_scores(q, k_ref, j, n, nb, _sink_column(sink_ref, g))
        kw, vw = k_ref[pl.ds(start, KWIN), :], v_ref[pl.ds(start, KWIN), :]
        kc, vc = k_ref[pl.ds(n, c), :], v_ref[pl.ds(n, c), :]
        dp_loc = lax.dot_general(do, vw, NT, preferred_element_type=F32)
        dp_ctx = lax.dot_general(do, vc, NT, preferred_element_type=F32)
        dl = jnp.sum(p_loc * dp_loc, axis=1, keepdims=True) + jnp.sum(p_ctx * dp_ctx, axis=1, keepdims=True)
        ds_loc = (p_loc * (dp_loc - dl)).astype(BF16)
        ds_ctx = (p_ctx * (dp_ctx - dl)).astype(BF16)
        dq = (lax.dot_general(ds_loc, kw, NN, preferred_element_type=F32)
              + lax.dot_general(ds_ctx, kc, NN, preferred_element_type=F32))
        for hh in range(GROUP):
            dq_ref[hh] = dq[hh * BLOCK:(hh + 1) * BLOCK]
        dk_ref[pl.ds(start, KWIN), :] += lax.dot_general(ds_loc, q, TN, preferred_element_type=F32)
        dv_ref[pl.ds(start, KWIN), :] += lax.dot_general(p_loc.astype(BF16), do, TN, preferred_element_type=F32)
        dk_ref[pl.ds(n, c), :] += lax.dot_general(ds_ctx, q, TN, preferred_element_type=F32)
        dv_ref[pl.ds(n, c), :] += lax.dot_general(p_ctx.astype(BF16), do, TN, preferred_element_type=F32)
        dsk = -(p_sink * dl)
        upd = [jnp.broadcast_to(jnp.sum(dsk[hh * BLOCK:(hh + 1) * BLOCK], axis=0, keepdims=True), (1, HEAD_DIM))
               for hh in range(GROUP)]
        dsink_ref[...] += jnp.concatenate(upd + [jnp.zeros((8 - GROUP, HEAD_DIM), F32)], axis=0)

    res = pl.BlockSpec((None, t, HEAD_DIM), lambda g, j: (g, 0, 0))
    return pl.pallas_call(
        body, name="attn_a_bwd", grid=(KV_A, nb),
        in_specs=[pl.BlockSpec((GROUP, BLOCK, HEAD_DIM), lambda g, j: (g, j, 0)), res, res,
                  pl.BlockSpec((HEADS_A, HEAD_DIM), lambda g, j: (0, 0)),
                  pl.BlockSpec((BLOCK, GROUP * HEAD_DIM), lambda g, j: (j, g))],
        out_specs=[pl.BlockSpec((GROUP, BLOCK, HEAD_DIM), lambda g, j: (g, j, 0)), res, res,
                   pl.BlockSpec((None, 8, HEAD_DIM), lambda g, j: (g, 0, 0))],
        out_shape=[jax.ShapeDtypeStruct((HEADS_A, n, HEAD_DIM), F32),
                   jax.ShapeDtypeStruct((KV_A, t, HEAD_DIM), F32),
                   jax.ShapeDtypeStruct((KV_A, t, HEAD_DIM), F32),
                   jax.ShapeDtypeStruct((KV_A, 8, HEAD_DIM), F32)],
        compiler_params=_params(("parallel", "arbitrary")),
    )(qa, ka, va, sink_b, dheads)


def _ln_stats(r):
    mu = jnp.mean(r, axis=-1, keepdims=True)
    cen = r - mu
    rstd = lax.rsqrt(jnp.mean(cen * cen, axis=-1, keepdims=True) + EPS)
    return cen * rstd, rstd


def _ln_bwd(dy, xhat, rstd, gain):
    dxh = dy * gain
    return rstd * (dxh - jnp.mean(dxh, axis=-1, keepdims=True)
                   - xhat * jnp.mean(dxh * xhat, axis=-1, keepdims=True))


def _accumulate_rows(ref, rows, i):
    pad = [jnp.zeros_like(rows[0])] * (8 - len(rows))
    upd = jnp.concatenate(rows + pad, axis=0)

    @pl.when(i == 0)
    def _():
        ref[...] = upd

    @pl.when(i != 0)
    def _():
        ref[...] += upd


def _colsum(v):
    return jnp.sum(v, axis=0, keepdims=True)


LN_TILE = 128


def _res_ln1(x, a, vec):
    n, d = x.shape

    def body(x_ref, a_ref, v_ref, xh_ref, rs_ref, u_ref):
        r1 = DN_ALPHA * x_ref[...] + v_ref[0:1, :] * a_ref[...]
        xhat, rstd = _ln_stats(r1)
        xh_ref[...] = xhat
        rs_ref[...] = rstd
        x1 = xhat * v_ref[1:2, :] + v_ref[2:3, :]
        u_ref[...] = (x1 * (1.0 + v_ref[3:4, :]) + v_ref[4:5, :]).astype(BF16)

    row = pl.BlockSpec((LN_TILE, d), lambda i: (i, 0))
    return pl.pallas_call(
        body, name="res_ln1", grid=(n // LN_TILE,),
        in_specs=[row, row, pl.BlockSpec((8, d), lambda i: (0, 0))],
        out_specs=[row, pl.BlockSpec((LN_TILE, 1), lambda i: (i, 0)), row],
        out_shape=[jax.ShapeDtypeStruct((n, d), F32), jax.ShapeDtypeStruct((n, 1), F32),
                   jax.ShapeDtypeStruct((n, d), BF16)],
        compiler_params=_params(("parallel",)),
    )(x, a, vec)


def _res_ln2_loss(xhat1, f, target, vec):
    n, d = f.shape

    def body(xh_ref, f_ref, t_ref, v_ref, dr_ref, df_ref, s_ref):
        i = pl.program_id(0)
        x1 = xh_ref[...] * v_ref[1:2, :] + v_ref[2:3, :]
        fv = f_ref[...]
        xhat, rstd = _ln_stats(DN_ALPHA * x1 + v_ref[0:1, :] * fv)
        err = xhat * v_ref[3:4, :] + v_ref[4:5, :] - t_ref[...]
        dy = err * (1.0 / d)
        dr2 = _ln_bwd(dy, xhat, rstd, v_ref[3:4, :])
        dr_ref[...] = dr2
        df_ref[...] = (dr2 * v_ref[0:1, :]).astype(BF16)
        _accumulate_rows(s_ref, [_colsum(dy * xhat), _colsum(dy), _colsum(dr2 * fv),
                                 _colsum(err * err) * (0.5 / d)], i)

    row = pl.BlockSpec((LN_TILE, d), lambda i: (i, 0))
    return pl.pallas_call(
        body, name="res_ln2_loss", grid=(n // LN_TILE,),
        in_specs=[row, row, row, pl.BlockSpec((8, d), lambda i: (0, 0))],
        out_specs=[row, row, pl.BlockSpec((8, d), lambda i: (0, 0))],
        out_shape=[jax.ShapeDtypeStruct((n, d), F32), jax.ShapeDtypeStruct((n, d), BF16),
                   jax.ShapeDtypeStruct((8, d), F32)],
        compiler_params=_params(("arbitrary",)),
    )(xhat1, f, target, vec)


def _ln1_bwd(du2, dr2, xhat1, rstd1, a, vec):
    n, d = du2.shape

    def body(du_ref, dr2_ref, xh_ref, rs_ref, a_ref, v_ref, dxp_ref, da_ref, s_ref):
        i = pl.program_id(0)
        du, xhat = du_ref[...], xh_ref[...]
        x1 = xhat * v_ref[1:2, :] + v_ref[2:3, :]
        dx1 = DN_ALPHA * dr2_ref[...] + du * (1.0 + v_ref[0:1, :])
        dr1 = _ln_bwd(dx1, xhat, rs_ref[...], v_ref[1:2, :])
        dxp_ref[...] = DN_ALPHA * dr1
        da_ref[...] = (dr1 * v_ref[3:4, :]).astype(BF16)
        _accumulate_rows(s_ref, [_colsum(du * x1), _colsum(du), _colsum(dx1 * xhat), _colsum(dx1),
                                 _colsum(dr1 * a_ref[...])], i)

    row = pl.BlockSpec((LN_TILE, d), lambda i: (i, 0))
    return pl.pallas_call(
        body, name="ln1_bwd", grid=(n // LN_TILE,),
        in_specs=[row, row, row, pl.BlockSpec((LN_TILE, 1), lambda i: (i, 0)), row,
                  pl.BlockSpec((8, d), lambda i: (0, 0))],
        out_specs=[row, row, pl.BlockSpec((8, d), lambda i: (0, 0))],
        out_shape=[jax.ShapeDtypeStruct((n, d), F32), jax.ShapeDtypeStruct((n, d), BF16),
                   jax.ShapeDtypeStruct((8, d), F32)],
        compiler_params=_params(("arbitrary",)),
    )(du2, dr2, xhat1, rstd1, a, vec)


def _mod1_bwd(du_all, dxp, x, ctx, mods):
    n, d = x.shape
    nx = n // ROW_TILE

    def body(du_ref, dxp_ref, x_ref, ctx_ref, m_ref, gx_ref, s_ref):
        i = pl.program_id(0)
        du = du_ref[...]
        zero = jnp.zeros((1, d), F32)

        @pl.when(i == 0)
        def _():
            s_ref[...] = jnp.zeros_like(s_ref)

        @pl.when(i < nx)
        def _():
            gx_ref[...] = dxp_ref[...] + du * (1.0 + m_ref[0:1, :])
            s_ref[...] += jnp.concatenate([_colsum(du * x_ref[...]), _colsum(du)] + [zero] * 6, axis=0)

        @pl.when(i >= nx)
        def _():
            s_ref[...] += jnp.concatenate([zero, zero, _colsum(du * ctx_ref[...]), _colsum(du)] + [zero] * 4, axis=0)

    lat = pl.BlockSpec((ROW_TILE, d), lambda i: (jnp.minimum(i, nx - 1), 0))
    return pl.pallas_call(
        body, name="mod1_bwd", grid=(nx + 1,),
        in_specs=[pl.BlockSpec((ROW_TILE, d), lambda i: (i, 0)), lat, lat,
                  pl.BlockSpec((ROW_TILE, d), lambda i: (0, 0)), pl.BlockSpec((8, d), lambda i: (0, 0))],
        out_specs=[lat, pl.BlockSpec((8, d), lambda i: (0, 0))],
        out_shape=[jax.ShapeDtypeStruct((n, d), F32), jax.ShapeDtypeStruct((8, d), F32)],
        compiler_params=_params(("arbitrary",)),
    )(du_all, dxp, x, ctx, mods)


FFN_TM = 512
FFN_TN = 512


def _gate_up(u2, wg, wu):
    n, d = u2.shape
    f = wg.shape[1]

    def body(u_ref, wg_ref, wu_ref, g_ref, up_ref, h_ref):
        u = u_ref[...]
        g = lax.dot_general(u, wg_ref[...], NN, preferred_element_type=F32)
        up = lax.dot_general(u, wu_ref[...], NN, preferred_element_type=F32)
        g_ref[...] = g
        up_ref[...] = up
        h_ref[...] = (g * jax.nn.sigmoid(g) * up).astype(BF16)

    wspec = pl.BlockSpec((d, FFN_TN), lambda j, i: (0, j))
    ospec = pl.BlockSpec((FFN_TM, FFN_TN), lambda j, i: (i, j))
    return pl.pallas_call(
        body, name="gate_up", grid=(f // FFN_TN, n // FFN_TM),
        in_specs=[pl.BlockSpec((FFN_TM, d), lambda j, i: (i, 0)), wspec, wspec],
        out_specs=[ospec, ospec, ospec],
        out_shape=[jax.ShapeDtypeStruct((n, f), F32), jax.ShapeDtypeStruct((n, f), F32),
                   jax.ShapeDtypeStruct((n, f), BF16)],
        compiler_params=_params(("parallel", "parallel")),
    )(u2, wg, wu)


def _glu_bwd(df, wd, g, u):
    n, d = df.shape
    f = wd.shape[0]

    def body(df_ref, wd_ref, g_ref, u_ref, dg_ref, du_ref):
        dh = lax.dot_general(df_ref[...], wd_ref[...], NT, preferred_element_type=F32)
        gv = g_ref[...]
        sig = jax.nn.sigmoid(gv)
        du_ref[...] = (dh * (gv * sig)).astype(BF16)
        dg_ref[...] = (dh * u_ref[...] * (sig * (1.0 + gv * (1.0 - sig)))).astype(BF16)

    ospec = pl.BlockSpec((FFN_TM, FFN_TN), lambda j, i: (i, j))
    return pl.pallas_call(
        body, name="glu_bwd", grid=(f // FFN_TN, n // FFN_TM),
        in_specs=[pl.BlockSpec((FFN_TM, d), lambda j, i: (i, 0)),
                  pl.BlockSpec((FFN_TN, d), lambda j, i: (j, 0)), ospec, ospec],
        out_specs=[ospec, ospec],
        out_shape=[jax.ShapeDtypeStruct((n, f), BF16), jax.ShapeDtypeStruct((n, f), BF16)],
        compiler_params=_params(("parallel", "parallel")),
    )(df, wd, g, u)


def _du2(dg, du, wg, wu):
    n, f = dg.shape
    d = wg.shape[0]
    tm, tn, tk = min(1024, n), 1024, 512
    nk = f // tk

    def body(dg_ref, du_ref, wg_ref, wu_ref, o_ref, acc_ref):
        kk = pl.program_id(2)
        part = (lax.dot_general(dg_ref[...], wg_ref[...], NT, preferred_element_type=F32)
                + lax.dot_general(du_ref[...], wu_ref[...], NT, preferred_element_type=F32))

        @pl.when(kk == 0)
        def _():
            acc_ref[...] = part

        @pl.when(kk != 0)
        def _():
            acc_ref[...] += part

        @pl.when(kk == nk - 1)
        def _():
            o_ref[...] = acc_ref[...]

    aspec = pl.BlockSpec((tm, tk), lambda i, j, kk: (i, kk))
    wspec = pl.BlockSpec((tn, tk), lambda i, j, kk: (j, kk))
    return pl.pallas_call(
        body, name="du2", grid=(n // tm, d // tn, nk),
        in_specs=[aspec, aspec, wspec, wspec],
        out_specs=pl.BlockSpec((tm, tn), lambda i, j, kk: (i, j)),
        out_shape=jax.ShapeDtypeStruct((n, d), F32),
        scratch_shapes=[pltpu.VMEM((tm, tn), F32)],
        compiler_params=_params(("parallel", "parallel", "arbitrary")),
    )(dg, du, wg, wu)


def _rows8(rows, d=D_MODEL):
    rows = [r.reshape(1, d).astype(F32) for r in rows]
    return jnp.concatenate(rows + [jnp.zeros((8 - len(rows), d), F32)], axis=0)


W_GROUPS = (("w_in",), ("w_out", "w_gate", "w_up"), ("w_down",))
G_GROUPS = (("w_down", "w_gate", "w_up"), ("w_out",), ("w_in",))


def _layer_fwd_bwd(x, ctx, target, mod, mod_ctx, weights, grads_out, q_g, k_g, sink, ln1_g, ln1_b, ln2_g, ln2_b):
    n, d = x.shape
    c = ctx.shape[0]
    sh1, sc1, g1, sh2, sc2, g2 = [mod[:, k * d:(k + 1) * d] for k in range(6)]
    csh1, csc1 = mod_ctx[:, 0:d], mod_ctx[:, d:2 * d]
    cos, sin_a, sin_b = _rope_tables(n, c)
    sink_b = jnp.broadcast_to(sink.reshape(HEADS_A, 1), (HEADS_A, HEAD_DIM)).astype(F32)

    u_all = _modulate_rows(x, ctx, _rows8([sc1, sh1, csc1, csh1]))
    (w_in,) = weights(0, u_all)
    h_all = _matmul(u_all, w_in, name="qkv_proj", tm=_fit(n + c, 1088), tn=1024, tk=2048, out_dtype=F32)
    qa, ka, va, qb, kb, vb = _qkv_post(h_all, cos, sin_a, sin_b, q_g, k_g)
    o_a = _attn_a_fwd(qa, ka, va, sink_b, n)
    o_b, lse = _attn_b_fwd(qb, kb, vb, n)
    heads = jnp.concatenate([o_a, o_b], axis=1)
    w_out, w_gate, w_up = weights(1, heads)
    a = _matmul(heads, w_out, name="out_proj", tm=1024, tn=1024, tk=2048, out_dtype=F32)
    xhat1, rstd1, u2 = _res_ln1(x, a, _rows8([g1, ln1_g, ln1_b, sc2, sh2]))
    gg, uu, hh = _gate_up(u2, w_gate, w_up)
    (w_down,) = weights(2, hh)
    f = _matmul(hh, w_down, name="ffn_down", tm=1024, tn=1024, tk=512, out_dtype=F32)
    dr2, df, s_ln2 = _res_ln2_loss(xhat1, f, target, _rows8([g2, ln1_g, ln1_b, ln2_g, ln2_b]))

    dgg, duu = _glu_bwd(df, w_down, gg, uu)
    dw_down = _matmul(hh, df, name="dw_down", ta=True, tm=512, tn=1024, tk=512, out_dtype=BF16)
    dw_gate = _matmul(u2, dgg, name="dw_gate", ta=True, tm=1024, tn=512, tk=512, out_dtype=BF16)
    dw_up = _matmul(u2, duu, name="dw_up", ta=True, tm=1024, tn=512, tk=512, out_dtype=BF16)
    zero = grads_out(0, [dw_down, dw_gate, dw_up])
    du2 = _du2(dgg, duu, w_gate, w_up)
    dxp, da, s_ln1 = _ln1_bwd(du2, dr2, xhat1, rstd1, a, _rows8([sc2, ln1_g, ln1_b, g1]) + zero)

    dheads = _matmul(da, w_out, name="d_heads", tb=True, tm=1024, tn=1024, tk=2048, out_dtype=BF16)
    dw_out = _matmul(heads, da, name="dw_out", ta=True, tm=1024, tn=1024, tk=512, out_dtype=BF16)
    zero = grads_out(1, [dw_out])
    delta = _delta_rows(dheads, heads)
    dqa, dka, dva, dsink = _attn_a_bwd(qa, ka, va, sink_b + zero, dheads, n)
    dqb, dkb, dvb = _attn_b_bwd(qb, kb, vb, dheads, lse, delta, n)
    dh_all, s_gain = _qkv_bwd_post(h_all, cos, sin_a, sin_b, q_g, k_g, dqa, dka, dva, dqb, dkb, dvb, n)
    dw_in = _matmul(u_all, dh_all, name="dw_in", ta=True, tm=1024, tn=1024, tk=_fit(n + c, 1088), out_dtype=BF16)
    zero = grads_out(2, [dw_in])
    du_all = _matmul(dh_all, w_in, name="d_u1", tb=True, tm=_fit(n + c, 1088), tn=1024, tk=1024, out_dtype=F32)
    grad_x, s_mod1 = _mod1_bwd(du_all, dxp, x, ctx, _rows8([sc1]) + zero)

    dsink_row = jnp.concatenate([dsink[0, 0:GROUP, 0], dsink[1, 0:GROUP, 0]]).reshape(1, HEADS_A)
    misc = jnp.concatenate([s_gain[0:1], s_gain[1:2], dsink_row,
                            jnp.zeros((1, d - 2 * HEAD_DIM - HEADS_A), F32)], axis=1)
    partial = jnp.concatenate([
        s_mod1[1:2], s_mod1[0:1], s_ln1[4:5],
        s_ln1[1:2], s_ln1[0:1], s_ln2[2:3],
        s_mod1[3:4], s_mod1[2:3],
        s_ln1[2:3], s_ln1[3:4], s_ln2[0:1], s_ln2[1:2],
        s_ln2[3:4], misc, jnp.zeros((2, d), F32)], axis=0)
    return grad_x, partial


ANY = pl.BlockSpec(memory_space=pl.ANY)
VMEM_FULL = pl.BlockSpec(memory_space=pltpu.VMEM)
N_CHIP_PEERS = 3


def _me():
    return lax.axis_index("x"), lax.axis_index("y"), lax.axis_index("c")


def _other_chips(x, y):
    return [(1 - x, y), (x, 1 - y), (1 - x, 1 - y)]


def _shard_of(chip):
    return 2 * chip[0] + chip[1]


def _dev_index(x, y, c):
    return 4 * x + 2 * y + c


def _rcopy(src, dst, send_sems, recv_sems, k, dev):
    return pltpu.make_async_remote_copy(src_ref=src, dst_ref=dst, send_sem=send_sems.at[k], recv_sem=recv_sems.at[k],
                                        device_id=dev, device_id_type=MESH)


BIG = (("w_in", (D_MODEL, IN_WIDTH), 1), ("w_out", (MIX_WIDTH, D_MODEL), 0), ("w_gate", (D_MODEL, FFN), 1),
       ("w_up", (D_MODEL, FFN), 1), ("w_down", (FFN, D_MODEL), 0))


def _sub(ref, axis, idx, size):
    start = pl.multiple_of(idx * size, size)
    return ref.at[pl.ds(start, size), :] if axis == 0 else ref.at[:, pl.ds(start, size)]


def _shape_div(shape, axis, parts):
    return tuple(s // parts if a == axis else s for a, s in enumerate(shape))


def _piece(a, ref, shard, half):
    _, full, axis = BIG[a]
    view = _sub(ref, axis, shard, full[axis] // N_SHARD)
    return _sub(view, 1 - axis, half, full[1 - axis] // 2)


HBM = pl.BlockSpec(memory_space=pltpu.HBM)
SEM = pl.BlockSpec(memory_space=pltpu.SEMAPHORE)
EFFECT = pltpu.SideEffectType.DATAFLOW_SIDE_EFFECTING
BIG_INDEX = {name: a for a, (name, _, _) in enumerate(BIG)}


def _in_hbm(arr):
    return pltpu.with_memory_space_constraint(arr, pltpu.HBM)


def _gather_start(tag, arrs, bufs, prev):
    n_arr = len(arrs)

    def body(*refs):
        ins = refs[:n_arr]
        send_sems, recv_sems = refs[n_arr + 1], refs[n_arr + 2]
        token = refs[-1]
        x, y, c = _me()
        s_me = _shard_of((x, y))
        for i, a in enumerate(arrs):
            mine = _piece(a, ins[i], s_me, c)
            for j, chip in enumerate(_other_chips(x, y)):
                _rcopy(mine, mine, send_sems, recv_sems, N_CHIP_PEERS * i + j, (*chip, c)).start()
        token[...] = jnp.zeros_like(token)

    n_sem = N_CHIP_PEERS * n_arr
    outs = pl.pallas_call(
        body, name="gather_start_" + tag,
        in_specs=[HBM] * n_arr + [ANY],
        out_specs=[SEM, SEM] + [HBM] * n_arr + [VMEM_FULL],
        out_shape=[pltpu.SemaphoreType.DMA((n_sem,)), pltpu.SemaphoreType.DMA((n_sem,))]
        + [pltpu.HBM(BIG[a][1], BF16) for a in arrs] + [jax.ShapeDtypeStruct((8, HEAD_DIM), F32)],
        input_output_aliases={i: 2 + i for i in range(n_arr)},
        compiler_params=pltpu.CompilerParams(has_side_effects=EFFECT),
    )(*[_in_hbm(b) for b in bufs], prev)
    return outs[0], outs[1], list(outs[2:2 + n_arr]), outs[-1]


def _gather_wait(tag, arrs, send_sems, recv_sems, bufs, after):
    n_arr = len(arrs)

    def body(*refs):
        ins = refs[:n_arr]
        send_sems_, recv_sems_ = refs[n_arr], refs[n_arr + 1]
        x, y, c = _me()
        s_me = _shard_of((x, y))
        for i, a in enumerate(arrs):
            mine = _piece(a, ins[i], s_me, c)
            for j, chip in enumerate(_other_chips(x, y)):
                landed = _piece(a, ins[i], _shard_of(chip), c)
                cp = _rcopy(mine, landed, send_sems_, recv_sems_, N_CHIP_PEERS * i + j, (*chip, c))
                cp.wait_send()
                cp.wait_recv()

    outs = pl.pallas_call(
        body, name="gather_wait_" + tag,
        in_specs=[HBM] * n_arr + [SEM, SEM, ANY],
        out_specs=[HBM] * n_arr,
        out_shape=[pltpu.HBM(BIG[a][1], BF16) for a in arrs],
        input_output_aliases={i: i for i in range(n_arr)},
        compiler_params=pltpu.CompilerParams(has_side_effects=EFFECT),
    )(*bufs, send_sems, recv_sems, after)
    return list(outs)


def _gather_forward(tag, arrs, bufs):
    n_arr = len(arrs)

    def body(*refs):
        outs = refs[n_arr:2 * n_arr]
        send_sems, recv_sems = refs[2 * n_arr:]
        x, y, c = _me()
        sibling = (x, y, 1 - c)
        chips = _other_chips(x, y)
        copies = []
        for i, a in enumerate(arrs):
            for j, chip in enumerate(chips):
                landed = _piece(a, outs[i], _shard_of(chip), c)
                cp = _rcopy(landed, landed, send_sems, recv_sems, N_CHIP_PEERS * i + j, sibling)
                cp.start()
                copies.append(cp)
        for i, a in enumerate(arrs):
            for j, chip in enumerate(chips):
                other = _piece(a, outs[i], _shard_of(chip), 1 - c)
                _rcopy(other, other, send_sems, recv_sems, N_CHIP_PEERS * i + j, sibling).wait_recv()
        for cp in copies:
            cp.wait_send()

    n_sem = N_CHIP_PEERS * n_arr
    return list(pl.pallas_call(
        body, name="gather_forward_" + tag,
        in_specs=[ANY] * n_arr, out_specs=[ANY] * n_arr,
        out_shape=[jax.ShapeDtypeStruct(BIG[a][1], BF16) for a in arrs],
        input_output_aliases={i: i for i in range(n_arr)},
        scratch_shapes=[pltpu.SemaphoreType.DMA((n_sem,)), pltpu.SemaphoreType.DMA((n_sem,))],
    )(*bufs))


def _exchange_halves(tag, arrs, dws):
    n_arr = len(arrs)

    def body(*refs):
        ins, outs = refs[:n_arr], refs[n_arr:2 * n_arr]
        send_sems, recv_sems = refs[2 * n_arr:]
        x, y, c = _me()
        copies = []
        for i, a in enumerate(arrs):
            _, full, axis = BIG[a]
            src = _sub(ins[i], 1 - axis, 1 - c, full[1 - axis] // 2)
            cp = _rcopy(src, outs[i], send_sems, recv_sems, i, (x, y, 1 - c))
            cp.start()
            copies.append(cp)
        for cp in copies:
            cp.wait()

    return list(pl.pallas_call(
        body, name="grad_exchange_halves_" + tag,
        in_specs=[ANY] * n_arr, out_specs=[ANY] * n_arr,
        out_shape=[jax.ShapeDtypeStruct(_shape_div(BIG[a][1], 1 - BIG[a][2], 2), BF16) for a in arrs],
        scratch_shapes=[pltpu.SemaphoreType.DMA((n_arr,)), pltpu.SemaphoreType.DMA((n_arr,))],
    )(*dws))


def _received_shape(a):
    _, full, axis = BIG[a]
    return (N_CHIP_PEERS,) + _shape_div(_shape_div(full, 1 - axis, 2), axis, N_SHARD)


def _pieces_start(tag, arrs, halves):
    n_arr = len(arrs)

    def body(*refs):
        srcs, lands = refs[:n_arr], refs[n_arr:2 * n_arr]
        send_sems, recv_sems = refs[2 * n_arr], refs[2 * n_arr + 1]
        token = refs[-1]
        x, y, c = _me()
        for i, a in enumerate(arrs):
            _, full, axis = BIG[a]
            for j, chip in enumerate(_other_chips(x, y)):
                src = _sub(srcs[i], axis, _shard_of(chip), full[axis] // N_SHARD)
                _rcopy(src, lands[i].at[j], send_sems, recv_sems, N_CHIP_PEERS * i + j, (*chip, c)).start()
        token[...] = jnp.zeros_like(token)

    n_sem = N_CHIP_PEERS * n_arr
    lands = [_in_hbm(lax.empty(_received_shape(a), BF16)) for a in arrs]
    outs = pl.pallas_call(
        body, name="grad_pieces_start_" + tag,
        in_specs=[HBM] * (2 * n_arr),
        out_specs=[SEM, SEM] + [HBM] * (2 * n_arr) + [VMEM_FULL],
        out_shape=[pltpu.SemaphoreType.DMA((n_sem,)), pltpu.SemaphoreType.DMA((n_sem,))]
        + [pltpu.HBM(h.shape, BF16) for h in halves] + [pltpu.HBM(_received_shape(a), BF16) for a in arrs]
        + [jax.ShapeDtypeStruct((8, HEAD_DIM), F32)],
        input_output_aliases={i: 2 + i for i in range(2 * n_arr)},
        compiler_params=pltpu.CompilerParams(has_side_effects=EFFECT),
    )(*[_in_hbm(h) for h in halves], *lands)
    return outs[0], outs[1], list(outs[2:2 + n_arr]), list(outs[2 + n_arr:2 + 2 * n_arr]), outs[-1]


def _pieces_wait(tag, arrs, send_sems, recv_sems, halves, lands, after):
    n_arr = len(arrs)

    def body(*refs):
        srcs, lands_ = refs[:n_arr], refs[n_arr:2 * n_arr]
        send_sems_, recv_sems_ = refs[2 * n_arr], refs[2 * n_arr + 1]
        x, y, c = _me()
        for i, a in enumerate(arrs):
            _, full, axis = BIG[a]
            for j, chip in enumerate(_other_chips(x, y)):
                src = _sub(srcs[i], axis, _shard_of(chip), full[axis] // N_SHARD)
                cp = _rcopy(src, lands_[i].at[j], send_sems_, recv_sems_, N_CHIP_PEERS * i + j, (*chip, c))
                cp.wait_send()
                cp.wait_recv()

    outs = pl.pallas_call(
        body, name="grad_pieces_wait_" + tag,
        in_specs=[HBM] * (2 * n_arr) + [SEM, SEM, ANY],
        out_specs=[HBM] * (2 * n_arr),
        out_shape=[pltpu.HBM(h.shape, BF16) for h in halves] + [pltpu.HBM(_received_shape(a), BF16) for a in arrs],
        input_output_aliases={i: i for i in range(2 * n_arr)},
        compiler_params=pltpu.CompilerParams(has_side_effects=EFFECT),
    )(*halves, *lands, send_sems, recv_sems, after)
    return list(outs[:n_arr]), list(outs[n_arr:])


def _join_halves(tag, g_halves):
    n_arr = len(g_halves)

    def body(*refs):
        ins, outs = refs[:n_arr], refs[n_arr:2 * n_arr]
        send_sems, recv_sems = refs[2 * n_arr:]
        x, y, c = _me()
        copies = []
        for i in range(n_arr):
            cp = _rcopy(ins[i], outs[i], send_sems, recv_sems, i, (x, y, 1 - c))
            cp.start()
            copies.append(cp)
        for cp in copies:
            cp.wait()

    return list(pl.pallas_call(
        body, name="grad_join_halves_" + tag,
        in_specs=[ANY] * n_arr, out_specs=[ANY] * n_arr,
        out_shape=[jax.ShapeDtypeStruct(g.shape, F32) for g in g_halves],
        scratch_shapes=[pltpu.SemaphoreType.DMA((n_arr,)), pltpu.SemaphoreType.DMA((n_arr,))],
    )(*g_halves))


def _window_sum(name, big, widx, axis, nwin, extra, stacked, out_dtype):
    rows, cols = _shape_div(big.shape, axis, nwin)
    tr = _fit(rows, ROW_TILE)
    nbr = rows // tr

    def body(w_ref, big_ref, ex_ref, o_ref):
        acc = big_ref[...].astype(F32)
        if stacked:
            for j in range(N_CHIP_PEERS):
                acc = acc + ex_ref[j].astype(F32)
        else:
            acc = acc + ex_ref[...].astype(F32)
        o_ref[...] = acc.astype(o_ref.dtype)

    if axis == 0:
        big_spec = pl.BlockSpec((tr, cols), lambda i, w: (w[0] * nbr + i, 0))
    else:
        big_spec = pl.BlockSpec((tr, cols), lambda i, w: (i, w[0]))
    ex_spec = (pl.BlockSpec((N_CHIP_PEERS, tr, cols), lambda i, w: (0, i, 0)) if stacked
               else pl.BlockSpec((tr, cols), lambda i, w: (i, 0)))
    return pl.pallas_call(
        body, name=name,
        grid_spec=pltpu.PrefetchScalarGridSpec(
            num_scalar_prefetch=1, grid=(nbr,), in_specs=[big_spec, ex_spec],
            out_specs=pl.BlockSpec((tr, cols), lambda i, w: (i, 0))),
        out_shape=jax.ShapeDtypeStruct((rows, cols), out_dtype),
        compiler_params=_params(("parallel",)),
    )(widx.reshape(1).astype(jnp.int32), big, extra)


def _scatter_begin(tag, arrs, dws):
    _, _, c = _me()
    from_sibling = _exchange_halves(tag, arrs, dws)
    halves = [_window_sum("grad_sum_halves_" + BIG[a][0], dw, c, 1 - BIG[a][2], 2, fs, False, BF16)
              for a, dw, fs in zip(arrs, dws, from_sibling)]
    send_sems, recv_sems, halves, lands, token = _pieces_start(tag, arrs, halves)
    return (send_sems, recv_sems, halves, lands), token[0, 0]


def _scatter_end(tag, arrs, state, after):
    x, y, _ = _me()
    send_sems, recv_sems, halves, lands = state
    halves, lands = _pieces_wait(tag, arrs, send_sems, recv_sems, halves, lands, after)
    g_own = [_window_sum("grad_sum_pieces_" + BIG[a][0], h, _shard_of((x, y)), BIG[a][2], N_SHARD, r, True, F32)
             for a, h, r in zip(arrs, halves, lands)]
    return g_own, _join_halves(tag, g_own)


def _gather_rows(block, name):
    r, d = block.shape

    def body(in_ref, out_ref, send_sems, recv_sems):
        x, y, c = _me()
        out_ref[_dev_index(x, y, c)] = in_ref[...]
        copies = []
        for mask in range(1, N_DEV):
            peer = (x ^ (mask >> 2), y ^ ((mask >> 1) & 1), c ^ (mask & 1))
            cp = _rcopy(in_ref, out_ref.at[_dev_index(x, y, c)], send_sems, recv_sems, mask - 1, peer)
            cp.start()
            copies.append((cp, peer))
        for mask in range(1, N_DEV):
            peer = (x ^ (mask >> 2), y ^ ((mask >> 1) & 1), c ^ (mask & 1))
            landed = out_ref.at[_dev_index(*peer)]
            _rcopy(landed, landed, send_sems, recv_sems, mask - 1, peer).wait_recv()
        for cp, _ in copies:
            cp.wait_send()

    return pl.pallas_call(
        body, name=name, in_specs=[VMEM_FULL], out_specs=VMEM_FULL,
        out_shape=jax.ShapeDtypeStruct((N_DEV, r, d), F32),
        scratch_shapes=[pltpu.SemaphoreType.DMA((N_DEV - 1,)), pltpu.SemaphoreType.DMA((N_DEV - 1,))],
    )(block)


ADA_ROWS = 80
ADA_W = 6 * D_MODEL // N_SHARD


def _ada_forward(c_block, cctx_block, w_ada, b_shard):
    d = c_block.shape[1]

    def body(c_ref, cc_ref, w_ref, b_ref, act_ref, mods_ref, raw, mloc, send_sems, recv_sems):
        x, y, c = _me()
        me = _dev_index(x, y, c)
        s_me = _shard_of((x, y))
        raw[72:ADA_ROWS, :] = jnp.zeros((ADA_ROWS - 72, d), F32)
        raw[pl.ds(pl.multiple_of(me * 8, 8), 8), :] = c_ref[...]
        raw[64:72, :] = cc_ref[...]
        sends = []
        for mask in range(1, N_DEV):
            peer = (x ^ (mask >> 2), y ^ ((mask >> 1) & 1), c ^ (mask & 1))
            cp = _rcopy(c_ref, raw.at[pl.ds(pl.multiple_of(me * 8, 8), 8), :], send_sems, recv_sems, mask - 1, peer)
            cp.start()
            sends.append(cp)
        for mask in range(1, N_DEV):
            peer = (x ^ (mask >> 2), y ^ ((mask >> 1) & 1), c ^ (mask & 1))
            landed = raw.at[pl.ds(pl.multiple_of(_dev_index(*peer) * 8, 8), 8), :]
            _rcopy(landed, landed, send_sems, recv_sems, mask - 1, peer).wait_recv()
        v = raw[...]
        act = v * jax.nn.sigmoid(v)
        act_ref[...] = act
        mloc[...] = lax.dot_general(act.astype(BF16), w_ref[...].astype(BF16), NN,
                                    preferred_element_type=F32) + b_ref[...]
        mods_ref[s_me, 0:8, :] = mloc[pl.ds(pl.multiple_of(me * 8, 8), 8), :]
        mods_ref[s_me, 8:16, :] = mloc[64:72, :]
        base = N_DEV - 1
        for j, chip in enumerate(_other_chips(x, y)):
            peer = (*chip, c)
            rows = mloc.at[pl.ds(pl.multiple_of(_dev_index(*peer) * 8, 8), 8), :]
            cp = _rcopy(rows, mods_ref.at[s_me, 0:8, :], send_sems, recv_sems, base + 2 * j, peer)
            cp.start()
            sends.append(cp)
            cp = _rcopy(mloc.at[64:72, :], mods_ref.at[s_me, 8:16, :], send_sems, recv_sems, base + 2 * j + 1, peer)
            cp.start()
            sends.append(cp)
        for j, chip in enumerate(_other_chips(x, y)):
            for part in range(2):
                landed = mods_ref.at[_shard_of(chip), 8 * part:8 * part + 8, :]
                _rcopy(landed, landed, send_sems, recv_sems, base + 2 * j + part, (*chip, c)).wait_recv()
        for cp in sends:
            cp.wait_send()

    n_sem = N_DEV - 1 + 2 * N_CHIP_PEERS
    return pl.pallas_call(
        body, name="ada_forward",
        in_specs=[VMEM_FULL] * 4, out_specs=[VMEM_FULL, VMEM_FULL],
        out_shape=[jax.ShapeDtypeStruct((ADA_ROWS, d), F32), jax.ShapeDtypeStruct((N_SHARD, 16, ADA_W), F32)],
        scratch_shapes=[pltpu.VMEM((ADA_ROWS, d), F32), pltpu.VMEM((ADA_ROWS, ADA_W), F32),
                        pltpu.SemaphoreType.DMA((n_sem,)), pltpu.SemaphoreType.DMA((n_sem,))],
        compiler_params=pltpu.CompilerParams(vmem_limit_bytes=VMEM_LIMIT),
    )(c_block, cctx_block, w_ada, b_shard)


def _small_reduce(gathered):
    d = gathered.shape[2]

    def body(g_ref, o_ref):
        tot = g_ref[0]
        for i in range(1, N_DEV):
            tot = tot + g_ref[i]
        o_ref[...] = tot
        o_ref[0:2, :] = tot[0:2] + tot[6:8]
        o_ref[12:13, :] = jnp.broadcast_to(jnp.sum(tot[12:13], axis=1, keepdims=True), (1, d))

    return pl.pallas_call(body, name="small_reduce", in_specs=[VMEM_FULL], out_specs=VMEM_FULL,
                          out_shape=jax.ShapeDtypeStruct((16, d), F32))(gathered)


def _cctx_grad(gathered, c_ctx):
    d = gathered.shape[2]

    def body(g_ref, c_ref, o_ref):
        tot = g_ref[0, 0:1, :]
        for chip in range(1, N_SHARD):
            tot = tot + g_ref[2 * chip, 0:1, :]
        v = c_ref[...]
        sig = jax.nn.sigmoid(v)
        o_ref[...] = tot * (sig * (1.0 + v * (1.0 - sig)))

    return pl.pallas_call(body, name="cctx_grad", in_specs=[VMEM_FULL, VMEM_FULL], out_specs=VMEM_FULL,
                          out_shape=jax.ShapeDtypeStruct((1, d), F32))(gathered, c_ctx.reshape(1, d))


def _cast_into_full(w, shard, full, axis, name):
    r, cdim = w.shape
    tr = _fit(r, ROW_TILE)
    nbr = r // tr

    def body(s_ref, w_ref, o_ref):
        o_ref[...] = w_ref[...].astype(BF16)

    if axis == 0:
        out_spec = pl.BlockSpec((tr, cdim), lambda i, s: (s[0] * nbr + i, 0))
    else:
        out_spec = pl.BlockSpec((tr, cdim), lambda i, s: (i, s[0]))
    return pl.pallas_call(
        body, name=name,
        grid_spec=pltpu.PrefetchScalarGridSpec(
            num_scalar_prefetch=1, grid=(nbr,), in_specs=[pl.BlockSpec((tr, cdim), lambda i, s: (i, 0))],
            out_specs=out_spec),
        out_shape=jax.ShapeDtypeStruct(full, BF16), compiler_params=_params(("parallel",)),
    )(shard.reshape(1).astype(jnp.int32), w)


def _adamw_halves(w, g_own, g_other, m, v, core, axis, name):
    r, cdim = w.shape
    hr, hc = (r // 2, cdim) if axis == 1 else (r, cdim // 2)
    assert g_own.shape == (hr, hc) and g_other.shape == (hr, hc)
    tr = _fit(hr, 128)
    nb = hr // tr
    c1 = 1.0 - ADAM_B1 ** ADAM_STEP
    c2 = 1.0 - ADAM_B2 ** ADAM_STEP

    def body(c_ref, w_ref, go_ref, gt_ref, m_ref, v_ref, g_ref, d_ref, nm_ref, nv_ref):
        gv = jnp.where(pl.program_id(0) == c_ref[0], go_ref[...], gt_ref[...])
        nm = ADAM_B1 * m_ref[...] + (1.0 - ADAM_B1) * gv
        nv = ADAM_B2 * v_ref[...] + (1.0 - ADAM_B2) * (gv * gv)
        g_ref[...] = gv
        nm_ref[...] = nm
        nv_ref[...] = nv
        d_ref[...] = -ADAM_LR * ((nm / c1) / (jnp.sqrt(nv / c2) + ADAM_EPS) + ADAM_WD * w_ref[...])

    if axis == 1:
        big = pl.BlockSpec((tr, hc), lambda p, i, c: (p * nb + i, 0))
    else:
        big = pl.BlockSpec((tr, hc), lambda p, i, c: (i, p))
    half = pl.BlockSpec((tr, hc), lambda p, i, c: (i, 0))
    sh = jax.ShapeDtypeStruct((r, cdim), F32)
    return pl.pallas_call(
        body, name=name,
        grid_spec=pltpu.PrefetchScalarGridSpec(
            num_scalar_prefetch=1, grid=(2, nb), in_specs=[big, half, half, big, big], out_specs=[big] * 4),
        out_shape=[sh] * 4, compiler_params=_params(("parallel", "parallel")),
    )(core.reshape(1).astype(jnp.int32), w, g_own, g_other, m, v)


def _adamw(w, g, m, v, name):
    r, cdim = w.shape
    tr = _fit(r, 128) if r % (ROW_TILE // 4) == 0 else r
    c1 = 1.0 - ADAM_B1 ** ADAM_STEP
    c2 = 1.0 - ADAM_B2 ** ADAM_STEP

    def body(w_ref, g_ref, m_ref, v_ref, d_ref, nm_ref, nv_ref):
        gv = g_ref[...]
        nm = ADAM_B1 * m_ref[...] + (1.0 - ADAM_B1) * gv
        nv = ADAM_B2 * v_ref[...] + (1.0 - ADAM_B2) * (gv * gv)
        nm_ref[...] = nm
        nv_ref[...] = nv
        d_ref[...] = -ADAM_LR * ((nm / c1) / (jnp.sqrt(nv / c2) + ADAM_EPS) + ADAM_WD * w_ref[...])

    spec = pl.BlockSpec((tr, cdim), lambda i: (i, 0))
    sh = jax.ShapeDtypeStruct((r, cdim), F32)
    return pl.pallas_call(body, name=name, grid=(r // tr,), in_specs=[spec] * 4, out_specs=[spec] * 3,
                          out_shape=[sh, sh, sh], compiler_params=_params(("parallel",)))(w, g, m, v)


SMALL = (("c_ctx", D_MODEL), ("b_ada", 6 * D_MODEL), ("q_norm_g", HEAD_DIM), ("k_norm_g", HEAD_DIM),
         ("sink_logit", HEADS_A), ("ln1_g", D_MODEL), ("ln1_b", D_MODEL), ("ln2_g", D_MODEL), ("ln2_b", D_MODEL))
WEIGHT_ORDER = ("c_ctx", "w_ada", "b_ada", "w_in", "q_norm_g", "k_norm_g", "sink_logit", "w_out", "ln1_g", "ln1_b",
                "w_gate", "w_up", "w_down", "ln2_g", "ln2_b")


def kernel(x, c, ctx, c_ctx, w_ada, b_ada, w_in, q_norm_g, k_norm_g, sink_logit, w_out, ln1_g, ln1_b, w_gate, w_up, w_down, ln2_g, ln2_b, loss_target, m_c_ctx, m_w_ada, m_b_ada, m_w_in, m_q_norm_g, m_k_norm_g, m_sink_logit, m_w_out, m_ln1_g, m_ln1_b, m_w_gate, m_w_up, m_w_down, m_ln2_g, m_ln2_b, v_c_ctx, v_w_ada, v_b_ada, v_w_in, v_q_norm_g, v_k_norm_g, v_sink_logit, v_w_out, v_ln1_g, v_ln1_b, v_w_gate, v_w_up, v_w_down, v_ln2_g, v_ln2_b):
    d = D_MODEL
    w = dict(c_ctx=c_ctx, w_ada=w_ada[0], b_ada=b_ada, w_in=w_in[0], q_norm_g=q_norm_g, k_norm_g=k_norm_g,
             sink_logit=sink_logit, w_out=w_out[0], ln1_g=ln1_g, ln1_b=ln1_b, w_gate=w_gate[0], w_up=w_up[0],
             w_down=w_down[0], ln2_g=ln2_g, ln2_b=ln2_b)
    m = dict(c_ctx=m_c_ctx, w_ada=m_w_ada[0], b_ada=m_b_ada, w_in=m_w_in[0], q_norm_g=m_q_norm_g, k_norm_g=m_k_norm_g,
             sink_logit=m_sink_logit, w_out=m_w_out[0], ln1_g=m_ln1_g, ln1_b=m_ln1_b, w_gate=m_w_gate[0],
             w_up=m_w_up[0], w_down=m_w_down[0], ln2_g=m_ln2_g, ln2_b=m_ln2_b)
    v = dict(c_ctx=v_c_ctx, w_ada=v_w_ada[0], b_ada=v_b_ada, w_in=v_w_in[0], q_norm_g=v_q_norm_g, k_norm_g=v_k_norm_g,
             sink_logit=v_sink_logit, w_out=v_w_out[0], ln1_g=v_ln1_g, ln1_b=v_ln1_b, w_gate=v_w_gate[0],
             w_up=v_w_up[0], w_down=v_w_down[0], ln2_g=v_ln2_g, ln2_b=v_ln2_b)
    mx, my, mc = _me()
    s_me = _shard_of((mx, my))
    me = _dev_index(mx, my, mc)
    pad8 = lambda row: jnp.concatenate([row.reshape(1, -1), jnp.zeros((7, row.size), F32)], axis=0)

    bufs = {name: _cast_into_full(w[name], s_me, shape, axis, "cast_" + name) for name, shape, axis in BIG}
    gathers = []
    prev = pad8(c)
    for k, names in enumerate(W_GROUPS):
        arrs = tuple(BIG_INDEX[name] for name in names)
        send_sems, recv_sems, thru, prev = _gather_start("g%d" % k, arrs, [bufs[name] for name in names], prev)
        gathers.append((arrs, send_sems, recv_sems, thru))

    def weights(k, after):
        arrs, send_sems, recv_sems, thru = gathers[k]
        landed = _gather_wait("g%d" % k, arrs, send_sems, recv_sems, thru, after)
        return _gather_forward("g%d" % k, arrs, landed)

    b_shard = lax.dynamic_slice(b_ada, (0, s_me * ADA_W), (1, ADA_W))
    act, mods4 = _ada_forward(pad8(c) + prev[0, 0], pad8(c_ctx), w["w_ada"], b_shard)
    mod = jnp.transpose(mods4[:, 0:1, :], (1, 0, 2)).reshape(1, 6 * d)
    mod_ctx = jnp.transpose(mods4[:, 8:9, :], (1, 0, 2)).reshape(1, 6 * d)

    scatters = {}

    def grads_out(k, dws):
        arrs = tuple(BIG_INDEX[name] for name in G_GROUPS[k])
        scatters[k], zero = _scatter_begin("g%d" % k, arrs, dws)
        return zero

    grad_x, partial = _layer_fwd_bwd(x[0], ctx[0], loss_target[0], mod, mod_ctx, weights, grads_out,
                                     q_norm_g, k_norm_g, sink_logit, ln1_g, ln1_b, ln2_g, ln2_b)
    grads, delta, new_m, new_v = {}, {}, {}, {}

    gathered = _gather_rows(partial, "gather_partials")
    tot = _small_reduce(gathered)
    grads["b_ada"] = tot[0:6].reshape(1, 6 * d)
    grads["ln1_g"], grads["ln1_b"], grads["ln2_g"], grads["ln2_b"] = tot[8:9], tot[9:10], tot[10:11], tot[11:12]
    grads["q_norm_g"] = tot[13:14, 0:HEAD_DIM]
    grads["k_norm_g"] = tot[13:14, HEAD_DIM:2 * HEAD_DIM]
    grads["sink_logit"] = tot[13:14, 2 * HEAD_DIM:2 * HEAD_DIM + HEADS_A]
    loss = tot[12, 0]

    dm_all = gathered[:, 0:6, :].reshape(N_DEV, 6 * d)
    dmc_tot = jnp.concatenate([tot[6:8].reshape(1, 2 * d), jnp.zeros((1, 4 * d), F32)], axis=1)
    dm_rows = jnp.concatenate([pad8(dm_all[i]) for i in range(N_DEV)] + [pad8(dmc_tot), jnp.zeros((8, 6 * d), F32)], axis=0)
    dm_shard = lax.dynamic_slice(dm_rows, (0, s_me * ADA_W), (ADA_ROWS, ADA_W))
    grads["w_ada"] = _matmul(act, dm_shard, name="dw_ada", ta=True, tm=1024, tn=1024, tk=ADA_ROWS, out_dtype=F32)
    dmc_shard = lax.dynamic_slice(pad8(dmc_tot), (0, s_me * ADA_W), (8, ADA_W))
    cc_part = _matmul(dmc_shard, w["w_ada"], name="d_cctx", tb=True, tm=8, tn=1024, tk=1536, out_dtype=F32)
    grads["c_ctx"] = _cctx_grad(_gather_rows(cc_part, "gather_cctx"), c_ctx).reshape(d)

    delta["w_ada"], new_m["w_ada"], new_v["w_ada"] = _adamw(w["w_ada"], grads["w_ada"], m["w_ada"], v["w_ada"],
                                                            "adamw_w_ada")
    pack = lambda t: jnp.concatenate([t[name].reshape(1, size) for name, size in SMALL], axis=1)
    pd, pm, pv = _adamw(pack(w), pack(grads), pack(m), pack(v), "adamw_small")
    off = 0
    for name, size in SMALL:
        delta[name], new_m[name], new_v[name] = [t[:, off:off + size].reshape(w[name].shape) for t in (pd, pm, pv)]
        grads[name] = grads[name].reshape(w[name].shape)
        off += size

    after = pd
    for k, names in enumerate(G_GROUPS):
        arrs = tuple(BIG_INDEX[name] for name in names)
        g_own, g_other = _scatter_end("g%d" % k, arrs, scatters[k], after)
        for name, own, other in zip(names, g_own, g_other):
            grads[name], delta[name], new_m[name], new_v[name] = _adamw_halves(
                w[name], own, other, m[name], v[name], mc, BIG[BIG_INDEX[name]][2], "adamw_" + name)
            after = new_v[name]

    lead = lambda name, t: t[None] if name in ("w_ada", "w_in", "w_out", "w_gate", "w_up", "w_down") else t
    outs = [loss, grad_x[None]]
    for group in (grads, delta, new_m, new_v):
        outs += [lead(name, group[name]) for name in WEIGHT_ORDER]
    return tuple(outs)
```

```python
import functools
import math

import jax
import jax.numpy as jnp
from jax import lax
from jax.experimental import pallas as pl
from jax.experimental.pallas import tpu as pltpu

F32 = jnp.float32
BF16 = jnp.bfloat16
MESH = pl.DeviceIdType.MESH

D_MODEL = 2048
HEAD_DIM = 128
HEADS_A = 8
HEADS_B = 8
KV_A = 2
KV_B = 2
GROUP = 4
GRID_W = 64
WINDOW = 128
BLOCK = 128
FFN = 5632
IN_WIDTH = 3072
MIX_WIDTH = 2048
ROPE_THETA = 10000.0
EPS = 1e-6
ATTN_SCALE = HEAD_DIM ** -0.5
DN_ALPHA = 2.0 ** 0.25
N_SHARD = 4
N_DEV = 8

ADAM_LR = 0.001
ADAM_B1 = 0.9
ADAM_B2 = 0.999
ADAM_EPS = 1e-08
ADAM_WD = 0.01
ADAM_STEP = 10

QA0, KA0, VA0, QB0, KB0, VB0 = 0, 1024, 1280, 1536, 2560, 2816

VMEM_LIMIT = 56 * 1024 * 1024
ROW_TILE = 256
NN = (((1,), (0,)), ((), ()))
NT = (((1,), (1,)), ((), ()))
TN = (((0,), (0,)), ((), ()))


def _fit(total, pref):
    step = ROW_TILE // 4
    best = step
    for cand in range(step, pref + 1, step):
        if total % cand == 0:
            best = cand
    return best


def _params(sem=None):
    return pltpu.CompilerParams(dimension_semantics=sem, vmem_limit_bytes=VMEM_LIMIT)


def _matmul(a, b, *, name, ta=False, tb=False, tm, tn, tk, out_dtype):
    m = a.shape[1] if ta else a.shape[0]
    k = a.shape[0] if ta else a.shape[1]
    n = b.shape[0] if tb else b.shape[1]
    assert (b.shape[1] if tb else b.shape[0]) == k
    tm, tn, tk = min(tm, m), min(tn, n), min(tk, k)
    assert m % tm == 0 and n % tn == 0 and k % tk == 0, (name, m, n, k, tm, tn, tk)
    nk = k // tk
    dn = (((0 if ta else 1,), (1 if tb else 0,)), ((), ()))

    def product(a_ref, b_ref):
        return lax.dot_general(a_ref[...].astype(BF16), b_ref[...].astype(BF16), dn, preferred_element_type=F32)

    def body_whole_k(a_ref, b_ref, o_ref):
        o_ref[...] = product(a_ref, b_ref).astype(o_ref.dtype)

    def body(a_ref, b_ref, o_ref, acc_ref):
        kk = pl.program_id(2)
        part = product(a_ref, b_ref)

        @pl.when(kk == 0)
        def _():
            acc_ref[...] = part

        @pl.when(kk != 0)
        def _():
            acc_ref[...] += part

        @pl.when(kk == nk - 1)
        def _():
            o_ref[...] = acc_ref[...].astype(o_ref.dtype)

    a_spec = (pl.BlockSpec((tk, tm), lambda i, j, kk: (kk, i)) if ta
              else pl.BlockSpec((tm, tk), lambda i, j, kk: (i, kk)))
    b_spec = (pl.BlockSpec((tn, tk), lambda i, j, kk: (j, kk)) if tb
              else pl.BlockSpec((tk, tn), lambda i, j, kk: (kk, j)))
    return pl.pallas_call(
        body_whole_k if nk == 1 else body, name=name, grid=(m // tm, n // tn, nk),
        in_specs=[a_spec, b_spec],
        out_specs=pl.BlockSpec((tm, tn), lambda i, j, kk: (i, j)),
        out_shape=jax.ShapeDtypeStruct((m, n), out_dtype),
        scratch_shapes=[] if nk == 1 else [pltpu.VMEM((tm, tn), F32)],
        compiler_params=_params(("parallel", "parallel", "arbitrary")),
    )(a, b)


def _modulate_rows(x, ctx, mods):
    n, d = x.shape
    c = ctx.shape[0]
    nx = n // ROW_TILE
    assert c == ROW_TILE

    def body(x_ref, ctx_ref, mods_ref, o_ref):
        i = pl.program_id(0)

        @pl.when(i < nx)
        def _():
            o_ref[...] = (x_ref[...] * (1.0 + mods_ref[0:1, :]) + mods_ref[1:2, :]).astype(BF16)

        @pl.when(i >= nx)
        def _():
            o_ref[...] = (ctx_ref[...] * (1.0 + mods_ref[2:3, :]) + mods_ref[3:4, :]).astype(BF16)

    return pl.pallas_call(
        body, name="modulate_rows", grid=(nx + 1,),
        in_specs=[pl.BlockSpec((ROW_TILE, d), lambda i: (jnp.minimum(i, nx - 1), 0)),
                  pl.BlockSpec((ROW_TILE, d), lambda i: (0, 0)),
                  pl.BlockSpec((8, d), lambda i: (0, 0))],
        out_specs=pl.BlockSpec((ROW_TILE, d), lambda i: (i, 0)),
        out_shape=jax.ShapeDtypeStruct((n + c, d), BF16),
        compiler_params=_params(("parallel",)),
    )(x, ctx, mods)


def _rope_tables(n, c):
    rows = n // GRID_W
    row_ids = jnp.repeat(jnp.arange(rows, dtype=F32), GRID_W)
    col_ids = jnp.tile(jnp.arange(GRID_W, dtype=F32), rows)
    axis_dim = HEAD_DIM // 2
    inv_freq = jnp.power(ROPE_THETA, -jnp.arange(0, axis_dim, 2, dtype=F32) / axis_dim)
    ang_r = row_ids[:, None] * inv_freq
    ang_c = col_ids[:, None] * inv_freq
    ang = jnp.concatenate([ang_r, ang_r, ang_c, ang_c], axis=-1)
    cos, sin = jnp.cos(ang), jnp.sin(ang)
    quarter = (jnp.arange(HEAD_DIM) // (HEAD_DIM // 4)) % 2
    sin_a = jnp.where(quarter == 0, -sin, 0.0)
    sin_b = jnp.where(quarter == 1, sin, 0.0)
    pad = lambda t, v: jnp.concatenate([t, jnp.full((c, HEAD_DIM), v, F32)], axis=0)
    return pad(cos, 1.0), pad(sin_a, 0.0), pad(sin_b, 0.0)


def _rope(x, cos, sin_a, sin_b):
    return x * cos + pltpu.roll(x, 96, 1) * sin_a + pltpu.roll(x, 32, 1) * sin_b


def _rope_t(dy, cos, sin_a, sin_b):
    return dy * cos - pltpu.roll(dy, 96, 1) * sin_a - pltpu.roll(dy, 32, 1) * sin_b


def _rms(x):
    r = lax.rsqrt(jnp.mean(x * x, axis=-1, keepdims=True) + EPS)
    return x * r, r


def _qkv_post(h_all, cos, sin_a, sin_b, q_g, k_g):
    t = h_all.shape[0]
    nt = t // ROW_TILE

    def body(h_ref, cos_ref, sa_ref, sb_ref, qg_ref, kg_ref, qa_ref, ka_ref, va_ref, qb_ref, kb_ref, vb_ref):
        cos_, sa, sb = cos_ref[...], sa_ref[...], sb_ref[...]
        sl = lambda off, hh: h_ref[:, off + hh * HEAD_DIM: off + (hh + 1) * HEAD_DIM]
        for hh in range(HEADS_A):
            qa_ref[hh] = (_rope(sl(QA0, hh), cos_, sa, sb) * ATTN_SCALE).astype(BF16)
        for hh in range(KV_A):
            ka_ref[hh] = _rope(sl(KA0, hh), cos_, sa, sb).astype(BF16)
            va_ref[hh] = sl(VA0, hh).astype(BF16)
        for hh in range(HEADS_B):
            xn, _ = _rms(sl(QB0, hh))
            qb_ref[hh] = (_rope(xn * qg_ref[...], cos_, sa, sb) * ATTN_SCALE).astype(BF16)
        for hh in range(KV_B):
            xn, _ = _rms(sl(KB0, hh))
            kb_ref[hh] = _rope(xn * kg_ref[...], cos_, sa, sb).astype(BF16)
            vb_ref[hh] = sl(VB0, hh).astype(BF16)

    tab = pl.BlockSpec((ROW_TILE, HEAD_DIM), lambda i: (i, 0))
    gain = pl.BlockSpec((1, HEAD_DIM), lambda i: (0, 0))
    hs = lambda nh: pl.BlockSpec((nh, ROW_TILE, HEAD_DIM), lambda i: (0, i, 0))
    sh = lambda nh: jax.ShapeDtypeStruct((nh, t, HEAD_DIM), BF16)
    return pl.pallas_call(
        body, name="qkv_post", grid=(nt,),
        in_specs=[pl.BlockSpec((ROW_TILE, IN_WIDTH), lambda i: (i, 0)), tab, tab, tab, gain, gain],
        out_specs=[hs(HEADS_A), hs(KV_A), hs(KV_A), hs(HEADS_B), hs(KV_B), hs(KV_B)],
        out_shape=[sh(HEADS_A), sh(KV_A), sh(KV_A), sh(HEADS_B), sh(KV_B), sh(KV_B)],
        compiler_params=_params(("parallel",)),
    )(h_all, cos, sin_a, sin_b, q_g, k_g)


def _qkv_bwd_post(h_all, cos, sin_a, sin_b, q_g, k_g, dqa, dka, dva, dqb, dkb, dvb, n):
    t = h_all.shape[0]
    nt = t // ROW_TILE
    nx = n // ROW_TILE

    def body(h_ref, cos_ref, sa_ref, sb_ref, qg_ref, kg_ref,
             dqa_ref, dka_ref, dva_ref, dqb_ref, dkb_ref, dvb_ref, dh_ref, gs_ref):
        i = pl.program_id(0)
        cos_, sa, sb = cos_ref[...], sa_ref[...], sb_ref[...]
        latent = (i < nx).astype(F32)
        sl = lambda off, hh: h_ref[:, off + hh * HEAD_DIM: off + (hh + 1) * HEAD_DIM]

        def put(off, hh, val):
            dh_ref[:, off + hh * HEAD_DIM: off + (hh + 1) * HEAD_DIM] = val.astype(BF16)

        def norm_bwd(x, gain, dy):
            xn, r = _rms(x)
            dxh = dy * gain
            dx = r * (dxh - xn * jnp.mean(dxh * xn, axis=-1, keepdims=True))
            return dx, jnp.sum(dy * xn, axis=0, keepdims=True)

        for hh in range(HEADS_A):
            put(QA0, hh, _rope_t(dqa_ref[hh] * (ATTN_SCALE * latent), cos_, sa, sb))
        for hh in range(KV_A):
            put(KA0, hh, _rope_t(dka_ref[hh], cos_, sa, sb))
            put(VA0, hh, dva_ref[hh])
        gq = jnp.zeros((1, HEAD_DIM), F32)
        gk = jnp.zeros((1, HEAD_DIM), F32)
        for hh in range(HEADS_B):
            dq_t = dqb_ref[hh // GROUP, :, (hh % GROUP) * ROW_TILE:(hh % GROUP + 1) * ROW_TILE]
            dy = _rope_t(dq_t.T * (ATTN_SCALE * latent), cos_, sa, sb)
            dx, g = norm_bwd(sl(QB0, hh), qg_ref[...], dy)
            put(QB0, hh, dx)
            gq = gq + g
        for hh in range(KV_B):
            dy = _rope_t(dkb_ref[hh], cos_, sa, sb)
            dx, g = norm_bwd(sl(KB0, hh), kg_ref[...], dy)
            put(KB0, hh, dx)
            gk = gk + g
            put(VB0, hh, dvb_ref[hh])
        upd = jnp.concatenate([gq, gk, jnp.zeros((6, HEAD_DIM), F32)], axis=0)

        @pl.when(i == 0)
        def _():
            gs_ref[...] = upd

        @pl.when(i != 0)
        def _():
            gs_ref[...] += upd

    tab = pl.BlockSpec((ROW_TILE, HEAD_DIM), lambda i: (i, 0))
    gain = pl.BlockSpec((1, HEAD_DIM), lambda i: (0, 0))
    lat = lambda nh: pl.BlockSpec((nh, ROW_TILE, HEAD_DIM), lambda i: (0, jnp.minimum(i, nx - 1), 0))
    full = lambda nh: pl.BlockSpec((nh, ROW_TILE, HEAD_DIM), lambda i: (0, i, 0))
    return pl.pallas_call(
        body, name="qkv_bwd_post", grid=(nt,),
        in_specs=[pl.BlockSpec((ROW_TILE, IN_WIDTH), lambda i: (i, 0)), tab, tab, tab, gain, gain,
                  lat(HEADS_A), full(KV_A), full(KV_A),
                  pl.BlockSpec((KV_B, None, HEAD_DIM, GROUP * ROW_TILE), lambda i: (0, jnp.minimum(i, nx - 1), 0, 0)),
                  full(KV_B), full(KV_B)],
        out_specs=[pl.BlockSpec((ROW_TILE, IN_WIDTH), lambda i: (i, 0)),
                   pl.BlockSpec((8, HEAD_DIM), lambda i: (0, 0))],
        out_shape=[jax.ShapeDtypeStruct((t, IN_WIDTH), BF16), jax.ShapeDtypeStruct((8, HEAD_DIM), F32)],
        compiler_params=_params(("arbitrary",)),
    )(h_all, cos, sin_a, sin_b, q_g, k_g, dqa, dka, dva, dqb, dkb, dvb)


GB_TQ = 256
GB_TK = 256


def _heads_rows(ref2d, tq):
    return jnp.concatenate([ref2d[:, hh * HEAD_DIM:(hh + 1) * HEAD_DIM] for hh in range(GROUP)], axis=0)


def _attn_b_fwd(qb, kb, vb, n):
    t = kb.shape[1]
    nk = t // GB_TK
    tq = GB_TQ
    rows = GROUP * tq

    def body(q_ref, k_ref, v_ref, o_ref, lse_ref, m_s, l_s, acc_s):
        q = q_ref[...].reshape(rows, HEAD_DIM)
        m_s[...] = jnp.full((1, rows), -jnp.inf, F32)
        l_s[...] = jnp.zeros((1, rows), F32)
        acc_s[...] = jnp.zeros((HEAD_DIM, rows), F32)

        def step(j, carry):
            start = pl.multiple_of(j * GB_TK, GB_TK)
            ks = k_ref[pl.ds(start, GB_TK), :]
            vs = v_ref[pl.ds(start, GB_TK), :]
            st = lax.dot_general(ks, q, NT, preferred_element_type=F32)
            m_prev = m_s[...]
            m_new = jnp.maximum(m_prev, jnp.max(st, axis=0, keepdims=True))
            p = jnp.exp(st - m_new)
            alpha = jnp.exp(m_prev - m_new)
            l_s[...] = alpha * l_s[...] + jnp.sum(p, axis=0, keepdims=True)
            acc_s[...] = alpha * acc_s[...] + lax.dot_general(vs, p.astype(BF16), TN, preferred_element_type=F32)
            m_s[...] = m_new
            return carry

        lax.fori_loop(0, nk, step, 0)
        ot = acc_s[...] * (1.0 / l_s[...])
        lse_ref[...] = m_s[...] + jnp.log(l_s[...])
        for hh in range(GROUP):
            o_ref[:, hh * HEAD_DIM:(hh + 1) * HEAD_DIM] = ot[:, hh * tq:(hh + 1) * tq].T.astype(BF16)

    return pl.pallas_call(
        body, name="attn_b_fwd", grid=(KV_B, n // tq),
        in_specs=[pl.BlockSpec((GROUP, tq, HEAD_DIM), lambda g, i: (g, i, 0)),
                  pl.BlockSpec((None, t, HEAD_DIM), lambda g, i: (g, 0, 0)),
                  pl.BlockSpec((None, t, HEAD_DIM), lambda g, i: (g, 0, 0))],
        out_specs=[pl.BlockSpec((tq, GROUP * HEAD_DIM), lambda g, i: (i, g)),
                   pl.BlockSpec((None, None, 1, rows), lambda g, i: (g, i, 0, 0))],
        out_shape=[jax.ShapeDtypeStruct((n, HEADS_B * HEAD_DIM), BF16),
                   jax.ShapeDtypeStruct((KV_B, n // tq, 1, rows), F32)],
        scratch_shapes=[pltpu.VMEM((1, rows), F32), pltpu.VMEM((1, rows), F32), pltpu.VMEM((HEAD_DIM, rows), F32)],
        compiler_params=_params(("parallel", "parallel")),
    )(qb, kb, vb)


def _attn_b_bwd(qb, kb, vb, dheads, lse, delta, n):
    t = kb.shape[1]
    nk = t // GB_TK
    tq = GB_TQ
    nq = n // tq
    rows = GROUP * tq

    def body(q_ref, k_ref, v_ref, do_ref, lse_ref, dl_ref, dq_ref, dk_ref, dv_ref):
        j = pl.program_id(1)
        i = pl.program_id(2)
        q = q_ref[...].reshape(rows, HEAD_DIM)
        do = _heads_rows(do_ref, tq)
        ks, vs = k_ref[...], v_ref[...]
        st = lax.dot_general(ks, q, NT, preferred_element_type=F32)
        p = jnp.exp(st - lse_ref[...])
        dpt = lax.dot_general(vs, do, NT, preferred_element_type=F32)
        ds = (p * (dpt - dl_ref[...])).astype(BF16)
        dv_part = lax.dot_general(p.astype(BF16), do, NN, preferred_element_type=F32)
        dk_part = lax.dot_general(ds, q, NN, preferred_element_type=F32)
        dq_part = lax.dot_general(ks, ds, TN, preferred_element_type=F32)

        @pl.when(i == 0)
        def _():
            dk_ref[...] = dk_part
            dv_ref[...] = dv_part

        @pl.when(i != 0)
        def _():
            dk_ref[...] += dk_part
            dv_ref[...] += dv_part

        @pl.when(j == 0)
        def _():
            dq_ref[i] = dq_part

        @pl.when(j != 0)
        def _():
            dq_ref[i] += dq_part

    kv = pl.BlockSpec((None, GB_TK, HEAD_DIM), lambda g, j, i: (g, j, 0))
    row = pl.BlockSpec((None, None, 1, rows), lambda g, j, i: (g, i, 0, 0))
    return pl.pallas_call(
        body, name="attn_b_bwd", grid=(KV_B, nk, nq),
        in_specs=[pl.BlockSpec((GROUP, tq, HEAD_DIM), lambda g, j, i: (g, i, 0)), kv, kv,
                  pl.BlockSpec((tq, GROUP * HEAD_DIM), lambda g, j, i: (i, KV_A + g)), row, row],
        out_specs=[pl.BlockSpec((None, nq, HEAD_DIM, rows), lambda g, j, i: (g, 0, 0, 0)), kv, kv],
        out_shape=[jax.ShapeDtypeStruct((KV_B, nq, HEAD_DIM, rows), F32),
                   jax.ShapeDtypeStruct((KV_B, t, HEAD_DIM), F32),
                   jax.ShapeDtypeStruct((KV_B, t, HEAD_DIM), F32)],
        compiler_params=_params(("parallel", "arbitrary", "arbitrary")),
    )(qb, kb, vb, dheads, lse, delta)


def _delta_rows(dheads, heads):
    n = heads.shape[0]
    tq = GB_TQ
    w = GROUP * HEAD_DIM

    def body(a_ref, b_ref, o_ref):
        prod = a_ref[...].astype(F32) * b_ref[...].astype(F32)
        cols = [jnp.sum(prod[:, hh * HEAD_DIM:(hh + 1) * HEAD_DIM].T, axis=0, keepdims=True) for hh in range(GROUP)]
        o_ref[...] = jnp.concatenate(cols, axis=1)

    blk = pl.BlockSpec((tq, w), lambda g, i: (i, KV_A + g))
    return pl.pallas_call(
        body, name="delta_rows", grid=(KV_B, n // tq),
        in_specs=[blk, blk],
        out_specs=pl.BlockSpec((None, None, 1, GROUP * tq), lambda g, i: (g, i, 0, 0)),
        out_shape=jax.ShapeDtypeStruct((KV_B, n // tq, 1, GROUP * tq), F32),
        compiler_params=_params(("parallel", "parallel")),
    )(dheads, heads)


KWIN = 3 * BLOCK


def _window_scores(q, k_ref, j, n, nb, sink_col):
    c = k_ref.shape[0] - n
    start = pl.multiple_of(jnp.clip(j - 1, 0, nb - 3) * BLOCK, BLOCK)
    kw = k_ref[pl.ds(start, KWIN), :]
    kc = k_ref[pl.ds(n, c), :]
    s_loc = lax.dot_general(q, kw, NT, preferred_element_type=F32)
    s_ctx = lax.dot_general(q, kc, NT, preferred_element_type=F32)
    rows = GROUP * BLOCK
    qpos = j * BLOCK + lax.broadcasted_iota(jnp.int32, (rows, KWIN), 0) % BLOCK
    kpos = start + lax.broadcasted_iota(jnp.int32, (rows, KWIN), 1)
    s_loc = jnp.where(jnp.abs(qpos - kpos) <= WINDOW, s_loc, -jnp.inf)
    m = jnp.maximum(jnp.maximum(jnp.max(s_loc, axis=1, keepdims=True), jnp.max(s_ctx, axis=1, keepdims=True)),
                    sink_col)
    e_loc, e_ctx, e_sink = jnp.exp(s_loc - m), jnp.exp(s_ctx - m), jnp.exp(sink_col - m)
    inv = 1.0 / (jnp.sum(e_loc, axis=1, keepdims=True) + jnp.sum(e_ctx, axis=1, keepdims=True) + e_sink)
    return e_loc * inv, e_ctx * inv, e_sink * inv, start


def _sink_column(sink_ref, g):
    cols = [jnp.broadcast_to(sink_ref[pl.ds(g * GROUP + hh, 1), 0:1], (BLOCK, 1)) for hh in range(GROUP)]
    return jnp.concatenate(cols, axis=0)


def _attn_a_fwd(qa, ka, va, sink_b, n):
    t = ka.shape[1]
    nb = n // BLOCK
    assert nb >= 3

    def body(q_ref, k_ref, v_ref, sink_ref, o_ref):
        g, j = pl.program_id(0), pl.program_id(1)
        q = q_ref[...].reshape(GROUP * BLOCK, HEAD_DIM)
        p_loc, p_ctx, _, start = _window_scores(q, k_ref, j, n, nb, _sink_column(sink_ref, g))
        vw = v_ref[pl.ds(start, KWIN), :]
        vc = v_ref[pl.ds(n, t - n), :]
        o = (lax.dot_general(p_loc.astype(BF16), vw, NN, preferred_element_type=F32)
             + lax.dot_general(p_ctx.astype(BF16), vc, NN, preferred_element_type=F32))
        for hh in range(GROUP):
            o_ref[:, hh * HEAD_DIM:(hh + 1) * HEAD_DIM] = o[hh * BLOCK:(hh + 1) * BLOCK].astype(BF16)

    return pl.pallas_call(
        body, name="attn_a_fwd", grid=(KV_A, nb),
        in_specs=[pl.BlockSpec((GROUP, BLOCK, HEAD_DIM), lambda g, j: (g, j, 0)),
                  pl.BlockSpec((None, t, HEAD_DIM), lambda g, j: (g, 0, 0)),
                  pl.BlockSpec((None, t, HEAD_DIM), lambda g, j: (g, 0, 0)),
                  pl.BlockSpec((HEADS_A, HEAD_DIM), lambda g, j: (0, 0))],
        out_specs=pl.BlockSpec((BLOCK, GROUP * HEAD_DIM), lambda g, j: (j, g)),
        out_shape=jax.ShapeDtypeStruct((n, HEADS_A * HEAD_DIM), BF16),
        compiler_params=_params(("parallel", "parallel")),
    )(qa, ka, va, sink_b)


def _attn_a_bwd(qa, ka, va, sink_b, dheads, n):
    t = ka.shape[1]
    c = t - n
    nb = n // BLOCK

    def body(q_ref, k_ref, v_ref, sink_ref, do_ref, dq_ref, dk_ref, dv_ref, dsink_ref):
        g, j = pl.program_id(0), pl.program_id(1)

        @pl.when(j == 0)
        def _():
            dk_ref[...] = jnp.zeros_like(dk_ref)
            dv_ref[...] = jnp.zeros_like(dv_ref)
            dsink_ref[...] = jnp.zeros_like(dsink_ref)

        q = q_ref[...].reshape(GROUP * BLOCK, HEAD_DIM)
        do = _heads_rows(do_ref, BLOCK)
        p_loc, p_ctx, p_sink, start = _window_scores(q, k_ref, j, n, nb, _sink_column(sink_ref, g))
        kw, vw = k_ref[pl.ds(start, KWIN), :], v_ref[pl.ds(start, KWIN), :]
        kc, vc = k_ref[pl.ds(n, c), :], v_ref[pl.ds(n, c), :]
        dp_loc = lax.dot_general(do, vw, NT, preferred_element_type=F32)
        dp_ctx = lax.dot_general(do, vc, NT, preferred_element_type=F32)
        dl = jnp.sum(p_loc * dp_loc, axis=1, keepdims=True) + jnp.sum(p_ctx * dp_ctx, axis=1, keepdims=True)
        ds_loc = (p_loc * (dp_loc - dl)).astype(BF16)
        ds_ctx = (p_ctx * (dp_ctx - dl)).astype(BF16)
        dq = (lax.dot_general(ds_loc, kw, NN, preferred_element_type=F32)
              + lax.dot_general(ds_ctx, kc, NN, preferred_element_type=F32))
        for hh in range(GROUP):
            dq_ref[hh] = dq[hh * BLOCK:(hh + 1) * BLOCK]
        dk_ref[pl.ds(start, KWIN), :] += lax.dot_general(ds_loc, q, TN, preferred_element_type=F32)
        dv_ref[pl.ds(start, KWIN), :] += lax.dot_general(p_loc.astype(BF16), do, TN, preferred_element_type=F32)
        dk_ref[pl.ds(n, c), :] += lax.dot_general(ds_ctx, q, TN, preferred_element_type=F32)
        dv_ref[pl.ds(n, c), :] += lax.dot_general(p_ctx.astype(BF16), do, TN, preferred_element_type=F32)
        dsk = -(p_sink * dl)
        upd = [jnp.broadcast_to(jnp.sum(dsk[hh * BLOCK:(hh + 1) * BLOCK], axis=0, keepdims=True), (1, HEAD_DIM))
               for hh in range(GROUP)]
        dsink_ref[...] += jnp.concatenate(upd + [jnp.zeros((8 - GROUP, HEAD_DIM), F32)], axis=0)

    res = pl.BlockSpec((None, t, HEAD_DIM), lambda g, j: (g, 0, 0))
    return pl.pallas_call(
        body, name="attn_a_bwd", grid=(KV_A, nb),
        in_specs=[pl.BlockSpec((GROUP, BLOCK, HEAD_DIM), lambda g, j: (g, j, 0)), res, res,
                  pl.BlockSpec((HEADS_A, HEAD_DIM), lambda g, j: (0, 0)),
                  pl.BlockSpec((BLOCK, GROUP * HEAD_DIM), lambda g, j: (j, g))],
        out_specs=[pl.BlockSpec((GROUP, BLOCK, HEAD_DIM), lambda g, j: (g, j, 0)), res, res,
                   pl.BlockSpec((None, 8, HEAD_DIM), lambda g, j: (g, 0, 0))],
        out_shape=[jax.ShapeDtypeStruct((HEADS_A, n, HEAD_DIM), F32),
                   jax.ShapeDtypeStruct((KV_A, t, HEAD_DIM), F32),
                   jax.ShapeDtypeStruct((KV_A, t, HEAD_DIM), F32),
                   jax.ShapeDtypeStruct((KV_A, 8, HEAD_DIM), F32)],
        compiler_params=_params(("parallel", "arbitrary")),
    )(qa, ka, va, sink_b, dheads)


def _ln_stats(r):
    mu = jnp.mean(r, axis=-1, keepdims=True)
    cen = r - mu
    rstd = lax.rsqrt(jnp.mean(cen * cen, axis=-1, keepdims=True) + EPS)
    return cen * rstd, rstd


def _ln_bwd(dy, xhat, rstd, gain):
    dxh = dy * gain
    return rstd * (dxh - jnp.mean(dxh, axis=-1, keepdims=True)
                   - xhat * jnp.mean(dxh * xhat, axis=-1, keepdims=True))


def _accumulate_rows(ref, rows, i):
    pad = [jnp.zeros_like(rows[0])] * (8 - len(rows))
    upd = jnp.concatenate(rows + pad, axis=0)

    @pl.when(i == 0)
    def _():
        ref[...] = upd

    @pl.when(i != 0)
    def _():
        ref[...] += upd


def _colsum(v):
    return jnp.sum(v, axis=0, keepdims=True)


LN_TILE = 128


def _res_ln1(x, a, vec):
    n, d = x.shape

    def body(x_ref, a_ref, v_ref, xh_ref, rs_ref, u_ref):
        r1 = DN_ALPHA * x_ref[...] + v_ref[0:1, :] * a_ref[...]
        xhat, rstd = _ln_stats(r1)
        xh_ref[...] = xhat
        rs_ref[...] = rstd
        x1 = xhat * v_ref[1:2, :] + v_ref[2:3, :]
        u_ref[...] = (x1 * (1.0 + v_ref[3:4, :]) + v_ref[4:5, :]).astype(BF16)

    row = pl.BlockSpec((LN_TILE, d), lambda i: (i, 0))
    return pl.pallas_call(
        body, name="res_ln1", grid=(n // LN_TILE,),
        in_specs=[row, row, pl.BlockSpec((8, d), lambda i: (0, 0))],
        out_specs=[row, pl.BlockSpec((LN_TILE, 1), lambda i: (i, 0)), row],
        out_shape=[jax.ShapeDtypeStruct((n, d), F32), jax.ShapeDtypeStruct((n, 1), F32),
                   jax.ShapeDtypeStruct((n, d), BF16)],
        compiler_params=_params(("parallel",)),
    )(x, a, vec)


def _res_ln2_loss(xhat1, f, target, vec):
    n, d = f.shape

    def body(xh_ref, f_ref, t_ref, v_ref, dr_ref, df_ref, s_ref):
        i = pl.program_id(0)
        x1 = xh_ref[...] * v_ref[1:2, :] + v_ref[2:3, :]
        fv = f_ref[...]
        xhat, rstd = _ln_stats(DN_ALPHA * x1 + v_ref[0:1, :] * fv)
        err = xhat * v_ref[3:4, :] + v_ref[4:5, :] - t_ref[...]
        dy = err * (1.0 / d)
        dr2 = _ln_bwd(dy, xhat, rstd, v_ref[3:4, :])
        dr_ref[...] = dr2
        df_ref[...] = (dr2 * v_ref[0:1, :]).astype(BF16)
        _accumulate_rows(s_ref, [_colsum(dy * xhat), _colsum(dy), _colsum(dr2 * fv),
                                 _colsum(err * err) * (0.5 / d)], i)

    row = pl.BlockSpec((LN_TILE, d), lambda i: (i, 0))
    return pl.pallas_call(
        body, name="res_ln2_loss", grid=(n // LN_TILE,),
        in_specs=[row, row, row, pl.BlockSpec((8, d), lambda i: (0, 0))],
        out_specs=[row, row, pl.BlockSpec((8, d), lambda i: (0, 0))],
        out_shape=[jax.ShapeDtypeStruct((n, d), F32), jax.ShapeDtypeStruct((n, d), BF16),
                   jax.ShapeDtypeStruct((8, d), F32)],
        compiler_params=_params(("arbitrary",)),
    )(xhat1, f, target, vec)


def _ln1_bwd(du2, dr2, xhat1, rstd1, a, vec):
    n, d = du2.shape

    def body(du_ref, dr2_ref, xh_ref, rs_ref, a_ref, v_ref, dxp_ref, da_ref, s_ref):
        i = pl.program_id(0)
        du, xhat = du_ref[...], xh_ref[...]
        x1 = xhat * v_ref[1:2, :] + v_ref[2:3, :]
        dx1 = DN_ALPHA * dr2_ref[...] + du * (1.0 + v_ref[0:1, :])
        dr1 = _ln_bwd(dx1, xhat, rs_ref[...], v_ref[1:2, :])
        dxp_ref[...] = DN_ALPHA * dr1
        da_ref[...] = (dr1 * v_ref[3:4, :]).astype(BF16)
        _accumulate_rows(s_ref, [_colsum(du * x1), _colsum(du), _colsum(dx1 * xhat), _colsum(dx1),
                                 _colsum(dr1 * a_ref[...])], i)

    row = pl.BlockSpec((LN_TILE, d), lambda i: (i, 0))
    return pl.pallas_call(
        body, name="ln1_bwd", grid=(n // LN_TILE,),
        in_specs=[row, row, row, pl.BlockSpec((LN_TILE, 1), lambda i: (i, 0)), row,
                  pl.BlockSpec((8, d), lambda i: (0, 0))],
        out_specs=[row, row, pl.BlockSpec((8, d), lambda i: (0, 0))],
        out_shape=[jax.ShapeDtypeStruct((n, d), F32), jax.ShapeDtypeStruct((n, d), BF16),
                   jax.ShapeDtypeStruct((8, d), F32)],
        compiler_params=_params(("arbitrary",)),
    )(du2, dr2, xhat1, rstd1, a, vec)


def _mod1_bwd(du_all, dxp, x, ctx, mods):
    n, d = x.shape
    nx = n // ROW_TILE

    def body(du_ref, dxp_ref, x_ref, ctx_ref, m_ref, gx_ref, s_ref):
        i = pl.program_id(0)
        du = du_ref[...]
        zero = jnp.zeros((1, d), F32)

        @pl.when(i == 0)
        def _():
            s_ref[...] = jnp.zeros_like(s_ref)

        @pl.when(i < nx)
        def _():
            gx_ref[...] = dxp_ref[...] + du * (1.0 + m_ref[0:1, :])
            s_ref[...] += jnp.concatenate([_colsum(du * x_ref[...]), _colsum(du)] + [zero] * 6, axis=0)

        @pl.when(i >= nx)
        def _():
            s_ref[...] += jnp.concatenate([zero, zero, _colsum(du * ctx_ref[...]), _colsum(du)] + [zero] * 4, axis=0)

    lat = pl.BlockSpec((ROW_TILE, d), lambda i: (jnp.minimum(i, nx - 1), 0))
    return pl.pallas_call(
        body, name="mod1_bwd", grid=(nx + 1,),
        in_specs=[pl.BlockSpec((ROW_TILE, d), lambda i: (i, 0)), lat, lat,
                  pl.BlockSpec((ROW_TILE, d), lambda i: (0, 0)), pl.BlockSpec((8, d), lambda i: (0, 0))],
        out_specs=[lat, pl.BlockSpec((8, d), lambda i: (0, 0))],
        out_shape=[jax.ShapeDtypeStruct((n, d), F32), jax.ShapeDtypeStruct((8, d), F32)],
        compiler_params=_params(("arbitrary",)),
    )(du_all, dxp, x, ctx, mods)


FFN_TM = 512
FFN_TN = 512


def _gate_up(u2, wg, wu):
    n, d = u2.shape
    f = wg.shape[1]

    def body(u_ref, wg_ref, wu_ref, g_ref, up_ref, h_ref):
        u = u_ref[...]
        g = lax.dot_general(u, wg_ref[...], NN, preferred_element_type=F32)
        up = lax.dot_general(u, wu_ref[...], NN, preferred_element_type=F32)
        g_ref[...] = g
        up_ref[...] = up
        h_ref[...] = (g * jax.nn.sigmoid(g) * up).astype(BF16)

    wspec = pl.BlockSpec((d, FFN_TN), lambda j, i: (0, j))
    ospec = pl.BlockSpec((FFN_TM, FFN_TN), lambda j, i: (i, j))
    return pl.pallas_call(
        body, name="gate_up", grid=(f // FFN_TN, n // FFN_TM),
        in_specs=[pl.BlockSpec((FFN_TM, d), lambda j, i: (i, 0)), wspec, wspec],
        out_specs=[ospec, ospec, ospec],
        out_shape=[jax.ShapeDtypeStruct((n, f), F32), jax.ShapeDtypeStruct((n, f), F32),
                   jax.ShapeDtypeStruct((n, f), BF16)],
        compiler_params=_params(("parallel", "parallel")),
    )(u2, wg, wu)


def _glu_bwd(df, wd, g, u):
    n, d = df.shape
    f = wd.shape[0]

    def body(df_ref, wd_ref, g_ref, u_ref, dg_ref, du_ref):
        dh = lax.dot_general(df_ref[...], wd_ref[...], NT, preferred_element_type=F32)
        gv = g_ref[...]
        sig = jax.nn.sigmoid(gv)
        du_ref[...] = (dh * (gv * sig)).astype(BF16)
        dg_ref[...] = (dh * u_ref[...] * (sig * (1.0 + gv * (1.0 - sig)))).astype(BF16)

    tm = min(2 * FFN_TM, n)
    ospec = pl.BlockSpec((tm, FFN_TN), lambda i, j: (i, j))
    return pl.pallas_call(
        body, name="glu_bwd", grid=(n // tm, f // FFN_TN),
        in_specs=[pl.BlockSpec((tm, d), lambda i, j: (i, 0)),
                  pl.BlockSpec((FFN_TN, d), lambda i, j: (j, 0)), ospec, ospec],
        out_specs=[ospec, ospec],
        out_shape=[jax.ShapeDtypeStruct((n, f), BF16), jax.ShapeDtypeStruct((n, f), BF16)],
        compiler_params=_params(("parallel", "parallel")),
    )(df, wd, g, u)


def _du2(dg, du, wg, wu):
    n, f = dg.shape
    d = wg.shape[0]
    tm, tn, tk = min(1024, n), 1024, 1408
    nk = f // tk

    def body(dg_ref, du_ref, wg_ref, wu_ref, o_ref, acc_ref):
        kk = pl.program_id(2)
        part = (lax.dot_general(dg_ref[...], wg_ref[...], NT, preferred_element_type=F32)
                + lax.dot_general(du_ref[...], wu_ref[...], NT, preferred_element_type=F32))

        @pl.when(kk == 0)
        def _():
            acc_ref[...] = part

        @pl.when(kk != 0)
        def _():
            acc_ref[...] += part

        @pl.when(kk == nk - 1)
        def _():
            o_ref[...] = acc_ref[...]

    aspec = pl.BlockSpec((tm, tk), lambda i, j, kk: (i, kk))
    wspec = pl.BlockSpec((tn, tk), lambda i, j, kk: (j, kk))
    return pl.pallas_call(
        body, name="du2", grid=(n // tm, d // tn, nk),
        in_specs=[aspec, aspec, wspec, wspec],
        out_specs=pl.BlockSpec((tm, tn), lambda i, j, kk: (i, j)),
        out_shape=jax.ShapeDtypeStruct((n, d), F32),
        scratch_shapes=[pltpu.VMEM((tm, tn), F32)],
        compiler_params=_params(("parallel", "parallel", "arbitrary")),
    )(dg, du, wg, wu)


def _rows8(rows, d=D_MODEL):
    rows = [r.reshape(1, d).astype(F32) for r in rows]
    return jnp.concatenate(rows + [jnp.zeros((8 - len(rows), d), F32)], axis=0)


W_GROUPS = (("w_in",), ("w_out", "w_gate", "w_up"), ("w_down",))
G_GROUPS = (("w_down", "w_gate", "w_up"), ("w_out",), ("w_in",))


def _layer_fwd_bwd(x, ctx, target, mod, mod_ctx, weights, grads_out, q_g, k_g, sink, ln1_g, ln1_b, ln2_g, ln2_b):
    n, d = x.shape
    c = ctx.shape[0]
    sh1, sc1, g1, sh2, sc2, g2 = [mod[:, k * d:(k + 1) * d] for k in range(6)]
    csh1, csc1 = mod_ctx[:, 0:d], mod_ctx[:, d:2 * d]
    cos, sin_a, sin_b = _rope_tables(n, c)
    sink_b = jnp.broadcast_to(sink.reshape(HEADS_A, 1), (HEADS_A, HEAD_DIM)).astype(F32)

    u_all = _modulate_rows(x, ctx, _rows8([sc1, sh1, csc1, csh1]))
    (w_in,) = weights(0, u_all)
    h_all = _matmul(u_all, w_in, name="qkv_proj", tm=_fit(n + c, 1088), tn=1024, tk=2048, out_dtype=F32)
    qa, ka, va, qb, kb, vb = _qkv_post(h_all, cos, sin_a, sin_b, q_g, k_g)
    o_a = _attn_a_fwd(qa, ka, va, sink_b, n)
    o_b, lse = _attn_b_fwd(qb, kb, vb, n)
    heads = jnp.concatenate([o_a, o_b], axis=1)
    w_out, w_gate, w_up = weights(1, heads)
    a = _matmul(heads, w_out, name="out_proj", tm=1024, tn=1024, tk=2048, out_dtype=F32)
    xhat1, rstd1, u2 = _res_ln1(x, a, _rows8([g1, ln1_g, ln1_b, sc2, sh2]))
    gg, uu, hh = _gate_up(u2, w_gate, w_up)
    (w_down,) = weights(2, hh)
    f = _matmul(hh, w_down, name="ffn_down", tm=1024, tn=1024, tk=1408, out_dtype=F32)
    dr2, df, s_ln2 = _res_ln2_loss(xhat1, f, target, _rows8([g2, ln1_g, ln1_b, ln2_g, ln2_b]))

    dgg, duu = _glu_bwd(df, w_down, gg, uu)
    dw_down = _matmul(hh, df, name="dw_down", ta=True, tm=512, tn=1024, tk=n, out_dtype=BF16)
    dw_gate = _matmul(u2, dgg, name="dw_gate", ta=True, tm=1024, tn=512, tk=n, out_dtype=BF16)
    dw_up = _matmul(u2, duu, name="dw_up", ta=True, tm=1024, tn=512, tk=n, out_dtype=BF16)
    zero = grads_out(0, [dw_down, dw_gate, dw_up])
    du2 = _du2(dgg, duu, w_gate, w_up)
    dxp, da, s_ln1 = _ln1_bwd(du2, dr2, xhat1, rstd1, a, _rows8([sc2, ln1_g, ln1_b, g1]) + zero)

    dheads = _matmul(da, w_out, name="d_heads", tb=True, tm=1024, tn=1024, tk=2048, out_dtype=BF16)
    dw_out = _matmul(heads, da, name="dw_out", ta=True, tm=1024, tn=1024, tk=n, out_dtype=BF16)
    zero = grads_out(1, [dw_out])
    delta = _delta_rows(dheads, heads)
    dqa, dka, dva, dsink = _attn_a_bwd(qa, ka, va, sink_b + zero, dheads, n)
    dqb, dkb, dvb = _attn_b_bwd(qb, kb, vb, dheads, lse, delta, n)
    dh_all, s_gain = _qkv_bwd_post(h_all, cos, sin_a, sin_b, q_g, k_g, dqa, dka, dva, dqb, dkb, dvb, n)
    dw_in = _matmul(u_all, dh_all, name="dw_in", ta=True, tm=1024, tn=1024, tk=n + c, out_dtype=BF16)
    zero = grads_out(2, [dw_in])
    du_all = _matmul(dh_all, w_in, name="d_u1", tb=True, tm=_fit(n + c, 1088), tn=1024, tk=IN_WIDTH, out_dtype=F32)
    grad_x, s_mod1 = _mod1_bwd(du_all, dxp, x, ctx, _rows8([sc1]) + zero)

    dsink_row = jnp.concatenate([dsink[0, 0:GROUP, 0], dsink[1, 0:GROUP, 0]]).reshape(1, HEADS_A)
    misc = jnp.concatenate([s_gain[0:1], s_gain[1:2], dsink_row,
                            jnp.zeros((1, d - 2 * HEAD_DIM - HEADS_A), F32)], axis=1)
    partial = jnp.concatenate([
        s_mod1[1:2], s_mod1[0:1], s_ln1[4:5],
        s_ln1[1:2], s_ln1[0:1], s_ln2[2:3],
        s_mod1[3:4], s_mod1[2:3],
        s_ln1[2:3], s_ln1[3:4], s_ln2[0:1], s_ln2[1:2],
        s_ln2[3:4], misc, jnp.zeros((2, d), F32)], axis=0)
    return grad_x, partial


ANY = pl.BlockSpec(memory_space=pl.ANY)
VMEM_FULL = pl.BlockSpec(memory_space=pltpu.VMEM)
N_CHIP_PEERS = 3


def _me():
    return lax.axis_index("x"), lax.axis_index("y"), lax.axis_index("c")


def _other_chips(x, y):
    return [(1 - x, y), (x, 1 - y), (1 - x, 1 - y)]


def _shard_of(chip):
    return 2 * chip[0] + chip[1]


def _dev_index(x, y, c):
    return 4 * x + 2 * y + c


def _rcopy(src, dst, send_sems, recv_sems, k, dev):
    return pltpu.make_async_remote_copy(src_ref=src, dst_ref=dst, send_sem=send_sems.at[k], recv_sem=recv_sems.at[k],
                                        device_id=dev, device_id_type=MESH)


BIG = (("w_in", (D_MODEL, IN_WIDTH), 1), ("w_out", (MIX_WIDTH, D_MODEL), 0), ("w_gate", (D_MODEL, FFN), 1),
       ("w_up", (D_MODEL, FFN), 1), ("w_down", (FFN, D_MODEL), 0))


def _sub(ref, axis, idx, size):
    start = pl.multiple_of(idx * size, size)
    return ref.at[pl.ds(start, size), :] if axis == 0 else ref.at[:, pl.ds(start, size)]


def _shape_div(shape, axis, parts):
    return tuple(s // parts if a == axis else s for a, s in enumerate(shape))


def _piece(a, ref, shard, half):
    _, full, axis = BIG[a]
    view = _sub(ref, axis, shard, full[axis] // N_SHARD)
    return _sub(view, 1 - axis, half, full[1 - axis] // 2)


HBM = pl.BlockSpec(memory_space=pltpu.HBM)
SEM = pl.BlockSpec(memory_space=pltpu.SEMAPHORE)
EFFECT = pltpu.SideEffectType.DATAFLOW_SIDE_EFFECTING
BIG_INDEX = {name: a for a, (name, _, _) in enumerate(BIG)}


def _in_hbm(arr):
    return pltpu.with_memory_space_constraint(arr, pltpu.HBM)


def _gather_start(tag, arrs, bufs, prev):
    n_arr = len(arrs)

    def body(*refs):
        ins = refs[:n_arr]
        send_sems, recv_sems = refs[n_arr + 1], refs[n_arr + 2]
        token = refs[-1]
        x, y, c = _me()
        s_me = _shard_of((x, y))
        for i, a in enumerate(arrs):
            mine = _piece(a, ins[i], s_me, c)
            for j, chip in enumerate(_other_chips(x, y)):
                _rcopy(mine, mine, send_sems, recv_sems, N_CHIP_PEERS * i + j, (*chip, c)).start()
        token[...] = jnp.zeros_like(token)

    n_sem = N_CHIP_PEERS * n_arr
    outs = pl.pallas_call(
        body, name="gather_start_" + tag,
        in_specs=[HBM] * n_arr + [ANY],
        out_specs=[SEM, SEM] + [HBM] * n_arr + [VMEM_FULL],
        out_shape=[pltpu.SemaphoreType.DMA((n_sem,)), pltpu.SemaphoreType.DMA((n_sem,))]
        + [pltpu.HBM(BIG[a][1], BF16) for a in arrs] + [jax.ShapeDtypeStruct((8, HEAD_DIM), F32)],
        input_output_aliases={i: 2 + i for i in range(n_arr)},
        compiler_params=pltpu.CompilerParams(has_side_effects=EFFECT),
    )(*[_in_hbm(b) for b in bufs], prev)
    return outs[0], outs[1], list(outs[2:2 + n_arr]), outs[-1]


def _gather_wait(tag, arrs, send_sems, recv_sems, bufs, after):
    n_arr = len(arrs)

    def body(*refs):
        ins = refs[:n_arr]
        send_sems_, recv_sems_ = refs[n_arr], refs[n_arr + 1]
        x, y, c = _me()
        s_me = _shard_of((x, y))
        for i, a in enumerate(arrs):
            mine = _piece(a, ins[i], s_me, c)
            for j, chip in enumerate(_other_chips(x, y)):
                landed = _piece(a, ins[i], _shard_of(chip), c)
                cp = _rcopy(mine, landed, send_sems_, recv_sems_, N_CHIP_PEERS * i + j, (*chip, c))
                cp.wait_send()
                cp.wait_recv()

    outs = pl.pallas_call(
        body, name="gather_wait_" + tag,
        in_specs=[HBM] * n_arr + [SEM, SEM, ANY],
        out_specs=[HBM] * n_arr,
        out_shape=[pltpu.HBM(BIG[a][1], BF16) for a in arrs],
        input_output_aliases={i: i for i in range(n_arr)},
        compiler_params=pltpu.CompilerParams(has_side_effects=EFFECT),
    )(*bufs, send_sems, recv_sems, after)
    return list(outs)


def _gather_forward(tag, arrs, bufs):
    n_arr = len(arrs)

    def body(*refs):
        outs = refs[n_arr:2 * n_arr]
        send_sems, recv_sems = refs[2 * n_arr:]
        x, y, c = _me()
        sibling = (x, y, 1 - c)
        chips = _other_chips(x, y)
        copies = []
        for i, a in enumerate(arrs):
            for j, chip in enumerate(chips):
                landed = _piece(a, outs[i], _shard_of(chip), c)
                cp = _rcopy(landed, landed, send_sems, recv_sems, N_CHIP_PEERS * i + j, sibling)
                cp.start()
                copies.append(cp)
        for i, a in enumerate(arrs):
            for j, chip in enumerate(chips):
                other = _piece(a, outs[i], _shard_of(chip), 1 - c)
                _rcopy(other, other, send_sems, recv_sems, N_CHIP_PEERS * i + j, sibling).wait_recv()
        for cp in copies:
            cp.wait_send()

    n_sem = N_CHIP_PEERS * n_arr
    return list(pl.pallas_call(
        body, name="gather_forward_" + tag,
        in_specs=[ANY] * n_arr, out_specs=[ANY] * n_arr,
        out_shape=[jax.ShapeDtypeStruct(BIG[a][1], BF16) for a in arrs],
        input_output_aliases={i: i for i in range(n_arr)},
        scratch_shapes=[pltpu.SemaphoreType.DMA((n_sem,)), pltpu.SemaphoreType.DMA((n_sem,))],
    )(*bufs))


def _exchange_halves(tag, arrs, dws):
    n_arr = len(arrs)

    def body(*refs):
        ins, outs = refs[:n_arr], refs[n_arr:2 * n_arr]
        send_sems, recv_sems = refs[2 * n_arr:]
        x, y, c = _me()
        copies = []
        for i, a in enumerate(arrs):
            _, full, axis = BIG[a]
            src = _sub(ins[i], 1 - axis, 1 - c, full[1 - axis] // 2)
            cp = _rcopy(src, outs[i], send_sems, recv_sems, i, (x, y, 1 - c))
            cp.start()
            copies.append(cp)
        for cp in copies:
            cp.wait()

    return list(pl.pallas_call(
        body, name="grad_exchange_halves_" + tag,
        in_specs=[ANY] * n_arr, out_specs=[ANY] * n_arr,
        out_shape=[jax.ShapeDtypeStruct(_shape_div(BIG[a][1], 1 - BIG[a][2], 2), BF16) for a in arrs],
        scratch_shapes=[pltpu.SemaphoreType.DMA((n_arr,)), pltpu.SemaphoreType.DMA((n_arr,))],
    )(*dws))


def _received_shape(a):
    _, full, axis = BIG[a]
    return (N_CHIP_PEERS,) + _shape_div(_shape_div(full, 1 - axis, 2), axis, N_SHARD)


def _pieces_start(tag, arrs, halves):
    n_arr = len(arrs)

    def body(*refs):
        srcs, lands = refs[:n_arr], refs[n_arr:2 * n_arr]
        send_sems, recv_sems = refs[2 * n_arr], refs[2 * n_arr + 1]
        token = refs[-1]
        x, y, c = _me()
        for i, a in enumerate(arrs):
            _, full, axis = BIG[a]
            for j, chip in enumerate(_other_chips(x, y)):
                src = _sub(srcs[i], axis, _shard_of(chip), full[axis] // N_SHARD)
                _rcopy(src, lands[i].at[j], send_sems, recv_sems, N_CHIP_PEERS * i + j, (*chip, c)).start()
        token[...] = jnp.zeros_like(token)

    n_sem = N_CHIP_PEERS * n_arr
    lands = [_in_hbm(lax.empty(_received_shape(a), BF16)) for a in arrs]
    outs = pl.pallas_call(
        body, name="grad_pieces_start_" + tag,
        in_specs=[HBM] * (2 * n_arr),
        out_specs=[SEM, SEM] + [HBM] * (2 * n_arr) + [VMEM_FULL],
        out_shape=[pltpu.SemaphoreType.DMA((n_sem,)), pltpu.SemaphoreType.DMA((n_sem,))]
        + [pltpu.HBM(h.shape, BF16) for h in halves] + [pltpu.HBM(_received_shape(a), BF16) for a in arrs]
        + [jax.ShapeDtypeStruct((8, HEAD_DIM), F32)],
        input_output_aliases={i: 2 + i for i in range(2 * n_arr)},
        compiler_params=pltpu.CompilerParams(has_side_effects=EFFECT),
    )(*[_in_hbm(h) for h in halves], *lands)
    return outs[0], outs[1], list(outs[2:2 + n_arr]), list(outs[2 + n_arr:2 + 2 * n_arr]), outs[-1]


def _pieces_wait(tag, arrs, send_sems, recv_sems, halves, lands, after):
    n_arr = len(arrs)

    def body(*refs):
        srcs, lands_ = refs[:n_arr], refs[n_arr:2 * n_arr]
        send_sems_, recv_sems_ = refs[2 * n_arr], refs[2 * n_arr + 1]
        x, y, c = _me()
        for i, a in enumerate(arrs):
            _, full, axis = BIG[a]
            for j, chip in enumerate(_other_chips(x, y)):
                src = _sub(srcs[i], axis, _shard_of(chip), full[axis] // N_SHARD)
                cp = _rcopy(src, lands_[i].at[j], send_sems_, recv_sems_, N_CHIP_PEERS * i + j, (*chip, c))
                cp.wait_send()
                cp.wait_recv()

    outs = pl.pallas_call(
        body, name="grad_pieces_wait_" + tag,
        in_specs=[HBM] * (2 * n_arr) + [SEM, SEM, ANY],
        out_specs=[HBM] * (2 * n_arr),
        out_shape=[pltpu.HBM(h.shape, BF16) for h in halves] + [pltpu.HBM(_received_shape(a), BF16) for a in arrs],
        input_output_aliases={i: i for i in range(2 * n_arr)},
        compiler_params=pltpu.CompilerParams(has_side_effects=EFFECT),
    )(*halves, *lands, send_sems, recv_sems, after)
    return list(outs[:n_arr]), list(outs[n_arr:])


def _join_halves(tag, g_halves):
    n_arr = len(g_halves)

    def body(*refs):
        ins, outs = refs[:n_arr], refs[n_arr:2 * n_arr]
        send_sems, recv_sems = refs[2 * n_arr:]
        x, y, c = _me()
        copies = []
        for i in range(n_arr):
            cp = _rcopy(ins[i], outs[i], send_sems, recv_sems, i, (x, y, 1 - c))
            cp.start()
            copies.append(cp)
        for cp in copies:
            cp.wait()

    return list(pl.pallas_call(
        body, name="grad_join_halves_" + tag,
        in_specs=[ANY] * n_arr, out_specs=[ANY] * n_arr,
        out_shape=[jax.ShapeDtypeStruct(g.shape, F32) for g in g_halves],
        scratch_shapes=[pltpu.SemaphoreType.DMA((n_arr,)), pltpu.SemaphoreType.DMA((n_arr,))],
    )(*g_halves))


def _window_sum(name, big, widx, axis, nwin, extra, stacked, out_dtype):
    rows, cols = _shape_div(big.shape, axis, nwin)
    tr = _fit(rows, ROW_TILE)
    nbr = rows // tr

    def body(w_ref, big_ref, ex_ref, o_ref):
        acc = big_ref[...].astype(F32)
        if stacked:
            for j in range(N_CHIP_PEERS):
                acc = acc + ex_ref[j].astype(F32)
        else:
            acc = acc + ex_ref[...].astype(F32)
        o_ref[...] = acc.astype(o_ref.dtype)

    if axis == 0:
        big_spec = pl.BlockSpec((tr, cols), lambda i, w: (w[0] * nbr + i, 0))
    else:
        big_spec = pl.BlockSpec((tr, cols), lambda i, w: (i, w[0]))
    ex_spec = (pl.BlockSpec((N_CHIP_PEERS, tr, cols), lambda i, w: (0, i, 0)) if stacked
               else pl.BlockSpec((tr, cols), lambda i, w: (i, 0)))
    return pl.pallas_call(
        body, name=name,
        grid_spec=pltpu.PrefetchScalarGridSpec(
            num_scalar_prefetch=1, grid=(nbr,), in_specs=[big_spec, ex_spec],
            out_specs=pl.BlockSpec((tr, cols), lambda i, w: (i, 0))),
        out_shape=jax.ShapeDtypeStruct((rows, cols), out_dtype),
        compiler_params=_params(("parallel",)),
    )(widx.reshape(1).astype(jnp.int32), big, extra)


def _scatter_begin(tag, arrs, dws):
    _, _, c = _me()
    from_sibling = _exchange_halves(tag, arrs, dws)
    halves = [_window_sum("grad_sum_halves_" + BIG[a][0], dw, c, 1 - BIG[a][2], 2, fs, False, BF16)
              for a, dw, fs in zip(arrs, dws, from_sibling)]
    send_sems, recv_sems, halves, lands, token = _pieces_start(tag, arrs, halves)
    return (send_sems, recv_sems, halves, lands), token[0, 0]


def _scatter_end(tag, arrs, state, after):
    x, y, _ = _me()
    send_sems, recv_sems, halves, lands = state
    halves, lands = _pieces_wait(tag, arrs, send_sems, recv_sems, halves, lands, after)
    g_own = [_window_sum("grad_sum_pieces_" + BIG[a][0], h, _shard_of((x, y)), BIG[a][2], N_SHARD, r, True, F32)
             for a, h, r in zip(arrs, halves, lands)]
    return g_own, _join_halves(tag, g_own)


def _gather_rows(block, name):
    r, d = block.shape

    def body(in_ref, out_ref, send_sems, recv_sems):
        x, y, c = _me()
        out_ref[_dev_index(x, y, c)] = in_ref[...]
        copies = []
        for mask in range(1, N_DEV):
            peer = (x ^ (mask >> 2), y ^ ((mask >> 1) & 1), c ^ (mask & 1))
            cp = _rcopy(in_ref, out_ref.at[_dev_index(x, y, c)], send_sems, recv_sems, mask - 1, peer)
            cp.start()
            copies.append((cp, peer))
        for mask in range(1, N_DEV):
            peer = (x ^ (mask >> 2), y ^ ((mask >> 1) & 1), c ^ (mask & 1))
            landed = out_ref.at[_dev_index(*peer)]
            _rcopy(landed, landed, send_sems, recv_sems, mask - 1, peer).wait_recv()
        for cp, _ in copies:
            cp.wait_send()

    return pl.pallas_call(
        body, name=name, in_specs=[VMEM_FULL], out_specs=VMEM_FULL,
        out_shape=jax.ShapeDtypeStruct((N_DEV, r, d), F32),
        scratch_shapes=[pltpu.SemaphoreType.DMA((N_DEV - 1,)), pltpu.SemaphoreType.DMA((N_DEV - 1,))],
    )(block)


ADA_ROWS = 80
ADA_W = 6 * D_MODEL // N_SHARD


def _ada_forward(c_block, cctx_block, w_ada, b_shard):
    d = c_block.shape[1]

    def body(c_ref, cc_ref, w_ref, b_ref, act_ref, mods_ref, raw, mloc, send_sems, recv_sems):
        x, y, c = _me()
        me = _dev_index(x, y, c)
        s_me = _shard_of((x, y))
        raw[72:ADA_ROWS, :] = jnp.zeros((ADA_ROWS - 72, d), F32)
        raw[pl.ds(pl.multiple_of(me * 8, 8), 8), :] = c_ref[...]
        raw[64:72, :] = cc_ref[...]
        sends = []
        for mask in range(1, N_DEV):
            peer = (x ^ (mask >> 2), y ^ ((mask >> 1) & 1), c ^ (mask & 1))
            cp = _rcopy(c_ref, raw.at[pl.ds(pl.multiple_of(me * 8, 8), 8), :], send_sems, recv_sems, mask - 1, peer)
            cp.start()
            sends.append(cp)
        for mask in range(1, N_DEV):
            peer = (x ^ (mask >> 2), y ^ ((mask >> 1) & 1), c ^ (mask & 1))
            landed = raw.at[pl.ds(pl.multiple_of(_dev_index(*peer) * 8, 8), 8), :]
            _rcopy(landed, landed, send_sems, recv_sems, mask - 1, peer).wait_recv()
        v = raw[...]
        act = v * jax.nn.sigmoid(v)
        act_ref[...] = act
        mloc[...] = lax.dot_general(act.astype(BF16), w_ref[...].astype(BF16), NN,
                                    preferred_element_type=F32) + b_ref[...]
        mods_ref[s_me, 0:8, :] = mloc[pl.ds(pl.multiple_of(me * 8, 8), 8), :]
        mods_ref[s_me, 8:16, :] = mloc[64:72, :]
        base = N_DEV - 1
        for j, chip in enumerate(_other_chips(x, y)):
            peer = (*chip, c)
            rows = mloc.at[pl.ds(pl.multiple_of(_dev_index(*peer) * 8, 8), 8), :]
            cp = _rcopy(rows, mods_ref.at[s_me, 0:8, :], send_sems, recv_sems, base + 2 * j, peer)
            cp.start()
            sends.append(cp)
            cp = _rcopy(mloc.at[64:72, :], mods_ref.at[s_me, 8:16, :], send_sems, recv_sems, base + 2 * j + 1, peer)
            cp.start()
            sends.append(cp)
        for j, chip in enumerate(_other_chips(x, y)):
            for part in range(2):
                landed = mods_ref.at[_shard_of(chip), 8 * part:8 * part + 8, :]
                _rcopy(landed, landed, send_sems, recv_sems, base + 2 * j + part, (*chip, c)).wait_recv()
        for cp in sends:
            cp.wait_send()

    n_sem = N_DEV - 1 + 2 * N_CHIP_PEERS
    return pl.pallas_call(
        body, name="ada_forward",
        in_specs=[VMEM_FULL] * 4, out_specs=[VMEM_FULL, VMEM_FULL],
        out_shape=[jax.ShapeDtypeStruct((ADA_ROWS, d), F32), jax.ShapeDtypeStruct((N_SHARD, 16, ADA_W), F32)],
        scratch_shapes=[pltpu.VMEM((ADA_ROWS, d), F32), pltpu.VMEM((ADA_ROWS, ADA_W), F32),
                        pltpu.SemaphoreType.DMA((n_sem,)), pltpu.SemaphoreType.DMA((n_sem,))],
        compiler_params=pltpu.CompilerParams(vmem_limit_bytes=VMEM_LIMIT),
    )(c_block, cctx_block, w_ada, b_shard)


def _small_reduce(gathered):
    d = gathered.shape[2]

    def body(g_ref, o_ref):
        tot = g_ref[0]
        for i in range(1, N_DEV):
            tot = tot + g_ref[i]
        o_ref[...] = tot
        o_ref[0:2, :] = tot[0:2] + tot[6:8]
        o_ref[12:13, :] = jnp.broadcast_to(jnp.sum(tot[12:13], axis=1, keepdims=True), (1, d))

    return pl.pallas_call(body, name="small_reduce", in_specs=[VMEM_FULL], out_specs=VMEM_FULL,
                          out_shape=jax.ShapeDtypeStruct((16, d), F32))(gathered)


def _cctx_grad(gathered, c_ctx):
    d = gathered.shape[2]

    def body(g_ref, c_ref, o_ref):
        tot = g_ref[0, 0:1, :]
        for chip in range(1, N_SHARD):
            tot = tot + g_ref[2 * chip, 0:1, :]
        v = c_ref[...]
        sig = jax.nn.sigmoid(v)
        o_ref[...] = tot * (sig * (1.0 + v * (1.0 - sig)))

    return pl.pallas_call(body, name="cctx_grad", in_specs=[VMEM_FULL, VMEM_FULL], out_specs=VMEM_FULL,
                          out_shape=jax.ShapeDtypeStruct((1, d), F32))(gathered, c_ctx.reshape(1, d))


def _cast_into_full(w, shard, full, axis, name):
    r, cdim = w.shape
    tr = _fit(r, ROW_TILE)
    nbr = r // tr

    def body(s_ref, w_ref, o_ref):
        o_ref[...] = w_ref[...].astype(BF16)

    if axis == 0:
        out_spec = pl.BlockSpec((tr, cdim), lambda i, s: (s[0] * nbr + i, 0))
    else:
        out_spec = pl.BlockSpec((tr, cdim), lambda i, s: (i, s[0]))
    return pl.pallas_call(
        body, name=name,
        grid_spec=pltpu.PrefetchScalarGridSpec(
            num_scalar_prefetch=1, grid=(nbr,), in_specs=[pl.BlockSpec((tr, cdim), lambda i, s: (i, 0))],
            out_specs=out_spec),
        out_shape=jax.ShapeDtypeStruct(full, BF16), compiler_params=_params(("parallel",)),
    )(shard.reshape(1).astype(jnp.int32), w)


def _adamw_halves(w, g_own, g_other, m, v, core, axis, name):
    r, cdim = w.shape
    hr, hc = (r // 2, cdim) if axis == 1 else (r, cdim // 2)
    assert g_own.shape == (hr, hc) and g_other.shape == (hr, hc)
    tr = _fit(hr, 128)
    nb = hr // tr
    c1 = 1.0 - ADAM_B1 ** ADAM_STEP
    c2 = 1.0 - ADAM_B2 ** ADAM_STEP

    def body(c_ref, w_ref, go_ref, gt_ref, m_ref, v_ref, g_ref, d_ref, nm_ref, nv_ref):
        gv = jnp.where(pl.program_id(0) == c_ref[0], go_ref[...], gt_ref[...])
        nm = ADAM_B1 * m_ref[...] + (1.0 - ADAM_B1) * gv
        nv = ADAM_B2 * v_ref[...] + (1.0 - ADAM_B2) * (gv * gv)
        g_ref[...] = gv
        nm_ref[...] = nm
        nv_ref[...] = nv
        d_ref[...] = -ADAM_LR * ((nm / c1) / (jnp.sqrt(nv / c2) + ADAM_EPS) + ADAM_WD * w_ref[...])

    if axis == 1:
        big = pl.BlockSpec((tr, hc), lambda p, i, c: (p * nb + i, 0))
    else:
        big = pl.BlockSpec((tr, hc), lambda p, i, c: (i, p))
    half = pl.BlockSpec((tr, hc), lambda p, i, c: (i, 0))
    sh = jax.ShapeDtypeStruct((r, cdim), F32)
    return pl.pallas_call(
        body, name=name,
        grid_spec=pltpu.PrefetchScalarGridSpec(
            num_scalar_prefetch=1, grid=(2, nb), in_specs=[big, half, half, big, big], out_specs=[big] * 4),
        out_shape=[sh] * 4, compiler_params=_params(("parallel", "parallel")),
    )(core.reshape(1).astype(jnp.int32), w, g_own, g_other, m, v)


def _adamw(w, g, m, v, name):
    r, cdim = w.shape
    tr = _fit(r, 128) if r % (ROW_TILE // 4) == 0 else r
    c1 = 1.0 - ADAM_B1 ** ADAM_STEP
    c2 = 1.0 - ADAM_B2 ** ADAM_STEP

    def body(w_ref, g_ref, m_ref, v_ref, d_ref, nm_ref, nv_ref):
        gv = g_ref[...]
        nm = ADAM_B1 * m_ref[...] + (1.0 - ADAM_B1) * gv
        nv = ADAM_B2 * v_ref[...] + (1.0 - ADAM_B2) * (gv * gv)
        nm_ref[...] = nm
        nv_ref[...] = nv
        d_ref[...] = -ADAM_LR * ((nm / c1) / (jnp.sqrt(nv / c2) + ADAM_EPS) + ADAM_WD * w_ref[...])

    spec = pl.BlockSpec((tr, cdim), lambda i: (i, 0))
    sh = jax.ShapeDtypeStruct((r, cdim), F32)
    return pl.pallas_call(body, name=name, grid=(r // tr,), in_specs=[spec] * 4, out_specs=[spec] * 3,
                          out_shape=[sh, sh, sh], compiler_params=_params(("parallel",)))(w, g, m, v)


SMALL = (("c_ctx", D_MODEL), ("b_ada", 6 * D_MODEL), ("q_norm_g", HEAD_DIM), ("k_norm_g", HEAD_DIM),
         ("sink_logit", HEADS_A), ("ln1_g", D_MODEL), ("ln1_b", D_MODEL), ("ln2_g", D_MODEL), ("ln2_b", D_MODEL))
WEIGHT_ORDER = ("c_ctx", "w_ada", "b_ada", "w_in", "q_norm_g", "k_norm_g", "sink_logit", "w_out", "ln1_g", "ln1_b",
                "w_gate", "w_up", "w_down", "ln2_g", "ln2_b")


def kernel(x, c, ctx, c_ctx, w_ada, b_ada, w_in, q_norm_g, k_norm_g, sink_logit, w_out, ln1_g, ln1_b, w_gate, w_up, w_down, ln2_g, ln2_b, loss_target, m_c_ctx, m_w_ada, m_b_ada, m_w_in, m_q_norm_g, m_k_norm_g, m_sink_logit, m_w_out, m_ln1_g, m_ln1_b, m_w_gate, m_w_up, m_w_down, m_ln2_g, m_ln2_b, v_c_ctx, v_w_ada, v_b_ada, v_w_in, v_q_norm_g, v_k_norm_g, v_sink_logit, v_w_out, v_ln1_g, v_ln1_b, v_w_gate, v_w_up, v_w_down, v_ln2_g, v_ln2_b):
    d = D_MODEL
    w = dict(c_ctx=c_ctx, w_ada=w_ada[0], b_ada=b_ada, w_in=w_in[0], q_norm_g=q_norm_g, k_norm_g=k_norm_g,
             sink_logit=sink_logit, w_out=w_out[0], ln1_g=ln1_g, ln1_b=ln1_b, w_gate=w_gate[0], w_up=w_up[0],
             w_down=w_down[0], ln2_g=ln2_g, ln2_b=ln2_b)
    m = dict(c_ctx=m_c_ctx, w_ada=m_w_ada[0], b_ada=m_b_ada, w_in=m_w_in[0], q_norm_g=m_q_norm_g, k_norm_g=m_k_norm_g,
             sink_logit=m_sink_logit, w_out=m_w_out[0], ln1_g=m_ln1_g, ln1_b=m_ln1_b, w_gate=m_w_gate[0],
             w_up=m_w_up[0], w_down=m_w_down[0], ln2_g=m_ln2_g, ln2_b=m_ln2_b)
    v = dict(c_ctx=v_c_ctx, w_ada=v_w_ada[0], b_ada=v_b_ada, w_in=v_w_in[0], q_norm_g=v_q_norm_g, k_norm_g=v_k_norm_g,
             sink_logit=v_sink_logit, w_out=v_w_out[0], ln1_g=v_ln1_g, ln1_b=v_ln1_b, w_gate=v_w_gate[0],
             w_up=v_w_up[0], w_down=v_w_down[0], ln2_g=v_ln2_g, ln2_b=v_ln2_b)
    mx, my, mc = _me()
    s_me = _shard_of((mx, my))
    me = _dev_index(mx, my, mc)
    pad8 = lambda row: jnp.concatenate([row.reshape(1, -1), jnp.zeros((7, row.size), F32)], axis=0)

    b_shard = lax.dynamic_slice(b_ada, (0, s_me * ADA_W), (1, ADA_W))
    act, mods4 = _ada_forward(pad8(c), pad8(c_ctx), w["w_ada"], b_shard)

    bufs = {name: _cast_into_full(w[name], s_me, shape, axis, "cast_" + name) for name, shape, axis in BIG}
    gathers = []
    prev = mods4
    for k, names in enumerate(W_GROUPS):
        arrs = tuple(BIG_INDEX[name] for name in names)
        send_sems, recv_sems, thru, prev = _gather_start("g%d" % k, arrs, [bufs[name] for name in names], prev)
        gathers.append((arrs, send_sems, recv_sems, thru))

    def weights(k, after):
        arrs, send_sems, recv_sems, thru = gathers[k]
        landed = _gather_wait("g%d" % k, arrs, send_sems, recv_sems, thru, after)
        return _gather_forward("g%d" % k, arrs, landed)

    mod = jnp.transpose(mods4[:, 0:1, :], (1, 0, 2)).reshape(1, 6 * d) + prev[0, 0]
    mod_ctx = jnp.transpose(mods4[:, 8:9, :], (1, 0, 2)).reshape(1, 6 * d)

    scatters = {}

    def grads_out(k, dws):
        arrs = tuple(BIG_INDEX[name] for name in G_GROUPS[k])
        scatters[k], zero = _scatter_begin("g%d" % k, arrs, dws)
        return zero

    grad_x, partial = _layer_fwd_bwd(x[0], ctx[0], loss_target[0], mod, mod_ctx, weights, grads_out,
                                     q_norm_g, k_norm_g, sink_logit, ln1_g, ln1_b, ln2_g, ln2_b)
    grads, delta, new_m, new_v = {}, {}, {}, {}

    gathered = _gather_rows(partial, "gather_partials")
    tot = _small_reduce(gathered)
    grads["b_ada"] = tot[0:6].reshape(1, 6 * d)
    grads["ln1_g"], grads["ln1_b"], grads["ln2_g"], grads["ln2_b"] = tot[8:9], tot[9:10], tot[10:11], tot[11:12]
    grads["q_norm_g"] = tot[13:14, 0:HEAD_DIM]
    grads["k_norm_g"] = tot[13:14, HEAD_DIM:2 * HEAD_DIM]
    grads["sink_logit"] = tot[13:14, 2 * HEAD_DIM:2 * HEAD_DIM + HEADS_A]
    loss = tot[12, 0]

    dm_all = gathered[:, 0:6, :].reshape(N_DEV, 6 * d)
    dmc_tot = jnp.concatenate([tot[6:8].reshape(1, 2 * d), jnp.zeros((1, 4 * d), F32)], axis=1)
    dm_rows = jnp.concatenate([pad8(dm_all[i]) for i in range(N_DEV)] + [pad8(dmc_tot), jnp.zeros((8, 6 * d), F32)], axis=0)
    dm_shard = lax.dynamic_slice(dm_rows, (0, s_me * ADA_W), (ADA_ROWS, ADA_W))
    grads["w_ada"] = _matmul(act, dm_shard, name="dw_ada", ta=True, tm=1024, tn=1024, tk=ADA_ROWS, out_dtype=F32)
    dmc_shard = lax.dynamic_slice(pad8(dmc_tot), (0, s_me * ADA_W), (8, ADA_W))
    cc_part = _matmul(dmc_shard, w["w_ada"], name="d_cctx", tb=True, tm=8, tn=1024, tk=1536, out_dtype=F32)
    grads["c_ctx"] = _cctx_grad(_gather_rows(cc_part, "gather_cctx"), c_ctx).reshape(d)

    delta["w_ada"], new_m["w_ada"], new_v["w_ada"] = _adamw(w["w_ada"], grads["w_ada"], m["w_ada"], v["w_ada"],
                                                            "adamw_w_ada")
    pack = lambda t: jnp.concatenate([t[name].reshape(1, size) for name, size in SMALL], axis=1)
    pd, pm, pv = _adamw(pack(w), pack(grads), pack(m), pack(v), "adamw_small")
    off = 0
    for name, size in SMALL:
        delta[name], new_m[name], new_v[name] = [t[:, off:off + size].reshape(w[name].shape) for t in (pd, pm, pv)]
        grads[name] = grads[name].reshape(w[name].shape)
        off += size

    after = pd
    for k, names in enumerate(G_GROUPS):
        arrs = tuple(BIG_INDEX[name] for name in names)
        g_own, g_other = _scatter_end("g%d" % k, arrs, scatters[k], after)
        for name, own, other in zip(names, g_own, g_other):
            grads[name], delta[name], new_m[name], new_v[name] = _adamw_halves(
                w[name], own, other, m[name], v[name], mc, BIG[BIG_INDEX[name]][2], "adamw_" + name)
            after = new_v[name]

    lead = lambda name, t: t[None] if name in ("w_ada", "w_in", "w_out", "w_gate", "w_up", "w_down") else t
    outs = [loss, grad_x[None]]
    for group in (grads, delta, new_m, new_v):
        outs += [lead(name, group[name]) for name in WEIGHT_ORDER]
    return tuple(outs)
```

```python
import functools
import math

import jax
import jax.numpy as jnp
from jax import lax
from jax.experimental import pallas as pl
from jax.experimental.pallas import tpu as pltpu

F32 = jnp.float32
BF16 = jnp.bfloat16
MESH = pl.DeviceIdType.MESH

D_MODEL = 2048
HEAD_DIM = 128
HEADS_A = 8
HEADS_B = 8
KV_A = 2
KV_B = 2
GROUP = 4
GRID_W = 64
WINDOW = 128
BLOCK = 128
FFN = 5632
IN_WIDTH = 3072
MIX_WIDTH = 2048
ROPE_THETA = 10000.0
EPS = 1e-6
ATTN_SCALE = HEAD_DIM ** -0.5
DN_ALPHA = 2.0 ** 0.25
N_SHARD = 4
N_DEV = 8

ADAM_LR = 0.001
ADAM_B1 = 0.9
ADAM_B2 = 0.999
ADAM_EPS = 1e-08
ADAM_WD = 0.01
ADAM_STEP = 10

QA0, KA0, VA0, QB0, KB0, VB0 = 0, 1024, 1280, 1536, 2560, 2816

VMEM_LIMIT = 56 * 1024 * 1024
ROW_TILE = 256
NN = (((1,), (0,)), ((), ()))
NT = (((1,), (1,)), ((), ()))
TN = (((0,), (0,)), ((), ()))


def _fit(total, pref):
    step = ROW_TILE // 4
    best = step
    for cand in range(step, pref + 1, step):
        if total % cand == 0:
            best = cand
    return best


def _params(sem=None):
    return pltpu.CompilerParams(dimension_semantics=sem, vmem_limit_bytes=VMEM_LIMIT)


def _matmul(a, b, *, name, ta=False, tb=False, tm, tn, tk, out_dtype):
    m = a.shape[1] if ta else a.shape[0]
    k = a.shape[0] if ta else a.shape[1]
    n = b.shape[0] if tb else b.shape[1]
    assert (b.shape[1] if tb else b.shape[0]) == k
    tm, tn, tk = min(tm, m), min(tn, n), min(tk, k)
    assert m % tm == 0 and n % tn == 0 and k % tk == 0, (name, m, n, k, tm, tn, tk)
    nk = k // tk
    dn = (((0 if ta else 1,), (1 if tb else 0,)), ((), ()))

    def product(a_ref, b_ref):
        return lax.dot_general(a_ref[...].astype(BF16), b_ref[...].astype(BF16), dn, preferred_element_type=F32)

    def body_whole_k(a_ref, b_ref, o_ref):
        o_ref[...] = product(a_ref, b_ref).astype(o_ref.dtype)

    def body(a_ref, b_ref, o_ref, acc_ref):
        kk = pl.program_id(2)
        part = product(a_ref, b_ref)

        @pl.when(kk == 0)
        def _():
            acc_ref[...] = part

        @pl.when(kk != 0)
        def _():
            acc_ref[...] += part

        @pl.when(kk == nk - 1)
        def _():
            o_ref[...] = acc_ref[...].astype(o_ref.dtype)

    a_spec = (pl.BlockSpec((tk, tm), lambda i, j, kk: (kk, i)) if ta
              else pl.BlockSpec((tm, tk), lambda i, j, kk: (i, kk)))
    b_spec = (pl.BlockSpec((tn, tk), lambda i, j, kk: (j, kk)) if tb
              else pl.BlockSpec((tk, tn), lambda i, j, kk: (kk, j)))
    return pl.pallas_call(
        body_whole_k if nk == 1 else body, name=name, grid=(m // tm, n // tn, nk),
        in_specs=[a_spec, b_spec],
        out_specs=pl.BlockSpec((tm, tn), lambda i, j, kk: (i, j)),
        out_shape=jax.ShapeDtypeStruct((m, n), out_dtype),
        scratch_shapes=[] if nk == 1 else [pltpu.VMEM((tm, tn), F32)],
        compiler_params=_params(("parallel", "parallel", "arbitrary")),
    )(a, b)


def _modulate_rows(x, ctx, mods):
    n, d = x.shape
    c = ctx.shape[0]
    nx = n // ROW_TILE
    assert c == ROW_TILE

    def body(x_ref, ctx_ref, mods_ref, o_ref):
        i = pl.program_id(0)

        @pl.when(i < nx)
        def _():
            o_ref[...] = (x_ref[...] * (1.0 + mods_ref[0:1, :]) + mods_ref[1:2, :]).astype(BF16)

        @pl.when(i >= nx)
        def _():
            o_ref[...] = (ctx_ref[...] * (1.0 + mods_ref[2:3, :]) + mods_ref[3:4, :]).astype(BF16)

    return pl.pallas_call(
        body, name="modulate_rows", grid=(nx + 1,),
        in_specs=[pl.BlockSpec((ROW_TILE, d), lambda i: (jnp.minimum(i, nx - 1), 0)),
                  pl.BlockSpec((ROW_TILE, d), lambda i: (0, 0)),
                  pl.BlockSpec((8, d), lambda i: (0, 0))],
        out_specs=pl.BlockSpec((ROW_TILE, d), lambda i: (i, 0)),
        out_shape=jax.ShapeDtypeStruct((n + c, d), BF16),
        compiler_params=_params(("parallel",)),
    )(x, ctx, mods)


def _rope_tables(n, c):
    rows = n // GRID_W
    row_ids = jnp.repeat(jnp.arange(rows, dtype=F32), GRID_W)
    col_ids = jnp.tile(jnp.arange(GRID_W, dtype=F32), rows)
    axis_dim = HEAD_DIM // 2
    inv_freq = jnp.power(ROPE_THETA, -jnp.arange(0, axis_dim, 2, dtype=F32) / axis_dim)
    ang_r = row_ids[:, None] * inv_freq
    ang_c = col_ids[:, None] * inv_freq
    ang = jnp.concatenate([ang_r, ang_r, ang_c, ang_c], axis=-1)
    cos, sin = jnp.cos(ang), jnp.sin(ang)
    quarter = (jnp.arange(HEAD_DIM) // (HEAD_DIM // 4)) % 2
    sin_a = jnp.where(quarter == 0, -sin, 0.0)
    sin_b = jnp.where(quarter == 1, sin, 0.0)
    pad = lambda t, v: jnp.concatenate([t, jnp.full((c, HEAD_DIM), v, F32)], axis=0)
    return pad(cos, 1.0), pad(sin_a, 0.0), pad(sin_b, 0.0)


def _rope(x, cos, sin_a, sin_b):
    return x * cos + pltpu.roll(x, 96, 1) * sin_a + pltpu.roll(x, 32, 1) * sin_b


def _rope_t(dy, cos, sin_a, sin_b):
    return dy * cos - pltpu.roll(dy, 96, 1) * sin_a - pltpu.roll(dy, 32, 1) * sin_b


def _rms(x):
    r = lax.rsqrt(jnp.mean(x * x, axis=-1, keepdims=True) + EPS)
    return x * r, r


def _qkv_post(h_all, cos, sin_a, sin_b, q_g, k_g):
    t = h_all.shape[0]
    nt = t // ROW_TILE

    def body(h_ref, cos_ref, sa_ref, sb_ref, qg_ref, kg_ref, qa_ref, ka_ref, va_ref, qb_ref, kb_ref, vb_ref):
        cos_, sa, sb = cos_ref[...], sa_ref[...], sb_ref[...]
        sl = lambda off, hh: h_ref[:, off + hh * HEAD_DIM: off + (hh + 1) * HEAD_DIM]
        for hh in range(HEADS_A):
            qa_ref[hh] = (_rope(sl(QA0, hh), cos_, sa, sb) * ATTN_SCALE).astype(BF16)
        for hh in range(KV_A):
            ka_ref[hh] = _rope(sl(KA0, hh), cos_, sa, sb).astype(BF16)
            va_ref[hh] = sl(VA0, hh).astype(BF16)
        for hh in range(HEADS_B):
            xn, _ = _rms(sl(QB0, hh))
            qb_ref[hh] = (_rope(xn * qg_ref[...], cos_, sa, sb) * ATTN_SCALE).astype(BF16)
        for hh in range(KV_B):
            xn, _ = _rms(sl(KB0, hh))
            kb_ref[hh] = _rope(xn * kg_ref[...], cos_, sa, sb).astype(BF16)
            vb_ref[hh] = sl(VB0, hh).astype(BF16)

    tab = pl.BlockSpec((ROW_TILE, HEAD_DIM), lambda i: (i, 0))
    gain = pl.BlockSpec((1, HEAD_DIM), lambda i: (0, 0))
    hs = lambda nh: pl.BlockSpec((nh, ROW_TILE, HEAD_DIM), lambda i: (0, i, 0))
    sh = lambda nh: jax.ShapeDtypeStruct((nh, t, HEAD_DIM), BF16)
    return pl.pallas_call(
        body, name="qkv_post", grid=(nt,),
        in_specs=[pl.BlockSpec((ROW_TILE, IN_WIDTH), lambda i: (i, 0)), tab, tab, tab, gain, gain],
        out_specs=[hs(HEADS_A), hs(KV_A), hs(KV_A), hs(HEADS_B), hs(KV_B), hs(KV_B)],
        out_shape=[sh(HEADS_A), sh(KV_A), sh(KV_A), sh(HEADS_B), sh(KV_B), sh(KV_B)],
        compiler_params=_params(("parallel",)),
    )(h_all, cos, sin_a, sin_b, q_g, k_g)


def _qkv_bwd_post(h_all, cos, sin_a, sin_b, q_g, k_g, dqa, dka, dva, dqb, dkb, dvb, n):
    t = h_all.shape[0]
    nt = t // ROW_TILE
    nx = n // ROW_TILE

    def body(h_ref, cos_ref, sa_ref, sb_ref, qg_ref, kg_ref,
             dqa_ref, dka_ref, dva_ref, dqb_ref, dkb_ref, dvb_ref, dh_ref, gs_ref):
        i = pl.program_id(0)
        cos_, sa, sb = cos_ref[...], sa_ref[...], sb_ref[...]
        latent = (i < nx).astype(F32)
        sl = lambda off, hh: h_ref[:, off + hh * HEAD_DIM: off + (hh + 1) * HEAD_DIM]

        def put(off, hh, val):
            dh_ref[:, off + hh * HEAD_DIM: off + (hh + 1) * HEAD_DIM] = val.astype(BF16)

        def norm_bwd(x, gain, dy):
            xn, r = _rms(x)
            dxh = dy * gain
            dx = r * (dxh - xn * jnp.mean(dxh * xn, axis=-1, keepdims=True))
            return dx, jnp.sum(dy * xn, axis=0, keepdims=True)

        for hh in range(HEADS_A):
            put(QA0, hh, _rope_t(dqa_ref[hh] * (ATTN_SCALE * latent), cos_, sa, sb))
        for hh in range(KV_A):
            put(KA0, hh, _rope_t(dka_ref[hh], cos_, sa, sb))
            put(VA0, hh, dva_ref[hh])
        gq = jnp.zeros((1, HEAD_DIM), F32)
        gk = jnp.zeros((1, HEAD_DIM), F32)
        for hh in range(HEADS_B):
            dq_t = dqb_ref[hh // GROUP, :, (hh % GROUP) * ROW_TILE:(hh % GROUP + 1) * ROW_TILE]
            dy = _rope_t(dq_t.T * (ATTN_SCALE * latent), cos_, sa, sb)
            dx, g = norm_bwd(sl(QB0, hh), qg_ref[...], dy)
            put(QB0, hh, dx)
            gq = gq + g
        for hh in range(KV_B):
            dy = _rope_t(dkb_ref[hh], cos_, sa, sb)
            dx, g = norm_bwd(sl(KB0, hh), kg_ref[...], dy)
            put(KB0, hh, dx)
            gk = gk + g
            put(VB0, hh, dvb_ref[hh])
        upd = jnp.concatenate([gq, gk, jnp.zeros((6, HEAD_DIM), F32)], axis=0)

        @pl.when(i == 0)
        def _():
            gs_ref[...] = upd

        @pl.when(i != 0)
        def _():
            gs_ref[...] += upd

    tab = pl.BlockSpec((ROW_TILE, HEAD_DIM), lambda i: (i, 0))
    gain = pl.BlockSpec((1, HEAD_DIM), lambda i: (0, 0))
    lat = lambda nh: pl.BlockSpec((nh, ROW_TILE, HEAD_DIM), lambda i: (0, jnp.minimum(i, nx - 1), 0))
    full = lambda nh: pl.BlockSpec((nh, ROW_TILE, HEAD_DIM), lambda i: (0, i, 0))
    return pl.pallas_call(
        body, name="qkv_bwd_post", grid=(nt,),
        in_specs=[pl.BlockSpec((ROW_TILE, IN_WIDTH), lambda i: (i, 0)), tab, tab, tab, gain, gain,
                  lat(HEADS_A), full(KV_A), full(KV_A),
                  pl.BlockSpec((KV_B, None, HEAD_DIM, GROUP * ROW_TILE), lambda i: (0, jnp.minimum(i, nx - 1), 0, 0)),
                  full(KV_B), full(KV_B)],
        out_specs=[pl.BlockSpec((ROW_TILE, IN_WIDTH), lambda i: (i, 0)),
                   pl.BlockSpec((8, HEAD_DIM), lambda i: (0, 0))],
        out_shape=[jax.ShapeDtypeStruct((t, IN_WIDTH), BF16), jax.ShapeDtypeStruct((8, HEAD_DIM), F32)],
        compiler_params=_params(("arbitrary",)),
    )(h_all, cos, sin_a, sin_b, q_g, k_g, dqa, dka, dva, dqb, dkb, dvb)


GB_TQ = 256
GB_TK = 256


def _heads_rows(ref2d, tq):
    return jnp.concatenate([ref2d[:, hh * HEAD_DIM:(hh + 1) * HEAD_DIM] for hh in range(GROUP)], axis=0)


def _attn_b_fwd(qb, kb, vb, n):
    t = kb.shape[1]
    nk = t // GB_TK
    tq = GB_TQ
    rows = GROUP * tq

    def body(q_ref, k_ref, v_ref, o_ref, lse_ref, m_s, l_s, acc_s):
        q = q_ref[...].reshape(rows, HEAD_DIM)
        m_s[...] = jnp.full((1, rows), -jnp.inf, F32)
        l_s[...] = jnp.zeros((1, rows), F32)
        acc_s[...] = jnp.zeros((HEAD_DIM, rows), F32)

        def scores(j):
            start = pl.multiple_of(j * GB_TK, GB_TK)
            return lax.dot_general(k_ref[pl.ds(start, GB_TK), :], q, NT, preferred_element_type=F32)

        def step(j, st):
            st_next = scores(jnp.minimum(j + 1, nk - 1))
            vs = v_ref[pl.ds(pl.multiple_of(j * GB_TK, GB_TK), GB_TK), :]
            m_prev = m_s[...]
            m_new = jnp.maximum(m_prev, jnp.max(st, axis=0, keepdims=True))
            p = jnp.exp(st - m_new)
            alpha = jnp.exp(m_prev - m_new)
            l_s[...] = alpha * l_s[...] + jnp.sum(p, axis=0, keepdims=True)
            acc_s[...] = alpha * acc_s[...] + lax.dot_general(vs, p.astype(BF16), TN, preferred_element_type=F32)
            m_s[...] = m_new
            return st_next

        lax.fori_loop(0, nk, step, scores(0))
        ot = acc_s[...] * (1.0 / l_s[...])
        lse_ref[...] = m_s[...] + jnp.log(l_s[...])
        for hh in range(GROUP):
            o_ref[:, hh * HEAD_DIM:(hh + 1) * HEAD_DIM] = ot[:, hh * tq:(hh + 1) * tq].T.astype(BF16)

    return pl.pallas_call(
        body, name="attn_b_fwd", grid=(KV_B, n // tq),
        in_specs=[pl.BlockSpec((GROUP, tq, HEAD_DIM), lambda g, i: (g, i, 0)),
                  pl.BlockSpec((None, t, HEAD_DIM), lambda g, i: (g, 0, 0)),
                  pl.BlockSpec((None, t, HEAD_DIM), lambda g, i: (g, 0, 0))],
        out_specs=[pl.BlockSpec((tq, GROUP * HEAD_DIM), lambda g, i: (i, g)),
                   pl.BlockSpec((None, None, 1, rows), lambda g, i: (g, i, 0, 0))],
        out_shape=[jax.ShapeDtypeStruct((n, HEADS_B * HEAD_DIM), BF16),
                   jax.ShapeDtypeStruct((KV_B, n // tq, 1, rows), F32)],
        scratch_shapes=[pltpu.VMEM((1, rows), F32), pltpu.VMEM((1, rows), F32), pltpu.VMEM((HEAD_DIM, rows), F32)],
        compiler_params=_params(("parallel", "parallel")),
    )(qb, kb, vb)


def _attn_b_bwd(qb, kb, vb, dheads, lse, delta, n):
    t = kb.shape[1]
    nk = t // GB_TK
    tq = GB_TQ
    nq = n // tq
    rows = GROUP * tq

    def body(q_ref, k_ref, v_ref, do_ref, lse_ref, dl_ref, dq_ref, dk_ref, dv_ref):
        j = pl.program_id(1)
        i = pl.program_id(2)

        q = q_ref[...].reshape(rows, HEAD_DIM)
        do = _heads_rows(do_ref, tq)
        ks, vs = k_ref[...], v_ref[...]
        st = lax.dot_general(ks, q, NT, preferred_element_type=F32)
        p = jnp.exp(st - lse_ref[...])
        dpt = lax.dot_general(vs, do, NT, preferred_element_type=F32)
        ds = (p * (dpt - dl_ref[...])).astype(BF16)
        dv_part = lax.dot_general(p.astype(BF16), do, NN, preferred_element_type=F32)
        dk_part = lax.dot_general(ds, q, NN, preferred_element_type=F32)
        dq_part = lax.dot_general(ks, ds, TN, preferred_element_type=F32)

        @pl.when(i == 0)
        def _():
            dk_ref[...] = dk_part
            dv_ref[...] = dv_part

        @pl.when(i != 0)
        def _():
            dk_ref[...] += dk_part
            dv_ref[...] += dv_part

        @pl.when(j == 0)
        def _():
            dq_ref[i] = dq_part

        @pl.when(j != 0)
        def _():
            dq_ref[i] += dq_part

    kv = pl.BlockSpec((None, GB_TK, HEAD_DIM), lambda g, j, i: (g, j, 0))
    row = pl.BlockSpec((None, None, 1, rows), lambda g, j, i: (g, i, 0, 0))
    return pl.pallas_call(
        body, name="attn_b_bwd", grid=(KV_B, nk, nq),
        in_specs=[pl.BlockSpec((GROUP, tq, HEAD_DIM), lambda g, j, i: (g, i, 0)), kv, kv,
                  pl.BlockSpec((tq, GROUP * HEAD_DIM), lambda g, j, i: (i, KV_A + g)), row, row],
        out_specs=[pl.BlockSpec((None, nq, HEAD_DIM, rows), lambda g, j, i: (g, 0, 0, 0)), kv, kv],
        out_shape=[jax.ShapeDtypeStruct((KV_B, nq, HEAD_DIM, rows), F32),
                   jax.ShapeDtypeStruct((KV_B, t, HEAD_DIM), F32),
                   jax.ShapeDtypeStruct((KV_B, t, HEAD_DIM), F32)],
        compiler_params=_params(("parallel", "arbitrary", "arbitrary")),
    )(qb, kb, vb, dheads, lse, delta)


def _delta_rows(dheads, heads):
    n = heads.shape[0]
    tq = GB_TQ
    w = GROUP * HEAD_DIM

    def body(a_ref, b_ref, o_ref):
        prod = a_ref[...].astype(F32) * b_ref[...].astype(F32)
        cols = [jnp.sum(prod[:, hh * HEAD_DIM:(hh + 1) * HEAD_DIM].T, axis=0, keepdims=True) for hh in range(GROUP)]
        o_ref[...] = jnp.concatenate(cols, axis=1)

    blk = pl.BlockSpec((tq, w), lambda g, i: (i, KV_A + g))
    return pl.pallas_call(
        body, name="delta_rows", grid=(KV_B, n // tq),
        in_specs=[blk, blk],
        out_specs=pl.BlockSpec((None, None, 1, GROUP * tq), lambda g, i: (g, i, 0, 0)),
        out_shape=jax.ShapeDtypeStruct((KV_B, n // tq, 1, GROUP * tq), F32),
        compiler_params=_params(("parallel", "parallel")),
    )(dheads, heads)


KWIN = 3 * BLOCK


def _window_scores(q, k_ref, j, n, nb, sink_row):
    c = k_ref.shape[0] - n
    start = pl.multiple_of(jnp.clip(j - 1, 0, nb - 3) * BLOCK, BLOCK)
    kw = k_ref[pl.ds(start, KWIN), :]
    kc = k_ref[pl.ds(n, c), :]
    s_loc = lax.dot_general(kw, q, NT, preferred_element_type=F32)
    s_ctx = lax.dot_general(kc, q, NT, preferred_element_type=F32)
    cols = GROUP * BLOCK
    qpos = j * BLOCK + lax.broadcasted_iota(jnp.int32, (KWIN, cols), 1) % BLOCK
    kpos = start + lax.broadcasted_iota(jnp.int32, (KWIN, cols), 0)
    s_loc = jnp.where(jnp.abs(qpos - kpos) <= WINDOW, s_loc, -jnp.inf)
    m = jnp.maximum(jnp.maximum(jnp.max(s_loc, axis=0, keepdims=True), jnp.max(s_ctx, axis=0, keepdims=True)),
                    sink_row)
    e_loc, e_ctx, e_sink = jnp.exp(s_loc - m), jnp.exp(s_ctx - m), jnp.exp(sink_row - m)
    inv = 1.0 / (jnp.sum(e_loc, axis=0, keepdims=True) + jnp.sum(e_ctx, axis=0, keepdims=True) + e_sink)
    return e_loc * inv, e_ctx * inv, e_sink * inv, start


def _sink_row(sink_ref, g):
    return jnp.concatenate([sink_ref[pl.ds(g * GROUP + hh, 1), :] for hh in range(GROUP)], axis=1)


def _attn_a_fwd(qa, ka, va, sink_b, n):
    t = ka.shape[1]
    nb = n // BLOCK
    assert nb >= 3

    def body(q_ref, k_ref, v_ref, sink_ref, o_ref):
        g, j = pl.program_id(0), pl.program_id(1)
        q = q_ref[...].reshape(GROUP * BLOCK, HEAD_DIM)
        p_loc, p_ctx, _, start = _window_scores(q, k_ref, j, n, nb, _sink_row(sink_ref, g))
        vw = v_ref[pl.ds(start, KWIN), :]
        vc = v_ref[pl.ds(n, t - n), :]
        ot = (lax.dot_general(vw, p_loc.astype(BF16), TN, preferred_element_type=F32)
              + lax.dot_general(vc, p_ctx.astype(BF16), TN, preferred_element_type=F32))
        for hh in range(GROUP):
            o_ref[:, hh * HEAD_DIM:(hh + 1) * HEAD_DIM] = ot[:, hh * BLOCK:(hh + 1) * BLOCK].T.astype(BF16)

    return pl.pallas_call(
        body, name="attn_a_fwd", grid=(KV_A, nb),
        in_specs=[pl.BlockSpec((GROUP, BLOCK, HEAD_DIM), lambda g, j: (g, j, 0)),
                  pl.BlockSpec((None, t, HEAD_DIM), lambda g, j: (g, 0, 0)),
                  pl.BlockSpec((None, t, HEAD_DIM), lambda g, j: (g, 0, 0)),
                  pl.BlockSpec((HEADS_A, HEAD_DIM), lambda g, j: (0, 0))],
        out_specs=pl.BlockSpec((BLOCK, GROUP * HEAD_DIM), lambda g, j: (j, g)),
        out_shape=jax.ShapeDtypeStruct((n, HEADS_A * HEAD_DIM), BF16),
        compiler_params=_params(("parallel", "parallel")),
    )(qa, ka, va, sink_b)


def _attn_a_bwd(qa, ka, va, sink_b, dheads, n):
    t = ka.shape[1]
    c = t - n
    nb = n // BLOCK

    def body(q_ref, k_ref, v_ref, sink_ref, do_ref, dq_ref, dk_ref, dv_ref, dsink_ref):
        g, j = pl.program_id(0), pl.program_id(1)

        @pl.when(j == 0)
        def _():
            dk_ref[...] = jnp.zeros_like(dk_ref)
            dv_ref[...] = jnp.zeros_like(dv_ref)
            dsink_ref[...] = jnp.zeros_like(dsink_ref)

        q = q_ref[...].reshape(GROUP * BLOCK, HEAD_DIM)
        do = _heads_rows(do_ref, BLOCK)
        p_loc, p_ctx, p_sink, start = _window_scores(q, k_ref, j, n, nb, _sink_row(sink_ref, g))
        kw, vw = k_ref[pl.ds(start, KWIN), :], v_ref[pl.ds(start, KWIN), :]
        kc, vc = k_ref[pl.ds(n, c), :], v_ref[pl.ds(n, c), :]
        dp_loc = lax.dot_general(vw, do, NT, preferred_element_type=F32)
        dp_ctx = lax.dot_general(vc, do, NT, preferred_element_type=F32)
        dl = jnp.sum(p_loc * dp_loc, axis=0, keepdims=True) + jnp.sum(p_ctx * dp_ctx, axis=0, keepdims=True)
        ds_loc = (p_loc * (dp_loc - dl)).astype(BF16)
        ds_ctx = (p_ctx * (dp_ctx - dl)).astype(BF16)
        dqt = (lax.dot_general(kw, ds_loc, TN, preferred_element_type=F32)
               + lax.dot_general(kc, ds_ctx, TN, preferred_element_type=F32))
        for hh in range(GROUP):
            dq_ref[hh] = dqt[:, hh * BLOCK:(hh + 1) * BLOCK].T
        dk_ref[pl.ds(start, KWIN), :] += lax.dot_general(ds_loc, q, NN, preferred_element_type=F32)
        dv_ref[pl.ds(start, KWIN), :] += lax.dot_general(p_loc.astype(BF16), do, NN, preferred_element_type=F32)
        dk_ref[pl.ds(n, c), :] += lax.dot_general(ds_ctx, q, NN, preferred_element_type=F32)
        dv_ref[pl.ds(n, c), :] += lax.dot_general(p_ctx.astype(BF16), do, NN, preferred_element_type=F32)
        dsk = -(p_sink * dl)
        upd = [jnp.broadcast_to(jnp.sum(dsk[:, hh * BLOCK:(hh + 1) * BLOCK], axis=1, keepdims=True), (1, HEAD_DIM))
               for hh in range(GROUP)]
        dsink_ref[...] += jnp.concatenate(upd + [jnp.zeros((8 - GROUP, HEAD_DIM), F32)], axis=0)

    res = pl.BlockSpec((None, t, HEAD_DIM), lambda g, j: (g, 0, 0))
    return pl.pallas_call(
        body, name="attn_a_bwd", grid=(KV_A, nb),
        in_specs=[pl.BlockSpec((GROUP, BLOCK, HEAD_DIM), lambda g, j: (g, j, 0)), res, res,
                  pl.BlockSpec((HEADS_A, HEAD_DIM), lambda g, j: (0, 0)),
                  pl.BlockSpec((BLOCK, GROUP * HEAD_DIM), lambda g, j: (j, g))],
        out_specs=[pl.BlockSpec((GROUP, BLOCK, HEAD_DIM), lambda g, j: (g, j, 0)), res, res,
                   pl.BlockSpec((None, 8, HEAD_DIM), lambda g, j: (g, 0, 0))],
        out_shape=[jax.ShapeDtypeStruct((HEADS_A, n, HEAD_DIM), F32),
                   jax.ShapeDtypeStruct((KV_A, t, HEAD_DIM), F32),
                   jax.ShapeDtypeStruct((KV_A, t, HEAD_DIM), F32),
                   jax.ShapeDtypeStruct((KV_A, 8, HEAD_DIM), F32)],
        compiler_params=_params(("parallel", "arbitrary")),
    )(qa, ka, va, sink_b, dheads)


def _ln_stats(r):
    mu = jnp.mean(r, axis=-1, keepdims=True)
    cen = r - mu
    rstd = lax.rsqrt(jnp.mean(cen * cen, axis=-1, keepdims=True) + EPS)
    return cen * rstd, rstd


def _ln_bwd(dy, xhat, rstd, gain):
    dxh = dy * gain
    return rstd * (dxh - jnp.mean(dxh, axis=-1, keepdims=True)
                   - xhat * jnp.mean(dxh * xhat, axis=-1, keepdims=True))


def _accumulate_rows(ref, rows, i):
    pad = [jnp.zeros_like(rows[0])] * (8 - len(rows))
    upd = jnp.concatenate(rows + pad, axis=0)

    @pl.when(i == 0)
    def _():
        ref[...] = upd

    @pl.when(i != 0)
    def _():
        ref[...] += upd


def _colsum(v):
    return jnp.sum(v, axis=0, keepdims=True)


LN_TILE = 128


def _res_ln1(x, a, vec):
    n, d = x.shape

    def body(x_ref, a_ref, v_ref, xh_ref, rs_ref, u_ref):
        r1 = DN_ALPHA * x_ref[...] + v_ref[0:1, :] * a_ref[...]
        xhat, rstd = _ln_stats(r1)
        xh_ref[...] = xhat
        rs_ref[...] = rstd
        x1 = xhat * v_ref[1:2, :] + v_ref[2:3, :]
        u_ref[...] = (x1 * (1.0 + v_ref[3:4, :]) + v_ref[4:5, :]).astype(BF16)

    row = pl.BlockSpec((LN_TILE, d), lambda i: (i, 0))
    return pl.pallas_call(
        body, name="res_ln1", grid=(n // LN_TILE,),
        in_specs=[row, row, pl.BlockSpec((8, d), lambda i: (0, 0))],
        out_specs=[row, pl.BlockSpec((LN_TILE, 1), lambda i: (i, 0)), row],
        out_shape=[jax.ShapeDtypeStruct((n, d), F32), jax.ShapeDtypeStruct((n, 1), F32),
                   jax.ShapeDtypeStruct((n, d), BF16)],
        compiler_params=_params(("parallel",)),
    )(x, a, vec)


def _res_ln2_loss(xhat1, f, target, vec):
    n, d = f.shape

    def body(xh_ref, f_ref, t_ref, v_ref, dr_ref, df_ref, s_ref):
        i = pl.program_id(0)
        x1 = xh_ref[...] * v_ref[1:2, :] + v_ref[2:3, :]
        fv = f_ref[...]
        xhat, rstd = _ln_stats(DN_ALPHA * x1 + v_ref[0:1, :] * fv)
        err = xhat * v_ref[3:4, :] + v_ref[4:5, :] - t_ref[...]
        dy = err * (1.0 / d)
        dr2 = _ln_bwd(dy, xhat, rstd, v_ref[3:4, :])
        dr_ref[...] = dr2
        df_ref[...] = (dr2 * v_ref[0:1, :]).astype(BF16)
        _accumulate_rows(s_ref, [_colsum(dy * xhat), _colsum(dy), _colsum(dr2 * fv),
                                 _colsum(err * err) * (0.5 / d)], i)

    row = pl.BlockSpec((LN_TILE, d), lambda i: (i, 0))
    return pl.pallas_call(
        body, name="res_ln2_loss", grid=(n // LN_TILE,),
        in_specs=[row, row, row, pl.BlockSpec((8, d), lambda i: (0, 0))],
        out_specs=[row, row, pl.BlockSpec((8, d), lambda i: (0, 0))],
        out_shape=[jax.ShapeDtypeStruct((n, d), F32), jax.ShapeDtypeStruct((n, d), BF16),
                   jax.ShapeDtypeStruct((8, d), F32)],
        compiler_params=_params(("arbitrary",)),
    )(xhat1, f, target, vec)


def _ln1_bwd(du2, dr2, xhat1, rstd1, a, vec):
    n, d = du2.shape

    def body(du_ref, dr2_ref, xh_ref, rs_ref, a_ref, v_ref, dxp_ref, da_ref, s_ref):
        i = pl.program_id(0)
        du, xhat = du_ref[...], xh_ref[...]
        x1 = xhat * v_ref[1:2, :] + v_ref[2:3, :]
        dx1 = DN_ALPHA * dr2_ref[...] + du * (1.0 + v_ref[0:1, :])
        dr1 = _ln_bwd(dx1, xhat, rs_ref[...], v_ref[1:2, :])
        dxp_ref[...] = DN_ALPHA * dr1
        da_ref[...] = (dr1 * v_ref[3:4, :]).astype(BF16)
        _accumulate_rows(s_ref, [_colsum(du * x1), _colsum(du), _colsum(dx1 * xhat), _colsum(dx1),
                                 _colsum(dr1 * a_ref[...])], i)

    row = pl.BlockSpec((LN_TILE, d), lambda i: (i, 0))
    return pl.pallas_call(
        body, name="ln1_bwd", grid=(n // LN_TILE,),
        in_specs=[row, row, row, pl.BlockSpec((LN_TILE, 1), lambda i: (i, 0)), row,
                  pl.BlockSpec((8, d), lambda i: (0, 0))],
        out_specs=[row, row, pl.BlockSpec((8, d), lambda i: (0, 0))],
        out_shape=[jax.ShapeDtypeStruct((n, d), F32), jax.ShapeDtypeStruct((n, d), BF16),
                   jax.ShapeDtypeStruct((8, d), F32)],
        compiler_params=_params(("arbitrary",)),
    )(du2, dr2, xhat1, rstd1, a, vec)


def _mod1_bwd(du_all, dxp, x, ctx, mods):
    n, d = x.shape
    nx = n // ROW_TILE

    def body(du_ref, dxp_ref, x_ref, ctx_ref, m_ref, gx_ref, s_ref):
        i = pl.program_id(0)
        du = du_ref[...]
        zero = jnp.zeros((1, d), F32)

        @pl.when(i == 0)
        def _():
            s_ref[...] = jnp.zeros_like(s_ref)

        @pl.when(i < nx)
        def _():
            gx_ref[...] = dxp_ref[...] + du * (1.0 + m_ref[0:1, :])
            s_ref[...] += jnp.concatenate([_colsum(du * x_ref[...]), _colsum(du)] + [zero] * 6, axis=0)

        @pl.when(i >= nx)
        def _():
            s_ref[...] += jnp.concatenate([zero, zero, _colsum(du * ctx_ref[...]), _colsum(du)] + [zero] * 4, axis=0)

    lat = pl.BlockSpec((ROW_TILE, d), lambda i: (jnp.minimum(i, nx - 1), 0))
    return pl.pallas_call(
        body, name="mod1_bwd", grid=(nx + 1,),
        in_specs=[pl.BlockSpec((ROW_TILE, d), lambda i: (i, 0)), lat, lat,
                  pl.BlockSpec((ROW_TILE, d), lambda i: (0, 0)), pl.BlockSpec((8, d), lambda i: (0, 0))],
        out_specs=[lat, pl.BlockSpec((8, d), lambda i: (0, 0))],
        out_shape=[jax.ShapeDtypeStruct((n, d), F32), jax.ShapeDtypeStruct((8, d), F32)],
        compiler_params=_params(("arbitrary",)),
    )(du_all, dxp, x, ctx, mods)


FFN_TM = 512
FFN_TN = 512


def _gate_up(u2, wg, wu):
    n, d = u2.shape
    f = wg.shape[1]

    def body(u_ref, wg_ref, wu_ref, g_ref, up_ref, h_ref):
        u = u_ref[...]
        g = lax.dot_general(u, wg_ref[...], NN, preferred_element_type=F32)
        up = lax.dot_general(u, wu_ref[...], NN, preferred_element_type=F32)
        g_ref[...] = g
        up_ref[...] = up
        h_ref[...] = (g * jax.nn.sigmoid(g) * up).astype(BF16)

    wspec = pl.BlockSpec((d, FFN_TN), lambda j, i: (0, j))
    ospec = pl.BlockSpec((FFN_TM, FFN_TN), lambda j, i: (i, j))
    return pl.pallas_call(
        body, name="gate_up", grid=(f // FFN_TN, n // FFN_TM),
        in_specs=[pl.BlockSpec((FFN_TM, d), lambda j, i: (i, 0)), wspec, wspec],
        out_specs=[ospec, ospec, ospec],
        out_shape=[jax.ShapeDtypeStruct((n, f), F32), jax.ShapeDtypeStruct((n, f), F32),
                   jax.ShapeDtypeStruct((n, f), BF16)],
        compiler_params=_params(("parallel", "parallel")),
    )(u2, wg, wu)


def _glu_bwd(df, wd, g, u):
    n, d = df.shape
    f = wd.shape[0]

    def body(df_ref, wd_ref, g_ref, u_ref, dg_ref, du_ref):
        dh = lax.dot_general(df_ref[...], wd_ref[...], NT, preferred_element_type=F32)
        gv = g_ref[...]
        sig = jax.nn.sigmoid(gv)
        du_ref[...] = (dh * (gv * sig)).astype(BF16)
        dg_ref[...] = (dh * u_ref[...] * (sig * (1.0 + gv * (1.0 - sig)))).astype(BF16)

    tm = min(2 * FFN_TM, n)
    ospec = pl.BlockSpec((tm, FFN_TN), lambda i, j: (i, j))
    return pl.pallas_call(
        body, name="glu_bwd", grid=(n // tm, f // FFN_TN),
        in_specs=[pl.BlockSpec((tm, d), lambda i, j: (i, 0)),
                  pl.BlockSpec((FFN_TN, d), lambda i, j: (j, 0)), ospec, ospec],
        out_specs=[ospec, ospec],
        out_shape=[jax.ShapeDtypeStruct((n, f), BF16), jax.ShapeDtypeStruct((n, f), BF16)],
        compiler_params=_params(("parallel", "parallel")),
    )(df, wd, g, u)


def _du2(dg, du, wg, wu):
    n, f = dg.shape
    d = wg.shape[0]
    tm, tn, tk = min(1024, n), 1024, 1408
    nk = f // tk

    def body(dg_ref, du_ref, wg_ref, wu_ref, o_ref, acc_ref):
        kk = pl.program_id(2)
        part = (lax.dot_general(dg_ref[...], wg_ref[...], NT, preferred_element_type=F32)
                + lax.dot_general(du_ref[...], wu_ref[...], NT, preferred_element_type=F32))

        @pl.when(kk == 0)
        def _():
            acc_ref[...] = part

        @pl.when(kk != 0)
        def _():
            acc_ref[...] += part

        @pl.when(kk == nk - 1)
        def _():
            o_ref[...] = acc_ref[...]

    aspec = pl.BlockSpec((tm, tk), lambda i, j, kk: (i, kk))
    wspec = pl.BlockSpec((tn, tk), lambda i, j, kk: (j, kk))
    return pl.pallas_call(
        body, name="du2", grid=(n // tm, d // tn, nk),
        in_specs=[aspec, aspec, wspec, wspec],
        out_specs=pl.BlockSpec((tm, tn), lambda i, j, kk: (i, j)),
        out_shape=jax.ShapeDtypeStruct((n, d), F32),
        scratch_shapes=[pltpu.VMEM((tm, tn), F32)],
        compiler_params=_params(("parallel", "parallel", "arbitrary")),
    )(dg, du, wg, wu)


def _rows8(rows, d=D_MODEL):
    rows = [r.reshape(1, d).astype(F32) for r in rows]
    return jnp.concatenate(rows + [jnp.zeros((8 - len(rows), d), F32)], axis=0)


W_GROUPS = (("w_in",), ("w_out", "w_gate", "w_up"), ("w_down",))
G_GROUPS = (("w_down", "w_gate", "w_up"), ("w_out",), ("w_in",))


def _layer_fwd_bwd(x, ctx, target, mod, mod_ctx, weights, grads_out, q_g, k_g, sink, ln1_g, ln1_b, ln2_g, ln2_b):
    n, d = x.shape
    c = ctx.shape[0]
    sh1, sc1, g1, sh2, sc2, g2 = [mod[:, k * d:(k + 1) * d] for k in range(6)]
    csh1, csc1 = mod_ctx[:, 0:d], mod_ctx[:, d:2 * d]
    cos, sin_a, sin_b = _rope_tables(n, c)
    sink_b = jnp.broadcast_to(sink.reshape(HEADS_A, 1), (HEADS_A, HEAD_DIM)).astype(F32)

    u_all = _modulate_rows(x, ctx, _rows8([sc1, sh1, csc1, csh1]))
    (w_in,) = weights(0, u_all)
    h_all = _matmul(u_all, w_in, name="qkv_proj", tm=_fit(n + c, 1088), tn=1024, tk=2048, out_dtype=F32)
    qa, ka, va, qb, kb, vb = _qkv_post(h_all, cos, sin_a, sin_b, q_g, k_g)
    o_a = _attn_a_fwd(qa, ka, va, sink_b, n)
    o_b, lse = _attn_b_fwd(qb, kb, vb, n)
    heads = jnp.concatenate([o_a, o_b], axis=1)
    w_out, w_gate, w_up = weights(1, heads)
    a = _matmul(heads, w_out, name="out_proj", tm=1024, tn=1024, tk=2048, out_dtype=F32)
    xhat1, rstd1, u2 = _res_ln1(x, a, _rows8([g1, ln1_g, ln1_b, sc2, sh2]))
    gg, uu, hh = _gate_up(u2, w_gate, w_up)
    (w_down,) = weights(2, hh)
    f = _matmul(hh, w_down, name="ffn_down", tm=1024, tn=1024, tk=1408, out_dtype=F32)
    dr2, df, s_ln2 = _res_ln2_loss(xhat1, f, target, _rows8([g2, ln1_g, ln1_b, ln2_g, ln2_b]))

    dgg, duu = _glu_bwd(df, w_down, gg, uu)
    dw_down = _matmul(hh, df, name="dw_down", ta=True, tm=512, tn=1024, tk=n, out_dtype=BF16)
    dw_gate = _matmul(u2, dgg, name="dw_gate", ta=True, tm=1024, tn=512, tk=n, out_dtype=BF16)
    dw_up = _matmul(u2, duu, name="dw_up", ta=True, tm=1024, tn=512, tk=n, out_dtype=BF16)
    zero = grads_out(0, [dw_down, dw_gate, dw_up])
    du2 = _du2(dgg, duu, w_gate, w_up)
    dxp, da, s_ln1 = _ln1_bwd(du2, dr2, xhat1, rstd1, a, _rows8([sc2, ln1_g, ln1_b, g1]) + zero)

    dheads = _matmul(da, w_out, name="d_heads", tb=True, tm=1024, tn=1024, tk=2048, out_dtype=BF16)
    dw_out = _matmul(heads, da, name="dw_out", ta=True, tm=1024, tn=1024, tk=n, out_dtype=BF16)
    zero = grads_out(1, [dw_out])
    delta = _delta_rows(dheads, heads)
    dqa, dka, dva, dsink = _attn_a_bwd(qa, ka, va, sink_b + zero, dheads, n)
    dqb, dkb, dvb = _attn_b_bwd(qb, kb, vb, dheads, lse, delta, n)
    dh_all, s_gain = _qkv_bwd_post(h_all, cos, sin_a, sin_b, q_g, k_g, dqa, dka, dva, dqb, dkb, dvb, n)
    dw_in = _matmul(u_all, dh_all, name="dw_in", ta=True, tm=1024, tn=1024, tk=n + c, out_dtype=BF16)
    zero = grads_out(2, [dw_in])
    du_all = _matmul(dh_all, w_in, name="d_u1", tb=True, tm=_fit(n + c, 1088), tn=1024, tk=IN_WIDTH, out_dtype=F32)
    grad_x, s_mod1 = _mod1_bwd(du_all, dxp, x, ctx, _rows8([sc1]) + zero)

    dsink_row = jnp.concatenate([dsink[0, 0:GROUP, 0], dsink[1, 0:GROUP, 0]]).reshape(1, HEADS_A)
    misc = jnp.concatenate([s_gain[0:1], s_gain[1:2], dsink_row,
                            jnp.zeros((1, d - 2 * HEAD_DIM - HEADS_A), F32)], axis=1)
    partial = jnp.concatenate([
        s_mod1[1:2], s_mod1[0:1], s_ln1[4:5],
        s_ln1[1:2], s_ln1[0:1], s_ln2[2:3],
        s_mod1[3:4], s_mod1[2:3],
        s_ln1[2:3], s_ln1[3:4], s_ln2[0:1], s_ln2[1:2],
        s_ln2[3:4], misc, jnp.zeros((2, d), F32)], axis=0)
    return grad_x, partial


ANY = pl.BlockSpec(memory_space=pl.ANY)
VMEM_FULL = pl.BlockSpec(memory_space=pltpu.VMEM)
N_CHIP_PEERS = 3


def _me():
    return lax.axis_index("x"), lax.axis_index("y"), lax.axis_index("c")


def _other_chips(x, y):
    return [(1 - x, y), (x, 1 - y), (1 - x, 1 - y)]


def _shard_of(chip):
    return 2 * chip[0] + chip[1]


def _dev_index(x, y, c):
    return 4 * x + 2 * y + c


def _rcopy(src, dst, send_sems, recv_sems, k, dev):
    return pltpu.make_async_remote_copy(src_ref=src, dst_ref=dst, send_sem=send_sems.at[k], recv_sem=recv_sems.at[k],
                                        device_id=dev, device_id_type=MESH)


BIG = (("w_in", (D_MODEL, IN_WIDTH), 1), ("w_out", (MIX_WIDTH, D_MODEL), 0), ("w_gate", (D_MODEL, FFN), 1),
       ("w_up", (D_MODEL, FFN), 1), ("w_down", (FFN, D_MODEL), 0))


def _sub(ref, axis, idx, size):
    start = pl.multiple_of(idx * size, size)
    return ref.at[pl.ds(start, size), :] if axis == 0 else ref.at[:, pl.ds(start, size)]


def _shape_div(shape, axis, parts):
    return tuple(s // parts if a == axis else s for a, s in enumerate(shape))


def _piece(a, ref, shard, half):
    _, full, axis = BIG[a]
    view = _sub(ref, axis, shard, full[axis] // N_SHARD)
    return _sub(view, 1 - axis, half, full[1 - axis] // 2)


HBM = pl.BlockSpec(memory_space=pltpu.HBM)
SEM = pl.BlockSpec(memory_space=pltpu.SEMAPHORE)
EFFECT = pltpu.SideEffectType.DATAFLOW_SIDE_EFFECTING
BIG_INDEX = {name: a for a, (name, _, _) in enumerate(BIG)}


def _in_hbm(arr):
    return pltpu.with_memory_space_constraint(arr, pltpu.HBM)


def _gather_start(tag, arrs, bufs, prev):
    n_arr = len(arrs)

    def body(*refs):
        ins = refs[:n_arr]
        send_sems, recv_sems = refs[n_arr + 1], refs[n_arr + 2]
        token = refs[-1]
        x, y, c = _me()
        s_me = _shard_of((x, y))
        for i, a in enumerate(arrs):
            mine = _piece(a, ins[i], s_me, c)
            for j, chip in enumerate(_other_chips(x, y)):
                _rcopy(mine, mine, send_sems, recv_sems, N_CHIP_PEERS * i + j, (*chip, c)).start()
        token[...] = jnp.zeros_like(token)

    n_sem = N_CHIP_PEERS * n_arr
    outs = pl.pallas_call(
        body, name="gather_start_" + tag,
        in_specs=[HBM] * n_arr + [ANY],
        out_specs=[SEM, SEM] + [HBM] * n_arr + [VMEM_FULL],
        out_shape=[pltpu.SemaphoreType.DMA((n_sem,)), pltpu.SemaphoreType.DMA((n_sem,))]
        + [pltpu.HBM(BIG[a][1], BF16) for a in arrs] + [jax.ShapeDtypeStruct((8, HEAD_DIM), F32)],
        input_output_aliases={i: 2 + i for i in range(n_arr)},
        compiler_params=pltpu.CompilerParams(has_side_effects=EFFECT),
    )(*[_in_hbm(b) for b in bufs], prev)
    return outs[0], outs[1], list(outs[2:2 + n_arr]), outs[-1]


def _gather_wait(tag, arrs, send_sems, recv_sems, bufs, after):
    n_arr = len(arrs)

    def body(*refs):
        ins = refs[:n_arr]
        send_sems_, recv_sems_ = refs[n_arr], refs[n_arr + 1]
        x, y, c = _me()
        s_me = _shard_of((x, y))
        for i, a in enumerate(arrs):
            mine = _piece(a, ins[i], s_me, c)
            for j, chip in enumerate(_other_chips(x, y)):
                landed = _piece(a, ins[i], _shard_of(chip), c)
                cp = _rcopy(mine, landed, send_sems_, recv_sems_, N_CHIP_PEERS * i + j, (*chip, c))
                cp.wait_send()
                cp.wait_recv()

    outs = pl.pallas_call(
        body, name="gather_wait_" + tag,
        in_specs=[HBM] * n_arr + [SEM, SEM, ANY],
        out_specs=[HBM] * n_arr,
        out_shape=[pltpu.HBM(BIG[a][1], BF16) for a in arrs],
        input_output_aliases={i: i for i in range(n_arr)},
        compiler_params=pltpu.CompilerParams(has_side_effects=EFFECT),
    )(*bufs, send_sems, recv_sems, after)
    return list(outs)


def _gather_forward(tag, arrs, bufs):
    n_arr = len(arrs)

    def body(*refs):
        outs = refs[n_arr:2 * n_arr]
        send_sems, recv_sems = refs[2 * n_arr:]
        x, y, c = _me()
        sibling = (x, y, 1 - c)
        chips = _other_chips(x, y)
        copies = []
        for i, a in enumerate(arrs):
            for j, chip in enumerate(chips):
                landed = _piece(a, outs[i], _shard_of(chip), c)
                cp = _rcopy(landed, landed, send_sems, recv_sems, N_CHIP_PEERS * i + j, sibling)
                cp.start()
                copies.append(cp)
        for i, a in enumerate(arrs):
            for j, chip in enumerate(chips):
                other = _piece(a, outs[i], _shard_of(chip), 1 - c)
                _rcopy(other, other, send_sems, recv_sems, N_CHIP_PEERS * i + j, sibling).wait_recv()
        for cp in copies:
            cp.wait_send()

    n_sem = N_CHIP_PEERS * n_arr
    return list(pl.pallas_call(
        body, name="gather_forward_" + tag,
        in_specs=[ANY] * n_arr, out_specs=[ANY] * n_arr,
        out_shape=[jax.ShapeDtypeStruct(BIG[a][1], BF16) for a in arrs],
        input_output_aliases={i: i for i in range(n_arr)},
        scratch_shapes=[pltpu.SemaphoreType.DMA((n_sem,)), pltpu.SemaphoreType.DMA((n_sem,))],
    )(*bufs))


def _exchange_halves(tag, arrs, dws):
    n_arr = len(arrs)

    def body(*refs):
        ins, outs = refs[:n_arr], refs[n_arr:2 * n_arr]
        send_sems, recv_sems = refs[2 * n_arr:]
        x, y, c = _me()
        copies = []
        for i, a in enumerate(arrs):
            _, full, axis = BIG[a]
            src = _sub(ins[i], 1 - axis, 1 - c, full[1 - axis] // 2)
            cp = _rcopy(src, outs[i], send_sems, recv_sems, i, (x, y, 1 - c))
            cp.start()
            copies.append(cp)
        for cp in copies:
            cp.wait()

    return list(pl.pallas_call(
        body, name="grad_exchange_halves_" + tag,
        in_specs=[ANY] * n_arr, out_specs=[ANY] * n_arr,
        out_shape=[jax.ShapeDtypeStruct(_shape_div(BIG[a][1], 1 - BIG[a][2], 2), BF16) for a in arrs],
        scratch_shapes=[pltpu.SemaphoreType.DMA((n_arr,)), pltpu.SemaphoreType.DMA((n_arr,))],
    )(*dws))


def _received_shape(a):
    _, full, axis = BIG[a]
    return (N_CHIP_PEERS,) + _shape_div(_shape_div(full, 1 - axis, 2), axis, N_SHARD)


def _pieces_start(tag, arrs, halves):
    n_arr = len(arrs)

    def body(*refs):
        srcs, lands = refs[:n_arr], refs[n_arr:2 * n_arr]
        send_sems, recv_sems = refs[2 * n_arr], refs[2 * n_arr + 1]
        token = refs[-1]
        x, y, c = _me()
        for i, a in enumerate(arrs):
            _, full, axis = BIG[a]
            for j, chip in enumerate(_other_chips(x, y)):
                src = _sub(srcs[i], axis, _shard_of(chip), full[axis] // N_SHARD)
                _rcopy(src, lands[i].at[j], send_sems, recv_sems, N_CHIP_PEERS * i + j, (*chip, c)).start()
        token[...] = jnp.zeros_like(token)

    n_sem = N_CHIP_PEERS * n_arr
    lands = [_in_hbm(lax.empty(_received_shape(a), BF16)) for a in arrs]
    outs = pl.pallas_call(
        body, name="grad_pieces_start_" + tag,
        in_specs=[HBM] * (2 * n_arr),
        out_specs=[SEM, SEM] + [HBM] * (2 * n_arr) + [VMEM_FULL],
        out_shape=[pltpu.SemaphoreType.DMA((n_sem,)), pltpu.SemaphoreType.DMA((n_sem,))]
        + [pltpu.HBM(h.shape, BF16) for h in halves] + [pltpu.HBM(_received_shape(a), BF16) for a in arrs]
        + [jax.ShapeDtypeStruct((8, HEAD_DIM), F32)],
        input_output_aliases={i: 2 + i for i in range(2 * n_arr)},
        compiler_params=pltpu.CompilerParams(has_side_effects=EFFECT),
    )(*[_in_hbm(h) for h in halves], *lands)
    return outs[0], outs[1], list(outs[2:2 + n_arr]), list(outs[2 + n_arr:2 + 2 * n_arr]), outs[-1]


def _pieces_wait(tag, arrs, send_sems, recv_sems, halves, lands, after):
    n_arr = len(arrs)

    def body(*refs):
        srcs, lands_ = refs[:n_arr], refs[n_arr:2 * n_arr]
        send_sems_, recv_sems_ = refs[2 * n_arr], refs[2 * n_arr + 1]
        x, y, c = _me()
        for i, a in enumerate(arrs):
            _, full, axis = BIG[a]
            for j, chip in enumerate(_other_chips(x, y)):
                src = _sub(srcs[i], axis, _shard_of(chip), full[axis] // N_SHARD)
                cp = _rcopy(src, lands_[i].at[j], send_sems_, recv_sems_, N_CHIP_PEERS * i + j, (*chip, c))
                cp.wait_send()
                cp.wait_recv()

    outs = pl.pallas_call(
        body, name="grad_pieces_wait_" + tag,
        in_specs=[HBM] * (2 * n_arr) + [SEM, SEM, ANY],
        out_specs=[HBM] * (2 * n_arr),
        out_shape=[pltpu.HBM(h.shape, BF16) for h in halves] + [pltpu.HBM(_received_shape(a), BF16) for a in arrs],
        input_output_aliases={i: i for i in range(2 * n_arr)},
        compiler_params=pltpu.CompilerParams(has_side_effects=EFFECT),
    )(*halves, *lands, send_sems, recv_sems, after)
    return list(outs[:n_arr]), list(outs[n_arr:])


def _join_halves(tag, g_halves):
    n_arr = len(g_halves)

    def body(*refs):
        ins, outs = refs[:n_arr], refs[n_arr:2 * n_arr]
        send_sems, recv_sems = refs[2 * n_arr:]
        x, y, c = _me()
        copies = []
        for i in range(n_arr):
            cp = _rcopy(ins[i], outs[i], send_sems, recv_sems, i, (x, y, 1 - c))
            cp.start()
            copies.append(cp)
        for cp in copies:
            cp.wait()

    return list(pl.pallas_call(
        body, name="grad_join_halves_" + tag,
        in_specs=[ANY] * n_arr, out_specs=[ANY] * n_arr,
        out_shape=[jax.ShapeDtypeStruct(g.shape, F32) for g in g_halves],
        scratch_shapes=[pltpu.SemaphoreType.DMA((n_arr,)), pltpu.SemaphoreType.DMA((n_arr,))],
    )(*g_halves))


def _window_sum(name, big, widx, axis, nwin, extra, stacked, out_dtype):
    rows, cols = _shape_div(big.shape, axis, nwin)
    tr = _fit(rows, ROW_TILE)
    nbr = rows // tr

    def body(w_ref, big_ref, ex_ref, o_ref):
        acc = big_ref[...].astype(F32)
        if stacked:
            for j in range(N_CHIP_PEERS):
                acc = acc + ex_ref[j].astype(F32)
        else:
            acc = acc + ex_ref[...].astype(F32)
        o_ref[...] = acc.astype(o_ref.dtype)

    if axis == 0:
        big_spec = pl.BlockSpec((tr, cols), lambda i, w: (w[0] * nbr + i, 0))
    else:
        big_spec = pl.BlockSpec((tr, cols), lambda i, w: (i, w[0]))
    ex_spec = (pl.BlockSpec((N_CHIP_PEERS, tr, cols), lambda i, w: (0, i, 0)) if stacked
               else pl.BlockSpec((tr, cols), lambda i, w: (i, 0)))
    return pl.pallas_call(
        body, name=name,
        grid_spec=pltpu.PrefetchScalarGridSpec(
            num_scalar_prefetch=1, grid=(nbr,), in_specs=[big_spec, ex_spec],
            out_specs=pl.BlockSpec((tr, cols), lambda i, w: (i, 0))),
        out_shape=jax.ShapeDtypeStruct((rows, cols), out_dtype),
        compiler_params=_params(("parallel",)),
    )(widx.reshape(1).astype(jnp.int32), big, extra)


def _scatter_begin(tag, arrs, dws):
    _, _, c = _me()
    from_sibling = _exchange_halves(tag, arrs, dws)
    halves = [_window_sum("grad_sum_halves_" + BIG[a][0], dw, c, 1 - BIG[a][2], 2, fs, False, BF16)
              for a, dw, fs in zip(arrs, dws, from_sibling)]
    send_sems, recv_sems, halves, lands, token = _pieces_start(tag, arrs, halves)
    return (send_sems, recv_sems, halves, lands), token[0, 0]


def _scatter_end(tag, arrs, state, after):
    x, y, _ = _me()
    send_sems, recv_sems, halves, lands = state
    halves, lands = _pieces_wait(tag, arrs, send_sems, recv_sems, halves, lands, after)
    g_own = [_window_sum("grad_sum_pieces_" + BIG[a][0], h, _shard_of((x, y)), BIG[a][2], N_SHARD, r, True, F32)
             for a, h, r in zip(arrs, halves, lands)]
    return g_own, _join_halves(tag, g_own)


def _gather_rows(block, name):
    r, d = block.shape

    def body(in_ref, out_ref, send_sems, recv_sems):
        x, y, c = _me()
        out_ref[_dev_index(x, y, c)] = in_ref[...]
        copies = []
        for mask in range(1, N_DEV):
            peer = (x ^ (mask >> 2), y ^ ((mask >> 1) & 1), c ^ (mask & 1))
            cp = _rcopy(in_ref, out_ref.at[_dev_index(x, y, c)], send_sems, recv_sems, mask - 1, peer)
            cp.start()
            copies.append((cp, peer))
        for mask in range(1, N_DEV):
            peer = (x ^ (mask >> 2), y ^ ((mask >> 1) & 1), c ^ (mask & 1))
            landed = out_ref.at[_dev_index(*peer)]
            _rcopy(landed, landed, send_sems, recv_sems, mask - 1, peer).wait_recv()
        for cp, _ in copies:
            cp.wait_send()

    return pl.pallas_call(
        body, name=name, in_specs=[VMEM_FULL], out_specs=VMEM_FULL,
        out_shape=jax.ShapeDtypeStruct((N_DEV, r, d), F32),
        scratch_shapes=[pltpu.SemaphoreType.DMA((N_DEV - 1,)), pltpu.SemaphoreType.DMA((N_DEV - 1,))],
    )(block)


ADA_ROWS = 80
ADA_W = 6 * D_MODEL // N_SHARD


def _ada_forward(c_block, cctx_block, w_ada, b_shard):
    d = c_block.shape[1]

    def body(c_ref, cc_ref, w_ref, b_ref, act_ref, mods_ref, raw, mloc, send_sems, recv_sems):
        x, y, c = _me()
        me = _dev_index(x, y, c)
        s_me = _shard_of((x, y))
        raw[72:ADA_ROWS, :] = jnp.zeros((ADA_ROWS - 72, d), F32)
        raw[pl.ds(pl.multiple_of(me * 8, 8), 8), :] = c_ref[...]
        raw[64:72, :] = cc_ref[...]
        sends = []
        for mask in range(1, N_DEV):
            peer = (x ^ (mask >> 2), y ^ ((mask >> 1) & 1), c ^ (mask & 1))
            cp = _rcopy(c_ref, raw.at[pl.ds(pl.multiple_of(me * 8, 8), 8), :], send_sems, recv_sems, mask - 1, peer)
            cp.start()
            sends.append(cp)
        for mask in range(1, N_DEV):
            peer = (x ^ (mask >> 2), y ^ ((mask >> 1) & 1), c ^ (mask & 1))
            landed = raw.at[pl.ds(pl.multiple_of(_dev_index(*peer) * 8, 8), 8), :]
            _rcopy(landed, landed, send_sems, recv_sems, mask - 1, peer).wait_recv()
        v = raw[...]
        act = v * jax.nn.sigmoid(v)
        act_ref[...] = act
        mloc[...] = lax.dot_general(act.astype(BF16), w_ref[...].astype(BF16), NN,
                                    preferred_element_type=F32) + b_ref[...]
        mods_ref[s_me, 0:8, :] = mloc[pl.ds(pl.multiple_of(me * 8, 8), 8), :]
        mods_ref[s_me, 8:16, :] = mloc[64:72, :]
        base = N_DEV - 1
        for j, chip in enumerate(_other_chips(x, y)):
            peer = (*chip, c)
            rows = mloc.at[pl.ds(pl.multiple_of(_dev_index(*peer) * 8, 8), 8), :]
            cp = _rcopy(rows, mods_ref.at[s_me, 0:8, :], send_sems, recv_sems, base + 2 * j, peer)
            cp.start()
            sends.append(cp)
            cp = _rcopy(mloc.at[64:72, :], mods_ref.at[s_me, 8:16, :], send_sems, recv_sems, base + 2 * j + 1, peer)
            cp.start()
            sends.append(cp)
        for j, chip in enumerate(_other_chips(x, y)):
            for part in range(2):
                landed = mods_ref.at[_shard_of(chip), 8 * part:8 * part + 8, :]
                _rcopy(landed, landed, send_sems, recv_sems, base + 2 * j + part, (*chip, c)).wait_recv()
        for cp in sends:
            cp.wait_send()

    n_sem = N_DEV - 1 + 2 * N_CHIP_PEERS
    return pl.pallas_call(
        body, name="ada_forward",
        in_specs=[VMEM_FULL] * 4, out_specs=[VMEM_FULL, VMEM_FULL],
        out_shape=[jax.ShapeDtypeStruct((ADA_ROWS, d), F32), jax.ShapeDtypeStruct((N_SHARD, 16, ADA_W), F32)],
        scratch_shapes=[pltpu.VMEM((ADA_ROWS, d), F32), pltpu.VMEM((ADA_ROWS, ADA_W), F32),
                        pltpu.SemaphoreType.DMA((n_sem,)), pltpu.SemaphoreType.DMA((n_sem,))],
        compiler_params=pltpu.CompilerParams(vmem_limit_bytes=VMEM_LIMIT),
    )(c_block, cctx_block, w_ada, b_shard)


def _small_reduce(gathered):
    d = gathered.shape[2]

    def body(g_ref, o_ref):
        tot = g_ref[0]
        for i in range(1, N_DEV):
            tot = tot + g_ref[i]
        o_ref[...] = tot
        o_ref[0:2, :] = tot[0:2] + tot[6:8]
        o_ref[12:13, :] = jnp.broadcast_to(jnp.sum(tot[12:13], axis=1, keepdims=True), (1, d))

    return pl.pallas_call(body, name="small_reduce", in_specs=[VMEM_FULL], out_specs=VMEM_FULL,
                          out_shape=jax.ShapeDtypeStruct((16, d), F32))(gathered)


def _cctx_grad(gathered, c_ctx):
    d = gathered.shape[2]

    def body(g_ref, c_ref, o_ref):
        tot = g_ref[0, 0:1, :]
        for chip in range(1, N_SHARD):
            tot = tot + g_ref[2 * chip, 0:1, :]
        v = c_ref[...]
        sig = jax.nn.sigmoid(v)
        o_ref[...] = tot * (sig * (1.0 + v * (1.0 - sig)))

    return pl.pallas_call(body, name="cctx_grad", in_specs=[VMEM_FULL, VMEM_FULL], out_specs=VMEM_FULL,
                          out_shape=jax.ShapeDtypeStruct((1, d), F32))(gathered, c_ctx.reshape(1, d))


def _cast_into_full(w, shard, full, axis, name):
    r, cdim = w.shape
    tr = _fit(r, ROW_TILE)
    nbr = r // tr

    def body(s_ref, w_ref, o_ref):
        o_ref[...] = w_ref[...].astype(BF16)

    if axis == 0:
        out_spec = pl.BlockSpec((tr, cdim), lambda i, s: (s[0] * nbr + i, 0))
    else:
        out_spec = pl.BlockSpec((tr, cdim), lambda i, s: (i, s[0]))
    return pl.pallas_call(
        body, name=name,
        grid_spec=pltpu.PrefetchScalarGridSpec(
            num_scalar_prefetch=1, grid=(nbr,), in_specs=[pl.BlockSpec((tr, cdim), lambda i, s: (i, 0))],
            out_specs=out_spec),
        out_shape=jax.ShapeDtypeStruct(full, BF16), compiler_params=_params(("parallel",)),
    )(shard.reshape(1).astype(jnp.int32), w)


def _adamw_halves(w, g_own, g_other, m, v, core, axis, name):
    r, cdim = w.shape
    hr, hc = (r // 2, cdim) if axis == 1 else (r, cdim // 2)
    assert g_own.shape == (hr, hc) and g_other.shape == (hr, hc)
    tr = _fit(hr, 128)
    nb = hr // tr
    c1 = 1.0 - ADAM_B1 ** ADAM_STEP
    c2 = 1.0 - ADAM_B2 ** ADAM_STEP

    def body(c_ref, w_ref, go_ref, gt_ref, m_ref, v_ref, g_ref, d_ref, nm_ref, nv_ref):
        gv = jnp.where(pl.program_id(0) == c_ref[0], go_ref[...], gt_ref[...])
        nm = ADAM_B1 * m_ref[...] + (1.0 - ADAM_B1) * gv
        nv = ADAM_B2 * v_ref[...] + (1.0 - ADAM_B2) * (gv * gv)
        g_ref[...] = gv
        nm_ref[...] = nm
        nv_ref[...] = nv
        d_ref[...] = -ADAM_LR * ((nm / c1) / (jnp.sqrt(nv / c2) + ADAM_EPS) + ADAM_WD * w_ref[...])

    if axis == 1:
        big = pl.BlockSpec((tr, hc), lambda p, i, c: (p * nb + i, 0))
    else:
        big = pl.BlockSpec((tr, hc), lambda p, i, c: (i, p))
    half = pl.BlockSpec((tr, hc), lambda p, i, c: (i, 0))
    sh = jax.ShapeDtypeStruct((r, cdim), F32)
    return pl.pallas_call(
        body, name=name,
        grid_spec=pltpu.PrefetchScalarGridSpec(
            num_scalar_prefetch=1, grid=(2, nb), in_specs=[big, half, half, big, big], out_specs=[big] * 4),
        out_shape=[sh] * 4, compiler_params=_params(("parallel", "parallel")),
    )(core.reshape(1).astype(jnp.int32), w, g_own, g_other, m, v)


def _adamw(w, g, m, v, name):
    r, cdim = w.shape
    tr = _fit(r, 128) if r % (ROW_TILE // 4) == 0 else r
    c1 = 1.0 - ADAM_B1 ** ADAM_STEP
    c2 = 1.0 - ADAM_B2 ** ADAM_STEP

    def body(w_ref, g_ref, m_ref, v_ref, d_ref, nm_ref, nv_ref):
        gv = g_ref[...]
        nm = ADAM_B1 * m_ref[...] + (1.0 - ADAM_B1) * gv
        nv = ADAM_B2 * v_ref[...] + (1.0 - ADAM_B2) * (gv * gv)
        nm_ref[...] = nm
        nv_ref[...] = nv
        d_ref[...] = -ADAM_LR * ((nm / c1) / (jnp.sqrt(nv / c2) + ADAM_EPS) + ADAM_WD * w_ref[...])

    spec = pl.BlockSpec((tr, cdim), lambda i: (i, 0))
    sh = jax.ShapeDtypeStruct((r, cdim), F32)
    return pl.pallas_call(body, name=name, grid=(r // tr,), in_specs=[spec] * 4, out_specs=[spec] * 3,
                          out_shape=[sh, sh, sh], compiler_params=_params(("parallel",)))(w, g, m, v)


SMALL = (("c_ctx", D_MODEL), ("b_ada", 6 * D_MODEL), ("q_norm_g", HEAD_DIM), ("k_norm_g", HEAD_DIM),
         ("sink_logit", HEADS_A), ("ln1_g", D_MODEL), ("ln1_b", D_MODEL), ("ln2_g", D_MODEL), ("ln2_b", D_MODEL))
WEIGHT_ORDER = ("c_ctx", "w_ada", "b_ada", "w_in", "q_norm_g", "k_norm_g", "sink_logit", "w_out", "ln1_g", "ln1_b",
                "w_gate", "w_up", "w_down", "ln2_g", "ln2_b")


def kernel(x, c, ctx, c_ctx, w_ada, b_ada, w_in, q_norm_g, k_norm_g, sink_logit, w_out, ln1_g, ln1_b, w_gate, w_up, w_down, ln2_g, ln2_b, loss_target, m_c_ctx, m_w_ada, m_b_ada, m_w_in, m_q_norm_g, m_k_norm_g, m_sink_logit, m_w_out, m_ln1_g, m_ln1_b, m_w_gate, m_w_up, m_w_down, m_ln2_g, m_ln2_b, v_c_ctx, v_w_ada, v_b_ada, v_w_in, v_q_norm_g, v_k_norm_g, v_sink_logit, v_w_out, v_ln1_g, v_ln1_b, v_w_gate, v_w_up, v_w_down, v_ln2_g, v_ln2_b):
    d = D_MODEL
    w = dict(c_ctx=c_ctx, w_ada=w_ada[0], b_ada=b_ada, w_in=w_in[0], q_norm_g=q_norm_g, k_norm_g=k_norm_g,
             sink_logit=sink_logit, w_out=w_out[0], ln1_g=ln1_g, ln1_b=ln1_b, w_gate=w_gate[0], w_up=w_up[0],
             w_down=w_down[0], ln2_g=ln2_g, ln2_b=ln2_b)
    m = dict(c_ctx=m_c_ctx, w_ada=m_w_ada[0], b_ada=m_b_ada, w_in=m_w_in[0], q_norm_g=m_q_norm_g, k_norm_g=m_k_norm_g,
             sink_logit=m_sink_logit, w_out=m_w_out[0], ln1_g=m_ln1_g, ln1_b=m_ln1_b, w_gate=m_w_gate[0],
             w_up=m_w_up[0], w_down=m_w_down[0], ln2_g=m_ln2_g, ln2_b=m_ln2_b)
    v = dict(c_ctx=v_c_ctx, w_ada=v_w_ada[0], b_ada=v_b_ada, w_in=v_w_in[0], q_norm_g=v_q_norm_g, k_norm_g=v_k_norm_g,
             sink_logit=v_sink_logit, w_out=v_w_out[0], ln1_g=v_ln1_g, ln1_b=v_ln1_b, w_gate=v_w_gate[0],
             w_up=v_w_up[0], w_down=v_w_down[0], ln2_g=v_ln2_g, ln2_b=v_ln2_b)
    mx, my, mc = _me()
    s_me = _shard_of((mx, my))
    me = _dev_index(mx, my, mc)
    pad8 = lambda row: jnp.concatenate([row.reshape(1, -1), jnp.zeros((7, row.size), F32)], axis=0)

    b_shard = lax.dynamic_slice(b_ada, (0, s_me * ADA_W), (1, ADA_W))
    act, mods4 = _ada_forward(pad8(c), pad8(c_ctx), w["w_ada"], b_shard)

    bufs = {name: _cast_into_full(w[name], s_me, shape, axis, "cast_" + name) for name, shape, axis in BIG}
    gathers = []
    prev = mods4
    for k, names in enumerate(W_GROUPS):
        arrs = tuple(BIG_INDEX[name] for name in names)
        send_sems, recv_sems, thru, prev = _gather_start("g%d" % k, arrs, [bufs[name] for name in names], prev)
        gathers.append((arrs, send_sems, recv_sems, thru))

    def weights(k, after):
        arrs, send_sems, recv_sems, thru = gathers[k]
        landed = _gather_wait("g%d" % k, arrs, send_sems, recv_sems, thru, after)
        return _gather_forward("g%d" % k, arrs, landed)

    mod = jnp.transpose(mods4[:, 0:1, :], (1, 0, 2)).reshape(1, 6 * d) + prev[0, 0]
    mod_ctx = jnp.transpose(mods4[:, 8:9, :], (1, 0, 2)).reshape(1, 6 * d)

    scatters = {}

    def grads_out(k, dws):
        arrs = tuple(BIG_INDEX[name] for name in G_GROUPS[k])
        scatters[k], zero = _scatter_begin("g%d" % k, arrs, dws)
        return zero

    grad_x, partial = _layer_fwd_bwd(x[0], ctx[0], loss_target[0], mod, mod_ctx, weights, grads_out,
                                     q_norm_g, k_norm_g, sink_logit, ln1_g, ln1_b, ln2_g, ln2_b)
    grads, delta, new_m, new_v = {}, {}, {}, {}

    gathered = _gather_rows(partial, "gather_partials")
    tot = _small_reduce(gathered)
    grads["b_ada"] = tot[0:6].reshape(1, 6 * d)
    grads["ln1_g"], grads["ln1_b"], grads["ln2_g"], grads["ln2_b"] = tot[8:9], tot[9:10], tot[10:11], tot[11:12]
    grads["q_norm_g"] = tot[13:14, 0:HEAD_DIM]
    grads["k_norm_g"] = tot[13:14, HEAD_DIM:2 * HEAD_DIM]
    grads["sink_logit"] = tot[13:14, 2 * HEAD_DIM:2 * HEAD_DIM + HEADS_A]
    loss = tot[12, 0]

    dm_all = gathered[:, 0:6, :].reshape(N_DEV, 6 * d)
    dmc_tot = jnp.concatenate([tot[6:8].reshape(1, 2 * d), jnp.zeros((1, 4 * d), F32)], axis=1)
    dm_rows = jnp.concatenate([pad8(dm_all[i]) for i in range(N_DEV)] + [pad8(dmc_tot), jnp.zeros((8, 6 * d), F32)], axis=0)
    dm_shard = lax.dynamic_slice(dm_rows, (0, s_me * ADA_W), (ADA_ROWS, ADA_W))
    grads["w_ada"] = _matmul(act, dm_shard, name="dw_ada", ta=True, tm=1024, tn=1024, tk=ADA_ROWS, out_dtype=F32)
    dmc_shard = lax.dynamic_slice(pad8(dmc_tot), (0, s_me * ADA_W), (8, ADA_W))
    cc_part = _matmul(dmc_shard, w["w_ada"], name="d_cctx", tb=True, tm=8, tn=1024, tk=1536, out_dtype=F32)
    grads["c_ctx"] = _cctx_grad(_gather_rows(cc_part, "gather_cctx"), c_ctx).reshape(d)

    delta["w_ada"], new_m["w_ada"], new_v["w_ada"] = _adamw(w["w_ada"], grads["w_ada"], m["w_ada"], v["w_ada"],
                                                            "adamw_w_ada")
    pack = lambda t: jnp.concatenate([t[name].reshape(1, size) for name, size in SMALL], axis=1)
    pd, pm, pv = _adamw(pack(w), pack(grads), pack(m), pack(v), "adamw_small")
    off = 0
    for name, size in SMALL:
        delta[name], new_m[name], new_v[name] = [t[:, off:off + size].reshape(w[name].shape) for t in (pd, pm, pv)]
        grads[name] = grads[name].reshape(w[name].shape)
        off += size

    after = pd
    for k, names in enumerate(G_GROUPS):
        arrs = tuple(BIG_INDEX[name] for name in names)
        g_own, g_other = _scatter_end("g%d" % k, arrs, scatters[k], after)
        for name, own, other in zip(names, g_own, g_other):
            grads[name], delta[name], new_m[name], new_v[name] = _adamw_halves(
                w[name], own, other, m[name], v[name], mc, BIG[BIG_INDEX[name]][2], "adamw_" + name)
            after = new_v[name]

    lead = lambda name, t: t[None] if name in ("w_ada", "w_in", "w_out", "w_gate", "w_up", "w_down") else t
    outs = [loss, grad_x[None]]
    for group in (grads, delta, new_m, new_v):
        outs += [lead(name, group[name]) for name in WEIGHT_ORDER]
    return tuple(outs)
```

```python
import functools
import math

import jax
import jax.numpy as jnp
from jax import lax
from jax.experimental import pallas as pl
from jax.experimental.pallas import tpu as pltpu

F32 = jnp.float32
BF16 = jnp.bfloat16
MESH = pl.DeviceIdType.MESH

D_MODEL = 2048
HEAD_DIM = 128
HEADS_A = 8
HEADS_B = 8
KV_A = 2
KV_B = 2
GROUP = 4
GRID_W = 64
WINDOW = 128
BLOCK = 128
FFN = 5632
IN_WIDTH = 3072
MIX_WIDTH = 2048
ROPE_THETA = 10000.0
EPS = 1e-6
ATTN_SCALE = HEAD_DIM ** -0.5
DN_ALPHA = 2.0 ** 0.25
N_SHARD = 4
N_DEV = 8

ADAM_LR = 0.001
ADAM_B1 = 0.9
ADAM_B2 = 0.999
ADAM_EPS = 1e-08
ADAM_WD = 0.01
ADAM_STEP = 10

QA0, KA0, VA0, QB0, KB0, VB0 = 0, 1024, 1280, 1536, 2560, 2816

VMEM_LIMIT = 56 * 1024 * 1024
ROW_TILE = 256
NN = (((1,), (0,)), ((), ()))
NT = (((1,), (1,)), ((), ()))
TN = (((0,), (0,)), ((), ()))


def _fit(total, pref):
    step = ROW_TILE // 4
    best = step
    for cand in range(step, pref + 1, step):
        if total % cand == 0:
            best = cand
    return best


def _params(sem=None):
    return pltpu.CompilerParams(dimension_semantics=sem, vmem_limit_bytes=VMEM_LIMIT)


def _matmul(a, b, *, name, ta=False, tb=False, tm, tn, tk, out_dtype):
    m = a.shape[1] if ta else a.shape[0]
    k = a.shape[0] if ta else a.shape[1]
    n = b.shape[0] if tb else b.shape[1]
    assert (b.shape[1] if tb else b.shape[0]) == k
    tm, tn, tk = min(tm, m), min(tn, n), min(tk, k)
    assert m % tm == 0 and n % tn == 0 and k % tk == 0, (name, m, n, k, tm, tn, tk)
    nk = k // tk
    dn = (((0 if ta else 1,), (1 if tb else 0,)), ((), ()))

    def product(a_ref, b_ref):
        return lax.dot_general(a_ref[...].astype(BF16), b_ref[...].astype(BF16), dn, preferred_element_type=F32)

    def body_whole_k(a_ref, b_ref, o_ref):
        o_ref[...] = product(a_ref, b_ref).astype(o_ref.dtype)

    def body(a_ref, b_ref, o_ref, acc_ref):
        kk = pl.program_id(2)
        part = product(a_ref, b_ref)

        @pl.when(kk == 0)
        def _():
            acc_ref[...] = part

        @pl.when(kk != 0)
        def _():
            acc_ref[...] += part

        @pl.when(kk == nk - 1)
        def _():
            o_ref[...] = acc_ref[...].astype(o_ref.dtype)

    a_spec = (pl.BlockSpec((tk, tm), lambda i, j, kk: (kk, i)) if ta
              else pl.BlockSpec((tm, tk), lambda i, j, kk: (i, kk)))
    b_spec = (pl.BlockSpec((tn, tk), lambda i, j, kk: (j, kk)) if tb
              else pl.BlockSpec((tk, tn), lambda i, j, kk: (kk, j)))
    return pl.pallas_call(
        body_whole_k if nk == 1 else body, name=name, grid=(m // tm, n // tn, nk),
        in_specs=[a_spec, b_spec],
        out_specs=pl.BlockSpec((tm, tn), lambda i, j, kk: (i, j)),
        out_shape=jax.ShapeDtypeStruct((m, n), out_dtype),
        scratch_shapes=[] if nk == 1 else [pltpu.VMEM((tm, tn), F32)],
        compiler_params=_params(("parallel", "parallel", "arbitrary")),
    )(a, b)


def _modulate_rows(x, ctx, mods):
    n, d = x.shape
    c = ctx.shape[0]
    nx = n // ROW_TILE
    assert c == ROW_TILE

    def body(x_ref, ctx_ref, mods_ref, o_ref):
        i = pl.program_id(0)

        @pl.when(i < nx)
        def _():
            o_ref[...] = (x_ref[...] * (1.0 + mods_ref[0:1, :]) + mods_ref[1:2, :]).astype(BF16)

        @pl.when(i >= nx)
        def _():
            o_ref[...] = (ctx_ref[...] * (1.0 + mods_ref[2:3, :]) + mods_ref[3:4, :]).astype(BF16)

    return pl.pallas_call(
        body, name="modulate_rows", grid=(nx + 1,),
        in_specs=[pl.BlockSpec((ROW_TILE, d), lambda i: (jnp.minimum(i, nx - 1), 0)),
                  pl.BlockSpec((ROW_TILE, d), lambda i: (0, 0)),
                  pl.BlockSpec((8, d), lambda i: (0, 0))],
        out_specs=pl.BlockSpec((ROW_TILE, d), lambda i: (i, 0)),
        out_shape=jax.ShapeDtypeStruct((n + c, d), BF16),
        compiler_params=_params(("parallel",)),
    )(x, ctx, mods)


def _rope_tables(n, c):
    rows = n // GRID_W
    row_ids = jnp.repeat(jnp.arange(rows, dtype=F32), GRID_W)
    col_ids = jnp.tile(jnp.arange(GRID_W, dtype=F32), rows)
    axis_dim = HEAD_DIM // 2
    inv_freq = jnp.power(ROPE_THETA, -jnp.arange(0, axis_dim, 2, dtype=F32) / axis_dim)
    ang_r = row_ids[:, None] * inv_freq
    ang_c = col_ids[:, None] * inv_freq
    ang = jnp.concatenate([ang_r, ang_r, ang_c, ang_c], axis=-1)
    cos, sin = jnp.cos(ang), jnp.sin(ang)
    quarter = (jnp.arange(HEAD_DIM) // (HEAD_DIM // 4)) % 2
    sin_a = jnp.where(quarter == 0, -sin, 0.0)
    sin_b = jnp.where(quarter == 1, sin, 0.0)
    pad = lambda t, v: jnp.concatenate([t, jnp.full((c, HEAD_DIM), v, F32)], axis=0)
    return pad(cos, 1.0), pad(sin_a, 0.0), pad(sin_b, 0.0)


def _rope(x, cos, sin_a, sin_b):
    return x * cos + pltpu.roll(x, 96, 1) * sin_a + pltpu.roll(x, 32, 1) * sin_b


def _rope_t(dy, cos, sin_a, sin_b):
    return dy * cos - pltpu.roll(dy, 96, 1) * sin_a - pltpu.roll(dy, 32, 1) * sin_b


def _rms(x):
    r = lax.rsqrt(jnp.mean(x * x, axis=-1, keepdims=True) + EPS)
    return x * r, r


def _qkv_post(h_all, cos, sin_a, sin_b, q_g, k_g):
    t = h_all.shape[0]
    nt = t // ROW_TILE

    def body(h_ref, cos_ref, sa_ref, sb_ref, qg_ref, kg_ref, qa_ref, ka_ref, va_ref, qb_ref, kb_ref, vb_ref):
        cos_, sa, sb = cos_ref[...], sa_ref[...], sb_ref[...]
        sl = lambda off, hh: h_ref[:, off + hh * HEAD_DIM: off + (hh + 1) * HEAD_DIM]
        for hh in range(HEADS_A):
            qa_ref[hh] = (_rope(sl(QA0, hh), cos_, sa, sb) * ATTN_SCALE).astype(BF16)
        for hh in range(KV_A):
            ka_ref[hh] = _rope(sl(KA0, hh), cos_, sa, sb).astype(BF16)
            va_ref[hh] = sl(VA0, hh).astype(BF16)
        for hh in range(HEADS_B):
            xn, _ = _rms(sl(QB0, hh))
            qb_ref[hh] = (_rope(xn * qg_ref[...], cos_, sa, sb) * ATTN_SCALE).astype(BF16)
        for hh in range(KV_B):
            xn, _ = _rms(sl(KB0, hh))
            kb_ref[hh] = _rope(xn * kg_ref[...], cos_, sa, sb).astype(BF16)
            vb_ref[hh] = sl(VB0, hh).astype(BF16)

    tab = pl.BlockSpec((ROW_TILE, HEAD_DIM), lambda i: (i, 0))
    gain = pl.BlockSpec((1, HEAD_DIM), lambda i: (0, 0))
    hs = lambda nh: pl.BlockSpec((nh, ROW_TILE, HEAD_DIM), lambda i: (0, i, 0))
    sh = lambda nh: jax.ShapeDtypeStruct((nh, t, HEAD_DIM), BF16)
    return pl.pallas_call(
        body, name="qkv_post", grid=(nt,),
        in_specs=[pl.BlockSpec((ROW_TILE, IN_WIDTH), lambda i: (i, 0)), tab, tab, tab, gain, gain],
        out_specs=[hs(HEADS_A), hs(KV_A), hs(KV_A), hs(HEADS_B), hs(KV_B), hs(KV_B)],
        out_shape=[sh(HEADS_A), sh(KV_A), sh(KV_A), sh(HEADS_B), sh(KV_B), sh(KV_B)],
        compiler_params=_params(("parallel",)),
    )(h_all, cos, sin_a, sin_b, q_g, k_g)


def _qkv_bwd_post(h_all, cos, sin_a, sin_b, q_g, k_g, dqa, dka, dva, dqb, dkb, dvb, n):
    t = h_all.shape[0]
    nt = t // ROW_TILE
    nx = n // ROW_TILE

    def body(h_ref, cos_ref, sa_ref, sb_ref, qg_ref, kg_ref,
             dqa_ref, dka_ref, dva_ref, dqb_ref, dkb_ref, dvb_ref, dh_ref, gs_ref):
        i = pl.program_id(0)
        cos_, sa, sb = cos_ref[...], sa_ref[...], sb_ref[...]
        latent = (i < nx).astype(F32)
        sl = lambda off, hh: h_ref[:, off + hh * HEAD_DIM: off + (hh + 1) * HEAD_DIM]

        def put(off, hh, val):
            dh_ref[:, off + hh * HEAD_DIM: off + (hh + 1) * HEAD_DIM] = val.astype(BF16)

        def norm_bwd(x, gain, dy):
            xn, r = _rms(x)
            dxh = dy * gain
            dx = r * (dxh - xn * jnp.mean(dxh * xn, axis=-1, keepdims=True))
            return dx, jnp.sum(dy * xn, axis=0, keepdims=True)

        for hh in range(HEADS_A):
            put(QA0, hh, _rope_t(dqa_ref[hh] * (ATTN_SCALE * latent), cos_, sa, sb))
        for hh in range(KV_A):
            put(KA0, hh, _rope_t(dka_ref[hh], cos_, sa, sb))
            put(VA0, hh, dva_ref[hh])
        gq = jnp.zeros((1, HEAD_DIM), F32)
        gk = jnp.zeros((1, HEAD_DIM), F32)
        for hh in range(HEADS_B):
            dq_t = dqb_ref[hh // GROUP, :, (hh % GROUP) * ROW_TILE:(hh % GROUP + 1) * ROW_TILE]
            dy = _rope_t(dq_t.T * (ATTN_SCALE * latent), cos_, sa, sb)
            dx, g = norm_bwd(sl(QB0, hh), qg_ref[...], dy)
            put(QB0, hh, dx)
            gq = gq + g
        for hh in range(KV_B):
            dy = _rope_t(dkb_ref[hh], cos_, sa, sb)
            dx, g = norm_bwd(sl(KB0, hh), kg_ref[...], dy)
            put(KB0, hh, dx)
            gk = gk + g
            put(VB0, hh, dvb_ref[hh])
        upd = jnp.concatenate([gq, gk, jnp.zeros((6, HEAD_DIM), F32)], axis=0)

        @pl.when(i == 0)
        def _():
            gs_ref[...] = upd

        @pl.when(i != 0)
        def _():
            gs_ref[...] += upd

    tab = pl.BlockSpec((ROW_TILE, HEAD_DIM), lambda i: (i, 0))
    gain = pl.BlockSpec((1, HEAD_DIM), lambda i: (0, 0))
    lat = lambda nh: pl.BlockSpec((nh, ROW_TILE, HEAD_DIM), lambda i: (0, jnp.minimum(i, nx - 1), 0))
    full = lambda nh: pl.BlockSpec((nh, ROW_TILE, HEAD_DIM), lambda i: (0, i, 0))
    return pl.pallas_call(
        body, name="qkv_bwd_post", grid=(nt,),
        in_specs=[pl.BlockSpec((ROW_TILE, IN_WIDTH), lambda i: (i, 0)), tab, tab, tab, gain, gain,
                  lat(HEADS_A), full(KV_A), full(KV_A),
                  pl.BlockSpec((KV_B, None, HEAD_DIM, GROUP * ROW_TILE), lambda i: (0, jnp.minimum(i, nx - 1), 0, 0)),
                  full(KV_B), full(KV_B)],
        out_specs=[pl.BlockSpec((ROW_TILE, IN_WIDTH), lambda i: (i, 0)),
                   pl.BlockSpec((8, HEAD_DIM), lambda i: (0, 0))],
        out_shape=[jax.ShapeDtypeStruct((t, IN_WIDTH), BF16), jax.ShapeDtypeStruct((8, HEAD_DIM), F32)],
        compiler_params=_params(("arbitrary",)),
    )(h_all, cos, sin_a, sin_b, q_g, k_g, dqa, dka, dva, dqb, dkb, dvb)


GB_TQ = 256
GB_TK = 256


def _heads_rows(ref2d, tq):
    return jnp.concatenate([ref2d[:, hh * HEAD_DIM:(hh + 1) * HEAD_DIM] for hh in range(GROUP)], axis=0)


def _attn_b_fwd(qb, kb, vb, n):
    t = kb.shape[1]
    nk = t // GB_TK
    tq = GB_TQ
    rows = GROUP * tq

    def body(q_ref, k_ref, v_ref, o_ref, lse_ref, m_s, l_s, acc_s):
        q = q_ref[...].reshape(rows, HEAD_DIM)
        m_s[...] = jnp.full((1, rows), -jnp.inf, F32)
        l_s[...] = jnp.zeros((1, rows), F32)
        acc_s[...] = jnp.zeros((HEAD_DIM, rows), F32)

        def scores(j):
            start = pl.multiple_of(j * GB_TK, GB_TK)
            return lax.dot_general(k_ref[pl.ds(start, GB_TK), :], q, NT, preferred_element_type=F32)

        def step(j, st):
            st_next = scores(jnp.minimum(j + 1, nk - 1))
            vs = v_ref[pl.ds(pl.multiple_of(j * GB_TK, GB_TK), GB_TK), :]
            m_prev = m_s[...]
            m_new = jnp.maximum(m_prev, jnp.max(st, axis=0, keepdims=True))
            p = jnp.exp(st - m_new)
            alpha = jnp.exp(m_prev - m_new)
            l_s[...] = alpha * l_s[...] + jnp.sum(p, axis=0, keepdims=True)
            acc_s[...] = alpha * acc_s[...] + lax.dot_general(vs, p.astype(BF16), TN, preferred_element_type=F32)
            m_s[...] = m_new
            return st_next

        lax.fori_loop(0, nk, step, scores(0))
        ot = acc_s[...] * (1.0 / l_s[...])
        lse_ref[...] = m_s[...] + jnp.log(l_s[...])
        for hh in range(GROUP):
            o_ref[:, hh * HEAD_DIM:(hh + 1) * HEAD_DIM] = ot[:, hh * tq:(hh + 1) * tq].T.astype(BF16)

    return pl.pallas_call(
        body, name="attn_b_fwd", grid=(KV_B, n // tq),
        in_specs=[pl.BlockSpec((GROUP, tq, HEAD_DIM), lambda g, i: (g, i, 0)),
                  pl.BlockSpec((None, t, HEAD_DIM), lambda g, i: (g, 0, 0)),
                  pl.BlockSpec((None, t, HEAD_DIM), lambda g, i: (g, 0, 0))],
        out_specs=[pl.BlockSpec((tq, GROUP * HEAD_DIM), lambda g, i: (i, g)),
                   pl.BlockSpec((None, None, 1, rows), lambda g, i: (g, i, 0, 0))],
        out_shape=[jax.ShapeDtypeStruct((n, HEADS_B * HEAD_DIM), BF16),
                   jax.ShapeDtypeStruct((KV_B, n // tq, 1, rows), F32)],
        scratch_shapes=[pltpu.VMEM((1, rows), F32), pltpu.VMEM((1, rows), F32), pltpu.VMEM((HEAD_DIM, rows), F32)],
        compiler_params=_params(("parallel", "parallel")),
    )(qb, kb, vb)


def _attn_b_bwd(qb, kb, vb, dheads, lse, delta, n):
    t = kb.shape[1]
    nk = t // GB_TK
    tq = GB_TQ
    nq = n // tq
    rows = GROUP * tq

    def body(q_ref, k_ref, v_ref, do_ref, lse_ref, dl_ref, dq_ref, dk_ref, dv_ref):
        j = pl.program_id(1)
        i = pl.program_id(2)

        q = q_ref[...].reshape(rows, HEAD_DIM)
        do = _heads_rows(do_ref, tq)
        ks, vs = k_ref[...], v_ref[...]
        st = lax.dot_general(ks, q, NT, preferred_element_type=F32)
        p = jnp.exp(st - lse_ref[...])
        dpt = lax.dot_general(vs, do, NT, preferred_element_type=F32)
        ds = (p * (dpt - dl_ref[...])).astype(BF16)
        dv_part = lax.dot_general(p.astype(BF16), do, NN, preferred_element_type=F32)
        dk_part = lax.dot_general(ds, q, NN, preferred_element_type=F32)
        dq_part = lax.dot_general(ks, ds, TN, preferred_element_type=F32)

        @pl.when(i == 0)
        def _():
            dk_ref[...] = dk_part
            dv_ref[...] = dv_part

        @pl.when(i != 0)
        def _():
            dk_ref[...] += dk_part
            dv_ref[...] += dv_part

        @pl.when(j == 0)
        def _():
            dq_ref[i] = dq_part

        @pl.when(j != 0)
        def _():
            dq_ref[i] += dq_part

    kv = pl.BlockSpec((None, GB_TK, HEAD_DIM), lambda g, j, i: (g, j, 0))
    row = pl.BlockSpec((None, None, 1, rows), lambda g, j, i: (g, i, 0, 0))
    return pl.pallas_call(
        body, name="attn_b_bwd", grid=(KV_B, nk, nq),
        in_specs=[pl.BlockSpec((GROUP, tq, HEAD_DIM), lambda g, j, i: (g, i, 0)), kv, kv,
                  pl.BlockSpec((tq, GROUP * HEAD_DIM), lambda g, j, i: (i, KV_A + g)), row, row],
        out_specs=[pl.BlockSpec((None, nq, HEAD_DIM, rows), lambda g, j, i: (g, 0, 0, 0)), kv, kv],
        out_shape=[jax.ShapeDtypeStruct((KV_B, nq, HEAD_DIM, rows), F32),
                   jax.ShapeDtypeStruct((KV_B, t, HEAD_DIM), F32),
                   jax.ShapeDtypeStruct((KV_B, t, HEAD_DIM), F32)],
        compiler_params=_params(("parallel", "arbitrary", "arbitrary")),
    )(qb, kb, vb, dheads, lse, delta)


def _delta_rows(dheads, heads):
    n = heads.shape[0]
    tq = GB_TQ
    w = GROUP * HEAD_DIM

    def body(a_ref, b_ref, o_ref):
        prod = a_ref[...].astype(F32) * b_ref[...].astype(F32)
        cols = [jnp.sum(prod[:, hh * HEAD_DIM:(hh + 1) * HEAD_DIM].T, axis=0, keepdims=True) for hh in range(GROUP)]
        o_ref[...] = jnp.concatenate(cols, axis=1)

    blk = pl.BlockSpec((tq, w), lambda g, i: (i, KV_A + g))
    return pl.pallas_call(
        body, name="delta_rows", grid=(KV_B, n // tq),
        in_specs=[blk, blk],
        out_specs=pl.BlockSpec((None, None, 1, GROUP * tq), lambda g, i: (g, i, 0, 0)),
        out_shape=jax.ShapeDtypeStruct((KV_B, n // tq, 1, GROUP * tq), F32),
        compiler_params=_params(("parallel", "parallel")),
    )(dheads, heads)


KWIN = 3 * BLOCK


def _window_scores(q, k_ref, j, n, nb, sink_row):
    c = k_ref.shape[0] - n
    start = pl.multiple_of(jnp.clip(j - 1, 0, nb - 3) * BLOCK, BLOCK)
    kw = k_ref[pl.ds(start, KWIN), :]
    kc = k_ref[pl.ds(n, c), :]
    s_loc = lax.dot_general(kw, q, NT, preferred_element_type=F32)
    s_ctx = lax.dot_general(kc, q, NT, preferred_element_type=F32)
    cols = GROUP * BLOCK
    qpos = j * BLOCK + lax.broadcasted_iota(jnp.int32, (KWIN, cols), 1) % BLOCK
    kpos = start + lax.broadcasted_iota(jnp.int32, (KWIN, cols), 0)
    s_loc = jnp.where(jnp.abs(qpos - kpos) <= WINDOW, s_loc, -jnp.inf)
    m = jnp.maximum(jnp.maximum(jnp.max(s_loc, axis=0, keepdims=True), jnp.max(s_ctx, axis=0, keepdims=True)),
                    sink_row)
    e_loc, e_ctx, e_sink = jnp.exp(s_loc - m), jnp.exp(s_ctx - m), jnp.exp(sink_row - m)
    inv = 1.0 / (jnp.sum(e_loc, axis=0, keepdims=True) + jnp.sum(e_ctx, axis=0, keepdims=True) + e_sink)
    return e_loc * inv, e_ctx * inv, e_sink * inv, start


def _sink_row(sink_ref, g):
    return jnp.concatenate([sink_ref[pl.ds(g * GROUP + hh, 1), :] for hh in range(GROUP)], axis=1)


def _attn_a_fwd(qa, ka, va, sink_b, n):
    t = ka.shape[1]
    nb = n // BLOCK
    assert nb >= 3

    def body(q_ref, k_ref, v_ref, sink_ref, o_ref):
        g, j = pl.program_id(0), pl.program_id(1)
        q = q_ref[...].reshape(GROUP * BLOCK, HEAD_DIM)
        p_loc, p_ctx, _, start = _window_scores(q, k_ref, j, n, nb, _sink_row(sink_ref, g))
        vw = v_ref[pl.ds(start, KWIN), :]
        vc = v_ref[pl.ds(n, t - n), :]
        ot = (lax.dot_general(vw, p_loc.astype(BF16), TN, preferred_element_type=F32)
              + lax.dot_general(vc, p_ctx.astype(BF16), TN, preferred_element_type=F32))
        for hh in range(GROUP):
            o_ref[:, hh * HEAD_DIM:(hh + 1) * HEAD_DIM] = ot[:, hh * BLOCK:(hh + 1) * BLOCK].T.astype(BF16)

    return pl.pallas_call(
        body, name="attn_a_fwd", grid=(KV_A, nb),
        in_specs=[pl.BlockSpec((GROUP, BLOCK, HEAD_DIM), lambda g, j: (g, j, 0)),
                  pl.BlockSpec((None, t, HEAD_DIM), lambda g, j: (g, 0, 0)),
                  pl.BlockSpec((None, t, HEAD_DIM), lambda g, j: (g, 0, 0)),
                  pl.BlockSpec((HEADS_A, HEAD_DIM), lambda g, j: (0, 0))],
        out_specs=pl.BlockSpec((BLOCK, GROUP * HEAD_DIM), lambda g, j: (j, g)),
        out_shape=jax.ShapeDtypeStruct((n, HEADS_A * HEAD_DIM), BF16),
        compiler_params=_params(("parallel", "parallel")),
    )(qa, ka, va, sink_b)


def _attn_a_bwd(qa, ka, va, sink_b, dheads, n):
    t = ka.shape[1]
    c = t - n
    nb = n // BLOCK

    def body(q_ref, k_ref, v_ref, sink_ref, do_ref, dq_ref, dk_ref, dv_ref, dsink_ref):
        g, j = pl.program_id(0), pl.program_id(1)

        @pl.when(j == 0)
        def _():
            dk_ref[...] = jnp.zeros_like(dk_ref)
            dv_ref[...] = jnp.zeros_like(dv_ref)
            dsink_ref[...] = jnp.zeros_like(dsink_ref)

        q = q_ref[...].reshape(GROUP * BLOCK, HEAD_DIM)
        do = _heads_rows(do_ref, BLOCK)
        p_loc, p_ctx, p_sink, start = _window_scores(q, k_ref, j, n, nb, _sink_row(sink_ref, g))
        kw, vw = k_ref[pl.ds(start, KWIN), :], v_ref[pl.ds(start, KWIN), :]
        kc, vc = k_ref[pl.ds(n, c), :], v_ref[pl.ds(n, c), :]
        dp_loc = lax.dot_general(vw, do, NT, preferred_element_type=F32)
        dp_ctx = lax.dot_general(vc, do, NT, preferred_element_type=F32)
        dl = jnp.sum(p_loc * dp_loc, axis=0, keepdims=True) + jnp.sum(p_ctx * dp_ctx, axis=0, keepdims=True)
        ds_loc = (p_loc * (dp_loc - dl)).astype(BF16)
        ds_ctx = (p_ctx * (dp_ctx - dl)).astype(BF16)
        dqt = (lax.dot_general(kw, ds_loc, TN, preferred_element_type=F32)
               + lax.dot_general(kc, ds_ctx, TN, preferred_element_type=F32))
        for hh in range(GROUP):
            dq_ref[hh] = dqt[:, hh * BLOCK:(hh + 1) * BLOCK].T
        dk_ref[pl.ds(start, KWIN), :] += lax.dot_general(ds_loc, q, NN, preferred_element_type=F32)
        dv_ref[pl.ds(start, KWIN), :] += lax.dot_general(p_loc.astype(BF16), do, NN, preferred_element_type=F32)
        dk_ref[pl.ds(n, c), :] += lax.dot_general(ds_ctx, q, NN, preferred_element_type=F32)
        dv_ref[pl.ds(n, c), :] += lax.dot_general(p_ctx.astype(BF16), do, NN, preferred_element_type=F32)
        dsk = -(p_sink * dl)
        upd = [jnp.broadcast_to(jnp.sum(dsk[:, hh * BLOCK:(hh + 1) * BLOCK], axis=1, keepdims=True), (1, HEAD_DIM))
               for hh in range(GROUP)]
        dsink_ref[...] += jnp.concatenate(upd + [jnp.zeros((8 - GROUP, HEAD_DIM), F32)], axis=0)

    res = pl.BlockSpec((None, t, HEAD_DIM), lambda g, j: (g, 0, 0))
    return pl.pallas_call(
        body, name="attn_a_bwd", grid=(KV_A, nb),
        in_specs=[pl.BlockSpec((GROUP, BLOCK, HEAD_DIM), lambda g, j: (g, j, 0)), res, res,
                  pl.BlockSpec((HEADS_A, HEAD_DIM), lambda g, j: (0, 0)),
                  pl.BlockSpec((BLOCK, GROUP * HEAD_DIM), lambda g, j: (j, g))],
        out_specs=[pl.BlockSpec((GROUP, BLOCK, HEAD_DIM), lambda g, j: (g, j, 0)), res, res,
                   pl.BlockSpec((None, 8, HEAD_DIM), lambda g, j: (g, 0, 0))],
        out_shape=[jax.ShapeDtypeStruct((HEADS_A, n, HEAD_DIM), F32),
                   jax.ShapeDtypeStruct((KV_A, t, HEAD_DIM), F32),
                   jax.ShapeDtypeStruct((KV_A, t, HEAD_DIM), F32),
                   jax.ShapeDtypeStruct((KV_A, 8, HEAD_DIM), F32)],
        compiler_params=_params(("parallel", "arbitrary")),
    )(qa, ka, va, sink_b, dheads)


def _ln_stats(r):
    mu = jnp.mean(r, axis=-1, keepdims=True)
    cen = r - mu
    rstd = lax.rsqrt(jnp.mean(cen * cen, axis=-1, keepdims=True) + EPS)
    return cen * rstd, rstd


def _ln_bwd(dy, xhat, rstd, gain):
    dxh = dy * gain
    return rstd * (dxh - jnp.mean(dxh, axis=-1, keepdims=True)
                   - xhat * jnp.mean(dxh * xhat, axis=-1, keepdims=True))


def _accumulate_rows(ref, rows, i):
    pad = [jnp.zeros_like(rows[0])] * (8 - len(rows))
    upd = jnp.concatenate(rows + pad, axis=0)

    @pl.when(i == 0)
    def _():
        ref[...] = upd

    @pl.when(i != 0)
    def _():
        ref[...] += upd


def _colsum(v):
    return jnp.sum(v, axis=0, keepdims=True)


LN_TILE = 256


def _res_ln1(x, a, vec):
    n, d = x.shape

    def body(x_ref, a_ref, v_ref, xh_ref, rs_ref, u_ref):
        r1 = DN_ALPHA * x_ref[...] + v_ref[0:1, :] * a_ref[...]
        xhat, rstd = _ln_stats(r1)
        xh_ref[...] = xhat
        rs_ref[...] = rstd
        x1 = xhat * v_ref[1:2, :] + v_ref[2:3, :]
        u_ref[...] = (x1 * (1.0 + v_ref[3:4, :]) + v_ref[4:5, :]).astype(BF16)

    row = pl.BlockSpec((LN_TILE, d), lambda i: (i, 0))
    return pl.pallas_call(
        body, name="res_ln1", grid=(n // LN_TILE,),
        in_specs=[row, row, pl.BlockSpec((8, d), lambda i: (0, 0))],
        out_specs=[row, pl.BlockSpec((LN_TILE, 1), lambda i: (i, 0)), row],
        out_shape=[jax.ShapeDtypeStruct((n, d), F32), jax.ShapeDtypeStruct((n, 1), F32),
                   jax.ShapeDtypeStruct((n, d), BF16)],
        compiler_params=_params(("parallel",)),
    )(x, a, vec)


def _res_ln2_loss(xhat1, f, target, vec):
    n, d = f.shape

    def body(xh_ref, f_ref, t_ref, v_ref, dr_ref, df_ref, s_ref):
        i = pl.program_id(0)
        x1 = xh_ref[...] * v_ref[1:2, :] + v_ref[2:3, :]
        fv = f_ref[...]
        xhat, rstd = _ln_stats(DN_ALPHA * x1 + v_ref[0:1, :] * fv)
        err = xhat * v_ref[3:4, :] + v_ref[4:5, :] - t_ref[...]
        dy = err * (1.0 / d)
        dr2 = _ln_bwd(dy, xhat, rstd, v_ref[3:4, :])
        dr_ref[...] = dr2
        df_ref[...] = (dr2 * v_ref[0:1, :]).astype(BF16)
        _accumulate_rows(s_ref, [_colsum(dy * xhat), _colsum(dy), _colsum(dr2 * fv),
                                 _colsum(err * err) * (0.5 / d)], i)

    row = pl.BlockSpec((LN_TILE, d), lambda i: (i, 0))
    return pl.pallas_call(
        body, name="res_ln2_loss", grid=(n // LN_TILE,),
        in_specs=[row, row, row, pl.BlockSpec((8, d), lambda i: (0, 0))],
        out_specs=[row, row, pl.BlockSpec((8, d), lambda i: (0, 0))],
        out_shape=[jax.ShapeDtypeStruct((n, d), F32), jax.ShapeDtypeStruct((n, d), BF16),
                   jax.ShapeDtypeStruct((8, d), F32)],
        compiler_params=_params(("arbitrary",)),
    )(xhat1, f, target, vec)


def _ln1_bwd(du2, dr2, xhat1, rstd1, a, vec):
    n, d = du2.shape

    def body(du_ref, dr2_ref, xh_ref, rs_ref, a_ref, v_ref, dxp_ref, da_ref, s_ref):
        i = pl.program_id(0)
        du, xhat = du_ref[...], xh_ref[...]
        x1 = xhat * v_ref[1:2, :] + v_ref[2:3, :]
        dx1 = DN_ALPHA * dr2_ref[...] + du * (1.0 + v_ref[0:1, :])
        dr1 = _ln_bwd(dx1, xhat, rs_ref[...], v_ref[1:2, :])
        dxp_ref[...] = DN_ALPHA * dr1
        da_ref[...] = (dr1 * v_ref[3:4, :]).astype(BF16)
        _accumulate_rows(s_ref, [_colsum(du * x1), _colsum(du), _colsum(dx1 * xhat), _colsum(dx1),
                                 _colsum(dr1 * a_ref[...])], i)

    row = pl.BlockSpec((LN_TILE, d), lambda i: (i, 0))
    return pl.pallas_call(
        body, name="ln1_bwd", grid=(n // LN_TILE,),
        in_specs=[row, row, row, pl.BlockSpec((LN_TILE, 1), lambda i: (i, 0)), row,
                  pl.BlockSpec((8, d), lambda i: (0, 0))],
        out_specs=[row, row, pl.BlockSpec((8, d), lambda i: (0, 0))],
        out_shape=[jax.ShapeDtypeStruct((n, d), F32), jax.ShapeDtypeStruct((n, d), BF16),
                   jax.ShapeDtypeStruct((8, d), F32)],
        compiler_params=_params(("arbitrary",)),
    )(du2, dr2, xhat1, rstd1, a, vec)


def _mod1_bwd(du_all, dxp, x, ctx, mods):
    n, d = x.shape
    nx = n // ROW_TILE

    def body(du_ref, dxp_ref, x_ref, ctx_ref, m_ref, gx_ref, s_ref):
        i = pl.program_id(0)
        du = du_ref[...]
        zero = jnp.zeros((1, d), F32)

        @pl.when(i == 0)
        def _():
            s_ref[...] = jnp.zeros_like(s_ref)

        @pl.when(i < nx)
        def _():
            gx_ref[...] = dxp_ref[...] + du * (1.0 + m_ref[0:1, :])
            s_ref[...] += jnp.concatenate([_colsum(du * x_ref[...]), _colsum(du)] + [zero] * 6, axis=0)

        @pl.when(i >= nx)
        def _():
            s_ref[...] += jnp.concatenate([zero, zero, _colsum(du * ctx_ref[...]), _colsum(du)] + [zero] * 4, axis=0)

    lat = pl.BlockSpec((ROW_TILE, d), lambda i: (jnp.minimum(i, nx - 1), 0))
    return pl.pallas_call(
        body, name="mod1_bwd", grid=(nx + 1,),
        in_specs=[pl.BlockSpec((ROW_TILE, d), lambda i: (i, 0)), lat, lat,
                  pl.BlockSpec((ROW_TILE, d), lambda i: (0, 0)), pl.BlockSpec((8, d), lambda i: (0, 0))],
        out_specs=[lat, pl.BlockSpec((8, d), lambda i: (0, 0))],
        out_shape=[jax.ShapeDtypeStruct((n, d), F32), jax.ShapeDtypeStruct((8, d), F32)],
        compiler_params=_params(("arbitrary",)),
    )(du_all, dxp, x, ctx, mods)


FFN_TM = 512
FFN_TN = 512


def _gate_up(u2, wg, wu):
    n, d = u2.shape
    f = wg.shape[1]

    def body(u_ref, wg_ref, wu_ref, g_ref, up_ref, h_ref):
        u = u_ref[...]
        g = lax.dot_general(u, wg_ref[...], NN, preferred_element_type=F32)
        up = lax.dot_general(u, wu_ref[...], NN, preferred_element_type=F32)
        g_ref[...] = g
        up_ref[...] = up
        h_ref[...] = (g * jax.nn.sigmoid(g) * up).astype(BF16)

    wspec = pl.BlockSpec((d, FFN_TN), lambda j, i: (0, j))
    ospec = pl.BlockSpec((FFN_TM, FFN_TN), lambda j, i: (i, j))
    return pl.pallas_call(
        body, name="gate_up", grid=(f // FFN_TN, n // FFN_TM),
        in_specs=[pl.BlockSpec((FFN_TM, d), lambda j, i: (i, 0)), wspec, wspec],
        out_specs=[ospec, ospec, ospec],
        out_shape=[jax.ShapeDtypeStruct((n, f), F32), jax.ShapeDtypeStruct((n, f), F32),
                   jax.ShapeDtypeStruct((n, f), BF16)],
        compiler_params=_params(("parallel", "parallel")),
    )(u2, wg, wu)


def _glu_bwd(df, wd, g, u):
    n, d = df.shape
    f = wd.shape[0]

    def body(df_ref, wd_ref, g_ref, u_ref, dg_ref, du_ref):
        dh = lax.dot_general(df_ref[...], wd_ref[...], NT, preferred_element_type=F32)
        gv = g_ref[...]
        sig = jax.nn.sigmoid(gv)
        du_ref[...] = (dh * (gv * sig)).astype(BF16)
        dg_ref[...] = (dh * u_ref[...] * (sig * (1.0 + gv * (1.0 - sig)))).astype(BF16)

    tm = min(2 * FFN_TM, n)
    ospec = pl.BlockSpec((tm, FFN_TN), lambda i, j: (i, j))
    return pl.pallas_call(
        body, name="glu_bwd", grid=(n // tm, f // FFN_TN),
        in_specs=[pl.BlockSpec((tm, d), lambda i, j: (i, 0)),
                  pl.BlockSpec((FFN_TN, d), lambda i, j: (j, 0)), ospec, ospec],
        out_specs=[ospec, ospec],
        out_shape=[jax.ShapeDtypeStruct((n, f), BF16), jax.ShapeDtypeStruct((n, f), BF16)],
        compiler_params=_params(("parallel", "parallel")),
    )(df, wd, g, u)


def _du2(dg, du, wg, wu):
    n, f = dg.shape
    d = wg.shape[0]
    tm, tn, tk = min(1024, n), 1024, 1408
    nk = f // tk

    def body(dg_ref, du_ref, wg_ref, wu_ref, o_ref, acc_ref):
        kk = pl.program_id(2)
        part = (lax.dot_general(dg_ref[...], wg_ref[...], NT, preferred_element_type=F32)
                + lax.dot_general(du_ref[...], wu_ref[...], NT, preferred_element_type=F32))

        @pl.when(kk == 0)
        def _():
            acc_ref[...] = part

        @pl.when(kk != 0)
        def _():
            acc_ref[...] += part

        @pl.when(kk == nk - 1)
        def _():
            o_ref[...] = acc_ref[...]

    aspec = pl.BlockSpec((tm, tk), lambda i, j, kk: (i, kk))
    wspec = pl.BlockSpec((tn, tk), lambda i, j, kk: (j, kk))
    return pl.pallas_call(
        body, name="du2", grid=(n // tm, d // tn, nk),
        in_specs=[aspec, aspec, wspec, wspec],
        out_specs=pl.BlockSpec((tm, tn), lambda i, j, kk: (i, j)),
        out_shape=jax.ShapeDtypeStruct((n, d), F32),
        scratch_shapes=[pltpu.VMEM((tm, tn), F32)],
        compiler_params=_params(("parallel", "parallel", "arbitrary")),
    )(dg, du, wg, wu)


def _rows8(rows, d=D_MODEL):
    rows = [r.reshape(1, d).astype(F32) for r in rows]
    return jnp.concatenate(rows + [jnp.zeros((8 - len(rows), d), F32)], axis=0)


W_GROUPS = (("w_in",), ("w_out", "w_gate", "w_up"), ("w_down",))
G_GROUPS = (("w_down", "w_gate", "w_up"), ("w_out",), ("w_in",))


def _layer_fwd_bwd(x, ctx, target, mod, mod_ctx, weights, grads_out, q_g, k_g, sink, ln1_g, ln1_b, ln2_g, ln2_b):
    n, d = x.shape
    c = ctx.shape[0]
    sh1, sc1, g1, sh2, sc2, g2 = [mod[:, k * d:(k + 1) * d] for k in range(6)]
    csh1, csc1 = mod_ctx[:, 0:d], mod_ctx[:, d:2 * d]
    cos, sin_a, sin_b = _rope_tables(n, c)
    sink_b = jnp.broadcast_to(sink.reshape(HEADS_A, 1), (HEADS_A, HEAD_DIM)).astype(F32)

    u_all = _modulate_rows(x, ctx, _rows8([sc1, sh1, csc1, csh1]))
    (w_in,) = weights(0, u_all)
    h_all = _matmul(u_all, w_in, name="qkv_proj", tm=_fit(n + c, 1088), tn=1024, tk=2048, out_dtype=F32)
    qa, ka, va, qb, kb, vb = _qkv_post(h_all, cos, sin_a, sin_b, q_g, k_g)
    o_a = _attn_a_fwd(qa, ka, va, sink_b, n)
    o_b, lse = _attn_b_fwd(qb, kb, vb, n)
    heads = jnp.concatenate([o_a, o_b], axis=1)
    w_out, w_gate, w_up = weights(1, heads)
    a = _matmul(heads, w_out, name="out_proj", tm=1024, tn=1024, tk=2048, out_dtype=F32)
    xhat1, rstd1, u2 = _res_ln1(x, a, _rows8([g1, ln1_g, ln1_b, sc2, sh2]))
    gg, uu, hh = _gate_up(u2, w_gate, w_up)
    (w_down,) = weights(2, hh)
    f = _matmul(hh, w_down, name="ffn_down", tm=1024, tn=1024, tk=1408, out_dtype=F32)
    dr2, df, s_ln2 = _res_ln2_loss(xhat1, f, target, _rows8([g2, ln1_g, ln1_b, ln2_g, ln2_b]))

    dgg, duu = _glu_bwd(df, w_down, gg, uu)
    dw_down = _matmul(hh, df, name="dw_down", ta=True, tm=512, tn=1024, tk=n, out_dtype=BF16)
    dw_gate = _matmul(u2, dgg, name="dw_gate", ta=True, tm=1024, tn=512, tk=n, out_dtype=BF16)
    dw_up = _matmul(u2, duu, name="dw_up", ta=True, tm=1024, tn=512, tk=n, out_dtype=BF16)
    zero = grads_out(0, [dw_down, dw_gate, dw_up])
    du2 = _du2(dgg, duu, w_gate, w_up)
    dxp, da, s_ln1 = _ln1_bwd(du2, dr2, xhat1, rstd1, a, _rows8([sc2, ln1_g, ln1_b, g1]) + zero)

    dheads = _matmul(da, w_out, name="d_heads", tb=True, tm=1024, tn=1024, tk=2048, out_dtype=BF16)
    dw_out = _matmul(heads, da, name="dw_out", ta=True, tm=1024, tn=1024, tk=n, out_dtype=BF16)
    zero = grads_out(1, [dw_out])
    delta = _delta_rows(dheads, heads)
    dqa, dka, dva, dsink = _attn_a_bwd(qa, ka, va, sink_b + zero, dheads, n)
    dqb, dkb, dvb = _attn_b_bwd(qb, kb, vb, dheads, lse, delta, n)
    dh_all, s_gain = _qkv_bwd_post(h_all, cos, sin_a, sin_b, q_g, k_g, dqa, dka, dva, dqb, dkb, dvb, n)
    dw_in = _matmul(u_all, dh_all, name="dw_in", ta=True, tm=1024, tn=1024, tk=n + c, out_dtype=BF16)
    zero = grads_out(2, [dw_in])
    du_all = _matmul(dh_all, w_in, name="d_u1", tb=True, tm=_fit(n + c, 1088), tn=1024, tk=IN_WIDTH, out_dtype=F32)
    grad_x, s_mod1 = _mod1_bwd(du_all, dxp, x, ctx, _rows8([sc1]) + zero)

    dsink_row = jnp.concatenate([dsink[0, 0:GROUP, 0], dsink[1, 0:GROUP, 0]]).reshape(1, HEADS_A)
    misc = jnp.concatenate([s_gain[0:1], s_gain[1:2], dsink_row,
                            jnp.zeros((1, d - 2 * HEAD_DIM - HEADS_A), F32)], axis=1)
    partial = jnp.concatenate([
        s_mod1[1:2], s_mod1[0:1], s_ln1[4:5],
        s_ln1[1:2], s_ln1[0:1], s_ln2[2:3],
        s_mod1[3:4], s_mod1[2:3],
        s_ln1[2:3], s_ln1[3:4], s_ln2[0:1], s_ln2[1:2],
        s_ln2[3:4], misc, jnp.zeros((2, d), F32)], axis=0)
    return grad_x, partial


ANY = pl.BlockSpec(memory_space=pl.ANY)
VMEM_FULL = pl.BlockSpec(memory_space=pltpu.VMEM)
N_CHIP_PEERS = 3


def _me():
    return lax.axis_index("x"), lax.axis_index("y"), lax.axis_index("c")


def _other_chips(x, y):
    return [(1 - x, y), (x, 1 - y), (1 - x, 1 - y)]


def _shard_of(chip):
    return 2 * chip[0] + chip[1]


def _dev_index(x, y, c):
    return 4 * x + 2 * y + c


def _rcopy(src, dst, send_sems, recv_sems, k, dev):
    return pltpu.make_async_remote_copy(src_ref=src, dst_ref=dst, send_sem=send_sems.at[k], recv_sem=recv_sems.at[k],
                                        device_id=dev, device_id_type=MESH)


BIG = (("w_in", (D_MODEL, IN_WIDTH), 1), ("w_out", (MIX_WIDTH, D_MODEL), 0), ("w_gate", (D_MODEL, FFN), 1),
       ("w_up", (D_MODEL, FFN), 1), ("w_down", (FFN, D_MODEL), 0))


def _sub(ref, axis, idx, size):
    start = pl.multiple_of(idx * size, size)
    return ref.at[pl.ds(start, size), :] if axis == 0 else ref.at[:, pl.ds(start, size)]


def _shape_div(shape, axis, parts):
    return tuple(s // parts if a == axis else s for a, s in enumerate(shape))


def _piece(a, ref, shard, half):
    _, full, axis = BIG[a]
    view = _sub(ref, axis, shard, full[axis] // N_SHARD)
    return _sub(view, 1 - axis, half, full[1 - axis] // 2)


HBM = pl.BlockSpec(memory_space=pltpu.HBM)
SEM = pl.BlockSpec(memory_space=pltpu.SEMAPHORE)
EFFECT = pltpu.SideEffectType.DATAFLOW_SIDE_EFFECTING
BIG_INDEX = {name: a for a, (name, _, _) in enumerate(BIG)}


def _in_hbm(arr):
    return pltpu.with_memory_space_constraint(arr, pltpu.HBM)


def _gather_start(tag, arrs, bufs, prev):
    n_arr = len(arrs)

    def body(*refs):
        ins = refs[:n_arr]
        send_sems, recv_sems = refs[n_arr + 1], refs[n_arr + 2]
        token = refs[-1]
        x, y, c = _me()
        s_me = _shard_of((x, y))
        for i, a in enumerate(arrs):
            mine = _piece(a, ins[i], s_me, c)
            for j, chip in enumerate(_other_chips(x, y)):
                _rcopy(mine, mine, send_sems, recv_sems, N_CHIP_PEERS * i + j, (*chip, c)).start()
        token[...] = jnp.zeros_like(token)

    n_sem = N_CHIP_PEERS * n_arr
    outs = pl.pallas_call(
        body, name="gather_start_" + tag,
        in_specs=[HBM] * n_arr + [ANY],
        out_specs=[SEM, SEM] + [HBM] * n_arr + [VMEM_FULL],
        out_shape=[pltpu.SemaphoreType.DMA((n_sem,)), pltpu.SemaphoreType.DMA((n_sem,))]
        + [pltpu.HBM(BIG[a][1], BF16) for a in arrs] + [jax.ShapeDtypeStruct((8, HEAD_DIM), F32)],
        input_output_aliases={i: 2 + i for i in range(n_arr)},
        compiler_params=pltpu.CompilerParams(has_side_effects=EFFECT),
    )(*[_in_hbm(b) for b in bufs], prev)
    return outs[0], outs[1], list(outs[2:2 + n_arr]), outs[-1]


def _gather_wait(tag, arrs, send_sems, recv_sems, bufs, after):
    n_arr = len(arrs)

    def body(*refs):
        ins = refs[:n_arr]
        send_sems_, recv_sems_ = refs[n_arr], refs[n_arr + 1]
        x, y, c = _me()
        s_me = _shard_of((x, y))
        for i, a in enumerate(arrs):
            mine = _piece(a, ins[i], s_me, c)
            for j, chip in enumerate(_other_chips(x, y)):
                landed = _piece(a, ins[i], _shard_of(chip), c)
                cp = _rcopy(mine, landed, send_sems_, recv_sems_, N_CHIP_PEERS * i + j, (*chip, c))
                cp.wait_send()
                cp.wait_recv()

    outs = pl.pallas_call(
        body, name="gather_wait_" + tag,
        in_specs=[HBM] * n_arr + [SEM, SEM, ANY],
        out_specs=[HBM] * n_arr,
        out_shape=[pltpu.HBM(BIG[a][1], BF16) for a in arrs],
        input_output_aliases={i: i for i in range(n_arr)},
        compiler_params=pltpu.CompilerParams(has_side_effects=EFFECT),
    )(*bufs, send_sems, recv_sems, after)
    return list(outs)


def _gather_forward(tag, arrs, bufs):
    n_arr = len(arrs)

    def body(*refs):
        outs = refs[n_arr:2 * n_arr]
        send_sems, recv_sems = refs[2 * n_arr:]
        x, y, c = _me()
        sibling = (x, y, 1 - c)
        chips = _other_chips(x, y)
        copies = []
        for i, a in enumerate(arrs):
            for j, chip in enumerate(chips):
                landed = _piece(a, outs[i], _shard_of(chip), c)
                cp = _rcopy(landed, landed, send_sems, recv_sems, N_CHIP_PEERS * i + j, sibling)
                cp.start()
                copies.append(cp)
        for i, a in enumerate(arrs):
            for j, chip in enumerate(chips):
                other = _piece(a, outs[i], _shard_of(chip), 1 - c)
                _rcopy(other, other, send_sems, recv_sems, N_CHIP_PEERS * i + j, sibling).wait_recv()
        for cp in copies:
            cp.wait_send()

    n_sem = N_CHIP_PEERS * n_arr
    return list(pl.pallas_call(
        body, name="gather_forward_" + tag,
        in_specs=[ANY] * n_arr, out_specs=[ANY] * n_arr,
        out_shape=[jax.ShapeDtypeStruct(BIG[a][1], BF16) for a in arrs],
        input_output_aliases={i: i for i in range(n_arr)},
        scratch_shapes=[pltpu.SemaphoreType.DMA((n_sem,)), pltpu.SemaphoreType.DMA((n_sem,))],
    )(*bufs))


def _peers(x, y, c):
    return [(x ^ (mask >> 2), y ^ ((mask >> 1) & 1), c ^ (mask & 1)) for mask in range(1, N_DEV)]


def _received_shape(a):
    _, full, axis = BIG[a]
    return (N_DEV - 1,) + _shape_div(_shape_div(full, 1 - axis, 2), axis, N_SHARD)


def _pieces_start(tag, arrs, dws):
    n_arr = len(arrs)

    def body(*refs):
        srcs, lands = refs[:n_arr], refs[n_arr:2 * n_arr]
        send_sems, recv_sems = refs[2 * n_arr], refs[2 * n_arr + 1]
        token = refs[-1]
        x, y, c = _me()
        for i, a in enumerate(arrs):
            for k, peer in enumerate(_peers(x, y, c)):
                src = _piece(a, srcs[i], _shard_of(peer[:2]), peer[2])
                _rcopy(src, lands[i].at[k], send_sems, recv_sems, (N_DEV - 1) * i + k, peer).start()
        token[...] = jnp.zeros_like(token)

    n_sem = (N_DEV - 1) * n_arr
    lands = [_in_hbm(lax.empty(_received_shape(a), BF16)) for a in arrs]
    outs = pl.pallas_call(
        body, name="grad_pieces_start_" + tag,
        in_specs=[HBM] * (2 * n_arr),
        out_specs=[SEM, SEM] + [HBM] * (2 * n_arr) + [VMEM_FULL],
        out_shape=[pltpu.SemaphoreType.DMA((n_sem,)), pltpu.SemaphoreType.DMA((n_sem,))]
        + [pltpu.HBM(BIG[a][1], BF16) for a in arrs] + [pltpu.HBM(_received_shape(a), BF16) for a in arrs]
        + [jax.ShapeDtypeStruct((8, HEAD_DIM), F32)],
        input_output_aliases={i: 2 + i for i in range(2 * n_arr)},
        compiler_params=pltpu.CompilerParams(has_side_effects=EFFECT),
    )(*[_in_hbm(dw) for dw in dws], *lands)
    return outs[0], outs[1], list(outs[2:2 + n_arr]), list(outs[2 + n_arr:2 + 2 * n_arr]), outs[-1]


def _pieces_wait(tag, arrs, send_sems, recv_sems, dws, lands, after):
    n_arr = len(arrs)

    def body(*refs):
        srcs, lands_ = refs[:n_arr], refs[n_arr:2 * n_arr]
        send_sems_, recv_sems_ = refs[2 * n_arr], refs[2 * n_arr + 1]
        x, y, c = _me()
        for i, a in enumerate(arrs):
            for k, peer in enumerate(_peers(x, y, c)):
                src = _piece(a, srcs[i], _shard_of(peer[:2]), peer[2])
                cp = _rcopy(src, lands_[i].at[k], send_sems_, recv_sems_, (N_DEV - 1) * i + k, peer)
                cp.wait_send()
                cp.wait_recv()

    outs = pl.pallas_call(
        body, name="grad_pieces_wait_" + tag,
        in_specs=[HBM] * (2 * n_arr) + [SEM, SEM, ANY],
        out_specs=[HBM] * (2 * n_arr),
        out_shape=[pltpu.HBM(BIG[a][1], BF16) for a in arrs] + [pltpu.HBM(_received_shape(a), BF16) for a in arrs],
        input_output_aliases={i: i for i in range(2 * n_arr)},
        compiler_params=pltpu.CompilerParams(has_side_effects=EFFECT),
    )(*dws, *lands, send_sems, recv_sems, after)
    return list(outs[:n_arr]), list(outs[n_arr:])


def _join_halves(tag, g_halves):
    n_arr = len(g_halves)

    def body(*refs):
        ins, outs = refs[:n_arr], refs[n_arr:2 * n_arr]
        send_sems, recv_sems = refs[2 * n_arr:]
        x, y, c = _me()
        copies = []
        for i in range(n_arr):
            cp = _rcopy(ins[i], outs[i], send_sems, recv_sems, i, (x, y, 1 - c))
            cp.start()
            copies.append(cp)
        for cp in copies:
            cp.wait()

    return list(pl.pallas_call(
        body, name="grad_join_halves_" + tag,
        in_specs=[ANY] * n_arr, out_specs=[ANY] * n_arr,
        out_shape=[jax.ShapeDtypeStruct(g.shape, F32) for g in g_halves],
        scratch_shapes=[pltpu.SemaphoreType.DMA((n_arr,)), pltpu.SemaphoreType.DMA((n_arr,))],
    )(*g_halves))


def _piece_sum(a, dw, shard, core, received):
    name, full, axis = BIG[a]
    rows, cols = _received_shape(a)[1:]
    tr = _fit(rows, ROW_TILE)
    nbr = rows // tr

    def body(w_ref, dw_ref, rec_ref, o_ref):
        acc = dw_ref[...].astype(F32)
        for k in range(N_DEV - 1):
            acc = acc + rec_ref[k].astype(F32)
        o_ref[...] = acc

    if axis == 0:
        own = pl.BlockSpec((tr, cols), lambda i, w: (w[0] * nbr + i, w[1]))
    else:
        own = pl.BlockSpec((tr, cols), lambda i, w: (w[1] * nbr + i, w[0]))
    return pl.pallas_call(
        body, name="grad_sum_pieces_" + name,
        grid_spec=pltpu.PrefetchScalarGridSpec(
            num_scalar_prefetch=1, grid=(nbr,),
            in_specs=[own, pl.BlockSpec((N_DEV - 1, tr, cols), lambda i, w: (0, i, 0))],
            out_specs=pl.BlockSpec((tr, cols), lambda i, w: (i, 0))),
        out_shape=jax.ShapeDtypeStruct((rows, cols), F32),
        compiler_params=_params(("parallel",)),
    )(jnp.stack([shard, core]).astype(jnp.int32), dw, received)


def _scatter_begin(tag, arrs, dws):
    send_sems, recv_sems, dws, lands, token = _pieces_start(tag, arrs, dws)
    return (send_sems, recv_sems, dws, lands), token[0, 0]


def _scatter_end(tag, arrs, state, after):
    x, y, c = _me()
    send_sems, recv_sems, dws, lands = state
    dws, lands = _pieces_wait(tag, arrs, send_sems, recv_sems, dws, lands, after)
    g_own = [_piece_sum(a, dw, _shard_of((x, y)), c, r) for a, dw, r in zip(arrs, dws, lands)]
    return g_own, _join_halves(tag, g_own)


def _gather_rows(block, name):
    r, d = block.shape

    def body(in_ref, out_ref, send_sems, recv_sems):
        x, y, c = _me()
        out_ref[_dev_index(x, y, c)] = in_ref[...]
        copies = []
        for mask in range(1, N_DEV):
            peer = (x ^ (mask >> 2), y ^ ((mask >> 1) & 1), c ^ (mask & 1))
            cp = _rcopy(in_ref, out_ref.at[_dev_index(x, y, c)], send_sems, recv_sems, mask - 1, peer)
            cp.start()
            copies.append((cp, peer))
        for mask in range(1, N_DEV):
            peer = (x ^ (mask >> 2), y ^ ((mask >> 1) & 1), c ^ (mask & 1))
            landed = out_ref.at[_dev_index(*peer)]
            _rcopy(landed, landed, send_sems, recv_sems, mask - 1, peer).wait_recv()
        for cp, _ in copies:
            cp.wait_send()

    return pl.pallas_call(
        body, name=name, in_specs=[VMEM_FULL], out_specs=VMEM_FULL,
        out_shape=jax.ShapeDtypeStruct((N_DEV, r, d), F32),
        scratch_shapes=[pltpu.SemaphoreType.DMA((N_DEV - 1,)), pltpu.SemaphoreType.DMA((N_DEV - 1,))],
    )(block)


ADA_ROWS = 80
ADA_W = 6 * D_MODEL // N_SHARD


def _ada_forward(c_block, cctx_block, w_ada, b_shard):
    d = c_block.shape[1]

    def body(c_ref, cc_ref, w_ref, b_ref, act_ref, mods_ref, raw, mloc, send_sems, recv_sems):
        x, y, c = _me()
        me = _dev_index(x, y, c)
        s_me = _shard_of((x, y))
        raw[72:ADA_ROWS, :] = jnp.zeros((ADA_ROWS - 72, d), F32)
        raw[pl.ds(pl.multiple_of(me * 8, 8), 8), :] = c_ref[...]
        raw[64:72, :] = cc_ref[...]
        sends = []
        for mask in range(1, N_DEV):
            peer = (x ^ (mask >> 2), y ^ ((mask >> 1) & 1), c ^ (mask & 1))
            cp = _rcopy(c_ref, raw.at[pl.ds(pl.multiple_of(me * 8, 8), 8), :], send_sems, recv_sems, mask - 1, peer)
            cp.start()
            sends.append(cp)
        for mask in range(1, N_DEV):
            peer = (x ^ (mask >> 2), y ^ ((mask >> 1) & 1), c ^ (mask & 1))
            landed = raw.at[pl.ds(pl.multiple_of(_dev_index(*peer) * 8, 8), 8), :]
            _rcopy(landed, landed, send_sems, recv_sems, mask - 1, peer).wait_recv()
        v = raw[...]
        act = v * jax.nn.sigmoid(v)
        act_ref[...] = act
        mloc[...] = lax.dot_general(act.astype(BF16), w_ref[...].astype(BF16), NN,
                                    preferred_element_type=F32) + b_ref[...]
        mods_ref[s_me, 0:8, :] = mloc[pl.ds(pl.multiple_of(me * 8, 8), 8), :]
        mods_ref[s_me, 8:16, :] = mloc[64:72, :]
        base = N_DEV - 1
        for j, chip in enumerate(_other_chips(x, y)):
            peer = (*chip, c)
            rows = mloc.at[pl.ds(pl.multiple_of(_dev_index(*peer) * 8, 8), 8), :]
            cp = _rcopy(rows, mods_ref.at[s_me, 0:8, :], send_sems, recv_sems, base + 2 * j, peer)
            cp.start()
            sends.append(cp)
            cp = _rcopy(mloc.at[64:72, :], mods_ref.at[s_me, 8:16, :], send_sems, recv_sems, base + 2 * j + 1, peer)
            cp.start()
            sends.append(cp)
        for j, chip in enumerate(_other_chips(x, y)):
            for part in range(2):
                landed = mods_ref.at[_shard_of(chip), 8 * part:8 * part + 8, :]
                _rcopy(landed, landed, send_sems, recv_sems, base + 2 * j + part, (*chip, c)).wait_recv()
        for cp in sends:
            cp.wait_send()

    n_sem = N_DEV - 1 + 2 * N_CHIP_PEERS
    return pl.pallas_call(
        body, name="ada_forward",
        in_specs=[VMEM_FULL] * 4, out_specs=[VMEM_FULL, VMEM_FULL],
        out_shape=[jax.ShapeDtypeStruct((ADA_ROWS, d), F32), jax.ShapeDtypeStruct((N_SHARD, 16, ADA_W), F32)],
        scratch_shapes=[pltpu.VMEM((ADA_ROWS, d), F32), pltpu.VMEM((ADA_ROWS, ADA_W), F32),
                        pltpu.SemaphoreType.DMA((n_sem,)), pltpu.SemaphoreType.DMA((n_sem,))],
        compiler_params=pltpu.CompilerParams(vmem_limit_bytes=VMEM_LIMIT),
    )(c_block, cctx_block, w_ada, b_shard)


def _small_reduce(gathered):
    d = gathered.shape[2]

    def body(g_ref, o_ref):
        tot = g_ref[0]
        for i in range(1, N_DEV):
            tot = tot + g_ref[i]
        o_ref[...] = tot
        o_ref[0:2, :] = tot[0:2] + tot[6:8]
        o_ref[12:13, :] = jnp.broadcast_to(jnp.sum(tot[12:13], axis=1, keepdims=True), (1, d))

    return pl.pallas_call(body, name="small_reduce", in_specs=[VMEM_FULL], out_specs=VMEM_FULL,
                          out_shape=jax.ShapeDtypeStruct((16, d), F32))(gathered)


def _cctx_grad(gathered, c_ctx):
    d = gathered.shape[2]

    def body(g_ref, c_ref, o_ref):
        tot = g_ref[0, 0:1, :]
        for chip in range(1, N_SHARD):
            tot = tot + g_ref[2 * chip, 0:1, :]
        v = c_ref[...]
        sig = jax.nn.sigmoid(v)
        o_ref[...] = tot * (sig * (1.0 + v * (1.0 - sig)))

    return pl.pallas_call(body, name="cctx_grad", in_specs=[VMEM_FULL, VMEM_FULL], out_specs=VMEM_FULL,
                          out_shape=jax.ShapeDtypeStruct((1, d), F32))(gathered, c_ctx.reshape(1, d))


def _cast_into_full(w, shard, full, axis, name):
    r, cdim = w.shape
    tr = _fit(r, ROW_TILE)
    nbr = r // tr

    def body(s_ref, w_ref, o_ref):
        o_ref[...] = w_ref[...].astype(BF16)

    if axis == 0:
        out_spec = pl.BlockSpec((tr, cdim), lambda i, s: (s[0] * nbr + i, 0))
    else:
        out_spec = pl.BlockSpec((tr, cdim), lambda i, s: (i, s[0]))
    return pl.pallas_call(
        body, name=name,
        grid_spec=pltpu.PrefetchScalarGridSpec(
            num_scalar_prefetch=1, grid=(nbr,), in_specs=[pl.BlockSpec((tr, cdim), lambda i, s: (i, 0))],
            out_specs=out_spec),
        out_shape=jax.ShapeDtypeStruct(full, BF16), compiler_params=_params(("parallel",)),
    )(shard.reshape(1).astype(jnp.int32), w)


def _adamw_halves(w, g_own, g_other, m, v, core, axis, name):
    r, cdim = w.shape
    hr, hc = (r // 2, cdim) if axis == 1 else (r, cdim // 2)
    assert g_own.shape == (hr, hc) and g_other.shape == (hr, hc)
    tr = _fit(hr, 128)
    nb = hr // tr
    c1 = 1.0 - ADAM_B1 ** ADAM_STEP
    c2 = 1.0 - ADAM_B2 ** ADAM_STEP

    def body(c_ref, w_ref, go_ref, gt_ref, m_ref, v_ref, g_ref, d_ref, nm_ref, nv_ref):
        gv = jnp.where(pl.program_id(0) == c_ref[0], go_ref[...], gt_ref[...])
        nm = ADAM_B1 * m_ref[...] + (1.0 - ADAM_B1) * gv
        nv = ADAM_B2 * v_ref[...] + (1.0 - ADAM_B2) * (gv * gv)
        g_ref[...] = gv
        nm_ref[...] = nm
        nv_ref[...] = nv
        d_ref[...] = -ADAM_LR * ((nm / c1) / (jnp.sqrt(nv / c2) + ADAM_EPS) + ADAM_WD * w_ref[...])

    if axis == 1:
        big = pl.BlockSpec((tr, hc), lambda p, i, c: (p * nb + i, 0))
    else:
        big = pl.BlockSpec((tr, hc), lambda p, i, c: (i, p))
    half = pl.BlockSpec((tr, hc), lambda p, i, c: (i, 0))
    sh = jax.ShapeDtypeStruct((r, cdim), F32)
    return pl.pallas_call(
        body, name=name,
        grid_spec=pltpu.PrefetchScalarGridSpec(
            num_scalar_prefetch=1, grid=(2, nb), in_specs=[big, half, half, big, big], out_specs=[big] * 4),
        out_shape=[sh] * 4, compiler_params=_params(("parallel", "parallel")),
    )(core.reshape(1).astype(jnp.int32), w, g_own, g_other, m, v)


def _adamw(w, g, m, v, name):
    r, cdim = w.shape
    tr = _fit(r, 128) if r % (ROW_TILE // 4) == 0 else r
    c1 = 1.0 - ADAM_B1 ** ADAM_STEP
    c2 = 1.0 - ADAM_B2 ** ADAM_STEP

    def body(w_ref, g_ref, m_ref, v_ref, d_ref, nm_ref, nv_ref):
        gv = g_ref[...]
        nm = ADAM_B1 * m_ref[...] + (1.0 - ADAM_B1) * gv
        nv = ADAM_B2 * v_ref[...] + (1.0 - ADAM_B2) * (gv * gv)
        nm_ref[...] = nm
        nv_ref[...] = nv
        d_ref[...] = -ADAM_LR * ((nm / c1) / (jnp.sqrt(nv / c2) + ADAM_EPS) + ADAM_WD * w_ref[...])

    spec = pl.BlockSpec((tr, cdim), lambda i: (i, 0))
    sh = jax.ShapeDtypeStruct((r, cdim), F32)
    return pl.pallas_call(body, name=name, grid=(r // tr,), in_specs=[spec] * 4, out_specs=[spec] * 3,
                          out_shape=[sh, sh, sh], compiler_params=_params(("parallel",)))(w, g, m, v)


SMALL = (("c_ctx", D_MODEL), ("b_ada", 6 * D_MODEL), ("q_norm_g", HEAD_DIM), ("k_norm_g", HEAD_DIM),
         ("sink_logit", HEADS_A), ("ln1_g", D_MODEL), ("ln1_b", D_MODEL), ("ln2_g", D_MODEL), ("ln2_b", D_MODEL))
WEIGHT_ORDER = ("c_ctx", "w_ada", "b_ada", "w_in", "q_norm_g", "k_norm_g", "sink_logit", "w_out", "ln1_g", "ln1_b",
                "w_gate", "w_up", "w_down", "ln2_g", "ln2_b")


def kernel(x, c, ctx, c_ctx, w_ada, b_ada, w_in, q_norm_g, k_norm_g, sink_logit, w_out, ln1_g, ln1_b, w_gate, w_up, w_down, ln2_g, ln2_b, loss_target, m_c_ctx, m_w_ada, m_b_ada, m_w_in, m_q_norm_g, m_k_norm_g, m_sink_logit, m_w_out, m_ln1_g, m_ln1_b, m_w_gate, m_w_up, m_w_down, m_ln2_g, m_ln2_b, v_c_ctx, v_w_ada, v_b_ada, v_w_in, v_q_norm_g, v_k_norm_g, v_sink_logit, v_w_out, v_ln1_g, v_ln1_b, v_w_gate, v_w_up, v_w_down, v_ln2_g, v_ln2_b):
    d = D_MODEL
    w = dict(c_ctx=c_ctx, w_ada=w_ada[0], b_ada=b_ada, w_in=w_in[0], q_norm_g=q_norm_g, k_norm_g=k_norm_g,
             sink_logit=sink_logit, w_out=w_out[0], ln1_g=ln1_g, ln1_b=ln1_b, w_gate=w_gate[0], w_up=w_up[0],
             w_down=w_down[0], ln2_g=ln2_g, ln2_b=ln2_b)
    m = dict(c_ctx=m_c_ctx, w_ada=m_w_ada[0], b_ada=m_b_ada, w_in=m_w_in[0], q_norm_g=m_q_norm_g, k_norm_g=m_k_norm_g,
             sink_logit=m_sink_logit, w_out=m_w_out[0], ln1_g=m_ln1_g, ln1_b=m_ln1_b, w_gate=m_w_gate[0],
             w_up=m_w_up[0], w_down=m_w_down[0], ln2_g=m_ln2_g, ln2_b=m_ln2_b)
    v = dict(c_ctx=v_c_ctx, w_ada=v_w_ada[0], b_ada=v_b_ada, w_in=v_w_in[0], q_norm_g=v_q_norm_g, k_norm_g=v_k_norm_g,
             sink_logit=v_sink_logit, w_out=v_w_out[0], ln1_g=v_ln1_g, ln1_b=v_ln1_b, w_gate=v_w_gate[0],
             w_up=v_w_up[0], w_down=v_w_down[0], ln2_g=v_ln2_g, ln2_b=v_ln2_b)
    mx, my, mc = _me()
    s_me = _shard_of((mx, my))
    me = _dev_index(mx, my, mc)
    pad8 = lambda row: jnp.concatenate([row.reshape(1, -1), jnp.zeros((7, row.size), F32)], axis=0)

    b_shard = lax.dynamic_slice(b_ada, (0, s_me * ADA_W), (1, ADA_W))
    act, mods4 = _ada_forward(pad8(c), pad8(c_ctx), w["w_ada"], b_shard)

    bufs = {name: _cast_into_full(w[name], s_me, shape, axis, "cast_" + name) for name, shape, axis in BIG}
    gathers = []
    prev = mods4
    for k, names in enumerate(W_GROUPS):
        arrs = tuple(BIG_INDEX[name] for name in names)
        send_sems, recv_sems, thru, prev = _gather_start("g%d" % k, arrs, [bufs[name] for name in names], prev)
        gathers.append((arrs, send_sems, recv_sems, thru))

    def weights(k, after):
        arrs, send_sems, recv_sems, thru = gathers[k]
        landed = _gather_wait("g%d" % k, arrs, send_sems, recv_sems, thru, after)
        return _gather_forward("g%d" % k, arrs, landed)

    mod = jnp.transpose(mods4[:, 0:1, :], (1, 0, 2)).reshape(1, 6 * d) + prev[0, 0]
    mod_ctx = jnp.transpose(mods4[:, 8:9, :], (1, 0, 2)).reshape(1, 6 * d)

    scatters = {}

    def grads_out(k, dws):
        arrs = tuple(BIG_INDEX[name] for name in G_GROUPS[k])
        scatters[k], zero = _scatter_begin("g%d" % k, arrs, dws)
        return zero

    grad_x, partial = _layer_fwd_bwd(x[0], ctx[0], loss_target[0], mod, mod_ctx, weights, grads_out,
                                     q_norm_g, k_norm_g, sink_logit, ln1_g, ln1_b, ln2_g, ln2_b)
    grads, delta, new_m, new_v = {}, {}, {}, {}

    gathered = _gather_rows(partial, "gather_partials")
    tot = _small_reduce(gathered)
    grads["b_ada"] = tot[0:6].reshape(1, 6 * d)
    grads["ln1_g"], grads["ln1_b"], grads["ln2_g"], grads["ln2_b"] = tot[8:9], tot[9:10], tot[10:11], tot[11:12]
    grads["q_norm_g"] = tot[13:14, 0:HEAD_DIM]
    grads["k_norm_g"] = tot[13:14, HEAD_DIM:2 * HEAD_DIM]
    grads["sink_logit"] = tot[13:14, 2 * HEAD_DIM:2 * HEAD_DIM + HEADS_A]
    loss = tot[12, 0]

    dm_all = gathered[:, 0:6, :].reshape(N_DEV, 6 * d)
    dmc_tot = jnp.concatenate([tot[6:8].reshape(1, 2 * d), jnp.zeros((1, 4 * d), F32)], axis=1)
    dm_rows = jnp.concatenate([pad8(dm_all[i]) for i in range(N_DEV)] + [pad8(dmc_tot), jnp.zeros((8, 6 * d), F32)], axis=0)
    dm_shard = lax.dynamic_slice(dm_rows, (0, s_me * ADA_W), (ADA_ROWS, ADA_W))
    grads["w_ada"] = _matmul(act, dm_shard, name="dw_ada", ta=True, tm=1024, tn=1024, tk=ADA_ROWS, out_dtype=F32)
    dmc_shard = lax.dynamic_slice(pad8(dmc_tot), (0, s_me * ADA_W), (8, ADA_W))
    cc_part = _matmul(dmc_shard, w["w_ada"], name="d_cctx", tb=True, tm=8, tn=1024, tk=1536, out_dtype=F32)
    grads["c_ctx"] = _cctx_grad(_gather_rows(cc_part, "gather_cctx"), c_ctx).reshape(d)

    delta["w_ada"], new_m["w_ada"], new_v["w_ada"] = _adamw(w["w_ada"], grads["w_ada"], m["w_ada"], v["w_ada"],
                                                            "adamw_w_ada")
    pack = lambda t: jnp.concatenate([t[name].reshape(1, size) for name, size in SMALL], axis=1)
    pd, pm, pv = _adamw(pack(w), pack(grads), pack(m), pack(v), "adamw_small")
    off = 0
    for name, size in SMALL:
        delta[name], new_m[name], new_v[name] = [t[:, off:off + size].reshape(w[name].shape) for t in (pd, pm, pv)]
        grads[name] = grads[name].reshape(w[name].shape)
        off += size

    after = pd
    for k, names in enumerate(G_GROUPS):
        arrs = tuple(BIG_INDEX[name] for name in names)
        g_own, g_other = _scatter_end("g%d" % k, arrs, scatters[k], after)
        for name, own, other in zip(names, g_own, g_other):
            grads[name], delta[name], new_m[name], new_v[name] = _adamw_halves(
                w[name], own, other, m[name], v[name], mc, BIG[BIG_INDEX[name]][2], "adamw_" + name)
            after = new_v[name]

    lead = lambda name, t: t[None] if name in ("w_ada", "w_in", "w_out", "w_gate", "w_up", "w_down") else t
    outs = [loss, grad_x[None]]
    for group in (grads, delta, new_m, new_v):
        outs += [lead(name, group[name]) for name in WEIGHT_ORDER]
    return tuple(outs)
```

```python
import functools
import math

import jax
import jax.numpy as jnp
from jax import lax
from jax.experimental import pallas as pl
from jax.experimental.pallas import tpu as pltpu

F32 = jnp.float32
BF16 = jnp.bfloat16
MESH = pl.DeviceIdType.MESH

D_MODEL = 2048
HEAD_DIM = 128
HEADS_A = 8
HEADS_B = 8
KV_A = 2
KV_B = 2
GROUP = 4
GRID_W = 64
WINDOW = 128
BLOCK = 128
FFN = 5632
IN_WIDTH = 3072
MIX_WIDTH = 2048
ROPE_THETA = 10000.0
EPS = 1e-6
ATTN_SCALE = HEAD_DIM ** -0.5
DN_ALPHA = 2.0 ** 0.25
N_SHARD = 4
N_DEV = 8

ADAM_LR = 0.001
ADAM_B1 = 0.9
ADAM_B2 = 0.999
ADAM_EPS = 1e-08
ADAM_WD = 0.01
ADAM_STEP = 10

QA0, KA0, VA0, QB0, KB0, VB0 = 0, 1024, 1280, 1536, 2560, 2816

VMEM_LIMIT = 56 * 1024 * 1024
ROW_TILE = 256
NN = (((1,), (0,)), ((), ()))
NT = (((1,), (1,)), ((), ()))
TN = (((0,), (0,)), ((), ()))


def _fit(total, pref):
    step = ROW_TILE // 4
    best = step
    for cand in range(step, pref + 1, step):
        if total % cand == 0:
            best = cand
    return best


def _params(sem=None):
    return pltpu.CompilerParams(dimension_semantics=sem, vmem_limit_bytes=VMEM_LIMIT)


def _matmul(a, b, *, name, ta=False, tb=False, tm, tn, tk, out_dtype):
    m = a.shape[1] if ta else a.shape[0]
    k = a.shape[0] if ta else a.shape[1]
    n = b.shape[0] if tb else b.shape[1]
    assert (b.shape[1] if tb else b.shape[0]) == k
    tm, tn, tk = min(tm, m), min(tn, n), min(tk, k)
    assert m % tm == 0 and n % tn == 0 and k % tk == 0, (name, m, n, k, tm, tn, tk)
    nk = k // tk
    dn = (((0 if ta else 1,), (1 if tb else 0,)), ((), ()))

    def product(a_ref, b_ref):
        return lax.dot_general(a_ref[...].astype(BF16), b_ref[...].astype(BF16), dn, preferred_element_type=F32)

    def body_whole_k(a_ref, b_ref, o_ref):
        o_ref[...] = product(a_ref, b_ref).astype(o_ref.dtype)

    def body(a_ref, b_ref, o_ref, acc_ref):
        kk = pl.program_id(2)
        part = product(a_ref, b_ref)

        @pl.when(kk == 0)
        def _():
            acc_ref[...] = part

        @pl.when(kk != 0)
        def _():
            acc_ref[...] += part

        @pl.when(kk == nk - 1)
        def _():
            o_ref[...] = acc_ref[...].astype(o_ref.dtype)

    a_spec = (pl.BlockSpec((tk, tm), lambda i, j, kk: (kk, i)) if ta
              else pl.BlockSpec((tm, tk), lambda i, j, kk: (i, kk)))
    b_spec = (pl.BlockSpec((tn, tk), lambda i, j, kk: (j, kk)) if tb
              else pl.BlockSpec((tk, tn), lambda i, j, kk: (kk, j)))
    return pl.pallas_call(
        body_whole_k if nk == 1 else body, name=name, grid=(m // tm, n // tn, nk),
        in_specs=[a_spec, b_spec],
        out_specs=pl.BlockSpec((tm, tn), lambda i, j, kk: (i, j)),
        out_shape=jax.ShapeDtypeStruct((m, n), out_dtype),
        scratch_shapes=[] if nk == 1 else [pltpu.VMEM((tm, tn), F32)],
        compiler_params=_params(("parallel", "parallel", "arbitrary")),
    )(a, b)


def _modulate_rows(x, ctx, mods):
    n, d = x.shape
    c = ctx.shape[0]
    nx = n // ROW_TILE
    assert c == ROW_TILE

    def body(x_ref, ctx_ref, mods_ref, o_ref):
        i = pl.program_id(0)

        @pl.when(i < nx)
        def _():
            o_ref[...] = (x_ref[...] * (1.0 + mods_ref[0:1, :]) + mods_ref[1:2, :]).astype(BF16)

        @pl.when(i >= nx)
        def _():
            o_ref[...] = (ctx_ref[...] * (1.0 + mods_ref[2:3, :]) + mods_ref[3:4, :]).astype(BF16)

    return pl.pallas_call(
        body, name="modulate_rows", grid=(nx + 1,),
        in_specs=[pl.BlockSpec((ROW_TILE, d), lambda i: (jnp.minimum(i, nx - 1), 0)),
                  pl.BlockSpec((ROW_TILE, d), lambda i: (0, 0)),
                  pl.BlockSpec((8, d), lambda i: (0, 0))],
        out_specs=pl.BlockSpec((ROW_TILE, d), lambda i: (i, 0)),
        out_shape=jax.ShapeDtypeStruct((n + c, d), BF16),
        compiler_params=_params(("parallel",)),
    )(x, ctx, mods)


def _rope_tables(n, c):
    rows = n // GRID_W
    row_ids = jnp.repeat(jnp.arange(rows, dtype=F32), GRID_W)
    col_ids = jnp.tile(jnp.arange(GRID_W, dtype=F32), rows)
    axis_dim = HEAD_DIM // 2
    inv_freq = jnp.power(ROPE_THETA, -jnp.arange(0, axis_dim, 2, dtype=F32) / axis_dim)
    ang_r = row_ids[:, None] * inv_freq
    ang_c = col_ids[:, None] * inv_freq
    ang = jnp.concatenate([ang_r, ang_r, ang_c, ang_c], axis=-1)
    cos, sin = jnp.cos(ang), jnp.sin(ang)
    quarter = (jnp.arange(HEAD_DIM) // (HEAD_DIM // 4)) % 2
    sin_a = jnp.where(quarter == 0, -sin, 0.0)
    sin_b = jnp.where(quarter == 1, sin, 0.0)
    pad = lambda t, v: jnp.concatenate([t, jnp.full((c, HEAD_DIM), v, F32)], axis=0)
    return pad(cos, 1.0), pad(sin_a, 0.0), pad(sin_b, 0.0)


def _rope(x, cos, sin_a, sin_b):
    return x * cos + pltpu.roll(x, 96, 1) * sin_a + pltpu.roll(x, 32, 1) * sin_b


def _rope_t(dy, cos, sin_a, sin_b):
    return dy * cos - pltpu.roll(dy, 96, 1) * sin_a - pltpu.roll(dy, 32, 1) * sin_b


def _rms(x):
    r = lax.rsqrt(jnp.mean(x * x, axis=-1, keepdims=True) + EPS)
    return x * r, r


def _qkv_post(h_all, cos, sin_a, sin_b, q_g, k_g):
    t = h_all.shape[0]
    nt = t // ROW_TILE

    def body(h_ref, cos_ref, sa_ref, sb_ref, qg_ref, kg_ref, qa_ref, ka_ref, va_ref, qb_ref, kb_ref, vb_ref):
        cos_, sa, sb = cos_ref[...], sa_ref[...], sb_ref[...]
        sl = lambda off, hh: h_ref[:, off + hh * HEAD_DIM: off + (hh + 1) * HEAD_DIM]
        for hh in range(HEADS_A):
            qa_ref[hh] = (_rope(sl(QA0, hh), cos_, sa, sb) * ATTN_SCALE).astype(BF16)
        for hh in range(KV_A):
            ka_ref[hh] = _rope(sl(KA0, hh), cos_, sa, sb).astype(BF16)
            va_ref[hh] = sl(VA0, hh).astype(BF16)
        for hh in range(HEADS_B):
            xn, _ = _rms(sl(QB0, hh))
            qb_ref[hh] = (_rope(xn * qg_ref[...], cos_, sa, sb) * ATTN_SCALE).astype(BF16)
        for hh in range(KV_B):
            xn, _ = _rms(sl(KB0, hh))
            kb_ref[hh] = _rope(xn * kg_ref[...], cos_, sa, sb).astype(BF16)
            vb_ref[hh] = sl(VB0, hh).astype(BF16)

    tab = pl.BlockSpec((ROW_TILE, HEAD_DIM), lambda i: (i, 0))
    gain = pl.BlockSpec((1, HEAD_DIM), lambda i: (0, 0))
    hs = lambda nh: pl.BlockSpec((nh, ROW_TILE, HEAD_DIM), lambda i: (0, i, 0))
    sh = lambda nh: jax.ShapeDtypeStruct((nh, t, HEAD_DIM), BF16)
    return pl.pallas_call(
        body, name="qkv_post", grid=(nt,),
        in_specs=[pl.BlockSpec((ROW_TILE, IN_WIDTH), lambda i: (i, 0)), tab, tab, tab, gain, gain],
        out_specs=[hs(HEADS_A), hs(KV_A), hs(KV_A), hs(HEADS_B), hs(KV_B), hs(KV_B)],
        out_shape=[sh(HEADS_A), sh(KV_A), sh(KV_A), sh(HEADS_B), sh(KV_B), sh(KV_B)],
        compiler_params=_params(("parallel",)),
    )(h_all, cos, sin_a, sin_b, q_g, k_g)


def _qkv_bwd_post(h_all, cos, sin_a, sin_b, q_g, k_g, dqa, dka, dva, dqb, dkb, dvb, n):
    t = h_all.shape[0]
    nt = t // ROW_TILE
    nx = n // ROW_TILE

    def body(h_ref, cos_ref, sa_ref, sb_ref, qg_ref, kg_ref,
             dqa_ref, dka_ref, dva_ref, dqb_ref, dkb_ref, dvb_ref, dh_ref, gs_ref):
        i = pl.program_id(0)
        cos_, sa, sb = cos_ref[...], sa_ref[...], sb_ref[...]
        latent = (i < nx).astype(F32)
        sl = lambda off, hh: h_ref[:, off + hh * HEAD_DIM: off + (hh + 1) * HEAD_DIM]

        def put(off, hh, val):
            dh_ref[:, off + hh * HEAD_DIM: off + (hh + 1) * HEAD_DIM] = val.astype(BF16)

        def norm_bwd(x, gain, dy):
            xn, r = _rms(x)
            dxh = dy * gain
            dx = r * (dxh - xn * jnp.mean(dxh * xn, axis=-1, keepdims=True))
            return dx, jnp.sum(dy * xn, axis=0, keepdims=True)

        for hh in range(HEADS_A):
            put(QA0, hh, _rope_t(dqa_ref[hh] * (ATTN_SCALE * latent), cos_, sa, sb))
        for hh in range(KV_A):
            put(KA0, hh, _rope_t(dka_ref[hh], cos_, sa, sb))
            put(VA0, hh, dva_ref[hh])
        gq = jnp.zeros((1, HEAD_DIM), F32)
        gk = jnp.zeros((1, HEAD_DIM), F32)
        for hh in range(HEADS_B):
            dq_t = dqb_ref[hh // GROUP, :, (hh % GROUP) * ROW_TILE:(hh % GROUP + 1) * ROW_TILE]
            dy = _rope_t(dq_t.T * (ATTN_SCALE * latent), cos_, sa, sb)
            dx, g = norm_bwd(sl(QB0, hh), qg_ref[...], dy)
            put(QB0, hh, dx)
            gq = gq + g
        for hh in range(KV_B):
            dy = _rope_t(dkb_ref[hh], cos_, sa, sb)
            dx, g = norm_bwd(sl(KB0, hh), kg_ref[...], dy)
            put(KB0, hh, dx)
            gk = gk + g
            put(VB0, hh, dvb_ref[hh])
        upd = jnp.concatenate([gq, gk, jnp.zeros((6, HEAD_DIM), F32)], axis=0)

        @pl.when(i == 0)
        def _():
            gs_ref[...] = upd

        @pl.when(i != 0)
        def _():
            gs_ref[...] += upd

    tab = pl.BlockSpec((ROW_TILE, HEAD_DIM), lambda i: (i, 0))
    gain = pl.BlockSpec((1, HEAD_DIM), lambda i: (0, 0))
    lat = lambda nh: pl.BlockSpec((nh, ROW_TILE, HEAD_DIM), lambda i: (0, jnp.minimum(i, nx - 1), 0))
    full = lambda nh: pl.BlockSpec((nh, ROW_TILE, HEAD_DIM), lambda i: (0, i, 0))
    return pl.pallas_call(
        body, name="qkv_bwd_post", grid=(nt,),
        in_specs=[pl.BlockSpec((ROW_TILE, IN_WIDTH), lambda i: (i, 0)), tab, tab, tab, gain, gain,
                  lat(HEADS_A), full(KV_A), full(KV_A),
                  pl.BlockSpec((KV_B, None, HEAD_DIM, GROUP * ROW_TILE), lambda i: (0, jnp.minimum(i, nx - 1), 0, 0)),
                  full(KV_B), full(KV_B)],
        out_specs=[pl.BlockSpec((ROW_TILE, IN_WIDTH), lambda i: (i, 0)),
                   pl.BlockSpec((8, HEAD_DIM), lambda i: (0, 0))],
        out_shape=[jax.ShapeDtypeStruct((t, IN_WIDTH), BF16), jax.ShapeDtypeStruct((8, HEAD_DIM), F32)],
        compiler_params=_params(("arbitrary",)),
    )(h_all, cos, sin_a, sin_b, q_g, k_g, dqa, dka, dva, dqb, dkb, dvb)


GB_TQ = 256
GB_TK = 256


def _heads_rows(ref2d, tq):
    return jnp.concatenate([ref2d[:, hh * HEAD_DIM:(hh + 1) * HEAD_DIM] for hh in range(GROUP)], axis=0)


def _attn_b_fwd(qb, kb, vb, heads_a, n):
    t = kb.shape[1]
    nk = t // GB_TK
    tq = GB_TQ
    rows = GROUP * tq

    def body(q_ref, k_ref, v_ref, heads_a_ref, o_ref, lse_ref, m_s, l_s, acc_s):
        q = q_ref[...].reshape(rows, HEAD_DIM)
        m_s[...] = jnp.full((1, rows), -jnp.inf, F32)
        l_s[...] = jnp.zeros((1, rows), F32)
        acc_s[...] = jnp.zeros((HEAD_DIM, rows), F32)

        def scores(j):
            start = pl.multiple_of(j * GB_TK, GB_TK)
            return lax.dot_general(k_ref[pl.ds(start, GB_TK), :], q, NT, preferred_element_type=F32)

        def step(j, st):
            st_next = scores(jnp.minimum(j + 1, nk - 1))
            vs = v_ref[pl.ds(pl.multiple_of(j * GB_TK, GB_TK), GB_TK), :]
            m_prev = m_s[...]
            m_new = jnp.maximum(m_prev, jnp.max(st, axis=0, keepdims=True))
            p = jnp.exp(st - m_new)
            alpha = jnp.exp(m_prev - m_new)
            l_s[...] = alpha * l_s[...] + jnp.sum(p, axis=0, keepdims=True)
            acc_s[...] = alpha * acc_s[...] + lax.dot_general(vs, p.astype(BF16), TN, preferred_element_type=F32)
            m_s[...] = m_new
            return st_next

        lax.fori_loop(0, nk, step, scores(0))
        ot = acc_s[...] * (1.0 / l_s[...])
        lse_ref[...] = m_s[...] + jnp.log(l_s[...])
        for hh in range(GROUP):
            o_ref[:, hh * HEAD_DIM:(hh + 1) * HEAD_DIM] = ot[:, hh * tq:(hh + 1) * tq].T.astype(BF16)

    return pl.pallas_call(
        body, name="attn_b_fwd", grid=(KV_B, n // tq),
        in_specs=[pl.BlockSpec((GROUP, tq, HEAD_DIM), lambda g, i: (g, i, 0)),
                  pl.BlockSpec((None, t, HEAD_DIM), lambda g, i: (g, 0, 0)),
                  pl.BlockSpec((None, t, HEAD_DIM), lambda g, i: (g, 0, 0)),
                  pl.BlockSpec(memory_space=pl.ANY)],
        out_specs=[pl.BlockSpec((tq, GROUP * HEAD_DIM), lambda g, i: (i, KV_A + g)),
                   pl.BlockSpec((None, None, 1, rows), lambda g, i: (g, i, 0, 0))],
        out_shape=[jax.ShapeDtypeStruct((n, MIX_WIDTH), BF16),
                   jax.ShapeDtypeStruct((KV_B, n // tq, 1, rows), F32)],
        input_output_aliases={3: 0},
        scratch_shapes=[pltpu.VMEM((1, rows), F32), pltpu.VMEM((1, rows), F32), pltpu.VMEM((HEAD_DIM, rows), F32)],
        compiler_params=_params(("parallel", "parallel")),
    )(qb, kb, vb, heads_a)


def _attn_b_bwd(qb, kb, vb, dheads, lse, delta, n):
    t = kb.shape[1]
    nk = t // GB_TK
    tq = GB_TQ
    nq = n // tq
    rows = GROUP * tq

    def body(q_ref, k_ref, v_ref, do_ref, lse_ref, dl_ref, dq_ref, dk_ref, dv_ref):
        j = pl.program_id(1)
        i = pl.program_id(2)

        q = q_ref[...].reshape(rows, HEAD_DIM)
        do = _heads_rows(do_ref, tq)
        ks, vs = k_ref[...], v_ref[...]
        st = lax.dot_general(ks, q, NT, preferred_element_type=F32)
        p = jnp.exp(st - lse_ref[...])
        dpt = lax.dot_general(vs, do, NT, preferred_element_type=F32)
        ds = (p * (dpt - dl_ref[...])).astype(BF16)
        dv_part = lax.dot_general(p.astype(BF16), do, NN, preferred_element_type=F32)
        dk_part = lax.dot_general(ds, q, NN, preferred_element_type=F32)
        dq_part = lax.dot_general(ks, ds, TN, preferred_element_type=F32)

        @pl.when(i == 0)
        def _():
            dk_ref[...] = dk_part
            dv_ref[...] = dv_part

        @pl.when(i != 0)
        def _():
            dk_ref[...] += dk_part
            dv_ref[...] += dv_part

        @pl.when(j == 0)
        def _():
            dq_ref[i] = dq_part

        @pl.when(j != 0)
        def _():
            dq_ref[i] += dq_part

    kv = pl.BlockSpec((None, GB_TK, HEAD_DIM), lambda g, j, i: (g, j, 0))
    row = pl.BlockSpec((None, None, 1, rows), lambda g, j, i: (g, i, 0, 0))
    return pl.pallas_call(
        body, name="attn_b_bwd", grid=(KV_B, nk, nq),
        in_specs=[pl.BlockSpec((GROUP, tq, HEAD_DIM), lambda g, j, i: (g, i, 0)), kv, kv,
                  pl.BlockSpec((tq, GROUP * HEAD_DIM), lambda g, j, i: (i, KV_A + g)), row, row],
        out_specs=[pl.BlockSpec((None, nq, HEAD_DIM, rows), lambda g, j, i: (g, 0, 0, 0)), kv, kv],
        out_shape=[jax.ShapeDtypeStruct((KV_B, nq, HEAD_DIM, rows), F32),
                   jax.ShapeDtypeStruct((KV_B, t, HEAD_DIM), F32),
                   jax.ShapeDtypeStruct((KV_B, t, HEAD_DIM), F32)],
        compiler_params=_params(("parallel", "arbitrary", "arbitrary")),
    )(qb, kb, vb, dheads, lse, delta)


def _delta_rows(dheads, heads):
    n = heads.shape[0]
    tq = GB_TQ
    w = GROUP * HEAD_DIM

    def body(a_ref, b_ref, o_ref):
        prod = a_ref[...].astype(F32) * b_ref[...].astype(F32)
        cols = [jnp.sum(prod[:, hh * HEAD_DIM:(hh + 1) * HEAD_DIM].T, axis=0, keepdims=True) for hh in range(GROUP)]
        o_ref[...] = jnp.concatenate(cols, axis=1)

    blk = pl.BlockSpec((tq, w), lambda g, i: (i, KV_A + g))
    return pl.pallas_call(
        body, name="delta_rows", grid=(KV_B, n // tq),
        in_specs=[blk, blk],
        out_specs=pl.BlockSpec((None, None, 1, GROUP * tq), lambda g, i: (g, i, 0, 0)),
        out_shape=jax.ShapeDtypeStruct((KV_B, n // tq, 1, GROUP * tq), F32),
        compiler_params=_params(("parallel", "parallel")),
    )(dheads, heads)


KWIN = 3 * BLOCK


def _window_scores(q, k_ref, j, n, nb, sink_row):
    c = k_ref.shape[0] - n
    start = pl.multiple_of(jnp.clip(j - 1, 0, nb - 3) * BLOCK, BLOCK)
    kw = k_ref[pl.ds(start, KWIN), :]
    kc = k_ref[pl.ds(n, c), :]
    s_loc = lax.dot_general(kw, q, NT, preferred_element_type=F32)
    s_ctx = lax.dot_general(kc, q, NT, preferred_element_type=F32)
    cols = GROUP * BLOCK
    qpos = j * BLOCK + lax.broadcasted_iota(jnp.int32, (KWIN, cols), 1) % BLOCK
    kpos = start + lax.broadcasted_iota(jnp.int32, (KWIN, cols), 0)
    s_loc = jnp.where(jnp.abs(qpos - kpos) <= WINDOW, s_loc, -jnp.inf)
    m = jnp.maximum(jnp.maximum(jnp.max(s_loc, axis=0, keepdims=True), jnp.max(s_ctx, axis=0, keepdims=True)),
                    sink_row)
    e_loc, e_ctx, e_sink = jnp.exp(s_loc - m), jnp.exp(s_ctx - m), jnp.exp(sink_row - m)
    inv = 1.0 / (jnp.sum(e_loc, axis=0, keepdims=True) + jnp.sum(e_ctx, axis=0, keepdims=True) + e_sink)
    return e_loc * inv, e_ctx * inv, e_sink * inv, start


def _sink_row(sink_ref, g):
    return jnp.concatenate([sink_ref[pl.ds(g * GROUP + hh, 1), :] for hh in range(GROUP)], axis=1)


def _attn_a_fwd(qa, ka, va, sink_b, n):
    t = ka.shape[1]
    nb = n // BLOCK
    assert nb >= 3

    def body(q_ref, k_ref, v_ref, sink_ref, o_ref):
        g, j = pl.program_id(0), pl.program_id(1)
        q = q_ref[...].reshape(GROUP * BLOCK, HEAD_DIM)
        p_loc, p_ctx, _, start = _window_scores(q, k_ref, j, n, nb, _sink_row(sink_ref, g))
        vw = v_ref[pl.ds(start, KWIN), :]
        vc = v_ref[pl.ds(n, t - n), :]
        ot = (lax.dot_general(vw, p_loc.astype(BF16), TN, preferred_element_type=F32)
              + lax.dot_general(vc, p_ctx.astype(BF16), TN, preferred_element_type=F32))
        for hh in range(GROUP):
            o_ref[:, hh * HEAD_DIM:(hh + 1) * HEAD_DIM] = ot[:, hh * BLOCK:(hh + 1) * BLOCK].T.astype(BF16)

    return pl.pallas_call(
        body, name="attn_a_fwd", grid=(KV_A, nb),
        in_specs=[pl.BlockSpec((GROUP, BLOCK, HEAD_DIM), lambda g, j: (g, j, 0)),
                  pl.BlockSpec((None, t, HEAD_DIM), lambda g, j: (g, 0, 0)),
                  pl.BlockSpec((None, t, HEAD_DIM), lambda g, j: (g, 0, 0)),
                  pl.BlockSpec((HEADS_A, HEAD_DIM), lambda g, j: (0, 0))],
        out_specs=pl.BlockSpec((BLOCK, GROUP * HEAD_DIM), lambda g, j: (j, g)),
        out_shape=jax.ShapeDtypeStruct((n, MIX_WIDTH), BF16),
        compiler_params=_params(("parallel", "parallel")),
    )(qa, ka, va, sink_b)


def _attn_a_bwd(qa, ka, va, sink_b, dheads, n):
    t = ka.shape[1]
    c = t - n
    nb = n // BLOCK

    def body(q_ref, k_ref, v_ref, sink_ref, do_ref, dq_ref, dk_ref, dv_ref, dsink_ref):
        g, j = pl.program_id(0), pl.program_id(1)

        @pl.when(j == 0)
        def _():
            dk_ref[...] = jnp.zeros_like(dk_ref)
            dv_ref[...] = jnp.zeros_like(dv_ref)
            dsink_ref[...] = jnp.zeros_like(dsink_ref)

        q = q_ref[...].reshape(GROUP * BLOCK, HEAD_DIM)
        do = _heads_rows(do_ref, BLOCK)
        p_loc, p_ctx, p_sink, start = _window_scores(q, k_ref, j, n, nb, _sink_row(sink_ref, g))
        kw, vw = k_ref[pl.ds(start, KWIN), :], v_ref[pl.ds(start, KWIN), :]
        kc, vc = k_ref[pl.ds(n, c), :], v_ref[pl.ds(n, c), :]
        dp_loc = lax.dot_general(vw, do, NT, preferred_element_type=F32)
        dp_ctx = lax.dot_general(vc, do, NT, preferred_element_type=F32)
        dl = jnp.sum(p_loc * dp_loc, axis=0, keepdims=True) + jnp.sum(p_ctx * dp_ctx, axis=0, keepdims=True)
        ds_loc = (p_loc * (dp_loc - dl)).astype(BF16)
        ds_ctx = (p_ctx * (dp_ctx - dl)).astype(BF16)
        dqt = (lax.dot_general(kw, ds_loc, TN, preferred_element_type=F32)
               + lax.dot_general(kc, ds_ctx, TN, preferred_element_type=F32))
        for hh in range(GROUP):
            dq_ref[hh] = dqt[:, hh * BLOCK:(hh + 1) * BLOCK].T
        dk_ref[pl.ds(start, KWIN), :] += lax.dot_general(ds_loc, q, NN, preferred_element_type=F32)
        dv_ref[pl.ds(start, KWIN), :] += lax.dot_general(p_loc.astype(BF16), do, NN, preferred_element_type=F32)
        dk_ref[pl.ds(n, c), :] += lax.dot_general(ds_ctx, q, NN, preferred_element_type=F32)
        dv_ref[pl.ds(n, c), :] += lax.dot_general(p_ctx.astype(BF16), do, NN, preferred_element_type=F32)
        dsk = -(p_sink * dl)
        upd = [jnp.broadcast_to(jnp.sum(dsk[:, hh * BLOCK:(hh + 1) * BLOCK], axis=1, keepdims=True), (1, HEAD_DIM))
               for hh in range(GROUP)]
        dsink_ref[...] += jnp.concatenate(upd + [jnp.zeros((8 - GROUP, HEAD_DIM), F32)], axis=0)

    res = pl.BlockSpec((None, t, HEAD_DIM), lambda g, j: (g, 0, 0))
    return pl.pallas_call(
        body, name="attn_a_bwd", grid=(KV_A, nb),
        in_specs=[pl.BlockSpec((GROUP, BLOCK, HEAD_DIM), lambda g, j: (g, j, 0)), res, res,
                  pl.BlockSpec((HEADS_A, HEAD_DIM), lambda g, j: (0, 0)),
                  pl.BlockSpec((BLOCK, GROUP * HEAD_DIM), lambda g, j: (j, g))],
        out_specs=[pl.BlockSpec((GROUP, BLOCK, HEAD_DIM), lambda g, j: (g, j, 0)), res, res,
                   pl.BlockSpec((None, 8, HEAD_DIM), lambda g, j: (g, 0, 0))],
        out_shape=[jax.ShapeDtypeStruct((HEADS_A, n, HEAD_DIM), F32),
                   jax.ShapeDtypeStruct((KV_A, t, HEAD_DIM), F32),
                   jax.ShapeDtypeStruct((KV_A, t, HEAD_DIM), F32),
                   jax.ShapeDtypeStruct((KV_A, 8, HEAD_DIM), F32)],
        compiler_params=_params(("parallel", "arbitrary")),
    )(qa, ka, va, sink_b, dheads)


def _ln_stats(r):
    mu = jnp.mean(r, axis=-1, keepdims=True)
    cen = r - mu
    rstd = lax.rsqrt(jnp.mean(cen * cen, axis=-1, keepdims=True) + EPS)
    return cen * rstd, rstd


def _ln_bwd(dy, xhat, rstd, gain):
    dxh = dy * gain
    return rstd * (dxh - jnp.mean(dxh, axis=-1, keepdims=True)
                   - xhat * jnp.mean(dxh * xhat, axis=-1, keepdims=True))


def _accumulate_rows(ref, rows, i):
    pad = [jnp.zeros_like(rows[0])] * (8 - len(rows))
    upd = jnp.concatenate(rows + pad, axis=0)

    @pl.when(i == 0)
    def _():
        ref[...] = upd

    @pl.when(i != 0)
    def _():
        ref[...] += upd


def _colsum(v):
    return jnp.sum(v, axis=0, keepdims=True)


LN_TILE = 256


def _res_ln1(x, a, vec):
    n, d = x.shape

    def body(x_ref, a_ref, v_ref, xh_ref, rs_ref, u_ref):
        r1 = DN_ALPHA * x_ref[...] + v_ref[0:1, :] * a_ref[...]
        xhat, rstd = _ln_stats(r1)
        xh_ref[...] = xhat
        rs_ref[...] = rstd
        x1 = xhat * v_ref[1:2, :] + v_ref[2:3, :]
        u_ref[...] = (x1 * (1.0 + v_ref[3:4, :]) + v_ref[4:5, :]).astype(BF16)

    row = pl.BlockSpec((LN_TILE, d), lambda i: (i, 0))
    return pl.pallas_call(
        body, name="res_ln1", grid=(n // LN_TILE,),
        in_specs=[row, row, pl.BlockSpec((8, d), lambda i: (0, 0))],
        out_specs=[row, pl.BlockSpec((LN_TILE, 1), lambda i: (i, 0)), row],
        out_shape=[jax.ShapeDtypeStruct((n, d), F32), jax.ShapeDtypeStruct((n, 1), F32),
                   jax.ShapeDtypeStruct((n, d), BF16)],
        compiler_params=_params(("parallel",)),
    )(x, a, vec)


def _res_ln2_loss(xhat1, f, target, vec):
    n, d = f.shape

    def body(xh_ref, f_ref, t_ref, v_ref, dr_ref, df_ref, s_ref):
        i = pl.program_id(0)
        x1 = xh_ref[...] * v_ref[1:2, :] + v_ref[2:3, :]
        fv = f_ref[...]
        xhat, rstd = _ln_stats(DN_ALPHA * x1 + v_ref[0:1, :] * fv)
        err = xhat * v_ref[3:4, :] + v_ref[4:5, :] - t_ref[...]
        dy = err * (1.0 / d)
        dr2 = _ln_bwd(dy, xhat, rstd, v_ref[3:4, :])
        dr_ref[...] = dr2
        df_ref[...] = (dr2 * v_ref[0:1, :]).astype(BF16)
        _accumulate_rows(s_ref, [_colsum(dy * xhat), _colsum(dy), _colsum(dr2 * fv),
                                 _colsum(err * err) * (0.5 / d)], i)

    row = pl.BlockSpec((LN_TILE, d), lambda i: (i, 0))
    return pl.pallas_call(
        body, name="res_ln2_loss", grid=(n // LN_TILE,),
        in_specs=[row, row, row, pl.BlockSpec((8, d), lambda i: (0, 0))],
        out_specs=[row, row, pl.BlockSpec((8, d), lambda i: (0, 0))],
        out_shape=[jax.ShapeDtypeStruct((n, d), F32), jax.ShapeDtypeStruct((n, d), BF16),
                   jax.ShapeDtypeStruct((8, d), F32)],
        compiler_params=_params(("arbitrary",)),
    )(xhat1, f, target, vec)


def _ln1_bwd(du2, dr2, xhat1, rstd1, a, vec):
    n, d = du2.shape

    def body(du_ref, dr2_ref, xh_ref, rs_ref, a_ref, v_ref, dxp_ref, da_ref, s_ref):
        i = pl.program_id(0)
        du, xhat = du_ref[...], xh_ref[...]
        x1 = xhat * v_ref[1:2, :] + v_ref[2:3, :]
        dx1 = DN_ALPHA * dr2_ref[...] + du * (1.0 + v_ref[0:1, :])
        dr1 = _ln_bwd(dx1, xhat, rs_ref[...], v_ref[1:2, :])
        dxp_ref[...] = DN_ALPHA * dr1
        da_ref[...] = (dr1 * v_ref[3:4, :]).astype(BF16)
        _accumulate_rows(s_ref, [_colsum(du * x1), _colsum(du), _colsum(dx1 * xhat), _colsum(dx1),
                                 _colsum(dr1 * a_ref[...])], i)

    row = pl.BlockSpec((LN_TILE, d), lambda i: (i, 0))
    return pl.pallas_call(
        body, name="ln1_bwd", grid=(n // LN_TILE,),
        in_specs=[row, row, row, pl.BlockSpec((LN_TILE, 1), lambda i: (i, 0)), row,
                  pl.BlockSpec((8, d), lambda i: (0, 0))],
        out_specs=[row, row, pl.BlockSpec((8, d), lambda i: (0, 0))],
        out_shape=[jax.ShapeDtypeStruct((n, d), F32), jax.ShapeDtypeStruct((n, d), BF16),
                   jax.ShapeDtypeStruct((8, d), F32)],
        compiler_params=_params(("arbitrary",)),
    )(du2, dr2, xhat1, rstd1, a, vec)


def _mod1_bwd(du_all, dxp, x, ctx, mods):
    n, d = x.shape
    nx = n // ROW_TILE

    def body(du_ref, dxp_ref, x_ref, ctx_ref, m_ref, gx_ref, s_ref):
        i = pl.program_id(0)
        du = du_ref[...]
        zero = jnp.zeros((1, d), F32)

        @pl.when(i == 0)
        def _():
            s_ref[...] = jnp.zeros_like(s_ref)

        @pl.when(i < nx)
        def _():
            gx_ref[...] = dxp_ref[...] + du * (1.0 + m_ref[0:1, :])
            s_ref[...] += jnp.concatenate([_colsum(du * x_ref[...]), _colsum(du)] + [zero] * 6, axis=0)

        @pl.when(i >= nx)
        def _():
            s_ref[...] += jnp.concatenate([zero, zero, _colsum(du * ctx_ref[...]), _colsum(du)] + [zero] * 4, axis=0)

    lat = pl.BlockSpec((ROW_TILE, d), lambda i: (jnp.minimum(i, nx - 1), 0))
    return pl.pallas_call(
        body, name="mod1_bwd", grid=(nx + 1,),
        in_specs=[pl.BlockSpec((ROW_TILE, d), lambda i: (i, 0)), lat, lat,
                  pl.BlockSpec((ROW_TILE, d), lambda i: (0, 0)), pl.BlockSpec((8, d), lambda i: (0, 0))],
        out_specs=[lat, pl.BlockSpec((8, d), lambda i: (0, 0))],
        out_shape=[jax.ShapeDtypeStruct((n, d), F32), jax.ShapeDtypeStruct((8, d), F32)],
        compiler_params=_params(("arbitrary",)),
    )(du_all, dxp, x, ctx, mods)


FFN_TM = 512
FFN_TN = 512


def _gate_up(u2, wg, wu):
    n, d = u2.shape
    f = wg.shape[1]

    def body(u_ref, wg_ref, wu_ref, g_ref, up_ref, h_ref):
        u = u_ref[...]
        g = lax.dot_general(u, wg_ref[...], NN, preferred_element_type=F32)
        up = lax.dot_general(u, wu_ref[...], NN, preferred_element_type=F32)
        g_ref[...] = g.astype(BF16)
        up_ref[...] = up.astype(BF16)
        h_ref[...] = (g * jax.nn.sigmoid(g) * up).astype(BF16)

    wspec = pl.BlockSpec((d, FFN_TN), lambda j, i: (0, j))
    ospec = pl.BlockSpec((FFN_TM, FFN_TN), lambda j, i: (i, j))
    return pl.pallas_call(
        body, name="gate_up", grid=(f // FFN_TN, n // FFN_TM),
        in_specs=[pl.BlockSpec((FFN_TM, d), lambda j, i: (i, 0)), wspec, wspec],
        out_specs=[ospec, ospec, ospec],
        out_shape=[jax.ShapeDtypeStruct((n, f), BF16)] * 3,
        compiler_params=_params(("parallel", "parallel")),
    )(u2, wg, wu)


def _glu_bwd(df, wd, g, u):
    n, d = df.shape
    f = wd.shape[0]

    def body(df_ref, wd_ref, g_ref, u_ref, dg_ref, du_ref):
        dh = lax.dot_general(df_ref[...], wd_ref[...], NT, preferred_element_type=F32)
        gv = g_ref[...].astype(F32)
        sig = jax.nn.sigmoid(gv)
        du_ref[...] = (dh * (gv * sig)).astype(BF16)
        dg_ref[...] = (dh * u_ref[...].astype(F32) * (sig * (1.0 + gv * (1.0 - sig)))).astype(BF16)

    tm = min(2 * FFN_TM, n)
    ospec = pl.BlockSpec((tm, FFN_TN), lambda i, j: (i, j))
    return pl.pallas_call(
        body, name="glu_bwd", grid=(n // tm, f // FFN_TN),
        in_specs=[pl.BlockSpec((tm, d), lambda i, j: (i, 0)),
                  pl.BlockSpec((FFN_TN, d), lambda i, j: (j, 0)), ospec, ospec],
        out_specs=[ospec, ospec],
        out_shape=[jax.ShapeDtypeStruct((n, f), BF16), jax.ShapeDtypeStruct((n, f), BF16)],
        compiler_params=_params(("parallel", "parallel")),
    )(df, wd, g, u)


def _du2(dg, du, wg, wu):
    n, f = dg.shape
    d = wg.shape[0]
    tm, tn, tk = min(1024, n), 1024, 1408
    nk = f // tk

    def body(dg_ref, du_ref, wg_ref, wu_ref, o_ref, acc_ref):
        kk = pl.program_id(2)
        part = (lax.dot_general(dg_ref[...], wg_ref[...], NT, preferred_element_type=F32)
                + lax.dot_general(du_ref[...], wu_ref[...], NT, preferred_element_type=F32))

        @pl.when(kk == 0)
        def _():
            acc_ref[...] = part

        @pl.when(kk != 0)
        def _():
            acc_ref[...] += part

        @pl.when(kk == nk - 1)
        def _():
            o_ref[...] = acc_ref[...]

    aspec = pl.BlockSpec((tm, tk), lambda i, j, kk: (i, kk))
    wspec = pl.BlockSpec((tn, tk), lambda i, j, kk: (j, kk))
    return pl.pallas_call(
        body, name="du2", grid=(n // tm, d // tn, nk),
        in_specs=[aspec, aspec, wspec, wspec],
        out_specs=pl.BlockSpec((tm, tn), lambda i, j, kk: (i, j)),
        out_shape=jax.ShapeDtypeStruct((n, d), F32),
        scratch_shapes=[pltpu.VMEM((tm, tn), F32)],
        compiler_params=_params(("parallel", "parallel", "arbitrary")),
    )(dg, du, wg, wu)


def _rows8(rows, d=D_MODEL):
    rows = [r.reshape(1, d).astype(F32) for r in rows]
    return jnp.concatenate(rows + [jnp.zeros((8 - len(rows), d), F32)], axis=0)


W_GROUPS = (("w_in",), ("w_out", "w_gate", "w_up"), ("w_down",))
G_GROUPS = (("w_down", "w_gate", "w_up"), ("w_out",), ("w_in",))


def _layer_fwd_bwd(x, ctx, target, mod, mod_ctx, weights, grads_out, q_g, k_g, sink, ln1_g, ln1_b, ln2_g, ln2_b):
    n, d = x.shape
    c = ctx.shape[0]
    sh1, sc1, g1, sh2, sc2, g2 = [mod[:, k * d:(k + 1) * d] for k in range(6)]
    csh1, csc1 = mod_ctx[:, 0:d], mod_ctx[:, d:2 * d]
    cos, sin_a, sin_b = _rope_tables(n, c)
    sink_b = jnp.broadcast_to(sink.reshape(HEADS_A, 1), (HEADS_A, HEAD_DIM)).astype(F32)

    u_all = _modulate_rows(x, ctx, _rows8([sc1, sh1, csc1, csh1]))
    (w_in,) = weights(0, u_all)
    h_all = _matmul(u_all, w_in, name="qkv_proj", tm=_fit(n + c, 1088), tn=1024, tk=2048, out_dtype=F32)
    qa, ka, va, qb, kb, vb = _qkv_post(h_all, cos, sin_a, sin_b, q_g, k_g)
    heads, lse = _attn_b_fwd(qb, kb, vb, _attn_a_fwd(qa, ka, va, sink_b, n), n)
    w_out, w_gate, w_up = weights(1, heads)
    a = _matmul(heads, w_out, name="out_proj", tm=1024, tn=1024, tk=2048, out_dtype=F32)
    xhat1, rstd1, u2 = _res_ln1(x, a, _rows8([g1, ln1_g, ln1_b, sc2, sh2]))
    gg, uu, hh = _gate_up(u2, w_gate, w_up)
    (w_down,) = weights(2, hh)
    f = _matmul(hh, w_down, name="ffn_down", tm=1024, tn=1024, tk=1408, out_dtype=F32)
    dr2, df, s_ln2 = _res_ln2_loss(xhat1, f, target, _rows8([g2, ln1_g, ln1_b, ln2_g, ln2_b]))

    dgg, duu = _glu_bwd(df, w_down, gg, uu)
    dw_down = _matmul(hh, df, name="dw_down", ta=True, tm=512, tn=1024, tk=n, out_dtype=BF16)
    dw_gate = _matmul(u2, dgg, name="dw_gate", ta=True, tm=1024, tn=512, tk=n, out_dtype=BF16)
    dw_up = _matmul(u2, duu, name="dw_up", ta=True, tm=1024, tn=512, tk=n, out_dtype=BF16)
    zero = grads_out(0, [dw_down, dw_gate, dw_up])
    du2 = _du2(dgg, duu, w_gate, w_up)
    dxp, da, s_ln1 = _ln1_bwd(du2, dr2, xhat1, rstd1, a, _rows8([sc2, ln1_g, ln1_b, g1]) + zero)

    dheads = _matmul(da, w_out, name="d_heads", tb=True, tm=1024, tn=1024, tk=2048, out_dtype=BF16)
    dw_out = _matmul(heads, da, name="dw_out", ta=True, tm=1024, tn=1024, tk=n, out_dtype=BF16)
    zero = grads_out(1, [dw_out])
    delta = _delta_rows(dheads, heads)
    dqa, dka, dva, dsink = _attn_a_bwd(qa, ka, va, sink_b + zero, dheads, n)
    dqb, dkb, dvb = _attn_b_bwd(qb, kb, vb, dheads, lse, delta, n)
    dh_all, s_gain = _qkv_bwd_post(h_all, cos, sin_a, sin_b, q_g, k_g, dqa, dka, dva, dqb, dkb, dvb, n)
    dw_in = _matmul(u_all, dh_all, name="dw_in", ta=True, tm=1024, tn=1024, tk=n + c, out_dtype=BF16)
    zero = grads_out(2, [dw_in])
    du_all = _matmul(dh_all, w_in, name="d_u1", tb=True, tm=_fit(n + c, 1088), tn=1024, tk=IN_WIDTH, out_dtype=F32)
    grad_x, s_mod1 = _mod1_bwd(du_all, dxp, x, ctx, _rows8([sc1]) + zero)

    dsink_row = jnp.concatenate([dsink[0, 0:GROUP, 0], dsink[1, 0:GROUP, 0]]).reshape(1, HEADS_A)
    misc = jnp.concatenate([s_gain[0:1], s_gain[1:2], dsink_row,
                            jnp.zeros((1, d - 2 * HEAD_DIM - HEADS_A), F32)], axis=1)
    partial = jnp.concatenate([
        s_mod1[1:2], s_mod1[0:1], s_ln1[4:5],
        s_ln1[1:2], s_ln1[0:1], s_ln2[2:3],
        s_mod1[3:4], s_mod1[2:3],
        s_ln1[2:3], s_ln1[3:4], s_ln2[0:1], s_ln2[1:2],
        s_ln2[3:4], misc, jnp.zeros((2, d), F32)], axis=0)
    return grad_x, partial


ANY = pl.BlockSpec(memory_space=pl.ANY)
VMEM_FULL = pl.BlockSpec(memory_space=pltpu.VMEM)
N_CHIP_PEERS = 3


def _me():
    return lax.axis_index("x"), lax.axis_index("y"), lax.axis_index("c")


def _other_chips(x, y):
    return [(1 - x, y), (x, 1 - y), (1 - x, 1 - y)]


def _shard_of(chip):
    return 2 * chip[0] + chip[1]


def _dev_index(x, y, c):
    return 4 * x + 2 * y + c


def _rcopy(src, dst, send_sems, recv_sems, k, dev):
    return pltpu.make_async_remote_copy(src_ref=src, dst_ref=dst, send_sem=send_sems.at[k], recv_sem=recv_sems.at[k],
                                        device_id=dev, device_id_type=MESH)


BIG = (("w_in", (D_MODEL, IN_WIDTH), 1), ("w_out", (MIX_WIDTH, D_MODEL), 0), ("w_gate", (D_MODEL, FFN), 1),
       ("w_up", (D_MODEL, FFN), 1), ("w_down", (FFN, D_MODEL), 0))


def _sub(ref, axis, idx, size):
    start = pl.multiple_of(idx * size, size)
    return ref.at[pl.ds(start, size), :] if axis == 0 else ref.at[:, pl.ds(start, size)]


def _shape_div(shape, axis, parts):
    return tuple(s // parts if a == axis else s for a, s in enumerate(shape))


def _piece(a, ref, shard, half):
    _, full, axis = BIG[a]
    view = _sub(ref, axis, shard, full[axis] // N_SHARD)
    return _sub(view, 1 - axis, half, full[1 - axis] // 2)


HBM = pl.BlockSpec(memory_space=pltpu.HBM)
SEM = pl.BlockSpec(memory_space=pltpu.SEMAPHORE)
EFFECT = pltpu.SideEffectType.DATAFLOW_SIDE_EFFECTING
BIG_INDEX = {name: a for a, (name, _, _) in enumerate(BIG)}


def _in_hbm(arr):
    return pltpu.with_memory_space_constraint(arr, pltpu.HBM)


def _gather_start(tag, arrs, bufs, prev):
    n_arr = len(arrs)

    def body(*refs):
        ins = refs[:n_arr]
        send_sems, recv_sems = refs[n_arr + 1], refs[n_arr + 2]
        token = refs[-1]
        x, y, c = _me()
        s_me = _shard_of((x, y))
        for i, a in enumerate(arrs):
            mine = _piece(a, ins[i], s_me, c)
            for j, chip in enumerate(_other_chips(x, y)):
                _rcopy(mine, mine, send_sems, recv_sems, N_CHIP_PEERS * i + j, (*chip, c)).start()
        token[...] = jnp.zeros_like(token)

    n_sem = N_CHIP_PEERS * n_arr
    outs = pl.pallas_call(
        body, name="gather_start_" + tag,
        in_specs=[HBM] * n_arr + [ANY],
        out_specs=[SEM, SEM] + [HBM] * n_arr + [VMEM_FULL],
        out_shape=[pltpu.SemaphoreType.DMA((n_sem,)), pltpu.SemaphoreType.DMA((n_sem,))]
        + [pltpu.HBM(BIG[a][1], BF16) for a in arrs] + [jax.ShapeDtypeStruct((8, HEAD_DIM), F32)],
        input_output_aliases={i: 2 + i for i in range(n_arr)},
        compiler_params=pltpu.CompilerParams(has_side_effects=EFFECT),
    )(*[_in_hbm(b) for b in bufs], prev)
    return outs[0], outs[1], list(outs[2:2 + n_arr]), outs[-1]


def _gather_wait(tag, arrs, send_sems, recv_sems, bufs, after):
    n_arr = len(arrs)

    def body(*refs):
        ins = refs[:n_arr]
        send_sems_, recv_sems_ = refs[n_arr], refs[n_arr + 1]
        x, y, c = _me()
        s_me = _shard_of((x, y))
        for i, a in enumerate(arrs):
            mine = _piece(a, ins[i], s_me, c)
            for j, chip in enumerate(_other_chips(x, y)):
                landed = _piece(a, ins[i], _shard_of(chip), c)
                cp = _rcopy(mine, landed, send_sems_, recv_sems_, N_CHIP_PEERS * i + j, (*chip, c))
                cp.wait_send()
                cp.wait_recv()

    outs = pl.pallas_call(
        body, name="gather_wait_" + tag,
        in_specs=[HBM] * n_arr + [SEM, SEM, ANY],
        out_specs=[HBM] * n_arr,
        out_shape=[pltpu.HBM(BIG[a][1], BF16) for a in arrs],
        input_output_aliases={i: i for i in range(n_arr)},
        compiler_params=pltpu.CompilerParams(has_side_effects=EFFECT),
    )(*bufs, send_sems, recv_sems, after)
    return list(outs)


def _gather_forward(tag, arrs, bufs):
    n_arr = len(arrs)

    def body(*refs):
        outs = refs[n_arr:2 * n_arr]
        send_sems, recv_sems = refs[2 * n_arr:]
        x, y, c = _me()
        sibling = (x, y, 1 - c)
        chips = _other_chips(x, y)
        copies = []
        for i, a in enumerate(arrs):
            for j, chip in enumerate(chips):
                landed = _piece(a, outs[i], _shard_of(chip), c)
                cp = _rcopy(landed, landed, send_sems, recv_sems, N_CHIP_PEERS * i + j, sibling)
                cp.start()
                copies.append(cp)
        for i, a in enumerate(arrs):
            for j, chip in enumerate(chips):
                other = _piece(a, outs[i], _shard_of(chip), 1 - c)
                _rcopy(other, other, send_sems, recv_sems, N_CHIP_PEERS * i + j, sibling).wait_recv()
        for cp in copies:
            cp.wait_send()

    n_sem = N_CHIP_PEERS * n_arr
    return list(pl.pallas_call(
        body, name="gather_forward_" + tag,
        in_specs=[ANY] * n_arr, out_specs=[ANY] * n_arr,
        out_shape=[jax.ShapeDtypeStruct(BIG[a][1], BF16) for a in arrs],
        input_output_aliases={i: i for i in range(n_arr)},
        scratch_shapes=[pltpu.SemaphoreType.DMA((n_sem,)), pltpu.SemaphoreType.DMA((n_sem,))],
    )(*bufs))


def _peers(x, y, c):
    return [(x ^ (mask >> 2), y ^ ((mask >> 1) & 1), c ^ (mask & 1)) for mask in range(1, N_DEV)]


def _received_shape(a):
    _, full, axis = BIG[a]
    return (N_DEV - 1,) + _shape_div(_shape_div(full, 1 - axis, 2), axis, N_SHARD)


def _pieces_start(tag, arrs, dws):
    n_arr = len(arrs)

    def body(*refs):
        srcs, lands = refs[:n_arr], refs[n_arr:2 * n_arr]
        send_sems, recv_sems = refs[2 * n_arr], refs[2 * n_arr + 1]
        token = refs[-1]
        x, y, c = _me()
        for i, a in enumerate(arrs):
            for k, peer in enumerate(_peers(x, y, c)):
                src = _piece(a, srcs[i], _shard_of(peer[:2]), peer[2])
                _rcopy(src, lands[i].at[k], send_sems, recv_sems, (N_DEV - 1) * i + k, peer).start()
        token[...] = jnp.zeros_like(token)

    n_sem = (N_DEV - 1) * n_arr
    lands = [_in_hbm(lax.empty(_received_shape(a), BF16)) for a in arrs]
    outs = pl.pallas_call(
        body, name="grad_pieces_start_" + tag,
        in_specs=[HBM] * (2 * n_arr),
        out_specs=[SEM, SEM] + [HBM] * (2 * n_arr) + [VMEM_FULL],
        out_shape=[pltpu.SemaphoreType.DMA((n_sem,)), pltpu.SemaphoreType.DMA((n_sem,))]
        + [pltpu.HBM(BIG[a][1], BF16) for a in arrs] + [pltpu.HBM(_received_shape(a), BF16) for a in arrs]
        + [jax.ShapeDtypeStruct((8, HEAD_DIM), F32)],
        input_output_aliases={i: 2 + i for i in range(2 * n_arr)},
        compiler_params=pltpu.CompilerParams(has_side_effects=EFFECT),
    )(*[_in_hbm(dw) for dw in dws], *lands)
    return outs[0], outs[1], list(outs[2:2 + n_arr]), list(outs[2 + n_arr:2 + 2 * n_arr]), outs[-1]


def _pieces_wait(tag, arrs, send_sems, recv_sems, dws, lands, after):
    n_arr = len(arrs)

    def body(*refs):
        srcs, lands_ = refs[:n_arr], refs[n_arr:2 * n_arr]
        send_sems_, recv_sems_ = refs[2 * n_arr], refs[2 * n_arr + 1]
        x, y, c = _me()
        for i, a in enumerate(arrs):
            for k, peer in enumerate(_peers(x, y, c)):
                src = _piece(a, srcs[i], _shard_of(peer[:2]), peer[2])
                cp = _rcopy(src, lands_[i].at[k], send_sems_, recv_sems_, (N_DEV - 1) * i + k, peer)
                cp.wait_send()
                cp.wait_recv()

    outs = pl.pallas_call(
        body, name="grad_pieces_wait_" + tag,
        in_specs=[HBM] * (2 * n_arr) + [SEM, SEM, ANY],
        out_specs=[HBM] * (2 * n_arr),
        out_shape=[pltpu.HBM(BIG[a][1], BF16) for a in arrs] + [pltpu.HBM(_received_shape(a), BF16) for a in arrs],
        input_output_aliases={i: i for i in range(2 * n_arr)},
        compiler_params=pltpu.CompilerParams(has_side_effects=EFFECT),
    )(*dws, *lands, send_sems, recv_sems, after)
    return list(outs[:n_arr]), list(outs[n_arr:])


def _join_halves(tag, g_halves):
    n_arr = len(g_halves)

    def body(*refs):
        ins, outs = refs[:n_arr], refs[n_arr:2 * n_arr]
        send_sems, recv_sems = refs[2 * n_arr:]
        x, y, c = _me()
        copies = []
        for i in range(n_arr):
            cp = _rcopy(ins[i], outs[i], send_sems, recv_sems, i, (x, y, 1 - c))
            cp.start()
            copies.append(cp)
        for cp in copies:
            cp.wait()

    return list(pl.pallas_call(
        body, name="grad_join_halves_" + tag,
        in_specs=[ANY] * n_arr, out_specs=[ANY] * n_arr,
        out_shape=[jax.ShapeDtypeStruct(g.shape, F32) for g in g_halves],
        scratch_shapes=[pltpu.SemaphoreType.DMA((n_arr,)), pltpu.SemaphoreType.DMA((n_arr,))],
    )(*g_halves))


def _piece_sum(a, dw, shard, core, received):
    name, full, axis = BIG[a]
    rows, cols = _received_shape(a)[1:]
    tr = _fit(rows, ROW_TILE)
    nbr = rows // tr

    def body(w_ref, dw_ref, rec_ref, o_ref):
        acc = dw_ref[...].astype(F32)
        for k in range(N_DEV - 1):
            acc = acc + rec_ref[k].astype(F32)
        o_ref[...] = acc

    if axis == 0:
        own = pl.BlockSpec((tr, cols), lambda i, w: (w[0] * nbr + i, w[1]))
    else:
        own = pl.BlockSpec((tr, cols), lambda i, w: (w[1] * nbr + i, w[0]))
    return pl.pallas_call(
        body, name="grad_sum_pieces_" + name,
        grid_spec=pltpu.PrefetchScalarGridSpec(
            num_scalar_prefetch=1, grid=(nbr,),
            in_specs=[own, pl.BlockSpec((N_DEV - 1, tr, cols), lambda i, w: (0, i, 0))],
            out_specs=pl.BlockSpec((tr, cols), lambda i, w: (i, 0))),
        out_shape=jax.ShapeDtypeStruct((rows, cols), F32),
        compiler_params=_params(("parallel",)),
    )(jnp.stack([shard, core]).astype(jnp.int32), dw, received)


def _scatter_begin(tag, arrs, dws):
    send_sems, recv_sems, dws, lands, token = _pieces_start(tag, arrs, dws)
    return (send_sems, recv_sems, dws, lands), token[0, 0]


def _scatter_end(tag, arrs, state, after):
    x, y, c = _me()
    send_sems, recv_sems, dws, lands = state
    dws, lands = _pieces_wait(tag, arrs, send_sems, recv_sems, dws, lands, after)
    g_own = [_piece_sum(a, dw, _shard_of((x, y)), c, r) for a, dw, r in zip(arrs, dws, lands)]
    return g_own, _join_halves(tag, g_own)


def _gather_rows(block, name):
    r, d = block.shape

    def body(in_ref, out_ref, send_sems, recv_sems):
        x, y, c = _me()
        out_ref[_dev_index(x, y, c)] = in_ref[...]
        copies = []
        for mask in range(1, N_DEV):
            peer = (x ^ (mask >> 2), y ^ ((mask >> 1) & 1), c ^ (mask & 1))
            cp = _rcopy(in_ref, out_ref.at[_dev_index(x, y, c)], send_sems, recv_sems, mask - 1, peer)
            cp.start()
            copies.append((cp, peer))
        for mask in range(1, N_DEV):
            peer = (x ^ (mask >> 2), y ^ ((mask >> 1) & 1), c ^ (mask & 1))
            landed = out_ref.at[_dev_index(*peer)]
            _rcopy(landed, landed, send_sems, recv_sems, mask - 1, peer).wait_recv()
        for cp, _ in copies:
            cp.wait_send()

    return pl.pallas_call(
        body, name=name, in_specs=[VMEM_FULL], out_specs=VMEM_FULL,
        out_shape=jax.ShapeDtypeStruct((N_DEV, r, d), F32),
        scratch_shapes=[pltpu.SemaphoreType.DMA((N_DEV - 1,)), pltpu.SemaphoreType.DMA((N_DEV - 1,))],
    )(block)


ADA_ROWS = 80
ADA_W = 6 * D_MODEL // N_SHARD


def _ada_forward(c_block, cctx_block, w_ada, b_shard):
    d = c_block.shape[1]

    def body(c_ref, cc_ref, w_ref, b_ref, act_ref, mods_ref, raw, mloc, send_sems, recv_sems):
        x, y, c = _me()
        me = _dev_index(x, y, c)
        s_me = _shard_of((x, y))
        raw[72:ADA_ROWS, :] = jnp.zeros((ADA_ROWS - 72, d), F32)
        raw[pl.ds(pl.multiple_of(me * 8, 8), 8), :] = c_ref[...]
        raw[64:72, :] = cc_ref[...]
        sends = []
        for mask in range(1, N_DEV):
            peer = (x ^ (mask >> 2), y ^ ((mask >> 1) & 1), c ^ (mask & 1))
            cp = _rcopy(c_ref, raw.at[pl.ds(pl.multiple_of(me * 8, 8), 8), :], send_sems, recv_sems, mask - 1, peer)
            cp.start()
            sends.append(cp)
        for mask in range(1, N_DEV):
            peer = (x ^ (mask >> 2), y ^ ((mask >> 1) & 1), c ^ (mask & 1))
            landed = raw.at[pl.ds(pl.multiple_of(_dev_index(*peer) * 8, 8), 8), :]
            _rcopy(landed, landed, send_sems, recv_sems, mask - 1, peer).wait_recv()
        v = raw[...]
        act = v * jax.nn.sigmoid(v)
        act_ref[...] = act
        mloc[...] = lax.dot_general(act.astype(BF16), w_ref[...].astype(BF16), NN,
                                    preferred_element_type=F32) + b_ref[...]
        mods_ref[s_me, 0:8, :] = mloc[pl.ds(pl.multiple_of(me * 8, 8), 8), :]
        mods_ref[s_me, 8:16, :] = mloc[64:72, :]
        base = N_DEV - 1
        for j, chip in enumerate(_other_chips(x, y)):
            peer = (*chip, c)
            rows = mloc.at[pl.ds(pl.multiple_of(_dev_index(*peer) * 8, 8), 8), :]
            cp = _rcopy(rows, mods_ref.at[s_me, 0:8, :], send_sems, recv_sems, base + 2 * j, peer)
            cp.start()
            sends.append(cp)
            cp = _rcopy(mloc.at[64:72, :], mods_ref.at[s_me, 8:16, :], send_sems, recv_sems, base + 2 * j + 1, peer)
            cp.start()
            sends.append(cp)
        for j, chip in enumerate(_other_chips(x, y)):
            for part in range(2):
                landed = mods_ref.at[_shard_of(chip), 8 * part:8 * part + 8, :]
                _rcopy(landed, landed, send_sems, recv_sems, base + 2 * j + part, (*chip, c)).wait_recv()
        for cp in sends:
            cp.wait_send()

    n_sem = N_DEV - 1 + 2 * N_CHIP_PEERS
    return pl.pallas_call(
        body, name="ada_forward",
        in_specs=[VMEM_FULL] * 4, out_specs=[VMEM_FULL, VMEM_FULL],
        out_shape=[jax.ShapeDtypeStruct((ADA_ROWS, d), F32), jax.ShapeDtypeStruct((N_SHARD, 16, ADA_W), F32)],
        scratch_shapes=[pltpu.VMEM((ADA_ROWS, d), F32), pltpu.VMEM((ADA_ROWS, ADA_W), F32),
                        pltpu.SemaphoreType.DMA((n_sem,)), pltpu.SemaphoreType.DMA((n_sem,))],
        compiler_params=pltpu.CompilerParams(vmem_limit_bytes=VMEM_LIMIT),
    )(c_block, cctx_block, w_ada, b_shard)


def _small_reduce(gathered):
    d = gathered.shape[2]

    def body(g_ref, o_ref):
        tot = g_ref[0]
        for i in range(1, N_DEV):
            tot = tot + g_ref[i]
        o_ref[...] = tot
        o_ref[0:2, :] = tot[0:2] + tot[6:8]
        o_ref[12:13, :] = jnp.broadcast_to(jnp.sum(tot[12:13], axis=1, keepdims=True), (1, d))

    return pl.pallas_call(body, name="small_reduce", in_specs=[VMEM_FULL], out_specs=VMEM_FULL,
                          out_shape=jax.ShapeDtypeStruct((16, d), F32))(gathered)


def _cctx_grad(gathered, c_ctx):
    d = gathered.shape[2]

    def body(g_ref, c_ref, o_ref):
        tot = g_ref[0, 0:1, :]
        for chip in range(1, N_SHARD):
            tot = tot + g_ref[2 * chip, 0:1, :]
        v = c_ref[...]
        sig = jax.nn.sigmoid(v)
        o_ref[...] = tot * (sig * (1.0 + v * (1.0 - sig)))

    return pl.pallas_call(body, name="cctx_grad", in_specs=[VMEM_FULL, VMEM_FULL], out_specs=VMEM_FULL,
                          out_shape=jax.ShapeDtypeStruct((1, d), F32))(gathered, c_ctx.reshape(1, d))


def _cast_into_full(w, shard, full, axis, name):
    r, cdim = w.shape
    tr = _fit(r, ROW_TILE)
    nbr = r // tr

    def body(s_ref, w_ref, o_ref):
        o_ref[...] = w_ref[...].astype(BF16)

    if axis == 0:
        out_spec = pl.BlockSpec((tr, cdim), lambda i, s: (s[0] * nbr + i, 0))
    else:
        out_spec = pl.BlockSpec((tr, cdim), lambda i, s: (i, s[0]))
    return pl.pallas_call(
        body, name=name,
        grid_spec=pltpu.PrefetchScalarGridSpec(
            num_scalar_prefetch=1, grid=(nbr,), in_specs=[pl.BlockSpec((tr, cdim), lambda i, s: (i, 0))],
            out_specs=out_spec),
        out_shape=jax.ShapeDtypeStruct(full, BF16), compiler_params=_params(("parallel",)),
    )(shard.reshape(1).astype(jnp.int32), w)


def _adamw_halves(w, g_own, g_other, m, v, core, axis, name):
    r, cdim = w.shape
    hr, hc = (r // 2, cdim) if axis == 1 else (r, cdim // 2)
    assert g_own.shape == (hr, hc) and g_other.shape == (hr, hc)
    tr = _fit(hr, 128)
    nb = hr // tr
    c1 = 1.0 - ADAM_B1 ** ADAM_STEP
    c2 = 1.0 - ADAM_B2 ** ADAM_STEP

    def body(c_ref, w_ref, go_ref, gt_ref, m_ref, v_ref, g_ref, d_ref, nm_ref, nv_ref):
        gv = jnp.where(pl.program_id(0) == c_ref[0], go_ref[...], gt_ref[...])
        nm = ADAM_B1 * m_ref[...] + (1.0 - ADAM_B1) * gv
        nv = ADAM_B2 * v_ref[...] + (1.0 - ADAM_B2) * (gv * gv)
        g_ref[...] = gv
        nm_ref[...] = nm
        nv_ref[...] = nv
        d_ref[...] = -ADAM_LR * ((nm / c1) / (jnp.sqrt(nv / c2) + ADAM_EPS) + ADAM_WD * w_ref[...])

    if axis == 1:
        big = pl.BlockSpec((tr, hc), lambda p, i, c: (p * nb + i, 0))
    else:
        big = pl.BlockSpec((tr, hc), lambda p, i, c: (i, p))
    half = pl.BlockSpec((tr, hc), lambda p, i, c: (i, 0))
    sh = jax.ShapeDtypeStruct((r, cdim), F32)
    return pl.pallas_call(
        body, name=name,
        grid_spec=pltpu.PrefetchScalarGridSpec(
            num_scalar_prefetch=1, grid=(2, nb), in_specs=[big, half, half, big, big], out_specs=[big] * 4),
        out_shape=[sh] * 4, compiler_params=_params(("parallel", "parallel")),
    )(core.reshape(1).astype(jnp.int32), w, g_own, g_other, m, v)


def _adamw(w, g, m, v, name):
    r, cdim = w.shape
    tr = _fit(r, 128) if r % (ROW_TILE // 4) == 0 else r
    c1 = 1.0 - ADAM_B1 ** ADAM_STEP
    c2 = 1.0 - ADAM_B2 ** ADAM_STEP

    def body(w_ref, g_ref, m_ref, v_ref, d_ref, nm_ref, nv_ref):
        gv = g_ref[...]
        nm = ADAM_B1 * m_ref[...] + (1.0 - ADAM_B1) * gv
        nv = ADAM_B2 * v_ref[...] + (1.0 - ADAM_B2) * (gv * gv)
        nm_ref[...] = nm
        nv_ref[...] = nv
        d_ref[...] = -ADAM_LR * ((nm / c1) / (jnp.sqrt(nv / c2) + ADAM_EPS) + ADAM_WD * w_ref[...])

    spec = pl.BlockSpec((tr, cdim), lambda i: (i, 0))
    sh = jax.ShapeDtypeStruct((r, cdim), F32)
    return pl.pallas_call(body, name=name, grid=(r // tr,), in_specs=[spec] * 4, out_specs=[spec] * 3,
                          out_shape=[sh, sh, sh], compiler_params=_params(("parallel",)))(w, g, m, v)


SMALL = (("c_ctx", D_MODEL), ("b_ada", 6 * D_MODEL), ("q_norm_g", HEAD_DIM), ("k_norm_g", HEAD_DIM),
         ("sink_logit", HEADS_A), ("ln1_g", D_MODEL), ("ln1_b", D_MODEL), ("ln2_g", D_MODEL), ("ln2_b", D_MODEL))
WEIGHT_ORDER = ("c_ctx", "w_ada", "b_ada", "w_in", "q_norm_g", "k_norm_g", "sink_logit", "w_out", "ln1_g", "ln1_b",
                "w_gate", "w_up", "w_down", "ln2_g", "ln2_b")


def kernel(x, c, ctx, c_ctx, w_ada, b_ada, w_in, q_norm_g, k_norm_g, sink_logit, w_out, ln1_g, ln1_b, w_gate, w_up, w_down, ln2_g, ln2_b, loss_target, m_c_ctx, m_w_ada, m_b_ada, m_w_in, m_q_norm_g, m_k_norm_g, m_sink_logit, m_w_out, m_ln1_g, m_ln1_b, m_w_gate, m_w_up, m_w_down, m_ln2_g, m_ln2_b, v_c_ctx, v_w_ada, v_b_ada, v_w_in, v_q_norm_g, v_k_norm_g, v_sink_logit, v_w_out, v_ln1_g, v_ln1_b, v_w_gate, v_w_up, v_w_down, v_ln2_g, v_ln2_b):
    d = D_MODEL
    w = dict(c_ctx=c_ctx, w_ada=w_ada[0], b_ada=b_ada, w_in=w_in[0], q_norm_g=q_norm_g, k_norm_g=k_norm_g,
             sink_logit=sink_logit, w_out=w_out[0], ln1_g=ln1_g, ln1_b=ln1_b, w_gate=w_gate[0], w_up=w_up[0],
             w_down=w_down[0], ln2_g=ln2_g, ln2_b=ln2_b)
    m = dict(c_ctx=m_c_ctx, w_ada=m_w_ada[0], b_ada=m_b_ada, w_in=m_w_in[0], q_norm_g=m_q_norm_g, k_norm_g=m_k_norm_g,
             sink_logit=m_sink_logit, w_out=m_w_out[0], ln1_g=m_ln1_g, ln1_b=m_ln1_b, w_gate=m_w_gate[0],
             w_up=m_w_up[0], w_down=m_w_down[0], ln2_g=m_ln2_g, ln2_b=m_ln2_b)
    v = dict(c_ctx=v_c_ctx, w_ada=v_w_ada[0], b_ada=v_b_ada, w_in=v_w_in[0], q_norm_g=v_q_norm_g, k_norm_g=v_k_norm_g,
             sink_logit=v_sink_logit, w_out=v_w_out[0], ln1_g=v_ln1_g, ln1_b=v_ln1_b, w_gate=v_w_gate[0],
             w_up=v_w_up[0], w_down=v_w_down[0], ln2_g=v_ln2_g, ln2_b=v_ln2_b)
    mx, my, mc = _me()
    s_me = _shard_of((mx, my))
    me = _dev_index(mx, my, mc)
    pad8 = lambda row: jnp.concatenate([row.reshape(1, -1), jnp.zeros((7, row.size), F32)], axis=0)

    b_shard = lax.dynamic_slice(b_ada, (0, s_me * ADA_W), (1, ADA_W))
    act, mods4 = _ada_forward(pad8(c), pad8(c_ctx), w["w_ada"], b_shard)

    bufs = {name: _cast_into_full(w[name], s_me, shape, axis, "cast_" + name) for name, shape, axis in BIG}
    gathers = []
    prev = mods4
    for k, names in enumerate(W_GROUPS):
        arrs = tuple(BIG_INDEX[name] for name in names)
        send_sems, recv_sems, thru, prev = _gather_start("g%d" % k, arrs, [bufs[name] for name in names], prev)
        gathers.append((arrs, send_sems, recv_sems, thru))

    def weights(k, after):
        arrs, send_sems, recv_sems, thru = gathers[k]
        landed = _gather_wait("g%d" % k, arrs, send_sems, recv_sems, thru, after)
        return _gather_forward("g%d" % k, arrs, landed)

    mod = jnp.transpose(mods4[:, 0:1, :], (1, 0, 2)).reshape(1, 6 * d) + prev[0, 0]
    mod_ctx = jnp.transpose(mods4[:, 8:9, :], (1, 0, 2)).reshape(1, 6 * d)

    scatters = {}

    def grads_out(k, dws):
        arrs = tuple(BIG_INDEX[name] for name in G_GROUPS[k])
        scatters[k], zero = _scatter_begin("g%d" % k, arrs, dws)
        return zero

    grad_x, partial = _layer_fwd_bwd(x[0], ctx[0], loss_target[0], mod, mod_ctx, weights, grads_out,
                                     q_norm_g, k_norm_g, sink_logit, ln1_g, ln1_b, ln2_g, ln2_b)
    grads, delta, new_m, new_v = {}, {}, {}, {}

    gathered = _gather_rows(partial, "gather_partials")
    tot = _small_reduce(gathered)
    grads["b_ada"] = tot[0:6].reshape(1, 6 * d)
    grads["ln1_g"], grads["ln1_b"], grads["ln2_g"], grads["ln2_b"] = tot[8:9], tot[9:10], tot[10:11], tot[11:12]
    grads["q_norm_g"] = tot[13:14, 0:HEAD_DIM]
    grads["k_norm_g"] = tot[13:14, HEAD_DIM:2 * HEAD_DIM]
    grads["sink_logit"] = tot[13:14, 2 * HEAD_DIM:2 * HEAD_DIM + HEADS_A]
    loss = tot[12, 0]

    dm_all = gathered[:, 0:6, :].reshape(N_DEV, 6 * d)
    dmc_tot = jnp.concatenate([tot[6:8].reshape(1, 2 * d), jnp.zeros((1, 4 * d), F32)], axis=1)
    dm_rows = jnp.concatenate([pad8(dm_all[i]) for i in range(N_DEV)] + [pad8(dmc_tot), jnp.zeros((8, 6 * d), F32)], axis=0)
    dm_shard = lax.dynamic_slice(dm_rows, (0, s_me * ADA_W), (ADA_ROWS, ADA_W))
    grads["w_ada"] = _matmul(act, dm_shard, name="dw_ada", ta=True, tm=1024, tn=1024, tk=ADA_ROWS, out_dtype=F32)
    dmc_shard = lax.dynamic_slice(pad8(dmc_tot), (0, s_me * ADA_W), (8, ADA_W))
    cc_part = _matmul(dmc_shard, w["w_ada"], name="d_cctx", tb=True, tm=8, tn=1024, tk=1536, out_dtype=F32)
    grads["c_ctx"] = _cctx_grad(_gather_rows(cc_part, "gather_cctx"), c_ctx).reshape(d)

    delta["w_ada"], new_m["w_ada"], new_v["w_ada"] = _adamw(w["w_ada"], grads["w_ada"], m["w_ada"], v["w_ada"],
                                                            "adamw_w_ada")
    pack = lambda t: jnp.concatenate([t[name].reshape(1, size) for name, size in SMALL], axis=1)
    pd, pm, pv = _adamw(pack(w), pack(grads), pack(m), pack(v), "adamw_small")
    off = 0
    for name, size in SMALL:
        delta[name], new_m[name], new_v[name] = [t[:, off:off + size].reshape(w[name].shape) for t in (pd, pm, pv)]
        grads[name] = grads[name].reshape(w[name].shape)
        off += size

    after = pd
    for k, names in enumerate(G_GROUPS):
        arrs = tuple(BIG_INDEX[name] for name in names)
        g_own, g_other = _scatter_end("g%d" % k, arrs, scatters[k], after)
        for name, own, other in zip(names, g_own, g_other):
            grads[name], delta[name], new_m[name], new_v[name] = _adamw_halves(
                w[name], own, other, m[name], v[name], mc, BIG[BIG_INDEX[name]][2], "adamw_" + name)
            after = new_v[name]

    lead = lambda name, t: t[None] if name in ("w_ada", "w_in", "w_out", "w_gate", "w_up", "w_down") else t
    outs = [loss, grad_x[None]]
    for group in (grads, delta, new_m, new_v):
        outs += [lead(name, group[name]) for name in WEIGHT_ORDER]
    return tuple(outs)
```

```python
import functools
import math

import jax
import jax.numpy as jnp
from jax import lax
from jax.experimental import pallas as pl
from jax.experimental.pallas import tpu as pltpu

F32 = jnp.float32
BF16 = jnp.bfloat16
MESH = pl.DeviceIdType.MESH

D_MODEL = 2048
HEAD_DIM = 128
HEADS_A = 8
HEADS_B = 8
KV_A = 2
KV_B = 2
GROUP = 4
GRID_W = 64
WINDOW = 128
BLOCK = 128
FFN = 5632
IN_WIDTH = 3072
MIX_WIDTH = 2048
ROPE_THETA = 10000.0
EPS = 1e-6
ATTN_SCALE = HEAD_DIM ** -0.5
DN_ALPHA = 2.0 ** 0.25
N_SHARD = 4
N_DEV = 8

ADAM_LR = 0.001
ADAM_B1 = 0.9
ADAM_B2 = 0.999
ADAM_EPS = 1e-08
ADAM_WD = 0.01
ADAM_STEP = 10

QA0, KA0, VA0, QB0, KB0, VB0 = 0, 1024, 1280, 1536, 2560, 2816

VMEM_LIMIT = 56 * 1024 * 1024
ROW_TILE = 256
NN = (((1,), (0,)), ((), ()))
NT = (((1,), (1,)), ((), ()))
TN = (((0,), (0,)), ((), ()))


def _fit(total, pref):
    step = ROW_TILE // 4
    best = step
    for cand in range(step, pref + 1, step):
        if total % cand == 0:
            best = cand
    return best


def _params(sem=None):
    return pltpu.CompilerParams(dimension_semantics=sem, vmem_limit_bytes=VMEM_LIMIT)


def _matmul(a, b, *, name, ta=False, tb=False, tm, tn, tk, out_dtype, after=None):
    m = a.shape[1] if ta else a.shape[0]
    k = a.shape[0] if ta else a.shape[1]
    n = b.shape[0] if tb else b.shape[1]
    assert (b.shape[1] if tb else b.shape[0]) == k
    tm, tn, tk = min(tm, m), min(tn, n), min(tk, k)
    assert m % tm == 0 and n % tn == 0 and k % tk == 0, (name, m, n, k, tm, tn, tk)
    nk = k // tk
    dn = (((0 if ta else 1,), (1 if tb else 0,)), ((), ()))

    def product(a_ref, b_ref):
        return lax.dot_general(a_ref[...].astype(BF16), b_ref[...].astype(BF16), dn, preferred_element_type=F32)

    def body_whole_k(a_ref, b_ref, *rest):
        o_ref = rest[-1]
        o_ref[...] = product(a_ref, b_ref).astype(o_ref.dtype)

    def body(a_ref, b_ref, *rest):
        o_ref, acc_ref = rest[-2:]
        kk = pl.program_id(2)
        part = product(a_ref, b_ref)

        @pl.when(kk == 0)
        def _():
            acc_ref[...] = part

        @pl.when(kk != 0)
        def _():
            acc_ref[...] += part

        @pl.when(kk == nk - 1)
        def _():
            o_ref[...] = acc_ref[...].astype(o_ref.dtype)

    a_spec = (pl.BlockSpec((tk, tm), lambda i, j, kk: (kk, i)) if ta
              else pl.BlockSpec((tm, tk), lambda i, j, kk: (i, kk)))
    b_spec = (pl.BlockSpec((tn, tk), lambda i, j, kk: (j, kk)) if tb
              else pl.BlockSpec((tk, tn), lambda i, j, kk: (kk, j)))
    return pl.pallas_call(
        body_whole_k if nk == 1 else body, name=name, grid=(m // tm, n // tn, nk),
        in_specs=[a_spec, b_spec] + ([] if after is None else [pl.BlockSpec(memory_space=pl.ANY)]),
        out_specs=pl.BlockSpec((tm, tn), lambda i, j, kk: (i, j)),
        out_shape=jax.ShapeDtypeStruct((m, n), out_dtype),
        scratch_shapes=[] if nk == 1 else [pltpu.VMEM((tm, tn), F32)],
        compiler_params=_params(("parallel", "parallel", "arbitrary")),
    )(a, b, *([] if after is None else [after]))


def _modulate_rows(x, ctx, mods):
    n, d = x.shape
    c = ctx.shape[0]
    nx = n // ROW_TILE
    assert c == ROW_TILE

    def body(x_ref, ctx_ref, mods_ref, o_ref):
        i = pl.program_id(0)

        @pl.when(i < nx)
        def _():
            o_ref[...] = (x_ref[...] * (1.0 + mods_ref[0:1, :]) + mods_ref[1:2, :]).astype(BF16)

        @pl.when(i >= nx)
        def _():
            o_ref[...] = (ctx_ref[...] * (1.0 + mods_ref[2:3, :]) + mods_ref[3:4, :]).astype(BF16)

    return pl.pallas_call(
        body, name="modulate_rows", grid=(nx + 1,),
        in_specs=[pl.BlockSpec((ROW_TILE, d), lambda i: (jnp.minimum(i, nx - 1), 0)),
                  pl.BlockSpec((ROW_TILE, d), lambda i: (0, 0)),
                  pl.BlockSpec((8, d), lambda i: (0, 0))],
        out_specs=pl.BlockSpec((ROW_TILE, d), lambda i: (i, 0)),
        out_shape=jax.ShapeDtypeStruct((n + c, d), BF16),
        compiler_params=_params(("parallel",)),
    )(x, ctx, mods)


def _rope_tables(n, c):
    rows = n // GRID_W
    row_ids = jnp.repeat(jnp.arange(rows, dtype=F32), GRID_W)
    col_ids = jnp.tile(jnp.arange(GRID_W, dtype=F32), rows)
    axis_dim = HEAD_DIM // 2
    inv_freq = jnp.power(ROPE_THETA, -jnp.arange(0, axis_dim, 2, dtype=F32) / axis_dim)
    ang_r = row_ids[:, None] * inv_freq
    ang_c = col_ids[:, None] * inv_freq
    ang = jnp.concatenate([ang_r, ang_r, ang_c, ang_c], axis=-1)
    cos, sin = jnp.cos(ang), jnp.sin(ang)
    quarter = (jnp.arange(HEAD_DIM) // (HEAD_DIM // 4)) % 2
    sin_a = jnp.where(quarter == 0, -sin, 0.0)
    sin_b = jnp.where(quarter == 1, sin, 0.0)
    pad = lambda t, v: jnp.concatenate([t, jnp.full((c, HEAD_DIM), v, F32)], axis=0)
    return pad(cos, 1.0), pad(sin_a, 0.0), pad(sin_b, 0.0)


def _rope(x, cos, sin_a, sin_b):
    return x * cos + pltpu.roll(x, 96, 1) * sin_a + pltpu.roll(x, 32, 1) * sin_b


def _rope_t(dy, cos, sin_a, sin_b):
    return dy * cos - pltpu.roll(dy, 96, 1) * sin_a - pltpu.roll(dy, 32, 1) * sin_b


def _rms(x):
    r = lax.rsqrt(jnp.mean(x * x, axis=-1, keepdims=True) + EPS)
    return x * r, r


def _qkv_post(h_all, cos, sin_a, sin_b, q_g, k_g):
    t = h_all.shape[0]
    nt = t // ROW_TILE

    def body(h_ref, cos_ref, sa_ref, sb_ref, qg_ref, kg_ref, qa_ref, ka_ref, va_ref, qb_ref, kb_ref, vb_ref):
        cos_, sa, sb = cos_ref[...], sa_ref[...], sb_ref[...]
        sl = lambda off, hh: h_ref[:, off + hh * HEAD_DIM: off + (hh + 1) * HEAD_DIM]
        for hh in range(HEADS_A):
            qa_ref[hh] = (_rope(sl(QA0, hh), cos_, sa, sb) * ATTN_SCALE).astype(BF16)
        for hh in range(KV_A):
            ka_ref[hh] = _rope(sl(KA0, hh), cos_, sa, sb).astype(BF16)
            va_ref[hh] = sl(VA0, hh).astype(BF16)
        for hh in range(HEADS_B):
            xn, _ = _rms(sl(QB0, hh))
            qb_ref[hh] = (_rope(xn * qg_ref[...], cos_, sa, sb) * ATTN_SCALE).astype(BF16)
        for hh in range(KV_B):
            xn, _ = _rms(sl(KB0, hh))
            kb_ref[hh] = _rope(xn * kg_ref[...], cos_, sa, sb).astype(BF16)
            vb_ref[hh] = sl(VB0, hh).astype(BF16)

    tab = pl.BlockSpec((ROW_TILE, HEAD_DIM), lambda i: (i, 0))
    gain = pl.BlockSpec((1, HEAD_DIM), lambda i: (0, 0))
    hs = lambda nh: pl.BlockSpec((nh, ROW_TILE, HEAD_DIM), lambda i: (0, i, 0))
    sh = lambda nh: jax.ShapeDtypeStruct((nh, t, HEAD_DIM), BF16)
    return pl.pallas_call(
        body, name="qkv_post", grid=(nt,),
        in_specs=[pl.BlockSpec((ROW_TILE, IN_WIDTH), lambda i: (i, 0)), tab, tab, tab, gain, gain],
        out_specs=[hs(HEADS_A), hs(KV_A), hs(KV_A), hs(HEADS_B), hs(KV_B), hs(KV_B)],
        out_shape=[sh(HEADS_A), sh(KV_A), sh(KV_A), sh(HEADS_B), sh(KV_B), sh(KV_B)],
        compiler_params=_params(("parallel",)),
    )(h_all, cos, sin_a, sin_b, q_g, k_g)


def _qkv_bwd_post(h_all, cos, sin_a, sin_b, q_g, k_g, dqa, dka, dva, dqb, dkb, dvb, n):
    t = h_all.shape[0]
    nt = t // ROW_TILE
    nx = n // ROW_TILE

    def body(h_ref, cos_ref, sa_ref, sb_ref, qg_ref, kg_ref,
             dqa_ref, dka_ref, dva_ref, dqb_ref, dkb_ref, dvb_ref, dh_ref, gs_ref):
        i = pl.program_id(0)
        cos_, sa, sb = cos_ref[...], sa_ref[...], sb_ref[...]
        latent = (i < nx).astype(F32)
        sl = lambda off, hh: h_ref[:, off + hh * HEAD_DIM: off + (hh + 1) * HEAD_DIM]

        def put(off, hh, val):
            dh_ref[:, off + hh * HEAD_DIM: off + (hh + 1) * HEAD_DIM] = val.astype(BF16)

        def norm_bwd(x, gain, dy):
            xn, r = _rms(x)
            dxh = dy * gain
            dx = r * (dxh - xn * jnp.mean(dxh * xn, axis=-1, keepdims=True))
            return dx, jnp.sum(dy * xn, axis=0, keepdims=True)

        for hh in range(HEADS_A):
            put(QA0, hh, _rope_t(dqa_ref[hh] * (ATTN_SCALE * latent), cos_, sa, sb))
        for hh in range(KV_A):
            put(KA0, hh, _rope_t(dka_ref[hh], cos_, sa, sb))
            put(VA0, hh, dva_ref[hh])
        gq = jnp.zeros((1, HEAD_DIM), F32)
        gk = jnp.zeros((1, HEAD_DIM), F32)
        for hh in range(HEADS_B):
            dq_t = dqb_ref[hh // GROUP, :, (hh % GROUP) * ROW_TILE:(hh % GROUP + 1) * ROW_TILE]
            dy = _rope_t(dq_t.T * (ATTN_SCALE * latent), cos_, sa, sb)
            dx, g = norm_bwd(sl(QB0, hh), qg_ref[...], dy)
            put(QB0, hh, dx)
            gq = gq + g
        for hh in range(KV_B):
            dy = _rope_t(dkb_ref[hh], cos_, sa, sb)
            dx, g = norm_bwd(sl(KB0, hh), kg_ref[...], dy)
            put(KB0, hh, dx)
            gk = gk + g
            put(VB0, hh, dvb_ref[hh])
        upd = jnp.concatenate([gq, gk, jnp.zeros((6, HEAD_DIM), F32)], axis=0)

        @pl.when(i == 0)
        def _():
            gs_ref[...] = upd

        @pl.when(i != 0)
        def _():
            gs_ref[...] += upd

    tab = pl.BlockSpec((ROW_TILE, HEAD_DIM), lambda i: (i, 0))
    gain = pl.BlockSpec((1, HEAD_DIM), lambda i: (0, 0))
    lat = lambda nh: pl.BlockSpec((nh, ROW_TILE, HEAD_DIM), lambda i: (0, jnp.minimum(i, nx - 1), 0))
    full = lambda nh: pl.BlockSpec((nh, ROW_TILE, HEAD_DIM), lambda i: (0, i, 0))
    return pl.pallas_call(
        body, name="qkv_bwd_post", grid=(nt,),
        in_specs=[pl.BlockSpec((ROW_TILE, IN_WIDTH), lambda i: (i, 0)), tab, tab, tab, gain, gain,
                  lat(HEADS_A), full(KV_A), full(KV_A),
                  pl.BlockSpec((KV_B, None, HEAD_DIM, GROUP * ROW_TILE), lambda i: (0, jnp.minimum(i, nx - 1), 0, 0)),
                  full(KV_B), full(KV_B)],
        out_specs=[pl.BlockSpec((ROW_TILE, IN_WIDTH), lambda i: (i, 0)),
                   pl.BlockSpec((8, HEAD_DIM), lambda i: (0, 0))],
        out_shape=[jax.ShapeDtypeStruct((t, IN_WIDTH), BF16), jax.ShapeDtypeStruct((8, HEAD_DIM), F32)],
        compiler_params=_params(("arbitrary",)),
    )(h_all, cos, sin_a, sin_b, q_g, k_g, dqa, dka, dva, dqb, dkb, dvb)


GB_TQ = 256
GB_TK = 256


def _heads_rows(ref2d, tq):
    return jnp.concatenate([ref2d[:, hh * HEAD_DIM:(hh + 1) * HEAD_DIM] for hh in range(GROUP)], axis=0)


def _attn_b_fwd(qb, kb, vb, heads_a, n):
    t = kb.shape[1]
    nk = t // GB_TK
    tq = GB_TQ
    rows = GROUP * tq

    def body(q_ref, k_ref, v_ref, heads_a_ref, o_ref, lse_ref, m_s, l_s, acc_s):
        q = q_ref[...].reshape(rows, HEAD_DIM)
        m_s[...] = jnp.full((1, rows), -jnp.inf, F32)
        l_s[...] = jnp.zeros((1, rows), F32)
        acc_s[...] = jnp.zeros((HEAD_DIM, rows), F32)

        def scores(j):
            start = pl.multiple_of(j * GB_TK, GB_TK)
            return lax.dot_general(k_ref[pl.ds(start, GB_TK), :], q, NT, preferred_element_type=F32)

        def step(j, st):
            st_next = scores(jnp.minimum(j + 1, nk - 1))
            vs = v_ref[pl.ds(pl.multiple_of(j * GB_TK, GB_TK), GB_TK), :]
            m_prev = m_s[...]
            m_new = jnp.maximum(m_prev, jnp.max(st, axis=0, keepdims=True))
            p = jnp.exp(st - m_new)
            alpha = jnp.exp(m_prev - m_new)
            l_s[...] = alpha * l_s[...] + jnp.sum(p, axis=0, keepdims=True)
            acc_s[...] = alpha * acc_s[...] + lax.dot_general(vs, p.astype(BF16), TN, preferred_element_type=F32)
            m_s[...] = m_new
            return st_next

        lax.fori_loop(0, nk, step, scores(0))
        ot = acc_s[...] * (1.0 / l_s[...])
        lse_ref[...] = m_s[...] + jnp.log(l_s[...])
        for hh in range(GROUP):
            o_ref[:, hh * HEAD_DIM:(hh + 1) * HEAD_DIM] = ot[:, hh * tq:(hh + 1) * tq].T.astype(BF16)

    return pl.pallas_call(
        body, name="attn_b_fwd", grid=(KV_B, n // tq),
        in_specs=[pl.BlockSpec((GROUP, tq, HEAD_DIM), lambda g, i: (g, i, 0)),
                  pl.BlockSpec((None, t, HEAD_DIM), lambda g, i: (g, 0, 0)),
                  pl.BlockSpec((None, t, HEAD_DIM), lambda g, i: (g, 0, 0)),
                  pl.BlockSpec(memory_space=pl.ANY)],
        out_specs=[pl.BlockSpec((tq, GROUP * HEAD_DIM), lambda g, i: (i, KV_A + g)),
                   pl.BlockSpec((None, None, 1, rows), lambda g, i: (g, i, 0, 0))],
        out_shape=[jax.ShapeDtypeStruct((n, MIX_WIDTH), BF16),
                   jax.ShapeDtypeStruct((KV_B, n // tq, 1, rows), F32)],
        input_output_aliases={3: 0},
        scratch_shapes=[pltpu.VMEM((1, rows), F32), pltpu.VMEM((1, rows), F32), pltpu.VMEM((HEAD_DIM, rows), F32)],
        compiler_params=_params(("parallel", "parallel")),
    )(qb, kb, vb, heads_a)


def _attn_b_bwd(qb, kb, vb, dheads, lse, delta, n):
    t = kb.shape[1]
    nk = t // GB_TK
    tq = GB_TQ
    nq = n // tq
    rows = GROUP * tq

    def body(q_ref, k_ref, v_ref, do_ref, lse_ref, dl_ref, dq_ref, dk_ref, dv_ref):
        j = pl.program_id(1)
        i = pl.program_id(2)

        q = q_ref[...].reshape(rows, HEAD_DIM)
        do = _heads_rows(do_ref, tq)
        ks, vs = k_ref[...], v_ref[...]
        st = lax.dot_general(ks, q, NT, preferred_element_type=F32)
        p = jnp.exp(st - lse_ref[...])
        dpt = lax.dot_general(vs, do, NT, preferred_element_type=F32)
        ds = (p * (dpt - dl_ref[...])).astype(BF16)
        dv_part = lax.dot_general(p.astype(BF16), do, NN, preferred_element_type=F32)
        dk_part = lax.dot_general(ds, q, NN, preferred_element_type=F32)
        dq_part = lax.dot_general(ks, ds, TN, preferred_element_type=F32)

        @pl.when(i == 0)
        def _():
            dk_ref[...] = dk_part
            dv_ref[...] = dv_part

        @pl.when(i != 0)
        def _():
            dk_ref[...] += dk_part
            dv_ref[...] += dv_part

        @pl.when(j == 0)
        def _():
            dq_ref[i] = dq_part

        @pl.when(j != 0)
        def _():
            dq_ref[i] += dq_part

    kv = pl.BlockSpec((None, GB_TK, HEAD_DIM), lambda g, j, i: (g, j, 0))
    row = pl.BlockSpec((None, None, 1, rows), lambda g, j, i: (g, i, 0, 0))
    return pl.pallas_call(
        body, name="attn_b_bwd", grid=(KV_B, nk, nq),
        in_specs=[pl.BlockSpec((GROUP, tq, HEAD_DIM), lambda g, j, i: (g, i, 0)), kv, kv,
                  pl.BlockSpec((tq, GROUP * HEAD_DIM), lambda g, j, i: (i, KV_A + g)), row, row],
        out_specs=[pl.BlockSpec((None, nq, HEAD_DIM, rows), lambda g, j, i: (g, 0, 0, 0)), kv, kv],
        out_shape=[jax.ShapeDtypeStruct((KV_B, nq, HEAD_DIM, rows), F32),
                   jax.ShapeDtypeStruct((KV_B, t, HEAD_DIM), F32),
                   jax.ShapeDtypeStruct((KV_B, t, HEAD_DIM), F32)],
        compiler_params=_params(("parallel", "arbitrary", "arbitrary")),
    )(qb, kb, vb, dheads, lse, delta)


def _delta_rows(dheads, heads):
    n = heads.shape[0]
    tq = GB_TQ
    w = GROUP * HEAD_DIM

    def body(a_ref, b_ref, o_ref):
        prod = a_ref[...].astype(F32) * b_ref[...].astype(F32)
        cols = [jnp.sum(prod[:, hh * HEAD_DIM:(hh + 1) * HEAD_DIM].T, axis=0, keepdims=True) for hh in range(GROUP)]
        o_ref[...] = jnp.concatenate(cols, axis=1)

    blk = pl.BlockSpec((tq, w), lambda g, i: (i, KV_A + g))
    return pl.pallas_call(
        body, name="delta_rows", grid=(KV_B, n // tq),
        in_specs=[blk, blk],
        out_specs=pl.BlockSpec((None, None, 1, GROUP * tq), lambda g, i: (g, i, 0, 0)),
        out_shape=jax.ShapeDtypeStruct((KV_B, n // tq, 1, GROUP * tq), F32),
        compiler_params=_params(("parallel", "parallel")),
    )(dheads, heads)


KWIN = 3 * BLOCK


def _window_scores(q, k_ref, j, n, nb, sink_row):
    c = k_ref.shape[0] - n
    start = pl.multiple_of(jnp.clip(j - 1, 0, nb - 3) * BLOCK, BLOCK)
    kw = k_ref[pl.ds(start, KWIN), :]
    kc = k_ref[pl.ds(n, c), :]
    s_loc = lax.dot_general(kw, q, NT, preferred_element_type=F32)
    s_ctx = lax.dot_general(kc, q, NT, preferred_element_type=F32)
    cols = GROUP * BLOCK
    qpos = j * BLOCK + lax.broadcasted_iota(jnp.int32, (KWIN, cols), 1) % BLOCK
    kpos = start + lax.broadcasted_iota(jnp.int32, (KWIN, cols), 0)
    s_loc = jnp.where(jnp.abs(qpos - kpos) <= WINDOW, s_loc, -jnp.inf)
    m = jnp.maximum(jnp.maximum(jnp.max(s_loc, axis=0, keepdims=True), jnp.max(s_ctx, axis=0, keepdims=True)),
                    sink_row)
    e_loc, e_ctx, e_sink = jnp.exp(s_loc - m), jnp.exp(s_ctx - m), jnp.exp(sink_row - m)
    inv = 1.0 / (jnp.sum(e_loc, axis=0, keepdims=True) + jnp.sum(e_ctx, axis=0, keepdims=True) + e_sink)
    return e_loc * inv, e_ctx * inv, e_sink * inv, start


def _sink_row(sink_ref, g):
    return jnp.concatenate([sink_ref[pl.ds(g * GROUP + hh, 1), :] for hh in range(GROUP)], axis=1)


def _attn_a_fwd(qa, ka, va, sink_b, n):
    t = ka.shape[1]
    nb = n // BLOCK
    assert nb >= 3

    def body(q_ref, k_ref, v_ref, sink_ref, o_ref):
        g, j = pl.program_id(0), pl.program_id(1)
        q = q_ref[...].reshape(GROUP * BLOCK, HEAD_DIM)
        p_loc, p_ctx, _, start = _window_scores(q, k_ref, j, n, nb, _sink_row(sink_ref, g))
        vw = v_ref[pl.ds(start, KWIN), :]
        vc = v_ref[pl.ds(n, t - n), :]
        ot = (lax.dot_general(vw, p_loc.astype(BF16), TN, preferred_element_type=F32)
              + lax.dot_general(vc, p_ctx.astype(BF16), TN, preferred_element_type=F32))
        for hh in range(GROUP):
            o_ref[:, hh * HEAD_DIM:(hh + 1) * HEAD_DIM] = ot[:, hh * BLOCK:(hh + 1) * BLOCK].T.astype(BF16)

    return pl.pallas_call(
        body, name="attn_a_fwd", grid=(KV_A, nb),
        in_specs=[pl.BlockSpec((GROUP, BLOCK, HEAD_DIM), lambda g, j: (g, j, 0)),
                  pl.BlockSpec((None, t, HEAD_DIM), lambda g, j: (g, 0, 0)),
                  pl.BlockSpec((None, t, HEAD_DIM), lambda g, j: (g, 0, 0)),
                  pl.BlockSpec((HEADS_A, HEAD_DIM), lambda g, j: (0, 0))],
        out_specs=pl.BlockSpec((BLOCK, GROUP * HEAD_DIM), lambda g, j: (j, g)),
        out_shape=jax.ShapeDtypeStruct((n, MIX_WIDTH), BF16),
        compiler_params=_params(("parallel", "parallel")),
    )(qa, ka, va, sink_b)


def _attn_a_bwd(qa, ka, va, sink_b, dheads, n):
    t = ka.shape[1]
    c = t - n
    nb = n // BLOCK

    def body(q_ref, k_ref, v_ref, sink_ref, do_ref, dq_ref, dk_ref, dv_ref, dsink_ref):
        g, j = pl.program_id(0), pl.program_id(1)

        @pl.when(j == 0)
        def _():
            dk_ref[...] = jnp.zeros_like(dk_ref)
            dv_ref[...] = jnp.zeros_like(dv_ref)
            dsink_ref[...] = jnp.zeros_like(dsink_ref)

        q = q_ref[...].reshape(GROUP * BLOCK, HEAD_DIM)
        do = _heads_rows(do_ref, BLOCK)
        p_loc, p_ctx, p_sink, start = _window_scores(q, k_ref, j, n, nb, _sink_row(sink_ref, g))
        kw, vw = k_ref[pl.ds(start, KWIN), :], v_ref[pl.ds(start, KWIN), :]
        kc, vc = k_ref[pl.ds(n, c), :], v_ref[pl.ds(n, c), :]
        dp_loc = lax.dot_general(vw, do, NT, preferred_element_type=F32)
        dp_ctx = lax.dot_general(vc, do, NT, preferred_element_type=F32)
        dl = jnp.sum(p_loc * dp_loc, axis=0, keepdims=True) + jnp.sum(p_ctx * dp_ctx, axis=0, keepdims=True)
        ds_loc = (p_loc * (dp_loc - dl)).astype(BF16)
        ds_ctx = (p_ctx * (dp_ctx - dl)).astype(BF16)
        dqt = (lax.dot_general(kw, ds_loc, TN, preferred_element_type=F32)
               + lax.dot_general(kc, ds_ctx, TN, preferred_element_type=F32))
        for hh in range(GROUP):
            dq_ref[hh] = dqt[:, hh * BLOCK:(hh + 1) * BLOCK].T
        dk_ref[pl.ds(start, KWIN), :] += lax.dot_general(ds_loc, q, NN, preferred_element_type=F32)
        dv_ref[pl.ds(start, KWIN), :] += lax.dot_general(p_loc.astype(BF16), do, NN, preferred_element_type=F32)
        dk_ref[pl.ds(n, c), :] += lax.dot_general(ds_ctx, q, NN, preferred_element_type=F32)
        dv_ref[pl.ds(n, c), :] += lax.dot_general(p_ctx.astype(BF16), do, NN, preferred_element_type=F32)
        dsk = -(p_sink * dl)
        upd = [jnp.broadcast_to(jnp.sum(dsk[:, hh * BLOCK:(hh + 1) * BLOCK], axis=1, keepdims=True), (1, HEAD_DIM))
               for hh in range(GROUP)]
        dsink_ref[...] += jnp.concatenate(upd + [jnp.zeros((8 - GROUP, HEAD_DIM), F32)], axis=0)

    res = pl.BlockSpec((None, t, HEAD_DIM), lambda g, j: (g, 0, 0))
    return pl.pallas_call(
        body, name="attn_a_bwd", grid=(KV_A, nb),
        in_specs=[pl.BlockSpec((GROUP, BLOCK, HEAD_DIM), lambda g, j: (g, j, 0)), res, res,
                  pl.BlockSpec((HEADS_A, HEAD_DIM), lambda g, j: (0, 0)),
                  pl.BlockSpec((BLOCK, GROUP * HEAD_DIM), lambda g, j: (j, g))],
        out_specs=[pl.BlockSpec((GROUP, BLOCK, HEAD_DIM), lambda g, j: (g, j, 0)), res, res,
                   pl.BlockSpec((None, 8, HEAD_DIM), lambda g, j: (g, 0, 0))],
        out_shape=[jax.ShapeDtypeStruct((HEADS_A, n, HEAD_DIM), F32),
                   jax.ShapeDtypeStruct((KV_A, t, HEAD_DIM), F32),
                   jax.ShapeDtypeStruct((KV_A, t, HEAD_DIM), F32),
                   jax.ShapeDtypeStruct((KV_A, 8, HEAD_DIM), F32)],
        compiler_params=_params(("parallel", "arbitrary")),
    )(qa, ka, va, sink_b, dheads)


def _ln_stats(r):
    mu = jnp.mean(r, axis=-1, keepdims=True)
    cen = r - mu
    rstd = lax.rsqrt(jnp.mean(cen * cen, axis=-1, keepdims=True) + EPS)
    return cen * rstd, rstd


def _ln_bwd(dy, xhat, rstd, gain):
    dxh = dy * gain
    return rstd * (dxh - jnp.mean(dxh, axis=-1, keepdims=True)
                   - xhat * jnp.mean(dxh * xhat, axis=-1, keepdims=True))


def _accumulate_rows(ref, rows, i):
    pad = [jnp.zeros_like(rows[0])] * (8 - len(rows))
    upd = jnp.concatenate(rows + pad, axis=0)

    @pl.when(i == 0)
    def _():
        ref[...] = upd

    @pl.when(i != 0)
    def _():
        ref[...] += upd


def _colsum(v):
    return jnp.sum(v, axis=0, keepdims=True)


LN_TILE = 256


def _res_ln1(x, a, vec):
    n, d = x.shape

    def body(x_ref, a_ref, v_ref, xh_ref, rs_ref, u_ref):
        r1 = DN_ALPHA * x_ref[...] + v_ref[0:1, :] * a_ref[...]
        xhat, rstd = _ln_stats(r1)
        xh_ref[...] = xhat
        rs_ref[...] = rstd
        x1 = xhat * v_ref[1:2, :] + v_ref[2:3, :]
        u_ref[...] = (x1 * (1.0 + v_ref[3:4, :]) + v_ref[4:5, :]).astype(BF16)

    row = pl.BlockSpec((LN_TILE, d), lambda i: (i, 0))
    return pl.pallas_call(
        body, name="res_ln1", grid=(n // LN_TILE,),
        in_specs=[row, row, pl.BlockSpec((8, d), lambda i: (0, 0))],
        out_specs=[row, pl.BlockSpec((LN_TILE, 1), lambda i: (i, 0)), row],
        out_shape=[jax.ShapeDtypeStruct((n, d), F32), jax.ShapeDtypeStruct((n, 1), F32),
                   jax.ShapeDtypeStruct((n, d), BF16)],
        compiler_params=_params(("parallel",)),
    )(x, a, vec)


def _res_ln2_loss(xhat1, f, target, vec):
    n, d = f.shape

    def body(xh_ref, f_ref, t_ref, v_ref, dr_ref, df_ref, s_ref):
        i = pl.program_id(0)
        x1 = xh_ref[...] * v_ref[1:2, :] + v_ref[2:3, :]
        fv = f_ref[...]
        xhat, rstd = _ln_stats(DN_ALPHA * x1 + v_ref[0:1, :] * fv)
        err = xhat * v_ref[3:4, :] + v_ref[4:5, :] - t_ref[...]
        dy = err * (1.0 / d)
        dr2 = _ln_bwd(dy, xhat, rstd, v_ref[3:4, :])
        dr_ref[...] = dr2
        df_ref[...] = (dr2 * v_ref[0:1, :]).astype(BF16)
        _accumulate_rows(s_ref, [_colsum(dy * xhat), _colsum(dy), _colsum(dr2 * fv),
                                 _colsum(err * err) * (0.5 / d)], i)

    row = pl.BlockSpec((LN_TILE, d), lambda i: (i, 0))
    return pl.pallas_call(
        body, name="res_ln2_loss", grid=(n // LN_TILE,),
        in_specs=[row, row, row, pl.BlockSpec((8, d), lambda i: (0, 0))],
        out_specs=[row, row, pl.BlockSpec((8, d), lambda i: (0, 0))],
        out_shape=[jax.ShapeDtypeStruct((n, d), F32), jax.ShapeDtypeStruct((n, d), BF16),
                   jax.ShapeDtypeStruct((8, d), F32)],
        compiler_params=_params(("arbitrary",)),
    )(xhat1, f, target, vec)


def _ln1_bwd(du2, dr2, xhat1, rstd1, a, vec):
    n, d = du2.shape

    def body(du_ref, dr2_ref, xh_ref, rs_ref, a_ref, v_ref, dxp_ref, da_ref, s_ref):
        i = pl.program_id(0)
        du, xhat = du_ref[...], xh_ref[...]
        x1 = xhat * v_ref[1:2, :] + v_ref[2:3, :]
        dx1 = DN_ALPHA * dr2_ref[...] + du * (1.0 + v_ref[0:1, :])
        dr1 = _ln_bwd(dx1, xhat, rs_ref[...], v_ref[1:2, :])
        dxp_ref[...] = DN_ALPHA * dr1
        da_ref[...] = (dr1 * v_ref[3:4, :]).astype(BF16)
        _accumulate_rows(s_ref, [_colsum(du * x1), _colsum(du), _colsum(dx1 * xhat), _colsum(dx1),
                                 _colsum(dr1 * a_ref[...])], i)

    row = pl.BlockSpec((LN_TILE, d), lambda i: (i, 0))
    return pl.pallas_call(
        body, name="ln1_bwd", grid=(n // LN_TILE,),
        in_specs=[row, row, row, pl.BlockSpec((LN_TILE, 1), lambda i: (i, 0)), row,
                  pl.BlockSpec((8, d), lambda i: (0, 0))],
        out_specs=[row, row, pl.BlockSpec((8, d), lambda i: (0, 0))],
        out_shape=[jax.ShapeDtypeStruct((n, d), F32), jax.ShapeDtypeStruct((n, d), BF16),
                   jax.ShapeDtypeStruct((8, d), F32)],
        compiler_params=_params(("arbitrary",)),
    )(du2, dr2, xhat1, rstd1, a, vec)


def _mod1_bwd(du_all, dxp, x, ctx, mods):
    n, d = x.shape
    nx = n // ROW_TILE

    def body(du_ref, dxp_ref, x_ref, ctx_ref, m_ref, gx_ref, s_ref):
        i = pl.program_id(0)
        du = du_ref[...]
        zero = jnp.zeros((1, d), F32)

        @pl.when(i == 0)
        def _():
            s_ref[...] = jnp.zeros_like(s_ref)

        @pl.when(i < nx)
        def _():
            gx_ref[...] = dxp_ref[...] + du * (1.0 + m_ref[0:1, :])
            s_ref[...] += jnp.concatenate([_colsum(du * x_ref[...]), _colsum(du)] + [zero] * 6, axis=0)

        @pl.when(i >= nx)
        def _():
            s_ref[...] += jnp.concatenate([zero, zero, _colsum(du * ctx_ref[...]), _colsum(du)] + [zero] * 4, axis=0)

    lat = pl.BlockSpec((ROW_TILE, d), lambda i: (jnp.minimum(i, nx - 1), 0))
    return pl.pallas_call(
        body, name="mod1_bwd", grid=(nx + 1,),
        in_specs=[pl.BlockSpec((ROW_TILE, d), lambda i: (i, 0)), lat, lat,
                  pl.BlockSpec((ROW_TILE, d), lambda i: (0, 0)), pl.BlockSpec((8, d), lambda i: (0, 0))],
        out_specs=[lat, pl.BlockSpec((8, d), lambda i: (0, 0))],
        out_shape=[jax.ShapeDtypeStruct((n, d), F32), jax.ShapeDtypeStruct((8, d), F32)],
        compiler_params=_params(("arbitrary",)),
    )(du_all, dxp, x, ctx, mods)


FFN_TM = 512
FFN_TN = 512


def _gate_up(u2, wg, wu):
    n, d = u2.shape
    f = wg.shape[1]

    def body(u_ref, wg_ref, wu_ref, g_ref, up_ref, h_ref):
        u = u_ref[...]
        g = lax.dot_general(u, wg_ref[...], NN, preferred_element_type=F32)
        up = lax.dot_general(u, wu_ref[...], NN, preferred_element_type=F32)
        g_ref[...] = g.astype(BF16)
        up_ref[...] = up.astype(BF16)
        h_ref[...] = (g * jax.nn.sigmoid(g) * up).astype(BF16)

    wspec = pl.BlockSpec((d, FFN_TN), lambda j, i: (0, j))
    ospec = pl.BlockSpec((FFN_TM, FFN_TN), lambda j, i: (i, j))
    return pl.pallas_call(
        body, name="gate_up", grid=(f // FFN_TN, n // FFN_TM),
        in_specs=[pl.BlockSpec((FFN_TM, d), lambda j, i: (i, 0)), wspec, wspec],
        out_specs=[ospec, ospec, ospec],
        out_shape=[jax.ShapeDtypeStruct((n, f), BF16)] * 3,
        compiler_params=_params(("parallel", "parallel")),
    )(u2, wg, wu)


def _glu_bwd(df, wd, g, u):
    n, d = df.shape
    f = wd.shape[0]

    def body(df_ref, wd_ref, g_ref, u_ref, dg_ref, du_ref):
        dh = lax.dot_general(df_ref[...], wd_ref[...], NT, preferred_element_type=F32)
        gv = g_ref[...].astype(F32)
        sig = jax.nn.sigmoid(gv)
        du_ref[...] = (dh * (gv * sig)).astype(BF16)
        dg_ref[...] = (dh * u_ref[...].astype(F32) * (sig * (1.0 + gv * (1.0 - sig)))).astype(BF16)

    tm = min(2 * FFN_TM, n)
    ospec = pl.BlockSpec((tm, FFN_TN), lambda i, j: (i, j))
    return pl.pallas_call(
        body, name="glu_bwd", grid=(n // tm, f // FFN_TN),
        in_specs=[pl.BlockSpec((tm, d), lambda i, j: (i, 0)),
                  pl.BlockSpec((FFN_TN, d), lambda i, j: (j, 0)), ospec, ospec],
        out_specs=[ospec, ospec],
        out_shape=[jax.ShapeDtypeStruct((n, f), BF16), jax.ShapeDtypeStruct((n, f), BF16)],
        compiler_params=_params(("parallel", "parallel")),
    )(df, wd, g, u)


def _du2(dg, du, wg, wu):
    n, f = dg.shape
    d = wg.shape[0]
    tm, tn, tk = min(1024, n), 1024, 1408
    nk = f // tk

    def body(dg_ref, du_ref, wg_ref, wu_ref, o_ref, acc_ref):
        kk = pl.program_id(2)
        part = (lax.dot_general(dg_ref[...], wg_ref[...], NT, preferred_element_type=F32)
                + lax.dot_general(du_ref[...], wu_ref[...], NT, preferred_element_type=F32))

        @pl.when(kk == 0)
        def _():
            acc_ref[...] = part

        @pl.when(kk != 0)
        def _():
            acc_ref[...] += part

        @pl.when(kk == nk - 1)
        def _():
            o_ref[...] = acc_ref[...]

    aspec = pl.BlockSpec((tm, tk), lambda i, j, kk: (i, kk))
    wspec = pl.BlockSpec((tn, tk), lambda i, j, kk: (j, kk))
    return pl.pallas_call(
        body, name="du2", grid=(n // tm, d // tn, nk),
        in_specs=[aspec, aspec, wspec, wspec],
        out_specs=pl.BlockSpec((tm, tn), lambda i, j, kk: (i, j)),
        out_shape=jax.ShapeDtypeStruct((n, d), F32),
        scratch_shapes=[pltpu.VMEM((tm, tn), F32)],
        compiler_params=_params(("parallel", "parallel", "arbitrary")),
    )(dg, du, wg, wu)


def _rows8(rows, d=D_MODEL):
    rows = [r.reshape(1, d).astype(F32) for r in rows]
    return jnp.concatenate(rows + [jnp.zeros((8 - len(rows), d), F32)], axis=0)


W_GROUPS = (("w_in",), ("w_out", "w_gate", "w_up"), ("w_down",))
G_GROUPS = (("w_down", "w_gate", "w_up"), ("w_out",), ("w_in",))


def _layer_fwd_bwd(x, ctx, target, mod, mod_ctx, weights, grads_out, q_g, k_g, sink, ln1_g, ln1_b, ln2_g, ln2_b):
    n, d = x.shape
    c = ctx.shape[0]
    sh1, sc1, g1, sh2, sc2, g2 = [mod[:, k * d:(k + 1) * d] for k in range(6)]
    csh1, csc1 = mod_ctx[:, 0:d], mod_ctx[:, d:2 * d]
    cos, sin_a, sin_b = _rope_tables(n, c)
    sink_b = jnp.broadcast_to(sink.reshape(HEADS_A, 1), (HEADS_A, HEAD_DIM)).astype(F32)

    u_all = _modulate_rows(x, ctx, _rows8([sc1, sh1, csc1, csh1]))
    (w_in,) = weights(0, u_all)
    h_all = _matmul(u_all, w_in, name="qkv_proj", tm=_fit(n + c, 1088), tn=1024, tk=2048, out_dtype=F32)
    qa, ka, va, qb, kb, vb = _qkv_post(h_all, cos, sin_a, sin_b, q_g, k_g)
    heads, lse = _attn_b_fwd(qb, kb, vb, _attn_a_fwd(qa, ka, va, sink_b, n), n)
    w_out, w_gate, w_up = weights(1, heads)
    a = _matmul(heads, w_out, name="out_proj", tm=1024, tn=1024, tk=2048, out_dtype=F32)
    xhat1, rstd1, u2 = _res_ln1(x, a, _rows8([g1, ln1_g, ln1_b, sc2, sh2]))
    gg, uu, hh = _gate_up(u2, w_gate, w_up)
    (w_down,) = weights(2, hh)
    f = _matmul(hh, w_down, name="ffn_down", tm=1024, tn=1024, tk=1408, out_dtype=F32)
    dr2, df, s_ln2 = _res_ln2_loss(xhat1, f, target, _rows8([g2, ln1_g, ln1_b, ln2_g, ln2_b]))

    dgg, duu = _glu_bwd(df, w_down, gg, uu)
    dw_down = _matmul(hh, df, name="dw_down", ta=True, tm=512, tn=1024, tk=n, out_dtype=BF16)
    dw_gate = _matmul(u2, dgg, name="dw_gate", ta=True, tm=1024, tn=512, tk=n, out_dtype=BF16)
    dw_up = _matmul(u2, duu, name="dw_up", ta=True, tm=1024, tn=512, tk=n, out_dtype=BF16)
    zero = grads_out(0, [dw_down, dw_gate, dw_up])
    du2 = _du2(dgg, duu, w_gate, w_up)
    dxp, da, s_ln1 = _ln1_bwd(du2, dr2, xhat1, rstd1, a, _rows8([sc2, ln1_g, ln1_b, g1]) + zero)

    dheads = _matmul(da, w_out, name="d_heads", tb=True, tm=1024, tn=1024, tk=2048, out_dtype=BF16)
    dw_out = _matmul(heads, da, name="dw_out", ta=True, tm=1024, tn=1024, tk=n, out_dtype=BF16)
    zero = grads_out(1, [dw_out])
    delta = _delta_rows(dheads, heads)
    dqa, dka, dva, dsink = _attn_a_bwd(qa, ka, va, sink_b + zero, dheads, n)
    dqb, dkb, dvb = _attn_b_bwd(qb, kb, vb, dheads, lse, delta, n)
    dh_all, s_gain = _qkv_bwd_post(h_all, cos, sin_a, sin_b, q_g, k_g, dqa, dka, dva, dqb, dkb, dvb, n)
    dw_in = _matmul(u_all, dh_all, name="dw_in", ta=True, tm=1024, tn=1024, tk=n + c, out_dtype=BF16)
    zero = grads_out(2, [dw_in])
    du_all = _matmul(dh_all, w_in, name="d_u1", tb=True, tm=_fit(n + c, 1088), tn=1024, tk=IN_WIDTH, out_dtype=F32,
                     after=zero.reshape(1, 1))
    grad_x, s_mod1 = _mod1_bwd(du_all, dxp, x, ctx, _rows8([sc1]) + zero)

    dsink_row = jnp.concatenate([dsink[0, 0:GROUP, 0], dsink[1, 0:GROUP, 0]]).reshape(1, HEADS_A)
    misc = jnp.concatenate([s_gain[0:1], s_gain[1:2], dsink_row,
                            jnp.zeros((1, d - 2 * HEAD_DIM - HEADS_A), F32)], axis=1)
    partial = jnp.concatenate([
        s_mod1[1:2], s_mod1[0:1], s_ln1[4:5],
        s_ln1[1:2], s_ln1[0:1], s_ln2[2:3],
        s_mod1[3:4], s_mod1[2:3],
        s_ln1[2:3], s_ln1[3:4], s_ln2[0:1], s_ln2[1:2],
        s_ln2[3:4], misc, jnp.zeros((2, d), F32)], axis=0)
    return grad_x, partial


ANY = pl.BlockSpec(memory_space=pl.ANY)
VMEM_FULL = pl.BlockSpec(memory_space=pltpu.VMEM)
N_CHIP_PEERS = 3


def _me():
    return lax.axis_index("x"), lax.axis_index("y"), lax.axis_index("c")


def _other_chips(x, y):
    return [(1 - x, y), (x, 1 - y), (1 - x, 1 - y)]


def _shard_of(chip):
    return 2 * chip[0] + chip[1]


def _dev_index(x, y, c):
    return 4 * x + 2 * y + c


def _rcopy(src, dst, send_sems, recv_sems, k, dev):
    return pltpu.make_async_remote_copy(src_ref=src, dst_ref=dst, send_sem=send_sems.at[k], recv_sem=recv_sems.at[k],
                                        device_id=dev, device_id_type=MESH)


BIG = (("w_in", (D_MODEL, IN_WIDTH), 1), ("w_out", (MIX_WIDTH, D_MODEL), 0), ("w_gate", (D_MODEL, FFN), 1),
       ("w_up", (D_MODEL, FFN), 1), ("w_down", (FFN, D_MODEL), 0))


def _sub(ref, axis, idx, size):
    start = pl.multiple_of(idx * size, size)
    return ref.at[pl.ds(start, size), :] if axis == 0 else ref.at[:, pl.ds(start, size)]


def _shape_div(shape, axis, parts):
    return tuple(s // parts if a == axis else s for a, s in enumerate(shape))


def _piece(a, ref, shard, half):
    _, full, axis = BIG[a]
    view = _sub(ref, axis, shard, full[axis] // N_SHARD)
    return _sub(view, 1 - axis, half, full[1 - axis] // 2)


HBM = pl.BlockSpec(memory_space=pltpu.HBM)
SEM = pl.BlockSpec(memory_space=pltpu.SEMAPHORE)
EFFECT = pltpu.SideEffectType.DATAFLOW_SIDE_EFFECTING
BIG_INDEX = {name: a for a, (name, _, _) in enumerate(BIG)}


def _in_hbm(arr):
    return pltpu.with_memory_space_constraint(arr, pltpu.HBM)


def _gather_start(tag, arrs, bufs, prev):
    n_arr = len(arrs)

    def body(*refs):
        ins = refs[:n_arr]
        send_sems, recv_sems = refs[n_arr + 1], refs[n_arr + 2]
        token = refs[-1]
        x, y, c = _me()
        s_me = _shard_of((x, y))
        for i, a in enumerate(arrs):
            mine = _piece(a, ins[i], s_me, c)
            for j, chip in enumerate(_other_chips(x, y)):
                _rcopy(mine, mine, send_sems, recv_sems, N_CHIP_PEERS * i + j, (*chip, c)).start()
        token[...] = jnp.zeros_like(token)

    n_sem = N_CHIP_PEERS * n_arr
    outs = pl.pallas_call(
        body, name="gather_start_" + tag,
        in_specs=[HBM] * n_arr + [ANY],
        out_specs=[SEM, SEM] + [HBM] * n_arr + [VMEM_FULL],
        out_shape=[pltpu.SemaphoreType.DMA((n_sem,)), pltpu.SemaphoreType.DMA((n_sem,))]
        + [pltpu.HBM(BIG[a][1], BF16) for a in arrs] + [jax.ShapeDtypeStruct((8, HEAD_DIM), F32)],
        input_output_aliases={i: 2 + i for i in range(n_arr)},
        compiler_params=pltpu.CompilerParams(has_side_effects=EFFECT),
    )(*[_in_hbm(b) for b in bufs], prev)
    return outs[0], outs[1], list(outs[2:2 + n_arr]), outs[-1]


def _gather_wait(tag, arrs, send_sems, recv_sems, bufs, after):
    n_arr = len(arrs)

    def body(*refs):
        ins = refs[:n_arr]
        send_sems_, recv_sems_ = refs[n_arr], refs[n_arr + 1]
        x, y, c = _me()
        s_me = _shard_of((x, y))
        for i, a in enumerate(arrs):
            mine = _piece(a, ins[i], s_me, c)
            for j, chip in enumerate(_other_chips(x, y)):
                landed = _piece(a, ins[i], _shard_of(chip), c)
                cp = _rcopy(mine, landed, send_sems_, recv_sems_, N_CHIP_PEERS * i + j, (*chip, c))
                cp.wait_send()
                cp.wait_recv()

    outs = pl.pallas_call(
        body, name="gather_wait_" + tag,
        in_specs=[HBM] * n_arr + [SEM, SEM, ANY],
        out_specs=[HBM] * n_arr,
        out_shape=[pltpu.HBM(BIG[a][1], BF16) for a in arrs],
        input_output_aliases={i: i for i in range(n_arr)},
        compiler_params=pltpu.CompilerParams(has_side_effects=EFFECT),
    )(*bufs, send_sems, recv_sems, after)
    return list(outs)


def _gather_forward(tag, arrs, bufs):
    n_arr = len(arrs)

    def body(*refs):
        outs = refs[n_arr:2 * n_arr]
        send_sems, recv_sems = refs[2 * n_arr:]
        x, y, c = _me()
        sibling = (x, y, 1 - c)
        chips = _other_chips(x, y)
        copies = []
        for i, a in enumerate(arrs):
            for j, chip in enumerate(chips):
                landed = _piece(a, outs[i], _shard_of(chip), c)
                cp = _rcopy(landed, landed, send_sems, recv_sems, N_CHIP_PEERS * i + j, sibling)
                cp.start()
                copies.append(cp)
        for i, a in enumerate(arrs):
            for j, chip in enumerate(chips):
                other = _piece(a, outs[i], _shard_of(chip), 1 - c)
                _rcopy(other, other, send_sems, recv_sems, N_CHIP_PEERS * i + j, sibling).wait_recv()
        for cp in copies:
            cp.wait_send()

    n_sem = N_CHIP_PEERS * n_arr
    return list(pl.pallas_call(
        body, name="gather_forward_" + tag,
        in_specs=[ANY] * n_arr, out_specs=[ANY] * n_arr,
        out_shape=[jax.ShapeDtypeStruct(BIG[a][1], BF16) for a in arrs],
        input_output_aliases={i: i for i in range(n_arr)},
        scratch_shapes=[pltpu.SemaphoreType.DMA((n_sem,)), pltpu.SemaphoreType.DMA((n_sem,))],
    )(*bufs))


def _peers(x, y, c):
    return [(x ^ (mask >> 2), y ^ ((mask >> 1) & 1), c ^ (mask & 1)) for mask in range(1, N_DEV)]


def _received_shape(a):
    _, full, axis = BIG[a]
    return (N_DEV - 1,) + _shape_div(_shape_div(full, 1 - axis, 2), axis, N_SHARD)


def _pieces_start(tag, arrs, dws):
    n_arr = len(arrs)

    def body(*refs):
        srcs, lands = refs[:n_arr], refs[n_arr:2 * n_arr]
        send_sems, recv_sems = refs[2 * n_arr], refs[2 * n_arr + 1]
        token = refs[-1]
        x, y, c = _me()
        for i, a in enumerate(arrs):
            for k, peer in enumerate(_peers(x, y, c)):
                src = _piece(a, srcs[i], _shard_of(peer[:2]), peer[2])
                _rcopy(src, lands[i].at[k], send_sems, recv_sems, (N_DEV - 1) * i + k, peer).start()
        token[...] = jnp.zeros_like(token)

    n_sem = (N_DEV - 1) * n_arr
    lands = [_in_hbm(lax.empty(_received_shape(a), BF16)) for a in arrs]
    outs = pl.pallas_call(
        body, name="grad_pieces_start_" + tag,
        in_specs=[HBM] * (2 * n_arr),
        out_specs=[SEM, SEM] + [HBM] * (2 * n_arr) + [VMEM_FULL],
        out_shape=[pltpu.SemaphoreType.DMA((n_sem,)), pltpu.SemaphoreType.DMA((n_sem,))]
        + [pltpu.HBM(BIG[a][1], BF16) for a in arrs] + [pltpu.HBM(_received_shape(a), BF16) for a in arrs]
        + [jax.ShapeDtypeStruct((8, HEAD_DIM), F32)],
        input_output_aliases={i: 2 + i for i in range(2 * n_arr)},
        compiler_params=pltpu.CompilerParams(has_side_effects=EFFECT),
    )(*[_in_hbm(dw) for dw in dws], *lands)
    return outs[0], outs[1], list(outs[2:2 + n_arr]), list(outs[2 + n_arr:2 + 2 * n_arr]), outs[-1]


def _pieces_wait(tag, arrs, send_sems, recv_sems, dws, lands, after):
    n_arr = len(arrs)

    def body(*refs):
        srcs, lands_ = refs[:n_arr], refs[n_arr:2 * n_arr]
        send_sems_, recv_sems_ = refs[2 * n_arr], refs[2 * n_arr + 1]
        x, y, c = _me()
        for i, a in enumerate(arrs):
            for k, peer in enumerate(_peers(x, y, c)):
                src = _piece(a, srcs[i], _shard_of(peer[:2]), peer[2])
                cp = _rcopy(src, lands_[i].at[k], send_sems_, recv_sems_, (N_DEV - 1) * i + k, peer)
                cp.wait_send()
                cp.wait_recv()

    outs = pl.pallas_call(
        body, name="grad_pieces_wait_" + tag,
        in_specs=[HBM] * (2 * n_arr) + [SEM, SEM, ANY],
        out_specs=[HBM] * (2 * n_arr),
        out_shape=[pltpu.HBM(BIG[a][1], BF16) for a in arrs] + [pltpu.HBM(_received_shape(a), BF16) for a in arrs],
        input_output_aliases={i: i for i in range(2 * n_arr)},
        compiler_params=pltpu.CompilerParams(has_side_effects=EFFECT),
    )(*dws, *lands, send_sems, recv_sems, after)
    return list(outs[:n_arr]), list(outs[n_arr:])


def _join_halves(tag, g_halves):
    n_arr = len(g_halves)

    def body(*refs):
        ins, outs = refs[:n_arr], refs[n_arr:2 * n_arr]
        send_sems, recv_sems = refs[2 * n_arr:]
        x, y, c = _me()
        copies = []
        for i in range(n_arr):
            cp = _rcopy(ins[i], outs[i], send_sems, recv_sems, i, (x, y, 1 - c))
            cp.start()
            copies.append(cp)
        for cp in copies:
            cp.wait()

    return list(pl.pallas_call(
        body, name="grad_join_halves_" + tag,
        in_specs=[ANY] * n_arr, out_specs=[ANY] * n_arr,
        out_shape=[jax.ShapeDtypeStruct(g.shape, F32) for g in g_halves],
        scratch_shapes=[pltpu.SemaphoreType.DMA((n_arr,)), pltpu.SemaphoreType.DMA((n_arr,))],
    )(*g_halves))


def _piece_sum(a, dw, shard, core, received):
    name, full, axis = BIG[a]
    rows, cols = _received_shape(a)[1:]
    tr = _fit(rows, ROW_TILE)
    nbr = rows // tr

    def body(w_ref, dw_ref, rec_ref, o_ref):
        acc = dw_ref[...].astype(F32)
        for k in range(N_DEV - 1):
            acc = acc + rec_ref[k].astype(F32)
        o_ref[...] = acc

    if axis == 0:
        own = pl.BlockSpec((tr, cols), lambda i, w: (w[0] * nbr + i, w[1]))
    else:
        own = pl.BlockSpec((tr, cols), lambda i, w: (w[1] * nbr + i, w[0]))
    return pl.pallas_call(
        body, name="grad_sum_pieces_" + name,
        grid_spec=pltpu.PrefetchScalarGridSpec(
            num_scalar_prefetch=1, grid=(nbr,),
            in_specs=[own, pl.BlockSpec((N_DEV - 1, tr, cols), lambda i, w: (0, i, 0))],
            out_specs=pl.BlockSpec((tr, cols), lambda i, w: (i, 0))),
        out_shape=jax.ShapeDtypeStruct((rows, cols), F32),
        compiler_params=_params(("parallel",)),
    )(jnp.stack([shard, core]).astype(jnp.int32), dw, received)


def _scatter_begin(tag, arrs, dws):
    send_sems, recv_sems, dws, lands, token = _pieces_start(tag, arrs, dws)
    return (send_sems, recv_sems, dws, lands), token[0, 0]


def _scatter_end(tag, arrs, state, after):
    x, y, c = _me()
    send_sems, recv_sems, dws, lands = state
    dws, lands = _pieces_wait(tag, arrs, send_sems, recv_sems, dws, lands, after)
    g_own = [_piece_sum(a, dw, _shard_of((x, y)), c, r) for a, dw, r in zip(arrs, dws, lands)]
    return g_own, _join_halves(tag, g_own)


def _gather_rows(block, name, after):
    r, d = block.shape

    def body(in_ref, after_ref, out_ref, send_sems, recv_sems):
        x, y, c = _me()
        out_ref[_dev_index(x, y, c)] = in_ref[...]
        copies = []
        for mask in range(1, N_DEV):
            peer = (x ^ (mask >> 2), y ^ ((mask >> 1) & 1), c ^ (mask & 1))
            cp = _rcopy(in_ref, out_ref.at[_dev_index(x, y, c)], send_sems, recv_sems, mask - 1, peer)
            cp.start()
            copies.append((cp, peer))
        for mask in range(1, N_DEV):
            peer = (x ^ (mask >> 2), y ^ ((mask >> 1) & 1), c ^ (mask & 1))
            landed = out_ref.at[_dev_index(*peer)]
            _rcopy(landed, landed, send_sems, recv_sems, mask - 1, peer).wait_recv()
        for cp, _ in copies:
            cp.wait_send()

    return pl.pallas_call(
        body, name=name, in_specs=[VMEM_FULL, ANY], out_specs=VMEM_FULL,
        out_shape=jax.ShapeDtypeStruct((N_DEV, r, d), F32),
        scratch_shapes=[pltpu.SemaphoreType.DMA((N_DEV - 1,)), pltpu.SemaphoreType.DMA((N_DEV - 1,))],
    )(block, after)


ADA_ROWS = 80
ADA_W = 6 * D_MODEL // N_SHARD


def _ada_forward(c_block, cctx_block, w_ada, b_shard):
    d = c_block.shape[1]

    def body(c_ref, cc_ref, w_ref, b_ref, act_ref, mods_ref, raw, mloc, send_sems, recv_sems):
        x, y, c = _me()
        me = _dev_index(x, y, c)
        s_me = _shard_of((x, y))
        raw[72:ADA_ROWS, :] = jnp.zeros((ADA_ROWS - 72, d), F32)
        raw[pl.ds(pl.multiple_of(me * 8, 8), 8), :] = c_ref[...]
        raw[64:72, :] = cc_ref[...]
        sends = []
        for mask in range(1, N_DEV):
            peer = (x ^ (mask >> 2), y ^ ((mask >> 1) & 1), c ^ (mask & 1))
            cp = _rcopy(c_ref, raw.at[pl.ds(pl.multiple_of(me * 8, 8), 8), :], send_sems, recv_sems, mask - 1, peer)
            cp.start()
            sends.append(cp)
        for mask in range(1, N_DEV):
            peer = (x ^ (mask >> 2), y ^ ((mask >> 1) & 1), c ^ (mask & 1))
            landed = raw.at[pl.ds(pl.multiple_of(_dev_index(*peer) * 8, 8), 8), :]
            _rcopy(landed, landed, send_sems, recv_sems, mask - 1, peer).wait_recv()
        v = raw[...]
        act = v * jax.nn.sigmoid(v)
        act_ref[...] = act
        mloc[...] = lax.dot_general(act.astype(BF16), w_ref[...].astype(BF16), NN,
                                    preferred_element_type=F32) + b_ref[...]
        mods_ref[s_me, 0:8, :] = mloc[pl.ds(pl.multiple_of(me * 8, 8), 8), :]
        mods_ref[s_me, 8:16, :] = mloc[64:72, :]
        base = N_DEV - 1
        for j, chip in enumerate(_other_chips(x, y)):
            peer = (*chip, c)
            rows = mloc.at[pl.ds(pl.multiple_of(_dev_index(*peer) * 8, 8), 8), :]
            cp = _rcopy(rows, mods_ref.at[s_me, 0:8, :], send_sems, recv_sems, base + 2 * j, peer)
            cp.start()
            sends.append(cp)
            cp = _rcopy(mloc.at[64:72, :], mods_ref.at[s_me, 8:16, :], send_sems, recv_sems, base + 2 * j + 1, peer)
            cp.start()
            sends.append(cp)
        for j, chip in enumerate(_other_chips(x, y)):
            for part in range(2):
                landed = mods_ref.at[_shard_of(chip), 8 * part:8 * part + 8, :]
                _rcopy(landed, landed, send_sems, recv_sems, base + 2 * j + part, (*chip, c)).wait_recv()
        for cp in sends:
            cp.wait_send()

    n_sem = N_DEV - 1 + 2 * N_CHIP_PEERS
    return pl.pallas_call(
        body, name="ada_forward",
        in_specs=[VMEM_FULL] * 4, out_specs=[VMEM_FULL, VMEM_FULL],
        out_shape=[jax.ShapeDtypeStruct((ADA_ROWS, d), F32), jax.ShapeDtypeStruct((N_SHARD, 16, ADA_W), F32)],
        scratch_shapes=[pltpu.VMEM((ADA_ROWS, d), F32), pltpu.VMEM((ADA_ROWS, ADA_W), F32),
                        pltpu.SemaphoreType.DMA((n_sem,)), pltpu.SemaphoreType.DMA((n_sem,))],
        compiler_params=pltpu.CompilerParams(vmem_limit_bytes=VMEM_LIMIT),
    )(c_block, cctx_block, w_ada, b_shard)


def _small_reduce(gathered):
    d = gathered.shape[2]

    def body(g_ref, o_ref):
        tot = g_ref[0]
        for i in range(1, N_DEV):
            tot = tot + g_ref[i]
        o_ref[...] = tot
        o_ref[0:2, :] = tot[0:2] + tot[6:8]
        o_ref[12:13, :] = jnp.broadcast_to(jnp.sum(tot[12:13], axis=1, keepdims=True), (1, d))

    return pl.pallas_call(body, name="small_reduce", in_specs=[VMEM_FULL], out_specs=VMEM_FULL,
                          out_shape=jax.ShapeDtypeStruct((16, d), F32))(gathered)


def _cctx_grad(gathered, c_ctx):
    d = gathered.shape[2]

    def body(g_ref, c_ref, o_ref):
        tot = g_ref[0, 0:1, :]
        for chip in range(1, N_SHARD):
            tot = tot + g_ref[2 * chip, 0:1, :]
        v = c_ref[...]
        sig = jax.nn.sigmoid(v)
        o_ref[...] = tot * (sig * (1.0 + v * (1.0 - sig)))

    return pl.pallas_call(body, name="cctx_grad", in_specs=[VMEM_FULL, VMEM_FULL], out_specs=VMEM_FULL,
                          out_shape=jax.ShapeDtypeStruct((1, d), F32))(gathered, c_ctx.reshape(1, d))


def _cast_into_full(w, shard, full, axis, name):
    r, cdim = w.shape
    tr = _fit(r, ROW_TILE)
    nbr = r // tr

    def body(s_ref, w_ref, o_ref):
        o_ref[...] = w_ref[...].astype(BF16)

    if axis == 0:
        out_spec = pl.BlockSpec((tr, cdim), lambda i, s: (s[0] * nbr + i, 0))
    else:
        out_spec = pl.BlockSpec((tr, cdim), lambda i, s: (i, s[0]))
    return pl.pallas_call(
        body, name=name,
        grid_spec=pltpu.PrefetchScalarGridSpec(
            num_scalar_prefetch=1, grid=(nbr,), in_specs=[pl.BlockSpec((tr, cdim), lambda i, s: (i, 0))],
            out_specs=out_spec),
        out_shape=jax.ShapeDtypeStruct(full, BF16), compiler_params=_params(("parallel",)),
    )(shard.reshape(1).astype(jnp.int32), w)


def _adamw_halves(w, g_own, g_other, m, v, core, axis, name):
    r, cdim = w.shape
    hr, hc = (r // 2, cdim) if axis == 1 else (r, cdim // 2)
    assert g_own.shape == (hr, hc) and g_other.shape == (hr, hc)
    tr = _fit(hr, 128)
    nb = hr // tr
    c1 = 1.0 - ADAM_B1 ** ADAM_STEP
    c2 = 1.0 - ADAM_B2 ** ADAM_STEP

    def body(c_ref, w_ref, go_ref, gt_ref, m_ref, v_ref, g_ref, d_ref, nm_ref, nv_ref):
        gv = jnp.where(pl.program_id(0) == c_ref[0], go_ref[...], gt_ref[...])
        nm = ADAM_B1 * m_ref[...] + (1.0 - ADAM_B1) * gv
        nv = ADAM_B2 * v_ref[...] + (1.0 - ADAM_B2) * (gv * gv)
        g_ref[...] = gv
        nm_ref[...] = nm
        nv_ref[...] = nv
        d_ref[...] = -ADAM_LR * ((nm / c1) / (jnp.sqrt(nv / c2) + ADAM_EPS) + ADAM_WD * w_ref[...])

    if axis == 1:
        big = pl.BlockSpec((tr, hc), lambda p, i, c: (p * nb + i, 0))
    else:
        big = pl.BlockSpec((tr, hc), lambda p, i, c: (i, p))
    half = pl.BlockSpec((tr, hc), lambda p, i, c: (i, 0))
    sh = jax.ShapeDtypeStruct((r, cdim), F32)
    return pl.pallas_call(
        body, name=name,
        grid_spec=pltpu.PrefetchScalarGridSpec(
            num_scalar_prefetch=1, grid=(2, nb), in_specs=[big, half, half, big, big], out_specs=[big] * 4),
        out_shape=[sh] * 4, compiler_params=_params(("parallel", "parallel")),
    )(core.reshape(1).astype(jnp.int32), w, g_own, g_other, m, v)


def _adamw(w, g, m, v, name):
    r, cdim = w.shape
    tr = _fit(r, 128) if r % (ROW_TILE // 4) == 0 else r
    c1 = 1.0 - ADAM_B1 ** ADAM_STEP
    c2 = 1.0 - ADAM_B2 ** ADAM_STEP

    def body(w_ref, g_ref, m_ref, v_ref, d_ref, nm_ref, nv_ref):
        gv = g_ref[...]
        nm = ADAM_B1 * m_ref[...] + (1.0 - ADAM_B1) * gv
        nv = ADAM_B2 * v_ref[...] + (1.0 - ADAM_B2) * (gv * gv)
        nm_ref[...] = nm
        nv_ref[...] = nv
        d_ref[...] = -ADAM_LR * ((nm / c1) / (jnp.sqrt(nv / c2) + ADAM_EPS) + ADAM_WD * w_ref[...])

    spec = pl.BlockSpec((tr, cdim), lambda i: (i, 0))
    sh = jax.ShapeDtypeStruct((r, cdim), F32)
    return pl.pallas_call(body, name=name, grid=(r // tr,), in_specs=[spec] * 4, out_specs=[spec] * 3,
                          out_shape=[sh, sh, sh], compiler_params=_params(("parallel",)))(w, g, m, v)


SMALL = (("c_ctx", D_MODEL), ("b_ada", 6 * D_MODEL), ("q_norm_g", HEAD_DIM), ("k_norm_g", HEAD_DIM),
         ("sink_logit", HEADS_A), ("ln1_g", D_MODEL), ("ln1_b", D_MODEL), ("ln2_g", D_MODEL), ("ln2_b", D_MODEL))
WEIGHT_ORDER = ("c_ctx", "w_ada", "b_ada", "w_in", "q_norm_g", "k_norm_g", "sink_logit", "w_out", "ln1_g", "ln1_b",
                "w_gate", "w_up", "w_down", "ln2_g", "ln2_b")


def kernel(x, c, ctx, c_ctx, w_ada, b_ada, w_in, q_norm_g, k_norm_g, sink_logit, w_out, ln1_g, ln1_b, w_gate, w_up, w_down, ln2_g, ln2_b, loss_target, m_c_ctx, m_w_ada, m_b_ada, m_w_in, m_q_norm_g, m_k_norm_g, m_sink_logit, m_w_out, m_ln1_g, m_ln1_b, m_w_gate, m_w_up, m_w_down, m_ln2_g, m_ln2_b, v_c_ctx, v_w_ada, v_b_ada, v_w_in, v_q_norm_g, v_k_norm_g, v_sink_logit, v_w_out, v_ln1_g, v_ln1_b, v_w_gate, v_w_up, v_w_down, v_ln2_g, v_ln2_b):
    d = D_MODEL
    w = dict(c_ctx=c_ctx, w_ada=w_ada[0], b_ada=b_ada, w_in=w_in[0], q_norm_g=q_norm_g, k_norm_g=k_norm_g,
             sink_logit=sink_logit, w_out=w_out[0], ln1_g=ln1_g, ln1_b=ln1_b, w_gate=w_gate[0], w_up=w_up[0],
             w_down=w_down[0], ln2_g=ln2_g, ln2_b=ln2_b)
    m = dict(c_ctx=m_c_ctx, w_ada=m_w_ada[0], b_ada=m_b_ada, w_in=m_w_in[0], q_norm_g=m_q_norm_g, k_norm_g=m_k_norm_g,
             sink_logit=m_sink_logit, w_out=m_w_out[0], ln1_g=m_ln1_g, ln1_b=m_ln1_b, w_gate=m_w_gate[0],
             w_up=m_w_up[0], w_down=m_w_down[0], ln2_g=m_ln2_g, ln2_b=m_ln2_b)
    v = dict(c_ctx=v_c_ctx, w_ada=v_w_ada[0], b_ada=v_b_ada, w_in=v_w_in[0], q_norm_g=v_q_norm_g, k_norm_g=v_k_norm_g,
             sink_logit=v_sink_logit, w_out=v_w_out[0], ln1_g=v_ln1_g, ln1_b=v_ln1_b, w_gate=v_w_gate[0],
             w_up=v_w_up[0], w_down=v_w_down[0], ln2_g=v_ln2_g, ln2_b=v_ln2_b)
    mx, my, mc = _me()
    s_me = _shard_of((mx, my))
    me = _dev_index(mx, my, mc)
    pad8 = lambda row: jnp.concatenate([row.reshape(1, -1), jnp.zeros((7, row.size), F32)], axis=0)

    b_shard = lax.dynamic_slice(b_ada, (0, s_me * ADA_W), (1, ADA_W))
    act, mods4 = _ada_forward(pad8(c), pad8(c_ctx), w["w_ada"], b_shard)

    gathers = []
    prev, shard = mods4, s_me
    for k, names in enumerate(W_GROUPS):
        arrs = tuple(BIG_INDEX[name] for name in names)
        bufs = [_cast_into_full(w[name], shard, BIG[a][1], BIG[a][2], "cast_" + name) for name, a in zip(names, arrs)]
        send_sems, recv_sems, thru, prev = _gather_start("g%d" % k, arrs, bufs, prev)
        shard = s_me + prev[0, 0].astype(jnp.int32)
        gathers.append((arrs, send_sems, recv_sems, thru))

    def weights(k, after):
        arrs, send_sems, recv_sems, thru = gathers[k]
        landed = _gather_wait("g%d" % k, arrs, send_sems, recv_sems, thru, after)
        return _gather_forward("g%d" % k, arrs, landed)

    mod = jnp.transpose(mods4[:, 0:1, :], (1, 0, 2)).reshape(1, 6 * d) + prev[0, 0]
    mod_ctx = jnp.transpose(mods4[:, 8:9, :], (1, 0, 2)).reshape(1, 6 * d)

    scatters = {}

    def grads_out(k, dws):
        arrs = tuple(BIG_INDEX[name] for name in G_GROUPS[k])
        scatters[k], zero = _scatter_begin("g%d" % k, arrs, dws)
        return zero

    grad_x, partial = _layer_fwd_bwd(x[0], ctx[0], loss_target[0], mod, mod_ctx, weights, grads_out,
                                     q_norm_g, k_norm_g, sink_logit, ln1_g, ln1_b, ln2_g, ln2_b)
    grads, delta, new_m, new_v = {}, {}, {}, {}

    after = partial
    for k, names in enumerate(G_GROUPS):
        arrs = tuple(BIG_INDEX[name] for name in names)
        g_own, g_other = _scatter_end("g%d" % k, arrs, scatters[k], after)
        for name, own, other in zip(names, g_own, g_other):
            grads[name], delta[name], new_m[name], new_v[name] = _adamw_halves(
                w[name], own, other, m[name], v[name], mc, BIG[BIG_INDEX[name]][2], "adamw_" + name)
            after = new_v[name]

    gathered = _gather_rows(partial, "gather_partials", after)
    tot = _small_reduce(gathered)
    grads["b_ada"] = tot[0:6].reshape(1, 6 * d)
    grads["ln1_g"], grads["ln1_b"], grads["ln2_g"], grads["ln2_b"] = tot[8:9], tot[9:10], tot[10:11], tot[11:12]
    grads["q_norm_g"] = tot[13:14, 0:HEAD_DIM]
    grads["k_norm_g"] = tot[13:14, HEAD_DIM:2 * HEAD_DIM]
    grads["sink_logit"] = tot[13:14, 2 * HEAD_DIM:2 * HEAD_DIM + HEADS_A]
    loss = tot[12, 0]

    dm_all = gathered[:, 0:6, :].reshape(N_DEV, 6 * d)
    dmc_tot = jnp.concatenate([tot[6:8].reshape(1, 2 * d), jnp.zeros((1, 4 * d), F32)], axis=1)
    dm_rows = jnp.concatenate([pad8(dm_all[i]) for i in range(N_DEV)] + [pad8(dmc_tot), jnp.zeros((8, 6 * d), F32)], axis=0)
    dm_shard = lax.dynamic_slice(dm_rows, (0, s_me * ADA_W), (ADA_ROWS, ADA_W))
    grads["w_ada"] = _matmul(act, dm_shard, name="dw_ada", ta=True, tm=1024, tn=1024, tk=ADA_ROWS, out_dtype=F32)
    dmc_shard = lax.dynamic_slice(pad8(dmc_tot), (0, s_me * ADA_W), (8, ADA_W))
    cc_part = _matmul(dmc_shard, w["w_ada"], name="d_cctx", tb=True, tm=8, tn=1024, tk=1536, out_dtype=F32)
    grads["c_ctx"] = _cctx_grad(_gather_rows(cc_part, "gather_cctx", tot), c_ctx).reshape(d)

    delta["w_ada"], new_m["w_ada"], new_v["w_ada"] = _adamw(w["w_ada"], grads["w_ada"], m["w_ada"], v["w_ada"],
                                                            "adamw_w_ada")
    pack = lambda t: jnp.concatenate([t[name].reshape(1, size) for name, size in SMALL], axis=1)
    pd, pm, pv = _adamw(pack(w), pack(grads), pack(m), pack(v), "adamw_small")
    off = 0
    for name, size in SMALL:
        delta[name], new_m[name], new_v[name] = [t[:, off:off + size].reshape(w[name].shape) for t in (pd, pm, pv)]
        grads[name] = grads[name].reshape(w[name].shape)
        off += size

    lead = lambda name, t: t[None] if name in ("w_ada", "w_in", "w_out", "w_gate", "w_up", "w_down") else t
    outs = [loss, grad_x[None]]
    for group in (grads, delta, new_m, new_v):
        outs += [lead(name, group[name]) for name in WEIGHT_ORDER]
    return tuple(outs)
```

```python
import functools
import math

import jax
import jax.numpy as jnp
from jax import lax
from jax.experimental import pallas as pl
from jax.experimental.pallas import tpu as pltpu

F32 = jnp.float32
BF16 = jnp.bfloat16
MESH = pl.DeviceIdType.MESH

D_MODEL = 2048
HEAD_DIM = 128
HEADS_A = 8
HEADS_B = 8
KV_A = 2
KV_B = 2
GROUP = 4
GRID_W = 64
WINDOW = 128
BLOCK = 128
FFN = 5632
IN_WIDTH = 3072
MIX_WIDTH = 2048
ROPE_THETA = 10000.0
EPS = 1e-6
ATTN_SCALE = HEAD_DIM ** -0.5
DN_ALPHA = 2.0 ** 0.25
N_SHARD = 4
N_DEV = 8

ADAM_LR = 0.001
ADAM_B1 = 0.9
ADAM_B2 = 0.999
ADAM_EPS = 1e-08
ADAM_WD = 0.01
ADAM_STEP = 10

QA0, KA0, VA0, QB0, KB0, VB0 = 0, 1024, 1280, 1536, 2560, 2816

VMEM_LIMIT = 56 * 1024 * 1024
ROW_TILE = 256
NN = (((1,), (0,)), ((), ()))
NT = (((1,), (1,)), ((), ()))
TN = (((0,), (0,)), ((), ()))


def _fit(total, pref):
    step = ROW_TILE // 4
    best = step
    for cand in range(step, pref + 1, step):
        if total % cand == 0:
            best = cand
    return best


def _params(sem=None):
    return pltpu.CompilerParams(dimension_semantics=sem, vmem_limit_bytes=VMEM_LIMIT)


def _matmul(a, b, *, name, ta=False, tb=False, tm, tn, tk, out_dtype, after=None):
    m = a.shape[1] if ta else a.shape[0]
    k = a.shape[0] if ta else a.shape[1]
    n = b.shape[0] if tb else b.shape[1]
    assert (b.shape[1] if tb else b.shape[0]) == k
    tm, tn, tk = min(tm, m), min(tn, n), min(tk, k)
    assert m % tm == 0 and n % tn == 0 and k % tk == 0, (name, m, n, k, tm, tn, tk)
    nk = k // tk
    dn = (((0 if ta else 1,), (1 if tb else 0,)), ((), ()))

    def product(a_ref, b_ref):
        return lax.dot_general(a_ref[...].astype(BF16), b_ref[...].astype(BF16), dn, preferred_element_type=F32)

    def body_whole_k(a_ref, b_ref, *rest):
        o_ref = rest[-1]
        o_ref[...] = product(a_ref, b_ref).astype(o_ref.dtype)

    def body(a_ref, b_ref, *rest):
        o_ref, acc_ref = rest[-2:]
        kk = pl.program_id(2)
        part = product(a_ref, b_ref)

        @pl.when(kk == 0)
        def _():
            acc_ref[...] = part

        @pl.when(kk != 0)
        def _():
            acc_ref[...] += part

        @pl.when(kk == nk - 1)
        def _():
            o_ref[...] = acc_ref[...].astype(o_ref.dtype)

    a_spec = (pl.BlockSpec((tk, tm), lambda i, j, kk: (kk, i)) if ta
              else pl.BlockSpec((tm, tk), lambda i, j, kk: (i, kk)))
    b_spec = (pl.BlockSpec((tn, tk), lambda i, j, kk: (j, kk)) if tb
              else pl.BlockSpec((tk, tn), lambda i, j, kk: (kk, j)))
    return pl.pallas_call(
        body_whole_k if nk == 1 else body, name=name, grid=(m // tm, n // tn, nk),
        in_specs=[a_spec, b_spec] + ([] if after is None else [pl.BlockSpec(memory_space=pl.ANY)]),
        out_specs=pl.BlockSpec((tm, tn), lambda i, j, kk: (i, j)),
        out_shape=jax.ShapeDtypeStruct((m, n), out_dtype),
        scratch_shapes=[] if nk == 1 else [pltpu.VMEM((tm, tn), F32)],
        compiler_params=_params(("parallel", "parallel", "arbitrary")),
    )(a, b, *([] if after is None else [after]))


def _modulate_rows(x, ctx, mods):
    n, d = x.shape
    c = ctx.shape[0]
    nx = n // ROW_TILE
    assert c == ROW_TILE

    def body(x_ref, ctx_ref, mods_ref, o_ref):
        i = pl.program_id(0)

        @pl.when(i < nx)
        def _():
            o_ref[...] = (x_ref[...] * (1.0 + mods_ref[0:1, :]) + mods_ref[1:2, :]).astype(BF16)

        @pl.when(i >= nx)
        def _():
            o_ref[...] = (ctx_ref[...] * (1.0 + mods_ref[2:3, :]) + mods_ref[3:4, :]).astype(BF16)

    return pl.pallas_call(
        body, name="modulate_rows", grid=(nx + 1,),
        in_specs=[pl.BlockSpec((ROW_TILE, d), lambda i: (jnp.minimum(i, nx - 1), 0)),
                  pl.BlockSpec((ROW_TILE, d), lambda i: (0, 0)),
                  pl.BlockSpec((8, d), lambda i: (0, 0))],
        out_specs=pl.BlockSpec((ROW_TILE, d), lambda i: (i, 0)),
        out_shape=jax.ShapeDtypeStruct((n + c, d), BF16),
        compiler_params=_params(("parallel",)),
    )(x, ctx, mods)


def _rope_tables(n, c):
    rows = n // GRID_W
    row_ids = jnp.repeat(jnp.arange(rows, dtype=F32), GRID_W)
    col_ids = jnp.tile(jnp.arange(GRID_W, dtype=F32), rows)
    axis_dim = HEAD_DIM // 2
    inv_freq = jnp.power(ROPE_THETA, -jnp.arange(0, axis_dim, 2, dtype=F32) / axis_dim)
    ang_r = row_ids[:, None] * inv_freq
    ang_c = col_ids[:, None] * inv_freq
    ang = jnp.concatenate([ang_r, ang_r, ang_c, ang_c], axis=-1)
    cos, sin = jnp.cos(ang), jnp.sin(ang)
    quarter = (jnp.arange(HEAD_DIM) // (HEAD_DIM // 4)) % 2
    sin_a = jnp.where(quarter == 0, -sin, 0.0)
    sin_b = jnp.where(quarter == 1, sin, 0.0)
    pad = lambda t, v: jnp.concatenate([t, jnp.full((c, HEAD_DIM), v, F32)], axis=0)
    return pad(cos, 1.0), pad(sin_a, 0.0), pad(sin_b, 0.0)


def _rope(x, cos, sin_a, sin_b):
    return x * cos + pltpu.roll(x, 96, 1) * sin_a + pltpu.roll(x, 32, 1) * sin_b


def _rope_t(dy, cos, sin_a, sin_b):
    return dy * cos - pltpu.roll(dy, 96, 1) * sin_a - pltpu.roll(dy, 32, 1) * sin_b


def _rms(x):
    r = lax.rsqrt(jnp.mean(x * x, axis=-1, keepdims=True) + EPS)
    return x * r, r


def _qkv_post(h_all, cos, sin_a, sin_b, q_g, k_g):
    t = h_all.shape[0]
    nt = t // ROW_TILE

    def body(h_ref, cos_ref, sa_ref, sb_ref, qg_ref, kg_ref, qa_ref, ka_ref, va_ref, qb_ref, kb_ref, vb_ref):
        cos_, sa, sb = cos_ref[...], sa_ref[...], sb_ref[...]
        sl = lambda off, hh: h_ref[:, off + hh * HEAD_DIM: off + (hh + 1) * HEAD_DIM]
        for hh in range(HEADS_A):
            qa_ref[hh] = (_rope(sl(QA0, hh), cos_, sa, sb) * ATTN_SCALE).astype(BF16)
        for hh in range(KV_A):
            ka_ref[hh] = _rope(sl(KA0, hh), cos_, sa, sb).astype(BF16)
            va_ref[hh] = sl(VA0, hh).astype(BF16)
        for hh in range(HEADS_B):
            xn, _ = _rms(sl(QB0, hh))
            qb_ref[hh] = (_rope(xn * qg_ref[...], cos_, sa, sb) * ATTN_SCALE).astype(BF16)
        for hh in range(KV_B):
            xn, _ = _rms(sl(KB0, hh))
            kb_ref[hh] = _rope(xn * kg_ref[...], cos_, sa, sb).astype(BF16)
            vb_ref[hh] = sl(VB0, hh).astype(BF16)

    tab = pl.BlockSpec((ROW_TILE, HEAD_DIM), lambda i: (i, 0))
    gain = pl.BlockSpec((1, HEAD_DIM), lambda i: (0, 0))
    hs = lambda nh: pl.BlockSpec((nh, ROW_TILE, HEAD_DIM), lambda i: (0, i, 0))
    sh = lambda nh: jax.ShapeDtypeStruct((nh, t, HEAD_DIM), BF16)
    return pl.pallas_call(
        body, name="qkv_post", grid=(nt,),
        in_specs=[pl.BlockSpec((ROW_TILE, IN_WIDTH), lambda i: (i, 0)), tab, tab, tab, gain, gain],
        out_specs=[hs(HEADS_A), hs(KV_A), hs(KV_A), hs(HEADS_B), hs(KV_B), hs(KV_B)],
        out_shape=[sh(HEADS_A), sh(KV_A), sh(KV_A), sh(HEADS_B), sh(KV_B), sh(KV_B)],
        compiler_params=_params(("parallel",)),
    )(h_all, cos, sin_a, sin_b, q_g, k_g)


def _qkv_bwd_post(h_all, cos, sin_a, sin_b, q_g, k_g, dqa, dka, dva, dqb, dkb, dvb, n):
    t = h_all.shape[0]
    nt = t // ROW_TILE
    nx = n // ROW_TILE

    def body(h_ref, cos_ref, sa_ref, sb_ref, qg_ref, kg_ref,
             dqa_ref, dka_ref, dva_ref, dqb_ref, dkb_ref, dvb_ref, dh_ref, gs_ref):
        i = pl.program_id(0)
        cos_, sa, sb = cos_ref[...], sa_ref[...], sb_ref[...]
        latent = (i < nx).astype(F32)
        sl = lambda off, hh: h_ref[:, off + hh * HEAD_DIM: off + (hh + 1) * HEAD_DIM]

        def put(off, hh, val):
            dh_ref[:, off + hh * HEAD_DIM: off + (hh + 1) * HEAD_DIM] = val.astype(BF16)

        def norm_bwd(x, gain, dy):
            xn, r = _rms(x)
            dxh = dy * gain
            dx = r * (dxh - xn * jnp.mean(dxh * xn, axis=-1, keepdims=True))
            return dx, jnp.sum(dy * xn, axis=0, keepdims=True)

        for hh in range(HEADS_A):
            put(QA0, hh, _rope_t(dqa_ref[hh] * (ATTN_SCALE * latent), cos_, sa, sb))
        for hh in range(KV_A):
            put(KA0, hh, _rope_t(dka_ref[hh], cos_, sa, sb))
            put(VA0, hh, dva_ref[hh])
        gq = jnp.zeros((1, HEAD_DIM), F32)
        gk = jnp.zeros((1, HEAD_DIM), F32)
        for hh in range(HEADS_B):
            dq_t = dqb_ref[hh // GROUP, :, (hh % GROUP) * ROW_TILE:(hh % GROUP + 1) * ROW_TILE]
            dy = _rope_t(dq_t.T * (ATTN_SCALE * latent), cos_, sa, sb)
            dx, g = norm_bwd(sl(QB0, hh), qg_ref[...], dy)
            put(QB0, hh, dx)
            gq = gq + g
        for hh in range(KV_B):
            dy = _rope_t(dkb_ref[hh], cos_, sa, sb)
            dx, g = norm_bwd(sl(KB0, hh), kg_ref[...], dy)
            put(KB0, hh, dx)
            gk = gk + g
            put(VB0, hh, dvb_ref[hh])
        upd = jnp.concatenate([gq, gk, jnp.zeros((6, HEAD_DIM), F32)], axis=0)

        @pl.when(i == 0)
        def _():
            gs_ref[...] = upd

        @pl.when(i != 0)
        def _():
            gs_ref[...] += upd

    tab = pl.BlockSpec((ROW_TILE, HEAD_DIM), lambda i: (i, 0))
    gain = pl.BlockSpec((1, HEAD_DIM), lambda i: (0, 0))
    lat = lambda nh: pl.BlockSpec((nh, ROW_TILE, HEAD_DIM), lambda i: (0, jnp.minimum(i, nx - 1), 0))
    full = lambda nh: pl.BlockSpec((nh, ROW_TILE, HEAD_DIM), lambda i: (0, i, 0))
    return pl.pallas_call(
        body, name="qkv_bwd_post", grid=(nt,),
        in_specs=[pl.BlockSpec((ROW_TILE, IN_WIDTH), lambda i: (i, 0)), tab, tab, tab, gain, gain,
                  lat(HEADS_A), full(KV_A), full(KV_A),
                  pl.BlockSpec((KV_B, None, HEAD_DIM, GROUP * ROW_TILE), lambda i: (0, jnp.minimum(i, nx - 1), 0, 0)),
                  full(KV_B), full(KV_B)],
        out_specs=[pl.BlockSpec((ROW_TILE, IN_WIDTH), lambda i: (i, 0)),
                   pl.BlockSpec((8, HEAD_DIM), lambda i: (0, 0))],
        out_shape=[jax.ShapeDtypeStruct((t, IN_WIDTH), BF16), jax.ShapeDtypeStruct((8, HEAD_DIM), F32)],
        compiler_params=_params(("arbitrary",)),
    )(h_all, cos, sin_a, sin_b, q_g, k_g, dqa, dka, dva, dqb, dkb, dvb)


GB_TQ = 256
GB_TK = 256


def _heads_rows(ref2d, tq):
    return jnp.concatenate([ref2d[:, hh * HEAD_DIM:(hh + 1) * HEAD_DIM] for hh in range(GROUP)], axis=0)


def _attn_b_fwd(qb, kb, vb, n):
    t = kb.shape[1]
    nk = t // GB_TK
    tq = GB_TQ
    rows = GROUP * tq

    def body(q_ref, k_ref, v_ref, o_ref, lse_ref, m_s, l_s, acc_s):
        q = q_ref[...].reshape(rows, HEAD_DIM)
        m_s[...] = jnp.full((1, rows), -jnp.inf, F32)
        l_s[...] = jnp.zeros((1, rows), F32)
        acc_s[...] = jnp.zeros((HEAD_DIM, rows), F32)

        def scores(j):
            start = pl.multiple_of(j * GB_TK, GB_TK)
            return lax.dot_general(k_ref[pl.ds(start, GB_TK), :], q, NT, preferred_element_type=F32)

        def step(j, st):
            st_next = scores(jnp.minimum(j + 1, nk - 1))
            vs = v_ref[pl.ds(pl.multiple_of(j * GB_TK, GB_TK), GB_TK), :]
            m_prev = m_s[...]
            m_new = jnp.maximum(m_prev, jnp.max(st, axis=0, keepdims=True))
            p = jnp.exp(st - m_new)
            alpha = jnp.exp(m_prev - m_new)
            l_s[...] = alpha * l_s[...] + jnp.sum(p, axis=0, keepdims=True)
            acc_s[...] = alpha * acc_s[...] + lax.dot_general(vs, p.astype(BF16), TN, preferred_element_type=F32)
            m_s[...] = m_new
            return st_next

        lax.fori_loop(0, nk, step, scores(0))
        ot = acc_s[...] * (1.0 / l_s[...])
        lse_ref[...] = m_s[...] + jnp.log(l_s[...])
        for hh in range(GROUP):
            o_ref[:, hh * HEAD_DIM:(hh + 1) * HEAD_DIM] = ot[:, hh * tq:(hh + 1) * tq].T.astype(BF16)

    return pl.pallas_call(
        body, name="attn_b_fwd", grid=(KV_B, n // tq),
        in_specs=[pl.BlockSpec((GROUP, tq, HEAD_DIM), lambda g, i: (g, i, 0)),
                  pl.BlockSpec((None, t, HEAD_DIM), lambda g, i: (g, 0, 0)),
                  pl.BlockSpec((None, t, HEAD_DIM), lambda g, i: (g, 0, 0))],
        out_specs=[pl.BlockSpec((tq, GROUP * HEAD_DIM), lambda g, i: (i, KV_A + g)),
                   pl.BlockSpec((None, None, 1, rows), lambda g, i: (g, i, 0, 0))],
        out_shape=[jax.ShapeDtypeStruct((n, MIX_WIDTH), BF16),
                   jax.ShapeDtypeStruct((KV_B, n // tq, 1, rows), F32)],
        scratch_shapes=[pltpu.VMEM((1, rows), F32), pltpu.VMEM((1, rows), F32), pltpu.VMEM((HEAD_DIM, rows), F32)],
        compiler_params=_params(("parallel", "parallel")),
    )(qb, kb, vb)


def _attn_b_bwd(qb, kb, vb, dheads, lse, delta, n):
    t = kb.shape[1]
    nk = t // GB_TK
    tq = GB_TQ
    nq = n // tq
    rows = GROUP * tq

    def body(q_ref, k_ref, v_ref, do_ref, lse_ref, dl_ref, dq_ref, dk_ref, dv_ref):
        j = pl.program_id(1)
        i = pl.program_id(2)

        q = q_ref[...].reshape(rows, HEAD_DIM)
        do = _heads_rows(do_ref, tq)
        ks, vs = k_ref[...], v_ref[...]
        st = lax.dot_general(ks, q, NT, preferred_element_type=F32)
        p = jnp.exp(st - lse_ref[...])
        dpt = lax.dot_general(vs, do, NT, preferred_element_type=F32)
        ds = (p * (dpt - dl_ref[...])).astype(BF16)
        dv_part = lax.dot_general(p.astype(BF16), do, NN, preferred_element_type=F32)
        dk_part = lax.dot_general(ds, q, NN, preferred_element_type=F32)
        dq_part = lax.dot_general(ks, ds, TN, preferred_element_type=F32)

        @pl.when(i == 0)
        def _():
            dk_ref[...] = dk_part
            dv_ref[...] = dv_part

        @pl.when(i != 0)
        def _():
            dk_ref[...] += dk_part
            dv_ref[...] += dv_part

        @pl.when(j == 0)
        def _():
            dq_ref[i] = dq_part

        @pl.when(j != 0)
        def _():
            dq_ref[i] += dq_part

    kv = pl.BlockSpec((None, GB_TK, HEAD_DIM), lambda g, j, i: (g, j, 0))
    row = pl.BlockSpec((None, None, 1, rows), lambda g, j, i: (g, i, 0, 0))
    return pl.pallas_call(
        body, name="attn_b_bwd", grid=(KV_B, nk, nq),
        in_specs=[pl.BlockSpec((GROUP, tq, HEAD_DIM), lambda g, j, i: (g, i, 0)), kv, kv,
                  pl.BlockSpec((tq, GROUP * HEAD_DIM), lambda g, j, i: (i, KV_A + g)), row, row],
        out_specs=[pl.BlockSpec((None, nq, HEAD_DIM, rows), lambda g, j, i: (g, 0, 0, 0)), kv, kv],
        out_shape=[jax.ShapeDtypeStruct((KV_B, nq, HEAD_DIM, rows), F32),
                   jax.ShapeDtypeStruct((KV_B, t, HEAD_DIM), F32),
                   jax.ShapeDtypeStruct((KV_B, t, HEAD_DIM), F32)],
        compiler_params=_params(("parallel", "arbitrary", "arbitrary")),
    )(qb, kb, vb, dheads, lse, delta)


def _delta_rows(dheads, heads):
    n = heads.shape[0]
    tq = GB_TQ
    w = GROUP * HEAD_DIM

    def body(a_ref, b_ref, o_ref):
        prod = a_ref[...].astype(F32) * b_ref[...].astype(F32)
        cols = [jnp.sum(prod[:, hh * HEAD_DIM:(hh + 1) * HEAD_DIM].T, axis=0, keepdims=True) for hh in range(GROUP)]
        o_ref[...] = jnp.concatenate(cols, axis=1)

    blk = pl.BlockSpec((tq, w), lambda g, i: (i, KV_A + g))
    return pl.pallas_call(
        body, name="delta_rows", grid=(KV_B, n // tq),
        in_specs=[blk, blk],
        out_specs=pl.BlockSpec((None, None, 1, GROUP * tq), lambda g, i: (g, i, 0, 0)),
        out_shape=jax.ShapeDtypeStruct((KV_B, n // tq, 1, GROUP * tq), F32),
        compiler_params=_params(("parallel", "parallel")),
    )(dheads, heads)


KWIN = 3 * BLOCK


def _window_scores(q, k_ref, j, n, nb, sink_row):
    c = k_ref.shape[0] - n
    start = pl.multiple_of(jnp.clip(j - 1, 0, nb - 3) * BLOCK, BLOCK)
    kw = k_ref[pl.ds(start, KWIN), :]
    kc = k_ref[pl.ds(n, c), :]
    s_loc = lax.dot_general(kw, q, NT, preferred_element_type=F32)
    s_ctx = lax.dot_general(kc, q, NT, preferred_element_type=F32)
    cols = GROUP * BLOCK
    qpos = j * BLOCK + lax.broadcasted_iota(jnp.int32, (KWIN, cols), 1) % BLOCK
    kpos = start + lax.broadcasted_iota(jnp.int32, (KWIN, cols), 0)
    s_loc = jnp.where(jnp.abs(qpos - kpos) <= WINDOW, s_loc, -jnp.inf)
    m = jnp.maximum(jnp.maximum(jnp.max(s_loc, axis=0, keepdims=True), jnp.max(s_ctx, axis=0, keepdims=True)),
                    sink_row)
    e_loc, e_ctx, e_sink = jnp.exp(s_loc - m), jnp.exp(s_ctx - m), jnp.exp(sink_row - m)
    inv = 1.0 / (jnp.sum(e_loc, axis=0, keepdims=True) + jnp.sum(e_ctx, axis=0, keepdims=True) + e_sink)
    return e_loc * inv, e_ctx * inv, e_sink * inv, start


def _sink_row(sink_ref, g):
    return jnp.concatenate([sink_ref[pl.ds(g * GROUP + hh, 1), :] for hh in range(GROUP)], axis=1)


def _attn_a_fwd(qa, ka, va, sink_b, heads_b, n):
    t = ka.shape[1]
    nb = n // BLOCK
    assert nb >= 3

    def body(q_ref, k_ref, v_ref, sink_ref, heads_b_ref, o_ref):
        g, j = pl.program_id(0), pl.program_id(1)
        q = q_ref[...].reshape(GROUP * BLOCK, HEAD_DIM)
        p_loc, p_ctx, _, start = _window_scores(q, k_ref, j, n, nb, _sink_row(sink_ref, g))
        vw = v_ref[pl.ds(start, KWIN), :]
        vc = v_ref[pl.ds(n, t - n), :]
        ot = (lax.dot_general(vw, p_loc.astype(BF16), TN, preferred_element_type=F32)
              + lax.dot_general(vc, p_ctx.astype(BF16), TN, preferred_element_type=F32))
        for hh in range(GROUP):
            o_ref[:, hh * HEAD_DIM:(hh + 1) * HEAD_DIM] = ot[:, hh * BLOCK:(hh + 1) * BLOCK].T.astype(BF16)

    return pl.pallas_call(
        body, name="attn_a_fwd", grid=(KV_A, nb),
        in_specs=[pl.BlockSpec((GROUP, BLOCK, HEAD_DIM), lambda g, j: (g, j, 0)),
                  pl.BlockSpec((None, t, HEAD_DIM), lambda g, j: (g, 0, 0)),
                  pl.BlockSpec((None, t, HEAD_DIM), lambda g, j: (g, 0, 0)),
                  pl.BlockSpec((HEADS_A, HEAD_DIM), lambda g, j: (0, 0)),
                  pl.BlockSpec(memory_space=pl.ANY)],
        out_specs=pl.BlockSpec((BLOCK, GROUP * HEAD_DIM), lambda g, j: (j, g)),
        out_shape=jax.ShapeDtypeStruct((n, MIX_WIDTH), BF16),
        input_output_aliases={4: 0},
        compiler_params=_params(("parallel", "parallel")),
    )(qa, ka, va, sink_b, heads_b)


def _attn_a_bwd(qa, ka, va, sink_b, dheads, n):
    t = ka.shape[1]
    c = t - n
    nb = n // BLOCK

    def body(q_ref, k_ref, v_ref, sink_ref, do_ref, dq_ref, dk_ref, dv_ref, dsink_ref):
        g, j = pl.program_id(0), pl.program_id(1)

        @pl.when(j == 0)
        def _():
            dk_ref[...] = jnp.zeros_like(dk_ref)
            dv_ref[...] = jnp.zeros_like(dv_ref)
            dsink_ref[...] = jnp.zeros_like(dsink_ref)

        q = q_ref[...].reshape(GROUP * BLOCK, HEAD_DIM)
        do = _heads_rows(do_ref, BLOCK)
        p_loc, p_ctx, p_sink, start = _window_scores(q, k_ref, j, n, nb, _sink_row(sink_ref, g))
        kw, vw = k_ref[pl.ds(start, KWIN), :], v_ref[pl.ds(start, KWIN), :]
        kc, vc = k_ref[pl.ds(n, c), :], v_ref[pl.ds(n, c), :]
        dp_loc = lax.dot_general(vw, do, NT, preferred_element_type=F32)
        dp_ctx = lax.dot_general(vc, do, NT, preferred_element_type=F32)
        dl = jnp.sum(p_loc * dp_loc, axis=0, keepdims=True) + jnp.sum(p_ctx * dp_ctx, axis=0, keepdims=True)
        ds_loc = (p_loc * (dp_loc - dl)).astype(BF16)
        ds_ctx = (p_ctx * (dp_ctx - dl)).astype(BF16)
        dqt = (lax.dot_general(kw, ds_loc, TN, preferred_element_type=F32)
               + lax.dot_general(kc, ds_ctx, TN, preferred_element_type=F32))
        for hh in range(GROUP):
            dq_ref[hh] = dqt[:, hh * BLOCK:(hh + 1) * BLOCK].T
        dk_ref[pl.ds(start, KWIN), :] += lax.dot_general(ds_loc, q, NN, preferred_element_type=F32)
        dv_ref[pl.ds(start, KWIN), :] += lax.dot_general(p_loc.astype(BF16), do, NN, preferred_element_type=F32)
        dk_ref[pl.ds(n, c), :] += lax.dot_general(ds_ctx, q, NN, preferred_element_type=F32)
        dv_ref[pl.ds(n, c), :] += lax.dot_general(p_ctx.astype(BF16), do, NN, preferred_element_type=F32)
        dsk = -(p_sink * dl)
        upd = [jnp.broadcast_to(jnp.sum(dsk[:, hh * BLOCK:(hh + 1) * BLOCK], axis=1, keepdims=True), (1, HEAD_DIM))
               for hh in range(GROUP)]
        dsink_ref[...] += jnp.concatenate(upd + [jnp.zeros((8 - GROUP, HEAD_DIM), F32)], axis=0)

    res = pl.BlockSpec((None, t, HEAD_DIM), lambda g, j: (g, 0, 0))
    return pl.pallas_call(
        body, name="attn_a_bwd", grid=(KV_A, nb),
        in_specs=[pl.BlockSpec((GROUP, BLOCK, HEAD_DIM), lambda g, j: (g, j, 0)), res, res,
                  pl.BlockSpec((HEADS_A, HEAD_DIM), lambda g, j: (0, 0)),
                  pl.BlockSpec((BLOCK, GROUP * HEAD_DIM), lambda g, j: (j, g))],
        out_specs=[pl.BlockSpec((GROUP, BLOCK, HEAD_DIM), lambda g, j: (g, j, 0)), res, res,
                   pl.BlockSpec((None, 8, HEAD_DIM), lambda g, j: (g, 0, 0))],
        out_shape=[jax.ShapeDtypeStruct((HEADS_A, n, HEAD_DIM), F32),
                   jax.ShapeDtypeStruct((KV_A, t, HEAD_DIM), F32),
                   jax.ShapeDtypeStruct((KV_A, t, HEAD_DIM), F32),
                   jax.ShapeDtypeStruct((KV_A, 8, HEAD_DIM), F32)],
        compiler_params=_params(("parallel", "arbitrary")),
    )(qa, ka, va, sink_b, dheads)


def _ln_stats(r):
    mu = jnp.mean(r, axis=-1, keepdims=True)
    cen = r - mu
    rstd = lax.rsqrt(jnp.mean(cen * cen, axis=-1, keepdims=True) + EPS)
    return cen * rstd, rstd


def _ln_bwd(dy, xhat, rstd, gain):
    dxh = dy * gain
    return rstd * (dxh - jnp.mean(dxh, axis=-1, keepdims=True)
                   - xhat * jnp.mean(dxh * xhat, axis=-1, keepdims=True))


def _accumulate_rows(ref, rows, i):
    pad = [jnp.zeros_like(rows[0])] * (8 - len(rows))
    upd = jnp.concatenate(rows + pad, axis=0)

    @pl.when(i == 0)
    def _():
        ref[...] = upd

    @pl.when(i != 0)
    def _():
        ref[...] += upd


def _colsum(v):
    return jnp.sum(v, axis=0, keepdims=True)


LN_TILE = 256


def _res_ln1(x, a, vec):
    n, d = x.shape

    def body(x_ref, a_ref, v_ref, xh_ref, rs_ref, u_ref):
        r1 = DN_ALPHA * x_ref[...] + v_ref[0:1, :] * a_ref[...]
        xhat, rstd = _ln_stats(r1)
        xh_ref[...] = xhat
        rs_ref[...] = rstd
        x1 = xhat * v_ref[1:2, :] + v_ref[2:3, :]
        u_ref[...] = (x1 * (1.0 + v_ref[3:4, :]) + v_ref[4:5, :]).astype(BF16)

    row = pl.BlockSpec((LN_TILE, d), lambda i: (i, 0))
    return pl.pallas_call(
        body, name="res_ln1", grid=(n // LN_TILE,),
        in_specs=[row, row, pl.BlockSpec((8, d), lambda i: (0, 0))],
        out_specs=[row, pl.BlockSpec((LN_TILE, 1), lambda i: (i, 0)), row],
        out_shape=[jax.ShapeDtypeStruct((n, d), F32), jax.ShapeDtypeStruct((n, 1), F32),
                   jax.ShapeDtypeStruct((n, d), BF16)],
        compiler_params=_params(("parallel",)),
    )(x, a, vec)


def _res_ln2_loss(xhat1, f, target, vec):
    n, d = f.shape

    def body(xh_ref, f_ref, t_ref, v_ref, dr_ref, df_ref, s_ref):
        i = pl.program_id(0)
        x1 = xh_ref[...] * v_ref[1:2, :] + v_ref[2:3, :]
        fv = f_ref[...]
        xhat, rstd = _ln_stats(DN_ALPHA * x1 + v_ref[0:1, :] * fv)
        err = xhat * v_ref[3:4, :] + v_ref[4:5, :] - t_ref[...]
        dy = err * (1.0 / d)
        dr2 = _ln_bwd(dy, xhat, rstd, v_ref[3:4, :])
        dr_ref[...] = dr2
        df_ref[...] = (dr2 * v_ref[0:1, :]).astype(BF16)
        _accumulate_rows(s_ref, [_colsum(dy * xhat), _colsum(dy), _colsum(dr2 * fv),
                                 _colsum(err * err) * (0.5 / d)], i)

    row = pl.BlockSpec((LN_TILE, d), lambda i: (i, 0))
    return pl.pallas_call(
        body, name="res_ln2_loss", grid=(n // LN_TILE,),
        in_specs=[row, row, row, pl.BlockSpec((8, d), lambda i: (0, 0))],
        out_specs=[row, row, pl.BlockSpec((8, d), lambda i: (0, 0))],
        out_shape=[jax.ShapeDtypeStruct((n, d), F32), jax.ShapeDtypeStruct((n, d), BF16),
                   jax.ShapeDtypeStruct((8, d), F32)],
        compiler_params=_params(("arbitrary",)),
    )(xhat1, f, target, vec)


def _ln1_bwd(du2, dr2, xhat1, rstd1, a, vec):
    n, d = du2.shape

    def body(du_ref, dr2_ref, xh_ref, rs_ref, a_ref, v_ref, dxp_ref, da_ref, s_ref):
        i = pl.program_id(0)
        du, xhat = du_ref[...], xh_ref[...]
        x1 = xhat * v_ref[1:2, :] + v_ref[2:3, :]
        dx1 = DN_ALPHA * dr2_ref[...] + du * (1.0 + v_ref[0:1, :])
        dr1 = _ln_bwd(dx1, xhat, rs_ref[...], v_ref[1:2, :])
        dxp_ref[...] = DN_ALPHA * dr1
        da_ref[...] = (dr1 * v_ref[3:4, :]).astype(BF16)
        _accumulate_rows(s_ref, [_colsum(du * x1), _colsum(du), _colsum(dx1 * xhat), _colsum(dx1),
                                 _colsum(dr1 * a_ref[...])], i)

    row = pl.BlockSpec((LN_TILE, d), lambda i: (i, 0))
    return pl.pallas_call(
        body, name="ln1_bwd", grid=(n // LN_TILE,),
        in_specs=[row, row, row, pl.BlockSpec((LN_TILE, 1), lambda i: (i, 0)), row,
                  pl.BlockSpec((8, d), lambda i: (0, 0))],
        out_specs=[row, row, pl.BlockSpec((8, d), lambda i: (0, 0))],
        out_shape=[jax.ShapeDtypeStruct((n, d), F32), jax.ShapeDtypeStruct((n, d), BF16),
                   jax.ShapeDtypeStruct((8, d), F32)],
        compiler_params=_params(("arbitrary",)),
    )(du2, dr2, xhat1, rstd1, a, vec)


def _mod1_bwd(du_all, dxp, x, ctx, mods):
    n, d = x.shape
    nx = n // ROW_TILE

    def body(du_ref, dxp_ref, x_ref, ctx_ref, m_ref, gx_ref, s_ref):
        i = pl.program_id(0)
        du = du_ref[...]
        zero = jnp.zeros((1, d), F32)

        @pl.when(i == 0)
        def _():
            s_ref[...] = jnp.zeros_like(s_ref)

        @pl.when(i < nx)
        def _():
            gx_ref[...] = dxp_ref[...] + du * (1.0 + m_ref[0:1, :])
            s_ref[...] += jnp.concatenate([_colsum(du * x_ref[...]), _colsum(du)] + [zero] * 6, axis=0)

        @pl.when(i >= nx)
        def _():
            s_ref[...] += jnp.concatenate([zero, zero, _colsum(du * ctx_ref[...]), _colsum(du)] + [zero] * 4, axis=0)

    lat = pl.BlockSpec((ROW_TILE, d), lambda i: (jnp.minimum(i, nx - 1), 0))
    return pl.pallas_call(
        body, name="mod1_bwd", grid=(nx + 1,),
        in_specs=[pl.BlockSpec((ROW_TILE, d), lambda i: (i, 0)), lat, lat,
                  pl.BlockSpec((ROW_TILE, d), lambda i: (0, 0)), pl.BlockSpec((8, d), lambda i: (0, 0))],
        out_specs=[lat, pl.BlockSpec((8, d), lambda i: (0, 0))],
        out_shape=[jax.ShapeDtypeStruct((n, d), F32), jax.ShapeDtypeStruct((8, d), F32)],
        compiler_params=_params(("arbitrary",)),
    )(du_all, dxp, x, ctx, mods)


FFN_TM = 512
FFN_TN = 512


def _gate_up(u2, wg, wu, after):
    n, d = u2.shape
    f = wg.shape[1]

    def body(u_ref, wg_ref, wu_ref, after_ref, g_ref, up_ref, h_ref):
        u = u_ref[...]
        g = lax.dot_general(u, wg_ref[...], NN, preferred_element_type=F32)
        up = lax.dot_general(u, wu_ref[...], NN, preferred_element_type=F32)
        g_ref[...] = g.astype(BF16)
        up_ref[...] = up.astype(BF16)
        h_ref[...] = (g * jax.nn.sigmoid(g) * up).astype(BF16)

    wspec = pl.BlockSpec((d, FFN_TN), lambda j, i: (0, j))
    ospec = pl.BlockSpec((FFN_TM, FFN_TN), lambda j, i: (i, j))
    return pl.pallas_call(
        body, name="gate_up", grid=(f // FFN_TN, n // FFN_TM),
        in_specs=[pl.BlockSpec((FFN_TM, d), lambda j, i: (i, 0)), wspec, wspec, pl.BlockSpec(memory_space=pl.ANY)],
        out_specs=[ospec, ospec, ospec],
        out_shape=[jax.ShapeDtypeStruct((n, f), BF16)] * 3,
        compiler_params=_params(("parallel", "parallel")),
    )(u2, wg, wu, after)


def _glu_bwd(df, wd, g, u):
    n, d = df.shape
    f = wd.shape[0]

    def body(df_ref, wd_ref, g_ref, u_ref, dg_ref, du_ref):
        dh = lax.dot_general(df_ref[...], wd_ref[...], NT, preferred_element_type=F32)
        gv = g_ref[...].astype(F32)
        sig = jax.nn.sigmoid(gv)
        du_ref[...] = (dh * (gv * sig)).astype(BF16)
        dg_ref[...] = (dh * u_ref[...].astype(F32) * (sig * (1.0 + gv * (1.0 - sig)))).astype(BF16)

    tm = min(2 * FFN_TM, n)
    ospec = pl.BlockSpec((tm, FFN_TN), lambda i, j: (i, j))
    return pl.pallas_call(
        body, name="glu_bwd", grid=(n // tm, f // FFN_TN),
        in_specs=[pl.BlockSpec((tm, d), lambda i, j: (i, 0)),
                  pl.BlockSpec((FFN_TN, d), lambda i, j: (j, 0)), ospec, ospec],
        out_specs=[ospec, ospec],
        out_shape=[jax.ShapeDtypeStruct((n, f), BF16), jax.ShapeDtypeStruct((n, f), BF16)],
        compiler_params=_params(("parallel", "parallel")),
    )(df, wd, g, u)


def _du2(dg, du, wg, wu):
    n, f = dg.shape
    d = wg.shape[0]
    tm, tn, tk = min(1024, n), 1024, 1408
    nk = f // tk

    def body(dg_ref, du_ref, wg_ref, wu_ref, o_ref, acc_ref):
        kk = pl.program_id(2)
        part = (lax.dot_general(dg_ref[...], wg_ref[...], NT, preferred_element_type=F32)
                + lax.dot_general(du_ref[...], wu_ref[...], NT, preferred_element_type=F32))

        @pl.when(kk == 0)
        def _():
            acc_ref[...] = part

        @pl.when(kk != 0)
        def _():
            acc_ref[...] += part

        @pl.when(kk == nk - 1)
        def _():
            o_ref[...] = acc_ref[...]

    aspec = pl.BlockSpec((tm, tk), lambda i, j, kk: (i, kk))
    wspec = pl.BlockSpec((tn, tk), lambda i, j, kk: (j, kk))
    return pl.pallas_call(
        body, name="du2", grid=(n // tm, d // tn, nk),
        in_specs=[aspec, aspec, wspec, wspec],
        out_specs=pl.BlockSpec((tm, tn), lambda i, j, kk: (i, j)),
        out_shape=jax.ShapeDtypeStruct((n, d), F32),
        scratch_shapes=[pltpu.VMEM((tm, tn), F32)],
        compiler_params=_params(("parallel", "parallel", "arbitrary")),
    )(dg, du, wg, wu)


def _rows8(rows, d=D_MODEL):
    rows = [r.reshape(1, d).astype(F32) for r in rows]
    return jnp.concatenate(rows + [jnp.zeros((8 - len(rows), d), F32)], axis=0)


W_GROUPS = (("w_in",), ("w_out", "w_gate", "w_up"), ("w_down",))
G_GROUPS = (("w_down", "w_gate", "w_up"), ("w_out",), ("w_in",))


def _layer_fwd_bwd(x, ctx, target, mod, mod_ctx, weights, prefetch, grads_out,
                   q_g, k_g, sink, ln1_g, ln1_b, ln2_g, ln2_b):
    n, d = x.shape
    c = ctx.shape[0]
    sh1, sc1, g1, sh2, sc2, g2 = [mod[:, k * d:(k + 1) * d] for k in range(6)]
    csh1, csc1 = mod_ctx[:, 0:d], mod_ctx[:, d:2 * d]
    cos, sin_a, sin_b = _rope_tables(n, c)
    sink_b = jnp.broadcast_to(sink.reshape(HEADS_A, 1), (HEADS_A, HEAD_DIM)).astype(F32)

    u_all = _modulate_rows(x, ctx, _rows8([sc1, sh1, csc1, csh1]))
    (w_in,) = weights(0, u_all)
    h_all = _matmul(u_all, w_in, name="qkv_proj", tm=_fit(n + c, 1088), tn=1024, tk=2048, out_dtype=F32)
    qa, ka, va, qb, kb, vb = _qkv_post(h_all, cos, sin_a, sin_b, q_g, k_g)
    heads_b, lse = _attn_b_fwd(qb, kb, vb, n)
    zero = prefetch(1, heads_b)
    heads = _attn_a_fwd(qa, ka, va, sink_b + zero, heads_b, n)
    w_out, w_gate, w_up = weights(1, heads)
    a = _matmul(heads, w_out, name="out_proj", tm=1024, tn=1024, tk=2048, out_dtype=F32)
    xhat1, rstd1, u2 = _res_ln1(x, a, _rows8([g1, ln1_g, ln1_b, sc2, sh2]))
    zero = prefetch(2, u2)
    gg, uu, hh = _gate_up(u2, w_gate, w_up, zero.reshape(1, 1))
    (w_down,) = weights(2, hh)
    f = _matmul(hh, w_down, name="ffn_down", tm=512, tn=512, tk=FFN, out_dtype=F32)
    dr2, df, s_ln2 = _res_ln2_loss(xhat1, f, target, _rows8([g2, ln1_g, ln1_b, ln2_g, ln2_b]))

    dgg, duu = _glu_bwd(df, w_down, gg, uu)
    dw_down = _matmul(hh, df, name="dw_down", ta=True, tm=512, tn=1024, tk=n, out_dtype=BF16)
    dw_gate = _matmul(u2, dgg, name="dw_gate", ta=True, tm=1024, tn=512, tk=n, out_dtype=BF16)
    dw_up = _matmul(u2, duu, name="dw_up", ta=True, tm=1024, tn=512, tk=n, out_dtype=BF16)
    zero = grads_out(0, [dw_down, dw_gate, dw_up])
    du2 = _du2(dgg, duu, w_gate, w_up)
    dxp, da, s_ln1 = _ln1_bwd(du2, dr2, xhat1, rstd1, a, _rows8([sc2, ln1_g, ln1_b, g1]) + zero)

    dheads = _matmul(da, w_out, name="d_heads", tb=True, tm=1024, tn=1024, tk=2048, out_dtype=BF16)
    dw_out = _matmul(heads, da, name="dw_out", ta=True, tm=1024, tn=1024, tk=n, out_dtype=BF16)
    zero = grads_out(1, [dw_out])
    delta = _delta_rows(dheads, heads)
    dqa, dka, dva, dsink = _attn_a_bwd(qa, ka, va, sink_b + zero, dheads, n)
    dqb, dkb, dvb = _attn_b_bwd(qb, kb, vb, dheads, lse, delta, n)
    dh_all, s_gain = _qkv_bwd_post(h_all, cos, sin_a, sin_b, q_g, k_g, dqa, dka, dva, dqb, dkb, dvb, n)
    dw_in = _matmul(u_all, dh_all, name="dw_in", ta=True, tm=1024, tn=1024, tk=n + c, out_dtype=BF16)
    zero = grads_out(2, [dw_in])
    du_all = _matmul(dh_all, w_in, name="d_u1", tb=True, tm=_fit(n + c, 1088), tn=1024, tk=IN_WIDTH, out_dtype=F32,
                     after=zero.reshape(1, 1))
    grad_x, s_mod1 = _mod1_bwd(du_all, dxp, x, ctx, _rows8([sc1]) + zero)

    dsink_row = jnp.concatenate([dsink[0, 0:GROUP, 0], dsink[1, 0:GROUP, 0]]).reshape(1, HEADS_A)
    misc = jnp.concatenate([s_gain[0:1], s_gain[1:2], dsink_row,
                            jnp.zeros((1, d - 2 * HEAD_DIM - HEADS_A), F32)], axis=1)
    partial = jnp.concatenate([
        s_mod1[1:2], s_mod1[0:1], s_ln1[4:5],
        s_ln1[1:2], s_ln1[0:1], s_ln2[2:3],
        s_mod1[3:4], s_mod1[2:3],
        s_ln1[2:3], s_ln1[3:4], s_ln2[0:1], s_ln2[1:2],
        s_ln2[3:4], misc, jnp.zeros((2, d), F32)], axis=0)
    return grad_x, partial


ANY = pl.BlockSpec(memory_space=pl.ANY)
VMEM_FULL = pl.BlockSpec(memory_space=pltpu.VMEM)
N_CHIP_PEERS = 3


def _me():
    return lax.axis_index("x"), lax.axis_index("y"), lax.axis_index("c")


def _other_chips(x, y):
    return [(1 - x, y), (x, 1 - y), (1 - x, 1 - y)]


def _shard_of(chip):
    return 2 * chip[0] + chip[1]


def _dev_index(x, y, c):
    return 4 * x + 2 * y + c


def _rcopy(src, dst, send_sems, recv_sems, k, dev):
    return pltpu.make_async_remote_copy(src_ref=src, dst_ref=dst, send_sem=send_sems.at[k], recv_sem=recv_sems.at[k],
                                        device_id=dev, device_id_type=MESH)


BIG = (("w_in", (D_MODEL, IN_WIDTH), 1), ("w_out", (MIX_WIDTH, D_MODEL), 0), ("w_gate", (D_MODEL, FFN), 1),
       ("w_up", (D_MODEL, FFN), 1), ("w_down", (FFN, D_MODEL), 0))


def _sub(ref, axis, idx, size):
    start = pl.multiple_of(idx * size, size)
    return ref.at[pl.ds(start, size), :] if axis == 0 else ref.at[:, pl.ds(start, size)]


def _shape_div(shape, axis, parts):
    return tuple(s // parts if a == axis else s for a, s in enumerate(shape))


def _piece(a, ref, shard, half):
    _, full, axis = BIG[a]
    view = _sub(ref, axis, shard, full[axis] // N_SHARD)
    return _sub(view, 1 - axis, half, full[1 - axis] // 2)


HBM = pl.BlockSpec(memory_space=pltpu.HBM)
SEM = pl.BlockSpec(memory_space=pltpu.SEMAPHORE)
EFFECT = pltpu.SideEffectType.DATAFLOW_SIDE_EFFECTING
BIG_INDEX = {name: a for a, (name, _, _) in enumerate(BIG)}


def _in_hbm(arr):
    return pltpu.with_memory_space_constraint(arr, pltpu.HBM)


def _gather_start(tag, arrs, bufs, prev):
    n_arr = len(arrs)

    def body(*refs):
        ins = refs[:n_arr]
        send_sems, recv_sems = refs[n_arr + 1], refs[n_arr + 2]
        token = refs[-1]
        x, y, c = _me()
        s_me = _shard_of((x, y))
        for i, a in enumerate(arrs):
            mine = _piece(a, ins[i], s_me, c)
            for j, chip in enumerate(_other_chips(x, y)):
                _rcopy(mine, mine, send_sems, recv_sems, N_CHIP_PEERS * i + j, (*chip, c)).start()
        token[...] = jnp.zeros_like(token)

    n_sem = N_CHIP_PEERS * n_arr
    outs = pl.pallas_call(
        body, name="gather_start_" + tag,
        in_specs=[HBM] * n_arr + [ANY],
        out_specs=[SEM, SEM] + [HBM] * n_arr + [VMEM_FULL],
        out_shape=[pltpu.SemaphoreType.DMA((n_sem,)), pltpu.SemaphoreType.DMA((n_sem,))]
        + [pltpu.HBM(BIG[a][1], BF16) for a in arrs] + [jax.ShapeDtypeStruct((8, HEAD_DIM), F32)],
        input_output_aliases={i: 2 + i for i in range(n_arr)},
        compiler_params=pltpu.CompilerParams(has_side_effects=EFFECT),
    )(*[_in_hbm(b) for b in bufs], prev)
    return outs[0], outs[1], list(outs[2:2 + n_arr]), outs[-1]


def _gather_wait(tag, arrs, send_sems, recv_sems, bufs, after):
    n_arr = len(arrs)

    def body(*refs):
        ins = refs[:n_arr]
        send_sems_, recv_sems_ = refs[n_arr], refs[n_arr + 1]
        x, y, c = _me()
        s_me = _shard_of((x, y))
        for i, a in enumerate(arrs):
            mine = _piece(a, ins[i], s_me, c)
            for j, chip in enumerate(_other_chips(x, y)):
                landed = _piece(a, ins[i], _shard_of(chip), c)
                cp = _rcopy(mine, landed, send_sems_, recv_sems_, N_CHIP_PEERS * i + j, (*chip, c))
                cp.wait_send()
                cp.wait_recv()

    outs = pl.pallas_call(
        body, name="gather_wait_" + tag,
        in_specs=[HBM] * n_arr + [SEM, SEM, ANY],
        out_specs=[HBM] * n_arr,
        out_shape=[pltpu.HBM(BIG[a][1], BF16) for a in arrs],
        input_output_aliases={i: i for i in range(n_arr)},
        compiler_params=pltpu.CompilerParams(has_side_effects=EFFECT),
    )(*bufs, send_sems, recv_sems, after)
    return list(outs)


def _gather_forward(tag, arrs, bufs):
    n_arr = len(arrs)

    def body(*refs):
        outs = refs[n_arr:2 * n_arr]
        send_sems, recv_sems = refs[2 * n_arr:]
        x, y, c = _me()
        sibling = (x, y, 1 - c)
        chips = _other_chips(x, y)
        copies = []
        for i, a in enumerate(arrs):
            for j, chip in enumerate(chips):
                landed = _piece(a, outs[i], _shard_of(chip), c)
                cp = _rcopy(landed, landed, send_sems, recv_sems, N_CHIP_PEERS * i + j, sibling)
                cp.start()
                copies.append(cp)
        for i, a in enumerate(arrs):
            for j, chip in enumerate(chips):
                other = _piece(a, outs[i], _shard_of(chip), 1 - c)
                _rcopy(other, other, send_sems, recv_sems, N_CHIP_PEERS * i + j, sibling).wait_recv()
        for cp in copies:
            cp.wait_send()

    n_sem = N_CHIP_PEERS * n_arr
    return list(pl.pallas_call(
        body, name="gather_forward_" + tag,
        in_specs=[ANY] * n_arr, out_specs=[ANY] * n_arr,
        out_shape=[jax.ShapeDtypeStruct(BIG[a][1], BF16) for a in arrs],
        input_output_aliases={i: i for i in range(n_arr)},
        scratch_shapes=[pltpu.SemaphoreType.DMA((n_sem,)), pltpu.SemaphoreType.DMA((n_sem,))],
    )(*bufs))


def _forward_start(tag, arrs, bufs):
    n_arr = len(arrs)

    def body(*refs):
        ins = refs[:n_arr]
        send_sems, recv_sems = refs[n_arr], refs[n_arr + 1]
        token = refs[-1]
        x, y, c = _me()
        for i, a in enumerate(arrs):
            for j, chip in enumerate(_other_chips(x, y)):
                landed = _piece(a, ins[i], _shard_of(chip), c)
                _rcopy(landed, landed, send_sems, recv_sems, N_CHIP_PEERS * i + j, (x, y, 1 - c)).start()
        token[...] = jnp.zeros_like(token)

    n_sem = N_CHIP_PEERS * n_arr
    outs = pl.pallas_call(
        body, name="gather_forward_start_" + tag,
        in_specs=[HBM] * n_arr,
        out_specs=[SEM, SEM] + [HBM] * n_arr + [VMEM_FULL],
        out_shape=[pltpu.SemaphoreType.DMA((n_sem,)), pltpu.SemaphoreType.DMA((n_sem,))]
        + [pltpu.HBM(BIG[a][1], BF16) for a in arrs] + [jax.ShapeDtypeStruct((8, HEAD_DIM), F32)],
        input_output_aliases={i: 2 + i for i in range(n_arr)},
        compiler_params=pltpu.CompilerParams(has_side_effects=EFFECT),
    )(*bufs)
    return outs[0], outs[1], list(outs[2:2 + n_arr]), outs[-1]


def _forward_wait(tag, arrs, send_sems, recv_sems, bufs, after):
    n_arr = len(arrs)

    def body(*refs):
        ins = refs[:n_arr]
        send_sems_, recv_sems_ = refs[n_arr], refs[n_arr + 1]
        x, y, c = _me()
        for i, a in enumerate(arrs):
            for j, chip in enumerate(_other_chips(x, y)):
                mine = _piece(a, ins[i], _shard_of(chip), c)
                other = _piece(a, ins[i], _shard_of(chip), 1 - c)
                cp = _rcopy(mine, other, send_sems_, recv_sems_, N_CHIP_PEERS * i + j, (x, y, 1 - c))
                cp.wait_send()
                cp.wait_recv()

    outs = pl.pallas_call(
        body, name="gather_forward_wait_" + tag,
        in_specs=[HBM] * n_arr + [SEM, SEM, ANY],
        out_specs=[HBM] * n_arr,
        out_shape=[pltpu.HBM(BIG[a][1], BF16) for a in arrs],
        input_output_aliases={i: i for i in range(n_arr)},
        compiler_params=pltpu.CompilerParams(has_side_effects=EFFECT),
    )(*bufs, send_sems, recv_sems, after)
    return list(outs)


def _peers(x, y, c):
    return [(x ^ (mask >> 2), y ^ ((mask >> 1) & 1), c ^ (mask & 1)) for mask in range(1, N_DEV)]


def _received_shape(a):
    _, full, axis = BIG[a]
    return (N_DEV - 1,) + _shape_div(_shape_div(full, 1 - axis, 2), axis, N_SHARD)


def _pieces_start(tag, arrs, dws):
    n_arr = len(arrs)

    def body(*refs):
        srcs, lands = refs[:n_arr], refs[n_arr:2 * n_arr]
        send_sems, recv_sems = refs[2 * n_arr], refs[2 * n_arr + 1]
        token = refs[-1]
        x, y, c = _me()
        for i, a in enumerate(arrs):
            for k, peer in enumerate(_peers(x, y, c)):
                src = _piece(a, srcs[i], _shard_of(peer[:2]), peer[2])
                _rcopy(src, lands[i].at[k], send_sems, recv_sems, (N_DEV - 1) * i + k, peer).start()
        token[...] = jnp.zeros_like(token)

    n_sem = (N_DEV - 1) * n_arr
    lands = [_in_hbm(lax.empty(_received_shape(a), BF16)) for a in arrs]
    outs = pl.pallas_call(
        body, name="grad_pieces_start_" + tag,
        in_specs=[HBM] * (2 * n_arr),
        out_specs=[SEM, SEM] + [HBM] * (2 * n_arr) + [VMEM_FULL],
        out_shape=[pltpu.SemaphoreType.DMA((n_sem,)), pltpu.SemaphoreType.DMA((n_sem,))]
        + [pltpu.HBM(BIG[a][1], BF16) for a in arrs] + [pltpu.HBM(_received_shape(a), BF16) for a in arrs]
        + [jax.ShapeDtypeStruct((8, HEAD_DIM), F32)],
        input_output_aliases={i: 2 + i for i in range(2 * n_arr)},
        compiler_params=pltpu.CompilerParams(has_side_effects=EFFECT),
    )(*[_in_hbm(dw) for dw in dws], *lands)
    return outs[0], outs[1], list(outs[2:2 + n_arr]), list(outs[2 + n_arr:2 + 2 * n_arr]), outs[-1]


def _pieces_wait(tag, arrs, send_sems, recv_sems, dws, lands, after):
    n_arr = len(arrs)

    def body(*refs):
        srcs, lands_ = refs[:n_arr], refs[n_arr:2 * n_arr]
        send_sems_, recv_sems_ = refs[2 * n_arr], refs[2 * n_arr + 1]
        x, y, c = _me()
        for i, a in enumerate(arrs):
            for k, peer in enumerate(_peers(x, y, c)):
                src = _piece(a, srcs[i], _shard_of(peer[:2]), peer[2])
                cp = _rcopy(src, lands_[i].at[k], send_sems_, recv_sems_, (N_DEV - 1) * i + k, peer)
                cp.wait_send()
                cp.wait_recv()

    outs = pl.pallas_call(
        body, name="grad_pieces_wait_" + tag,
        in_specs=[HBM] * (2 * n_arr) + [SEM, SEM, ANY],
        out_specs=[HBM] * (2 * n_arr),
        out_shape=[pltpu.HBM(BIG[a][1], BF16) for a in arrs] + [pltpu.HBM(_received_shape(a), BF16) for a in arrs],
        input_output_aliases={i: i for i in range(2 * n_arr)},
        compiler_params=pltpu.CompilerParams(has_side_effects=EFFECT),
    )(*dws, *lands, send_sems, recv_sems, after)
    return list(outs[:n_arr]), list(outs[n_arr:])


def _join_halves(tag, g_halves):
    n_arr = len(g_halves)

    def body(*refs):
        ins, outs = refs[:n_arr], refs[n_arr:2 * n_arr]
        send_sems, recv_sems = refs[2 * n_arr:]
        x, y, c = _me()
        copies = []
        for i in range(n_arr):
            cp = _rcopy(ins[i], outs[i], send_sems, recv_sems, i, (x, y, 1 - c))
            cp.start()
            copies.append(cp)
        for cp in copies:
            cp.wait()

    return list(pl.pallas_call(
        body, name="grad_join_halves_" + tag,
        in_specs=[ANY] * n_arr, out_specs=[ANY] * n_arr,
        out_shape=[jax.ShapeDtypeStruct(g.shape, F32) for g in g_halves],
        scratch_shapes=[pltpu.SemaphoreType.DMA((n_arr,)), pltpu.SemaphoreType.DMA((n_arr,))],
    )(*g_halves))


def _piece_sum(a, dw, shard, core, received):
    name, full, axis = BIG[a]
    rows, cols = _received_shape(a)[1:]
    tr = _fit(rows, ROW_TILE)
    nbr = rows // tr

    def body(w_ref, dw_ref, rec_ref, o_ref):
        acc = dw_ref[...].astype(F32)
        for k in range(N_DEV - 1):
            acc = acc + rec_ref[k].astype(F32)
        o_ref[...] = acc

    if axis == 0:
        own = pl.BlockSpec((tr, cols), lambda i, w: (w[0] * nbr + i, w[1]))
    else:
        own = pl.BlockSpec((tr, cols), lambda i, w: (w[1] * nbr + i, w[0]))
    return pl.pallas_call(
        body, name="grad_sum_pieces_" + name,
        grid_spec=pltpu.PrefetchScalarGridSpec(
            num_scalar_prefetch=1, grid=(nbr,),
            in_specs=[own, pl.BlockSpec((N_DEV - 1, tr, cols), lambda i, w: (0, i, 0))],
            out_specs=pl.BlockSpec((tr, cols), lambda i, w: (i, 0))),
        out_shape=jax.ShapeDtypeStruct((rows, cols), F32),
        compiler_params=_params(("parallel",)),
    )(jnp.stack([shard, core]).astype(jnp.int32), dw, received)


def _scatter_begin(tag, arrs, dws):
    send_sems, recv_sems, dws, lands, token = _pieces_start(tag, arrs, dws)
    return (send_sems, recv_sems, dws, lands), token[0, 0]


def _scatter_end(tag, arrs, state, after):
    x, y, c = _me()
    send_sems, recv_sems, dws, lands = state
    dws, lands = _pieces_wait(tag, arrs, send_sems, recv_sems, dws, lands, after)
    g_own = [_piece_sum(a, dw, _shard_of((x, y)), c, r) for a, dw, r in zip(arrs, dws, lands)]
    return g_own, _join_halves(tag, g_own)


def _gather_rows(block, name, after):
    r, d = block.shape

    def body(in_ref, after_ref, out_ref, send_sems, recv_sems):
        x, y, c = _me()
        out_ref[_dev_index(x, y, c)] = in_ref[...]
        copies = []
        for mask in range(1, N_DEV):
            peer = (x ^ (mask >> 2), y ^ ((mask >> 1) & 1), c ^ (mask & 1))
            cp = _rcopy(in_ref, out_ref.at[_dev_index(x, y, c)], send_sems, recv_sems, mask - 1, peer)
            cp.start()
            copies.append((cp, peer))
        for mask in range(1, N_DEV):
            peer = (x ^ (mask >> 2), y ^ ((mask >> 1) & 1), c ^ (mask & 1))
            landed = out_ref.at[_dev_index(*peer)]
            _rcopy(landed, landed, send_sems, recv_sems, mask - 1, peer).wait_recv()
        for cp, _ in copies:
            cp.wait_send()

    return pl.pallas_call(
        body, name=name, in_specs=[VMEM_FULL, ANY], out_specs=VMEM_FULL,
        out_shape=jax.ShapeDtypeStruct((N_DEV, r, d), F32),
        scratch_shapes=[pltpu.SemaphoreType.DMA((N_DEV - 1,)), pltpu.SemaphoreType.DMA((N_DEV - 1,))],
    )(block, after)


ADA_ROWS = 80
ADA_W = 6 * D_MODEL // N_SHARD


def _ada_forward(c_block, cctx_block, w_ada, b_shard):
    d = c_block.shape[1]

    def body(c_ref, cc_ref, w_ref, b_ref, act_ref, mods_ref, raw, mloc, send_sems, recv_sems):
        x, y, c = _me()
        me = _dev_index(x, y, c)
        s_me = _shard_of((x, y))
        raw[72:ADA_ROWS, :] = jnp.zeros((ADA_ROWS - 72, d), F32)
        raw[pl.ds(pl.multiple_of(me * 8, 8), 8), :] = c_ref[...]
        raw[64:72, :] = cc_ref[...]
        sends = []
        for mask in range(1, N_DEV):
            peer = (x ^ (mask >> 2), y ^ ((mask >> 1) & 1), c ^ (mask & 1))
            cp = _rcopy(c_ref, raw.at[pl.ds(pl.multiple_of(me * 8, 8), 8), :], send_sems, recv_sems, mask - 1, peer)
            cp.start()
            sends.append(cp)
        for mask in range(1, N_DEV):
            peer = (x ^ (mask >> 2), y ^ ((mask >> 1) & 1), c ^ (mask & 1))
            landed = raw.at[pl.ds(pl.multiple_of(_dev_index(*peer) * 8, 8), 8), :]
            _rcopy(landed, landed, send_sems, recv_sems, mask - 1, peer).wait_recv()
        v = raw[...]
        act = v * jax.nn.sigmoid(v)
        act_ref[...] = act
        mloc[...] = lax.dot_general(act.astype(BF16), w_ref[...].astype(BF16), NN,
                                    preferred_element_type=F32) + b_ref[...]
        mods_ref[s_me, 0:8, :] = mloc[pl.ds(pl.multiple_of(me * 8, 8), 8), :]
        mods_ref[s_me, 8:16, :] = mloc[64:72, :]
        base = N_DEV - 1
        for j, chip in enumerate(_other_chips(x, y)):
            peer = (*chip, c)
            rows = mloc.at[pl.ds(pl.multiple_of(_dev_index(*peer) * 8, 8), 8), :]
            cp = _rcopy(rows, mods_ref.at[s_me, 0:8, :], send_sems, recv_sems, base + 2 * j, peer)
            cp.start()
            sends.append(cp)
            cp = _rcopy(mloc.at[64:72, :], mods_ref.at[s_me, 8:16, :], send_sems, recv_sems, base + 2 * j + 1, peer)
            cp.start()
            sends.append(cp)
        for j, chip in enumerate(_other_chips(x, y)):
            for part in range(2):
                landed = mods_ref.at[_shard_of(chip), 8 * part:8 * part + 8, :]
                _rcopy(landed, landed, send_sems, recv_sems, base + 2 * j + part, (*chip, c)).wait_recv()
        for cp in sends:
            cp.wait_send()

    n_sem = N_DEV - 1 + 2 * N_CHIP_PEERS
    return pl.pallas_call(
        body, name="ada_forward",
        in_specs=[VMEM_FULL] * 4, out_specs=[VMEM_FULL, VMEM_FULL],
        out_shape=[jax.ShapeDtypeStruct((ADA_ROWS, d), F32), jax.ShapeDtypeStruct((N_SHARD, 16, ADA_W), F32)],
        scratch_shapes=[pltpu.VMEM((ADA_ROWS, d), F32), pltpu.VMEM((ADA_ROWS, ADA_W), F32),
                        pltpu.SemaphoreType.DMA((n_sem,)), pltpu.SemaphoreType.DMA((n_sem,))],
        compiler_params=pltpu.CompilerParams(vmem_limit_bytes=VMEM_LIMIT),
    )(c_block, cctx_block, w_ada, b_shard)


def _small_reduce(gathered):
    d = gathered.shape[2]

    def body(g_ref, o_ref):
        tot = g_ref[0]
        for i in range(1, N_DEV):
            tot = tot + g_ref[i]
        o_ref[...] = tot
        o_ref[0:2, :] = tot[0:2] + tot[6:8]
        o_ref[12:13, :] = jnp.broadcast_to(jnp.sum(tot[12:13], axis=1, keepdims=True), (1, d))

    return pl.pallas_call(body, name="small_reduce", in_specs=[VMEM_FULL], out_specs=VMEM_FULL,
                          out_shape=jax.ShapeDtypeStruct((16, d), F32))(gathered)


def _cctx_grad(gathered, c_ctx):
    d = gathered.shape[2]

    def body(g_ref, c_ref, o_ref):
        tot = g_ref[0, 0:1, :]
        for chip in range(1, N_SHARD):
            tot = tot + g_ref[2 * chip, 0:1, :]
        v = c_ref[...]
        sig = jax.nn.sigmoid(v)
        o_ref[...] = tot * (sig * (1.0 + v * (1.0 - sig)))

    return pl.pallas_call(body, name="cctx_grad", in_specs=[VMEM_FULL, VMEM_FULL], out_specs=VMEM_FULL,
                          out_shape=jax.ShapeDtypeStruct((1, d), F32))(gathered, c_ctx.reshape(1, d))


def _cast_into_full(w, shard, full, axis, name):
    r, cdim = w.shape
    tr = _fit(r, ROW_TILE)
    nbr = r // tr

    def body(s_ref, w_ref, o_ref):
        o_ref[...] = w_ref[...].astype(BF16)

    if axis == 0:
        out_spec = pl.BlockSpec((tr, cdim), lambda i, s: (s[0] * nbr + i, 0))
    else:
        out_spec = pl.BlockSpec((tr, cdim), lambda i, s: (i, s[0]))
    return pl.pallas_call(
        body, name=name,
        grid_spec=pltpu.PrefetchScalarGridSpec(
            num_scalar_prefetch=1, grid=(nbr,), in_specs=[pl.BlockSpec((tr, cdim), lambda i, s: (i, 0))],
            out_specs=out_spec),
        out_shape=jax.ShapeDtypeStruct(full, BF16), compiler_params=_params(("parallel",)),
    )(shard.reshape(1).astype(jnp.int32), w)


def _adamw_halves(w, g_own, g_other, m, v, core, axis, name):
    r, cdim = w.shape
    hr, hc = (r // 2, cdim) if axis == 1 else (r, cdim // 2)
    assert g_own.shape == (hr, hc) and g_other.shape == (hr, hc)
    tr = _fit(hr, 128)
    nb = hr // tr
    c1 = 1.0 - ADAM_B1 ** ADAM_STEP
    c2 = 1.0 - ADAM_B2 ** ADAM_STEP

    def body(c_ref, w_ref, go_ref, gt_ref, m_ref, v_ref, g_ref, d_ref, nm_ref, nv_ref):
        gv = jnp.where(pl.program_id(0) == c_ref[0], go_ref[...], gt_ref[...])
        nm = ADAM_B1 * m_ref[...] + (1.0 - ADAM_B1) * gv
        nv = ADAM_B2 * v_ref[...] + (1.0 - ADAM_B2) * (gv * gv)
        g_ref[...] = gv
        nm_ref[...] = nm
        nv_ref[...] = nv
        d_ref[...] = -ADAM_LR * ((nm / c1) / (jnp.sqrt(nv / c2) + ADAM_EPS) + ADAM_WD * w_ref[...])

    if axis == 1:
        big = pl.BlockSpec((tr, hc), lambda p, i, c: (p * nb + i, 0))
    else:
        big = pl.BlockSpec((tr, hc), lambda p, i, c: (i, p))
    half = pl.BlockSpec((tr, hc), lambda p, i, c: (i, 0))
    sh = jax.ShapeDtypeStruct((r, cdim), F32)
    return pl.pallas_call(
        body, name=name,
        grid_spec=pltpu.PrefetchScalarGridSpec(
            num_scalar_prefetch=1, grid=(2, nb), in_specs=[big, half, half, big, big], out_specs=[big] * 4),
        out_shape=[sh] * 4, compiler_params=_params(("parallel", "parallel")),
    )(core.reshape(1).astype(jnp.int32), w, g_own, g_other, m, v)


def _adamw(w, g, m, v, name):
    r, cdim = w.shape
    tr = _fit(r, 128) if r % (ROW_TILE // 4) == 0 else r
    c1 = 1.0 - ADAM_B1 ** ADAM_STEP
    c2 = 1.0 - ADAM_B2 ** ADAM_STEP

    def body(w_ref, g_ref, m_ref, v_ref, d_ref, nm_ref, nv_ref):
        gv = g_ref[...]
        nm = ADAM_B1 * m_ref[...] + (1.0 - ADAM_B1) * gv
        nv = ADAM_B2 * v_ref[...] + (1.0 - ADAM_B2) * (gv * gv)
        nm_ref[...] = nm
        nv_ref[...] = nv
        d_ref[...] = -ADAM_LR * ((nm / c1) / (jnp.sqrt(nv / c2) + ADAM_EPS) + ADAM_WD * w_ref[...])

    spec = pl.BlockSpec((tr, cdim), lambda i: (i, 0))
    sh = jax.ShapeDtypeStruct((r, cdim), F32)
    return pl.pallas_call(body, name=name, grid=(r // tr,), in_specs=[spec] * 4, out_specs=[spec] * 3,
                          out_shape=[sh, sh, sh], compiler_params=_params(("parallel",)))(w, g, m, v)


SMALL = (("c_ctx", D_MODEL), ("b_ada", 6 * D_MODEL), ("q_norm_g", HEAD_DIM), ("k_norm_g", HEAD_DIM),
         ("sink_logit", HEADS_A), ("ln1_g", D_MODEL), ("ln1_b", D_MODEL), ("ln2_g", D_MODEL), ("ln2_b", D_MODEL))
WEIGHT_ORDER = ("c_ctx", "w_ada", "b_ada", "w_in", "q_norm_g", "k_norm_g", "sink_logit", "w_out", "ln1_g", "ln1_b",
                "w_gate", "w_up", "w_down", "ln2_g", "ln2_b")


def kernel(x, c, ctx, c_ctx, w_ada, b_ada, w_in, q_norm_g, k_norm_g, sink_logit, w_out, ln1_g, ln1_b, w_gate, w_up, w_down, ln2_g, ln2_b, loss_target, m_c_ctx, m_w_ada, m_b_ada, m_w_in, m_q_norm_g, m_k_norm_g, m_sink_logit, m_w_out, m_ln1_g, m_ln1_b, m_w_gate, m_w_up, m_w_down, m_ln2_g, m_ln2_b, v_c_ctx, v_w_ada, v_b_ada, v_w_in, v_q_norm_g, v_k_norm_g, v_sink_logit, v_w_out, v_ln1_g, v_ln1_b, v_w_gate, v_w_up, v_w_down, v_ln2_g, v_ln2_b):
    d = D_MODEL
    w = dict(c_ctx=c_ctx, w_ada=w_ada[0], b_ada=b_ada, w_in=w_in[0], q_norm_g=q_norm_g, k_norm_g=k_norm_g,
             sink_logit=sink_logit, w_out=w_out[0], ln1_g=ln1_g, ln1_b=ln1_b, w_gate=w_gate[0], w_up=w_up[0],
             w_down=w_down[0], ln2_g=ln2_g, ln2_b=ln2_b)
    m = dict(c_ctx=m_c_ctx, w_ada=m_w_ada[0], b_ada=m_b_ada, w_in=m_w_in[0], q_norm_g=m_q_norm_g, k_norm_g=m_k_norm_g,
             sink_logit=m_sink_logit, w_out=m_w_out[0], ln1_g=m_ln1_g, ln1_b=m_ln1_b, w_gate=m_w_gate[0],
             w_up=m_w_up[0], w_down=m_w_down[0], ln2_g=m_ln2_g, ln2_b=m_ln2_b)
    v = dict(c_ctx=v_c_ctx, w_ada=v_w_ada[0], b_ada=v_b_ada, w_in=v_w_in[0], q_norm_g=v_q_norm_g, k_norm_g=v_k_norm_g,
             sink_logit=v_sink_logit, w_out=v_w_out[0], ln1_g=v_ln1_g, ln1_b=v_ln1_b, w_gate=v_w_gate[0],
             w_up=v_w_up[0], w_down=v_w_down[0], ln2_g=v_ln2_g, ln2_b=v_ln2_b)
    mx, my, mc = _me()
    s_me = _shard_of((mx, my))
    me = _dev_index(mx, my, mc)
    pad8 = lambda row: jnp.concatenate([row.reshape(1, -1), jnp.zeros((7, row.size), F32)], axis=0)

    b_shard = lax.dynamic_slice(b_ada, (0, s_me * ADA_W), (1, ADA_W))
    act, mods4 = _ada_forward(pad8(c), pad8(c_ctx), w["w_ada"], b_shard)

    gathers = []
    prev, shard = mods4, s_me
    for k, names in enumerate(W_GROUPS):
        arrs = tuple(BIG_INDEX[name] for name in names)
        bufs = [_cast_into_full(w[name], shard, BIG[a][1], BIG[a][2], "cast_" + name) for name, a in zip(names, arrs)]
        send_sems, recv_sems, thru, prev = _gather_start("g%d" % k, arrs, bufs, prev)
        shard = s_me + prev[0, 0].astype(jnp.int32)
        gathers.append((arrs, send_sems, recv_sems, thru))

    forwards = {}

    def prefetch(k, after):
        arrs, send_sems, recv_sems, thru = gathers[k]
        landed = _gather_wait("g%d" % k, arrs, send_sems, recv_sems, thru, after)
        fwd_send, fwd_recv, landed, token = _forward_start("g%d" % k, arrs, landed)
        forwards[k] = (fwd_send, fwd_recv, landed)
        return token[0, 0]

    def weights(k, after):
        arrs, send_sems, recv_sems, thru = gathers[k]
        if k in forwards:
            return _forward_wait("g%d" % k, arrs, *forwards[k], after)
        landed = _gather_wait("g%d" % k, arrs, send_sems, recv_sems, thru, after)
        return _gather_forward("g%d" % k, arrs, landed)

    mod = jnp.transpose(mods4[:, 0:1, :], (1, 0, 2)).reshape(1, 6 * d) + prev[0, 0]
    mod_ctx = jnp.transpose(mods4[:, 8:9, :], (1, 0, 2)).reshape(1, 6 * d)

    scatters = {}

    def grads_out(k, dws):
        arrs = tuple(BIG_INDEX[name] for name in G_GROUPS[k])
        scatters[k], zero = _scatter_begin("g%d" % k, arrs, dws)
        return zero

    grad_x, partial = _layer_fwd_bwd(x[0], ctx[0], loss_target[0], mod, mod_ctx, weights, prefetch, grads_out,
                                     q_norm_g, k_norm_g, sink_logit, ln1_g, ln1_b, ln2_g, ln2_b)
    grads, delta, new_m, new_v = {}, {}, {}, {}

    after = partial
    for k, names in enumerate(G_GROUPS):
        arrs = tuple(BIG_INDEX[name] for name in names)
        g_own, g_other = _scatter_end("g%d" % k, arrs, scatters[k], after)
        for name, own, other in zip(names, g_own, g_other):
            grads[name], delta[name], new_m[name], new_v[name] = _adamw_halves(
                w[name], own, other, m[name], v[name], mc, BIG[BIG_INDEX[name]][2], "adamw_" + name)
            after = new_v[name]

    gathered = _gather_rows(partial, "gather_partials", after)
    tot = _small_reduce(gathered)
    grads["b_ada"] = tot[0:6].reshape(1, 6 * d)
    grads["ln1_g"], grads["ln1_b"], grads["ln2_g"], grads["ln2_b"] = tot[8:9], tot[9:10], tot[10:11], tot[11:12]
    grads["q_norm_g"] = tot[13:14, 0:HEAD_DIM]
    grads["k_norm_g"] = tot[13:14, HEAD_DIM:2 * HEAD_DIM]
    grads["sink_logit"] = tot[13:14, 2 * HEAD_DIM:2 * HEAD_DIM + HEADS_A]
    loss = tot[12, 0]

    dm_all = gathered[:, 0:6, :].reshape(N_DEV, 6 * d)
    dmc_tot = jnp.concatenate([tot[6:8].reshape(1, 2 * d), jnp.zeros((1, 4 * d), F32)], axis=1)
    dm_rows = jnp.concatenate([pad8(dm_all[i]) for i in range(N_DEV)] + [pad8(dmc_tot), jnp.zeros((8, 6 * d), F32)], axis=0)
    dm_shard = lax.dynamic_slice(dm_rows, (0, s_me * ADA_W), (ADA_ROWS, ADA_W))
    grads["w_ada"] = _matmul(act, dm_shard, name="dw_ada", ta=True, tm=1024, tn=1024, tk=ADA_ROWS, out_dtype=F32)
    dmc_shard = lax.dynamic_slice(pad8(dmc_tot), (0, s_me * ADA_W), (8, ADA_W))
    cc_part = _matmul(dmc_shard, w["w_ada"], name="d_cctx", tb=True, tm=8, tn=1024, tk=1536, out_dtype=F32)
    grads["c_ctx"] = _cctx_grad(_gather_rows(cc_part, "gather_cctx", tot), c_ctx).reshape(d)

    delta["w_ada"], new_m["w_ada"], new_v["w_ada"] = _adamw(w["w_ada"], grads["w_ada"], m["w_ada"], v["w_ada"],
                                                            "adamw_w_ada")
    pack = lambda t: jnp.concatenate([t[name].reshape(1, size) for name, size in SMALL], axis=1)
    pd, pm, pv = _adamw(pack(w), pack(grads), pack(m), pack(v), "adamw_small")
    off = 0
    for name, size in SMALL:
        delta[name], new_m[name], new_v[name] = [t[:, off:off + size].reshape(w[name].shape) for t in (pd, pm, pv)]
        grads[name] = grads[name].reshape(w[name].shape)
        off += size

    lead = lambda name, t: t[None] if name in ("w_ada", "w_in", "w_out", "w_gate", "w_up", "w_down") else t
    outs = [loss, grad_x[None]]
    for group in (grads, delta, new_m, new_v):
        outs += [lead(name, group[name]) for name in WEIGHT_ORDER]
    return tuple(outs)
```

```python
import functools
import math

import jax
import jax.numpy as jnp
from jax import lax
from jax.experimental import pallas as pl
from jax.experimental.pallas import tpu as pltpu

F32 = jnp.float32
BF16 = jnp.bfloat16
MESH = pl.DeviceIdType.MESH

D_MODEL = 2048
HEAD_DIM = 128
HEADS_A = 8
HEADS_B = 8
KV_A = 2
KV_B = 2
GROUP = 4
GRID_W = 64
WINDOW = 128
BLOCK = 128
FFN = 5632
IN_WIDTH = 3072
MIX_WIDTH = 2048
ROPE_THETA = 10000.0
EPS = 1e-6
ATTN_SCALE = HEAD_DIM ** -0.5
DN_ALPHA = 2.0 ** 0.25
N_SHARD = 4
N_DEV = 8

ADAM_LR = 0.001
ADAM_B1 = 0.9
ADAM_B2 = 0.999
ADAM_EPS = 1e-08
ADAM_WD = 0.01
ADAM_STEP = 10

QA0, KA0, VA0, QB0, KB0, VB0 = 0, 1024, 1280, 1536, 2560, 2816

VMEM_LIMIT = 56 * 1024 * 1024
ROW_TILE = 256
NN = (((1,), (0,)), ((), ()))
NT = (((1,), (1,)), ((), ()))
TN = (((0,), (0,)), ((), ()))


def _fit(total, pref):
    step = ROW_TILE // 4
    best = step
    for cand in range(step, pref + 1, step):
        if total % cand == 0:
            best = cand
    return best


def _params(sem=None):
    return pltpu.CompilerParams(dimension_semantics=sem, vmem_limit_bytes=VMEM_LIMIT)


def _matmul(a, b, *, name, ta=False, tb=False, tm, tn, tk, out_dtype, after=None):
    m = a.shape[1] if ta else a.shape[0]
    k = a.shape[0] if ta else a.shape[1]
    n = b.shape[0] if tb else b.shape[1]
    assert (b.shape[1] if tb else b.shape[0]) == k
    tm, tn, tk = min(tm, m), min(tn, n), min(tk, k)
    assert m % tm == 0 and n % tn == 0 and k % tk == 0, (name, m, n, k, tm, tn, tk)
    nk = k // tk
    dn = (((0 if ta else 1,), (1 if tb else 0,)), ((), ()))

    def product(a_ref, b_ref):
        return lax.dot_general(a_ref[...].astype(BF16), b_ref[...].astype(BF16), dn, preferred_element_type=F32)

    def body_whole_k(a_ref, b_ref, *rest):
        o_ref = rest[-1]
        o_ref[...] = product(a_ref, b_ref).astype(o_ref.dtype)

    def body(a_ref, b_ref, *rest):
        o_ref, acc_ref = rest[-2:]
        kk = pl.program_id(2)
        part = product(a_ref, b_ref)

        @pl.when(kk == 0)
        def _():
            acc_ref[...] = part

        @pl.when(kk != 0)
        def _():
            acc_ref[...] += part

        @pl.when(kk == nk - 1)
        def _():
            o_ref[...] = acc_ref[...].astype(o_ref.dtype)

    a_spec = (pl.BlockSpec((tk, tm), lambda i, j, kk: (kk, i)) if ta
              else pl.BlockSpec((tm, tk), lambda i, j, kk: (i, kk)))
    b_spec = (pl.BlockSpec((tn, tk), lambda i, j, kk: (j, kk)) if tb
              else pl.BlockSpec((tk, tn), lambda i, j, kk: (kk, j)))
    return pl.pallas_call(
        body_whole_k if nk == 1 else body, name=name, grid=(m // tm, n // tn, nk),
        in_specs=[a_spec, b_spec] + ([] if after is None else [pl.BlockSpec(memory_space=pl.ANY)]),
        out_specs=pl.BlockSpec((tm, tn), lambda i, j, kk: (i, j)),
        out_shape=jax.ShapeDtypeStruct((m, n), out_dtype),
        scratch_shapes=[] if nk == 1 else [pltpu.VMEM((tm, tn), F32)],
        compiler_params=_params(("parallel", "parallel", "arbitrary")),
    )(a, b, *([] if after is None else [after]))


def _modulate_rows(x, ctx, mods):
    n, d = x.shape
    c = ctx.shape[0]
    nx = n // ROW_TILE
    assert c == ROW_TILE

    def body(x_ref, ctx_ref, mods_ref, o_ref):
        i = pl.program_id(0)

        @pl.when(i < nx)
        def _():
            o_ref[...] = (x_ref[...] * (1.0 + mods_ref[0:1, :]) + mods_ref[1:2, :]).astype(BF16)

        @pl.when(i >= nx)
        def _():
            o_ref[...] = (ctx_ref[...] * (1.0 + mods_ref[2:3, :]) + mods_ref[3:4, :]).astype(BF16)

    return pl.pallas_call(
        body, name="modulate_rows", grid=(nx + 1,),
        in_specs=[pl.BlockSpec((ROW_TILE, d), lambda i: (jnp.minimum(i, nx - 1), 0)),
                  pl.BlockSpec((ROW_TILE, d), lambda i: (0, 0)),
                  pl.BlockSpec((8, d), lambda i: (0, 0))],
        out_specs=pl.BlockSpec((ROW_TILE, d), lambda i: (i, 0)),
        out_shape=jax.ShapeDtypeStruct((n + c, d), BF16),
        compiler_params=_params(("parallel",)),
    )(x, ctx, mods)


def _rope_tables(n, c):
    rows = n // GRID_W
    row_ids = jnp.repeat(jnp.arange(rows, dtype=F32), GRID_W)
    col_ids = jnp.tile(jnp.arange(GRID_W, dtype=F32), rows)
    axis_dim = HEAD_DIM // 2
    inv_freq = jnp.power(ROPE_THETA, -jnp.arange(0, axis_dim, 2, dtype=F32) / axis_dim)
    ang_r = row_ids[:, None] * inv_freq
    ang_c = col_ids[:, None] * inv_freq
    ang = jnp.concatenate([ang_r, ang_r, ang_c, ang_c], axis=-1)
    cos, sin = jnp.cos(ang), jnp.sin(ang)
    quarter = (jnp.arange(HEAD_DIM) // (HEAD_DIM // 4)) % 2
    sin_a = jnp.where(quarter == 0, -sin, 0.0)
    sin_b = jnp.where(quarter == 1, sin, 0.0)
    pad = lambda t, v: jnp.concatenate([t, jnp.full((c, HEAD_DIM), v, F32)], axis=0)
    return pad(cos, 1.0), pad(sin_a, 0.0), pad(sin_b, 0.0)


def _rope(x, cos, sin_a, sin_b):
    return x * cos + pltpu.roll(x, 96, 1) * sin_a + pltpu.roll(x, 32, 1) * sin_b


def _rope_t(dy, cos, sin_a, sin_b):
    return dy * cos - pltpu.roll(dy, 96, 1) * sin_a - pltpu.roll(dy, 32, 1) * sin_b


def _rms(x):
    r = lax.rsqrt(jnp.mean(x * x, axis=-1, keepdims=True) + EPS)
    return x * r, r


def _qkv_post(h_all, cos, sin_a, sin_b, q_g, k_g):
    t = h_all.shape[0]
    nt = t // ROW_TILE

    def body(h_ref, cos_ref, sa_ref, sb_ref, qg_ref, kg_ref, qa_ref, ka_ref, va_ref, qb_ref, kb_ref, vb_ref):
        cos_, sa, sb = cos_ref[...], sa_ref[...], sb_ref[...]
        sl = lambda off, hh: h_ref[:, off + hh * HEAD_DIM: off + (hh + 1) * HEAD_DIM]
        for hh in range(HEADS_A):
            qa_ref[hh] = (_rope(sl(QA0, hh), cos_, sa, sb) * ATTN_SCALE).astype(BF16)
        for hh in range(KV_A):
            ka_ref[hh] = _rope(sl(KA0, hh), cos_, sa, sb).astype(BF16)
            va_ref[hh] = sl(VA0, hh).astype(BF16)
        for hh in range(HEADS_B):
            xn, _ = _rms(sl(QB0, hh))
            qb_ref[hh] = (_rope(xn * qg_ref[...], cos_, sa, sb) * ATTN_SCALE).astype(BF16)
        for hh in range(KV_B):
            xn, _ = _rms(sl(KB0, hh))
            kb_ref[hh] = _rope(xn * kg_ref[...], cos_, sa, sb).astype(BF16)
            vb_ref[hh] = sl(VB0, hh).astype(BF16)

    tab = pl.BlockSpec((ROW_TILE, HEAD_DIM), lambda i: (i, 0))
    gain = pl.BlockSpec((1, HEAD_DIM), lambda i: (0, 0))
    hs = lambda nh: pl.BlockSpec((nh, ROW_TILE, HEAD_DIM), lambda i: (0, i, 0))
    sh = lambda nh: jax.ShapeDtypeStruct((nh, t, HEAD_DIM), BF16)
    return pl.pallas_call(
        body, name="qkv_post", grid=(nt,),
        in_specs=[pl.BlockSpec((ROW_TILE, IN_WIDTH), lambda i: (i, 0)), tab, tab, tab, gain, gain],
        out_specs=[hs(HEADS_A), hs(KV_A), hs(KV_A), hs(HEADS_B), hs(KV_B), hs(KV_B)],
        out_shape=[sh(HEADS_A), sh(KV_A), sh(KV_A), sh(HEADS_B), sh(KV_B), sh(KV_B)],
        compiler_params=_params(("parallel",)),
    )(h_all, cos, sin_a, sin_b, q_g, k_g)


def _qkv_bwd_post(h_all, cos, sin_a, sin_b, q_g, k_g, dqa, dka, dva, dqb, dkb, dvb, n):
    t = h_all.shape[0]
    nt = t // ROW_TILE
    nx = n // ROW_TILE

    def body(h_ref, cos_ref, sa_ref, sb_ref, qg_ref, kg_ref,
             dqa_ref, dka_ref, dva_ref, dqb_ref, dkb_ref, dvb_ref, dh_ref, gs_ref):
        i = pl.program_id(0)
        cos_, sa, sb = cos_ref[...], sa_ref[...], sb_ref[...]
        latent = (i < nx).astype(F32)
        sl = lambda off, hh: h_ref[:, off + hh * HEAD_DIM: off + (hh + 1) * HEAD_DIM]

        def put(off, hh, val):
            dh_ref[:, off + hh * HEAD_DIM: off + (hh + 1) * HEAD_DIM] = val.astype(BF16)

        def norm_bwd(x, gain, dy):
            xn, r = _rms(x)
            dxh = dy * gain
            dx = r * (dxh - xn * jnp.mean(dxh * xn, axis=-1, keepdims=True))
            return dx, jnp.sum(dy * xn, axis=0, keepdims=True)

        for hh in range(HEADS_A):
            put(QA0, hh, _rope_t(dqa_ref[hh] * (ATTN_SCALE * latent), cos_, sa, sb))
        for hh in range(KV_A):
            put(KA0, hh, _rope_t(dka_ref[hh], cos_, sa, sb))
            put(VA0, hh, dva_ref[hh])
        gq = jnp.zeros((1, HEAD_DIM), F32)
        gk = jnp.zeros((1, HEAD_DIM), F32)
        for hh in range(HEADS_B):
            dq_t = dqb_ref[hh // GROUP, :, (hh % GROUP) * ROW_TILE:(hh % GROUP + 1) * ROW_TILE]
            dy = _rope_t(dq_t.T * (ATTN_SCALE * latent), cos_, sa, sb)
            dx, g = norm_bwd(sl(QB0, hh), qg_ref[...], dy)
            put(QB0, hh, dx)
            gq = gq + g
        for hh in range(KV_B):
            dy = _rope_t(dkb_ref[hh], cos_, sa, sb)
            dx, g = norm_bwd(sl(KB0, hh), kg_ref[...], dy)
            put(KB0, hh, dx)
            gk = gk + g
            put(VB0, hh, dvb_ref[hh])
        upd = jnp.concatenate([gq, gk, jnp.zeros((6, HEAD_DIM), F32)], axis=0)

        @pl.when(i == 0)
        def _():
            gs_ref[...] = upd

        @pl.when(i != 0)
        def _():
            gs_ref[...] += upd

    tab = pl.BlockSpec((ROW_TILE, HEAD_DIM), lambda i: (i, 0))
    gain = pl.BlockSpec((1, HEAD_DIM), lambda i: (0, 0))
    lat = lambda nh: pl.BlockSpec((nh, ROW_TILE, HEAD_DIM), lambda i: (0, jnp.minimum(i, nx - 1), 0))
    full = lambda nh: pl.BlockSpec((nh, ROW_TILE, HEAD_DIM), lambda i: (0, i, 0))
    return pl.pallas_call(
        body, name="qkv_bwd_post", grid=(nt,),
        in_specs=[pl.BlockSpec((ROW_TILE, IN_WIDTH), lambda i: (i, 0)), tab, tab, tab, gain, gain,
                  lat(HEADS_A), full(KV_A), full(KV_A),
                  pl.BlockSpec((KV_B, None, HEAD_DIM, GROUP * ROW_TILE), lambda i: (0, jnp.minimum(i, nx - 1), 0, 0)),
                  full(KV_B), full(KV_B)],
        out_specs=[pl.BlockSpec((ROW_TILE, IN_WIDTH), lambda i: (i, 0)),
                   pl.BlockSpec((8, HEAD_DIM), lambda i: (0, 0))],
        out_shape=[jax.ShapeDtypeStruct((t, IN_WIDTH), BF16), jax.ShapeDtypeStruct((8, HEAD_DIM), F32)],
        compiler_params=_params(("arbitrary",)),
    )(h_all, cos, sin_a, sin_b, q_g, k_g, dqa, dka, dva, dqb, dkb, dvb)


GB_TQ = 256
GB_TK = 256


def _heads_rows(ref2d, tq):
    return jnp.concatenate([ref2d[:, hh * HEAD_DIM:(hh + 1) * HEAD_DIM] for hh in range(GROUP)], axis=0)


def _attn_b_fwd(qb, kb, vb, n):
    t = kb.shape[1]
    nk = t // GB_TK
    tq = GB_TQ
    rows = GROUP * tq

    def body(q_ref, k_ref, v_ref, o_ref, lse_ref, m_s, l_s, acc_s):
        q = q_ref[...].reshape(rows, HEAD_DIM)
        m_s[...] = jnp.full((1, rows), -jnp.inf, F32)
        l_s[...] = jnp.zeros((1, rows), F32)
        acc_s[...] = jnp.zeros((HEAD_DIM, rows), F32)

        def scores(j):
            start = pl.multiple_of(j * GB_TK, GB_TK)
            return lax.dot_general(k_ref[pl.ds(start, GB_TK), :], q, NT, preferred_element_type=F32)

        def step(j, st):
            st_next = scores(jnp.minimum(j + 1, nk - 1))
            vs = v_ref[pl.ds(pl.multiple_of(j * GB_TK, GB_TK), GB_TK), :]
            m_prev = m_s[...]
            m_new = jnp.maximum(m_prev, jnp.max(st, axis=0, keepdims=True))
            p = jnp.exp(st - m_new)
            alpha = jnp.exp(m_prev - m_new)
            l_s[...] = alpha * l_s[...] + jnp.sum(p, axis=0, keepdims=True)
            acc_s[...] = alpha * acc_s[...] + lax.dot_general(vs, p.astype(BF16), TN, preferred_element_type=F32)
            m_s[...] = m_new
            return st_next

        lax.fori_loop(0, nk, step, scores(0))
        ot = acc_s[...] * (1.0 / l_s[...])
        lse_ref[...] = m_s[...] + jnp.log(l_s[...])
        for hh in range(GROUP):
            o_ref[:, hh * HEAD_DIM:(hh + 1) * HEAD_DIM] = ot[:, hh * tq:(hh + 1) * tq].T.astype(BF16)

    return pl.pallas_call(
        body, name="attn_b_fwd", grid=(KV_B, n // tq),
        in_specs=[pl.BlockSpec((GROUP, tq, HEAD_DIM), lambda g, i: (g, i, 0)),
                  pl.BlockSpec((None, t, HEAD_DIM), lambda g, i: (g, 0, 0)),
                  pl.BlockSpec((None, t, HEAD_DIM), lambda g, i: (g, 0, 0))],
        out_specs=[pl.BlockSpec((tq, GROUP * HEAD_DIM), lambda g, i: (i, KV_A + g)),
                   pl.BlockSpec((None, None, 1, rows), lambda g, i: (g, i, 0, 0))],
        out_shape=[jax.ShapeDtypeStruct((n, MIX_WIDTH), BF16),
                   jax.ShapeDtypeStruct((KV_B, n // tq, 1, rows), F32)],
        scratch_shapes=[pltpu.VMEM((1, rows), F32), pltpu.VMEM((1, rows), F32), pltpu.VMEM((HEAD_DIM, rows), F32)],
        compiler_params=_params(("parallel", "parallel")),
    )(qb, kb, vb)


def _attn_b_bwd(qb, kb, vb, dheads, lse, delta, n):
    t = kb.shape[1]
    nk = t // GB_TK
    tq = GB_TQ
    nq = n // tq
    rows = GROUP * tq

    def body(q_ref, k_ref, v_ref, do_ref, lse_ref, dl_ref, dq_ref, dk_ref, dv_ref):
        j = pl.program_id(1)
        i = pl.program_id(2)

        q = q_ref[...].reshape(rows, HEAD_DIM)
        do = _heads_rows(do_ref, tq)
        ks, vs = k_ref[...], v_ref[...]
        st = lax.dot_general(ks, q, NT, preferred_element_type=F32)
        p = jnp.exp(st - lse_ref[...])
        dpt = lax.dot_general(vs, do, NT, preferred_element_type=F32)
        ds = (p * (dpt - dl_ref[...])).astype(BF16)
        dv_part = lax.dot_general(p.astype(BF16), do, NN, preferred_element_type=F32)
        dk_part = lax.dot_general(ds, q, NN, preferred_element_type=F32)
        dq_part = lax.dot_general(ks, ds, TN, preferred_element_type=F32)

        @pl.when(i == 0)
        def _():
            dk_ref[...] = dk_part
            dv_ref[...] = dv_part

        @pl.when(i != 0)
        def _():
            dk_ref[...] += dk_part
            dv_ref[...] += dv_part

        @pl.when(j == 0)
        def _():
            dq_ref[i] = dq_part

        @pl.when(j != 0)
        def _():
            dq_ref[i] += dq_part

    kv = pl.BlockSpec((None, GB_TK, HEAD_DIM), lambda g, j, i: (g, j, 0))
    row = pl.BlockSpec((None, None, 1, rows), lambda g, j, i: (g, i, 0, 0))
    return pl.pallas_call(
        body, name="attn_b_bwd", grid=(KV_B, nk, nq),
        in_specs=[pl.BlockSpec((GROUP, tq, HEAD_DIM), lambda g, j, i: (g, i, 0)), kv, kv,
                  pl.BlockSpec((tq, GROUP * HEAD_DIM), lambda g, j, i: (i, KV_A + g)), row, row],
        out_specs=[pl.BlockSpec((None, nq, HEAD_DIM, rows), lambda g, j, i: (g, 0, 0, 0)), kv, kv],
        out_shape=[jax.ShapeDtypeStruct((KV_B, nq, HEAD_DIM, rows), F32),
                   jax.ShapeDtypeStruct((KV_B, t, HEAD_DIM), F32),
                   jax.ShapeDtypeStruct((KV_B, t, HEAD_DIM), F32)],
        compiler_params=_params(("parallel", "arbitrary", "arbitrary")),
    )(qb, kb, vb, dheads, lse, delta)


def _delta_rows(dheads, heads):
    n = heads.shape[0]
    tq = GB_TQ
    w = GROUP * HEAD_DIM

    def body(a_ref, b_ref, o_ref):
        prod = a_ref[...].astype(F32) * b_ref[...].astype(F32)
        cols = [jnp.sum(prod[:, hh * HEAD_DIM:(hh + 1) * HEAD_DIM].T, axis=0, keepdims=True) for hh in range(GROUP)]
        o_ref[...] = jnp.concatenate(cols, axis=1)

    blk = pl.BlockSpec((tq, w), lambda g, i: (i, KV_A + g))
    return pl.pallas_call(
        body, name="delta_rows", grid=(KV_B, n // tq),
        in_specs=[blk, blk],
        out_specs=pl.BlockSpec((None, None, 1, GROUP * tq), lambda g, i: (g, i, 0, 0)),
        out_shape=jax.ShapeDtypeStruct((KV_B, n // tq, 1, GROUP * tq), F32),
        compiler_params=_params(("parallel", "parallel")),
    )(dheads, heads)


KWIN = 3 * BLOCK


def _window_scores(q, k_ref, j, n, nb, sink_row):
    c = k_ref.shape[0] - n
    start = pl.multiple_of(jnp.clip(j - 1, 0, nb - 3) * BLOCK, BLOCK)
    kw = k_ref[pl.ds(start, KWIN), :]
    kc = k_ref[pl.ds(n, c), :]
    s_loc = lax.dot_general(kw, q, NT, preferred_element_type=F32)
    s_ctx = lax.dot_general(kc, q, NT, preferred_element_type=F32)
    cols = GROUP * BLOCK
    qpos = j * BLOCK + lax.broadcasted_iota(jnp.int32, (KWIN, cols), 1) % BLOCK
    kpos = start + lax.broadcasted_iota(jnp.int32, (KWIN, cols), 0)
    s_loc = jnp.where(jnp.abs(qpos - kpos) <= WINDOW, s_loc, -jnp.inf)
    m = jnp.maximum(jnp.maximum(jnp.max(s_loc, axis=0, keepdims=True), jnp.max(s_ctx, axis=0, keepdims=True)),
                    sink_row)
    e_loc, e_ctx, e_sink = jnp.exp(s_loc - m), jnp.exp(s_ctx - m), jnp.exp(sink_row - m)
    inv = 1.0 / (jnp.sum(e_loc, axis=0, keepdims=True) + jnp.sum(e_ctx, axis=0, keepdims=True) + e_sink)
    return e_loc * inv, e_ctx * inv, e_sink * inv, start


def _sink_row(sink_ref, g):
    return jnp.concatenate([sink_ref[pl.ds(g * GROUP + hh, 1), :] for hh in range(GROUP)], axis=1)


def _attn_a_fwd(qa, ka, va, sink_b, heads_b, n):
    t = ka.shape[1]
    nb = n // BLOCK
    assert nb >= 3

    def body(q_ref, k_ref, v_ref, sink_ref, heads_b_ref, o_ref):
        g, j = pl.program_id(0), pl.program_id(1)
        q = q_ref[...].reshape(GROUP * BLOCK, HEAD_DIM)
        p_loc, p_ctx, _, start = _window_scores(q, k_ref, j, n, nb, _sink_row(sink_ref, g))
        vw = v_ref[pl.ds(start, KWIN), :]
        vc = v_ref[pl.ds(n, t - n), :]
        ot = (lax.dot_general(vw, p_loc.astype(BF16), TN, preferred_element_type=F32)
              + lax.dot_general(vc, p_ctx.astype(BF16), TN, preferred_element_type=F32))
        for hh in range(GROUP):
            o_ref[:, hh * HEAD_DIM:(hh + 1) * HEAD_DIM] = ot[:, hh * BLOCK:(hh + 1) * BLOCK].T.astype(BF16)

    return pl.pallas_call(
        body, name="attn_a_fwd", grid=(KV_A, nb),
        in_specs=[pl.BlockSpec((GROUP, BLOCK, HEAD_DIM), lambda g, j: (g, j, 0)),
                  pl.BlockSpec((None, t, HEAD_DIM), lambda g, j: (g, 0, 0)),
                  pl.BlockSpec((None, t, HEAD_DIM), lambda g, j: (g, 0, 0)),
                  pl.BlockSpec((HEADS_A, HEAD_DIM), lambda g, j: (0, 0)),
                  pl.BlockSpec(memory_space=pl.ANY)],
        out_specs=pl.BlockSpec((BLOCK, GROUP * HEAD_DIM), lambda g, j: (j, g)),
        out_shape=jax.ShapeDtypeStruct((n, MIX_WIDTH), BF16),
        input_output_aliases={4: 0},
        compiler_params=_params(("parallel", "parallel")),
    )(qa, ka, va, sink_b, heads_b)


def _attn_a_bwd(qa, ka, va, sink_b, dheads, n):
    t = ka.shape[1]
    c = t - n
    nb = n // BLOCK

    def body(q_ref, k_ref, v_ref, sink_ref, do_ref, dq_ref, dk_ref, dv_ref, dsink_ref):
        g, j = pl.program_id(0), pl.program_id(1)

        @pl.when(j == 0)
        def _():
            dk_ref[...] = jnp.zeros_like(dk_ref)
            dv_ref[...] = jnp.zeros_like(dv_ref)
            dsink_ref[...] = jnp.zeros_like(dsink_ref)

        q = q_ref[...].reshape(GROUP * BLOCK, HEAD_DIM)
        do = _heads_rows(do_ref, BLOCK)
        p_loc, p_ctx, p_sink, start = _window_scores(q, k_ref, j, n, nb, _sink_row(sink_ref, g))
        kw, vw = k_ref[pl.ds(start, KWIN), :], v_ref[pl.ds(start, KWIN), :]
        kc, vc = k_ref[pl.ds(n, c), :], v_ref[pl.ds(n, c), :]
        dp_loc = lax.dot_general(vw, do, NT, preferred_element_type=F32)
        dp_ctx = lax.dot_general(vc, do, NT, preferred_element_type=F32)
        dl = jnp.sum(p_loc * dp_loc, axis=0, keepdims=True) + jnp.sum(p_ctx * dp_ctx, axis=0, keepdims=True)
        ds_loc = (p_loc * (dp_loc - dl)).astype(BF16)
        ds_ctx = (p_ctx * (dp_ctx - dl)).astype(BF16)
        dqt = (lax.dot_general(kw, ds_loc, TN, preferred_element_type=F32)
               + lax.dot_general(kc, ds_ctx, TN, preferred_element_type=F32))
        for hh in range(GROUP):
            dq_ref[hh] = dqt[:, hh * BLOCK:(hh + 1) * BLOCK].T
        dk_ref[pl.ds(start, KWIN), :] += lax.dot_general(ds_loc, q, NN, preferred_element_type=F32)
        dv_ref[pl.ds(start, KWIN), :] += lax.dot_general(p_loc.astype(BF16), do, NN, preferred_element_type=F32)
        dk_ref[pl.ds(n, c), :] += lax.dot_general(ds_ctx, q, NN, preferred_element_type=F32)
        dv_ref[pl.ds(n, c), :] += lax.dot_general(p_ctx.astype(BF16), do, NN, preferred_element_type=F32)
        dsk = -(p_sink * dl)
        upd = [jnp.broadcast_to(jnp.sum(dsk[:, hh * BLOCK:(hh + 1) * BLOCK], axis=1, keepdims=True), (1, HEAD_DIM))
               for hh in range(GROUP)]
        dsink_ref[...] += jnp.concatenate(upd + [jnp.zeros((8 - GROUP, HEAD_DIM), F32)], axis=0)

    res = pl.BlockSpec((None, t, HEAD_DIM), lambda g, j: (g, 0, 0))
    return pl.pallas_call(
        body, name="attn_a_bwd", grid=(KV_A, nb),
        in_specs=[pl.BlockSpec((GROUP, BLOCK, HEAD_DIM), lambda g, j: (g, j, 0)), res, res,
                  pl.BlockSpec((HEADS_A, HEAD_DIM), lambda g, j: (0, 0)),
                  pl.BlockSpec((BLOCK, GROUP * HEAD_DIM), lambda g, j: (j, g))],
        out_specs=[pl.BlockSpec((GROUP, BLOCK, HEAD_DIM), lambda g, j: (g, j, 0)), res, res,
                   pl.BlockSpec((None, 8, HEAD_DIM), lambda g, j: (g, 0, 0))],
        out_shape=[jax.ShapeDtypeStruct((HEADS_A, n, HEAD_DIM), F32),
                   jax.ShapeDtypeStruct((KV_A, t, HEAD_DIM), F32),
                   jax.ShapeDtypeStruct((KV_A, t, HEAD_DIM), F32),
                   jax.ShapeDtypeStruct((KV_A, 8, HEAD_DIM), F32)],
        compiler_params=_params(("parallel", "arbitrary")),
    )(qa, ka, va, sink_b, dheads)


def _ln_stats(r):
    mu = jnp.mean(r, axis=-1, keepdims=True)
    cen = r - mu
    rstd = lax.rsqrt(jnp.mean(cen * cen, axis=-1, keepdims=True) + EPS)
    return cen * rstd, rstd


def _ln_bwd(dy, xhat, rstd, gain):
    dxh = dy * gain
    return rstd * (dxh - jnp.mean(dxh, axis=-1, keepdims=True)
                   - xhat * jnp.mean(dxh * xhat, axis=-1, keepdims=True))


def _accumulate_rows(ref, rows, i):
    pad = [jnp.zeros_like(rows[0])] * (8 - len(rows))
    upd = jnp.concatenate(rows + pad, axis=0)

    @pl.when(i == 0)
    def _():
        ref[...] = upd

    @pl.when(i != 0)
    def _():
        ref[...] += upd


def _colsum(v):
    return jnp.sum(v, axis=0, keepdims=True)


LN_TILE = 256


def _res_ln1(x, a, vec):
    n, d = x.shape

    def body(x_ref, a_ref, v_ref, xh_ref, rs_ref, u_ref):
        r1 = DN_ALPHA * x_ref[...] + v_ref[0:1, :] * a_ref[...]
        xhat, rstd = _ln_stats(r1)
        xh_ref[...] = xhat
        rs_ref[...] = rstd
        x1 = xhat * v_ref[1:2, :] + v_ref[2:3, :]
        u_ref[...] = (x1 * (1.0 + v_ref[3:4, :]) + v_ref[4:5, :]).astype(BF16)

    row = pl.BlockSpec((LN_TILE, d), lambda i: (i, 0))
    return pl.pallas_call(
        body, name="res_ln1", grid=(n // LN_TILE,),
        in_specs=[row, row, pl.BlockSpec((8, d), lambda i: (0, 0))],
        out_specs=[row, pl.BlockSpec((LN_TILE, 1), lambda i: (i, 0)), row],
        out_shape=[jax.ShapeDtypeStruct((n, d), F32), jax.ShapeDtypeStruct((n, 1), F32),
                   jax.ShapeDtypeStruct((n, d), BF16)],
        compiler_params=_params(("parallel",)),
    )(x, a, vec)


def _res_ln2_loss(xhat1, f, target, vec):
    n, d = f.shape

    def body(xh_ref, f_ref, t_ref, v_ref, dr_ref, df_ref, s_ref):
        i = pl.program_id(0)
        x1 = xh_ref[...] * v_ref[1:2, :] + v_ref[2:3, :]
        fv = f_ref[...]
        xhat, rstd = _ln_stats(DN_ALPHA * x1 + v_ref[0:1, :] * fv)
        err = xhat * v_ref[3:4, :] + v_ref[4:5, :] - t_ref[...]
        dy = err * (1.0 / d)
        dr2 = _ln_bwd(dy, xhat, rstd, v_ref[3:4, :])
        dr_ref[...] = dr2
        df_ref[...] = (dr2 * v_ref[0:1, :]).astype(BF16)
        _accumulate_rows(s_ref, [_colsum(dy * xhat), _colsum(dy), _colsum(dr2 * fv),
                                 _colsum(err * err) * (0.5 / d)], i)

    row = pl.BlockSpec((LN_TILE, d), lambda i: (i, 0))
    return pl.pallas_call(
        body, name="res_ln2_loss", grid=(n // LN_TILE,),
        in_specs=[row, row, row, pl.BlockSpec((8, d), lambda i: (0, 0))],
        out_specs=[row, row, pl.BlockSpec((8, d), lambda i: (0, 0))],
        out_shape=[jax.ShapeDtypeStruct((n, d), F32), jax.ShapeDtypeStruct((n, d), BF16),
                   jax.ShapeDtypeStruct((8, d), F32)],
        compiler_params=_params(("arbitrary",)),
    )(xhat1, f, target, vec)


def _ln1_bwd(du2, dr2, xhat1, rstd1, a, vec):
    n, d = du2.shape

    def body(du_ref, dr2_ref, xh_ref, rs_ref, a_ref, v_ref, dxp_ref, da_ref, s_ref):
        i = pl.program_id(0)
        du, xhat = du_ref[...], xh_ref[...]
        x1 = xhat * v_ref[1:2, :] + v_ref[2:3, :]
        dx1 = DN_ALPHA * dr2_ref[...] + du * (1.0 + v_ref[0:1, :])
        dr1 = _ln_bwd(dx1, xhat, rs_ref[...], v_ref[1:2, :])
        dxp_ref[...] = DN_ALPHA * dr1
        da_ref[...] = (dr1 * v_ref[3:4, :]).astype(BF16)
        _accumulate_rows(s_ref, [_colsum(du * x1), _colsum(du), _colsum(dx1 * xhat), _colsum(dx1),
                                 _colsum(dr1 * a_ref[...])], i)

    row = pl.BlockSpec((LN_TILE, d), lambda i: (i, 0))
    return pl.pallas_call(
        body, name="ln1_bwd", grid=(n // LN_TILE,),
        in_specs=[row, row, row, pl.BlockSpec((LN_TILE, 1), lambda i: (i, 0)), row,
                  pl.BlockSpec((8, d), lambda i: (0, 0))],
        out_specs=[row, row, pl.BlockSpec((8, d), lambda i: (0, 0))],
        out_shape=[jax.ShapeDtypeStruct((n, d), F32), jax.ShapeDtypeStruct((n, d), BF16),
                   jax.ShapeDtypeStruct((8, d), F32)],
        compiler_params=_params(("arbitrary",)),
    )(du2, dr2, xhat1, rstd1, a, vec)


def _mod1_bwd(du_all, dxp, x, ctx, mods):
    n, d = x.shape
    nx = n // ROW_TILE

    def body(du_ref, dxp_ref, x_ref, ctx_ref, m_ref, gx_ref, s_ref):
        i = pl.program_id(0)
        du = du_ref[...]
        zero = jnp.zeros((1, d), F32)

        @pl.when(i == 0)
        def _():
            s_ref[...] = jnp.zeros_like(s_ref)

        @pl.when(i < nx)
        def _():
            gx_ref[...] = dxp_ref[...] + du * (1.0 + m_ref[0:1, :])
            s_ref[...] += jnp.concatenate([_colsum(du * x_ref[...]), _colsum(du)] + [zero] * 6, axis=0)

        @pl.when(i >= nx)
        def _():
            s_ref[...] += jnp.concatenate([zero, zero, _colsum(du * ctx_ref[...]), _colsum(du)] + [zero] * 4, axis=0)

    lat = pl.BlockSpec((ROW_TILE, d), lambda i: (jnp.minimum(i, nx - 1), 0))
    return pl.pallas_call(
        body, name="mod1_bwd", grid=(nx + 1,),
        in_specs=[pl.BlockSpec((ROW_TILE, d), lambda i: (i, 0)), lat, lat,
                  pl.BlockSpec((ROW_TILE, d), lambda i: (0, 0)), pl.BlockSpec((8, d), lambda i: (0, 0))],
        out_specs=[lat, pl.BlockSpec((8, d), lambda i: (0, 0))],
        out_shape=[jax.ShapeDtypeStruct((n, d), F32), jax.ShapeDtypeStruct((8, d), F32)],
        compiler_params=_params(("arbitrary",)),
    )(du_all, dxp, x, ctx, mods)


FFN_TM = 1024
FFN_TN = 512


def _gate_up(u2, wg, wu, after):
    n, d = u2.shape
    f = wg.shape[1]

    def body(u_ref, wg_ref, wu_ref, after_ref, g_ref, up_ref, h_ref):
        u = u_ref[...]
        g = lax.dot_general(u, wg_ref[...], NN, preferred_element_type=F32)
        up = lax.dot_general(u, wu_ref[...], NN, preferred_element_type=F32)
        g_ref[...] = g.astype(BF16)
        up_ref[...] = up.astype(BF16)
        h_ref[...] = (g * jax.nn.sigmoid(g) * up).astype(BF16)

    tm = min(FFN_TM, n)
    wspec = pl.BlockSpec((d, FFN_TN), lambda j, i: (0, j))
    ospec = pl.BlockSpec((tm, FFN_TN), lambda j, i: (i, j))
    return pl.pallas_call(
        body, name="gate_up", grid=(f // FFN_TN, n // tm),
        in_specs=[pl.BlockSpec((tm, d), lambda j, i: (i, 0)), wspec, wspec, pl.BlockSpec(memory_space=pl.ANY)],
        out_specs=[ospec, ospec, ospec],
        out_shape=[jax.ShapeDtypeStruct((n, f), BF16)] * 3,
        compiler_params=_params(("parallel", "parallel")),
    )(u2, wg, wu, after)


def _glu_bwd(df, wd, g, u):
    n, d = df.shape
    f = wd.shape[0]

    def body(df_ref, wd_ref, g_ref, u_ref, dg_ref, du_ref):
        dh = lax.dot_general(df_ref[...], wd_ref[...], NT, preferred_element_type=F32)
        gv = g_ref[...].astype(F32)
        sig = jax.nn.sigmoid(gv)
        du_ref[...] = (dh * (gv * sig)).astype(BF16)
        dg_ref[...] = (dh * u_ref[...].astype(F32) * (sig * (1.0 + gv * (1.0 - sig)))).astype(BF16)

    tm = min(FFN_TM, n)
    ospec = pl.BlockSpec((tm, FFN_TN), lambda i, j: (i, j))
    return pl.pallas_call(
        body, name="glu_bwd", grid=(n // tm, f // FFN_TN),
        in_specs=[pl.BlockSpec((tm, d), lambda i, j: (i, 0)),
                  pl.BlockSpec((FFN_TN, d), lambda i, j: (j, 0)), ospec, ospec],
        out_specs=[ospec, ospec],
        out_shape=[jax.ShapeDtypeStruct((n, f), BF16), jax.ShapeDtypeStruct((n, f), BF16)],
        compiler_params=_params(("parallel", "parallel")),
    )(df, wd, g, u)


def _du2(dg, du, wg, wu):
    n, f = dg.shape
    d = wg.shape[0]
    tm, tn, tk = min(1024, n), 1024, 1408
    nk = f // tk

    def body(dg_ref, du_ref, wg_ref, wu_ref, o_ref, acc_ref):
        kk = pl.program_id(2)
        part = (lax.dot_general(dg_ref[...], wg_ref[...], NT, preferred_element_type=F32)
                + lax.dot_general(du_ref[...], wu_ref[...], NT, preferred_element_type=F32))

        @pl.when(kk == 0)
        def _():
            acc_ref[...] = part

        @pl.when(kk != 0)
        def _():
            acc_ref[...] += part

        @pl.when(kk == nk - 1)
        def _():
            o_ref[...] = acc_ref[...]

    aspec = pl.BlockSpec((tm, tk), lambda i, j, kk: (i, kk))
    wspec = pl.BlockSpec((tn, tk), lambda i, j, kk: (j, kk))
    return pl.pallas_call(
        body, name="du2", grid=(n // tm, d // tn, nk),
        in_specs=[aspec, aspec, wspec, wspec],
        out_specs=pl.BlockSpec((tm, tn), lambda i, j, kk: (i, j)),
        out_shape=jax.ShapeDtypeStruct((n, d), F32),
        scratch_shapes=[pltpu.VMEM((tm, tn), F32)],
        compiler_params=_params(("parallel", "parallel", "arbitrary")),
    )(dg, du, wg, wu)


def _rows8(rows, d=D_MODEL):
    rows = [r.reshape(1, d).astype(F32) for r in rows]
    return jnp.concatenate(rows + [jnp.zeros((8 - len(rows), d), F32)], axis=0)


W_GROUPS = (("w_in",), ("w_out", "w_gate", "w_up"), ("w_down",))
G_GROUPS = (("w_down", "w_gate", "w_up"), ("w_out",), ("w_in",))


def _layer_fwd_bwd(x, ctx, target, mod, mod_ctx, weights, prefetch, grads_out,
                   q_g, k_g, sink, ln1_g, ln1_b, ln2_g, ln2_b):
    n, d = x.shape
    c = ctx.shape[0]
    sh1, sc1, g1, sh2, sc2, g2 = [mod[:, k * d:(k + 1) * d] for k in range(6)]
    csh1, csc1 = mod_ctx[:, 0:d], mod_ctx[:, d:2 * d]
    cos, sin_a, sin_b = _rope_tables(n, c)
    sink_b = jnp.broadcast_to(sink.reshape(HEADS_A, 1), (HEADS_A, HEAD_DIM)).astype(F32)

    u_all = _modulate_rows(x, ctx, _rows8([sc1, sh1, csc1, csh1]))
    (w_in,) = weights(0, u_all)
    h_all = _matmul(u_all, w_in, name="qkv_proj", tm=_fit(n + c, 1088), tn=1024, tk=2048, out_dtype=F32)
    qa, ka, va, qb, kb, vb = _qkv_post(h_all, cos, sin_a, sin_b, q_g, k_g)
    heads_b, lse = _attn_b_fwd(qb, kb, vb, n)
    zero = prefetch(1, heads_b)
    heads = _attn_a_fwd(qa, ka, va, sink_b + zero, heads_b, n)
    w_out, w_gate, w_up = weights(1, heads)
    a = _matmul(heads, w_out, name="out_proj", tm=1024, tn=1024, tk=2048, out_dtype=F32)
    xhat1, rstd1, u2 = _res_ln1(x, a, _rows8([g1, ln1_g, ln1_b, sc2, sh2]))
    zero = prefetch(2, u2)
    gg, uu, hh = _gate_up(u2, w_gate, w_up, zero.reshape(1, 1))
    (w_down,) = weights(2, hh)
    f = _matmul(hh, w_down, name="ffn_down", tm=512, tn=512, tk=FFN, out_dtype=F32)
    dr2, df, s_ln2 = _res_ln2_loss(xhat1, f, target, _rows8([g2, ln1_g, ln1_b, ln2_g, ln2_b]))

    dgg, duu = _glu_bwd(df, w_down, gg, uu)
    dw_down = _matmul(hh, df, name="dw_down", ta=True, tm=512, tn=1024, tk=n, out_dtype=BF16)
    dw_gate = _matmul(u2, dgg, name="dw_gate", ta=True, tm=1024, tn=512, tk=n, out_dtype=BF16)
    dw_up = _matmul(u2, duu, name="dw_up", ta=True, tm=1024, tn=512, tk=n, out_dtype=BF16)
    zero = grads_out(0, [dw_down, dw_gate, dw_up])
    du2 = _du2(dgg, duu, w_gate, w_up)
    dxp, da, s_ln1 = _ln1_bwd(du2, dr2, xhat1, rstd1, a, _rows8([sc2, ln1_g, ln1_b, g1]) + zero)

    dheads = _matmul(da, w_out, name="d_heads", tb=True, tm=1024, tn=1024, tk=2048, out_dtype=BF16)
    dw_out = _matmul(heads, da, name="dw_out", ta=True, tm=1024, tn=1024, tk=n, out_dtype=BF16)
    zero = grads_out(1, [dw_out])
    delta = _delta_rows(dheads, heads)
    dqa, dka, dva, dsink = _attn_a_bwd(qa, ka, va, sink_b + zero, dheads, n)
    dqb, dkb, dvb = _attn_b_bwd(qb, kb, vb, dheads, lse, delta, n)
    dh_all, s_gain = _qkv_bwd_post(h_all, cos, sin_a, sin_b, q_g, k_g, dqa, dka, dva, dqb, dkb, dvb, n)
    dw_in = _matmul(u_all, dh_all, name="dw_in", ta=True, tm=1024, tn=1024, tk=n + c, out_dtype=BF16)
    zero = grads_out(2, [dw_in])
    du_all = _matmul(dh_all, w_in, name="d_u1", tb=True, tm=_fit(n + c, 1088), tn=1024, tk=IN_WIDTH, out_dtype=F32,
                     after=zero.reshape(1, 1))
    grad_x, s_mod1 = _mod1_bwd(du_all, dxp, x, ctx, _rows8([sc1]) + zero)

    dsink_row = jnp.concatenate([dsink[0, 0:GROUP, 0], dsink[1, 0:GROUP, 0]]).reshape(1, HEADS_A)
    misc = jnp.concatenate([s_gain[0:1], s_gain[1:2], dsink_row,
                            jnp.zeros((1, d - 2 * HEAD_DIM - HEADS_A), F32)], axis=1)
    partial = jnp.concatenate([
        s_mod1[1:2], s_mod1[0:1], s_ln1[4:5],
        s_ln1[1:2], s_ln1[0:1], s_ln2[2:3],
        s_mod1[3:4], s_mod1[2:3],
        s_ln1[2:3], s_ln1[3:4], s_ln2[0:1], s_ln2[1:2],
        s_ln2[3:4], misc, jnp.zeros((2, d), F32)], axis=0)
    return grad_x, partial


ANY = pl.BlockSpec(memory_space=pl.ANY)
VMEM_FULL = pl.BlockSpec(memory_space=pltpu.VMEM)
N_CHIP_PEERS = 3


def _me():
    return lax.axis_index("x"), lax.axis_index("y"), lax.axis_index("c")


def _other_chips(x, y):
    return [(1 - x, y), (x, 1 - y), (1 - x, 1 - y)]


def _shard_of(chip):
    return 2 * chip[0] + chip[1]


def _dev_index(x, y, c):
    return 4 * x + 2 * y + c


def _rcopy(src, dst, send_sems, recv_sems, k, dev):
    return pltpu.make_async_remote_copy(src_ref=src, dst_ref=dst, send_sem=send_sems.at[k], recv_sem=recv_sems.at[k],
                                        device_id=dev, device_id_type=MESH)


BIG = (("w_in", (D_MODEL, IN_WIDTH), 1), ("w_out", (MIX_WIDTH, D_MODEL), 0), ("w_gate", (D_MODEL, FFN), 1),
       ("w_up", (D_MODEL, FFN), 1), ("w_down", (FFN, D_MODEL), 0))


def _sub(ref, axis, idx, size):
    start = pl.multiple_of(idx * size, size)
    return ref.at[pl.ds(start, size), :] if axis == 0 else ref.at[:, pl.ds(start, size)]


def _shape_div(shape, axis, parts):
    return tuple(s // parts if a == axis else s for a, s in enumerate(shape))


def _piece(a, ref, shard, half):
    _, full, axis = BIG[a]
    view = _sub(ref, axis, shard, full[axis] // N_SHARD)
    return _sub(view, 1 - axis, half, full[1 - axis] // 2)


HBM = pl.BlockSpec(memory_space=pltpu.HBM)
SEM = pl.BlockSpec(memory_space=pltpu.SEMAPHORE)
EFFECT = pltpu.SideEffectType.DATAFLOW_SIDE_EFFECTING
BIG_INDEX = {name: a for a, (name, _, _) in enumerate(BIG)}


def _in_hbm(arr):
    return pltpu.with_memory_space_constraint(arr, pltpu.HBM)


def _gather_start(tag, arrs, bufs, prev):
    n_arr = len(arrs)

    def body(*refs):
        ins = refs[:n_arr]
        send_sems, recv_sems = refs[n_arr + 1], refs[n_arr + 2]
        token = refs[-1]
        x, y, c = _me()
        s_me = _shard_of((x, y))
        for i, a in enumerate(arrs):
            mine = _piece(a, ins[i], s_me, c)
            for j, chip in enumerate(_other_chips(x, y)):
                _rcopy(mine, mine, send_sems, recv_sems, N_CHIP_PEERS * i + j, (*chip, c)).start()
        token[...] = jnp.zeros_like(token)

    n_sem = N_CHIP_PEERS * n_arr
    outs = pl.pallas_call(
        body, name="gather_start_" + tag,
        in_specs=[HBM] * n_arr + [ANY],
        out_specs=[SEM, SEM] + [HBM] * n_arr + [VMEM_FULL],
        out_shape=[pltpu.SemaphoreType.DMA((n_sem,)), pltpu.SemaphoreType.DMA((n_sem,))]
        + [pltpu.HBM(BIG[a][1], BF16) for a in arrs] + [jax.ShapeDtypeStruct((8, HEAD_DIM), F32)],
        input_output_aliases={i: 2 + i for i in range(n_arr)},
        compiler_params=pltpu.CompilerParams(has_side_effects=EFFECT),
    )(*[_in_hbm(b) for b in bufs], prev)
    return outs[0], outs[1], list(outs[2:2 + n_arr]), outs[-1]


def _gather_wait(tag, arrs, send_sems, recv_sems, bufs, after):
    n_arr = len(arrs)

    def body(*refs):
        ins = refs[:n_arr]
        send_sems_, recv_sems_ = refs[n_arr], refs[n_arr + 1]
        x, y, c = _me()
        s_me = _shard_of((x, y))
        for i, a in enumerate(arrs):
            mine = _piece(a, ins[i], s_me, c)
            for j, chip in enumerate(_other_chips(x, y)):
                landed = _piece(a, ins[i], _shard_of(chip), c)
                cp = _rcopy(mine, landed, send_sems_, recv_sems_, N_CHIP_PEERS * i + j, (*chip, c))
                cp.wait_send()
                cp.wait_recv()

    outs = pl.pallas_call(
        body, name="gather_wait_" + tag,
        in_specs=[HBM] * n_arr + [SEM, SEM, ANY],
        out_specs=[HBM] * n_arr,
        out_shape=[pltpu.HBM(BIG[a][1], BF16) for a in arrs],
        input_output_aliases={i: i for i in range(n_arr)},
        compiler_params=pltpu.CompilerParams(has_side_effects=EFFECT),
    )(*bufs, send_sems, recv_sems, after)
    return list(outs)


def _gather_forward(tag, arrs, bufs):
    n_arr = len(arrs)

    def body(*refs):
        outs = refs[n_arr:2 * n_arr]
        send_sems, recv_sems = refs[2 * n_arr:]
        x, y, c = _me()
        sibling = (x, y, 1 - c)
        chips = _other_chips(x, y)
        copies = []
        for i, a in enumerate(arrs):
            for j, chip in enumerate(chips):
                landed = _piece(a, outs[i], _shard_of(chip), c)
                cp = _rcopy(landed, landed, send_sems, recv_sems, N_CHIP_PEERS * i + j, sibling)
                cp.start()
                copies.append(cp)
        for i, a in enumerate(arrs):
            for j, chip in enumerate(chips):
                other = _piece(a, outs[i], _shard_of(chip), 1 - c)
                _rcopy(other, other, send_sems, recv_sems, N_CHIP_PEERS * i + j, sibling).wait_recv()
        for cp in copies:
            cp.wait_send()

    n_sem = N_CHIP_PEERS * n_arr
    return list(pl.pallas_call(
        body, name="gather_forward_" + tag,
        in_specs=[ANY] * n_arr, out_specs=[ANY] * n_arr,
        out_shape=[jax.ShapeDtypeStruct(BIG[a][1], BF16) for a in arrs],
        input_output_aliases={i: i for i in range(n_arr)},
        scratch_shapes=[pltpu.SemaphoreType.DMA((n_sem,)), pltpu.SemaphoreType.DMA((n_sem,))],
    )(*bufs))


def _forward_start(tag, arrs, bufs):
    n_arr = len(arrs)

    def body(*refs):
        ins = refs[:n_arr]
        send_sems, recv_sems = refs[n_arr], refs[n_arr + 1]
        token = refs[-1]
        x, y, c = _me()
        for i, a in enumerate(arrs):
            for j, chip in enumerate(_other_chips(x, y)):
                landed = _piece(a, ins[i], _shard_of(chip), c)
                _rcopy(landed, landed, send_sems, recv_sems, N_CHIP_PEERS * i + j, (x, y, 1 - c)).start()
        token[...] = jnp.zeros_like(token)

    n_sem = N_CHIP_PEERS * n_arr
    outs = pl.pallas_call(
        body, name="gather_forward_start_" + tag,
        in_specs=[HBM] * n_arr,
        out_specs=[SEM, SEM] + [HBM] * n_arr + [VMEM_FULL],
        out_shape=[pltpu.SemaphoreType.DMA((n_sem,)), pltpu.SemaphoreType.DMA((n_sem,))]
        + [pltpu.HBM(BIG[a][1], BF16) for a in arrs] + [jax.ShapeDtypeStruct((8, HEAD_DIM), F32)],
        input_output_aliases={i: 2 + i for i in range(n_arr)},
        compiler_params=pltpu.CompilerParams(has_side_effects=EFFECT),
    )(*bufs)
    return outs[0], outs[1], list(outs[2:2 + n_arr]), outs[-1]


def _forward_wait(tag, arrs, send_sems, recv_sems, bufs, after):
    n_arr = len(arrs)

    def body(*refs):
        ins = refs[:n_arr]
        send_sems_, recv_sems_ = refs[n_arr], refs[n_arr + 1]
        x, y, c = _me()
        for i, a in enumerate(arrs):
            for j, chip in enumerate(_other_chips(x, y)):
                mine = _piece(a, ins[i], _shard_of(chip), c)
                other = _piece(a, ins[i], _shard_of(chip), 1 - c)
                cp = _rcopy(mine, other, send_sems_, recv_sems_, N_CHIP_PEERS * i + j, (x, y, 1 - c))
                cp.wait_send()
                cp.wait_recv()

    outs = pl.pallas_call(
        body, name="gather_forward_wait_" + tag,
        in_specs=[HBM] * n_arr + [SEM, SEM, ANY],
        out_specs=[HBM] * n_arr,
        out_shape=[pltpu.HBM(BIG[a][1], BF16) for a in arrs],
        input_output_aliases={i: i for i in range(n_arr)},
        compiler_params=pltpu.CompilerParams(has_side_effects=EFFECT),
    )(*bufs, send_sems, recv_sems, after)
    return list(outs)


def _peers(x, y, c):
    return [(x ^ (mask >> 2), y ^ ((mask >> 1) & 1), c ^ (mask & 1)) for mask in range(1, N_DEV)]


def _received_shape(a):
    _, full, axis = BIG[a]
    return (N_DEV - 1,) + _shape_div(_shape_div(full, 1 - axis, 2), axis, N_SHARD)


def _pieces_start(tag, arrs, dws):
    n_arr = len(arrs)

    def body(*refs):
        srcs, lands = refs[:n_arr], refs[n_arr:2 * n_arr]
        send_sems, recv_sems = refs[2 * n_arr], refs[2 * n_arr + 1]
        token = refs[-1]
        x, y, c = _me()
        for i, a in enumerate(arrs):
            for k, peer in enumerate(_peers(x, y, c)):
                src = _piece(a, srcs[i], _shard_of(peer[:2]), peer[2])
                _rcopy(src, lands[i].at[k], send_sems, recv_sems, (N_DEV - 1) * i + k, peer).start()
        token[...] = jnp.zeros_like(token)

    n_sem = (N_DEV - 1) * n_arr
    lands = [_in_hbm(lax.empty(_received_shape(a), BF16)) for a in arrs]
    outs = pl.pallas_call(
        body, name="grad_pieces_start_" + tag,
        in_specs=[HBM] * (2 * n_arr),
        out_specs=[SEM, SEM] + [HBM] * (2 * n_arr) + [VMEM_FULL],
        out_shape=[pltpu.SemaphoreType.DMA((n_sem,)), pltpu.SemaphoreType.DMA((n_sem,))]
        + [pltpu.HBM(BIG[a][1], BF16) for a in arrs] + [pltpu.HBM(_received_shape(a), BF16) for a in arrs]
        + [jax.ShapeDtypeStruct((8, HEAD_DIM), F32)],
        input_output_aliases={i: 2 + i for i in range(2 * n_arr)},
        compiler_params=pltpu.CompilerParams(has_side_effects=EFFECT),
    )(*[_in_hbm(dw) for dw in dws], *lands)
    return outs[0], outs[1], list(outs[2:2 + n_arr]), list(outs[2 + n_arr:2 + 2 * n_arr]), outs[-1]


def _pieces_wait(tag, arrs, send_sems, recv_sems, dws, lands, after):
    n_arr = len(arrs)

    def body(*refs):
        srcs, lands_ = refs[:n_arr], refs[n_arr:2 * n_arr]
        send_sems_, recv_sems_ = refs[2 * n_arr], refs[2 * n_arr + 1]
        x, y, c = _me()
        for i, a in enumerate(arrs):
            for k, peer in enumerate(_peers(x, y, c)):
                src = _piece(a, srcs[i], _shard_of(peer[:2]), peer[2])
                cp = _rcopy(src, lands_[i].at[k], send_sems_, recv_sems_, (N_DEV - 1) * i + k, peer)
                cp.wait_send()
                cp.wait_recv()

    outs = pl.pallas_call(
        body, name="grad_pieces_wait_" + tag,
        in_specs=[HBM] * (2 * n_arr) + [SEM, SEM, ANY],
        out_specs=[HBM] * (2 * n_arr),
        out_shape=[pltpu.HBM(BIG[a][1], BF16) for a in arrs] + [pltpu.HBM(_received_shape(a), BF16) for a in arrs],
        input_output_aliases={i: i for i in range(2 * n_arr)},
        compiler_params=pltpu.CompilerParams(has_side_effects=EFFECT),
    )(*dws, *lands, send_sems, recv_sems, after)
    return list(outs[:n_arr]), list(outs[n_arr:])


def _join_start(tag, g_halves):
    n_arr = len(g_halves)

    def body(*refs):
        srcs, lands = refs[:n_arr], refs[n_arr:2 * n_arr]
        send_sems, recv_sems = refs[2 * n_arr], refs[2 * n_arr + 1]
        token = refs[-1]
        x, y, c = _me()
        for i in range(n_arr):
            _rcopy(srcs[i], lands[i], send_sems, recv_sems, i, (x, y, 1 - c)).start()
        token[...] = jnp.zeros_like(token)

    shapes = [pltpu.HBM(g.shape, F32) for g in g_halves]
    outs = pl.pallas_call(
        body, name="grad_join_start_" + tag,
        in_specs=[HBM] * (2 * n_arr),
        out_specs=[SEM, SEM] + [HBM] * (2 * n_arr) + [VMEM_FULL],
        out_shape=[pltpu.SemaphoreType.DMA((n_arr,)), pltpu.SemaphoreType.DMA((n_arr,))] + shapes + shapes
        + [jax.ShapeDtypeStruct((8, HEAD_DIM), F32)],
        input_output_aliases={i: 2 + i for i in range(2 * n_arr)},
        compiler_params=pltpu.CompilerParams(has_side_effects=EFFECT),
    )(*[_in_hbm(g) for g in g_halves], *[_in_hbm(lax.empty(g.shape, F32)) for g in g_halves])
    return outs[0], outs[1], list(outs[2:2 + n_arr]), list(outs[2 + n_arr:2 + 2 * n_arr]), outs[-1]


def _join_wait(tag, send_sems, recv_sems, g_halves, lands, after):
    n_arr = len(g_halves)

    def body(*refs):
        srcs, lands_ = refs[:n_arr], refs[n_arr:2 * n_arr]
        send_sems_, recv_sems_ = refs[2 * n_arr], refs[2 * n_arr + 1]
        x, y, c = _me()
        for i in range(n_arr):
            cp = _rcopy(srcs[i], lands_[i], send_sems_, recv_sems_, i, (x, y, 1 - c))
            cp.wait_send()
            cp.wait_recv()

    shapes = [pltpu.HBM(g.shape, F32) for g in g_halves]
    outs = pl.pallas_call(
        body, name="grad_join_wait_" + tag,
        in_specs=[HBM] * (2 * n_arr) + [SEM, SEM, ANY],
        out_specs=[HBM] * (2 * n_arr),
        out_shape=shapes + shapes,
        input_output_aliases={i: i for i in range(2 * n_arr)},
        compiler_params=pltpu.CompilerParams(has_side_effects=EFFECT),
    )(*g_halves, *lands, send_sems, recv_sems, after)
    return list(outs[:n_arr]), list(outs[n_arr:])


def _piece_sum(a, dw, shard, core, received):
    name, full, axis = BIG[a]
    rows, cols = _received_shape(a)[1:]
    tr = _fit(rows, ROW_TILE)
    nbr = rows // tr

    def body(w_ref, dw_ref, rec_ref, o_ref):
        acc = dw_ref[...].astype(F32)
        for k in range(N_DEV - 1):
            acc = acc + rec_ref[k].astype(F32)
        o_ref[...] = acc

    if axis == 0:
        own = pl.BlockSpec((tr, cols), lambda i, w: (w[0] * nbr + i, w[1]))
    else:
        own = pl.BlockSpec((tr, cols), lambda i, w: (w[1] * nbr + i, w[0]))
    return pl.pallas_call(
        body, name="grad_sum_pieces_" + name,
        grid_spec=pltpu.PrefetchScalarGridSpec(
            num_scalar_prefetch=1, grid=(nbr,),
            in_specs=[own, pl.BlockSpec((N_DEV - 1, tr, cols), lambda i, w: (0, i, 0))],
            out_specs=pl.BlockSpec((tr, cols), lambda i, w: (i, 0))),
        out_shape=jax.ShapeDtypeStruct((rows, cols), F32),
        compiler_params=_params(("parallel",)),
    )(jnp.stack([shard, core]).astype(jnp.int32), dw, received)


def _scatter_begin(tag, arrs, dws):
    send_sems, recv_sems, dws, lands, token = _pieces_start(tag, arrs, dws)
    return (send_sems, recv_sems, dws, lands), token[0, 0]


def _scatter_reduce(tag, arrs, state, after):
    x, y, c = _me()
    send_sems, recv_sems, dws, lands = state
    dws, lands = _pieces_wait(tag, arrs, send_sems, recv_sems, dws, lands, after)
    g_own = [_piece_sum(a, dw, _shard_of((x, y)), c, r) for a, dw, r in zip(arrs, dws, lands)]
    send_sems, recv_sems, g_own, lands, token = _join_start(tag, g_own)
    return (send_sems, recv_sems, g_own, lands), token


def _scatter_end(tag, state, after):
    return _join_wait(tag, *state, after)


def _gather_rows(block, name, after):
    r, d = block.shape

    def body(in_ref, after_ref, out_ref, send_sems, recv_sems):
        x, y, c = _me()
        out_ref[_dev_index(x, y, c)] = in_ref[...]
        copies = []
        for mask in range(1, N_DEV):
            peer = (x ^ (mask >> 2), y ^ ((mask >> 1) & 1), c ^ (mask & 1))
            cp = _rcopy(in_ref, out_ref.at[_dev_index(x, y, c)], send_sems, recv_sems, mask - 1, peer)
            cp.start()
            copies.append((cp, peer))
        for mask in range(1, N_DEV):
            peer = (x ^ (mask >> 2), y ^ ((mask >> 1) & 1), c ^ (mask & 1))
            landed = out_ref.at[_dev_index(*peer)]
            _rcopy(landed, landed, send_sems, recv_sems, mask - 1, peer).wait_recv()
        for cp, _ in copies:
            cp.wait_send()

    return pl.pallas_call(
        body, name=name, in_specs=[VMEM_FULL, ANY], out_specs=VMEM_FULL,
        out_shape=jax.ShapeDtypeStruct((N_DEV, r, d), F32),
        scratch_shapes=[pltpu.SemaphoreType.DMA((N_DEV - 1,)), pltpu.SemaphoreType.DMA((N_DEV - 1,))],
    )(block, after)


ADA_ROWS = 80
ADA_W = 6 * D_MODEL // N_SHARD


def _ada_forward(c_block, cctx_block, w_ada, b_shard):
    d = c_block.shape[1]

    def body(c_ref, cc_ref, w_ref, b_ref, act_ref, mods_ref, raw, mloc, send_sems, recv_sems):
        x, y, c = _me()
        me = _dev_index(x, y, c)
        s_me = _shard_of((x, y))
        raw[72:ADA_ROWS, :] = jnp.zeros((ADA_ROWS - 72, d), F32)
        raw[pl.ds(pl.multiple_of(me * 8, 8), 8), :] = c_ref[...]
        raw[64:72, :] = cc_ref[...]
        sends = []
        for mask in range(1, N_DEV):
            peer = (x ^ (mask >> 2), y ^ ((mask >> 1) & 1), c ^ (mask & 1))
            cp = _rcopy(c_ref, raw.at[pl.ds(pl.multiple_of(me * 8, 8), 8), :], send_sems, recv_sems, mask - 1, peer)
            cp.start()
            sends.append(cp)
        for mask in range(1, N_DEV):
            peer = (x ^ (mask >> 2), y ^ ((mask >> 1) & 1), c ^ (mask & 1))
            landed = raw.at[pl.ds(pl.multiple_of(_dev_index(*peer) * 8, 8), 8), :]
            _rcopy(landed, landed, send_sems, recv_sems, mask - 1, peer).wait_recv()
        v = raw[...]
        act = v * jax.nn.sigmoid(v)
        act_ref[...] = act
        mloc[...] = lax.dot_general(act.astype(BF16), w_ref[...].astype(BF16), NN,
                                    preferred_element_type=F32) + b_ref[...]
        mods_ref[s_me, 0:8, :] = mloc[pl.ds(pl.multiple_of(me * 8, 8), 8), :]
        mods_ref[s_me, 8:16, :] = mloc[64:72, :]
        base = N_DEV - 1
        for j, chip in enumerate(_other_chips(x, y)):
            peer = (*chip, c)
            rows = mloc.at[pl.ds(pl.multiple_of(_dev_index(*peer) * 8, 8), 8), :]
            cp = _rcopy(rows, mods_ref.at[s_me, 0:8, :], send_sems, recv_sems, base + 2 * j, peer)
            cp.start()
            sends.append(cp)
            cp = _rcopy(mloc.at[64:72, :], mods_ref.at[s_me, 8:16, :], send_sems, recv_sems, base + 2 * j + 1, peer)
            cp.start()
            sends.append(cp)
        for j, chip in enumerate(_other_chips(x, y)):
            for part in range(2):
                landed = mods_ref.at[_shard_of(chip), 8 * part:8 * part + 8, :]
                _rcopy(landed, landed, send_sems, recv_sems, base + 2 * j + part, (*chip, c)).wait_recv()
        for cp in sends:
            cp.wait_send()

    n_sem = N_DEV - 1 + 2 * N_CHIP_PEERS
    return pl.pallas_call(
        body, name="ada_forward",
        in_specs=[VMEM_FULL] * 4, out_specs=[VMEM_FULL, VMEM_FULL],
        out_shape=[jax.ShapeDtypeStruct((ADA_ROWS, d), F32), jax.ShapeDtypeStruct((N_SHARD, 16, ADA_W), F32)],
        scratch_shapes=[pltpu.VMEM((ADA_ROWS, d), F32), pltpu.VMEM((ADA_ROWS, ADA_W), F32),
                        pltpu.SemaphoreType.DMA((n_sem,)), pltpu.SemaphoreType.DMA((n_sem,))],
        compiler_params=pltpu.CompilerParams(vmem_limit_bytes=VMEM_LIMIT),
    )(c_block, cctx_block, w_ada, b_shard)


def _small_reduce(gathered):
    d = gathered.shape[2]

    def body(g_ref, o_ref):
        tot = g_ref[0]
        for i in range(1, N_DEV):
            tot = tot + g_ref[i]
        o_ref[...] = tot
        o_ref[0:2, :] = tot[0:2] + tot[6:8]
        o_ref[12:13, :] = jnp.broadcast_to(jnp.sum(tot[12:13], axis=1, keepdims=True), (1, d))

    return pl.pallas_call(body, name="small_reduce", in_specs=[VMEM_FULL], out_specs=VMEM_FULL,
                          out_shape=jax.ShapeDtypeStruct((16, d), F32))(gathered)


def _cctx_grad(gathered, c_ctx):
    d = gathered.shape[2]

    def body(g_ref, c_ref, o_ref):
        tot = g_ref[0, 0:1, :]
        for chip in range(1, N_SHARD):
            tot = tot + g_ref[2 * chip, 0:1, :]
        v = c_ref[...]
        sig = jax.nn.sigmoid(v)
        o_ref[...] = tot * (sig * (1.0 + v * (1.0 - sig)))

    return pl.pallas_call(body, name="cctx_grad", in_specs=[VMEM_FULL, VMEM_FULL], out_specs=VMEM_FULL,
                          out_shape=jax.ShapeDtypeStruct((1, d), F32))(gathered, c_ctx.reshape(1, d))


def _cast_into_full(w, shard, full, axis, name):
    r, cdim = w.shape
    tr = _fit(r, ROW_TILE)
    nbr = r // tr

    def body(s_ref, w_ref, o_ref):
        o_ref[...] = w_ref[...].astype(BF16)

    if axis == 0:
        out_spec = pl.BlockSpec((tr, cdim), lambda i, s: (s[0] * nbr + i, 0))
    else:
        out_spec = pl.BlockSpec((tr, cdim), lambda i, s: (i, s[0]))
    return pl.pallas_call(
        body, name=name,
        grid_spec=pltpu.PrefetchScalarGridSpec(
            num_scalar_prefetch=1, grid=(nbr,), in_specs=[pl.BlockSpec((tr, cdim), lambda i, s: (i, 0))],
            out_specs=out_spec),
        out_shape=jax.ShapeDtypeStruct(full, BF16), compiler_params=_params(("parallel",)),
    )(shard.reshape(1).astype(jnp.int32), w)


def _adamw_halves(w, g_own, g_other, m, v, core, axis, name):
    r, cdim = w.shape
    hr, hc = (r // 2, cdim) if axis == 1 else (r, cdim // 2)
    assert g_own.shape == (hr, hc) and g_other.shape == (hr, hc)
    tr = _fit(hr, 128)
    nb = hr // tr
    c1 = 1.0 - ADAM_B1 ** ADAM_STEP
    c2 = 1.0 - ADAM_B2 ** ADAM_STEP

    def body(c_ref, w_ref, go_ref, gt_ref, m_ref, v_ref, g_ref, d_ref, nm_ref, nv_ref):
        gv = jnp.where(pl.program_id(0) == c_ref[0], go_ref[...], gt_ref[...])
        nm = ADAM_B1 * m_ref[...] + (1.0 - ADAM_B1) * gv
        nv = ADAM_B2 * v_ref[...] + (1.0 - ADAM_B2) * (gv * gv)
        g_ref[...] = gv
        nm_ref[...] = nm
        nv_ref[...] = nv
        d_ref[...] = -ADAM_LR * ((nm / c1) / (jnp.sqrt(nv / c2) + ADAM_EPS) + ADAM_WD * w_ref[...])

    if axis == 1:
        big = pl.BlockSpec((tr, hc), lambda p, i, c: (p * nb + i, 0))
    else:
        big = pl.BlockSpec((tr, hc), lambda p, i, c: (i, p))
    half = pl.BlockSpec((tr, hc), lambda p, i, c: (i, 0))
    sh = jax.ShapeDtypeStruct((r, cdim), F32)
    return pl.pallas_call(
        body, name=name,
        grid_spec=pltpu.PrefetchScalarGridSpec(
            num_scalar_prefetch=1, grid=(2, nb), in_specs=[big, half, half, big, big], out_specs=[big] * 4),
        out_shape=[sh] * 4, compiler_params=_params(("parallel", "parallel")),
    )(core.reshape(1).astype(jnp.int32), w, g_own, g_other, m, v)


def _adamw(w, g, m, v, name):
    r, cdim = w.shape
    tr = _fit(r, 128) if r % (ROW_TILE // 4) == 0 else r
    c1 = 1.0 - ADAM_B1 ** ADAM_STEP
    c2 = 1.0 - ADAM_B2 ** ADAM_STEP

    def body(w_ref, g_ref, m_ref, v_ref, d_ref, nm_ref, nv_ref):
        gv = g_ref[...]
        nm = ADAM_B1 * m_ref[...] + (1.0 - ADAM_B1) * gv
        nv = ADAM_B2 * v_ref[...] + (1.0 - ADAM_B2) * (gv * gv)
        nm_ref[...] = nm
        nv_ref[...] = nv
        d_ref[...] = -ADAM_LR * ((nm / c1) / (jnp.sqrt(nv / c2) + ADAM_EPS) + ADAM_WD * w_ref[...])

    spec = pl.BlockSpec((tr, cdim), lambda i: (i, 0))
    sh = jax.ShapeDtypeStruct((r, cdim), F32)
    return pl.pallas_call(body, name=name, grid=(r // tr,), in_specs=[spec] * 4, out_specs=[spec] * 3,
                          out_shape=[sh, sh, sh], compiler_params=_params(("parallel",)))(w, g, m, v)


SMALL = (("c_ctx", D_MODEL), ("b_ada", 6 * D_MODEL), ("q_norm_g", HEAD_DIM), ("k_norm_g", HEAD_DIM),
         ("sink_logit", HEADS_A), ("ln1_g", D_MODEL), ("ln1_b", D_MODEL), ("ln2_g", D_MODEL), ("ln2_b", D_MODEL))
WEIGHT_ORDER = ("c_ctx", "w_ada", "b_ada", "w_in", "q_norm_g", "k_norm_g", "sink_logit", "w_out", "ln1_g", "ln1_b",
                "w_gate", "w_up", "w_down", "ln2_g", "ln2_b")


def kernel(x, c, ctx, c_ctx, w_ada, b_ada, w_in, q_norm_g, k_norm_g, sink_logit, w_out, ln1_g, ln1_b, w_gate, w_up, w_down, ln2_g, ln2_b, loss_target, m_c_ctx, m_w_ada, m_b_ada, m_w_in, m_q_norm_g, m_k_norm_g, m_sink_logit, m_w_out, m_ln1_g, m_ln1_b, m_w_gate, m_w_up, m_w_down, m_ln2_g, m_ln2_b, v_c_ctx, v_w_ada, v_b_ada, v_w_in, v_q_norm_g, v_k_norm_g, v_sink_logit, v_w_out, v_ln1_g, v_ln1_b, v_w_gate, v_w_up, v_w_down, v_ln2_g, v_ln2_b):
    d = D_MODEL
    w = dict(c_ctx=c_ctx, w_ada=w_ada[0], b_ada=b_ada, w_in=w_in[0], q_norm_g=q_norm_g, k_norm_g=k_norm_g,
             sink_logit=sink_logit, w_out=w_out[0], ln1_g=ln1_g, ln1_b=ln1_b, w_gate=w_gate[0], w_up=w_up[0],
             w_down=w_down[0], ln2_g=ln2_g, ln2_b=ln2_b)
    m = dict(c_ctx=m_c_ctx, w_ada=m_w_ada[0], b_ada=m_b_ada, w_in=m_w_in[0], q_norm_g=m_q_norm_g, k_norm_g=m_k_norm_g,
             sink_logit=m_sink_logit, w_out=m_w_out[0], ln1_g=m_ln1_g, ln1_b=m_ln1_b, w_gate=m_w_gate[0],
             w_up=m_w_up[0], w_down=m_w_down[0], ln2_g=m_ln2_g, ln2_b=m_ln2_b)
    v = dict(c_ctx=v_c_ctx, w_ada=v_w_ada[0], b_ada=v_b_ada, w_in=v_w_in[0], q_norm_g=v_q_norm_g, k_norm_g=v_k_norm_g,
             sink_logit=v_sink_logit, w_out=v_w_out[0], ln1_g=v_ln1_g, ln1_b=v_ln1_b, w_gate=v_w_gate[0],
             w_up=v_w_up[0], w_down=v_w_down[0], ln2_g=v_ln2_g, ln2_b=v_ln2_b)
    mx, my, mc = _me()
    s_me = _shard_of((mx, my))
    me = _dev_index(mx, my, mc)
    pad8 = lambda row: jnp.concatenate([row.reshape(1, -1), jnp.zeros((7, row.size), F32)], axis=0)

    b_shard = lax.dynamic_slice(b_ada, (0, s_me * ADA_W), (1, ADA_W))
    act, mods4 = _ada_forward(pad8(c), pad8(c_ctx), w["w_ada"], b_shard)

    gathers = []
    prev, shard = mods4, s_me
    for k, names in enumerate(W_GROUPS):
        arrs = tuple(BIG_INDEX[name] for name in names)
        bufs = [_cast_into_full(w[name], shard, BIG[a][1], BIG[a][2], "cast_" + name) for name, a in zip(names, arrs)]
        send_sems, recv_sems, thru, prev = _gather_start("g%d" % k, arrs, bufs, prev)
        shard = s_me + prev[0, 0].astype(jnp.int32)
        gathers.append((arrs, send_sems, recv_sems, thru))

    forwards = {}

    def prefetch(k, after):
        arrs, send_sems, recv_sems, thru = gathers[k]
        landed = _gather_wait("g%d" % k, arrs, send_sems, recv_sems, thru, after)
        fwd_send, fwd_recv, landed, token = _forward_start("g%d" % k, arrs, landed)
        forwards[k] = (fwd_send, fwd_recv, landed)
        return token[0, 0]

    def weights(k, after):
        arrs, send_sems, recv_sems, thru = gathers[k]
        if k in forwards:
            return _forward_wait("g%d" % k, arrs, *forwards[k], after)
        landed = _gather_wait("g%d" % k, arrs, send_sems, recv_sems, thru, after)
        return _gather_forward("g%d" % k, arrs, landed)

    mod = jnp.transpose(mods4[:, 0:1, :], (1, 0, 2)).reshape(1, 6 * d) + prev[0, 0]
    mod_ctx = jnp.transpose(mods4[:, 8:9, :], (1, 0, 2)).reshape(1, 6 * d)

    scatters = {}

    def grads_out(k, dws):
        arrs = tuple(BIG_INDEX[name] for name in G_GROUPS[k])
        scatters[k], zero = _scatter_begin("g%d" % k, arrs, dws)
        return zero

    grad_x, partial = _layer_fwd_bwd(x[0], ctx[0], loss_target[0], mod, mod_ctx, weights, prefetch, grads_out,
                                     q_norm_g, k_norm_g, sink_logit, ln1_g, ln1_b, ln2_g, ln2_b)
    grads, delta, new_m, new_v = {}, {}, {}, {}

    after, joins = partial, []
    for k, names in enumerate(G_GROUPS):
        arrs = tuple(BIG_INDEX[name] for name in names)
        state, after = _scatter_reduce("g%d" % k, arrs, scatters[k], after)
        joins.append(state)
    for k, names in enumerate(G_GROUPS):
        g_own, g_other = _scatter_end("g%d" % k, joins[k], after)
        for name, own, other in zip(names, g_own, g_other):
            grads[name], delta[name], new_m[name], new_v[name] = _adamw_halves(
                w[name], own, other, m[name], v[name], mc, BIG[BIG_INDEX[name]][2], "adamw_" + name)
            after = new_v[name]

    gathered = _gather_rows(partial, "gather_partials", after)
    tot = _small_reduce(gathered)
    grads["b_ada"] = tot[0:6].reshape(1, 6 * d)
    grads["ln1_g"], grads["ln1_b"], grads["ln2_g"], grads["ln2_b"] = tot[8:9], tot[9:10], tot[10:11], tot[11:12]
    grads["q_norm_g"] = tot[13:14, 0:HEAD_DIM]
    grads["k_norm_g"] = tot[13:14, HEAD_DIM:2 * HEAD_DIM]
    grads["sink_logit"] = tot[13:14, 2 * HEAD_DIM:2 * HEAD_DIM + HEADS_A]
    loss = tot[12, 0]

    dm_all = gathered[:, 0:6, :].reshape(N_DEV, 6 * d)
    dmc_tot = jnp.concatenate([tot[6:8].reshape(1, 2 * d), jnp.zeros((1, 4 * d), F32)], axis=1)
    dm_rows = jnp.concatenate([pad8(dm_all[i]) for i in range(N_DEV)] + [pad8(dmc_tot), jnp.zeros((8, 6 * d), F32)], axis=0)
    dm_shard = lax.dynamic_slice(dm_rows, (0, s_me * ADA_W), (ADA_ROWS, ADA_W))
    grads["w_ada"] = _matmul(act, dm_shard, name="dw_ada", ta=True, tm=1024, tn=1024, tk=ADA_ROWS, out_dtype=F32)
    dmc_shard = lax.dynamic_slice(pad8(dmc_tot), (0, s_me * ADA_W), (8, ADA_W))
    cc_part = _matmul(dmc_shard, w["w_ada"], name="d_cctx", tb=True, tm=8, tn=1024, tk=1536, out_dtype=F32)
    grads["c_ctx"] = _cctx_grad(_gather_rows(cc_part, "gather_cctx", tot), c_ctx).reshape(d)

    delta["w_ada"], new_m["w_ada"], new_v["w_ada"] = _adamw(w["w_ada"], grads["w_ada"], m["w_ada"], v["w_ada"],
                                                            "adamw_w_ada")
    pack = lambda t: jnp.concatenate([t[name].reshape(1, size) for name, size in SMALL], axis=1)
    pd, pm, pv = _adamw(pack(w), pack(grads), pack(m), pack(v), "adamw_small")
    off = 0
    for name, size in SMALL:
        delta[name], new_m[name], new_v[name] = [t[:, off:off + size].reshape(w[name].shape) for t in (pd, pm, pv)]
        grads[name] = grads[name].reshape(w[name].shape)
        off += size

    lead = lambda name, t: t[None] if name in ("w_ada", "w_in", "w_out", "w_gate", "w_up", "w_down") else t
    outs = [loss, grad_x[None]]
    for group in (grads, delta, new_m, new_v):
        outs += [lead(name, group[name]) for name in WEIGHT_ORDER]
    return tuple(outs)
```

```python
import functools
import math

import jax
import jax.numpy as jnp
from jax import lax
from jax.experimental import pallas as pl
from jax.experimental.pallas import tpu as pltpu

F32 = jnp.float32
BF16 = jnp.bfloat16
MESH = pl.DeviceIdType.MESH

D_MODEL = 2048
HEAD_DIM = 128
HEADS_A = 8
HEADS_B = 8
KV_A = 2
KV_B = 2
GROUP = 4
GRID_W = 64
WINDOW = 128
BLOCK = 128
FFN = 5632
IN_WIDTH = 3072
MIX_WIDTH = 2048
ROPE_THETA = 10000.0
EPS = 1e-6
ATTN_SCALE = HEAD_DIM ** -0.5
DN_ALPHA = 2.0 ** 0.25
N_SHARD = 4
N_DEV = 8

ADAM_LR = 0.001
ADAM_B1 = 0.9
ADAM_B2 = 0.999
ADAM_EPS = 1e-08
ADAM_WD = 0.01
ADAM_STEP = 10

QA0, KA0, VA0, QB0, KB0, VB0 = 0, 1024, 1280, 1536, 2560, 2816

VMEM_LIMIT = 56 * 1024 * 1024
ROW_TILE = 256
NN = (((1,), (0,)), ((), ()))
NT = (((1,), (1,)), ((), ()))
TN = (((0,), (0,)), ((), ()))


def _fit(total, pref):
    step = ROW_TILE // 4
    best = step
    for cand in range(step, pref + 1, step):
        if total % cand == 0:
            best = cand
    return best


def _params(sem=None):
    return pltpu.CompilerParams(dimension_semantics=sem, vmem_limit_bytes=VMEM_LIMIT)


def _matmul(a, b, *, name, ta=False, tb=False, tm, tn, tk, out_dtype, after=None):
    m = a.shape[1] if ta else a.shape[0]
    k = a.shape[0] if ta else a.shape[1]
    n = b.shape[0] if tb else b.shape[1]
    assert (b.shape[1] if tb else b.shape[0]) == k
    tm, tn, tk = min(tm, m), min(tn, n), min(tk, k)
    assert m % tm == 0 and n % tn == 0 and k % tk == 0, (name, m, n, k, tm, tn, tk)
    nk = k // tk
    dn = (((0 if ta else 1,), (1 if tb else 0,)), ((), ()))

    def product(a_ref, b_ref):
        return lax.dot_general(a_ref[...].astype(BF16), b_ref[...].astype(BF16), dn, preferred_element_type=F32)

    def body_whole_k(a_ref, b_ref, *rest):
        o_ref = rest[-1]
        o_ref[...] = product(a_ref, b_ref).astype(o_ref.dtype)

    def body(a_ref, b_ref, *rest):
        o_ref, acc_ref = rest[-2:]
        kk = pl.program_id(2)
        part = product(a_ref, b_ref)

        @pl.when(kk == 0)
        def _():
            acc_ref[...] = part

        @pl.when(kk != 0)
        def _():
            acc_ref[...] += part

        @pl.when(kk == nk - 1)
        def _():
            o_ref[...] = acc_ref[...].astype(o_ref.dtype)

    a_spec = (pl.BlockSpec((tk, tm), lambda i, j, kk: (kk, i)) if ta
              else pl.BlockSpec((tm, tk), lambda i, j, kk: (i, kk)))
    b_spec = (pl.BlockSpec((tn, tk), lambda i, j, kk: (j, kk)) if tb
              else pl.BlockSpec((tk, tn), lambda i, j, kk: (kk, j)))
    return pl.pallas_call(
        body_whole_k if nk == 1 else body, name=name, grid=(m // tm, n // tn, nk),
        in_specs=[a_spec, b_spec] + ([] if after is None else [pl.BlockSpec(memory_space=pl.ANY)]),
        out_specs=pl.BlockSpec((tm, tn), lambda i, j, kk: (i, j)),
        out_shape=jax.ShapeDtypeStruct((m, n), out_dtype),
        scratch_shapes=[] if nk == 1 else [pltpu.VMEM((tm, tn), F32)],
        compiler_params=_params(("parallel", "parallel", "arbitrary")),
    )(a, b, *([] if after is None else [after]))


def _modulate_rows(x, ctx, mods):
    n, d = x.shape
    c = ctx.shape[0]
    nx = n // ROW_TILE
    assert c == ROW_TILE

    def body(x_ref, ctx_ref, mods_ref, o_ref):
        i = pl.program_id(0)

        @pl.when(i < nx)
        def _():
            o_ref[...] = (x_ref[...] * (1.0 + mods_ref[0:1, :]) + mods_ref[1:2, :]).astype(BF16)

        @pl.when(i >= nx)
        def _():
            o_ref[...] = (ctx_ref[...] * (1.0 + mods_ref[2:3, :]) + mods_ref[3:4, :]).astype(BF16)

    return pl.pallas_call(
        body, name="modulate_rows", grid=(nx + 1,),
        in_specs=[pl.BlockSpec((ROW_TILE, d), lambda i: (jnp.minimum(i, nx - 1), 0)),
                  pl.BlockSpec((ROW_TILE, d), lambda i: (0, 0)),
                  pl.BlockSpec((8, d), lambda i: (0, 0))],
        out_specs=pl.BlockSpec((ROW_TILE, d), lambda i: (i, 0)),
        out_shape=jax.ShapeDtypeStruct((n + c, d), BF16),
        compiler_params=_params(("parallel",)),
    )(x, ctx, mods)


def _rope_tables(n, c):
    rows = n // GRID_W
    row_ids = jnp.repeat(jnp.arange(rows, dtype=F32), GRID_W)
    col_ids = jnp.tile(jnp.arange(GRID_W, dtype=F32), rows)
    axis_dim = HEAD_DIM // 2
    inv_freq = jnp.power(ROPE_THETA, -jnp.arange(0, axis_dim, 2, dtype=F32) / axis_dim)
    ang_r = row_ids[:, None] * inv_freq
    ang_c = col_ids[:, None] * inv_freq
    ang = jnp.concatenate([ang_r, ang_r, ang_c, ang_c], axis=-1)
    cos, sin = jnp.cos(ang), jnp.sin(ang)
    quarter = (jnp.arange(HEAD_DIM) // (HEAD_DIM // 4)) % 2
    sin_a = jnp.where(quarter == 0, -sin, 0.0)
    sin_b = jnp.where(quarter == 1, sin, 0.0)
    pad = lambda t, v: jnp.concatenate([t, jnp.full((c, HEAD_DIM), v, F32)], axis=0)
    return pad(cos, 1.0), pad(sin_a, 0.0), pad(sin_b, 0.0)


def _rope(x, cos, sin_a, sin_b):
    return x * cos + pltpu.roll(x, 96, 1) * sin_a + pltpu.roll(x, 32, 1) * sin_b


def _rope_t(dy, cos, sin_a, sin_b):
    return dy * cos - pltpu.roll(dy, 96, 1) * sin_a - pltpu.roll(dy, 32, 1) * sin_b


def _rms(x):
    r = lax.rsqrt(jnp.mean(x * x, axis=-1, keepdims=True) + EPS)
    return x * r, r


def _qkv_post(h_all, cos, sin_a, sin_b, q_g, k_g):
    t = h_all.shape[0]
    nt = t // ROW_TILE

    def body(h_ref, cos_ref, sa_ref, sb_ref, qg_ref, kg_ref, qa_ref, ka_ref, va_ref, qb_ref, kb_ref, vb_ref):
        cos_, sa, sb = cos_ref[...], sa_ref[...], sb_ref[...]
        sl = lambda off, hh: h_ref[:, off + hh * HEAD_DIM: off + (hh + 1) * HEAD_DIM]
        for hh in range(HEADS_A):
            qa_ref[hh] = (_rope(sl(QA0, hh), cos_, sa, sb) * ATTN_SCALE).astype(BF16)
        for hh in range(KV_A):
            ka_ref[hh] = _rope(sl(KA0, hh), cos_, sa, sb).astype(BF16)
            va_ref[hh] = sl(VA0, hh).astype(BF16)
        for hh in range(HEADS_B):
            xn, _ = _rms(sl(QB0, hh))
            qb_ref[hh] = (_rope(xn * qg_ref[...], cos_, sa, sb) * ATTN_SCALE).astype(BF16)
        for hh in range(KV_B):
            xn, _ = _rms(sl(KB0, hh))
            kb_ref[hh] = _rope(xn * kg_ref[...], cos_, sa, sb).astype(BF16)
            vb_ref[hh] = sl(VB0, hh).astype(BF16)

    tab = pl.BlockSpec((ROW_TILE, HEAD_DIM), lambda i: (i, 0))
    gain = pl.BlockSpec((1, HEAD_DIM), lambda i: (0, 0))
    hs = lambda nh: pl.BlockSpec((nh, ROW_TILE, HEAD_DIM), lambda i: (0, i, 0))
    sh = lambda nh: jax.ShapeDtypeStruct((nh, t, HEAD_DIM), BF16)
    return pl.pallas_call(
        body, name="qkv_post", grid=(nt,),
        in_specs=[pl.BlockSpec((ROW_TILE, IN_WIDTH), lambda i: (i, 0)), tab, tab, tab, gain, gain],
        out_specs=[hs(HEADS_A), hs(KV_A), hs(KV_A), hs(HEADS_B), hs(KV_B), hs(KV_B)],
        out_shape=[sh(HEADS_A), sh(KV_A), sh(KV_A), sh(HEADS_B), sh(KV_B), sh(KV_B)],
        compiler_params=_params(("parallel",)),
    )(h_all, cos, sin_a, sin_b, q_g, k_g)


def _qkv_bwd_post(h_all, cos, sin_a, sin_b, q_g, k_g, dqa, dka, dva, dqb, dkb, dvb, n):
    t = h_all.shape[0]
    nt = t // ROW_TILE
    nx = n // ROW_TILE

    def body(h_ref, cos_ref, sa_ref, sb_ref, qg_ref, kg_ref,
             dqa_ref, dka_ref, dva_ref, dqb_ref, dkb_ref, dvb_ref, dh_ref, gs_ref):
        i = pl.program_id(0)
        cos_, sa, sb = cos_ref[...], sa_ref[...], sb_ref[...]
        latent = (i < nx).astype(F32)
        sl = lambda off, hh: h_ref[:, off + hh * HEAD_DIM: off + (hh + 1) * HEAD_DIM]

        def put(off, hh, val):
            dh_ref[:, off + hh * HEAD_DIM: off + (hh + 1) * HEAD_DIM] = val.astype(BF16)

        def norm_bwd(x, gain, dy):
            xn, r = _rms(x)
            dxh = dy * gain
            dx = r * (dxh - xn * jnp.mean(dxh * xn, axis=-1, keepdims=True))
            return dx, jnp.sum(dy * xn, axis=0, keepdims=True)

        for hh in range(HEADS_A):
            put(QA0, hh, _rope_t(dqa_ref[hh] * (ATTN_SCALE * latent), cos_, sa, sb))
        for hh in range(KV_A):
            put(KA0, hh, _rope_t(dka_ref[hh], cos_, sa, sb))
            put(VA0, hh, dva_ref[hh])
        gq = jnp.zeros((1, HEAD_DIM), F32)
        gk = jnp.zeros((1, HEAD_DIM), F32)
        for hh in range(HEADS_B):
            dq_t = dqb_ref[hh // GROUP, :, (hh % GROUP) * ROW_TILE:(hh % GROUP + 1) * ROW_TILE]
            dy = _rope_t(dq_t.T * (ATTN_SCALE * latent), cos_, sa, sb)
            dx, g = norm_bwd(sl(QB0, hh), qg_ref[...], dy)
            put(QB0, hh, dx)
            gq = gq + g
        for hh in range(KV_B):
            dy = _rope_t(dkb_ref[hh], cos_, sa, sb)
            dx, g = norm_bwd(sl(KB0, hh), kg_ref[...], dy)
            put(KB0, hh, dx)
            gk = gk + g
            put(VB0, hh, dvb_ref[hh])
        upd = jnp.concatenate([gq, gk, jnp.zeros((6, HEAD_DIM), F32)], axis=0)

        @pl.when(i == 0)
        def _():
            gs_ref[...] = upd

        @pl.when(i != 0)
        def _():
            gs_ref[...] += upd

    tab = pl.BlockSpec((ROW_TILE, HEAD_DIM), lambda i: (i, 0))
    gain = pl.BlockSpec((1, HEAD_DIM), lambda i: (0, 0))
    lat = lambda nh: pl.BlockSpec((nh, ROW_TILE, HEAD_DIM), lambda i: (0, jnp.minimum(i, nx - 1), 0))
    full = lambda nh: pl.BlockSpec((nh, ROW_TILE, HEAD_DIM), lambda i: (0, i, 0))
    return pl.pallas_call(
        body, name="qkv_bwd_post", grid=(nt,),
        in_specs=[pl.BlockSpec((ROW_TILE, IN_WIDTH), lambda i: (i, 0)), tab, tab, tab, gain, gain,
                  lat(HEADS_A), full(KV_A), full(KV_A),
                  pl.BlockSpec((KV_B, None, HEAD_DIM, GROUP * ROW_TILE), lambda i: (0, jnp.minimum(i, nx - 1), 0, 0)),
                  full(KV_B), full(KV_B)],
        out_specs=[pl.BlockSpec((ROW_TILE, IN_WIDTH), lambda i: (i, 0)),
                   pl.BlockSpec((8, HEAD_DIM), lambda i: (0, 0))],
        out_shape=[jax.ShapeDtypeStruct((t, IN_WIDTH), BF16), jax.ShapeDtypeStruct((8, HEAD_DIM), F32)],
        compiler_params=_params(("arbitrary",)),
    )(h_all, cos, sin_a, sin_b, q_g, k_g, dqa, dka, dva, dqb, dkb, dvb)


GB_TQ = 256
GB_TK = 256


def _heads_rows(ref2d, tq):
    return jnp.concatenate([ref2d[:, hh * HEAD_DIM:(hh + 1) * HEAD_DIM] for hh in range(GROUP)], axis=0)


def _attn_b_fwd(qb, kb, vb, n):
    t = kb.shape[1]
    nk = t // GB_TK
    tq = GB_TQ
    rows = GROUP * tq

    def body(q_ref, k_ref, v_ref, o_ref, lse_ref, m_s, l_s, acc_s):
        q = q_ref[...].reshape(rows, HEAD_DIM)
        m_s[...] = jnp.full((1, rows), -jnp.inf, F32)
        l_s[...] = jnp.zeros((1, rows), F32)
        acc_s[...] = jnp.zeros((HEAD_DIM, rows), F32)

        def scores(j):
            start = pl.multiple_of(j * GB_TK, GB_TK)
            return lax.dot_general(k_ref[pl.ds(start, GB_TK), :], q, NT, preferred_element_type=F32)

        def step(j, st):
            st_next = scores(jnp.minimum(j + 1, nk - 1))
            vs = v_ref[pl.ds(pl.multiple_of(j * GB_TK, GB_TK), GB_TK), :]
            m_prev = m_s[...]
            m_new = jnp.maximum(m_prev, jnp.max(st, axis=0, keepdims=True))
            p = jnp.exp(st - m_new)
            alpha = jnp.exp(m_prev - m_new)
            l_s[...] = alpha * l_s[...] + jnp.sum(p, axis=0, keepdims=True)
            acc_s[...] = alpha * acc_s[...] + lax.dot_general(vs, p.astype(BF16), TN, preferred_element_type=F32)
            m_s[...] = m_new
            return st_next

        lax.fori_loop(0, nk, step, scores(0))
        ot = acc_s[...] * (1.0 / l_s[...])
        lse_ref[...] = m_s[...] + jnp.log(l_s[...])
        for hh in range(GROUP):
            o_ref[:, hh * HEAD_DIM:(hh + 1) * HEAD_DIM] = ot[:, hh * tq:(hh + 1) * tq].T.astype(BF16)

    return pl.pallas_call(
        body, name="attn_b_fwd", grid=(KV_B, n // tq),
        in_specs=[pl.BlockSpec((GROUP, tq, HEAD_DIM), lambda g, i: (g, i, 0)),
                  pl.BlockSpec((None, t, HEAD_DIM), lambda g, i: (g, 0, 0)),
                  pl.BlockSpec((None, t, HEAD_DIM), lambda g, i: (g, 0, 0))],
        out_specs=[pl.BlockSpec((tq, GROUP * HEAD_DIM), lambda g, i: (i, KV_A + g)),
                   pl.BlockSpec((None, None, 1, rows), lambda g, i: (g, i, 0, 0))],
        out_shape=[jax.ShapeDtypeStruct((n, MIX_WIDTH), BF16),
                   jax.ShapeDtypeStruct((KV_B, n // tq, 1, rows), F32)],
        scratch_shapes=[pltpu.VMEM((1, rows), F32), pltpu.VMEM((1, rows), F32), pltpu.VMEM((HEAD_DIM, rows), F32)],
        compiler_params=_params(("parallel", "parallel")),
    )(qb, kb, vb)


def _attn_b_bwd(qb, kb, vb, dheads, lse, delta, n):
    t = kb.shape[1]
    nk = t // GB_TK
    tq = GB_TQ
    nq = n // tq
    rows = GROUP * tq

    def body(q_ref, k_ref, v_ref, do_ref, lse_ref, dl_ref, dq_ref, dk_ref, dv_ref):
        j = pl.program_id(1)
        i = pl.program_id(2)

        q = q_ref[...].reshape(rows, HEAD_DIM)
        do = _heads_rows(do_ref, tq)
        ks, vs = k_ref[...], v_ref[...]
        st = lax.dot_general(ks, q, NT, preferred_element_type=F32)
        p = jnp.exp(st - lse_ref[...])
        dpt = lax.dot_general(vs, do, NT, preferred_element_type=F32)
        ds = (p * (dpt - dl_ref[...])).astype(BF16)
        dv_part = lax.dot_general(p.astype(BF16), do, NN, preferred_element_type=F32)
        dk_part = lax.dot_general(ds, q, NN, preferred_element_type=F32)
        dq_part = lax.dot_general(ks, ds, TN, preferred_element_type=F32)

        @pl.when(i == 0)
        def _():
            dk_ref[...] = dk_part
            dv_ref[...] = dv_part

        @pl.when(i != 0)
        def _():
            dk_ref[...] += dk_part
            dv_ref[...] += dv_part

        @pl.when(j == 0)
        def _():
            dq_ref[i] = dq_part

        @pl.when(j != 0)
        def _():
            dq_ref[i] += dq_part

    kv = pl.BlockSpec((None, GB_TK, HEAD_DIM), lambda g, j, i: (g, j, 0))
    row = pl.BlockSpec((None, None, 1, rows), lambda g, j, i: (g, i, 0, 0))
    return pl.pallas_call(
        body, name="attn_b_bwd", grid=(KV_B, nk, nq),
        in_specs=[pl.BlockSpec((GROUP, tq, HEAD_DIM), lambda g, j, i: (g, i, 0)), kv, kv,
                  pl.BlockSpec((tq, GROUP * HEAD_DIM), lambda g, j, i: (i, KV_A + g)), row, row],
        out_specs=[pl.BlockSpec((None, nq, HEAD_DIM, rows), lambda g, j, i: (g, 0, 0, 0)), kv, kv],
        out_shape=[jax.ShapeDtypeStruct((KV_B, nq, HEAD_DIM, rows), F32),
                   jax.ShapeDtypeStruct((KV_B, t, HEAD_DIM), F32),
                   jax.ShapeDtypeStruct((KV_B, t, HEAD_DIM), F32)],
        compiler_params=_params(("parallel", "arbitrary", "arbitrary")),
    )(qb, kb, vb, dheads, lse, delta)


def _delta_rows(dheads, heads):
    n = heads.shape[0]
    tq = GB_TQ
    w = GROUP * HEAD_DIM

    def body(a_ref, b_ref, o_ref):
        prod = a_ref[...].astype(F32) * b_ref[...].astype(F32)
        cols = [jnp.sum(prod[:, hh * HEAD_DIM:(hh + 1) * HEAD_DIM].T, axis=0, keepdims=True) for hh in range(GROUP)]
        o_ref[...] = jnp.concatenate(cols, axis=1)

    blk = pl.BlockSpec((tq, w), lambda g, i: (i, KV_A + g))
    return pl.pallas_call(
        body, name="delta_rows", grid=(KV_B, n // tq),
        in_specs=[blk, blk],
        out_specs=pl.BlockSpec((None, None, 1, GROUP * tq), lambda g, i: (g, i, 0, 0)),
        out_shape=jax.ShapeDtypeStruct((KV_B, n // tq, 1, GROUP * tq), F32),
        compiler_params=_params(("parallel", "parallel")),
    )(dheads, heads)


KWIN = 3 * BLOCK


def _window_bias():
    row = jnp.arange(KWIN)[None, :, None]
    col = jnp.arange(GROUP * BLOCK)[None, None, :] % BLOCK
    shift = jnp.arange(3)[:, None, None] * BLOCK
    return jnp.where(jnp.abs(shift + col - row) <= WINDOW, 0.0, -jnp.inf).astype(F32)


def _window_variant(j, nb):
    return j - jnp.clip(j - 1, 0, nb - 3)


def _window_scores(q, k_ref, j, n, nb, sink_row, bias):
    c = k_ref.shape[0] - n
    start = pl.multiple_of(jnp.clip(j - 1, 0, nb - 3) * BLOCK, BLOCK)
    kw = k_ref[pl.ds(start, KWIN), :]
    kc = k_ref[pl.ds(n, c), :]
    s_loc = lax.dot_general(kw, q, NT, preferred_element_type=F32) + bias
    s_ctx = lax.dot_general(kc, q, NT, preferred_element_type=F32)
    m = jnp.maximum(jnp.maximum(jnp.max(s_loc, axis=0, keepdims=True), jnp.max(s_ctx, axis=0, keepdims=True)),
                    sink_row)
    e_loc, e_ctx, e_sink = jnp.exp(s_loc - m), jnp.exp(s_ctx - m), jnp.exp(sink_row - m)
    inv = 1.0 / (jnp.sum(e_loc, axis=0, keepdims=True) + jnp.sum(e_ctx, axis=0, keepdims=True) + e_sink)
    return e_loc * inv, e_ctx * inv, e_sink * inv, start


def _sink_row(sink_ref, g):
    return jnp.concatenate([sink_ref[pl.ds(g * GROUP + hh, 1), :] for hh in range(GROUP)], axis=1)


def _attn_a_fwd(qa, ka, va, sink_b, heads_b, n):
    t = ka.shape[1]
    nb = n // BLOCK
    assert nb >= 3

    def body(q_ref, k_ref, v_ref, sink_ref, heads_b_ref, bias_ref, o_ref):
        g, j = pl.program_id(0), pl.program_id(1)
        q = q_ref[...].reshape(GROUP * BLOCK, HEAD_DIM)
        p_loc, p_ctx, _, start = _window_scores(q, k_ref, j, n, nb, _sink_row(sink_ref, g), bias_ref[...])
        vw = v_ref[pl.ds(start, KWIN), :]
        vc = v_ref[pl.ds(n, t - n), :]
        ot = (lax.dot_general(vw, p_loc.astype(BF16), TN, preferred_element_type=F32)
              + lax.dot_general(vc, p_ctx.astype(BF16), TN, preferred_element_type=F32))
        for hh in range(GROUP):
            o_ref[:, hh * HEAD_DIM:(hh + 1) * HEAD_DIM] = ot[:, hh * BLOCK:(hh + 1) * BLOCK].T.astype(BF16)

    return pl.pallas_call(
        body, name="attn_a_fwd", grid=(KV_A, nb),
        in_specs=[pl.BlockSpec((GROUP, BLOCK, HEAD_DIM), lambda g, j: (g, j, 0)),
                  pl.BlockSpec((None, t, HEAD_DIM), lambda g, j: (g, 0, 0)),
                  pl.BlockSpec((None, t, HEAD_DIM), lambda g, j: (g, 0, 0)),
                  pl.BlockSpec((HEADS_A, HEAD_DIM), lambda g, j: (0, 0)),
                  pl.BlockSpec(memory_space=pl.ANY),
                  pl.BlockSpec((None, KWIN, GROUP * BLOCK), lambda g, j: (_window_variant(j, nb), 0, 0))],
        out_specs=pl.BlockSpec((BLOCK, GROUP * HEAD_DIM), lambda g, j: (j, g)),
        out_shape=jax.ShapeDtypeStruct((n, MIX_WIDTH), BF16),
        input_output_aliases={4: 0},
        compiler_params=_params(("parallel", "parallel")),
    )(qa, ka, va, sink_b, heads_b, _window_bias())


def _attn_a_bwd(qa, ka, va, sink_b, dheads, n):
    t = ka.shape[1]
    c = t - n
    nb = n // BLOCK

    def body(q_ref, k_ref, v_ref, sink_ref, do_ref, bias_ref, dq_ref, dk_ref, dv_ref, dsink_ref):
        g, j = pl.program_id(0), pl.program_id(1)

        @pl.when(j == 0)
        def _():
            dk_ref[...] = jnp.zeros_like(dk_ref)
            dv_ref[...] = jnp.zeros_like(dv_ref)
            dsink_ref[...] = jnp.zeros_like(dsink_ref)

        q = q_ref[...].reshape(GROUP * BLOCK, HEAD_DIM)
        do = _heads_rows(do_ref, BLOCK)
        p_loc, p_ctx, p_sink, start = _window_scores(q, k_ref, j, n, nb, _sink_row(sink_ref, g), bias_ref[...])
        kw, vw = k_ref[pl.ds(start, KWIN), :], v_ref[pl.ds(start, KWIN), :]
        kc, vc = k_ref[pl.ds(n, c), :], v_ref[pl.ds(n, c), :]
        dp_loc = lax.dot_general(vw, do, NT, preferred_element_type=F32)
        dp_ctx = lax.dot_general(vc, do, NT, preferred_element_type=F32)
        dl = jnp.sum(p_loc * dp_loc, axis=0, keepdims=True) + jnp.sum(p_ctx * dp_ctx, axis=0, keepdims=True)
        ds_loc = (p_loc * (dp_loc - dl)).astype(BF16)
        ds_ctx = (p_ctx * (dp_ctx - dl)).astype(BF16)
        dqt = (lax.dot_general(kw, ds_loc, TN, preferred_element_type=F32)
               + lax.dot_general(kc, ds_ctx, TN, preferred_element_type=F32))
        for hh in range(GROUP):
            dq_ref[hh] = dqt[:, hh * BLOCK:(hh + 1) * BLOCK].T
        dk_ref[pl.ds(start, KWIN), :] += lax.dot_general(ds_loc, q, NN, preferred_element_type=F32)
        dv_ref[pl.ds(start, KWIN), :] += lax.dot_general(p_loc.astype(BF16), do, NN, preferred_element_type=F32)
        dk_ref[pl.ds(n, c), :] += lax.dot_general(ds_ctx, q, NN, preferred_element_type=F32)
        dv_ref[pl.ds(n, c), :] += lax.dot_general(p_ctx.astype(BF16), do, NN, preferred_element_type=F32)
        dsk = -(p_sink * dl)
        upd = [jnp.broadcast_to(jnp.sum(dsk[:, hh * BLOCK:(hh + 1) * BLOCK], axis=1, keepdims=True), (1, HEAD_DIM))
               for hh in range(GROUP)]
        dsink_ref[...] += jnp.concatenate(upd + [jnp.zeros((8 - GROUP, HEAD_DIM), F32)], axis=0)

    res = pl.BlockSpec((None, t, HEAD_DIM), lambda g, j: (g, 0, 0))
    return pl.pallas_call(
        body, name="attn_a_bwd", grid=(KV_A, nb),
        in_specs=[pl.BlockSpec((GROUP, BLOCK, HEAD_DIM), lambda g, j: (g, j, 0)), res, res,
                  pl.BlockSpec((HEADS_A, HEAD_DIM), lambda g, j: (0, 0)),
                  pl.BlockSpec((BLOCK, GROUP * HEAD_DIM), lambda g, j: (j, g)),
                  pl.BlockSpec((None, KWIN, GROUP * BLOCK), lambda g, j: (_window_variant(j, nb), 0, 0))],
        out_specs=[pl.BlockSpec((GROUP, BLOCK, HEAD_DIM), lambda g, j: (g, j, 0)), res, res,
                   pl.BlockSpec((None, 8, HEAD_DIM), lambda g, j: (g, 0, 0))],
        out_shape=[jax.ShapeDtypeStruct((HEADS_A, n, HEAD_DIM), F32),
                   jax.ShapeDtypeStruct((KV_A, t, HEAD_DIM), F32),
                   jax.ShapeDtypeStruct((KV_A, t, HEAD_DIM), F32),
                   jax.ShapeDtypeStruct((KV_A, 8, HEAD_DIM), F32)],
        compiler_params=_params(("parallel", "arbitrary")),
    )(qa, ka, va, sink_b, dheads, _window_bias())


def _ln_stats(r):
    mu = jnp.mean(r, axis=-1, keepdims=True)
    cen = r - mu
    rstd = lax.rsqrt(jnp.mean(cen * cen, axis=-1, keepdims=True) + EPS)
    return cen * rstd, rstd


def _ln_bwd(dy, xhat, rstd, gain):
    dxh = dy * gain
    return rstd * (dxh - jnp.mean(dxh, axis=-1, keepdims=True)
                   - xhat * jnp.mean(dxh * xhat, axis=-1, keepdims=True))


def _accumulate_rows(ref, rows, i):
    pad = [jnp.zeros_like(rows[0])] * (8 - len(rows))
    upd = jnp.concatenate(rows + pad, axis=0)

    @pl.when(i == 0)
    def _():
        ref[...] = upd

    @pl.when(i != 0)
    def _():
        ref[...] += upd


def _colsum(v):
    return jnp.sum(v, axis=0, keepdims=True)


LN_TILE = 256


def _res_ln1(x, a, vec):
    n, d = x.shape

    def body(x_ref, a_ref, v_ref, xh_ref, rs_ref, u_ref):
        r1 = DN_ALPHA * x_ref[...] + v_ref[0:1, :] * a_ref[...]
        xhat, rstd = _ln_stats(r1)
        xh_ref[...] = xhat
        rs_ref[...] = rstd
        x1 = xhat * v_ref[1:2, :] + v_ref[2:3, :]
        u_ref[...] = (x1 * (1.0 + v_ref[3:4, :]) + v_ref[4:5, :]).astype(BF16)

    row = pl.BlockSpec((LN_TILE, d), lambda i: (i, 0))
    return pl.pallas_call(
        body, name="res_ln1", grid=(n // LN_TILE,),
        in_specs=[row, row, pl.BlockSpec((8, d), lambda i: (0, 0))],
        out_specs=[row, pl.BlockSpec((LN_TILE, 1), lambda i: (i, 0)), row],
        out_shape=[jax.ShapeDtypeStruct((n, d), F32), jax.ShapeDtypeStruct((n, 1), F32),
                   jax.ShapeDtypeStruct((n, d), BF16)],
        compiler_params=_params(("parallel",)),
    )(x, a, vec)


def _res_ln2_loss(xhat1, f, target, vec):
    n, d = f.shape

    def body(xh_ref, f_ref, t_ref, v_ref, dr_ref, df_ref, s_ref):
        i = pl.program_id(0)
        x1 = xh_ref[...] * v_ref[1:2, :] + v_ref[2:3, :]
        fv = f_ref[...]
        xhat, rstd = _ln_stats(DN_ALPHA * x1 + v_ref[0:1, :] * fv)
        err = xhat * v_ref[3:4, :] + v_ref[4:5, :] - t_ref[...]
        dy = err * (1.0 / d)
        dr2 = _ln_bwd(dy, xhat, rstd, v_ref[3:4, :])
        dr_ref[...] = dr2
        df_ref[...] = (dr2 * v_ref[0:1, :]).astype(BF16)
        _accumulate_rows(s_ref, [_colsum(dy * xhat), _colsum(dy), _colsum(dr2 * fv),
                                 _colsum(err * err) * (0.5 / d)], i)

    row = pl.BlockSpec((LN_TILE, d), lambda i: (i, 0))
    return pl.pallas_call(
        body, name="res_ln2_loss", grid=(n // LN_TILE,),
        in_specs=[row, row, row, pl.BlockSpec((8, d), lambda i: (0, 0))],
        out_specs=[row, row, pl.BlockSpec((8, d), lambda i: (0, 0))],
        out_shape=[jax.ShapeDtypeStruct((n, d), F32), jax.ShapeDtypeStruct((n, d), BF16),
                   jax.ShapeDtypeStruct((8, d), F32)],
        compiler_params=_params(("arbitrary",)),
    )(xhat1, f, target, vec)


def _ln1_bwd(du2, dr2, xhat1, rstd1, a, vec):
    n, d = du2.shape

    def body(du_ref, dr2_ref, xh_ref, rs_ref, a_ref, v_ref, dxp_ref, da_ref, s_ref):
        i = pl.program_id(0)
        du, xhat = du_ref[...], xh_ref[...]
        x1 = xhat * v_ref[1:2, :] + v_ref[2:3, :]
        dx1 = DN_ALPHA * dr2_ref[...] + du * (1.0 + v_ref[0:1, :])
        dr1 = _ln_bwd(dx1, xhat, rs_ref[...], v_ref[1:2, :])
        dxp_ref[...] = DN_ALPHA * dr1
        da_ref[...] = (dr1 * v_ref[3:4, :]).astype(BF16)
        _accumulate_rows(s_ref, [_colsum(du * x1), _colsum(du), _colsum(dx1 * xhat), _colsum(dx1),
                                 _colsum(dr1 * a_ref[...])], i)

    row = pl.BlockSpec((LN_TILE, d), lambda i: (i, 0))
    return pl.pallas_call(
        body, name="ln1_bwd", grid=(n // LN_TILE,),
        in_specs=[row, row, row, pl.BlockSpec((LN_TILE, 1), lambda i: (i, 0)), row,
                  pl.BlockSpec((8, d), lambda i: (0, 0))],
        out_specs=[row, row, pl.BlockSpec((8, d), lambda i: (0, 0))],
        out_shape=[jax.ShapeDtypeStruct((n, d), F32), jax.ShapeDtypeStruct((n, d), BF16),
                   jax.ShapeDtypeStruct((8, d), F32)],
        compiler_params=_params(("arbitrary",)),
    )(du2, dr2, xhat1, rstd1, a, vec)


def _mod1_bwd(du_all, dxp, x, ctx, mods):
    n, d = x.shape
    nx = n // ROW_TILE

    def body(du_ref, dxp_ref, x_ref, ctx_ref, m_ref, gx_ref, s_ref):
        i = pl.program_id(0)
        du = du_ref[...]
        zero = jnp.zeros((1, d), F32)

        @pl.when(i == 0)
        def _():
            s_ref[...] = jnp.zeros_like(s_ref)

        @pl.when(i < nx)
        def _():
            gx_ref[...] = dxp_ref[...] + du * (1.0 + m_ref[0:1, :])
            s_ref[...] += jnp.concatenate([_colsum(du * x_ref[...]), _colsum(du)] + [zero] * 6, axis=0)

        @pl.when(i >= nx)
        def _():
            s_ref[...] += jnp.concatenate([zero, zero, _colsum(du * ctx_ref[...]), _colsum(du)] + [zero] * 4, axis=0)

    lat = pl.BlockSpec((ROW_TILE, d), lambda i: (jnp.minimum(i, nx - 1), 0))
    return pl.pallas_call(
        body, name="mod1_bwd", grid=(nx + 1,),
        in_specs=[pl.BlockSpec((ROW_TILE, d), lambda i: (i, 0)), lat, lat,
                  pl.BlockSpec((ROW_TILE, d), lambda i: (0, 0)), pl.BlockSpec((8, d), lambda i: (0, 0))],
        out_specs=[lat, pl.BlockSpec((8, d), lambda i: (0, 0))],
        out_shape=[jax.ShapeDtypeStruct((n, d), F32), jax.ShapeDtypeStruct((8, d), F32)],
        compiler_params=_params(("arbitrary",)),
    )(du_all, dxp, x, ctx, mods)


FFN_TM = 1024
FFN_TN = 512


def _gate_up(u2, wg, wu, after):
    n, d = u2.shape
    f = wg.shape[1]

    def body(u_ref, wg_ref, wu_ref, after_ref, g_ref, up_ref, h_ref):
        u = u_ref[...]
        g = lax.dot_general(u, wg_ref[...], NN, preferred_element_type=F32)
        up = lax.dot_general(u, wu_ref[...], NN, preferred_element_type=F32)
        g_ref[...] = g.astype(BF16)
        up_ref[...] = up.astype(BF16)
        h_ref[...] = (g * jax.nn.sigmoid(g) * up).astype(BF16)

    tm = min(FFN_TM, n)
    wspec = pl.BlockSpec((d, FFN_TN), lambda j, i: (0, j))
    ospec = pl.BlockSpec((tm, FFN_TN), lambda j, i: (i, j))
    return pl.pallas_call(
        body, name="gate_up", grid=(f // FFN_TN, n // tm),
        in_specs=[pl.BlockSpec((tm, d), lambda j, i: (i, 0)), wspec, wspec, pl.BlockSpec(memory_space=pl.ANY)],
        out_specs=[ospec, ospec, ospec],
        out_shape=[jax.ShapeDtypeStruct((n, f), BF16)] * 3,
        compiler_params=_params(("parallel", "parallel")),
    )(u2, wg, wu, after)


def _glu_bwd(df, wd, g, u):
    n, d = df.shape
    f = wd.shape[0]

    def body(df_ref, wd_ref, g_ref, u_ref, dg_ref, du_ref):
        dh = lax.dot_general(df_ref[...], wd_ref[...], NT, preferred_element_type=F32)
        gv = g_ref[...].astype(F32)
        sig = jax.nn.sigmoid(gv)
        du_ref[...] = (dh * (gv * sig)).astype(BF16)
        dg_ref[...] = (dh * u_ref[...].astype(F32) * (sig * (1.0 + gv * (1.0 - sig)))).astype(BF16)

    tm = min(FFN_TM, n)
    ospec = pl.BlockSpec((tm, FFN_TN), lambda i, j: (i, j))
    return pl.pallas_call(
        body, name="glu_bwd", grid=(n // tm, f // FFN_TN),
        in_specs=[pl.BlockSpec((tm, d), lambda i, j: (i, 0)),
                  pl.BlockSpec((FFN_TN, d), lambda i, j: (j, 0)), ospec, ospec],
        out_specs=[ospec, ospec],
        out_shape=[jax.ShapeDtypeStruct((n, f), BF16), jax.ShapeDtypeStruct((n, f), BF16)],
        compiler_params=_params(("parallel", "parallel")),
    )(df, wd, g, u)


def _du2(dg, du, wg, wu):
    n, f = dg.shape
    d = wg.shape[0]
    tm, tn, tk = min(1024, n), 1024, 1408
    nk = f // tk

    def body(dg_ref, du_ref, wg_ref, wu_ref, o_ref, acc_ref):
        kk = pl.program_id(2)
        part = (lax.dot_general(dg_ref[...], wg_ref[...], NT, preferred_element_type=F32)
                + lax.dot_general(du_ref[...], wu_ref[...], NT, preferred_element_type=F32))

        @pl.when(kk == 0)
        def _():
            acc_ref[...] = part

        @pl.when(kk != 0)
        def _():
            acc_ref[...] += part

        @pl.when(kk == nk - 1)
        def _():
            o_ref[...] = acc_ref[...]

    aspec = pl.BlockSpec((tm, tk), lambda i, j, kk: (i, kk))
    wspec = pl.BlockSpec((tn, tk), lambda i, j, kk: (j, kk))
    return pl.pallas_call(
        body, name="du2", grid=(n // tm, d // tn, nk),
        in_specs=[aspec, aspec, wspec, wspec],
        out_specs=pl.BlockSpec((tm, tn), lambda i, j, kk: (i, j)),
        out_shape=jax.ShapeDtypeStruct((n, d), F32),
        scratch_shapes=[pltpu.VMEM((tm, tn), F32)],
        compiler_params=_params(("parallel", "parallel", "arbitrary")),
    )(dg, du, wg, wu)


def _rows8(rows, d=D_MODEL):
    rows = [r.reshape(1, d).astype(F32) for r in rows]
    return jnp.concatenate(rows + [jnp.zeros((8 - len(rows), d), F32)], axis=0)


W_GROUPS = (("w_in",), ("w_out", "w_gate", "w_up"), ("w_down",))
G_GROUPS = (("w_down", "w_gate", "w_up"), ("w_out",), ("w_in",))


def _layer_fwd_bwd(x, ctx, target, mod, mod_ctx, weights, prefetch, grads_out,
                   q_g, k_g, sink, ln1_g, ln1_b, ln2_g, ln2_b):
    n, d = x.shape
    c = ctx.shape[0]
    sh1, sc1, g1, sh2, sc2, g2 = [mod[:, k * d:(k + 1) * d] for k in range(6)]
    csh1, csc1 = mod_ctx[:, 0:d], mod_ctx[:, d:2 * d]
    cos, sin_a, sin_b = _rope_tables(n, c)
    sink_b = jnp.broadcast_to(sink.reshape(HEADS_A, 1), (HEADS_A, HEAD_DIM)).astype(F32)

    u_all = _modulate_rows(x, ctx, _rows8([sc1, sh1, csc1, csh1]))
    (w_in,) = weights(0, u_all)
    h_all = _matmul(u_all, w_in, name="qkv_proj", tm=_fit(n + c, 1088), tn=1024, tk=2048, out_dtype=F32)
    qa, ka, va, qb, kb, vb = _qkv_post(h_all, cos, sin_a, sin_b, q_g, k_g)
    heads_b, lse = _attn_b_fwd(qb, kb, vb, n)
    zero = prefetch(1, heads_b)
    heads = _attn_a_fwd(qa, ka, va, sink_b + zero, heads_b, n)
    w_out, w_gate, w_up = weights(1, heads)
    a = _matmul(heads, w_out, name="out_proj", tm=1024, tn=1024, tk=2048, out_dtype=F32)
    xhat1, rstd1, u2 = _res_ln1(x, a, _rows8([g1, ln1_g, ln1_b, sc2, sh2]))
    zero = prefetch(2, u2)
    gg, uu, hh = _gate_up(u2, w_gate, w_up, zero.reshape(1, 1))
    (w_down,) = weights(2, hh)
    f = _matmul(hh, w_down, name="ffn_down", tm=1024, tn=512, tk=FFN, out_dtype=F32)
    dr2, df, s_ln2 = _res_ln2_loss(xhat1, f, target, _rows8([g2, ln1_g, ln1_b, ln2_g, ln2_b]))

    dgg, duu = _glu_bwd(df, w_down, gg, uu)
    dw_down = _matmul(hh, df, name="dw_down", ta=True, tm=512, tn=1024, tk=n, out_dtype=BF16)
    dw_gate = _matmul(u2, dgg, name="dw_gate", ta=True, tm=1024, tn=512, tk=n, out_dtype=BF16)
    dw_up = _matmul(u2, duu, name="dw_up", ta=True, tm=1024, tn=512, tk=n, out_dtype=BF16)
    zero = grads_out(0, [dw_down, dw_gate, dw_up])
    du2 = _du2(dgg, duu, w_gate, w_up)
    dxp, da, s_ln1 = _ln1_bwd(du2, dr2, xhat1, rstd1, a, _rows8([sc2, ln1_g, ln1_b, g1]) + zero)

    dheads = _matmul(da, w_out, name="d_heads", tb=True, tm=1024, tn=1024, tk=2048, out_dtype=BF16)
    dw_out = _matmul(heads, da, name="dw_out", ta=True, tm=1024, tn=1024, tk=n, out_dtype=BF16)
    zero = grads_out(1, [dw_out])
    delta = _delta_rows(dheads, heads)
    dqa, dka, dva, dsink = _attn_a_bwd(qa, ka, va, sink_b + zero, dheads, n)
    dqb, dkb, dvb = _attn_b_bwd(qb, kb, vb, dheads, lse, delta, n)
    dh_all, s_gain = _qkv_bwd_post(h_all, cos, sin_a, sin_b, q_g, k_g, dqa, dka, dva, dqb, dkb, dvb, n)
    dw_in = _matmul(u_all, dh_all, name="dw_in", ta=True, tm=1024, tn=1024, tk=n + c, out_dtype=BF16)
    zero = grads_out(2, [dw_in])
    du_all = _matmul(dh_all, w_in, name="d_u1", tb=True, tm=_fit(n + c, 1088), tn=1024, tk=IN_WIDTH, out_dtype=F32,
                     after=zero.reshape(1, 1))
    grad_x, s_mod1 = _mod1_bwd(du_all, dxp, x, ctx, _rows8([sc1]) + zero)

    dsink_row = jnp.concatenate([dsink[0, 0:GROUP, 0], dsink[1, 0:GROUP, 0]]).reshape(1, HEADS_A)
    misc = jnp.concatenate([s_gain[0:1], s_gain[1:2], dsink_row,
                            jnp.zeros((1, d - 2 * HEAD_DIM - HEADS_A), F32)], axis=1)
    partial = jnp.concatenate([
        s_mod1[1:2], s_mod1[0:1], s_ln1[4:5],
        s_ln1[1:2], s_ln1[0:1], s_ln2[2:3],
        s_mod1[3:4], s_mod1[2:3],
        s_ln1[2:3], s_ln1[3:4], s_ln2[0:1], s_ln2[1:2],
        s_ln2[3:4], misc, jnp.zeros((2, d), F32)], axis=0)
    return grad_x, partial


ANY = pl.BlockSpec(memory_space=pl.ANY)
VMEM_FULL = pl.BlockSpec(memory_space=pltpu.VMEM)
N_CHIP_PEERS = 3


def _me():
    return lax.axis_index("x"), lax.axis_index("y"), lax.axis_index("c")


def _other_chips(x, y):
    return [(1 - x, y), (x, 1 - y), (1 - x, 1 - y)]


def _shard_of(chip):
    return 2 * chip[0] + chip[1]


def _dev_index(x, y, c):
    return 4 * x + 2 * y + c


def _rcopy(src, dst, send_sems, recv_sems, k, dev):
    return pltpu.make_async_remote_copy(src_ref=src, dst_ref=dst, send_sem=send_sems.at[k], recv_sem=recv_sems.at[k],
                                        device_id=dev, device_id_type=MESH)


BIG = (("w_in", (D_MODEL, IN_WIDTH), 1), ("w_out", (MIX_WIDTH, D_MODEL), 0), ("w_gate", (D_MODEL, FFN), 1),
       ("w_up", (D_MODEL, FFN), 1), ("w_down", (FFN, D_MODEL), 0))


def _sub(ref, axis, idx, size):
    start = pl.multiple_of(idx * size, size)
    return ref.at[pl.ds(start, size), :] if axis == 0 else ref.at[:, pl.ds(start, size)]


def _shape_div(shape, axis, parts):
    return tuple(s // parts if a == axis else s for a, s in enumerate(shape))


def _piece(a, ref, shard, half):
    _, full, axis = BIG[a]
    view = _sub(ref, axis, shard, full[axis] // N_SHARD)
    return _sub(view, 1 - axis, half, full[1 - axis] // 2)


HBM = pl.BlockSpec(memory_space=pltpu.HBM)
SEM = pl.BlockSpec(memory_space=pltpu.SEMAPHORE)
EFFECT = pltpu.SideEffectType.DATAFLOW_SIDE_EFFECTING
BIG_INDEX = {name: a for a, (name, _, _) in enumerate(BIG)}


def _in_hbm(arr):
    return pltpu.with_memory_space_constraint(arr, pltpu.HBM)


def _gather_start(tag, arrs, bufs, prev):
    n_arr = len(arrs)

    def body(*refs):
        ins = refs[:n_arr]
        send_sems, recv_sems = refs[n_arr + 1], refs[n_arr + 2]
        token = refs[-1]
        x, y, c = _me()
        s_me = _shard_of((x, y))
        for i, a in enumerate(arrs):
            mine = _piece(a, ins[i], s_me, c)
            for j, chip in enumerate(_other_chips(x, y)):
                _rcopy(mine, mine, send_sems, recv_sems, N_CHIP_PEERS * i + j, (*chip, c)).start()
        token[...] = jnp.zeros_like(token)

    n_sem = N_CHIP_PEERS * n_arr
    outs = pl.pallas_call(
        body, name="gather_start_" + tag,
        in_specs=[HBM] * n_arr + [ANY],
        out_specs=[SEM, SEM] + [HBM] * n_arr + [VMEM_FULL],
        out_shape=[pltpu.SemaphoreType.DMA((n_sem,)), pltpu.SemaphoreType.DMA((n_sem,))]
        + [pltpu.HBM(BIG[a][1], BF16) for a in arrs] + [jax.ShapeDtypeStruct((8, HEAD_DIM), F32)],
        input_output_aliases={i: 2 + i for i in range(n_arr)},
        compiler_params=pltpu.CompilerParams(has_side_effects=EFFECT),
    )(*[_in_hbm(b) for b in bufs], prev)
    return outs[0], outs[1], list(outs[2:2 + n_arr]), outs[-1]


def _gather_wait(tag, arrs, send_sems, recv_sems, bufs, after):
    n_arr = len(arrs)

    def body(*refs):
        ins = refs[:n_arr]
        send_sems_, recv_sems_ = refs[n_arr], refs[n_arr + 1]
        x, y, c = _me()
        s_me = _shard_of((x, y))
        for i, a in enumerate(arrs):
            mine = _piece(a, ins[i], s_me, c)
            for j, chip in enumerate(_other_chips(x, y)):
                landed = _piece(a, ins[i], _shard_of(chip), c)
                cp = _rcopy(mine, landed, send_sems_, recv_sems_, N_CHIP_PEERS * i + j, (*chip, c))
                cp.wait_send()
                cp.wait_recv()

    outs = pl.pallas_call(
        body, name="gather_wait_" + tag,
        in_specs=[HBM] * n_arr + [SEM, SEM, ANY],
        out_specs=[HBM] * n_arr,
        out_shape=[pltpu.HBM(BIG[a][1], BF16) for a in arrs],
        input_output_aliases={i: i for i in range(n_arr)},
        compiler_params=pltpu.CompilerParams(has_side_effects=EFFECT),
    )(*bufs, send_sems, recv_sems, after)
    return list(outs)


def _gather_forward(tag, arrs, bufs):
    n_arr = len(arrs)

    def body(*refs):
        outs = refs[n_arr:2 * n_arr]
        send_sems, recv_sems = refs[2 * n_arr:]
        x, y, c = _me()
        sibling = (x, y, 1 - c)
        chips = _other_chips(x, y)
        copies = []
        for i, a in enumerate(arrs):
            for j, chip in enumerate(chips):
                landed = _piece(a, outs[i], _shard_of(chip), c)
                cp = _rcopy(landed, landed, send_sems, recv_sems, N_CHIP_PEERS * i + j, sibling)
                cp.start()
                copies.append(cp)
        for i, a in enumerate(arrs):
            for j, chip in enumerate(chips):
                other = _piece(a, outs[i], _shard_of(chip), 1 - c)
                _rcopy(other, other, send_sems, recv_sems, N_CHIP_PEERS * i + j, sibling).wait_recv()
        for cp in copies:
            cp.wait_send()

    n_sem = N_CHIP_PEERS * n_arr
    return list(pl.pallas_call(
        body, name="gather_forward_" + tag,
        in_specs=[ANY] * n_arr, out_specs=[ANY] * n_arr,
        out_shape=[jax.ShapeDtypeStruct(BIG[a][1], BF16) for a in arrs],
        input_output_aliases={i: i for i in range(n_arr)},
        scratch_shapes=[pltpu.SemaphoreType.DMA((n_sem,)), pltpu.SemaphoreType.DMA((n_sem,))],
    )(*bufs))


def _forward_start(tag, arrs, bufs):
    n_arr = len(arrs)

    def body(*refs):
        ins = refs[:n_arr]
        send_sems, recv_sems = refs[n_arr], refs[n_arr + 1]
        token = refs[-1]
        x, y, c = _me()
        for i, a in enumerate(arrs):
            for j, chip in enumerate(_other_chips(x, y)):
                landed = _piece(a, ins[i], _shard_of(chip), c)
                _rcopy(landed, landed, send_sems, recv_sems, N_CHIP_PEERS * i + j, (x, y, 1 - c)).start()
        token[...] = jnp.zeros_like(token)

    n_sem = N_CHIP_PEERS * n_arr
    outs = pl.pallas_call(
        body, name="gather_forward_start_" + tag,
        in_specs=[HBM] * n_arr,
        out_specs=[SEM, SEM] + [HBM] * n_arr + [VMEM_FULL],
        out_shape=[pltpu.SemaphoreType.DMA((n_sem,)), pltpu.SemaphoreType.DMA((n_sem,))]
        + [pltpu.HBM(BIG[a][1], BF16) for a in arrs] + [jax.ShapeDtypeStruct((8, HEAD_DIM), F32)],
        input_output_aliases={i: 2 + i for i in range(n_arr)},
        compiler_params=pltpu.CompilerParams(has_side_effects=EFFECT),
    )(*bufs)
    return outs[0], outs[1], list(outs[2:2 + n_arr]), outs[-1]


def _forward_wait(tag, arrs, send_sems, recv_sems, bufs, after):
    n_arr = len(arrs)

    def body(*refs):
        ins = refs[:n_arr]
        send_sems_, recv_sems_ = refs[n_arr], refs[n_arr + 1]
        x, y, c = _me()
        for i, a in enumerate(arrs):
            for j, chip in enumerate(_other_chips(x, y)):
                mine = _piece(a, ins[i], _shard_of(chip), c)
                other = _piece(a, ins[i], _shard_of(chip), 1 - c)
                cp = _rcopy(mine, other, send_sems_, recv_sems_, N_CHIP_PEERS * i + j, (x, y, 1 - c))
                cp.wait_send()
                cp.wait_recv()

    outs = pl.pallas_call(
        body, name="gather_forward_wait_" + tag,
        in_specs=[HBM] * n_arr + [SEM, SEM, ANY],
        out_specs=[HBM] * n_arr,
        out_shape=[pltpu.HBM(BIG[a][1], BF16) for a in arrs],
        input_output_aliases={i: i for i in range(n_arr)},
        compiler_params=pltpu.CompilerParams(has_side_effects=EFFECT),
    )(*bufs, send_sems, recv_sems, after)
    return list(outs)


def _peers(x, y, c):
    return [(x ^ (mask >> 2), y ^ ((mask >> 1) & 1), c ^ (mask & 1)) for mask in range(1, N_DEV)]


def _received_shape(a):
    _, full, axis = BIG[a]
    return (N_DEV - 1,) + _shape_div(_shape_div(full, 1 - axis, 2), axis, N_SHARD)


def _pieces_start(tag, arrs, dws):
    n_arr = len(arrs)

    def body(*refs):
        srcs, lands = refs[:n_arr], refs[n_arr:2 * n_arr]
        send_sems, recv_sems = refs[2 * n_arr], refs[2 * n_arr + 1]
        token = refs[-1]
        x, y, c = _me()
        for i, a in enumerate(arrs):
            for k, peer in enumerate(_peers(x, y, c)):
                src = _piece(a, srcs[i], _shard_of(peer[:2]), peer[2])
                _rcopy(src, lands[i].at[k], send_sems, recv_sems, (N_DEV - 1) * i + k, peer).start()
        token[...] = jnp.zeros_like(token)

    n_sem = (N_DEV - 1) * n_arr
    lands = [_in_hbm(lax.empty(_received_shape(a), BF16)) for a in arrs]
    outs = pl.pallas_call(
        body, name="grad_pieces_start_" + tag,
        in_specs=[HBM] * (2 * n_arr),
        out_specs=[SEM, SEM] + [HBM] * (2 * n_arr) + [VMEM_FULL],
        out_shape=[pltpu.SemaphoreType.DMA((n_sem,)), pltpu.SemaphoreType.DMA((n_sem,))]
        + [pltpu.HBM(BIG[a][1], BF16) for a in arrs] + [pltpu.HBM(_received_shape(a), BF16) for a in arrs]
        + [jax.ShapeDtypeStruct((8, HEAD_DIM), F32)],
        input_output_aliases={i: 2 + i for i in range(2 * n_arr)},
        compiler_params=pltpu.CompilerParams(has_side_effects=EFFECT),
    )(*[_in_hbm(dw) for dw in dws], *lands)
    return outs[0], outs[1], list(outs[2:2 + n_arr]), list(outs[2 + n_arr:2 + 2 * n_arr]), outs[-1]


def _pieces_wait(tag, arrs, send_sems, recv_sems, dws, lands, after):
    n_arr = len(arrs)

    def body(*refs):
        srcs, lands_ = refs[:n_arr], refs[n_arr:2 * n_arr]
        send_sems_, recv_sems_ = refs[2 * n_arr], refs[2 * n_arr + 1]
        x, y, c = _me()
        for i, a in enumerate(arrs):
            for k, peer in enumerate(_peers(x, y, c)):
                src = _piece(a, srcs[i], _shard_of(peer[:2]), peer[2])
                cp = _rcopy(src, lands_[i].at[k], send_sems_, recv_sems_, (N_DEV - 1) * i + k, peer)
                cp.wait_send()
                cp.wait_recv()

    outs = pl.pallas_call(
        body, name="grad_pieces_wait_" + tag,
        in_specs=[HBM] * (2 * n_arr) + [SEM, SEM, ANY],
        out_specs=[HBM] * (2 * n_arr),
        out_shape=[pltpu.HBM(BIG[a][1], BF16) for a in arrs] + [pltpu.HBM(_received_shape(a), BF16) for a in arrs],
        input_output_aliases={i: i for i in range(2 * n_arr)},
        compiler_params=pltpu.CompilerParams(has_side_effects=EFFECT),
    )(*dws, *lands, send_sems, recv_sems, after)
    return list(outs[:n_arr]), list(outs[n_arr:])


def _join_start(tag, g_halves):
    n_arr = len(g_halves)

    def body(*refs):
        srcs, lands = refs[:n_arr], refs[n_arr:2 * n_arr]
        send_sems, recv_sems = refs[2 * n_arr], refs[2 * n_arr + 1]
        token = refs[-1]
        x, y, c = _me()
        for i in range(n_arr):
            _rcopy(srcs[i], lands[i], send_sems, recv_sems, i, (x, y, 1 - c)).start()
        token[...] = jnp.zeros_like(token)

    shapes = [pltpu.HBM(g.shape, F32) for g in g_halves]
    outs = pl.pallas_call(
        body, name="grad_join_start_" + tag,
        in_specs=[HBM] * (2 * n_arr),
        out_specs=[SEM, SEM] + [HBM] * (2 * n_arr) + [VMEM_FULL],
        out_shape=[pltpu.SemaphoreType.DMA((n_arr,)), pltpu.SemaphoreType.DMA((n_arr,))] + shapes + shapes
        + [jax.ShapeDtypeStruct((8, HEAD_DIM), F32)],
        input_output_aliases={i: 2 + i for i in range(2 * n_arr)},
        compiler_params=pltpu.CompilerParams(has_side_effects=EFFECT),
    )(*[_in_hbm(g) for g in g_halves], *[_in_hbm(lax.empty(g.shape, F32)) for g in g_halves])
    return outs[0], outs[1], list(outs[2:2 + n_arr]), list(outs[2 + n_arr:2 + 2 * n_arr]), outs[-1]


def _join_wait(tag, send_sems, recv_sems, g_halves, lands, after):
    n_arr = len(g_halves)

    def body(*refs):
        srcs, lands_ = refs[:n_arr], refs[n_arr:2 * n_arr]
        send_sems_, recv_sems_ = refs[2 * n_arr], refs[2 * n_arr + 1]
        x, y, c = _me()
        for i in range(n_arr):
            cp = _rcopy(srcs[i], lands_[i], send_sems_, recv_sems_, i, (x, y, 1 - c))
            cp.wait_send()
            cp.wait_recv()

    shapes = [pltpu.HBM(g.shape, F32) for g in g_halves]
    outs = pl.pallas_call(
        body, name="grad_join_wait_" + tag,
        in_specs=[HBM] * (2 * n_arr) + [SEM, SEM, ANY],
        out_specs=[HBM] * (2 * n_arr),
        out_shape=shapes + shapes,
        input_output_aliases={i: i for i in range(2 * n_arr)},
        compiler_params=pltpu.CompilerParams(has_side_effects=EFFECT),
    )(*g_halves, *lands, send_sems, recv_sems, after)
    return list(outs[:n_arr]), list(outs[n_arr:])


def _piece_sum(a, dw, shard, core, received):
    name, full, axis = BIG[a]
    rows, cols = _received_shape(a)[1:]
    tr = _fit(rows, ROW_TILE)
    nbr = rows // tr

    def body(w_ref, dw_ref, rec_ref, o_ref):
        acc = dw_ref[...].astype(F32)
        for k in range(N_DEV - 1):
            acc = acc + rec_ref[k].astype(F32)
        o_ref[...] = acc

    if axis == 0:
        own = pl.BlockSpec((tr, cols), lambda i, w: (w[0] * nbr + i, w[1]))
    else:
        own = pl.BlockSpec((tr, cols), lambda i, w: (w[1] * nbr + i, w[0]))
    return pl.pallas_call(
        body, name="grad_sum_pieces_" + name,
        grid_spec=pltpu.PrefetchScalarGridSpec(
            num_scalar_prefetch=1, grid=(nbr,),
            in_specs=[own, pl.BlockSpec((N_DEV - 1, tr, cols), lambda i, w: (0, i, 0))],
            out_specs=pl.BlockSpec((tr, cols), lambda i, w: (i, 0))),
        out_shape=jax.ShapeDtypeStruct((rows, cols), F32),
        compiler_params=_params(("parallel",)),
    )(jnp.stack([shard, core]).astype(jnp.int32), dw, received)


def _scatter_begin(tag, arrs, dws):
    send_sems, recv_sems, dws, lands, token = _pieces_start(tag, arrs, dws)
    return (send_sems, recv_sems, dws, lands), token[0, 0]


def _scatter_reduce(tag, arrs, state, after):
    x, y, c = _me()
    send_sems, recv_sems, dws, lands = state
    dws, lands = _pieces_wait(tag, arrs, send_sems, recv_sems, dws, lands, after)
    g_own = [_piece_sum(a, dw, _shard_of((x, y)), c, r) for a, dw, r in zip(arrs, dws, lands)]
    send_sems, recv_sems, g_own, lands, token = _join_start(tag, g_own)
    return (send_sems, recv_sems, g_own, lands), token


def _scatter_end(tag, state, after):
    return _join_wait(tag, *state, after)


def _gather_rows(block, name, after):
    r, d = block.shape

    def body(in_ref, after_ref, out_ref, send_sems, recv_sems):
        x, y, c = _me()
        out_ref[_dev_index(x, y, c)] = in_ref[...]
        copies = []
        for mask in range(1, N_DEV):
            peer = (x ^ (mask >> 2), y ^ ((mask >> 1) & 1), c ^ (mask & 1))
            cp = _rcopy(in_ref, out_ref.at[_dev_index(x, y, c)], send_sems, recv_sems, mask - 1, peer)
            cp.start()
            copies.append((cp, peer))
        for mask in range(1, N_DEV):
            peer = (x ^ (mask >> 2), y ^ ((mask >> 1) & 1), c ^ (mask & 1))
            landed = out_ref.at[_dev_index(*peer)]
            _rcopy(landed, landed, send_sems, recv_sems, mask - 1, peer).wait_recv()
        for cp, _ in copies:
            cp.wait_send()

    return pl.pallas_call(
        body, name=name, in_specs=[VMEM_FULL, ANY], out_specs=VMEM_FULL,
        out_shape=jax.ShapeDtypeStruct((N_DEV, r, d), F32),
        scratch_shapes=[pltpu.SemaphoreType.DMA((N_DEV - 1,)), pltpu.SemaphoreType.DMA((N_DEV - 1,))],
    )(block, after)


ADA_ROWS = 80
ADA_W = 6 * D_MODEL // N_SHARD


def _ada_forward(c_block, cctx_block, w_ada, b_shard):
    d = c_block.shape[1]

    def body(c_ref, cc_ref, w_ref, b_ref, act_ref, mods_ref, raw, mloc, send_sems, recv_sems):
        x, y, c = _me()
        me = _dev_index(x, y, c)
        s_me = _shard_of((x, y))
        raw[72:ADA_ROWS, :] = jnp.zeros((ADA_ROWS - 72, d), F32)
        raw[pl.ds(pl.multiple_of(me * 8, 8), 8), :] = c_ref[...]
        raw[64:72, :] = cc_ref[...]
        sends = []
        for mask in range(1, N_DEV):
            peer = (x ^ (mask >> 2), y ^ ((mask >> 1) & 1), c ^ (mask & 1))
            cp = _rcopy(c_ref, raw.at[pl.ds(pl.multiple_of(me * 8, 8), 8), :], send_sems, recv_sems, mask - 1, peer)
            cp.start()
            sends.append(cp)
        for mask in range(1, N_DEV):
            peer = (x ^ (mask >> 2), y ^ ((mask >> 1) & 1), c ^ (mask & 1))
            landed = raw.at[pl.ds(pl.multiple_of(_dev_index(*peer) * 8, 8), 8), :]
            _rcopy(landed, landed, send_sems, recv_sems, mask - 1, peer).wait_recv()
        v = raw[...]
        act = v * jax.nn.sigmoid(v)
        act_ref[...] = act
        mloc[...] = lax.dot_general(act.astype(BF16), w_ref[...].astype(BF16), NN,
                                    preferred_element_type=F32) + b_ref[...]
        mods_ref[s_me, 0:8, :] = mloc[pl.ds(pl.multiple_of(me * 8, 8), 8), :]
        mods_ref[s_me, 8:16, :] = mloc[64:72, :]
        base = N_DEV - 1
        for j, chip in enumerate(_other_chips(x, y)):
            peer = (*chip, c)
            rows = mloc.at[pl.ds(pl.multiple_of(_dev_index(*peer) * 8, 8), 8), :]
            cp = _rcopy(rows, mods_ref.at[s_me, 0:8, :], send_sems, recv_sems, base + 2 * j, peer)
            cp.start()
            sends.append(cp)
            cp = _rcopy(mloc.at[64:72, :], mods_ref.at[s_me, 8:16, :], send_sems, recv_sems, base + 2 * j + 1, peer)
            cp.start()
            sends.append(cp)
        for j, chip in enumerate(_other_chips(x, y)):
            for part in range(2):
                landed = mods_ref.at[_shard_of(chip), 8 * part:8 * part + 8, :]
                _rcopy(landed, landed, send_sems, recv_sems, base + 2 * j + part, (*chip, c)).wait_recv()
        for cp in sends:
            cp.wait_send()

    n_sem = N_DEV - 1 + 2 * N_CHIP_PEERS
    return pl.pallas_call(
        body, name="ada_forward",
        in_specs=[VMEM_FULL] * 4, out_specs=[VMEM_FULL, VMEM_FULL],
        out_shape=[jax.ShapeDtypeStruct((ADA_ROWS, d), F32), jax.ShapeDtypeStruct((N_SHARD, 16, ADA_W), F32)],
        scratch_shapes=[pltpu.VMEM((ADA_ROWS, d), F32), pltpu.VMEM((ADA_ROWS, ADA_W), F32),
                        pltpu.SemaphoreType.DMA((n_sem,)), pltpu.SemaphoreType.DMA((n_sem,))],
        compiler_params=pltpu.CompilerParams(vmem_limit_bytes=VMEM_LIMIT),
    )(c_block, cctx_block, w_ada, b_shard)


def _small_reduce(gathered):
    d = gathered.shape[2]

    def body(g_ref, o_ref):
        tot = g_ref[0]
        for i in range(1, N_DEV):
            tot = tot + g_ref[i]
        o_ref[...] = tot
        o_ref[0:2, :] = tot[0:2] + tot[6:8]
        o_ref[12:13, :] = jnp.broadcast_to(jnp.sum(tot[12:13], axis=1, keepdims=True), (1, d))

    return pl.pallas_call(body, name="small_reduce", in_specs=[VMEM_FULL], out_specs=VMEM_FULL,
                          out_shape=jax.ShapeDtypeStruct((16, d), F32))(gathered)


def _cctx_grad(gathered, c_ctx):
    d = gathered.shape[2]

    def body(g_ref, c_ref, o_ref):
        tot = g_ref[0, 0:1, :]
        for chip in range(1, N_SHARD):
            tot = tot + g_ref[2 * chip, 0:1, :]
        v = c_ref[...]
        sig = jax.nn.sigmoid(v)
        o_ref[...] = tot * (sig * (1.0 + v * (1.0 - sig)))

    return pl.pallas_call(body, name="cctx_grad", in_specs=[VMEM_FULL, VMEM_FULL], out_specs=VMEM_FULL,
                          out_shape=jax.ShapeDtypeStruct((1, d), F32))(gathered, c_ctx.reshape(1, d))


def _cast_into_full(w, shard, full, axis, name):
    r, cdim = w.shape
    tr = _fit(r, ROW_TILE)
    nbr = r // tr

    def body(s_ref, w_ref, o_ref):
        o_ref[...] = w_ref[...].astype(BF16)

    if axis == 0:
        out_spec = pl.BlockSpec((tr, cdim), lambda i, s: (s[0] * nbr + i, 0))
    else:
        out_spec = pl.BlockSpec((tr, cdim), lambda i, s: (i, s[0]))
    return pl.pallas_call(
        body, name=name,
        grid_spec=pltpu.PrefetchScalarGridSpec(
            num_scalar_prefetch=1, grid=(nbr,), in_specs=[pl.BlockSpec((tr, cdim), lambda i, s: (i, 0))],
            out_specs=out_spec),
        out_shape=jax.ShapeDtypeStruct(full, BF16), compiler_params=_params(("parallel",)),
    )(shard.reshape(1).astype(jnp.int32), w)


def _adamw_halves(w, g_own, g_other, m, v, core, axis, name):
    r, cdim = w.shape
    hr, hc = (r // 2, cdim) if axis == 1 else (r, cdim // 2)
    assert g_own.shape == (hr, hc) and g_other.shape == (hr, hc)
    tr = _fit(hr, 256)
    nb = hr // tr
    c1 = 1.0 - ADAM_B1 ** ADAM_STEP
    c2 = 1.0 - ADAM_B2 ** ADAM_STEP

    def body(c_ref, w_ref, go_ref, gt_ref, m_ref, v_ref, g_ref, d_ref, nm_ref, nv_ref):
        gv = jnp.where(pl.program_id(0) == c_ref[0], go_ref[...], gt_ref[...])
        nm = ADAM_B1 * m_ref[...] + (1.0 - ADAM_B1) * gv
        nv = ADAM_B2 * v_ref[...] + (1.0 - ADAM_B2) * (gv * gv)
        g_ref[...] = gv
        nm_ref[...] = nm
        nv_ref[...] = nv
        d_ref[...] = -ADAM_LR * ((nm / c1) / (jnp.sqrt(nv / c2) + ADAM_EPS) + ADAM_WD * w_ref[...])

    if axis == 1:
        big = pl.BlockSpec((tr, hc), lambda p, i, c: (p * nb + i, 0))
    else:
        big = pl.BlockSpec((tr, hc), lambda p, i, c: (i, p))
    half = pl.BlockSpec((tr, hc), lambda p, i, c: (i, 0))
    sh = jax.ShapeDtypeStruct((r, cdim), F32)
    return pl.pallas_call(
        body, name=name,
        grid_spec=pltpu.PrefetchScalarGridSpec(
            num_scalar_prefetch=1, grid=(2, nb), in_specs=[big, half, half, big, big], out_specs=[big] * 4),
        out_shape=[sh] * 4, compiler_params=_params(("parallel", "parallel")),
    )(core.reshape(1).astype(jnp.int32), w, g_own, g_other, m, v)


def _adamw(w, g, m, v, name):
    r, cdim = w.shape
    tr = _fit(r, 128) if r % (ROW_TILE // 4) == 0 else r
    c1 = 1.0 - ADAM_B1 ** ADAM_STEP
    c2 = 1.0 - ADAM_B2 ** ADAM_STEP

    def body(w_ref, g_ref, m_ref, v_ref, d_ref, nm_ref, nv_ref):
        gv = g_ref[...]
        nm = ADAM_B1 * m_ref[...] + (1.0 - ADAM_B1) * gv
        nv = ADAM_B2 * v_ref[...] + (1.0 - ADAM_B2) * (gv * gv)
        nm_ref[...] = nm
        nv_ref[...] = nv
        d_ref[...] = -ADAM_LR * ((nm / c1) / (jnp.sqrt(nv / c2) + ADAM_EPS) + ADAM_WD * w_ref[...])

    spec = pl.BlockSpec((tr, cdim), lambda i: (i, 0))
    sh = jax.ShapeDtypeStruct((r, cdim), F32)
    return pl.pallas_call(body, name=name, grid=(r // tr,), in_specs=[spec] * 4, out_specs=[spec] * 3,
                          out_shape=[sh, sh, sh], compiler_params=_params(("parallel",)))(w, g, m, v)


SMALL = (("c_ctx", D_MODEL), ("b_ada", 6 * D_MODEL), ("q_norm_g", HEAD_DIM), ("k_norm_g", HEAD_DIM),
         ("sink_logit", HEADS_A), ("ln1_g", D_MODEL), ("ln1_b", D_MODEL), ("ln2_g", D_MODEL), ("ln2_b", D_MODEL))
WEIGHT_ORDER = ("c_ctx", "w_ada", "b_ada", "w_in", "q_norm_g", "k_norm_g", "sink_logit", "w_out", "ln1_g", "ln1_b",
                "w_gate", "w_up", "w_down", "ln2_g", "ln2_b")


def kernel(x, c, ctx, c_ctx, w_ada, b_ada, w_in, q_norm_g, k_norm_g, sink_logit, w_out, ln1_g, ln1_b, w_gate, w_up, w_down, ln2_g, ln2_b, loss_target, m_c_ctx, m_w_ada, m_b_ada, m_w_in, m_q_norm_g, m_k_norm_g, m_sink_logit, m_w_out, m_ln1_g, m_ln1_b, m_w_gate, m_w_up, m_w_down, m_ln2_g, m_ln2_b, v_c_ctx, v_w_ada, v_b_ada, v_w_in, v_q_norm_g, v_k_norm_g, v_sink_logit, v_w_out, v_ln1_g, v_ln1_b, v_w_gate, v_w_up, v_w_down, v_ln2_g, v_ln2_b):
    d = D_MODEL
    w = dict(c_ctx=c_ctx, w_ada=w_ada[0], b_ada=b_ada, w_in=w_in[0], q_norm_g=q_norm_g, k_norm_g=k_norm_g,
             sink_logit=sink_logit, w_out=w_out[0], ln1_g=ln1_g, ln1_b=ln1_b, w_gate=w_gate[0], w_up=w_up[0],
             w_down=w_down[0], ln2_g=ln2_g, ln2_b=ln2_b)
    m = dict(c_ctx=m_c_ctx, w_ada=m_w_ada[0], b_ada=m_b_ada, w_in=m_w_in[0], q_norm_g=m_q_norm_g, k_norm_g=m_k_norm_g,
             sink_logit=m_sink_logit, w_out=m_w_out[0], ln1_g=m_ln1_g, ln1_b=m_ln1_b, w_gate=m_w_gate[0],
             w_up=m_w_up[0], w_down=m_w_down[0], ln2_g=m_ln2_g, ln2_b=m_ln2_b)
    v = dict(c_ctx=v_c_ctx, w_ada=v_w_ada[0], b_ada=v_b_ada, w_in=v_w_in[0], q_norm_g=v_q_norm_g, k_norm_g=v_k_norm_g,
             sink_logit=v_sink_logit, w_out=v_w_out[0], ln1_g=v_ln1_g, ln1_b=v_ln1_b, w_gate=v_w_gate[0],
             w_up=v_w_up[0], w_down=v_w_down[0], ln2_g=v_ln2_g, ln2_b=v_ln2_b)
    mx, my, mc = _me()
    s_me = _shard_of((mx, my))
    me = _dev_index(mx, my, mc)
    pad8 = lambda row: jnp.concatenate([row.reshape(1, -1), jnp.zeros((7, row.size), F32)], axis=0)

    b_shard = lax.dynamic_slice(b_ada, (0, s_me * ADA_W), (1, ADA_W))
    act, mods4 = _ada_forward(pad8(c), pad8(c_ctx), w["w_ada"], b_shard)

    gathers = []
    prev, shard = mods4, s_me
    for k, names in enumerate(W_GROUPS):
        arrs = tuple(BIG_INDEX[name] for name in names)
        bufs = [_cast_into_full(w[name], shard, BIG[a][1], BIG[a][2], "cast_" + name) for name, a in zip(names, arrs)]
        send_sems, recv_sems, thru, prev = _gather_start("g%d" % k, arrs, bufs, prev)
        shard = s_me + prev[0, 0].astype(jnp.int32)
        gathers.append((arrs, send_sems, recv_sems, thru))

    forwards = {}

    def prefetch(k, after):
        arrs, send_sems, recv_sems, thru = gathers[k]
        landed = _gather_wait("g%d" % k, arrs, send_sems, recv_sems, thru, after)
        fwd_send, fwd_recv, landed, token = _forward_start("g%d" % k, arrs, landed)
        forwards[k] = (fwd_send, fwd_recv, landed)
        return token[0, 0]

    def weights(k, after):
        arrs, send_sems, recv_sems, thru = gathers[k]
        if k in forwards:
            return _forward_wait("g%d" % k, arrs, *forwards[k], after)
        landed = _gather_wait("g%d" % k, arrs, send_sems, recv_sems, thru, after)
        return _gather_forward("g%d" % k, arrs, landed)

    mod = jnp.transpose(mods4[:, 0:1, :], (1, 0, 2)).reshape(1, 6 * d) + prev[0, 0]
    mod_ctx = jnp.transpose(mods4[:, 8:9, :], (1, 0, 2)).reshape(1, 6 * d)

    scatters = {}

    def grads_out(k, dws):
        arrs = tuple(BIG_INDEX[name] for name in G_GROUPS[k])
        scatters[k], zero = _scatter_begin("g%d" % k, arrs, dws)
        return zero

    grad_x, partial = _layer_fwd_bwd(x[0], ctx[0], loss_target[0], mod, mod_ctx, weights, prefetch, grads_out,
                                     q_norm_g, k_norm_g, sink_logit, ln1_g, ln1_b, ln2_g, ln2_b)
    grads, delta, new_m, new_v = {}, {}, {}, {}

    after, joins = partial, []
    for k, names in enumerate(G_GROUPS):
        arrs = tuple(BIG_INDEX[name] for name in names)
        state, after = _scatter_reduce("g%d" % k, arrs, scatters[k], after)
        joins.append(state)
    for k, names in enumerate(G_GROUPS):
        g_own, g_other = _scatter_end("g%d" % k, joins[k], after)
        for name, own, other in zip(names, g_own, g_other):
            grads[name], delta[name], new_m[name], new_v[name] = _adamw_halves(
                w[name], own, other, m[name], v[name], mc, BIG[BIG_INDEX[name]][2], "adamw_" + name)
            after = new_v[name]

    gathered = _gather_rows(partial, "gather_partials", after)
    tot = _small_reduce(gathered)
    grads["b_ada"] = tot[0:6].reshape(1, 6 * d)
    grads["ln1_g"], grads["ln1_b"], grads["ln2_g"], grads["ln2_b"] = tot[8:9], tot[9:10], tot[10:11], tot[11:12]
    grads["q_norm_g"] = tot[13:14, 0:HEAD_DIM]
    grads["k_norm_g"] = tot[13:14, HEAD_DIM:2 * HEAD_DIM]
    grads["sink_logit"] = tot[13:14, 2 * HEAD_DIM:2 * HEAD_DIM + HEADS_A]
    loss = tot[12, 0]

    dm_all = gathered[:, 0:6, :].reshape(N_DEV, 6 * d)
    dmc_tot = jnp.concatenate([tot[6:8].reshape(1, 2 * d), jnp.zeros((1, 4 * d), F32)], axis=1)
    dm_rows = jnp.concatenate([pad8(dm_all[i]) for i in range(N_DEV)] + [pad8(dmc_tot), jnp.zeros((8, 6 * d), F32)], axis=0)
    dm_shard = lax.dynamic_slice(dm_rows, (0, s_me * ADA_W), (ADA_ROWS, ADA_W))
    grads["w_ada"] = _matmul(act, dm_shard, name="dw_ada", ta=True, tm=1024, tn=1024, tk=ADA_ROWS, out_dtype=F32)
    dmc_shard = lax.dynamic_slice(pad8(dmc_tot), (0, s_me * ADA_W), (8, ADA_W))
    cc_part = _matmul(dmc_shard, w["w_ada"], name="d_cctx", tb=True, tm=8, tn=1024, tk=1536, out_dtype=F32)
    grads["c_ctx"] = _cctx_grad(_gather_rows(cc_part, "gather_cctx", tot), c_ctx).reshape(d)

    delta["w_ada"], new_m["w_ada"], new_v["w_ada"] = _adamw(w["w_ada"], grads["w_ada"], m["w_ada"], v["w_ada"],
                                                            "adamw_w_ada")
    pack = lambda t: jnp.concatenate([t[name].reshape(1, size) for name, size in SMALL], axis=1)
    pd, pm, pv = _adamw(pack(w), pack(grads), pack(m), pack(v), "adamw_small")
    off = 0
    for name, size in SMALL:
        delta[name], new_m[name], new_v[name] = [t[:, off:off + size].reshape(w[name].shape) for t in (pd, pm, pv)]
        grads[name] = grads[name].reshape(w[name].shape)
        off += size

    lead = lambda name, t: t[None] if name in ("w_ada", "w_in", "w_out", "w_gate", "w_up", "w_down") else t
    outs = [loss, grad_x[None]]
    for group in (grads, delta, new_m, new_v):
        outs += [lead(name, group[name]) for name in WEIGHT_ORDER]
    return tuple(outs)
```

```python
import functools
import math

import jax
import jax.numpy as jnp
from jax import lax
from jax.experimental import pallas as pl
from jax.experimental.pallas import tpu as pltpu

F32 = jnp.float32
BF16 = jnp.bfloat16
MESH = pl.DeviceIdType.MESH

D_MODEL = 2048
HEAD_DIM = 128
HEADS_A = 8
HEADS_B = 8
KV_A = 2
KV_B = 2
GROUP = 4
GRID_W = 64
WINDOW = 128
BLOCK = 128
FFN = 5632
IN_WIDTH = 3072
MIX_WIDTH = 2048
ROPE_THETA = 10000.0
EPS = 1e-6
ATTN_SCALE = HEAD_DIM ** -0.5
DN_ALPHA = 2.0 ** 0.25
N_SHARD = 4
N_DEV = 8

ADAM_LR = 0.001
ADAM_B1 = 0.9
ADAM_B2 = 0.999
ADAM_EPS = 1e-08
ADAM_WD = 0.01
ADAM_STEP = 10

QA0, KA0, VA0, QB0, KB0, VB0 = 0, 1024, 1280, 1536, 2560, 2816

VMEM_LIMIT = 56 * 1024 * 1024
ROW_TILE = 256
NN = (((1,), (0,)), ((), ()))
NT = (((1,), (1,)), ((), ()))
TN = (((0,), (0,)), ((), ()))


def _fit(total, pref):
    step = ROW_TILE // 4
    best = step
    for cand in range(step, pref + 1, step):
        if total % cand == 0:
            best = cand
    return best


def _params(sem=None):
    return pltpu.CompilerParams(dimension_semantics=sem, vmem_limit_bytes=VMEM_LIMIT)


def _matmul(a, b, *, name, ta=False, tb=False, tm, tn, tk, out_dtype, after=None):
    m = a.shape[1] if ta else a.shape[0]
    k = a.shape[0] if ta else a.shape[1]
    n = b.shape[0] if tb else b.shape[1]
    assert (b.shape[1] if tb else b.shape[0]) == k
    tm, tn, tk = min(tm, m), min(tn, n), min(tk, k)
    assert m % tm == 0 and n % tn == 0 and k % tk == 0, (name, m, n, k, tm, tn, tk)
    nk = k // tk
    dn = (((0 if ta else 1,), (1 if tb else 0,)), ((), ()))

    def product(a_ref, b_ref):
        return lax.dot_general(a_ref[...].astype(BF16), b_ref[...].astype(BF16), dn, preferred_element_type=F32)

    def body_whole_k(a_ref, b_ref, *rest):
        o_ref = rest[-1]
        o_ref[...] = product(a_ref, b_ref).astype(o_ref.dtype)

    def body(a_ref, b_ref, *rest):
        o_ref, acc_ref = rest[-2:]
        kk = pl.program_id(2)
        part = product(a_ref, b_ref)

        @pl.when(kk == 0)
        def _():
            acc_ref[...] = part

        @pl.when(kk != 0)
        def _():
            acc_ref[...] += part

        @pl.when(kk == nk - 1)
        def _():
            o_ref[...] = acc_ref[...].astype(o_ref.dtype)

    a_spec = (pl.BlockSpec((tk, tm), lambda i, j, kk: (kk, i)) if ta
              else pl.BlockSpec((tm, tk), lambda i, j, kk: (i, kk)))
    b_spec = (pl.BlockSpec((tn, tk), lambda i, j, kk: (j, kk)) if tb
              else pl.BlockSpec((tk, tn), lambda i, j, kk: (kk, j)))
    return pl.pallas_call(
        body_whole_k if nk == 1 else body, name=name, grid=(m // tm, n // tn, nk),
        in_specs=[a_spec, b_spec] + ([] if after is None else [pl.BlockSpec(memory_space=pl.ANY)]),
        out_specs=pl.BlockSpec((tm, tn), lambda i, j, kk: (i, j)),
        out_shape=jax.ShapeDtypeStruct((m, n), out_dtype),
        scratch_shapes=[] if nk == 1 else [pltpu.VMEM((tm, tn), F32)],
        compiler_params=_params(("parallel", "parallel", "arbitrary")),
    )(a, b, *([] if after is None else [after]))


def _modulate_rows(x, ctx, mods):
    n, d = x.shape
    c = ctx.shape[0]
    nx = n // ROW_TILE
    assert c == ROW_TILE

    def body(x_ref, ctx_ref, mods_ref, o_ref):
        i = pl.program_id(0)

        @pl.when(i < nx)
        def _():
            o_ref[...] = (x_ref[...] * (1.0 + mods_ref[0:1, :]) + mods_ref[1:2, :]).astype(BF16)

        @pl.when(i >= nx)
        def _():
            o_ref[...] = (ctx_ref[...] * (1.0 + mods_ref[2:3, :]) + mods_ref[3:4, :]).astype(BF16)

    return pl.pallas_call(
        body, name="modulate_rows", grid=(nx + 1,),
        in_specs=[pl.BlockSpec((ROW_TILE, d), lambda i: (jnp.minimum(i, nx - 1), 0)),
                  pl.BlockSpec((ROW_TILE, d), lambda i: (0, 0)),
                  pl.BlockSpec((8, d), lambda i: (0, 0))],
        out_specs=pl.BlockSpec((ROW_TILE, d), lambda i: (i, 0)),
        out_shape=jax.ShapeDtypeStruct((n + c, d), BF16),
        compiler_params=_params(("parallel",)),
    )(x, ctx, mods)


def _rope_tables(n, c):
    rows = n // GRID_W
    row_ids = jnp.repeat(jnp.arange(rows, dtype=F32), GRID_W)
    col_ids = jnp.tile(jnp.arange(GRID_W, dtype=F32), rows)
    axis_dim = HEAD_DIM // 2
    inv_freq = jnp.power(ROPE_THETA, -jnp.arange(0, axis_dim, 2, dtype=F32) / axis_dim)
    ang_r = row_ids[:, None] * inv_freq
    ang_c = col_ids[:, None] * inv_freq
    ang = jnp.concatenate([ang_r, ang_r, ang_c, ang_c], axis=-1)
    cos, sin = jnp.cos(ang), jnp.sin(ang)
    quarter = (jnp.arange(HEAD_DIM) // (HEAD_DIM // 4)) % 2
    sin_a = jnp.where(quarter == 0, -sin, 0.0)
    sin_b = jnp.where(quarter == 1, sin, 0.0)
    pad = lambda t, v: jnp.concatenate([t, jnp.full((c, HEAD_DIM), v, F32)], axis=0)
    return pad(cos, 1.0), pad(sin_a, 0.0), pad(sin_b, 0.0)


def _rope(x, cos, sin_a, sin_b):
    return x * cos + pltpu.roll(x, 96, 1) * sin_a + pltpu.roll(x, 32, 1) * sin_b


def _rope_t(dy, cos, sin_a, sin_b):
    return dy * cos - pltpu.roll(dy, 96, 1) * sin_a - pltpu.roll(dy, 32, 1) * sin_b


def _rms(x):
    r = lax.rsqrt(jnp.mean(x * x, axis=-1, keepdims=True) + EPS)
    return x * r, r


def _qkv_post(h_all, cos, sin_a, sin_b, q_g, k_g):
    t = h_all.shape[0]
    nt = t // ROW_TILE

    def body(h_ref, cos_ref, sa_ref, sb_ref, qg_ref, kg_ref, qa_ref, ka_ref, va_ref, qb_ref, kb_ref, vb_ref):
        cos_, sa, sb = cos_ref[...], sa_ref[...], sb_ref[...]
        sl = lambda off, hh: h_ref[:, off + hh * HEAD_DIM: off + (hh + 1) * HEAD_DIM]
        for hh in range(HEADS_A):
            qa_ref[hh] = (_rope(sl(QA0, hh), cos_, sa, sb) * ATTN_SCALE).astype(BF16)
        for hh in range(KV_A):
            ka_ref[hh] = _rope(sl(KA0, hh), cos_, sa, sb).astype(BF16)
            va_ref[hh] = sl(VA0, hh).astype(BF16)
        for hh in range(HEADS_B):
            xn, _ = _rms(sl(QB0, hh))
            qb_ref[hh] = (_rope(xn * qg_ref[...], cos_, sa, sb) * ATTN_SCALE).astype(BF16)
        for hh in range(KV_B):
            xn, _ = _rms(sl(KB0, hh))
            kb_ref[hh] = _rope(xn * kg_ref[...], cos_, sa, sb).astype(BF16)
            vb_ref[hh] = sl(VB0, hh).astype(BF16)

    tab = pl.BlockSpec((ROW_TILE, HEAD_DIM), lambda i: (i, 0))
    gain = pl.BlockSpec((1, HEAD_DIM), lambda i: (0, 0))
    hs = lambda nh: pl.BlockSpec((nh, ROW_TILE, HEAD_DIM), lambda i: (0, i, 0))
    sh = lambda nh: jax.ShapeDtypeStruct((nh, t, HEAD_DIM), BF16)
    return pl.pallas_call(
        body, name="qkv_post", grid=(nt,),
        in_specs=[pl.BlockSpec((ROW_TILE, IN_WIDTH), lambda i: (i, 0)), tab, tab, tab, gain, gain],
        out_specs=[hs(HEADS_A), hs(KV_A), hs(KV_A), hs(HEADS_B), hs(KV_B), hs(KV_B)],
        out_shape=[sh(HEADS_A), sh(KV_A), sh(KV_A), sh(HEADS_B), sh(KV_B), sh(KV_B)],
        compiler_params=_params(("parallel",)),
    )(h_all, cos, sin_a, sin_b, q_g, k_g)


def _qkv_bwd_post(h_all, cos, sin_a, sin_b, q_g, k_g, dqa, dka, dva, dqb, dkb, dvb, n):
    t = h_all.shape[0]
    nt = t // ROW_TILE
    nx = n // ROW_TILE

    def body(h_ref, cos_ref, sa_ref, sb_ref, qg_ref, kg_ref,
             dqa_ref, dka_ref, dva_ref, dqb_ref, dkb_ref, dvb_ref, dh_ref, gs_ref):
        i = pl.program_id(0)
        cos_, sa, sb = cos_ref[...], sa_ref[...], sb_ref[...]
        latent = (i < nx).astype(F32)
        sl = lambda off, hh: h_ref[:, off + hh * HEAD_DIM: off + (hh + 1) * HEAD_DIM]

        def put(off, hh, val):
            dh_ref[:, off + hh * HEAD_DIM: off + (hh + 1) * HEAD_DIM] = val.astype(BF16)

        def norm_bwd(x, gain, dy):
            xn, r = _rms(x)
            dxh = dy * gain
            dx = r * (dxh - xn * jnp.mean(dxh * xn, axis=-1, keepdims=True))
            return dx, jnp.sum(dy * xn, axis=0, keepdims=True)

        for hh in range(HEADS_A):
            put(QA0, hh, _rope_t(dqa_ref[hh] * (ATTN_SCALE * latent), cos_, sa, sb))
        for hh in range(KV_A):
            put(KA0, hh, _rope_t(dka_ref[hh], cos_, sa, sb))
            put(VA0, hh, dva_ref[hh])
        gq = jnp.zeros((1, HEAD_DIM), F32)
        gk = jnp.zeros((1, HEAD_DIM), F32)
        for hh in range(HEADS_B):
            dq_t = dqb_ref[hh // GROUP, :, (hh % GROUP) * ROW_TILE:(hh % GROUP + 1) * ROW_TILE]
            dy = _rope_t(dq_t.T * (ATTN_SCALE * latent), cos_, sa, sb)
            dx, g = norm_bwd(sl(QB0, hh), qg_ref[...], dy)
            put(QB0, hh, dx)
            gq = gq + g
        for hh in range(KV_B):
            dy = _rope_t(dkb_ref[hh], cos_, sa, sb)
            dx, g = norm_bwd(sl(KB0, hh), kg_ref[...], dy)
            put(KB0, hh, dx)
            gk = gk + g
            put(VB0, hh, dvb_ref[hh])
        upd = jnp.concatenate([gq, gk, jnp.zeros((6, HEAD_DIM), F32)], axis=0)

        @pl.when(i == 0)
        def _():
            gs_ref[...] = upd

        @pl.when(i != 0)
        def _():
            gs_ref[...] += upd

    tab = pl.BlockSpec((ROW_TILE, HEAD_DIM), lambda i: (i, 0))
    gain = pl.BlockSpec((1, HEAD_DIM), lambda i: (0, 0))
    lat = lambda nh: pl.BlockSpec((nh, ROW_TILE, HEAD_DIM), lambda i: (0, jnp.minimum(i, nx - 1), 0))
    full = lambda nh: pl.BlockSpec((nh, ROW_TILE, HEAD_DIM), lambda i: (0, i, 0))
    return pl.pallas_call(
        body, name="qkv_bwd_post", grid=(nt,),
        in_specs=[pl.BlockSpec((ROW_TILE, IN_WIDTH), lambda i: (i, 0)), tab, tab, tab, gain, gain,
                  lat(HEADS_A), full(KV_A), full(KV_A),
                  pl.BlockSpec((KV_B, None, HEAD_DIM, GROUP * ROW_TILE), lambda i: (0, jnp.minimum(i, nx - 1), 0, 0)),
                  full(KV_B), full(KV_B)],
        out_specs=[pl.BlockSpec((ROW_TILE, IN_WIDTH), lambda i: (i, 0)),
                   pl.BlockSpec((8, HEAD_DIM), lambda i: (0, 0))],
        out_shape=[jax.ShapeDtypeStruct((t, IN_WIDTH), BF16), jax.ShapeDtypeStruct((8, HEAD_DIM), F32)],
        compiler_params=_params(("arbitrary",)),
    )(h_all, cos, sin_a, sin_b, q_g, k_g, dqa, dka, dva, dqb, dkb, dvb)


GB_TQ = 256
GB_TK = 256


def _heads_rows(ref2d, tq):
    return jnp.concatenate([ref2d[:, hh * HEAD_DIM:(hh + 1) * HEAD_DIM] for hh in range(GROUP)], axis=0)


def _attn_b_fwd(qb, kb, vb, n):
    t = kb.shape[1]
    nk = t // GB_TK
    tq = GB_TQ
    rows = GROUP * tq

    def body(q_ref, k_ref, v_ref, o_ref, lse_ref, m_s, l_s, acc_s):
        q = q_ref[...].reshape(rows, HEAD_DIM)
        m_s[...] = jnp.full((1, rows), -jnp.inf, F32)
        l_s[...] = jnp.zeros((1, rows), F32)
        acc_s[...] = jnp.zeros((HEAD_DIM, rows), F32)

        def scores(j):
            start = pl.multiple_of(j * GB_TK, GB_TK)
            return lax.dot_general(k_ref[pl.ds(start, GB_TK), :], q, NT, preferred_element_type=F32)

        def step(j, st):
            st_next = scores(jnp.minimum(j + 1, nk - 1))
            vs = v_ref[pl.ds(pl.multiple_of(j * GB_TK, GB_TK), GB_TK), :]
            m_prev = m_s[...]
            m_new = jnp.maximum(m_prev, jnp.max(st, axis=0, keepdims=True))
            p = jnp.exp(st - m_new)
            alpha = jnp.exp(m_prev - m_new)
            l_s[...] = alpha * l_s[...] + jnp.sum(p, axis=0, keepdims=True)
            acc_s[...] = alpha * acc_s[...] + lax.dot_general(vs, p.astype(BF16), TN, preferred_element_type=F32)
            m_s[...] = m_new
            return st_next

        lax.fori_loop(0, nk, step, scores(0))
        ot = acc_s[...] * (1.0 / l_s[...])
        lse_ref[...] = m_s[...] + jnp.log(l_s[...])
        for hh in range(GROUP):
            o_ref[:, hh * HEAD_DIM:(hh + 1) * HEAD_DIM] = ot[:, hh * tq:(hh + 1) * tq].T.astype(BF16)

    return pl.pallas_call(
        body, name="attn_b_fwd", grid=(KV_B, n // tq),
        in_specs=[pl.BlockSpec((GROUP, tq, HEAD_DIM), lambda g, i: (g, i, 0)),
                  pl.BlockSpec((None, t, HEAD_DIM), lambda g, i: (g, 0, 0)),
                  pl.BlockSpec((None, t, HEAD_DIM), lambda g, i: (g, 0, 0))],
        out_specs=[pl.BlockSpec((tq, GROUP * HEAD_DIM), lambda g, i: (i, KV_A + g)),
                   pl.BlockSpec((None, None, 1, rows), lambda g, i: (g, i, 0, 0))],
        out_shape=[jax.ShapeDtypeStruct((n, MIX_WIDTH), BF16),
                   jax.ShapeDtypeStruct((KV_B, n // tq, 1, rows), F32)],
        scratch_shapes=[pltpu.VMEM((1, rows), F32), pltpu.VMEM((1, rows), F32), pltpu.VMEM((HEAD_DIM, rows), F32)],
        compiler_params=_params(("parallel", "parallel")),
    )(qb, kb, vb)


def _attn_b_bwd(qb, kb, vb, dheads, lse, delta, n):
    t = kb.shape[1]
    nk = t // GB_TK
    tq = GB_TQ
    nq = n // tq
    rows = GROUP * tq

    qb_step = 4 if nq % 4 == 0 else 1

    def body(q_ref, k_ref, v_ref, do_ref, lse_ref, dl_ref, dq_ref, dk_ref, dv_ref):
        j = pl.program_id(1)
        i = pl.program_id(2)

        blocks = [(s, hh) for s in range(qb_step) for hh in range(GROUP)]
        q = jnp.concatenate([q_ref[hh, s * tq:(s + 1) * tq, :] for s, hh in blocks], axis=0)
        do = jnp.concatenate([do_ref[s * tq:(s + 1) * tq, hh * HEAD_DIM:(hh + 1) * HEAD_DIM] for s, hh in blocks], axis=0)
        lse_row = jnp.concatenate([lse_ref[s] for s in range(qb_step)], axis=1)
        dl_row = jnp.concatenate([dl_ref[s] for s in range(qb_step)], axis=1)
        ks, vs = k_ref[...], v_ref[...]
        st = lax.dot_general(ks, q, NT, preferred_element_type=F32)
        p = jnp.exp(st - lse_row)
        dpt = lax.dot_general(vs, do, NT, preferred_element_type=F32)
        ds = (p * (dpt - dl_row)).astype(BF16)
        dv_part = lax.dot_general(p.astype(BF16), do, NN, preferred_element_type=F32)
        dk_part = lax.dot_general(ds, q, NN, preferred_element_type=F32)
        dq_part = lax.dot_general(ks, ds, TN, preferred_element_type=F32)

        @pl.when(i == 0)
        def _():
            dk_ref[...] = dk_part
            dv_ref[...] = dv_part

        @pl.when(i != 0)
        def _():
            dk_ref[...] += dk_part
            dv_ref[...] += dv_part

        for s in range(qb_step):
            piece = dq_part[:, s * rows:(s + 1) * rows]

            @pl.when(j == 0)
            def _():
                dq_ref[qb_step * i + s] = piece

            @pl.when(j != 0)
            def _():
                dq_ref[qb_step * i + s] += piece

    kv = pl.BlockSpec((None, GB_TK, HEAD_DIM), lambda g, j, i: (g, j, 0))
    row = pl.BlockSpec((None, qb_step, 1, rows), lambda g, j, i: (g, i, 0, 0))
    return pl.pallas_call(
        body, name="attn_b_bwd", grid=(KV_B, nk, nq // qb_step),
        in_specs=[pl.BlockSpec((GROUP, qb_step * tq, HEAD_DIM), lambda g, j, i: (g, i, 0)), kv, kv,
                  pl.BlockSpec((qb_step * tq, GROUP * HEAD_DIM), lambda g, j, i: (i, KV_A + g)), row, row],
        out_specs=[pl.BlockSpec((None, nq, HEAD_DIM, rows), lambda g, j, i: (g, 0, 0, 0)), kv, kv],
        out_shape=[jax.ShapeDtypeStruct((KV_B, nq, HEAD_DIM, rows), F32),
                   jax.ShapeDtypeStruct((KV_B, t, HEAD_DIM), F32),
                   jax.ShapeDtypeStruct((KV_B, t, HEAD_DIM), F32)],
        compiler_params=_params(("parallel", "arbitrary", "arbitrary")),
    )(qb, kb, vb, dheads, lse, delta)


def _delta_rows(dheads, heads):
    n = heads.shape[0]
    tq = GB_TQ
    w = GROUP * HEAD_DIM

    def body(a_ref, b_ref, o_ref):
        prod = a_ref[...].astype(F32) * b_ref[...].astype(F32)
        cols = [jnp.sum(prod[:, hh * HEAD_DIM:(hh + 1) * HEAD_DIM].T, axis=0, keepdims=True) for hh in range(GROUP)]
        o_ref[...] = jnp.concatenate(cols, axis=1)

    blk = pl.BlockSpec((tq, w), lambda g, i: (i, KV_A + g))
    return pl.pallas_call(
        body, name="delta_rows", grid=(KV_B, n // tq),
        in_specs=[blk, blk],
        out_specs=pl.BlockSpec((None, None, 1, GROUP * tq), lambda g, i: (g, i, 0, 0)),
        out_shape=jax.ShapeDtypeStruct((KV_B, n // tq, 1, GROUP * tq), F32),
        compiler_params=_params(("parallel", "parallel")),
    )(dheads, heads)


KWIN = 3 * BLOCK


def _window_bias():
    row = jnp.arange(KWIN)[None, :, None]
    col = jnp.arange(GROUP * BLOCK)[None, None, :] % BLOCK
    shift = jnp.arange(3)[:, None, None] * BLOCK
    return jnp.where(jnp.abs(shift + col - row) <= WINDOW, 0.0, -jnp.inf).astype(F32)


def _window_variant(j, nb):
    return j - jnp.clip(j - 1, 0, nb - 3)


def _window_scores(q, k_ref, j, n, nb, sink_row, bias):
    c = k_ref.shape[0] - n
    start = pl.multiple_of(jnp.clip(j - 1, 0, nb - 3) * BLOCK, BLOCK)
    kw = k_ref[pl.ds(start, KWIN), :]
    kc = k_ref[pl.ds(n, c), :]
    s_loc = lax.dot_general(kw, q, NT, preferred_element_type=F32) + bias
    s_ctx = lax.dot_general(kc, q, NT, preferred_element_type=F32)
    m = jnp.maximum(jnp.maximum(jnp.max(s_loc, axis=0, keepdims=True), jnp.max(s_ctx, axis=0, keepdims=True)),
                    sink_row)
    e_loc, e_ctx, e_sink = jnp.exp(s_loc - m), jnp.exp(s_ctx - m), jnp.exp(sink_row - m)
    inv = 1.0 / (jnp.sum(e_loc, axis=0, keepdims=True) + jnp.sum(e_ctx, axis=0, keepdims=True) + e_sink)
    return e_loc * inv, e_ctx * inv, e_sink * inv, start


def _sink_row(sink_ref, g):
    return jnp.concatenate([sink_ref[pl.ds(g * GROUP + hh, 1), :] for hh in range(GROUP)], axis=1)


def _attn_a_fwd(qa, ka, va, sink_b, heads_b, n):
    t = ka.shape[1]
    nb = n // BLOCK
    assert nb >= 3

    def body(q_ref, k_ref, v_ref, sink_ref, heads_b_ref, bias_ref, o_ref):
        g, j = pl.program_id(0), pl.program_id(1)
        q = q_ref[...].reshape(GROUP * BLOCK, HEAD_DIM)
        p_loc, p_ctx, _, start = _window_scores(q, k_ref, j, n, nb, _sink_row(sink_ref, g), bias_ref[...])
        vw = v_ref[pl.ds(start, KWIN), :]
        vc = v_ref[pl.ds(n, t - n), :]
        ot = (lax.dot_general(vw, p_loc.astype(BF16), TN, preferred_element_type=F32)
              + lax.dot_general(vc, p_ctx.astype(BF16), TN, preferred_element_type=F32))
        for hh in range(GROUP):
            o_ref[:, hh * HEAD_DIM:(hh + 1) * HEAD_DIM] = ot[:, hh * BLOCK:(hh + 1) * BLOCK].T.astype(BF16)

    return pl.pallas_call(
        body, name="attn_a_fwd", grid=(KV_A, nb),
        in_specs=[pl.BlockSpec((GROUP, BLOCK, HEAD_DIM), lambda g, j: (g, j, 0)),
                  pl.BlockSpec((None, t, HEAD_DIM), lambda g, j: (g, 0, 0)),
                  pl.BlockSpec((None, t, HEAD_DIM), lambda g, j: (g, 0, 0)),
                  pl.BlockSpec((HEADS_A, HEAD_DIM), lambda g, j: (0, 0)),
                  pl.BlockSpec(memory_space=pl.ANY),
                  pl.BlockSpec((None, KWIN, GROUP * BLOCK), lambda g, j: (_window_variant(j, nb), 0, 0))],
        out_specs=pl.BlockSpec((BLOCK, GROUP * HEAD_DIM), lambda g, j: (j, g)),
        out_shape=jax.ShapeDtypeStruct((n, MIX_WIDTH), BF16),
        input_output_aliases={4: 0},
        compiler_params=_params(("parallel", "parallel")),
    )(qa, ka, va, sink_b, heads_b, _window_bias())


def _attn_a_bwd(qa, ka, va, sink_b, dheads, n):
    t = ka.shape[1]
    c = t - n
    nb = n // BLOCK

    def body(q_ref, k_ref, v_ref, sink_ref, do_ref, bias_ref, dq_ref, dk_ref, dv_ref, dsink_ref):
        g, j = pl.program_id(0), pl.program_id(1)

        @pl.when(j == 0)
        def _():
            dk_ref[...] = jnp.zeros_like(dk_ref)
            dv_ref[...] = jnp.zeros_like(dv_ref)
            dsink_ref[...] = jnp.zeros_like(dsink_ref)

        q = q_ref[...].reshape(GROUP * BLOCK, HEAD_DIM)
        do = _heads_rows(do_ref, BLOCK)
        p_loc, p_ctx, p_sink, start = _window_scores(q, k_ref, j, n, nb, _sink_row(sink_ref, g), bias_ref[...])
        kw, vw = k_ref[pl.ds(start, KWIN), :], v_ref[pl.ds(start, KWIN), :]
        kc, vc = k_ref[pl.ds(n, c), :], v_ref[pl.ds(n, c), :]
        dp_loc = lax.dot_general(vw, do, NT, preferred_element_type=F32)
        dp_ctx = lax.dot_general(vc, do, NT, preferred_element_type=F32)
        dl = jnp.sum(p_loc * dp_loc, axis=0, keepdims=True) + jnp.sum(p_ctx * dp_ctx, axis=0, keepdims=True)
        ds_loc = (p_loc * (dp_loc - dl)).astype(BF16)
        ds_ctx = (p_ctx * (dp_ctx - dl)).astype(BF16)
        dqt = (lax.dot_general(kw, ds_loc, TN, preferred_element_type=F32)
               + lax.dot_general(kc, ds_ctx, TN, preferred_element_type=F32))
        for hh in range(GROUP):
            dq_ref[hh] = dqt[:, hh * BLOCK:(hh + 1) * BLOCK].T
        dk_ref[pl.ds(start, KWIN), :] += lax.dot_general(ds_loc, q, NN, preferred_element_type=F32)
        dv_ref[pl.ds(start, KWIN), :] += lax.dot_general(p_loc.astype(BF16), do, NN, preferred_element_type=F32)
        dk_ref[pl.ds(n, c), :] += lax.dot_general(ds_ctx, q, NN, preferred_element_type=F32)
        dv_ref[pl.ds(n, c), :] += lax.dot_general(p_ctx.astype(BF16), do, NN, preferred_element_type=F32)
        dsk = -(p_sink * dl)
        upd = [jnp.broadcast_to(jnp.sum(dsk[:, hh * BLOCK:(hh + 1) * BLOCK], axis=1, keepdims=True), (1, HEAD_DIM))
               for hh in range(GROUP)]
        dsink_ref[...] += jnp.concatenate(upd + [jnp.zeros((8 - GROUP, HEAD_DIM), F32)], axis=0)

    res = pl.BlockSpec((None, t, HEAD_DIM), lambda g, j: (g, 0, 0))
    return pl.pallas_call(
        body, name="attn_a_bwd", grid=(KV_A, nb),
        in_specs=[pl.BlockSpec((GROUP, BLOCK, HEAD_DIM), lambda g, j: (g, j, 0)), res, res,
                  pl.BlockSpec((HEADS_A, HEAD_DIM), lambda g, j: (0, 0)),
                  pl.BlockSpec((BLOCK, GROUP * HEAD_DIM), lambda g, j: (j, g)),
                  pl.BlockSpec((None, KWIN, GROUP * BLOCK), lambda g, j: (_window_variant(j, nb), 0, 0))],
        out_specs=[pl.BlockSpec((GROUP, BLOCK, HEAD_DIM), lambda g, j: (g, j, 0)), res, res,
                   pl.BlockSpec((None, 8, HEAD_DIM), lambda g, j: (g, 0, 0))],
        out_shape=[jax.ShapeDtypeStruct((HEADS_A, n, HEAD_DIM), F32),
                   jax.ShapeDtypeStruct((KV_A, t, HEAD_DIM), F32),
                   jax.ShapeDtypeStruct((KV_A, t, HEAD_DIM), F32),
                   jax.ShapeDtypeStruct((KV_A, 8, HEAD_DIM), F32)],
        compiler_params=_params(("parallel", "arbitrary")),
    )(qa, ka, va, sink_b, dheads, _window_bias())


def _ln_stats(r):
    mu = jnp.mean(r, axis=-1, keepdims=True)
    cen = r - mu
    rstd = lax.rsqrt(jnp.mean(cen * cen, axis=-1, keepdims=True) + EPS)
    return cen * rstd, rstd


def _ln_bwd(dy, xhat, rstd, gain):
    dxh = dy * gain
    return rstd * (dxh - jnp.mean(dxh, axis=-1, keepdims=True)
                   - xhat * jnp.mean(dxh * xhat, axis=-1, keepdims=True))


def _accumulate_rows(ref, rows, i):
    pad = [jnp.zeros_like(rows[0])] * (8 - len(rows))
    upd = jnp.concatenate(rows + pad, axis=0)

    @pl.when(i == 0)
    def _():
        ref[...] = upd

    @pl.when(i != 0)
    def _():
        ref[...] += upd


def _colsum(v):
    return jnp.sum(v, axis=0, keepdims=True)


LN_TILE = 256


def _res_ln1(x, a, vec):
    n, d = x.shape

    def body(x_ref, a_ref, v_ref, xh_ref, rs_ref, u_ref):
        r1 = DN_ALPHA * x_ref[...] + v_ref[0:1, :] * a_ref[...]
        xhat, rstd = _ln_stats(r1)
        xh_ref[...] = xhat
        rs_ref[...] = rstd
        x1 = xhat * v_ref[1:2, :] + v_ref[2:3, :]
        u_ref[...] = (x1 * (1.0 + v_ref[3:4, :]) + v_ref[4:5, :]).astype(BF16)

    row = pl.BlockSpec((LN_TILE, d), lambda i: (i, 0))
    return pl.pallas_call(
        body, name="res_ln1", grid=(n // LN_TILE,),
        in_specs=[row, row, pl.BlockSpec((8, d), lambda i: (0, 0))],
        out_specs=[row, pl.BlockSpec((LN_TILE, 1), lambda i: (i, 0)), row],
        out_shape=[jax.ShapeDtypeStruct((n, d), F32), jax.ShapeDtypeStruct((n, 1), F32),
                   jax.ShapeDtypeStruct((n, d), BF16)],
        compiler_params=_params(("parallel",)),
    )(x, a, vec)


def _res_ln2_loss(xhat1, f, target, vec):
    n, d = f.shape

    def body(xh_ref, f_ref, t_ref, v_ref, dr_ref, df_ref, s_ref):
        i = pl.program_id(0)
        x1 = xh_ref[...] * v_ref[1:2, :] + v_ref[2:3, :]
        fv = f_ref[...]
        xhat, rstd = _ln_stats(DN_ALPHA * x1 + v_ref[0:1, :] * fv)
        err = xhat * v_ref[3:4, :] + v_ref[4:5, :] - t_ref[...]
        dy = err * (1.0 / d)
        dr2 = _ln_bwd(dy, xhat, rstd, v_ref[3:4, :])
        dr_ref[...] = dr2
        df_ref[...] = (dr2 * v_ref[0:1, :]).astype(BF16)
        _accumulate_rows(s_ref, [_colsum(dy * xhat), _colsum(dy), _colsum(dr2 * fv),
                                 _colsum(err * err) * (0.5 / d)], i)

    row = pl.BlockSpec((LN_TILE, d), lambda i: (i, 0))
    return pl.pallas_call(
        body, name="res_ln2_loss", grid=(n // LN_TILE,),
        in_specs=[row, row, row, pl.BlockSpec((8, d), lambda i: (0, 0))],
        out_specs=[row, row, pl.BlockSpec((8, d), lambda i: (0, 0))],
        out_shape=[jax.ShapeDtypeStruct((n, d), F32), jax.ShapeDtypeStruct((n, d), BF16),
                   jax.ShapeDtypeStruct((8, d), F32)],
        compiler_params=_params(("arbitrary",)),
    )(xhat1, f, target, vec)


def _ln1_bwd(du2, dr2, xhat1, rstd1, a, vec):
    n, d = du2.shape

    def body(du_ref, dr2_ref, xh_ref, rs_ref, a_ref, v_ref, dxp_ref, da_ref, s_ref):
        i = pl.program_id(0)
        du, xhat = du_ref[...], xh_ref[...]
        x1 = xhat * v_ref[1:2, :] + v_ref[2:3, :]
        dx1 = DN_ALPHA * dr2_ref[...] + du * (1.0 + v_ref[0:1, :])
        dr1 = _ln_bwd(dx1, xhat, rs_ref[...], v_ref[1:2, :])
        dxp_ref[...] = DN_ALPHA * dr1
        da_ref[...] = (dr1 * v_ref[3:4, :]).astype(BF16)
        _accumulate_rows(s_ref, [_colsum(du * x1), _colsum(du), _colsum(dx1 * xhat), _colsum(dx1),
                                 _colsum(dr1 * a_ref[...])], i)

    row = pl.BlockSpec((LN_TILE, d), lambda i: (i, 0))
    return pl.pallas_call(
        body, name="ln1_bwd", grid=(n // LN_TILE,),
        in_specs=[row, row, row, pl.BlockSpec((LN_TILE, 1), lambda i: (i, 0)), row,
                  pl.BlockSpec((8, d), lambda i: (0, 0))],
        out_specs=[row, row, pl.BlockSpec((8, d), lambda i: (0, 0))],
        out_shape=[jax.ShapeDtypeStruct((n, d), F32), jax.ShapeDtypeStruct((n, d), BF16),
                   jax.ShapeDtypeStruct((8, d), F32)],
        compiler_params=_params(("arbitrary",)),
    )(du2, dr2, xhat1, rstd1, a, vec)


def _mod1_bwd(du_all, dxp, x, ctx, mods):
    n, d = x.shape
    nx = n // ROW_TILE

    def body(du_ref, dxp_ref, x_ref, ctx_ref, m_ref, gx_ref, s_ref):
        i = pl.program_id(0)
        du = du_ref[...]
        zero = jnp.zeros((1, d), F32)

        @pl.when(i == 0)
        def _():
            s_ref[...] = jnp.zeros_like(s_ref)

        @pl.when(i < nx)
        def _():
            gx_ref[...] = dxp_ref[...] + du * (1.0 + m_ref[0:1, :])
            s_ref[...] += jnp.concatenate([_colsum(du * x_ref[...]), _colsum(du)] + [zero] * 6, axis=0)

        @pl.when(i >= nx)
        def _():
            s_ref[...] += jnp.concatenate([zero, zero, _colsum(du * ctx_ref[...]), _colsum(du)] + [zero] * 4, axis=0)

    lat = pl.BlockSpec((ROW_TILE, d), lambda i: (jnp.minimum(i, nx - 1), 0))
    return pl.pallas_call(
        body, name="mod1_bwd", grid=(nx + 1,),
        in_specs=[pl.BlockSpec((ROW_TILE, d), lambda i: (i, 0)), lat, lat,
                  pl.BlockSpec((ROW_TILE, d), lambda i: (0, 0)), pl.BlockSpec((8, d), lambda i: (0, 0))],
        out_specs=[lat, pl.BlockSpec((8, d), lambda i: (0, 0))],
        out_shape=[jax.ShapeDtypeStruct((n, d), F32), jax.ShapeDtypeStruct((8, d), F32)],
        compiler_params=_params(("arbitrary",)),
    )(du_all, dxp, x, ctx, mods)


FFN_TM = 1024
FFN_TN = 512


def _gate_up(u2, wg, wu, after):
    n, d = u2.shape
    f = wg.shape[1]

    def body(u_ref, wg_ref, wu_ref, after_ref, g_ref, up_ref, h_ref):
        u = u_ref[...]
        g = lax.dot_general(u, wg_ref[...], NN, preferred_element_type=F32)
        up = lax.dot_general(u, wu_ref[...], NN, preferred_element_type=F32)
        g_ref[...] = g.astype(BF16)
        up_ref[...] = up.astype(BF16)
        h_ref[...] = (g * jax.nn.sigmoid(g) * up).astype(BF16)

    tm = min(FFN_TM, n)
    wspec = pl.BlockSpec((d, FFN_TN), lambda j, i: (0, j))
    ospec = pl.BlockSpec((tm, FFN_TN), lambda j, i: (i, j))
    return pl.pallas_call(
        body, name="gate_up", grid=(f // FFN_TN, n // tm),
        in_specs=[pl.BlockSpec((tm, d), lambda j, i: (i, 0)), wspec, wspec, pl.BlockSpec(memory_space=pl.ANY)],
        out_specs=[ospec, ospec, ospec],
        out_shape=[jax.ShapeDtypeStruct((n, f), BF16)] * 3,
        compiler_params=_params(("parallel", "parallel")),
    )(u2, wg, wu, after)


def _glu_bwd(df, wd, g, u):
    n, d = df.shape
    f = wd.shape[0]

    def body(df_ref, wd_ref, g_ref, u_ref, dg_ref, du_ref):
        dh = lax.dot_general(df_ref[...], wd_ref[...], NT, preferred_element_type=F32)
        gv = g_ref[...].astype(F32)
        sig = jax.nn.sigmoid(gv)
        du_ref[...] = (dh * (gv * sig)).astype(BF16)
        dg_ref[...] = (dh * u_ref[...].astype(F32) * (sig * (1.0 + gv * (1.0 - sig)))).astype(BF16)

    tm = min(FFN_TM, n)
    ospec = pl.BlockSpec((tm, FFN_TN), lambda i, j: (i, j))
    return pl.pallas_call(
        body, name="glu_bwd", grid=(n // tm, f // FFN_TN),
        in_specs=[pl.BlockSpec((tm, d), lambda i, j: (i, 0)),
                  pl.BlockSpec((FFN_TN, d), lambda i, j: (j, 0)), ospec, ospec],
        out_specs=[ospec, ospec],
        out_shape=[jax.ShapeDtypeStruct((n, f), BF16), jax.ShapeDtypeStruct((n, f), BF16)],
        compiler_params=_params(("parallel", "parallel")),
    )(df, wd, g, u)


def _du2(dg, du, wg, wu):
    n, f = dg.shape
    d = wg.shape[0]
    tm, tn, tk = min(1024, n), 1024, 1408
    nk = f // tk

    def body(dg_ref, du_ref, wg_ref, wu_ref, o_ref, acc_ref):
        kk = pl.program_id(2)
        part = (lax.dot_general(dg_ref[...], wg_ref[...], NT, preferred_element_type=F32)
                + lax.dot_general(du_ref[...], wu_ref[...], NT, preferred_element_type=F32))

        @pl.when(kk == 0)
        def _():
            acc_ref[...] = part

        @pl.when(kk != 0)
        def _():
            acc_ref[...] += part

        @pl.when(kk == nk - 1)
        def _():
            o_ref[...] = acc_ref[...]

    aspec = pl.BlockSpec((tm, tk), lambda i, j, kk: (i, kk))
    wspec = pl.BlockSpec((tn, tk), lambda i, j, kk: (j, kk))
    return pl.pallas_call(
        body, name="du2", grid=(n // tm, d // tn, nk),
        in_specs=[aspec, aspec, wspec, wspec],
        out_specs=pl.BlockSpec((tm, tn), lambda i, j, kk: (i, j)),
        out_shape=jax.ShapeDtypeStruct((n, d), F32),
        scratch_shapes=[pltpu.VMEM((tm, tn), F32)],
        compiler_params=_params(("parallel", "parallel", "arbitrary")),
    )(dg, du, wg, wu)


def _rows8(rows, d=D_MODEL):
    rows = [r.reshape(1, d).astype(F32) for r in rows]
    return jnp.concatenate(rows + [jnp.zeros((8 - len(rows), d), F32)], axis=0)


W_GROUPS = (("w_in",), ("w_out", "w_gate", "w_up"), ("w_down",))
G_GROUPS = (("w_down", "w_gate", "w_up"), ("w_out",), ("w_in",))


def _layer_fwd_bwd(x, ctx, target, mod, mod_ctx, weights, prefetch, grads_out,
                   q_g, k_g, sink, ln1_g, ln1_b, ln2_g, ln2_b):
    n, d = x.shape
    c = ctx.shape[0]
    sh1, sc1, g1, sh2, sc2, g2 = [mod[:, k * d:(k + 1) * d] for k in range(6)]
    csh1, csc1 = mod_ctx[:, 0:d], mod_ctx[:, d:2 * d]
    cos, sin_a, sin_b = _rope_tables(n, c)
    sink_b = jnp.broadcast_to(sink.reshape(HEADS_A, 1), (HEADS_A, HEAD_DIM)).astype(F32)

    u_all = _modulate_rows(x, ctx, _rows8([sc1, sh1, csc1, csh1]))
    (w_in,) = weights(0, u_all)
    h_all = _matmul(u_all, w_in, name="qkv_proj", tm=_fit(n + c, 1088), tn=1024, tk=2048, out_dtype=F32)
    qa, ka, va, qb, kb, vb = _qkv_post(h_all, cos, sin_a, sin_b, q_g, k_g)
    heads_b, lse = _attn_b_fwd(qb, kb, vb, n)
    zero = prefetch(1, heads_b)
    heads = _attn_a_fwd(qa, ka, va, sink_b + zero, heads_b, n)
    w_out, w_gate, w_up = weights(1, heads)
    a = _matmul(heads, w_out, name="out_proj", tm=1024, tn=1024, tk=2048, out_dtype=F32)
    xhat1, rstd1, u2 = _res_ln1(x, a, _rows8([g1, ln1_g, ln1_b, sc2, sh2]))
    zero = prefetch(2, u2)
    gg, uu, hh = _gate_up(u2, w_gate, w_up, zero.reshape(1, 1))
    (w_down,) = weights(2, hh)
    f = _matmul(hh, w_down, name="ffn_down", tm=1024, tn=512, tk=FFN, out_dtype=F32)
    dr2, df, s_ln2 = _res_ln2_loss(xhat1, f, target, _rows8([g2, ln1_g, ln1_b, ln2_g, ln2_b]))

    dgg, duu = _glu_bwd(df, w_down, gg, uu)
    dw_down = _matmul(hh, df, name="dw_down", ta=True, tm=512, tn=1024, tk=n, out_dtype=BF16)
    dw_gate = _matmul(u2, dgg, name="dw_gate", ta=True, tm=1024, tn=512, tk=n, out_dtype=BF16)
    dw_up = _matmul(u2, duu, name="dw_up", ta=True, tm=1024, tn=512, tk=n, out_dtype=BF16)
    zero = grads_out(0, [dw_down, dw_gate, dw_up])
    du2 = _du2(dgg, duu, w_gate, w_up)
    dxp, da, s_ln1 = _ln1_bwd(du2, dr2, xhat1, rstd1, a, _rows8([sc2, ln1_g, ln1_b, g1]) + zero)

    dheads = _matmul(da, w_out, name="d_heads", tb=True, tm=1024, tn=1024, tk=2048, out_dtype=BF16)
    dw_out = _matmul(heads, da, name="dw_out", ta=True, tm=1024, tn=1024, tk=n, out_dtype=BF16)
    zero = grads_out(1, [dw_out])
    delta = _delta_rows(dheads, heads)
    dqa, dka, dva, dsink = _attn_a_bwd(qa, ka, va, sink_b + zero, dheads, n)
    dqb, dkb, dvb = _attn_b_bwd(qb, kb, vb, dheads, lse, delta, n)
    dh_all, s_gain = _qkv_bwd_post(h_all, cos, sin_a, sin_b, q_g, k_g, dqa, dka, dva, dqb, dkb, dvb, n)
    dw_in = _matmul(u_all, dh_all, name="dw_in", ta=True, tm=1024, tn=1024, tk=n + c, out_dtype=BF16)
    zero = grads_out(2, [dw_in])
    du_all = _matmul(dh_all, w_in, name="d_u1", tb=True, tm=_fit(n + c, 1088), tn=1024, tk=IN_WIDTH, out_dtype=F32,
                     after=zero.reshape(1, 1))
    grad_x, s_mod1 = _mod1_bwd(du_all, dxp, x, ctx, _rows8([sc1]) + zero)

    dsink_row = jnp.concatenate([dsink[0, 0:GROUP, 0], dsink[1, 0:GROUP, 0]]).reshape(1, HEADS_A)
    misc = jnp.concatenate([s_gain[0:1], s_gain[1:2], dsink_row,
                            jnp.zeros((1, d - 2 * HEAD_DIM - HEADS_A), F32)], axis=1)
    partial = jnp.concatenate([
        s_mod1[1:2], s_mod1[0:1], s_ln1[4:5],
        s_ln1[1:2], s_ln1[0:1], s_ln2[2:3],
        s_mod1[3:4], s_mod1[2:3],
        s_ln1[2:3], s_ln1[3:4], s_ln2[0:1], s_ln2[1:2],
        s_ln2[3:4], misc, jnp.zeros((2, d), F32)], axis=0)
    return grad_x, partial


ANY = pl.BlockSpec(memory_space=pl.ANY)
VMEM_FULL = pl.BlockSpec(memory_space=pltpu.VMEM)
N_CHIP_PEERS = 3


def _me():
    return lax.axis_index("x"), lax.axis_index("y"), lax.axis_index("c")


def _other_chips(x, y):
    return [(1 - x, y), (x, 1 - y), (1 - x, 1 - y)]


def _shard_of(chip):
    return 2 * chip[0] + chip[1]


def _dev_index(x, y, c):
    return 4 * x + 2 * y + c


def _rcopy(src, dst, send_sems, recv_sems, k, dev):
    return pltpu.make_async_remote_copy(src_ref=src, dst_ref=dst, send_sem=send_sems.at[k], recv_sem=recv_sems.at[k],
                                        device_id=dev, device_id_type=MESH)


BIG = (("w_in", (D_MODEL, IN_WIDTH), 1), ("w_out", (MIX_WIDTH, D_MODEL), 0), ("w_gate", (D_MODEL, FFN), 1),
       ("w_up", (D_MODEL, FFN), 1), ("w_down", (FFN, D_MODEL), 0))


def _sub(ref, axis, idx, size):
    start = pl.multiple_of(idx * size, size)
    return ref.at[pl.ds(start, size), :] if axis == 0 else ref.at[:, pl.ds(start, size)]


def _shape_div(shape, axis, parts):
    return tuple(s // parts if a == axis else s for a, s in enumerate(shape))


def _piece(a, ref, shard, half):
    _, full, axis = BIG[a]
    view = _sub(ref, axis, shard, full[axis] // N_SHARD)
    return _sub(view, 1 - axis, half, full[1 - axis] // 2)


HBM = pl.BlockSpec(memory_space=pltpu.HBM)
SEM = pl.BlockSpec(memory_space=pltpu.SEMAPHORE)
EFFECT = pltpu.SideEffectType.DATAFLOW_SIDE_EFFECTING
BIG_INDEX = {name: a for a, (name, _, _) in enumerate(BIG)}


def _in_hbm(arr):
    return pltpu.with_memory_space_constraint(arr, pltpu.HBM)


def _gather_start(tag, arrs, bufs, prev):
    n_arr = len(arrs)

    def body(*refs):
        ins = refs[:n_arr]
        send_sems, recv_sems = refs[n_arr + 1], refs[n_arr + 2]
        token = refs[-1]
        x, y, c = _me()
        s_me = _shard_of((x, y))
        for i, a in enumerate(arrs):
            mine = _piece(a, ins[i], s_me, c)
            for j, chip in enumerate(_other_chips(x, y)):
                _rcopy(mine, mine, send_sems, recv_sems, N_CHIP_PEERS * i + j, (*chip, c)).start()
        token[...] = jnp.zeros_like(token)

    n_sem = N_CHIP_PEERS * n_arr
    outs = pl.pallas_call(
        body, name="gather_start_" + tag,
        in_specs=[HBM] * n_arr + [ANY],
        out_specs=[SEM, SEM] + [HBM] * n_arr + [VMEM_FULL],
        out_shape=[pltpu.SemaphoreType.DMA((n_sem,)), pltpu.SemaphoreType.DMA((n_sem,))]
        + [pltpu.HBM(BIG[a][1], BF16) for a in arrs] + [jax.ShapeDtypeStruct((8, HEAD_DIM), F32)],
        input_output_aliases={i: 2 + i for i in range(n_arr)},
        compiler_params=pltpu.CompilerParams(has_side_effects=EFFECT),
    )(*[_in_hbm(b) for b in bufs], prev)
    return outs[0], outs[1], list(outs[2:2 + n_arr]), outs[-1]


def _gather_wait(tag, arrs, send_sems, recv_sems, bufs, after):
    n_arr = len(arrs)

    def body(*refs):
        ins = refs[:n_arr]
        send_sems_, recv_sems_ = refs[n_arr], refs[n_arr + 1]
        x, y, c = _me()
        s_me = _shard_of((x, y))
        for i, a in enumerate(arrs):
            mine = _piece(a, ins[i], s_me, c)
            for j, chip in enumerate(_other_chips(x, y)):
                landed = _piece(a, ins[i], _shard_of(chip), c)
                cp = _rcopy(mine, landed, send_sems_, recv_sems_, N_CHIP_PEERS * i + j, (*chip, c))
                cp.wait_send()
                cp.wait_recv()

    outs = pl.pallas_call(
        body, name="gather_wait_" + tag,
        in_specs=[HBM] * n_arr + [SEM, SEM, ANY],
        out_specs=[HBM] * n_arr,
        out_shape=[pltpu.HBM(BIG[a][1], BF16) for a in arrs],
        input_output_aliases={i: i for i in range(n_arr)},
        compiler_params=pltpu.CompilerParams(has_side_effects=EFFECT),
    )(*bufs, send_sems, recv_sems, after)
    return list(outs)


def _gather_forward(tag, arrs, bufs):
    n_arr = len(arrs)

    def body(*refs):
        outs = refs[n_arr:2 * n_arr]
        send_sems, recv_sems = refs[2 * n_arr:]
        x, y, c = _me()
        sibling = (x, y, 1 - c)
        chips = _other_chips(x, y)
        copies = []
        for i, a in enumerate(arrs):
            for j, chip in enumerate(chips):
                landed = _piece(a, outs[i], _shard_of(chip), c)
                cp = _rcopy(landed, landed, send_sems, recv_sems, N_CHIP_PEERS * i + j, sibling)
                cp.start()
                copies.append(cp)
        for i, a in enumerate(arrs):
            for j, chip in enumerate(chips):
                other = _piece(a, outs[i], _shard_of(chip), 1 - c)
                _rcopy(other, other, send_sems, recv_sems, N_CHIP_PEERS * i + j, sibling).wait_recv()
        for cp in copies:
            cp.wait_send()

    n_sem = N_CHIP_PEERS * n_arr
    return list(pl.pallas_call(
        body, name="gather_forward_" + tag,
        in_specs=[ANY] * n_arr, out_specs=[ANY] * n_arr,
        out_shape=[jax.ShapeDtypeStruct(BIG[a][1], BF16) for a in arrs],
        input_output_aliases={i: i for i in range(n_arr)},
        scratch_shapes=[pltpu.SemaphoreType.DMA((n_sem,)), pltpu.SemaphoreType.DMA((n_sem,))],
    )(*bufs))


def _forward_start(tag, arrs, bufs):
    n_arr = len(arrs)

    def body(*refs):
        ins = refs[:n_arr]
        send_sems, recv_sems = refs[n_arr], refs[n_arr + 1]
        token = refs[-1]
        x, y, c = _me()
        for i, a in enumerate(arrs):
            for j, chip in enumerate(_other_chips(x, y)):
                landed = _piece(a, ins[i], _shard_of(chip), c)
                _rcopy(landed, landed, send_sems, recv_sems, N_CHIP_PEERS * i + j, (x, y, 1 - c)).start()
        token[...] = jnp.zeros_like(token)

    n_sem = N_CHIP_PEERS * n_arr
    outs = pl.pallas_call(
        body, name="gather_forward_start_" + tag,
        in_specs=[HBM] * n_arr,
        out_specs=[SEM, SEM] + [HBM] * n_arr + [VMEM_FULL],
        out_shape=[pltpu.SemaphoreType.DMA((n_sem,)), pltpu.SemaphoreType.DMA((n_sem,))]
        + [pltpu.HBM(BIG[a][1], BF16) for a in arrs] + [jax.ShapeDtypeStruct((8, HEAD_DIM), F32)],
        input_output_aliases={i: 2 + i for i in range(n_arr)},
        compiler_params=pltpu.CompilerParams(has_side_effects=EFFECT),
    )(*bufs)
    return outs[0], outs[1], list(outs[2:2 + n_arr]), outs[-1]


def _forward_wait(tag, arrs, send_sems, recv_sems, bufs, after):
    n_arr = len(arrs)

    def body(*refs):
        ins = refs[:n_arr]
        send_sems_, recv_sems_ = refs[n_arr], refs[n_arr + 1]
        x, y, c = _me()
        for i, a in enumerate(arrs):
            for j, chip in enumerate(_other_chips(x, y)):
                mine = _piece(a, ins[i], _shard_of(chip), c)
                other = _piece(a, ins[i], _shard_of(chip), 1 - c)
                cp = _rcopy(mine, other, send_sems_, recv_sems_, N_CHIP_PEERS * i + j, (x, y, 1 - c))
                cp.wait_send()
                cp.wait_recv()

    outs = pl.pallas_call(
        body, name="gather_forward_wait_" + tag,
        in_specs=[HBM] * n_arr + [SEM, SEM, ANY],
        out_specs=[HBM] * n_arr,
        out_shape=[pltpu.HBM(BIG[a][1], BF16) for a in arrs],
        input_output_aliases={i: i for i in range(n_arr)},
        compiler_params=pltpu.CompilerParams(has_side_effects=EFFECT),
    )(*bufs, send_sems, recv_sems, after)
    return list(outs)


def _peers(x, y, c):
    return [(x ^ (mask >> 2), y ^ ((mask >> 1) & 1), c ^ (mask & 1)) for mask in range(1, N_DEV)]


def _received_shape(a):
    _, full, axis = BIG[a]
    return (N_DEV - 1,) + _shape_div(_shape_div(full, 1 - axis, 2), axis, N_SHARD)


def _pieces_start(tag, arrs, dws):
    n_arr = len(arrs)

    def body(*refs):
        srcs, lands = refs[:n_arr], refs[n_arr:2 * n_arr]
        send_sems, recv_sems = refs[2 * n_arr], refs[2 * n_arr + 1]
        token = refs[-1]
        x, y, c = _me()
        for i, a in enumerate(arrs):
            for k, peer in enumerate(_peers(x, y, c)):
                src = _piece(a, srcs[i], _shard_of(peer[:2]), peer[2])
                _rcopy(src, lands[i].at[k], send_sems, recv_sems, (N_DEV - 1) * i + k, peer).start()
        token[...] = jnp.zeros_like(token)

    n_sem = (N_DEV - 1) * n_arr
    lands = [_in_hbm(lax.empty(_received_shape(a), BF16)) for a in arrs]
    outs = pl.pallas_call(
        body, name="grad_pieces_start_" + tag,
        in_specs=[HBM] * (2 * n_arr),
        out_specs=[SEM, SEM] + [HBM] * (2 * n_arr) + [VMEM_FULL],
        out_shape=[pltpu.SemaphoreType.DMA((n_sem,)), pltpu.SemaphoreType.DMA((n_sem,))]
        + [pltpu.HBM(BIG[a][1], BF16) for a in arrs] + [pltpu.HBM(_received_shape(a), BF16) for a in arrs]
        + [jax.ShapeDtypeStruct((8, HEAD_DIM), F32)],
        input_output_aliases={i: 2 + i for i in range(2 * n_arr)},
        compiler_params=pltpu.CompilerParams(has_side_effects=EFFECT),
    )(*[_in_hbm(dw) for dw in dws], *lands)
    return outs[0], outs[1], list(outs[2:2 + n_arr]), list(outs[2 + n_arr:2 + 2 * n_arr]), outs[-1]


def _pieces_wait(tag, arrs, send_sems, recv_sems, dws, lands, after):
    n_arr = len(arrs)

    def body(*refs):
        srcs, lands_ = refs[:n_arr], refs[n_arr:2 * n_arr]
        send_sems_, recv_sems_ = refs[2 * n_arr], refs[2 * n_arr + 1]
        x, y, c = _me()
        for i, a in enumerate(arrs):
            for k, peer in enumerate(_peers(x, y, c)):
                src = _piece(a, srcs[i], _shard_of(peer[:2]), peer[2])
                cp = _rcopy(src, lands_[i].at[k], send_sems_, recv_sems_, (N_DEV - 1) * i + k, peer)
                cp.wait_send()
                cp.wait_recv()

    outs = pl.pallas_call(
        body, name="grad_pieces_wait_" + tag,
        in_specs=[HBM] * (2 * n_arr) + [SEM, SEM, ANY],
        out_specs=[HBM] * (2 * n_arr),
        out_shape=[pltpu.HBM(BIG[a][1], BF16) for a in arrs] + [pltpu.HBM(_received_shape(a), BF16) for a in arrs],
        input_output_aliases={i: i for i in range(2 * n_arr)},
        compiler_params=pltpu.CompilerParams(has_side_effects=EFFECT),
    )(*dws, *lands, send_sems, recv_sems, after)
    return list(outs[:n_arr]), list(outs[n_arr:])


def _join_start(tag, g_halves):
    n_arr = len(g_halves)

    def body(*refs):
        srcs, lands = refs[:n_arr], refs[n_arr:2 * n_arr]
        send_sems, recv_sems = refs[2 * n_arr], refs[2 * n_arr + 1]
        token = refs[-1]
        x, y, c = _me()
        for i in range(n_arr):
            _rcopy(srcs[i], lands[i], send_sems, recv_sems, i, (x, y, 1 - c)).start()
        token[...] = jnp.zeros_like(token)

    shapes = [pltpu.HBM(g.shape, F32) for g in g_halves]
    outs = pl.pallas_call(
        body, name="grad_join_start_" + tag,
        in_specs=[HBM] * (2 * n_arr),
        out_specs=[SEM, SEM] + [HBM] * (2 * n_arr) + [VMEM_FULL],
        out_shape=[pltpu.SemaphoreType.DMA((n_arr,)), pltpu.SemaphoreType.DMA((n_arr,))] + shapes + shapes
        + [jax.ShapeDtypeStruct((8, HEAD_DIM), F32)],
        input_output_aliases={i: 2 + i for i in range(2 * n_arr)},
        compiler_params=pltpu.CompilerParams(has_side_effects=EFFECT),
    )(*[_in_hbm(g) for g in g_halves], *[_in_hbm(lax.empty(g.shape, F32)) for g in g_halves])
    return outs[0], outs[1], list(outs[2:2 + n_arr]), list(outs[2 + n_arr:2 + 2 * n_arr]), outs[-1]


def _join_wait(tag, send_sems, recv_sems, g_halves, lands, after):
    n_arr = len(g_halves)

    def body(*refs):
        srcs, lands_ = refs[:n_arr], refs[n_arr:2 * n_arr]
        send_sems_, recv_sems_ = refs[2 * n_arr], refs[2 * n_arr + 1]
        x, y, c = _me()
        for i in range(n_arr):
            cp = _rcopy(srcs[i], lands_[i], send_sems_, recv_sems_, i, (x, y, 1 - c))
            cp.wait_send()
            cp.wait_recv()

    shapes = [pltpu.HBM(g.shape, F32) for g in g_halves]
    outs = pl.pallas_call(
        body, name="grad_join_wait_" + tag,
        in_specs=[HBM] * (2 * n_arr) + [SEM, SEM, ANY],
        out_specs=[HBM] * (2 * n_arr),
        out_shape=shapes + shapes,
        input_output_aliases={i: i for i in range(2 * n_arr)},
        compiler_params=pltpu.CompilerParams(has_side_effects=EFFECT),
    )(*g_halves, *lands, send_sems, recv_sems, after)
    return list(outs[:n_arr]), list(outs[n_arr:])


def _piece_sum(a, dw, shard, core, received):
    name, full, axis = BIG[a]
    rows, cols = _received_shape(a)[1:]
    tr = _fit(rows, ROW_TILE)
    nbr = rows // tr

    def body(w_ref, dw_ref, rec_ref, o_ref):
        acc = dw_ref[...].astype(F32)
        for k in range(N_DEV - 1):
            acc = acc + rec_ref[k].astype(F32)
        o_ref[...] = acc

    if axis == 0:
        own = pl.BlockSpec((tr, cols), lambda i, w: (w[0] * nbr + i, w[1]))
    else:
        own = pl.BlockSpec((tr, cols), lambda i, w: (w[1] * nbr + i, w[0]))
    return pl.pallas_call(
        body, name="grad_sum_pieces_" + name,
        grid_spec=pltpu.PrefetchScalarGridSpec(
            num_scalar_prefetch=1, grid=(nbr,),
            in_specs=[own, pl.BlockSpec((N_DEV - 1, tr, cols), lambda i, w: (0, i, 0))],
            out_specs=pl.BlockSpec((tr, cols), lambda i, w: (i, 0))),
        out_shape=jax.ShapeDtypeStruct((rows, cols), F32),
        compiler_params=_params(("parallel",)),
    )(jnp.stack([shard, core]).astype(jnp.int32), dw, received)


def _scatter_begin(tag, arrs, dws):
    send_sems, recv_sems, dws, lands, token = _pieces_start(tag, arrs, dws)
    return (send_sems, recv_sems, dws, lands), token[0, 0]


def _scatter_reduce(tag, arrs, state, after):
    x, y, c = _me()
    send_sems, recv_sems, dws, lands = state
    dws, lands = _pieces_wait(tag, arrs, send_sems, recv_sems, dws, lands, after)
    g_own = [_piece_sum(a, dw, _shard_of((x, y)), c, r) for a, dw, r in zip(arrs, dws, lands)]
    send_sems, recv_sems, g_own, lands, token = _join_start(tag, g_own)
    return (send_sems, recv_sems, g_own, lands), token


def _scatter_end(tag, state, after):
    return _join_wait(tag, *state, after)


def _gather_rows(block, name, after):
    r, d = block.shape

    def body(in_ref, after_ref, out_ref, send_sems, recv_sems):
        x, y, c = _me()
        out_ref[_dev_index(x, y, c)] = in_ref[...]
        copies = []
        for mask in range(1, N_DEV):
            peer = (x ^ (mask >> 2), y ^ ((mask >> 1) & 1), c ^ (mask & 1))
            cp = _rcopy(in_ref, out_ref.at[_dev_index(x, y, c)], send_sems, recv_sems, mask - 1, peer)
            cp.start()
            copies.append((cp, peer))
        for mask in range(1, N_DEV):
            peer = (x ^ (mask >> 2), y ^ ((mask >> 1) & 1), c ^ (mask & 1))
            landed = out_ref.at[_dev_index(*peer)]
            _rcopy(landed, landed, send_sems, recv_sems, mask - 1, peer).wait_recv()
        for cp, _ in copies:
            cp.wait_send()

    return pl.pallas_call(
        body, name=name, in_specs=[VMEM_FULL, ANY], out_specs=VMEM_FULL,
        out_shape=jax.ShapeDtypeStruct((N_DEV, r, d), F32),
        scratch_shapes=[pltpu.SemaphoreType.DMA((N_DEV - 1,)), pltpu.SemaphoreType.DMA((N_DEV - 1,))],
    )(block, after)


ADA_ROWS = 80
ADA_W = 6 * D_MODEL // N_SHARD


def _ada_forward(c_block, cctx_block, w_ada, b_shard):
    d = c_block.shape[1]

    def body(c_ref, cc_ref, w_ref, b_ref, act_ref, mods_ref, raw, mloc, send_sems, recv_sems):
        x, y, c = _me()
        me = _dev_index(x, y, c)
        s_me = _shard_of((x, y))
        raw[72:ADA_ROWS, :] = jnp.zeros((ADA_ROWS - 72, d), F32)
        raw[pl.ds(pl.multiple_of(me * 8, 8), 8), :] = c_ref[...]
        raw[64:72, :] = cc_ref[...]
        sends = []
        for mask in range(1, N_DEV):
            peer = (x ^ (mask >> 2), y ^ ((mask >> 1) & 1), c ^ (mask & 1))
            cp = _rcopy(c_ref, raw.at[pl.ds(pl.multiple_of(me * 8, 8), 8), :], send_sems, recv_sems, mask - 1, peer)
            cp.start()
            sends.append(cp)
        for mask in range(1, N_DEV):
            peer = (x ^ (mask >> 2), y ^ ((mask >> 1) & 1), c ^ (mask & 1))
            landed = raw.at[pl.ds(pl.multiple_of(_dev_index(*peer) * 8, 8), 8), :]
            _rcopy(landed, landed, send_sems, recv_sems, mask - 1, peer).wait_recv()
        v = raw[...]
        act = v * jax.nn.sigmoid(v)
        act_ref[...] = act
        mloc[...] = lax.dot_general(act.astype(BF16), w_ref[...].astype(BF16), NN,
                                    preferred_element_type=F32) + b_ref[...]
        mods_ref[s_me, 0:8, :] = mloc[pl.ds(pl.multiple_of(me * 8, 8), 8), :]
        mods_ref[s_me, 8:16, :] = mloc[64:72, :]
        base = N_DEV - 1
        for j, chip in enumerate(_other_chips(x, y)):
            peer = (*chip, c)
            rows = mloc.at[pl.ds(pl.multiple_of(_dev_index(*peer) * 8, 8), 8), :]
            cp = _rcopy(rows, mods_ref.at[s_me, 0:8, :], send_sems, recv_sems, base + 2 * j, peer)
            cp.start()
            sends.append(cp)
            cp = _rcopy(mloc.at[64:72, :], mods_ref.at[s_me, 8:16, :], send_sems, recv_sems, base + 2 * j + 1, peer)
            cp.start()
            sends.append(cp)
        for j, chip in enumerate(_other_chips(x, y)):
            for part in range(2):
                landed = mods_ref.at[_shard_of(chip), 8 * part:8 * part + 8, :]
                _rcopy(landed, landed, send_sems, recv_sems, base + 2 * j + part, (*chip, c)).wait_recv()
        for cp in sends:
            cp.wait_send()

    n_sem = N_DEV - 1 + 2 * N_CHIP_PEERS
    return pl.pallas_call(
        body, name="ada_forward",
        in_specs=[VMEM_FULL] * 4, out_specs=[VMEM_FULL, VMEM_FULL],
        out_shape=[jax.ShapeDtypeStruct((ADA_ROWS, d), F32), jax.ShapeDtypeStruct((N_SHARD, 16, ADA_W), F32)],
        scratch_shapes=[pltpu.VMEM((ADA_ROWS, d), F32), pltpu.VMEM((ADA_ROWS, ADA_W), F32),
                        pltpu.SemaphoreType.DMA((n_sem,)), pltpu.SemaphoreType.DMA((n_sem,))],
        compiler_params=pltpu.CompilerParams(vmem_limit_bytes=VMEM_LIMIT),
    )(c_block, cctx_block, w_ada, b_shard)


def _small_reduce(gathered):
    d = gathered.shape[2]

    def body(g_ref, o_ref):
        tot = g_ref[0]
        for i in range(1, N_DEV):
            tot = tot + g_ref[i]
        o_ref[...] = tot
        o_ref[0:2, :] = tot[0:2] + tot[6:8]
        o_ref[12:13, :] = jnp.broadcast_to(jnp.sum(tot[12:13], axis=1, keepdims=True), (1, d))

    return pl.pallas_call(body, name="small_reduce", in_specs=[VMEM_FULL], out_specs=VMEM_FULL,
                          out_shape=jax.ShapeDtypeStruct((16, d), F32))(gathered)


def _cctx_grad(gathered, c_ctx):
    d = gathered.shape[2]

    def body(g_ref, c_ref, o_ref):
        tot = g_ref[0, 0:1, :]
        for chip in range(1, N_SHARD):
            tot = tot + g_ref[2 * chip, 0:1, :]
        v = c_ref[...]
        sig = jax.nn.sigmoid(v)
        o_ref[...] = tot * (sig * (1.0 + v * (1.0 - sig)))

    return pl.pallas_call(body, name="cctx_grad", in_specs=[VMEM_FULL, VMEM_FULL], out_specs=VMEM_FULL,
                          out_shape=jax.ShapeDtypeStruct((1, d), F32))(gathered, c_ctx.reshape(1, d))


def _cast_into_full(w, shard, full, axis, name):
    r, cdim = w.shape
    tr = _fit(r, ROW_TILE)
    nbr = r // tr

    def body(s_ref, w_ref, o_ref):
        o_ref[...] = w_ref[...].astype(BF16)

    if axis == 0:
        out_spec = pl.BlockSpec((tr, cdim), lambda i, s: (s[0] * nbr + i, 0))
    else:
        out_spec = pl.BlockSpec((tr, cdim), lambda i, s: (i, s[0]))
    return pl.pallas_call(
        body, name=name,
        grid_spec=pltpu.PrefetchScalarGridSpec(
            num_scalar_prefetch=1, grid=(nbr,), in_specs=[pl.BlockSpec((tr, cdim), lambda i, s: (i, 0))],
            out_specs=out_spec),
        out_shape=jax.ShapeDtypeStruct(full, BF16), compiler_params=_params(("parallel",)),
    )(shard.reshape(1).astype(jnp.int32), w)


def _adamw_halves(w, g_own, g_other, m, v, core, axis, name):
    r, cdim = w.shape
    hr, hc = (r // 2, cdim) if axis == 1 else (r, cdim // 2)
    assert g_own.shape == (hr, hc) and g_other.shape == (hr, hc)
    tr = _fit(hr, 256)
    nb = hr // tr
    c1 = 1.0 - ADAM_B1 ** ADAM_STEP
    c2 = 1.0 - ADAM_B2 ** ADAM_STEP

    def body(c_ref, w_ref, go_ref, gt_ref, m_ref, v_ref, g_ref, d_ref, nm_ref, nv_ref):
        gv = jnp.where(pl.program_id(0) == c_ref[0], go_ref[...], gt_ref[...])
        nm = ADAM_B1 * m_ref[...] + (1.0 - ADAM_B1) * gv
        nv = ADAM_B2 * v_ref[...] + (1.0 - ADAM_B2) * (gv * gv)
        g_ref[...] = gv
        nm_ref[...] = nm
        nv_ref[...] = nv
        d_ref[...] = -ADAM_LR * ((nm / c1) / (jnp.sqrt(nv / c2) + ADAM_EPS) + ADAM_WD * w_ref[...])

    if axis == 1:
        big = pl.BlockSpec((tr, hc), lambda p, i, c: (p * nb + i, 0))
    else:
        big = pl.BlockSpec((tr, hc), lambda p, i, c: (i, p))
    half = pl.BlockSpec((tr, hc), lambda p, i, c: (i, 0))
    sh = jax.ShapeDtypeStruct((r, cdim), F32)
    return pl.pallas_call(
        body, name=name,
        grid_spec=pltpu.PrefetchScalarGridSpec(
            num_scalar_prefetch=1, grid=(2, nb), in_specs=[big, half, half, big, big], out_specs=[big] * 4),
        out_shape=[sh] * 4, compiler_params=_params(("parallel", "parallel")),
    )(core.reshape(1).astype(jnp.int32), w, g_own, g_other, m, v)


def _adamw(w, g, m, v, name):
    r, cdim = w.shape
    tr = _fit(r, 128) if r % (ROW_TILE // 4) == 0 else r
    c1 = 1.0 - ADAM_B1 ** ADAM_STEP
    c2 = 1.0 - ADAM_B2 ** ADAM_STEP

    def body(w_ref, g_ref, m_ref, v_ref, d_ref, nm_ref, nv_ref):
        gv = g_ref[...]
        nm = ADAM_B1 * m_ref[...] + (1.0 - ADAM_B1) * gv
        nv = ADAM_B2 * v_ref[...] + (1.0 - ADAM_B2) * (gv * gv)
        nm_ref[...] = nm
        nv_ref[...] = nv
        d_ref[...] = -ADAM_LR * ((nm / c1) / (jnp.sqrt(nv / c2) + ADAM_EPS) + ADAM_WD * w_ref[...])

    spec = pl.BlockSpec((tr, cdim), lambda i: (i, 0))
    sh = jax.ShapeDtypeStruct((r, cdim), F32)
    return pl.pallas_call(body, name=name, grid=(r // tr,), in_specs=[spec] * 4, out_specs=[spec] * 3,
                          out_shape=[sh, sh, sh], compiler_params=_params(("parallel",)))(w, g, m, v)


SMALL = (("c_ctx", D_MODEL), ("b_ada", 6 * D_MODEL), ("q_norm_g", HEAD_DIM), ("k_norm_g", HEAD_DIM),
         ("sink_logit", HEADS_A), ("ln1_g", D_MODEL), ("ln1_b", D_MODEL), ("ln2_g", D_MODEL), ("ln2_b", D_MODEL))
WEIGHT_ORDER = ("c_ctx", "w_ada", "b_ada", "w_in", "q_norm_g", "k_norm_g", "sink_logit", "w_out", "ln1_g", "ln1_b",
                "w_gate", "w_up", "w_down", "ln2_g", "ln2_b")


def kernel(x, c, ctx, c_ctx, w_ada, b_ada, w_in, q_norm_g, k_norm_g, sink_logit, w_out, ln1_g, ln1_b, w_gate, w_up, w_down, ln2_g, ln2_b, loss_target, m_c_ctx, m_w_ada, m_b_ada, m_w_in, m_q_norm_g, m_k_norm_g, m_sink_logit, m_w_out, m_ln1_g, m_ln1_b, m_w_gate, m_w_up, m_w_down, m_ln2_g, m_ln2_b, v_c_ctx, v_w_ada, v_b_ada, v_w_in, v_q_norm_g, v_k_norm_g, v_sink_logit, v_w_out, v_ln1_g, v_ln1_b, v_w_gate, v_w_up, v_w_down, v_ln2_g, v_ln2_b):
    d = D_MODEL
    w = dict(c_ctx=c_ctx, w_ada=w_ada[0], b_ada=b_ada, w_in=w_in[0], q_norm_g=q_norm_g, k_norm_g=k_norm_g,
             sink_logit=sink_logit, w_out=w_out[0], ln1_g=ln1_g, ln1_b=ln1_b, w_gate=w_gate[0], w_up=w_up[0],
             w_down=w_down[0], ln2_g=ln2_g, ln2_b=ln2_b)
    m = dict(c_ctx=m_c_ctx, w_ada=m_w_ada[0], b_ada=m_b_ada, w_in=m_w_in[0], q_norm_g=m_q_norm_g, k_norm_g=m_k_norm_g,
             sink_logit=m_sink_logit, w_out=m_w_out[0], ln1_g=m_ln1_g, ln1_b=m_ln1_b, w_gate=m_w_gate[0],
             w_up=m_w_up[0], w_down=m_w_down[0], ln2_g=m_ln2_g, ln2_b=m_ln2_b)
    v = dict(c_ctx=v_c_ctx, w_ada=v_w_ada[0], b_ada=v_b_ada, w_in=v_w_in[0], q_norm_g=v_q_norm_g, k_norm_g=v_k_norm_g,
             sink_logit=v_sink_logit, w_out=v_w_out[0], ln1_g=v_ln1_g, ln1_b=v_ln1_b, w_gate=v_w_gate[0],
             w_up=v_w_up[0], w_down=v_w_down[0], ln2_g=v_ln2_g, ln2_b=v_ln2_b)
    mx, my, mc = _me()
    s_me = _shard_of((mx, my))
    me = _dev_index(mx, my, mc)
    pad8 = lambda row: jnp.concatenate([row.reshape(1, -1), jnp.zeros((7, row.size), F32)], axis=0)

    b_shard = lax.dynamic_slice(b_ada, (0, s_me * ADA_W), (1, ADA_W))
    act, mods4 = _ada_forward(pad8(c), pad8(c_ctx), w["w_ada"], b_shard)

    gathers = []
    prev, shard = mods4, s_me
    for k, names in enumerate(W_GROUPS):
        arrs = tuple(BIG_INDEX[name] for name in names)
        bufs = [_cast_into_full(w[name], shard, BIG[a][1], BIG[a][2], "cast_" + name) for name, a in zip(names, arrs)]
        send_sems, recv_sems, thru, prev = _gather_start("g%d" % k, arrs, bufs, prev)
        shard = s_me + prev[0, 0].astype(jnp.int32)
        gathers.append((arrs, send_sems, recv_sems, thru))

    forwards = {}

    def prefetch(k, after):
        arrs, send_sems, recv_sems, thru = gathers[k]
        landed = _gather_wait("g%d" % k, arrs, send_sems, recv_sems, thru, after)
        fwd_send, fwd_recv, landed, token = _forward_start("g%d" % k, arrs, landed)
        forwards[k] = (fwd_send, fwd_recv, landed)
        return token[0, 0]

    def weights(k, after):
        arrs, send_sems, recv_sems, thru = gathers[k]
        if k in forwards:
            return _forward_wait("g%d" % k, arrs, *forwards[k], after)
        landed = _gather_wait("g%d" % k, arrs, send_sems, recv_sems, thru, after)
        return _gather_forward("g%d" % k, arrs, landed)

    mod = jnp.transpose(mods4[:, 0:1, :], (1, 0, 2)).reshape(1, 6 * d) + prev[0, 0]
    mod_ctx = jnp.transpose(mods4[:, 8:9, :], (1, 0, 2)).reshape(1, 6 * d)

    scatters = {}

    def grads_out(k, dws):
        arrs = tuple(BIG_INDEX[name] for name in G_GROUPS[k])
        scatters[k], zero = _scatter_begin("g%d" % k, arrs, dws)
        return zero

    grad_x, partial = _layer_fwd_bwd(x[0], ctx[0], loss_target[0], mod, mod_ctx, weights, prefetch, grads_out,
                                     q_norm_g, k_norm_g, sink_logit, ln1_g, ln1_b, ln2_g, ln2_b)
    grads, delta, new_m, new_v = {}, {}, {}, {}

    after, joins = partial, []
    for k, names in enumerate(G_GROUPS):
        arrs = tuple(BIG_INDEX[name] for name in names)
        state, after = _scatter_reduce("g%d" % k, arrs, scatters[k], after)
        joins.append(state)
    for k, names in enumerate(G_GROUPS):
        g_own, g_other = _scatter_end("g%d" % k, joins[k], after)
        for name, own, other in zip(names, g_own, g_other):
            grads[name], delta[name], new_m[name], new_v[name] = _adamw_halves(
                w[name], own, other, m[name], v[name], mc, BIG[BIG_INDEX[name]][2], "adamw_" + name)
            after = new_v[name]

    gathered = _gather_rows(partial, "gather_partials", after)
    tot = _small_reduce(gathered)
    grads["b_ada"] = tot[0:6].reshape(1, 6 * d)
    grads["ln1_g"], grads["ln1_b"], grads["ln2_g"], grads["ln2_b"] = tot[8:9], tot[9:10], tot[10:11], tot[11:12]
    grads["q_norm_g"] = tot[13:14, 0:HEAD_DIM]
    grads["k_norm_g"] = tot[13:14, HEAD_DIM:2 * HEAD_DIM]
    grads["sink_logit"] = tot[13:14, 2 * HEAD_DIM:2 * HEAD_DIM + HEADS_A]
    loss = tot[12, 0]

    dm_all = gathered[:, 0:6, :].reshape(N_DEV, 6 * d)
    dmc_tot = jnp.concatenate([tot[6:8].reshape(1, 2 * d), jnp.zeros((1, 4 * d), F32)], axis=1)
    dm_rows = jnp.concatenate([pad8(dm_all[i]) for i in range(N_DEV)] + [pad8(dmc_tot), jnp.zeros((8, 6 * d), F32)], axis=0)
    dm_shard = lax.dynamic_slice(dm_rows, (0, s_me * ADA_W), (ADA_ROWS, ADA_W))
    grads["w_ada"] = _matmul(act, dm_shard, name="dw_ada", ta=True, tm=1024, tn=1024, tk=ADA_ROWS, out_dtype=F32)
    dmc_shard = lax.dynamic_slice(pad8(dmc_tot), (0, s_me * ADA_W), (8, ADA_W))
    cc_part = _matmul(dmc_shard, w["w_ada"], name="d_cctx", tb=True, tm=8, tn=1024, tk=1536, out_dtype=F32)
    grads["c_ctx"] = _cctx_grad(_gather_rows(cc_part, "gather_cctx", tot), c_ctx).reshape(d)

    delta["w_ada"], new_m["w_ada"], new_v["w_ada"] = _adamw(w["w_ada"], grads["w_ada"], m["w_ada"], v["w_ada"],
                                                            "adamw_w_ada")
    pack = lambda t: jnp.concatenate([t[name].reshape(1, size) for name, size in SMALL], axis=1)
    pd, pm, pv = _adamw(pack(w), pack(grads), pack(m), pack(v), "adamw_small")
    off = 0
    for name, size in SMALL:
        delta[name], new_m[name], new_v[name] = [t[:, off:off + size].reshape(w[name].shape) for t in (pd, pm, pv)]
        grads[name] = grads[name].reshape(w[name].shape)
        off += size

    lead = lambda name, t: t[None] if name in ("w_ada", "w_in", "w_out", "w_gate", "w_up", "w_down") else t
    outs = [loss, grad_x[None]]
    for group in (grads, delta, new_m, new_v):
        outs += [lead(name, group[name]) for name in WEIGHT_ORDER]
    return tuple(outs)
```

```python
import functools
import math

import jax
import jax.numpy as jnp
from jax import lax
from jax.experimental import pallas as pl
from jax.experimental.pallas import tpu as pltpu

F32 = jnp.float32
BF16 = jnp.bfloat16
MESH = pl.DeviceIdType.MESH

D_MODEL = 2048
HEAD_DIM = 128
HEADS_A = 8
HEADS_B = 8
KV_A = 2
KV_B = 2
GROUP = 4
GRID_W = 64
WINDOW = 128
BLOCK = 128
FFN = 5632
IN_WIDTH = 3072
MIX_WIDTH = 2048
ROPE_THETA = 10000.0
EPS = 1e-6
ATTN_SCALE = HEAD_DIM ** -0.5
DN_ALPHA = 2.0 ** 0.25
N_SHARD = 4
N_DEV = 8

ADAM_LR = 0.001
ADAM_B1 = 0.9
ADAM_B2 = 0.999
ADAM_EPS = 1e-08
ADAM_WD = 0.01
ADAM_STEP = 10

QA0, KA0, VA0, QB0, KB0, VB0 = 0, 1024, 1280, 1536, 2560, 2816

VMEM_LIMIT = 56 * 1024 * 1024
ROW_TILE = 256
NN = (((1,), (0,)), ((), ()))
NT = (((1,), (1,)), ((), ()))
TN = (((0,), (0,)), ((), ()))


def _fit(total, pref):
    step = ROW_TILE // 4
    best = step
    for cand in range(step, pref + 1, step):
        if total % cand == 0:
            best = cand
    return best


def _params(sem=None):
    return pltpu.CompilerParams(dimension_semantics=sem, vmem_limit_bytes=VMEM_LIMIT)


def _matmul(a, b, *, name, ta=False, tb=False, tm, tn, tk, out_dtype, after=None):
    m = a.shape[1] if ta else a.shape[0]
    k = a.shape[0] if ta else a.shape[1]
    n = b.shape[0] if tb else b.shape[1]
    assert (b.shape[1] if tb else b.shape[0]) == k
    tm, tn, tk = min(tm, m), min(tn, n), min(tk, k)
    assert m % tm == 0 and n % tn == 0 and k % tk == 0, (name, m, n, k, tm, tn, tk)
    nk = k // tk
    dn = (((0 if ta else 1,), (1 if tb else 0,)), ((), ()))

    def product(a_ref, b_ref):
        return lax.dot_general(a_ref[...].astype(BF16), b_ref[...].astype(BF16), dn, preferred_element_type=F32)

    def body_whole_k(a_ref, b_ref, *rest):
        o_ref = rest[-1]
        o_ref[...] = product(a_ref, b_ref).astype(o_ref.dtype)

    def body(a_ref, b_ref, *rest):
        o_ref, acc_ref = rest[-2:]
        kk = pl.program_id(2)
        part = product(a_ref, b_ref)

        @pl.when(kk == 0)
        def _():
            acc_ref[...] = part

        @pl.when(kk != 0)
        def _():
            acc_ref[...] += part

        @pl.when(kk == nk - 1)
        def _():
            o_ref[...] = acc_ref[...].astype(o_ref.dtype)

    a_spec = (pl.BlockSpec((tk, tm), lambda i, j, kk: (kk, i)) if ta
              else pl.BlockSpec((tm, tk), lambda i, j, kk: (i, kk)))
    b_spec = (pl.BlockSpec((tn, tk), lambda i, j, kk: (j, kk)) if tb
              else pl.BlockSpec((tk, tn), lambda i, j, kk: (kk, j)))
    return pl.pallas_call(
        body_whole_k if nk == 1 else body, name=name, grid=(m // tm, n // tn, nk),
        in_specs=[a_spec, b_spec] + ([] if after is None else [pl.BlockSpec(memory_space=pl.ANY)]),
        out_specs=pl.BlockSpec((tm, tn), lambda i, j, kk: (i, j)),
        out_shape=jax.ShapeDtypeStruct((m, n), out_dtype),
        scratch_shapes=[] if nk == 1 else [pltpu.VMEM((tm, tn), F32)],
        compiler_params=_params(("parallel", "parallel", "arbitrary")),
    )(a, b, *([] if after is None else [after]))


def _modulate_rows(x, ctx, mods):
    n, d = x.shape
    c = ctx.shape[0]
    nx = n // ROW_TILE
    assert c == ROW_TILE

    def body(x_ref, ctx_ref, mods_ref, o_ref):
        i = pl.program_id(0)

        @pl.when(i < nx)
        def _():
            o_ref[...] = (x_ref[...] * (1.0 + mods_ref[0:1, :]) + mods_ref[1:2, :]).astype(BF16)

        @pl.when(i >= nx)
        def _():
            o_ref[...] = (ctx_ref[...] * (1.0 + mods_ref[2:3, :]) + mods_ref[3:4, :]).astype(BF16)

    return pl.pallas_call(
        body, name="modulate_rows", grid=(nx + 1,),
        in_specs=[pl.BlockSpec((ROW_TILE, d), lambda i: (jnp.minimum(i, nx - 1), 0)),
                  pl.BlockSpec((ROW_TILE, d), lambda i: (0, 0)),
                  pl.BlockSpec((8, d), lambda i: (0, 0))],
        out_specs=pl.BlockSpec((ROW_TILE, d), lambda i: (i, 0)),
        out_shape=jax.ShapeDtypeStruct((n + c, d), BF16),
        compiler_params=_params(("parallel",)),
    )(x, ctx, mods)


def _rope_tables(n, c):
    rows = n // GRID_W
    row_ids = jnp.repeat(jnp.arange(rows, dtype=F32), GRID_W)
    col_ids = jnp.tile(jnp.arange(GRID_W, dtype=F32), rows)
    axis_dim = HEAD_DIM // 2
    inv_freq = jnp.power(ROPE_THETA, -jnp.arange(0, axis_dim, 2, dtype=F32) / axis_dim)
    ang_r = row_ids[:, None] * inv_freq
    ang_c = col_ids[:, None] * inv_freq
    ang = jnp.concatenate([ang_r, ang_r, ang_c, ang_c], axis=-1)
    cos, sin = jnp.cos(ang), jnp.sin(ang)
    quarter = (jnp.arange(HEAD_DIM) // (HEAD_DIM // 4)) % 2
    sin_a = jnp.where(quarter == 0, -sin, 0.0)
    sin_b = jnp.where(quarter == 1, sin, 0.0)
    pad = lambda t, v: jnp.concatenate([t, jnp.full((c, HEAD_DIM), v, F32)], axis=0)
    return pad(cos, 1.0), pad(sin_a, 0.0), pad(sin_b, 0.0)


def _rope(x, cos, sin_a, sin_b):
    return x * cos + pltpu.roll(x, 96, 1) * sin_a + pltpu.roll(x, 32, 1) * sin_b


def _rope_t(dy, cos, sin_a, sin_b):
    return dy * cos - pltpu.roll(dy, 96, 1) * sin_a - pltpu.roll(dy, 32, 1) * sin_b


def _rms(x):
    r = lax.rsqrt(jnp.mean(x * x, axis=-1, keepdims=True) + EPS)
    return x * r, r


def _qkv_post(h_all, cos, sin_a, sin_b, q_g, k_g):
    t = h_all.shape[0]
    nt = t // ROW_TILE

    def body(h_ref, cos_ref, sa_ref, sb_ref, qg_ref, kg_ref, qa_ref, ka_ref, va_ref, qb_ref, kb_ref, vb_ref):
        cos_, sa, sb = cos_ref[...], sa_ref[...], sb_ref[...]
        sl = lambda off, hh: h_ref[:, off + hh * HEAD_DIM: off + (hh + 1) * HEAD_DIM]
        for hh in range(HEADS_A):
            qa_ref[hh] = (_rope(sl(QA0, hh), cos_, sa, sb) * ATTN_SCALE).astype(BF16)
        for hh in range(KV_A):
            ka_ref[hh] = _rope(sl(KA0, hh), cos_, sa, sb).astype(BF16)
            va_ref[hh] = sl(VA0, hh).astype(BF16)
        for hh in range(HEADS_B):
            xn, _ = _rms(sl(QB0, hh))
            qb_ref[hh] = (_rope(xn * qg_ref[...], cos_, sa, sb) * ATTN_SCALE).astype(BF16)
        for hh in range(KV_B):
            xn, _ = _rms(sl(KB0, hh))
            kb_ref[hh] = _rope(xn * kg_ref[...], cos_, sa, sb).astype(BF16)
            vb_ref[hh] = sl(VB0, hh).astype(BF16)

    tab = pl.BlockSpec((ROW_TILE, HEAD_DIM), lambda i: (i, 0))
    gain = pl.BlockSpec((1, HEAD_DIM), lambda i: (0, 0))
    hs = lambda nh: pl.BlockSpec((nh, ROW_TILE, HEAD_DIM), lambda i: (0, i, 0))
    sh = lambda nh: jax.ShapeDtypeStruct((nh, t, HEAD_DIM), BF16)
    return pl.pallas_call(
        body, name="qkv_post", grid=(nt,),
        in_specs=[pl.BlockSpec((ROW_TILE, IN_WIDTH), lambda i: (i, 0)), tab, tab, tab, gain, gain],
        out_specs=[hs(HEADS_A), hs(KV_A), hs(KV_A), hs(HEADS_B), hs(KV_B), hs(KV_B)],
        out_shape=[sh(HEADS_A), sh(KV_A), sh(KV_A), sh(HEADS_B), sh(KV_B), sh(KV_B)],
        compiler_params=_params(("parallel",)),
    )(h_all, cos, sin_a, sin_b, q_g, k_g)


def _qkv_bwd_post(h_all, cos, sin_a, sin_b, q_g, k_g, dqa, dka, dva, dqb, dkb, dvb, n):
    t = h_all.shape[0]
    nt = t // ROW_TILE
    nx = n // ROW_TILE

    def body(h_ref, cos_ref, sa_ref, sb_ref, qg_ref, kg_ref,
             dqa_ref, dka_ref, dva_ref, dqb_ref, dkb_ref, dvb_ref, dh_ref, gs_ref):
        i = pl.program_id(0)
        cos_, sa, sb = cos_ref[...], sa_ref[...], sb_ref[...]
        latent = (i < nx).astype(F32)
        sl = lambda off, hh: h_ref[:, off + hh * HEAD_DIM: off + (hh + 1) * HEAD_DIM]

        def put(off, hh, val):
            dh_ref[:, off + hh * HEAD_DIM: off + (hh + 1) * HEAD_DIM] = val.astype(BF16)

        def norm_bwd(x, gain, dy):
            xn, r = _rms(x)
            dxh = dy * gain
            dx = r * (dxh - xn * jnp.mean(dxh * xn, axis=-1, keepdims=True))
            return dx, jnp.sum(dy * xn, axis=0, keepdims=True)

        for hh in range(HEADS_A):
            put(QA0, hh, _rope_t(dqa_ref[hh] * (ATTN_SCALE * latent), cos_, sa, sb))
        for hh in range(KV_A):
            put(KA0, hh, _rope_t(dka_ref[hh], cos_, sa, sb))
            put(VA0, hh, dva_ref[hh])
        gq = jnp.zeros((1, HEAD_DIM), F32)
        gk = jnp.zeros((1, HEAD_DIM), F32)
        for hh in range(HEADS_B):
            dq_t = dqb_ref[hh // GROUP, :, (hh % GROUP) * ROW_TILE:(hh % GROUP + 1) * ROW_TILE]
            dy = _rope_t(dq_t.T * (ATTN_SCALE * latent), cos_, sa, sb)
            dx, g = norm_bwd(sl(QB0, hh), qg_ref[...], dy)
            put(QB0, hh, dx)
            gq = gq + g
        for hh in range(KV_B):
            dy = _rope_t(dkb_ref[hh], cos_, sa, sb)
            dx, g = norm_bwd(sl(KB0, hh), kg_ref[...], dy)
            put(KB0, hh, dx)
            gk = gk + g
            put(VB0, hh, dvb_ref[hh])
        upd = jnp.concatenate([gq, gk, jnp.zeros((6, HEAD_DIM), F32)], axis=0)

        @pl.when(i == 0)
        def _():
            gs_ref[...] = upd

        @pl.when(i != 0)
        def _():
            gs_ref[...] += upd

    tab = pl.BlockSpec((ROW_TILE, HEAD_DIM), lambda i: (i, 0))
    gain = pl.BlockSpec((1, HEAD_DIM), lambda i: (0, 0))
    lat = lambda nh: pl.BlockSpec((nh, ROW_TILE, HEAD_DIM), lambda i: (0, jnp.minimum(i, nx - 1), 0))
    full = lambda nh: pl.BlockSpec((nh, ROW_TILE, HEAD_DIM), lambda i: (0, i, 0))
    return pl.pallas_call(
        body, name="qkv_bwd_post", grid=(nt,),
        in_specs=[pl.BlockSpec((ROW_TILE, IN_WIDTH), lambda i: (i, 0)), tab, tab, tab, gain, gain,
                  lat(HEADS_A), full(KV_A), full(KV_A),
                  pl.BlockSpec((KV_B, None, HEAD_DIM, GROUP * ROW_TILE), lambda i: (0, jnp.minimum(i, nx - 1), 0, 0)),
                  full(KV_B), full(KV_B)],
        out_specs=[pl.BlockSpec((ROW_TILE, IN_WIDTH), lambda i: (i, 0)),
                   pl.BlockSpec((8, HEAD_DIM), lambda i: (0, 0))],
        out_shape=[jax.ShapeDtypeStruct((t, IN_WIDTH), BF16), jax.ShapeDtypeStruct((8, HEAD_DIM), F32)],
        compiler_params=_params(("arbitrary",)),
    )(h_all, cos, sin_a, sin_b, q_g, k_g, dqa, dka, dva, dqb, dkb, dvb)


GB_TQ = 256
GB_TK = 256


def _heads_rows(ref2d, tq):
    return jnp.concatenate([ref2d[:, hh * HEAD_DIM:(hh + 1) * HEAD_DIM] for hh in range(GROUP)], axis=0)


def _attn_b_fwd(qb, kb, vb, n):
    t = kb.shape[1]
    nk = t // GB_TK
    tq = GB_TQ
    nq = n // tq
    qb_step = 2 if nq % 2 == 0 else 1
    rows = qb_step * GROUP * tq

    def body(q_ref, k_ref, v_ref, o_ref, lse_ref, m_s, l_s, acc_s):
        blocks = [(s, hh) for s in range(qb_step) for hh in range(GROUP)]
        q = jnp.concatenate([q_ref[hh, s * tq:(s + 1) * tq, :] for s, hh in blocks], axis=0)
        m_s[...] = jnp.full((1, rows), -jnp.inf, F32)
        l_s[...] = jnp.zeros((1, rows), F32)
        acc_s[...] = jnp.zeros((HEAD_DIM, rows), F32)

        def scores(j):
            start = pl.multiple_of(j * GB_TK, GB_TK)
            return lax.dot_general(k_ref[pl.ds(start, GB_TK), :], q, NT, preferred_element_type=F32)

        def step(j, st):
            st_next = scores(jnp.minimum(j + 1, nk - 1))
            vs = v_ref[pl.ds(pl.multiple_of(j * GB_TK, GB_TK), GB_TK), :]
            m_prev = m_s[...]
            m_new = jnp.maximum(m_prev, jnp.max(st, axis=0, keepdims=True))
            p = jnp.exp(st - m_new)
            alpha = jnp.exp(m_prev - m_new)
            l_s[...] = alpha * l_s[...] + jnp.sum(p, axis=0, keepdims=True)
            acc_s[...] = alpha * acc_s[...] + lax.dot_general(vs, p.astype(BF16), TN, preferred_element_type=F32)
            m_s[...] = m_new
            return st_next

        lax.fori_loop(0, nk, step, scores(0))
        ot = acc_s[...] * (1.0 / l_s[...])
        lse = m_s[...] + jnp.log(l_s[...])
        width = GROUP * tq
        for s in range(qb_step):
            lse_ref[s] = lse[:, s * width:(s + 1) * width]
        for k, (s, hh) in enumerate(blocks):
            o_ref[s * tq:(s + 1) * tq, hh * HEAD_DIM:(hh + 1) * HEAD_DIM] = ot[:, k * tq:(k + 1) * tq].T.astype(BF16)

    return pl.pallas_call(
        body, name="attn_b_fwd", grid=(KV_B, nq // qb_step),
        in_specs=[pl.BlockSpec((GROUP, qb_step * tq, HEAD_DIM), lambda g, i: (g, i, 0)),
                  pl.BlockSpec((None, t, HEAD_DIM), lambda g, i: (g, 0, 0)),
                  pl.BlockSpec((None, t, HEAD_DIM), lambda g, i: (g, 0, 0))],
        out_specs=[pl.BlockSpec((qb_step * tq, GROUP * HEAD_DIM), lambda g, i: (i, KV_A + g)),
                   pl.BlockSpec((None, qb_step, 1, GROUP * tq), lambda g, i: (g, i, 0, 0))],
        out_shape=[jax.ShapeDtypeStruct((n, MIX_WIDTH), BF16),
                   jax.ShapeDtypeStruct((KV_B, nq, 1, GROUP * tq), F32)],
        scratch_shapes=[pltpu.VMEM((1, rows), F32), pltpu.VMEM((1, rows), F32), pltpu.VMEM((HEAD_DIM, rows), F32)],
        compiler_params=_params(("parallel", "parallel")),
    )(qb, kb, vb)


def _attn_b_bwd(qb, kb, vb, dheads, lse, delta, n):
    t = kb.shape[1]
    nk = t // GB_TK
    tq = GB_TQ
    nq = n // tq
    rows = GROUP * tq

    qb_step = 4 if nq % 4 == 0 else 1

    def body(q_ref, k_ref, v_ref, do_ref, lse_ref, dl_ref, dq_ref, dk_ref, dv_ref):
        j = pl.program_id(1)
        i = pl.program_id(2)

        blocks = [(s, hh) for s in range(qb_step) for hh in range(GROUP)]
        q = jnp.concatenate([q_ref[hh, s * tq:(s + 1) * tq, :] for s, hh in blocks], axis=0)
        do = jnp.concatenate([do_ref[s * tq:(s + 1) * tq, hh * HEAD_DIM:(hh + 1) * HEAD_DIM] for s, hh in blocks], axis=0)
        lse_row = jnp.concatenate([lse_ref[s] for s in range(qb_step)], axis=1)
        dl_row = jnp.concatenate([dl_ref[s] for s in range(qb_step)], axis=1)
        ks, vs = k_ref[...], v_ref[...]
        st = lax.dot_general(ks, q, NT, preferred_element_type=F32)
        p = jnp.exp(st - lse_row)
        dpt = lax.dot_general(vs, do, NT, preferred_element_type=F32)
        ds = (p * (dpt - dl_row)).astype(BF16)
        dv_part = lax.dot_general(p.astype(BF16), do, NN, preferred_element_type=F32)
        dk_part = lax.dot_general(ds, q, NN, preferred_element_type=F32)
        dq_part = lax.dot_general(ks, ds, TN, preferred_element_type=F32)

        @pl.when(i == 0)
        def _():
            dk_ref[...] = dk_part
            dv_ref[...] = dv_part

        @pl.when(i != 0)
        def _():
            dk_ref[...] += dk_part
            dv_ref[...] += dv_part

        for s in range(qb_step):
            piece = dq_part[:, s * rows:(s + 1) * rows]

            @pl.when(j == 0)
            def _():
                dq_ref[qb_step * i + s] = piece

            @pl.when(j != 0)
            def _():
                dq_ref[qb_step * i + s] += piece

    kv = pl.BlockSpec((None, GB_TK, HEAD_DIM), lambda g, j, i: (g, j, 0))
    row = pl.BlockSpec((None, qb_step, 1, rows), lambda g, j, i: (g, i, 0, 0))
    return pl.pallas_call(
        body, name="attn_b_bwd", grid=(KV_B, nk, nq // qb_step),
        in_specs=[pl.BlockSpec((GROUP, qb_step * tq, HEAD_DIM), lambda g, j, i: (g, i, 0)), kv, kv,
                  pl.BlockSpec((qb_step * tq, GROUP * HEAD_DIM), lambda g, j, i: (i, KV_A + g)), row, row],
        out_specs=[pl.BlockSpec((None, nq, HEAD_DIM, rows), lambda g, j, i: (g, 0, 0, 0)), kv, kv],
        out_shape=[jax.ShapeDtypeStruct((KV_B, nq, HEAD_DIM, rows), F32),
                   jax.ShapeDtypeStruct((KV_B, t, HEAD_DIM), F32),
                   jax.ShapeDtypeStruct((KV_B, t, HEAD_DIM), F32)],
        compiler_params=_params(("parallel", "arbitrary", "arbitrary")),
    )(qb, kb, vb, dheads, lse, delta)


def _delta_rows(dheads, heads):
    n = heads.shape[0]
    tq = GB_TQ
    w = GROUP * HEAD_DIM

    def body(a_ref, b_ref, o_ref):
        prod = a_ref[...].astype(F32) * b_ref[...].astype(F32)
        cols = [jnp.sum(prod[:, hh * HEAD_DIM:(hh + 1) * HEAD_DIM].T, axis=0, keepdims=True) for hh in range(GROUP)]
        o_ref[...] = jnp.concatenate(cols, axis=1)

    blk = pl.BlockSpec((tq, w), lambda g, i: (i, KV_A + g))
    return pl.pallas_call(
        body, name="delta_rows", grid=(KV_B, n // tq),
        in_specs=[blk, blk],
        out_specs=pl.BlockSpec((None, None, 1, GROUP * tq), lambda g, i: (g, i, 0, 0)),
        out_shape=jax.ShapeDtypeStruct((KV_B, n // tq, 1, GROUP * tq), F32),
        compiler_params=_params(("parallel", "parallel")),
    )(dheads, heads)


KWIN = 3 * BLOCK


def _window_bias():
    row = jnp.arange(KWIN)[None, :, None]
    col = jnp.arange(GROUP * BLOCK)[None, None, :] % BLOCK
    shift = jnp.arange(3)[:, None, None] * BLOCK
    return jnp.where(jnp.abs(shift + col - row) <= WINDOW, 0.0, -jnp.inf).astype(F32)


def _window_variant(j, nb):
    return j - jnp.clip(j - 1, 0, nb - 3)


def _window_scores(q, k_ref, j, n, nb, sink_row, bias):
    c = k_ref.shape[0] - n
    start = pl.multiple_of(jnp.clip(j - 1, 0, nb - 3) * BLOCK, BLOCK)
    kw = k_ref[pl.ds(start, KWIN), :]
    kc = k_ref[pl.ds(n, c), :]
    s_loc = lax.dot_general(kw, q, NT, preferred_element_type=F32) + bias
    s_ctx = lax.dot_general(kc, q, NT, preferred_element_type=F32)
    m = jnp.maximum(jnp.maximum(jnp.max(s_loc, axis=0, keepdims=True), jnp.max(s_ctx, axis=0, keepdims=True)),
                    sink_row)
    e_loc, e_ctx, e_sink = jnp.exp(s_loc - m), jnp.exp(s_ctx - m), jnp.exp(sink_row - m)
    inv = 1.0 / (jnp.sum(e_loc, axis=0, keepdims=True) + jnp.sum(e_ctx, axis=0, keepdims=True) + e_sink)
    return e_loc * inv, e_ctx * inv, e_sink * inv, start


def _sink_row(sink_ref, g):
    return jnp.concatenate([sink_ref[pl.ds(g * GROUP + hh, 1), :] for hh in range(GROUP)], axis=1)


def _attn_a_fwd(qa, ka, va, sink_b, heads_b, n):
    t = ka.shape[1]
    nb = n // BLOCK
    assert nb >= 3

    def body(q_ref, k_ref, v_ref, sink_ref, heads_b_ref, bias_ref, o_ref):
        g, j = pl.program_id(0), pl.program_id(1)
        q = q_ref[...].reshape(GROUP * BLOCK, HEAD_DIM)
        p_loc, p_ctx, _, start = _window_scores(q, k_ref, j, n, nb, _sink_row(sink_ref, g), bias_ref[...])
        vw = v_ref[pl.ds(start, KWIN), :]
        vc = v_ref[pl.ds(n, t - n), :]
        ot = (lax.dot_general(vw, p_loc.astype(BF16), TN, preferred_element_type=F32)
              + lax.dot_general(vc, p_ctx.astype(BF16), TN, preferred_element_type=F32))
        for hh in range(GROUP):
            o_ref[:, hh * HEAD_DIM:(hh + 1) * HEAD_DIM] = ot[:, hh * BLOCK:(hh + 1) * BLOCK].T.astype(BF16)

    return pl.pallas_call(
        body, name="attn_a_fwd", grid=(KV_A, nb),
        in_specs=[pl.BlockSpec((GROUP, BLOCK, HEAD_DIM), lambda g, j: (g, j, 0)),
                  pl.BlockSpec((None, t, HEAD_DIM), lambda g, j: (g, 0, 0)),
                  pl.BlockSpec((None, t, HEAD_DIM), lambda g, j: (g, 0, 0)),
                  pl.BlockSpec((HEADS_A, HEAD_DIM), lambda g, j: (0, 0)),
                  pl.BlockSpec(memory_space=pl.ANY),
                  pl.BlockSpec((None, KWIN, GROUP * BLOCK), lambda g, j: (_window_variant(j, nb), 0, 0))],
        out_specs=pl.BlockSpec((BLOCK, GROUP * HEAD_DIM), lambda g, j: (j, g)),
        out_shape=jax.ShapeDtypeStruct((n, MIX_WIDTH), BF16),
        input_output_aliases={4: 0},
        compiler_params=_params(("parallel", "parallel")),
    )(qa, ka, va, sink_b, heads_b, _window_bias())


def _attn_a_bwd(qa, ka, va, sink_b, dheads, n):
    t = ka.shape[1]
    c = t - n
    nb = n // BLOCK

    def body(q_ref, k_ref, v_ref, sink_ref, do_ref, bias_ref, dq_ref, dk_ref, dv_ref, dsink_ref):
        g, j = pl.program_id(0), pl.program_id(1)

        @pl.when(j == 0)
        def _():
            dk_ref[...] = jnp.zeros_like(dk_ref)
            dv_ref[...] = jnp.zeros_like(dv_ref)
            dsink_ref[...] = jnp.zeros_like(dsink_ref)

        q = q_ref[...].reshape(GROUP * BLOCK, HEAD_DIM)
        do = _heads_rows(do_ref, BLOCK)
        p_loc, p_ctx, p_sink, start = _window_scores(q, k_ref, j, n, nb, _sink_row(sink_ref, g), bias_ref[...])
        kw, vw = k_ref[pl.ds(start, KWIN), :], v_ref[pl.ds(start, KWIN), :]
        kc, vc = k_ref[pl.ds(n, c), :], v_ref[pl.ds(n, c), :]
        dp_loc = lax.dot_general(vw, do, NT, preferred_element_type=F32)
        dp_ctx = lax.dot_general(vc, do, NT, preferred_element_type=F32)
        dl = jnp.sum(p_loc * dp_loc, axis=0, keepdims=True) + jnp.sum(p_ctx * dp_ctx, axis=0, keepdims=True)
        ds_loc = (p_loc * (dp_loc - dl)).astype(BF16)
        ds_ctx = (p_ctx * (dp_ctx - dl)).astype(BF16)
        dqt = (lax.dot_general(kw, ds_loc, TN, preferred_element_type=F32)
               + lax.dot_general(kc, ds_ctx, TN, preferred_element_type=F32))
        for hh in range(GROUP):
            dq_ref[hh] = dqt[:, hh * BLOCK:(hh + 1) * BLOCK].T
        dk_ref[pl.ds(start, KWIN), :] += lax.dot_general(ds_loc, q, NN, preferred_element_type=F32)
        dv_ref[pl.ds(start, KWIN), :] += lax.dot_general(p_loc.astype(BF16), do, NN, preferred_element_type=F32)
        dk_ref[pl.ds(n, c), :] += lax.dot_general(ds_ctx, q, NN, preferred_element_type=F32)
        dv_ref[pl.ds(n, c), :] += lax.dot_general(p_ctx.astype(BF16), do, NN, preferred_element_type=F32)
        dsk = -(p_sink * dl)
        upd = [jnp.broadcast_to(jnp.sum(dsk[:, hh * BLOCK:(hh + 1) * BLOCK], axis=1, keepdims=True), (1, HEAD_DIM))
               for hh in range(GROUP)]
        dsink_ref[...] += jnp.concatenate(upd + [jnp.zeros((8 - GROUP, HEAD_DIM), F32)], axis=0)

    res = pl.BlockSpec((None, t, HEAD_DIM), lambda g, j: (g, 0, 0))
    return pl.pallas_call(
        body, name="attn_a_bwd", grid=(KV_A, nb),
        in_specs=[pl.BlockSpec((GROUP, BLOCK, HEAD_DIM), lambda g, j: (g, j, 0)), res, res,
                  pl.BlockSpec((HEADS_A, HEAD_DIM), lambda g, j: (0, 0)),
                  pl.BlockSpec((BLOCK, GROUP * HEAD_DIM), lambda g, j: (j, g)),
                  pl.BlockSpec((None, KWIN, GROUP * BLOCK), lambda g, j: (_window_variant(j, nb), 0, 0))],
        out_specs=[pl.BlockSpec((GROUP, BLOCK, HEAD_DIM), lambda g, j: (g, j, 0)), res, res,
                   pl.BlockSpec((None, 8, HEAD_DIM), lambda g, j: (g, 0, 0))],
        out_shape=[jax.ShapeDtypeStruct((HEADS_A, n, HEAD_DIM), F32),
                   jax.ShapeDtypeStruct((KV_A, t, HEAD_DIM), F32),
                   jax.ShapeDtypeStruct((KV_A, t, HEAD_DIM), F32),
                   jax.ShapeDtypeStruct((KV_A, 8, HEAD_DIM), F32)],
        compiler_params=_params(("parallel", "arbitrary")),
    )(qa, ka, va, sink_b, dheads, _window_bias())


def _ln_stats(r):
    mu = jnp.mean(r, axis=-1, keepdims=True)
    cen = r - mu
    rstd = lax.rsqrt(jnp.mean(cen * cen, axis=-1, keepdims=True) + EPS)
    return cen * rstd, rstd


def _ln_bwd(dy, xhat, rstd, gain):
    dxh = dy * gain
    return rstd * (dxh - jnp.mean(dxh, axis=-1, keepdims=True)
                   - xhat * jnp.mean(dxh * xhat, axis=-1, keepdims=True))


def _accumulate_rows(ref, rows, i):
    pad = [jnp.zeros_like(rows[0])] * (8 - len(rows))
    upd = jnp.concatenate(rows + pad, axis=0)

    @pl.when(i == 0)
    def _():
        ref[...] = upd

    @pl.when(i != 0)
    def _():
        ref[...] += upd


def _colsum(v):
    return jnp.sum(v, axis=0, keepdims=True)


LN_TILE = 256


def _res_ln1(x, a, vec):
    n, d = x.shape

    def body(x_ref, a_ref, v_ref, xh_ref, rs_ref, u_ref):
        r1 = DN_ALPHA * x_ref[...] + v_ref[0:1, :] * a_ref[...]
        xhat, rstd = _ln_stats(r1)
        xh_ref[...] = xhat
        rs_ref[...] = rstd
        x1 = xhat * v_ref[1:2, :] + v_ref[2:3, :]
        u_ref[...] = (x1 * (1.0 + v_ref[3:4, :]) + v_ref[4:5, :]).astype(BF16)

    row = pl.BlockSpec((LN_TILE, d), lambda i: (i, 0))
    return pl.pallas_call(
        body, name="res_ln1", grid=(n // LN_TILE,),
        in_specs=[row, row, pl.BlockSpec((8, d), lambda i: (0, 0))],
        out_specs=[row, pl.BlockSpec((LN_TILE, 1), lambda i: (i, 0)), row],
        out_shape=[jax.ShapeDtypeStruct((n, d), F32), jax.ShapeDtypeStruct((n, 1), F32),
                   jax.ShapeDtypeStruct((n, d), BF16)],
        compiler_params=_params(("parallel",)),
    )(x, a, vec)


def _res_ln2_loss(xhat1, f, target, vec):
    n, d = f.shape

    def body(xh_ref, f_ref, t_ref, v_ref, dr_ref, df_ref, s_ref):
        i = pl.program_id(0)
        x1 = xh_ref[...] * v_ref[1:2, :] + v_ref[2:3, :]
        fv = f_ref[...]
        xhat, rstd = _ln_stats(DN_ALPHA * x1 + v_ref[0:1, :] * fv)
        err = xhat * v_ref[3:4, :] + v_ref[4:5, :] - t_ref[...]
        dy = err * (1.0 / d)
        dr2 = _ln_bwd(dy, xhat, rstd, v_ref[3:4, :])
        dr_ref[...] = dr2
        df_ref[...] = (dr2 * v_ref[0:1, :]).astype(BF16)
        _accumulate_rows(s_ref, [_colsum(dy * xhat), _colsum(dy), _colsum(dr2 * fv),
                                 _colsum(err * err) * (0.5 / d)], i)

    row = pl.BlockSpec((LN_TILE, d), lambda i: (i, 0))
    return pl.pallas_call(
        body, name="res_ln2_loss", grid=(n // LN_TILE,),
        in_specs=[row, row, row, pl.BlockSpec((8, d), lambda i: (0, 0))],
        out_specs=[row, row, pl.BlockSpec((8, d), lambda i: (0, 0))],
        out_shape=[jax.ShapeDtypeStruct((n, d), F32), jax.ShapeDtypeStruct((n, d), BF16),
                   jax.ShapeDtypeStruct((8, d), F32)],
        compiler_params=_params(("arbitrary",)),
    )(xhat1, f, target, vec)


def _ln1_bwd(du2, dr2, xhat1, rstd1, a, vec):
    n, d = du2.shape

    def body(du_ref, dr2_ref, xh_ref, rs_ref, a_ref, v_ref, dxp_ref, da_ref, s_ref):
        i = pl.program_id(0)
        du, xhat = du_ref[...], xh_ref[...]
        x1 = xhat * v_ref[1:2, :] + v_ref[2:3, :]
        dx1 = DN_ALPHA * dr2_ref[...] + du * (1.0 + v_ref[0:1, :])
        dr1 = _ln_bwd(dx1, xhat, rs_ref[...], v_ref[1:2, :])
        dxp_ref[...] = DN_ALPHA * dr1
        da_ref[...] = (dr1 * v_ref[3:4, :]).astype(BF16)
        _accumulate_rows(s_ref, [_colsum(du * x1), _colsum(du), _colsum(dx1 * xhat), _colsum(dx1),
                                 _colsum(dr1 * a_ref[...])], i)

    row = pl.BlockSpec((LN_TILE, d), lambda i: (i, 0))
    return pl.pallas_call(
        body, name="ln1_bwd", grid=(n // LN_TILE,),
        in_specs=[row, row, row, pl.BlockSpec((LN_TILE, 1), lambda i: (i, 0)), row,
                  pl.BlockSpec((8, d), lambda i: (0, 0))],
        out_specs=[row, row, pl.BlockSpec((8, d), lambda i: (0, 0))],
        out_shape=[jax.ShapeDtypeStruct((n, d), F32), jax.ShapeDtypeStruct((n, d), BF16),
                   jax.ShapeDtypeStruct((8, d), F32)],
        compiler_params=_params(("arbitrary",)),
    )(du2, dr2, xhat1, rstd1, a, vec)


def _mod1_bwd(du_all, dxp, x, ctx, mods):
    n, d = x.shape
    nx = n // ROW_TILE

    def body(du_ref, dxp_ref, x_ref, ctx_ref, m_ref, gx_ref, s_ref):
        i = pl.program_id(0)
        du = du_ref[...]
        zero = jnp.zeros((1, d), F32)

        @pl.when(i == 0)
        def _():
            s_ref[...] = jnp.zeros_like(s_ref)

        @pl.when(i < nx)
        def _():
            gx_ref[...] = dxp_ref[...] + du * (1.0 + m_ref[0:1, :])
            s_ref[...] += jnp.concatenate([_colsum(du * x_ref[...]), _colsum(du)] + [zero] * 6, axis=0)

        @pl.when(i >= nx)
        def _():
            s_ref[...] += jnp.concatenate([zero, zero, _colsum(du * ctx_ref[...]), _colsum(du)] + [zero] * 4, axis=0)

    lat = pl.BlockSpec((ROW_TILE, d), lambda i: (jnp.minimum(i, nx - 1), 0))
    return pl.pallas_call(
        body, name="mod1_bwd", grid=(nx + 1,),
        in_specs=[pl.BlockSpec((ROW_TILE, d), lambda i: (i, 0)), lat, lat,
                  pl.BlockSpec((ROW_TILE, d), lambda i: (0, 0)), pl.BlockSpec((8, d), lambda i: (0, 0))],
        out_specs=[lat, pl.BlockSpec((8, d), lambda i: (0, 0))],
        out_shape=[jax.ShapeDtypeStruct((n, d), F32), jax.ShapeDtypeStruct((8, d), F32)],
        compiler_params=_params(("arbitrary",)),
    )(du_all, dxp, x, ctx, mods)


FFN_TM = 1024
FFN_TN = 512


def _gate_up(u2, wg, wu, after):
    n, d = u2.shape
    f = wg.shape[1]

    def body(u_ref, wg_ref, wu_ref, after_ref, g_ref, up_ref, h_ref):
        u = u_ref[...]
        g = lax.dot_general(u, wg_ref[...], NN, preferred_element_type=F32)
        up = lax.dot_general(u, wu_ref[...], NN, preferred_element_type=F32)
        g_ref[...] = g.astype(BF16)
        up_ref[...] = up.astype(BF16)
        h_ref[...] = (g * jax.nn.sigmoid(g) * up).astype(BF16)

    tm = min(FFN_TM, n)
    wspec = pl.BlockSpec((d, FFN_TN), lambda j, i: (0, j))
    ospec = pl.BlockSpec((tm, FFN_TN), lambda j, i: (i, j))
    return pl.pallas_call(
        body, name="gate_up", grid=(f // FFN_TN, n // tm),
        in_specs=[pl.BlockSpec((tm, d), lambda j, i: (i, 0)), wspec, wspec, pl.BlockSpec(memory_space=pl.ANY)],
        out_specs=[ospec, ospec, ospec],
        out_shape=[jax.ShapeDtypeStruct((n, f), BF16)] * 3,
        compiler_params=_params(("parallel", "parallel")),
    )(u2, wg, wu, after)


def _glu_bwd(df, wd, g, u):
    n, d = df.shape
    f = wd.shape[0]

    def body(df_ref, wd_ref, g_ref, u_ref, dg_ref, du_ref):
        dh = lax.dot_general(df_ref[...], wd_ref[...], NT, preferred_element_type=F32)
        gv = g_ref[...].astype(F32)
        sig = jax.nn.sigmoid(gv)
        du_ref[...] = (dh * (gv * sig)).astype(BF16)
        dg_ref[...] = (dh * u_ref[...].astype(F32) * (sig * (1.0 + gv * (1.0 - sig)))).astype(BF16)

    tm = min(FFN_TM, n)
    ospec = pl.BlockSpec((tm, FFN_TN), lambda i, j: (i, j))
    return pl.pallas_call(
        body, name="glu_bwd", grid=(n // tm, f // FFN_TN),
        in_specs=[pl.BlockSpec((tm, d), lambda i, j: (i, 0)),
                  pl.BlockSpec((FFN_TN, d), lambda i, j: (j, 0)), ospec, ospec],
        out_specs=[ospec, ospec],
        out_shape=[jax.ShapeDtypeStruct((n, f), BF16), jax.ShapeDtypeStruct((n, f), BF16)],
        compiler_params=_params(("parallel", "parallel")),
    )(df, wd, g, u)


def _du2(dg, du, wg, wu):
    n, f = dg.shape
    d = wg.shape[0]
    tm, tn, tk = min(1024, n), 1024, 1408
    nk = f // tk

    def body(dg_ref, du_ref, wg_ref, wu_ref, o_ref, acc_ref):
        kk = pl.program_id(2)
        part = (lax.dot_general(dg_ref[...], wg_ref[...], NT, preferred_element_type=F32)
                + lax.dot_general(du_ref[...], wu_ref[...], NT, preferred_element_type=F32))

        @pl.when(kk == 0)
        def _():
            acc_ref[...] = part

        @pl.when(kk != 0)
        def _():
            acc_ref[...] += part

        @pl.when(kk == nk - 1)
        def _():
            o_ref[...] = acc_ref[...]

    aspec = pl.BlockSpec((tm, tk), lambda i, j, kk: (i, kk))
    wspec = pl.BlockSpec((tn, tk), lambda i, j, kk: (j, kk))
    return pl.pallas_call(
        body, name="du2", grid=(n // tm, d // tn, nk),
        in_specs=[aspec, aspec, wspec, wspec],
        out_specs=pl.BlockSpec((tm, tn), lambda i, j, kk: (i, j)),
        out_shape=jax.ShapeDtypeStruct((n, d), F32),
        scratch_shapes=[pltpu.VMEM((tm, tn), F32)],
        compiler_params=_params(("parallel", "parallel", "arbitrary")),
    )(dg, du, wg, wu)


def _rows8(rows, d=D_MODEL):
    rows = [r.reshape(1, d).astype(F32) for r in rows]
    return jnp.concatenate(rows + [jnp.zeros((8 - len(rows), d), F32)], axis=0)


W_GROUPS = (("w_in",), ("w_out", "w_gate", "w_up"), ("w_down",))
G_GROUPS = (("w_down", "w_gate", "w_up"), ("w_out",), ("w_in",))


def _layer_fwd_bwd(x, ctx, target, mod, mod_ctx, weights, prefetch, grads_out,
                   q_g, k_g, sink, ln1_g, ln1_b, ln2_g, ln2_b):
    n, d = x.shape
    c = ctx.shape[0]
    sh1, sc1, g1, sh2, sc2, g2 = [mod[:, k * d:(k + 1) * d] for k in range(6)]
    csh1, csc1 = mod_ctx[:, 0:d], mod_ctx[:, d:2 * d]
    cos, sin_a, sin_b = _rope_tables(n, c)
    sink_b = jnp.broadcast_to(sink.reshape(HEADS_A, 1), (HEADS_A, HEAD_DIM)).astype(F32)

    u_all = _modulate_rows(x, ctx, _rows8([sc1, sh1, csc1, csh1]))
    (w_in,) = weights(0, u_all)
    h_all = _matmul(u_all, w_in, name="qkv_proj", tm=_fit(n + c, 1088), tn=1024, tk=2048, out_dtype=F32)
    qa, ka, va, qb, kb, vb = _qkv_post(h_all, cos, sin_a, sin_b, q_g, k_g)
    heads_b, lse = _attn_b_fwd(qb, kb, vb, n)
    zero = prefetch(1, heads_b)
    heads = _attn_a_fwd(qa, ka, va, sink_b + zero, heads_b, n)
    w_out, w_gate, w_up = weights(1, heads)
    a = _matmul(heads, w_out, name="out_proj", tm=1024, tn=1024, tk=2048, out_dtype=F32)
    xhat1, rstd1, u2 = _res_ln1(x, a, _rows8([g1, ln1_g, ln1_b, sc2, sh2]))
    zero = prefetch(2, u2)
    gg, uu, hh = _gate_up(u2, w_gate, w_up, zero.reshape(1, 1))
    (w_down,) = weights(2, hh)
    f = _matmul(hh, w_down, name="ffn_down", tm=1024, tn=512, tk=FFN, out_dtype=F32)
    dr2, df, s_ln2 = _res_ln2_loss(xhat1, f, target, _rows8([g2, ln1_g, ln1_b, ln2_g, ln2_b]))

    dgg, duu = _glu_bwd(df, w_down, gg, uu)
    dw_down = _matmul(hh, df, name="dw_down", ta=True, tm=512, tn=1024, tk=n, out_dtype=BF16)
    dw_gate = _matmul(u2, dgg, name="dw_gate", ta=True, tm=1024, tn=512, tk=n, out_dtype=BF16)
    dw_up = _matmul(u2, duu, name="dw_up", ta=True, tm=1024, tn=512, tk=n, out_dtype=BF16)
    zero = grads_out(0, [dw_down, dw_gate, dw_up])
    du2 = _du2(dgg, duu, w_gate, w_up)
    dxp, da, s_ln1 = _ln1_bwd(du2, dr2, xhat1, rstd1, a, _rows8([sc2, ln1_g, ln1_b, g1]) + zero)

    dheads = _matmul(da, w_out, name="d_heads", tb=True, tm=1024, tn=1024, tk=2048, out_dtype=BF16)
    dw_out = _matmul(heads, da, name="dw_out", ta=True, tm=1024, tn=1024, tk=n, out_dtype=BF16)
    zero = grads_out(1, [dw_out])
    delta = _delta_rows(dheads, heads)
    dqa, dka, dva, dsink = _attn_a_bwd(qa, ka, va, sink_b + zero, dheads, n)
    dqb, dkb, dvb = _attn_b_bwd(qb, kb, vb, dheads, lse, delta, n)
    dh_all, s_gain = _qkv_bwd_post(h_all, cos, sin_a, sin_b, q_g, k_g, dqa, dka, dva, dqb, dkb, dvb, n)
    dw_in = _matmul(u_all, dh_all, name="dw_in", ta=True, tm=1024, tn=1024, tk=n + c, out_dtype=BF16)
    zero = grads_out(2, [dw_in])
    du_all = _matmul(dh_all, w_in, name="d_u1", tb=True, tm=_fit(n + c, 1088), tn=1024, tk=IN_WIDTH, out_dtype=F32,
                     after=zero.reshape(1, 1))
    grad_x, s_mod1 = _mod1_bwd(du_all, dxp, x, ctx, _rows8([sc1]) + zero)

    dsink_row = jnp.concatenate([dsink[0, 0:GROUP, 0], dsink[1, 0:GROUP, 0]]).reshape(1, HEADS_A)
    misc = jnp.concatenate([s_gain[0:1], s_gain[1:2], dsink_row,
                            jnp.zeros((1, d - 2 * HEAD_DIM - HEADS_A), F32)], axis=1)
    partial = jnp.concatenate([
        s_mod1[1:2], s_mod1[0:1], s_ln1[4:5],
        s_ln1[1:2], s_ln1[0:1], s_ln2[2:3],
        s_mod1[3:4], s_mod1[2:3],
        s_ln1[2:3], s_ln1[3:4], s_ln2[0:1], s_ln2[1:2],
        s_ln2[3:4], misc, jnp.zeros((2, d), F32)], axis=0)
    return grad_x, partial


ANY = pl.BlockSpec(memory_space=pl.ANY)
VMEM_FULL = pl.BlockSpec(memory_space=pltpu.VMEM)
N_CHIP_PEERS = 3


def _me():
    return lax.axis_index("x"), lax.axis_index("y"), lax.axis_index("c")


def _other_chips(x, y):
    return [(1 - x, y), (x, 1 - y), (1 - x, 1 - y)]


def _shard_of(chip):
    return 2 * chip[0] + chip[1]


def _dev_index(x, y, c):
    return 4 * x + 2 * y + c


def _rcopy(src, dst, send_sems, recv_sems, k, dev):
    return pltpu.make_async_remote_copy(src_ref=src, dst_ref=dst, send_sem=send_sems.at[k], recv_sem=recv_sems.at[k],
                                        device_id=dev, device_id_type=MESH)


BIG = (("w_in", (D_MODEL, IN_WIDTH), 1), ("w_out", (MIX_WIDTH, D_MODEL), 0), ("w_gate", (D_MODEL, FFN), 1),
       ("w_up", (D_MODEL, FFN), 1), ("w_down", (FFN, D_MODEL), 0))


def _sub(ref, axis, idx, size):
    start = pl.multiple_of(idx * size, size)
    return ref.at[pl.ds(start, size), :] if axis == 0 else ref.at[:, pl.ds(start, size)]


def _shape_div(shape, axis, parts):
    return tuple(s // parts if a == axis else s for a, s in enumerate(shape))


def _piece(a, ref, shard, half):
    _, full, axis = BIG[a]
    view = _sub(ref, axis, shard, full[axis] // N_SHARD)
    return _sub(view, 1 - axis, half, full[1 - axis] // 2)


HBM = pl.BlockSpec(memory_space=pltpu.HBM)
SEM = pl.BlockSpec(memory_space=pltpu.SEMAPHORE)
EFFECT = pltpu.SideEffectType.DATAFLOW_SIDE_EFFECTING
BIG_INDEX = {name: a for a, (name, _, _) in enumerate(BIG)}


def _in_hbm(arr):
    return pltpu.with_memory_space_constraint(arr, pltpu.HBM)


def _gather_start(tag, arrs, bufs, prev):
    n_arr = len(arrs)

    def body(*refs):
        ins = refs[:n_arr]
        send_sems, recv_sems = refs[n_arr + 1], refs[n_arr + 2]
        token = refs[-1]
        x, y, c = _me()
        s_me = _shard_of((x, y))
        for i, a in enumerate(arrs):
            mine = _piece(a, ins[i], s_me, c)
            for j, chip in enumerate(_other_chips(x, y)):
                _rcopy(mine, mine, send_sems, recv_sems, N_CHIP_PEERS * i + j, (*chip, c)).start()
        token[...] = jnp.zeros_like(token)

    n_sem = N_CHIP_PEERS * n_arr
    outs = pl.pallas_call(
        body, name="gather_start_" + tag,
        in_specs=[HBM] * n_arr + [ANY],
        out_specs=[SEM, SEM] + [HBM] * n_arr + [VMEM_FULL],
        out_shape=[pltpu.SemaphoreType.DMA((n_sem,)), pltpu.SemaphoreType.DMA((n_sem,))]
        + [pltpu.HBM(BIG[a][1], BF16) for a in arrs] + [jax.ShapeDtypeStruct((8, HEAD_DIM), F32)],
        input_output_aliases={i: 2 + i for i in range(n_arr)},
        compiler_params=pltpu.CompilerParams(has_side_effects=EFFECT),
    )(*[_in_hbm(b) for b in bufs], prev)
    return outs[0], outs[1], list(outs[2:2 + n_arr]), outs[-1]


def _gather_wait(tag, arrs, send_sems, recv_sems, bufs, after):
    n_arr = len(arrs)

    def body(*refs):
        ins = refs[:n_arr]
        send_sems_, recv_sems_ = refs[n_arr], refs[n_arr + 1]
        x, y, c = _me()
        s_me = _shard_of((x, y))
        for i, a in enumerate(arrs):
            mine = _piece(a, ins[i], s_me, c)
            for j, chip in enumerate(_other_chips(x, y)):
                landed = _piece(a, ins[i], _shard_of(chip), c)
                cp = _rcopy(mine, landed, send_sems_, recv_sems_, N_CHIP_PEERS * i + j, (*chip, c))
                cp.wait_send()
                cp.wait_recv()

    outs = pl.pallas_call(
        body, name="gather_wait_" + tag,
        in_specs=[HBM] * n_arr + [SEM, SEM, ANY],
        out_specs=[HBM] * n_arr,
        out_shape=[pltpu.HBM(BIG[a][1], BF16) for a in arrs],
        input_output_aliases={i: i for i in range(n_arr)},
        compiler_params=pltpu.CompilerParams(has_side_effects=EFFECT),
    )(*bufs, send_sems, recv_sems, after)
    return list(outs)


def _gather_forward(tag, arrs, bufs):
    n_arr = len(arrs)

    def body(*refs):
        outs = refs[n_arr:2 * n_arr]
        send_sems, recv_sems = refs[2 * n_arr:]
        x, y, c = _me()
        sibling = (x, y, 1 - c)
        chips = _other_chips(x, y)
        copies = []
        for i, a in enumerate(arrs):
            for j, chip in enumerate(chips):
                landed = _piece(a, outs[i], _shard_of(chip), c)
                cp = _rcopy(landed, landed, send_sems, recv_sems, N_CHIP_PEERS * i + j, sibling)
                cp.start()
                copies.append(cp)
        for i, a in enumerate(arrs):
            for j, chip in enumerate(chips):
                other = _piece(a, outs[i], _shard_of(chip), 1 - c)
                _rcopy(other, other, send_sems, recv_sems, N_CHIP_PEERS * i + j, sibling).wait_recv()
        for cp in copies:
            cp.wait_send()

    n_sem = N_CHIP_PEERS * n_arr
    return list(pl.pallas_call(
        body, name="gather_forward_" + tag,
        in_specs=[ANY] * n_arr, out_specs=[ANY] * n_arr,
        out_shape=[jax.ShapeDtypeStruct(BIG[a][1], BF16) for a in arrs],
        input_output_aliases={i: i for i in range(n_arr)},
        scratch_shapes=[pltpu.SemaphoreType.DMA((n_sem,)), pltpu.SemaphoreType.DMA((n_sem,))],
    )(*bufs))


def _forward_start(tag, arrs, bufs):
    n_arr = len(arrs)

    def body(*refs):
        ins = refs[:n_arr]
        send_sems, recv_sems = refs[n_arr], refs[n_arr + 1]
        token = refs[-1]
        x, y, c = _me()
        for i, a in enumerate(arrs):
            for j, chip in enumerate(_other_chips(x, y)):
                landed = _piece(a, ins[i], _shard_of(chip), c)
                _rcopy(landed, landed, send_sems, recv_sems, N_CHIP_PEERS * i + j, (x, y, 1 - c)).start()
        token[...] = jnp.zeros_like(token)

    n_sem = N_CHIP_PEERS * n_arr
    outs = pl.pallas_call(
        body, name="gather_forward_start_" + tag,
        in_specs=[HBM] * n_arr,
        out_specs=[SEM, SEM] + [HBM] * n_arr + [VMEM_FULL],
        out_shape=[pltpu.SemaphoreType.DMA((n_sem,)), pltpu.SemaphoreType.DMA((n_sem,))]
        + [pltpu.HBM(BIG[a][1], BF16) for a in arrs] + [jax.ShapeDtypeStruct((8, HEAD_DIM), F32)],
        input_output_aliases={i: 2 + i for i in range(n_arr)},
        compiler_params=pltpu.CompilerParams(has_side_effects=EFFECT),
    )(*bufs)
    return outs[0], outs[1], list(outs[2:2 + n_arr]), outs[-1]


def _forward_wait(tag, arrs, send_sems, recv_sems, bufs, after):
    n_arr = len(arrs)

    def body(*refs):
        ins = refs[:n_arr]
        send_sems_, recv_sems_ = refs[n_arr], refs[n_arr + 1]
        x, y, c = _me()
        for i, a in enumerate(arrs):
            for j, chip in enumerate(_other_chips(x, y)):
                mine = _piece(a, ins[i], _shard_of(chip), c)
                other = _piece(a, ins[i], _shard_of(chip), 1 - c)
                cp = _rcopy(mine, other, send_sems_, recv_sems_, N_CHIP_PEERS * i + j, (x, y, 1 - c))
                cp.wait_send()
                cp.wait_recv()

    outs = pl.pallas_call(
        body, name="gather_forward_wait_" + tag,
        in_specs=[HBM] * n_arr + [SEM, SEM, ANY],
        out_specs=[HBM] * n_arr,
        out_shape=[pltpu.HBM(BIG[a][1], BF16) for a in arrs],
        input_output_aliases={i: i for i in range(n_arr)},
        compiler_params=pltpu.CompilerParams(has_side_effects=EFFECT),
    )(*bufs, send_sems, recv_sems, after)
    return list(outs)


def _peers(x, y, c):
    return [(x ^ (mask >> 2), y ^ ((mask >> 1) & 1), c ^ (mask & 1)) for mask in range(1, N_DEV)]


def _received_shape(a):
    _, full, axis = BIG[a]
    return (N_DEV - 1,) + _shape_div(_shape_div(full, 1 - axis, 2), axis, N_SHARD)


def _pieces_start(tag, arrs, dws):
    n_arr = len(arrs)

    def body(*refs):
        srcs, lands = refs[:n_arr], refs[n_arr:2 * n_arr]
        send_sems, recv_sems = refs[2 * n_arr], refs[2 * n_arr + 1]
        token = refs[-1]
        x, y, c = _me()
        for i, a in enumerate(arrs):
            for k, peer in enumerate(_peers(x, y, c)):
                src = _piece(a, srcs[i], _shard_of(peer[:2]), peer[2])
                _rcopy(src, lands[i].at[k], send_sems, recv_sems, (N_DEV - 1) * i + k, peer).start()
        token[...] = jnp.zeros_like(token)

    n_sem = (N_DEV - 1) * n_arr
    lands = [_in_hbm(lax.empty(_received_shape(a), BF16)) for a in arrs]
    outs = pl.pallas_call(
        body, name="grad_pieces_start_" + tag,
        in_specs=[HBM] * (2 * n_arr),
        out_specs=[SEM, SEM] + [HBM] * (2 * n_arr) + [VMEM_FULL],
        out_shape=[pltpu.SemaphoreType.DMA((n_sem,)), pltpu.SemaphoreType.DMA((n_sem,))]
        + [pltpu.HBM(BIG[a][1], BF16) for a in arrs] + [pltpu.HBM(_received_shape(a), BF16) for a in arrs]
        + [jax.ShapeDtypeStruct((8, HEAD_DIM), F32)],
        input_output_aliases={i: 2 + i for i in range(2 * n_arr)},
        compiler_params=pltpu.CompilerParams(has_side_effects=EFFECT),
    )(*[_in_hbm(dw) for dw in dws], *lands)
    return outs[0], outs[1], list(outs[2:2 + n_arr]), list(outs[2 + n_arr:2 + 2 * n_arr]), outs[-1]


def _pieces_wait(tag, arrs, send_sems, recv_sems, dws, lands, after):
    n_arr = len(arrs)

    def body(*refs):
        srcs, lands_ = refs[:n_arr], refs[n_arr:2 * n_arr]
        send_sems_, recv_sems_ = refs[2 * n_arr], refs[2 * n_arr + 1]
        x, y, c = _me()
        for i, a in enumerate(arrs):
            for k, peer in enumerate(_peers(x, y, c)):
                src = _piece(a, srcs[i], _shard_of(peer[:2]), peer[2])
                cp = _rcopy(src, lands_[i].at[k], send_sems_, recv_sems_, (N_DEV - 1) * i + k, peer)
                cp.wait_send()
                cp.wait_recv()

    outs = pl.pallas_call(
        body, name="grad_pieces_wait_" + tag,
        in_specs=[HBM] * (2 * n_arr) + [SEM, SEM, ANY],
        out_specs=[HBM] * (2 * n_arr),
        out_shape=[pltpu.HBM(BIG[a][1], BF16) for a in arrs] + [pltpu.HBM(_received_shape(a), BF16) for a in arrs],
        input_output_aliases={i: i for i in range(2 * n_arr)},
        compiler_params=pltpu.CompilerParams(has_side_effects=EFFECT),
    )(*dws, *lands, send_sems, recv_sems, after)
    return list(outs[:n_arr]), list(outs[n_arr:])


def _join_start(tag, g_halves):
    n_arr = len(g_halves)

    def body(*refs):
        srcs, lands = refs[:n_arr], refs[n_arr:2 * n_arr]
        send_sems, recv_sems = refs[2 * n_arr], refs[2 * n_arr + 1]
        token = refs[-1]
        x, y, c = _me()
        for i in range(n_arr):
            _rcopy(srcs[i], lands[i], send_sems, recv_sems, i, (x, y, 1 - c)).start()
        token[...] = jnp.zeros_like(token)

    shapes = [pltpu.HBM(g.shape, F32) for g in g_halves]
    outs = pl.pallas_call(
        body, name="grad_join_start_" + tag,
        in_specs=[HBM] * (2 * n_arr),
        out_specs=[SEM, SEM] + [HBM] * (2 * n_arr) + [VMEM_FULL],
        out_shape=[pltpu.SemaphoreType.DMA((n_arr,)), pltpu.SemaphoreType.DMA((n_arr,))] + shapes + shapes
        + [jax.ShapeDtypeStruct((8, HEAD_DIM), F32)],
        input_output_aliases={i: 2 + i for i in range(2 * n_arr)},
        compiler_params=pltpu.CompilerParams(has_side_effects=EFFECT),
    )(*[_in_hbm(g) for g in g_halves], *[_in_hbm(lax.empty(g.shape, F32)) for g in g_halves])
    return outs[0], outs[1], list(outs[2:2 + n_arr]), list(outs[2 + n_arr:2 + 2 * n_arr]), outs[-1]


def _join_wait(tag, send_sems, recv_sems, g_halves, lands, after):
    n_arr = len(g_halves)

    def body(*refs):
        srcs, lands_ = refs[:n_arr], refs[n_arr:2 * n_arr]
        send_sems_, recv_sems_ = refs[2 * n_arr], refs[2 * n_arr + 1]
        x, y, c = _me()
        for i in range(n_arr):
            cp = _rcopy(srcs[i], lands_[i], send_sems_, recv_sems_, i, (x, y, 1 - c))
            cp.wait_send()
            cp.wait_recv()

    shapes = [pltpu.HBM(g.shape, F32) for g in g_halves]
    outs = pl.pallas_call(
        body, name="grad_join_wait_" + tag,
        in_specs=[HBM] * (2 * n_arr) + [SEM, SEM, ANY],
        out_specs=[HBM] * (2 * n_arr),
        out_shape=shapes + shapes,
        input_output_aliases={i: i for i in range(2 * n_arr)},
        compiler_params=pltpu.CompilerParams(has_side_effects=EFFECT),
    )(*g_halves, *lands, send_sems, recv_sems, after)
    return list(outs[:n_arr]), list(outs[n_arr:])


def _piece_sum(a, dw, shard, core, received):
    name, full, axis = BIG[a]
    rows, cols = _received_shape(a)[1:]
    tr = _fit(rows, ROW_TILE)
    nbr = rows // tr

    def body(w_ref, dw_ref, rec_ref, o_ref):
        acc = dw_ref[...].astype(F32)
        for k in range(N_DEV - 1):
            acc = acc + rec_ref[k].astype(F32)
        o_ref[...] = acc

    if axis == 0:
        own = pl.BlockSpec((tr, cols), lambda i, w: (w[0] * nbr + i, w[1]))
    else:
        own = pl.BlockSpec((tr, cols), lambda i, w: (w[1] * nbr + i, w[0]))
    return pl.pallas_call(
        body, name="grad_sum_pieces_" + name,
        grid_spec=pltpu.PrefetchScalarGridSpec(
            num_scalar_prefetch=1, grid=(nbr,),
            in_specs=[own, pl.BlockSpec((N_DEV - 1, tr, cols), lambda i, w: (0, i, 0))],
            out_specs=pl.BlockSpec((tr, cols), lambda i, w: (i, 0))),
        out_shape=jax.ShapeDtypeStruct((rows, cols), F32),
        compiler_params=_params(("parallel",)),
    )(jnp.stack([shard, core]).astype(jnp.int32), dw, received)


def _scatter_begin(tag, arrs, dws):
    send_sems, recv_sems, dws, lands, token = _pieces_start(tag, arrs, dws)
    return (send_sems, recv_sems, dws, lands), token[0, 0]


def _scatter_reduce(tag, arrs, state, after):
    x, y, c = _me()
    send_sems, recv_sems, dws, lands = state
    dws, lands = _pieces_wait(tag, arrs, send_sems, recv_sems, dws, lands, after)
    g_own = [_piece_sum(a, dw, _shard_of((x, y)), c, r) for a, dw, r in zip(arrs, dws, lands)]
    send_sems, recv_sems, g_own, lands, token = _join_start(tag, g_own)
    return (send_sems, recv_sems, g_own, lands), token


def _scatter_end(tag, state, after):
    return _join_wait(tag, *state, after)


def _gather_rows(block, name, after):
    r, d = block.shape

    def body(in_ref, after_ref, out_ref, send_sems, recv_sems):
        x, y, c = _me()
        out_ref[_dev_index(x, y, c)] = in_ref[...]
        copies = []
        for mask in range(1, N_DEV):
            peer = (x ^ (mask >> 2), y ^ ((mask >> 1) & 1), c ^ (mask & 1))
            cp = _rcopy(in_ref, out_ref.at[_dev_index(x, y, c)], send_sems, recv_sems, mask - 1, peer)
            cp.start()
            copies.append((cp, peer))
        for mask in range(1, N_DEV):
            peer = (x ^ (mask >> 2), y ^ ((mask >> 1) & 1), c ^ (mask & 1))
            landed = out_ref.at[_dev_index(*peer)]
            _rcopy(landed, landed, send_sems, recv_sems, mask - 1, peer).wait_recv()
        for cp, _ in copies:
            cp.wait_send()

    return pl.pallas_call(
        body, name=name, in_specs=[VMEM_FULL, ANY], out_specs=VMEM_FULL,
        out_shape=jax.ShapeDtypeStruct((N_DEV, r, d), F32),
        scratch_shapes=[pltpu.SemaphoreType.DMA((N_DEV - 1,)), pltpu.SemaphoreType.DMA((N_DEV - 1,))],
    )(block, after)


ADA_ROWS = 80
ADA_W = 6 * D_MODEL // N_SHARD


def _ada_forward(c_block, cctx_block, w_ada, b_shard):
    d = c_block.shape[1]

    def body(c_ref, cc_ref, w_ref, b_ref, act_ref, mods_ref, raw, mloc, send_sems, recv_sems):
        x, y, c = _me()
        me = _dev_index(x, y, c)
        s_me = _shard_of((x, y))
        raw[72:ADA_ROWS, :] = jnp.zeros((ADA_ROWS - 72, d), F32)
        raw[pl.ds(pl.multiple_of(me * 8, 8), 8), :] = c_ref[...]
        raw[64:72, :] = cc_ref[...]
        sends = []
        for mask in range(1, N_DEV):
            peer = (x ^ (mask >> 2), y ^ ((mask >> 1) & 1), c ^ (mask & 1))
            cp = _rcopy(c_ref, raw.at[pl.ds(pl.multiple_of(me * 8, 8), 8), :], send_sems, recv_sems, mask - 1, peer)
            cp.start()
            sends.append(cp)
        for mask in range(1, N_DEV):
            peer = (x ^ (mask >> 2), y ^ ((mask >> 1) & 1), c ^ (mask & 1))
            landed = raw.at[pl.ds(pl.multiple_of(_dev_index(*peer) * 8, 8), 8), :]
            _rcopy(landed, landed, send_sems, recv_sems, mask - 1, peer).wait_recv()
        v = raw[...]
        act = v * jax.nn.sigmoid(v)
        act_ref[...] = act
        mloc[...] = lax.dot_general(act.astype(BF16), w_ref[...].astype(BF16), NN,
                                    preferred_element_type=F32) + b_ref[...]
        mods_ref[s_me, 0:8, :] = mloc[pl.ds(pl.multiple_of(me * 8, 8), 8), :]
        mods_ref[s_me, 8:16, :] = mloc[64:72, :]
        base = N_DEV - 1
        for j, chip in enumerate(_other_chips(x, y)):
            peer = (*chip, c)
            rows = mloc.at[pl.ds(pl.multiple_of(_dev_index(*peer) * 8, 8), 8), :]
            cp = _rcopy(rows, mods_ref.at[s_me, 0:8, :], send_sems, recv_sems, base + 2 * j, peer)
            cp.start()
            sends.append(cp)
            cp = _rcopy(mloc.at[64:72, :], mods_ref.at[s_me, 8:16, :], send_sems, recv_sems, base + 2 * j + 1, peer)
            cp.start()
            sends.append(cp)
        for j, chip in enumerate(_other_chips(x, y)):
            for part in range(2):
                landed = mods_ref.at[_shard_of(chip), 8 * part:8 * part + 8, :]
                _rcopy(landed, landed, send_sems, recv_sems, base + 2 * j + part, (*chip, c)).wait_recv()
        for cp in sends:
            cp.wait_send()

    n_sem = N_DEV - 1 + 2 * N_CHIP_PEERS
    return pl.pallas_call(
        body, name="ada_forward",
        in_specs=[VMEM_FULL] * 4, out_specs=[VMEM_FULL, VMEM_FULL],
        out_shape=[jax.ShapeDtypeStruct((ADA_ROWS, d), F32), jax.ShapeDtypeStruct((N_SHARD, 16, ADA_W), F32)],
        scratch_shapes=[pltpu.VMEM((ADA_ROWS, d), F32), pltpu.VMEM((ADA_ROWS, ADA_W), F32),
                        pltpu.SemaphoreType.DMA((n_sem,)), pltpu.SemaphoreType.DMA((n_sem,))],
        compiler_params=pltpu.CompilerParams(vmem_limit_bytes=VMEM_LIMIT),
    )(c_block, cctx_block, w_ada, b_shard)


def _small_reduce(gathered):
    d = gathered.shape[2]

    def body(g_ref, o_ref):
        tot = g_ref[0]
        for i in range(1, N_DEV):
            tot = tot + g_ref[i]
        o_ref[...] = tot
        o_ref[0:2, :] = tot[0:2] + tot[6:8]
        o_ref[12:13, :] = jnp.broadcast_to(jnp.sum(tot[12:13], axis=1, keepdims=True), (1, d))

    return pl.pallas_call(body, name="small_reduce", in_specs=[VMEM_FULL], out_specs=VMEM_FULL,
                          out_shape=jax.ShapeDtypeStruct((16, d), F32))(gathered)


def _cctx_grad(gathered, c_ctx):
    d = gathered.shape[2]

    def body(g_ref, c_ref, o_ref):
        tot = g_ref[0, 0:1, :]
        for chip in range(1, N_SHARD):
            tot = tot + g_ref[2 * chip, 0:1, :]
        v = c_ref[...]
        sig = jax.nn.sigmoid(v)
        o_ref[...] = tot * (sig * (1.0 + v * (1.0 - sig)))

    return pl.pallas_call(body, name="cctx_grad", in_specs=[VMEM_FULL, VMEM_FULL], out_specs=VMEM_FULL,
                          out_shape=jax.ShapeDtypeStruct((1, d), F32))(gathered, c_ctx.reshape(1, d))


def _cast_into_full(w, shard, full, axis, name):
    r, cdim = w.shape
    tr = _fit(r, ROW_TILE)
    nbr = r // tr

    def body(s_ref, w_ref, o_ref):
        o_ref[...] = w_ref[...].astype(BF16)

    if axis == 0:
        out_spec = pl.BlockSpec((tr, cdim), lambda i, s: (s[0] * nbr + i, 0))
    else:
        out_spec = pl.BlockSpec((tr, cdim), lambda i, s: (i, s[0]))
    return pl.pallas_call(
        body, name=name,
        grid_spec=pltpu.PrefetchScalarGridSpec(
            num_scalar_prefetch=1, grid=(nbr,), in_specs=[pl.BlockSpec((tr, cdim), lambda i, s: (i, 0))],
            out_specs=out_spec),
        out_shape=jax.ShapeDtypeStruct(full, BF16), compiler_params=_params(("parallel",)),
    )(shard.reshape(1).astype(jnp.int32), w)


def _adamw_halves(w, g_own, g_other, m, v, core, axis, name):
    r, cdim = w.shape
    hr, hc = (r // 2, cdim) if axis == 1 else (r, cdim // 2)
    assert g_own.shape == (hr, hc) and g_other.shape == (hr, hc)
    tr = _fit(hr, 256)
    nb = hr // tr
    c1 = 1.0 - ADAM_B1 ** ADAM_STEP
    c2 = 1.0 - ADAM_B2 ** ADAM_STEP

    def body(c_ref, w_ref, go_ref, gt_ref, m_ref, v_ref, g_ref, d_ref, nm_ref, nv_ref):
        gv = jnp.where(pl.program_id(0) == c_ref[0], go_ref[...], gt_ref[...])
        nm = ADAM_B1 * m_ref[...] + (1.0 - ADAM_B1) * gv
        nv = ADAM_B2 * v_ref[...] + (1.0 - ADAM_B2) * (gv * gv)
        g_ref[...] = gv
        nm_ref[...] = nm
        nv_ref[...] = nv
        d_ref[...] = -ADAM_LR * ((nm / c1) / (jnp.sqrt(nv / c2) + ADAM_EPS) + ADAM_WD * w_ref[...])

    if axis == 1:
        big = pl.BlockSpec((tr, hc), lambda p, i, c: (p * nb + i, 0))
    else:
        big = pl.BlockSpec((tr, hc), lambda p, i, c: (i, p))
    half = pl.BlockSpec((tr, hc), lambda p, i, c: (i, 0))
    sh = jax.ShapeDtypeStruct((r, cdim), F32)
    return pl.pallas_call(
        body, name=name,
        grid_spec=pltpu.PrefetchScalarGridSpec(
            num_scalar_prefetch=1, grid=(2, nb), in_specs=[big, half, half, big, big], out_specs=[big] * 4),
        out_shape=[sh] * 4, compiler_params=_params(("parallel", "parallel")),
    )(core.reshape(1).astype(jnp.int32), w, g_own, g_other, m, v)


def _adamw(w, g, m, v, name):
    r, cdim = w.shape
    tr = _fit(r, 128) if r % (ROW_TILE // 4) == 0 else r
    c1 = 1.0 - ADAM_B1 ** ADAM_STEP
    c2 = 1.0 - ADAM_B2 ** ADAM_STEP

    def body(w_ref, g_ref, m_ref, v_ref, d_ref, nm_ref, nv_ref):
        gv = g_ref[...]
        nm = ADAM_B1 * m_ref[...] + (1.0 - ADAM_B1) * gv
        nv = ADAM_B2 * v_ref[...] + (1.0 - ADAM_B2) * (gv * gv)
        nm_ref[...] = nm
        nv_ref[...] = nv
        d_ref[...] = -ADAM_LR * ((nm / c1) / (jnp.sqrt(nv / c2) + ADAM_EPS) + ADAM_WD * w_ref[...])

    spec = pl.BlockSpec((tr, cdim), lambda i: (i, 0))
    sh = jax.ShapeDtypeStruct((r, cdim), F32)
    return pl.pallas_call(body, name=name, grid=(r // tr,), in_specs=[spec] * 4, out_specs=[spec] * 3,
                          out_shape=[sh, sh, sh], compiler_params=_params(("parallel",)))(w, g, m, v)


SMALL = (("c_ctx", D_MODEL), ("b_ada", 6 * D_MODEL), ("q_norm_g", HEAD_DIM), ("k_norm_g", HEAD_DIM),
         ("sink_logit", HEADS_A), ("ln1_g", D_MODEL), ("ln1_b", D_MODEL), ("ln2_g", D_MODEL), ("ln2_b", D_MODEL))
WEIGHT_ORDER = ("c_ctx", "w_ada", "b_ada", "w_in", "q_norm_g", "k_norm_g", "sink_logit", "w_out", "ln1_g", "ln1_b",
                "w_gate", "w_up", "w_down", "ln2_g", "ln2_b")


def kernel(x, c, ctx, c_ctx, w_ada, b_ada, w_in, q_norm_g, k_norm_g, sink_logit, w_out, ln1_g, ln1_b, w_gate, w_up, w_down, ln2_g, ln2_b, loss_target, m_c_ctx, m_w_ada, m_b_ada, m_w_in, m_q_norm_g, m_k_norm_g, m_sink_logit, m_w_out, m_ln1_g, m_ln1_b, m_w_gate, m_w_up, m_w_down, m_ln2_g, m_ln2_b, v_c_ctx, v_w_ada, v_b_ada, v_w_in, v_q_norm_g, v_k_norm_g, v_sink_logit, v_w_out, v_ln1_g, v_ln1_b, v_w_gate, v_w_up, v_w_down, v_ln2_g, v_ln2_b):
    d = D_MODEL
    w = dict(c_ctx=c_ctx, w_ada=w_ada[0], b_ada=b_ada, w_in=w_in[0], q_norm_g=q_norm_g, k_norm_g=k_norm_g,
             sink_logit=sink_logit, w_out=w_out[0], ln1_g=ln1_g, ln1_b=ln1_b, w_gate=w_gate[0], w_up=w_up[0],
             w_down=w_down[0], ln2_g=ln2_g, ln2_b=ln2_b)
    m = dict(c_ctx=m_c_ctx, w_ada=m_w_ada[0], b_ada=m_b_ada, w_in=m_w_in[0], q_norm_g=m_q_norm_g, k_norm_g=m_k_norm_g,
             sink_logit=m_sink_logit, w_out=m_w_out[0], ln1_g=m_ln1_g, ln1_b=m_ln1_b, w_gate=m_w_gate[0],
             w_up=m_w_up[0], w_down=m_w_down[0], ln2_g=m_ln2_g, ln2_b=m_ln2_b)
    v = dict(c_ctx=v_c_ctx, w_ada=v_w_ada[0], b_ada=v_b_ada, w_in=v_w_in[0], q_norm_g=v_q_norm_g, k_norm_g=v_k_norm_g,
             sink_logit=v_sink_logit, w_out=v_w_out[0], ln1_g=v_ln1_g, ln1_b=v_ln1_b, w_gate=v_w_gate[0],
             w_up=v_w_up[0], w_down=v_w_down[0], ln2_g=v_ln2_g, ln2_b=v_ln2_b)
    mx, my, mc = _me()
    s_me = _shard_of((mx, my))
    me = _dev_index(mx, my, mc)
    pad8 = lambda row: jnp.concatenate([row.reshape(1, -1), jnp.zeros((7, row.size), F32)], axis=0)

    b_shard = lax.dynamic_slice(b_ada, (0, s_me * ADA_W), (1, ADA_W))
    act, mods4 = _ada_forward(pad8(c), pad8(c_ctx), w["w_ada"], b_shard)

    gathers = []
    prev, shard = mods4, s_me
    for k, names in enumerate(W_GROUPS):
        arrs = tuple(BIG_INDEX[name] for name in names)
        bufs = [_cast_into_full(w[name], shard, BIG[a][1], BIG[a][2], "cast_" + name) for name, a in zip(names, arrs)]
        send_sems, recv_sems, thru, prev = _gather_start("g%d" % k, arrs, bufs, prev)
        shard = s_me + prev[0, 0].astype(jnp.int32)
        gathers.append((arrs, send_sems, recv_sems, thru))

    forwards = {}

    def prefetch(k, after):
        arrs, send_sems, recv_sems, thru = gathers[k]
        landed = _gather_wait("g%d" % k, arrs, send_sems, recv_sems, thru, after)
        fwd_send, fwd_recv, landed, token = _forward_start("g%d" % k, arrs, landed)
        forwards[k] = (fwd_send, fwd_recv, landed)
        return token[0, 0]

    def weights(k, after):
        arrs, send_sems, recv_sems, thru = gathers[k]
        if k in forwards:
            return _forward_wait("g%d" % k, arrs, *forwards[k], after)
        landed = _gather_wait("g%d" % k, arrs, send_sems, recv_sems, thru, after)
        return _gather_forward("g%d" % k, arrs, landed)

    mod = jnp.transpose(mods4[:, 0:1, :], (1, 0, 2)).reshape(1, 6 * d) + prev[0, 0]
    mod_ctx = jnp.transpose(mods4[:, 8:9, :], (1, 0, 2)).reshape(1, 6 * d)

    scatters = {}

    def grads_out(k, dws):
        arrs = tuple(BIG_INDEX[name] for name in G_GROUPS[k])
        scatters[k], zero = _scatter_begin("g%d" % k, arrs, dws)
        return zero

    grad_x, partial = _layer_fwd_bwd(x[0], ctx[0], loss_target[0], mod, mod_ctx, weights, prefetch, grads_out,
                                     q_norm_g, k_norm_g, sink_logit, ln1_g, ln1_b, ln2_g, ln2_b)
    grads, delta, new_m, new_v = {}, {}, {}, {}

    after, joins = partial, []
    for k, names in enumerate(G_GROUPS):
        arrs = tuple(BIG_INDEX[name] for name in names)
        state, after = _scatter_reduce("g%d" % k, arrs, scatters[k], after)
        joins.append(state)
    for k, names in enumerate(G_GROUPS):
        g_own, g_other = _scatter_end("g%d" % k, joins[k], after)
        for name, own, other in zip(names, g_own, g_other):
            grads[name], delta[name], new_m[name], new_v[name] = _adamw_halves(
                w[name], own, other, m[name], v[name], mc, BIG[BIG_INDEX[name]][2], "adamw_" + name)
            after = new_v[name]

    gathered = _gather_rows(partial, "gather_partials", after)
    tot = _small_reduce(gathered)
    grads["b_ada"] = tot[0:6].reshape(1, 6 * d)
    grads["ln1_g"], grads["ln1_b"], grads["ln2_g"], grads["ln2_b"] = tot[8:9], tot[9:10], tot[10:11], tot[11:12]
    grads["q_norm_g"] = tot[13:14, 0:HEAD_DIM]
    grads["k_norm_g"] = tot[13:14, HEAD_DIM:2 * HEAD_DIM]
    grads["sink_logit"] = tot[13:14, 2 * HEAD_DIM:2 * HEAD_DIM + HEADS_A]
    loss = tot[12, 0]

    dm_all = gathered[:, 0:6, :].reshape(N_DEV, 6 * d)
    dmc_tot = jnp.concatenate([tot[6:8].reshape(1, 2 * d), jnp.zeros((1, 4 * d), F32)], axis=1)
    dm_rows = jnp.concatenate([pad8(dm_all[i]) for i in range(N_DEV)] + [pad8(dmc_tot), jnp.zeros((8, 6 * d), F32)], axis=0)
    dm_shard = lax.dynamic_slice(dm_rows, (0, s_me * ADA_W), (ADA_ROWS, ADA_W))
    grads["w_ada"] = _matmul(act, dm_shard, name="dw_ada", ta=True, tm=1024, tn=1024, tk=ADA_ROWS, out_dtype=F32)
    dmc_shard = lax.dynamic_slice(pad8(dmc_tot), (0, s_me * ADA_W), (8, ADA_W))
    cc_part = _matmul(dmc_shard, w["w_ada"], name="d_cctx", tb=True, tm=8, tn=1024, tk=1536, out_dtype=F32)
    grads["c_ctx"] = _cctx_grad(_gather_rows(cc_part, "gather_cctx", tot), c_ctx).reshape(d)

    delta["w_ada"], new_m["w_ada"], new_v["w_ada"] = _adamw(w["w_ada"], grads["w_ada"], m["w_ada"], v["w_ada"],
                                                            "adamw_w_ada")
    pack = lambda t: jnp.concatenate([t[name].reshape(1, size) for name, size in SMALL], axis=1)
    pd, pm, pv = _adamw(pack(w), pack(grads), pack(m), pack(v), "adamw_small")
    off = 0
    for name, size in SMALL:
        delta[name], new_m[name], new_v[name] = [t[:, off:off + size].reshape(w[name].shape) for t in (pd, pm, pv)]
        grads[name] = grads[name].reshape(w[name].shape)
        off += size

    lead = lambda name, t: t[None] if name in ("w_ada", "w_in", "w_out", "w_gate", "w_up", "w_down") else t
    outs = [loss, grad_x[None]]
    for group in (grads, delta, new_m, new_v):
        outs += [lead(name, group[name]) for name in WEIGHT_ORDER]
    return tuple(outs)
```

```python
import functools
import math

import jax
import jax.numpy as jnp
from jax import lax
from jax.experimental import pallas as pl
from jax.experimental.pallas import tpu as pltpu

F32 = jnp.float32
BF16 = jnp.bfloat16
MESH = pl.DeviceIdType.MESH

D_MODEL = 2048
HEAD_DIM = 128
HEADS_A = 8
HEADS_B = 8
KV_A = 2
KV_B = 2
GROUP = 4
GRID_W = 64
WINDOW = 128
BLOCK = 128
FFN = 5632
IN_WIDTH = 3072
MIX_WIDTH = 2048
ROPE_THETA = 10000.0
EPS = 1e-6
ATTN_SCALE = HEAD_DIM ** -0.5
DN_ALPHA = 2.0 ** 0.25
N_SHARD = 4
N_DEV = 8

ADAM_LR = 0.001
ADAM_B1 = 0.9
ADAM_B2 = 0.999
ADAM_EPS = 1e-08
ADAM_WD = 0.01
ADAM_STEP = 10

QA0, KA0, VA0, QB0, KB0, VB0 = 0, 1024, 1280, 1536, 2560, 2816

VMEM_LIMIT = 56 * 1024 * 1024
ROW_TILE = 256
NN = (((1,), (0,)), ((), ()))
NT = (((1,), (1,)), ((), ()))
TN = (((0,), (0,)), ((), ()))


def _fit(total, pref):
    step = ROW_TILE // 4
    best = step
    for cand in range(step, pref + 1, step):
        if total % cand == 0:
            best = cand
    return best


def _params(sem=None):
    return pltpu.CompilerParams(dimension_semantics=sem, vmem_limit_bytes=VMEM_LIMIT)


def _matmul(a, b, *, name, ta=False, tb=False, tm, tn, tk, out_dtype, after=None):
    m = a.shape[1] if ta else a.shape[0]
    k = a.shape[0] if ta else a.shape[1]
    n = b.shape[0] if tb else b.shape[1]
    assert (b.shape[1] if tb else b.shape[0]) == k
    tm, tn, tk = min(tm, m), min(tn, n), min(tk, k)
    assert m % tm == 0 and n % tn == 0 and k % tk == 0, (name, m, n, k, tm, tn, tk)
    nk = k // tk
    dn = (((0 if ta else 1,), (1 if tb else 0,)), ((), ()))

    def product(a_ref, b_ref):
        return lax.dot_general(a_ref[...].astype(BF16), b_ref[...].astype(BF16), dn, preferred_element_type=F32)

    def body_whole_k(a_ref, b_ref, *rest):
        o_ref = rest[-1]
        o_ref[...] = product(a_ref, b_ref).astype(o_ref.dtype)

    def body(a_ref, b_ref, *rest):
        o_ref, acc_ref = rest[-2:]
        kk = pl.program_id(2)
        part = product(a_ref, b_ref)

        @pl.when(kk == 0)
        def _():
            acc_ref[...] = part

        @pl.when(kk != 0)
        def _():
            acc_ref[...] += part

        @pl.when(kk == nk - 1)
        def _():
            o_ref[...] = acc_ref[...].astype(o_ref.dtype)

    a_spec = (pl.BlockSpec((tk, tm), lambda i, j, kk: (kk, i)) if ta
              else pl.BlockSpec((tm, tk), lambda i, j, kk: (i, kk)))
    b_spec = (pl.BlockSpec((tn, tk), lambda i, j, kk: (j, kk)) if tb
              else pl.BlockSpec((tk, tn), lambda i, j, kk: (kk, j)))
    return pl.pallas_call(
        body_whole_k if nk == 1 else body, name=name, grid=(m // tm, n // tn, nk),
        in_specs=[a_spec, b_spec] + ([] if after is None else [pl.BlockSpec(memory_space=pl.ANY)]),
        out_specs=pl.BlockSpec((tm, tn), lambda i, j, kk: (i, j)),
        out_shape=jax.ShapeDtypeStruct((m, n), out_dtype),
        scratch_shapes=[] if nk == 1 else [pltpu.VMEM((tm, tn), F32)],
        compiler_params=_params(("parallel", "parallel", "arbitrary")),
    )(a, b, *([] if after is None else [after]))


def _modulate_rows(x, ctx, mods):
    n, d = x.shape
    c = ctx.shape[0]
    nx = n // ROW_TILE
    assert c == ROW_TILE

    def body(x_ref, ctx_ref, mods_ref, o_ref):
        i = pl.program_id(0)

        @pl.when(i < nx)
        def _():
            o_ref[...] = (x_ref[...] * (1.0 + mods_ref[0:1, :]) + mods_ref[1:2, :]).astype(BF16)

        @pl.when(i >= nx)
        def _():
            o_ref[...] = (ctx_ref[...] * (1.0 + mods_ref[2:3, :]) + mods_ref[3:4, :]).astype(BF16)

    return pl.pallas_call(
        body, name="modulate_rows", grid=(nx + 1,),
        in_specs=[pl.BlockSpec((ROW_TILE, d), lambda i: (jnp.minimum(i, nx - 1), 0)),
                  pl.BlockSpec((ROW_TILE, d), lambda i: (0, 0)),
                  pl.BlockSpec((8, d), lambda i: (0, 0))],
        out_specs=pl.BlockSpec((ROW_TILE, d), lambda i: (i, 0)),
        out_shape=jax.ShapeDtypeStruct((n + c, d), BF16),
        compiler_params=_params(("parallel",)),
    )(x, ctx, mods)


def _rope_tables(n, c):
    rows = n // GRID_W
    row_ids = jnp.repeat(jnp.arange(rows, dtype=F32), GRID_W)
    col_ids = jnp.tile(jnp.arange(GRID_W, dtype=F32), rows)
    axis_dim = HEAD_DIM // 2
    inv_freq = jnp.power(ROPE_THETA, -jnp.arange(0, axis_dim, 2, dtype=F32) / axis_dim)
    ang_r = row_ids[:, None] * inv_freq
    ang_c = col_ids[:, None] * inv_freq
    ang = jnp.concatenate([ang_r, ang_r, ang_c, ang_c], axis=-1)
    cos, sin = jnp.cos(ang), jnp.sin(ang)
    quarter = (jnp.arange(HEAD_DIM) // (HEAD_DIM // 4)) % 2
    sin_a = jnp.where(quarter == 0, -sin, 0.0)
    sin_b = jnp.where(quarter == 1, sin, 0.0)
    pad = lambda t, v: jnp.concatenate([t, jnp.full((c, HEAD_DIM), v, F32)], axis=0)
    return pad(cos, 1.0), pad(sin_a, 0.0), pad(sin_b, 0.0)


def _rope(x, cos, sin_a, sin_b):
    return x * cos + pltpu.roll(x, 96, 1) * sin_a + pltpu.roll(x, 32, 1) * sin_b


def _rope_t(dy, cos, sin_a, sin_b):
    return dy * cos - pltpu.roll(dy, 96, 1) * sin_a - pltpu.roll(dy, 32, 1) * sin_b


def _rms(x):
    r = lax.rsqrt(jnp.mean(x * x, axis=-1, keepdims=True) + EPS)
    return x * r, r


def _qkv_post(h_all, cos, sin_a, sin_b, q_g, k_g):
    t = h_all.shape[0]
    nt = t // ROW_TILE

    def body(h_ref, cos_ref, sa_ref, sb_ref, qg_ref, kg_ref, qa_ref, ka_ref, va_ref, qb_ref, kb_ref, vb_ref):
        cos_, sa, sb = cos_ref[...], sa_ref[...], sb_ref[...]
        sl = lambda off, hh: h_ref[:, off + hh * HEAD_DIM: off + (hh + 1) * HEAD_DIM]
        for hh in range(HEADS_A):
            qa_ref[hh] = (_rope(sl(QA0, hh), cos_, sa, sb) * ATTN_SCALE).astype(BF16)
        for hh in range(KV_A):
            ka_ref[hh] = _rope(sl(KA0, hh), cos_, sa, sb).astype(BF16)
            va_ref[hh] = sl(VA0, hh).astype(BF16)
        for hh in range(HEADS_B):
            xn, _ = _rms(sl(QB0, hh))
            qb_ref[hh] = (_rope(xn * qg_ref[...], cos_, sa, sb) * ATTN_SCALE).astype(BF16)
        for hh in range(KV_B):
            xn, _ = _rms(sl(KB0, hh))
            kb_ref[hh] = _rope(xn * kg_ref[...], cos_, sa, sb).astype(BF16)
            vb_ref[hh] = sl(VB0, hh).astype(BF16)

    tab = pl.BlockSpec((ROW_TILE, HEAD_DIM), lambda i: (i, 0))
    gain = pl.BlockSpec((1, HEAD_DIM), lambda i: (0, 0))
    hs = lambda nh: pl.BlockSpec((nh, ROW_TILE, HEAD_DIM), lambda i: (0, i, 0))
    sh = lambda nh: jax.ShapeDtypeStruct((nh, t, HEAD_DIM), BF16)
    return pl.pallas_call(
        body, name="qkv_post", grid=(nt,),
        in_specs=[pl.BlockSpec((ROW_TILE, IN_WIDTH), lambda i: (i, 0)), tab, tab, tab, gain, gain],
        out_specs=[hs(HEADS_A), hs(KV_A), hs(KV_A), hs(HEADS_B), hs(KV_B), hs(KV_B)],
        out_shape=[sh(HEADS_A), sh(KV_A), sh(KV_A), sh(HEADS_B), sh(KV_B), sh(KV_B)],
        compiler_params=_params(("parallel",)),
    )(h_all, cos, sin_a, sin_b, q_g, k_g)


def _qkv_bwd_post(h_all, cos, sin_a, sin_b, q_g, k_g, dqa, dka, dva, dqb, dkb, dvb, n):
    t = h_all.shape[0]
    nt = t // ROW_TILE
    nx = n // ROW_TILE

    def body(h_ref, cos_ref, sa_ref, sb_ref, qg_ref, kg_ref,
             dqa_ref, dka_ref, dva_ref, dqb_ref, dkb_ref, dvb_ref, dh_ref, gs_ref):
        i = pl.program_id(0)
        cos_, sa, sb = cos_ref[...], sa_ref[...], sb_ref[...]
        latent = (i < nx).astype(F32)
        sl = lambda off, hh: h_ref[:, off + hh * HEAD_DIM: off + (hh + 1) * HEAD_DIM]

        def put(off, hh, val):
            dh_ref[:, off + hh * HEAD_DIM: off + (hh + 1) * HEAD_DIM] = val.astype(BF16)

        def norm_bwd(x, gain, dy):
            xn, r = _rms(x)
            dxh = dy * gain
            dx = r * (dxh - xn * jnp.mean(dxh * xn, axis=-1, keepdims=True))
            return dx, jnp.sum(dy * xn, axis=0, keepdims=True)

        for hh in range(HEADS_A):
            put(QA0, hh, _rope_t(dqa_ref[hh] * (ATTN_SCALE * latent), cos_, sa, sb))
        for hh in range(KV_A):
            put(KA0, hh, _rope_t(dka_ref[hh], cos_, sa, sb))
            put(VA0, hh, dva_ref[hh])
        gq = jnp.zeros((1, HEAD_DIM), F32)
        gk = jnp.zeros((1, HEAD_DIM), F32)
        for hh in range(HEADS_B):
            dq_t = dqb_ref[hh // GROUP, :, (hh % GROUP) * ROW_TILE:(hh % GROUP + 1) * ROW_TILE]
            dy = _rope_t(dq_t.T * (ATTN_SCALE * latent), cos_, sa, sb)
            dx, g = norm_bwd(sl(QB0, hh), qg_ref[...], dy)
            put(QB0, hh, dx)
            gq = gq + g
        for hh in range(KV_B):
            dy = _rope_t(dkb_ref[hh], cos_, sa, sb)
            dx, g = norm_bwd(sl(KB0, hh), kg_ref[...], dy)
            put(KB0, hh, dx)
            gk = gk + g
            put(VB0, hh, dvb_ref[hh])
        upd = jnp.concatenate([gq, gk, jnp.zeros((6, HEAD_DIM), F32)], axis=0)

        @pl.when(i == 0)
        def _():
            gs_ref[...] = upd

        @pl.when(i != 0)
        def _():
            gs_ref[...] += upd

    tab = pl.BlockSpec((ROW_TILE, HEAD_DIM), lambda i: (i, 0))
    gain = pl.BlockSpec((1, HEAD_DIM), lambda i: (0, 0))
    lat = lambda nh: pl.BlockSpec((nh, ROW_TILE, HEAD_DIM), lambda i: (0, jnp.minimum(i, nx - 1), 0))
    full = lambda nh: pl.BlockSpec((nh, ROW_TILE, HEAD_DIM), lambda i: (0, i, 0))
    return pl.pallas_call(
        body, name="qkv_bwd_post", grid=(nt,),
        in_specs=[pl.BlockSpec((ROW_TILE, IN_WIDTH), lambda i: (i, 0)), tab, tab, tab, gain, gain,
                  lat(HEADS_A), full(KV_A), full(KV_A),
                  pl.BlockSpec((KV_B, None, HEAD_DIM, GROUP * ROW_TILE), lambda i: (0, jnp.minimum(i, nx - 1), 0, 0)),
                  full(KV_B), full(KV_B)],
        out_specs=[pl.BlockSpec((ROW_TILE, IN_WIDTH), lambda i: (i, 0)),
                   pl.BlockSpec((8, HEAD_DIM), lambda i: (0, 0))],
        out_shape=[jax.ShapeDtypeStruct((t, IN_WIDTH), BF16), jax.ShapeDtypeStruct((8, HEAD_DIM), F32)],
        compiler_params=_params(("arbitrary",)),
    )(h_all, cos, sin_a, sin_b, q_g, k_g, dqa, dka, dva, dqb, dkb, dvb)


GB_TQ = 256
GB_TK = 256


def _heads_rows(ref2d, tq):
    return jnp.concatenate([ref2d[:, hh * HEAD_DIM:(hh + 1) * HEAD_DIM] for hh in range(GROUP)], axis=0)


def _attn_b_fwd(qb, kb, vb, n):
    t = kb.shape[1]
    nk = t // GB_TK
    tq = GB_TQ
    nq = n // tq
    qb_step = 2 if nq % 2 == 0 else 1
    rows = qb_step * GROUP * tq

    def body(q_ref, k_ref, v_ref, o_ref, lse_ref, m_s, l_s, acc_s):
        blocks = [(s, hh) for s in range(qb_step) for hh in range(GROUP)]
        q = jnp.concatenate([q_ref[hh, s * tq:(s + 1) * tq, :] for s, hh in blocks], axis=0)
        m_s[...] = jnp.full((1, rows), -jnp.inf, F32)
        l_s[...] = jnp.zeros((1, rows), F32)
        acc_s[...] = jnp.zeros((HEAD_DIM, rows), F32)

        def scores(j):
            start = pl.multiple_of(j * GB_TK, GB_TK)
            return lax.dot_general(k_ref[pl.ds(start, GB_TK), :], q, NT, preferred_element_type=F32)

        def step(j, st):
            st_next = scores(jnp.minimum(j + 1, nk - 1))
            vs = v_ref[pl.ds(pl.multiple_of(j * GB_TK, GB_TK), GB_TK), :]
            m_prev = m_s[...]
            m_new = jnp.maximum(m_prev, jnp.max(st, axis=0, keepdims=True))
            p = jnp.exp(st - m_new)
            alpha = jnp.exp(m_prev - m_new)
            l_s[...] = alpha * l_s[...] + jnp.sum(p, axis=0, keepdims=True)
            acc_s[...] = alpha * acc_s[...] + lax.dot_general(vs, p.astype(BF16), TN, preferred_element_type=F32)
            m_s[...] = m_new
            return st_next

        lax.fori_loop(0, nk, step, scores(0))
        ot = acc_s[...] * (1.0 / l_s[...])
        lse = m_s[...] + jnp.log(l_s[...])
        width = GROUP * tq
        for s in range(qb_step):
            lse_ref[s] = lse[:, s * width:(s + 1) * width]
        for k, (s, hh) in enumerate(blocks):
            o_ref[s * tq:(s + 1) * tq, hh * HEAD_DIM:(hh + 1) * HEAD_DIM] = ot[:, k * tq:(k + 1) * tq].T.astype(BF16)

    return pl.pallas_call(
        body, name="attn_b_fwd", grid=(KV_B, nq // qb_step),
        in_specs=[pl.BlockSpec((GROUP, qb_step * tq, HEAD_DIM), lambda g, i: (g, i, 0)),
                  pl.BlockSpec((None, t, HEAD_DIM), lambda g, i: (g, 0, 0)),
                  pl.BlockSpec((None, t, HEAD_DIM), lambda g, i: (g, 0, 0))],
        out_specs=[pl.BlockSpec((qb_step * tq, GROUP * HEAD_DIM), lambda g, i: (i, KV_A + g)),
                   pl.BlockSpec((None, qb_step, 1, GROUP * tq), lambda g, i: (g, i, 0, 0))],
        out_shape=[jax.ShapeDtypeStruct((n, MIX_WIDTH), BF16),
                   jax.ShapeDtypeStruct((KV_B, nq, 1, GROUP * tq), F32)],
        scratch_shapes=[pltpu.VMEM((1, rows), F32), pltpu.VMEM((1, rows), F32), pltpu.VMEM((HEAD_DIM, rows), F32)],
        compiler_params=_params(("parallel", "parallel")),
    )(qb, kb, vb)


def _attn_b_bwd(qb, kb, vb, dheads, lse, delta, n):
    t = kb.shape[1]
    nk = t // GB_TK
    tq = GB_TQ
    nq = n // tq
    rows = GROUP * tq

    qb_step = 4 if nq % 4 == 0 else 1

    def body(q_ref, k_ref, v_ref, do_ref, lse_ref, dl_ref, dq_ref, dk_ref, dv_ref):
        j = pl.program_id(1)
        i = pl.program_id(2)

        blocks = [(s, hh) for s in range(qb_step) for hh in range(GROUP)]
        q = jnp.concatenate([q_ref[hh, s * tq:(s + 1) * tq, :] for s, hh in blocks], axis=0)
        do = jnp.concatenate([do_ref[s * tq:(s + 1) * tq, hh * HEAD_DIM:(hh + 1) * HEAD_DIM] for s, hh in blocks], axis=0)
        lse_row = jnp.concatenate([lse_ref[s] for s in range(qb_step)], axis=1)
        dl_row = jnp.concatenate([dl_ref[s] for s in range(qb_step)], axis=1)
        ks, vs = k_ref[...], v_ref[...]
        st = lax.dot_general(ks, q, NT, preferred_element_type=F32)
        p = jnp.exp(st - lse_row)
        dpt = lax.dot_general(vs, do, NT, preferred_element_type=F32)
        ds = (p * (dpt - dl_row)).astype(BF16)
        dv_part = lax.dot_general(p.astype(BF16), do, NN, preferred_element_type=F32)
        dk_part = lax.dot_general(ds, q, NN, preferred_element_type=F32)
        dq_part = lax.dot_general(ks, ds, TN, preferred_element_type=F32)

        @pl.when(i == 0)
        def _():
            dk_ref[...] = dk_part
            dv_ref[...] = dv_part

        @pl.when(i != 0)
        def _():
            dk_ref[...] += dk_part
            dv_ref[...] += dv_part

        for s in range(qb_step):
            piece = dq_part[:, s * rows:(s + 1) * rows]

            @pl.when(j == 0)
            def _():
                dq_ref[qb_step * i + s] = piece

            @pl.when(j != 0)
            def _():
                dq_ref[qb_step * i + s] += piece

    kv = pl.BlockSpec((None, GB_TK, HEAD_DIM), lambda g, j, i: (g, j, 0))
    row = pl.BlockSpec((None, qb_step, 1, rows), lambda g, j, i: (g, i, 0, 0))
    return pl.pallas_call(
        body, name="attn_b_bwd", grid=(KV_B, nk, nq // qb_step),
        in_specs=[pl.BlockSpec((GROUP, qb_step * tq, HEAD_DIM), lambda g, j, i: (g, i, 0)), kv, kv,
                  pl.BlockSpec((qb_step * tq, GROUP * HEAD_DIM), lambda g, j, i: (i, KV_A + g)), row, row],
        out_specs=[pl.BlockSpec((None, nq, HEAD_DIM, rows), lambda g, j, i: (g, 0, 0, 0)), kv, kv],
        out_shape=[jax.ShapeDtypeStruct((KV_B, nq, HEAD_DIM, rows), F32),
                   jax.ShapeDtypeStruct((KV_B, t, HEAD_DIM), F32),
                   jax.ShapeDtypeStruct((KV_B, t, HEAD_DIM), F32)],
        compiler_params=_params(("parallel", "arbitrary", "arbitrary")),
    )(qb, kb, vb, dheads, lse, delta)


def _delta_rows(dheads, heads):
    n = heads.shape[0]
    tq = GB_TQ
    w = GROUP * HEAD_DIM

    def body(a_ref, b_ref, o_ref):
        prod = a_ref[...].astype(F32) * b_ref[...].astype(F32)
        cols = [jnp.sum(prod[:, hh * HEAD_DIM:(hh + 1) * HEAD_DIM].T, axis=0, keepdims=True) for hh in range(GROUP)]
        o_ref[...] = jnp.concatenate(cols, axis=1)

    blk = pl.BlockSpec((tq, w), lambda g, i: (i, KV_A + g))
    return pl.pallas_call(
        body, name="delta_rows", grid=(KV_B, n // tq),
        in_specs=[blk, blk],
        out_specs=pl.BlockSpec((None, None, 1, GROUP * tq), lambda g, i: (g, i, 0, 0)),
        out_shape=jax.ShapeDtypeStruct((KV_B, n // tq, 1, GROUP * tq), F32),
        compiler_params=_params(("parallel", "parallel")),
    )(dheads, heads)


KWIN = 3 * BLOCK


def _window_scores(q, k_ref, j, n, nb, sink_row):
    c = k_ref.shape[0] - n
    start = pl.multiple_of(jnp.clip(j - 1, 0, nb - 3) * BLOCK, BLOCK)
    kw = k_ref[pl.ds(start, KWIN), :]
    kc = k_ref[pl.ds(n, c), :]
    s_loc = lax.dot_general(kw, q, NT, preferred_element_type=F32)
    s_ctx = lax.dot_general(kc, q, NT, preferred_element_type=F32)
    cols = GROUP * BLOCK
    qpos = j * BLOCK + lax.broadcasted_iota(jnp.int32, (KWIN, cols), 1) % BLOCK
    kpos = start + lax.broadcasted_iota(jnp.int32, (KWIN, cols), 0)
    s_loc = jnp.where(jnp.abs(qpos - kpos) <= WINDOW, s_loc, -jnp.inf)
    m = jnp.maximum(jnp.maximum(jnp.max(s_loc, axis=0, keepdims=True), jnp.max(s_ctx, axis=0, keepdims=True)),
                    sink_row)
    e_loc, e_ctx, e_sink = jnp.exp(s_loc - m), jnp.exp(s_ctx - m), jnp.exp(sink_row - m)
    inv = 1.0 / (jnp.sum(e_loc, axis=0, keepdims=True) + jnp.sum(e_ctx, axis=0, keepdims=True) + e_sink)
    return e_loc * inv, e_ctx * inv, e_sink * inv, start


def _sink_row(sink_ref, g):
    return jnp.concatenate([sink_ref[pl.ds(g * GROUP + hh, 1), :] for hh in range(GROUP)], axis=1)


def _attn_a_fwd(qa, ka, va, sink_b, heads_b, n):
    t = ka.shape[1]
    nb = n // BLOCK
    assert nb >= 3

    def body(q_ref, k_ref, v_ref, sink_ref, heads_b_ref, o_ref):
        g, j = pl.program_id(0), pl.program_id(1)
        q = q_ref[...].reshape(GROUP * BLOCK, HEAD_DIM)
        p_loc, p_ctx, _, start = _window_scores(q, k_ref, j, n, nb, _sink_row(sink_ref, g))
        vw = v_ref[pl.ds(start, KWIN), :]
        vc = v_ref[pl.ds(n, t - n), :]
        ot = (lax.dot_general(vw, p_loc.astype(BF16), TN, preferred_element_type=F32)
              + lax.dot_general(vc, p_ctx.astype(BF16), TN, preferred_element_type=F32))
        for hh in range(GROUP):
            o_ref[:, hh * HEAD_DIM:(hh + 1) * HEAD_DIM] = ot[:, hh * BLOCK:(hh + 1) * BLOCK].T.astype(BF16)

    return pl.pallas_call(
        body, name="attn_a_fwd", grid=(KV_A, nb),
        in_specs=[pl.BlockSpec((GROUP, BLOCK, HEAD_DIM), lambda g, j: (g, j, 0)),
                  pl.BlockSpec((None, t, HEAD_DIM), lambda g, j: (g, 0, 0)),
                  pl.BlockSpec((None, t, HEAD_DIM), lambda g, j: (g, 0, 0)),
                  pl.BlockSpec((HEADS_A, HEAD_DIM), lambda g, j: (0, 0)),
                  pl.BlockSpec(memory_space=pl.ANY)],
        out_specs=pl.BlockSpec((BLOCK, GROUP * HEAD_DIM), lambda g, j: (j, g)),
        out_shape=jax.ShapeDtypeStruct((n, MIX_WIDTH), BF16),
        input_output_aliases={4: 0},
        compiler_params=_params(("parallel", "parallel")),
    )(qa, ka, va, sink_b, heads_b)


def _attn_a_bwd(qa, ka, va, sink_b, dheads, n):
    t = ka.shape[1]
    c = t - n
    nb = n // BLOCK

    def body(q_ref, k_ref, v_ref, sink_ref, do_ref, dq_ref, dk_ref, dv_ref, dsink_ref):
        g, j = pl.program_id(0), pl.program_id(1)

        @pl.when(j == 0)
        def _():
            dk_ref[...] = jnp.zeros_like(dk_ref)
            dv_ref[...] = jnp.zeros_like(dv_ref)
            dsink_ref[...] = jnp.zeros_like(dsink_ref)

        q = q_ref[...].reshape(GROUP * BLOCK, HEAD_DIM)
        do = _heads_rows(do_ref, BLOCK)
        p_loc, p_ctx, p_sink, start = _window_scores(q, k_ref, j, n, nb, _sink_row(sink_ref, g))
        kw, vw = k_ref[pl.ds(start, KWIN), :], v_ref[pl.ds(start, KWIN), :]
        kc, vc = k_ref[pl.ds(n, c), :], v_ref[pl.ds(n, c), :]
        dp_loc = lax.dot_general(vw, do, NT, preferred_element_type=F32)
        dp_ctx = lax.dot_general(vc, do, NT, preferred_element_type=F32)
        dl = jnp.sum(p_loc * dp_loc, axis=0, keepdims=True) + jnp.sum(p_ctx * dp_ctx, axis=0, keepdims=True)
        ds_loc = (p_loc * (dp_loc - dl)).astype(BF16)
        ds_ctx = (p_ctx * (dp_ctx - dl)).astype(BF16)
        dqt = (lax.dot_general(kw, ds_loc, TN, preferred_element_type=F32)
               + lax.dot_general(kc, ds_ctx, TN, preferred_element_type=F32))
        for hh in range(GROUP):
            dq_ref[hh] = dqt[:, hh * BLOCK:(hh + 1) * BLOCK].T
        dk_ref[pl.ds(start, KWIN), :] += lax.dot_general(ds_loc, q, NN, preferred_element_type=F32)
        dv_ref[pl.ds(start, KWIN), :] += lax.dot_general(p_loc.astype(BF16), do, NN, preferred_element_type=F32)
        dk_ref[pl.ds(n, c), :] += lax.dot_general(ds_ctx, q, NN, preferred_element_type=F32)
        dv_ref[pl.ds(n, c), :] += lax.dot_general(p_ctx.astype(BF16), do, NN, preferred_element_type=F32)
        dsk = -(p_sink * dl)
        upd = [jnp.broadcast_to(jnp.sum(dsk[:, hh * BLOCK:(hh + 1) * BLOCK], axis=1, keepdims=True), (1, HEAD_DIM))
               for hh in range(GROUP)]
        dsink_ref[...] += jnp.concatenate(upd + [jnp.zeros((8 - GROUP, HEAD_DIM), F32)], axis=0)

    res = pl.BlockSpec((None, t, HEAD_DIM), lambda g, j: (g, 0, 0))
    return pl.pallas_call(
        body, name="attn_a_bwd", grid=(KV_A, nb),
        in_specs=[pl.BlockSpec((GROUP, BLOCK, HEAD_DIM), lambda g, j: (g, j, 0)), res, res,
                  pl.BlockSpec((HEADS_A, HEAD_DIM), lambda g, j: (0, 0)),
                  pl.BlockSpec((BLOCK, GROUP * HEAD_DIM), lambda g, j: (j, g))],
        out_specs=[pl.BlockSpec((GROUP, BLOCK, HEAD_DIM), lambda g, j: (g, j, 0)), res, res,
                   pl.BlockSpec((None, 8, HEAD_DIM), lambda g, j: (g, 0, 0))],
        out_shape=[jax.ShapeDtypeStruct((HEADS_A, n, HEAD_DIM), F32),
                   jax.ShapeDtypeStruct((KV_A, t, HEAD_DIM), F32),
                   jax.ShapeDtypeStruct((KV_A, t, HEAD_DIM), F32),
                   jax.ShapeDtypeStruct((KV_A, 8, HEAD_DIM), F32)],
        compiler_params=_params(("parallel", "arbitrary")),
    )(qa, ka, va, sink_b, dheads)


def _ln_stats(r):
    mu = jnp.mean(r, axis=-1, keepdims=True)
    cen = r - mu
    rstd = lax.rsqrt(jnp.mean(cen * cen, axis=-1, keepdims=True) + EPS)
    return cen * rstd, rstd


def _ln_bwd(dy, xhat, rstd, gain):
    dxh = dy * gain
    return rstd * (dxh - jnp.mean(dxh, axis=-1, keepdims=True)
                   - xhat * jnp.mean(dxh * xhat, axis=-1, keepdims=True))


def _accumulate_rows(ref, rows, i):
    pad = [jnp.zeros_like(rows[0])] * (8 - len(rows))
    upd = jnp.concatenate(rows + pad, axis=0)

    @pl.when(i == 0)
    def _():
        ref[...] = upd

    @pl.when(i != 0)
    def _():
        ref[...] += upd


def _colsum(v):
    return jnp.sum(v, axis=0, keepdims=True)


LN_TILE = 256


def _res_ln1(x, a, vec):
    n, d = x.shape

    def body(x_ref, a_ref, v_ref, xh_ref, rs_ref, u_ref):
        r1 = DN_ALPHA * x_ref[...] + v_ref[0:1, :] * a_ref[...]
        xhat, rstd = _ln_stats(r1)
        xh_ref[...] = xhat
        rs_ref[...] = rstd
        x1 = xhat * v_ref[1:2, :] + v_ref[2:3, :]
        u_ref[...] = (x1 * (1.0 + v_ref[3:4, :]) + v_ref[4:5, :]).astype(BF16)

    row = pl.BlockSpec((LN_TILE, d), lambda i: (i, 0))
    return pl.pallas_call(
        body, name="res_ln1", grid=(n // LN_TILE,),
        in_specs=[row, row, pl.BlockSpec((8, d), lambda i: (0, 0))],
        out_specs=[row, pl.BlockSpec((LN_TILE, 1), lambda i: (i, 0)), row],
        out_shape=[jax.ShapeDtypeStruct((n, d), F32), jax.ShapeDtypeStruct((n, 1), F32),
                   jax.ShapeDtypeStruct((n, d), BF16)],
        compiler_params=_params(("parallel",)),
    )(x, a, vec)


def _res_ln2_loss(xhat1, f, target, vec):
    n, d = f.shape

    def body(xh_ref, f_ref, t_ref, v_ref, dr_ref, df_ref, s_ref):
        i = pl.program_id(0)
        x1 = xh_ref[...] * v_ref[1:2, :] + v_ref[2:3, :]
        fv = f_ref[...]
        xhat, rstd = _ln_stats(DN_ALPHA * x1 + v_ref[0:1, :] * fv)
        err = xhat * v_ref[3:4, :] + v_ref[4:5, :] - t_ref[...]
        dy = err * (1.0 / d)
        dr2 = _ln_bwd(dy, xhat, rstd, v_ref[3:4, :])
        dr_ref[...] = dr2
        df_ref[...] = (dr2 * v_ref[0:1, :]).astype(BF16)
        _accumulate_rows(s_ref, [_colsum(dy * xhat), _colsum(dy), _colsum(dr2 * fv),
                                 _colsum(err * err) * (0.5 / d)], i)

    row = pl.BlockSpec((LN_TILE, d), lambda i: (i, 0))
    return pl.pallas_call(
        body, name="res_ln2_loss", grid=(n // LN_TILE,),
        in_specs=[row, row, row, pl.BlockSpec((8, d), lambda i: (0, 0))],
        out_specs=[row, row, pl.BlockSpec((8, d), lambda i: (0, 0))],
        out_shape=[jax.ShapeDtypeStruct((n, d), F32), jax.ShapeDtypeStruct((n, d), BF16),
                   jax.ShapeDtypeStruct((8, d), F32)],
        compiler_params=_params(("arbitrary",)),
    )(xhat1, f, target, vec)


def _ln1_bwd(du2, dr2, xhat1, rstd1, a, vec):
    n, d = du2.shape

    def body(du_ref, dr2_ref, xh_ref, rs_ref, a_ref, v_ref, dxp_ref, da_ref, s_ref):
        i = pl.program_id(0)
        du, xhat = du_ref[...], xh_ref[...]
        x1 = xhat * v_ref[1:2, :] + v_ref[2:3, :]
        dx1 = DN_ALPHA * dr2_ref[...] + du * (1.0 + v_ref[0:1, :])
        dr1 = _ln_bwd(dx1, xhat, rs_ref[...], v_ref[1:2, :])
        dxp_ref[...] = DN_ALPHA * dr1
        da_ref[...] = (dr1 * v_ref[3:4, :]).astype(BF16)
        _accumulate_rows(s_ref, [_colsum(du * x1), _colsum(du), _colsum(dx1 * xhat), _colsum(dx1),
                                 _colsum(dr1 * a_ref[...])], i)

    row = pl.BlockSpec((LN_TILE, d), lambda i: (i, 0))
    return pl.pallas_call(
        body, name="ln1_bwd", grid=(n // LN_TILE,),
        in_specs=[row, row, row, pl.BlockSpec((LN_TILE, 1), lambda i: (i, 0)), row,
                  pl.BlockSpec((8, d), lambda i: (0, 0))],
        out_specs=[row, row, pl.BlockSpec((8, d), lambda i: (0, 0))],
        out_shape=[jax.ShapeDtypeStruct((n, d), F32), jax.ShapeDtypeStruct((n, d), BF16),
                   jax.ShapeDtypeStruct((8, d), F32)],
        compiler_params=_params(("arbitrary",)),
    )(du2, dr2, xhat1, rstd1, a, vec)


def _mod1_bwd(du_all, dxp, x, ctx, mods):
    n, d = x.shape
    nx = n // ROW_TILE

    def body(du_ref, dxp_ref, x_ref, ctx_ref, m_ref, gx_ref, s_ref):
        i = pl.program_id(0)
        du = du_ref[...]
        zero = jnp.zeros((1, d), F32)

        @pl.when(i == 0)
        def _():
            s_ref[...] = jnp.zeros_like(s_ref)

        @pl.when(i < nx)
        def _():
            gx_ref[...] = dxp_ref[...] + du * (1.0 + m_ref[0:1, :])
            s_ref[...] += jnp.concatenate([_colsum(du * x_ref[...]), _colsum(du)] + [zero] * 6, axis=0)

        @pl.when(i >= nx)
        def _():
            s_ref[...] += jnp.concatenate([zero, zero, _colsum(du * ctx_ref[...]), _colsum(du)] + [zero] * 4, axis=0)

    lat = pl.BlockSpec((ROW_TILE, d), lambda i: (jnp.minimum(i, nx - 1), 0))
    return pl.pallas_call(
        body, name="mod1_bwd", grid=(nx + 1,),
        in_specs=[pl.BlockSpec((ROW_TILE, d), lambda i: (i, 0)), lat, lat,
                  pl.BlockSpec((ROW_TILE, d), lambda i: (0, 0)), pl.BlockSpec((8, d), lambda i: (0, 0))],
        out_specs=[lat, pl.BlockSpec((8, d), lambda i: (0, 0))],
        out_shape=[jax.ShapeDtypeStruct((n, d), F32), jax.ShapeDtypeStruct((8, d), F32)],
        compiler_params=_params(("arbitrary",)),
    )(du_all, dxp, x, ctx, mods)


FFN_TM = 1024
FFN_TN = 512


def _gate_up(u2, wg, wu, after):
    n, d = u2.shape
    f = wg.shape[1]

    def body(u_ref, wg_ref, wu_ref, after_ref, g_ref, up_ref, h_ref):
        u = u_ref[...]
        g = lax.dot_general(u, wg_ref[...], NN, preferred_element_type=F32)
        up = lax.dot_general(u, wu_ref[...], NN, preferred_element_type=F32)
        g_ref[...] = g.astype(BF16)
        up_ref[...] = up.astype(BF16)
        h_ref[...] = (g * jax.nn.sigmoid(g) * up).astype(BF16)

    tm = min(FFN_TM, n)
    wspec = pl.BlockSpec((d, FFN_TN), lambda j, i: (0, j))
    ospec = pl.BlockSpec((tm, FFN_TN), lambda j, i: (i, j))
    return pl.pallas_call(
        body, name="gate_up", grid=(f // FFN_TN, n // tm),
        in_specs=[pl.BlockSpec((tm, d), lambda j, i: (i, 0)), wspec, wspec, pl.BlockSpec(memory_space=pl.ANY)],
        out_specs=[ospec, ospec, ospec],
        out_shape=[jax.ShapeDtypeStruct((n, f), BF16)] * 3,
        compiler_params=_params(("parallel", "parallel")),
    )(u2, wg, wu, after)


def _glu_bwd(df, wd, g, u):
    n, d = df.shape
    f = wd.shape[0]

    def body(df_ref, wd_ref, g_ref, u_ref, dg_ref, du_ref):
        dh = lax.dot_general(df_ref[...], wd_ref[...], NT, preferred_element_type=F32)
        gv = g_ref[...].astype(F32)
        sig = jax.nn.sigmoid(gv)
        du_ref[...] = (dh * (gv * sig)).astype(BF16)
        dg_ref[...] = (dh * u_ref[...].astype(F32) * (sig * (1.0 + gv * (1.0 - sig)))).astype(BF16)

    tm = min(FFN_TM, n)
    ospec = pl.BlockSpec((tm, FFN_TN), lambda i, j: (i, j))
    return pl.pallas_call(
        body, name="glu_bwd", grid=(n // tm, f // FFN_TN),
        in_specs=[pl.BlockSpec((tm, d), lambda i, j: (i, 0)),
                  pl.BlockSpec((FFN_TN, d), lambda i, j: (j, 0)), ospec, ospec],
        out_specs=[ospec, ospec],
        out_shape=[jax.ShapeDtypeStruct((n, f), BF16), jax.ShapeDtypeStruct((n, f), BF16)],
        compiler_params=_params(("parallel", "parallel")),
    )(df, wd, g, u)


def _du2(dg, du, wg, wu):
    n, f = dg.shape
    d = wg.shape[0]
    tm, tn, tk = min(1024, n), 1024, 1408
    nk = f // tk

    def body(dg_ref, du_ref, wg_ref, wu_ref, o_ref, acc_ref):
        kk = pl.program_id(2)
        part = (lax.dot_general(dg_ref[...], wg_ref[...], NT, preferred_element_type=F32)
                + lax.dot_general(du_ref[...], wu_ref[...], NT, preferred_element_type=F32))

        @pl.when(kk == 0)
        def _():
            acc_ref[...] = part

        @pl.when(kk != 0)
        def _():
            acc_ref[...] += part

        @pl.when(kk == nk - 1)
        def _():
            o_ref[...] = acc_ref[...]

    aspec = pl.BlockSpec((tm, tk), lambda i, j, kk: (i, kk))
    wspec = pl.BlockSpec((tn, tk), lambda i, j, kk: (j, kk))
    return pl.pallas_call(
        body, name="du2", grid=(n // tm, d // tn, nk),
        in_specs=[aspec, aspec, wspec, wspec],
        out_specs=pl.BlockSpec((tm, tn), lambda i, j, kk: (i, j)),
        out_shape=jax.ShapeDtypeStruct((n, d), F32),
        scratch_shapes=[pltpu.VMEM((tm, tn), F32)],
        compiler_params=_params(("parallel", "parallel", "arbitrary")),
    )(dg, du, wg, wu)


def _rows8(rows, d=D_MODEL):
    rows = [r.reshape(1, d).astype(F32) for r in rows]
    return jnp.concatenate(rows + [jnp.zeros((8 - len(rows), d), F32)], axis=0)


W_GROUPS = (("w_in",), ("w_out", "w_gate", "w_up"), ("w_down",))
G_GROUPS = (("w_down", "w_gate", "w_up"), ("w_out",), ("w_in",))


def _layer_fwd_bwd(x, ctx, target, mod, mod_ctx, weights, prefetch, grads_out,
                   q_g, k_g, sink, ln1_g, ln1_b, ln2_g, ln2_b):
    n, d = x.shape
    c = ctx.shape[0]
    sh1, sc1, g1, sh2, sc2, g2 = [mod[:, k * d:(k + 1) * d] for k in range(6)]
    csh1, csc1 = mod_ctx[:, 0:d], mod_ctx[:, d:2 * d]
    cos, sin_a, sin_b = _rope_tables(n, c)
    sink_b = jnp.broadcast_to(sink.reshape(HEADS_A, 1), (HEADS_A, HEAD_DIM)).astype(F32)

    u_all = _modulate_rows(x, ctx, _rows8([sc1, sh1, csc1, csh1]))
    (w_in,) = weights(0, u_all)
    h_all = _matmul(u_all, w_in, name="qkv_proj", tm=_fit(n + c, 1088), tn=1024, tk=2048, out_dtype=F32)
    qa, ka, va, qb, kb, vb = _qkv_post(h_all, cos, sin_a, sin_b, q_g, k_g)
    heads_b, lse = _attn_b_fwd(qb, kb, vb, n)
    zero = prefetch(1, heads_b)
    heads = _attn_a_fwd(qa, ka, va, sink_b + zero, heads_b, n)
    w_out, w_gate, w_up = weights(1, heads)
    a = _matmul(heads, w_out, name="out_proj", tm=1024, tn=1024, tk=2048, out_dtype=F32)
    xhat1, rstd1, u2 = _res_ln1(x, a, _rows8([g1, ln1_g, ln1_b, sc2, sh2]))
    zero = prefetch(2, u2)
    gg, uu, hh = _gate_up(u2, w_gate, w_up, zero.reshape(1, 1))
    (w_down,) = weights(2, hh)
    f = _matmul(hh, w_down, name="ffn_down", tm=1024, tn=512, tk=FFN, out_dtype=F32)
    dr2, df, s_ln2 = _res_ln2_loss(xhat1, f, target, _rows8([g2, ln1_g, ln1_b, ln2_g, ln2_b]))

    dgg, duu = _glu_bwd(df, w_down, gg, uu)
    dw_down = _matmul(hh, df, name="dw_down", ta=True, tm=512, tn=1024, tk=n, out_dtype=BF16)
    dw_gate = _matmul(u2, dgg, name="dw_gate", ta=True, tm=1024, tn=512, tk=n, out_dtype=BF16)
    dw_up = _matmul(u2, duu, name="dw_up", ta=True, tm=1024, tn=512, tk=n, out_dtype=BF16)
    zero = grads_out(0, [dw_down, dw_gate, dw_up])
    du2 = _du2(dgg, duu, w_gate, w_up)
    dxp, da, s_ln1 = _ln1_bwd(du2, dr2, xhat1, rstd1, a, _rows8([sc2, ln1_g, ln1_b, g1]) + zero)

    dheads = _matmul(da, w_out, name="d_heads", tb=True, tm=1024, tn=1024, tk=2048, out_dtype=BF16)
    dw_out = _matmul(heads, da, name="dw_out", ta=True, tm=1024, tn=1024, tk=n, out_dtype=BF16)
    zero = grads_out(1, [dw_out])
    delta = _delta_rows(dheads, heads)
    dqa, dka, dva, dsink = _attn_a_bwd(qa, ka, va, sink_b + zero, dheads, n)
    dqb, dkb, dvb = _attn_b_bwd(qb, kb, vb, dheads, lse, delta, n)
    dh_all, s_gain = _qkv_bwd_post(h_all, cos, sin_a, sin_b, q_g, k_g, dqa, dka, dva, dqb, dkb, dvb, n)
    dw_in = _matmul(u_all, dh_all, name="dw_in", ta=True, tm=1024, tn=1024, tk=n + c, out_dtype=BF16)
    zero = grads_out(2, [dw_in])
    du_all = _matmul(dh_all, w_in, name="d_u1", tb=True, tm=_fit(n + c, 1088), tn=1024, tk=IN_WIDTH, out_dtype=F32,
                     after=zero.reshape(1, 1))
    grad_x, s_mod1 = _mod1_bwd(du_all, dxp, x, ctx, _rows8([sc1]) + zero)

    dsink_row = jnp.concatenate([dsink[0, 0:GROUP, 0], dsink[1, 0:GROUP, 0]]).reshape(1, HEADS_A)
    misc = jnp.concatenate([s_gain[0:1], s_gain[1:2], dsink_row,
                            jnp.zeros((1, d - 2 * HEAD_DIM - HEADS_A), F32)], axis=1)
    partial = jnp.concatenate([
        s_mod1[1:2], s_mod1[0:1], s_ln1[4:5],
        s_ln1[1:2], s_ln1[0:1], s_ln2[2:3],
        s_mod1[3:4], s_mod1[2:3],
        s_ln1[2:3], s_ln1[3:4], s_ln2[0:1], s_ln2[1:2],
        s_ln2[3:4], misc, jnp.zeros((2, d), F32)], axis=0)
    return grad_x, partial


ANY = pl.BlockSpec(memory_space=pl.ANY)
VMEM_FULL = pl.BlockSpec(memory_space=pltpu.VMEM)
N_CHIP_PEERS = 3


def _me():
    return lax.axis_index("x"), lax.axis_index("y"), lax.axis_index("c")


def _other_chips(x, y):
    return [(1 - x, y), (x, 1 - y), (1 - x, 1 - y)]


def _shard_of(chip):
    return 2 * chip[0] + chip[1]


def _dev_index(x, y, c):
    return 4 * x + 2 * y + c


def _rcopy(src, dst, send_sems, recv_sems, k, dev):
    return pltpu.make_async_remote_copy(src_ref=src, dst_ref=dst, send_sem=send_sems.at[k], recv_sem=recv_sems.at[k],
                                        device_id=dev, device_id_type=MESH)


BIG = (("w_in", (D_MODEL, IN_WIDTH), 1), ("w_out", (MIX_WIDTH, D_MODEL), 0), ("w_gate", (D_MODEL, FFN), 1),
       ("w_up", (D_MODEL, FFN), 1), ("w_down", (FFN, D_MODEL), 0))


def _sub(ref, axis, idx, size):
    start = pl.multiple_of(idx * size, size)
    return ref.at[pl.ds(start, size), :] if axis == 0 else ref.at[:, pl.ds(start, size)]


def _shape_div(shape, axis, parts):
    return tuple(s // parts if a == axis else s for a, s in enumerate(shape))


def _piece(a, ref, shard, half):
    _, full, axis = BIG[a]
    view = _sub(ref, axis, shard, full[axis] // N_SHARD)
    return _sub(view, 1 - axis, half, full[1 - axis] // 2)


HBM = pl.BlockSpec(memory_space=pltpu.HBM)
SEM = pl.BlockSpec(memory_space=pltpu.SEMAPHORE)
EFFECT = pltpu.SideEffectType.DATAFLOW_SIDE_EFFECTING
BIG_INDEX = {name: a for a, (name, _, _) in enumerate(BIG)}


def _in_hbm(arr):
    return pltpu.with_memory_space_constraint(arr, pltpu.HBM)


def _gather_start(tag, arrs, bufs, prev):
    n_arr = len(arrs)

    def body(*refs):
        ins = refs[:n_arr]
        send_sems, recv_sems = refs[n_arr + 1], refs[n_arr + 2]
        token = refs[-1]
        x, y, c = _me()
        s_me = _shard_of((x, y))
        for i, a in enumerate(arrs):
            mine = _piece(a, ins[i], s_me, c)
            for j, chip in enumerate(_other_chips(x, y)):
                _rcopy(mine, mine, send_sems, recv_sems, N_CHIP_PEERS * i + j, (*chip, c)).start()
        token[...] = jnp.zeros_like(token)

    n_sem = N_CHIP_PEERS * n_arr
    outs = pl.pallas_call(
        body, name="gather_start_" + tag,
        in_specs=[HBM] * n_arr + [ANY],
        out_specs=[SEM, SEM] + [HBM] * n_arr + [VMEM_FULL],
        out_shape=[pltpu.SemaphoreType.DMA((n_sem,)), pltpu.SemaphoreType.DMA((n_sem,))]
        + [pltpu.HBM(BIG[a][1], BF16) for a in arrs] + [jax.ShapeDtypeStruct((8, HEAD_DIM), F32)],
        input_output_aliases={i: 2 + i for i in range(n_arr)},
        compiler_params=pltpu.CompilerParams(has_side_effects=EFFECT),
    )(*[_in_hbm(b) for b in bufs], prev)
    return outs[0], outs[1], list(outs[2:2 + n_arr]), outs[-1]


def _gather_wait(tag, arrs, send_sems, recv_sems, bufs, after):
    n_arr = len(arrs)

    def body(*refs):
        ins = refs[:n_arr]
        send_sems_, recv_sems_ = refs[n_arr], refs[n_arr + 1]
        x, y, c = _me()
        s_me = _shard_of((x, y))
        for i, a in enumerate(arrs):
            mine = _piece(a, ins[i], s_me, c)
            for j, chip in enumerate(_other_chips(x, y)):
                landed = _piece(a, ins[i], _shard_of(chip), c)
                cp = _rcopy(mine, landed, send_sems_, recv_sems_, N_CHIP_PEERS * i + j, (*chip, c))
                cp.wait_send()
                cp.wait_recv()

    outs = pl.pallas_call(
        body, name="gather_wait_" + tag,
        in_specs=[HBM] * n_arr + [SEM, SEM, ANY],
        out_specs=[HBM] * n_arr,
        out_shape=[pltpu.HBM(BIG[a][1], BF16) for a in arrs],
        input_output_aliases={i: i for i in range(n_arr)},
        compiler_params=pltpu.CompilerParams(has_side_effects=EFFECT),
    )(*bufs, send_sems, recv_sems, after)
    return list(outs)


def _gather_forward(tag, arrs, bufs):
    n_arr = len(arrs)

    def body(*refs):
        outs = refs[n_arr:2 * n_arr]
        send_sems, recv_sems = refs[2 * n_arr:]
        x, y, c = _me()
        sibling = (x, y, 1 - c)
        chips = _other_chips(x, y)
        copies = []
        for i, a in enumerate(arrs):
            for j, chip in enumerate(chips):
                landed = _piece(a, outs[i], _shard_of(chip), c)
                cp = _rcopy(landed, landed, send_sems, recv_sems, N_CHIP_PEERS * i + j, sibling)
                cp.start()
                copies.append(cp)
        for i, a in enumerate(arrs):
            for j, chip in enumerate(chips):
                other = _piece(a, outs[i], _shard_of(chip), 1 - c)
                _rcopy(other, other, send_sems, recv_sems, N_CHIP_PEERS * i + j, sibling).wait_recv()
        for cp in copies:
            cp.wait_send()

    n_sem = N_CHIP_PEERS * n_arr
    return list(pl.pallas_call(
        body, name="gather_forward_" + tag,
        in_specs=[ANY] * n_arr, out_specs=[ANY] * n_arr,
        out_shape=[jax.ShapeDtypeStruct(BIG[a][1], BF16) for a in arrs],
        input_output_aliases={i: i for i in range(n_arr)},
        scratch_shapes=[pltpu.SemaphoreType.DMA((n_sem,)), pltpu.SemaphoreType.DMA((n_sem,))],
    )(*bufs))


def _forward_start(tag, arrs, bufs):
    n_arr = len(arrs)

    def body(*refs):
        ins = refs[:n_arr]
        send_sems, recv_sems = refs[n_arr], refs[n_arr + 1]
        token = refs[-1]
        x, y, c = _me()
        for i, a in enumerate(arrs):
            for j, chip in enumerate(_other_chips(x, y)):
                landed = _piece(a, ins[i], _shard_of(chip), c)
                _rcopy(landed, landed, send_sems, recv_sems, N_CHIP_PEERS * i + j, (x, y, 1 - c)).start()
        token[...] = jnp.zeros_like(token)

    n_sem = N_CHIP_PEERS * n_arr
    outs = pl.pallas_call(
        body, name="gather_forward_start_" + tag,
        in_specs=[HBM] * n_arr,
        out_specs=[SEM, SEM] + [HBM] * n_arr + [VMEM_FULL],
        out_shape=[pltpu.SemaphoreType.DMA((n_sem,)), pltpu.SemaphoreType.DMA((n_sem,))]
        + [pltpu.HBM(BIG[a][1], BF16) for a in arrs] + [jax.ShapeDtypeStruct((8, HEAD_DIM), F32)],
        input_output_aliases={i: 2 + i for i in range(n_arr)},
        compiler_params=pltpu.CompilerParams(has_side_effects=EFFECT),
    )(*bufs)
    return outs[0], outs[1], list(outs[2:2 + n_arr]), outs[-1]


def _forward_wait(tag, arrs, send_sems, recv_sems, bufs, after):
    n_arr = len(arrs)

    def body(*refs):
        ins = refs[:n_arr]
        send_sems_, recv_sems_ = refs[n_arr], refs[n_arr + 1]
        x, y, c = _me()
        for i, a in enumerate(arrs):
            for j, chip in enumerate(_other_chips(x, y)):
                mine = _piece(a, ins[i], _shard_of(chip), c)
                other = _piece(a, ins[i], _shard_of(chip), 1 - c)
                cp = _rcopy(mine, other, send_sems_, recv_sems_, N_CHIP_PEERS * i + j, (x, y, 1 - c))
                cp.wait_send()
                cp.wait_recv()

    outs = pl.pallas_call(
        body, name="gather_forward_wait_" + tag,
        in_specs=[HBM] * n_arr + [SEM, SEM, ANY],
        out_specs=[HBM] * n_arr,
        out_shape=[pltpu.HBM(BIG[a][1], BF16) for a in arrs],
        input_output_aliases={i: i for i in range(n_arr)},
        compiler_params=pltpu.CompilerParams(has_side_effects=EFFECT),
    )(*bufs, send_sems, recv_sems, after)
    return list(outs)


def _peers(x, y, c):
    return [(x ^ (mask >> 2), y ^ ((mask >> 1) & 1), c ^ (mask & 1)) for mask in range(1, N_DEV)]


def _received_shape(a):
    _, full, axis = BIG[a]
    return (N_DEV - 1,) + _shape_div(_shape_div(full, 1 - axis, 2), axis, N_SHARD)


def _pieces_start(tag, arrs, dws):
    n_arr = len(arrs)

    def body(*refs):
        srcs, lands = refs[:n_arr], refs[n_arr:2 * n_arr]
        send_sems, recv_sems = refs[2 * n_arr], refs[2 * n_arr + 1]
        token = refs[-1]
        x, y, c = _me()
        for i, a in enumerate(arrs):
            for k, peer in enumerate(_peers(x, y, c)):
                src = _piece(a, srcs[i], _shard_of(peer[:2]), peer[2])
                _rcopy(src, lands[i].at[k], send_sems, recv_sems, (N_DEV - 1) * i + k, peer).start()
        token[...] = jnp.zeros_like(token)

    n_sem = (N_DEV - 1) * n_arr
    lands = [_in_hbm(lax.empty(_received_shape(a), BF16)) for a in arrs]
    outs = pl.pallas_call(
        body, name="grad_pieces_start_" + tag,
        in_specs=[HBM] * (2 * n_arr),
        out_specs=[SEM, SEM] + [HBM] * (2 * n_arr) + [VMEM_FULL],
        out_shape=[pltpu.SemaphoreType.DMA((n_sem,)), pltpu.SemaphoreType.DMA((n_sem,))]
        + [pltpu.HBM(BIG[a][1], BF16) for a in arrs] + [pltpu.HBM(_received_shape(a), BF16) for a in arrs]
        + [jax.ShapeDtypeStruct((8, HEAD_DIM), F32)],
        input_output_aliases={i: 2 + i for i in range(2 * n_arr)},
        compiler_params=pltpu.CompilerParams(has_side_effects=EFFECT),
    )(*[_in_hbm(dw) for dw in dws], *lands)
    return outs[0], outs[1], list(outs[2:2 + n_arr]), list(outs[2 + n_arr:2 + 2 * n_arr]), outs[-1]


def _pieces_wait(tag, arrs, send_sems, recv_sems, dws, lands, after):
    n_arr = len(arrs)

    def body(*refs):
        srcs, lands_ = refs[:n_arr], refs[n_arr:2 * n_arr]
        send_sems_, recv_sems_ = refs[2 * n_arr], refs[2 * n_arr + 1]
        x, y, c = _me()
        for i, a in enumerate(arrs):
            for k, peer in enumerate(_peers(x, y, c)):
                src = _piece(a, srcs[i], _shard_of(peer[:2]), peer[2])
                cp = _rcopy(src, lands_[i].at[k], send_sems_, recv_sems_, (N_DEV - 1) * i + k, peer)
                cp.wait_send()
                cp.wait_recv()

    outs = pl.pallas_call(
        body, name="grad_pieces_wait_" + tag,
        in_specs=[HBM] * (2 * n_arr) + [SEM, SEM, ANY],
        out_specs=[HBM] * (2 * n_arr),
        out_shape=[pltpu.HBM(BIG[a][1], BF16) for a in arrs] + [pltpu.HBM(_received_shape(a), BF16) for a in arrs],
        input_output_aliases={i: i for i in range(2 * n_arr)},
        compiler_params=pltpu.CompilerParams(has_side_effects=EFFECT),
    )(*dws, *lands, send_sems, recv_sems, after)
    return list(outs[:n_arr]), list(outs[n_arr:])


def _join_start(tag, g_halves):
    n_arr = len(g_halves)

    def body(*refs):
        srcs, lands = refs[:n_arr], refs[n_arr:2 * n_arr]
        send_sems, recv_sems = refs[2 * n_arr], refs[2 * n_arr + 1]
        token = refs[-1]
        x, y, c = _me()
        for i in range(n_arr):
            _rcopy(srcs[i], lands[i], send_sems, recv_sems, i, (x, y, 1 - c)).start()
        token[...] = jnp.zeros_like(token)

    shapes = [pltpu.HBM(g.shape, F32) for g in g_halves]
    outs = pl.pallas_call(
        body, name="grad_join_start_" + tag,
        in_specs=[HBM] * (2 * n_arr),
        out_specs=[SEM, SEM] + [HBM] * (2 * n_arr) + [VMEM_FULL],
        out_shape=[pltpu.SemaphoreType.DMA((n_arr,)), pltpu.SemaphoreType.DMA((n_arr,))] + shapes + shapes
        + [jax.ShapeDtypeStruct((8, HEAD_DIM), F32)],
        input_output_aliases={i: 2 + i for i in range(2 * n_arr)},
        compiler_params=pltpu.CompilerParams(has_side_effects=EFFECT),
    )(*[_in_hbm(g) for g in g_halves], *[_in_hbm(lax.empty(g.shape, F32)) for g in g_halves])
    return outs[0], outs[1], list(outs[2:2 + n_arr]), list(outs[2 + n_arr:2 + 2 * n_arr]), outs[-1]


def _join_wait(tag, send_sems, recv_sems, g_halves, lands, after):
    n_arr = len(g_halves)

    def body(*refs):
        srcs, lands_ = refs[:n_arr], refs[n_arr:2 * n_arr]
        send_sems_, recv_sems_ = refs[2 * n_arr], refs[2 * n_arr + 1]
        x, y, c = _me()
        for i in range(n_arr):
            cp = _rcopy(srcs[i], lands_[i], send_sems_, recv_sems_, i, (x, y, 1 - c))
            cp.wait_send()
            cp.wait_recv()

    shapes = [pltpu.HBM(g.shape, F32) for g in g_halves]
    outs = pl.pallas_call(
        body, name="grad_join_wait_" + tag,
        in_specs=[HBM] * (2 * n_arr) + [SEM, SEM, ANY],
        out_specs=[HBM] * (2 * n_arr),
        out_shape=shapes + shapes,
        input_output_aliases={i: i for i in range(2 * n_arr)},
        compiler_params=pltpu.CompilerParams(has_side_effects=EFFECT),
    )(*g_halves, *lands, send_sems, recv_sems, after)
    return list(outs[:n_arr]), list(outs[n_arr:])


def _piece_sum(a, dw, shard, core, received):
    name, full, axis = BIG[a]
    rows, cols = _received_shape(a)[1:]
    tr = _fit(rows, ROW_TILE)
    nbr = rows // tr

    def body(w_ref, dw_ref, rec_ref, o_ref):
        acc = dw_ref[...].astype(F32)
        for k in range(N_DEV - 1):
            acc = acc + rec_ref[k].astype(F32)
        o_ref[...] = acc

    if axis == 0:
        own = pl.BlockSpec((tr, cols), lambda i, w: (w[0] * nbr + i, w[1]))
    else:
        own = pl.BlockSpec((tr, cols), lambda i, w: (w[1] * nbr + i, w[0]))
    return pl.pallas_call(
        body, name="grad_sum_pieces_" + name,
        grid_spec=pltpu.PrefetchScalarGridSpec(
            num_scalar_prefetch=1, grid=(nbr,),
            in_specs=[own, pl.BlockSpec((N_DEV - 1, tr, cols), lambda i, w: (0, i, 0))],
            out_specs=pl.BlockSpec((tr, cols), lambda i, w: (i, 0))),
        out_shape=jax.ShapeDtypeStruct((rows, cols), F32),
        compiler_params=_params(("parallel",)),
    )(jnp.stack([shard, core]).astype(jnp.int32), dw, received)


def _scatter_begin(tag, arrs, dws):
    send_sems, recv_sems, dws, lands, token = _pieces_start(tag, arrs, dws)
    return (send_sems, recv_sems, dws, lands), token[0, 0]


def _scatter_reduce(tag, arrs, state, after):
    x, y, c = _me()
    send_sems, recv_sems, dws, lands = state
    dws, lands = _pieces_wait(tag, arrs, send_sems, recv_sems, dws, lands, after)
    g_own = [_piece_sum(a, dw, _shard_of((x, y)), c, r) for a, dw, r in zip(arrs, dws, lands)]
    send_sems, recv_sems, g_own, lands, token = _join_start(tag, g_own)
    return (send_sems, recv_sems, g_own, lands), token


def _scatter_end(tag, state, after):
    return _join_wait(tag, *state, after)


def _gather_rows(block, name, after):
    r, d = block.shape

    def body(in_ref, after_ref, out_ref, send_sems, recv_sems):
        x, y, c = _me()
        out_ref[_dev_index(x, y, c)] = in_ref[...]
        copies = []
        for mask in range(1, N_DEV):
            peer = (x ^ (mask >> 2), y ^ ((mask >> 1) & 1), c ^ (mask & 1))
            cp = _rcopy(in_ref, out_ref.at[_dev_index(x, y, c)], send_sems, recv_sems, mask - 1, peer)
            cp.start()
            copies.append((cp, peer))
        for mask in range(1, N_DEV):
            peer = (x ^ (mask >> 2), y ^ ((mask >> 1) & 1), c ^ (mask & 1))
            landed = out_ref.at[_dev_index(*peer)]
            _rcopy(landed, landed, send_sems, recv_sems, mask - 1, peer).wait_recv()
        for cp, _ in copies:
            cp.wait_send()

    return pl.pallas_call(
        body, name=name, in_specs=[VMEM_FULL, ANY], out_specs=VMEM_FULL,
        out_shape=jax.ShapeDtypeStruct((N_DEV, r, d), F32),
        scratch_shapes=[pltpu.SemaphoreType.DMA((N_DEV - 1,)), pltpu.SemaphoreType.DMA((N_DEV - 1,))],
    )(block, after)


ADA_ROWS = 80
ADA_W = 6 * D_MODEL // N_SHARD


def _ada_forward(c_block, cctx_block, w_ada, b_shard):
    d = c_block.shape[1]

    def body(c_ref, cc_ref, w_ref, b_ref, act_ref, mods_ref, raw, mloc, send_sems, recv_sems):
        x, y, c = _me()
        me = _dev_index(x, y, c)
        s_me = _shard_of((x, y))
        raw[72:ADA_ROWS, :] = jnp.zeros((ADA_ROWS - 72, d), F32)
        raw[pl.ds(pl.multiple_of(me * 8, 8), 8), :] = c_ref[...]
        raw[64:72, :] = cc_ref[...]
        sends = []
        for mask in range(1, N_DEV):
            peer = (x ^ (mask >> 2), y ^ ((mask >> 1) & 1), c ^ (mask & 1))
            cp = _rcopy(c_ref, raw.at[pl.ds(pl.multiple_of(me * 8, 8), 8), :], send_sems, recv_sems, mask - 1, peer)
            cp.start()
            sends.append(cp)
        for mask in range(1, N_DEV):
            peer = (x ^ (mask >> 2), y ^ ((mask >> 1) & 1), c ^ (mask & 1))
            landed = raw.at[pl.ds(pl.multiple_of(_dev_index(*peer) * 8, 8), 8), :]
            _rcopy(landed, landed, send_sems, recv_sems, mask - 1, peer).wait_recv()
        v = raw[...]
        act = v * jax.nn.sigmoid(v)
        act_ref[...] = act
        mloc[...] = lax.dot_general(act.astype(BF16), w_ref[...].astype(BF16), NN,
                                    preferred_element_type=F32) + b_ref[...]
        mods_ref[s_me, 0:8, :] = mloc[pl.ds(pl.multiple_of(me * 8, 8), 8), :]
        mods_ref[s_me, 8:16, :] = mloc[64:72, :]
        base = N_DEV - 1
        for j, chip in enumerate(_other_chips(x, y)):
            peer = (*chip, c)
            rows = mloc.at[pl.ds(pl.multiple_of(_dev_index(*peer) * 8, 8), 8), :]
            cp = _rcopy(rows, mods_ref.at[s_me, 0:8, :], send_sems, recv_sems, base + 2 * j, peer)
            cp.start()
            sends.append(cp)
            cp = _rcopy(mloc.at[64:72, :], mods_ref.at[s_me, 8:16, :], send_sems, recv_sems, base + 2 * j + 1, peer)
            cp.start()
            sends.append(cp)
        for j, chip in enumerate(_other_chips(x, y)):
            for part in range(2):
                landed = mods_ref.at[_shard_of(chip), 8 * part:8 * part + 8, :]
                _rcopy(landed, landed, send_sems, recv_sems, base + 2 * j + part, (*chip, c)).wait_recv()
        for cp in sends:
            cp.wait_send()

    n_sem = N_DEV - 1 + 2 * N_CHIP_PEERS
    return pl.pallas_call(
        body, name="ada_forward",
        in_specs=[VMEM_FULL] * 4, out_specs=[VMEM_FULL, VMEM_FULL],
        out_shape=[jax.ShapeDtypeStruct((ADA_ROWS, d), F32), jax.ShapeDtypeStruct((N_SHARD, 16, ADA_W), F32)],
        scratch_shapes=[pltpu.VMEM((ADA_ROWS, d), F32), pltpu.VMEM((ADA_ROWS, ADA_W), F32),
                        pltpu.SemaphoreType.DMA((n_sem,)), pltpu.SemaphoreType.DMA((n_sem,))],
        compiler_params=pltpu.CompilerParams(vmem_limit_bytes=VMEM_LIMIT),
    )(c_block, cctx_block, w_ada, b_shard)


def _small_reduce(gathered):
    d = gathered.shape[2]

    def body(g_ref, o_ref):
        tot = g_ref[0]
        for i in range(1, N_DEV):
            tot = tot + g_ref[i]
        o_ref[...] = tot
        o_ref[0:2, :] = tot[0:2] + tot[6:8]
        o_ref[12:13, :] = jnp.broadcast_to(jnp.sum(tot[12:13], axis=1, keepdims=True), (1, d))

    return pl.pallas_call(body, name="small_reduce", in_specs=[VMEM_FULL], out_specs=VMEM_FULL,
                          out_shape=jax.ShapeDtypeStruct((16, d), F32))(gathered)


def _cctx_grad(gathered, c_ctx):
    d = gathered.shape[2]

    def body(g_ref, c_ref, o_ref):
        tot = g_ref[0, 0:1, :]
        for chip in range(1, N_SHARD):
            tot = tot + g_ref[2 * chip, 0:1, :]
        v = c_ref[...]
        sig = jax.nn.sigmoid(v)
        o_ref[...] = tot * (sig * (1.0 + v * (1.0 - sig)))

    return pl.pallas_call(body, name="cctx_grad", in_specs=[VMEM_FULL, VMEM_FULL], out_specs=VMEM_FULL,
                          out_shape=jax.ShapeDtypeStruct((1, d), F32))(gathered, c_ctx.reshape(1, d))


def _cast_into_full(w, shard, full, axis, name):
    r, cdim = w.shape
    tr = _fit(r, ROW_TILE)
    nbr = r // tr

    def body(s_ref, w_ref, o_ref):
        o_ref[...] = w_ref[...].astype(BF16)

    if axis == 0:
        out_spec = pl.BlockSpec((tr, cdim), lambda i, s: (s[0] * nbr + i, 0))
    else:
        out_spec = pl.BlockSpec((tr, cdim), lambda i, s: (i, s[0]))
    return pl.pallas_call(
        body, name=name,
        grid_spec=pltpu.PrefetchScalarGridSpec(
            num_scalar_prefetch=1, grid=(nbr,), in_specs=[pl.BlockSpec((tr, cdim), lambda i, s: (i, 0))],
            out_specs=out_spec),
        out_shape=jax.ShapeDtypeStruct(full, BF16), compiler_params=_params(("parallel",)),
    )(shard.reshape(1).astype(jnp.int32), w)


def _adamw_halves(w, g_own, g_other, m, v, core, axis, name):
    r, cdim = w.shape
    hr, hc = (r // 2, cdim) if axis == 1 else (r, cdim // 2)
    assert g_own.shape == (hr, hc) and g_other.shape == (hr, hc)
    tr = _fit(hr, 256)
    nb = hr // tr
    c1 = 1.0 - ADAM_B1 ** ADAM_STEP
    c2 = 1.0 - ADAM_B2 ** ADAM_STEP

    def body(c_ref, w_ref, go_ref, gt_ref, m_ref, v_ref, g_ref, d_ref, nm_ref, nv_ref):
        gv = jnp.where(pl.program_id(0) == c_ref[0], go_ref[...], gt_ref[...])
        nm = ADAM_B1 * m_ref[...] + (1.0 - ADAM_B1) * gv
        nv = ADAM_B2 * v_ref[...] + (1.0 - ADAM_B2) * (gv * gv)
        g_ref[...] = gv
        nm_ref[...] = nm
        nv_ref[...] = nv
        d_ref[...] = -ADAM_LR * ((nm / c1) / (jnp.sqrt(nv / c2) + ADAM_EPS) + ADAM_WD * w_ref[...])

    if axis == 1:
        big = pl.BlockSpec((tr, hc), lambda p, i, c: (p * nb + i, 0))
    else:
        big = pl.BlockSpec((tr, hc), lambda p, i, c: (i, p))
    own = pl.BlockSpec((tr, hc), lambda p, i, c: (jnp.where(p == c[0], i, 0), 0))
    other = pl.BlockSpec((tr, hc), lambda p, i, c: (jnp.where(p == c[0], 0, i), 0))
    sh = jax.ShapeDtypeStruct((r, cdim), F32)
    return pl.pallas_call(
        body, name=name,
        grid_spec=pltpu.PrefetchScalarGridSpec(
            num_scalar_prefetch=1, grid=(2, nb), in_specs=[big, own, other, big, big], out_specs=[big] * 4),
        out_shape=[sh] * 4, compiler_params=_params(("parallel", "parallel")),
    )(core.reshape(1).astype(jnp.int32), w, g_own, g_other, m, v)


def _adamw(w, g, m, v, name):
    r, cdim = w.shape
    tr = _fit(r, 128) if r % (ROW_TILE // 4) == 0 else r
    c1 = 1.0 - ADAM_B1 ** ADAM_STEP
    c2 = 1.0 - ADAM_B2 ** ADAM_STEP

    def body(w_ref, g_ref, m_ref, v_ref, d_ref, nm_ref, nv_ref):
        gv = g_ref[...]
        nm = ADAM_B1 * m_ref[...] + (1.0 - ADAM_B1) * gv
        nv = ADAM_B2 * v_ref[...] + (1.0 - ADAM_B2) * (gv * gv)
        nm_ref[...] = nm
        nv_ref[...] = nv
        d_ref[...] = -ADAM_LR * ((nm / c1) / (jnp.sqrt(nv / c2) + ADAM_EPS) + ADAM_WD * w_ref[...])

    spec = pl.BlockSpec((tr, cdim), lambda i: (i, 0))
    sh = jax.ShapeDtypeStruct((r, cdim), F32)
    return pl.pallas_call(body, name=name, grid=(r // tr,), in_specs=[spec] * 4, out_specs=[spec] * 3,
                          out_shape=[sh, sh, sh], compiler_params=_params(("parallel",)))(w, g, m, v)


def _adamw_small(ws, gs, ms, vs):
    k = len(ws)
    c1 = 1.0 - ADAM_B1 ** ADAM_STEP
    c2 = 1.0 - ADAM_B2 ** ADAM_STEP

    def body(*refs):
        w_refs, g_refs, m_refs, v_refs = refs[0:k], refs[k:2 * k], refs[2 * k:3 * k], refs[3 * k:4 * k]
        d_refs, nm_refs, nv_refs = refs[4 * k:5 * k], refs[5 * k:6 * k], refs[6 * k:7 * k]
        for i in range(k):
            gv = g_refs[i][...]
            nm = ADAM_B1 * m_refs[i][...] + (1.0 - ADAM_B1) * gv
            nv = ADAM_B2 * v_refs[i][...] + (1.0 - ADAM_B2) * (gv * gv)
            nm_refs[i][...] = nm
            nv_refs[i][...] = nv
            d_refs[i][...] = -ADAM_LR * ((nm / c1) / (jnp.sqrt(nv / c2) + ADAM_EPS) + ADAM_WD * w_refs[i][...])

    shapes = [jax.ShapeDtypeStruct(w.shape, F32) for w in ws]
    outs = pl.pallas_call(body, name="adamw_small", in_specs=[VMEM_FULL] * (4 * k), out_specs=[VMEM_FULL] * (3 * k),
                          out_shape=shapes * 3)(*ws, *gs, *ms, *vs)
    return outs[0:k], outs[k:2 * k], outs[2 * k:3 * k]


SMALL = (("c_ctx", D_MODEL), ("b_ada", 6 * D_MODEL), ("q_norm_g", HEAD_DIM), ("k_norm_g", HEAD_DIM),
         ("sink_logit", HEADS_A), ("ln1_g", D_MODEL), ("ln1_b", D_MODEL), ("ln2_g", D_MODEL), ("ln2_b", D_MODEL))
WEIGHT_ORDER = ("c_ctx", "w_ada", "b_ada", "w_in", "q_norm_g", "k_norm_g", "sink_logit", "w_out", "ln1_g", "ln1_b",
                "w_gate", "w_up", "w_down", "ln2_g", "ln2_b")


def kernel(x, c, ctx, c_ctx, w_ada, b_ada, w_in, q_norm_g, k_norm_g, sink_logit, w_out, ln1_g, ln1_b, w_gate, w_up, w_down, ln2_g, ln2_b, loss_target, m_c_ctx, m_w_ada, m_b_ada, m_w_in, m_q_norm_g, m_k_norm_g, m_sink_logit, m_w_out, m_ln1_g, m_ln1_b, m_w_gate, m_w_up, m_w_down, m_ln2_g, m_ln2_b, v_c_ctx, v_w_ada, v_b_ada, v_w_in, v_q_norm_g, v_k_norm_g, v_sink_logit, v_w_out, v_ln1_g, v_ln1_b, v_w_gate, v_w_up, v_w_down, v_ln2_g, v_ln2_b):
    d = D_MODEL
    w = dict(c_ctx=c_ctx, w_ada=w_ada[0], b_ada=b_ada, w_in=w_in[0], q_norm_g=q_norm_g, k_norm_g=k_norm_g,
             sink_logit=sink_logit, w_out=w_out[0], ln1_g=ln1_g, ln1_b=ln1_b, w_gate=w_gate[0], w_up=w_up[0],
             w_down=w_down[0], ln2_g=ln2_g, ln2_b=ln2_b)
    m = dict(c_ctx=m_c_ctx, w_ada=m_w_ada[0], b_ada=m_b_ada, w_in=m_w_in[0], q_norm_g=m_q_norm_g, k_norm_g=m_k_norm_g,
             sink_logit=m_sink_logit, w_out=m_w_out[0], ln1_g=m_ln1_g, ln1_b=m_ln1_b, w_gate=m_w_gate[0],
             w_up=m_w_up[0], w_down=m_w_down[0], ln2_g=m_ln2_g, ln2_b=m_ln2_b)
    v = dict(c_ctx=v_c_ctx, w_ada=v_w_ada[0], b_ada=v_b_ada, w_in=v_w_in[0], q_norm_g=v_q_norm_g, k_norm_g=v_k_norm_g,
             sink_logit=v_sink_logit, w_out=v_w_out[0], ln1_g=v_ln1_g, ln1_b=v_ln1_b, w_gate=v_w_gate[0],
             w_up=v_w_up[0], w_down=v_w_down[0], ln2_g=v_ln2_g, ln2_b=v_ln2_b)
    mx, my, mc = _me()
    s_me = _shard_of((mx, my))
    me = _dev_index(mx, my, mc)
    pad8 = lambda row: jnp.concatenate([row.reshape(1, -1), jnp.zeros((7, row.size), F32)], axis=0)

    b_shard = lax.dynamic_slice(b_ada, (0, s_me * ADA_W), (1, ADA_W))
    act, mods4 = _ada_forward(pad8(c), pad8(c_ctx), w["w_ada"], b_shard)

    gathers = []
    prev, shard = mods4, s_me
    for k, names in enumerate(W_GROUPS):
        arrs = tuple(BIG_INDEX[name] for name in names)
        bufs = [_cast_into_full(w[name], shard, BIG[a][1], BIG[a][2], "cast_" + name) for name, a in zip(names, arrs)]
        send_sems, recv_sems, thru, prev = _gather_start("g%d" % k, arrs, bufs, prev)
        shard = s_me + prev[0, 0].astype(jnp.int32)
        gathers.append((arrs, send_sems, recv_sems, thru))

    forwards = {}

    def prefetch(k, after):
        arrs, send_sems, recv_sems, thru = gathers[k]
        landed = _gather_wait("g%d" % k, arrs, send_sems, recv_sems, thru, after)
        fwd_send, fwd_recv, landed, token = _forward_start("g%d" % k, arrs, landed)
        forwards[k] = (fwd_send, fwd_recv, landed)
        return token[0, 0]

    def weights(k, after):
        arrs, send_sems, recv_sems, thru = gathers[k]
        if k in forwards:
            return _forward_wait("g%d" % k, arrs, *forwards[k], after)
        landed = _gather_wait("g%d" % k, arrs, send_sems, recv_sems, thru, after)
        return _gather_forward("g%d" % k, arrs, landed)

    mod = jnp.transpose(mods4[:, 0:1, :], (1, 0, 2)).reshape(1, 6 * d) + prev[0, 0]
    mod_ctx = jnp.transpose(mods4[:, 8:9, :], (1, 0, 2)).reshape(1, 6 * d)

    scatters = {}

    def grads_out(k, dws):
        arrs = tuple(BIG_INDEX[name] for name in G_GROUPS[k])
        scatters[k], zero = _scatter_begin("g%d" % k, arrs, dws)
        return zero

    grad_x, partial = _layer_fwd_bwd(x[0], ctx[0], loss_target[0], mod, mod_ctx, weights, prefetch, grads_out,
                                     q_norm_g, k_norm_g, sink_logit, ln1_g, ln1_b, ln2_g, ln2_b)
    grads, delta, new_m, new_v = {}, {}, {}, {}

    after, joins = partial, []
    for k, names in enumerate(G_GROUPS):
        arrs = tuple(BIG_INDEX[name] for name in names)
        state, after = _scatter_reduce("g%d" % k, arrs, scatters[k], after)
        joins.append(state)
    for k, names in enumerate(G_GROUPS):
        g_own, g_other = _scatter_end("g%d" % k, joins[k], after)
        for name, own, other in zip(names, g_own, g_other):
            grads[name], delta[name], new_m[name], new_v[name] = _adamw_halves(
                w[name], own, other, m[name], v[name], mc, BIG[BIG_INDEX[name]][2], "adamw_" + name)
            after = new_v[name]

    gathered = _gather_rows(partial, "gather_partials", after)
    tot = _small_reduce(gathered)
    grads["b_ada"] = tot[0:6].reshape(1, 6 * d)
    grads["ln1_g"], grads["ln1_b"], grads["ln2_g"], grads["ln2_b"] = tot[8:9], tot[9:10], tot[10:11], tot[11:12]
    grads["q_norm_g"] = tot[13:14, 0:HEAD_DIM]
    grads["k_norm_g"] = tot[13:14, HEAD_DIM:2 * HEAD_DIM]
    grads["sink_logit"] = tot[13:14, 2 * HEAD_DIM:2 * HEAD_DIM + HEADS_A]
    loss = tot[12, 0]

    dm_all = gathered[:, 0:6, :].reshape(N_DEV, 6 * d)
    dmc_tot = jnp.concatenate([tot[6:8].reshape(1, 2 * d), jnp.zeros((1, 4 * d), F32)], axis=1)
    dm_rows = jnp.concatenate([pad8(dm_all[i]) for i in range(N_DEV)] + [pad8(dmc_tot), jnp.zeros((8, 6 * d), F32)], axis=0)
    dm_shard = lax.dynamic_slice(dm_rows, (0, s_me * ADA_W), (ADA_ROWS, ADA_W))
    grads["w_ada"] = _matmul(act, dm_shard, name="dw_ada", ta=True, tm=1024, tn=1024, tk=ADA_ROWS, out_dtype=F32)
    dmc_shard = lax.dynamic_slice(pad8(dmc_tot), (0, s_me * ADA_W), (8, ADA_W))
    cc_part = _matmul(dmc_shard, w["w_ada"], name="d_cctx", tb=True, tm=8, tn=1024, tk=1536, out_dtype=F32)
    grads["c_ctx"] = _cctx_grad(_gather_rows(cc_part, "gather_cctx", tot), c_ctx).reshape(d)

    delta["w_ada"], new_m["w_ada"], new_v["w_ada"] = _adamw(w["w_ada"], grads["w_ada"], m["w_ada"], v["w_ada"],
                                                            "adamw_w_ada")
    rows = lambda t: [t[name].reshape(1, size) for name, size in SMALL]
    small = _adamw_small(rows(w), rows(grads), rows(m), rows(v))
    for k, (name, size) in enumerate(SMALL):
        delta[name], new_m[name], new_v[name] = [t[k].reshape(w[name].shape) for t in small]
        grads[name] = grads[name].reshape(w[name].shape)

    lead = lambda name, t: t[None] if name in ("w_ada", "w_in", "w_out", "w_gate", "w_up", "w_down") else t
    outs = [loss, grad_x[None]]
    for group in (grads, delta, new_m, new_v):
        outs += [lead(name, group[name]) for name in WEIGHT_ORDER]
    return tuple(outs)
```

```python
import functools
import math

import jax
import jax.numpy as jnp
from jax import lax
from jax.experimental import pallas as pl
from jax.experimental.pallas import tpu as pltpu

F32 = jnp.float32
BF16 = jnp.bfloat16
MESH = pl.DeviceIdType.MESH

D_MODEL = 2048
HEAD_DIM = 128
HEADS_A = 8
HEADS_B = 8
KV_A = 2
KV_B = 2
GROUP = 4
GRID_W = 64
WINDOW = 128
BLOCK = 128
FFN = 5632
IN_WIDTH = 3072
MIX_WIDTH = 2048
ROPE_THETA = 10000.0
EPS = 1e-6
ATTN_SCALE = HEAD_DIM ** -0.5
DN_ALPHA = 2.0 ** 0.25
N_SHARD = 4
N_DEV = 8

ADAM_LR = 0.001
ADAM_B1 = 0.9
ADAM_B2 = 0.999
ADAM_EPS = 1e-08
ADAM_WD = 0.01
ADAM_STEP = 10

QA0, KA0, VA0, QB0, KB0, VB0 = 0, 1024, 1280, 1536, 2560, 2816

VMEM_LIMIT = 56 * 1024 * 1024
ROW_TILE = 256
NN = (((1,), (0,)), ((), ()))
NT = (((1,), (1,)), ((), ()))
TN = (((0,), (0,)), ((), ()))


def _fit(total, pref):
    step = ROW_TILE // 4
    best = step
    for cand in range(step, pref + 1, step):
        if total % cand == 0:
            best = cand
    return best


def _params(sem=None):
    return pltpu.CompilerParams(dimension_semantics=sem, vmem_limit_bytes=VMEM_LIMIT)


def _matmul(a, b, *, name, ta=False, tb=False, tm, tn, tk, out_dtype, after=None):
    m = a.shape[1] if ta else a.shape[0]
    k = a.shape[0] if ta else a.shape[1]
    n = b.shape[0] if tb else b.shape[1]
    assert (b.shape[1] if tb else b.shape[0]) == k
    tm, tn, tk = min(tm, m), min(tn, n), min(tk, k)
    assert m % tm == 0 and n % tn == 0 and k % tk == 0, (name, m, n, k, tm, tn, tk)
    nk = k // tk
    dn = (((0 if ta else 1,), (1 if tb else 0,)), ((), ()))

    def product(a_ref, b_ref):
        return lax.dot_general(a_ref[...].astype(BF16), b_ref[...].astype(BF16), dn, preferred_element_type=F32)

    def body_whole_k(a_ref, b_ref, *rest):
        o_ref = rest[-1]
        o_ref[...] = product(a_ref, b_ref).astype(o_ref.dtype)

    def body(a_ref, b_ref, *rest):
        o_ref, acc_ref = rest[-2:]
        kk = pl.program_id(2)
        part = product(a_ref, b_ref)

        @pl.when(kk == 0)
        def _():
            acc_ref[...] = part

        @pl.when(kk != 0)
        def _():
            acc_ref[...] += part

        @pl.when(kk == nk - 1)
        def _():
            o_ref[...] = acc_ref[...].astype(o_ref.dtype)

    a_spec = (pl.BlockSpec((tk, tm), lambda i, j, kk: (kk, i)) if ta
              else pl.BlockSpec((tm, tk), lambda i, j, kk: (i, kk)))
    b_spec = (pl.BlockSpec((tn, tk), lambda i, j, kk: (j, kk)) if tb
              else pl.BlockSpec((tk, tn), lambda i, j, kk: (kk, j)))
    return pl.pallas_call(
        body_whole_k if nk == 1 else body, name=name, grid=(m // tm, n // tn, nk),
        in_specs=[a_spec, b_spec] + ([] if after is None else [pl.BlockSpec(memory_space=pl.ANY)]),
        out_specs=pl.BlockSpec((tm, tn), lambda i, j, kk: (i, j)),
        out_shape=jax.ShapeDtypeStruct((m, n), out_dtype),
        scratch_shapes=[] if nk == 1 else [pltpu.VMEM((tm, tn), F32)],
        compiler_params=_params(("parallel", "parallel", "arbitrary")),
    )(a, b, *([] if after is None else [after]))


def _modulate_rows(x, ctx, mods):
    n, d = x.shape
    c = ctx.shape[0]
    nx = n // ROW_TILE
    assert c == ROW_TILE

    def body(x_ref, ctx_ref, mods_ref, o_ref):
        i = pl.program_id(0)

        @pl.when(i < nx)
        def _():
            o_ref[...] = (x_ref[...] * (1.0 + mods_ref[0:1, :]) + mods_ref[1:2, :]).astype(BF16)

        @pl.when(i >= nx)
        def _():
            o_ref[...] = (ctx_ref[...] * (1.0 + mods_ref[2:3, :]) + mods_ref[3:4, :]).astype(BF16)

    return pl.pallas_call(
        body, name="modulate_rows", grid=(nx + 1,),
        in_specs=[pl.BlockSpec((ROW_TILE, d), lambda i: (jnp.minimum(i, nx - 1), 0)),
                  pl.BlockSpec((ROW_TILE, d), lambda i: (0, 0)),
                  pl.BlockSpec((8, d), lambda i: (0, 0))],
        out_specs=pl.BlockSpec((ROW_TILE, d), lambda i: (i, 0)),
        out_shape=jax.ShapeDtypeStruct((n + c, d), BF16),
        compiler_params=_params(("parallel",)),
    )(x, ctx, mods)


def _rope_tables(n, c):
    rows = n // GRID_W
    row_ids = jnp.repeat(jnp.arange(rows, dtype=F32), GRID_W)
    col_ids = jnp.tile(jnp.arange(GRID_W, dtype=F32), rows)
    axis_dim = HEAD_DIM // 2
    inv_freq = jnp.power(ROPE_THETA, -jnp.arange(0, axis_dim, 2, dtype=F32) / axis_dim)
    ang_r = row_ids[:, None] * inv_freq
    ang_c = col_ids[:, None] * inv_freq
    ang = jnp.concatenate([ang_r, ang_r, ang_c, ang_c], axis=-1)
    cos, sin = jnp.cos(ang), jnp.sin(ang)
    quarter = (jnp.arange(HEAD_DIM) // (HEAD_DIM // 4)) % 2
    sin_a = jnp.where(quarter == 0, -sin, 0.0)
    sin_b = jnp.where(quarter == 1, sin, 0.0)
    pad = lambda t, v: jnp.concatenate([t, jnp.full((c, HEAD_DIM), v, F32)], axis=0)
    return pad(cos, 1.0), pad(sin_a, 0.0), pad(sin_b, 0.0)


def _rope(x, cos, sin_a, sin_b):
    return x * cos + pltpu.roll(x, 96, 1) * sin_a + pltpu.roll(x, 32, 1) * sin_b


def _rope_t(dy, cos, sin_a, sin_b):
    return dy * cos - pltpu.roll(dy, 96, 1) * sin_a - pltpu.roll(dy, 32, 1) * sin_b


def _rms(x):
    r = lax.rsqrt(jnp.mean(x * x, axis=-1, keepdims=True) + EPS)
    return x * r, r


def _qkv_post(h_all, cos, sin_a, sin_b, q_g, k_g):
    t = h_all.shape[0]
    nt = t // ROW_TILE

    def body(h_ref, cos_ref, sa_ref, sb_ref, qg_ref, kg_ref, qa_ref, ka_ref, va_ref, qb_ref, kb_ref, vb_ref):
        cos_, sa, sb = cos_ref[...], sa_ref[...], sb_ref[...]
        sl = lambda off, hh: h_ref[:, off + hh * HEAD_DIM: off + (hh + 1) * HEAD_DIM]
        for hh in range(HEADS_A):
            qa_ref[hh] = (_rope(sl(QA0, hh), cos_, sa, sb) * ATTN_SCALE).astype(BF16)
        for hh in range(KV_A):
            ka_ref[hh] = _rope(sl(KA0, hh), cos_, sa, sb).astype(BF16)
            va_ref[hh] = sl(VA0, hh).astype(BF16)
        for hh in range(HEADS_B):
            xn, _ = _rms(sl(QB0, hh))
            qb_ref[hh] = (_rope(xn * qg_ref[...], cos_, sa, sb) * ATTN_SCALE).astype(BF16)
        for hh in range(KV_B):
            xn, _ = _rms(sl(KB0, hh))
            kb_ref[hh] = _rope(xn * kg_ref[...], cos_, sa, sb).astype(BF16)
            vb_ref[hh] = sl(VB0, hh).astype(BF16)

    tab = pl.BlockSpec((ROW_TILE, HEAD_DIM), lambda i: (i, 0))
    gain = pl.BlockSpec((1, HEAD_DIM), lambda i: (0, 0))
    hs = lambda nh: pl.BlockSpec((nh, ROW_TILE, HEAD_DIM), lambda i: (0, i, 0))
    sh = lambda nh: jax.ShapeDtypeStruct((nh, t, HEAD_DIM), BF16)
    return pl.pallas_call(
        body, name="qkv_post", grid=(nt,),
        in_specs=[pl.BlockSpec((ROW_TILE, IN_WIDTH), lambda i: (i, 0)), tab, tab, tab, gain, gain],
        out_specs=[hs(HEADS_A), hs(KV_A), hs(KV_A), hs(HEADS_B), hs(KV_B), hs(KV_B)],
        out_shape=[sh(HEADS_A), sh(KV_A), sh(KV_A), sh(HEADS_B), sh(KV_B), sh(KV_B)],
        compiler_params=_params(("parallel",)),
    )(h_all, cos, sin_a, sin_b, q_g, k_g)


def _qkv_bwd_post(h_all, cos, sin_a, sin_b, q_g, k_g, dqa, dka, dva, dqb, dkb, dvb, n):
    t = h_all.shape[0]
    nt = t // ROW_TILE
    nx = n // ROW_TILE

    def body(h_ref, cos_ref, sa_ref, sb_ref, qg_ref, kg_ref,
             dqa_ref, dka_ref, dva_ref, dqb_ref, dkb_ref, dvb_ref, dh_ref, gs_ref):
        i = pl.program_id(0)
        cos_, sa, sb = cos_ref[...], sa_ref[...], sb_ref[...]
        latent = (i < nx).astype(F32)
        sl = lambda off, hh: h_ref[:, off + hh * HEAD_DIM: off + (hh + 1) * HEAD_DIM]

        def put(off, hh, val):
            dh_ref[:, off + hh * HEAD_DIM: off + (hh + 1) * HEAD_DIM] = val.astype(BF16)

        def norm_bwd(x, gain, dy):
            xn, r = _rms(x)
            dxh = dy * gain
            dx = r * (dxh - xn * jnp.mean(dxh * xn, axis=-1, keepdims=True))
            return dx, jnp.sum(dy * xn, axis=0, keepdims=True)

        for hh in range(HEADS_A):
            put(QA0, hh, _rope_t(dqa_ref[hh] * (ATTN_SCALE * latent), cos_, sa, sb))
        for hh in range(KV_A):
            put(KA0, hh, _rope_t(dka_ref[hh], cos_, sa, sb))
            put(VA0, hh, dva_ref[hh])
        gq = jnp.zeros((1, HEAD_DIM), F32)
        gk = jnp.zeros((1, HEAD_DIM), F32)
        for hh in range(HEADS_B):
            dq_t = dqb_ref[hh // GROUP, :, (hh % GROUP) * ROW_TILE:(hh % GROUP + 1) * ROW_TILE]
            dy = _rope_t(dq_t.T * (ATTN_SCALE * latent), cos_, sa, sb)
            dx, g = norm_bwd(sl(QB0, hh), qg_ref[...], dy)
            put(QB0, hh, dx)
            gq = gq + g
        for hh in range(KV_B):
            dy = _rope_t(dkb_ref[hh], cos_, sa, sb)
            dx, g = norm_bwd(sl(KB0, hh), kg_ref[...], dy)
            put(KB0, hh, dx)
            gk = gk + g
            put(VB0, hh, dvb_ref[hh])
        upd = jnp.concatenate([gq, gk, jnp.zeros((6, HEAD_DIM), F32)], axis=0)

        @pl.when(i == 0)
        def _():
            gs_ref[...] = upd

        @pl.when(i != 0)
        def _():
            gs_ref[...] += upd

    tab = pl.BlockSpec((ROW_TILE, HEAD_DIM), lambda i: (i, 0))
    gain = pl.BlockSpec((1, HEAD_DIM), lambda i: (0, 0))
    lat = lambda nh: pl.BlockSpec((nh, ROW_TILE, HEAD_DIM), lambda i: (0, jnp.minimum(i, nx - 1), 0))
    full = lambda nh: pl.BlockSpec((nh, ROW_TILE, HEAD_DIM), lambda i: (0, i, 0))
    return pl.pallas_call(
        body, name="qkv_bwd_post", grid=(nt,),
        in_specs=[pl.BlockSpec((ROW_TILE, IN_WIDTH), lambda i: (i, 0)), tab, tab, tab, gain, gain,
                  lat(HEADS_A), full(KV_A), full(KV_A),
                  pl.BlockSpec((KV_B, None, HEAD_DIM, GROUP * ROW_TILE), lambda i: (0, jnp.minimum(i, nx - 1), 0, 0)),
                  full(KV_B), full(KV_B)],
        out_specs=[pl.BlockSpec((ROW_TILE, IN_WIDTH), lambda i: (i, 0)),
                   pl.BlockSpec((8, HEAD_DIM), lambda i: (0, 0))],
        out_shape=[jax.ShapeDtypeStruct((t, IN_WIDTH), BF16), jax.ShapeDtypeStruct((8, HEAD_DIM), F32)],
        compiler_params=_params(("arbitrary",)),
    )(h_all, cos, sin_a, sin_b, q_g, k_g, dqa, dka, dva, dqb, dkb, dvb)


GB_TQ = 256
GB_TK = 256


def _heads_rows(ref2d, tq):
    return jnp.concatenate([ref2d[:, hh * HEAD_DIM:(hh + 1) * HEAD_DIM] for hh in range(GROUP)], axis=0)


def _attn_b_fwd(qb, kb, vb, n):
    t = kb.shape[1]
    nk = t // GB_TK
    tq = GB_TQ
    nq = n // tq
    qb_step = 2 if nq % 2 == 0 else 1
    rows = qb_step * GROUP * tq

    def body(q_ref, k_ref, v_ref, o_ref, lse_ref, m_s, l_s, acc_s):
        blocks = [(s, hh) for s in range(qb_step) for hh in range(GROUP)]
        q = jnp.concatenate([q_ref[hh, s * tq:(s + 1) * tq, :] for s, hh in blocks], axis=0)
        m_s[...] = jnp.full((1, rows), -jnp.inf, F32)
        l_s[...] = jnp.zeros((1, rows), F32)
        acc_s[...] = jnp.zeros((HEAD_DIM, rows), F32)

        def scores(j):
            start = pl.multiple_of(j * GB_TK, GB_TK)
            return lax.dot_general(k_ref[pl.ds(start, GB_TK), :], q, NT, preferred_element_type=F32)

        def step(j, st):
            st_next = scores(jnp.minimum(j + 1, nk - 1))
            vs = v_ref[pl.ds(pl.multiple_of(j * GB_TK, GB_TK), GB_TK), :]
            m_prev = m_s[...]
            m_new = jnp.maximum(m_prev, jnp.max(st, axis=0, keepdims=True))
            p = jnp.exp(st - m_new)
            alpha = jnp.exp(m_prev - m_new)
            l_s[...] = alpha * l_s[...] + jnp.sum(p, axis=0, keepdims=True)
            acc_s[...] = alpha * acc_s[...] + lax.dot_general(vs, p.astype(BF16), TN, preferred_element_type=F32)
            m_s[...] = m_new
            return st_next

        lax.fori_loop(0, nk, step, scores(0))
        ot = acc_s[...] * (1.0 / l_s[...])
        lse = m_s[...] + jnp.log(l_s[...])
        width = GROUP * tq
        for s in range(qb_step):
            lse_ref[s] = lse[:, s * width:(s + 1) * width]
        for k, (s, hh) in enumerate(blocks):
            o_ref[s * tq:(s + 1) * tq, hh * HEAD_DIM:(hh + 1) * HEAD_DIM] = ot[:, k * tq:(k + 1) * tq].T.astype(BF16)

    return pl.pallas_call(
        body, name="attn_b_fwd", grid=(KV_B, nq // qb_step),
        in_specs=[pl.BlockSpec((GROUP, qb_step * tq, HEAD_DIM), lambda g, i: (g, i, 0)),
                  pl.BlockSpec((None, t, HEAD_DIM), lambda g, i: (g, 0, 0)),
                  pl.BlockSpec((None, t, HEAD_DIM), lambda g, i: (g, 0, 0))],
        out_specs=[pl.BlockSpec((qb_step * tq, GROUP * HEAD_DIM), lambda g, i: (i, KV_A + g)),
                   pl.BlockSpec((None, qb_step, 1, GROUP * tq), lambda g, i: (g, i, 0, 0))],
        out_shape=[jax.ShapeDtypeStruct((n, MIX_WIDTH), BF16),
                   jax.ShapeDtypeStruct((KV_B, nq, 1, GROUP * tq), F32)],
        scratch_shapes=[pltpu.VMEM((1, rows), F32), pltpu.VMEM((1, rows), F32), pltpu.VMEM((HEAD_DIM, rows), F32)],
        compiler_params=_params(("parallel", "parallel")),
    )(qb, kb, vb)


def _attn_b_bwd(qb, kb, vb, dheads, lse, delta, n):
    t = kb.shape[1]
    nk = t // GB_TK
    tq = GB_TQ
    nq = n // tq
    rows = GROUP * tq

    qb_step = 4 if nq % 4 == 0 else 1

    def body(q_ref, k_ref, v_ref, do_ref, lse_ref, dl_ref, dq_ref, dk_ref, dv_ref):
        j = pl.program_id(1)
        i = pl.program_id(2)

        blocks = [(s, hh) for s in range(qb_step) for hh in range(GROUP)]
        q = jnp.concatenate([q_ref[hh, s * tq:(s + 1) * tq, :] for s, hh in blocks], axis=0)
        do = jnp.concatenate([do_ref[s * tq:(s + 1) * tq, hh * HEAD_DIM:(hh + 1) * HEAD_DIM] for s, hh in blocks], axis=0)
        lse_row = jnp.concatenate([lse_ref[s] for s in range(qb_step)], axis=1)
        dl_row = jnp.concatenate([dl_ref[s] for s in range(qb_step)], axis=1)
        ks, vs = k_ref[...], v_ref[...]
        st = lax.dot_general(ks, q, NT, preferred_element_type=F32)
        p = jnp.exp(st - lse_row)
        dpt = lax.dot_general(vs, do, NT, preferred_element_type=F32)
        ds = (p * (dpt - dl_row)).astype(BF16)
        dv_part = lax.dot_general(p.astype(BF16), do, NN, preferred_element_type=F32)
        dk_part = lax.dot_general(ds, q, NN, preferred_element_type=F32)
        dq_part = lax.dot_general(ks, ds, TN, preferred_element_type=F32)

        @pl.when(i == 0)
        def _():
            dk_ref[...] = dk_part
            dv_ref[...] = dv_part

        @pl.when(i != 0)
        def _():
            dk_ref[...] += dk_part
            dv_ref[...] += dv_part

        for s in range(qb_step):
            piece = dq_part[:, s * rows:(s + 1) * rows]

            @pl.when(j == 0)
            def _():
                dq_ref[qb_step * i + s] = piece

            @pl.when(j != 0)
            def _():
                dq_ref[qb_step * i + s] += piece

    kv = pl.BlockSpec((None, GB_TK, HEAD_DIM), lambda g, j, i: (g, j, 0))
    row = pl.BlockSpec((None, qb_step, 1, rows), lambda g, j, i: (g, i, 0, 0))
    return pl.pallas_call(
        body, name="attn_b_bwd", grid=(KV_B, nk, nq // qb_step),
        in_specs=[pl.BlockSpec((GROUP, qb_step * tq, HEAD_DIM), lambda g, j, i: (g, i, 0)), kv, kv,
                  pl.BlockSpec((qb_step * tq, GROUP * HEAD_DIM), lambda g, j, i: (i, KV_A + g)), row, row],
        out_specs=[pl.BlockSpec((None, nq, HEAD_DIM, rows), lambda g, j, i: (g, 0, 0, 0)), kv, kv],
        out_shape=[jax.ShapeDtypeStruct((KV_B, nq, HEAD_DIM, rows), F32),
                   jax.ShapeDtypeStruct((KV_B, t, HEAD_DIM), F32),
                   jax.ShapeDtypeStruct((KV_B, t, HEAD_DIM), F32)],
        compiler_params=_params(("parallel", "arbitrary", "arbitrary")),
    )(qb, kb, vb, dheads, lse, delta)


def _delta_rows(dheads, heads):
    n = heads.shape[0]
    tq = GB_TQ
    w = GROUP * HEAD_DIM

    def body(a_ref, b_ref, o_ref):
        prod = a_ref[...].astype(F32) * b_ref[...].astype(F32)
        cols = [jnp.sum(prod[:, hh * HEAD_DIM:(hh + 1) * HEAD_DIM].T, axis=0, keepdims=True) for hh in range(GROUP)]
        o_ref[...] = jnp.concatenate(cols, axis=1)

    blk = pl.BlockSpec((tq, w), lambda g, i: (i, KV_A + g))
    return pl.pallas_call(
        body, name="delta_rows", grid=(KV_B, n // tq),
        in_specs=[blk, blk],
        out_specs=pl.BlockSpec((None, None, 1, GROUP * tq), lambda g, i: (g, i, 0, 0)),
        out_shape=jax.ShapeDtypeStruct((KV_B, n // tq, 1, GROUP * tq), F32),
        compiler_params=_params(("parallel", "parallel")),
    )(dheads, heads)


KWIN = 3 * BLOCK


def _window_scores(q, k_ref, j, n, nb, sink_row):
    c = k_ref.shape[0] - n
    start = pl.multiple_of(jnp.clip(j - 1, 0, nb - 3) * BLOCK, BLOCK)
    kw = k_ref[pl.ds(start, KWIN), :]
    kc = k_ref[pl.ds(n, c), :]
    s_loc = lax.dot_general(kw, q, NT, preferred_element_type=F32)
    s_ctx = lax.dot_general(kc, q, NT, preferred_element_type=F32)
    cols = GROUP * BLOCK
    qpos = j * BLOCK + lax.broadcasted_iota(jnp.int32, (KWIN, cols), 1) % BLOCK
    kpos = start + lax.broadcasted_iota(jnp.int32, (KWIN, cols), 0)
    s_loc = jnp.where(jnp.abs(qpos - kpos) <= WINDOW, s_loc, -jnp.inf)
    m = jnp.maximum(jnp.maximum(jnp.max(s_loc, axis=0, keepdims=True), jnp.max(s_ctx, axis=0, keepdims=True)),
                    sink_row)
    e_loc, e_ctx, e_sink = jnp.exp(s_loc - m), jnp.exp(s_ctx - m), jnp.exp(sink_row - m)
    inv = 1.0 / (jnp.sum(e_loc, axis=0, keepdims=True) + jnp.sum(e_ctx, axis=0, keepdims=True) + e_sink)
    return e_loc * inv, e_ctx * inv, e_sink * inv, start


def _sink_row(sink_ref, g):
    return jnp.concatenate([sink_ref[pl.ds(g * GROUP + hh, 1), :] for hh in range(GROUP)], axis=1)


def _attn_a_fwd(qa, ka, va, sink_b, heads_b, n):
    t = ka.shape[1]
    nb = n // BLOCK
    assert nb >= 3

    def body(q_ref, k_ref, v_ref, sink_ref, heads_b_ref, o_ref):
        g, j = pl.program_id(0), pl.program_id(1)
        q = q_ref[...].reshape(GROUP * BLOCK, HEAD_DIM)
        p_loc, p_ctx, _, start = _window_scores(q, k_ref, j, n, nb, _sink_row(sink_ref, g))
        vw = v_ref[pl.ds(start, KWIN), :]
        vc = v_ref[pl.ds(n, t - n), :]
        ot = (lax.dot_general(vw, p_loc.astype(BF16), TN, preferred_element_type=F32)
              + lax.dot_general(vc, p_ctx.astype(BF16), TN, preferred_element_type=F32))
        for hh in range(GROUP):
            o_ref[:, hh * HEAD_DIM:(hh + 1) * HEAD_DIM] = ot[:, hh * BLOCK:(hh + 1) * BLOCK].T.astype(BF16)

    return pl.pallas_call(
        body, name="attn_a_fwd", grid=(KV_A, nb),
        in_specs=[pl.BlockSpec((GROUP, BLOCK, HEAD_DIM), lambda g, j: (g, j, 0)),
                  pl.BlockSpec((None, t, HEAD_DIM), lambda g, j: (g, 0, 0)),
                  pl.BlockSpec((None, t, HEAD_DIM), lambda g, j: (g, 0, 0)),
                  pl.BlockSpec((HEADS_A, HEAD_DIM), lambda g, j: (0, 0)),
                  pl.BlockSpec(memory_space=pl.ANY)],
        out_specs=pl.BlockSpec((BLOCK, GROUP * HEAD_DIM), lambda g, j: (j, g)),
        out_shape=jax.ShapeDtypeStruct((n, MIX_WIDTH), BF16),
        input_output_aliases={4: 0},
        compiler_params=_params(("parallel", "parallel")),
    )(qa, ka, va, sink_b, heads_b)


def _attn_a_bwd(qa, ka, va, sink_b, dheads, n):
    t = ka.shape[1]
    c = t - n
    nb = n // BLOCK

    def body(q_ref, k_ref, v_ref, sink_ref, do_ref, dq_ref, dk_ref, dv_ref, dsink_ref):
        g, j = pl.program_id(0), pl.program_id(1)

        @pl.when(j == 0)
        def _():
            dk_ref[...] = jnp.zeros_like(dk_ref)
            dv_ref[...] = jnp.zeros_like(dv_ref)
            dsink_ref[...] = jnp.zeros_like(dsink_ref)

        q = q_ref[...].reshape(GROUP * BLOCK, HEAD_DIM)
        do = _heads_rows(do_ref, BLOCK)
        p_loc, p_ctx, p_sink, start = _window_scores(q, k_ref, j, n, nb, _sink_row(sink_ref, g))
        kw, vw = k_ref[pl.ds(start, KWIN), :], v_ref[pl.ds(start, KWIN), :]
        kc, vc = k_ref[pl.ds(n, c), :], v_ref[pl.ds(n, c), :]
        dp_loc = lax.dot_general(vw, do, NT, preferred_element_type=F32)
        dp_ctx = lax.dot_general(vc, do, NT, preferred_element_type=F32)
        dl = jnp.sum(p_loc * dp_loc, axis=0, keepdims=True) + jnp.sum(p_ctx * dp_ctx, axis=0, keepdims=True)
        ds_loc = (p_loc * (dp_loc - dl)).astype(BF16)
        ds_ctx = (p_ctx * (dp_ctx - dl)).astype(BF16)
        dqt = (lax.dot_general(kw, ds_loc, TN, preferred_element_type=F32)
               + lax.dot_general(kc, ds_ctx, TN, preferred_element_type=F32))
        for hh in range(GROUP):
            dq_ref[hh] = dqt[:, hh * BLOCK:(hh + 1) * BLOCK].T
        dk_ref[pl.ds(start, KWIN), :] += lax.dot_general(ds_loc, q, NN, preferred_element_type=F32)
        dv_ref[pl.ds(start, KWIN), :] += lax.dot_general(p_loc.astype(BF16), do, NN, preferred_element_type=F32)
        dk_ref[pl.ds(n, c), :] += lax.dot_general(ds_ctx, q, NN, preferred_element_type=F32)
        dv_ref[pl.ds(n, c), :] += lax.dot_general(p_ctx.astype(BF16), do, NN, preferred_element_type=F32)
        dsk = -(p_sink * dl)
        upd = [jnp.broadcast_to(jnp.sum(dsk[:, hh * BLOCK:(hh + 1) * BLOCK], axis=1, keepdims=True), (1, HEAD_DIM))
               for hh in range(GROUP)]
        dsink_ref[...] += jnp.concatenate(upd + [jnp.zeros((8 - GROUP, HEAD_DIM), F32)], axis=0)

    res = pl.BlockSpec((None, t, HEAD_DIM), lambda g, j: (g, 0, 0))
    return pl.pallas_call(
        body, name="attn_a_bwd", grid=(KV_A, nb),
        in_specs=[pl.BlockSpec((GROUP, BLOCK, HEAD_DIM), lambda g, j: (g, j, 0)), res, res,
                  pl.BlockSpec((HEADS_A, HEAD_DIM), lambda g, j: (0, 0)),
                  pl.BlockSpec((BLOCK, GROUP * HEAD_DIM), lambda g, j: (j, g))],
        out_specs=[pl.BlockSpec((GROUP, BLOCK, HEAD_DIM), lambda g, j: (g, j, 0)), res, res,
                   pl.BlockSpec((None, 8, HEAD_DIM), lambda g, j: (g, 0, 0))],
        out_shape=[jax.ShapeDtypeStruct((HEADS_A, n, HEAD_DIM), F32),
                   jax.ShapeDtypeStruct((KV_A, t, HEAD_DIM), F32),
                   jax.ShapeDtypeStruct((KV_A, t, HEAD_DIM), F32),
                   jax.ShapeDtypeStruct((KV_A, 8, HEAD_DIM), F32)],
        compiler_params=_params(("parallel", "arbitrary")),
    )(qa, ka, va, sink_b, dheads)


def _ln_stats(r):
    mu = jnp.mean(r, axis=-1, keepdims=True)
    cen = r - mu
    rstd = lax.rsqrt(jnp.mean(cen * cen, axis=-1, keepdims=True) + EPS)
    return cen * rstd, rstd


def _ln_bwd(dy, xhat, rstd, gain):
    dxh = dy * gain
    return rstd * (dxh - jnp.mean(dxh, axis=-1, keepdims=True)
                   - xhat * jnp.mean(dxh * xhat, axis=-1, keepdims=True))


def _accumulate_rows(ref, rows, i):
    pad = [jnp.zeros_like(rows[0])] * (8 - len(rows))
    upd = jnp.concatenate(rows + pad, axis=0)

    @pl.when(i == 0)
    def _():
        ref[...] = upd

    @pl.when(i != 0)
    def _():
        ref[...] += upd


def _colsum(v):
    return jnp.sum(v, axis=0, keepdims=True)


LN_TILE = 256


def _res_ln1(x, a, vec):
    n, d = x.shape

    def body(x_ref, a_ref, v_ref, xh_ref, rs_ref, u_ref):
        r1 = DN_ALPHA * x_ref[...] + v_ref[0:1, :] * a_ref[...]
        xhat, rstd = _ln_stats(r1)
        xh_ref[...] = xhat
        rs_ref[...] = rstd
        x1 = xhat * v_ref[1:2, :] + v_ref[2:3, :]
        u_ref[...] = (x1 * (1.0 + v_ref[3:4, :]) + v_ref[4:5, :]).astype(BF16)

    row = pl.BlockSpec((LN_TILE, d), lambda i: (i, 0))
    return pl.pallas_call(
        body, name="res_ln1", grid=(n // LN_TILE,),
        in_specs=[row, row, pl.BlockSpec((8, d), lambda i: (0, 0))],
        out_specs=[row, pl.BlockSpec((LN_TILE, 1), lambda i: (i, 0)), row],
        out_shape=[jax.ShapeDtypeStruct((n, d), F32), jax.ShapeDtypeStruct((n, 1), F32),
                   jax.ShapeDtypeStruct((n, d), BF16)],
        compiler_params=_params(("parallel",)),
    )(x, a, vec)


def _res_ln2_loss(xhat1, f, target, vec):
    n, d = f.shape

    def body(xh_ref, f_ref, t_ref, v_ref, dr_ref, df_ref, s_ref):
        i = pl.program_id(0)
        x1 = xh_ref[...] * v_ref[1:2, :] + v_ref[2:3, :]
        fv = f_ref[...]
        xhat, rstd = _ln_stats(DN_ALPHA * x1 + v_ref[0:1, :] * fv)
        err = xhat * v_ref[3:4, :] + v_ref[4:5, :] - t_ref[...]
        dy = err * (1.0 / d)
        dr2 = _ln_bwd(dy, xhat, rstd, v_ref[3:4, :])
        dr_ref[...] = dr2
        df_ref[...] = (dr2 * v_ref[0:1, :]).astype(BF16)
        _accumulate_rows(s_ref, [_colsum(dy * xhat), _colsum(dy), _colsum(dr2 * fv),
                                 _colsum(err * err) * (0.5 / d)], i)

    row = pl.BlockSpec((LN_TILE, d), lambda i: (i, 0))
    return pl.pallas_call(
        body, name="res_ln2_loss", grid=(n // LN_TILE,),
        in_specs=[row, row, row, pl.BlockSpec((8, d), lambda i: (0, 0))],
        out_specs=[row, row, pl.BlockSpec((8, d), lambda i: (0, 0))],
        out_shape=[jax.ShapeDtypeStruct((n, d), F32), jax.ShapeDtypeStruct((n, d), BF16),
                   jax.ShapeDtypeStruct((8, d), F32)],
        compiler_params=_params(("arbitrary",)),
    )(xhat1, f, target, vec)


def _ln1_bwd(du2, dr2, xhat1, rstd1, a, vec):
    n, d = du2.shape

    def body(du_ref, dr2_ref, xh_ref, rs_ref, a_ref, v_ref, dxp_ref, da_ref, s_ref):
        i = pl.program_id(0)
        du, xhat = du_ref[...], xh_ref[...]
        x1 = xhat * v_ref[1:2, :] + v_ref[2:3, :]
        dx1 = DN_ALPHA * dr2_ref[...] + du * (1.0 + v_ref[0:1, :])
        dr1 = _ln_bwd(dx1, xhat, rs_ref[...], v_ref[1:2, :])
        dxp_ref[...] = DN_ALPHA * dr1
        da_ref[...] = (dr1 * v_ref[3:4, :]).astype(BF16)
        _accumulate_rows(s_ref, [_colsum(du * x1), _colsum(du), _colsum(dx1 * xhat), _colsum(dx1),
                                 _colsum(dr1 * a_ref[...])], i)

    row = pl.BlockSpec((LN_TILE, d), lambda i: (i, 0))
    return pl.pallas_call(
        body, name="ln1_bwd", grid=(n // LN_TILE,),
        in_specs=[row, row, row, pl.BlockSpec((LN_TILE, 1), lambda i: (i, 0)), row,
                  pl.BlockSpec((8, d), lambda i: (0, 0))],
        out_specs=[row, row, pl.BlockSpec((8, d), lambda i: (0, 0))],
        out_shape=[jax.ShapeDtypeStruct((n, d), F32), jax.ShapeDtypeStruct((n, d), BF16),
                   jax.ShapeDtypeStruct((8, d), F32)],
        compiler_params=_params(("arbitrary",)),
    )(du2, dr2, xhat1, rstd1, a, vec)


def _mod1_bwd(du_all, dxp, x, ctx, mods):
    n, d = x.shape
    nx = n // ROW_TILE

    def body(du_ref, dxp_ref, x_ref, ctx_ref, m_ref, gx_ref, s_ref):
        i = pl.program_id(0)
        du = du_ref[...]
        zero = jnp.zeros((1, d), F32)

        @pl.when(i == 0)
        def _():
            s_ref[...] = jnp.zeros_like(s_ref)

        @pl.when(i < nx)
        def _():
            gx_ref[...] = dxp_ref[...] + du * (1.0 + m_ref[0:1, :])
            s_ref[...] += jnp.concatenate([_colsum(du * x_ref[...]), _colsum(du)] + [zero] * 6, axis=0)

        @pl.when(i >= nx)
        def _():
            s_ref[...] += jnp.concatenate([zero, zero, _colsum(du * ctx_ref[...]), _colsum(du)] + [zero] * 4, axis=0)

    lat = pl.BlockSpec((ROW_TILE, d), lambda i: (jnp.minimum(i, nx - 1), 0))
    return pl.pallas_call(
        body, name="mod1_bwd", grid=(nx + 1,),
        in_specs=[pl.BlockSpec((ROW_TILE, d), lambda i: (i, 0)), lat, lat,
                  pl.BlockSpec((ROW_TILE, d), lambda i: (0, 0)), pl.BlockSpec((8, d), lambda i: (0, 0))],
        out_specs=[lat, pl.BlockSpec((8, d), lambda i: (0, 0))],
        out_shape=[jax.ShapeDtypeStruct((n, d), F32), jax.ShapeDtypeStruct((8, d), F32)],
        compiler_params=_params(("arbitrary",)),
    )(du_all, dxp, x, ctx, mods)


FFN_TM = 1024
FFN_TN = 512


def _gate_up(u2, wg, wu, after):
    n, d = u2.shape
    f = wg.shape[1]

    def body(u_ref, wg_ref, wu_ref, after_ref, g_ref, up_ref, h_ref):
        u = u_ref[...]
        g = lax.dot_general(u, wg_ref[...], NN, preferred_element_type=F32)
        up = lax.dot_general(u, wu_ref[...], NN, preferred_element_type=F32)
        g_ref[...] = g.astype(BF16)
        up_ref[...] = up.astype(BF16)
        h_ref[...] = (g * jax.nn.sigmoid(g) * up).astype(BF16)

    tm = min(FFN_TM, n)
    wspec = pl.BlockSpec((d, FFN_TN), lambda j, i: (0, j))
    ospec = pl.BlockSpec((tm, FFN_TN), lambda j, i: (i, j))
    return pl.pallas_call(
        body, name="gate_up", grid=(f // FFN_TN, n // tm),
        in_specs=[pl.BlockSpec((tm, d), lambda j, i: (i, 0)), wspec, wspec, pl.BlockSpec(memory_space=pl.ANY)],
        out_specs=[ospec, ospec, ospec],
        out_shape=[jax.ShapeDtypeStruct((n, f), BF16)] * 3,
        compiler_params=_params(("parallel", "parallel")),
    )(u2, wg, wu, after)


def _glu_bwd(df, wd, g, u):
    n, d = df.shape
    f = wd.shape[0]

    def body(df_ref, wd_ref, g_ref, u_ref, dg_ref, du_ref):
        dh = lax.dot_general(df_ref[...], wd_ref[...], NT, preferred_element_type=F32)
        gv = g_ref[...].astype(F32)
        sig = jax.nn.sigmoid(gv)
        du_ref[...] = (dh * (gv * sig)).astype(BF16)
        dg_ref[...] = (dh * u_ref[...].astype(F32) * (sig * (1.0 + gv * (1.0 - sig)))).astype(BF16)

    tm = min(FFN_TM, n)
    ospec = pl.BlockSpec((tm, FFN_TN), lambda i, j: (i, j))
    return pl.pallas_call(
        body, name="glu_bwd", grid=(n // tm, f // FFN_TN),
        in_specs=[pl.BlockSpec((tm, d), lambda i, j: (i, 0)),
                  pl.BlockSpec((FFN_TN, d), lambda i, j: (j, 0)), ospec, ospec],
        out_specs=[ospec, ospec],
        out_shape=[jax.ShapeDtypeStruct((n, f), BF16), jax.ShapeDtypeStruct((n, f), BF16)],
        compiler_params=_params(("parallel", "parallel")),
    )(df, wd, g, u)


def _du2(dg, du, wg, wu):
    n, f = dg.shape
    d = wg.shape[0]
    tm, tn, tk = min(1024, n), 1024, 1408
    nk = f // tk

    def body(dg_ref, du_ref, wg_ref, wu_ref, o_ref, acc_ref):
        kk = pl.program_id(2)
        part = (lax.dot_general(dg_ref[...], wg_ref[...], NT, preferred_element_type=F32)
                + lax.dot_general(du_ref[...], wu_ref[...], NT, preferred_element_type=F32))

        @pl.when(kk == 0)
        def _():
            acc_ref[...] = part

        @pl.when(kk != 0)
        def _():
            acc_ref[...] += part

        @pl.when(kk == nk - 1)
        def _():
            o_ref[...] = acc_ref[...]

    aspec = pl.BlockSpec((tm, tk), lambda i, j, kk: (i, kk))
    wspec = pl.BlockSpec((tn, tk), lambda i, j, kk: (j, kk))
    return pl.pallas_call(
        body, name="du2", grid=(n // tm, d // tn, nk),
        in_specs=[aspec, aspec, wspec, wspec],
        out_specs=pl.BlockSpec((tm, tn), lambda i, j, kk: (i, j)),
        out_shape=jax.ShapeDtypeStruct((n, d), F32),
        scratch_shapes=[pltpu.VMEM((tm, tn), F32)],
        compiler_params=_params(("parallel", "parallel", "arbitrary")),
    )(dg, du, wg, wu)


def _rows8(rows, d=D_MODEL):
    rows = [r.reshape(1, d).astype(F32) for r in rows]
    return jnp.concatenate(rows + [jnp.zeros((8 - len(rows), d), F32)], axis=0)


W_GROUPS = (("w_in",), ("w_out", "w_gate", "w_up"), ("w_down",))
G_GROUPS = (("w_down", "w_gate", "w_up"), ("w_out",), ("w_in",))


def _layer_fwd_bwd(x, ctx, target, mod, mod_ctx, weights, prefetch, grads_out,
                   q_g, k_g, sink, ln1_g, ln1_b, ln2_g, ln2_b):
    n, d = x.shape
    c = ctx.shape[0]
    sh1, sc1, g1, sh2, sc2, g2 = [mod[:, k * d:(k + 1) * d] for k in range(6)]
    csh1, csc1 = mod_ctx[:, 0:d], mod_ctx[:, d:2 * d]
    cos, sin_a, sin_b = _rope_tables(n, c)
    sink_b = jnp.broadcast_to(sink.reshape(HEADS_A, 1), (HEADS_A, HEAD_DIM)).astype(F32)

    u_all = _modulate_rows(x, ctx, _rows8([sc1, sh1, csc1, csh1]))
    (w_in,) = weights(0, u_all)
    h_all = _matmul(u_all, w_in, name="qkv_proj", tm=_fit(n + c, 1088), tn=1024, tk=2048, out_dtype=F32)
    qa, ka, va, qb, kb, vb = _qkv_post(h_all, cos, sin_a, sin_b, q_g, k_g)
    heads_b, lse = _attn_b_fwd(qb, kb, vb, n)
    zero = prefetch(1, heads_b)
    heads = _attn_a_fwd(qa, ka, va, sink_b + zero, heads_b, n)
    w_out, w_gate, w_up = weights(1, heads)
    a = _matmul(heads, w_out, name="out_proj", tm=1024, tn=1024, tk=2048, out_dtype=F32)
    xhat1, rstd1, u2 = _res_ln1(x, a, _rows8([g1, ln1_g, ln1_b, sc2, sh2]))
    zero = prefetch(2, u2)
    gg, uu, hh = _gate_up(u2, w_gate, w_up, zero.reshape(1, 1))
    (w_down,) = weights(2, hh)
    f = _matmul(hh, w_down, name="ffn_down", tm=1024, tn=512, tk=FFN, out_dtype=F32)
    dr2, df, s_ln2 = _res_ln2_loss(xhat1, f, target, _rows8([g2, ln1_g, ln1_b, ln2_g, ln2_b]))

    dgg, duu = _glu_bwd(df, w_down, gg, uu)
    dw_down = _matmul(hh, df, name="dw_down", ta=True, tm=512, tn=1024, tk=n, out_dtype=BF16)
    dw_gate = _matmul(u2, dgg, name="dw_gate", ta=True, tm=1024, tn=512, tk=n, out_dtype=BF16)
    dw_up = _matmul(u2, duu, name="dw_up", ta=True, tm=1024, tn=512, tk=n, out_dtype=BF16)
    zero = grads_out(0, [dw_down, dw_gate, dw_up])
    du2 = _du2(dgg, duu, w_gate, w_up)
    dxp, da, s_ln1 = _ln1_bwd(du2, dr2, xhat1, rstd1, a, _rows8([sc2, ln1_g, ln1_b, g1]) + zero)

    dheads = _matmul(da, w_out, name="d_heads", tb=True, tm=1024, tn=1024, tk=2048, out_dtype=BF16)
    dw_out = _matmul(heads, da, name="dw_out", ta=True, tm=1024, tn=1024, tk=n, out_dtype=BF16)
    zero = grads_out(1, [dw_out])
    delta = _delta_rows(dheads, heads)
    dqa, dka, dva, dsink = _attn_a_bwd(qa, ka, va, sink_b + zero, dheads, n)
    dqb, dkb, dvb = _attn_b_bwd(qb, kb, vb, dheads, lse, delta, n)
    dh_all, s_gain = _qkv_bwd_post(h_all, cos, sin_a, sin_b, q_g, k_g, dqa, dka, dva, dqb, dkb, dvb, n)
    dw_in = _matmul(u_all, dh_all, name="dw_in", ta=True, tm=1024, tn=1024, tk=n + c, out_dtype=BF16)
    zero = grads_out(2, [dw_in])
    du_all = _matmul(dh_all, w_in, name="d_u1", tb=True, tm=_fit(n + c, 1088), tn=1024, tk=IN_WIDTH, out_dtype=F32,
                     after=zero.reshape(1, 1))
    grad_x, s_mod1 = _mod1_bwd(du_all, dxp, x, ctx, _rows8([sc1]) + zero)

    dsink_row = jnp.concatenate([dsink[0, 0:GROUP, 0], dsink[1, 0:GROUP, 0]]).reshape(1, HEADS_A)
    misc = jnp.concatenate([s_gain[0:1], s_gain[1:2], dsink_row,
                            jnp.zeros((1, d - 2 * HEAD_DIM - HEADS_A), F32)], axis=1)
    partial = jnp.concatenate([
        s_mod1[1:2], s_mod1[0:1], s_ln1[4:5],
        s_ln1[1:2], s_ln1[0:1], s_ln2[2:3],
        s_mod1[3:4], s_mod1[2:3],
        s_ln1[2:3], s_ln1[3:4], s_ln2[0:1], s_ln2[1:2],
        s_ln2[3:4], misc, jnp.zeros((2, d), F32)], axis=0)
    return grad_x, partial


ANY = pl.BlockSpec(memory_space=pl.ANY)
VMEM_FULL = pl.BlockSpec(memory_space=pltpu.VMEM)
N_CHIP_PEERS = 3


def _me():
    return lax.axis_index("x"), lax.axis_index("y"), lax.axis_index("c")


def _other_chips(x, y):
    return [(1 - x, y), (x, 1 - y), (1 - x, 1 - y)]


def _shard_of(chip):
    return 2 * chip[0] + chip[1]


def _dev_index(x, y, c):
    return 4 * x + 2 * y + c


def _rcopy(src, dst, send_sems, recv_sems, k, dev):
    return pltpu.make_async_remote_copy(src_ref=src, dst_ref=dst, send_sem=send_sems.at[k], recv_sem=recv_sems.at[k],
                                        device_id=dev, device_id_type=MESH)


BIG = (("w_in", (D_MODEL, IN_WIDTH), 1), ("w_out", (MIX_WIDTH, D_MODEL), 0), ("w_gate", (D_MODEL, FFN), 1),
       ("w_up", (D_MODEL, FFN), 1), ("w_down", (FFN, D_MODEL), 0))


def _sub(ref, axis, idx, size):
    start = pl.multiple_of(idx * size, size)
    return ref.at[pl.ds(start, size), :] if axis == 0 else ref.at[:, pl.ds(start, size)]


def _shape_div(shape, axis, parts):
    return tuple(s // parts if a == axis else s for a, s in enumerate(shape))


def _piece(a, ref, shard, half):
    _, full, axis = BIG[a]
    view = _sub(ref, axis, shard, full[axis] // N_SHARD)
    return _sub(view, 1 - axis, half, full[1 - axis] // 2)


HBM = pl.BlockSpec(memory_space=pltpu.HBM)
SEM = pl.BlockSpec(memory_space=pltpu.SEMAPHORE)
EFFECT = pltpu.SideEffectType.DATAFLOW_SIDE_EFFECTING
BIG_INDEX = {name: a for a, (name, _, _) in enumerate(BIG)}


def _in_hbm(arr):
    return pltpu.with_memory_space_constraint(arr, pltpu.HBM)


def _gather_start(tag, arrs, bufs, prev):
    n_arr = len(arrs)

    def body(*refs):
        ins = refs[:n_arr]
        send_sems, recv_sems = refs[n_arr + 1], refs[n_arr + 2]
        token = refs[-1]
        x, y, c = _me()
        s_me = _shard_of((x, y))
        for i, a in enumerate(arrs):
            mine = _piece(a, ins[i], s_me, c)
            for j, chip in enumerate(_other_chips(x, y)):
                _rcopy(mine, mine, send_sems, recv_sems, N_CHIP_PEERS * i + j, (*chip, c)).start()
        token[...] = jnp.zeros_like(token)

    n_sem = N_CHIP_PEERS * n_arr
    outs = pl.pallas_call(
        body, name="gather_start_" + tag,
        in_specs=[HBM] * n_arr + [ANY],
        out_specs=[SEM, SEM] + [HBM] * n_arr + [VMEM_FULL],
        out_shape=[pltpu.SemaphoreType.DMA((n_sem,)), pltpu.SemaphoreType.DMA((n_sem,))]
        + [pltpu.HBM(BIG[a][1], BF16) for a in arrs] + [jax.ShapeDtypeStruct((8, HEAD_DIM), F32)],
        input_output_aliases={i: 2 + i for i in range(n_arr)},
        compiler_params=pltpu.CompilerParams(has_side_effects=EFFECT),
    )(*[_in_hbm(b) for b in bufs], prev)
    return outs[0], outs[1], list(outs[2:2 + n_arr]), outs[-1]


def _gather_wait(tag, arrs, send_sems, recv_sems, bufs, after):
    n_arr = len(arrs)

    def body(*refs):
        ins = refs[:n_arr]
        send_sems_, recv_sems_ = refs[n_arr], refs[n_arr + 1]
        x, y, c = _me()
        s_me = _shard_of((x, y))
        for i, a in enumerate(arrs):
            mine = _piece(a, ins[i], s_me, c)
            for j, chip in enumerate(_other_chips(x, y)):
                landed = _piece(a, ins[i], _shard_of(chip), c)
                cp = _rcopy(mine, landed, send_sems_, recv_sems_, N_CHIP_PEERS * i + j, (*chip, c))
                cp.wait_send()
                cp.wait_recv()

    outs = pl.pallas_call(
        body, name="gather_wait_" + tag,
        in_specs=[HBM] * n_arr + [SEM, SEM, ANY],
        out_specs=[HBM] * n_arr,
        out_shape=[pltpu.HBM(BIG[a][1], BF16) for a in arrs],
        input_output_aliases={i: i for i in range(n_arr)},
        compiler_params=pltpu.CompilerParams(has_side_effects=EFFECT),
    )(*bufs, send_sems, recv_sems, after)
    return list(outs)


def _gather_forward(tag, arrs, bufs):
    n_arr = len(arrs)

    def body(*refs):
        outs = refs[n_arr:2 * n_arr]
        send_sems, recv_sems = refs[2 * n_arr:]
        x, y, c = _me()
        sibling = (x, y, 1 - c)
        chips = _other_chips(x, y)
        copies = []
        for i, a in enumerate(arrs):
            for j, chip in enumerate(chips):
                landed = _piece(a, outs[i], _shard_of(chip), c)
                cp = _rcopy(landed, landed, send_sems, recv_sems, N_CHIP_PEERS * i + j, sibling)
                cp.start()
                copies.append(cp)
        for i, a in enumerate(arrs):
            for j, chip in enumerate(chips):
                other = _piece(a, outs[i], _shard_of(chip), 1 - c)
                _rcopy(other, other, send_sems, recv_sems, N_CHIP_PEERS * i + j, sibling).wait_recv()
        for cp in copies:
            cp.wait_send()

    n_sem = N_CHIP_PEERS * n_arr
    return list(pl.pallas_call(
        body, name="gather_forward_" + tag,
        in_specs=[ANY] * n_arr, out_specs=[ANY] * n_arr,
        out_shape=[jax.ShapeDtypeStruct(BIG[a][1], BF16) for a in arrs],
        input_output_aliases={i: i for i in range(n_arr)},
        scratch_shapes=[pltpu.SemaphoreType.DMA((n_sem,)), pltpu.SemaphoreType.DMA((n_sem,))],
    )(*bufs))


def _forward_start(tag, arrs, bufs):
    n_arr = len(arrs)

    def body(*refs):
        ins = refs[:n_arr]
        send_sems, recv_sems = refs[n_arr], refs[n_arr + 1]
        token = refs[-1]
        x, y, c = _me()
        for i, a in enumerate(arrs):
            for j, chip in enumerate(_other_chips(x, y)):
                landed = _piece(a, ins[i], _shard_of(chip), c)
                _rcopy(landed, landed, send_sems, recv_sems, N_CHIP_PEERS * i + j, (x, y, 1 - c)).start()
        token[...] = jnp.zeros_like(token)

    n_sem = N_CHIP_PEERS * n_arr
    outs = pl.pallas_call(
        body, name="gather_forward_start_" + tag,
        in_specs=[HBM] * n_arr,
        out_specs=[SEM, SEM] + [HBM] * n_arr + [VMEM_FULL],
        out_shape=[pltpu.SemaphoreType.DMA((n_sem,)), pltpu.SemaphoreType.DMA((n_sem,))]
        + [pltpu.HBM(BIG[a][1], BF16) for a in arrs] + [jax.ShapeDtypeStruct((8, HEAD_DIM), F32)],
        input_output_aliases={i: 2 + i for i in range(n_arr)},
        compiler_params=pltpu.CompilerParams(has_side_effects=EFFECT),
    )(*bufs)
    return outs[0], outs[1], list(outs[2:2 + n_arr]), outs[-1]


def _forward_wait(tag, arrs, send_sems, recv_sems, bufs, after):
    n_arr = len(arrs)

    def body(*refs):
        ins = refs[:n_arr]
        send_sems_, recv_sems_ = refs[n_arr], refs[n_arr + 1]
        x, y, c = _me()
        for i, a in enumerate(arrs):
            for j, chip in enumerate(_other_chips(x, y)):
                mine = _piece(a, ins[i], _shard_of(chip), c)
                other = _piece(a, ins[i], _shard_of(chip), 1 - c)
                cp = _rcopy(mine, other, send_sems_, recv_sems_, N_CHIP_PEERS * i + j, (x, y, 1 - c))
                cp.wait_send()
                cp.wait_recv()

    outs = pl.pallas_call(
        body, name="gather_forward_wait_" + tag,
        in_specs=[HBM] * n_arr + [SEM, SEM, ANY],
        out_specs=[HBM] * n_arr,
        out_shape=[pltpu.HBM(BIG[a][1], BF16) for a in arrs],
        input_output_aliases={i: i for i in range(n_arr)},
        compiler_params=pltpu.CompilerParams(has_side_effects=EFFECT),
    )(*bufs, send_sems, recv_sems, after)
    return list(outs)


def _peers(x, y, c):
    return [(x ^ (mask >> 2), y ^ ((mask >> 1) & 1), c ^ (mask & 1)) for mask in range(1, N_DEV)]


def _received_shape(a):
    _, full, axis = BIG[a]
    return (N_DEV - 1,) + _shape_div(_shape_div(full, 1 - axis, 2), axis, N_SHARD)


def _pieces_start(tag, arrs, dws):
    n_arr = len(arrs)

    def body(*refs):
        srcs, lands = refs[:n_arr], refs[n_arr:2 * n_arr]
        send_sems, recv_sems = refs[2 * n_arr], refs[2 * n_arr + 1]
        token = refs[-1]
        x, y, c = _me()
        for i, a in enumerate(arrs):
            for k, peer in enumerate(_peers(x, y, c)):
                src = _piece(a, srcs[i], _shard_of(peer[:2]), peer[2])
                _rcopy(src, lands[i].at[k], send_sems, recv_sems, (N_DEV - 1) * i + k, peer).start()
        token[...] = jnp.zeros_like(token)

    n_sem = (N_DEV - 1) * n_arr
    lands = [_in_hbm(lax.empty(_received_shape(a), BF16)) for a in arrs]
    outs = pl.pallas_call(
        body, name="grad_pieces_start_" + tag,
        in_specs=[HBM] * (2 * n_arr),
        out_specs=[SEM, SEM] + [HBM] * (2 * n_arr) + [VMEM_FULL],
        out_shape=[pltpu.SemaphoreType.DMA((n_sem,)), pltpu.SemaphoreType.DMA((n_sem,))]
        + [pltpu.HBM(BIG[a][1], BF16) for a in arrs] + [pltpu.HBM(_received_shape(a), BF16) for a in arrs]
        + [jax.ShapeDtypeStruct((8, HEAD_DIM), F32)],
        input_output_aliases={i: 2 + i for i in range(2 * n_arr)},
        compiler_params=pltpu.CompilerParams(has_side_effects=EFFECT),
    )(*[_in_hbm(dw) for dw in dws], *lands)
    return outs[0], outs[1], list(outs[2:2 + n_arr]), list(outs[2 + n_arr:2 + 2 * n_arr]), outs[-1]


def _pieces_wait(tag, arrs, send_sems, recv_sems, dws, lands, after):
    n_arr = len(arrs)

    def body(*refs):
        srcs, lands_ = refs[:n_arr], refs[n_arr:2 * n_arr]
        send_sems_, recv_sems_ = refs[2 * n_arr], refs[2 * n_arr + 1]
        x, y, c = _me()
        for i, a in enumerate(arrs):
            for k, peer in enumerate(_peers(x, y, c)):
                src = _piece(a, srcs[i], _shard_of(peer[:2]), peer[2])
                cp = _rcopy(src, lands_[i].at[k], send_sems_, recv_sems_, (N_DEV - 1) * i + k, peer)
                cp.wait_send()
                cp.wait_recv()

    outs = pl.pallas_call(
        body, name="grad_pieces_wait_" + tag,
        in_specs=[HBM] * (2 * n_arr) + [SEM, SEM, ANY],
        out_specs=[HBM] * (2 * n_arr),
        out_shape=[pltpu.HBM(BIG[a][1], BF16) for a in arrs] + [pltpu.HBM(_received_shape(a), BF16) for a in arrs],
        input_output_aliases={i: i for i in range(2 * n_arr)},
        compiler_params=pltpu.CompilerParams(has_side_effects=EFFECT),
    )(*dws, *lands, send_sems, recv_sems, after)
    return list(outs[:n_arr]), list(outs[n_arr:])


def _join_start(tag, g_halves):
    n_arr = len(g_halves)

    def body(*refs):
        srcs, lands = refs[:n_arr], refs[n_arr:2 * n_arr]
        send_sems, recv_sems = refs[2 * n_arr], refs[2 * n_arr + 1]
        token = refs[-1]
        x, y, c = _me()
        for i in range(n_arr):
            _rcopy(srcs[i], lands[i], send_sems, recv_sems, i, (x, y, 1 - c)).start()
        token[...] = jnp.zeros_like(token)

    shapes = [pltpu.HBM(g.shape, F32) for g in g_halves]
    outs = pl.pallas_call(
        body, name="grad_join_start_" + tag,
        in_specs=[HBM] * (2 * n_arr),
        out_specs=[SEM, SEM] + [HBM] * (2 * n_arr) + [VMEM_FULL],
        out_shape=[pltpu.SemaphoreType.DMA((n_arr,)), pltpu.SemaphoreType.DMA((n_arr,))] + shapes + shapes
        + [jax.ShapeDtypeStruct((8, HEAD_DIM), F32)],
        input_output_aliases={i: 2 + i for i in range(2 * n_arr)},
        compiler_params=pltpu.CompilerParams(has_side_effects=EFFECT),
    )(*[_in_hbm(g) for g in g_halves], *[_in_hbm(lax.empty(g.shape, F32)) for g in g_halves])
    return outs[0], outs[1], list(outs[2:2 + n_arr]), list(outs[2 + n_arr:2 + 2 * n_arr]), outs[-1]


def _join_wait(tag, send_sems, recv_sems, g_halves, lands, after):
    n_arr = len(g_halves)

    def body(*refs):
        srcs, lands_ = refs[:n_arr], refs[n_arr:2 * n_arr]
        send_sems_, recv_sems_ = refs[2 * n_arr], refs[2 * n_arr + 1]
        x, y, c = _me()
        for i in range(n_arr):
            cp = _rcopy(srcs[i], lands_[i], send_sems_, recv_sems_, i, (x, y, 1 - c))
            cp.wait_send()
            cp.wait_recv()

    shapes = [pltpu.HBM(g.shape, F32) for g in g_halves]
    outs = pl.pallas_call(
        body, name="grad_join_wait_" + tag,
        in_specs=[HBM] * (2 * n_arr) + [SEM, SEM, ANY],
        out_specs=[HBM] * (2 * n_arr),
        out_shape=shapes + shapes,
        input_output_aliases={i: i for i in range(2 * n_arr)},
        compiler_params=pltpu.CompilerParams(has_side_effects=EFFECT),
    )(*g_halves, *lands, send_sems, recv_sems, after)
    return list(outs[:n_arr]), list(outs[n_arr:])


def _piece_sum(a, dw, shard, core, received):
    name, full, axis = BIG[a]
    rows, cols = _received_shape(a)[1:]
    tr = _fit(rows, ROW_TILE)
    nbr = rows // tr

    def body(w_ref, dw_ref, rec_ref, o_ref):
        acc = dw_ref[...].astype(F32)
        for k in range(N_DEV - 1):
            acc = acc + rec_ref[k].astype(F32)
        o_ref[...] = acc

    if axis == 0:
        own = pl.BlockSpec((tr, cols), lambda i, w: (w[0] * nbr + i, w[1]))
    else:
        own = pl.BlockSpec((tr, cols), lambda i, w: (w[1] * nbr + i, w[0]))
    return pl.pallas_call(
        body, name="grad_sum_pieces_" + name,
        grid_spec=pltpu.PrefetchScalarGridSpec(
            num_scalar_prefetch=1, grid=(nbr,),
            in_specs=[own, pl.BlockSpec((N_DEV - 1, tr, cols), lambda i, w: (0, i, 0))],
            out_specs=pl.BlockSpec((tr, cols), lambda i, w: (i, 0))),
        out_shape=jax.ShapeDtypeStruct((rows, cols), F32),
        compiler_params=_params(("parallel",)),
    )(jnp.stack([shard, core]).astype(jnp.int32), dw, received)


def _scatter_begin(tag, arrs, dws):
    send_sems, recv_sems, dws, lands, token = _pieces_start(tag, arrs, dws)
    return (send_sems, recv_sems, dws, lands), token[0, 0]


def _scatter_reduce(tag, arrs, state, after):
    x, y, c = _me()
    send_sems, recv_sems, dws, lands = state
    dws, lands = _pieces_wait(tag, arrs, send_sems, recv_sems, dws, lands, after)
    g_own = [_piece_sum(a, dw, _shard_of((x, y)), c, r) for a, dw, r in zip(arrs, dws, lands)]
    send_sems, recv_sems, g_own, lands, token = _join_start(tag, g_own)
    return (send_sems, recv_sems, g_own, lands), token


def _scatter_end(tag, state, after):
    return _join_wait(tag, *state, after)


def _rows_start(tag, block):
    r, d = block.shape

    def body(src, land, send_sems, recv_sems, src_thru, land_thru, token):
        x, y, c = _me()
        for k, peer in enumerate(_peers(x, y, c)):
            _rcopy(src, land.at[_dev_index(x, y, c)], send_sems, recv_sems, k, peer).start()
        token[...] = jnp.zeros_like(token)

    outs = pl.pallas_call(
        body, name="rows_start_" + tag,
        in_specs=[HBM, HBM],
        out_specs=[SEM, SEM, HBM, HBM, VMEM_FULL],
        out_shape=[pltpu.SemaphoreType.DMA((N_DEV - 1,)), pltpu.SemaphoreType.DMA((N_DEV - 1,)),
                   pltpu.HBM((r, d), F32), pltpu.HBM((N_DEV, r, d), F32), jax.ShapeDtypeStruct((8, HEAD_DIM), F32)],
        input_output_aliases={0: 2, 1: 3},
        compiler_params=pltpu.CompilerParams(has_side_effects=EFFECT),
    )(_in_hbm(block), _in_hbm(lax.empty((N_DEV, r, d), F32)))
    return outs[0], outs[1], outs[2], outs[3], outs[4]


def _rows_wait(tag, send_sems, recv_sems, block, land, after):
    r, d = block.shape

    def body(src, land_, send_sems_, recv_sems_, after_ref, src_thru, land_thru):
        x, y, c = _me()
        for k, peer in enumerate(_peers(x, y, c)):
            cp = _rcopy(src, land_.at[_dev_index(*peer)], send_sems_, recv_sems_, k, peer)
            cp.wait_send()
            cp.wait_recv()

    outs = pl.pallas_call(
        body, name="rows_wait_" + tag,
        in_specs=[HBM, HBM, SEM, SEM, ANY],
        out_specs=[HBM, HBM],
        out_shape=[pltpu.HBM((r, d), F32), pltpu.HBM((N_DEV, r, d), F32)],
        input_output_aliases={0: 0, 1: 1},
        compiler_params=pltpu.CompilerParams(has_side_effects=EFFECT),
    )(block, land, send_sems, recv_sems, after)
    me = _dev_index(*_me())
    return lax.dynamic_update_slice(outs[1], outs[0][None], (me, 0, 0))


ADA_ROWS = 80
ADA_W = 6 * D_MODEL // N_SHARD


def _ada_forward(c_block, cctx_block, w_ada, b_shard):
    d = c_block.shape[1]

    def body(c_ref, cc_ref, w_ref, b_ref, act_ref, mods_ref, raw, mloc, send_sems, recv_sems):
        x, y, c = _me()
        me = _dev_index(x, y, c)
        s_me = _shard_of((x, y))
        raw[72:ADA_ROWS, :] = jnp.zeros((ADA_ROWS - 72, d), F32)
        raw[pl.ds(pl.multiple_of(me * 8, 8), 8), :] = c_ref[...]
        raw[64:72, :] = cc_ref[...]
        sends = []
        for mask in range(1, N_DEV):
            peer = (x ^ (mask >> 2), y ^ ((mask >> 1) & 1), c ^ (mask & 1))
            cp = _rcopy(c_ref, raw.at[pl.ds(pl.multiple_of(me * 8, 8), 8), :], send_sems, recv_sems, mask - 1, peer)
            cp.start()
            sends.append(cp)
        for mask in range(1, N_DEV):
            peer = (x ^ (mask >> 2), y ^ ((mask >> 1) & 1), c ^ (mask & 1))
            landed = raw.at[pl.ds(pl.multiple_of(_dev_index(*peer) * 8, 8), 8), :]
            _rcopy(landed, landed, send_sems, recv_sems, mask - 1, peer).wait_recv()
        v = raw[...]
        act = v * jax.nn.sigmoid(v)
        act_ref[...] = act
        mloc[...] = lax.dot_general(act.astype(BF16), w_ref[...].astype(BF16), NN,
                                    preferred_element_type=F32) + b_ref[...]
        mods_ref[s_me, 0:8, :] = mloc[pl.ds(pl.multiple_of(me * 8, 8), 8), :]
        mods_ref[s_me, 8:16, :] = mloc[64:72, :]
        base = N_DEV - 1
        for j, chip in enumerate(_other_chips(x, y)):
            peer = (*chip, c)
            rows = mloc.at[pl.ds(pl.multiple_of(_dev_index(*peer) * 8, 8), 8), :]
            cp = _rcopy(rows, mods_ref.at[s_me, 0:8, :], send_sems, recv_sems, base + 2 * j, peer)
            cp.start()
            sends.append(cp)
            cp = _rcopy(mloc.at[64:72, :], mods_ref.at[s_me, 8:16, :], send_sems, recv_sems, base + 2 * j + 1, peer)
            cp.start()
            sends.append(cp)
        for j, chip in enumerate(_other_chips(x, y)):
            for part in range(2):
                landed = mods_ref.at[_shard_of(chip), 8 * part:8 * part + 8, :]
                _rcopy(landed, landed, send_sems, recv_sems, base + 2 * j + part, (*chip, c)).wait_recv()
        for cp in sends:
            cp.wait_send()

    n_sem = N_DEV - 1 + 2 * N_CHIP_PEERS
    return pl.pallas_call(
        body, name="ada_forward",
        in_specs=[VMEM_FULL] * 4, out_specs=[VMEM_FULL, VMEM_FULL],
        out_shape=[jax.ShapeDtypeStruct((ADA_ROWS, d), F32), jax.ShapeDtypeStruct((N_SHARD, 16, ADA_W), F32)],
        scratch_shapes=[pltpu.VMEM((ADA_ROWS, d), F32), pltpu.VMEM((ADA_ROWS, ADA_W), F32),
                        pltpu.SemaphoreType.DMA((n_sem,)), pltpu.SemaphoreType.DMA((n_sem,))],
        compiler_params=pltpu.CompilerParams(vmem_limit_bytes=VMEM_LIMIT),
    )(c_block, cctx_block, w_ada, b_shard)


def _small_reduce(gathered):
    d = gathered.shape[2]

    def body(g_ref, o_ref):
        tot = g_ref[0]
        for i in range(1, N_DEV):
            tot = tot + g_ref[i]
        o_ref[...] = tot
        o_ref[0:2, :] = tot[0:2] + tot[6:8]
        o_ref[12:13, :] = jnp.broadcast_to(jnp.sum(tot[12:13], axis=1, keepdims=True), (1, d))

    return pl.pallas_call(body, name="small_reduce", in_specs=[VMEM_FULL], out_specs=VMEM_FULL,
                          out_shape=jax.ShapeDtypeStruct((16, d), F32))(gathered)


def _cctx_grad(gathered, c_ctx):
    d = gathered.shape[2]

    def body(g_ref, c_ref, o_ref):
        tot = g_ref[0, 0:1, :]
        for chip in range(1, N_SHARD):
            tot = tot + g_ref[2 * chip, 0:1, :]
        v = c_ref[...]
        sig = jax.nn.sigmoid(v)
        o_ref[...] = tot * (sig * (1.0 + v * (1.0 - sig)))

    return pl.pallas_call(body, name="cctx_grad", in_specs=[VMEM_FULL, VMEM_FULL], out_specs=VMEM_FULL,
                          out_shape=jax.ShapeDtypeStruct((1, d), F32))(gathered, c_ctx.reshape(1, d))


def _cast_into_full(w, shard, full, axis, name):
    r, cdim = w.shape
    tr = _fit(r, ROW_TILE)
    nbr = r // tr

    def body(s_ref, w_ref, o_ref):
        o_ref[...] = w_ref[...].astype(BF16)

    if axis == 0:
        out_spec = pl.BlockSpec((tr, cdim), lambda i, s: (s[0] * nbr + i, 0))
    else:
        out_spec = pl.BlockSpec((tr, cdim), lambda i, s: (i, s[0]))
    return pl.pallas_call(
        body, name=name,
        grid_spec=pltpu.PrefetchScalarGridSpec(
            num_scalar_prefetch=1, grid=(nbr,), in_specs=[pl.BlockSpec((tr, cdim), lambda i, s: (i, 0))],
            out_specs=out_spec),
        out_shape=jax.ShapeDtypeStruct(full, BF16), compiler_params=_params(("parallel",)),
    )(shard.reshape(1).astype(jnp.int32), w)


def _adamw_halves(w, g_own, g_other, m, v, core, axis, name):
    r, cdim = w.shape
    hr, hc = (r // 2, cdim) if axis == 1 else (r, cdim // 2)
    assert g_own.shape == (hr, hc) and g_other.shape == (hr, hc)
    tr = _fit(hr, 256)
    nb = hr // tr
    c1 = 1.0 - ADAM_B1 ** ADAM_STEP
    c2 = 1.0 - ADAM_B2 ** ADAM_STEP

    def body(c_ref, w_ref, go_ref, gt_ref, m_ref, v_ref, g_ref, d_ref, nm_ref, nv_ref):
        gv = jnp.where(pl.program_id(0) == c_ref[0], go_ref[...], gt_ref[...])
        nm = ADAM_B1 * m_ref[...] + (1.0 - ADAM_B1) * gv
        nv = ADAM_B2 * v_ref[...] + (1.0 - ADAM_B2) * (gv * gv)
        g_ref[...] = gv
        nm_ref[...] = nm
        nv_ref[...] = nv
        d_ref[...] = -ADAM_LR * ((nm / c1) / (jnp.sqrt(nv / c2) + ADAM_EPS) + ADAM_WD * w_ref[...])

    if axis == 1:
        big = pl.BlockSpec((tr, hc), lambda p, i, c: (p * nb + i, 0))
    else:
        big = pl.BlockSpec((tr, hc), lambda p, i, c: (i, p))
    own = pl.BlockSpec((tr, hc), lambda p, i, c: (jnp.where(p == c[0], i, 0), 0))
    other = pl.BlockSpec((tr, hc), lambda p, i, c: (jnp.where(p == c[0], 0, i), 0))
    sh = jax.ShapeDtypeStruct((r, cdim), F32)
    return pl.pallas_call(
        body, name=name,
        grid_spec=pltpu.PrefetchScalarGridSpec(
            num_scalar_prefetch=1, grid=(2, nb), in_specs=[big, own, other, big, big], out_specs=[big] * 4),
        out_shape=[sh] * 4, compiler_params=_params(("parallel", "parallel")),
    )(core.reshape(1).astype(jnp.int32), w, g_own, g_other, m, v)


def _adamw(w, g, m, v, name):
    r, cdim = w.shape
    tr = _fit(r, 128) if r % (ROW_TILE // 4) == 0 else r
    c1 = 1.0 - ADAM_B1 ** ADAM_STEP
    c2 = 1.0 - ADAM_B2 ** ADAM_STEP

    def body(w_ref, g_ref, m_ref, v_ref, d_ref, nm_ref, nv_ref):
        gv = g_ref[...]
        nm = ADAM_B1 * m_ref[...] + (1.0 - ADAM_B1) * gv
        nv = ADAM_B2 * v_ref[...] + (1.0 - ADAM_B2) * (gv * gv)
        nm_ref[...] = nm
        nv_ref[...] = nv
        d_ref[...] = -ADAM_LR * ((nm / c1) / (jnp.sqrt(nv / c2) + ADAM_EPS) + ADAM_WD * w_ref[...])

    spec = pl.BlockSpec((tr, cdim), lambda i: (i, 0))
    sh = jax.ShapeDtypeStruct((r, cdim), F32)
    return pl.pallas_call(body, name=name, grid=(r // tr,), in_specs=[spec] * 4, out_specs=[spec] * 3,
                          out_shape=[sh, sh, sh], compiler_params=_params(("parallel",)))(w, g, m, v)


def _adamw_small(ws, gs, ms, vs):
    k = len(ws)
    c1 = 1.0 - ADAM_B1 ** ADAM_STEP
    c2 = 1.0 - ADAM_B2 ** ADAM_STEP

    def body(*refs):
        w_refs, g_refs, m_refs, v_refs = refs[0:k], refs[k:2 * k], refs[2 * k:3 * k], refs[3 * k:4 * k]
        d_refs, nm_refs, nv_refs = refs[4 * k:5 * k], refs[5 * k:6 * k], refs[6 * k:7 * k]
        for i in range(k):
            gv = g_refs[i][...]
            nm = ADAM_B1 * m_refs[i][...] + (1.0 - ADAM_B1) * gv
            nv = ADAM_B2 * v_refs[i][...] + (1.0 - ADAM_B2) * (gv * gv)
            nm_refs[i][...] = nm
            nv_refs[i][...] = nv
            d_refs[i][...] = -ADAM_LR * ((nm / c1) / (jnp.sqrt(nv / c2) + ADAM_EPS) + ADAM_WD * w_refs[i][...])

    shapes = [jax.ShapeDtypeStruct(w.shape, F32) for w in ws]
    outs = pl.pallas_call(body, name="adamw_small", in_specs=[VMEM_FULL] * (4 * k), out_specs=[VMEM_FULL] * (3 * k),
                          out_shape=shapes * 3)(*ws, *gs, *ms, *vs)
    return outs[0:k], outs[k:2 * k], outs[2 * k:3 * k]


SMALL = (("c_ctx", D_MODEL), ("b_ada", 6 * D_MODEL), ("q_norm_g", HEAD_DIM), ("k_norm_g", HEAD_DIM),
         ("sink_logit", HEADS_A), ("ln1_g", D_MODEL), ("ln1_b", D_MODEL), ("ln2_g", D_MODEL), ("ln2_b", D_MODEL))
WEIGHT_ORDER = ("c_ctx", "w_ada", "b_ada", "w_in", "q_norm_g", "k_norm_g", "sink_logit", "w_out", "ln1_g", "ln1_b",
                "w_gate", "w_up", "w_down", "ln2_g", "ln2_b")


def kernel(x, c, ctx, c_ctx, w_ada, b_ada, w_in, q_norm_g, k_norm_g, sink_logit, w_out, ln1_g, ln1_b, w_gate, w_up, w_down, ln2_g, ln2_b, loss_target, m_c_ctx, m_w_ada, m_b_ada, m_w_in, m_q_norm_g, m_k_norm_g, m_sink_logit, m_w_out, m_ln1_g, m_ln1_b, m_w_gate, m_w_up, m_w_down, m_ln2_g, m_ln2_b, v_c_ctx, v_w_ada, v_b_ada, v_w_in, v_q_norm_g, v_k_norm_g, v_sink_logit, v_w_out, v_ln1_g, v_ln1_b, v_w_gate, v_w_up, v_w_down, v_ln2_g, v_ln2_b):
    d = D_MODEL
    w = dict(c_ctx=c_ctx, w_ada=w_ada[0], b_ada=b_ada, w_in=w_in[0], q_norm_g=q_norm_g, k_norm_g=k_norm_g,
             sink_logit=sink_logit, w_out=w_out[0], ln1_g=ln1_g, ln1_b=ln1_b, w_gate=w_gate[0], w_up=w_up[0],
             w_down=w_down[0], ln2_g=ln2_g, ln2_b=ln2_b)
    m = dict(c_ctx=m_c_ctx, w_ada=m_w_ada[0], b_ada=m_b_ada, w_in=m_w_in[0], q_norm_g=m_q_norm_g, k_norm_g=m_k_norm_g,
             sink_logit=m_sink_logit, w_out=m_w_out[0], ln1_g=m_ln1_g, ln1_b=m_ln1_b, w_gate=m_w_gate[0],
             w_up=m_w_up[0], w_down=m_w_down[0], ln2_g=m_ln2_g, ln2_b=m_ln2_b)
    v = dict(c_ctx=v_c_ctx, w_ada=v_w_ada[0], b_ada=v_b_ada, w_in=v_w_in[0], q_norm_g=v_q_norm_g, k_norm_g=v_k_norm_g,
             sink_logit=v_sink_logit, w_out=v_w_out[0], ln1_g=v_ln1_g, ln1_b=v_ln1_b, w_gate=v_w_gate[0],
             w_up=v_w_up[0], w_down=v_w_down[0], ln2_g=v_ln2_g, ln2_b=v_ln2_b)
    mx, my, mc = _me()
    s_me = _shard_of((mx, my))
    me = _dev_index(mx, my, mc)
    pad8 = lambda row: jnp.concatenate([row.reshape(1, -1), jnp.zeros((7, row.size), F32)], axis=0)

    b_shard = lax.dynamic_slice(b_ada, (0, s_me * ADA_W), (1, ADA_W))
    act, mods4 = _ada_forward(pad8(c), pad8(c_ctx), w["w_ada"], b_shard)

    gathers = []
    prev, shard = mods4, s_me
    for k, names in enumerate(W_GROUPS):
        arrs = tuple(BIG_INDEX[name] for name in names)
        bufs = [_cast_into_full(w[name], shard, BIG[a][1], BIG[a][2], "cast_" + name) for name, a in zip(names, arrs)]
        send_sems, recv_sems, thru, prev = _gather_start("g%d" % k, arrs, bufs, prev)
        shard = s_me + prev[0, 0].astype(jnp.int32)
        gathers.append((arrs, send_sems, recv_sems, thru))

    forwards = {}

    def prefetch(k, after):
        arrs, send_sems, recv_sems, thru = gathers[k]
        landed = _gather_wait("g%d" % k, arrs, send_sems, recv_sems, thru, after)
        fwd_send, fwd_recv, landed, token = _forward_start("g%d" % k, arrs, landed)
        forwards[k] = (fwd_send, fwd_recv, landed)
        return token[0, 0]

    def weights(k, after):
        arrs, send_sems, recv_sems, thru = gathers[k]
        if k in forwards:
            return _forward_wait("g%d" % k, arrs, *forwards[k], after)
        landed = _gather_wait("g%d" % k, arrs, send_sems, recv_sems, thru, after)
        return _gather_forward("g%d" % k, arrs, landed)

    mod = jnp.transpose(mods4[:, 0:1, :], (1, 0, 2)).reshape(1, 6 * d) + prev[0, 0]
    mod_ctx = jnp.transpose(mods4[:, 8:9, :], (1, 0, 2)).reshape(1, 6 * d)

    scatters = {}

    def grads_out(k, dws):
        arrs = tuple(BIG_INDEX[name] for name in G_GROUPS[k])
        scatters[k], zero = _scatter_begin("g%d" % k, arrs, dws)
        return zero

    grad_x, partial = _layer_fwd_bwd(x[0], ctx[0], loss_target[0], mod, mod_ctx, weights, prefetch, grads_out,
                                     q_norm_g, k_norm_g, sink_logit, ln1_g, ln1_b, ln2_g, ln2_b)
    grads, delta, new_m, new_v = {}, {}, {}, {}

    p_send, p_recv, partial, p_land, after = _rows_start("partials", partial)
    joins = []
    for k, names in enumerate(G_GROUPS):
        arrs = tuple(BIG_INDEX[name] for name in names)
        state, after = _scatter_reduce("g%d" % k, arrs, scatters[k], after)
        joins.append(state)
    for k, names in enumerate(G_GROUPS):
        g_own, g_other = _scatter_end("g%d" % k, joins[k], after)
        for name, own, other in zip(names, g_own, g_other):
            grads[name], delta[name], new_m[name], new_v[name] = _adamw_halves(
                w[name], own, other, m[name], v[name], mc, BIG[BIG_INDEX[name]][2], "adamw_" + name)
            after = new_v[name]

    gathered = _rows_wait("partials", p_send, p_recv, partial, p_land, after)
    tot = _small_reduce(gathered)
    grads["b_ada"] = tot[0:6].reshape(1, 6 * d)
    grads["ln1_g"], grads["ln1_b"], grads["ln2_g"], grads["ln2_b"] = tot[8:9], tot[9:10], tot[10:11], tot[11:12]
    grads["q_norm_g"] = tot[13:14, 0:HEAD_DIM]
    grads["k_norm_g"] = tot[13:14, HEAD_DIM:2 * HEAD_DIM]
    grads["sink_logit"] = tot[13:14, 2 * HEAD_DIM:2 * HEAD_DIM + HEADS_A]
    loss = tot[12, 0]

    dm_all = gathered[:, 0:6, :].reshape(N_DEV, 6 * d)
    dmc_tot = jnp.concatenate([tot[6:8].reshape(1, 2 * d), jnp.zeros((1, 4 * d), F32)], axis=1)
    dm_rows = jnp.concatenate([pad8(dm_all[i]) for i in range(N_DEV)] + [pad8(dmc_tot), jnp.zeros((8, 6 * d), F32)], axis=0)
    dm_shard = lax.dynamic_slice(dm_rows, (0, s_me * ADA_W), (ADA_ROWS, ADA_W))
    dmc_shard = lax.dynamic_slice(pad8(dmc_tot), (0, s_me * ADA_W), (8, ADA_W))
    cc_part = _matmul(dmc_shard, w["w_ada"], name="d_cctx", tb=True, tm=8, tn=1024, tk=1536, out_dtype=F32)
    c_send, c_recv, cc_part, c_land, c_token = _rows_start("cctx", cc_part)
    grads["w_ada"] = _matmul(act, dm_shard, name="dw_ada", ta=True, tm=1024, tn=1024, tk=ADA_ROWS, out_dtype=F32,
                             after=c_token)
    delta["w_ada"], new_m["w_ada"], new_v["w_ada"] = _adamw(w["w_ada"], grads["w_ada"], m["w_ada"], v["w_ada"],
                                                            "adamw_w_ada")
    gathered_cc = _rows_wait("cctx", c_send, c_recv, cc_part, c_land, new_v["w_ada"])
    grads["c_ctx"] = _cctx_grad(gathered_cc, c_ctx).reshape(d)
    rows = lambda t: [t[name].reshape(1, size) for name, size in SMALL]
    small = _adamw_small(rows(w), rows(grads), rows(m), rows(v))
    for k, (name, size) in enumerate(SMALL):
        delta[name], new_m[name], new_v[name] = [t[k].reshape(w[name].shape) for t in small]
        grads[name] = grads[name].reshape(w[name].shape)

    lead = lambda name, t: t[None] if name in ("w_ada", "w_in", "w_out", "w_gate", "w_up", "w_down") else t
    outs = [loss, grad_x[None]]
    for group in (grads, delta, new_m, new_v):
        outs += [lead(name, group[name]) for name in WEIGHT_ORDER]
    return tuple(outs)
```

```python
import functools
import math

import jax
import jax.numpy as jnp
from jax import lax
from jax.experimental import pallas as pl
from jax.experimental.pallas import tpu as pltpu

F32 = jnp.float32
BF16 = jnp.bfloat16
MESH = pl.DeviceIdType.MESH

D_MODEL = 2048
HEAD_DIM = 128
HEADS_A = 8
HEADS_B = 8
KV_A = 2
KV_B = 2
GROUP = 4
GRID_W = 64
WINDOW = 128
BLOCK = 128
FFN = 5632
IN_WIDTH = 3072
MIX_WIDTH = 2048
ROPE_THETA = 10000.0
EPS = 1e-6
ATTN_SCALE = HEAD_DIM ** -0.5
DN_ALPHA = 2.0 ** 0.25
N_SHARD = 4
N_DEV = 8

ADAM_LR = 0.001
ADAM_B1 = 0.9
ADAM_B2 = 0.999
ADAM_EPS = 1e-08
ADAM_WD = 0.01
ADAM_STEP = 10

QA0, KA0, VA0, QB0, KB0, VB0 = 0, 1024, 1280, 1536, 2560, 2816

VMEM_LIMIT = 56 * 1024 * 1024
ROW_TILE = 256
NN = (((1,), (0,)), ((), ()))
NT = (((1,), (1,)), ((), ()))
TN = (((0,), (0,)), ((), ()))


def _fit(total, pref):
    step = ROW_TILE // 4
    best = step
    for cand in range(step, pref + 1, step):
        if total % cand == 0:
            best = cand
    return best


def _params(sem=None):
    return pltpu.CompilerParams(dimension_semantics=sem, vmem_limit_bytes=VMEM_LIMIT)


def _matmul(a, b, *, name, ta=False, tb=False, tm, tn, tk, out_dtype, after=None):
    m = a.shape[1] if ta else a.shape[0]
    k = a.shape[0] if ta else a.shape[1]
    n = b.shape[0] if tb else b.shape[1]
    assert (b.shape[1] if tb else b.shape[0]) == k
    tm, tn, tk = min(tm, m), min(tn, n), min(tk, k)
    assert m % tm == 0 and n % tn == 0 and k % tk == 0, (name, m, n, k, tm, tn, tk)
    nk = k // tk
    dn = (((0 if ta else 1,), (1 if tb else 0,)), ((), ()))

    def product(a_ref, b_ref):
        return lax.dot_general(a_ref[...].astype(BF16), b_ref[...].astype(BF16), dn, preferred_element_type=F32)

    def body_whole_k(a_ref, b_ref, *rest):
        o_ref = rest[-1]
        o_ref[...] = product(a_ref, b_ref).astype(o_ref.dtype)

    def body(a_ref, b_ref, *rest):
        o_ref, acc_ref = rest[-2:]
        kk = pl.program_id(2)
        part = product(a_ref, b_ref)

        @pl.when(kk == 0)
        def _():
            acc_ref[...] = part

        @pl.when(kk != 0)
        def _():
            acc_ref[...] += part

        @pl.when(kk == nk - 1)
        def _():
            o_ref[...] = acc_ref[...].astype(o_ref.dtype)

    a_spec = (pl.BlockSpec((tk, tm), lambda i, j, kk: (kk, i)) if ta
              else pl.BlockSpec((tm, tk), lambda i, j, kk: (i, kk)))
    b_spec = (pl.BlockSpec((tn, tk), lambda i, j, kk: (j, kk)) if tb
              else pl.BlockSpec((tk, tn), lambda i, j, kk: (kk, j)))
    return pl.pallas_call(
        body_whole_k if nk == 1 else body, name=name, grid=(m // tm, n // tn, nk),
        in_specs=[a_spec, b_spec] + ([] if after is None else [pl.BlockSpec(memory_space=pl.ANY)]),
        out_specs=pl.BlockSpec((tm, tn), lambda i, j, kk: (i, j)),
        out_shape=jax.ShapeDtypeStruct((m, n), out_dtype),
        scratch_shapes=[] if nk == 1 else [pltpu.VMEM((tm, tn), F32)],
        compiler_params=_params(("parallel", "parallel", "arbitrary")),
    )(a, b, *([] if after is None else [after]))


def _modulate_rows(x, ctx, mods):
    n, d = x.shape
    c = ctx.shape[0]
    nx = n // ROW_TILE
    assert c == ROW_TILE

    def body(x_ref, ctx_ref, mods_ref, o_ref):
        i = pl.program_id(0)

        @pl.when(i < nx)
        def _():
            o_ref[...] = (x_ref[...] * (1.0 + mods_ref[0:1, :]) + mods_ref[1:2, :]).astype(BF16)

        @pl.when(i >= nx)
        def _():
            o_ref[...] = (ctx_ref[...] * (1.0 + mods_ref[2:3, :]) + mods_ref[3:4, :]).astype(BF16)

    return pl.pallas_call(
        body, name="modulate_rows", grid=(nx + 1,),
        in_specs=[pl.BlockSpec((ROW_TILE, d), lambda i: (jnp.minimum(i, nx - 1), 0)),
                  pl.BlockSpec((ROW_TILE, d), lambda i: (0, 0)),
                  pl.BlockSpec((8, d), lambda i: (0, 0))],
        out_specs=pl.BlockSpec((ROW_TILE, d), lambda i: (i, 0)),
        out_shape=jax.ShapeDtypeStruct((n + c, d), BF16),
        compiler_params=_params(("parallel",)),
    )(x, ctx, mods)


def _rope_tables(n, c):
    rows = n // GRID_W
    row_ids = jnp.repeat(jnp.arange(rows, dtype=F32), GRID_W)
    col_ids = jnp.tile(jnp.arange(GRID_W, dtype=F32), rows)
    axis_dim = HEAD_DIM // 2
    inv_freq = jnp.power(ROPE_THETA, -jnp.arange(0, axis_dim, 2, dtype=F32) / axis_dim)
    ang_r = row_ids[:, None] * inv_freq
    ang_c = col_ids[:, None] * inv_freq
    ang = jnp.concatenate([ang_r, ang_r, ang_c, ang_c], axis=-1)
    cos, sin = jnp.cos(ang), jnp.sin(ang)
    quarter = (jnp.arange(HEAD_DIM) // (HEAD_DIM // 4)) % 2
    sin_a = jnp.where(quarter == 0, -sin, 0.0)
    sin_b = jnp.where(quarter == 1, sin, 0.0)
    pad = lambda t, v: jnp.concatenate([t, jnp.full((c, HEAD_DIM), v, F32)], axis=0)
    return pad(cos, 1.0), pad(sin_a, 0.0), pad(sin_b, 0.0)


def _rope(x, cos, sin_a, sin_b):
    return x * cos + pltpu.roll(x, 96, 1) * sin_a + pltpu.roll(x, 32, 1) * sin_b


def _rope_t(dy, cos, sin_a, sin_b):
    return dy * cos - pltpu.roll(dy, 96, 1) * sin_a - pltpu.roll(dy, 32, 1) * sin_b


def _rms(x):
    r = lax.rsqrt(jnp.mean(x * x, axis=-1, keepdims=True) + EPS)
    return x * r, r


def _qkv_post(h_all, cos, sin_a, sin_b, q_g, k_g):
    t = h_all.shape[0]
    nt = t // ROW_TILE

    def body(h_ref, cos_ref, sa_ref, sb_ref, qg_ref, kg_ref, qa_ref, ka_ref, va_ref, qb_ref, kb_ref, vb_ref):
        cos_, sa, sb = cos_ref[...], sa_ref[...], sb_ref[...]
        sl = lambda off, hh: h_ref[:, off + hh * HEAD_DIM: off + (hh + 1) * HEAD_DIM]
        for hh in range(HEADS_A):
            qa_ref[hh] = (_rope(sl(QA0, hh), cos_, sa, sb) * ATTN_SCALE).astype(BF16)
        for hh in range(KV_A):
            ka_ref[hh] = _rope(sl(KA0, hh), cos_, sa, sb).astype(BF16)
            va_ref[hh] = sl(VA0, hh).astype(BF16)
        for hh in range(HEADS_B):
            xn, _ = _rms(sl(QB0, hh))
            qb_ref[hh] = (_rope(xn * qg_ref[...], cos_, sa, sb) * ATTN_SCALE).astype(BF16)
        for hh in range(KV_B):
            xn, _ = _rms(sl(KB0, hh))
            kb_ref[hh] = _rope(xn * kg_ref[...], cos_, sa, sb).astype(BF16)
            vb_ref[hh] = sl(VB0, hh).astype(BF16)

    tab = pl.BlockSpec((ROW_TILE, HEAD_DIM), lambda i: (i, 0))
    gain = pl.BlockSpec((1, HEAD_DIM), lambda i: (0, 0))
    hs = lambda nh: pl.BlockSpec((nh, ROW_TILE, HEAD_DIM), lambda i: (0, i, 0))
    sh = lambda nh: jax.ShapeDtypeStruct((nh, t, HEAD_DIM), BF16)
    return pl.pallas_call(
        body, name="qkv_post", grid=(nt,),
        in_specs=[pl.BlockSpec((ROW_TILE, IN_WIDTH), lambda i: (i, 0)), tab, tab, tab, gain, gain],
        out_specs=[hs(HEADS_A), hs(KV_A), hs(KV_A), hs(HEADS_B), hs(KV_B), hs(KV_B)],
        out_shape=[sh(HEADS_A), sh(KV_A), sh(KV_A), sh(HEADS_B), sh(KV_B), sh(KV_B)],
        compiler_params=_params(("parallel",)),
    )(h_all, cos, sin_a, sin_b, q_g, k_g)


def _qkv_bwd_post(h_all, cos, sin_a, sin_b, q_g, k_g, dqa, dka, dva, dqb, dkb, dvb, n):
    t = h_all.shape[0]
    nt = t // ROW_TILE
    nx = n // ROW_TILE

    def body(h_ref, cos_ref, sa_ref, sb_ref, qg_ref, kg_ref,
             dqa_ref, dka_ref, dva_ref, dqb_ref, dkb_ref, dvb_ref, dh_ref, gs_ref):
        i = pl.program_id(0)
        cos_, sa, sb = cos_ref[...], sa_ref[...], sb_ref[...]
        latent = (i < nx).astype(F32)
        sl = lambda off, hh: h_ref[:, off + hh * HEAD_DIM: off + (hh + 1) * HEAD_DIM]

        def put(off, hh, val):
            dh_ref[:, off + hh * HEAD_DIM: off + (hh + 1) * HEAD_DIM] = val.astype(BF16)

        def norm_bwd(x, gain, dy):
            xn, r = _rms(x)
            dxh = dy * gain
            dx = r * (dxh - xn * jnp.mean(dxh * xn, axis=-1, keepdims=True))
            return dx, jnp.sum(dy * xn, axis=0, keepdims=True)

        for hh in range(HEADS_A):
            put(QA0, hh, _rope_t(dqa_ref[hh] * (ATTN_SCALE * latent), cos_, sa, sb))
        for hh in range(KV_A):
            put(KA0, hh, _rope_t(dka_ref[hh], cos_, sa, sb))
            put(VA0, hh, dva_ref[hh])
        gq = jnp.zeros((1, HEAD_DIM), F32)
        gk = jnp.zeros((1, HEAD_DIM), F32)
        for hh in range(HEADS_B):
            dq_t = dqb_ref[hh // GROUP, :, (hh % GROUP) * ROW_TILE:(hh % GROUP + 1) * ROW_TILE]
            dy = _rope_t(dq_t.T * (ATTN_SCALE * latent), cos_, sa, sb)
            dx, g = norm_bwd(sl(QB0, hh), qg_ref[...], dy)
            put(QB0, hh, dx)
            gq = gq + g
        for hh in range(KV_B):
            dy = _rope_t(dkb_ref[hh], cos_, sa, sb)
            dx, g = norm_bwd(sl(KB0, hh), kg_ref[...], dy)
            put(KB0, hh, dx)
            gk = gk + g
            put(VB0, hh, dvb_ref[hh])
        upd = jnp.concatenate([gq, gk, jnp.zeros((6, HEAD_DIM), F32)], axis=0)

        @pl.when(i == 0)
        def _():
            gs_ref[...] = upd

        @pl.when(i != 0)
        def _():
            gs_ref[...] += upd

    tab = pl.BlockSpec((ROW_TILE, HEAD_DIM), lambda i: (i, 0))
    gain = pl.BlockSpec((1, HEAD_DIM), lambda i: (0, 0))
    lat = lambda nh: pl.BlockSpec((nh, ROW_TILE, HEAD_DIM), lambda i: (0, jnp.minimum(i, nx - 1), 0))
    full = lambda nh: pl.BlockSpec((nh, ROW_TILE, HEAD_DIM), lambda i: (0, i, 0))
    return pl.pallas_call(
        body, name="qkv_bwd_post", grid=(nt,),
        in_specs=[pl.BlockSpec((ROW_TILE, IN_WIDTH), lambda i: (i, 0)), tab, tab, tab, gain, gain,
                  lat(HEADS_A), full(KV_A), full(KV_A),
                  pl.BlockSpec((KV_B, None, HEAD_DIM, GROUP * ROW_TILE), lambda i: (0, jnp.minimum(i, nx - 1), 0, 0)),
                  full(KV_B), full(KV_B)],
        out_specs=[pl.BlockSpec((ROW_TILE, IN_WIDTH), lambda i: (i, 0)),
                   pl.BlockSpec((8, HEAD_DIM), lambda i: (0, 0))],
        out_shape=[jax.ShapeDtypeStruct((t, IN_WIDTH), BF16), jax.ShapeDtypeStruct((8, HEAD_DIM), F32)],
        compiler_params=_params(("arbitrary",)),
    )(h_all, cos, sin_a, sin_b, q_g, k_g, dqa, dka, dva, dqb, dkb, dvb)


GB_TQ = 256
GB_TK = 256


def _heads_rows(ref2d, tq):
    return jnp.concatenate([ref2d[:, hh * HEAD_DIM:(hh + 1) * HEAD_DIM] for hh in range(GROUP)], axis=0)


def _attn_b_fwd(qb, kb, vb, n):
    t = kb.shape[1]
    nk = t // GB_TK
    tq = GB_TQ
    nq = n // tq
    qb_step = 2 if nq % 2 == 0 else 1
    rows = qb_step * GROUP * tq

    def body(q_ref, k_ref, v_ref, o_ref, lse_ref, m_s, l_s, acc_s):
        blocks = [(s, hh) for s in range(qb_step) for hh in range(GROUP)]
        q = jnp.concatenate([q_ref[hh, s * tq:(s + 1) * tq, :] for s, hh in blocks], axis=0)
        m_s[...] = jnp.full((1, rows), -jnp.inf, F32)
        l_s[...] = jnp.zeros((1, rows), F32)
        acc_s[...] = jnp.zeros((HEAD_DIM, rows), F32)

        def scores(j):
            start = pl.multiple_of(j * GB_TK, GB_TK)
            return lax.dot_general(k_ref[pl.ds(start, GB_TK), :], q, NT, preferred_element_type=F32)

        def step(j, st):
            st_next = scores(jnp.minimum(j + 1, nk - 1))
            vs = v_ref[pl.ds(pl.multiple_of(j * GB_TK, GB_TK), GB_TK), :]
            m_prev = m_s[...]
            m_new = jnp.maximum(m_prev, jnp.max(st, axis=0, keepdims=True))
            p = jnp.exp(st - m_new)
            alpha = jnp.exp(m_prev - m_new)
            l_s[...] = alpha * l_s[...] + jnp.sum(p, axis=0, keepdims=True)
            acc_s[...] = alpha * acc_s[...] + lax.dot_general(vs, p.astype(BF16), TN, preferred_element_type=F32)
            m_s[...] = m_new
            return st_next

        lax.fori_loop(0, nk, step, scores(0))
        ot = acc_s[...] * (1.0 / l_s[...])
        lse = m_s[...] + jnp.log(l_s[...])
        width = GROUP * tq
        for s in range(qb_step):
            lse_ref[s] = lse[:, s * width:(s + 1) * width]
        for k, (s, hh) in enumerate(blocks):
            o_ref[s * tq:(s + 1) * tq, hh * HEAD_DIM:(hh + 1) * HEAD_DIM] = ot[:, k * tq:(k + 1) * tq].T.astype(BF16)

    return pl.pallas_call(
        body, name="attn_b_fwd", grid=(KV_B, nq // qb_step),
        in_specs=[pl.BlockSpec((GROUP, qb_step * tq, HEAD_DIM), lambda g, i: (g, i, 0)),
                  pl.BlockSpec((None, t, HEAD_DIM), lambda g, i: (g, 0, 0)),
                  pl.BlockSpec((None, t, HEAD_DIM), lambda g, i: (g, 0, 0))],
        out_specs=[pl.BlockSpec((qb_step * tq, GROUP * HEAD_DIM), lambda g, i: (i, KV_A + g)),
                   pl.BlockSpec((None, qb_step, 1, GROUP * tq), lambda g, i: (g, i, 0, 0))],
        out_shape=[jax.ShapeDtypeStruct((n, MIX_WIDTH), BF16),
                   jax.ShapeDtypeStruct((KV_B, nq, 1, GROUP * tq), F32)],
        scratch_shapes=[pltpu.VMEM((1, rows), F32), pltpu.VMEM((1, rows), F32), pltpu.VMEM((HEAD_DIM, rows), F32)],
        compiler_params=_params(("parallel", "parallel")),
    )(qb, kb, vb)


def _attn_b_bwd(qb, kb, vb, dheads, lse, delta, n):
    t = kb.shape[1]
    nk = t // GB_TK
    tq = GB_TQ
    nq = n // tq
    rows = GROUP * tq

    qb_step = 4 if nq % 4 == 0 else 1

    def body(q_ref, k_ref, v_ref, do_ref, lse_ref, dl_ref, dq_ref, dk_ref, dv_ref):
        j = pl.program_id(1)
        i = pl.program_id(2)

        blocks = [(s, hh) for s in range(qb_step) for hh in range(GROUP)]
        q = jnp.concatenate([q_ref[hh, s * tq:(s + 1) * tq, :] for s, hh in blocks], axis=0)
        do = jnp.concatenate([do_ref[s * tq:(s + 1) * tq, hh * HEAD_DIM:(hh + 1) * HEAD_DIM] for s, hh in blocks], axis=0)
        lse_row = jnp.concatenate([lse_ref[s] for s in range(qb_step)], axis=1)
        dl_row = jnp.concatenate([dl_ref[s] for s in range(qb_step)], axis=1)
        ks, vs = k_ref[...], v_ref[...]
        st = lax.dot_general(ks, q, NT, preferred_element_type=F32)
        p = jnp.exp(st - lse_row)
        dpt = lax.dot_general(vs, do, NT, preferred_element_type=F32)
        ds = (p * (dpt - dl_row)).astype(BF16)
        dv_part = lax.dot_general(p.astype(BF16), do, NN, preferred_element_type=F32)
        dk_part = lax.dot_general(ds, q, NN, preferred_element_type=F32)
        dq_part = lax.dot_general(ks, ds, TN, preferred_element_type=F32)

        @pl.when(i == 0)
        def _():
            dk_ref[...] = dk_part
            dv_ref[...] = dv_part

        @pl.when(i != 0)
        def _():
            dk_ref[...] += dk_part
            dv_ref[...] += dv_part

        for s in range(qb_step):
            piece = dq_part[:, s * rows:(s + 1) * rows]

            @pl.when(j == 0)
            def _():
                dq_ref[qb_step * i + s] = piece

            @pl.when(j != 0)
            def _():
                dq_ref[qb_step * i + s] += piece

    kv = pl.BlockSpec((None, GB_TK, HEAD_DIM), lambda g, j, i: (g, j, 0))
    row = pl.BlockSpec((None, qb_step, 1, rows), lambda g, j, i: (g, i, 0, 0))
    return pl.pallas_call(
        body, name="attn_b_bwd", grid=(KV_B, nk, nq // qb_step),
        in_specs=[pl.BlockSpec((GROUP, qb_step * tq, HEAD_DIM), lambda g, j, i: (g, i, 0)), kv, kv,
                  pl.BlockSpec((qb_step * tq, GROUP * HEAD_DIM), lambda g, j, i: (i, KV_A + g)), row, row],
        out_specs=[pl.BlockSpec((None, nq, HEAD_DIM, rows), lambda g, j, i: (g, 0, 0, 0)), kv, kv],
        out_shape=[jax.ShapeDtypeStruct((KV_B, nq, HEAD_DIM, rows), F32),
                   jax.ShapeDtypeStruct((KV_B, t, HEAD_DIM), F32),
                   jax.ShapeDtypeStruct((KV_B, t, HEAD_DIM), F32)],
        compiler_params=_params(("parallel", "arbitrary", "arbitrary")),
    )(qb, kb, vb, dheads, lse, delta)


def _delta_rows(dheads, heads):
    n = heads.shape[0]
    tq = GB_TQ
    w = GROUP * HEAD_DIM

    def body(a_ref, b_ref, o_ref):
        prod = a_ref[...].astype(F32) * b_ref[...].astype(F32)
        cols = [jnp.sum(prod[:, hh * HEAD_DIM:(hh + 1) * HEAD_DIM].T, axis=0, keepdims=True) for hh in range(GROUP)]
        o_ref[...] = jnp.concatenate(cols, axis=1)

    blk = pl.BlockSpec((tq, w), lambda g, i: (i, KV_A + g))
    return pl.pallas_call(
        body, name="delta_rows", grid=(KV_B, n // tq),
        in_specs=[blk, blk],
        out_specs=pl.BlockSpec((None, None, 1, GROUP * tq), lambda g, i: (g, i, 0, 0)),
        out_shape=jax.ShapeDtypeStruct((KV_B, n // tq, 1, GROUP * tq), F32),
        compiler_params=_params(("parallel", "parallel")),
    )(dheads, heads)


KWIN = 3 * BLOCK


def _window_scores(q, k_ref, j, n, nb, sink_row):
    c = k_ref.shape[0] - n
    start = pl.multiple_of(jnp.clip(j - 1, 0, nb - 3) * BLOCK, BLOCK)
    kw = k_ref[pl.ds(start, KWIN), :]
    kc = k_ref[pl.ds(n, c), :]
    s_loc = lax.dot_general(kw, q, NT, preferred_element_type=F32)
    s_ctx = lax.dot_general(kc, q, NT, preferred_element_type=F32)
    cols = GROUP * BLOCK
    qpos = j * BLOCK + lax.broadcasted_iota(jnp.int32, (KWIN, cols), 1) % BLOCK
    kpos = start + lax.broadcasted_iota(jnp.int32, (KWIN, cols), 0)
    s_loc = jnp.where(jnp.abs(qpos - kpos) <= WINDOW, s_loc, -jnp.inf)
    m = jnp.maximum(jnp.maximum(jnp.max(s_loc, axis=0, keepdims=True), jnp.max(s_ctx, axis=0, keepdims=True)),
                    sink_row)
    e_loc, e_ctx, e_sink = jnp.exp(s_loc - m), jnp.exp(s_ctx - m), jnp.exp(sink_row - m)
    inv = 1.0 / (jnp.sum(e_loc, axis=0, keepdims=True) + jnp.sum(e_ctx, axis=0, keepdims=True) + e_sink)
    return e_loc * inv, e_ctx * inv, e_sink * inv, start


def _sink_row(sink_ref, g):
    return jnp.concatenate([sink_ref[pl.ds(g * GROUP + hh, 1), :] for hh in range(GROUP)], axis=1)


def _attn_a_fwd(qa, ka, va, sink_b, heads_b, n):
    t = ka.shape[1]
    nb = n // BLOCK
    assert nb >= 3
    wb = 2 if nb % 2 == 0 else 1

    def body(q_ref, k_ref, v_ref, sink_ref, heads_b_ref, o_ref):
        g, jj = pl.program_id(0), pl.program_id(1)
        for s in range(wb):
            j = jj * wb + s
            rows = slice(s * BLOCK, (s + 1) * BLOCK)
            q = q_ref[:, rows, :].reshape(GROUP * BLOCK, HEAD_DIM)
            p_loc, p_ctx, _, start = _window_scores(q, k_ref, j, n, nb, _sink_row(sink_ref, g))
            vw = v_ref[pl.ds(start, KWIN), :]
            vc = v_ref[pl.ds(n, t - n), :]
            ot = (lax.dot_general(vw, p_loc.astype(BF16), TN, preferred_element_type=F32)
                  + lax.dot_general(vc, p_ctx.astype(BF16), TN, preferred_element_type=F32))
            for hh in range(GROUP):
                o_ref[rows, hh * HEAD_DIM:(hh + 1) * HEAD_DIM] = ot[:, hh * BLOCK:(hh + 1) * BLOCK].T.astype(BF16)

    return pl.pallas_call(
        body, name="attn_a_fwd", grid=(KV_A, nb // wb),
        in_specs=[pl.BlockSpec((GROUP, wb * BLOCK, HEAD_DIM), lambda g, j: (g, j, 0)),
                  pl.BlockSpec((None, t, HEAD_DIM), lambda g, j: (g, 0, 0)),
                  pl.BlockSpec((None, t, HEAD_DIM), lambda g, j: (g, 0, 0)),
                  pl.BlockSpec((HEADS_A, HEAD_DIM), lambda g, j: (0, 0)),
                  pl.BlockSpec(memory_space=pl.ANY)],
        out_specs=pl.BlockSpec((wb * BLOCK, GROUP * HEAD_DIM), lambda g, j: (j, g)),
        out_shape=jax.ShapeDtypeStruct((n, MIX_WIDTH), BF16),
        input_output_aliases={4: 0},
        compiler_params=_params(("parallel", "parallel")),
    )(qa, ka, va, sink_b, heads_b)


def _attn_a_bwd(qa, ka, va, sink_b, dheads, n):
    t = ka.shape[1]
    c = t - n
    nb = n // BLOCK
    wb = 2 if nb % 2 == 0 else 1

    def body(q_ref, k_ref, v_ref, sink_ref, do_ref, dq_ref, dk_ref, dv_ref, dsink_ref):
        g, jj = pl.program_id(0), pl.program_id(1)

        @pl.when(jj == 0)
        def _():
            dk_ref[...] = jnp.zeros_like(dk_ref)
            dv_ref[...] = jnp.zeros_like(dv_ref)
            dsink_ref[...] = jnp.zeros_like(dsink_ref)

        for s in range(wb):
            j = jj * wb + s
            rows = slice(s * BLOCK, (s + 1) * BLOCK)
            q = q_ref[:, rows, :].reshape(GROUP * BLOCK, HEAD_DIM)
            do = _heads_rows(do_ref[rows, :], BLOCK)
            p_loc, p_ctx, p_sink, start = _window_scores(q, k_ref, j, n, nb, _sink_row(sink_ref, g))
            kw, vw = k_ref[pl.ds(start, KWIN), :], v_ref[pl.ds(start, KWIN), :]
            kc, vc = k_ref[pl.ds(n, c), :], v_ref[pl.ds(n, c), :]
            dp_loc = lax.dot_general(vw, do, NT, preferred_element_type=F32)
            dp_ctx = lax.dot_general(vc, do, NT, preferred_element_type=F32)
            dl = jnp.sum(p_loc * dp_loc, axis=0, keepdims=True) + jnp.sum(p_ctx * dp_ctx, axis=0, keepdims=True)
            ds_loc = (p_loc * (dp_loc - dl)).astype(BF16)
            ds_ctx = (p_ctx * (dp_ctx - dl)).astype(BF16)
            dqt = (lax.dot_general(kw, ds_loc, TN, preferred_element_type=F32)
                   + lax.dot_general(kc, ds_ctx, TN, preferred_element_type=F32))
            for hh in range(GROUP):
                dq_ref[hh, rows, :] = dqt[:, hh * BLOCK:(hh + 1) * BLOCK].T
            dk_ref[pl.ds(start, KWIN), :] += lax.dot_general(ds_loc, q, NN, preferred_element_type=F32)
            dv_ref[pl.ds(start, KWIN), :] += lax.dot_general(p_loc.astype(BF16), do, NN, preferred_element_type=F32)
            dk_ref[pl.ds(n, c), :] += lax.dot_general(ds_ctx, q, NN, preferred_element_type=F32)
            dv_ref[pl.ds(n, c), :] += lax.dot_general(p_ctx.astype(BF16), do, NN, preferred_element_type=F32)
            dsk = -(p_sink * dl)
            upd = [jnp.broadcast_to(jnp.sum(dsk[:, hh * BLOCK:(hh + 1) * BLOCK], axis=1, keepdims=True), (1, HEAD_DIM))
                   for hh in range(GROUP)]
            dsink_ref[...] += jnp.concatenate(upd + [jnp.zeros((8 - GROUP, HEAD_DIM), F32)], axis=0)

    res = pl.BlockSpec((None, t, HEAD_DIM), lambda g, j: (g, 0, 0))
    return pl.pallas_call(
        body, name="attn_a_bwd", grid=(KV_A, nb // wb),
        in_specs=[pl.BlockSpec((GROUP, wb * BLOCK, HEAD_DIM), lambda g, j: (g, j, 0)), res, res,
                  pl.BlockSpec((HEADS_A, HEAD_DIM), lambda g, j: (0, 0)),
                  pl.BlockSpec((wb * BLOCK, GROUP * HEAD_DIM), lambda g, j: (j, g))],
        out_specs=[pl.BlockSpec((GROUP, wb * BLOCK, HEAD_DIM), lambda g, j: (g, j, 0)), res, res,
                   pl.BlockSpec((None, 8, HEAD_DIM), lambda g, j: (g, 0, 0))],
        out_shape=[jax.ShapeDtypeStruct((HEADS_A, n, HEAD_DIM), F32),
                   jax.ShapeDtypeStruct((KV_A, t, HEAD_DIM), F32),
                   jax.ShapeDtypeStruct((KV_A, t, HEAD_DIM), F32),
                   jax.ShapeDtypeStruct((KV_A, 8, HEAD_DIM), F32)],
        compiler_params=_params(("parallel", "arbitrary")),
    )(qa, ka, va, sink_b, dheads)


def _ln_stats(r):
    mu = jnp.mean(r, axis=-1, keepdims=True)
    cen = r - mu
    rstd = lax.rsqrt(jnp.mean(cen * cen, axis=-1, keepdims=True) + EPS)
    return cen * rstd, rstd


def _ln_bwd(dy, xhat, rstd, gain):
    dxh = dy * gain
    return rstd * (dxh - jnp.mean(dxh, axis=-1, keepdims=True)
                   - xhat * jnp.mean(dxh * xhat, axis=-1, keepdims=True))


def _accumulate_rows(ref, rows, i):
    pad = [jnp.zeros_like(rows[0])] * (8 - len(rows))
    upd = jnp.concatenate(rows + pad, axis=0)

    @pl.when(i == 0)
    def _():
        ref[...] = upd

    @pl.when(i != 0)
    def _():
        ref[...] += upd


def _colsum(v):
    return jnp.sum(v, axis=0, keepdims=True)


LN_TILE = 256


def _res_ln1(x, a, vec):
    n, d = x.shape

    def body(x_ref, a_ref, v_ref, xh_ref, rs_ref, u_ref):
        r1 = DN_ALPHA * x_ref[...] + v_ref[0:1, :] * a_ref[...]
        xhat, rstd = _ln_stats(r1)
        xh_ref[...] = xhat
        rs_ref[...] = rstd
        x1 = xhat * v_ref[1:2, :] + v_ref[2:3, :]
        u_ref[...] = (x1 * (1.0 + v_ref[3:4, :]) + v_ref[4:5, :]).astype(BF16)

    row = pl.BlockSpec((LN_TILE, d), lambda i: (i, 0))
    return pl.pallas_call(
        body, name="res_ln1", grid=(n // LN_TILE,),
        in_specs=[row, row, pl.BlockSpec((8, d), lambda i: (0, 0))],
        out_specs=[row, pl.BlockSpec((LN_TILE, 1), lambda i: (i, 0)), row],
        out_shape=[jax.ShapeDtypeStruct((n, d), F32), jax.ShapeDtypeStruct((n, 1), F32),
                   jax.ShapeDtypeStruct((n, d), BF16)],
        compiler_params=_params(("parallel",)),
    )(x, a, vec)


def _res_ln2_loss(xhat1, f, target, vec):
    n, d = f.shape

    def body(xh_ref, f_ref, t_ref, v_ref, dr_ref, df_ref, s_ref):
        i = pl.program_id(0)
        x1 = xh_ref[...] * v_ref[1:2, :] + v_ref[2:3, :]
        fv = f_ref[...]
        xhat, rstd = _ln_stats(DN_ALPHA * x1 + v_ref[0:1, :] * fv)
        err = xhat * v_ref[3:4, :] + v_ref[4:5, :] - t_ref[...]
        dy = err * (1.0 / d)
        dr2 = _ln_bwd(dy, xhat, rstd, v_ref[3:4, :])
        dr_ref[...] = dr2
        df_ref[...] = (dr2 * v_ref[0:1, :]).astype(BF16)
        _accumulate_rows(s_ref, [_colsum(dy * xhat), _colsum(dy), _colsum(dr2 * fv),
                                 _colsum(err * err) * (0.5 / d)], i)

    row = pl.BlockSpec((LN_TILE, d), lambda i: (i, 0))
    return pl.pallas_call(
        body, name="res_ln2_loss", grid=(n // LN_TILE,),
        in_specs=[row, row, row, pl.BlockSpec((8, d), lambda i: (0, 0))],
        out_specs=[row, row, pl.BlockSpec((8, d), lambda i: (0, 0))],
        out_shape=[jax.ShapeDtypeStruct((n, d), F32), jax.ShapeDtypeStruct((n, d), BF16),
                   jax.ShapeDtypeStruct((8, d), F32)],
        compiler_params=_params(("arbitrary",)),
    )(xhat1, f, target, vec)


def _ln1_bwd(du2, dr2, xhat1, rstd1, a, vec):
    n, d = du2.shape

    def body(du_ref, dr2_ref, xh_ref, rs_ref, a_ref, v_ref, dxp_ref, da_ref, s_ref):
        i = pl.program_id(0)
        du, xhat = du_ref[...], xh_ref[...]
        x1 = xhat * v_ref[1:2, :] + v_ref[2:3, :]
        dx1 = DN_ALPHA * dr2_ref[...] + du * (1.0 + v_ref[0:1, :])
        dr1 = _ln_bwd(dx1, xhat, rs_ref[...], v_ref[1:2, :])
        dxp_ref[...] = DN_ALPHA * dr1
        da_ref[...] = (dr1 * v_ref[3:4, :]).astype(BF16)
        _accumulate_rows(s_ref, [_colsum(du * x1), _colsum(du), _colsum(dx1 * xhat), _colsum(dx1),
                                 _colsum(dr1 * a_ref[...])], i)

    row = pl.BlockSpec((LN_TILE, d), lambda i: (i, 0))
    return pl.pallas_call(
        body, name="ln1_bwd", grid=(n // LN_TILE,),
        in_specs=[row, row, row, pl.BlockSpec((LN_TILE, 1), lambda i: (i, 0)), row,
                  pl.BlockSpec((8, d), lambda i: (0, 0))],
        out_specs=[row, row, pl.BlockSpec((8, d), lambda i: (0, 0))],
        out_shape=[jax.ShapeDtypeStruct((n, d), F32), jax.ShapeDtypeStruct((n, d), BF16),
                   jax.ShapeDtypeStruct((8, d), F32)],
        compiler_params=_params(("arbitrary",)),
    )(du2, dr2, xhat1, rstd1, a, vec)


def _mod1_bwd(du_all, dxp, x, ctx, mods):
    n, d = x.shape
    nx = n // ROW_TILE

    def body(du_ref, dxp_ref, x_ref, ctx_ref, m_ref, gx_ref, s_ref):
        i = pl.program_id(0)
        du = du_ref[...]
        zero = jnp.zeros((1, d), F32)

        @pl.when(i == 0)
        def _():
            s_ref[...] = jnp.zeros_like(s_ref)

        @pl.when(i < nx)
        def _():
            gx_ref[...] = dxp_ref[...] + du * (1.0 + m_ref[0:1, :])
            s_ref[...] += jnp.concatenate([_colsum(du * x_ref[...]), _colsum(du)] + [zero] * 6, axis=0)

        @pl.when(i >= nx)
        def _():
            s_ref[...] += jnp.concatenate([zero, zero, _colsum(du * ctx_ref[...]), _colsum(du)] + [zero] * 4, axis=0)

    lat = pl.BlockSpec((ROW_TILE, d), lambda i: (jnp.minimum(i, nx - 1), 0))
    return pl.pallas_call(
        body, name="mod1_bwd", grid=(nx + 1,),
        in_specs=[pl.BlockSpec((ROW_TILE, d), lambda i: (i, 0)), lat, lat,
                  pl.BlockSpec((ROW_TILE, d), lambda i: (0, 0)), pl.BlockSpec((8, d), lambda i: (0, 0))],
        out_specs=[lat, pl.BlockSpec((8, d), lambda i: (0, 0))],
        out_shape=[jax.ShapeDtypeStruct((n, d), F32), jax.ShapeDtypeStruct((8, d), F32)],
        compiler_params=_params(("arbitrary",)),
    )(du_all, dxp, x, ctx, mods)


FFN_TM = 1024
FFN_TN = 512


def _gate_up(u2, wg, wu, after):
    n, d = u2.shape
    f = wg.shape[1]

    def body(u_ref, wg_ref, wu_ref, after_ref, g_ref, up_ref, h_ref):
        u = u_ref[...]
        g = lax.dot_general(u, wg_ref[...], NN, preferred_element_type=F32)
        up = lax.dot_general(u, wu_ref[...], NN, preferred_element_type=F32)
        g_ref[...] = g.astype(BF16)
        up_ref[...] = up.astype(BF16)
        h_ref[...] = (g * jax.nn.sigmoid(g) * up).astype(BF16)

    tm = min(FFN_TM, n)
    wspec = pl.BlockSpec((d, FFN_TN), lambda j, i: (0, j))
    ospec = pl.BlockSpec((tm, FFN_TN), lambda j, i: (i, j))
    return pl.pallas_call(
        body, name="gate_up", grid=(f // FFN_TN, n // tm),
        in_specs=[pl.BlockSpec((tm, d), lambda j, i: (i, 0)), wspec, wspec, pl.BlockSpec(memory_space=pl.ANY)],
        out_specs=[ospec, ospec, ospec],
        out_shape=[jax.ShapeDtypeStruct((n, f), BF16)] * 3,
        compiler_params=_params(("parallel", "parallel")),
    )(u2, wg, wu, after)


def _glu_bwd(df, wd, g, u):
    n, d = df.shape
    f = wd.shape[0]

    def body(df_ref, wd_ref, g_ref, u_ref, dg_ref, du_ref):
        dh = lax.dot_general(df_ref[...], wd_ref[...], NT, preferred_element_type=F32)
        gv = g_ref[...].astype(F32)
        sig = jax.nn.sigmoid(gv)
        du_ref[...] = (dh * (gv * sig)).astype(BF16)
        dg_ref[...] = (dh * u_ref[...].astype(F32) * (sig * (1.0 + gv * (1.0 - sig)))).astype(BF16)

    tm = min(FFN_TM, n)
    ospec = pl.BlockSpec((tm, FFN_TN), lambda i, j: (i, j))
    return pl.pallas_call(
        body, name="glu_bwd", grid=(n // tm, f // FFN_TN),
        in_specs=[pl.BlockSpec((tm, d), lambda i, j: (i, 0)),
                  pl.BlockSpec((FFN_TN, d), lambda i, j: (j, 0)), ospec, ospec],
        out_specs=[ospec, ospec],
        out_shape=[jax.ShapeDtypeStruct((n, f), BF16), jax.ShapeDtypeStruct((n, f), BF16)],
        compiler_params=_params(("parallel", "parallel")),
    )(df, wd, g, u)


def _du2(dg, du, wg, wu):
    n, f = dg.shape
    d = wg.shape[0]
    tm, tn, tk = min(1024, n), 1024, 1408
    nk = f // tk

    def body(dg_ref, du_ref, wg_ref, wu_ref, o_ref, acc_ref):
        kk = pl.program_id(2)
        part = (lax.dot_general(dg_ref[...], wg_ref[...], NT, preferred_element_type=F32)
                + lax.dot_general(du_ref[...], wu_ref[...], NT, preferred_element_type=F32))

        @pl.when(kk == 0)
        def _():
            acc_ref[...] = part

        @pl.when(kk != 0)
        def _():
            acc_ref[...] += part

        @pl.when(kk == nk - 1)
        def _():
            o_ref[...] = acc_ref[...]

    aspec = pl.BlockSpec((tm, tk), lambda i, j, kk: (i, kk))
    wspec = pl.BlockSpec((tn, tk), lambda i, j, kk: (j, kk))
    return pl.pallas_call(
        body, name="du2", grid=(n // tm, d // tn, nk),
        in_specs=[aspec, aspec, wspec, wspec],
        out_specs=pl.BlockSpec((tm, tn), lambda i, j, kk: (i, j)),
        out_shape=jax.ShapeDtypeStruct((n, d), F32),
        scratch_shapes=[pltpu.VMEM((tm, tn), F32)],
        compiler_params=_params(("parallel", "parallel", "arbitrary")),
    )(dg, du, wg, wu)


def _rows8(rows, d=D_MODEL):
    rows = [r.reshape(1, d).astype(F32) for r in rows]
    return jnp.concatenate(rows + [jnp.zeros((8 - len(rows), d), F32)], axis=0)


W_GROUPS = (("w_in",), ("w_out", "w_gate", "w_up"), ("w_down",))
G_GROUPS = (("w_down", "w_gate", "w_up"), ("w_out",), ("w_in",))


def _layer_fwd_bwd(x, ctx, target, mod, mod_ctx, weights, prefetch, grads_out,
                   q_g, k_g, sink, ln1_g, ln1_b, ln2_g, ln2_b):
    n, d = x.shape
    c = ctx.shape[0]
    sh1, sc1, g1, sh2, sc2, g2 = [mod[:, k * d:(k + 1) * d] for k in range(6)]
    csh1, csc1 = mod_ctx[:, 0:d], mod_ctx[:, d:2 * d]
    cos, sin_a, sin_b = _rope_tables(n, c)
    sink_b = jnp.broadcast_to(sink.reshape(HEADS_A, 1), (HEADS_A, HEAD_DIM)).astype(F32)

    u_all = _modulate_rows(x, ctx, _rows8([sc1, sh1, csc1, csh1]))
    (w_in,) = weights(0, u_all)
    h_all = _matmul(u_all, w_in, name="qkv_proj", tm=_fit(n + c, 1088), tn=1024, tk=2048, out_dtype=F32)
    qa, ka, va, qb, kb, vb = _qkv_post(h_all, cos, sin_a, sin_b, q_g, k_g)
    heads_b, lse = _attn_b_fwd(qb, kb, vb, n)
    zero = prefetch(1, heads_b)
    heads = _attn_a_fwd(qa, ka, va, sink_b + zero, heads_b, n)
    w_out, w_gate, w_up = weights(1, heads)
    a = _matmul(heads, w_out, name="out_proj", tm=1024, tn=1024, tk=2048, out_dtype=F32)
    xhat1, rstd1, u2 = _res_ln1(x, a, _rows8([g1, ln1_g, ln1_b, sc2, sh2]))
    zero = prefetch(2, u2)
    gg, uu, hh = _gate_up(u2, w_gate, w_up, zero.reshape(1, 1))
    (w_down,) = weights(2, hh)
    f = _matmul(hh, w_down, name="ffn_down", tm=1024, tn=512, tk=FFN, out_dtype=F32)
    dr2, df, s_ln2 = _res_ln2_loss(xhat1, f, target, _rows8([g2, ln1_g, ln1_b, ln2_g, ln2_b]))

    dgg, duu = _glu_bwd(df, w_down, gg, uu)
    dw_down = _matmul(hh, df, name="dw_down", ta=True, tm=512, tn=1024, tk=n, out_dtype=BF16)
    dw_gate = _matmul(u2, dgg, name="dw_gate", ta=True, tm=1024, tn=512, tk=n, out_dtype=BF16)
    dw_up = _matmul(u2, duu, name="dw_up", ta=True, tm=1024, tn=512, tk=n, out_dtype=BF16)
    zero = grads_out(0, [dw_down, dw_gate, dw_up])
    du2 = _du2(dgg, duu, w_gate, w_up)
    dxp, da, s_ln1 = _ln1_bwd(du2, dr2, xhat1, rstd1, a, _rows8([sc2, ln1_g, ln1_b, g1]) + zero)

    dheads = _matmul(da, w_out, name="d_heads", tb=True, tm=1024, tn=1024, tk=2048, out_dtype=BF16)
    dw_out = _matmul(heads, da, name="dw_out", ta=True, tm=1024, tn=1024, tk=n, out_dtype=BF16)
    zero = grads_out(1, [dw_out])
    delta = _delta_rows(dheads, heads)
    dqa, dka, dva, dsink = _attn_a_bwd(qa, ka, va, sink_b + zero, dheads, n)
    dqb, dkb, dvb = _attn_b_bwd(qb, kb, vb, dheads, lse, delta, n)
    dh_all, s_gain = _qkv_bwd_post(h_all, cos, sin_a, sin_b, q_g, k_g, dqa, dka, dva, dqb, dkb, dvb, n)
    dw_in = _matmul(u_all, dh_all, name="dw_in", ta=True, tm=1024, tn=1024, tk=n + c, out_dtype=BF16)
    zero = grads_out(2, [dw_in])
    du_all = _matmul(dh_all, w_in, name="d_u1", tb=True, tm=_fit(n + c, 1088), tn=1024, tk=IN_WIDTH, out_dtype=F32,
                     after=zero.reshape(1, 1))
    grad_x, s_mod1 = _mod1_bwd(du_all, dxp, x, ctx, _rows8([sc1]) + zero)

    dsink_row = jnp.concatenate([dsink[0, 0:GROUP, 0], dsink[1, 0:GROUP, 0]]).reshape(1, HEADS_A)
    misc = jnp.concatenate([s_gain[0:1], s_gain[1:2], dsink_row,
                            jnp.zeros((1, d - 2 * HEAD_DIM - HEADS_A), F32)], axis=1)
    partial = jnp.concatenate([
        s_mod1[1:2], s_mod1[0:1], s_ln1[4:5],
        s_ln1[1:2], s_ln1[0:1], s_ln2[2:3],
        s_mod1[3:4], s_mod1[2:3],
        s_ln1[2:3], s_ln1[3:4], s_ln2[0:1], s_ln2[1:2],
        s_ln2[3:4], misc, jnp.zeros((2, d), F32)], axis=0)
    return grad_x, partial


ANY = pl.BlockSpec(memory_space=pl.ANY)
VMEM_FULL = pl.BlockSpec(memory_space=pltpu.VMEM)
N_CHIP_PEERS = 3


def _me():
    return lax.axis_index("x"), lax.axis_index("y"), lax.axis_index("c")


def _other_chips(x, y):
    return [(1 - x, y), (x, 1 - y), (1 - x, 1 - y)]


def _shard_of(chip):
    return 2 * chip[0] + chip[1]


def _dev_index(x, y, c):
    return 4 * x + 2 * y + c


def _rcopy(src, dst, send_sems, recv_sems, k, dev):
    return pltpu.make_async_remote_copy(src_ref=src, dst_ref=dst, send_sem=send_sems.at[k], recv_sem=recv_sems.at[k],
                                        device_id=dev, device_id_type=MESH)


BIG = (("w_in", (D_MODEL, IN_WIDTH), 1), ("w_out", (MIX_WIDTH, D_MODEL), 0), ("w_gate", (D_MODEL, FFN), 1),
       ("w_up", (D_MODEL, FFN), 1), ("w_down", (FFN, D_MODEL), 0))


def _sub(ref, axis, idx, size):
    start = pl.multiple_of(idx * size, size)
    return ref.at[pl.ds(start, size), :] if axis == 0 else ref.at[:, pl.ds(start, size)]


def _shape_div(shape, axis, parts):
    return tuple(s // parts if a == axis else s for a, s in enumerate(shape))


def _piece(a, ref, shard, half):
    _, full, axis = BIG[a]
    view = _sub(ref, axis, shard, full[axis] // N_SHARD)
    return _sub(view, 1 - axis, half, full[1 - axis] // 2)


HBM = pl.BlockSpec(memory_space=pltpu.HBM)
SEM = pl.BlockSpec(memory_space=pltpu.SEMAPHORE)
EFFECT = pltpu.SideEffectType.DATAFLOW_SIDE_EFFECTING
BIG_INDEX = {name: a for a, (name, _, _) in enumerate(BIG)}


def _in_hbm(arr):
    return pltpu.with_memory_space_constraint(arr, pltpu.HBM)


def _gather_start(tag, arrs, bufs, prev):
    n_arr = len(arrs)

    def body(*refs):
        ins = refs[:n_arr]
        send_sems, recv_sems = refs[n_arr + 1], refs[n_arr + 2]
        token = refs[-1]
        x, y, c = _me()
        s_me = _shard_of((x, y))
        for i, a in enumerate(arrs):
            mine = _piece(a, ins[i], s_me, c)
            for j, chip in enumerate(_other_chips(x, y)):
                _rcopy(mine, mine, send_sems, recv_sems, N_CHIP_PEERS * i + j, (*chip, c)).start()
        token[...] = jnp.zeros_like(token)

    n_sem = N_CHIP_PEERS * n_arr
    outs = pl.pallas_call(
        body, name="gather_start_" + tag,
        in_specs=[HBM] * n_arr + [ANY],
        out_specs=[SEM, SEM] + [HBM] * n_arr + [VMEM_FULL],
        out_shape=[pltpu.SemaphoreType.DMA((n_sem,)), pltpu.SemaphoreType.DMA((n_sem,))]
        + [pltpu.HBM(BIG[a][1], BF16) for a in arrs] + [jax.ShapeDtypeStruct((8, HEAD_DIM), F32)],
        input_output_aliases={i: 2 + i for i in range(n_arr)},
        compiler_params=pltpu.CompilerParams(has_side_effects=EFFECT),
    )(*[_in_hbm(b) for b in bufs], prev)
    return outs[0], outs[1], list(outs[2:2 + n_arr]), outs[-1]


def _gather_wait(tag, arrs, send_sems, recv_sems, bufs, after):
    n_arr = len(arrs)

    def body(*refs):
        ins = refs[:n_arr]
        send_sems_, recv_sems_ = refs[n_arr], refs[n_arr + 1]
        x, y, c = _me()
        s_me = _shard_of((x, y))
        for i, a in enumerate(arrs):
            mine = _piece(a, ins[i], s_me, c)
            for j, chip in enumerate(_other_chips(x, y)):
                landed = _piece(a, ins[i], _shard_of(chip), c)
                cp = _rcopy(mine, landed, send_sems_, recv_sems_, N_CHIP_PEERS * i + j, (*chip, c))
                cp.wait_send()
                cp.wait_recv()

    outs = pl.pallas_call(
        body, name="gather_wait_" + tag,
        in_specs=[HBM] * n_arr + [SEM, SEM, ANY],
        out_specs=[HBM] * n_arr,
        out_shape=[pltpu.HBM(BIG[a][1], BF16) for a in arrs],
        input_output_aliases={i: i for i in range(n_arr)},
        compiler_params=pltpu.CompilerParams(has_side_effects=EFFECT),
    )(*bufs, send_sems, recv_sems, after)
    return list(outs)


def _gather_forward(tag, arrs, bufs):
    n_arr = len(arrs)

    def body(*refs):
        outs = refs[n_arr:2 * n_arr]
        send_sems, recv_sems = refs[2 * n_arr:]
        x, y, c = _me()
        sibling = (x, y, 1 - c)
        chips = _other_chips(x, y)
        copies = []
        for i, a in enumerate(arrs):
            for j, chip in enumerate(chips):
                landed = _piece(a, outs[i], _shard_of(chip), c)
                cp = _rcopy(landed, landed, send_sems, recv_sems, N_CHIP_PEERS * i + j, sibling)
                cp.start()
                copies.append(cp)
        for i, a in enumerate(arrs):
            for j, chip in enumerate(chips):
                other = _piece(a, outs[i], _shard_of(chip), 1 - c)
                _rcopy(other, other, send_sems, recv_sems, N_CHIP_PEERS * i + j, sibling).wait_recv()
        for cp in copies:
            cp.wait_send()

    n_sem = N_CHIP_PEERS * n_arr
    return list(pl.pallas_call(
        body, name="gather_forward_" + tag,
        in_specs=[ANY] * n_arr, out_specs=[ANY] * n_arr,
        out_shape=[jax.ShapeDtypeStruct(BIG[a][1], BF16) for a in arrs],
        input_output_aliases={i: i for i in range(n_arr)},
        scratch_shapes=[pltpu.SemaphoreType.DMA((n_sem,)), pltpu.SemaphoreType.DMA((n_sem,))],
    )(*bufs))


def _forward_start(tag, arrs, bufs):
    n_arr = len(arrs)

    def body(*refs):
        ins = refs[:n_arr]
        send_sems, recv_sems = refs[n_arr], refs[n_arr + 1]
        token = refs[-1]
        x, y, c = _me()
        for i, a in enumerate(arrs):
            for j, chip in enumerate(_other_chips(x, y)):
                landed = _piece(a, ins[i], _shard_of(chip), c)
                _rcopy(landed, landed, send_sems, recv_sems, N_CHIP_PEERS * i + j, (x, y, 1 - c)).start()
        token[...] = jnp.zeros_like(token)

    n_sem = N_CHIP_PEERS * n_arr
    outs = pl.pallas_call(
        body, name="gather_forward_start_" + tag,
        in_specs=[HBM] * n_arr,
        out_specs=[SEM, SEM] + [HBM] * n_arr + [VMEM_FULL],
        out_shape=[pltpu.SemaphoreType.DMA((n_sem,)), pltpu.SemaphoreType.DMA((n_sem,))]
        + [pltpu.HBM(BIG[a][1], BF16) for a in arrs] + [jax.ShapeDtypeStruct((8, HEAD_DIM), F32)],
        input_output_aliases={i: 2 + i for i in range(n_arr)},
        compiler_params=pltpu.CompilerParams(has_side_effects=EFFECT),
    )(*bufs)
    return outs[0], outs[1], list(outs[2:2 + n_arr]), outs[-1]


def _forward_wait(tag, arrs, send_sems, recv_sems, bufs, after):
    n_arr = len(arrs)

    def body(*refs):
        ins = refs[:n_arr]
        send_sems_, recv_sems_ = refs[n_arr], refs[n_arr + 1]
        x, y, c = _me()
        for i, a in enumerate(arrs):
            for j, chip in enumerate(_other_chips(x, y)):
                mine = _piece(a, ins[i], _shard_of(chip), c)
                other = _piece(a, ins[i], _shard_of(chip), 1 - c)
                cp = _rcopy(mine, other, send_sems_, recv_sems_, N_CHIP_PEERS * i + j, (x, y, 1 - c))
                cp.wait_send()
                cp.wait_recv()

    outs = pl.pallas_call(
        body, name="gather_forward_wait_" + tag,
        in_specs=[HBM] * n_arr + [SEM, SEM, ANY],
        out_specs=[HBM] * n_arr,
        out_shape=[pltpu.HBM(BIG[a][1], BF16) for a in arrs],
        input_output_aliases={i: i for i in range(n_arr)},
        compiler_params=pltpu.CompilerParams(has_side_effects=EFFECT),
    )(*bufs, send_sems, recv_sems, after)
    return list(outs)


def _peers(x, y, c):
    return [(x ^ (mask >> 2), y ^ ((mask >> 1) & 1), c ^ (mask & 1)) for mask in range(1, N_DEV)]


def _received_shape(a):
    _, full, axis = BIG[a]
    return (N_DEV - 1,) + _shape_div(_shape_div(full, 1 - axis, 2), axis, N_SHARD)


def _pieces_start(tag, arrs, dws):
    n_arr = len(arrs)

    def body(*refs):
        srcs, lands = refs[:n_arr], refs[n_arr:2 * n_arr]
        send_sems, recv_sems = refs[2 * n_arr], refs[2 * n_arr + 1]
        token = refs[-1]
        x, y, c = _me()
        for i, a in enumerate(arrs):
            for k, peer in enumerate(_peers(x, y, c)):
                src = _piece(a, srcs[i], _shard_of(peer[:2]), peer[2])
                _rcopy(src, lands[i].at[k], send_sems, recv_sems, (N_DEV - 1) * i + k, peer).start()
        token[...] = jnp.zeros_like(token)

    n_sem = (N_DEV - 1) * n_arr
    lands = [_in_hbm(lax.empty(_received_shape(a), BF16)) for a in arrs]
    outs = pl.pallas_call(
        body, name="grad_pieces_start_" + tag,
        in_specs=[HBM] * (2 * n_arr),
        out_specs=[SEM, SEM] + [HBM] * (2 * n_arr) + [VMEM_FULL],
        out_shape=[pltpu.SemaphoreType.DMA((n_sem,)), pltpu.SemaphoreType.DMA((n_sem,))]
        + [pltpu.HBM(BIG[a][1], BF16) for a in arrs] + [pltpu.HBM(_received_shape(a), BF16) for a in arrs]
        + [jax.ShapeDtypeStruct((8, HEAD_DIM), F32)],
        input_output_aliases={i: 2 + i for i in range(2 * n_arr)},
        compiler_params=pltpu.CompilerParams(has_side_effects=EFFECT),
    )(*[_in_hbm(dw) for dw in dws], *lands)
    return outs[0], outs[1], list(outs[2:2 + n_arr]), list(outs[2 + n_arr:2 + 2 * n_arr]), outs[-1]


def _pieces_wait(tag, arrs, send_sems, recv_sems, dws, lands, after):
    n_arr = len(arrs)

    def body(*refs):
        srcs, lands_ = refs[:n_arr], refs[n_arr:2 * n_arr]
        send_sems_, recv_sems_ = refs[2 * n_arr], refs[2 * n_arr + 1]
        x, y, c = _me()
        for i, a in enumerate(arrs):
            for k, peer in enumerate(_peers(x, y, c)):
                src = _piece(a, srcs[i], _shard_of(peer[:2]), peer[2])
                cp = _rcopy(src, lands_[i].at[k], send_sems_, recv_sems_, (N_DEV - 1) * i + k, peer)
                cp.wait_send()
                cp.wait_recv()

    outs = pl.pallas_call(
        body, name="grad_pieces_wait_" + tag,
        in_specs=[HBM] * (2 * n_arr) + [SEM, SEM, ANY],
        out_specs=[HBM] * (2 * n_arr),
        out_shape=[pltpu.HBM(BIG[a][1], BF16) for a in arrs] + [pltpu.HBM(_received_shape(a), BF16) for a in arrs],
        input_output_aliases={i: i for i in range(2 * n_arr)},
        compiler_params=pltpu.CompilerParams(has_side_effects=EFFECT),
    )(*dws, *lands, send_sems, recv_sems, after)
    return list(outs[:n_arr]), list(outs[n_arr:])


def _join_start(tag, g_halves):
    n_arr = len(g_halves)

    def body(*refs):
        srcs, lands = refs[:n_arr], refs[n_arr:2 * n_arr]
        send_sems, recv_sems = refs[2 * n_arr], refs[2 * n_arr + 1]
        token = refs[-1]
        x, y, c = _me()
        for i in range(n_arr):
            _rcopy(srcs[i], lands[i], send_sems, recv_sems, i, (x, y, 1 - c)).start()
        token[...] = jnp.zeros_like(token)

    shapes = [pltpu.HBM(g.shape, F32) for g in g_halves]
    outs = pl.pallas_call(
        body, name="grad_join_start_" + tag,
        in_specs=[HBM] * (2 * n_arr),
        out_specs=[SEM, SEM] + [HBM] * (2 * n_arr) + [VMEM_FULL],
        out_shape=[pltpu.SemaphoreType.DMA((n_arr,)), pltpu.SemaphoreType.DMA((n_arr,))] + shapes + shapes
        + [jax.ShapeDtypeStruct((8, HEAD_DIM), F32)],
        input_output_aliases={i: 2 + i for i in range(2 * n_arr)},
        compiler_params=pltpu.CompilerParams(has_side_effects=EFFECT),
    )(*[_in_hbm(g) for g in g_halves], *[_in_hbm(lax.empty(g.shape, F32)) for g in g_halves])
    return outs[0], outs[1], list(outs[2:2 + n_arr]), list(outs[2 + n_arr:2 + 2 * n_arr]), outs[-1]


def _join_wait(tag, send_sems, recv_sems, g_halves, lands, after):
    n_arr = len(g_halves)

    def body(*refs):
        srcs, lands_ = refs[:n_arr], refs[n_arr:2 * n_arr]
        send_sems_, recv_sems_ = refs[2 * n_arr], refs[2 * n_arr + 1]
        x, y, c = _me()
        for i in range(n_arr):
            cp = _rcopy(srcs[i], lands_[i], send_sems_, recv_sems_, i, (x, y, 1 - c))
            cp.wait_send()
            cp.wait_recv()

    shapes = [pltpu.HBM(g.shape, F32) for g in g_halves]
    outs = pl.pallas_call(
        body, name="grad_join_wait_" + tag,
        in_specs=[HBM] * (2 * n_arr) + [SEM, SEM, ANY],
        out_specs=[HBM] * (2 * n_arr),
        out_shape=shapes + shapes,
        input_output_aliases={i: i for i in range(2 * n_arr)},
        compiler_params=pltpu.CompilerParams(has_side_effects=EFFECT),
    )(*g_halves, *lands, send_sems, recv_sems, after)
    return list(outs[:n_arr]), list(outs[n_arr:])


def _piece_sum(a, dw, shard, core, received):
    name, full, axis = BIG[a]
    rows, cols = _received_shape(a)[1:]
    tr = _fit(rows, ROW_TILE)
    nbr = rows // tr

    def body(w_ref, dw_ref, rec_ref, o_ref):
        acc = dw_ref[...].astype(F32)
        for k in range(N_DEV - 1):
            acc = acc + rec_ref[k].astype(F32)
        o_ref[...] = acc

    if axis == 0:
        own = pl.BlockSpec((tr, cols), lambda i, w: (w[0] * nbr + i, w[1]))
    else:
        own = pl.BlockSpec((tr, cols), lambda i, w: (w[1] * nbr + i, w[0]))
    return pl.pallas_call(
        body, name="grad_sum_pieces_" + name,
        grid_spec=pltpu.PrefetchScalarGridSpec(
            num_scalar_prefetch=1, grid=(nbr,),
            in_specs=[own, pl.BlockSpec((N_DEV - 1, tr, cols), lambda i, w: (0, i, 0))],
            out_specs=pl.BlockSpec((tr, cols), lambda i, w: (i, 0))),
        out_shape=jax.ShapeDtypeStruct((rows, cols), F32),
        compiler_params=_params(("parallel",)),
    )(jnp.stack([shard, core]).astype(jnp.int32), dw, received)


def _scatter_begin(tag, arrs, dws):
    send_sems, recv_sems, dws, lands, token = _pieces_start(tag, arrs, dws)
    return (send_sems, recv_sems, dws, lands), token[0, 0]


def _scatter_reduce(tag, arrs, state, after):
    x, y, c = _me()
    send_sems, recv_sems, dws, lands = state
    dws, lands = _pieces_wait(tag, arrs, send_sems, recv_sems, dws, lands, after)
    g_own = [_piece_sum(a, dw, _shard_of((x, y)), c, r) for a, dw, r in zip(arrs, dws, lands)]
    send_sems, recv_sems, g_own, lands, token = _join_start(tag, g_own)
    return (send_sems, recv_sems, g_own, lands), token


def _scatter_end(tag, state, after):
    return _join_wait(tag, *state, after)


def _rows_start(tag, block):
    r, d = block.shape

    def body(src, land, send_sems, recv_sems, src_thru, land_thru, token):
        x, y, c = _me()
        for k, peer in enumerate(_peers(x, y, c)):
            _rcopy(src, land.at[_dev_index(x, y, c)], send_sems, recv_sems, k, peer).start()
        token[...] = jnp.zeros_like(token)

    outs = pl.pallas_call(
        body, name="rows_start_" + tag,
        in_specs=[HBM, HBM],
        out_specs=[SEM, SEM, HBM, HBM, VMEM_FULL],
        out_shape=[pltpu.SemaphoreType.DMA((N_DEV - 1,)), pltpu.SemaphoreType.DMA((N_DEV - 1,)),
                   pltpu.HBM((r, d), F32), pltpu.HBM((N_DEV, r, d), F32), jax.ShapeDtypeStruct((8, HEAD_DIM), F32)],
        input_output_aliases={0: 2, 1: 3},
        compiler_params=pltpu.CompilerParams(has_side_effects=EFFECT),
    )(_in_hbm(block), _in_hbm(lax.empty((N_DEV, r, d), F32)))
    return outs[0], outs[1], outs[2], outs[3], outs[4]


def _rows_wait(tag, send_sems, recv_sems, block, land, after):
    r, d = block.shape

    def body(src, land_, send_sems_, recv_sems_, after_ref, src_thru, land_thru):
        x, y, c = _me()
        for k, peer in enumerate(_peers(x, y, c)):
            cp = _rcopy(src, land_.at[_dev_index(*peer)], send_sems_, recv_sems_, k, peer)
            cp.wait_send()
            cp.wait_recv()

    outs = pl.pallas_call(
        body, name="rows_wait_" + tag,
        in_specs=[HBM, HBM, SEM, SEM, ANY],
        out_specs=[HBM, HBM],
        out_shape=[pltpu.HBM((r, d), F32), pltpu.HBM((N_DEV, r, d), F32)],
        input_output_aliases={0: 0, 1: 1},
        compiler_params=pltpu.CompilerParams(has_side_effects=EFFECT),
    )(block, land, send_sems, recv_sems, after)
    me = _dev_index(*_me())
    return lax.dynamic_update_slice(outs[1], outs[0][None], (me, 0, 0))


ADA_ROWS = 80
ADA_W = 6 * D_MODEL // N_SHARD


def _ada_forward(c_block, cctx_block, w_ada, b_shard):
    d = c_block.shape[1]

    def body(c_ref, cc_ref, w_ref, b_ref, act_ref, mods_ref, raw, mloc, send_sems, recv_sems):
        x, y, c = _me()
        me = _dev_index(x, y, c)
        s_me = _shard_of((x, y))
        raw[72:ADA_ROWS, :] = jnp.zeros((ADA_ROWS - 72, d), F32)
        raw[pl.ds(pl.multiple_of(me * 8, 8), 8), :] = c_ref[...]
        raw[64:72, :] = cc_ref[...]
        sends = []
        for mask in range(1, N_DEV):
            peer = (x ^ (mask >> 2), y ^ ((mask >> 1) & 1), c ^ (mask & 1))
            cp = _rcopy(c_ref, raw.at[pl.ds(pl.multiple_of(me * 8, 8), 8), :], send_sems, recv_sems, mask - 1, peer)
            cp.start()
            sends.append(cp)
        for mask in range(1, N_DEV):
            peer = (x ^ (mask >> 2), y ^ ((mask >> 1) & 1), c ^ (mask & 1))
            landed = raw.at[pl.ds(pl.multiple_of(_dev_index(*peer) * 8, 8), 8), :]
            _rcopy(landed, landed, send_sems, recv_sems, mask - 1, peer).wait_recv()
        v = raw[...]
        act = v * jax.nn.sigmoid(v)
        act_ref[...] = act
        mloc[...] = lax.dot_general(act.astype(BF16), w_ref[...].astype(BF16), NN,
                                    preferred_element_type=F32) + b_ref[...]
        mods_ref[s_me, 0:8, :] = mloc[pl.ds(pl.multiple_of(me * 8, 8), 8), :]
        mods_ref[s_me, 8:16, :] = mloc[64:72, :]
        base = N_DEV - 1
        for j, chip in enumerate(_other_chips(x, y)):
            peer = (*chip, c)
            rows = mloc.at[pl.ds(pl.multiple_of(_dev_index(*peer) * 8, 8), 8), :]
            cp = _rcopy(rows, mods_ref.at[s_me, 0:8, :], send_sems, recv_sems, base + 2 * j, peer)
            cp.start()
            sends.append(cp)
            cp = _rcopy(mloc.at[64:72, :], mods_ref.at[s_me, 8:16, :], send_sems, recv_sems, base + 2 * j + 1, peer)
            cp.start()
            sends.append(cp)
        for j, chip in enumerate(_other_chips(x, y)):
            for part in range(2):
                landed = mods_ref.at[_shard_of(chip), 8 * part:8 * part + 8, :]
                _rcopy(landed, landed, send_sems, recv_sems, base + 2 * j + part, (*chip, c)).wait_recv()
        for cp in sends:
            cp.wait_send()

    n_sem = N_DEV - 1 + 2 * N_CHIP_PEERS
    return pl.pallas_call(
        body, name="ada_forward",
        in_specs=[VMEM_FULL] * 4, out_specs=[VMEM_FULL, VMEM_FULL],
        out_shape=[jax.ShapeDtypeStruct((ADA_ROWS, d), F32), jax.ShapeDtypeStruct((N_SHARD, 16, ADA_W), F32)],
        scratch_shapes=[pltpu.VMEM((ADA_ROWS, d), F32), pltpu.VMEM((ADA_ROWS, ADA_W), F32),
                        pltpu.SemaphoreType.DMA((n_sem,)), pltpu.SemaphoreType.DMA((n_sem,))],
        compiler_params=pltpu.CompilerParams(vmem_limit_bytes=VMEM_LIMIT),
    )(c_block, cctx_block, w_ada, b_shard)


def _small_reduce(gathered):
    d = gathered.shape[2]

    def body(g_ref, o_ref):
        tot = g_ref[0]
        for i in range(1, N_DEV):
            tot = tot + g_ref[i]
        o_ref[...] = tot
        o_ref[0:2, :] = tot[0:2] + tot[6:8]
        o_ref[12:13, :] = jnp.broadcast_to(jnp.sum(tot[12:13], axis=1, keepdims=True), (1, d))

    return pl.pallas_call(body, name="small_reduce", in_specs=[VMEM_FULL], out_specs=VMEM_FULL,
                          out_shape=jax.ShapeDtypeStruct((16, d), F32))(gathered)


def _cctx_grad(gathered, c_ctx):
    d = gathered.shape[2]

    def body(g_ref, c_ref, o_ref):
        tot = g_ref[0, 0:1, :]
        for chip in range(1, N_SHARD):
            tot = tot + g_ref[2 * chip, 0:1, :]
        v = c_ref[...]
        sig = jax.nn.sigmoid(v)
        o_ref[...] = tot * (sig * (1.0 + v * (1.0 - sig)))

    return pl.pallas_call(body, name="cctx_grad", in_specs=[VMEM_FULL, VMEM_FULL], out_specs=VMEM_FULL,
                          out_shape=jax.ShapeDtypeStruct((1, d), F32))(gathered, c_ctx.reshape(1, d))


def _cast_into_full(w, shard, full, axis, name):
    r, cdim = w.shape
    tr = _fit(r, ROW_TILE)
    nbr = r // tr

    def body(s_ref, w_ref, o_ref):
        o_ref[...] = w_ref[...].astype(BF16)

    if axis == 0:
        out_spec = pl.BlockSpec((tr, cdim), lambda i, s: (s[0] * nbr + i, 0))
    else:
        out_spec = pl.BlockSpec((tr, cdim), lambda i, s: (i, s[0]))
    return pl.pallas_call(
        body, name=name,
        grid_spec=pltpu.PrefetchScalarGridSpec(
            num_scalar_prefetch=1, grid=(nbr,), in_specs=[pl.BlockSpec((tr, cdim), lambda i, s: (i, 0))],
            out_specs=out_spec),
        out_shape=jax.ShapeDtypeStruct(full, BF16), compiler_params=_params(("parallel",)),
    )(shard.reshape(1).astype(jnp.int32), w)


def _adamw_halves(w, g_own, g_other, m, v, core, axis, name):
    r, cdim = w.shape
    hr, hc = (r // 2, cdim) if axis == 1 else (r, cdim // 2)
    assert g_own.shape == (hr, hc) and g_other.shape == (hr, hc)
    tr = _fit(hr, 256)
    nb = hr // tr
    c1 = 1.0 - ADAM_B1 ** ADAM_STEP
    c2 = 1.0 - ADAM_B2 ** ADAM_STEP

    def body(c_ref, w_ref, go_ref, gt_ref, m_ref, v_ref, g_ref, d_ref, nm_ref, nv_ref):
        gv = jnp.where(pl.program_id(0) == c_ref[0], go_ref[...], gt_ref[...])
        nm = ADAM_B1 * m_ref[...] + (1.0 - ADAM_B1) * gv
        nv = ADAM_B2 * v_ref[...] + (1.0 - ADAM_B2) * (gv * gv)
        g_ref[...] = gv
        nm_ref[...] = nm
        nv_ref[...] = nv
        d_ref[...] = -ADAM_LR * ((nm / c1) / (jnp.sqrt(nv / c2) + ADAM_EPS) + ADAM_WD * w_ref[...])

    if axis == 1:
        big = pl.BlockSpec((tr, hc), lambda p, i, c: (p * nb + i, 0))
    else:
        big = pl.BlockSpec((tr, hc), lambda p, i, c: (i, p))
    own = pl.BlockSpec((tr, hc), lambda p, i, c: (jnp.where(p == c[0], i, 0), 0))
    other = pl.BlockSpec((tr, hc), lambda p, i, c: (jnp.where(p == c[0], 0, i), 0))
    sh = jax.ShapeDtypeStruct((r, cdim), F32)
    return pl.pallas_call(
        body, name=name,
        grid_spec=pltpu.PrefetchScalarGridSpec(
            num_scalar_prefetch=1, grid=(2, nb), in_specs=[big, own, other, big, big], out_specs=[big] * 4),
        out_shape=[sh] * 4, compiler_params=_params(("parallel", "parallel")),
    )(core.reshape(1).astype(jnp.int32), w, g_own, g_other, m, v)


def _adamw(w, g, m, v, name):
    r, cdim = w.shape
    tr = _fit(r, 128) if r % (ROW_TILE // 4) == 0 else r
    c1 = 1.0 - ADAM_B1 ** ADAM_STEP
    c2 = 1.0 - ADAM_B2 ** ADAM_STEP

    def body(w_ref, g_ref, m_ref, v_ref, d_ref, nm_ref, nv_ref):
        gv = g_ref[...]
        nm = ADAM_B1 * m_ref[...] + (1.0 - ADAM_B1) * gv
        nv = ADAM_B2 * v_ref[...] + (1.0 - ADAM_B2) * (gv * gv)
        nm_ref[...] = nm
        nv_ref[...] = nv
        d_ref[...] = -ADAM_LR * ((nm / c1) / (jnp.sqrt(nv / c2) + ADAM_EPS) + ADAM_WD * w_ref[...])

    spec = pl.BlockSpec((tr, cdim), lambda i: (i, 0))
    sh = jax.ShapeDtypeStruct((r, cdim), F32)
    return pl.pallas_call(body, name=name, grid=(r // tr,), in_specs=[spec] * 4, out_specs=[spec] * 3,
                          out_shape=[sh, sh, sh], compiler_params=_params(("parallel",)))(w, g, m, v)


def _adamw_small(ws, gs, ms, vs):
    k = len(ws)
    c1 = 1.0 - ADAM_B1 ** ADAM_STEP
    c2 = 1.0 - ADAM_B2 ** ADAM_STEP

    def body(*refs):
        w_refs, g_refs, m_refs, v_refs = refs[0:k], refs[k:2 * k], refs[2 * k:3 * k], refs[3 * k:4 * k]
        d_refs, nm_refs, nv_refs = refs[4 * k:5 * k], refs[5 * k:6 * k], refs[6 * k:7 * k]
        for i in range(k):
            gv = g_refs[i][...]
            nm = ADAM_B1 * m_refs[i][...] + (1.0 - ADAM_B1) * gv
            nv = ADAM_B2 * v_refs[i][...] + (1.0 - ADAM_B2) * (gv * gv)
            nm_refs[i][...] = nm
            nv_refs[i][...] = nv
            d_refs[i][...] = -ADAM_LR * ((nm / c1) / (jnp.sqrt(nv / c2) + ADAM_EPS) + ADAM_WD * w_refs[i][...])

    shapes = [jax.ShapeDtypeStruct(w.shape, F32) for w in ws]
    outs = pl.pallas_call(body, name="adamw_small", in_specs=[VMEM_FULL] * (4 * k), out_specs=[VMEM_FULL] * (3 * k),
                          out_shape=shapes * 3)(*ws, *gs, *ms, *vs)
    return outs[0:k], outs[k:2 * k], outs[2 * k:3 * k]


SMALL = (("c_ctx", D_MODEL), ("b_ada", 6 * D_MODEL), ("q_norm_g", HEAD_DIM), ("k_norm_g", HEAD_DIM),
         ("sink_logit", HEADS_A), ("ln1_g", D_MODEL), ("ln1_b", D_MODEL), ("ln2_g", D_MODEL), ("ln2_b", D_MODEL))
WEIGHT_ORDER = ("c_ctx", "w_ada", "b_ada", "w_in", "q_norm_g", "k_norm_g", "sink_logit", "w_out", "ln1_g", "ln1_b",
                "w_gate", "w_up", "w_down", "ln2_g", "ln2_b")


def kernel(x, c, ctx, c_ctx, w_ada, b_ada, w_in, q_norm_g, k_norm_g, sink_logit, w_out, ln1_g, ln1_b, w_gate, w_up, w_down, ln2_g, ln2_b, loss_target, m_c_ctx, m_w_ada, m_b_ada, m_w_in, m_q_norm_g, m_k_norm_g, m_sink_logit, m_w_out, m_ln1_g, m_ln1_b, m_w_gate, m_w_up, m_w_down, m_ln2_g, m_ln2_b, v_c_ctx, v_w_ada, v_b_ada, v_w_in, v_q_norm_g, v_k_norm_g, v_sink_logit, v_w_out, v_ln1_g, v_ln1_b, v_w_gate, v_w_up, v_w_down, v_ln2_g, v_ln2_b):
    d = D_MODEL
    w = dict(c_ctx=c_ctx, w_ada=w_ada[0], b_ada=b_ada, w_in=w_in[0], q_norm_g=q_norm_g, k_norm_g=k_norm_g,
             sink_logit=sink_logit, w_out=w_out[0], ln1_g=ln1_g, ln1_b=ln1_b, w_gate=w_gate[0], w_up=w_up[0],
             w_down=w_down[0], ln2_g=ln2_g, ln2_b=ln2_b)
    m = dict(c_ctx=m_c_ctx, w_ada=m_w_ada[0], b_ada=m_b_ada, w_in=m_w_in[0], q_norm_g=m_q_norm_g, k_norm_g=m_k_norm_g,
             sink_logit=m_sink_logit, w_out=m_w_out[0], ln1_g=m_ln1_g, ln1_b=m_ln1_b, w_gate=m_w_gate[0],
             w_up=m_w_up[0], w_down=m_w_down[0], ln2_g=m_ln2_g, ln2_b=m_ln2_b)
    v = dict(c_ctx=v_c_ctx, w_ada=v_w_ada[0], b_ada=v_b_ada, w_in=v_w_in[0], q_norm_g=v_q_norm_g, k_norm_g=v_k_norm_g,
             sink_logit=v_sink_logit, w_out=v_w_out[0], ln1_g=v_ln1_g, ln1_b=v_ln1_b, w_gate=v_w_gate[0],
             w_up=v_w_up[0], w_down=v_w_down[0], ln2_g=v_ln2_g, ln2_b=v_ln2_b)
    mx, my, mc = _me()
    s_me = _shard_of((mx, my))
    me = _dev_index(mx, my, mc)
    pad8 = lambda row: jnp.concatenate([row.reshape(1, -1), jnp.zeros((7, row.size), F32)], axis=0)

    b_shard = lax.dynamic_slice(b_ada, (0, s_me * ADA_W), (1, ADA_W))
    act, mods4 = _ada_forward(pad8(c), pad8(c_ctx), w["w_ada"], b_shard)

    gathers = []
    prev, shard = mods4, s_me
    for k, names in enumerate(W_GROUPS):
        arrs = tuple(BIG_INDEX[name] for name in names)
        bufs = [_cast_into_full(w[name], shard, BIG[a][1], BIG[a][2], "cast_" + name) for name, a in zip(names, arrs)]
        send_sems, recv_sems, thru, prev = _gather_start("g%d" % k, arrs, bufs, prev)
        shard = s_me + prev[0, 0].astype(jnp.int32)
        gathers.append((arrs, send_sems, recv_sems, thru))

    forwards = {}

    def prefetch(k, after):
        arrs, send_sems, recv_sems, thru = gathers[k]
        landed = _gather_wait("g%d" % k, arrs, send_sems, recv_sems, thru, after)
        fwd_send, fwd_recv, landed, token = _forward_start("g%d" % k, arrs, landed)
        forwards[k] = (fwd_send, fwd_recv, landed)
        return token[0, 0]

    def weights(k, after):
        arrs, send_sems, recv_sems, thru = gathers[k]
        if k in forwards:
            return _forward_wait("g%d" % k, arrs, *forwards[k], after)
        landed = _gather_wait("g%d" % k, arrs, send_sems, recv_sems, thru, after)
        return _gather_forward("g%d" % k, arrs, landed)

    mod = jnp.transpose(mods4[:, 0:1, :], (1, 0, 2)).reshape(1, 6 * d) + prev[0, 0]
    mod_ctx = jnp.transpose(mods4[:, 8:9, :], (1, 0, 2)).reshape(1, 6 * d)

    scatters = {}

    def grads_out(k, dws):
        arrs = tuple(BIG_INDEX[name] for name in G_GROUPS[k])
        scatters[k], zero = _scatter_begin("g%d" % k, arrs, dws)
        return zero

    grad_x, partial = _layer_fwd_bwd(x[0], ctx[0], loss_target[0], mod, mod_ctx, weights, prefetch, grads_out,
                                     q_norm_g, k_norm_g, sink_logit, ln1_g, ln1_b, ln2_g, ln2_b)
    grads, delta, new_m, new_v = {}, {}, {}, {}

    p_send, p_recv, partial, p_land, after = _rows_start("partials", partial)
    joins = []
    for k, names in enumerate(G_GROUPS):
        arrs = tuple(BIG_INDEX[name] for name in names)
        state, after = _scatter_reduce("g%d" % k, arrs, scatters[k], after)
        joins.append(state)
    for k, names in enumerate(G_GROUPS):
        g_own, g_other = _scatter_end("g%d" % k, joins[k], after)
        for name, own, other in zip(names, g_own, g_other):
            grads[name], delta[name], new_m[name], new_v[name] = _adamw_halves(
                w[name], own, other, m[name], v[name], mc, BIG[BIG_INDEX[name]][2], "adamw_" + name)
            after = new_v[name]

    gathered = _rows_wait("partials", p_send, p_recv, partial, p_land, after)
    tot = _small_reduce(gathered)
    grads["b_ada"] = tot[0:6].reshape(1, 6 * d)
    grads["ln1_g"], grads["ln1_b"], grads["ln2_g"], grads["ln2_b"] = tot[8:9], tot[9:10], tot[10:11], tot[11:12]
    grads["q_norm_g"] = tot[13:14, 0:HEAD_DIM]
    grads["k_norm_g"] = tot[13:14, HEAD_DIM:2 * HEAD_DIM]
    grads["sink_logit"] = tot[13:14, 2 * HEAD_DIM:2 * HEAD_DIM + HEADS_A]
    loss = tot[12, 0]

    dm_all = gathered[:, 0:6, :].reshape(N_DEV, 6 * d)
    dmc_tot = jnp.concatenate([tot[6:8].reshape(1, 2 * d), jnp.zeros((1, 4 * d), F32)], axis=1)
    dm_rows = jnp.concatenate([pad8(dm_all[i]) for i in range(N_DEV)] + [pad8(dmc_tot), jnp.zeros((8, 6 * d), F32)], axis=0)
    dm_shard = lax.dynamic_slice(dm_rows, (0, s_me * ADA_W), (ADA_ROWS, ADA_W))
    dmc_shard = lax.dynamic_slice(pad8(dmc_tot), (0, s_me * ADA_W), (8, ADA_W))
    cc_part = _matmul(dmc_shard, w["w_ada"], name="d_cctx", tb=True, tm=8, tn=1024, tk=1536, out_dtype=F32)
    c_send, c_recv, cc_part, c_land, c_token = _rows_start("cctx", cc_part)
    grads["w_ada"] = _matmul(act, dm_shard, name="dw_ada", ta=True, tm=1024, tn=1024, tk=ADA_ROWS, out_dtype=F32,
                             after=c_token)
    delta["w_ada"], new_m["w_ada"], new_v["w_ada"] = _adamw(w["w_ada"], grads["w_ada"], m["w_ada"], v["w_ada"],
                                                            "adamw_w_ada")
    gathered_cc = _rows_wait("cctx", c_send, c_recv, cc_part, c_land, new_v["w_ada"])
    grads["c_ctx"] = _cctx_grad(gathered_cc, c_ctx).reshape(d)
    rows = lambda t: [t[name].reshape(1, size) for name, size in SMALL]
    small = _adamw_small(rows(w), rows(grads), rows(m), rows(v))
    for k, (name, size) in enumerate(SMALL):
        delta[name], new_m[name], new_v[name] = [t[k].reshape(w[name].shape) for t in small]
        grads[name] = grads[name].reshape(w[name].shape)

    lead = lambda name, t: t[None] if name in ("w_ada", "w_in", "w_out", "w_gate", "w_up", "w_down") else t
    outs = [loss, grad_x[None]]
    for group in (grads, delta, new_m, new_v):
        outs += [lead(name, group[name]) for name in WEIGHT_ORDER]
    return tuple(outs)
```

```python
import functools
import math

import jax
import jax.numpy as jnp
from jax import lax
from jax.experimental import pallas as pl
from jax.experimental.pallas import tpu as pltpu

F32 = jnp.float32
BF16 = jnp.bfloat16
MESH = pl.DeviceIdType.MESH

D_MODEL = 2048
HEAD_DIM = 128
HEADS_A = 8
HEADS_B = 8
KV_A = 2
KV_B = 2
GROUP = 4
GRID_W = 64
WINDOW = 128
BLOCK = 128
FFN = 5632
IN_WIDTH = 3072
MIX_WIDTH = 2048
ROPE_THETA = 10000.0
EPS = 1e-6
ATTN_SCALE = HEAD_DIM ** -0.5
DN_ALPHA = 2.0 ** 0.25
N_SHARD = 4
N_DEV = 8

ADAM_LR = 0.001
ADAM_B1 = 0.9
ADAM_B2 = 0.999
ADAM_EPS = 1e-08
ADAM_WD = 0.01
ADAM_STEP = 10

QA0, KA0, VA0, QB0, KB0, VB0 = 0, 1024, 1280, 1536, 2560, 2816

VMEM_LIMIT = 56 * 1024 * 1024
ROW_TILE = 256
NN = (((1,), (0,)), ((), ()))
NT = (((1,), (1,)), ((), ()))
TN = (((0,), (0,)), ((), ()))


def _fit(total, pref):
    step = ROW_TILE // 4
    best = step
    for cand in range(step, pref + 1, step):
        if total % cand == 0:
            best = cand
    return best


def _params(sem=None):
    return pltpu.CompilerParams(dimension_semantics=sem, vmem_limit_bytes=VMEM_LIMIT)


def _matmul(a, b, *, name, ta=False, tb=False, tm, tn, tk, out_dtype, after=None):
    m = a.shape[1] if ta else a.shape[0]
    k = a.shape[0] if ta else a.shape[1]
    n = b.shape[0] if tb else b.shape[1]
    assert (b.shape[1] if tb else b.shape[0]) == k
    tm, tn, tk = min(tm, m), min(tn, n), min(tk, k)
    assert m % tm == 0 and n % tn == 0 and k % tk == 0, (name, m, n, k, tm, tn, tk)
    nk = k // tk
    dn = (((0 if ta else 1,), (1 if tb else 0,)), ((), ()))

    def product(a_ref, b_ref):
        return lax.dot_general(a_ref[...].astype(BF16), b_ref[...].astype(BF16), dn, preferred_element_type=F32)

    def body_whole_k(a_ref, b_ref, *rest):
        o_ref = rest[-1]
        o_ref[...] = product(a_ref, b_ref).astype(o_ref.dtype)

    def body(a_ref, b_ref, *rest):
        o_ref, acc_ref = rest[-2:]
        kk = pl.program_id(2)
        part = product(a_ref, b_ref)

        @pl.when(kk == 0)
        def _():
            acc_ref[...] = part

        @pl.when(kk != 0)
        def _():
            acc_ref[...] += part

        @pl.when(kk == nk - 1)
        def _():
            o_ref[...] = acc_ref[...].astype(o_ref.dtype)

    a_spec = (pl.BlockSpec((tk, tm), lambda i, j, kk: (kk, i)) if ta
              else pl.BlockSpec((tm, tk), lambda i, j, kk: (i, kk)))
    b_spec = (pl.BlockSpec((tn, tk), lambda i, j, kk: (j, kk)) if tb
              else pl.BlockSpec((tk, tn), lambda i, j, kk: (kk, j)))
    return pl.pallas_call(
        body_whole_k if nk == 1 else body, name=name, grid=(m // tm, n // tn, nk),
        in_specs=[a_spec, b_spec] + ([] if after is None else [pl.BlockSpec(memory_space=pl.ANY)]),
        out_specs=pl.BlockSpec((tm, tn), lambda i, j, kk: (i, j)),
        out_shape=jax.ShapeDtypeStruct((m, n), out_dtype),
        scratch_shapes=[] if nk == 1 else [pltpu.VMEM((tm, tn), F32)],
        compiler_params=_params(("parallel", "parallel", "arbitrary")),
    )(a, b, *([] if after is None else [after]))


def _modulate_rows(x, ctx, mods):
    n, d = x.shape
    c = ctx.shape[0]
    nx = n // ROW_TILE
    assert c == ROW_TILE

    def body(x_ref, ctx_ref, mods_ref, o_ref):
        i = pl.program_id(0)

        @pl.when(i < nx)
        def _():
            o_ref[...] = (x_ref[...] * (1.0 + mods_ref[0:1, :]) + mods_ref[1:2, :]).astype(BF16)

        @pl.when(i >= nx)
        def _():
            o_ref[...] = (ctx_ref[...] * (1.0 + mods_ref[2:3, :]) + mods_ref[3:4, :]).astype(BF16)

    return pl.pallas_call(
        body, name="modulate_rows", grid=(nx + 1,),
        in_specs=[pl.BlockSpec((ROW_TILE, d), lambda i: (jnp.minimum(i, nx - 1), 0)),
                  pl.BlockSpec((ROW_TILE, d), lambda i: (0, 0)),
                  pl.BlockSpec((8, d), lambda i: (0, 0))],
        out_specs=pl.BlockSpec((ROW_TILE, d), lambda i: (i, 0)),
        out_shape=jax.ShapeDtypeStruct((n + c, d), BF16),
        compiler_params=_params(("parallel",)),
    )(x, ctx, mods)


def _rope_tables(n, c):
    rows = n // GRID_W
    row_ids = jnp.repeat(jnp.arange(rows, dtype=F32), GRID_W)
    col_ids = jnp.tile(jnp.arange(GRID_W, dtype=F32), rows)
    axis_dim = HEAD_DIM // 2
    inv_freq = jnp.power(ROPE_THETA, -jnp.arange(0, axis_dim, 2, dtype=F32) / axis_dim)
    ang_r = row_ids[:, None] * inv_freq
    ang_c = col_ids[:, None] * inv_freq
    ang = jnp.concatenate([ang_r, ang_r, ang_c, ang_c], axis=-1)
    cos, sin = jnp.cos(ang), jnp.sin(ang)
    quarter = (jnp.arange(HEAD_DIM) // (HEAD_DIM // 4)) % 2
    sin_a = jnp.where(quarter == 0, -sin, 0.0)
    sin_b = jnp.where(quarter == 1, sin, 0.0)
    pad = lambda t, v: jnp.concatenate([t, jnp.full((c, HEAD_DIM), v, F32)], axis=0)
    return pad(cos, 1.0), pad(sin_a, 0.0), pad(sin_b, 0.0)


def _rope(x, cos, sin_a, sin_b):
    return x * cos + pltpu.roll(x, 96, 1) * sin_a + pltpu.roll(x, 32, 1) * sin_b


def _rope_t(dy, cos, sin_a, sin_b):
    return dy * cos - pltpu.roll(dy, 96, 1) * sin_a - pltpu.roll(dy, 32, 1) * sin_b


def _rms(x):
    r = lax.rsqrt(jnp.mean(x * x, axis=-1, keepdims=True) + EPS)
    return x * r, r


def _qkv_post(h_all, cos, sin_a, sin_b, q_g, k_g):
    t = h_all.shape[0]
    nt = t // ROW_TILE

    def body(h_ref, cos_ref, sa_ref, sb_ref, qg_ref, kg_ref, qa_ref, ka_ref, va_ref, qb_ref, kb_ref, vb_ref):
        cos_, sa, sb = cos_ref[...], sa_ref[...], sb_ref[...]
        sl = lambda off, hh: h_ref[:, off + hh * HEAD_DIM: off + (hh + 1) * HEAD_DIM]
        for hh in range(HEADS_A):
            qa_ref[hh] = (_rope(sl(QA0, hh), cos_, sa, sb) * ATTN_SCALE).astype(BF16)
        for hh in range(KV_A):
            ka_ref[hh] = _rope(sl(KA0, hh), cos_, sa, sb).astype(BF16)
            va_ref[hh] = sl(VA0, hh).astype(BF16)
        for hh in range(HEADS_B):
            xn, _ = _rms(sl(QB0, hh))
            qb_ref[hh] = (_rope(xn * qg_ref[...], cos_, sa, sb) * ATTN_SCALE).astype(BF16)
        for hh in range(KV_B):
            xn, _ = _rms(sl(KB0, hh))
            kb_ref[hh] = _rope(xn * kg_ref[...], cos_, sa, sb).astype(BF16)
            vb_ref[hh] = sl(VB0, hh).astype(BF16)

    tab = pl.BlockSpec((ROW_TILE, HEAD_DIM), lambda i: (i, 0))
    gain = pl.BlockSpec((1, HEAD_DIM), lambda i: (0, 0))
    hs = lambda nh: pl.BlockSpec((nh, ROW_TILE, HEAD_DIM), lambda i: (0, i, 0))
    sh = lambda nh: jax.ShapeDtypeStruct((nh, t, HEAD_DIM), BF16)
    return pl.pallas_call(
        body, name="qkv_post", grid=(nt,),
        in_specs=[pl.BlockSpec((ROW_TILE, IN_WIDTH), lambda i: (i, 0)), tab, tab, tab, gain, gain],
        out_specs=[hs(HEADS_A), hs(KV_A), hs(KV_A), hs(HEADS_B), hs(KV_B), hs(KV_B)],
        out_shape=[sh(HEADS_A), sh(KV_A), sh(KV_A), sh(HEADS_B), sh(KV_B), sh(KV_B)],
        compiler_params=_params(("parallel",)),
    )(h_all, cos, sin_a, sin_b, q_g, k_g)


def _qkv_bwd_post(h_all, cos, sin_a, sin_b, q_g, k_g, dqa, dka, dva, dqb, dkb, dvb, n):
    t = h_all.shape[0]
    nt = t // ROW_TILE
    nx = n // ROW_TILE

    def body(h_ref, cos_ref, sa_ref, sb_ref, qg_ref, kg_ref,
             dqa_ref, dka_ref, dva_ref, dqb_ref, dkb_ref, dvb_ref, dh_ref, gs_ref):
        i = pl.program_id(0)
        cos_, sa, sb = cos_ref[...], sa_ref[...], sb_ref[...]
        latent = (i < nx).astype(F32)
        sl = lambda off, hh: h_ref[:, off + hh * HEAD_DIM: off + (hh + 1) * HEAD_DIM]

        def put(off, hh, val):
            dh_ref[:, off + hh * HEAD_DIM: off + (hh + 1) * HEAD_DIM] = val.astype(BF16)

        def norm_bwd(x, gain, dy):
            xn, r = _rms(x)
            dxh = dy * gain
            dx = r * (dxh - xn * jnp.mean(dxh * xn, axis=-1, keepdims=True))
            return dx, jnp.sum(dy * xn, axis=0, keepdims=True)

        for hh in range(HEADS_A):
            put(QA0, hh, _rope_t(dqa_ref[hh] * (ATTN_SCALE * latent), cos_, sa, sb))
        for hh in range(KV_A):
            put(KA0, hh, _rope_t(dka_ref[hh], cos_, sa, sb))
            put(VA0, hh, dva_ref[hh])
        gq = jnp.zeros((1, HEAD_DIM), F32)
        gk = jnp.zeros((1, HEAD_DIM), F32)
        for hh in range(HEADS_B):
            dq_t = dqb_ref[hh // GROUP, :, (hh % GROUP) * ROW_TILE:(hh % GROUP + 1) * ROW_TILE]
            dy = _rope_t(dq_t.T * (ATTN_SCALE * latent), cos_, sa, sb)
            dx, g = norm_bwd(sl(QB0, hh), qg_ref[...], dy)
            put(QB0, hh, dx)
            gq = gq + g
        for hh in range(KV_B):
            dy = _rope_t(dkb_ref[hh], cos_, sa, sb)
            dx, g = norm_bwd(sl(KB0, hh), kg_ref[...], dy)
            put(KB0, hh, dx)
            gk = gk + g
            put(VB0, hh, dvb_ref[hh])
        upd = jnp.concatenate([gq, gk, jnp.zeros((6, HEAD_DIM), F32)], axis=0)

        @pl.when(i == 0)
        def _():
            gs_ref[...] = upd

        @pl.when(i != 0)
        def _():
            gs_ref[...] += upd

    tab = pl.BlockSpec((ROW_TILE, HEAD_DIM), lambda i: (i, 0))
    gain = pl.BlockSpec((1, HEAD_DIM), lambda i: (0, 0))
    lat = lambda nh: pl.BlockSpec((nh, ROW_TILE, HEAD_DIM), lambda i: (0, jnp.minimum(i, nx - 1), 0))
    full = lambda nh: pl.BlockSpec((nh, ROW_TILE, HEAD_DIM), lambda i: (0, i, 0))
    return pl.pallas_call(
        body, name="qkv_bwd_post", grid=(nt,),
        in_specs=[pl.BlockSpec((ROW_TILE, IN_WIDTH), lambda i: (i, 0)), tab, tab, tab, gain, gain,
                  lat(HEADS_A), full(KV_A), full(KV_A),
                  pl.BlockSpec((KV_B, None, HEAD_DIM, GROUP * ROW_TILE), lambda i: (0, jnp.minimum(i, nx - 1), 0, 0)),
                  full(KV_B), full(KV_B)],
        out_specs=[pl.BlockSpec((ROW_TILE, IN_WIDTH), lambda i: (i, 0)),
                   pl.BlockSpec((8, HEAD_DIM), lambda i: (0, 0))],
        out_shape=[jax.ShapeDtypeStruct((t, IN_WIDTH), BF16), jax.ShapeDtypeStruct((8, HEAD_DIM), F32)],
        compiler_params=_params(("arbitrary",)),
    )(h_all, cos, sin_a, sin_b, q_g, k_g, dqa, dka, dva, dqb, dkb, dvb)


GB_TQ = 256
GB_TK = 256


def _heads_rows(ref2d, tq):
    return jnp.concatenate([ref2d[:, hh * HEAD_DIM:(hh + 1) * HEAD_DIM] for hh in range(GROUP)], axis=0)


def _attn_b_fwd(qb, kb, vb, n):
    t = kb.shape[1]
    nk = t // GB_TK
    tq = GB_TQ
    nq = n // tq
    qb_step = 2 if nq % 2 == 0 else 1
    rows = qb_step * GROUP * tq

    def body(q_ref, k_ref, v_ref, o_ref, lse_ref, m_s, l_s, acc_s):
        blocks = [(s, hh) for s in range(qb_step) for hh in range(GROUP)]
        q = jnp.concatenate([q_ref[hh, s * tq:(s + 1) * tq, :] for s, hh in blocks], axis=0)
        m_s[...] = jnp.full((1, rows), -jnp.inf, F32)
        l_s[...] = jnp.zeros((1, rows), F32)
        acc_s[...] = jnp.zeros((HEAD_DIM, rows), F32)

        def scores(j):
            start = pl.multiple_of(j * GB_TK, GB_TK)
            return lax.dot_general(k_ref[pl.ds(start, GB_TK), :], q, NT, preferred_element_type=F32)

        def step(j, st):
            st_next = scores(jnp.minimum(j + 1, nk - 1))
            vs = v_ref[pl.ds(pl.multiple_of(j * GB_TK, GB_TK), GB_TK), :]
            m_prev = m_s[...]
            m_new = jnp.maximum(m_prev, jnp.max(st, axis=0, keepdims=True))
            p = jnp.exp(st - m_new)
            alpha = jnp.exp(m_prev - m_new)
            l_s[...] = alpha * l_s[...] + jnp.sum(p, axis=0, keepdims=True)
            acc_s[...] = alpha * acc_s[...] + lax.dot_general(vs, p.astype(BF16), TN, preferred_element_type=F32)
            m_s[...] = m_new
            return st_next

        lax.fori_loop(0, nk, step, scores(0))
        ot = acc_s[...] * (1.0 / l_s[...])
        lse = m_s[...] + jnp.log(l_s[...])
        width = GROUP * tq
        for s in range(qb_step):
            lse_ref[s] = lse[:, s * width:(s + 1) * width]
        for k, (s, hh) in enumerate(blocks):
            o_ref[s * tq:(s + 1) * tq, hh * HEAD_DIM:(hh + 1) * HEAD_DIM] = ot[:, k * tq:(k + 1) * tq].T.astype(BF16)

    return pl.pallas_call(
        body, name="attn_b_fwd", grid=(KV_B, nq // qb_step),
        in_specs=[pl.BlockSpec((GROUP, qb_step * tq, HEAD_DIM), lambda g, i: (g, i, 0)),
                  pl.BlockSpec((None, t, HEAD_DIM), lambda g, i: (g, 0, 0)),
                  pl.BlockSpec((None, t, HEAD_DIM), lambda g, i: (g, 0, 0))],
        out_specs=[pl.BlockSpec((qb_step * tq, GROUP * HEAD_DIM), lambda g, i: (i, KV_A + g)),
                   pl.BlockSpec((None, qb_step, 1, GROUP * tq), lambda g, i: (g, i, 0, 0))],
        out_shape=[jax.ShapeDtypeStruct((n, MIX_WIDTH), BF16),
                   jax.ShapeDtypeStruct((KV_B, nq, 1, GROUP * tq), F32)],
        scratch_shapes=[pltpu.VMEM((1, rows), F32), pltpu.VMEM((1, rows), F32), pltpu.VMEM((HEAD_DIM, rows), F32)],
        compiler_params=_params(("parallel", "parallel")),
    )(qb, kb, vb)


def _attn_b_bwd(qb, kb, vb, dheads, lse, delta, n):
    t = kb.shape[1]
    nk = t // GB_TK
    tq = GB_TQ
    nq = n // tq
    rows = GROUP * tq

    qb_step = 4 if nq % 4 == 0 else 1

    def body(q_ref, k_ref, v_ref, do_ref, lse_ref, dl_ref, dq_ref, dk_ref, dv_ref):
        j = pl.program_id(1)
        i = pl.program_id(2)

        blocks = [(s, hh) for s in range(qb_step) for hh in range(GROUP)]
        q = jnp.concatenate([q_ref[hh, s * tq:(s + 1) * tq, :] for s, hh in blocks], axis=0)
        do = jnp.concatenate([do_ref[s * tq:(s + 1) * tq, hh * HEAD_DIM:(hh + 1) * HEAD_DIM] for s, hh in blocks], axis=0)
        lse_row = jnp.concatenate([lse_ref[s] for s in range(qb_step)], axis=1)
        dl_row = jnp.concatenate([dl_ref[s] for s in range(qb_step)], axis=1)
        ks, vs = k_ref[...], v_ref[...]
        st = lax.dot_general(ks, q, NT, preferred_element_type=F32)
        p = jnp.exp(st - lse_row)
        dpt = lax.dot_general(vs, do, NT, preferred_element_type=F32)
        ds = (p * (dpt - dl_row)).astype(BF16)
        dv_part = lax.dot_general(p.astype(BF16), do, NN, preferred_element_type=F32)
        dk_part = lax.dot_general(ds, q, NN, preferred_element_type=F32)
        dq_part = lax.dot_general(ks, ds, TN, preferred_element_type=F32)

        @pl.when(i == 0)
        def _():
            dk_ref[...] = dk_part
            dv_ref[...] = dv_part

        @pl.when(i != 0)
        def _():
            dk_ref[...] += dk_part
            dv_ref[...] += dv_part

        for s in range(qb_step):
            piece = dq_part[:, s * rows:(s + 1) * rows]

            @pl.when(j == 0)
            def _():
                dq_ref[qb_step * i + s] = piece

            @pl.when(j != 0)
            def _():
                dq_ref[qb_step * i + s] += piece

    kv = pl.BlockSpec((None, GB_TK, HEAD_DIM), lambda g, j, i: (g, j, 0))
    row = pl.BlockSpec((None, qb_step, 1, rows), lambda g, j, i: (g, i, 0, 0))
    return pl.pallas_call(
        body, name="attn_b_bwd", grid=(KV_B, nk, nq // qb_step),
        in_specs=[pl.BlockSpec((GROUP, qb_step * tq, HEAD_DIM), lambda g, j, i: (g, i, 0)), kv, kv,
                  pl.BlockSpec((qb_step * tq, GROUP * HEAD_DIM), lambda g, j, i: (i, KV_A + g)), row, row],
        out_specs=[pl.BlockSpec((None, nq, HEAD_DIM, rows), lambda g, j, i: (g, 0, 0, 0)), kv, kv],
        out_shape=[jax.ShapeDtypeStruct((KV_B, nq, HEAD_DIM, rows), F32),
                   jax.ShapeDtypeStruct((KV_B, t, HEAD_DIM), F32),
                   jax.ShapeDtypeStruct((KV_B, t, HEAD_DIM), F32)],
        compiler_params=_params(("parallel", "arbitrary", "arbitrary")),
    )(qb, kb, vb, dheads, lse, delta)


def _delta_rows(dheads, heads):
    n = heads.shape[0]
    tq = GB_TQ
    w = GROUP * HEAD_DIM

    def body(a_ref, b_ref, o_ref):
        prod = a_ref[...].astype(F32) * b_ref[...].astype(F32)
        cols = [jnp.sum(prod[:, hh * HEAD_DIM:(hh + 1) * HEAD_DIM].T, axis=0, keepdims=True) for hh in range(GROUP)]
        o_ref[...] = jnp.concatenate(cols, axis=1)

    blk = pl.BlockSpec((tq, w), lambda g, i: (i, KV_A + g))
    return pl.pallas_call(
        body, name="delta_rows", grid=(KV_B, n // tq),
        in_specs=[blk, blk],
        out_specs=pl.BlockSpec((None, None, 1, GROUP * tq), lambda g, i: (g, i, 0, 0)),
        out_shape=jax.ShapeDtypeStruct((KV_B, n // tq, 1, GROUP * tq), F32),
        compiler_params=_params(("parallel", "parallel")),
    )(dheads, heads)


KWIN = 3 * BLOCK


def _window_scores(q, k_ref, j, n, nb, sink_row):
    c = k_ref.shape[0] - n
    start = pl.multiple_of(jnp.clip(j - 1, 0, nb - 3) * BLOCK, BLOCK)
    kw = k_ref[pl.ds(start, KWIN), :]
    kc = k_ref[pl.ds(n, c), :]
    s_loc = lax.dot_general(kw, q, NT, preferred_element_type=F32)
    s_ctx = lax.dot_general(kc, q, NT, preferred_element_type=F32)
    cols = GROUP * BLOCK
    qpos = j * BLOCK + lax.broadcasted_iota(jnp.int32, (KWIN, cols), 1) % BLOCK
    kpos = start + lax.broadcasted_iota(jnp.int32, (KWIN, cols), 0)
    s_loc = jnp.where(jnp.abs(qpos - kpos) <= WINDOW, s_loc, -jnp.inf)
    m = jnp.maximum(jnp.maximum(jnp.max(s_loc, axis=0, keepdims=True), jnp.max(s_ctx, axis=0, keepdims=True)),
                    sink_row)
    e_loc, e_ctx, e_sink = jnp.exp(s_loc - m), jnp.exp(s_ctx - m), jnp.exp(sink_row - m)
    inv = 1.0 / (jnp.sum(e_loc, axis=0, keepdims=True) + jnp.sum(e_ctx, axis=0, keepdims=True) + e_sink)
    return e_loc * inv, e_ctx * inv, e_sink * inv, start


def _sink_row(sink_ref, g):
    return jnp.concatenate([sink_ref[pl.ds(g * GROUP + hh, 1), :] for hh in range(GROUP)], axis=1)


def _attn_a_fwd(qa, ka, va, sink_b, heads_b, n):
    t = ka.shape[1]
    nb = n // BLOCK
    assert nb >= 3
    wb = 4 if nb % 4 == 0 else 1

    def body(q_ref, k_ref, v_ref, sink_ref, heads_b_ref, o_ref):
        g, jj = pl.program_id(0), pl.program_id(1)
        for s in range(wb):
            j = jj * wb + s
            rows = slice(s * BLOCK, (s + 1) * BLOCK)
            q = q_ref[:, rows, :].reshape(GROUP * BLOCK, HEAD_DIM)
            p_loc, p_ctx, _, start = _window_scores(q, k_ref, j, n, nb, _sink_row(sink_ref, g))
            vw = v_ref[pl.ds(start, KWIN), :]
            vc = v_ref[pl.ds(n, t - n), :]
            ot = (lax.dot_general(vw, p_loc.astype(BF16), TN, preferred_element_type=F32)
                  + lax.dot_general(vc, p_ctx.astype(BF16), TN, preferred_element_type=F32))
            for hh in range(GROUP):
                o_ref[rows, hh * HEAD_DIM:(hh + 1) * HEAD_DIM] = ot[:, hh * BLOCK:(hh + 1) * BLOCK].T.astype(BF16)

    return pl.pallas_call(
        body, name="attn_a_fwd", grid=(KV_A, nb // wb),
        in_specs=[pl.BlockSpec((GROUP, wb * BLOCK, HEAD_DIM), lambda g, j: (g, j, 0)),
                  pl.BlockSpec((None, t, HEAD_DIM), lambda g, j: (g, 0, 0)),
                  pl.BlockSpec((None, t, HEAD_DIM), lambda g, j: (g, 0, 0)),
                  pl.BlockSpec((HEADS_A, HEAD_DIM), lambda g, j: (0, 0)),
                  pl.BlockSpec(memory_space=pl.ANY)],
        out_specs=pl.BlockSpec((wb * BLOCK, GROUP * HEAD_DIM), lambda g, j: (j, g)),
        out_shape=jax.ShapeDtypeStruct((n, MIX_WIDTH), BF16),
        input_output_aliases={4: 0},
        compiler_params=_params(("parallel", "parallel")),
    )(qa, ka, va, sink_b, heads_b)


def _attn_a_bwd(qa, ka, va, sink_b, dheads, n):
    t = ka.shape[1]
    c = t - n
    nb = n // BLOCK
    wb = 4 if nb % 4 == 0 else 1

    def body(q_ref, k_ref, v_ref, sink_ref, do_ref, dq_ref, dk_ref, dv_ref, dsink_ref):
        g, jj = pl.program_id(0), pl.program_id(1)

        @pl.when(jj == 0)
        def _():
            dk_ref[...] = jnp.zeros_like(dk_ref)
            dv_ref[...] = jnp.zeros_like(dv_ref)
            dsink_ref[...] = jnp.zeros_like(dsink_ref)

        for s in range(wb):
            j = jj * wb + s
            rows = slice(s * BLOCK, (s + 1) * BLOCK)
            q = q_ref[:, rows, :].reshape(GROUP * BLOCK, HEAD_DIM)
            do = _heads_rows(do_ref[rows, :], BLOCK)
            p_loc, p_ctx, p_sink, start = _window_scores(q, k_ref, j, n, nb, _sink_row(sink_ref, g))
            kw, vw = k_ref[pl.ds(start, KWIN), :], v_ref[pl.ds(start, KWIN), :]
            kc, vc = k_ref[pl.ds(n, c), :], v_ref[pl.ds(n, c), :]
            dp_loc = lax.dot_general(vw, do, NT, preferred_element_type=F32)
            dp_ctx = lax.dot_general(vc, do, NT, preferred_element_type=F32)
            dl = jnp.sum(p_loc * dp_loc, axis=0, keepdims=True) + jnp.sum(p_ctx * dp_ctx, axis=0, keepdims=True)
            ds_loc = (p_loc * (dp_loc - dl)).astype(BF16)
            ds_ctx = (p_ctx * (dp_ctx - dl)).astype(BF16)
            dqt = (lax.dot_general(kw, ds_loc, TN, preferred_element_type=F32)
                   + lax.dot_general(kc, ds_ctx, TN, preferred_element_type=F32))
            for hh in range(GROUP):
                dq_ref[hh, rows, :] = dqt[:, hh * BLOCK:(hh + 1) * BLOCK].T
            dk_ref[pl.ds(start, KWIN), :] += lax.dot_general(ds_loc, q, NN, preferred_element_type=F32)
            dv_ref[pl.ds(start, KWIN), :] += lax.dot_general(p_loc.astype(BF16), do, NN, preferred_element_type=F32)
            dk_ref[pl.ds(n, c), :] += lax.dot_general(ds_ctx, q, NN, preferred_element_type=F32)
            dv_ref[pl.ds(n, c), :] += lax.dot_general(p_ctx.astype(BF16), do, NN, preferred_element_type=F32)
            dsk = -(p_sink * dl)
            upd = [jnp.broadcast_to(jnp.sum(dsk[:, hh * BLOCK:(hh + 1) * BLOCK], axis=1, keepdims=True), (1, HEAD_DIM))
                   for hh in range(GROUP)]
            dsink_ref[...] += jnp.concatenate(upd + [jnp.zeros((8 - GROUP, HEAD_DIM), F32)], axis=0)

    res = pl.BlockSpec((None, t, HEAD_DIM), lambda g, j: (g, 0, 0))
    return pl.pallas_call(
        body, name="attn_a_bwd", grid=(KV_A, nb // wb),
        in_specs=[pl.BlockSpec((GROUP, wb * BLOCK, HEAD_DIM), lambda g, j: (g, j, 0)), res, res,
                  pl.BlockSpec((HEADS_A, HEAD_DIM), lambda g, j: (0, 0)),
                  pl.BlockSpec((wb * BLOCK, GROUP * HEAD_DIM), lambda g, j: (j, g))],
        out_specs=[pl.BlockSpec((GROUP, wb * BLOCK, HEAD_DIM), lambda g, j: (g, j, 0)), res, res,
                   pl.BlockSpec((None, 8, HEAD_DIM), lambda g, j: (g, 0, 0))],
        out_shape=[jax.ShapeDtypeStruct((HEADS_A, n, HEAD_DIM), F32),
                   jax.ShapeDtypeStruct((KV_A, t, HEAD_DIM), F32),
                   jax.ShapeDtypeStruct((KV_A, t, HEAD_DIM), F32),
                   jax.ShapeDtypeStruct((KV_A, 8, HEAD_DIM), F32)],
        compiler_params=_params(("parallel", "arbitrary")),
    )(qa, ka, va, sink_b, dheads)


def _ln_stats(r):
    mu = jnp.mean(r, axis=-1, keepdims=True)
    cen = r - mu
    rstd = lax.rsqrt(jnp.mean(cen * cen, axis=-1, keepdims=True) + EPS)
    return cen * rstd, rstd


def _ln_bwd(dy, xhat, rstd, gain):
    dxh = dy * gain
    return rstd * (dxh - jnp.mean(dxh, axis=-1, keepdims=True)
                   - xhat * jnp.mean(dxh * xhat, axis=-1, keepdims=True))


def _accumulate_rows(ref, rows, i):
    pad = [jnp.zeros_like(rows[0])] * (8 - len(rows))
    upd = jnp.concatenate(rows + pad, axis=0)

    @pl.when(i == 0)
    def _():
        ref[...] = upd

    @pl.when(i != 0)
    def _():
        ref[...] += upd


def _colsum(v):
    return jnp.sum(v, axis=0, keepdims=True)


LN_TILE = 256


def _res_ln1(x, a, vec):
    n, d = x.shape

    def body(x_ref, a_ref, v_ref, xh_ref, rs_ref, u_ref):
        r1 = DN_ALPHA * x_ref[...] + v_ref[0:1, :] * a_ref[...]
        xhat, rstd = _ln_stats(r1)
        xh_ref[...] = xhat
        rs_ref[...] = rstd
        x1 = xhat * v_ref[1:2, :] + v_ref[2:3, :]
        u_ref[...] = (x1 * (1.0 + v_ref[3:4, :]) + v_ref[4:5, :]).astype(BF16)

    row = pl.BlockSpec((LN_TILE, d), lambda i: (i, 0))
    return pl.pallas_call(
        body, name="res_ln1", grid=(n // LN_TILE,),
        in_specs=[row, row, pl.BlockSpec((8, d), lambda i: (0, 0))],
        out_specs=[row, pl.BlockSpec((LN_TILE, 1), lambda i: (i, 0)), row],
        out_shape=[jax.ShapeDtypeStruct((n, d), F32), jax.ShapeDtypeStruct((n, 1), F32),
                   jax.ShapeDtypeStruct((n, d), BF16)],
        compiler_params=_params(("parallel",)),
    )(x, a, vec)


def _res_ln2_loss(xhat1, f, target, vec):
    n, d = f.shape

    def body(xh_ref, f_ref, t_ref, v_ref, dr_ref, df_ref, s_ref):
        i = pl.program_id(0)
        x1 = xh_ref[...] * v_ref[1:2, :] + v_ref[2:3, :]
        fv = f_ref[...]
        xhat, rstd = _ln_stats(DN_ALPHA * x1 + v_ref[0:1, :] * fv)
        err = xhat * v_ref[3:4, :] + v_ref[4:5, :] - t_ref[...]
        dy = err * (1.0 / d)
        dr2 = _ln_bwd(dy, xhat, rstd, v_ref[3:4, :])
        dr_ref[...] = dr2
        df_ref[...] = (dr2 * v_ref[0:1, :]).astype(BF16)
        _accumulate_rows(s_ref, [_colsum(dy * xhat), _colsum(dy), _colsum(dr2 * fv),
                                 _colsum(err * err) * (0.5 / d)], i)

    row = pl.BlockSpec((LN_TILE, d), lambda i: (i, 0))
    return pl.pallas_call(
        body, name="res_ln2_loss", grid=(n // LN_TILE,),
        in_specs=[row, row, row, pl.BlockSpec((8, d), lambda i: (0, 0))],
        out_specs=[row, row, pl.BlockSpec((8, d), lambda i: (0, 0))],
        out_shape=[jax.ShapeDtypeStruct((n, d), F32), jax.ShapeDtypeStruct((n, d), BF16),
                   jax.ShapeDtypeStruct((8, d), F32)],
        compiler_params=_params(("arbitrary",)),
    )(xhat1, f, target, vec)


def _ln1_bwd(du2, dr2, xhat1, rstd1, a, vec):
    n, d = du2.shape

    def body(du_ref, dr2_ref, xh_ref, rs_ref, a_ref, v_ref, dxp_ref, da_ref, s_ref):
        i = pl.program_id(0)
        du, xhat = du_ref[...], xh_ref[...]
        x1 = xhat * v_ref[1:2, :] + v_ref[2:3, :]
        dx1 = DN_ALPHA * dr2_ref[...] + du * (1.0 + v_ref[0:1, :])
        dr1 = _ln_bwd(dx1, xhat, rs_ref[...], v_ref[1:2, :])
        dxp_ref[...] = DN_ALPHA * dr1
        da_ref[...] = (dr1 * v_ref[3:4, :]).astype(BF16)
        _accumulate_rows(s_ref, [_colsum(du * x1), _colsum(du), _colsum(dx1 * xhat), _colsum(dx1),
                                 _colsum(dr1 * a_ref[...])], i)

    row = pl.BlockSpec((LN_TILE, d), lambda i: (i, 0))
    return pl.pallas_call(
        body, name="ln1_bwd", grid=(n // LN_TILE,),
        in_specs=[row, row, row, pl.BlockSpec((LN_TILE, 1), lambda i: (i, 0)), row,
                  pl.BlockSpec((8, d), lambda i: (0, 0))],
        out_specs=[row, row, pl.BlockSpec((8, d), lambda i: (0, 0))],
        out_shape=[jax.ShapeDtypeStruct((n, d), F32), jax.ShapeDtypeStruct((n, d), BF16),
                   jax.ShapeDtypeStruct((8, d), F32)],
        compiler_params=_params(("arbitrary",)),
    )(du2, dr2, xhat1, rstd1, a, vec)


def _mod1_bwd(du_all, dxp, x, ctx, mods):
    n, d = x.shape
    nx = n // ROW_TILE

    def body(du_ref, dxp_ref, x_ref, ctx_ref, m_ref, gx_ref, s_ref):
        i = pl.program_id(0)
        du = du_ref[...]
        zero = jnp.zeros((1, d), F32)

        @pl.when(i == 0)
        def _():
            s_ref[...] = jnp.zeros_like(s_ref)

        @pl.when(i < nx)
        def _():
            gx_ref[...] = dxp_ref[...] + du * (1.0 + m_ref[0:1, :])
            s_ref[...] += jnp.concatenate([_colsum(du * x_ref[...]), _colsum(du)] + [zero] * 6, axis=0)

        @pl.when(i >= nx)
        def _():
            s_ref[...] += jnp.concatenate([zero, zero, _colsum(du * ctx_ref[...]), _colsum(du)] + [zero] * 4, axis=0)

    lat = pl.BlockSpec((ROW_TILE, d), lambda i: (jnp.minimum(i, nx - 1), 0))
    return pl.pallas_call(
        body, name="mod1_bwd", grid=(nx + 1,),
        in_specs=[pl.BlockSpec((ROW_TILE, d), lambda i: (i, 0)), lat, lat,
                  pl.BlockSpec((ROW_TILE, d), lambda i: (0, 0)), pl.BlockSpec((8, d), lambda i: (0, 0))],
        out_specs=[lat, pl.BlockSpec((8, d), lambda i: (0, 0))],
        out_shape=[jax.ShapeDtypeStruct((n, d), F32), jax.ShapeDtypeStruct((8, d), F32)],
        compiler_params=_params(("arbitrary",)),
    )(du_all, dxp, x, ctx, mods)


FFN_TM = 1024
FFN_TN = 512


def _gate_up(u2, wg, wu, after):
    n, d = u2.shape
    f = wg.shape[1]

    def body(u_ref, wg_ref, wu_ref, after_ref, g_ref, up_ref, h_ref):
        u = u_ref[...]
        g = lax.dot_general(u, wg_ref[...], NN, preferred_element_type=F32)
        up = lax.dot_general(u, wu_ref[...], NN, preferred_element_type=F32)
        g_ref[...] = g.astype(BF16)
        up_ref[...] = up.astype(BF16)
        h_ref[...] = (g * jax.nn.sigmoid(g) * up).astype(BF16)

    tm = min(FFN_TM, n)
    wspec = pl.BlockSpec((d, FFN_TN), lambda j, i: (0, j))
    ospec = pl.BlockSpec((tm, FFN_TN), lambda j, i: (i, j))
    return pl.pallas_call(
        body, name="gate_up", grid=(f // FFN_TN, n // tm),
        in_specs=[pl.BlockSpec((tm, d), lambda j, i: (i, 0)), wspec, wspec, pl.BlockSpec(memory_space=pl.ANY)],
        out_specs=[ospec, ospec, ospec],
        out_shape=[jax.ShapeDtypeStruct((n, f), BF16)] * 3,
        compiler_params=_params(("parallel", "parallel")),
    )(u2, wg, wu, after)


def _glu_bwd(df, wd, g, u):
    n, d = df.shape
    f = wd.shape[0]

    def body(df_ref, wd_ref, g_ref, u_ref, dg_ref, du_ref):
        dh = lax.dot_general(df_ref[...], wd_ref[...], NT, preferred_element_type=F32)
        gv = g_ref[...].astype(F32)
        sig = jax.nn.sigmoid(gv)
        du_ref[...] = (dh * (gv * sig)).astype(BF16)
        dg_ref[...] = (dh * u_ref[...].astype(F32) * (sig * (1.0 + gv * (1.0 - sig)))).astype(BF16)

    tm = min(FFN_TM, n)
    ospec = pl.BlockSpec((tm, FFN_TN), lambda i, j: (i, j))
    return pl.pallas_call(
        body, name="glu_bwd", grid=(n // tm, f // FFN_TN),
        in_specs=[pl.BlockSpec((tm, d), lambda i, j: (i, 0)),
                  pl.BlockSpec((FFN_TN, d), lambda i, j: (j, 0)), ospec, ospec],
        out_specs=[ospec, ospec],
        out_shape=[jax.ShapeDtypeStruct((n, f), BF16), jax.ShapeDtypeStruct((n, f), BF16)],
        compiler_params=_params(("parallel", "parallel")),
    )(df, wd, g, u)


def _du2(dg, du, wg, wu):
    n, f = dg.shape
    d = wg.shape[0]
    tm, tn, tk = min(1024, n), 1024, 1408
    nk = f // tk

    def body(dg_ref, du_ref, wg_ref, wu_ref, o_ref, acc_ref):
        kk = pl.program_id(2)
        part = (lax.dot_general(dg_ref[...], wg_ref[...], NT, preferred_element_type=F32)
                + lax.dot_general(du_ref[...], wu_ref[...], NT, preferred_element_type=F32))

        @pl.when(kk == 0)
        def _():
            acc_ref[...] = part

        @pl.when(kk != 0)
        def _():
            acc_ref[...] += part

        @pl.when(kk == nk - 1)
        def _():
            o_ref[...] = acc_ref[...]

    aspec = pl.BlockSpec((tm, tk), lambda i, j, kk: (i, kk))
    wspec = pl.BlockSpec((tn, tk), lambda i, j, kk: (j, kk))
    return pl.pallas_call(
        body, name="du2", grid=(n // tm, d // tn, nk),
        in_specs=[aspec, aspec, wspec, wspec],
        out_specs=pl.BlockSpec((tm, tn), lambda i, j, kk: (i, j)),
        out_shape=jax.ShapeDtypeStruct((n, d), F32),
        scratch_shapes=[pltpu.VMEM((tm, tn), F32)],
        compiler_params=_params(("parallel", "parallel", "arbitrary")),
    )(dg, du, wg, wu)


def _rows8(rows, d=D_MODEL):
    rows = [r.reshape(1, d).astype(F32) for r in rows]
    return jnp.concatenate(rows + [jnp.zeros((8 - len(rows), d), F32)], axis=0)


W_GROUPS = (("w_in",), ("w_out", "w_gate", "w_up"), ("w_down",))
G_GROUPS = (("w_down", "w_gate", "w_up"), ("w_out",), ("w_in",))


def _layer_fwd_bwd(x, ctx, target, mod, mod_ctx, weights, prefetch, grads_out,
                   q_g, k_g, sink, ln1_g, ln1_b, ln2_g, ln2_b):
    n, d = x.shape
    c = ctx.shape[0]
    sh1, sc1, g1, sh2, sc2, g2 = [mod[:, k * d:(k + 1) * d] for k in range(6)]
    csh1, csc1 = mod_ctx[:, 0:d], mod_ctx[:, d:2 * d]
    cos, sin_a, sin_b = _rope_tables(n, c)
    sink_b = jnp.broadcast_to(sink.reshape(HEADS_A, 1), (HEADS_A, HEAD_DIM)).astype(F32)

    u_all = _modulate_rows(x, ctx, _rows8([sc1, sh1, csc1, csh1]))
    (w_in,) = weights(0, u_all)
    h_all = _matmul(u_all, w_in, name="qkv_proj", tm=_fit(n + c, 1088), tn=1024, tk=2048, out_dtype=F32)
    qa, ka, va, qb, kb, vb = _qkv_post(h_all, cos, sin_a, sin_b, q_g, k_g)
    heads_b, lse = _attn_b_fwd(qb, kb, vb, n)
    zero = prefetch(1, heads_b)
    heads = _attn_a_fwd(qa, ka, va, sink_b + zero, heads_b, n)
    w_out, w_gate, w_up = weights(1, heads)
    a = _matmul(heads, w_out, name="out_proj", tm=1024, tn=1024, tk=2048, out_dtype=F32)
    xhat1, rstd1, u2 = _res_ln1(x, a, _rows8([g1, ln1_g, ln1_b, sc2, sh2]))
    zero = prefetch(2, u2)
    gg, uu, hh = _gate_up(u2, w_gate, w_up, zero.reshape(1, 1))
    (w_down,) = weights(2, hh)
    f = _matmul(hh, w_down, name="ffn_down", tm=1024, tn=512, tk=FFN, out_dtype=F32)
    dr2, df, s_ln2 = _res_ln2_loss(xhat1, f, target, _rows8([g2, ln1_g, ln1_b, ln2_g, ln2_b]))

    dgg, duu = _glu_bwd(df, w_down, gg, uu)
    dw_down = _matmul(hh, df, name="dw_down", ta=True, tm=512, tn=1024, tk=n, out_dtype=BF16)
    dw_gate = _matmul(u2, dgg, name="dw_gate", ta=True, tm=1024, tn=512, tk=n, out_dtype=BF16)
    dw_up = _matmul(u2, duu, name="dw_up", ta=True, tm=1024, tn=512, tk=n, out_dtype=BF16)
    zero = grads_out(0, [dw_down, dw_gate, dw_up])
    du2 = _du2(dgg, duu, w_gate, w_up)
    dxp, da, s_ln1 = _ln1_bwd(du2, dr2, xhat1, rstd1, a, _rows8([sc2, ln1_g, ln1_b, g1]) + zero)

    dheads = _matmul(da, w_out, name="d_heads", tb=True, tm=1024, tn=1024, tk=2048, out_dtype=BF16)
    dw_out = _matmul(heads, da, name="dw_out", ta=True, tm=1024, tn=1024, tk=n, out_dtype=BF16)
    zero = grads_out(1, [dw_out])
    delta = _delta_rows(dheads, heads)
    dqa, dka, dva, dsink = _attn_a_bwd(qa, ka, va, sink_b + zero, dheads, n)
    dqb, dkb, dvb = _attn_b_bwd(qb, kb, vb, dheads, lse, delta, n)
    dh_all, s_gain = _qkv_bwd_post(h_all, cos, sin_a, sin_b, q_g, k_g, dqa, dka, dva, dqb, dkb, dvb, n)
    dw_in = _matmul(u_all, dh_all, name="dw_in", ta=True, tm=1024, tn=1024, tk=n + c, out_dtype=BF16)
    zero = grads_out(2, [dw_in])
    du_all = _matmul(dh_all, w_in, name="d_u1", tb=True, tm=_fit(n + c, 1088), tn=1024, tk=IN_WIDTH, out_dtype=F32,
                     after=zero.reshape(1, 1))
    grad_x, s_mod1 = _mod1_bwd(du_all, dxp, x, ctx, _rows8([sc1]) + zero)

    dsink_row = jnp.concatenate([dsink[0, 0:GROUP, 0], dsink[1, 0:GROUP, 0]]).reshape(1, HEADS_A)
    misc = jnp.concatenate([s_gain[0:1], s_gain[1:2], dsink_row,
                            jnp.zeros((1, d - 2 * HEAD_DIM - HEADS_A), F32)], axis=1)
    partial = jnp.concatenate([
        s_mod1[1:2], s_mod1[0:1], s_ln1[4:5],
        s_ln1[1:2], s_ln1[0:1], s_ln2[2:3],
        s_mod1[3:4], s_mod1[2:3],
        s_ln1[2:3], s_ln1[3:4], s_ln2[0:1], s_ln2[1:2],
        s_ln2[3:4], misc, jnp.zeros((2, d), F32)], axis=0)
    return grad_x, partial


ANY = pl.BlockSpec(memory_space=pl.ANY)
VMEM_FULL = pl.BlockSpec(memory_space=pltpu.VMEM)
N_CHIP_PEERS = 3


def _me():
    return lax.axis_index("x"), lax.axis_index("y"), lax.axis_index("c")


def _other_chips(x, y):
    return [(1 - x, y), (x, 1 - y), (1 - x, 1 - y)]


def _shard_of(chip):
    return 2 * chip[0] + chip[1]


def _dev_index(x, y, c):
    return 4 * x + 2 * y + c


def _rcopy(src, dst, send_sems, recv_sems, k, dev):
    return pltpu.make_async_remote_copy(src_ref=src, dst_ref=dst, send_sem=send_sems.at[k], recv_sem=recv_sems.at[k],
                                        device_id=dev, device_id_type=MESH)


BIG = (("w_in", (D_MODEL, IN_WIDTH), 1), ("w_out", (MIX_WIDTH, D_MODEL), 0), ("w_gate", (D_MODEL, FFN), 1),
       ("w_up", (D_MODEL, FFN), 1), ("w_down", (FFN, D_MODEL), 0))


def _sub(ref, axis, idx, size):
    start = pl.multiple_of(idx * size, size)
    return ref.at[pl.ds(start, size), :] if axis == 0 else ref.at[:, pl.ds(start, size)]


def _shape_div(shape, axis, parts):
    return tuple(s // parts if a == axis else s for a, s in enumerate(shape))


def _piece(a, ref, shard, half):
    _, full, axis = BIG[a]
    view = _sub(ref, axis, shard, full[axis] // N_SHARD)
    return _sub(view, 1 - axis, half, full[1 - axis] // 2)


HBM = pl.BlockSpec(memory_space=pltpu.HBM)
SEM = pl.BlockSpec(memory_space=pltpu.SEMAPHORE)
EFFECT = pltpu.SideEffectType.DATAFLOW_SIDE_EFFECTING
BIG_INDEX = {name: a for a, (name, _, _) in enumerate(BIG)}


def _in_hbm(arr):
    return pltpu.with_memory_space_constraint(arr, pltpu.HBM)


def _gather_start(tag, arrs, bufs, prev):
    n_arr = len(arrs)

    def body(*refs):
        ins = refs[:n_arr]
        send_sems, recv_sems = refs[n_arr + 1], refs[n_arr + 2]
        token = refs[-1]
        x, y, c = _me()
        s_me = _shard_of((x, y))
        for i, a in enumerate(arrs):
            mine = _piece(a, ins[i], s_me, c)
            for j, chip in enumerate(_other_chips(x, y)):
                _rcopy(mine, mine, send_sems, recv_sems, N_CHIP_PEERS * i + j, (*chip, c)).start()
        token[...] = jnp.zeros_like(token)

    n_sem = N_CHIP_PEERS * n_arr
    outs = pl.pallas_call(
        body, name="gather_start_" + tag,
        in_specs=[HBM] * n_arr + [ANY],
        out_specs=[SEM, SEM] + [HBM] * n_arr + [VMEM_FULL],
        out_shape=[pltpu.SemaphoreType.DMA((n_sem,)), pltpu.SemaphoreType.DMA((n_sem,))]
        + [pltpu.HBM(BIG[a][1], BF16) for a in arrs] + [jax.ShapeDtypeStruct((8, HEAD_DIM), F32)],
        input_output_aliases={i: 2 + i for i in range(n_arr)},
        compiler_params=pltpu.CompilerParams(has_side_effects=EFFECT),
    )(*[_in_hbm(b) for b in bufs], prev)
    return outs[0], outs[1], list(outs[2:2 + n_arr]), outs[-1]


def _gather_wait(tag, arrs, send_sems, recv_sems, bufs, after):
    n_arr = len(arrs)

    def body(*refs):
        ins = refs[:n_arr]
        send_sems_, recv_sems_ = refs[n_arr], refs[n_arr + 1]
        x, y, c = _me()
        s_me = _shard_of((x, y))
        for i, a in enumerate(arrs):
            mine = _piece(a, ins[i], s_me, c)
            for j, chip in enumerate(_other_chips(x, y)):
                landed = _piece(a, ins[i], _shard_of(chip), c)
                cp = _rcopy(mine, landed, send_sems_, recv_sems_, N_CHIP_PEERS * i + j, (*chip, c))
                cp.wait_send()
                cp.wait_recv()

    outs = pl.pallas_call(
        body, name="gather_wait_" + tag,
        in_specs=[HBM] * n_arr + [SEM, SEM, ANY],
        out_specs=[HBM] * n_arr,
        out_shape=[pltpu.HBM(BIG[a][1], BF16) for a in arrs],
        input_output_aliases={i: i for i in range(n_arr)},
        compiler_params=pltpu.CompilerParams(has_side_effects=EFFECT),
    )(*bufs, send_sems, recv_sems, after)
    return list(outs)


def _gather_forward(tag, arrs, bufs):
    n_arr = len(arrs)

    def body(*refs):
        outs = refs[n_arr:2 * n_arr]
        send_sems, recv_sems = refs[2 * n_arr:]
        x, y, c = _me()
        sibling = (x, y, 1 - c)
        chips = _other_chips(x, y)
        copies = []
        for i, a in enumerate(arrs):
            for j, chip in enumerate(chips):
                landed = _piece(a, outs[i], _shard_of(chip), c)
                cp = _rcopy(landed, landed, send_sems, recv_sems, N_CHIP_PEERS * i + j, sibling)
                cp.start()
                copies.append(cp)
        for i, a in enumerate(arrs):
            for j, chip in enumerate(chips):
                other = _piece(a, outs[i], _shard_of(chip), 1 - c)
                _rcopy(other, other, send_sems, recv_sems, N_CHIP_PEERS * i + j, sibling).wait_recv()
        for cp in copies:
            cp.wait_send()

    n_sem = N_CHIP_PEERS * n_arr
    return list(pl.pallas_call(
        body, name="gather_forward_" + tag,
        in_specs=[ANY] * n_arr, out_specs=[ANY] * n_arr,
        out_shape=[jax.ShapeDtypeStruct(BIG[a][1], BF16) for a in arrs],
        input_output_aliases={i: i for i in range(n_arr)},
        scratch_shapes=[pltpu.SemaphoreType.DMA((n_sem,)), pltpu.SemaphoreType.DMA((n_sem,))],
    )(*bufs))


def _forward_start(tag, arrs, bufs):
    n_arr = len(arrs)

    def body(*refs):
        ins = refs[:n_arr]
        send_sems, recv_sems = refs[n_arr], refs[n_arr + 1]
        token = refs[-1]
        x, y, c = _me()
        for i, a in enumerate(arrs):
            for j, chip in enumerate(_other_chips(x, y)):
                landed = _piece(a, ins[i], _shard_of(chip), c)
                _rcopy(landed, landed, send_sems, recv_sems, N_CHIP_PEERS * i + j, (x, y, 1 - c)).start()
        token[...] = jnp.zeros_like(token)

    n_sem = N_CHIP_PEERS * n_arr
    outs = pl.pallas_call(
        body, name="gather_forward_start_" + tag,
        in_specs=[HBM] * n_arr,
        out_specs=[SEM, SEM] + [HBM] * n_arr + [VMEM_FULL],
        out_shape=[pltpu.SemaphoreType.DMA((n_sem,)), pltpu.SemaphoreType.DMA((n_sem,))]
        + [pltpu.HBM(BIG[a][1], BF16) for a in arrs] + [jax.ShapeDtypeStruct((8, HEAD_DIM), F32)],
        input_output_aliases={i: 2 + i for i in range(n_arr)},
        compiler_params=pltpu.CompilerParams(has_side_effects=EFFECT),
    )(*bufs)
    return outs[0], outs[1], list(outs[2:2 + n_arr]), outs[-1]


def _forward_wait(tag, arrs, send_sems, recv_sems, bufs, after):
    n_arr = len(arrs)

    def body(*refs):
        ins = refs[:n_arr]
        send_sems_, recv_sems_ = refs[n_arr], refs[n_arr + 1]
        x, y, c = _me()
        for i, a in enumerate(arrs):
            for j, chip in enumerate(_other_chips(x, y)):
                mine = _piece(a, ins[i], _shard_of(chip), c)
                other = _piece(a, ins[i], _shard_of(chip), 1 - c)
                cp = _rcopy(mine, other, send_sems_, recv_sems_, N_CHIP_PEERS * i + j, (x, y, 1 - c))
                cp.wait_send()
                cp.wait_recv()

    outs = pl.pallas_call(
        body, name="gather_forward_wait_" + tag,
        in_specs=[HBM] * n_arr + [SEM, SEM, ANY],
        out_specs=[HBM] * n_arr,
        out_shape=[pltpu.HBM(BIG[a][1], BF16) for a in arrs],
        input_output_aliases={i: i for i in range(n_arr)},
        compiler_params=pltpu.CompilerParams(has_side_effects=EFFECT),
    )(*bufs, send_sems, recv_sems, after)
    return list(outs)


def _peers(x, y, c):
    return [(x ^ (mask >> 2), y ^ ((mask >> 1) & 1), c ^ (mask & 1)) for mask in range(1, N_DEV)]


def _received_shape(a):
    _, full, axis = BIG[a]
    return (N_DEV - 1,) + _shape_div(_shape_div(full, 1 - axis, 2), axis, N_SHARD)


def _pieces_start(tag, arrs, dws):
    n_arr = len(arrs)

    def body(*refs):
        srcs, lands = refs[:n_arr], refs[n_arr:2 * n_arr]
        send_sems, recv_sems = refs[2 * n_arr], refs[2 * n_arr + 1]
        token = refs[-1]
        x, y, c = _me()
        for i, a in enumerate(arrs):
            for k, peer in enumerate(_peers(x, y, c)):
                src = _piece(a, srcs[i], _shard_of(peer[:2]), peer[2])
                _rcopy(src, lands[i].at[k], send_sems, recv_sems, (N_DEV - 1) * i + k, peer).start()
        token[...] = jnp.zeros_like(token)

    n_sem = (N_DEV - 1) * n_arr
    lands = [_in_hbm(lax.empty(_received_shape(a), BF16)) for a in arrs]
    outs = pl.pallas_call(
        body, name="grad_pieces_start_" + tag,
        in_specs=[HBM] * (2 * n_arr),
        out_specs=[SEM, SEM] + [HBM] * (2 * n_arr) + [VMEM_FULL],
        out_shape=[pltpu.SemaphoreType.DMA((n_sem,)), pltpu.SemaphoreType.DMA((n_sem,))]
        + [pltpu.HBM(BIG[a][1], BF16) for a in arrs] + [pltpu.HBM(_received_shape(a), BF16) for a in arrs]
        + [jax.ShapeDtypeStruct((8, HEAD_DIM), F32)],
        input_output_aliases={i: 2 + i for i in range(2 * n_arr)},
        compiler_params=pltpu.CompilerParams(has_side_effects=EFFECT),
    )(*[_in_hbm(dw) for dw in dws], *lands)
    return outs[0], outs[1], list(outs[2:2 + n_arr]), list(outs[2 + n_arr:2 + 2 * n_arr]), outs[-1]


def _pieces_wait(tag, arrs, send_sems, recv_sems, dws, lands, after):
    n_arr = len(arrs)

    def body(*refs):
        srcs, lands_ = refs[:n_arr], refs[n_arr:2 * n_arr]
        send_sems_, recv_sems_ = refs[2 * n_arr], refs[2 * n_arr + 1]
        x, y, c = _me()
        for i, a in enumerate(arrs):
            for k, peer in enumerate(_peers(x, y, c)):
                src = _piece(a, srcs[i], _shard_of(peer[:2]), peer[2])
                cp = _rcopy(src, lands_[i].at[k], send_sems_, recv_sems_, (N_DEV - 1) * i + k, peer)
                cp.wait_send()
                cp.wait_recv()

    outs = pl.pallas_call(
        body, name="grad_pieces_wait_" + tag,
        in_specs=[HBM] * (2 * n_arr) + [SEM, SEM, ANY],
        out_specs=[HBM] * (2 * n_arr),
        out_shape=[pltpu.HBM(BIG[a][1], BF16) for a in arrs] + [pltpu.HBM(_received_shape(a), BF16) for a in arrs],
        input_output_aliases={i: i for i in range(2 * n_arr)},
        compiler_params=pltpu.CompilerParams(has_side_effects=EFFECT),
    )(*dws, *lands, send_sems, recv_sems, after)
    return list(outs[:n_arr]), list(outs[n_arr:])


def _join_start(tag, g_halves):
    n_arr = len(g_halves)

    def body(*refs):
        srcs, lands = refs[:n_arr], refs[n_arr:2 * n_arr]
        send_sems, recv_sems = refs[2 * n_arr], refs[2 * n_arr + 1]
        token = refs[-1]
        x, y, c = _me()
        for i in range(n_arr):
            _rcopy(srcs[i], lands[i], send_sems, recv_sems, i, (x, y, 1 - c)).start()
        token[...] = jnp.zeros_like(token)

    shapes = [pltpu.HBM(g.shape, F32) for g in g_halves]
    outs = pl.pallas_call(
        body, name="grad_join_start_" + tag,
        in_specs=[HBM] * (2 * n_arr),
        out_specs=[SEM, SEM] + [HBM] * (2 * n_arr) + [VMEM_FULL],
        out_shape=[pltpu.SemaphoreType.DMA((n_arr,)), pltpu.SemaphoreType.DMA((n_arr,))] + shapes + shapes
        + [jax.ShapeDtypeStruct((8, HEAD_DIM), F32)],
        input_output_aliases={i: 2 + i for i in range(2 * n_arr)},
        compiler_params=pltpu.CompilerParams(has_side_effects=EFFECT),
    )(*[_in_hbm(g) for g in g_halves], *[_in_hbm(lax.empty(g.shape, F32)) for g in g_halves])
    return outs[0], outs[1], list(outs[2:2 + n_arr]), list(outs[2 + n_arr:2 + 2 * n_arr]), outs[-1]


def _join_wait(tag, send_sems, recv_sems, g_halves, lands, after):
    n_arr = len(g_halves)

    def body(*refs):
        srcs, lands_ = refs[:n_arr], refs[n_arr:2 * n_arr]
        send_sems_, recv_sems_ = refs[2 * n_arr], refs[2 * n_arr + 1]
        x, y, c = _me()
        for i in range(n_arr):
            cp = _rcopy(srcs[i], lands_[i], send_sems_, recv_sems_, i, (x, y, 1 - c))
            cp.wait_send()
            cp.wait_recv()

    shapes = [pltpu.HBM(g.shape, F32) for g in g_halves]
    outs = pl.pallas_call(
        body, name="grad_join_wait_" + tag,
        in_specs=[HBM] * (2 * n_arr) + [SEM, SEM, ANY],
        out_specs=[HBM] * (2 * n_arr),
        out_shape=shapes + shapes,
        input_output_aliases={i: i for i in range(2 * n_arr)},
        compiler_params=pltpu.CompilerParams(has_side_effects=EFFECT),
    )(*g_halves, *lands, send_sems, recv_sems, after)
    return list(outs[:n_arr]), list(outs[n_arr:])


def _piece_sum(a, dw, shard, core, received):
    name, full, axis = BIG[a]
    rows, cols = _received_shape(a)[1:]
    tr = _fit(rows, ROW_TILE)
    nbr = rows // tr

    def body(w_ref, dw_ref, rec_ref, o_ref):
        acc = dw_ref[...].astype(F32)
        for k in range(N_DEV - 1):
            acc = acc + rec_ref[k].astype(F32)
        o_ref[...] = acc

    if axis == 0:
        own = pl.BlockSpec((tr, cols), lambda i, w: (w[0] * nbr + i, w[1]))
    else:
        own = pl.BlockSpec((tr, cols), lambda i, w: (w[1] * nbr + i, w[0]))
    return pl.pallas_call(
        body, name="grad_sum_pieces_" + name,
        grid_spec=pltpu.PrefetchScalarGridSpec(
            num_scalar_prefetch=1, grid=(nbr,),
            in_specs=[own, pl.BlockSpec((N_DEV - 1, tr, cols), lambda i, w: (0, i, 0))],
            out_specs=pl.BlockSpec((tr, cols), lambda i, w: (i, 0))),
        out_shape=jax.ShapeDtypeStruct((rows, cols), F32),
        compiler_params=_params(("parallel",)),
    )(jnp.stack([shard, core]).astype(jnp.int32), dw, received)


def _scatter_begin(tag, arrs, dws):
    send_sems, recv_sems, dws, lands, token = _pieces_start(tag, arrs, dws)
    return (send_sems, recv_sems, dws, lands), token[0, 0]


def _scatter_reduce(tag, arrs, state, after):
    x, y, c = _me()
    send_sems, recv_sems, dws, lands = state
    dws, lands = _pieces_wait(tag, arrs, send_sems, recv_sems, dws, lands, after)
    g_own = [_piece_sum(a, dw, _shard_of((x, y)), c, r) for a, dw, r in zip(arrs, dws, lands)]
    send_sems, recv_sems, g_own, lands, token = _join_start(tag, g_own)
    return (send_sems, recv_sems, g_own, lands), token


def _scatter_end(tag, state, after):
    return _join_wait(tag, *state, after)


def _rows_start(tag, block):
    r, d = block.shape

    def body(src, land, send_sems, recv_sems, src_thru, land_thru, token):
        x, y, c = _me()
        for k, peer in enumerate(_peers(x, y, c)):
            _rcopy(src, land.at[_dev_index(x, y, c)], send_sems, recv_sems, k, peer).start()
        token[...] = jnp.zeros_like(token)

    outs = pl.pallas_call(
        body, name="rows_start_" + tag,
        in_specs=[HBM, HBM],
        out_specs=[SEM, SEM, HBM, HBM, VMEM_FULL],
        out_shape=[pltpu.SemaphoreType.DMA((N_DEV - 1,)), pltpu.SemaphoreType.DMA((N_DEV - 1,)),
                   pltpu.HBM((r, d), F32), pltpu.HBM((N_DEV, r, d), F32), jax.ShapeDtypeStruct((8, HEAD_DIM), F32)],
        input_output_aliases={0: 2, 1: 3},
        compiler_params=pltpu.CompilerParams(has_side_effects=EFFECT),
    )(_in_hbm(block), _in_hbm(lax.empty((N_DEV, r, d), F32)))
    return outs[0], outs[1], outs[2], outs[3], outs[4]


def _rows_wait(tag, send_sems, recv_sems, block, land, after):
    r, d = block.shape

    def body(src, land_, send_sems_, recv_sems_, after_ref, src_thru, land_thru):
        x, y, c = _me()
        for k, peer in enumerate(_peers(x, y, c)):
            cp = _rcopy(src, land_.at[_dev_index(*peer)], send_sems_, recv_sems_, k, peer)
            cp.wait_send()
            cp.wait_recv()

    outs = pl.pallas_call(
        body, name="rows_wait_" + tag,
        in_specs=[HBM, HBM, SEM, SEM, ANY],
        out_specs=[HBM, HBM],
        out_shape=[pltpu.HBM((r, d), F32), pltpu.HBM((N_DEV, r, d), F32)],
        input_output_aliases={0: 0, 1: 1},
        compiler_params=pltpu.CompilerParams(has_side_effects=EFFECT),
    )(block, land, send_sems, recv_sems, after)
    me = _dev_index(*_me())
    return lax.dynamic_update_slice(outs[1], outs[0][None], (me, 0, 0))


ADA_ROWS = 80
ADA_W = 6 * D_MODEL // N_SHARD


def _ada_forward(c_block, cctx_block, w_ada, b_shard):
    d = c_block.shape[1]

    def body(c_ref, cc_ref, w_ref, b_ref, act_ref, mods_ref, raw, mloc, send_sems, recv_sems):
        x, y, c = _me()
        me = _dev_index(x, y, c)
        s_me = _shard_of((x, y))
        raw[72:ADA_ROWS, :] = jnp.zeros((ADA_ROWS - 72, d), F32)
        raw[pl.ds(pl.multiple_of(me * 8, 8), 8), :] = c_ref[...]
        raw[64:72, :] = cc_ref[...]
        sends = []
        for mask in range(1, N_DEV):
            peer = (x ^ (mask >> 2), y ^ ((mask >> 1) & 1), c ^ (mask & 1))
            cp = _rcopy(c_ref, raw.at[pl.ds(pl.multiple_of(me * 8, 8), 8), :], send_sems, recv_sems, mask - 1, peer)
            cp.start()
            sends.append(cp)
        for mask in range(1, N_DEV):
            peer = (x ^ (mask >> 2), y ^ ((mask >> 1) & 1), c ^ (mask & 1))
            landed = raw.at[pl.ds(pl.multiple_of(_dev_index(*peer) * 8, 8), 8), :]
            _rcopy(landed, landed, send_sems, recv_sems, mask - 1, peer).wait_recv()
        v = raw[...]
        act = v * jax.nn.sigmoid(v)
        act_ref[...] = act
        mloc[...] = lax.dot_general(act.astype(BF16), w_ref[...].astype(BF16), NN,
                                    preferred_element_type=F32) + b_ref[...]
        mods_ref[s_me, 0:8, :] = mloc[pl.ds(pl.multiple_of(me * 8, 8), 8), :]
        mods_ref[s_me, 8:16, :] = mloc[64:72, :]
        base = N_DEV - 1
        for j, chip in enumerate(_other_chips(x, y)):
            peer = (*chip, c)
            rows = mloc.at[pl.ds(pl.multiple_of(_dev_index(*peer) * 8, 8), 8), :]
            cp = _rcopy(rows, mods_ref.at[s_me, 0:8, :], send_sems, recv_sems, base + 2 * j, peer)
            cp.start()
            sends.append(cp)
            cp = _rcopy(mloc.at[64:72, :], mods_ref.at[s_me, 8:16, :], send_sems, recv_sems, base + 2 * j + 1, peer)
            cp.start()
            sends.append(cp)
        for j, chip in enumerate(_other_chips(x, y)):
            for part in range(2):
                landed = mods_ref.at[_shard_of(chip), 8 * part:8 * part + 8, :]
                _rcopy(landed, landed, send_sems, recv_sems, base + 2 * j + part, (*chip, c)).wait_recv()
        for cp in sends:
            cp.wait_send()

    n_sem = N_DEV - 1 + 2 * N_CHIP_PEERS
    return pl.pallas_call(
        body, name="ada_forward",
        in_specs=[VMEM_FULL] * 4, out_specs=[VMEM_FULL, VMEM_FULL],
        out_shape=[jax.ShapeDtypeStruct((ADA_ROWS, d), F32), jax.ShapeDtypeStruct((N_SHARD, 16, ADA_W), F32)],
        scratch_shapes=[pltpu.VMEM((ADA_ROWS, d), F32), pltpu.VMEM((ADA_ROWS, ADA_W), F32),
                        pltpu.SemaphoreType.DMA((n_sem,)), pltpu.SemaphoreType.DMA((n_sem,))],
        compiler_params=pltpu.CompilerParams(vmem_limit_bytes=VMEM_LIMIT),
    )(c_block, cctx_block, w_ada, b_shard)


def _small_reduce(gathered):
    d = gathered.shape[2]

    def body(g_ref, o_ref):
        tot = g_ref[0]
        for i in range(1, N_DEV):
            tot = tot + g_ref[i]
        o_ref[...] = tot
        o_ref[0:2, :] = tot[0:2] + tot[6:8]
        o_ref[12:13, :] = jnp.broadcast_to(jnp.sum(tot[12:13], axis=1, keepdims=True), (1, d))

    return pl.pallas_call(body, name="small_reduce", in_specs=[VMEM_FULL], out_specs=VMEM_FULL,
                          out_shape=jax.ShapeDtypeStruct((16, d), F32))(gathered)


def _cctx_grad(gathered, c_ctx):
    d = gathered.shape[2]

    def body(g_ref, c_ref, o_ref):
        tot = g_ref[0, 0:1, :]
        for chip in range(1, N_SHARD):
            tot = tot + g_ref[2 * chip, 0:1, :]
        v = c_ref[...]
        sig = jax.nn.sigmoid(v)
        o_ref[...] = tot * (sig * (1.0 + v * (1.0 - sig)))

    return pl.pallas_call(body, name="cctx_grad", in_specs=[VMEM_FULL, VMEM_FULL], out_specs=VMEM_FULL,
                          out_shape=jax.ShapeDtypeStruct((1, d), F32))(gathered, c_ctx.reshape(1, d))


def _cast_into_full(w, shard, full, axis, name):
    r, cdim = w.shape
    tr = _fit(r, ROW_TILE)
    nbr = r // tr

    def body(s_ref, w_ref, o_ref):
        o_ref[...] = w_ref[...].astype(BF16)

    if axis == 0:
        out_spec = pl.BlockSpec((tr, cdim), lambda i, s: (s[0] * nbr + i, 0))
    else:
        out_spec = pl.BlockSpec((tr, cdim), lambda i, s: (i, s[0]))
    return pl.pallas_call(
        body, name=name,
        grid_spec=pltpu.PrefetchScalarGridSpec(
            num_scalar_prefetch=1, grid=(nbr,), in_specs=[pl.BlockSpec((tr, cdim), lambda i, s: (i, 0))],
            out_specs=out_spec),
        out_shape=jax.ShapeDtypeStruct(full, BF16), compiler_params=_params(("parallel",)),
    )(shard.reshape(1).astype(jnp.int32), w)


def _adamw_halves(w, g_own, g_other, m, v, core, axis, name):
    r, cdim = w.shape
    hr, hc = (r // 2, cdim) if axis == 1 else (r, cdim // 2)
    assert g_own.shape == (hr, hc) and g_other.shape == (hr, hc)
    tr = _fit(hr, 256)
    nb = hr // tr
    c1 = 1.0 - ADAM_B1 ** ADAM_STEP
    c2 = 1.0 - ADAM_B2 ** ADAM_STEP

    def body(c_ref, w_ref, go_ref, gt_ref, m_ref, v_ref, g_ref, d_ref, nm_ref, nv_ref):
        gv = jnp.where(pl.program_id(0) == c_ref[0], go_ref[...], gt_ref[...])
        nm = ADAM_B1 * m_ref[...] + (1.0 - ADAM_B1) * gv
        nv = ADAM_B2 * v_ref[...] + (1.0 - ADAM_B2) * (gv * gv)
        g_ref[...] = gv
        nm_ref[...] = nm
        nv_ref[...] = nv
        d_ref[...] = -ADAM_LR * ((nm / c1) / (jnp.sqrt(nv / c2) + ADAM_EPS) + ADAM_WD * w_ref[...])

    if axis == 1:
        big = pl.BlockSpec((tr, hc), lambda p, i, c: (p * nb + i, 0))
    else:
        big = pl.BlockSpec((tr, hc), lambda p, i, c: (i, p))
    own = pl.BlockSpec((tr, hc), lambda p, i, c: (jnp.where(p == c[0], i, 0), 0))
    other = pl.BlockSpec((tr, hc), lambda p, i, c: (jnp.where(p == c[0], 0, i), 0))
    sh = jax.ShapeDtypeStruct((r, cdim), F32)
    return pl.pallas_call(
        body, name=name,
        grid_spec=pltpu.PrefetchScalarGridSpec(
            num_scalar_prefetch=1, grid=(2, nb), in_specs=[big, own, other, big, big], out_specs=[big] * 4),
        out_shape=[sh] * 4, compiler_params=_params(("parallel", "parallel")),
    )(core.reshape(1).astype(jnp.int32), w, g_own, g_other, m, v)


def _adamw(w, g, m, v, name):
    r, cdim = w.shape
    tr = _fit(r, 128) if r % (ROW_TILE // 4) == 0 else r
    c1 = 1.0 - ADAM_B1 ** ADAM_STEP
    c2 = 1.0 - ADAM_B2 ** ADAM_STEP

    def body(w_ref, g_ref, m_ref, v_ref, d_ref, nm_ref, nv_ref):
        gv = g_ref[...]
        nm = ADAM_B1 * m_ref[...] + (1.0 - ADAM_B1) * gv
        nv = ADAM_B2 * v_ref[...] + (1.0 - ADAM_B2) * (gv * gv)
        nm_ref[...] = nm
        nv_ref[...] = nv
        d_ref[...] = -ADAM_LR * ((nm / c1) / (jnp.sqrt(nv / c2) + ADAM_EPS) + ADAM_WD * w_ref[...])

    spec = pl.BlockSpec((tr, cdim), lambda i: (i, 0))
    sh = jax.ShapeDtypeStruct((r, cdim), F32)
    return pl.pallas_call(body, name=name, grid=(r // tr,), in_specs=[spec] * 4, out_specs=[spec] * 3,
                          out_shape=[sh, sh, sh], compiler_params=_params(("parallel",)))(w, g, m, v)


def _adamw_small(ws, gs, ms, vs):
    k = len(ws)
    c1 = 1.0 - ADAM_B1 ** ADAM_STEP
    c2 = 1.0 - ADAM_B2 ** ADAM_STEP

    def body(*refs):
        w_refs, g_refs, m_refs, v_refs = refs[0:k], refs[k:2 * k], refs[2 * k:3 * k], refs[3 * k:4 * k]
        d_refs, nm_refs, nv_refs = refs[4 * k:5 * k], refs[5 * k:6 * k], refs[6 * k:7 * k]
        for i in range(k):
            gv = g_refs[i][...]
            nm = ADAM_B1 * m_refs[i][...] + (1.0 - ADAM_B1) * gv
            nv = ADAM_B2 * v_refs[i][...] + (1.0 - ADAM_B2) * (gv * gv)
            nm_refs[i][...] = nm
            nv_refs[i][...] = nv
            d_refs[i][...] = -ADAM_LR * ((nm / c1) / (jnp.sqrt(nv / c2) + ADAM_EPS) + ADAM_WD * w_refs[i][...])

    shapes = [jax.ShapeDtypeStruct(w.shape, F32) for w in ws]
    outs = pl.pallas_call(body, name="adamw_small", in_specs=[VMEM_FULL] * (4 * k), out_specs=[VMEM_FULL] * (3 * k),
                          out_shape=shapes * 3)(*ws, *gs, *ms, *vs)
    return outs[0:k], outs[k:2 * k], outs[2 * k:3 * k]


SMALL = (("c_ctx", D_MODEL), ("b_ada", 6 * D_MODEL), ("q_norm_g", HEAD_DIM), ("k_norm_g", HEAD_DIM),
         ("sink_logit", HEADS_A), ("ln1_g", D_MODEL), ("ln1_b", D_MODEL), ("ln2_g", D_MODEL), ("ln2_b", D_MODEL))
WEIGHT_ORDER = ("c_ctx", "w_ada", "b_ada", "w_in", "q_norm_g", "k_norm_g", "sink_logit", "w_out", "ln1_g", "ln1_b",
                "w_gate", "w_up", "w_down", "ln2_g", "ln2_b")


def kernel(x, c, ctx, c_ctx, w_ada, b_ada, w_in, q_norm_g, k_norm_g, sink_logit, w_out, ln1_g, ln1_b, w_gate, w_up, w_down, ln2_g, ln2_b, loss_target, m_c_ctx, m_w_ada, m_b_ada, m_w_in, m_q_norm_g, m_k_norm_g, m_sink_logit, m_w_out, m_ln1_g, m_ln1_b, m_w_gate, m_w_up, m_w_down, m_ln2_g, m_ln2_b, v_c_ctx, v_w_ada, v_b_ada, v_w_in, v_q_norm_g, v_k_norm_g, v_sink_logit, v_w_out, v_ln1_g, v_ln1_b, v_w_gate, v_w_up, v_w_down, v_ln2_g, v_ln2_b):
    d = D_MODEL
    w = dict(c_ctx=c_ctx, w_ada=w_ada[0], b_ada=b_ada, w_in=w_in[0], q_norm_g=q_norm_g, k_norm_g=k_norm_g,
             sink_logit=sink_logit, w_out=w_out[0], ln1_g=ln1_g, ln1_b=ln1_b, w_gate=w_gate[0], w_up=w_up[0],
             w_down=w_down[0], ln2_g=ln2_g, ln2_b=ln2_b)
    m = dict(c_ctx=m_c_ctx, w_ada=m_w_ada[0], b_ada=m_b_ada, w_in=m_w_in[0], q_norm_g=m_q_norm_g, k_norm_g=m_k_norm_g,
             sink_logit=m_sink_logit, w_out=m_w_out[0], ln1_g=m_ln1_g, ln1_b=m_ln1_b, w_gate=m_w_gate[0],
             w_up=m_w_up[0], w_down=m_w_down[0], ln2_g=m_ln2_g, ln2_b=m_ln2_b)
    v = dict(c_ctx=v_c_ctx, w_ada=v_w_ada[0], b_ada=v_b_ada, w_in=v_w_in[0], q_norm_g=v_q_norm_g, k_norm_g=v_k_norm_g,
             sink_logit=v_sink_logit, w_out=v_w_out[0], ln1_g=v_ln1_g, ln1_b=v_ln1_b, w_gate=v_w_gate[0],
             w_up=v_w_up[0], w_down=v_w_down[0], ln2_g=v_ln2_g, ln2_b=v_ln2_b)
    mx, my, mc = _me()
    s_me = _shard_of((mx, my))
    me = _dev_index(mx, my, mc)
    pad8 = lambda row: jnp.concatenate([row.reshape(1, -1), jnp.zeros((7, row.size), F32)], axis=0)

    b_shard = lax.dynamic_slice(b_ada, (0, s_me * ADA_W), (1, ADA_W))
    act, mods4 = _ada_forward(pad8(c), pad8(c_ctx), w["w_ada"], b_shard)

    gathers = []
    prev, shard = mods4, s_me
    for k, names in enumerate(W_GROUPS):
        arrs = tuple(BIG_INDEX[name] for name in names)
        bufs = [_cast_into_full(w[name], shard, BIG[a][1], BIG[a][2], "cast_" + name) for name, a in zip(names, arrs)]
        send_sems, recv_sems, thru, prev = _gather_start("g%d" % k, arrs, bufs, prev)
        shard = s_me + prev[0, 0].astype(jnp.int32)
        gathers.append((arrs, send_sems, recv_sems, thru))

    forwards = {}

    def prefetch(k, after):
        arrs, send_sems, recv_sems, thru = gathers[k]
        landed = _gather_wait("g%d" % k, arrs, send_sems, recv_sems, thru, after)
        fwd_send, fwd_recv, landed, token = _forward_start("g%d" % k, arrs, landed)
        forwards[k] = (fwd_send, fwd_recv, landed)
        return token[0, 0]

    def weights(k, after):
        arrs, send_sems, recv_sems, thru = gathers[k]
        if k in forwards:
            return _forward_wait("g%d" % k, arrs, *forwards[k], after)
        landed = _gather_wait("g%d" % k, arrs, send_sems, recv_sems, thru, after)
        return _gather_forward("g%d" % k, arrs, landed)

    mod = jnp.transpose(mods4[:, 0:1, :], (1, 0, 2)).reshape(1, 6 * d) + prev[0, 0]
    mod_ctx = jnp.transpose(mods4[:, 8:9, :], (1, 0, 2)).reshape(1, 6 * d)

    scatters = {}

    def grads_out(k, dws):
        arrs = tuple(BIG_INDEX[name] for name in G_GROUPS[k])
        scatters[k], zero = _scatter_begin("g%d" % k, arrs, dws)
        return zero

    grad_x, partial = _layer_fwd_bwd(x[0], ctx[0], loss_target[0], mod, mod_ctx, weights, prefetch, grads_out,
                                     q_norm_g, k_norm_g, sink_logit, ln1_g, ln1_b, ln2_g, ln2_b)
    grads, delta, new_m, new_v = {}, {}, {}, {}

    p_send, p_recv, partial, p_land, after = _rows_start("partials", partial)
    joins = []
    for k, names in enumerate(G_GROUPS):
        arrs = tuple(BIG_INDEX[name] for name in names)
        state, after = _scatter_reduce("g%d" % k, arrs, scatters[k], after)
        joins.append(state)
    for k, names in enumerate(G_GROUPS):
        g_own, g_other = _scatter_end("g%d" % k, joins[k], after)
        for name, own, other in zip(names, g_own, g_other):
            grads[name], delta[name], new_m[name], new_v[name] = _adamw_halves(
                w[name], own, other, m[name], v[name], mc, BIG[BIG_INDEX[name]][2], "adamw_" + name)
            after = new_v[name]

    gathered = _rows_wait("partials", p_send, p_recv, partial, p_land, after)
    tot = _small_reduce(gathered)
    grads["b_ada"] = tot[0:6].reshape(1, 6 * d)
    grads["ln1_g"], grads["ln1_b"], grads["ln2_g"], grads["ln2_b"] = tot[8:9], tot[9:10], tot[10:11], tot[11:12]
    grads["q_norm_g"] = tot[13:14, 0:HEAD_DIM]
    grads["k_norm_g"] = tot[13:14, HEAD_DIM:2 * HEAD_DIM]
    grads["sink_logit"] = tot[13:14, 2 * HEAD_DIM:2 * HEAD_DIM + HEADS_A]
    loss = tot[12, 0]

    dm_all = gathered[:, 0:6, :].reshape(N_DEV, 6 * d)
    dmc_tot = jnp.concatenate([tot[6:8].reshape(1, 2 * d), jnp.zeros((1, 4 * d), F32)], axis=1)
    dm_rows = jnp.concatenate([pad8(dm_all[i]) for i in range(N_DEV)] + [pad8(dmc_tot), jnp.zeros((8, 6 * d), F32)], axis=0)
    dm_shard = lax.dynamic_slice(dm_rows, (0, s_me * ADA_W), (ADA_ROWS, ADA_W))
    dmc_shard = lax.dynamic_slice(pad8(dmc_tot), (0, s_me * ADA_W), (8, ADA_W))
    cc_part = _matmul(dmc_shard, w["w_ada"], name="d_cctx", tb=True, tm=8, tn=1024, tk=1536, out_dtype=F32)
    c_send, c_recv, cc_part, c_land, c_token = _rows_start("cctx", cc_part)
    grads["w_ada"] = _matmul(act, dm_shard, name="dw_ada", ta=True, tm=1024, tn=1024, tk=ADA_ROWS, out_dtype=F32,
                             after=c_token)
    delta["w_ada"], new_m["w_ada"], new_v["w_ada"] = _adamw(w["w_ada"], grads["w_ada"], m["w_ada"], v["w_ada"],
                                                            "adamw_w_ada")
    gathered_cc = _rows_wait("cctx", c_send, c_recv, cc_part, c_land, new_v["w_ada"])
    grads["c_ctx"] = _cctx_grad(gathered_cc, c_ctx).reshape(d)
    rows = lambda t: [t[name].reshape(1, size) for name, size in SMALL]
    small = _adamw_small(rows(w), rows(grads), rows(m), rows(v))
    for k, (name, size) in enumerate(SMALL):
        delta[name], new_m[name], new_v[name] = [t[k].reshape(w[name].shape) for t in small]
        grads[name] = grads[name].reshape(w[name].shape)

    lead = lambda name, t: t[None] if name in ("w_ada", "w_in", "w_out", "w_gate", "w_up", "w_down") else t
    outs = [loss, grad_x[None]]
    for group in (grads, delta, new_m, new_v):
        outs += [lead(name, group[name]) for name in WEIGHT_ORDER]
    return tuple(outs)
```

```python
import functools
import math

import jax
import jax.numpy as jnp
from jax import lax
from jax.experimental import pallas as pl
from jax.experimental.pallas import tpu as pltpu

F32 = jnp.float32
BF16 = jnp.bfloat16
MESH = pl.DeviceIdType.MESH

D_MODEL = 2048
HEAD_DIM = 128
HEADS_A = 8
HEADS_B = 8
KV_A = 2
KV_B = 2
GROUP = 4
GRID_W = 64
WINDOW = 128
BLOCK = 128
FFN = 5632
IN_WIDTH = 3072
MIX_WIDTH = 2048
ROPE_THETA = 10000.0
EPS = 1e-6
ATTN_SCALE = HEAD_DIM ** -0.5
DN_ALPHA = 2.0 ** 0.25
N_SHARD = 4
N_DEV = 8

ADAM_LR = 0.001
ADAM_B1 = 0.9
ADAM_B2 = 0.999
ADAM_EPS = 1e-08
ADAM_WD = 0.01
ADAM_STEP = 10

QA0, KA0, VA0, QB0, KB0, VB0 = 0, 1024, 1280, 1536, 2560, 2816

VMEM_LIMIT = 56 * 1024 * 1024
ROW_TILE = 256
NN = (((1,), (0,)), ((), ()))
NT = (((1,), (1,)), ((), ()))
TN = (((0,), (0,)), ((), ()))


def _fit(total, pref):
    step = ROW_TILE // 4
    best = step
    for cand in range(step, pref + 1, step):
        if total % cand == 0:
            best = cand
    return best


def _params(sem=None):
    return pltpu.CompilerParams(dimension_semantics=sem, vmem_limit_bytes=VMEM_LIMIT)


def _matmul(a, b, *, name, ta=False, tb=False, tm, tn, tk, out_dtype, after=None):
    m = a.shape[1] if ta else a.shape[0]
    k = a.shape[0] if ta else a.shape[1]
    n = b.shape[0] if tb else b.shape[1]
    assert (b.shape[1] if tb else b.shape[0]) == k
    tm, tn, tk = min(tm, m), min(tn, n), min(tk, k)
    assert m % tm == 0 and n % tn == 0 and k % tk == 0, (name, m, n, k, tm, tn, tk)
    nk = k // tk
    dn = (((0 if ta else 1,), (1 if tb else 0,)), ((), ()))

    def product(a_ref, b_ref):
        return lax.dot_general(a_ref[...].astype(BF16), b_ref[...].astype(BF16), dn, preferred_element_type=F32)

    def body_whole_k(a_ref, b_ref, *rest):
        o_ref = rest[-1]
        o_ref[...] = product(a_ref, b_ref).astype(o_ref.dtype)

    def body(a_ref, b_ref, *rest):
        o_ref, acc_ref = rest[-2:]
        kk = pl.program_id(2)
        part = product(a_ref, b_ref)

        @pl.when(kk == 0)
        def _():
            acc_ref[...] = part

        @pl.when(kk != 0)
        def _():
            acc_ref[...] += part

        @pl.when(kk == nk - 1)
        def _():
            o_ref[...] = acc_ref[...].astype(o_ref.dtype)

    a_spec = (pl.BlockSpec((tk, tm), lambda i, j, kk: (kk, i)) if ta
              else pl.BlockSpec((tm, tk), lambda i, j, kk: (i, kk)))
    b_spec = (pl.BlockSpec((tn, tk), lambda i, j, kk: (j, kk)) if tb
              else pl.BlockSpec((tk, tn), lambda i, j, kk: (kk, j)))
    return pl.pallas_call(
        body_whole_k if nk == 1 else body, name=name, grid=(m // tm, n // tn, nk),
        in_specs=[a_spec, b_spec] + ([] if after is None else [pl.BlockSpec(memory_space=pl.ANY)]),
        out_specs=pl.BlockSpec((tm, tn), lambda i, j, kk: (i, j)),
        out_shape=jax.ShapeDtypeStruct((m, n), out_dtype),
        scratch_shapes=[] if nk == 1 else [pltpu.VMEM((tm, tn), F32)],
        compiler_params=_params(("parallel", "parallel", "arbitrary")),
    )(a, b, *([] if after is None else [after]))


def _modulate_rows(x, ctx, mods):
    n, d = x.shape
    c = ctx.shape[0]
    nx = n // ROW_TILE
    assert c == ROW_TILE

    def body(x_ref, ctx_ref, mods_ref, o_ref):
        i = pl.program_id(0)

        @pl.when(i < nx)
        def _():
            o_ref[...] = (x_ref[...] * (1.0 + mods_ref[0:1, :]) + mods_ref[1:2, :]).astype(BF16)

        @pl.when(i >= nx)
        def _():
            o_ref[...] = (ctx_ref[...] * (1.0 + mods_ref[2:3, :]) + mods_ref[3:4, :]).astype(BF16)

    return pl.pallas_call(
        body, name="modulate_rows", grid=(nx + 1,),
        in_specs=[pl.BlockSpec((ROW_TILE, d), lambda i: (jnp.minimum(i, nx - 1), 0)),
                  pl.BlockSpec((ROW_TILE, d), lambda i: (0, 0)),
                  pl.BlockSpec((8, d), lambda i: (0, 0))],
        out_specs=pl.BlockSpec((ROW_TILE, d), lambda i: (i, 0)),
        out_shape=jax.ShapeDtypeStruct((n + c, d), BF16),
        compiler_params=_params(("parallel",)),
    )(x, ctx, mods)


def _rope_tables(n, c):
    rows = n // GRID_W
    row_ids = jnp.repeat(jnp.arange(rows, dtype=F32), GRID_W)
    col_ids = jnp.tile(jnp.arange(GRID_W, dtype=F32), rows)
    axis_dim = HEAD_DIM // 2
    inv_freq = jnp.power(ROPE_THETA, -jnp.arange(0, axis_dim, 2, dtype=F32) / axis_dim)
    ang_r = row_ids[:, None] * inv_freq
    ang_c = col_ids[:, None] * inv_freq
    ang = jnp.concatenate([ang_r, ang_r, ang_c, ang_c], axis=-1)
    cos, sin = jnp.cos(ang), jnp.sin(ang)
    quarter = (jnp.arange(HEAD_DIM) // (HEAD_DIM // 4)) % 2
    sin_a = jnp.where(quarter == 0, -sin, 0.0)
    sin_b = jnp.where(quarter == 1, sin, 0.0)
    pad = lambda t, v: jnp.concatenate([t, jnp.full((c, HEAD_DIM), v, F32)], axis=0)
    return pad(cos, 1.0), pad(sin_a, 0.0), pad(sin_b, 0.0)


def _rope(x, cos, sin_a, sin_b):
    return x * cos + pltpu.roll(x, 96, 1) * sin_a + pltpu.roll(x, 32, 1) * sin_b


def _rope_t(dy, cos, sin_a, sin_b):
    return dy * cos - pltpu.roll(dy, 96, 1) * sin_a - pltpu.roll(dy, 32, 1) * sin_b


def _rms(x):
    r = lax.rsqrt(jnp.mean(x * x, axis=-1, keepdims=True) + EPS)
    return x * r, r


def _qkv_post(h_all, cos, sin_a, sin_b, q_g, k_g):
    t = h_all.shape[0]
    nt = t // ROW_TILE

    def body(h_ref, cos_ref, sa_ref, sb_ref, qg_ref, kg_ref, qa_ref, ka_ref, va_ref, qb_ref, kb_ref, vb_ref):
        cos_, sa, sb = cos_ref[...], sa_ref[...], sb_ref[...]
        sl = lambda off, hh: h_ref[:, off + hh * HEAD_DIM: off + (hh + 1) * HEAD_DIM]
        for hh in range(HEADS_A):
            qa_ref[hh] = (_rope(sl(QA0, hh), cos_, sa, sb) * ATTN_SCALE).astype(BF16)
        for hh in range(KV_A):
            ka_ref[hh] = _rope(sl(KA0, hh), cos_, sa, sb).astype(BF16)
            va_ref[hh] = sl(VA0, hh).astype(BF16)
        for hh in range(HEADS_B):
            xn, _ = _rms(sl(QB0, hh))
            qb_ref[hh] = (_rope(xn * qg_ref[...], cos_, sa, sb) * ATTN_SCALE).astype(BF16)
        for hh in range(KV_B):
            xn, _ = _rms(sl(KB0, hh))
            kb_ref[hh] = _rope(xn * kg_ref[...], cos_, sa, sb).astype(BF16)
            vb_ref[hh] = sl(VB0, hh).astype(BF16)

    tab = pl.BlockSpec((ROW_TILE, HEAD_DIM), lambda i: (i, 0))
    gain = pl.BlockSpec((1, HEAD_DIM), lambda i: (0, 0))
    hs = lambda nh: pl.BlockSpec((nh, ROW_TILE, HEAD_DIM), lambda i: (0, i, 0))
    sh = lambda nh: jax.ShapeDtypeStruct((nh, t, HEAD_DIM), BF16)
    return pl.pallas_call(
        body, name="qkv_post", grid=(nt,),
        in_specs=[pl.BlockSpec((ROW_TILE, IN_WIDTH), lambda i: (i, 0)), tab, tab, tab, gain, gain],
        out_specs=[hs(HEADS_A), hs(KV_A), hs(KV_A), hs(HEADS_B), hs(KV_B), hs(KV_B)],
        out_shape=[sh(HEADS_A), sh(KV_A), sh(KV_A), sh(HEADS_B), sh(KV_B), sh(KV_B)],
        compiler_params=_params(("parallel",)),
    )(h_all, cos, sin_a, sin_b, q_g, k_g)


def _qkv_bwd_post(h_all, cos, sin_a, sin_b, q_g, k_g, dqa, dka, dva, dqb, dkb, dvb, n):
    t = h_all.shape[0]
    nt = t // ROW_TILE
    nx = n // ROW_TILE

    def body(h_ref, cos_ref, sa_ref, sb_ref, qg_ref, kg_ref,
             dqa_ref, dka_ref, dva_ref, dqb_ref, dkb_ref, dvb_ref, dh_ref, gs_ref):
        i = pl.program_id(0)
        cos_, sa, sb = cos_ref[...], sa_ref[...], sb_ref[...]
        latent = (i < nx).astype(F32)
        sl = lambda off, hh: h_ref[:, off + hh * HEAD_DIM: off + (hh + 1) * HEAD_DIM]

        def put(off, hh, val):
            dh_ref[:, off + hh * HEAD_DIM: off + (hh + 1) * HEAD_DIM] = val.astype(BF16)

        def norm_bwd(x, gain, dy):
            xn, r = _rms(x)
            dxh = dy * gain
            dx = r * (dxh - xn * jnp.mean(dxh * xn, axis=-1, keepdims=True))
            return dx, jnp.sum(dy * xn, axis=0, keepdims=True)

        for hh in range(HEADS_A):
            put(QA0, hh, _rope_t(dqa_ref[hh] * (ATTN_SCALE * latent), cos_, sa, sb))
        for hh in range(KV_A):
            put(KA0, hh, _rope_t(dka_ref[hh], cos_, sa, sb))
            put(VA0, hh, dva_ref[hh])
        gq = jnp.zeros((1, HEAD_DIM), F32)
        gk = jnp.zeros((1, HEAD_DIM), F32)
        for hh in range(HEADS_B):
            dq_t = dqb_ref[hh // GROUP, :, (hh % GROUP) * ROW_TILE:(hh % GROUP + 1) * ROW_TILE]
            dy = _rope_t(dq_t.T * (ATTN_SCALE * latent), cos_, sa, sb)
            dx, g = norm_bwd(sl(QB0, hh), qg_ref[...], dy)
            put(QB0, hh, dx)
            gq = gq + g
        for hh in range(KV_B):
            dy = _rope_t(dkb_ref[hh], cos_, sa, sb)
            dx, g = norm_bwd(sl(KB0, hh), kg_ref[...], dy)
            put(KB0, hh, dx)
            gk = gk + g
            put(VB0, hh, dvb_ref[hh])
        upd = jnp.concatenate([gq, gk, jnp.zeros((6, HEAD_DIM), F32)], axis=0)

        @pl.when(i == 0)
        def _():
            gs_ref[...] = upd

        @pl.when(i != 0)
        def _():
            gs_ref[...] += upd

    tab = pl.BlockSpec((ROW_TILE, HEAD_DIM), lambda i: (i, 0))
    gain = pl.BlockSpec((1, HEAD_DIM), lambda i: (0, 0))
    lat = lambda nh: pl.BlockSpec((nh, ROW_TILE, HEAD_DIM), lambda i: (0, jnp.minimum(i, nx - 1), 0))
    full = lambda nh: pl.BlockSpec((nh, ROW_TILE, HEAD_DIM), lambda i: (0, i, 0))
    return pl.pallas_call(
        body, name="qkv_bwd_post", grid=(nt,),
        in_specs=[pl.BlockSpec((ROW_TILE, IN_WIDTH), lambda i: (i, 0)), tab, tab, tab, gain, gain,
                  lat(HEADS_A), full(KV_A), full(KV_A),
                  pl.BlockSpec((KV_B, None, HEAD_DIM, GROUP * ROW_TILE), lambda i: (0, jnp.minimum(i, nx - 1), 0, 0)),
                  full(KV_B), full(KV_B)],
        out_specs=[pl.BlockSpec((ROW_TILE, IN_WIDTH), lambda i: (i, 0)),
                   pl.BlockSpec((8, HEAD_DIM), lambda i: (0, 0))],
        out_shape=[jax.ShapeDtypeStruct((t, IN_WIDTH), BF16), jax.ShapeDtypeStruct((8, HEAD_DIM), F32)],
        compiler_params=_params(("arbitrary",)),
    )(h_all, cos, sin_a, sin_b, q_g, k_g, dqa, dka, dva, dqb, dkb, dvb)


GB_TQ = 256
GB_TK = 256


def _heads_rows(ref2d, tq):
    return jnp.concatenate([ref2d[:, hh * HEAD_DIM:(hh + 1) * HEAD_DIM] for hh in range(GROUP)], axis=0)


def _attn_b_fwd(qb, kb, vb, n):
    t = kb.shape[1]
    nk = t // GB_TK
    tq = GB_TQ
    nq = n // tq
    qb_step = 2 if nq % 2 == 0 else 1
    rows = qb_step * GROUP * tq

    def body(q_ref, k_ref, v_ref, o_ref, lse_ref, m_s, l_s, acc_s):
        blocks = [(s, hh) for s in range(qb_step) for hh in range(GROUP)]
        q = jnp.concatenate([q_ref[hh, s * tq:(s + 1) * tq, :] for s, hh in blocks], axis=0)
        m_s[...] = jnp.full((1, rows), -jnp.inf, F32)
        l_s[...] = jnp.zeros((1, rows), F32)
        acc_s[...] = jnp.zeros((HEAD_DIM, rows), F32)

        def scores(j):
            start = pl.multiple_of(j * GB_TK, GB_TK)
            return lax.dot_general(k_ref[pl.ds(start, GB_TK), :], q, NT, preferred_element_type=F32)

        def step(j, carry):
            st = scores(j)
            vs = v_ref[pl.ds(pl.multiple_of(j * GB_TK, GB_TK), GB_TK), :]
            m_prev = m_s[...]
            m_new = jnp.maximum(m_prev, jnp.max(st, axis=0, keepdims=True))
            p = jnp.exp(st - m_new)
            alpha = jnp.exp(m_prev - m_new)
            l_s[...] = alpha * l_s[...] + jnp.sum(p, axis=0, keepdims=True)
            acc_s[...] = alpha * acc_s[...] + lax.dot_general(vs, p.astype(BF16), TN, preferred_element_type=F32)
            m_s[...] = m_new
            return carry

        lax.fori_loop(0, nk, step, 0, unroll=4)
        ot = acc_s[...] * (1.0 / l_s[...])
        lse = m_s[...] + jnp.log(l_s[...])
        width = GROUP * tq
        for s in range(qb_step):
            lse_ref[s] = lse[:, s * width:(s + 1) * width]
        for k, (s, hh) in enumerate(blocks):
            o_ref[s * tq:(s + 1) * tq, hh * HEAD_DIM:(hh + 1) * HEAD_DIM] = ot[:, k * tq:(k + 1) * tq].T.astype(BF16)

    return pl.pallas_call(
        body, name="attn_b_fwd", grid=(KV_B, nq // qb_step),
        in_specs=[pl.BlockSpec((GROUP, qb_step * tq, HEAD_DIM), lambda g, i: (g, i, 0)),
                  pl.BlockSpec((None, t, HEAD_DIM), lambda g, i: (g, 0, 0)),
                  pl.BlockSpec((None, t, HEAD_DIM), lambda g, i: (g, 0, 0))],
        out_specs=[pl.BlockSpec((qb_step * tq, GROUP * HEAD_DIM), lambda g, i: (i, KV_A + g)),
                   pl.BlockSpec((None, qb_step, 1, GROUP * tq), lambda g, i: (g, i, 0, 0))],
        out_shape=[jax.ShapeDtypeStruct((n, MIX_WIDTH), BF16),
                   jax.ShapeDtypeStruct((KV_B, nq, 1, GROUP * tq), F32)],
        scratch_shapes=[pltpu.VMEM((1, rows), F32), pltpu.VMEM((1, rows), F32), pltpu.VMEM((HEAD_DIM, rows), F32)],
        compiler_params=_params(("parallel", "parallel")),
    )(qb, kb, vb)


def _attn_b_bwd(qb, kb, vb, dheads, lse, delta, n):
    t = kb.shape[1]
    nk = t // GB_TK
    tq = GB_TQ
    nq = n // tq
    rows = GROUP * tq

    qb_step = 4 if nq % 4 == 0 else 1

    def body(q_ref, k_ref, v_ref, do_ref, lse_ref, dl_ref, dq_ref, dk_ref, dv_ref):
        j = pl.program_id(1)
        i = pl.program_id(2)

        blocks = [(s, hh) for s in range(qb_step) for hh in range(GROUP)]
        q = jnp.concatenate([q_ref[hh, s * tq:(s + 1) * tq, :] for s, hh in blocks], axis=0)
        do = jnp.concatenate([do_ref[s * tq:(s + 1) * tq, hh * HEAD_DIM:(hh + 1) * HEAD_DIM] for s, hh in blocks], axis=0)
        lse_row = jnp.concatenate([lse_ref[s] for s in range(qb_step)], axis=1)
        dl_row = jnp.concatenate([dl_ref[s] for s in range(qb_step)], axis=1)
        ks, vs = k_ref[...], v_ref[...]
        st = lax.dot_general(ks, q, NT, preferred_element_type=F32)
        p = jnp.exp(st - lse_row)
        dpt = lax.dot_general(vs, do, NT, preferred_element_type=F32)
        ds = (p * (dpt - dl_row)).astype(BF16)
        dv_part = lax.dot_general(p.astype(BF16), do, NN, preferred_element_type=F32)
        dk_part = lax.dot_general(ds, q, NN, preferred_element_type=F32)
        dq_part = lax.dot_general(ks, ds, TN, preferred_element_type=F32)

        @pl.when(i == 0)
        def _():
            dk_ref[...] = dk_part
            dv_ref[...] = dv_part

        @pl.when(i != 0)
        def _():
            dk_ref[...] += dk_part
            dv_ref[...] += dv_part

        for s in range(qb_step):
            piece = dq_part[:, s * rows:(s + 1) * rows]

            @pl.when(j == 0)
            def _():
                dq_ref[qb_step * i + s] = piece

            @pl.when(j != 0)
            def _():
                dq_ref[qb_step * i + s] += piece

    kv = pl.BlockSpec((None, GB_TK, HEAD_DIM), lambda g, j, i: (g, j, 0))
    row = pl.BlockSpec((None, qb_step, 1, rows), lambda g, j, i: (g, i, 0, 0))
    return pl.pallas_call(
        body, name="attn_b_bwd", grid=(KV_B, nk, nq // qb_step),
        in_specs=[pl.BlockSpec((GROUP, qb_step * tq, HEAD_DIM), lambda g, j, i: (g, i, 0)), kv, kv,
                  pl.BlockSpec((qb_step * tq, GROUP * HEAD_DIM), lambda g, j, i: (i, KV_A + g)), row, row],
        out_specs=[pl.BlockSpec((None, nq, HEAD_DIM, rows), lambda g, j, i: (g, 0, 0, 0)), kv, kv],
        out_shape=[jax.ShapeDtypeStruct((KV_B, nq, HEAD_DIM, rows), F32),
                   jax.ShapeDtypeStruct((KV_B, t, HEAD_DIM), F32),
                   jax.ShapeDtypeStruct((KV_B, t, HEAD_DIM), F32)],
        compiler_params=_params(("parallel", "arbitrary", "arbitrary")),
    )(qb, kb, vb, dheads, lse, delta)


def _delta_rows(dheads, heads):
    n = heads.shape[0]
    tq = GB_TQ
    w = GROUP * HEAD_DIM

    def body(a_ref, b_ref, o_ref):
        prod = a_ref[...].astype(F32) * b_ref[...].astype(F32)
        cols = [jnp.sum(prod[:, hh * HEAD_DIM:(hh + 1) * HEAD_DIM].T, axis=0, keepdims=True) for hh in range(GROUP)]
        o_ref[...] = jnp.concatenate(cols, axis=1)

    blk = pl.BlockSpec((tq, w), lambda g, i: (i, KV_A + g))
    return pl.pallas_call(
        body, name="delta_rows", grid=(KV_B, n // tq),
        in_specs=[blk, blk],
        out_specs=pl.BlockSpec((None, None, 1, GROUP * tq), lambda g, i: (g, i, 0, 0)),
        out_shape=jax.ShapeDtypeStruct((KV_B, n // tq, 1, GROUP * tq), F32),
        compiler_params=_params(("parallel", "parallel")),
    )(dheads, heads)


KWIN = 3 * BLOCK


def _window_scores(q, k_ref, j, n, nb, sink_row):
    c = k_ref.shape[0] - n
    start = pl.multiple_of(jnp.clip(j - 1, 0, nb - 3) * BLOCK, BLOCK)
    kw = k_ref[pl.ds(start, KWIN), :]
    kc = k_ref[pl.ds(n, c), :]
    s_loc = lax.dot_general(kw, q, NT, preferred_element_type=F32)
    s_ctx = lax.dot_general(kc, q, NT, preferred_element_type=F32)
    cols = GROUP * BLOCK
    qpos = j * BLOCK + lax.broadcasted_iota(jnp.int32, (KWIN, cols), 1) % BLOCK
    kpos = start + lax.broadcasted_iota(jnp.int32, (KWIN, cols), 0)
    s_loc = jnp.where(jnp.abs(qpos - kpos) <= WINDOW, s_loc, -jnp.inf)
    m = jnp.maximum(jnp.maximum(jnp.max(s_loc, axis=0, keepdims=True), jnp.max(s_ctx, axis=0, keepdims=True)),
                    sink_row)
    e_loc, e_ctx, e_sink = jnp.exp(s_loc - m), jnp.exp(s_ctx - m), jnp.exp(sink_row - m)
    inv = 1.0 / (jnp.sum(e_loc, axis=0, keepdims=True) + jnp.sum(e_ctx, axis=0, keepdims=True) + e_sink)
    return e_loc * inv, e_ctx * inv, e_sink * inv, start


def _sink_row(sink_ref, g):
    return jnp.concatenate([sink_ref[pl.ds(g * GROUP + hh, 1), :] for hh in range(GROUP)], axis=1)


def _attn_a_fwd(qa, ka, va, sink_b, heads_b, n):
    t = ka.shape[1]
    nb = n // BLOCK
    assert nb >= 3
    wb = 4 if nb % 4 == 0 else 1

    def body(q_ref, k_ref, v_ref, sink_ref, heads_b_ref, o_ref):
        g, jj = pl.program_id(0), pl.program_id(1)
        for s in range(wb):
            j = jj * wb + s
            rows = slice(s * BLOCK, (s + 1) * BLOCK)
            q = q_ref[:, rows, :].reshape(GROUP * BLOCK, HEAD_DIM)
            p_loc, p_ctx, _, start = _window_scores(q, k_ref, j, n, nb, _sink_row(sink_ref, g))
            vw = v_ref[pl.ds(start, KWIN), :]
            vc = v_ref[pl.ds(n, t - n), :]
            ot = (lax.dot_general(vw, p_loc.astype(BF16), TN, preferred_element_type=F32)
                  + lax.dot_general(vc, p_ctx.astype(BF16), TN, preferred_element_type=F32))
            for hh in range(GROUP):
                o_ref[rows, hh * HEAD_DIM:(hh + 1) * HEAD_DIM] = ot[:, hh * BLOCK:(hh + 1) * BLOCK].T.astype(BF16)

    return pl.pallas_call(
        body, name="attn_a_fwd", grid=(KV_A, nb // wb),
        in_specs=[pl.BlockSpec((GROUP, wb * BLOCK, HEAD_DIM), lambda g, j: (g, j, 0)),
                  pl.BlockSpec((None, t, HEAD_DIM), lambda g, j: (g, 0, 0)),
                  pl.BlockSpec((None, t, HEAD_DIM), lambda g, j: (g, 0, 0)),
                  pl.BlockSpec((HEADS_A, HEAD_DIM), lambda g, j: (0, 0)),
                  pl.BlockSpec(memory_space=pl.ANY)],
        out_specs=pl.BlockSpec((wb * BLOCK, GROUP * HEAD_DIM), lambda g, j: (j, g)),
        out_shape=jax.ShapeDtypeStruct((n, MIX_WIDTH), BF16),
        input_output_aliases={4: 0},
        compiler_params=_params(("parallel", "parallel")),
    )(qa, ka, va, sink_b, heads_b)


def _attn_a_bwd(qa, ka, va, sink_b, dheads, n):
    t = ka.shape[1]
    c = t - n
    nb = n // BLOCK
    wb = 4 if nb % 4 == 0 else 1

    def body(q_ref, k_ref, v_ref, sink_ref, do_ref, dq_ref, dk_ref, dv_ref, dsink_ref):
        g, jj = pl.program_id(0), pl.program_id(1)

        @pl.when(jj == 0)
        def _():
            dk_ref[...] = jnp.zeros_like(dk_ref)
            dv_ref[...] = jnp.zeros_like(dv_ref)
            dsink_ref[...] = jnp.zeros_like(dsink_ref)

        for s in range(wb):
            j = jj * wb + s
            rows = slice(s * BLOCK, (s + 1) * BLOCK)
            q = q_ref[:, rows, :].reshape(GROUP * BLOCK, HEAD_DIM)
            do = _heads_rows(do_ref[rows, :], BLOCK)
            p_loc, p_ctx, p_sink, start = _window_scores(q, k_ref, j, n, nb, _sink_row(sink_ref, g))
            kw, vw = k_ref[pl.ds(start, KWIN), :], v_ref[pl.ds(start, KWIN), :]
            kc, vc = k_ref[pl.ds(n, c), :], v_ref[pl.ds(n, c), :]
            dp_loc = lax.dot_general(vw, do, NT, preferred_element_type=F32)
            dp_ctx = lax.dot_general(vc, do, NT, preferred_element_type=F32)
            dl = jnp.sum(p_loc * dp_loc, axis=0, keepdims=True) + jnp.sum(p_ctx * dp_ctx, axis=0, keepdims=True)
            ds_loc = (p_loc * (dp_loc - dl)).astype(BF16)
            ds_ctx = (p_ctx * (dp_ctx - dl)).astype(BF16)
            dqt = (lax.dot_general(kw, ds_loc, TN, preferred_element_type=F32)
                   + lax.dot_general(kc, ds_ctx, TN, preferred_element_type=F32))
            for hh in range(GROUP):
                dq_ref[hh, rows, :] = dqt[:, hh * BLOCK:(hh + 1) * BLOCK].T
            dk_ref[pl.ds(start, KWIN), :] += lax.dot_general(ds_loc, q, NN, preferred_element_type=F32)
            dv_ref[pl.ds(start, KWIN), :] += lax.dot_general(p_loc.astype(BF16), do, NN, preferred_element_type=F32)
            dk_ref[pl.ds(n, c), :] += lax.dot_general(ds_ctx, q, NN, preferred_element_type=F32)
            dv_ref[pl.ds(n, c), :] += lax.dot_general(p_ctx.astype(BF16), do, NN, preferred_element_type=F32)
            dsk = -(p_sink * dl)
            upd = [jnp.broadcast_to(jnp.sum(dsk[:, hh * BLOCK:(hh + 1) * BLOCK], axis=1, keepdims=True), (1, HEAD_DIM))
                   for hh in range(GROUP)]
            dsink_ref[...] += jnp.concatenate(upd + [jnp.zeros((8 - GROUP, HEAD_DIM), F32)], axis=0)

    res = pl.BlockSpec((None, t, HEAD_DIM), lambda g, j: (g, 0, 0))
    return pl.pallas_call(
        body, name="attn_a_bwd", grid=(KV_A, nb // wb),
        in_specs=[pl.BlockSpec((GROUP, wb * BLOCK, HEAD_DIM), lambda g, j: (g, j, 0)), res, res,
                  pl.BlockSpec((HEADS_A, HEAD_DIM), lambda g, j: (0, 0)),
                  pl.BlockSpec((wb * BLOCK, GROUP * HEAD_DIM), lambda g, j: (j, g))],
        out_specs=[pl.BlockSpec((GROUP, wb * BLOCK, HEAD_DIM), lambda g, j: (g, j, 0)), res, res,
                   pl.BlockSpec((None, 8, HEAD_DIM), lambda g, j: (g, 0, 0))],
        out_shape=[jax.ShapeDtypeStruct((HEADS_A, n, HEAD_DIM), F32),
                   jax.ShapeDtypeStruct((KV_A, t, HEAD_DIM), F32),
                   jax.ShapeDtypeStruct((KV_A, t, HEAD_DIM), F32),
                   jax.ShapeDtypeStruct((KV_A, 8, HEAD_DIM), F32)],
        compiler_params=_params(("parallel", "arbitrary")),
    )(qa, ka, va, sink_b, dheads)


def _ln_stats(r):
    mu = jnp.mean(r, axis=-1, keepdims=True)
    cen = r - mu
    rstd = lax.rsqrt(jnp.mean(cen * cen, axis=-1, keepdims=True) + EPS)
    return cen * rstd, rstd


def _ln_bwd(dy, xhat, rstd, gain):
    dxh = dy * gain
    return rstd * (dxh - jnp.mean(dxh, axis=-1, keepdims=True)
                   - xhat * jnp.mean(dxh * xhat, axis=-1, keepdims=True))


def _accumulate_rows(ref, rows, i):
    pad = [jnp.zeros_like(rows[0])] * (8 - len(rows))
    upd = jnp.concatenate(rows + pad, axis=0)

    @pl.when(i == 0)
    def _():
        ref[...] = upd

    @pl.when(i != 0)
    def _():
        ref[...] += upd


def _colsum(v):
    return jnp.sum(v, axis=0, keepdims=True)


LN_TILE = 256


def _res_ln1(x, a, vec):
    n, d = x.shape

    def body(x_ref, a_ref, v_ref, xh_ref, rs_ref, u_ref):
        r1 = DN_ALPHA * x_ref[...] + v_ref[0:1, :] * a_ref[...]
        xhat, rstd = _ln_stats(r1)
        xh_ref[...] = xhat
        rs_ref[...] = rstd
        x1 = xhat * v_ref[1:2, :] + v_ref[2:3, :]
        u_ref[...] = (x1 * (1.0 + v_ref[3:4, :]) + v_ref[4:5, :]).astype(BF16)

    row = pl.BlockSpec((LN_TILE, d), lambda i: (i, 0))
    return pl.pallas_call(
        body, name="res_ln1", grid=(n // LN_TILE,),
        in_specs=[row, row, pl.BlockSpec((8, d), lambda i: (0, 0))],
        out_specs=[row, pl.BlockSpec((LN_TILE, 1), lambda i: (i, 0)), row],
        out_shape=[jax.ShapeDtypeStruct((n, d), F32), jax.ShapeDtypeStruct((n, 1), F32),
                   jax.ShapeDtypeStruct((n, d), BF16)],
        compiler_params=_params(("parallel",)),
    )(x, a, vec)


def _res_ln2_loss(xhat1, f, target, vec):
    n, d = f.shape

    def body(xh_ref, f_ref, t_ref, v_ref, dr_ref, df_ref, s_ref):
        i = pl.program_id(0)
        x1 = xh_ref[...] * v_ref[1:2, :] + v_ref[2:3, :]
        fv = f_ref[...]
        xhat, rstd = _ln_stats(DN_ALPHA * x1 + v_ref[0:1, :] * fv)
        err = xhat * v_ref[3:4, :] + v_ref[4:5, :] - t_ref[...]
        dy = err * (1.0 / d)
        dr2 = _ln_bwd(dy, xhat, rstd, v_ref[3:4, :])
        dr_ref[...] = dr2
        df_ref[...] = (dr2 * v_ref[0:1, :]).astype(BF16)
        _accumulate_rows(s_ref, [_colsum(dy * xhat), _colsum(dy), _colsum(dr2 * fv),
                                 _colsum(err * err) * (0.5 / d)], i)

    row = pl.BlockSpec((LN_TILE, d), lambda i: (i, 0))
    return pl.pallas_call(
        body, name="res_ln2_loss", grid=(n // LN_TILE,),
        in_specs=[row, row, row, pl.BlockSpec((8, d), lambda i: (0, 0))],
        out_specs=[row, row, pl.BlockSpec((8, d), lambda i: (0, 0))],
        out_shape=[jax.ShapeDtypeStruct((n, d), F32), jax.ShapeDtypeStruct((n, d), BF16),
                   jax.ShapeDtypeStruct((8, d), F32)],
        compiler_params=_params(("arbitrary",)),
    )(xhat1, f, target, vec)


def _ln1_bwd(du2, dr2, xhat1, rstd1, a, vec):
    n, d = du2.shape

    def body(du_ref, dr2_ref, xh_ref, rs_ref, a_ref, v_ref, dxp_ref, da_ref, s_ref):
        i = pl.program_id(0)
        du, xhat = du_ref[...], xh_ref[...]
        x1 = xhat * v_ref[1:2, :] + v_ref[2:3, :]
        dx1 = DN_ALPHA * dr2_ref[...] + du * (1.0 + v_ref[0:1, :])
        dr1 = _ln_bwd(dx1, xhat, rs_ref[...], v_ref[1:2, :])
        dxp_ref[...] = DN_ALPHA * dr1
        da_ref[...] = (dr1 * v_ref[3:4, :]).astype(BF16)
        _accumulate_rows(s_ref, [_colsum(du * x1), _colsum(du), _colsum(dx1 * xhat), _colsum(dx1),
                                 _colsum(dr1 * a_ref[...])], i)

    row = pl.BlockSpec((LN_TILE, d), lambda i: (i, 0))
    return pl.pallas_call(
        body, name="ln1_bwd", grid=(n // LN_TILE,),
        in_specs=[row, row, row, pl.BlockSpec((LN_TILE, 1), lambda i: (i, 0)), row,
                  pl.BlockSpec((8, d), lambda i: (0, 0))],
        out_specs=[row, row, pl.BlockSpec((8, d), lambda i: (0, 0))],
        out_shape=[jax.ShapeDtypeStruct((n, d), F32), jax.ShapeDtypeStruct((n, d), BF16),
                   jax.ShapeDtypeStruct((8, d), F32)],
        compiler_params=_params(("arbitrary",)),
    )(du2, dr2, xhat1, rstd1, a, vec)


def _mod1_bwd(du_all, dxp, x, ctx, mods):
    n, d = x.shape
    nx = n // ROW_TILE

    def body(du_ref, dxp_ref, x_ref, ctx_ref, m_ref, gx_ref, s_ref):
        i = pl.program_id(0)
        du = du_ref[...]
        zero = jnp.zeros((1, d), F32)

        @pl.when(i == 0)
        def _():
            s_ref[...] = jnp.zeros_like(s_ref)

        @pl.when(i < nx)
        def _():
            gx_ref[...] = dxp_ref[...] + du * (1.0 + m_ref[0:1, :])
            s_ref[...] += jnp.concatenate([_colsum(du * x_ref[...]), _colsum(du)] + [zero] * 6, axis=0)

        @pl.when(i >= nx)
        def _():
            s_ref[...] += jnp.concatenate([zero, zero, _colsum(du * ctx_ref[...]), _colsum(du)] + [zero] * 4, axis=0)

    lat = pl.BlockSpec((ROW_TILE, d), lambda i: (jnp.minimum(i, nx - 1), 0))
    return pl.pallas_call(
        body, name="mod1_bwd", grid=(nx + 1,),
        in_specs=[pl.BlockSpec((ROW_TILE, d), lambda i: (i, 0)), lat, lat,
                  pl.BlockSpec((ROW_TILE, d), lambda i: (0, 0)), pl.BlockSpec((8, d), lambda i: (0, 0))],
        out_specs=[lat, pl.BlockSpec((8, d), lambda i: (0, 0))],
        out_shape=[jax.ShapeDtypeStruct((n, d), F32), jax.ShapeDtypeStruct((8, d), F32)],
        compiler_params=_params(("arbitrary",)),
    )(du_all, dxp, x, ctx, mods)


FFN_TM = 1024
FFN_TN = 512


def _gate_up(u2, wg, wu, after):
    n, d = u2.shape
    f = wg.shape[1]

    def body(u_ref, wg_ref, wu_ref, after_ref, g_ref, up_ref, h_ref):
        u = u_ref[...]
        g = lax.dot_general(u, wg_ref[...], NN, preferred_element_type=F32)
        up = lax.dot_general(u, wu_ref[...], NN, preferred_element_type=F32)
        g_ref[...] = g.astype(BF16)
        up_ref[...] = up.astype(BF16)
        h_ref[...] = (g * jax.nn.sigmoid(g) * up).astype(BF16)

    tm = min(FFN_TM, n)
    wspec = pl.BlockSpec((d, FFN_TN), lambda j, i: (0, j))
    ospec = pl.BlockSpec((tm, FFN_TN), lambda j, i: (i, j))
    return pl.pallas_call(
        body, name="gate_up", grid=(f // FFN_TN, n // tm),
        in_specs=[pl.BlockSpec((tm, d), lambda j, i: (i, 0)), wspec, wspec, pl.BlockSpec(memory_space=pl.ANY)],
        out_specs=[ospec, ospec, ospec],
        out_shape=[jax.ShapeDtypeStruct((n, f), BF16)] * 3,
        compiler_params=_params(("parallel", "parallel")),
    )(u2, wg, wu, after)


def _glu_bwd(df, wd, g, u):
    n, d = df.shape
    f = wd.shape[0]

    def body(df_ref, wd_ref, g_ref, u_ref, dg_ref, du_ref):
        dh = lax.dot_general(df_ref[...], wd_ref[...], NT, preferred_element_type=F32)
        gv = g_ref[...].astype(F32)
        sig = jax.nn.sigmoid(gv)
        du_ref[...] = (dh * (gv * sig)).astype(BF16)
        dg_ref[...] = (dh * u_ref[...].astype(F32) * (sig * (1.0 + gv * (1.0 - sig)))).astype(BF16)

    tm = min(FFN_TM, n)
    ospec = pl.BlockSpec((tm, FFN_TN), lambda i, j: (i, j))
    return pl.pallas_call(
        body, name="glu_bwd", grid=(n // tm, f // FFN_TN),
        in_specs=[pl.BlockSpec((tm, d), lambda i, j: (i, 0)),
                  pl.BlockSpec((FFN_TN, d), lambda i, j: (j, 0)), ospec, ospec],
        out_specs=[ospec, ospec],
        out_shape=[jax.ShapeDtypeStruct((n, f), BF16), jax.ShapeDtypeStruct((n, f), BF16)],
        compiler_params=_params(("parallel", "parallel")),
    )(df, wd, g, u)


def _du2(dg, du, wg, wu):
    n, f = dg.shape
    d = wg.shape[0]
    tm, tn, tk = min(1024, n), 1024, 1408
    nk = f // tk

    def body(dg_ref, du_ref, wg_ref, wu_ref, o_ref, acc_ref):
        kk = pl.program_id(2)
        part = (lax.dot_general(dg_ref[...], wg_ref[...], NT, preferred_element_type=F32)
                + lax.dot_general(du_ref[...], wu_ref[...], NT, preferred_element_type=F32))

        @pl.when(kk == 0)
        def _():
            acc_ref[...] = part

        @pl.when(kk != 0)
        def _():
            acc_ref[...] += part

        @pl.when(kk == nk - 1)
        def _():
            o_ref[...] = acc_ref[...]

    aspec = pl.BlockSpec((tm, tk), lambda i, j, kk: (i, kk))
    wspec = pl.BlockSpec((tn, tk), lambda i, j, kk: (j, kk))
    return pl.pallas_call(
        body, name="du2", grid=(n // tm, d // tn, nk),
        in_specs=[aspec, aspec, wspec, wspec],
        out_specs=pl.BlockSpec((tm, tn), lambda i, j, kk: (i, j)),
        out_shape=jax.ShapeDtypeStruct((n, d), F32),
        scratch_shapes=[pltpu.VMEM((tm, tn), F32)],
        compiler_params=_params(("parallel", "parallel", "arbitrary")),
    )(dg, du, wg, wu)


def _rows8(rows, d=D_MODEL):
    rows = [r.reshape(1, d).astype(F32) for r in rows]
    return jnp.concatenate(rows + [jnp.zeros((8 - len(rows), d), F32)], axis=0)


W_GROUPS = (("w_in",), ("w_out", "w_gate", "w_up"), ("w_down",))
G_GROUPS = (("w_down", "w_gate", "w_up"), ("w_out",), ("w_in",))


def _layer_fwd_bwd(x, ctx, target, mod, mod_ctx, weights, prefetch, grads_out,
                   q_g, k_g, sink, ln1_g, ln1_b, ln2_g, ln2_b):
    n, d = x.shape
    c = ctx.shape[0]
    sh1, sc1, g1, sh2, sc2, g2 = [mod[:, k * d:(k + 1) * d] for k in range(6)]
    csh1, csc1 = mod_ctx[:, 0:d], mod_ctx[:, d:2 * d]
    cos, sin_a, sin_b = _rope_tables(n, c)
    sink_b = jnp.broadcast_to(sink.reshape(HEADS_A, 1), (HEADS_A, HEAD_DIM)).astype(F32)

    u_all = _modulate_rows(x, ctx, _rows8([sc1, sh1, csc1, csh1]))
    (w_in,) = weights(0, u_all)
    h_all = _matmul(u_all, w_in, name="qkv_proj", tm=_fit(n + c, 1088), tn=1024, tk=2048, out_dtype=F32)
    qa, ka, va, qb, kb, vb = _qkv_post(h_all, cos, sin_a, sin_b, q_g, k_g)
    heads_b, lse = _attn_b_fwd(qb, kb, vb, n)
    zero = prefetch(1, heads_b)
    heads = _attn_a_fwd(qa, ka, va, sink_b + zero, heads_b, n)
    w_out, w_gate, w_up = weights(1, heads)
    a = _matmul(heads, w_out, name="out_proj", tm=1024, tn=1024, tk=2048, out_dtype=F32)
    xhat1, rstd1, u2 = _res_ln1(x, a, _rows8([g1, ln1_g, ln1_b, sc2, sh2]))
    zero = prefetch(2, u2)
    gg, uu, hh = _gate_up(u2, w_gate, w_up, zero.reshape(1, 1))
    (w_down,) = weights(2, hh)
    f = _matmul(hh, w_down, name="ffn_down", tm=1024, tn=512, tk=FFN, out_dtype=F32)
    dr2, df, s_ln2 = _res_ln2_loss(xhat1, f, target, _rows8([g2, ln1_g, ln1_b, ln2_g, ln2_b]))

    dgg, duu = _glu_bwd(df, w_down, gg, uu)
    dw_down = _matmul(hh, df, name="dw_down", ta=True, tm=512, tn=1024, tk=n, out_dtype=BF16)
    dw_gate = _matmul(u2, dgg, name="dw_gate", ta=True, tm=1024, tn=512, tk=n, out_dtype=BF16)
    dw_up = _matmul(u2, duu, name="dw_up", ta=True, tm=1024, tn=512, tk=n, out_dtype=BF16)
    zero = grads_out(0, [dw_down, dw_gate, dw_up])
    du2 = _du2(dgg, duu, w_gate, w_up)
    dxp, da, s_ln1 = _ln1_bwd(du2, dr2, xhat1, rstd1, a, _rows8([sc2, ln1_g, ln1_b, g1]) + zero)

    dheads = _matmul(da, w_out, name="d_heads", tb=True, tm=1024, tn=1024, tk=2048, out_dtype=BF16)
    dw_out = _matmul(heads, da, name="dw_out", ta=True, tm=1024, tn=1024, tk=n, out_dtype=BF16)
    zero = grads_out(1, [dw_out])
    delta = _delta_rows(dheads, heads)
    dqa, dka, dva, dsink = _attn_a_bwd(qa, ka, va, sink_b + zero, dheads, n)
    dqb, dkb, dvb = _attn_b_bwd(qb, kb, vb, dheads, lse, delta, n)
    dh_all, s_gain = _qkv_bwd_post(h_all, cos, sin_a, sin_b, q_g, k_g, dqa, dka, dva, dqb, dkb, dvb, n)
    dw_in = _matmul(u_all, dh_all, name="dw_in", ta=True, tm=1024, tn=1024, tk=n + c, out_dtype=BF16)
    zero = grads_out(2, [dw_in])
    du_all = _matmul(dh_all, w_in, name="d_u1", tb=True, tm=_fit(n + c, 1088), tn=1024, tk=IN_WIDTH, out_dtype=F32,
                     after=zero.reshape(1, 1))
    grad_x, s_mod1 = _mod1_bwd(du_all, dxp, x, ctx, _rows8([sc1]) + zero)

    dsink_row = jnp.concatenate([dsink[0, 0:GROUP, 0], dsink[1, 0:GROUP, 0]]).reshape(1, HEADS_A)
    misc = jnp.concatenate([s_gain[0:1], s_gain[1:2], dsink_row,
                            jnp.zeros((1, d - 2 * HEAD_DIM - HEADS_A), F32)], axis=1)
    partial = jnp.concatenate([
        s_mod1[1:2], s_mod1[0:1], s_ln1[4:5],
        s_ln1[1:2], s_ln1[0:1], s_ln2[2:3],
        s_mod1[3:4], s_mod1[2:3],
        s_ln1[2:3], s_ln1[3:4], s_ln2[0:1], s_ln2[1:2],
        s_ln2[3:4], misc, jnp.zeros((2, d), F32)], axis=0)
    return grad_x, partial


ANY = pl.BlockSpec(memory_space=pl.ANY)
VMEM_FULL = pl.BlockSpec(memory_space=pltpu.VMEM)
N_CHIP_PEERS = 3


def _me():
    return lax.axis_index("x"), lax.axis_index("y"), lax.axis_index("c")


def _other_chips(x, y):
    return [(1 - x, y), (x, 1 - y), (1 - x, 1 - y)]


def _shard_of(chip):
    return 2 * chip[0] + chip[1]


def _dev_index(x, y, c):
    return 4 * x + 2 * y + c


def _rcopy(src, dst, send_sems, recv_sems, k, dev):
    return pltpu.make_async_remote_copy(src_ref=src, dst_ref=dst, send_sem=send_sems.at[k], recv_sem=recv_sems.at[k],
                                        device_id=dev, device_id_type=MESH)


BIG = (("w_in", (D_MODEL, IN_WIDTH), 1), ("w_out", (MIX_WIDTH, D_MODEL), 0), ("w_gate", (D_MODEL, FFN), 1),
       ("w_up", (D_MODEL, FFN), 1), ("w_down", (FFN, D_MODEL), 0))


def _sub(ref, axis, idx, size):
    start = pl.multiple_of(idx * size, size)
    return ref.at[pl.ds(start, size), :] if axis == 0 else ref.at[:, pl.ds(start, size)]


def _shape_div(shape, axis, parts):
    return tuple(s // parts if a == axis else s for a, s in enumerate(shape))


def _piece(a, ref, shard, half):
    _, full, axis = BIG[a]
    view = _sub(ref, axis, shard, full[axis] // N_SHARD)
    return _sub(view, 1 - axis, half, full[1 - axis] // 2)


HBM = pl.BlockSpec(memory_space=pltpu.HBM)
SEM = pl.BlockSpec(memory_space=pltpu.SEMAPHORE)
EFFECT = pltpu.SideEffectType.DATAFLOW_SIDE_EFFECTING
BIG_INDEX = {name: a for a, (name, _, _) in enumerate(BIG)}


def _in_hbm(arr):
    return pltpu.with_memory_space_constraint(arr, pltpu.HBM)


def _gather_start(tag, arrs, bufs, prev):
    n_arr = len(arrs)

    def body(*refs):
        ins = refs[:n_arr]
        send_sems, recv_sems = refs[n_arr + 1], refs[n_arr + 2]
        token = refs[-1]
        x, y, c = _me()
        s_me = _shard_of((x, y))
        for i, a in enumerate(arrs):
            mine = _piece(a, ins[i], s_me, c)
            for j, chip in enumerate(_other_chips(x, y)):
                _rcopy(mine, mine, send_sems, recv_sems, N_CHIP_PEERS * i + j, (*chip, c)).start()
        token[...] = jnp.zeros_like(token)

    n_sem = N_CHIP_PEERS * n_arr
    outs = pl.pallas_call(
        body, name="gather_start_" + tag,
        in_specs=[HBM] * n_arr + [ANY],
        out_specs=[SEM, SEM] + [HBM] * n_arr + [VMEM_FULL],
        out_shape=[pltpu.SemaphoreType.DMA((n_sem,)), pltpu.SemaphoreType.DMA((n_sem,))]
        + [pltpu.HBM(BIG[a][1], BF16) for a in arrs] + [jax.ShapeDtypeStruct((8, HEAD_DIM), F32)],
        input_output_aliases={i: 2 + i for i in range(n_arr)},
        compiler_params=pltpu.CompilerParams(has_side_effects=EFFECT),
    )(*[_in_hbm(b) for b in bufs], prev)
    return outs[0], outs[1], list(outs[2:2 + n_arr]), outs[-1]


def _gather_wait(tag, arrs, send_sems, recv_sems, bufs, after):
    n_arr = len(arrs)

    def body(*refs):
        ins = refs[:n_arr]
        send_sems_, recv_sems_ = refs[n_arr], refs[n_arr + 1]
        x, y, c = _me()
        s_me = _shard_of((x, y))
        for i, a in enumerate(arrs):
            mine = _piece(a, ins[i], s_me, c)
            for j, chip in enumerate(_other_chips(x, y)):
                landed = _piece(a, ins[i], _shard_of(chip), c)
                cp = _rcopy(mine, landed, send_sems_, recv_sems_, N_CHIP_PEERS * i + j, (*chip, c))
                cp.wait_send()
                cp.wait_recv()

    outs = pl.pallas_call(
        body, name="gather_wait_" + tag,
        in_specs=[HBM] * n_arr + [SEM, SEM, ANY],
        out_specs=[HBM] * n_arr,
        out_shape=[pltpu.HBM(BIG[a][1], BF16) for a in arrs],
        input_output_aliases={i: i for i in range(n_arr)},
        compiler_params=pltpu.CompilerParams(has_side_effects=EFFECT),
    )(*bufs, send_sems, recv_sems, after)
    return list(outs)


def _gather_forward(tag, arrs, bufs):
    n_arr = len(arrs)

    def body(*refs):
        outs = refs[n_arr:2 * n_arr]
        send_sems, recv_sems = refs[2 * n_arr:]
        x, y, c = _me()
        sibling = (x, y, 1 - c)
        chips = _other_chips(x, y)
        copies = []
        for i, a in enumerate(arrs):
            for j, chip in enumerate(chips):
                landed = _piece(a, outs[i], _shard_of(chip), c)
                cp = _rcopy(landed, landed, send_sems, recv_sems, N_CHIP_PEERS * i + j, sibling)
                cp.start()
                copies.append(cp)
        for i, a in enumerate(arrs):
            for j, chip in enumerate(chips):
                other = _piece(a, outs[i], _shard_of(chip), 1 - c)
                _rcopy(other, other, send_sems, recv_sems, N_CHIP_PEERS * i + j, sibling).wait_recv()
        for cp in copies:
            cp.wait_send()

    n_sem = N_CHIP_PEERS * n_arr
    return list(pl.pallas_call(
        body, name="gather_forward_" + tag,
        in_specs=[ANY] * n_arr, out_specs=[ANY] * n_arr,
        out_shape=[jax.ShapeDtypeStruct(BIG[a][1], BF16) for a in arrs],
        input_output_aliases={i: i for i in range(n_arr)},
        scratch_shapes=[pltpu.SemaphoreType.DMA((n_sem,)), pltpu.SemaphoreType.DMA((n_sem,))],
    )(*bufs))


def _forward_start(tag, arrs, bufs):
    n_arr = len(arrs)

    def body(*refs):
        ins = refs[:n_arr]
        send_sems, recv_sems = refs[n_arr], refs[n_arr + 1]
        token = refs[-1]
        x, y, c = _me()
        for i, a in enumerate(arrs):
            for j, chip in enumerate(_other_chips(x, y)):
                landed = _piece(a, ins[i], _shard_of(chip), c)
                _rcopy(landed, landed, send_sems, recv_sems, N_CHIP_PEERS * i + j, (x, y, 1 - c)).start()
        token[...] = jnp.zeros_like(token)

    n_sem = N_CHIP_PEERS * n_arr
    outs = pl.pallas_call(
        body, name="gather_forward_start_" + tag,
        in_specs=[HBM] * n_arr,
        out_specs=[SEM, SEM] + [HBM] * n_arr + [VMEM_FULL],
        out_shape=[pltpu.SemaphoreType.DMA((n_sem,)), pltpu.SemaphoreType.DMA((n_sem,))]
        + [pltpu.HBM(BIG[a][1], BF16) for a in arrs] + [jax.ShapeDtypeStruct((8, HEAD_DIM), F32)],
        input_output_aliases={i: 2 + i for i in range(n_arr)},
        compiler_params=pltpu.CompilerParams(has_side_effects=EFFECT),
    )(*bufs)
    return outs[0], outs[1], list(outs[2:2 + n_arr]), outs[-1]


def _forward_wait(tag, arrs, send_sems, recv_sems, bufs, after):
    n_arr = len(arrs)

    def body(*refs):
        ins = refs[:n_arr]
        send_sems_, recv_sems_ = refs[n_arr], refs[n_arr + 1]
        x, y, c = _me()
        for i, a in enumerate(arrs):
            for j, chip in enumerate(_other_chips(x, y)):
                mine = _piece(a, ins[i], _shard_of(chip), c)
                other = _piece(a, ins[i], _shard_of(chip), 1 - c)
                cp = _rcopy(mine, other, send_sems_, recv_sems_, N_CHIP_PEERS * i + j, (x, y, 1 - c))
                cp.wait_send()
                cp.wait_recv()

    outs = pl.pallas_call(
        body, name="gather_forward_wait_" + tag,
        in_specs=[HBM] * n_arr + [SEM, SEM, ANY],
        out_specs=[HBM] * n_arr,
        out_shape=[pltpu.HBM(BIG[a][1], BF16) for a in arrs],
        input_output_aliases={i: i for i in range(n_arr)},
        compiler_params=pltpu.CompilerParams(has_side_effects=EFFECT),
    )(*bufs, send_sems, recv_sems, after)
    return list(outs)


def _peers(x, y, c):
    return [(x ^ (mask >> 2), y ^ ((mask >> 1) & 1), c ^ (mask & 1)) for mask in range(1, N_DEV)]


def _received_shape(a):
    _, full, axis = BIG[a]
    return (N_DEV - 1,) + _shape_div(_shape_div(full, 1 - axis, 2), axis, N_SHARD)


def _pieces_start(tag, arrs, dws):
    n_arr = len(arrs)

    def body(*refs):
        srcs, lands = refs[:n_arr], refs[n_arr:2 * n_arr]
        send_sems, recv_sems = refs[2 * n_arr], refs[2 * n_arr + 1]
        token = refs[-1]
        x, y, c = _me()
        for i, a in enumerate(arrs):
            for k, peer in enumerate(_peers(x, y, c)):
                src = _piece(a, srcs[i], _shard_of(peer[:2]), peer[2])
                _rcopy(src, lands[i].at[k], send_sems, recv_sems, (N_DEV - 1) * i + k, peer).start()
        token[...] = jnp.zeros_like(token)

    n_sem = (N_DEV - 1) * n_arr
    lands = [_in_hbm(lax.empty(_received_shape(a), BF16)) for a in arrs]
    outs = pl.pallas_call(
        body, name="grad_pieces_start_" + tag,
        in_specs=[HBM] * (2 * n_arr),
        out_specs=[SEM, SEM] + [HBM] * (2 * n_arr) + [VMEM_FULL],
        out_shape=[pltpu.SemaphoreType.DMA((n_sem,)), pltpu.SemaphoreType.DMA((n_sem,))]
        + [pltpu.HBM(BIG[a][1], BF16) for a in arrs] + [pltpu.HBM(_received_shape(a), BF16) for a in arrs]
        + [jax.ShapeDtypeStruct((8, HEAD_DIM), F32)],
        input_output_aliases={i: 2 + i for i in range(2 * n_arr)},
        compiler_params=pltpu.CompilerParams(has_side_effects=EFFECT),
    )(*[_in_hbm(dw) for dw in dws], *lands)
    return outs[0], outs[1], list(outs[2:2 + n_arr]), list(outs[2 + n_arr:2 + 2 * n_arr]), outs[-1]


def _pieces_wait(tag, arrs, send_sems, recv_sems, dws, lands, after):
    n_arr = len(arrs)

    def body(*refs):
        srcs, lands_ = refs[:n_arr], refs[n_arr:2 * n_arr]
        send_sems_, recv_sems_ = refs[2 * n_arr], refs[2 * n_arr + 1]
        x, y, c = _me()
        for i, a in enumerate(arrs):
            for k, peer in enumerate(_peers(x, y, c)):
                src = _piece(a, srcs[i], _shard_of(peer[:2]), peer[2])
                cp = _rcopy(src, lands_[i].at[k], send_sems_, recv_sems_, (N_DEV - 1) * i + k, peer)
                cp.wait_send()
                cp.wait_recv()

    outs = pl.pallas_call(
        body, name="grad_pieces_wait_" + tag,
        in_specs=[HBM] * (2 * n_arr) + [SEM, SEM, ANY],
        out_specs=[HBM] * (2 * n_arr),
        out_shape=[pltpu.HBM(BIG[a][1], BF16) for a in arrs] + [pltpu.HBM(_received_shape(a), BF16) for a in arrs],
        input_output_aliases={i: i for i in range(2 * n_arr)},
        compiler_params=pltpu.CompilerParams(has_side_effects=EFFECT),
    )(*dws, *lands, send_sems, recv_sems, after)
    return list(outs[:n_arr]), list(outs[n_arr:])


def _join_start(tag, g_halves):
    n_arr = len(g_halves)

    def body(*refs):
        srcs, lands = refs[:n_arr], refs[n_arr:2 * n_arr]
        send_sems, recv_sems = refs[2 * n_arr], refs[2 * n_arr + 1]
        token = refs[-1]
        x, y, c = _me()
        for i in range(n_arr):
            _rcopy(srcs[i], lands[i], send_sems, recv_sems, i, (x, y, 1 - c)).start()
        token[...] = jnp.zeros_like(token)

    shapes = [pltpu.HBM(g.shape, F32) for g in g_halves]
    outs = pl.pallas_call(
        body, name="grad_join_start_" + tag,
        in_specs=[HBM] * (2 * n_arr),
        out_specs=[SEM, SEM] + [HBM] * (2 * n_arr) + [VMEM_FULL],
        out_shape=[pltpu.SemaphoreType.DMA((n_arr,)), pltpu.SemaphoreType.DMA((n_arr,))] + shapes + shapes
        + [jax.ShapeDtypeStruct((8, HEAD_DIM), F32)],
        input_output_aliases={i: 2 + i for i in range(2 * n_arr)},
        compiler_params=pltpu.CompilerParams(has_side_effects=EFFECT),
    )(*[_in_hbm(g) for g in g_halves], *[_in_hbm(lax.empty(g.shape, F32)) for g in g_halves])
    return outs[0], outs[1], list(outs[2:2 + n_arr]), list(outs[2 + n_arr:2 + 2 * n_arr]), outs[-1]


def _join_wait(tag, send_sems, recv_sems, g_halves, lands, after):
    n_arr = len(g_halves)

    def body(*refs):
        srcs, lands_ = refs[:n_arr], refs[n_arr:2 * n_arr]
        send_sems_, recv_sems_ = refs[2 * n_arr], refs[2 * n_arr + 1]
        x, y, c = _me()
        for i in range(n_arr):
            cp = _rcopy(srcs[i], lands_[i], send_sems_, recv_sems_, i, (x, y, 1 - c))
            cp.wait_send()
            cp.wait_recv()

    shapes = [pltpu.HBM(g.shape, F32) for g in g_halves]
    outs = pl.pallas_call(
        body, name="grad_join_wait_" + tag,
        in_specs=[HBM] * (2 * n_arr) + [SEM, SEM, ANY],
        out_specs=[HBM] * (2 * n_arr),
        out_shape=shapes + shapes,
        input_output_aliases={i: i for i in range(2 * n_arr)},
        compiler_params=pltpu.CompilerParams(has_side_effects=EFFECT),
    )(*g_halves, *lands, send_sems, recv_sems, after)
    return list(outs[:n_arr]), list(outs[n_arr:])


def _piece_sum(a, dw, shard, core, received):
    name, full, axis = BIG[a]
    rows, cols = _received_shape(a)[1:]
    tr = _fit(rows, ROW_TILE)
    nbr = rows // tr

    def body(w_ref, dw_ref, rec_ref, o_ref):
        acc = dw_ref[...].astype(F32)
        for k in range(N_DEV - 1):
            acc = acc + rec_ref[k].astype(F32)
        o_ref[...] = acc

    if axis == 0:
        own = pl.BlockSpec((tr, cols), lambda i, w: (w[0] * nbr + i, w[1]))
    else:
        own = pl.BlockSpec((tr, cols), lambda i, w: (w[1] * nbr + i, w[0]))
    return pl.pallas_call(
        body, name="grad_sum_pieces_" + name,
        grid_spec=pltpu.PrefetchScalarGridSpec(
            num_scalar_prefetch=1, grid=(nbr,),
            in_specs=[own, pl.BlockSpec((N_DEV - 1, tr, cols), lambda i, w: (0, i, 0))],
            out_specs=pl.BlockSpec((tr, cols), lambda i, w: (i, 0))),
        out_shape=jax.ShapeDtypeStruct((rows, cols), F32),
        compiler_params=_params(("parallel",)),
    )(jnp.stack([shard, core]).astype(jnp.int32), dw, received)


def _scatter_begin(tag, arrs, dws):
    send_sems, recv_sems, dws, lands, token = _pieces_start(tag, arrs, dws)
    return (send_sems, recv_sems, dws, lands), token[0, 0]


def _scatter_reduce(tag, arrs, state, after):
    x, y, c = _me()
    send_sems, recv_sems, dws, lands = state
    dws, lands = _pieces_wait(tag, arrs, send_sems, recv_sems, dws, lands, after)
    g_own = [_piece_sum(a, dw, _shard_of((x, y)), c, r) for a, dw, r in zip(arrs, dws, lands)]
    send_sems, recv_sems, g_own, lands, token = _join_start(tag, g_own)
    return (send_sems, recv_sems, g_own, lands), token


def _scatter_end(tag, state, after):
    return _join_wait(tag, *state, after)


def _rows_start(tag, block):
    r, d = block.shape

    def body(src, land, send_sems, recv_sems, src_thru, land_thru, token):
        x, y, c = _me()
        for k, peer in enumerate(_peers(x, y, c)):
            _rcopy(src, land.at[_dev_index(x, y, c)], send_sems, recv_sems, k, peer).start()
        token[...] = jnp.zeros_like(token)

    outs = pl.pallas_call(
        body, name="rows_start_" + tag,
        in_specs=[HBM, HBM],
        out_specs=[SEM, SEM, HBM, HBM, VMEM_FULL],
        out_shape=[pltpu.SemaphoreType.DMA((N_DEV - 1,)), pltpu.SemaphoreType.DMA((N_DEV - 1,)),
                   pltpu.HBM((r, d), F32), pltpu.HBM((N_DEV, r, d), F32), jax.ShapeDtypeStruct((8, HEAD_DIM), F32)],
        input_output_aliases={0: 2, 1: 3},
        compiler_params=pltpu.CompilerParams(has_side_effects=EFFECT),
    )(_in_hbm(block), _in_hbm(lax.empty((N_DEV, r, d), F32)))
    return outs[0], outs[1], outs[2], outs[3], outs[4]


def _rows_wait(tag, send_sems, recv_sems, block, land, after):
    r, d = block.shape

    def body(src, land_, send_sems_, recv_sems_, after_ref, src_thru, land_thru):
        x, y, c = _me()
        for k, peer in enumerate(_peers(x, y, c)):
            cp = _rcopy(src, land_.at[_dev_index(*peer)], send_sems_, recv_sems_, k, peer)
            cp.wait_send()
            cp.wait_recv()

    outs = pl.pallas_call(
        body, name="rows_wait_" + tag,
        in_specs=[HBM, HBM, SEM, SEM, ANY],
        out_specs=[HBM, HBM],
        out_shape=[pltpu.HBM((r, d), F32), pltpu.HBM((N_DEV, r, d), F32)],
        input_output_aliases={0: 0, 1: 1},
        compiler_params=pltpu.CompilerParams(has_side_effects=EFFECT),
    )(block, land, send_sems, recv_sems, after)
    me = _dev_index(*_me())
    return lax.dynamic_update_slice(outs[1], outs[0][None], (me, 0, 0))


ADA_ROWS = 80
ADA_W = 6 * D_MODEL // N_SHARD


def _ada_forward(c_block, cctx_block, w_ada, b_shard):
    d = c_block.shape[1]

    def body(c_ref, cc_ref, w_ref, b_ref, act_ref, mods_ref, raw, mloc, send_sems, recv_sems):
        x, y, c = _me()
        me = _dev_index(x, y, c)
        s_me = _shard_of((x, y))
        raw[72:ADA_ROWS, :] = jnp.zeros((ADA_ROWS - 72, d), F32)
        raw[pl.ds(pl.multiple_of(me * 8, 8), 8), :] = c_ref[...]
        raw[64:72, :] = cc_ref[...]
        sends = []
        for mask in range(1, N_DEV):
            peer = (x ^ (mask >> 2), y ^ ((mask >> 1) & 1), c ^ (mask & 1))
            cp = _rcopy(c_ref, raw.at[pl.ds(pl.multiple_of(me * 8, 8), 8), :], send_sems, recv_sems, mask - 1, peer)
            cp.start()
            sends.append(cp)
        for mask in range(1, N_DEV):
            peer = (x ^ (mask >> 2), y ^ ((mask >> 1) & 1), c ^ (mask & 1))
            landed = raw.at[pl.ds(pl.multiple_of(_dev_index(*peer) * 8, 8), 8), :]
            _rcopy(landed, landed, send_sems, recv_sems, mask - 1, peer).wait_recv()
        v = raw[...]
        act = v * jax.nn.sigmoid(v)
        act_ref[...] = act
        mloc[...] = lax.dot_general(act.astype(BF16), w_ref[...].astype(BF16), NN,
                                    preferred_element_type=F32) + b_ref[...]
        mods_ref[s_me, 0:8, :] = mloc[pl.ds(pl.multiple_of(me * 8, 8), 8), :]
        mods_ref[s_me, 8:16, :] = mloc[64:72, :]
        base = N_DEV - 1
        for j, chip in enumerate(_other_chips(x, y)):
            peer = (*chip, c)
            rows = mloc.at[pl.ds(pl.multiple_of(_dev_index(*peer) * 8, 8), 8), :]
            cp = _rcopy(rows, mods_ref.at[s_me, 0:8, :], send_sems, recv_sems, base + 2 * j, peer)
            cp.start()
            sends.append(cp)
            cp = _rcopy(mloc.at[64:72, :], mods_ref.at[s_me, 8:16, :], send_sems, recv_sems, base + 2 * j + 1, peer)
            cp.start()
            sends.append(cp)
        for j, chip in enumerate(_other_chips(x, y)):
            for part in range(2):
                landed = mods_ref.at[_shard_of(chip), 8 * part:8 * part + 8, :]
                _rcopy(landed, landed, send_sems, recv_sems, base + 2 * j + part, (*chip, c)).wait_recv()
        for cp in sends:
            cp.wait_send()

    n_sem = N_DEV - 1 + 2 * N_CHIP_PEERS
    return pl.pallas_call(
        body, name="ada_forward",
        in_specs=[VMEM_FULL] * 4, out_specs=[VMEM_FULL, VMEM_FULL],
        out_shape=[jax.ShapeDtypeStruct((ADA_ROWS, d), F32), jax.ShapeDtypeStruct((N_SHARD, 16, ADA_W), F32)],
        scratch_shapes=[pltpu.VMEM((ADA_ROWS, d), F32), pltpu.VMEM((ADA_ROWS, ADA_W), F32),
                        pltpu.SemaphoreType.DMA((n_sem,)), pltpu.SemaphoreType.DMA((n_sem,))],
        compiler_params=pltpu.CompilerParams(vmem_limit_bytes=VMEM_LIMIT),
    )(c_block, cctx_block, w_ada, b_shard)


def _small_reduce(gathered):
    d = gathered.shape[2]

    def body(g_ref, o_ref):
        tot = g_ref[0]
        for i in range(1, N_DEV):
            tot = tot + g_ref[i]
        o_ref[...] = tot
        o_ref[0:2, :] = tot[0:2] + tot[6:8]
        o_ref[12:13, :] = jnp.broadcast_to(jnp.sum(tot[12:13], axis=1, keepdims=True), (1, d))

    return pl.pallas_call(body, name="small_reduce", in_specs=[VMEM_FULL], out_specs=VMEM_FULL,
                          out_shape=jax.ShapeDtypeStruct((16, d), F32))(gathered)


def _cctx_grad(gathered, c_ctx):
    d = gathered.shape[2]

    def body(g_ref, c_ref, o_ref):
        tot = g_ref[0, 0:1, :]
        for chip in range(1, N_SHARD):
            tot = tot + g_ref[2 * chip, 0:1, :]
        v = c_ref[...]
        sig = jax.nn.sigmoid(v)
        o_ref[...] = tot * (sig * (1.0 + v * (1.0 - sig)))

    return pl.pallas_call(body, name="cctx_grad", in_specs=[VMEM_FULL, VMEM_FULL], out_specs=VMEM_FULL,
                          out_shape=jax.ShapeDtypeStruct((1, d), F32))(gathered, c_ctx.reshape(1, d))


def _cast_into_full(w, shard, full, axis, name):
    r, cdim = w.shape
    tr = _fit(r, ROW_TILE)
    nbr = r // tr

    def body(s_ref, w_ref, o_ref):
        o_ref[...] = w_ref[...].astype(BF16)

    if axis == 0:
        out_spec = pl.BlockSpec((tr, cdim), lambda i, s: (s[0] * nbr + i, 0))
    else:
        out_spec = pl.BlockSpec((tr, cdim), lambda i, s: (i, s[0]))
    return pl.pallas_call(
        body, name=name,
        grid_spec=pltpu.PrefetchScalarGridSpec(
            num_scalar_prefetch=1, grid=(nbr,), in_specs=[pl.BlockSpec((tr, cdim), lambda i, s: (i, 0))],
            out_specs=out_spec),
        out_shape=jax.ShapeDtypeStruct(full, BF16), compiler_params=_params(("parallel",)),
    )(shard.reshape(1).astype(jnp.int32), w)


def _adamw_halves(w, g_own, g_other, m, v, core, axis, name):
    r, cdim = w.shape
    hr, hc = (r // 2, cdim) if axis == 1 else (r, cdim // 2)
    assert g_own.shape == (hr, hc) and g_other.shape == (hr, hc)
    tr = _fit(hr, 256)
    nb = hr // tr
    c1 = 1.0 - ADAM_B1 ** ADAM_STEP
    c2 = 1.0 - ADAM_B2 ** ADAM_STEP

    def body(c_ref, w_ref, go_ref, gt_ref, m_ref, v_ref, g_ref, d_ref, nm_ref, nv_ref):
        gv = jnp.where(pl.program_id(0) == c_ref[0], go_ref[...], gt_ref[...])
        nm = ADAM_B1 * m_ref[...] + (1.0 - ADAM_B1) * gv
        nv = ADAM_B2 * v_ref[...] + (1.0 - ADAM_B2) * (gv * gv)
        g_ref[...] = gv
        nm_ref[...] = nm
        nv_ref[...] = nv
        d_ref[...] = -ADAM_LR * ((nm / c1) / (jnp.sqrt(nv / c2) + ADAM_EPS) + ADAM_WD * w_ref[...])

    if axis == 1:
        big = pl.BlockSpec((tr, hc), lambda p, i, c: (p * nb + i, 0))
    else:
        big = pl.BlockSpec((tr, hc), lambda p, i, c: (i, p))
    own = pl.BlockSpec((tr, hc), lambda p, i, c: (jnp.where(p == c[0], i, 0), 0))
    other = pl.BlockSpec((tr, hc), lambda p, i, c: (jnp.where(p == c[0], 0, i), 0))
    sh = jax.ShapeDtypeStruct((r, cdim), F32)
    return pl.pallas_call(
        body, name=name,
        grid_spec=pltpu.PrefetchScalarGridSpec(
            num_scalar_prefetch=1, grid=(2, nb), in_specs=[big, own, other, big, big], out_specs=[big] * 4),
        out_shape=[sh] * 4, compiler_params=_params(("parallel", "parallel")),
    )(core.reshape(1).astype(jnp.int32), w, g_own, g_other, m, v)


def _adamw(w, g, m, v, name):
    r, cdim = w.shape
    tr = _fit(r, 128) if r % (ROW_TILE // 4) == 0 else r
    c1 = 1.0 - ADAM_B1 ** ADAM_STEP
    c2 = 1.0 - ADAM_B2 ** ADAM_STEP

    def body(w_ref, g_ref, m_ref, v_ref, d_ref, nm_ref, nv_ref):
        gv = g_ref[...]
        nm = ADAM_B1 * m_ref[...] + (1.0 - ADAM_B1) * gv
        nv = ADAM_B2 * v_ref[...] + (1.0 - ADAM_B2) * (gv * gv)
        nm_ref[...] = nm
        nv_ref[...] = nv
        d_ref[...] = -ADAM_LR * ((nm / c1) / (jnp.sqrt(nv / c2) + ADAM_EPS) + ADAM_WD * w_ref[...])

    spec = pl.BlockSpec((tr, cdim), lambda i: (i, 0))
    sh = jax.ShapeDtypeStruct((r, cdim), F32)
    return pl.pallas_call(body, name=name, grid=(r // tr,), in_specs=[spec] * 4, out_specs=[spec] * 3,
                          out_shape=[sh, sh, sh], compiler_params=_params(("parallel",)))(w, g, m, v)


def _adamw_small(ws, gs, ms, vs):
    k = len(ws)
    c1 = 1.0 - ADAM_B1 ** ADAM_STEP
    c2 = 1.0 - ADAM_B2 ** ADAM_STEP

    def body(*refs):
        w_refs, g_refs, m_refs, v_refs = refs[0:k], refs[k:2 * k], refs[2 * k:3 * k], refs[3 * k:4 * k]
        d_refs, nm_refs, nv_refs = refs[4 * k:5 * k], refs[5 * k:6 * k], refs[6 * k:7 * k]
        for i in range(k):
            gv = g_refs[i][...]
            nm = ADAM_B1 * m_refs[i][...] + (1.0 - ADAM_B1) * gv
            nv = ADAM_B2 * v_refs[i][...] + (1.0 - ADAM_B2) * (gv * gv)
            nm_refs[i][...] = nm
            nv_refs[i][...] = nv
            d_refs[i][...] = -ADAM_LR * ((nm / c1) / (jnp.sqrt(nv / c2) + ADAM_EPS) + ADAM_WD * w_refs[i][...])

    shapes = [jax.ShapeDtypeStruct(w.shape, F32) for w in ws]
    outs = pl.pallas_call(body, name="adamw_small", in_specs=[VMEM_FULL] * (4 * k), out_specs=[VMEM_FULL] * (3 * k),
                          out_shape=shapes * 3)(*ws, *gs, *ms, *vs)
    return outs[0:k], outs[k:2 * k], outs[2 * k:3 * k]


SMALL = (("c_ctx", D_MODEL), ("b_ada", 6 * D_MODEL), ("q_norm_g", HEAD_DIM), ("k_norm_g", HEAD_DIM),
         ("sink_logit", HEADS_A), ("ln1_g", D_MODEL), ("ln1_b", D_MODEL), ("ln2_g", D_MODEL), ("ln2_b", D_MODEL))
WEIGHT_ORDER = ("c_ctx", "w_ada", "b_ada", "w_in", "q_norm_g", "k_norm_g", "sink_logit", "w_out", "ln1_g", "ln1_b",
                "w_gate", "w_up", "w_down", "ln2_g", "ln2_b")


def kernel(x, c, ctx, c_ctx, w_ada, b_ada, w_in, q_norm_g, k_norm_g, sink_logit, w_out, ln1_g, ln1_b, w_gate, w_up, w_down, ln2_g, ln2_b, loss_target, m_c_ctx, m_w_ada, m_b_ada, m_w_in, m_q_norm_g, m_k_norm_g, m_sink_logit, m_w_out, m_ln1_g, m_ln1_b, m_w_gate, m_w_up, m_w_down, m_ln2_g, m_ln2_b, v_c_ctx, v_w_ada, v_b_ada, v_w_in, v_q_norm_g, v_k_norm_g, v_sink_logit, v_w_out, v_ln1_g, v_ln1_b, v_w_gate, v_w_up, v_w_down, v_ln2_g, v_ln2_b):
    d = D_MODEL
    w = dict(c_ctx=c_ctx, w_ada=w_ada[0], b_ada=b_ada, w_in=w_in[0], q_norm_g=q_norm_g, k_norm_g=k_norm_g,
             sink_logit=sink_logit, w_out=w_out[0], ln1_g=ln1_g, ln1_b=ln1_b, w_gate=w_gate[0], w_up=w_up[0],
             w_down=w_down[0], ln2_g=ln2_g, ln2_b=ln2_b)
    m = dict(c_ctx=m_c_ctx, w_ada=m_w_ada[0], b_ada=m_b_ada, w_in=m_w_in[0], q_norm_g=m_q_norm_g, k_norm_g=m_k_norm_g,
             sink_logit=m_sink_logit, w_out=m_w_out[0], ln1_g=m_ln1_g, ln1_b=m_ln1_b, w_gate=m_w_gate[0],
             w_up=m_w_up[0], w_down=m_w_down[0], ln2_g=m_ln2_g, ln2_b=m_ln2_b)
    v = dict(c_ctx=v_c_ctx, w_ada=v_w_ada[0], b_ada=v_b_ada, w_in=v_w_in[0], q_norm_g=v_q_norm_g, k_norm_g=v_k_norm_g,
             sink_logit=v_sink_logit, w_out=v_w_out[0], ln1_g=v_ln1_g, ln1_b=v_ln1_b, w_gate=v_w_gate[0],
             w_up=v_w_up[0], w_down=v_w_down[0], ln2_g=v_ln2_g, ln2_b=v_ln2_b)
    mx, my, mc = _me()
    s_me = _shard_of((mx, my))
    me = _dev_index(mx, my, mc)
    pad8 = lambda row: jnp.concatenate([row.reshape(1, -1), jnp.zeros((7, row.size), F32)], axis=0)

    b_shard = lax.dynamic_slice(b_ada, (0, s_me * ADA_W), (1, ADA_W))
    act, mods4 = _ada_forward(pad8(c), pad8(c_ctx), w["w_ada"], b_shard)

    gathers = []
    prev, shard = mods4, s_me
    for k, names in enumerate(W_GROUPS):
        arrs = tuple(BIG_INDEX[name] for name in names)
        bufs = [_cast_into_full(w[name], shard, BIG[a][1], BIG[a][2], "cast_" + name) for name, a in zip(names, arrs)]
        send_sems, recv_sems, thru, prev = _gather_start("g%d" % k, arrs, bufs, prev)
        shard = s_me + prev[0, 0].astype(jnp.int32)
        gathers.append((arrs, send_sems, recv_sems, thru))

    forwards = {}

    def prefetch(k, after):
        arrs, send_sems, recv_sems, thru = gathers[k]
        landed = _gather_wait("g%d" % k, arrs, send_sems, recv_sems, thru, after)
        fwd_send, fwd_recv, landed, token = _forward_start("g%d" % k, arrs, landed)
        forwards[k] = (fwd_send, fwd_recv, landed)
        return token[0, 0]

    def weights(k, after):
        arrs, send_sems, recv_sems, thru = gathers[k]
        if k in forwards:
            return _forward_wait("g%d" % k, arrs, *forwards[k], after)
        landed = _gather_wait("g%d" % k, arrs, send_sems, recv_sems, thru, after)
        return _gather_forward("g%d" % k, arrs, landed)

    mod = jnp.transpose(mods4[:, 0:1, :], (1, 0, 2)).reshape(1, 6 * d) + prev[0, 0]
    mod_ctx = jnp.transpose(mods4[:, 8:9, :], (1, 0, 2)).reshape(1, 6 * d)

    scatters = {}

    def grads_out(k, dws):
        arrs = tuple(BIG_INDEX[name] for name in G_GROUPS[k])
        scatters[k], zero = _scatter_begin("g%d" % k, arrs, dws)
        return zero

    grad_x, partial = _layer_fwd_bwd(x[0], ctx[0], loss_target[0], mod, mod_ctx, weights, prefetch, grads_out,
                                     q_norm_g, k_norm_g, sink_logit, ln1_g, ln1_b, ln2_g, ln2_b)
    grads, delta, new_m, new_v = {}, {}, {}, {}

    p_send, p_recv, partial, p_land, after = _rows_start("partials", partial)
    joins = []
    for k, names in enumerate(G_GROUPS):
        arrs = tuple(BIG_INDEX[name] for name in names)
        state, after = _scatter_reduce("g%d" % k, arrs, scatters[k], after)
        joins.append(state)
    for k, names in enumerate(G_GROUPS):
        g_own, g_other = _scatter_end("g%d" % k, joins[k], after)
        for name, own, other in zip(names, g_own, g_other):
            grads[name], delta[name], new_m[name], new_v[name] = _adamw_halves(
                w[name], own, other, m[name], v[name], mc, BIG[BIG_INDEX[name]][2], "adamw_" + name)
            after = new_v[name]

    gathered = _rows_wait("partials", p_send, p_recv, partial, p_land, after)
    tot = _small_reduce(gathered)
    grads["b_ada"] = tot[0:6].reshape(1, 6 * d)
    grads["ln1_g"], grads["ln1_b"], grads["ln2_g"], grads["ln2_b"] = tot[8:9], tot[9:10], tot[10:11], tot[11:12]
    grads["q_norm_g"] = tot[13:14, 0:HEAD_DIM]
    grads["k_norm_g"] = tot[13:14, HEAD_DIM:2 * HEAD_DIM]
    grads["sink_logit"] = tot[13:14, 2 * HEAD_DIM:2 * HEAD_DIM + HEADS_A]
    loss = tot[12, 0]

    dm_all = gathered[:, 0:6, :].reshape(N_DEV, 6 * d)
    dmc_tot = jnp.concatenate([tot[6:8].reshape(1, 2 * d), jnp.zeros((1, 4 * d), F32)], axis=1)
    dm_rows = jnp.concatenate([pad8(dm_all[i]) for i in range(N_DEV)] + [pad8(dmc_tot), jnp.zeros((8, 6 * d), F32)], axis=0)
    dm_shard = lax.dynamic_slice(dm_rows, (0, s_me * ADA_W), (ADA_ROWS, ADA_W))
    dmc_shard = lax.dynamic_slice(pad8(dmc_tot), (0, s_me * ADA_W), (8, ADA_W))
    cc_part = _matmul(dmc_shard, w["w_ada"], name="d_cctx", tb=True, tm=8, tn=1024, tk=1536, out_dtype=F32)
    c_send, c_recv, cc_part, c_land, c_token = _rows_start("cctx", cc_part)
    grads["w_ada"] = _matmul(act, dm_shard, name="dw_ada", ta=True, tm=1024, tn=1024, tk=ADA_ROWS, out_dtype=F32,
                             after=c_token)
    delta["w_ada"], new_m["w_ada"], new_v["w_ada"] = _adamw(w["w_ada"], grads["w_ada"], m["w_ada"], v["w_ada"],
                                                            "adamw_w_ada")
    gathered_cc = _rows_wait("cctx", c_send, c_recv, cc_part, c_land, new_v["w_ada"])
    grads["c_ctx"] = _cctx_grad(gathered_cc, c_ctx).reshape(d)
    rows = lambda t: [t[name].reshape(1, size) for name, size in SMALL]
    small = _adamw_small(rows(w), rows(grads), rows(m), rows(v))
    for k, (name, size) in enumerate(SMALL):
        delta[name], new_m[name], new_v[name] = [t[k].reshape(w[name].shape) for t in small]
        grads[name] = grads[name].reshape(w[name].shape)

    lead = lambda name, t: t[None] if name in ("w_ada", "w_in", "w_out", "w_gate", "w_up", "w_down") else t
    outs = [loss, grad_x[None]]
    for group in (grads, delta, new_m, new_v):
        outs += [lead(name, group[name]) for name in WEIGHT_ORDER]
    return tuple(outs)
```

```python
import functools
import math

import jax
import jax.numpy as jnp
from jax import lax
from jax.experimental import pallas as pl
from jax.experimental.pallas import tpu as pltpu

F32 = jnp.float32
BF16 = jnp.bfloat16
MESH = pl.DeviceIdType.MESH

D_MODEL = 2048
HEAD_DIM = 128
HEADS_A = 8
HEADS_B = 8
KV_A = 2
KV_B = 2
GROUP = 4
GRID_W = 64
WINDOW = 128
BLOCK = 128
FFN = 5632
IN_WIDTH = 3072
MIX_WIDTH = 2048
ROPE_THETA = 10000.0
EPS = 1e-6
ATTN_SCALE = HEAD_DIM ** -0.5
DN_ALPHA = 2.0 ** 0.25
N_SHARD = 4
N_DEV = 8

ADAM_LR = 0.001
ADAM_B1 = 0.9
ADAM_B2 = 0.999
ADAM_EPS = 1e-08
ADAM_WD = 0.01
ADAM_STEP = 10

QA0, KA0, VA0, QB0, KB0, VB0 = 0, 1024, 1280, 1536, 2560, 2816

VMEM_LIMIT = 56 * 1024 * 1024
ROW_TILE = 256
NN = (((1,), (0,)), ((), ()))
NT = (((1,), (1,)), ((), ()))
TN = (((0,), (0,)), ((), ()))


def _fit(total, pref):
    step = ROW_TILE // 4
    best = step
    for cand in range(step, pref + 1, step):
        if total % cand == 0:
            best = cand
    return best


def _params(sem=None):
    return pltpu.CompilerParams(dimension_semantics=sem, vmem_limit_bytes=VMEM_LIMIT)


def _matmul(a, b, *, name, ta=False, tb=False, tm, tn, tk, out_dtype, after=None):
    m = a.shape[1] if ta else a.shape[0]
    k = a.shape[0] if ta else a.shape[1]
    n = b.shape[0] if tb else b.shape[1]
    assert (b.shape[1] if tb else b.shape[0]) == k
    tm, tn, tk = min(tm, m), min(tn, n), min(tk, k)
    assert m % tm == 0 and n % tn == 0 and k % tk == 0, (name, m, n, k, tm, tn, tk)
    nk = k // tk
    dn = (((0 if ta else 1,), (1 if tb else 0,)), ((), ()))

    def product(a_ref, b_ref):
        return lax.dot_general(a_ref[...].astype(BF16), b_ref[...].astype(BF16), dn, preferred_element_type=F32)

    def body_whole_k(a_ref, b_ref, *rest):
        o_ref = rest[-1]
        o_ref[...] = product(a_ref, b_ref).astype(o_ref.dtype)

    def body(a_ref, b_ref, *rest):
        o_ref, acc_ref = rest[-2:]
        kk = pl.program_id(2)
        part = product(a_ref, b_ref)

        @pl.when(kk == 0)
        def _():
            acc_ref[...] = part

        @pl.when(kk != 0)
        def _():
            acc_ref[...] += part

        @pl.when(kk == nk - 1)
        def _():
            o_ref[...] = acc_ref[...].astype(o_ref.dtype)

    a_spec = (pl.BlockSpec((tk, tm), lambda i, j, kk: (kk, i)) if ta
              else pl.BlockSpec((tm, tk), lambda i, j, kk: (i, kk)))
    b_spec = (pl.BlockSpec((tn, tk), lambda i, j, kk: (j, kk)) if tb
              else pl.BlockSpec((tk, tn), lambda i, j, kk: (kk, j)))
    return pl.pallas_call(
        body_whole_k if nk == 1 else body, name=name, grid=(m // tm, n // tn, nk),
        in_specs=[a_spec, b_spec] + ([] if after is None else [pl.BlockSpec(memory_space=pl.ANY)]),
        out_specs=pl.BlockSpec((tm, tn), lambda i, j, kk: (i, j)),
        out_shape=jax.ShapeDtypeStruct((m, n), out_dtype),
        scratch_shapes=[] if nk == 1 else [pltpu.VMEM((tm, tn), F32)],
        compiler_params=_params(("parallel", "parallel", "arbitrary")),
    )(a, b, *([] if after is None else [after]))


def _modulate_rows(x, ctx, mods):
    n, d = x.shape
    c = ctx.shape[0]
    nx = n // ROW_TILE
    assert c == ROW_TILE

    def body(x_ref, ctx_ref, mods_ref, o_ref):
        i = pl.program_id(0)

        @pl.when(i < nx)
        def _():
            o_ref[...] = (x_ref[...] * (1.0 + mods_ref[0:1, :]) + mods_ref[1:2, :]).astype(BF16)

        @pl.when(i >= nx)
        def _():
            o_ref[...] = (ctx_ref[...] * (1.0 + mods_ref[2:3, :]) + mods_ref[3:4, :]).astype(BF16)

    return pl.pallas_call(
        body, name="modulate_rows", grid=(nx + 1,),
        in_specs=[pl.BlockSpec((ROW_TILE, d), lambda i: (jnp.minimum(i, nx - 1), 0)),
                  pl.BlockSpec((ROW_TILE, d), lambda i: (0, 0)),
                  pl.BlockSpec((8, d), lambda i: (0, 0))],
        out_specs=pl.BlockSpec((ROW_TILE, d), lambda i: (i, 0)),
        out_shape=jax.ShapeDtypeStruct((n + c, d), BF16),
        compiler_params=_params(("parallel",)),
    )(x, ctx, mods)


def _rope_tables(n, c):
    rows = n // GRID_W
    row_ids = jnp.repeat(jnp.arange(rows, dtype=F32), GRID_W)
    col_ids = jnp.tile(jnp.arange(GRID_W, dtype=F32), rows)
    axis_dim = HEAD_DIM // 2
    inv_freq = jnp.power(ROPE_THETA, -jnp.arange(0, axis_dim, 2, dtype=F32) / axis_dim)
    ang_r = row_ids[:, None] * inv_freq
    ang_c = col_ids[:, None] * inv_freq
    ang = jnp.concatenate([ang_r, ang_r, ang_c, ang_c], axis=-1)
    cos, sin = jnp.cos(ang), jnp.sin(ang)
    quarter = (jnp.arange(HEAD_DIM) // (HEAD_DIM // 4)) % 2
    sin_a = jnp.where(quarter == 0, -sin, 0.0)
    sin_b = jnp.where(quarter == 1, sin, 0.0)
    pad = lambda t, v: jnp.concatenate([t, jnp.full((c, HEAD_DIM), v, F32)], axis=0)
    return pad(cos, 1.0), pad(sin_a, 0.0), pad(sin_b, 0.0)


def _rope(x, cos, sin_a, sin_b):
    return x * cos + pltpu.roll(x, 96, 1) * sin_a + pltpu.roll(x, 32, 1) * sin_b


def _rope_t(dy, cos, sin_a, sin_b):
    return dy * cos - pltpu.roll(dy, 96, 1) * sin_a - pltpu.roll(dy, 32, 1) * sin_b


def _rms(x):
    r = lax.rsqrt(jnp.mean(x * x, axis=-1, keepdims=True) + EPS)
    return x * r, r


def _qkv_post(h_all, cos, sin_a, sin_b, q_g, k_g):
    t = h_all.shape[0]
    nt = t // ROW_TILE

    def body(h_ref, cos_ref, sa_ref, sb_ref, qg_ref, kg_ref, qa_ref, ka_ref, va_ref, qb_ref, kb_ref, vb_ref):
        cos_, sa, sb = cos_ref[...], sa_ref[...], sb_ref[...]
        sl = lambda off, hh: h_ref[:, off + hh * HEAD_DIM: off + (hh + 1) * HEAD_DIM]
        for hh in range(HEADS_A):
            qa_ref[hh] = (_rope(sl(QA0, hh), cos_, sa, sb) * ATTN_SCALE).astype(BF16)
        for hh in range(KV_A):
            ka_ref[hh] = _rope(sl(KA0, hh), cos_, sa, sb).astype(BF16)
            va_ref[hh] = sl(VA0, hh).astype(BF16)
        for hh in range(HEADS_B):
            xn, _ = _rms(sl(QB0, hh))
            qb_ref[hh] = (_rope(xn * qg_ref[...], cos_, sa, sb) * ATTN_SCALE).astype(BF16)
        for hh in range(KV_B):
            xn, _ = _rms(sl(KB0, hh))
            kb_ref[hh] = _rope(xn * kg_ref[...], cos_, sa, sb).astype(BF16)
            vb_ref[hh] = sl(VB0, hh).astype(BF16)

    tab = pl.BlockSpec((ROW_TILE, HEAD_DIM), lambda i: (i, 0))
    gain = pl.BlockSpec((1, HEAD_DIM), lambda i: (0, 0))
    hs = lambda nh: pl.BlockSpec((nh, ROW_TILE, HEAD_DIM), lambda i: (0, i, 0))
    sh = lambda nh: jax.ShapeDtypeStruct((nh, t, HEAD_DIM), BF16)
    return pl.pallas_call(
        body, name="qkv_post", grid=(nt,),
        in_specs=[pl.BlockSpec((ROW_TILE, IN_WIDTH), lambda i: (i, 0)), tab, tab, tab, gain, gain],
        out_specs=[hs(HEADS_A), hs(KV_A), hs(KV_A), hs(HEADS_B), hs(KV_B), hs(KV_B)],
        out_shape=[sh(HEADS_A), sh(KV_A), sh(KV_A), sh(HEADS_B), sh(KV_B), sh(KV_B)],
        compiler_params=_params(("parallel",)),
    )(h_all, cos, sin_a, sin_b, q_g, k_g)


def _qkv_bwd_post(h_all, cos, sin_a, sin_b, q_g, k_g, dqa, dka, dva, dqb, dkb, dvb, n):
    t = h_all.shape[0]
    nt = t // ROW_TILE
    nx = n // ROW_TILE

    def body(h_ref, cos_ref, sa_ref, sb_ref, qg_ref, kg_ref,
             dqa_ref, dka_ref, dva_ref, dqb_ref, dkb_ref, dvb_ref, dh_ref, gs_ref):
        i = pl.program_id(0)
        cos_, sa, sb = cos_ref[...], sa_ref[...], sb_ref[...]
        latent = (i < nx).astype(F32)
        sl = lambda off, hh: h_ref[:, off + hh * HEAD_DIM: off + (hh + 1) * HEAD_DIM]

        def put(off, hh, val):
            dh_ref[:, off + hh * HEAD_DIM: off + (hh + 1) * HEAD_DIM] = val.astype(BF16)

        def norm_bwd(x, gain, dy):
            xn, r = _rms(x)
            dxh = dy * gain
            dx = r * (dxh - xn * jnp.mean(dxh * xn, axis=-1, keepdims=True))
            return dx, jnp.sum(dy * xn, axis=0, keepdims=True)

        for hh in range(HEADS_A):
            put(QA0, hh, _rope_t(dqa_ref[hh] * (ATTN_SCALE * latent), cos_, sa, sb))
        for hh in range(KV_A):
            put(KA0, hh, _rope_t(dka_ref[hh], cos_, sa, sb))
            put(VA0, hh, dva_ref[hh])
        gq = jnp.zeros((1, HEAD_DIM), F32)
        gk = jnp.zeros((1, HEAD_DIM), F32)
        for hh in range(HEADS_B):
            dq_t = dqb_ref[hh // GROUP, :, (hh % GROUP) * ROW_TILE:(hh % GROUP + 1) * ROW_TILE]
            dy = _rope_t(dq_t.T * (ATTN_SCALE * latent), cos_, sa, sb)
            dx, g = norm_bwd(sl(QB0, hh), qg_ref[...], dy)
            put(QB0, hh, dx)
            gq = gq + g
        for hh in range(KV_B):
            dy = _rope_t(dkb_ref[hh], cos_, sa, sb)
            dx, g = norm_bwd(sl(KB0, hh), kg_ref[...], dy)
            put(KB0, hh, dx)
            gk = gk + g
            put(VB0, hh, dvb_ref[hh])
        upd = jnp.concatenate([gq, gk, jnp.zeros((6, HEAD_DIM), F32)], axis=0)

        @pl.when(i == 0)
        def _():
            gs_ref[...] = upd

        @pl.when(i != 0)
        def _():
            gs_ref[...] += upd

    tab = pl.BlockSpec((ROW_TILE, HEAD_DIM), lambda i: (i, 0))
    gain = pl.BlockSpec((1, HEAD_DIM), lambda i: (0, 0))
    lat = lambda nh: pl.BlockSpec((nh, ROW_TILE, HEAD_DIM), lambda i: (0, jnp.minimum(i, nx - 1), 0))
    full = lambda nh: pl.BlockSpec((nh, ROW_TILE, HEAD_DIM), lambda i: (0, i, 0))
    return pl.pallas_call(
        body, name="qkv_bwd_post", grid=(nt,),
        in_specs=[pl.BlockSpec((ROW_TILE, IN_WIDTH), lambda i: (i, 0)), tab, tab, tab, gain, gain,
                  lat(HEADS_A), full(KV_A), full(KV_A),
                  pl.BlockSpec((KV_B, None, HEAD_DIM, GROUP * ROW_TILE), lambda i: (0, jnp.minimum(i, nx - 1), 0, 0)),
                  full(KV_B), full(KV_B)],
        out_specs=[pl.BlockSpec((ROW_TILE, IN_WIDTH), lambda i: (i, 0)),
                   pl.BlockSpec((8, HEAD_DIM), lambda i: (0, 0))],
        out_shape=[jax.ShapeDtypeStruct((t, IN_WIDTH), BF16), jax.ShapeDtypeStruct((8, HEAD_DIM), F32)],
        compiler_params=_params(("arbitrary",)),
    )(h_all, cos, sin_a, sin_b, q_g, k_g, dqa, dka, dva, dqb, dkb, dvb)


GB_TQ = 256
GB_TK = 256


def _heads_rows(ref2d, tq):
    return jnp.concatenate([ref2d[:, hh * HEAD_DIM:(hh + 1) * HEAD_DIM] for hh in range(GROUP)], axis=0)


def _attn_b_fwd(qb, kb, vb, n):
    t = kb.shape[1]
    nk = t // GB_TK
    tq = GB_TQ
    nq = n // tq
    qb_step = 2 if nq % 2 == 0 else 1
    rows = qb_step * GROUP * tq

    def body(q_ref, k_ref, v_ref, o_ref, lse_ref, m_s, l_s, acc_s):
        blocks = [(s, hh) for s in range(qb_step) for hh in range(GROUP)]
        q = jnp.concatenate([q_ref[hh, s * tq:(s + 1) * tq, :] for s, hh in blocks], axis=0)
        m_s[...] = jnp.full((1, rows), -jnp.inf, F32)
        l_s[...] = jnp.zeros((1, rows), F32)
        acc_s[...] = jnp.zeros((HEAD_DIM, rows), F32)

        def scores(j):
            start = pl.multiple_of(j * GB_TK, GB_TK)
            return lax.dot_general(k_ref[pl.ds(start, GB_TK), :], q, NT, preferred_element_type=F32)

        def step(j, carry):
            st = scores(j)
            vs = v_ref[pl.ds(pl.multiple_of(j * GB_TK, GB_TK), GB_TK), :]
            m_prev = m_s[...]
            m_new = jnp.maximum(m_prev, jnp.max(st, axis=0, keepdims=True))
            p = jnp.exp(st - m_new)
            alpha = jnp.exp(m_prev - m_new)
            l_s[...] = alpha * l_s[...] + jnp.sum(p, axis=0, keepdims=True)
            acc_s[...] = alpha * acc_s[...] + lax.dot_general(vs, p.astype(BF16), TN, preferred_element_type=F32)
            m_s[...] = m_new
            return carry

        lax.fori_loop(0, nk, step, 0, unroll=4)
        ot = acc_s[...] * (1.0 / l_s[...])
        lse = m_s[...] + jnp.log(l_s[...])
        width = GROUP * tq
        for s in range(qb_step):
            lse_ref[s] = lse[:, s * width:(s + 1) * width]
        for k, (s, hh) in enumerate(blocks):
            o_ref[s * tq:(s + 1) * tq, hh * HEAD_DIM:(hh + 1) * HEAD_DIM] = ot[:, k * tq:(k + 1) * tq].T.astype(BF16)

    return pl.pallas_call(
        body, name="attn_b_fwd", grid=(KV_B, nq // qb_step),
        in_specs=[pl.BlockSpec((GROUP, qb_step * tq, HEAD_DIM), lambda g, i: (g, i, 0)),
                  pl.BlockSpec((None, t, HEAD_DIM), lambda g, i: (g, 0, 0)),
                  pl.BlockSpec((None, t, HEAD_DIM), lambda g, i: (g, 0, 0))],
        out_specs=[pl.BlockSpec((qb_step * tq, GROUP * HEAD_DIM), lambda g, i: (i, KV_A + g)),
                   pl.BlockSpec((None, qb_step, 1, GROUP * tq), lambda g, i: (g, i, 0, 0))],
        out_shape=[jax.ShapeDtypeStruct((n, MIX_WIDTH), BF16),
                   jax.ShapeDtypeStruct((KV_B, nq, 1, GROUP * tq), F32)],
        scratch_shapes=[pltpu.VMEM((1, rows), F32), pltpu.VMEM((1, rows), F32), pltpu.VMEM((HEAD_DIM, rows), F32)],
        compiler_params=_params(("parallel", "parallel")),
    )(qb, kb, vb)


def _attn_b_bwd(qb, kb, vb, dheads, lse, delta, n):
    t = kb.shape[1]
    nk = t // GB_TK
    tq = GB_TQ
    nq = n // tq
    rows = GROUP * tq

    qb_step = 4 if nq % 4 == 0 else 1

    def body(q_ref, k_ref, v_ref, do_ref, lse_ref, dl_ref, dq_ref, dk_ref, dv_ref):
        j = pl.program_id(1)
        i = pl.program_id(2)

        @pl.when(i == 0)
        def _():
            dk_ref[...] = jnp.zeros_like(dk_ref)
            dv_ref[...] = jnp.zeros_like(dv_ref)

        @pl.when(j == 0)
        def _():
            for s in range(qb_step):
                dq_ref[qb_step * i + s] = jnp.zeros((HEAD_DIM, rows), F32)

        blocks = [(s, hh) for s in range(qb_step) for hh in range(GROUP)]
        q = jnp.concatenate([q_ref[hh, s * tq:(s + 1) * tq, :] for s, hh in blocks], axis=0)
        do = jnp.concatenate([do_ref[s * tq:(s + 1) * tq, hh * HEAD_DIM:(hh + 1) * HEAD_DIM] for s, hh in blocks], axis=0)
        lse_row = jnp.concatenate([lse_ref[s] for s in range(qb_step)], axis=1)
        dl_row = jnp.concatenate([dl_ref[s] for s in range(qb_step)], axis=1)
        ks, vs = k_ref[...], v_ref[...]
        st = lax.dot_general(ks, q, NT, preferred_element_type=F32)
        p = jnp.exp(st - lse_row)
        dpt = lax.dot_general(vs, do, NT, preferred_element_type=F32)
        ds = (p * (dpt - dl_row)).astype(BF16)
        dv_part = lax.dot_general(p.astype(BF16), do, NN, preferred_element_type=F32)
        dk_part = lax.dot_general(ds, q, NN, preferred_element_type=F32)
        dq_part = lax.dot_general(ks, ds, TN, preferred_element_type=F32)

        dk_ref[...] += dk_part
        dv_ref[...] += dv_part
        for s in range(qb_step):
            dq_ref[qb_step * i + s] += dq_part[:, s * rows:(s + 1) * rows]

    kv = pl.BlockSpec((None, GB_TK, HEAD_DIM), lambda g, j, i: (g, j, 0))
    row = pl.BlockSpec((None, qb_step, 1, rows), lambda g, j, i: (g, i, 0, 0))
    return pl.pallas_call(
        body, name="attn_b_bwd", grid=(KV_B, nk, nq // qb_step),
        in_specs=[pl.BlockSpec((GROUP, qb_step * tq, HEAD_DIM), lambda g, j, i: (g, i, 0)), kv, kv,
                  pl.BlockSpec((qb_step * tq, GROUP * HEAD_DIM), lambda g, j, i: (i, KV_A + g)), row, row],
        out_specs=[pl.BlockSpec((None, nq, HEAD_DIM, rows), lambda g, j, i: (g, 0, 0, 0)), kv, kv],
        out_shape=[jax.ShapeDtypeStruct((KV_B, nq, HEAD_DIM, rows), F32),
                   jax.ShapeDtypeStruct((KV_B, t, HEAD_DIM), F32),
                   jax.ShapeDtypeStruct((KV_B, t, HEAD_DIM), F32)],
        compiler_params=_params(("parallel", "arbitrary", "arbitrary")),
    )(qb, kb, vb, dheads, lse, delta)


def _delta_rows(dheads, heads):
    n = heads.shape[0]
    tq = GB_TQ
    w = GROUP * HEAD_DIM

    def body(a_ref, b_ref, o_ref):
        prod = a_ref[...].astype(F32) * b_ref[...].astype(F32)
        cols = [jnp.sum(prod[:, hh * HEAD_DIM:(hh + 1) * HEAD_DIM].T, axis=0, keepdims=True) for hh in range(GROUP)]
        o_ref[...] = jnp.concatenate(cols, axis=1)

    blk = pl.BlockSpec((tq, w), lambda g, i: (i, KV_A + g))
    return pl.pallas_call(
        body, name="delta_rows", grid=(KV_B, n // tq),
        in_specs=[blk, blk],
        out_specs=pl.BlockSpec((None, None, 1, GROUP * tq), lambda g, i: (g, i, 0, 0)),
        out_shape=jax.ShapeDtypeStruct((KV_B, n // tq, 1, GROUP * tq), F32),
        compiler_params=_params(("parallel", "parallel")),
    )(dheads, heads)


KWIN = 3 * BLOCK


def _window_scores(q, k_ref, j, n, nb, sink_row):
    c = k_ref.shape[0] - n
    start = pl.multiple_of(jnp.clip(j - 1, 0, nb - 3) * BLOCK, BLOCK)
    kw = k_ref[pl.ds(start, KWIN), :]
    kc = k_ref[pl.ds(n, c), :]
    s_loc = lax.dot_general(kw, q, NT, preferred_element_type=F32)
    s_ctx = lax.dot_general(kc, q, NT, preferred_element_type=F32)
    cols = GROUP * BLOCK
    qpos = j * BLOCK + lax.broadcasted_iota(jnp.int32, (KWIN, cols), 1) % BLOCK
    kpos = start + lax.broadcasted_iota(jnp.int32, (KWIN, cols), 0)
    s_loc = jnp.where(jnp.abs(qpos - kpos) <= WINDOW, s_loc, -jnp.inf)
    m = jnp.maximum(jnp.maximum(jnp.max(s_loc, axis=0, keepdims=True), jnp.max(s_ctx, axis=0, keepdims=True)),
                    sink_row)
    e_loc, e_ctx, e_sink = jnp.exp(s_loc - m), jnp.exp(s_ctx - m), jnp.exp(sink_row - m)
    inv = 1.0 / (jnp.sum(e_loc, axis=0, keepdims=True) + jnp.sum(e_ctx, axis=0, keepdims=True) + e_sink)
    return e_loc * inv, e_ctx * inv, e_sink * inv, start


def _sink_row(sink_ref, g):
    return jnp.concatenate([sink_ref[pl.ds(g * GROUP + hh, 1), :] for hh in range(GROUP)], axis=1)


def _attn_a_fwd(qa, ka, va, sink_b, heads_b, n):
    t = ka.shape[1]
    nb = n // BLOCK
    assert nb >= 3
    wb = 4 if nb % 4 == 0 else 1

    def body(q_ref, k_ref, v_ref, sink_ref, heads_b_ref, o_ref):
        g, jj = pl.program_id(0), pl.program_id(1)
        for s in range(wb):
            j = jj * wb + s
            rows = slice(s * BLOCK, (s + 1) * BLOCK)
            q = q_ref[:, rows, :].reshape(GROUP * BLOCK, HEAD_DIM)
            p_loc, p_ctx, _, start = _window_scores(q, k_ref, j, n, nb, _sink_row(sink_ref, g))
            vw = v_ref[pl.ds(start, KWIN), :]
            vc = v_ref[pl.ds(n, t - n), :]
            ot = (lax.dot_general(vw, p_loc.astype(BF16), TN, preferred_element_type=F32)
                  + lax.dot_general(vc, p_ctx.astype(BF16), TN, preferred_element_type=F32))
            for hh in range(GROUP):
                o_ref[rows, hh * HEAD_DIM:(hh + 1) * HEAD_DIM] = ot[:, hh * BLOCK:(hh + 1) * BLOCK].T.astype(BF16)

    return pl.pallas_call(
        body, name="attn_a_fwd", grid=(KV_A, nb // wb),
        in_specs=[pl.BlockSpec((GROUP, wb * BLOCK, HEAD_DIM), lambda g, j: (g, j, 0)),
                  pl.BlockSpec((None, t, HEAD_DIM), lambda g, j: (g, 0, 0)),
                  pl.BlockSpec((None, t, HEAD_DIM), lambda g, j: (g, 0, 0)),
                  pl.BlockSpec((HEADS_A, HEAD_DIM), lambda g, j: (0, 0)),
                  pl.BlockSpec(memory_space=pl.ANY)],
        out_specs=pl.BlockSpec((wb * BLOCK, GROUP * HEAD_DIM), lambda g, j: (j, g)),
        out_shape=jax.ShapeDtypeStruct((n, MIX_WIDTH), BF16),
        input_output_aliases={4: 0},
        compiler_params=_params(("parallel", "parallel")),
    )(qa, ka, va, sink_b, heads_b)


def _attn_a_bwd(qa, ka, va, sink_b, dheads, n):
    t = ka.shape[1]
    c = t - n
    nb = n // BLOCK
    wb = 4 if nb % 4 == 0 else 1

    def body(q_ref, k_ref, v_ref, sink_ref, do_ref, dq_ref, dk_ref, dv_ref, dsink_ref):
        g, jj = pl.program_id(0), pl.program_id(1)

        @pl.when(jj == 0)
        def _():
            dk_ref[...] = jnp.zeros_like(dk_ref)
            dv_ref[...] = jnp.zeros_like(dv_ref)
            dsink_ref[...] = jnp.zeros_like(dsink_ref)

        for s in range(wb):
            j = jj * wb + s
            rows = slice(s * BLOCK, (s + 1) * BLOCK)
            q = q_ref[:, rows, :].reshape(GROUP * BLOCK, HEAD_DIM)
            do = _heads_rows(do_ref[rows, :], BLOCK)
            p_loc, p_ctx, p_sink, start = _window_scores(q, k_ref, j, n, nb, _sink_row(sink_ref, g))
            kw, vw = k_ref[pl.ds(start, KWIN), :], v_ref[pl.ds(start, KWIN), :]
            kc, vc = k_ref[pl.ds(n, c), :], v_ref[pl.ds(n, c), :]
            dp_loc = lax.dot_general(vw, do, NT, preferred_element_type=F32)
            dp_ctx = lax.dot_general(vc, do, NT, preferred_element_type=F32)
            dl = jnp.sum(p_loc * dp_loc, axis=0, keepdims=True) + jnp.sum(p_ctx * dp_ctx, axis=0, keepdims=True)
            ds_loc = (p_loc * (dp_loc - dl)).astype(BF16)
            ds_ctx = (p_ctx * (dp_ctx - dl)).astype(BF16)
            dqt = (lax.dot_general(kw, ds_loc, TN, preferred_element_type=F32)
                   + lax.dot_general(kc, ds_ctx, TN, preferred_element_type=F32))
            for hh in range(GROUP):
                dq_ref[hh, rows, :] = dqt[:, hh * BLOCK:(hh + 1) * BLOCK].T
            dk_ref[pl.ds(start, KWIN), :] += lax.dot_general(ds_loc, q, NN, preferred_element_type=F32)
            dv_ref[pl.ds(start, KWIN), :] += lax.dot_general(p_loc.astype(BF16), do, NN, preferred_element_type=F32)
            dk_ref[pl.ds(n, c), :] += lax.dot_general(ds_ctx, q, NN, preferred_element_type=F32)
            dv_ref[pl.ds(n, c), :] += lax.dot_general(p_ctx.astype(BF16), do, NN, preferred_element_type=F32)
            dsk = -(p_sink * dl)
            upd = [jnp.broadcast_to(jnp.sum(dsk[:, hh * BLOCK:(hh + 1) * BLOCK], axis=1, keepdims=True), (1, HEAD_DIM))
                   for hh in range(GROUP)]
            dsink_ref[...] += jnp.concatenate(upd + [jnp.zeros((8 - GROUP, HEAD_DIM), F32)], axis=0)

    res = pl.BlockSpec((None, t, HEAD_DIM), lambda g, j: (g, 0, 0))
    return pl.pallas_call(
        body, name="attn_a_bwd", grid=(KV_A, nb // wb),
        in_specs=[pl.BlockSpec((GROUP, wb * BLOCK, HEAD_DIM), lambda g, j: (g, j, 0)), res, res,
                  pl.BlockSpec((HEADS_A, HEAD_DIM), lambda g, j: (0, 0)),
                  pl.BlockSpec((wb * BLOCK, GROUP * HEAD_DIM), lambda g, j: (j, g))],
        out_specs=[pl.BlockSpec((GROUP, wb * BLOCK, HEAD_DIM), lambda g, j: (g, j, 0)), res, res,
                   pl.BlockSpec((None, 8, HEAD_DIM), lambda g, j: (g, 0, 0))],
        out_shape=[jax.ShapeDtypeStruct((HEADS_A, n, HEAD_DIM), F32),
                   jax.ShapeDtypeStruct((KV_A, t, HEAD_DIM), F32),
                   jax.ShapeDtypeStruct((KV_A, t, HEAD_DIM), F32),
                   jax.ShapeDtypeStruct((KV_A, 8, HEAD_DIM), F32)],
        compiler_params=_params(("parallel", "arbitrary")),
    )(qa, ka, va, sink_b, dheads)


def _ln_stats(r):
    mu = jnp.mean(r, axis=-1, keepdims=True)
    cen = r - mu
    rstd = lax.rsqrt(jnp.mean(cen * cen, axis=-1, keepdims=True) + EPS)
    return cen * rstd, rstd


def _ln_bwd(dy, xhat, rstd, gain):
    dxh = dy * gain
    return rstd * (dxh - jnp.mean(dxh, axis=-1, keepdims=True)
                   - xhat * jnp.mean(dxh * xhat, axis=-1, keepdims=True))


def _accumulate_rows(ref, rows, i):
    pad = [jnp.zeros_like(rows[0])] * (8 - len(rows))
    upd = jnp.concatenate(rows + pad, axis=0)

    @pl.when(i == 0)
    def _():
        ref[...] = upd

    @pl.when(i != 0)
    def _():
        ref[...] += upd


def _colsum(v):
    return jnp.sum(v, axis=0, keepdims=True)


LN_TILE = 256


def _res_ln1(x, a, vec):
    n, d = x.shape

    def body(x_ref, a_ref, v_ref, xh_ref, rs_ref, u_ref):
        r1 = DN_ALPHA * x_ref[...] + v_ref[0:1, :] * a_ref[...]
        xhat, rstd = _ln_stats(r1)
        xh_ref[...] = xhat
        rs_ref[...] = rstd
        x1 = xhat * v_ref[1:2, :] + v_ref[2:3, :]
        u_ref[...] = (x1 * (1.0 + v_ref[3:4, :]) + v_ref[4:5, :]).astype(BF16)

    row = pl.BlockSpec((LN_TILE, d), lambda i: (i, 0))
    return pl.pallas_call(
        body, name="res_ln1", grid=(n // LN_TILE,),
        in_specs=[row, row, pl.BlockSpec((8, d), lambda i: (0, 0))],
        out_specs=[row, pl.BlockSpec((LN_TILE, 1), lambda i: (i, 0)), row],
        out_shape=[jax.ShapeDtypeStruct((n, d), F32), jax.ShapeDtypeStruct((n, 1), F32),
                   jax.ShapeDtypeStruct((n, d), BF16)],
        compiler_params=_params(("parallel",)),
    )(x, a, vec)


def _res_ln2_loss(xhat1, f, target, vec):
    n, d = f.shape

    def body(xh_ref, f_ref, t_ref, v_ref, dr_ref, df_ref, s_ref):
        i = pl.program_id(0)
        x1 = xh_ref[...] * v_ref[1:2, :] + v_ref[2:3, :]
        fv = f_ref[...]
        xhat, rstd = _ln_stats(DN_ALPHA * x1 + v_ref[0:1, :] * fv)
        err = xhat * v_ref[3:4, :] + v_ref[4:5, :] - t_ref[...]
        dy = err * (1.0 / d)
        dr2 = _ln_bwd(dy, xhat, rstd, v_ref[3:4, :])
        dr_ref[...] = dr2
        df_ref[...] = (dr2 * v_ref[0:1, :]).astype(BF16)
        _accumulate_rows(s_ref, [_colsum(dy * xhat), _colsum(dy), _colsum(dr2 * fv),
                                 _colsum(err * err) * (0.5 / d)], i)

    row = pl.BlockSpec((LN_TILE, d), lambda i: (i, 0))
    return pl.pallas_call(
        body, name="res_ln2_loss", grid=(n // LN_TILE,),
        in_specs=[row, row, row, pl.BlockSpec((8, d), lambda i: (0, 0))],
        out_specs=[row, row, pl.BlockSpec((8, d), lambda i: (0, 0))],
        out_shape=[jax.ShapeDtypeStruct((n, d), F32), jax.ShapeDtypeStruct((n, d), BF16),
                   jax.ShapeDtypeStruct((8, d), F32)],
        compiler_params=_params(("arbitrary",)),
    )(xhat1, f, target, vec)


def _ln1_bwd(du2, dr2, xhat1, rstd1, a, vec):
    n, d = du2.shape

    def body(du_ref, dr2_ref, xh_ref, rs_ref, a_ref, v_ref, dxp_ref, da_ref, s_ref):
        i = pl.program_id(0)
        du, xhat = du_ref[...], xh_ref[...]
        x1 = xhat * v_ref[1:2, :] + v_ref[2:3, :]
        dx1 = DN_ALPHA * dr2_ref[...] + du * (1.0 + v_ref[0:1, :])
        dr1 = _ln_bwd(dx1, xhat, rs_ref[...], v_ref[1:2, :])
        dxp_ref[...] = DN_ALPHA * dr1
        da_ref[...] = (dr1 * v_ref[3:4, :]).astype(BF16)
        _accumulate_rows(s_ref, [_colsum(du * x1), _colsum(du), _colsum(dx1 * xhat), _colsum(dx1),
                                 _colsum(dr1 * a_ref[...])], i)

    row = pl.BlockSpec((LN_TILE, d), lambda i: (i, 0))
    return pl.pallas_call(
        body, name="ln1_bwd", grid=(n // LN_TILE,),
        in_specs=[row, row, row, pl.BlockSpec((LN_TILE, 1), lambda i: (i, 0)), row,
                  pl.BlockSpec((8, d), lambda i: (0, 0))],
        out_specs=[row, row, pl.BlockSpec((8, d), lambda i: (0, 0))],
        out_shape=[jax.ShapeDtypeStruct((n, d), F32), jax.ShapeDtypeStruct((n, d), BF16),
                   jax.ShapeDtypeStruct((8, d), F32)],
        compiler_params=_params(("arbitrary",)),
    )(du2, dr2, xhat1, rstd1, a, vec)


def _mod1_bwd(du_all, dxp, x, ctx, mods):
    n, d = x.shape
    nx = n // ROW_TILE

    def body(du_ref, dxp_ref, x_ref, ctx_ref, m_ref, gx_ref, s_ref):
        i = pl.program_id(0)
        du = du_ref[...]
        zero = jnp.zeros((1, d), F32)

        @pl.when(i == 0)
        def _():
            s_ref[...] = jnp.zeros_like(s_ref)

        @pl.when(i < nx)
        def _():
            gx_ref[...] = dxp_ref[...] + du * (1.0 + m_ref[0:1, :])
            s_ref[...] += jnp.concatenate([_colsum(du * x_ref[...]), _colsum(du)] + [zero] * 6, axis=0)

        @pl.when(i >= nx)
        def _():
            s_ref[...] += jnp.concatenate([zero, zero, _colsum(du * ctx_ref[...]), _colsum(du)] + [zero] * 4, axis=0)

    lat = pl.BlockSpec((ROW_TILE, d), lambda i: (jnp.minimum(i, nx - 1), 0))
    return pl.pallas_call(
        body, name="mod1_bwd", grid=(nx + 1,),
        in_specs=[pl.BlockSpec((ROW_TILE, d), lambda i: (i, 0)), lat, lat,
                  pl.BlockSpec((ROW_TILE, d), lambda i: (0, 0)), pl.BlockSpec((8, d), lambda i: (0, 0))],
        out_specs=[lat, pl.BlockSpec((8, d), lambda i: (0, 0))],
        out_shape=[jax.ShapeDtypeStruct((n, d), F32), jax.ShapeDtypeStruct((8, d), F32)],
        compiler_params=_params(("arbitrary",)),
    )(du_all, dxp, x, ctx, mods)


FFN_TM = 1024
FFN_TN = 512


def _gate_up(u2, wg, wu, after):
    n, d = u2.shape
    f = wg.shape[1]

    def body(u_ref, wg_ref, wu_ref, after_ref, g_ref, up_ref, h_ref):
        u = u_ref[...]
        g = lax.dot_general(u, wg_ref[...], NN, preferred_element_type=F32)
        up = lax.dot_general(u, wu_ref[...], NN, preferred_element_type=F32)
        g_ref[...] = g.astype(BF16)
        up_ref[...] = up.astype(BF16)
        h_ref[...] = (g * jax.nn.sigmoid(g) * up).astype(BF16)

    tm = min(FFN_TM, n)
    wspec = pl.BlockSpec((d, FFN_TN), lambda j, i: (0, j))
    ospec = pl.BlockSpec((tm, FFN_TN), lambda j, i: (i, j))
    return pl.pallas_call(
        body, name="gate_up", grid=(f // FFN_TN, n // tm),
        in_specs=[pl.BlockSpec((tm, d), lambda j, i: (i, 0)), wspec, wspec, pl.BlockSpec(memory_space=pl.ANY)],
        out_specs=[ospec, ospec, ospec],
        out_shape=[jax.ShapeDtypeStruct((n, f), BF16)] * 3,
        compiler_params=_params(("parallel", "parallel")),
    )(u2, wg, wu, after)


def _glu_bwd(df, wd, g, u):
    n, d = df.shape
    f = wd.shape[0]

    def body(df_ref, wd_ref, g_ref, u_ref, dg_ref, du_ref):
        dh = lax.dot_general(df_ref[...], wd_ref[...], NT, preferred_element_type=F32)
        gv = g_ref[...].astype(F32)
        sig = jax.nn.sigmoid(gv)
        du_ref[...] = (dh * (gv * sig)).astype(BF16)
        dg_ref[...] = (dh * u_ref[...].astype(F32) * (sig * (1.0 + gv * (1.0 - sig)))).astype(BF16)

    tm = min(FFN_TM, n)
    ospec = pl.BlockSpec((tm, FFN_TN), lambda i, j: (i, j))
    return pl.pallas_call(
        body, name="glu_bwd", grid=(n // tm, f // FFN_TN),
        in_specs=[pl.BlockSpec((tm, d), lambda i, j: (i, 0)),
                  pl.BlockSpec((FFN_TN, d), lambda i, j: (j, 0)), ospec, ospec],
        out_specs=[ospec, ospec],
        out_shape=[jax.ShapeDtypeStruct((n, f), BF16), jax.ShapeDtypeStruct((n, f), BF16)],
        compiler_params=_params(("parallel", "parallel")),
    )(df, wd, g, u)


def _du2(dg, du, wg, wu):
    n, f = dg.shape
    d = wg.shape[0]
    tm, tn, tk = min(1024, n), 1024, 1408
    nk = f // tk

    def body(dg_ref, du_ref, wg_ref, wu_ref, o_ref):
        @pl.when(pl.program_id(2) == 0)
        def _():
            o_ref[...] = jnp.zeros_like(o_ref)

        o_ref[...] += (lax.dot_general(dg_ref[...], wg_ref[...], NT, preferred_element_type=F32)
                       + lax.dot_general(du_ref[...], wu_ref[...], NT, preferred_element_type=F32))

    aspec = pl.BlockSpec((tm, tk), lambda i, j, kk: (i, kk))
    wspec = pl.BlockSpec((tn, tk), lambda i, j, kk: (j, kk))
    return pl.pallas_call(
        body, name="du2", grid=(n // tm, d // tn, nk),
        in_specs=[aspec, aspec, wspec, wspec],
        out_specs=pl.BlockSpec((tm, tn), lambda i, j, kk: (i, j)),
        out_shape=jax.ShapeDtypeStruct((n, d), F32),
        compiler_params=_params(("parallel", "parallel", "arbitrary")),
    )(dg, du, wg, wu)


def _rows8(rows, d=D_MODEL):
    rows = [r.reshape(1, d).astype(F32) for r in rows]
    return jnp.concatenate(rows + [jnp.zeros((8 - len(rows), d), F32)], axis=0)


W_GROUPS = (("w_in",), ("w_out", "w_gate", "w_up"), ("w_down",))
G_GROUPS = (("w_down", "w_gate", "w_up"), ("w_out",), ("w_in",))


def _layer_fwd_bwd(x, ctx, target, mod, mod_ctx, weights, prefetch, grads_out,
                   q_g, k_g, sink, ln1_g, ln1_b, ln2_g, ln2_b):
    n, d = x.shape
    c = ctx.shape[0]
    sh1, sc1, g1, sh2, sc2, g2 = [mod[:, k * d:(k + 1) * d] for k in range(6)]
    csh1, csc1 = mod_ctx[:, 0:d], mod_ctx[:, d:2 * d]
    cos, sin_a, sin_b = _rope_tables(n, c)
    sink_b = jnp.broadcast_to(sink.reshape(HEADS_A, 1), (HEADS_A, HEAD_DIM)).astype(F32)

    u_all = _modulate_rows(x, ctx, _rows8([sc1, sh1, csc1, csh1]))
    (w_in,) = weights(0, u_all)
    h_all = _matmul(u_all, w_in, name="qkv_proj", tm=_fit(n + c, 1088), tn=1024, tk=2048, out_dtype=F32)
    qa, ka, va, qb, kb, vb = _qkv_post(h_all, cos, sin_a, sin_b, q_g, k_g)
    heads_b, lse = _attn_b_fwd(qb, kb, vb, n)
    zero = prefetch(1, heads_b)
    heads = _attn_a_fwd(qa, ka, va, sink_b + zero, heads_b, n)
    w_out, w_gate, w_up = weights(1, heads)
    a = _matmul(heads, w_out, name="out_proj", tm=1024, tn=1024, tk=2048, out_dtype=F32)
    xhat1, rstd1, u2 = _res_ln1(x, a, _rows8([g1, ln1_g, ln1_b, sc2, sh2]))
    zero = prefetch(2, u2)
    gg, uu, hh = _gate_up(u2, w_gate, w_up, zero.reshape(1, 1))
    (w_down,) = weights(2, hh)
    f = _matmul(hh, w_down, name="ffn_down", tm=1024, tn=512, tk=FFN, out_dtype=F32)
    dr2, df, s_ln2 = _res_ln2_loss(xhat1, f, target, _rows8([g2, ln1_g, ln1_b, ln2_g, ln2_b]))

    dgg, duu = _glu_bwd(df, w_down, gg, uu)
    dw_down = _matmul(hh, df, name="dw_down", ta=True, tm=512, tn=1024, tk=n, out_dtype=BF16)
    dw_gate = _matmul(u2, dgg, name="dw_gate", ta=True, tm=1024, tn=512, tk=n, out_dtype=BF16)
    dw_up = _matmul(u2, duu, name="dw_up", ta=True, tm=1024, tn=512, tk=n, out_dtype=BF16)
    zero = grads_out(0, [dw_down, dw_gate, dw_up])
    du2 = _du2(dgg, duu, w_gate, w_up)
    dxp, da, s_ln1 = _ln1_bwd(du2, dr2, xhat1, rstd1, a, _rows8([sc2, ln1_g, ln1_b, g1]) + zero)

    dheads = _matmul(da, w_out, name="d_heads", tb=True, tm=1024, tn=1024, tk=2048, out_dtype=BF16)
    dw_out = _matmul(heads, da, name="dw_out", ta=True, tm=1024, tn=1024, tk=n, out_dtype=BF16)
    zero = grads_out(1, [dw_out])
    delta = _delta_rows(dheads, heads)
    dqa, dka, dva, dsink = _attn_a_bwd(qa, ka, va, sink_b + zero, dheads, n)
    dqb, dkb, dvb = _attn_b_bwd(qb, kb, vb, dheads, lse, delta, n)
    dh_all, s_gain = _qkv_bwd_post(h_all, cos, sin_a, sin_b, q_g, k_g, dqa, dka, dva, dqb, dkb, dvb, n)
    dw_in = _matmul(u_all, dh_all, name="dw_in", ta=True, tm=1024, tn=1024, tk=n + c, out_dtype=BF16)
    zero = grads_out(2, [dw_in])
    du_all = _matmul(dh_all, w_in, name="d_u1", tb=True, tm=_fit(n + c, 1088), tn=1024, tk=IN_WIDTH, out_dtype=F32,
                     after=zero.reshape(1, 1))
    grad_x, s_mod1 = _mod1_bwd(du_all, dxp, x, ctx, _rows8([sc1]) + zero)

    dsink_row = jnp.concatenate([dsink[0, 0:GROUP, 0], dsink[1, 0:GROUP, 0]]).reshape(1, HEADS_A)
    misc = jnp.concatenate([s_gain[0:1], s_gain[1:2], dsink_row,
                            jnp.zeros((1, d - 2 * HEAD_DIM - HEADS_A), F32)], axis=1)
    partial = jnp.concatenate([
        s_mod1[1:2], s_mod1[0:1], s_ln1[4:5],
        s_ln1[1:2], s_ln1[0:1], s_ln2[2:3],
        s_mod1[3:4], s_mod1[2:3],
        s_ln1[2:3], s_ln1[3:4], s_ln2[0:1], s_ln2[1:2],
        s_ln2[3:4], misc, jnp.zeros((2, d), F32)], axis=0)
    return grad_x, partial


ANY = pl.BlockSpec(memory_space=pl.ANY)
VMEM_FULL = pl.BlockSpec(memory_space=pltpu.VMEM)
N_CHIP_PEERS = 3


def _me():
    return lax.axis_index("x"), lax.axis_index("y"), lax.axis_index("c")


def _other_chips(x, y):
    return [(1 - x, y), (x, 1 - y), (1 - x, 1 - y)]


def _shard_of(chip):
    return 2 * chip[0] + chip[1]


def _dev_index(x, y, c):
    return 4 * x + 2 * y + c


def _rcopy(src, dst, send_sems, recv_sems, k, dev):
    return pltpu.make_async_remote_copy(src_ref=src, dst_ref=dst, send_sem=send_sems.at[k], recv_sem=recv_sems.at[k],
                                        device_id=dev, device_id_type=MESH)


BIG = (("w_in", (D_MODEL, IN_WIDTH), 1), ("w_out", (MIX_WIDTH, D_MODEL), 0), ("w_gate", (D_MODEL, FFN), 1),
       ("w_up", (D_MODEL, FFN), 1), ("w_down", (FFN, D_MODEL), 0))


def _sub(ref, axis, idx, size):
    start = pl.multiple_of(idx * size, size)
    return ref.at[pl.ds(start, size), :] if axis == 0 else ref.at[:, pl.ds(start, size)]


def _shape_div(shape, axis, parts):
    return tuple(s // parts if a == axis else s for a, s in enumerate(shape))


def _piece(a, ref, shard, half):
    _, full, axis = BIG[a]
    view = _sub(ref, axis, shard, full[axis] // N_SHARD)
    return _sub(view, 1 - axis, half, full[1 - axis] // 2)


HBM = pl.BlockSpec(memory_space=pltpu.HBM)
SEM = pl.BlockSpec(memory_space=pltpu.SEMAPHORE)
EFFECT = pltpu.SideEffectType.DATAFLOW_SIDE_EFFECTING
BIG_INDEX = {name: a for a, (name, _, _) in enumerate(BIG)}


def _in_hbm(arr):
    return pltpu.with_memory_space_constraint(arr, pltpu.HBM)


def _gather_start(tag, arrs, bufs, prev):
    n_arr = len(arrs)

    def body(*refs):
        ins = refs[:n_arr]
        send_sems, recv_sems = refs[n_arr + 1], refs[n_arr + 2]
        token = refs[-1]
        x, y, c = _me()
        s_me = _shard_of((x, y))
        for i, a in enumerate(arrs):
            mine = _piece(a, ins[i], s_me, c)
            for j, chip in enumerate(_other_chips(x, y)):
                _rcopy(mine, mine, send_sems, recv_sems, N_CHIP_PEERS * i + j, (*chip, c)).start()
        token[...] = jnp.zeros_like(token)

    n_sem = N_CHIP_PEERS * n_arr
    outs = pl.pallas_call(
        body, name="gather_start_" + tag,
        in_specs=[HBM] * n_arr + [ANY],
        out_specs=[SEM, SEM] + [HBM] * n_arr + [VMEM_FULL],
        out_shape=[pltpu.SemaphoreType.DMA((n_sem,)), pltpu.SemaphoreType.DMA((n_sem,))]
        + [pltpu.HBM(BIG[a][1], BF16) for a in arrs] + [jax.ShapeDtypeStruct((8, HEAD_DIM), F32)],
        input_output_aliases={i: 2 + i for i in range(n_arr)},
        compiler_params=pltpu.CompilerParams(has_side_effects=EFFECT),
    )(*[_in_hbm(b) for b in bufs], prev)
    return outs[0], outs[1], list(outs[2:2 + n_arr]), outs[-1]


def _gather_wait(tag, arrs, send_sems, recv_sems, bufs, after):
    n_arr = len(arrs)

    def body(*refs):
        ins = refs[:n_arr]
        send_sems_, recv_sems_ = refs[n_arr], refs[n_arr + 1]
        x, y, c = _me()
        s_me = _shard_of((x, y))
        for i, a in enumerate(arrs):
            mine = _piece(a, ins[i], s_me, c)
            for j, chip in enumerate(_other_chips(x, y)):
                landed = _piece(a, ins[i], _shard_of(chip), c)
                cp = _rcopy(mine, landed, send_sems_, recv_sems_, N_CHIP_PEERS * i + j, (*chip, c))
                cp.wait_send()
                cp.wait_recv()

    outs = pl.pallas_call(
        body, name="gather_wait_" + tag,
        in_specs=[HBM] * n_arr + [SEM, SEM, ANY],
        out_specs=[HBM] * n_arr,
        out_shape=[pltpu.HBM(BIG[a][1], BF16) for a in arrs],
        input_output_aliases={i: i for i in range(n_arr)},
        compiler_params=pltpu.CompilerParams(has_side_effects=EFFECT),
    )(*bufs, send_sems, recv_sems, after)
    return list(outs)


def _gather_forward(tag, arrs, bufs):
    n_arr = len(arrs)

    def body(*refs):
        outs = refs[n_arr:2 * n_arr]
        send_sems, recv_sems = refs[2 * n_arr:]
        x, y, c = _me()
        sibling = (x, y, 1 - c)
        chips = _other_chips(x, y)
        copies = []
        for i, a in enumerate(arrs):
            for j, chip in enumerate(chips):
                landed = _piece(a, outs[i], _shard_of(chip), c)
                cp = _rcopy(landed, landed, send_sems, recv_sems, N_CHIP_PEERS * i + j, sibling)
                cp.start()
                copies.append(cp)
        for i, a in enumerate(arrs):
            for j, chip in enumerate(chips):
                other = _piece(a, outs[i], _shard_of(chip), 1 - c)
                _rcopy(other, other, send_sems, recv_sems, N_CHIP_PEERS * i + j, sibling).wait_recv()
        for cp in copies:
            cp.wait_send()

    n_sem = N_CHIP_PEERS * n_arr
    return list(pl.pallas_call(
        body, name="gather_forward_" + tag,
        in_specs=[ANY] * n_arr, out_specs=[ANY] * n_arr,
        out_shape=[jax.ShapeDtypeStruct(BIG[a][1], BF16) for a in arrs],
        input_output_aliases={i: i for i in range(n_arr)},
        scratch_shapes=[pltpu.SemaphoreType.DMA((n_sem,)), pltpu.SemaphoreType.DMA((n_sem,))],
    )(*bufs))


def _forward_start(tag, arrs, bufs):
    n_arr = len(arrs)

    def body(*refs):
        ins = refs[:n_arr]
        send_sems, recv_sems = refs[n_arr], refs[n_arr + 1]
        token = refs[-1]
        x, y, c = _me()
        for i, a in enumerate(arrs):
            for j, chip in enumerate(_other_chips(x, y)):
                landed = _piece(a, ins[i], _shard_of(chip), c)
                _rcopy(landed, landed, send_sems, recv_sems, N_CHIP_PEERS * i + j, (x, y, 1 - c)).start()
        token[...] = jnp.zeros_like(token)

    n_sem = N_CHIP_PEERS * n_arr
    outs = pl.pallas_call(
        body, name="gather_forward_start_" + tag,
        in_specs=[HBM] * n_arr,
        out_specs=[SEM, SEM] + [HBM] * n_arr + [VMEM_FULL],
        out_shape=[pltpu.SemaphoreType.DMA((n_sem,)), pltpu.SemaphoreType.DMA((n_sem,))]
        + [pltpu.HBM(BIG[a][1], BF16) for a in arrs] + [jax.ShapeDtypeStruct((8, HEAD_DIM), F32)],
        input_output_aliases={i: 2 + i for i in range(n_arr)},
        compiler_params=pltpu.CompilerParams(has_side_effects=EFFECT),
    )(*bufs)
    return outs[0], outs[1], list(outs[2:2 + n_arr]), outs[-1]


def _forward_wait(tag, arrs, send_sems, recv_sems, bufs, after):
    n_arr = len(arrs)

    def body(*refs):
        ins = refs[:n_arr]
        send_sems_, recv_sems_ = refs[n_arr], refs[n_arr + 1]
        x, y, c = _me()
        for i, a in enumerate(arrs):
            for j, chip in enumerate(_other_chips(x, y)):
                mine = _piece(a, ins[i], _shard_of(chip), c)
                other = _piece(a, ins[i], _shard_of(chip), 1 - c)
                cp = _rcopy(mine, other, send_sems_, recv_sems_, N_CHIP_PEERS * i + j, (x, y, 1 - c))
                cp.wait_send()
                cp.wait_recv()

    outs = pl.pallas_call(
        body, name="gather_forward_wait_" + tag,
        in_specs=[HBM] * n_arr + [SEM, SEM, ANY],
        out_specs=[HBM] * n_arr,
        out_shape=[pltpu.HBM(BIG[a][1], BF16) for a in arrs],
        input_output_aliases={i: i for i in range(n_arr)},
        compiler_params=pltpu.CompilerParams(has_side_effects=EFFECT),
    )(*bufs, send_sems, recv_sems, after)
    return list(outs)


def _peers(x, y, c):
    return [(x ^ (mask >> 2), y ^ ((mask >> 1) & 1), c ^ (mask & 1)) for mask in range(1, N_DEV)]


def _received_shape(a):
    _, full, axis = BIG[a]
    return (N_DEV - 1,) + _shape_div(_shape_div(full, 1 - axis, 2), axis, N_SHARD)


def _pieces_start(tag, arrs, dws):
    n_arr = len(arrs)

    def body(*refs):
        srcs, lands = refs[:n_arr], refs[n_arr:2 * n_arr]
        send_sems, recv_sems = refs[2 * n_arr], refs[2 * n_arr + 1]
        token = refs[-1]
        x, y, c = _me()
        for i, a in enumerate(arrs):
            for k, peer in enumerate(_peers(x, y, c)):
                src = _piece(a, srcs[i], _shard_of(peer[:2]), peer[2])
                _rcopy(src, lands[i].at[k], send_sems, recv_sems, (N_DEV - 1) * i + k, peer).start()
        token[...] = jnp.zeros_like(token)

    n_sem = (N_DEV - 1) * n_arr
    lands = [_in_hbm(lax.empty(_received_shape(a), BF16)) for a in arrs]
    outs = pl.pallas_call(
        body, name="grad_pieces_start_" + tag,
        in_specs=[HBM] * (2 * n_arr),
        out_specs=[SEM, SEM] + [HBM] * (2 * n_arr) + [VMEM_FULL],
        out_shape=[pltpu.SemaphoreType.DMA((n_sem,)), pltpu.SemaphoreType.DMA((n_sem,))]
        + [pltpu.HBM(BIG[a][1], BF16) for a in arrs] + [pltpu.HBM(_received_shape(a), BF16) for a in arrs]
        + [jax.ShapeDtypeStruct((8, HEAD_DIM), F32)],
        input_output_aliases={i: 2 + i for i in range(2 * n_arr)},
        compiler_params=pltpu.CompilerParams(has_side_effects=EFFECT),
    )(*[_in_hbm(dw) for dw in dws], *lands)
    return outs[0], outs[1], list(outs[2:2 + n_arr]), list(outs[2 + n_arr:2 + 2 * n_arr]), outs[-1]


def _pieces_wait(tag, arrs, send_sems, recv_sems, dws, lands, after):
    n_arr = len(arrs)

    def body(*refs):
        srcs, lands_ = refs[:n_arr], refs[n_arr:2 * n_arr]
        send_sems_, recv_sems_ = refs[2 * n_arr], refs[2 * n_arr + 1]
        x, y, c = _me()
        for i, a in enumerate(arrs):
            for k, peer in enumerate(_peers(x, y, c)):
                src = _piece(a, srcs[i], _shard_of(peer[:2]), peer[2])
                cp = _rcopy(src, lands_[i].at[k], send_sems_, recv_sems_, (N_DEV - 1) * i + k, peer)
                cp.wait_send()
                cp.wait_recv()

    outs = pl.pallas_call(
        body, name="grad_pieces_wait_" + tag,
        in_specs=[HBM] * (2 * n_arr) + [SEM, SEM, ANY],
        out_specs=[HBM] * (2 * n_arr),
        out_shape=[pltpu.HBM(BIG[a][1], BF16) for a in arrs] + [pltpu.HBM(_received_shape(a), BF16) for a in arrs],
        input_output_aliases={i: i for i in range(2 * n_arr)},
        compiler_params=pltpu.CompilerParams(has_side_effects=EFFECT),
    )(*dws, *lands, send_sems, recv_sems, after)
    return list(outs[:n_arr]), list(outs[n_arr:])


def _join_start(tag, g_halves):
    n_arr = len(g_halves)

    def body(*refs):
        srcs, lands = refs[:n_arr], refs[n_arr:2 * n_arr]
        send_sems, recv_sems = refs[2 * n_arr], refs[2 * n_arr + 1]
        token = refs[-1]
        x, y, c = _me()
        for i in range(n_arr):
            _rcopy(srcs[i], lands[i], send_sems, recv_sems, i, (x, y, 1 - c)).start()
        token[...] = jnp.zeros_like(token)

    shapes = [pltpu.HBM(g.shape, F32) for g in g_halves]
    outs = pl.pallas_call(
        body, name="grad_join_start_" + tag,
        in_specs=[HBM] * (2 * n_arr),
        out_specs=[SEM, SEM] + [HBM] * (2 * n_arr) + [VMEM_FULL],
        out_shape=[pltpu.SemaphoreType.DMA((n_arr,)), pltpu.SemaphoreType.DMA((n_arr,))] + shapes + shapes
        + [jax.ShapeDtypeStruct((8, HEAD_DIM), F32)],
        input_output_aliases={i: 2 + i for i in range(2 * n_arr)},
        compiler_params=pltpu.CompilerParams(has_side_effects=EFFECT),
    )(*[_in_hbm(g) for g in g_halves], *[_in_hbm(lax.empty(g.shape, F32)) for g in g_halves])
    return outs[0], outs[1], list(outs[2:2 + n_arr]), list(outs[2 + n_arr:2 + 2 * n_arr]), outs[-1]


def _join_wait(tag, send_sems, recv_sems, g_halves, lands, after):
    n_arr = len(g_halves)

    def body(*refs):
        srcs, lands_ = refs[:n_arr], refs[n_arr:2 * n_arr]
        send_sems_, recv_sems_ = refs[2 * n_arr], refs[2 * n_arr + 1]
        x, y, c = _me()
        for i in range(n_arr):
            cp = _rcopy(srcs[i], lands_[i], send_sems_, recv_sems_, i, (x, y, 1 - c))
            cp.wait_send()
            cp.wait_recv()

    shapes = [pltpu.HBM(g.shape, F32) for g in g_halves]
    outs = pl.pallas_call(
        body, name="grad_join_wait_" + tag,
        in_specs=[HBM] * (2 * n_arr) + [SEM, SEM, ANY],
        out_specs=[HBM] * (2 * n_arr),
        out_shape=shapes + shapes,
        input_output_aliases={i: i for i in range(2 * n_arr)},
        compiler_params=pltpu.CompilerParams(has_side_effects=EFFECT),
    )(*g_halves, *lands, send_sems, recv_sems, after)
    return list(outs[:n_arr]), list(outs[n_arr:])


def _piece_sum(a, dw, shard, core, received):
    name, full, axis = BIG[a]
    rows, cols = _received_shape(a)[1:]
    tr = _fit(rows, ROW_TILE)
    nbr = rows // tr

    def body(w_ref, dw_ref, rec_ref, o_ref):
        acc = dw_ref[...].astype(F32)
        for k in range(N_DEV - 1):
            acc = acc + rec_ref[k].astype(F32)
        o_ref[...] = acc

    if axis == 0:
        own = pl.BlockSpec((tr, cols), lambda i, w: (w[0] * nbr + i, w[1]))
    else:
        own = pl.BlockSpec((tr, cols), lambda i, w: (w[1] * nbr + i, w[0]))
    return pl.pallas_call(
        body, name="grad_sum_pieces_" + name,
        grid_spec=pltpu.PrefetchScalarGridSpec(
            num_scalar_prefetch=1, grid=(nbr,),
            in_specs=[own, pl.BlockSpec((N_DEV - 1, tr, cols), lambda i, w: (0, i, 0))],
            out_specs=pl.BlockSpec((tr, cols), lambda i, w: (i, 0))),
        out_shape=jax.ShapeDtypeStruct((rows, cols), F32),
        compiler_params=_params(("parallel",)),
    )(jnp.stack([shard, core]).astype(jnp.int32), dw, received)


def _scatter_begin(tag, arrs, dws):
    send_sems, recv_sems, dws, lands, token = _pieces_start(tag, arrs, dws)
    return (send_sems, recv_sems, dws, lands), token[0, 0]


def _scatter_reduce(tag, arrs, state, after):
    x, y, c = _me()
    send_sems, recv_sems, dws, lands = state
    dws, lands = _pieces_wait(tag, arrs, send_sems, recv_sems, dws, lands, after)
    g_own = [_piece_sum(a, dw, _shard_of((x, y)), c, r) for a, dw, r in zip(arrs, dws, lands)]
    send_sems, recv_sems, g_own, lands, token = _join_start(tag, g_own)
    return (send_sems, recv_sems, g_own, lands), token


def _scatter_end(tag, state, after):
    return _join_wait(tag, *state, after)


def _rows_start(tag, block):
    r, d = block.shape

    def body(src, land, send_sems, recv_sems, src_thru, land_thru, token):
        x, y, c = _me()
        for k, peer in enumerate(_peers(x, y, c)):
            _rcopy(src, land.at[_dev_index(x, y, c)], send_sems, recv_sems, k, peer).start()
        token[...] = jnp.zeros_like(token)

    outs = pl.pallas_call(
        body, name="rows_start_" + tag,
        in_specs=[HBM, HBM],
        out_specs=[SEM, SEM, HBM, HBM, VMEM_FULL],
        out_shape=[pltpu.SemaphoreType.DMA((N_DEV - 1,)), pltpu.SemaphoreType.DMA((N_DEV - 1,)),
                   pltpu.HBM((r, d), F32), pltpu.HBM((N_DEV, r, d), F32), jax.ShapeDtypeStruct((8, HEAD_DIM), F32)],
        input_output_aliases={0: 2, 1: 3},
        compiler_params=pltpu.CompilerParams(has_side_effects=EFFECT),
    )(_in_hbm(block), _in_hbm(lax.empty((N_DEV, r, d), F32)))
    return outs[0], outs[1], outs[2], outs[3], outs[4]


def _rows_wait(tag, send_sems, recv_sems, block, land, after):
    r, d = block.shape

    def body(src, land_, send_sems_, recv_sems_, after_ref, src_thru, land_thru):
        x, y, c = _me()
        for k, peer in enumerate(_peers(x, y, c)):
            cp = _rcopy(src, land_.at[_dev_index(*peer)], send_sems_, recv_sems_, k, peer)
            cp.wait_send()
            cp.wait_recv()

    outs = pl.pallas_call(
        body, name="rows_wait_" + tag,
        in_specs=[HBM, HBM, SEM, SEM, ANY],
        out_specs=[HBM, HBM],
        out_shape=[pltpu.HBM((r, d), F32), pltpu.HBM((N_DEV, r, d), F32)],
        input_output_aliases={0: 0, 1: 1},
        compiler_params=pltpu.CompilerParams(has_side_effects=EFFECT),
    )(block, land, send_sems, recv_sems, after)
    me = _dev_index(*_me())
    return lax.dynamic_update_slice(outs[1], outs[0][None], (me, 0, 0))


ADA_ROWS = 80
ADA_W = 6 * D_MODEL // N_SHARD


def _ada_forward(c_block, cctx_block, w_ada, b_shard):
    d = c_block.shape[1]

    def body(c_ref, cc_ref, w_ref, b_ref, act_ref, mods_ref, raw, mloc, send_sems, recv_sems):
        x, y, c = _me()
        me = _dev_index(x, y, c)
        s_me = _shard_of((x, y))
        raw[72:ADA_ROWS, :] = jnp.zeros((ADA_ROWS - 72, d), F32)
        raw[pl.ds(pl.multiple_of(me * 8, 8), 8), :] = c_ref[...]
        raw[64:72, :] = cc_ref[...]
        sends = []
        for mask in range(1, N_DEV):
            peer = (x ^ (mask >> 2), y ^ ((mask >> 1) & 1), c ^ (mask & 1))
            cp = _rcopy(c_ref, raw.at[pl.ds(pl.multiple_of(me * 8, 8), 8), :], send_sems, recv_sems, mask - 1, peer)
            cp.start()
            sends.append(cp)
        for mask in range(1, N_DEV):
            peer = (x ^ (mask >> 2), y ^ ((mask >> 1) & 1), c ^ (mask & 1))
            landed = raw.at[pl.ds(pl.multiple_of(_dev_index(*peer) * 8, 8), 8), :]
            _rcopy(landed, landed, send_sems, recv_sems, mask - 1, peer).wait_recv()
        v = raw[...]
        act = v * jax.nn.sigmoid(v)
        act_ref[...] = act
        mloc[...] = lax.dot_general(act.astype(BF16), w_ref[...].astype(BF16), NN,
                                    preferred_element_type=F32) + b_ref[...]
        mods_ref[s_me, 0:8, :] = mloc[pl.ds(pl.multiple_of(me * 8, 8), 8), :]
        mods_ref[s_me, 8:16, :] = mloc[64:72, :]
        base = N_DEV - 1
        for j, chip in enumerate(_other_chips(x, y)):
            peer = (*chip, c)
            rows = mloc.at[pl.ds(pl.multiple_of(_dev_index(*peer) * 8, 8), 8), :]
            cp = _rcopy(rows, mods_ref.at[s_me, 0:8, :], send_sems, recv_sems, base + 2 * j, peer)
            cp.start()
            sends.append(cp)
            cp = _rcopy(mloc.at[64:72, :], mods_ref.at[s_me, 8:16, :], send_sems, recv_sems, base + 2 * j + 1, peer)
            cp.start()
            sends.append(cp)
        for j, chip in enumerate(_other_chips(x, y)):
            for part in range(2):
                landed = mods_ref.at[_shard_of(chip), 8 * part:8 * part + 8, :]
                _rcopy(landed, landed, send_sems, recv_sems, base + 2 * j + part, (*chip, c)).wait_recv()
        for cp in sends:
            cp.wait_send()

    n_sem = N_DEV - 1 + 2 * N_CHIP_PEERS
    return pl.pallas_call(
        body, name="ada_forward",
        in_specs=[VMEM_FULL] * 4, out_specs=[VMEM_FULL, VMEM_FULL],
        out_shape=[jax.ShapeDtypeStruct((ADA_ROWS, d), F32), jax.ShapeDtypeStruct((N_SHARD, 16, ADA_W), F32)],
        scratch_shapes=[pltpu.VMEM((ADA_ROWS, d), F32), pltpu.VMEM((ADA_ROWS, ADA_W), F32),
                        pltpu.SemaphoreType.DMA((n_sem,)), pltpu.SemaphoreType.DMA((n_sem,))],
        compiler_params=pltpu.CompilerParams(vmem_limit_bytes=VMEM_LIMIT),
    )(c_block, cctx_block, w_ada, b_shard)


def _small_reduce(gathered):
    d = gathered.shape[2]

    def body(g_ref, o_ref):
        tot = g_ref[0]
        for i in range(1, N_DEV):
            tot = tot + g_ref[i]
        o_ref[...] = tot
        o_ref[0:2, :] = tot[0:2] + tot[6:8]
        o_ref[12:13, :] = jnp.broadcast_to(jnp.sum(tot[12:13], axis=1, keepdims=True), (1, d))

    return pl.pallas_call(body, name="small_reduce", in_specs=[VMEM_FULL], out_specs=VMEM_FULL,
                          out_shape=jax.ShapeDtypeStruct((16, d), F32))(gathered)


def _cctx_grad(gathered, c_ctx):
    d = gathered.shape[2]

    def body(g_ref, c_ref, o_ref):
        tot = g_ref[0, 0:1, :]
        for chip in range(1, N_SHARD):
            tot = tot + g_ref[2 * chip, 0:1, :]
        v = c_ref[...]
        sig = jax.nn.sigmoid(v)
        o_ref[...] = tot * (sig * (1.0 + v * (1.0 - sig)))

    return pl.pallas_call(body, name="cctx_grad", in_specs=[VMEM_FULL, VMEM_FULL], out_specs=VMEM_FULL,
                          out_shape=jax.ShapeDtypeStruct((1, d), F32))(gathered, c_ctx.reshape(1, d))


def _cast_into_full(w, shard, full, axis, name):
    r, cdim = w.shape
    tr = _fit(r, ROW_TILE)
    nbr = r // tr

    def body(s_ref, w_ref, o_ref):
        o_ref[...] = w_ref[...].astype(BF16)

    if axis == 0:
        out_spec = pl.BlockSpec((tr, cdim), lambda i, s: (s[0] * nbr + i, 0))
    else:
        out_spec = pl.BlockSpec((tr, cdim), lambda i, s: (i, s[0]))
    return pl.pallas_call(
        body, name=name,
        grid_spec=pltpu.PrefetchScalarGridSpec(
            num_scalar_prefetch=1, grid=(nbr,), in_specs=[pl.BlockSpec((tr, cdim), lambda i, s: (i, 0))],
            out_specs=out_spec),
        out_shape=jax.ShapeDtypeStruct(full, BF16), compiler_params=_params(("parallel",)),
    )(shard.reshape(1).astype(jnp.int32), w)


def _adamw_halves(w, g_own, g_other, m, v, core, axis, name):
    r, cdim = w.shape
    hr, hc = (r // 2, cdim) if axis == 1 else (r, cdim // 2)
    assert g_own.shape == (hr, hc) and g_other.shape == (hr, hc)
    tr = _fit(hr, 256)
    nb = hr // tr
    c1 = 1.0 - ADAM_B1 ** ADAM_STEP
    c2 = 1.0 - ADAM_B2 ** ADAM_STEP

    def body(c_ref, w_ref, go_ref, gt_ref, m_ref, v_ref, g_ref, d_ref, nm_ref, nv_ref):
        gv = jnp.where(pl.program_id(0) == c_ref[0], go_ref[...], gt_ref[...])
        nm = ADAM_B1 * m_ref[...] + (1.0 - ADAM_B1) * gv
        nv = ADAM_B2 * v_ref[...] + (1.0 - ADAM_B2) * (gv * gv)
        g_ref[...] = gv
        nm_ref[...] = nm
        nv_ref[...] = nv
        d_ref[...] = -ADAM_LR * ((nm / c1) / (jnp.sqrt(nv / c2) + ADAM_EPS) + ADAM_WD * w_ref[...])

    if axis == 1:
        big = pl.BlockSpec((tr, hc), lambda p, i, c: (p * nb + i, 0))
    else:
        big = pl.BlockSpec((tr, hc), lambda p, i, c: (i, p))
    own = pl.BlockSpec((tr, hc), lambda p, i, c: (jnp.where(p == c[0], i, 0), 0))
    other = pl.BlockSpec((tr, hc), lambda p, i, c: (jnp.where(p == c[0], 0, i), 0))
    sh = jax.ShapeDtypeStruct((r, cdim), F32)
    return pl.pallas_call(
        body, name=name,
        grid_spec=pltpu.PrefetchScalarGridSpec(
            num_scalar_prefetch=1, grid=(2, nb), in_specs=[big, own, other, big, big], out_specs=[big] * 4),
        out_shape=[sh] * 4, compiler_params=_params(("parallel", "parallel")),
    )(core.reshape(1).astype(jnp.int32), w, g_own, g_other, m, v)


def _adamw(w, g, m, v, name):
    r, cdim = w.shape
    tr = _fit(r, 128) if r % (ROW_TILE // 4) == 0 else r
    c1 = 1.0 - ADAM_B1 ** ADAM_STEP
    c2 = 1.0 - ADAM_B2 ** ADAM_STEP

    def body(w_ref, g_ref, m_ref, v_ref, d_ref, nm_ref, nv_ref):
        gv = g_ref[...]
        nm = ADAM_B1 * m_ref[...] + (1.0 - ADAM_B1) * gv
        nv = ADAM_B2 * v_ref[...] + (1.0 - ADAM_B2) * (gv * gv)
        nm_ref[...] = nm
        nv_ref[...] = nv
        d_ref[...] = -ADAM_LR * ((nm / c1) / (jnp.sqrt(nv / c2) + ADAM_EPS) + ADAM_WD * w_ref[...])

    spec = pl.BlockSpec((tr, cdim), lambda i: (i, 0))
    sh = jax.ShapeDtypeStruct((r, cdim), F32)
    return pl.pallas_call(body, name=name, grid=(r // tr,), in_specs=[spec] * 4, out_specs=[spec] * 3,
                          out_shape=[sh, sh, sh], compiler_params=_params(("parallel",)))(w, g, m, v)


def _adamw_small(ws, gs, ms, vs):
    k = len(ws)
    c1 = 1.0 - ADAM_B1 ** ADAM_STEP
    c2 = 1.0 - ADAM_B2 ** ADAM_STEP

    def body(*refs):
        w_refs, g_refs, m_refs, v_refs = refs[0:k], refs[k:2 * k], refs[2 * k:3 * k], refs[3 * k:4 * k]
        d_refs, nm_refs, nv_refs = refs[4 * k:5 * k], refs[5 * k:6 * k], refs[6 * k:7 * k]
        for i in range(k):
            gv = g_refs[i][...]
            nm = ADAM_B1 * m_refs[i][...] + (1.0 - ADAM_B1) * gv
            nv = ADAM_B2 * v_refs[i][...] + (1.0 - ADAM_B2) * (gv * gv)
            nm_refs[i][...] = nm
            nv_refs[i][...] = nv
            d_refs[i][...] = -ADAM_LR * ((nm / c1) / (jnp.sqrt(nv / c2) + ADAM_EPS) + ADAM_WD * w_refs[i][...])

    shapes = [jax.ShapeDtypeStruct(w.shape, F32) for w in ws]
    outs = pl.pallas_call(body, name="adamw_small", in_specs=[VMEM_FULL] * (4 * k), out_specs=[VMEM_FULL] * (3 * k),
                          out_shape=shapes * 3)(*ws, *gs, *ms, *vs)
    return outs[0:k], outs[k:2 * k], outs[2 * k:3 * k]


SMALL = (("c_ctx", D_MODEL), ("b_ada", 6 * D_MODEL), ("q_norm_g", HEAD_DIM), ("k_norm_g", HEAD_DIM),
         ("sink_logit", HEADS_A), ("ln1_g", D_MODEL), ("ln1_b", D_MODEL), ("ln2_g", D_MODEL), ("ln2_b", D_MODEL))
WEIGHT_ORDER = ("c_ctx", "w_ada", "b_ada", "w_in", "q_norm_g", "k_norm_g", "sink_logit", "w_out", "ln1_g", "ln1_b",
                "w_gate", "w_up", "w_down", "ln2_g", "ln2_b")


def kernel(x, c, ctx, c_ctx, w_ada, b_ada, w_in, q_norm_g, k_norm_g, sink_logit, w_out, ln1_g, ln1_b, w_gate, w_up, w_down, ln2_g, ln2_b, loss_target, m_c_ctx, m_w_ada, m_b_ada, m_w_in, m_q_norm_g, m_k_norm_g, m_sink_logit, m_w_out, m_ln1_g, m_ln1_b, m_w_gate, m_w_up, m_w_down, m_ln2_g, m_ln2_b, v_c_ctx, v_w_ada, v_b_ada, v_w_in, v_q_norm_g, v_k_norm_g, v_sink_logit, v_w_out, v_ln1_g, v_ln1_b, v_w_gate, v_w_up, v_w_down, v_ln2_g, v_ln2_b):
    d = D_MODEL
    w = dict(c_ctx=c_ctx, w_ada=w_ada[0], b_ada=b_ada, w_in=w_in[0], q_norm_g=q_norm_g, k_norm_g=k_norm_g,
             sink_logit=sink_logit, w_out=w_out[0], ln1_g=ln1_g, ln1_b=ln1_b, w_gate=w_gate[0], w_up=w_up[0],
             w_down=w_down[0], ln2_g=ln2_g, ln2_b=ln2_b)
    m = dict(c_ctx=m_c_ctx, w_ada=m_w_ada[0], b_ada=m_b_ada, w_in=m_w_in[0], q_norm_g=m_q_norm_g, k_norm_g=m_k_norm_g,
             sink_logit=m_sink_logit, w_out=m_w_out[0], ln1_g=m_ln1_g, ln1_b=m_ln1_b, w_gate=m_w_gate[0],
             w_up=m_w_up[0], w_down=m_w_down[0], ln2_g=m_ln2_g, ln2_b=m_ln2_b)
    v = dict(c_ctx=v_c_ctx, w_ada=v_w_ada[0], b_ada=v_b_ada, w_in=v_w_in[0], q_norm_g=v_q_norm_g, k_norm_g=v_k_norm_g,
             sink_logit=v_sink_logit, w_out=v_w_out[0], ln1_g=v_ln1_g, ln1_b=v_ln1_b, w_gate=v_w_gate[0],
             w_up=v_w_up[0], w_down=v_w_down[0], ln2_g=v_ln2_g, ln2_b=v_ln2_b)
    mx, my, mc = _me()
    s_me = _shard_of((mx, my))
    me = _dev_index(mx, my, mc)
    pad8 = lambda row: jnp.concatenate([row.reshape(1, -1), jnp.zeros((7, row.size), F32)], axis=0)

    b_shard = lax.dynamic_slice(b_ada, (0, s_me * ADA_W), (1, ADA_W))
    act, mods4 = _ada_forward(pad8(c), pad8(c_ctx), w["w_ada"], b_shard)

    gathers = []
    prev, shard = mods4, s_me
    for k, names in enumerate(W_GROUPS):
        arrs = tuple(BIG_INDEX[name] for name in names)
        bufs = [_cast_into_full(w[name], shard, BIG[a][1], BIG[a][2], "cast_" + name) for name, a in zip(names, arrs)]
        send_sems, recv_sems, thru, prev = _gather_start("g%d" % k, arrs, bufs, prev)
        shard = s_me + prev[0, 0].astype(jnp.int32)
        gathers.append((arrs, send_sems, recv_sems, thru))

    forwards = {}

    def prefetch(k, after):
        arrs, send_sems, recv_sems, thru = gathers[k]
        landed = _gather_wait("g%d" % k, arrs, send_sems, recv_sems, thru, after)
        fwd_send, fwd_recv, landed, token = _forward_start("g%d" % k, arrs, landed)
        forwards[k] = (fwd_send, fwd_recv, landed)
        return token[0, 0]

    def weights(k, after):
        arrs, send_sems, recv_sems, thru = gathers[k]
        if k in forwards:
            return _forward_wait("g%d" % k, arrs, *forwards[k], after)
        landed = _gather_wait("g%d" % k, arrs, send_sems, recv_sems, thru, after)
        return _gather_forward("g%d" % k, arrs, landed)

    mod = jnp.transpose(mods4[:, 0:1, :], (1, 0, 2)).reshape(1, 6 * d) + prev[0, 0]
    mod_ctx = jnp.transpose(mods4[:, 8:9, :], (1, 0, 2)).reshape(1, 6 * d)

    scatters = {}

    def grads_out(k, dws):
        arrs = tuple(BIG_INDEX[name] for name in G_GROUPS[k])
        scatters[k], zero = _scatter_begin("g%d" % k, arrs, dws)
        return zero

    grad_x, partial = _layer_fwd_bwd(x[0], ctx[0], loss_target[0], mod, mod_ctx, weights, prefetch, grads_out,
                                     q_norm_g, k_norm_g, sink_logit, ln1_g, ln1_b, ln2_g, ln2_b)
    grads, delta, new_m, new_v = {}, {}, {}, {}

    p_send, p_recv, partial, p_land, after = _rows_start("partials", partial)
    joins = []
    for k, names in enumerate(G_GROUPS):
        arrs = tuple(BIG_INDEX[name] for name in names)
        state, after = _scatter_reduce("g%d" % k, arrs, scatters[k], after)
        joins.append(state)
    for k, names in enumerate(G_GROUPS):
        g_own, g_other = _scatter_end("g%d" % k, joins[k], after)
        for name, own, other in zip(names, g_own, g_other):
            grads[name], delta[name], new_m[name], new_v[name] = _adamw_halves(
                w[name], own, other, m[name], v[name], mc, BIG[BIG_INDEX[name]][2], "adamw_" + name)
            after = new_v[name]

    gathered = _rows_wait("partials", p_send, p_recv, partial, p_land, after)
    tot = _small_reduce(gathered)
    grads["b_ada"] = tot[0:6].reshape(1, 6 * d)
    grads["ln1_g"], grads["ln1_b"], grads["ln2_g"], grads["ln2_b"] = tot[8:9], tot[9:10], tot[10:11], tot[11:12]
    grads["q_norm_g"] = tot[13:14, 0:HEAD_DIM]
    grads["k_norm_g"] = tot[13:14, HEAD_DIM:2 * HEAD_DIM]
    grads["sink_logit"] = tot[13:14, 2 * HEAD_DIM:2 * HEAD_DIM + HEADS_A]
    loss = tot[12, 0]

    dm_all = gathered[:, 0:6, :].reshape(N_DEV, 6 * d)
    dmc_tot = jnp.concatenate([tot[6:8].reshape(1, 2 * d), jnp.zeros((1, 4 * d), F32)], axis=1)
    dm_rows = jnp.concatenate([pad8(dm_all[i]) for i in range(N_DEV)] + [pad8(dmc_tot), jnp.zeros((8, 6 * d), F32)], axis=0)
    dm_shard = lax.dynamic_slice(dm_rows, (0, s_me * ADA_W), (ADA_ROWS, ADA_W))
    dmc_shard = lax.dynamic_slice(pad8(dmc_tot), (0, s_me * ADA_W), (8, ADA_W))
    cc_part = _matmul(dmc_shard, w["w_ada"], name="d_cctx", tb=True, tm=8, tn=1024, tk=1536, out_dtype=F32)
    c_send, c_recv, cc_part, c_land, c_token = _rows_start("cctx", cc_part)
    grads["w_ada"] = _matmul(act, dm_shard, name="dw_ada", ta=True, tm=1024, tn=1024, tk=ADA_ROWS, out_dtype=F32,
                             after=c_token)
    delta["w_ada"], new_m["w_ada"], new_v["w_ada"] = _adamw(w["w_ada"], grads["w_ada"], m["w_ada"], v["w_ada"],
                                                            "adamw_w_ada")
    gathered_cc = _rows_wait("cctx", c_send, c_recv, cc_part, c_land, new_v["w_ada"])
    grads["c_ctx"] = _cctx_grad(gathered_cc, c_ctx).reshape(d)
    rows = lambda t: [t[name].reshape(1, size) for name, size in SMALL]
    small = _adamw_small(rows(w), rows(grads), rows(m), rows(v))
    for k, (name, size) in enumerate(SMALL):
        delta[name], new_m[name], new_v[name] = [t[k].reshape(w[name].shape) for t in small]
        grads[name] = grads[name].reshape(w[name].shape)

    lead = lambda name, t: t[None] if name in ("w_ada", "w_in", "w_out", "w_gate", "w_up", "w_down") else t
    outs = [loss, grad_x[None]]
    for group in (grads, delta, new_m, new_v):
        outs += [lead(name, group[name]) for name in WEIGHT_ORDER]
    return tuple(outs)
```

```python
import functools
import math

import jax
import jax.numpy as jnp
from jax import lax
from jax.experimental import pallas as pl
from jax.experimental.pallas import tpu as pltpu

F32 = jnp.float32
BF16 = jnp.bfloat16
MESH = pl.DeviceIdType.MESH

D_MODEL = 2048
HEAD_DIM = 128
HEADS_A = 8
HEADS_B = 8
KV_A = 2
KV_B = 2
GROUP = 4
GRID_W = 64
WINDOW = 128
BLOCK = 128
FFN = 5632
IN_WIDTH = 3072
MIX_WIDTH = 2048
ROPE_THETA = 10000.0
EPS = 1e-6
ATTN_SCALE = HEAD_DIM ** -0.5
DN_ALPHA = 2.0 ** 0.25
N_SHARD = 4
N_DEV = 8

ADAM_LR = 0.001
ADAM_B1 = 0.9
ADAM_B2 = 0.999
ADAM_EPS = 1e-08
ADAM_WD = 0.01
ADAM_STEP = 10

QA0, KA0, VA0, QB0, KB0, VB0 = 0, 1024, 1280, 1536, 2560, 2816

VMEM_LIMIT = 56 * 1024 * 1024
ROW_TILE = 256
NN = (((1,), (0,)), ((), ()))
NT = (((1,), (1,)), ((), ()))
TN = (((0,), (0,)), ((), ()))


def _fit(total, pref):
    step = ROW_TILE // 4
    best = step
    for cand in range(step, pref + 1, step):
        if total % cand == 0:
            best = cand
    return best


def _params(sem=None):
    return pltpu.CompilerParams(dimension_semantics=sem, vmem_limit_bytes=VMEM_LIMIT)


def _matmul(a, b, *, name, ta=False, tb=False, tm, tn, tk, out_dtype, after=None):
    m = a.shape[1] if ta else a.shape[0]
    k = a.shape[0] if ta else a.shape[1]
    n = b.shape[0] if tb else b.shape[1]
    assert (b.shape[1] if tb else b.shape[0]) == k
    tm, tn, tk = min(tm, m), min(tn, n), min(tk, k)
    assert m % tm == 0 and n % tn == 0 and k % tk == 0, (name, m, n, k, tm, tn, tk)
    nk = k // tk
    dn = (((0 if ta else 1,), (1 if tb else 0,)), ((), ()))

    def product(a_ref, b_ref):
        return lax.dot_general(a_ref[...].astype(BF16), b_ref[...].astype(BF16), dn, preferred_element_type=F32)

    def body_whole_k(a_ref, b_ref, *rest):
        o_ref = rest[-1]
        o_ref[...] = product(a_ref, b_ref).astype(o_ref.dtype)

    def body(a_ref, b_ref, *rest):
        o_ref, acc_ref = rest[-2:]
        kk = pl.program_id(2)
        part = product(a_ref, b_ref)

        @pl.when(kk == 0)
        def _():
            acc_ref[...] = part

        @pl.when(kk != 0)
        def _():
            acc_ref[...] += part

        @pl.when(kk == nk - 1)
        def _():
            o_ref[...] = acc_ref[...].astype(o_ref.dtype)

    a_spec = (pl.BlockSpec((tk, tm), lambda i, j, kk: (kk, i)) if ta
              else pl.BlockSpec((tm, tk), lambda i, j, kk: (i, kk)))
    b_spec = (pl.BlockSpec((tn, tk), lambda i, j, kk: (j, kk)) if tb
              else pl.BlockSpec((tk, tn), lambda i, j, kk: (kk, j)))
    return pl.pallas_call(
        body_whole_k if nk == 1 else body, name=name, grid=(m // tm, n // tn, nk),
        in_specs=[a_spec, b_spec] + ([] if after is None else [pl.BlockSpec(memory_space=pl.ANY)]),
        out_specs=pl.BlockSpec((tm, tn), lambda i, j, kk: (i, j)),
        out_shape=jax.ShapeDtypeStruct((m, n), out_dtype),
        scratch_shapes=[] if nk == 1 else [pltpu.VMEM((tm, tn), F32)],
        compiler_params=_params(("parallel", "parallel", "arbitrary")),
    )(a, b, *([] if after is None else [after]))


def _modulate_rows(x, ctx, mods):
    n, d = x.shape
    c = ctx.shape[0]
    nx = n // ROW_TILE
    assert c == ROW_TILE

    def body(x_ref, ctx_ref, mods_ref, o_ref):
        i = pl.program_id(0)

        @pl.when(i < nx)
        def _():
            o_ref[...] = (x_ref[...] * (1.0 + mods_ref[0:1, :]) + mods_ref[1:2, :]).astype(BF16)

        @pl.when(i >= nx)
        def _():
            o_ref[...] = (ctx_ref[...] * (1.0 + mods_ref[2:3, :]) + mods_ref[3:4, :]).astype(BF16)

    return pl.pallas_call(
        body, name="modulate_rows", grid=(nx + 1,),
        in_specs=[pl.BlockSpec((ROW_TILE, d), lambda i: (jnp.minimum(i, nx - 1), 0)),
                  pl.BlockSpec((ROW_TILE, d), lambda i: (0, 0)),
                  pl.BlockSpec((8, d), lambda i: (0, 0))],
        out_specs=pl.BlockSpec((ROW_TILE, d), lambda i: (i, 0)),
        out_shape=jax.ShapeDtypeStruct((n + c, d), BF16),
        compiler_params=_params(("parallel",)),
    )(x, ctx, mods)


def _rope_tables(n, c):
    rows = n // GRID_W
    row_ids = jnp.repeat(jnp.arange(rows, dtype=F32), GRID_W)
    col_ids = jnp.tile(jnp.arange(GRID_W, dtype=F32), rows)
    axis_dim = HEAD_DIM // 2
    inv_freq = jnp.power(ROPE_THETA, -jnp.arange(0, axis_dim, 2, dtype=F32) / axis_dim)
    ang_r = row_ids[:, None] * inv_freq
    ang_c = col_ids[:, None] * inv_freq
    ang = jnp.concatenate([ang_r, ang_r, ang_c, ang_c], axis=-1)
    cos, sin = jnp.cos(ang), jnp.sin(ang)
    quarter = (jnp.arange(HEAD_DIM) // (HEAD_DIM // 4)) % 2
    sin_a = jnp.where(quarter == 0, -sin, 0.0)
    sin_b = jnp.where(quarter == 1, sin, 0.0)
    pad = lambda t, v: jnp.concatenate([t, jnp.full((c, HEAD_DIM), v, F32)], axis=0)
    return pad(cos, 1.0), pad(sin_a, 0.0), pad(sin_b, 0.0)


def _rope(x, cos, sin_a, sin_b):
    return x * cos + pltpu.roll(x, 96, 1) * sin_a + pltpu.roll(x, 32, 1) * sin_b


def _rope_t(dy, cos, sin_a, sin_b):
    return dy * cos - pltpu.roll(dy, 96, 1) * sin_a - pltpu.roll(dy, 32, 1) * sin_b


def _rms(x):
    r = lax.rsqrt(jnp.mean(x * x, axis=-1, keepdims=True) + EPS)
    return x * r, r


def _qkv_post(h_all, cos, sin_a, sin_b, q_g, k_g):
    t = h_all.shape[0]
    nt = t // ROW_TILE

    def body(h_ref, cos_ref, sa_ref, sb_ref, qg_ref, kg_ref, qa_ref, ka_ref, va_ref, qb_ref, kb_ref, vb_ref):
        cos_, sa, sb = cos_ref[...], sa_ref[...], sb_ref[...]
        sl = lambda off, hh: h_ref[:, off + hh * HEAD_DIM: off + (hh + 1) * HEAD_DIM]
        for hh in range(HEADS_A):
            qa_ref[hh] = (_rope(sl(QA0, hh), cos_, sa, sb) * ATTN_SCALE).astype(BF16)
        for hh in range(KV_A):
            ka_ref[hh] = _rope(sl(KA0, hh), cos_, sa, sb).astype(BF16)
            va_ref[hh] = sl(VA0, hh).astype(BF16)
        for hh in range(HEADS_B):
            xn, _ = _rms(sl(QB0, hh))
            qb_ref[hh] = (_rope(xn * qg_ref[...], cos_, sa, sb) * ATTN_SCALE).astype(BF16)
        for hh in range(KV_B):
            xn, _ = _rms(sl(KB0, hh))
            kb_ref[hh] = _rope(xn * kg_ref[...], cos_, sa, sb).astype(BF16)
            vb_ref[hh] = sl(VB0, hh).astype(BF16)

    tab = pl.BlockSpec((ROW_TILE, HEAD_DIM), lambda i: (i, 0))
    gain = pl.BlockSpec((1, HEAD_DIM), lambda i: (0, 0))
    hs = lambda nh: pl.BlockSpec((nh, ROW_TILE, HEAD_DIM), lambda i: (0, i, 0))
    sh = lambda nh: jax.ShapeDtypeStruct((nh, t, HEAD_DIM), BF16)
    return pl.pallas_call(
        body, name="qkv_post", grid=(nt,),
        in_specs=[pl.BlockSpec((ROW_TILE, IN_WIDTH), lambda i: (i, 0)), tab, tab, tab, gain, gain],
        out_specs=[hs(HEADS_A), hs(KV_A), hs(KV_A), hs(HEADS_B), hs(KV_B), hs(KV_B)],
        out_shape=[sh(HEADS_A), sh(KV_A), sh(KV_A), sh(HEADS_B), sh(KV_B), sh(KV_B)],
        compiler_params=_params(("parallel",)),
    )(h_all, cos, sin_a, sin_b, q_g, k_g)


def _qkv_bwd_post(h_all, cos, sin_a, sin_b, q_g, k_g, dqa, dka, dva, dqb, dkb, dvb, n):
    t = h_all.shape[0]
    nt = t // ROW_TILE
    nx = n // ROW_TILE

    def body(h_ref, cos_ref, sa_ref, sb_ref, qg_ref, kg_ref,
             dqa_ref, dka_ref, dva_ref, dqb_ref, dkb_ref, dvb_ref, dh_ref, gs_ref):
        i = pl.program_id(0)
        cos_, sa, sb = cos_ref[...], sa_ref[...], sb_ref[...]
        latent = (i < nx).astype(F32)
        sl = lambda off, hh: h_ref[:, off + hh * HEAD_DIM: off + (hh + 1) * HEAD_DIM]

        def put(off, hh, val):
            dh_ref[:, off + hh * HEAD_DIM: off + (hh + 1) * HEAD_DIM] = val.astype(BF16)

        def norm_bwd(x, gain, dy):
            xn, r = _rms(x)
            dxh = dy * gain
            dx = r * (dxh - xn * jnp.mean(dxh * xn, axis=-1, keepdims=True))
            return dx, jnp.sum(dy * xn, axis=0, keepdims=True)

        for hh in range(HEADS_A):
            put(QA0, hh, _rope_t(dqa_ref[hh] * (ATTN_SCALE * latent), cos_, sa, sb))
        for hh in range(KV_A):
            put(KA0, hh, _rope_t(dka_ref[hh], cos_, sa, sb))
            put(VA0, hh, dva_ref[hh])
        gq = jnp.zeros((1, HEAD_DIM), F32)
        gk = jnp.zeros((1, HEAD_DIM), F32)
        for hh in range(HEADS_B):
            dq_t = dqb_ref[hh // GROUP, :, (hh % GROUP) * ROW_TILE:(hh % GROUP + 1) * ROW_TILE]
            dy = _rope_t(dq_t.T * (ATTN_SCALE * latent), cos_, sa, sb)
            dx, g = norm_bwd(sl(QB0, hh), qg_ref[...], dy)
            put(QB0, hh, dx)
            gq = gq + g
        for hh in range(KV_B):
            dy = _rope_t(dkb_ref[hh], cos_, sa, sb)
            dx, g = norm_bwd(sl(KB0, hh), kg_ref[...], dy)
            put(KB0, hh, dx)
            gk = gk + g
            put(VB0, hh, dvb_ref[hh])
        upd = jnp.concatenate([gq, gk, jnp.zeros((6, HEAD_DIM), F32)], axis=0)

        @pl.when(i == 0)
        def _():
            gs_ref[...] = upd

        @pl.when(i != 0)
        def _():
            gs_ref[...] += upd

    tab = pl.BlockSpec((ROW_TILE, HEAD_DIM), lambda i: (i, 0))
    gain = pl.BlockSpec((1, HEAD_DIM), lambda i: (0, 0))
    lat = lambda nh: pl.BlockSpec((nh, ROW_TILE, HEAD_DIM), lambda i: (0, jnp.minimum(i, nx - 1), 0))
    full = lambda nh: pl.BlockSpec((nh, ROW_TILE, HEAD_DIM), lambda i: (0, i, 0))
    return pl.pallas_call(
        body, name="qkv_bwd_post", grid=(nt,),
        in_specs=[pl.BlockSpec((ROW_TILE, IN_WIDTH), lambda i: (i, 0)), tab, tab, tab, gain, gain,
                  lat(HEADS_A), full(KV_A), full(KV_A),
                  pl.BlockSpec((KV_B, None, HEAD_DIM, GROUP * ROW_TILE), lambda i: (0, jnp.minimum(i, nx - 1), 0, 0)),
                  full(KV_B), full(KV_B)],
        out_specs=[pl.BlockSpec((ROW_TILE, IN_WIDTH), lambda i: (i, 0)),
                   pl.BlockSpec((8, HEAD_DIM), lambda i: (0, 0))],
        out_shape=[jax.ShapeDtypeStruct((t, IN_WIDTH), BF16), jax.ShapeDtypeStruct((8, HEAD_DIM), F32)],
        compiler_params=_params(("arbitrary",)),
    )(h_all, cos, sin_a, sin_b, q_g, k_g, dqa, dka, dva, dqb, dkb, dvb)


GB_TQ = 256
GB_TK = 256


def _heads_rows(ref2d, tq):
    return jnp.concatenate([ref2d[:, hh * HEAD_DIM:(hh + 1) * HEAD_DIM] for hh in range(GROUP)], axis=0)


def _attn_b_fwd(qb, kb, vb, n):
    t = kb.shape[1]
    nk = t // GB_TK
    tq = GB_TQ
    nq = n // tq
    qb_step = 8 if nq % 8 == 0 else (4 if nq % 4 == 0 else 1)
    rows = qb_step * GROUP * tq

    def body(q_ref, k_ref, v_ref, o_ref, lse_ref, m_s, l_s, acc_s):
        blocks = [(s, hh) for s in range(qb_step) for hh in range(GROUP)]
        q = jnp.concatenate([q_ref[hh, s * tq:(s + 1) * tq, :] for s, hh in blocks], axis=0)
        m_s[...] = jnp.full((1, rows), -jnp.inf, F32)
        l_s[...] = jnp.zeros((1, rows), F32)
        acc_s[...] = jnp.zeros((HEAD_DIM, rows), F32)

        def scores(j):
            start = pl.multiple_of(j * GB_TK, GB_TK)
            return lax.dot_general(k_ref[pl.ds(start, GB_TK), :], q, NT, preferred_element_type=F32)

        def step(j, carry):
            st = scores(j)
            vs = v_ref[pl.ds(pl.multiple_of(j * GB_TK, GB_TK), GB_TK), :]
            m_prev = m_s[...]
            m_new = jnp.maximum(m_prev, jnp.max(st, axis=0, keepdims=True))
            p = jnp.exp(st - m_new)
            alpha = jnp.exp(m_prev - m_new)
            l_s[...] = alpha * l_s[...] + jnp.sum(p, axis=0, keepdims=True)
            acc_s[...] = alpha * acc_s[...] + lax.dot_general(vs, p.astype(BF16), TN, preferred_element_type=F32)
            m_s[...] = m_new
            return carry

        lax.fori_loop(0, nk, step, 0, unroll=4)
        ot = acc_s[...] * (1.0 / l_s[...])
        lse = m_s[...] + jnp.log(l_s[...])
        width = GROUP * tq
        for s in range(qb_step):
            lse_ref[s] = lse[:, s * width:(s + 1) * width]
        for k, (s, hh) in enumerate(blocks):
            o_ref[s * tq:(s + 1) * tq, hh * HEAD_DIM:(hh + 1) * HEAD_DIM] = ot[:, k * tq:(k + 1) * tq].T.astype(BF16)

    return pl.pallas_call(
        body, name="attn_b_fwd", grid=(KV_B, nq // qb_step),
        in_specs=[pl.BlockSpec((GROUP, qb_step * tq, HEAD_DIM), lambda g, i: (g, i, 0)),
                  pl.BlockSpec((None, t, HEAD_DIM), lambda g, i: (g, 0, 0)),
                  pl.BlockSpec((None, t, HEAD_DIM), lambda g, i: (g, 0, 0))],
        out_specs=[pl.BlockSpec((qb_step * tq, GROUP * HEAD_DIM), lambda g, i: (i, KV_A + g)),
                   pl.BlockSpec((None, qb_step, 1, GROUP * tq), lambda g, i: (g, i, 0, 0))],
        out_shape=[jax.ShapeDtypeStruct((n, MIX_WIDTH), BF16),
                   jax.ShapeDtypeStruct((KV_B, nq, 1, GROUP * tq), F32)],
        scratch_shapes=[pltpu.VMEM((1, rows), F32), pltpu.VMEM((1, rows), F32), pltpu.VMEM((HEAD_DIM, rows), F32)],
        compiler_params=_params(("parallel", "parallel")),
    )(qb, kb, vb)


def _attn_b_bwd(qb, kb, vb, dheads, lse, delta, n):
    t = kb.shape[1]
    nk = t // GB_TK
    tq = GB_TQ
    nq = n // tq
    rows = GROUP * tq

    qb_step = 8 if nq % 8 == 0 else (4 if nq % 4 == 0 else 1)

    def body(q_ref, k_ref, v_ref, do_ref, lse_ref, dl_ref, dq_ref, dk_ref, dv_ref):
        j = pl.program_id(1)
        i = pl.program_id(2)

        @pl.when(i == 0)
        def _():
            dk_ref[...] = jnp.zeros_like(dk_ref)
            dv_ref[...] = jnp.zeros_like(dv_ref)

        @pl.when(j == 0)
        def _():
            for s in range(qb_step):
                dq_ref[qb_step * i + s] = jnp.zeros((HEAD_DIM, rows), F32)

        blocks = [(s, hh) for s in range(qb_step) for hh in range(GROUP)]
        q = jnp.concatenate([q_ref[hh, s * tq:(s + 1) * tq, :] for s, hh in blocks], axis=0)
        do = jnp.concatenate([do_ref[s * tq:(s + 1) * tq, hh * HEAD_DIM:(hh + 1) * HEAD_DIM] for s, hh in blocks], axis=0)
        lse_row = jnp.concatenate([lse_ref[s] for s in range(qb_step)], axis=1)
        dl_row = jnp.concatenate([dl_ref[s] for s in range(qb_step)], axis=1)
        ks, vs = k_ref[...], v_ref[...]
        st = lax.dot_general(ks, q, NT, preferred_element_type=F32)
        p = jnp.exp(st - lse_row)
        dpt = lax.dot_general(vs, do, NT, preferred_element_type=F32)
        ds = (p * (dpt - dl_row)).astype(BF16)
        dv_part = lax.dot_general(p.astype(BF16), do, NN, preferred_element_type=F32)
        dk_part = lax.dot_general(ds, q, NN, preferred_element_type=F32)
        dq_part = lax.dot_general(ks, ds, TN, preferred_element_type=F32)

        dk_ref[...] += dk_part
        dv_ref[...] += dv_part
        for s in range(qb_step):
            dq_ref[qb_step * i + s] += dq_part[:, s * rows:(s + 1) * rows]

    kv = pl.BlockSpec((None, GB_TK, HEAD_DIM), lambda g, j, i: (g, j, 0))
    row = pl.BlockSpec((None, qb_step, 1, rows), lambda g, j, i: (g, i, 0, 0))
    return pl.pallas_call(
        body, name="attn_b_bwd", grid=(KV_B, nk, nq // qb_step),
        in_specs=[pl.BlockSpec((GROUP, qb_step * tq, HEAD_DIM), lambda g, j, i: (g, i, 0)), kv, kv,
                  pl.BlockSpec((qb_step * tq, GROUP * HEAD_DIM), lambda g, j, i: (i, KV_A + g)), row, row],
        out_specs=[pl.BlockSpec((None, nq, HEAD_DIM, rows), lambda g, j, i: (g, 0, 0, 0)), kv, kv],
        out_shape=[jax.ShapeDtypeStruct((KV_B, nq, HEAD_DIM, rows), F32),
                   jax.ShapeDtypeStruct((KV_B, t, HEAD_DIM), F32),
                   jax.ShapeDtypeStruct((KV_B, t, HEAD_DIM), F32)],
        compiler_params=_params(("parallel", "arbitrary", "arbitrary")),
    )(qb, kb, vb, dheads, lse, delta)


def _delta_rows(dheads, heads):
    n = heads.shape[0]
    tq = GB_TQ
    w = GROUP * HEAD_DIM

    def body(a_ref, b_ref, o_ref):
        prod = a_ref[...].astype(F32) * b_ref[...].astype(F32)
        cols = [jnp.sum(prod[:, hh * HEAD_DIM:(hh + 1) * HEAD_DIM].T, axis=0, keepdims=True) for hh in range(GROUP)]
        o_ref[...] = jnp.concatenate(cols, axis=1)

    blk = pl.BlockSpec((tq, w), lambda g, i: (i, KV_A + g))
    return pl.pallas_call(
        body, name="delta_rows", grid=(KV_B, n // tq),
        in_specs=[blk, blk],
        out_specs=pl.BlockSpec((None, None, 1, GROUP * tq), lambda g, i: (g, i, 0, 0)),
        out_shape=jax.ShapeDtypeStruct((KV_B, n // tq, 1, GROUP * tq), F32),
        compiler_params=_params(("parallel", "parallel")),
    )(dheads, heads)


KWIN = 3 * BLOCK


def _window_scores(q, k_ref, j, n, nb, sink_row):
    c = k_ref.shape[0] - n
    start = pl.multiple_of(jnp.clip(j - 1, 0, nb - 3) * BLOCK, BLOCK)
    kw = k_ref[pl.ds(start, KWIN), :]
    kc = k_ref[pl.ds(n, c), :]
    s_loc = lax.dot_general(kw, q, NT, preferred_element_type=F32)
    s_ctx = lax.dot_general(kc, q, NT, preferred_element_type=F32)
    cols = GROUP * BLOCK
    qpos = j * BLOCK + lax.broadcasted_iota(jnp.int32, (KWIN, cols), 1) % BLOCK
    kpos = start + lax.broadcasted_iota(jnp.int32, (KWIN, cols), 0)
    s_loc = jnp.where(jnp.abs(qpos - kpos) <= WINDOW, s_loc, -jnp.inf)
    m = jnp.maximum(jnp.maximum(jnp.max(s_loc, axis=0, keepdims=True), jnp.max(s_ctx, axis=0, keepdims=True)),
                    sink_row)
    e_loc, e_ctx, e_sink = jnp.exp(s_loc - m), jnp.exp(s_ctx - m), jnp.exp(sink_row - m)
    inv = 1.0 / (jnp.sum(e_loc, axis=0, keepdims=True) + jnp.sum(e_ctx, axis=0, keepdims=True) + e_sink)
    return e_loc * inv, e_ctx * inv, e_sink * inv, start


def _sink_row(sink_ref, g):
    return jnp.concatenate([sink_ref[pl.ds(g * GROUP + hh, 1), :] for hh in range(GROUP)], axis=1)


def _attn_a_fwd(qa, ka, va, sink_b, heads_b, n):
    t = ka.shape[1]
    nb = n // BLOCK
    assert nb >= 3
    wb = 4 if nb % 4 == 0 else 1

    def body(q_ref, k_ref, v_ref, sink_ref, heads_b_ref, o_ref):
        g, jj = pl.program_id(0), pl.program_id(1)
        for s in range(wb):
            j = jj * wb + s
            rows = slice(s * BLOCK, (s + 1) * BLOCK)
            q = q_ref[:, rows, :].reshape(GROUP * BLOCK, HEAD_DIM)
            p_loc, p_ctx, _, start = _window_scores(q, k_ref, j, n, nb, _sink_row(sink_ref, g))
            vw = v_ref[pl.ds(start, KWIN), :]
            vc = v_ref[pl.ds(n, t - n), :]
            ot = (lax.dot_general(vw, p_loc.astype(BF16), TN, preferred_element_type=F32)
                  + lax.dot_general(vc, p_ctx.astype(BF16), TN, preferred_element_type=F32))
            for hh in range(GROUP):
                o_ref[rows, hh * HEAD_DIM:(hh + 1) * HEAD_DIM] = ot[:, hh * BLOCK:(hh + 1) * BLOCK].T.astype(BF16)

    return pl.pallas_call(
        body, name="attn_a_fwd", grid=(KV_A, nb // wb),
        in_specs=[pl.BlockSpec((GROUP, wb * BLOCK, HEAD_DIM), lambda g, j: (g, j, 0)),
                  pl.BlockSpec((None, t, HEAD_DIM), lambda g, j: (g, 0, 0)),
                  pl.BlockSpec((None, t, HEAD_DIM), lambda g, j: (g, 0, 0)),
                  pl.BlockSpec((HEADS_A, HEAD_DIM), lambda g, j: (0, 0)),
                  pl.BlockSpec(memory_space=pl.ANY)],
        out_specs=pl.BlockSpec((wb * BLOCK, GROUP * HEAD_DIM), lambda g, j: (j, g)),
        out_shape=jax.ShapeDtypeStruct((n, MIX_WIDTH), BF16),
        input_output_aliases={4: 0},
        compiler_params=_params(("parallel", "parallel")),
    )(qa, ka, va, sink_b, heads_b)


def _attn_a_bwd(qa, ka, va, sink_b, dheads, n):
    t = ka.shape[1]
    c = t - n
    nb = n // BLOCK
    wb = 4 if nb % 4 == 0 else 1

    def body(q_ref, k_ref, v_ref, sink_ref, do_ref, dq_ref, dk_ref, dv_ref, dsink_ref):
        g, jj = pl.program_id(0), pl.program_id(1)

        @pl.when(jj == 0)
        def _():
            dk_ref[...] = jnp.zeros_like(dk_ref)
            dv_ref[...] = jnp.zeros_like(dv_ref)
            dsink_ref[...] = jnp.zeros_like(dsink_ref)

        for s in range(wb):
            j = jj * wb + s
            rows = slice(s * BLOCK, (s + 1) * BLOCK)
            q = q_ref[:, rows, :].reshape(GROUP * BLOCK, HEAD_DIM)
            do = _heads_rows(do_ref[rows, :], BLOCK)
            p_loc, p_ctx, p_sink, start = _window_scores(q, k_ref, j, n, nb, _sink_row(sink_ref, g))
            kw, vw = k_ref[pl.ds(start, KWIN), :], v_ref[pl.ds(start, KWIN), :]
            kc, vc = k_ref[pl.ds(n, c), :], v_ref[pl.ds(n, c), :]
            dp_loc = lax.dot_general(vw, do, NT, preferred_element_type=F32)
            dp_ctx = lax.dot_general(vc, do, NT, preferred_element_type=F32)
            dl = jnp.sum(p_loc * dp_loc, axis=0, keepdims=True) + jnp.sum(p_ctx * dp_ctx, axis=0, keepdims=True)
            ds_loc = (p_loc * (dp_loc - dl)).astype(BF16)
            ds_ctx = (p_ctx * (dp_ctx - dl)).astype(BF16)
            dqt = (lax.dot_general(kw, ds_loc, TN, preferred_element_type=F32)
                   + lax.dot_general(kc, ds_ctx, TN, preferred_element_type=F32))
            for hh in range(GROUP):
                dq_ref[hh, rows, :] = dqt[:, hh * BLOCK:(hh + 1) * BLOCK].T
            dk_ref[pl.ds(start, KWIN), :] += lax.dot_general(ds_loc, q, NN, preferred_element_type=F32)
            dv_ref[pl.ds(start, KWIN), :] += lax.dot_general(p_loc.astype(BF16), do, NN, preferred_element_type=F32)
            dk_ref[pl.ds(n, c), :] += lax.dot_general(ds_ctx, q, NN, preferred_element_type=F32)
            dv_ref[pl.ds(n, c), :] += lax.dot_general(p_ctx.astype(BF16), do, NN, preferred_element_type=F32)
            dsk = -(p_sink * dl)
            upd = [jnp.broadcast_to(jnp.sum(dsk[:, hh * BLOCK:(hh + 1) * BLOCK], axis=1, keepdims=True), (1, HEAD_DIM))
                   for hh in range(GROUP)]
            dsink_ref[...] += jnp.concatenate(upd + [jnp.zeros((8 - GROUP, HEAD_DIM), F32)], axis=0)

    res = pl.BlockSpec((None, t, HEAD_DIM), lambda g, j: (g, 0, 0))
    return pl.pallas_call(
        body, name="attn_a_bwd", grid=(KV_A, nb // wb),
        in_specs=[pl.BlockSpec((GROUP, wb * BLOCK, HEAD_DIM), lambda g, j: (g, j, 0)), res, res,
                  pl.BlockSpec((HEADS_A, HEAD_DIM), lambda g, j: (0, 0)),
                  pl.BlockSpec((wb * BLOCK, GROUP * HEAD_DIM), lambda g, j: (j, g))],
        out_specs=[pl.BlockSpec((GROUP, wb * BLOCK, HEAD_DIM), lambda g, j: (g, j, 0)), res, res,
                   pl.BlockSpec((None, 8, HEAD_DIM), lambda g, j: (g, 0, 0))],
        out_shape=[jax.ShapeDtypeStruct((HEADS_A, n, HEAD_DIM), F32),
                   jax.ShapeDtypeStruct((KV_A, t, HEAD_DIM), F32),
                   jax.ShapeDtypeStruct((KV_A, t, HEAD_DIM), F32),
                   jax.ShapeDtypeStruct((KV_A, 8, HEAD_DIM), F32)],
        compiler_params=_params(("parallel", "arbitrary")),
    )(qa, ka, va, sink_b, dheads)


def _ln_stats(r):
    mu = jnp.mean(r, axis=-1, keepdims=True)
    cen = r - mu
    rstd = lax.rsqrt(jnp.mean(cen * cen, axis=-1, keepdims=True) + EPS)
    return cen * rstd, rstd


def _ln_bwd(dy, xhat, rstd, gain):
    dxh = dy * gain
    return rstd * (dxh - jnp.mean(dxh, axis=-1, keepdims=True)
                   - xhat * jnp.mean(dxh * xhat, axis=-1, keepdims=True))


def _accumulate_rows(ref, rows, i):
    pad = [jnp.zeros_like(rows[0])] * (8 - len(rows))
    upd = jnp.concatenate(rows + pad, axis=0)

    @pl.when(i == 0)
    def _():
        ref[...] = upd

    @pl.when(i != 0)
    def _():
        ref[...] += upd


def _colsum(v):
    return jnp.sum(v, axis=0, keepdims=True)


LN_TILE = 256


def _res_ln1(x, a, vec):
    n, d = x.shape

    def body(x_ref, a_ref, v_ref, xh_ref, rs_ref, u_ref):
        r1 = DN_ALPHA * x_ref[...] + v_ref[0:1, :] * a_ref[...]
        xhat, rstd = _ln_stats(r1)
        xh_ref[...] = xhat
        rs_ref[...] = rstd
        x1 = xhat * v_ref[1:2, :] + v_ref[2:3, :]
        u_ref[...] = (x1 * (1.0 + v_ref[3:4, :]) + v_ref[4:5, :]).astype(BF16)

    row = pl.BlockSpec((LN_TILE, d), lambda i: (i, 0))
    return pl.pallas_call(
        body, name="res_ln1", grid=(n // LN_TILE,),
        in_specs=[row, row, pl.BlockSpec((8, d), lambda i: (0, 0))],
        out_specs=[row, pl.BlockSpec((LN_TILE, 1), lambda i: (i, 0)), row],
        out_shape=[jax.ShapeDtypeStruct((n, d), F32), jax.ShapeDtypeStruct((n, 1), F32),
                   jax.ShapeDtypeStruct((n, d), BF16)],
        compiler_params=_params(("parallel",)),
    )(x, a, vec)


def _res_ln2_loss(xhat1, f, target, vec):
    n, d = f.shape

    def body(xh_ref, f_ref, t_ref, v_ref, dr_ref, df_ref, s_ref):
        i = pl.program_id(0)
        x1 = xh_ref[...] * v_ref[1:2, :] + v_ref[2:3, :]
        fv = f_ref[...]
        xhat, rstd = _ln_stats(DN_ALPHA * x1 + v_ref[0:1, :] * fv)
        err = xhat * v_ref[3:4, :] + v_ref[4:5, :] - t_ref[...]
        dy = err * (1.0 / d)
        dr2 = _ln_bwd(dy, xhat, rstd, v_ref[3:4, :])
        dr_ref[...] = dr2
        df_ref[...] = (dr2 * v_ref[0:1, :]).astype(BF16)
        _accumulate_rows(s_ref, [_colsum(dy * xhat), _colsum(dy), _colsum(dr2 * fv),
                                 _colsum(err * err) * (0.5 / d)], i)

    row = pl.BlockSpec((LN_TILE, d), lambda i: (i, 0))
    return pl.pallas_call(
        body, name="res_ln2_loss", grid=(n // LN_TILE,),
        in_specs=[row, row, row, pl.BlockSpec((8, d), lambda i: (0, 0))],
        out_specs=[row, row, pl.BlockSpec((8, d), lambda i: (0, 0))],
        out_shape=[jax.ShapeDtypeStruct((n, d), F32), jax.ShapeDtypeStruct((n, d), BF16),
                   jax.ShapeDtypeStruct((8, d), F32)],
        compiler_params=_params(("arbitrary",)),
    )(xhat1, f, target, vec)


def _ln1_bwd(du2, dr2, xhat1, rstd1, a, vec):
    n, d = du2.shape

    def body(du_ref, dr2_ref, xh_ref, rs_ref, a_ref, v_ref, dxp_ref, da_ref, s_ref):
        i = pl.program_id(0)
        du, xhat = du_ref[...], xh_ref[...]
        x1 = xhat * v_ref[1:2, :] + v_ref[2:3, :]
        dx1 = DN_ALPHA * dr2_ref[...] + du * (1.0 + v_ref[0:1, :])
        dr1 = _ln_bwd(dx1, xhat, rs_ref[...], v_ref[1:2, :])
        dxp_ref[...] = DN_ALPHA * dr1
        da_ref[...] = (dr1 * v_ref[3:4, :]).astype(BF16)
        _accumulate_rows(s_ref, [_colsum(du * x1), _colsum(du), _colsum(dx1 * xhat), _colsum(dx1),
                                 _colsum(dr1 * a_ref[...])], i)

    row = pl.BlockSpec((LN_TILE, d), lambda i: (i, 0))
    return pl.pallas_call(
        body, name="ln1_bwd", grid=(n // LN_TILE,),
        in_specs=[row, row, row, pl.BlockSpec((LN_TILE, 1), lambda i: (i, 0)), row,
                  pl.BlockSpec((8, d), lambda i: (0, 0))],
        out_specs=[row, row, pl.BlockSpec((8, d), lambda i: (0, 0))],
        out_shape=[jax.ShapeDtypeStruct((n, d), F32), jax.ShapeDtypeStruct((n, d), BF16),
                   jax.ShapeDtypeStruct((8, d), F32)],
        compiler_params=_params(("arbitrary",)),
    )(du2, dr2, xhat1, rstd1, a, vec)


def _mod1_bwd(du_all, dxp, x, ctx, mods):
    n, d = x.shape
    nx = n // ROW_TILE

    def body(du_ref, dxp_ref, x_ref, ctx_ref, m_ref, gx_ref, s_ref):
        i = pl.program_id(0)
        du = du_ref[...]
        zero = jnp.zeros((1, d), F32)

        @pl.when(i == 0)
        def _():
            s_ref[...] = jnp.zeros_like(s_ref)

        @pl.when(i < nx)
        def _():
            gx_ref[...] = dxp_ref[...] + du * (1.0 + m_ref[0:1, :])
            s_ref[...] += jnp.concatenate([_colsum(du * x_ref[...]), _colsum(du)] + [zero] * 6, axis=0)

        @pl.when(i >= nx)
        def _():
            s_ref[...] += jnp.concatenate([zero, zero, _colsum(du * ctx_ref[...]), _colsum(du)] + [zero] * 4, axis=0)

    lat = pl.BlockSpec((ROW_TILE, d), lambda i: (jnp.minimum(i, nx - 1), 0))
    return pl.pallas_call(
        body, name="mod1_bwd", grid=(nx + 1,),
        in_specs=[pl.BlockSpec((ROW_TILE, d), lambda i: (i, 0)), lat, lat,
                  pl.BlockSpec((ROW_TILE, d), lambda i: (0, 0)), pl.BlockSpec((8, d), lambda i: (0, 0))],
        out_specs=[lat, pl.BlockSpec((8, d), lambda i: (0, 0))],
        out_shape=[jax.ShapeDtypeStruct((n, d), F32), jax.ShapeDtypeStruct((8, d), F32)],
        compiler_params=_params(("arbitrary",)),
    )(du_all, dxp, x, ctx, mods)


FFN_TM = 1024
FFN_TN = 512


def _gate_up(u2, wg, wu, after):
    n, d = u2.shape
    f = wg.shape[1]

    def body(u_ref, wg_ref, wu_ref, after_ref, g_ref, up_ref, h_ref):
        u = u_ref[...]
        g = lax.dot_general(u, wg_ref[...], NN, preferred_element_type=F32)
        up = lax.dot_general(u, wu_ref[...], NN, preferred_element_type=F32)
        g_ref[...] = g.astype(BF16)
        up_ref[...] = up.astype(BF16)
        h_ref[...] = (g * jax.nn.sigmoid(g) * up).astype(BF16)

    tm = min(FFN_TM, n)
    wspec = pl.BlockSpec((d, FFN_TN), lambda j, i: (0, j))
    ospec = pl.BlockSpec((tm, FFN_TN), lambda j, i: (i, j))
    return pl.pallas_call(
        body, name="gate_up", grid=(f // FFN_TN, n // tm),
        in_specs=[pl.BlockSpec((tm, d), lambda j, i: (i, 0)), wspec, wspec, pl.BlockSpec(memory_space=pl.ANY)],
        out_specs=[ospec, ospec, ospec],
        out_shape=[jax.ShapeDtypeStruct((n, f), BF16)] * 3,
        compiler_params=_params(("parallel", "parallel")),
    )(u2, wg, wu, after)


def _glu_bwd(df, wd, g, u):
    n, d = df.shape
    f = wd.shape[0]

    def body(df_ref, wd_ref, g_ref, u_ref, dg_ref, du_ref):
        dh = lax.dot_general(df_ref[...], wd_ref[...], NT, preferred_element_type=F32)
        gv = g_ref[...].astype(F32)
        sig = jax.nn.sigmoid(gv)
        du_ref[...] = (dh * (gv * sig)).astype(BF16)
        dg_ref[...] = (dh * u_ref[...].astype(F32) * (sig * (1.0 + gv * (1.0 - sig)))).astype(BF16)

    tm = min(FFN_TM, n)
    ospec = pl.BlockSpec((tm, FFN_TN), lambda i, j: (i, j))
    return pl.pallas_call(
        body, name="glu_bwd", grid=(n // tm, f // FFN_TN),
        in_specs=[pl.BlockSpec((tm, d), lambda i, j: (i, 0)),
                  pl.BlockSpec((FFN_TN, d), lambda i, j: (j, 0)), ospec, ospec],
        out_specs=[ospec, ospec],
        out_shape=[jax.ShapeDtypeStruct((n, f), BF16), jax.ShapeDtypeStruct((n, f), BF16)],
        compiler_params=_params(("parallel", "parallel")),
    )(df, wd, g, u)


def _du2(dg, du, wg, wu):
    n, f = dg.shape
    d = wg.shape[0]
    tm, tn, tk = min(1024, n), 1024, 1408
    nk = f // tk

    def body(dg_ref, du_ref, wg_ref, wu_ref, o_ref):
        @pl.when(pl.program_id(2) == 0)
        def _():
            o_ref[...] = jnp.zeros_like(o_ref)

        o_ref[...] += (lax.dot_general(dg_ref[...], wg_ref[...], NT, preferred_element_type=F32)
                       + lax.dot_general(du_ref[...], wu_ref[...], NT, preferred_element_type=F32))

    aspec = pl.BlockSpec((tm, tk), lambda i, j, kk: (i, kk))
    wspec = pl.BlockSpec((tn, tk), lambda i, j, kk: (j, kk))
    return pl.pallas_call(
        body, name="du2", grid=(n // tm, d // tn, nk),
        in_specs=[aspec, aspec, wspec, wspec],
        out_specs=pl.BlockSpec((tm, tn), lambda i, j, kk: (i, j)),
        out_shape=jax.ShapeDtypeStruct((n, d), F32),
        compiler_params=_params(("parallel", "parallel", "arbitrary")),
    )(dg, du, wg, wu)


def _rows8(rows, d=D_MODEL):
    rows = [r.reshape(1, d).astype(F32) for r in rows]
    return jnp.concatenate(rows + [jnp.zeros((8 - len(rows), d), F32)], axis=0)


W_GROUPS = (("w_in",), ("w_out", "w_gate", "w_up"), ("w_down",))
G_GROUPS = (("w_down", "w_gate", "w_up"), ("w_out",), ("w_in",))


def _layer_fwd_bwd(x, ctx, target, mod, mod_ctx, weights, prefetch, grads_out,
                   q_g, k_g, sink, ln1_g, ln1_b, ln2_g, ln2_b):
    n, d = x.shape
    c = ctx.shape[0]
    sh1, sc1, g1, sh2, sc2, g2 = [mod[:, k * d:(k + 1) * d] for k in range(6)]
    csh1, csc1 = mod_ctx[:, 0:d], mod_ctx[:, d:2 * d]
    cos, sin_a, sin_b = _rope_tables(n, c)
    sink_b = jnp.broadcast_to(sink.reshape(HEADS_A, 1), (HEADS_A, HEAD_DIM)).astype(F32)

    u_all = _modulate_rows(x, ctx, _rows8([sc1, sh1, csc1, csh1]))
    (w_in,) = weights(0, u_all)
    h_all = _matmul(u_all, w_in, name="qkv_proj", tm=_fit(n + c, 1088), tn=1024, tk=2048, out_dtype=F32)
    qa, ka, va, qb, kb, vb = _qkv_post(h_all, cos, sin_a, sin_b, q_g, k_g)
    heads_b, lse = _attn_b_fwd(qb, kb, vb, n)
    zero = prefetch(1, heads_b)
    heads = _attn_a_fwd(qa, ka, va, sink_b + zero, heads_b, n)
    w_out, w_gate, w_up = weights(1, heads)
    a = _matmul(heads, w_out, name="out_proj", tm=1024, tn=1024, tk=2048, out_dtype=F32)
    xhat1, rstd1, u2 = _res_ln1(x, a, _rows8([g1, ln1_g, ln1_b, sc2, sh2]))
    zero = prefetch(2, u2)
    gg, uu, hh = _gate_up(u2, w_gate, w_up, zero.reshape(1, 1))
    (w_down,) = weights(2, hh)
    f = _matmul(hh, w_down, name="ffn_down", tm=1024, tn=512, tk=FFN, out_dtype=F32)
    dr2, df, s_ln2 = _res_ln2_loss(xhat1, f, target, _rows8([g2, ln1_g, ln1_b, ln2_g, ln2_b]))

    dgg, duu = _glu_bwd(df, w_down, gg, uu)
    dw_down = _matmul(hh, df, name="dw_down", ta=True, tm=512, tn=1024, tk=n, out_dtype=BF16)
    dw_gate = _matmul(u2, dgg, name="dw_gate", ta=True, tm=1024, tn=512, tk=n, out_dtype=BF16)
    dw_up = _matmul(u2, duu, name="dw_up", ta=True, tm=1024, tn=512, tk=n, out_dtype=BF16)
    zero = grads_out(0, [dw_down, dw_gate, dw_up])
    du2 = _du2(dgg, duu, w_gate, w_up)
    dxp, da, s_ln1 = _ln1_bwd(du2, dr2, xhat1, rstd1, a, _rows8([sc2, ln1_g, ln1_b, g1]) + zero)

    dheads = _matmul(da, w_out, name="d_heads", tb=True, tm=1024, tn=1024, tk=2048, out_dtype=BF16)
    dw_out = _matmul(heads, da, name="dw_out", ta=True, tm=1024, tn=1024, tk=n, out_dtype=BF16)
    zero = grads_out(1, [dw_out])
    delta = _delta_rows(dheads, heads)
    dqa, dka, dva, dsink = _attn_a_bwd(qa, ka, va, sink_b + zero, dheads, n)
    dqb, dkb, dvb = _attn_b_bwd(qb, kb, vb, dheads, lse, delta, n)
    dh_all, s_gain = _qkv_bwd_post(h_all, cos, sin_a, sin_b, q_g, k_g, dqa, dka, dva, dqb, dkb, dvb, n)
    dw_in = _matmul(u_all, dh_all, name="dw_in", ta=True, tm=1024, tn=1024, tk=n + c, out_dtype=BF16)
    zero = grads_out(2, [dw_in])
    du_all = _matmul(dh_all, w_in, name="d_u1", tb=True, tm=_fit(n + c, 1088), tn=1024, tk=IN_WIDTH, out_dtype=F32,
                     after=zero.reshape(1, 1))
    grad_x, s_mod1 = _mod1_bwd(du_all, dxp, x, ctx, _rows8([sc1]) + zero)

    dsink_row = jnp.concatenate([dsink[0, 0:GROUP, 0], dsink[1, 0:GROUP, 0]]).reshape(1, HEADS_A)
    misc = jnp.concatenate([s_gain[0:1], s_gain[1:2], dsink_row,
                            jnp.zeros((1, d - 2 * HEAD_DIM - HEADS_A), F32)], axis=1)
    partial = jnp.concatenate([
        s_mod1[1:2], s_mod1[0:1], s_ln1[4:5],
        s_ln1[1:2], s_ln1[0:1], s_ln2[2:3],
        s_mod1[3:4], s_mod1[2:3],
        s_ln1[2:3], s_ln1[3:4], s_ln2[0:1], s_ln2[1:2],
        s_ln2[3:4], misc, jnp.zeros((2, d), F32)], axis=0)
    return grad_x, partial


ANY = pl.BlockSpec(memory_space=pl.ANY)
VMEM_FULL = pl.BlockSpec(memory_space=pltpu.VMEM)
N_CHIP_PEERS = 3


def _me():
    return lax.axis_index("x"), lax.axis_index("y"), lax.axis_index("c")


def _other_chips(x, y):
    return [(1 - x, y), (x, 1 - y), (1 - x, 1 - y)]


def _shard_of(chip):
    return 2 * chip[0] + chip[1]


def _dev_index(x, y, c):
    return 4 * x + 2 * y + c


def _rcopy(src, dst, send_sems, recv_sems, k, dev):
    return pltpu.make_async_remote_copy(src_ref=src, dst_ref=dst, send_sem=send_sems.at[k], recv_sem=recv_sems.at[k],
                                        device_id=dev, device_id_type=MESH)


BIG = (("w_in", (D_MODEL, IN_WIDTH), 1), ("w_out", (MIX_WIDTH, D_MODEL), 0), ("w_gate", (D_MODEL, FFN), 1),
       ("w_up", (D_MODEL, FFN), 1), ("w_down", (FFN, D_MODEL), 0))


def _sub(ref, axis, idx, size):
    start = pl.multiple_of(idx * size, size)
    return ref.at[pl.ds(start, size), :] if axis == 0 else ref.at[:, pl.ds(start, size)]


def _shape_div(shape, axis, parts):
    return tuple(s // parts if a == axis else s for a, s in enumerate(shape))


def _piece(a, ref, shard, half):
    _, full, axis = BIG[a]
    view = _sub(ref, axis, shard, full[axis] // N_SHARD)
    return _sub(view, 1 - axis, half, full[1 - axis] // 2)


HBM = pl.BlockSpec(memory_space=pltpu.HBM)
SEM = pl.BlockSpec(memory_space=pltpu.SEMAPHORE)
EFFECT = pltpu.SideEffectType.DATAFLOW_SIDE_EFFECTING
BIG_INDEX = {name: a for a, (name, _, _) in enumerate(BIG)}


def _in_hbm(arr):
    return pltpu.with_memory_space_constraint(arr, pltpu.HBM)


def _gather_start(tag, arrs, bufs, prev):
    n_arr = len(arrs)

    def body(*refs):
        ins = refs[:n_arr]
        send_sems, recv_sems = refs[n_arr + 1], refs[n_arr + 2]
        token = refs[-1]
        x, y, c = _me()
        s_me = _shard_of((x, y))
        for i, a in enumerate(arrs):
            mine = _piece(a, ins[i], s_me, c)
            for j, chip in enumerate(_other_chips(x, y)):
                _rcopy(mine, mine, send_sems, recv_sems, N_CHIP_PEERS * i + j, (*chip, c)).start()
        token[...] = jnp.zeros_like(token)

    n_sem = N_CHIP_PEERS * n_arr
    outs = pl.pallas_call(
        body, name="gather_start_" + tag,
        in_specs=[HBM] * n_arr + [ANY],
        out_specs=[SEM, SEM] + [HBM] * n_arr + [VMEM_FULL],
        out_shape=[pltpu.SemaphoreType.DMA((n_sem,)), pltpu.SemaphoreType.DMA((n_sem,))]
        + [pltpu.HBM(BIG[a][1], BF16) for a in arrs] + [jax.ShapeDtypeStruct((8, HEAD_DIM), F32)],
        input_output_aliases={i: 2 + i for i in range(n_arr)},
        compiler_params=pltpu.CompilerParams(has_side_effects=EFFECT),
    )(*[_in_hbm(b) for b in bufs], prev)
    return outs[0], outs[1], list(outs[2:2 + n_arr]), outs[-1]


def _gather_wait(tag, arrs, send_sems, recv_sems, bufs, after):
    n_arr = len(arrs)

    def body(*refs):
        ins = refs[:n_arr]
        send_sems_, recv_sems_ = refs[n_arr], refs[n_arr + 1]
        x, y, c = _me()
        s_me = _shard_of((x, y))
        for i, a in enumerate(arrs):
            mine = _piece(a, ins[i], s_me, c)
            for j, chip in enumerate(_other_chips(x, y)):
                landed = _piece(a, ins[i], _shard_of(chip), c)
                cp = _rcopy(mine, landed, send_sems_, recv_sems_, N_CHIP_PEERS * i + j, (*chip, c))
                cp.wait_send()
                cp.wait_recv()

    outs = pl.pallas_call(
        body, name="gather_wait_" + tag,
        in_specs=[HBM] * n_arr + [SEM, SEM, ANY],
        out_specs=[HBM] * n_arr,
        out_shape=[pltpu.HBM(BIG[a][1], BF16) for a in arrs],
        input_output_aliases={i: i for i in range(n_arr)},
        compiler_params=pltpu.CompilerParams(has_side_effects=EFFECT),
    )(*bufs, send_sems, recv_sems, after)
    return list(outs)


def _gather_forward(tag, arrs, bufs):
    n_arr = len(arrs)

    def body(*refs):
        outs = refs[n_arr:2 * n_arr]
        send_sems, recv_sems = refs[2 * n_arr:]
        x, y, c = _me()
        sibling = (x, y, 1 - c)
        chips = _other_chips(x, y)
        copies = []
        for i, a in enumerate(arrs):
            for j, chip in enumerate(chips):
                landed = _piece(a, outs[i], _shard_of(chip), c)
                cp = _rcopy(landed, landed, send_sems, recv_sems, N_CHIP_PEERS * i + j, sibling)
                cp.start()
                copies.append(cp)
        for i, a in enumerate(arrs):
            for j, chip in enumerate(chips):
                other = _piece(a, outs[i], _shard_of(chip), 1 - c)
                _rcopy(other, other, send_sems, recv_sems, N_CHIP_PEERS * i + j, sibling).wait_recv()
        for cp in copies:
            cp.wait_send()

    n_sem = N_CHIP_PEERS * n_arr
    return list(pl.pallas_call(
        body, name="gather_forward_" + tag,
        in_specs=[ANY] * n_arr, out_specs=[ANY] * n_arr,
        out_shape=[jax.ShapeDtypeStruct(BIG[a][1], BF16) for a in arrs],
        input_output_aliases={i: i for i in range(n_arr)},
        scratch_shapes=[pltpu.SemaphoreType.DMA((n_sem,)), pltpu.SemaphoreType.DMA((n_sem,))],
    )(*bufs))


def _forward_start(tag, arrs, bufs):
    n_arr = len(arrs)

    def body(*refs):
        ins = refs[:n_arr]
        send_sems, recv_sems = refs[n_arr], refs[n_arr + 1]
        token = refs[-1]
        x, y, c = _me()
        for i, a in enumerate(arrs):
            for j, chip in enumerate(_other_chips(x, y)):
                landed = _piece(a, ins[i], _shard_of(chip), c)
                _rcopy(landed, landed, send_sems, recv_sems, N_CHIP_PEERS * i + j, (x, y, 1 - c)).start()
        token[...] = jnp.zeros_like(token)

    n_sem = N_CHIP_PEERS * n_arr
    outs = pl.pallas_call(
        body, name="gather_forward_start_" + tag,
        in_specs=[HBM] * n_arr,
        out_specs=[SEM, SEM] + [HBM] * n_arr + [VMEM_FULL],
        out_shape=[pltpu.SemaphoreType.DMA((n_sem,)), pltpu.SemaphoreType.DMA((n_sem,))]
        + [pltpu.HBM(BIG[a][1], BF16) for a in arrs] + [jax.ShapeDtypeStruct((8, HEAD_DIM), F32)],
        input_output_aliases={i: 2 + i for i in range(n_arr)},
        compiler_params=pltpu.CompilerParams(has_side_effects=EFFECT),
    )(*bufs)
    return outs[0], outs[1], list(outs[2:2 + n_arr]), outs[-1]


def _forward_wait(tag, arrs, send_sems, recv_sems, bufs, after):
    n_arr = len(arrs)

    def body(*refs):
        ins = refs[:n_arr]
        send_sems_, recv_sems_ = refs[n_arr], refs[n_arr + 1]
        x, y, c = _me()
        for i, a in enumerate(arrs):
            for j, chip in enumerate(_other_chips(x, y)):
                mine = _piece(a, ins[i], _shard_of(chip), c)
                other = _piece(a, ins[i], _shard_of(chip), 1 - c)
                cp = _rcopy(mine, other, send_sems_, recv_sems_, N_CHIP_PEERS * i + j, (x, y, 1 - c))
                cp.wait_send()
                cp.wait_recv()

    outs = pl.pallas_call(
        body, name="gather_forward_wait_" + tag,
        in_specs=[HBM] * n_arr + [SEM, SEM, ANY],
        out_specs=[HBM] * n_arr,
        out_shape=[pltpu.HBM(BIG[a][1], BF16) for a in arrs],
        input_output_aliases={i: i for i in range(n_arr)},
        compiler_params=pltpu.CompilerParams(has_side_effects=EFFECT),
    )(*bufs, send_sems, recv_sems, after)
    return list(outs)


def _peers(x, y, c):
    return [(x ^ (mask >> 2), y ^ ((mask >> 1) & 1), c ^ (mask & 1)) for mask in range(1, N_DEV)]


def _received_shape(a):
    _, full, axis = BIG[a]
    return (N_DEV - 1,) + _shape_div(_shape_div(full, 1 - axis, 2), axis, N_SHARD)


def _pieces_start(tag, arrs, dws):
    n_arr = len(arrs)

    def body(*refs):
        srcs, lands = refs[:n_arr], refs[n_arr:2 * n_arr]
        send_sems, recv_sems = refs[2 * n_arr], refs[2 * n_arr + 1]
        token = refs[-1]
        x, y, c = _me()
        for i, a in enumerate(arrs):
            for k, peer in enumerate(_peers(x, y, c)):
                src = _piece(a, srcs[i], _shard_of(peer[:2]), peer[2])
                _rcopy(src, lands[i].at[k], send_sems, recv_sems, (N_DEV - 1) * i + k, peer).start()
        token[...] = jnp.zeros_like(token)

    n_sem = (N_DEV - 1) * n_arr
    lands = [_in_hbm(lax.empty(_received_shape(a), BF16)) for a in arrs]
    outs = pl.pallas_call(
        body, name="grad_pieces_start_" + tag,
        in_specs=[HBM] * (2 * n_arr),
        out_specs=[SEM, SEM] + [HBM] * (2 * n_arr) + [VMEM_FULL],
        out_shape=[pltpu.SemaphoreType.DMA((n_sem,)), pltpu.SemaphoreType.DMA((n_sem,))]
        + [pltpu.HBM(BIG[a][1], BF16) for a in arrs] + [pltpu.HBM(_received_shape(a), BF16) for a in arrs]
        + [jax.ShapeDtypeStruct((8, HEAD_DIM), F32)],
        input_output_aliases={i: 2 + i for i in range(2 * n_arr)},
        compiler_params=pltpu.CompilerParams(has_side_effects=EFFECT),
    )(*[_in_hbm(dw) for dw in dws], *lands)
    return outs[0], outs[1], list(outs[2:2 + n_arr]), list(outs[2 + n_arr:2 + 2 * n_arr]), outs[-1]


def _pieces_wait(tag, arrs, send_sems, recv_sems, dws, lands, after):
    n_arr = len(arrs)

    def body(*refs):
        srcs, lands_ = refs[:n_arr], refs[n_arr:2 * n_arr]
        send_sems_, recv_sems_ = refs[2 * n_arr], refs[2 * n_arr + 1]
        x, y, c = _me()
        for i, a in enumerate(arrs):
            for k, peer in enumerate(_peers(x, y, c)):
                src = _piece(a, srcs[i], _shard_of(peer[:2]), peer[2])
                cp = _rcopy(src, lands_[i].at[k], send_sems_, recv_sems_, (N_DEV - 1) * i + k, peer)
                cp.wait_send()
                cp.wait_recv()

    outs = pl.pallas_call(
        body, name="grad_pieces_wait_" + tag,
        in_specs=[HBM] * (2 * n_arr) + [SEM, SEM, ANY],
        out_specs=[HBM] * (2 * n_arr),
        out_shape=[pltpu.HBM(BIG[a][1], BF16) for a in arrs] + [pltpu.HBM(_received_shape(a), BF16) for a in arrs],
        input_output_aliases={i: i for i in range(2 * n_arr)},
        compiler_params=pltpu.CompilerParams(has_side_effects=EFFECT),
    )(*dws, *lands, send_sems, recv_sems, after)
    return list(outs[:n_arr]), list(outs[n_arr:])


def _join_start(tag, g_halves):
    n_arr = len(g_halves)

    def body(*refs):
        srcs, lands = refs[:n_arr], refs[n_arr:2 * n_arr]
        send_sems, recv_sems = refs[2 * n_arr], refs[2 * n_arr + 1]
        token = refs[-1]
        x, y, c = _me()
        for i in range(n_arr):
            _rcopy(srcs[i], lands[i], send_sems, recv_sems, i, (x, y, 1 - c)).start()
        token[...] = jnp.zeros_like(token)

    shapes = [pltpu.HBM(g.shape, F32) for g in g_halves]
    outs = pl.pallas_call(
        body, name="grad_join_start_" + tag,
        in_specs=[HBM] * (2 * n_arr),
        out_specs=[SEM, SEM] + [HBM] * (2 * n_arr) + [VMEM_FULL],
        out_shape=[pltpu.SemaphoreType.DMA((n_arr,)), pltpu.SemaphoreType.DMA((n_arr,))] + shapes + shapes
        + [jax.ShapeDtypeStruct((8, HEAD_DIM), F32)],
        input_output_aliases={i: 2 + i for i in range(2 * n_arr)},
        compiler_params=pltpu.CompilerParams(has_side_effects=EFFECT),
    )(*[_in_hbm(g) for g in g_halves], *[_in_hbm(lax.empty(g.shape, F32)) for g in g_halves])
    return outs[0], outs[1], list(outs[2:2 + n_arr]), list(outs[2 + n_arr:2 + 2 * n_arr]), outs[-1]


def _join_wait(tag, send_sems, recv_sems, g_halves, lands, after):
    n_arr = len(g_halves)

    def body(*refs):
        srcs, lands_ = refs[:n_arr], refs[n_arr:2 * n_arr]
        send_sems_, recv_sems_ = refs[2 * n_arr], refs[2 * n_arr + 1]
        x, y, c = _me()
        for i in range(n_arr):
            cp = _rcopy(srcs[i], lands_[i], send_sems_, recv_sems_, i, (x, y, 1 - c))
            cp.wait_send()
            cp.wait_recv()

    shapes = [pltpu.HBM(g.shape, F32) for g in g_halves]
    outs = pl.pallas_call(
        body, name="grad_join_wait_" + tag,
        in_specs=[HBM] * (2 * n_arr) + [SEM, SEM, ANY],
        out_specs=[HBM] * (2 * n_arr),
        out_shape=shapes + shapes,
        input_output_aliases={i: i for i in range(2 * n_arr)},
        compiler_params=pltpu.CompilerParams(has_side_effects=EFFECT),
    )(*g_halves, *lands, send_sems, recv_sems, after)
    return list(outs[:n_arr]), list(outs[n_arr:])


def _piece_sum(a, dw, shard, core, received):
    name, full, axis = BIG[a]
    rows, cols = _received_shape(a)[1:]
    tr = _fit(rows, ROW_TILE)
    nbr = rows // tr

    def body(w_ref, dw_ref, rec_ref, o_ref):
        acc = dw_ref[...].astype(F32)
        for k in range(N_DEV - 1):
            acc = acc + rec_ref[k].astype(F32)
        o_ref[...] = acc

    if axis == 0:
        own = pl.BlockSpec((tr, cols), lambda i, w: (w[0] * nbr + i, w[1]))
    else:
        own = pl.BlockSpec((tr, cols), lambda i, w: (w[1] * nbr + i, w[0]))
    return pl.pallas_call(
        body, name="grad_sum_pieces_" + name,
        grid_spec=pltpu.PrefetchScalarGridSpec(
            num_scalar_prefetch=1, grid=(nbr,),
            in_specs=[own, pl.BlockSpec((N_DEV - 1, tr, cols), lambda i, w: (0, i, 0))],
            out_specs=pl.BlockSpec((tr, cols), lambda i, w: (i, 0))),
        out_shape=jax.ShapeDtypeStruct((rows, cols), F32),
        compiler_params=_params(("parallel",)),
    )(jnp.stack([shard, core]).astype(jnp.int32), dw, received)


def _scatter_begin(tag, arrs, dws):
    send_sems, recv_sems, dws, lands, token = _pieces_start(tag, arrs, dws)
    return (send_sems, recv_sems, dws, lands), token[0, 0]


def _scatter_reduce(tag, arrs, state, after):
    x, y, c = _me()
    send_sems, recv_sems, dws, lands = state
    dws, lands = _pieces_wait(tag, arrs, send_sems, recv_sems, dws, lands, after)
    g_own = [_piece_sum(a, dw, _shard_of((x, y)), c, r) for a, dw, r in zip(arrs, dws, lands)]
    send_sems, recv_sems, g_own, lands, token = _join_start(tag, g_own)
    return (send_sems, recv_sems, g_own, lands), token


def _scatter_end(tag, state, after):
    return _join_wait(tag, *state, after)


def _rows_start(tag, block):
    r, d = block.shape

    def body(src, land, send_sems, recv_sems, src_thru, land_thru, token):
        x, y, c = _me()
        for k, peer in enumerate(_peers(x, y, c)):
            _rcopy(src, land.at[_dev_index(x, y, c)], send_sems, recv_sems, k, peer).start()
        token[...] = jnp.zeros_like(token)

    outs = pl.pallas_call(
        body, name="rows_start_" + tag,
        in_specs=[HBM, HBM],
        out_specs=[SEM, SEM, HBM, HBM, VMEM_FULL],
        out_shape=[pltpu.SemaphoreType.DMA((N_DEV - 1,)), pltpu.SemaphoreType.DMA((N_DEV - 1,)),
                   pltpu.HBM((r, d), F32), pltpu.HBM((N_DEV, r, d), F32), jax.ShapeDtypeStruct((8, HEAD_DIM), F32)],
        input_output_aliases={0: 2, 1: 3},
        compiler_params=pltpu.CompilerParams(has_side_effects=EFFECT),
    )(_in_hbm(block), _in_hbm(lax.empty((N_DEV, r, d), F32)))
    return outs[0], outs[1], outs[2], outs[3], outs[4]


def _rows_wait(tag, send_sems, recv_sems, block, land, after):
    r, d = block.shape

    def body(src, land_, send_sems_, recv_sems_, after_ref, src_thru, land_thru):
        x, y, c = _me()
        for k, peer in enumerate(_peers(x, y, c)):
            cp = _rcopy(src, land_.at[_dev_index(*peer)], send_sems_, recv_sems_, k, peer)
            cp.wait_send()
            cp.wait_recv()

    outs = pl.pallas_call(
        body, name="rows_wait_" + tag,
        in_specs=[HBM, HBM, SEM, SEM, ANY],
        out_specs=[HBM, HBM],
        out_shape=[pltpu.HBM((r, d), F32), pltpu.HBM((N_DEV, r, d), F32)],
        input_output_aliases={0: 0, 1: 1},
        compiler_params=pltpu.CompilerParams(has_side_effects=EFFECT),
    )(block, land, send_sems, recv_sems, after)
    me = _dev_index(*_me())
    return lax.dynamic_update_slice(outs[1], outs[0][None], (me, 0, 0))


ADA_ROWS = 80
ADA_W = 6 * D_MODEL // N_SHARD


def _ada_forward(c_block, cctx_block, w_ada, b_shard):
    d = c_block.shape[1]

    def body(c_ref, cc_ref, w_ref, b_ref, act_ref, mods_ref, raw, mloc, send_sems, recv_sems):
        x, y, c = _me()
        me = _dev_index(x, y, c)
        s_me = _shard_of((x, y))
        raw[72:ADA_ROWS, :] = jnp.zeros((ADA_ROWS - 72, d), F32)
        raw[pl.ds(pl.multiple_of(me * 8, 8), 8), :] = c_ref[...]
        raw[64:72, :] = cc_ref[...]
        sends = []
        for mask in range(1, N_DEV):
            peer = (x ^ (mask >> 2), y ^ ((mask >> 1) & 1), c ^ (mask & 1))
            cp = _rcopy(c_ref, raw.at[pl.ds(pl.multiple_of(me * 8, 8), 8), :], send_sems, recv_sems, mask - 1, peer)
            cp.start()
            sends.append(cp)
        for mask in range(1, N_DEV):
            peer = (x ^ (mask >> 2), y ^ ((mask >> 1) & 1), c ^ (mask & 1))
            landed = raw.at[pl.ds(pl.multiple_of(_dev_index(*peer) * 8, 8), 8), :]
            _rcopy(landed, landed, send_sems, recv_sems, mask - 1, peer).wait_recv()
        v = raw[...]
        act = v * jax.nn.sigmoid(v)
        act_ref[...] = act
        mloc[...] = lax.dot_general(act.astype(BF16), w_ref[...].astype(BF16), NN,
                                    preferred_element_type=F32) + b_ref[...]
        mods_ref[s_me, 0:8, :] = mloc[pl.ds(pl.multiple_of(me * 8, 8), 8), :]
        mods_ref[s_me, 8:16, :] = mloc[64:72, :]
        base = N_DEV - 1
        for j, chip in enumerate(_other_chips(x, y)):
            peer = (*chip, c)
            rows = mloc.at[pl.ds(pl.multiple_of(_dev_index(*peer) * 8, 8), 8), :]
            cp = _rcopy(rows, mods_ref.at[s_me, 0:8, :], send_sems, recv_sems, base + 2 * j, peer)
            cp.start()
            sends.append(cp)
            cp = _rcopy(mloc.at[64:72, :], mods_ref.at[s_me, 8:16, :], send_sems, recv_sems, base + 2 * j + 1, peer)
            cp.start()
            sends.append(cp)
        for j, chip in enumerate(_other_chips(x, y)):
            for part in range(2):
                landed = mods_ref.at[_shard_of(chip), 8 * part:8 * part + 8, :]
                _rcopy(landed, landed, send_sems, recv_sems, base + 2 * j + part, (*chip, c)).wait_recv()
        for cp in sends:
            cp.wait_send()

    n_sem = N_DEV - 1 + 2 * N_CHIP_PEERS
    return pl.pallas_call(
        body, name="ada_forward",
        in_specs=[VMEM_FULL] * 4, out_specs=[VMEM_FULL, VMEM_FULL],
        out_shape=[jax.ShapeDtypeStruct((ADA_ROWS, d), F32), jax.ShapeDtypeStruct((N_SHARD, 16, ADA_W), F32)],
        scratch_shapes=[pltpu.VMEM((ADA_ROWS, d), F32), pltpu.VMEM((ADA_ROWS, ADA_W), F32),
                        pltpu.SemaphoreType.DMA((n_sem,)), pltpu.SemaphoreType.DMA((n_sem,))],
        compiler_params=pltpu.CompilerParams(vmem_limit_bytes=VMEM_LIMIT),
    )(c_block, cctx_block, w_ada, b_shard)


def _small_reduce(gathered):
    d = gathered.shape[2]

    def body(g_ref, o_ref):
        tot = g_ref[0]
        for i in range(1, N_DEV):
            tot = tot + g_ref[i]
        o_ref[...] = tot
        o_ref[0:2, :] = tot[0:2] + tot[6:8]
        o_ref[12:13, :] = jnp.broadcast_to(jnp.sum(tot[12:13], axis=1, keepdims=True), (1, d))

    return pl.pallas_call(body, name="small_reduce", in_specs=[VMEM_FULL], out_specs=VMEM_FULL,
                          out_shape=jax.ShapeDtypeStruct((16, d), F32))(gathered)


def _cctx_grad(gathered, c_ctx):
    d = gathered.shape[2]

    def body(g_ref, c_ref, o_ref):
        tot = g_ref[0, 0:1, :]
        for chip in range(1, N_SHARD):
            tot = tot + g_ref[2 * chip, 0:1, :]
        v = c_ref[...]
        sig = jax.nn.sigmoid(v)
        o_ref[...] = tot * (sig * (1.0 + v * (1.0 - sig)))

    return pl.pallas_call(body, name="cctx_grad", in_specs=[VMEM_FULL, VMEM_FULL], out_specs=VMEM_FULL,
                          out_shape=jax.ShapeDtypeStruct((1, d), F32))(gathered, c_ctx.reshape(1, d))


def _cast_into_full(w, shard, full, axis, name):
    r, cdim = w.shape
    tr = _fit(r, ROW_TILE)
    nbr = r // tr

    def body(s_ref, w_ref, o_ref):
        o_ref[...] = w_ref[...].astype(BF16)

    if axis == 0:
        out_spec = pl.BlockSpec((tr, cdim), lambda i, s: (s[0] * nbr + i, 0))
    else:
        out_spec = pl.BlockSpec((tr, cdim), lambda i, s: (i, s[0]))
    return pl.pallas_call(
        body, name=name,
        grid_spec=pltpu.PrefetchScalarGridSpec(
            num_scalar_prefetch=1, grid=(nbr,), in_specs=[pl.BlockSpec((tr, cdim), lambda i, s: (i, 0))],
            out_specs=out_spec),
        out_shape=jax.ShapeDtypeStruct(full, BF16), compiler_params=_params(("parallel",)),
    )(shard.reshape(1).astype(jnp.int32), w)


def _adamw_halves(w, g_own, g_other, m, v, core, axis, name):
    r, cdim = w.shape
    hr, hc = (r // 2, cdim) if axis == 1 else (r, cdim // 2)
    assert g_own.shape == (hr, hc) and g_other.shape == (hr, hc)
    tr = _fit(hr, 256)
    nb = hr // tr
    c1 = 1.0 - ADAM_B1 ** ADAM_STEP
    c2 = 1.0 - ADAM_B2 ** ADAM_STEP

    def body(c_ref, w_ref, go_ref, gt_ref, m_ref, v_ref, g_ref, d_ref, nm_ref, nv_ref):
        gv = jnp.where(pl.program_id(0) == c_ref[0], go_ref[...], gt_ref[...])
        nm = ADAM_B1 * m_ref[...] + (1.0 - ADAM_B1) * gv
        nv = ADAM_B2 * v_ref[...] + (1.0 - ADAM_B2) * (gv * gv)
        g_ref[...] = gv
        nm_ref[...] = nm
        nv_ref[...] = nv
        d_ref[...] = -ADAM_LR * ((nm / c1) / (jnp.sqrt(nv / c2) + ADAM_EPS) + ADAM_WD * w_ref[...])

    if axis == 1:
        big = pl.BlockSpec((tr, hc), lambda p, i, c: (p * nb + i, 0))
    else:
        big = pl.BlockSpec((tr, hc), lambda p, i, c: (i, p))
    own = pl.BlockSpec((tr, hc), lambda p, i, c: (jnp.where(p == c[0], i, 0), 0))
    other = pl.BlockSpec((tr, hc), lambda p, i, c: (jnp.where(p == c[0], 0, i), 0))
    sh = jax.ShapeDtypeStruct((r, cdim), F32)
    return pl.pallas_call(
        body, name=name,
        grid_spec=pltpu.PrefetchScalarGridSpec(
            num_scalar_prefetch=1, grid=(2, nb), in_specs=[big, own, other, big, big], out_specs=[big] * 4),
        out_shape=[sh] * 4, compiler_params=_params(("parallel", "parallel")),
    )(core.reshape(1).astype(jnp.int32), w, g_own, g_other, m, v)


def _adamw(w, g, m, v, name):
    r, cdim = w.shape
    tr = _fit(r, 128) if r % (ROW_TILE // 4) == 0 else r
    c1 = 1.0 - ADAM_B1 ** ADAM_STEP
    c2 = 1.0 - ADAM_B2 ** ADAM_STEP

    def body(w_ref, g_ref, m_ref, v_ref, d_ref, nm_ref, nv_ref):
        gv = g_ref[...]
        nm = ADAM_B1 * m_ref[...] + (1.0 - ADAM_B1) * gv
        nv = ADAM_B2 * v_ref[...] + (1.0 - ADAM_B2) * (gv * gv)
        nm_ref[...] = nm
        nv_ref[...] = nv
        d_ref[...] = -ADAM_LR * ((nm / c1) / (jnp.sqrt(nv / c2) + ADAM_EPS) + ADAM_WD * w_ref[...])

    spec = pl.BlockSpec((tr, cdim), lambda i: (i, 0))
    sh = jax.ShapeDtypeStruct((r, cdim), F32)
    return pl.pallas_call(body, name=name, grid=(r // tr,), in_specs=[spec] * 4, out_specs=[spec] * 3,
                          out_shape=[sh, sh, sh], compiler_params=_params(("parallel",)))(w, g, m, v)


def _adamw_small(ws, gs, ms, vs):
    k = len(ws)
    c1 = 1.0 - ADAM_B1 ** ADAM_STEP
    c2 = 1.0 - ADAM_B2 ** ADAM_STEP

    def body(*refs):
        w_refs, g_refs, m_refs, v_refs = refs[0:k], refs[k:2 * k], refs[2 * k:3 * k], refs[3 * k:4 * k]
        d_refs, nm_refs, nv_refs = refs[4 * k:5 * k], refs[5 * k:6 * k], refs[6 * k:7 * k]
        for i in range(k):
            gv = g_refs[i][...]
            nm = ADAM_B1 * m_refs[i][...] + (1.0 - ADAM_B1) * gv
            nv = ADAM_B2 * v_refs[i][...] + (1.0 - ADAM_B2) * (gv * gv)
            nm_refs[i][...] = nm
            nv_refs[i][...] = nv
            d_refs[i][...] = -ADAM_LR * ((nm / c1) / (jnp.sqrt(nv / c2) + ADAM_EPS) + ADAM_WD * w_refs[i][...])

    shapes = [jax.ShapeDtypeStruct(w.shape, F32) for w in ws]
    outs = pl.pallas_call(body, name="adamw_small", in_specs=[VMEM_FULL] * (4 * k), out_specs=[VMEM_FULL] * (3 * k),
                          out_shape=shapes * 3)(*ws, *gs, *ms, *vs)
    return outs[0:k], outs[k:2 * k], outs[2 * k:3 * k]


SMALL = (("c_ctx", D_MODEL), ("b_ada", 6 * D_MODEL), ("q_norm_g", HEAD_DIM), ("k_norm_g", HEAD_DIM),
         ("sink_logit", HEADS_A), ("ln1_g", D_MODEL), ("ln1_b", D_MODEL), ("ln2_g", D_MODEL), ("ln2_b", D_MODEL))
WEIGHT_ORDER = ("c_ctx", "w_ada", "b_ada", "w_in", "q_norm_g", "k_norm_g", "sink_logit", "w_out", "ln1_g", "ln1_b",
                "w_gate", "w_up", "w_down", "ln2_g", "ln2_b")


def kernel(x, c, ctx, c_ctx, w_ada, b_ada, w_in, q_norm_g, k_norm_g, sink_logit, w_out, ln1_g, ln1_b, w_gate, w_up, w_down, ln2_g, ln2_b, loss_target, m_c_ctx, m_w_ada, m_b_ada, m_w_in, m_q_norm_g, m_k_norm_g, m_sink_logit, m_w_out, m_ln1_g, m_ln1_b, m_w_gate, m_w_up, m_w_down, m_ln2_g, m_ln2_b, v_c_ctx, v_w_ada, v_b_ada, v_w_in, v_q_norm_g, v_k_norm_g, v_sink_logit, v_w_out, v_ln1_g, v_ln1_b, v_w_gate, v_w_up, v_w_down, v_ln2_g, v_ln2_b):
    d = D_MODEL
    w = dict(c_ctx=c_ctx, w_ada=w_ada[0], b_ada=b_ada, w_in=w_in[0], q_norm_g=q_norm_g, k_norm_g=k_norm_g,
             sink_logit=sink_logit, w_out=w_out[0], ln1_g=ln1_g, ln1_b=ln1_b, w_gate=w_gate[0], w_up=w_up[0],
             w_down=w_down[0], ln2_g=ln2_g, ln2_b=ln2_b)
    m = dict(c_ctx=m_c_ctx, w_ada=m_w_ada[0], b_ada=m_b_ada, w_in=m_w_in[0], q_norm_g=m_q_norm_g, k_norm_g=m_k_norm_g,
             sink_logit=m_sink_logit, w_out=m_w_out[0], ln1_g=m_ln1_g, ln1_b=m_ln1_b, w_gate=m_w_gate[0],
             w_up=m_w_up[0], w_down=m_w_down[0], ln2_g=m_ln2_g, ln2_b=m_ln2_b)
    v = dict(c_ctx=v_c_ctx, w_ada=v_w_ada[0], b_ada=v_b_ada, w_in=v_w_in[0], q_norm_g=v_q_norm_g, k_norm_g=v_k_norm_g,
             sink_logit=v_sink_logit, w_out=v_w_out[0], ln1_g=v_ln1_g, ln1_b=v_ln1_b, w_gate=v_w_gate[0],
             w_up=v_w_up[0], w_down=v_w_down[0], ln2_g=v_ln2_g, ln2_b=v_ln2_b)
    mx, my, mc = _me()
    s_me = _shard_of((mx, my))
    me = _dev_index(mx, my, mc)
    pad8 = lambda row: jnp.concatenate([row.reshape(1, -1), jnp.zeros((7, row.size), F32)], axis=0)

    b_shard = lax.dynamic_slice(b_ada, (0, s_me * ADA_W), (1, ADA_W))
    act, mods4 = _ada_forward(pad8(c), pad8(c_ctx), w["w_ada"], b_shard)

    gathers = []
    prev, shard = mods4, s_me
    for k, names in enumerate(W_GROUPS):
        arrs = tuple(BIG_INDEX[name] for name in names)
        bufs = [_cast_into_full(w[name], shard, BIG[a][1], BIG[a][2], "cast_" + name) for name, a in zip(names, arrs)]
        send_sems, recv_sems, thru, prev = _gather_start("g%d" % k, arrs, bufs, prev)
        shard = s_me + prev[0, 0].astype(jnp.int32)
        gathers.append((arrs, send_sems, recv_sems, thru))

    forwards = {}

    def prefetch(k, after):
        arrs, send_sems, recv_sems, thru = gathers[k]
        landed = _gather_wait("g%d" % k, arrs, send_sems, recv_sems, thru, after)
        fwd_send, fwd_recv, landed, token = _forward_start("g%d" % k, arrs, landed)
        forwards[k] = (fwd_send, fwd_recv, landed)
        return token[0, 0]

    def weights(k, after):
        arrs, send_sems, recv_sems, thru = gathers[k]
        if k in forwards:
            return _forward_wait("g%d" % k, arrs, *forwards[k], after)
        landed = _gather_wait("g%d" % k, arrs, send_sems, recv_sems, thru, after)
        return _gather_forward("g%d" % k, arrs, landed)

    mod = jnp.transpose(mods4[:, 0:1, :], (1, 0, 2)).reshape(1, 6 * d) + prev[0, 0]
    mod_ctx = jnp.transpose(mods4[:, 8:9, :], (1, 0, 2)).reshape(1, 6 * d)

    scatters = {}

    def grads_out(k, dws):
        arrs = tuple(BIG_INDEX[name] for name in G_GROUPS[k])
        scatters[k], zero = _scatter_begin("g%d" % k, arrs, dws)
        return zero

    grad_x, partial = _layer_fwd_bwd(x[0], ctx[0], loss_target[0], mod, mod_ctx, weights, prefetch, grads_out,
                                     q_norm_g, k_norm_g, sink_logit, ln1_g, ln1_b, ln2_g, ln2_b)
    grads, delta, new_m, new_v = {}, {}, {}, {}

    p_send, p_recv, partial, p_land, after = _rows_start("partials", partial)
    joins = []
    for k, names in enumerate(G_GROUPS):
        arrs = tuple(BIG_INDEX[name] for name in names)
        state, after = _scatter_reduce("g%d" % k, arrs, scatters[k], after)
        joins.append(state)
    for k, names in enumerate(G_GROUPS):
        g_own, g_other = _scatter_end("g%d" % k, joins[k], after)
        for name, own, other in zip(names, g_own, g_other):
            grads[name], delta[name], new_m[name], new_v[name] = _adamw_halves(
                w[name], own, other, m[name], v[name], mc, BIG[BIG_INDEX[name]][2], "adamw_" + name)
            after = new_v[name]

    gathered = _rows_wait("partials", p_send, p_recv, partial, p_land, after)
    tot = _small_reduce(gathered)
    grads["b_ada"] = tot[0:6].reshape(1, 6 * d)
    grads["ln1_g"], grads["ln1_b"], grads["ln2_g"], grads["ln2_b"] = tot[8:9], tot[9:10], tot[10:11], tot[11:12]
    grads["q_norm_g"] = tot[13:14, 0:HEAD_DIM]
    grads["k_norm_g"] = tot[13:14, HEAD_DIM:2 * HEAD_DIM]
    grads["sink_logit"] = tot[13:14, 2 * HEAD_DIM:2 * HEAD_DIM + HEADS_A]
    loss = tot[12, 0]

    dm_all = gathered[:, 0:6, :].reshape(N_DEV, 6 * d)
    dmc_tot = jnp.concatenate([tot[6:8].reshape(1, 2 * d), jnp.zeros((1, 4 * d), F32)], axis=1)
    dm_rows = jnp.concatenate([pad8(dm_all[i]) for i in range(N_DEV)] + [pad8(dmc_tot), jnp.zeros((8, 6 * d), F32)], axis=0)
    dm_shard = lax.dynamic_slice(dm_rows, (0, s_me * ADA_W), (ADA_ROWS, ADA_W))
    dmc_shard = lax.dynamic_slice(pad8(dmc_tot), (0, s_me * ADA_W), (8, ADA_W))
    cc_part = _matmul(dmc_shard, w["w_ada"], name="d_cctx", tb=True, tm=8, tn=1024, tk=1536, out_dtype=F32)
    c_send, c_recv, cc_part, c_land, c_token = _rows_start("cctx", cc_part)
    grads["w_ada"] = _matmul(act, dm_shard, name="dw_ada", ta=True, tm=1024, tn=1024, tk=ADA_ROWS, out_dtype=F32,
                             after=c_token)
    delta["w_ada"], new_m["w_ada"], new_v["w_ada"] = _adamw(w["w_ada"], grads["w_ada"], m["w_ada"], v["w_ada"],
                                                            "adamw_w_ada")
    gathered_cc = _rows_wait("cctx", c_send, c_recv, cc_part, c_land, new_v["w_ada"])
    grads["c_ctx"] = _cctx_grad(gathered_cc, c_ctx).reshape(d)
    rows = lambda t: [t[name].reshape(1, size) for name, size in SMALL]
    small = _adamw_small(rows(w), rows(grads), rows(m), rows(v))
    for k, (name, size) in enumerate(SMALL):
        delta[name], new_m[name], new_v[name] = [t[k].reshape(w[name].shape) for t in small]
        grads[name] = grads[name].reshape(w[name].shape)

    lead = lambda name, t: t[None] if name in ("w_ada", "w_in", "w_out", "w_gate", "w_up", "w_down") else t
    outs = [loss, grad_x[None]]
    for group in (grads, delta, new_m, new_v):
        outs += [lead(name, group[name]) for name in WEIGHT_ORDER]
    return tuple(outs)
```

```python
import functools
import math

import jax
import jax.numpy as jnp
from jax import lax
from jax.experimental import pallas as pl
from jax.experimental.pallas import tpu as pltpu

F32 = jnp.float32
BF16 = jnp.bfloat16
MESH = pl.DeviceIdType.MESH

D_MODEL = 2048
HEAD_DIM = 128
HEADS_A = 8
HEADS_B = 8
KV_A = 2
KV_B = 2
GROUP = 4
GRID_W = 64
WINDOW = 128
BLOCK = 128
FFN = 5632
IN_WIDTH = 3072
MIX_WIDTH = 2048
ROPE_THETA = 10000.0
EPS = 1e-6
ATTN_SCALE = HEAD_DIM ** -0.5
DN_ALPHA = 2.0 ** 0.25
N_SHARD = 4
N_DEV = 8

ADAM_LR = 0.001
ADAM_B1 = 0.9
ADAM_B2 = 0.999
ADAM_EPS = 1e-08
ADAM_WD = 0.01
ADAM_STEP = 10

QA0, KA0, VA0, QB0, KB0, VB0 = 0, 1024, 1280, 1536, 2560, 2816

VMEM_LIMIT = 56 * 1024 * 1024
ROW_TILE = 256
NN = (((1,), (0,)), ((), ()))
NT = (((1,), (1,)), ((), ()))
TN = (((0,), (0,)), ((), ()))


def _fit(total, pref):
    step = ROW_TILE // 4
    best = step
    for cand in range(step, pref + 1, step):
        if total % cand == 0:
            best = cand
    return best


def _params(sem=None):
    return pltpu.CompilerParams(dimension_semantics=sem, vmem_limit_bytes=VMEM_LIMIT)


def _matmul(a, b, *, name, ta=False, tb=False, tm, tn, tk, out_dtype, after=None):
    m = a.shape[1] if ta else a.shape[0]
    k = a.shape[0] if ta else a.shape[1]
    n = b.shape[0] if tb else b.shape[1]
    assert (b.shape[1] if tb else b.shape[0]) == k
    tm, tn, tk = min(tm, m), min(tn, n), min(tk, k)
    assert m % tm == 0 and n % tn == 0 and k % tk == 0, (name, m, n, k, tm, tn, tk)
    nk = k // tk
    dn = (((0 if ta else 1,), (1 if tb else 0,)), ((), ()))

    def product(a_ref, b_ref):
        return lax.dot_general(a_ref[...].astype(BF16), b_ref[...].astype(BF16), dn, preferred_element_type=F32)

    def body_whole_k(a_ref, b_ref, *rest):
        o_ref = rest[-1]
        o_ref[...] = product(a_ref, b_ref).astype(o_ref.dtype)

    def body(a_ref, b_ref, *rest):
        o_ref, acc_ref = rest[-2:]
        kk = pl.program_id(2)
        part = product(a_ref, b_ref)

        @pl.when(kk == 0)
        def _():
            acc_ref[...] = part

        @pl.when(kk != 0)
        def _():
            acc_ref[...] += part

        @pl.when(kk == nk - 1)
        def _():
            o_ref[...] = acc_ref[...].astype(o_ref.dtype)

    a_spec = (pl.BlockSpec((tk, tm), lambda i, j, kk: (kk, i)) if ta
              else pl.BlockSpec((tm, tk), lambda i, j, kk: (i, kk)))
    b_spec = (pl.BlockSpec((tn, tk), lambda i, j, kk: (j, kk)) if tb
              else pl.BlockSpec((tk, tn), lambda i, j, kk: (kk, j)))
    return pl.pallas_call(
        body_whole_k if nk == 1 else body, name=name, grid=(m // tm, n // tn, nk),
        in_specs=[a_spec, b_spec] + ([] if after is None else [pl.BlockSpec(memory_space=pl.ANY)]),
        out_specs=pl.BlockSpec((tm, tn), lambda i, j, kk: (i, j)),
        out_shape=jax.ShapeDtypeStruct((m, n), out_dtype),
        scratch_shapes=[] if nk == 1 else [pltpu.VMEM((tm, tn), F32)],
        compiler_params=_params(("parallel", "parallel", "arbitrary")),
    )(a, b, *([] if after is None else [after]))


def _modulate_rows(x, ctx, mods):
    n, d = x.shape
    c = ctx.shape[0]
    nx = n // ROW_TILE
    assert c == ROW_TILE

    def body(x_ref, ctx_ref, mods_ref, o_ref):
        i = pl.program_id(0)

        @pl.when(i < nx)
        def _():
            o_ref[...] = (x_ref[...] * (1.0 + mods_ref[0:1, :]) + mods_ref[1:2, :]).astype(BF16)

        @pl.when(i >= nx)
        def _():
            o_ref[...] = (ctx_ref[...] * (1.0 + mods_ref[2:3, :]) + mods_ref[3:4, :]).astype(BF16)

    return pl.pallas_call(
        body, name="modulate_rows", grid=(nx + 1,),
        in_specs=[pl.BlockSpec((ROW_TILE, d), lambda i: (jnp.minimum(i, nx - 1), 0)),
                  pl.BlockSpec((ROW_TILE, d), lambda i: (0, 0)),
                  pl.BlockSpec((8, d), lambda i: (0, 0))],
        out_specs=pl.BlockSpec((ROW_TILE, d), lambda i: (i, 0)),
        out_shape=jax.ShapeDtypeStruct((n + c, d), BF16),
        compiler_params=_params(("parallel",)),
    )(x, ctx, mods)


def _rope_tables(n, c):
    rows = n // GRID_W
    row_ids = jnp.repeat(jnp.arange(rows, dtype=F32), GRID_W)
    col_ids = jnp.tile(jnp.arange(GRID_W, dtype=F32), rows)
    axis_dim = HEAD_DIM // 2
    inv_freq = jnp.power(ROPE_THETA, -jnp.arange(0, axis_dim, 2, dtype=F32) / axis_dim)
    ang_r = row_ids[:, None] * inv_freq
    ang_c = col_ids[:, None] * inv_freq
    ang = jnp.concatenate([ang_r, ang_r, ang_c, ang_c], axis=-1)
    cos, sin = jnp.cos(ang), jnp.sin(ang)
    quarter = (jnp.arange(HEAD_DIM) // (HEAD_DIM // 4)) % 2
    sin_a = jnp.where(quarter == 0, -sin, 0.0)
    sin_b = jnp.where(quarter == 1, sin, 0.0)
    pad = lambda t, v: jnp.concatenate([t, jnp.full((c, HEAD_DIM), v, F32)], axis=0)
    return pad(cos, 1.0), pad(sin_a, 0.0), pad(sin_b, 0.0)


def _rope(x, cos, sin_a, sin_b):
    return x * cos + pltpu.roll(x, 96, 1) * sin_a + pltpu.roll(x, 32, 1) * sin_b


def _rope_t(dy, cos, sin_a, sin_b):
    return dy * cos - pltpu.roll(dy, 96, 1) * sin_a - pltpu.roll(dy, 32, 1) * sin_b


def _rms(x):
    r = lax.rsqrt(jnp.mean(x * x, axis=-1, keepdims=True) + EPS)
    return x * r, r


def _qkv_post(h_all, cos, sin_a, sin_b, q_g, k_g):
    t = h_all.shape[0]
    nt = t // ROW_TILE

    def body(h_ref, cos_ref, sa_ref, sb_ref, qg_ref, kg_ref, qa_ref, ka_ref, va_ref, qb_ref, kb_ref, vb_ref):
        cos_, sa, sb = cos_ref[...], sa_ref[...], sb_ref[...]
        sl = lambda off, hh: h_ref[:, off + hh * HEAD_DIM: off + (hh + 1) * HEAD_DIM]
        for hh in range(HEADS_A):
            qa_ref[hh] = (_rope(sl(QA0, hh), cos_, sa, sb) * ATTN_SCALE).astype(BF16)
        for hh in range(KV_A):
            ka_ref[hh] = _rope(sl(KA0, hh), cos_, sa, sb).astype(BF16)
            va_ref[hh] = sl(VA0, hh).astype(BF16)
        for hh in range(HEADS_B):
            xn, _ = _rms(sl(QB0, hh))
            qb_ref[hh] = (_rope(xn * qg_ref[...], cos_, sa, sb) * ATTN_SCALE).astype(BF16)
        for hh in range(KV_B):
            xn, _ = _rms(sl(KB0, hh))
            kb_ref[hh] = _rope(xn * kg_ref[...], cos_, sa, sb).astype(BF16)
            vb_ref[hh] = sl(VB0, hh).astype(BF16)

    tab = pl.BlockSpec((ROW_TILE, HEAD_DIM), lambda i: (i, 0))
    gain = pl.BlockSpec((1, HEAD_DIM), lambda i: (0, 0))
    hs = lambda nh: pl.BlockSpec((nh, ROW_TILE, HEAD_DIM), lambda i: (0, i, 0))
    sh = lambda nh: jax.ShapeDtypeStruct((nh, t, HEAD_DIM), BF16)
    return pl.pallas_call(
        body, name="qkv_post", grid=(nt,),
        in_specs=[pl.BlockSpec((ROW_TILE, IN_WIDTH), lambda i: (i, 0)), tab, tab, tab, gain, gain],
        out_specs=[hs(HEADS_A), hs(KV_A), hs(KV_A), hs(HEADS_B), hs(KV_B), hs(KV_B)],
        out_shape=[sh(HEADS_A), sh(KV_A), sh(KV_A), sh(HEADS_B), sh(KV_B), sh(KV_B)],
        compiler_params=_params(("parallel",)),
    )(h_all, cos, sin_a, sin_b, q_g, k_g)


def _qkv_bwd_post(h_all, cos, sin_a, sin_b, q_g, k_g, dqa, dka, dva, dqb, dkb, dvb, n):
    t = h_all.shape[0]
    nt = t // ROW_TILE
    nx = n // ROW_TILE

    def body(h_ref, cos_ref, sa_ref, sb_ref, qg_ref, kg_ref,
             dqa_ref, dka_ref, dva_ref, dqb_ref, dkb_ref, dvb_ref, dh_ref, gs_ref):
        i = pl.program_id(0)
        cos_, sa, sb = cos_ref[...], sa_ref[...], sb_ref[...]
        latent = (i < nx).astype(F32)
        sl = lambda off, hh: h_ref[:, off + hh * HEAD_DIM: off + (hh + 1) * HEAD_DIM]

        def put(off, hh, val):
            dh_ref[:, off + hh * HEAD_DIM: off + (hh + 1) * HEAD_DIM] = val.astype(BF16)

        def norm_bwd(x, gain, dy):
            xn, r = _rms(x)
            dxh = dy * gain
            dx = r * (dxh - xn * jnp.mean(dxh * xn, axis=-1, keepdims=True))
            return dx, jnp.sum(dy * xn, axis=0, keepdims=True)

        for hh in range(HEADS_A):
            put(QA0, hh, _rope_t(dqa_ref[hh] * (ATTN_SCALE * latent), cos_, sa, sb))
        for hh in range(KV_A):
            put(KA0, hh, _rope_t(dka_ref[hh], cos_, sa, sb))
            put(VA0, hh, dva_ref[hh])
        gq = jnp.zeros((1, HEAD_DIM), F32)
        gk = jnp.zeros((1, HEAD_DIM), F32)
        for hh in range(HEADS_B):
            dq_t = dqb_ref[hh // GROUP, :, (hh % GROUP) * ROW_TILE:(hh % GROUP + 1) * ROW_TILE]
            dy = _rope_t(dq_t.T * (ATTN_SCALE * latent), cos_, sa, sb)
            dx, g = norm_bwd(sl(QB0, hh), qg_ref[...], dy)
            put(QB0, hh, dx)
            gq = gq + g
        for hh in range(KV_B):
            dy = _rope_t(dkb_ref[hh], cos_, sa, sb)
            dx, g = norm_bwd(sl(KB0, hh), kg_ref[...], dy)
            put(KB0, hh, dx)
            gk = gk + g
            put(VB0, hh, dvb_ref[hh])
        upd = jnp.concatenate([gq, gk, jnp.zeros((6, HEAD_DIM), F32)], axis=0)

        @pl.when(i == 0)
        def _():
            gs_ref[...] = upd

        @pl.when(i != 0)
        def _():
            gs_ref[...] += upd

    tab = pl.BlockSpec((ROW_TILE, HEAD_DIM), lambda i: (i, 0))
    gain = pl.BlockSpec((1, HEAD_DIM), lambda i: (0, 0))
    lat = lambda nh: pl.BlockSpec((nh, ROW_TILE, HEAD_DIM), lambda i: (0, jnp.minimum(i, nx - 1), 0))
    full = lambda nh: pl.BlockSpec((nh, ROW_TILE, HEAD_DIM), lambda i: (0, i, 0))
    return pl.pallas_call(
        body, name="qkv_bwd_post", grid=(nt,),
        in_specs=[pl.BlockSpec((ROW_TILE, IN_WIDTH), lambda i: (i, 0)), tab, tab, tab, gain, gain,
                  lat(HEADS_A), full(KV_A), full(KV_A),
                  pl.BlockSpec((KV_B, None, HEAD_DIM, GROUP * ROW_TILE), lambda i: (0, jnp.minimum(i, nx - 1), 0, 0)),
                  full(KV_B), full(KV_B)],
        out_specs=[pl.BlockSpec((ROW_TILE, IN_WIDTH), lambda i: (i, 0)),
                   pl.BlockSpec((8, HEAD_DIM), lambda i: (0, 0))],
        out_shape=[jax.ShapeDtypeStruct((t, IN_WIDTH), BF16), jax.ShapeDtypeStruct((8, HEAD_DIM), F32)],
        compiler_params=_params(("arbitrary",)),
    )(h_all, cos, sin_a, sin_b, q_g, k_g, dqa, dka, dva, dqb, dkb, dvb)


GB_TQ = 256
GB_TK = 256


def _heads_rows(ref2d, tq):
    return jnp.concatenate([ref2d[:, hh * HEAD_DIM:(hh + 1) * HEAD_DIM] for hh in range(GROUP)], axis=0)


def _attn_b_fwd(qb, kb, vb, n):
    t = kb.shape[1]
    nk = t // GB_TK
    tq = GB_TQ
    nq = n // tq
    qb_step = 8 if nq % 8 == 0 else (4 if nq % 4 == 0 else 1)
    rows = qb_step * GROUP * tq

    def body(q_ref, k_ref, v_ref, o_ref, lse_ref, m_s, l_s, acc_s):
        blocks = [(s, hh) for s in range(qb_step) for hh in range(GROUP)]
        q = jnp.concatenate([q_ref[hh, s * tq:(s + 1) * tq, :] for s, hh in blocks], axis=0)
        m_s[...] = jnp.full((1, rows), -jnp.inf, F32)
        l_s[...] = jnp.zeros((1, rows), F32)
        acc_s[...] = jnp.zeros((HEAD_DIM, rows), F32)

        def scores(j):
            start = pl.multiple_of(j * GB_TK, GB_TK)
            return lax.dot_general(k_ref[pl.ds(start, GB_TK), :], q, NT, preferred_element_type=F32)

        def step(j, carry):
            st = scores(j)
            vs = v_ref[pl.ds(pl.multiple_of(j * GB_TK, GB_TK), GB_TK), :]
            m_prev = m_s[...]
            m_new = jnp.maximum(m_prev, jnp.max(st, axis=0, keepdims=True))
            p = jnp.exp(st - m_new)
            alpha = jnp.exp(m_prev - m_new)
            l_s[...] = alpha * l_s[...] + jnp.sum(p, axis=0, keepdims=True)
            acc_s[...] = alpha * acc_s[...] + lax.dot_general(vs, p.astype(BF16), TN, preferred_element_type=F32)
            m_s[...] = m_new
            return carry

        lax.fori_loop(0, nk, step, 0, unroll=4)
        ot = acc_s[...] * (1.0 / l_s[...])
        lse = m_s[...] + jnp.log(l_s[...])
        width = GROUP * tq
        for s in range(qb_step):
            lse_ref[s] = lse[:, s * width:(s + 1) * width]
        for k, (s, hh) in enumerate(blocks):
            o_ref[s * tq:(s + 1) * tq, hh * HEAD_DIM:(hh + 1) * HEAD_DIM] = ot[:, k * tq:(k + 1) * tq].T.astype(BF16)

    return pl.pallas_call(
        body, name="attn_b_fwd", grid=(KV_B, nq // qb_step),
        in_specs=[pl.BlockSpec((GROUP, qb_step * tq, HEAD_DIM), lambda g, i: (g, i, 0)),
                  pl.BlockSpec((None, t, HEAD_DIM), lambda g, i: (g, 0, 0)),
                  pl.BlockSpec((None, t, HEAD_DIM), lambda g, i: (g, 0, 0))],
        out_specs=[pl.BlockSpec((qb_step * tq, GROUP * HEAD_DIM), lambda g, i: (i, KV_A + g)),
                   pl.BlockSpec((None, qb_step, 1, GROUP * tq), lambda g, i: (g, i, 0, 0))],
        out_shape=[jax.ShapeDtypeStruct((n, MIX_WIDTH), BF16),
                   jax.ShapeDtypeStruct((KV_B, nq, 1, GROUP * tq), F32)],
        scratch_shapes=[pltpu.VMEM((1, rows), F32), pltpu.VMEM((1, rows), F32), pltpu.VMEM((HEAD_DIM, rows), F32)],
        compiler_params=_params(("parallel", "parallel")),
    )(qb, kb, vb)


def _attn_b_bwd(qb, kb, vb, dheads, lse, delta, n):
    t = kb.shape[1]
    nk = t // GB_TK
    tq = GB_TQ
    nq = n // tq
    rows = GROUP * tq

    qb_step = 8 if nq % 8 == 0 else (4 if nq % 4 == 0 else 1)

    def body(q_ref, k_ref, v_ref, do_ref, lse_ref, dl_ref, dq_ref, dk_ref, dv_ref):
        j = pl.program_id(1)
        i = pl.program_id(2)

        @pl.when(i == 0)
        def _():
            dk_ref[...] = jnp.zeros_like(dk_ref)
            dv_ref[...] = jnp.zeros_like(dv_ref)

        @pl.when(j == 0)
        def _():
            for s in range(qb_step):
                dq_ref[qb_step * i + s] = jnp.zeros((HEAD_DIM, rows), F32)

        blocks = [(s, hh) for s in range(qb_step) for hh in range(GROUP)]
        q = jnp.concatenate([q_ref[hh, s * tq:(s + 1) * tq, :] for s, hh in blocks], axis=0)
        do = jnp.concatenate([do_ref[s * tq:(s + 1) * tq, hh * HEAD_DIM:(hh + 1) * HEAD_DIM] for s, hh in blocks], axis=0)
        lse_row = jnp.concatenate([lse_ref[s] for s in range(qb_step)], axis=1)
        dl_row = jnp.concatenate([dl_ref[s] for s in range(qb_step)], axis=1)
        ks, vs = k_ref[...], v_ref[...]
        st = lax.dot_general(ks, q, NT, preferred_element_type=F32)
        p = jnp.exp(st - lse_row)
        dpt = lax.dot_general(vs, do, NT, preferred_element_type=F32)
        ds = (p * (dpt - dl_row)).astype(BF16)
        dv_part = lax.dot_general(p.astype(BF16), do, NN, preferred_element_type=F32)
        dk_part = lax.dot_general(ds, q, NN, preferred_element_type=F32)
        dq_part = lax.dot_general(ks, ds, TN, preferred_element_type=F32)

        dk_ref[...] += dk_part
        dv_ref[...] += dv_part
        for s in range(qb_step):
            dq_ref[qb_step * i + s] += dq_part[:, s * rows:(s + 1) * rows]

    kv = pl.BlockSpec((None, GB_TK, HEAD_DIM), lambda g, j, i: (g, j, 0))
    row = pl.BlockSpec((None, qb_step, 1, rows), lambda g, j, i: (g, i, 0, 0))
    return pl.pallas_call(
        body, name="attn_b_bwd", grid=(KV_B, nk, nq // qb_step),
        in_specs=[pl.BlockSpec((GROUP, qb_step * tq, HEAD_DIM), lambda g, j, i: (g, i, 0)), kv, kv,
                  pl.BlockSpec((qb_step * tq, GROUP * HEAD_DIM), lambda g, j, i: (i, KV_A + g)), row, row],
        out_specs=[pl.BlockSpec((None, nq, HEAD_DIM, rows), lambda g, j, i: (g, 0, 0, 0)), kv, kv],
        out_shape=[jax.ShapeDtypeStruct((KV_B, nq, HEAD_DIM, rows), F32),
                   jax.ShapeDtypeStruct((KV_B, t, HEAD_DIM), F32),
                   jax.ShapeDtypeStruct((KV_B, t, HEAD_DIM), F32)],
        compiler_params=_params(("parallel", "arbitrary", "arbitrary")),
    )(qb, kb, vb, dheads, lse, delta)


def _delta_rows(dheads, heads):
    n = heads.shape[0]
    tq = GB_TQ
    w = GROUP * HEAD_DIM

    def body(a_ref, b_ref, o_ref):
        prod = a_ref[...].astype(F32) * b_ref[...].astype(F32)
        cols = [jnp.sum(prod[:, hh * HEAD_DIM:(hh + 1) * HEAD_DIM].T, axis=0, keepdims=True) for hh in range(GROUP)]
        o_ref[...] = jnp.concatenate(cols, axis=1)

    blk = pl.BlockSpec((tq, w), lambda g, i: (i, KV_A + g))
    return pl.pallas_call(
        body, name="delta_rows", grid=(KV_B, n // tq),
        in_specs=[blk, blk],
        out_specs=pl.BlockSpec((None, None, 1, GROUP * tq), lambda g, i: (g, i, 0, 0)),
        out_shape=jax.ShapeDtypeStruct((KV_B, n // tq, 1, GROUP * tq), F32),
        compiler_params=_params(("parallel", "parallel")),
    )(dheads, heads)


KWIN = 3 * BLOCK


def _window_scores(q, k_ref, j, n, nb, sink_row):
    c = k_ref.shape[0] - n
    start = pl.multiple_of(jnp.clip(j - 1, 0, nb - 3) * BLOCK, BLOCK)
    kw = k_ref[pl.ds(start, KWIN), :]
    kc = k_ref[pl.ds(n, c), :]
    s_loc = lax.dot_general(kw, q, NT, preferred_element_type=F32)
    s_ctx = lax.dot_general(kc, q, NT, preferred_element_type=F32)
    cols = GROUP * BLOCK
    qpos = j * BLOCK + lax.broadcasted_iota(jnp.int32, (KWIN, cols), 1) % BLOCK
    kpos = start + lax.broadcasted_iota(jnp.int32, (KWIN, cols), 0)
    s_loc = jnp.where(jnp.abs(qpos - kpos) <= WINDOW, s_loc, -jnp.inf)
    m = jnp.maximum(jnp.maximum(jnp.max(s_loc, axis=0, keepdims=True), jnp.max(s_ctx, axis=0, keepdims=True)),
                    sink_row)
    e_loc, e_ctx, e_sink = jnp.exp(s_loc - m), jnp.exp(s_ctx - m), jnp.exp(sink_row - m)
    inv = 1.0 / (jnp.sum(e_loc, axis=0, keepdims=True) + jnp.sum(e_ctx, axis=0, keepdims=True) + e_sink)
    return e_loc * inv, e_ctx * inv, e_sink * inv, start


def _sink_row(sink_ref, g):
    return jnp.concatenate([sink_ref[pl.ds(g * GROUP + hh, 1), :] for hh in range(GROUP)], axis=1)


def _attn_a_fwd(qa, ka, va, sink_b, heads_b, n):
    t = ka.shape[1]
    nb = n // BLOCK
    assert nb >= 3
    wb = 8 if nb % 8 == 0 else (4 if nb % 4 == 0 else 1)

    def body(q_ref, k_ref, v_ref, sink_ref, heads_b_ref, o_ref):
        g, jj = pl.program_id(0), pl.program_id(1)
        for s in range(wb):
            j = jj * wb + s
            rows = slice(s * BLOCK, (s + 1) * BLOCK)
            q = q_ref[:, rows, :].reshape(GROUP * BLOCK, HEAD_DIM)
            p_loc, p_ctx, _, start = _window_scores(q, k_ref, j, n, nb, _sink_row(sink_ref, g))
            vw = v_ref[pl.ds(start, KWIN), :]
            vc = v_ref[pl.ds(n, t - n), :]
            ot = (lax.dot_general(vw, p_loc.astype(BF16), TN, preferred_element_type=F32)
                  + lax.dot_general(vc, p_ctx.astype(BF16), TN, preferred_element_type=F32))
            for hh in range(GROUP):
                o_ref[rows, hh * HEAD_DIM:(hh + 1) * HEAD_DIM] = ot[:, hh * BLOCK:(hh + 1) * BLOCK].T.astype(BF16)

    return pl.pallas_call(
        body, name="attn_a_fwd", grid=(KV_A, nb // wb),
        in_specs=[pl.BlockSpec((GROUP, wb * BLOCK, HEAD_DIM), lambda g, j: (g, j, 0)),
                  pl.BlockSpec((None, t, HEAD_DIM), lambda g, j: (g, 0, 0)),
                  pl.BlockSpec((None, t, HEAD_DIM), lambda g, j: (g, 0, 0)),
                  pl.BlockSpec((HEADS_A, HEAD_DIM), lambda g, j: (0, 0)),
                  pl.BlockSpec(memory_space=pl.ANY)],
        out_specs=pl.BlockSpec((wb * BLOCK, GROUP * HEAD_DIM), lambda g, j: (j, g)),
        out_shape=jax.ShapeDtypeStruct((n, MIX_WIDTH), BF16),
        input_output_aliases={4: 0},
        compiler_params=_params(("parallel", "parallel")),
    )(qa, ka, va, sink_b, heads_b)


def _attn_a_bwd(qa, ka, va, sink_b, dheads, n):
    t = ka.shape[1]
    c = t - n
    nb = n // BLOCK
    wb = 8 if nb % 8 == 0 else (4 if nb % 4 == 0 else 1)

    def body(q_ref, k_ref, v_ref, sink_ref, do_ref, dq_ref, dk_ref, dv_ref, dsink_ref):
        g, jj = pl.program_id(0), pl.program_id(1)

        @pl.when(jj == 0)
        def _():
            dk_ref[...] = jnp.zeros_like(dk_ref)
            dv_ref[...] = jnp.zeros_like(dv_ref)
            dsink_ref[...] = jnp.zeros_like(dsink_ref)

        for s in range(wb):
            j = jj * wb + s
            rows = slice(s * BLOCK, (s + 1) * BLOCK)
            q = q_ref[:, rows, :].reshape(GROUP * BLOCK, HEAD_DIM)
            do = _heads_rows(do_ref[rows, :], BLOCK)
            p_loc, p_ctx, p_sink, start = _window_scores(q, k_ref, j, n, nb, _sink_row(sink_ref, g))
            kw, vw = k_ref[pl.ds(start, KWIN), :], v_ref[pl.ds(start, KWIN), :]
            kc, vc = k_ref[pl.ds(n, c), :], v_ref[pl.ds(n, c), :]
            dp_loc = lax.dot_general(vw, do, NT, preferred_element_type=F32)
            dp_ctx = lax.dot_general(vc, do, NT, preferred_element_type=F32)
            dl = jnp.sum(p_loc * dp_loc, axis=0, keepdims=True) + jnp.sum(p_ctx * dp_ctx, axis=0, keepdims=True)
            ds_loc = (p_loc * (dp_loc - dl)).astype(BF16)
            ds_ctx = (p_ctx * (dp_ctx - dl)).astype(BF16)
            dqt = (lax.dot_general(kw, ds_loc, TN, preferred_element_type=F32)
                   + lax.dot_general(kc, ds_ctx, TN, preferred_element_type=F32))
            for hh in range(GROUP):
                dq_ref[hh, rows, :] = dqt[:, hh * BLOCK:(hh + 1) * BLOCK].T
            dk_ref[pl.ds(start, KWIN), :] += lax.dot_general(ds_loc, q, NN, preferred_element_type=F32)
            dv_ref[pl.ds(start, KWIN), :] += lax.dot_general(p_loc.astype(BF16), do, NN, preferred_element_type=F32)
            dk_ref[pl.ds(n, c), :] += lax.dot_general(ds_ctx, q, NN, preferred_element_type=F32)
            dv_ref[pl.ds(n, c), :] += lax.dot_general(p_ctx.astype(BF16), do, NN, preferred_element_type=F32)
            dsk = -(p_sink * dl)
            upd = [jnp.broadcast_to(jnp.sum(dsk[:, hh * BLOCK:(hh + 1) * BLOCK], axis=1, keepdims=True), (1, HEAD_DIM))
                   for hh in range(GROUP)]
            dsink_ref[...] += jnp.concatenate(upd + [jnp.zeros((8 - GROUP, HEAD_DIM), F32)], axis=0)

    res = pl.BlockSpec((None, t, HEAD_DIM), lambda g, j: (g, 0, 0))
    return pl.pallas_call(
        body, name="attn_a_bwd", grid=(KV_A, nb // wb),
        in_specs=[pl.BlockSpec((GROUP, wb * BLOCK, HEAD_DIM), lambda g, j: (g, j, 0)), res, res,
                  pl.BlockSpec((HEADS_A, HEAD_DIM), lambda g, j: (0, 0)),
                  pl.BlockSpec((wb * BLOCK, GROUP * HEAD_DIM), lambda g, j: (j, g))],
        out_specs=[pl.BlockSpec((GROUP, wb * BLOCK, HEAD_DIM), lambda g, j: (g, j, 0)), res, res,
                   pl.BlockSpec((None, 8, HEAD_DIM), lambda g, j: (g, 0, 0))],
        out_shape=[jax.ShapeDtypeStruct((HEADS_A, n, HEAD_DIM), F32),
                   jax.ShapeDtypeStruct((KV_A, t, HEAD_DIM), F32),
                   jax.ShapeDtypeStruct((KV_A, t, HEAD_DIM), F32),
                   jax.ShapeDtypeStruct((KV_A, 8, HEAD_DIM), F32)],
        compiler_params=_params(("parallel", "arbitrary")),
    )(qa, ka, va, sink_b, dheads)


def _ln_stats(r):
    mu = jnp.mean(r, axis=-1, keepdims=True)
    cen = r - mu
    rstd = lax.rsqrt(jnp.mean(cen * cen, axis=-1, keepdims=True) + EPS)
    return cen * rstd, rstd


def _ln_bwd(dy, xhat, rstd, gain):
    dxh = dy * gain
    return rstd * (dxh - jnp.mean(dxh, axis=-1, keepdims=True)
                   - xhat * jnp.mean(dxh * xhat, axis=-1, keepdims=True))


def _accumulate_rows(ref, rows, i):
    pad = [jnp.zeros_like(rows[0])] * (8 - len(rows))
    upd = jnp.concatenate(rows + pad, axis=0)

    @pl.when(i == 0)
    def _():
        ref[...] = upd

    @pl.when(i != 0)
    def _():
        ref[...] += upd


def _colsum(v):
    return jnp.sum(v, axis=0, keepdims=True)


LN_TILE = 256


def _res_ln1(x, a, vec):
    n, d = x.shape

    def body(x_ref, a_ref, v_ref, xh_ref, rs_ref, u_ref):
        r1 = DN_ALPHA * x_ref[...] + v_ref[0:1, :] * a_ref[...]
        xhat, rstd = _ln_stats(r1)
        xh_ref[...] = xhat
        rs_ref[...] = rstd
        x1 = xhat * v_ref[1:2, :] + v_ref[2:3, :]
        u_ref[...] = (x1 * (1.0 + v_ref[3:4, :]) + v_ref[4:5, :]).astype(BF16)

    row = pl.BlockSpec((LN_TILE, d), lambda i: (i, 0))
    return pl.pallas_call(
        body, name="res_ln1", grid=(n // LN_TILE,),
        in_specs=[row, row, pl.BlockSpec((8, d), lambda i: (0, 0))],
        out_specs=[row, pl.BlockSpec((LN_TILE, 1), lambda i: (i, 0)), row],
        out_shape=[jax.ShapeDtypeStruct((n, d), F32), jax.ShapeDtypeStruct((n, 1), F32),
                   jax.ShapeDtypeStruct((n, d), BF16)],
        compiler_params=_params(("parallel",)),
    )(x, a, vec)


def _res_ln2_loss(xhat1, f, target, vec):
    n, d = f.shape

    def body(xh_ref, f_ref, t_ref, v_ref, dr_ref, df_ref, s_ref):
        i = pl.program_id(0)
        x1 = xh_ref[...] * v_ref[1:2, :] + v_ref[2:3, :]
        fv = f_ref[...]
        xhat, rstd = _ln_stats(DN_ALPHA * x1 + v_ref[0:1, :] * fv)
        err = xhat * v_ref[3:4, :] + v_ref[4:5, :] - t_ref[...]
        dy = err * (1.0 / d)
        dr2 = _ln_bwd(dy, xhat, rstd, v_ref[3:4, :])
        dr_ref[...] = dr2
        df_ref[...] = (dr2 * v_ref[0:1, :]).astype(BF16)
        _accumulate_rows(s_ref, [_colsum(dy * xhat), _colsum(dy), _colsum(dr2 * fv),
                                 _colsum(err * err) * (0.5 / d)], i)

    row = pl.BlockSpec((LN_TILE, d), lambda i: (i, 0))
    return pl.pallas_call(
        body, name="res_ln2_loss", grid=(n // LN_TILE,),
        in_specs=[row, row, row, pl.BlockSpec((8, d), lambda i: (0, 0))],
        out_specs=[row, row, pl.BlockSpec((8, d), lambda i: (0, 0))],
        out_shape=[jax.ShapeDtypeStruct((n, d), F32), jax.ShapeDtypeStruct((n, d), BF16),
                   jax.ShapeDtypeStruct((8, d), F32)],
        compiler_params=_params(("arbitrary",)),
    )(xhat1, f, target, vec)


def _ln1_bwd(du2, dr2, xhat1, rstd1, a, vec):
    n, d = du2.shape

    def body(du_ref, dr2_ref, xh_ref, rs_ref, a_ref, v_ref, dxp_ref, da_ref, s_ref):
        i = pl.program_id(0)
        du, xhat = du_ref[...], xh_ref[...]
        x1 = xhat * v_ref[1:2, :] + v_ref[2:3, :]
        dx1 = DN_ALPHA * dr2_ref[...] + du * (1.0 + v_ref[0:1, :])
        dr1 = _ln_bwd(dx1, xhat, rs_ref[...], v_ref[1:2, :])
        dxp_ref[...] = DN_ALPHA * dr1
        da_ref[...] = (dr1 * v_ref[3:4, :]).astype(BF16)
        _accumulate_rows(s_ref, [_colsum(du * x1), _colsum(du), _colsum(dx1 * xhat), _colsum(dx1),
                                 _colsum(dr1 * a_ref[...])], i)

    row = pl.BlockSpec((LN_TILE, d), lambda i: (i, 0))
    return pl.pallas_call(
        body, name="ln1_bwd", grid=(n // LN_TILE,),
        in_specs=[row, row, row, pl.BlockSpec((LN_TILE, 1), lambda i: (i, 0)), row,
                  pl.BlockSpec((8, d), lambda i: (0, 0))],
        out_specs=[row, row, pl.BlockSpec((8, d), lambda i: (0, 0))],
        out_shape=[jax.ShapeDtypeStruct((n, d), F32), jax.ShapeDtypeStruct((n, d), BF16),
                   jax.ShapeDtypeStruct((8, d), F32)],
        compiler_params=_params(("arbitrary",)),
    )(du2, dr2, xhat1, rstd1, a, vec)


def _mod1_bwd(du_all, dxp, x, ctx, mods):
    n, d = x.shape
    nx = n // ROW_TILE

    def body(du_ref, dxp_ref, x_ref, ctx_ref, m_ref, gx_ref, s_ref):
        i = pl.program_id(0)
        du = du_ref[...]
        zero = jnp.zeros((1, d), F32)

        @pl.when(i == 0)
        def _():
            s_ref[...] = jnp.zeros_like(s_ref)

        @pl.when(i < nx)
        def _():
            gx_ref[...] = dxp_ref[...] + du * (1.0 + m_ref[0:1, :])
            s_ref[...] += jnp.concatenate([_colsum(du * x_ref[...]), _colsum(du)] + [zero] * 6, axis=0)

        @pl.when(i >= nx)
        def _():
            s_ref[...] += jnp.concatenate([zero, zero, _colsum(du * ctx_ref[...]), _colsum(du)] + [zero] * 4, axis=0)

    lat = pl.BlockSpec((ROW_TILE, d), lambda i: (jnp.minimum(i, nx - 1), 0))
    return pl.pallas_call(
        body, name="mod1_bwd", grid=(nx + 1,),
        in_specs=[pl.BlockSpec((ROW_TILE, d), lambda i: (i, 0)), lat, lat,
                  pl.BlockSpec((ROW_TILE, d), lambda i: (0, 0)), pl.BlockSpec((8, d), lambda i: (0, 0))],
        out_specs=[lat, pl.BlockSpec((8, d), lambda i: (0, 0))],
        out_shape=[jax.ShapeDtypeStruct((n, d), F32), jax.ShapeDtypeStruct((8, d), F32)],
        compiler_params=_params(("arbitrary",)),
    )(du_all, dxp, x, ctx, mods)


FFN_TM = 1024
FFN_TN = 512


def _gate_up(u2, wg, wu, after):
    n, d = u2.shape
    f = wg.shape[1]

    def body(u_ref, wg_ref, wu_ref, after_ref, g_ref, up_ref, h_ref):
        u = u_ref[...]
        g = lax.dot_general(u, wg_ref[...], NN, preferred_element_type=F32)
        up = lax.dot_general(u, wu_ref[...], NN, preferred_element_type=F32)
        g_ref[...] = g.astype(BF16)
        up_ref[...] = up.astype(BF16)
        h_ref[...] = (g * jax.nn.sigmoid(g) * up).astype(BF16)

    tm = min(FFN_TM, n)
    wspec = pl.BlockSpec((d, FFN_TN), lambda j, i: (0, j))
    ospec = pl.BlockSpec((tm, FFN_TN), lambda j, i: (i, j))
    return pl.pallas_call(
        body, name="gate_up", grid=(f // FFN_TN, n // tm),
        in_specs=[pl.BlockSpec((tm, d), lambda j, i: (i, 0)), wspec, wspec, pl.BlockSpec(memory_space=pl.ANY)],
        out_specs=[ospec, ospec, ospec],
        out_shape=[jax.ShapeDtypeStruct((n, f), BF16)] * 3,
        compiler_params=_params(("parallel", "parallel")),
    )(u2, wg, wu, after)


def _glu_bwd(df, wd, g, u):
    n, d = df.shape
    f = wd.shape[0]

    def body(df_ref, wd_ref, g_ref, u_ref, dg_ref, du_ref):
        chunk = min(256, df_ref.shape[0])
        for r in range(df_ref.shape[0] // chunk):
            rows = slice(r * chunk, (r + 1) * chunk)
            dh = lax.dot_general(df_ref[rows, :], wd_ref[...], NT, preferred_element_type=F32)
            gv = g_ref[rows, :].astype(F32)
            sig = jax.nn.sigmoid(gv)
            du_ref[rows, :] = (dh * (gv * sig)).astype(BF16)
            dg_ref[rows, :] = (dh * u_ref[rows, :].astype(F32) * (sig * (1.0 + gv * (1.0 - sig)))).astype(BF16)

    tm = min(FFN_TM, n)
    ospec = pl.BlockSpec((tm, FFN_TN), lambda i, j: (i, j))
    return pl.pallas_call(
        body, name="glu_bwd", grid=(n // tm, f // FFN_TN),
        in_specs=[pl.BlockSpec((tm, d), lambda i, j: (i, 0)),
                  pl.BlockSpec((FFN_TN, d), lambda i, j: (j, 0)), ospec, ospec],
        out_specs=[ospec, ospec],
        out_shape=[jax.ShapeDtypeStruct((n, f), BF16), jax.ShapeDtypeStruct((n, f), BF16)],
        compiler_params=_params(("parallel", "parallel")),
    )(df, wd, g, u)


def _du2(dg, du, wg, wu):
    n, f = dg.shape
    d = wg.shape[0]
    tm, tn, tk = min(1024, n), 1024, 1408
    nk = f // tk

    def body(dg_ref, du_ref, wg_ref, wu_ref, o_ref):
        @pl.when(pl.program_id(2) == 0)
        def _():
            o_ref[...] = jnp.zeros_like(o_ref)

        o_ref[...] += (lax.dot_general(dg_ref[...], wg_ref[...], NT, preferred_element_type=F32)
                       + lax.dot_general(du_ref[...], wu_ref[...], NT, preferred_element_type=F32))

    aspec = pl.BlockSpec((tm, tk), lambda i, j, kk: (i, kk))
    wspec = pl.BlockSpec((tn, tk), lambda i, j, kk: (j, kk))
    return pl.pallas_call(
        body, name="du2", grid=(n // tm, d // tn, nk),
        in_specs=[aspec, aspec, wspec, wspec],
        out_specs=pl.BlockSpec((tm, tn), lambda i, j, kk: (i, j)),
        out_shape=jax.ShapeDtypeStruct((n, d), F32),
        compiler_params=_params(("parallel", "parallel", "arbitrary")),
    )(dg, du, wg, wu)


def _rows8(rows, d=D_MODEL):
    rows = [r.reshape(1, d).astype(F32) for r in rows]
    return jnp.concatenate(rows + [jnp.zeros((8 - len(rows), d), F32)], axis=0)


W_GROUPS = (("w_in",), ("w_out", "w_gate", "w_up"), ("w_down",))
G_GROUPS = (("w_down", "w_gate", "w_up"), ("w_out",), ("w_in",))


def _layer_fwd_bwd(x, ctx, target, mod, mod_ctx, weights, prefetch, grads_out,
                   q_g, k_g, sink, ln1_g, ln1_b, ln2_g, ln2_b):
    n, d = x.shape
    c = ctx.shape[0]
    sh1, sc1, g1, sh2, sc2, g2 = [mod[:, k * d:(k + 1) * d] for k in range(6)]
    csh1, csc1 = mod_ctx[:, 0:d], mod_ctx[:, d:2 * d]
    cos, sin_a, sin_b = _rope_tables(n, c)
    sink_b = jnp.broadcast_to(sink.reshape(HEADS_A, 1), (HEADS_A, HEAD_DIM)).astype(F32)

    u_all = _modulate_rows(x, ctx, _rows8([sc1, sh1, csc1, csh1]))
    (w_in,) = weights(0, u_all)
    h_all = _matmul(u_all, w_in, name="qkv_proj", tm=_fit(n + c, 1088), tn=1024, tk=2048, out_dtype=F32)
    qa, ka, va, qb, kb, vb = _qkv_post(h_all, cos, sin_a, sin_b, q_g, k_g)
    heads_b, lse = _attn_b_fwd(qb, kb, vb, n)
    zero = prefetch(1, heads_b)
    heads = _attn_a_fwd(qa, ka, va, sink_b + zero, heads_b, n)
    w_out, w_gate, w_up = weights(1, heads)
    a = _matmul(heads, w_out, name="out_proj", tm=1024, tn=1024, tk=2048, out_dtype=F32)
    xhat1, rstd1, u2 = _res_ln1(x, a, _rows8([g1, ln1_g, ln1_b, sc2, sh2]))
    zero = prefetch(2, u2)
    gg, uu, hh = _gate_up(u2, w_gate, w_up, zero.reshape(1, 1))
    (w_down,) = weights(2, hh)
    f = _matmul(hh, w_down, name="ffn_down", tm=1024, tn=512, tk=FFN, out_dtype=F32)
    dr2, df, s_ln2 = _res_ln2_loss(xhat1, f, target, _rows8([g2, ln1_g, ln1_b, ln2_g, ln2_b]))

    dgg, duu = _glu_bwd(df, w_down, gg, uu)
    dw_down = _matmul(hh, df, name="dw_down", ta=True, tm=512, tn=1024, tk=n, out_dtype=BF16)
    dw_gate = _matmul(u2, dgg, name="dw_gate", ta=True, tm=1024, tn=512, tk=n, out_dtype=BF16)
    dw_up = _matmul(u2, duu, name="dw_up", ta=True, tm=1024, tn=512, tk=n, out_dtype=BF16)
    zero = grads_out(0, [dw_down, dw_gate, dw_up])
    du2 = _du2(dgg, duu, w_gate, w_up)
    dxp, da, s_ln1 = _ln1_bwd(du2, dr2, xhat1, rstd1, a, _rows8([sc2, ln1_g, ln1_b, g1]) + zero)

    dheads = _matmul(da, w_out, name="d_heads", tb=True, tm=1024, tn=1024, tk=2048, out_dtype=BF16)
    dw_out = _matmul(heads, da, name="dw_out", ta=True, tm=1024, tn=1024, tk=n, out_dtype=BF16)
    zero = grads_out(1, [dw_out])
    delta = _delta_rows(dheads, heads)
    dqa, dka, dva, dsink = _attn_a_bwd(qa, ka, va, sink_b + zero, dheads, n)
    dqb, dkb, dvb = _attn_b_bwd(qb, kb, vb, dheads, lse, delta, n)
    dh_all, s_gain = _qkv_bwd_post(h_all, cos, sin_a, sin_b, q_g, k_g, dqa, dka, dva, dqb, dkb, dvb, n)
    dw_in = _matmul(u_all, dh_all, name="dw_in", ta=True, tm=1024, tn=1024, tk=n + c, out_dtype=BF16)
    zero = grads_out(2, [dw_in])
    du_all = _matmul(dh_all, w_in, name="d_u1", tb=True, tm=_fit(n + c, 1088), tn=1024, tk=IN_WIDTH, out_dtype=F32,
                     after=zero.reshape(1, 1))
    grad_x, s_mod1 = _mod1_bwd(du_all, dxp, x, ctx, _rows8([sc1]) + zero)

    dsink_row = jnp.concatenate([dsink[0, 0:GROUP, 0], dsink[1, 0:GROUP, 0]]).reshape(1, HEADS_A)
    misc = jnp.concatenate([s_gain[0:1], s_gain[1:2], dsink_row,
                            jnp.zeros((1, d - 2 * HEAD_DIM - HEADS_A), F32)], axis=1)
    partial = jnp.concatenate([
        s_mod1[1:2], s_mod1[0:1], s_ln1[4:5],
        s_ln1[1:2], s_ln1[0:1], s_ln2[2:3],
        s_mod1[3:4], s_mod1[2:3],
        s_ln1[2:3], s_ln1[3:4], s_ln2[0:1], s_ln2[1:2],
        s_ln2[3:4], misc, jnp.zeros((2, d), F32)], axis=0)
    return grad_x, partial


ANY = pl.BlockSpec(memory_space=pl.ANY)
VMEM_FULL = pl.BlockSpec(memory_space=pltpu.VMEM)
N_CHIP_PEERS = 3


def _me():
    return lax.axis_index("x"), lax.axis_index("y"), lax.axis_index("c")


def _other_chips(x, y):
    return [(1 - x, y), (x, 1 - y), (1 - x, 1 - y)]


def _shard_of(chip):
    return 2 * chip[0] + chip[1]


def _dev_index(x, y, c):
    return 4 * x + 2 * y + c


def _rcopy(src, dst, send_sems, recv_sems, k, dev):
    return pltpu.make_async_remote_copy(src_ref=src, dst_ref=dst, send_sem=send_sems.at[k], recv_sem=recv_sems.at[k],
                                        device_id=dev, device_id_type=MESH)


BIG = (("w_in", (D_MODEL, IN_WIDTH), 1), ("w_out", (MIX_WIDTH, D_MODEL), 0), ("w_gate", (D_MODEL, FFN), 1),
       ("w_up", (D_MODEL, FFN), 1), ("w_down", (FFN, D_MODEL), 0))


def _sub(ref, axis, idx, size):
    start = pl.multiple_of(idx * size, size)
    return ref.at[pl.ds(start, size), :] if axis == 0 else ref.at[:, pl.ds(start, size)]


def _shape_div(shape, axis, parts):
    return tuple(s // parts if a == axis else s for a, s in enumerate(shape))


def _piece(a, ref, shard, half):
    _, full, axis = BIG[a]
    view = _sub(ref, axis, shard, full[axis] // N_SHARD)
    return _sub(view, 1 - axis, half, full[1 - axis] // 2)


HBM = pl.BlockSpec(memory_space=pltpu.HBM)
SEM = pl.BlockSpec(memory_space=pltpu.SEMAPHORE)
EFFECT = pltpu.SideEffectType.DATAFLOW_SIDE_EFFECTING
BIG_INDEX = {name: a for a, (name, _, _) in enumerate(BIG)}


def _in_hbm(arr):
    return pltpu.with_memory_space_constraint(arr, pltpu.HBM)


def _gather_start(tag, arrs, bufs, prev):
    n_arr = len(arrs)

    def body(*refs):
        ins = refs[:n_arr]
        send_sems, recv_sems = refs[n_arr + 1], refs[n_arr + 2]
        token = refs[-1]
        x, y, c = _me()
        s_me = _shard_of((x, y))
        for i, a in enumerate(arrs):
            mine = _piece(a, ins[i], s_me, c)
            for j, chip in enumerate(_other_chips(x, y)):
                _rcopy(mine, mine, send_sems, recv_sems, N_CHIP_PEERS * i + j, (*chip, c)).start()
        token[...] = jnp.zeros_like(token)

    n_sem = N_CHIP_PEERS * n_arr
    outs = pl.pallas_call(
        body, name="gather_start_" + tag,
        in_specs=[HBM] * n_arr + [ANY],
        out_specs=[SEM, SEM] + [HBM] * n_arr + [VMEM_FULL],
        out_shape=[pltpu.SemaphoreType.DMA((n_sem,)), pltpu.SemaphoreType.DMA((n_sem,))]
        + [pltpu.HBM(BIG[a][1], BF16) for a in arrs] + [jax.ShapeDtypeStruct((8, HEAD_DIM), F32)],
        input_output_aliases={i: 2 + i for i in range(n_arr)},
        compiler_params=pltpu.CompilerParams(has_side_effects=EFFECT),
    )(*[_in_hbm(b) for b in bufs], prev)
    return outs[0], outs[1], list(outs[2:2 + n_arr]), outs[-1]


def _gather_wait(tag, arrs, send_sems, recv_sems, bufs, after):
    n_arr = len(arrs)

    def body(*refs):
        ins = refs[:n_arr]
        send_sems_, recv_sems_ = refs[n_arr], refs[n_arr + 1]
        x, y, c = _me()
        s_me = _shard_of((x, y))
        for i, a in enumerate(arrs):
            mine = _piece(a, ins[i], s_me, c)
            for j, chip in enumerate(_other_chips(x, y)):
                landed = _piece(a, ins[i], _shard_of(chip), c)
                cp = _rcopy(mine, landed, send_sems_, recv_sems_, N_CHIP_PEERS * i + j, (*chip, c))
                cp.wait_send()
                cp.wait_recv()

    outs = pl.pallas_call(
        body, name="gather_wait_" + tag,
        in_specs=[HBM] * n_arr + [SEM, SEM, ANY],
        out_specs=[HBM] * n_arr,
        out_shape=[pltpu.HBM(BIG[a][1], BF16) for a in arrs],
        input_output_aliases={i: i for i in range(n_arr)},
        compiler_params=pltpu.CompilerParams(has_side_effects=EFFECT),
    )(*bufs, send_sems, recv_sems, after)
    return list(outs)


def _gather_forward(tag, arrs, bufs):
    n_arr = len(arrs)

    def body(*refs):
        outs = refs[n_arr:2 * n_arr]
        send_sems, recv_sems = refs[2 * n_arr:]
        x, y, c = _me()
        sibling = (x, y, 1 - c)
        chips = _other_chips(x, y)
        copies = []
        for i, a in enumerate(arrs):
            for j, chip in enumerate(chips):
                landed = _piece(a, outs[i], _shard_of(chip), c)
                cp = _rcopy(landed, landed, send_sems, recv_sems, N_CHIP_PEERS * i + j, sibling)
                cp.start()
                copies.append(cp)
        for i, a in enumerate(arrs):
            for j, chip in enumerate(chips):
                other = _piece(a, outs[i], _shard_of(chip), 1 - c)
                _rcopy(other, other, send_sems, recv_sems, N_CHIP_PEERS * i + j, sibling).wait_recv()
        for cp in copies:
            cp.wait_send()

    n_sem = N_CHIP_PEERS * n_arr
    return list(pl.pallas_call(
        body, name="gather_forward_" + tag,
        in_specs=[ANY] * n_arr, out_specs=[ANY] * n_arr,
        out_shape=[jax.ShapeDtypeStruct(BIG[a][1], BF16) for a in arrs],
        input_output_aliases={i: i for i in range(n_arr)},
        scratch_shapes=[pltpu.SemaphoreType.DMA((n_sem,)), pltpu.SemaphoreType.DMA((n_sem,))],
    )(*bufs))


def _forward_start(tag, arrs, bufs):
    n_arr = len(arrs)

    def body(*refs):
        ins = refs[:n_arr]
        send_sems, recv_sems = refs[n_arr], refs[n_arr + 1]
        token = refs[-1]
        x, y, c = _me()
        for i, a in enumerate(arrs):
            for j, chip in enumerate(_other_chips(x, y)):
                landed = _piece(a, ins[i], _shard_of(chip), c)
                _rcopy(landed, landed, send_sems, recv_sems, N_CHIP_PEERS * i + j, (x, y, 1 - c)).start()
        token[...] = jnp.zeros_like(token)

    n_sem = N_CHIP_PEERS * n_arr
    outs = pl.pallas_call(
        body, name="gather_forward_start_" + tag,
        in_specs=[HBM] * n_arr,
        out_specs=[SEM, SEM] + [HBM] * n_arr + [VMEM_FULL],
        out_shape=[pltpu.SemaphoreType.DMA((n_sem,)), pltpu.SemaphoreType.DMA((n_sem,))]
        + [pltpu.HBM(BIG[a][1], BF16) for a in arrs] + [jax.ShapeDtypeStruct((8, HEAD_DIM), F32)],
        input_output_aliases={i: 2 + i for i in range(n_arr)},
        compiler_params=pltpu.CompilerParams(has_side_effects=EFFECT),
    )(*bufs)
    return outs[0], outs[1], list(outs[2:2 + n_arr]), outs[-1]


def _forward_wait(tag, arrs, send_sems, recv_sems, bufs, after):
    n_arr = len(arrs)

    def body(*refs):
        ins = refs[:n_arr]
        send_sems_, recv_sems_ = refs[n_arr], refs[n_arr + 1]
        x, y, c = _me()
        for i, a in enumerate(arrs):
            for j, chip in enumerate(_other_chips(x, y)):
                mine = _piece(a, ins[i], _shard_of(chip), c)
                other = _piece(a, ins[i], _shard_of(chip), 1 - c)
                cp = _rcopy(mine, other, send_sems_, recv_sems_, N_CHIP_PEERS * i + j, (x, y, 1 - c))
                cp.wait_send()
                cp.wait_recv()

    outs = pl.pallas_call(
        body, name="gather_forward_wait_" + tag,
        in_specs=[HBM] * n_arr + [SEM, SEM, ANY],
        out_specs=[HBM] * n_arr,
        out_shape=[pltpu.HBM(BIG[a][1], BF16) for a in arrs],
        input_output_aliases={i: i for i in range(n_arr)},
        compiler_params=pltpu.CompilerParams(has_side_effects=EFFECT),
    )(*bufs, send_sems, recv_sems, after)
    return list(outs)


def _peers(x, y, c):
    return [(x ^ (mask >> 2), y ^ ((mask >> 1) & 1), c ^ (mask & 1)) for mask in range(1, N_DEV)]


def _received_shape(a):
    _, full, axis = BIG[a]
    return (N_DEV - 1,) + _shape_div(_shape_div(full, 1 - axis, 2), axis, N_SHARD)


def _pieces_start(tag, arrs, dws):
    n_arr = len(arrs)

    def body(*refs):
        srcs, lands = refs[:n_arr], refs[n_arr:2 * n_arr]
        send_sems, recv_sems = refs[2 * n_arr], refs[2 * n_arr + 1]
        token = refs[-1]
        x, y, c = _me()
        for i, a in enumerate(arrs):
            for k, peer in enumerate(_peers(x, y, c)):
                src = _piece(a, srcs[i], _shard_of(peer[:2]), peer[2])
                _rcopy(src, lands[i].at[k], send_sems, recv_sems, (N_DEV - 1) * i + k, peer).start()
        token[...] = jnp.zeros_like(token)

    n_sem = (N_DEV - 1) * n_arr
    lands = [_in_hbm(lax.empty(_received_shape(a), BF16)) for a in arrs]
    outs = pl.pallas_call(
        body, name="grad_pieces_start_" + tag,
        in_specs=[HBM] * (2 * n_arr),
        out_specs=[SEM, SEM] + [HBM] * (2 * n_arr) + [VMEM_FULL],
        out_shape=[pltpu.SemaphoreType.DMA((n_sem,)), pltpu.SemaphoreType.DMA((n_sem,))]
        + [pltpu.HBM(BIG[a][1], BF16) for a in arrs] + [pltpu.HBM(_received_shape(a), BF16) for a in arrs]
        + [jax.ShapeDtypeStruct((8, HEAD_DIM), F32)],
        input_output_aliases={i: 2 + i for i in range(2 * n_arr)},
        compiler_params=pltpu.CompilerParams(has_side_effects=EFFECT),
    )(*[_in_hbm(dw) for dw in dws], *lands)
    return outs[0], outs[1], list(outs[2:2 + n_arr]), list(outs[2 + n_arr:2 + 2 * n_arr]), outs[-1]


def _pieces_wait(tag, arrs, send_sems, recv_sems, dws, lands, after):
    n_arr = len(arrs)

    def body(*refs):
        srcs, lands_ = refs[:n_arr], refs[n_arr:2 * n_arr]
        send_sems_, recv_sems_ = refs[2 * n_arr], refs[2 * n_arr + 1]
        x, y, c = _me()
        for i, a in enumerate(arrs):
            for k, peer in enumerate(_peers(x, y, c)):
                src = _piece(a, srcs[i], _shard_of(peer[:2]), peer[2])
                cp = _rcopy(src, lands_[i].at[k], send_sems_, recv_sems_, (N_DEV - 1) * i + k, peer)
                cp.wait_send()
                cp.wait_recv()

    outs = pl.pallas_call(
        body, name="grad_pieces_wait_" + tag,
        in_specs=[HBM] * (2 * n_arr) + [SEM, SEM, ANY],
        out_specs=[HBM] * (2 * n_arr),
        out_shape=[pltpu.HBM(BIG[a][1], BF16) for a in arrs] + [pltpu.HBM(_received_shape(a), BF16) for a in arrs],
        input_output_aliases={i: i for i in range(2 * n_arr)},
        compiler_params=pltpu.CompilerParams(has_side_effects=EFFECT),
    )(*dws, *lands, send_sems, recv_sems, after)
    return list(outs[:n_arr]), list(outs[n_arr:])


def _join_start(tag, g_halves):
    n_arr = len(g_halves)

    def body(*refs):
        srcs, lands = refs[:n_arr], refs[n_arr:2 * n_arr]
        send_sems, recv_sems = refs[2 * n_arr], refs[2 * n_arr + 1]
        token = refs[-1]
        x, y, c = _me()
        for i in range(n_arr):
            _rcopy(srcs[i], lands[i], send_sems, recv_sems, i, (x, y, 1 - c)).start()
        token[...] = jnp.zeros_like(token)

    shapes = [pltpu.HBM(g.shape, F32) for g in g_halves]
    outs = pl.pallas_call(
        body, name="grad_join_start_" + tag,
        in_specs=[HBM] * (2 * n_arr),
        out_specs=[SEM, SEM] + [HBM] * (2 * n_arr) + [VMEM_FULL],
        out_shape=[pltpu.SemaphoreType.DMA((n_arr,)), pltpu.SemaphoreType.DMA((n_arr,))] + shapes + shapes
        + [jax.ShapeDtypeStruct((8, HEAD_DIM), F32)],
        input_output_aliases={i: 2 + i for i in range(2 * n_arr)},
        compiler_params=pltpu.CompilerParams(has_side_effects=EFFECT),
    )(*[_in_hbm(g) for g in g_halves], *[_in_hbm(lax.empty(g.shape, F32)) for g in g_halves])
    return outs[0], outs[1], list(outs[2:2 + n_arr]), list(outs[2 + n_arr:2 + 2 * n_arr]), outs[-1]


def _join_wait(tag, send_sems, recv_sems, g_halves, lands, after):
    n_arr = len(g_halves)

    def body(*refs):
        srcs, lands_ = refs[:n_arr], refs[n_arr:2 * n_arr]
        send_sems_, recv_sems_ = refs[2 * n_arr], refs[2 * n_arr + 1]
        x, y, c = _me()
        for i in range(n_arr):
            cp = _rcopy(srcs[i], lands_[i], send_sems_, recv_sems_, i, (x, y, 1 - c))
            cp.wait_send()
            cp.wait_recv()

    shapes = [pltpu.HBM(g.shape, F32) for g in g_halves]
    outs = pl.pallas_call(
        body, name="grad_join_wait_" + tag,
        in_specs=[HBM] * (2 * n_arr) + [SEM, SEM, ANY],
        out_specs=[HBM] * (2 * n_arr),
        out_shape=shapes + shapes,
        input_output_aliases={i: i for i in range(2 * n_arr)},
        compiler_params=pltpu.CompilerParams(has_side_effects=EFFECT),
    )(*g_halves, *lands, send_sems, recv_sems, after)
    return list(outs[:n_arr]), list(outs[n_arr:])


def _piece_sum(a, dw, shard, core, received):
    name, full, axis = BIG[a]
    rows, cols = _received_shape(a)[1:]
    tr = _fit(rows, ROW_TILE)
    nbr = rows // tr

    def body(w_ref, dw_ref, rec_ref, o_ref):
        acc = dw_ref[...].astype(F32)
        for k in range(N_DEV - 1):
            acc = acc + rec_ref[k].astype(F32)
        o_ref[...] = acc

    if axis == 0:
        own = pl.BlockSpec((tr, cols), lambda i, w: (w[0] * nbr + i, w[1]))
    else:
        own = pl.BlockSpec((tr, cols), lambda i, w: (w[1] * nbr + i, w[0]))
    return pl.pallas_call(
        body, name="grad_sum_pieces_" + name,
        grid_spec=pltpu.PrefetchScalarGridSpec(
            num_scalar_prefetch=1, grid=(nbr,),
            in_specs=[own, pl.BlockSpec((N_DEV - 1, tr, cols), lambda i, w: (0, i, 0))],
            out_specs=pl.BlockSpec((tr, cols), lambda i, w: (i, 0))),
        out_shape=jax.ShapeDtypeStruct((rows, cols), F32),
        compiler_params=_params(("parallel",)),
    )(jnp.stack([shard, core]).astype(jnp.int32), dw, received)


def _scatter_begin(tag, arrs, dws):
    send_sems, recv_sems, dws, lands, token = _pieces_start(tag, arrs, dws)
    return (send_sems, recv_sems, dws, lands), token[0, 0]


def _scatter_reduce(tag, arrs, state, after):
    x, y, c = _me()
    send_sems, recv_sems, dws, lands = state
    dws, lands = _pieces_wait(tag, arrs, send_sems, recv_sems, dws, lands, after)
    g_own = [_piece_sum(a, dw, _shard_of((x, y)), c, r) for a, dw, r in zip(arrs, dws, lands)]
    send_sems, recv_sems, g_own, lands, token = _join_start(tag, g_own)
    return (send_sems, recv_sems, g_own, lands), token


def _scatter_end(tag, state, after):
    return _join_wait(tag, *state, after)


def _rows_start(tag, block):
    r, d = block.shape

    def body(src, land, send_sems, recv_sems, src_thru, land_thru, token):
        x, y, c = _me()
        for k, peer in enumerate(_peers(x, y, c)):
            _rcopy(src, land.at[_dev_index(x, y, c)], send_sems, recv_sems, k, peer).start()
        token[...] = jnp.zeros_like(token)

    outs = pl.pallas_call(
        body, name="rows_start_" + tag,
        in_specs=[HBM, HBM],
        out_specs=[SEM, SEM, HBM, HBM, VMEM_FULL],
        out_shape=[pltpu.SemaphoreType.DMA((N_DEV - 1,)), pltpu.SemaphoreType.DMA((N_DEV - 1,)),
                   pltpu.HBM((r, d), F32), pltpu.HBM((N_DEV, r, d), F32), jax.ShapeDtypeStruct((8, HEAD_DIM), F32)],
        input_output_aliases={0: 2, 1: 3},
        compiler_params=pltpu.CompilerParams(has_side_effects=EFFECT),
    )(_in_hbm(block), _in_hbm(lax.empty((N_DEV, r, d), F32)))
    return outs[0], outs[1], outs[2], outs[3], outs[4]


def _rows_wait(tag, send_sems, recv_sems, block, land, after):
    r, d = block.shape

    def body(src, land_, send_sems_, recv_sems_, after_ref, src_thru, land_thru):
        x, y, c = _me()
        for k, peer in enumerate(_peers(x, y, c)):
            cp = _rcopy(src, land_.at[_dev_index(*peer)], send_sems_, recv_sems_, k, peer)
            cp.wait_send()
            cp.wait_recv()

    outs = pl.pallas_call(
        body, name="rows_wait_" + tag,
        in_specs=[HBM, HBM, SEM, SEM, ANY],
        out_specs=[HBM, HBM],
        out_shape=[pltpu.HBM((r, d), F32), pltpu.HBM((N_DEV, r, d), F32)],
        input_output_aliases={0: 0, 1: 1},
        compiler_params=pltpu.CompilerParams(has_side_effects=EFFECT),
    )(block, land, send_sems, recv_sems, after)
    me = _dev_index(*_me())
    return lax.dynamic_update_slice(outs[1], outs[0][None], (me, 0, 0))


ADA_ROWS = 80
ADA_W = 6 * D_MODEL // N_SHARD


def _ada_forward(c_block, cctx_block, w_ada, b_shard):
    d = c_block.shape[1]

    def body(c_ref, cc_ref, w_ref, b_ref, act_ref, mods_ref, raw, mloc, send_sems, recv_sems):
        x, y, c = _me()
        me = _dev_index(x, y, c)
        s_me = _shard_of((x, y))
        raw[72:ADA_ROWS, :] = jnp.zeros((ADA_ROWS - 72, d), F32)
        raw[pl.ds(pl.multiple_of(me * 8, 8), 8), :] = c_ref[...]
        raw[64:72, :] = cc_ref[...]
        sends = []
        for mask in range(1, N_DEV):
            peer = (x ^ (mask >> 2), y ^ ((mask >> 1) & 1), c ^ (mask & 1))
            cp = _rcopy(c_ref, raw.at[pl.ds(pl.multiple_of(me * 8, 8), 8), :], send_sems, recv_sems, mask - 1, peer)
            cp.start()
            sends.append(cp)
        for mask in range(1, N_DEV):
            peer = (x ^ (mask >> 2), y ^ ((mask >> 1) & 1), c ^ (mask & 1))
            landed = raw.at[pl.ds(pl.multiple_of(_dev_index(*peer) * 8, 8), 8), :]
            _rcopy(landed, landed, send_sems, recv_sems, mask - 1, peer).wait_recv()
        v = raw[...]
        act = v * jax.nn.sigmoid(v)
        act_ref[...] = act
        mloc[...] = lax.dot_general(act.astype(BF16), w_ref[...].astype(BF16), NN,
                                    preferred_element_type=F32) + b_ref[...]
        mods_ref[s_me, 0:8, :] = mloc[pl.ds(pl.multiple_of(me * 8, 8), 8), :]
        mods_ref[s_me, 8:16, :] = mloc[64:72, :]
        base = N_DEV - 1
        for j, chip in enumerate(_other_chips(x, y)):
            peer = (*chip, c)
            rows = mloc.at[pl.ds(pl.multiple_of(_dev_index(*peer) * 8, 8), 8), :]
            cp = _rcopy(rows, mods_ref.at[s_me, 0:8, :], send_sems, recv_sems, base + 2 * j, peer)
            cp.start()
            sends.append(cp)
            cp = _rcopy(mloc.at[64:72, :], mods_ref.at[s_me, 8:16, :], send_sems, recv_sems, base + 2 * j + 1, peer)
            cp.start()
            sends.append(cp)
        for j, chip in enumerate(_other_chips(x, y)):
            for part in range(2):
                landed = mods_ref.at[_shard_of(chip), 8 * part:8 * part + 8, :]
                _rcopy(landed, landed, send_sems, recv_sems, base + 2 * j + part, (*chip, c)).wait_recv()
        for cp in sends:
            cp.wait_send()

    n_sem = N_DEV - 1 + 2 * N_CHIP_PEERS
    return pl.pallas_call(
        body, name="ada_forward",
        in_specs=[VMEM_FULL] * 4, out_specs=[VMEM_FULL, VMEM_FULL],
        out_shape=[jax.ShapeDtypeStruct((ADA_ROWS, d), F32), jax.ShapeDtypeStruct((N_SHARD, 16, ADA_W), F32)],
        scratch_shapes=[pltpu.VMEM((ADA_ROWS, d), F32), pltpu.VMEM((ADA_ROWS, ADA_W), F32),
                        pltpu.SemaphoreType.DMA((n_sem,)), pltpu.SemaphoreType.DMA((n_sem,))],
        compiler_params=pltpu.CompilerParams(vmem_limit_bytes=VMEM_LIMIT),
    )(c_block, cctx_block, w_ada, b_shard)


def _small_reduce(gathered):
    d = gathered.shape[2]

    def body(g_ref, o_ref):
        tot = g_ref[0]
        for i in range(1, N_DEV):
            tot = tot + g_ref[i]
        o_ref[...] = tot
        o_ref[0:2, :] = tot[0:2] + tot[6:8]
        o_ref[12:13, :] = jnp.broadcast_to(jnp.sum(tot[12:13], axis=1, keepdims=True), (1, d))

    return pl.pallas_call(body, name="small_reduce", in_specs=[VMEM_FULL], out_specs=VMEM_FULL,
                          out_shape=jax.ShapeDtypeStruct((16, d), F32))(gathered)


def _cctx_grad(gathered, c_ctx):
    d = gathered.shape[2]

    def body(g_ref, c_ref, o_ref):
        tot = g_ref[0, 0:1, :]
        for chip in range(1, N_SHARD):
            tot = tot + g_ref[2 * chip, 0:1, :]
        v = c_ref[...]
        sig = jax.nn.sigmoid(v)
        o_ref[...] = tot * (sig * (1.0 + v * (1.0 - sig)))

    return pl.pallas_call(body, name="cctx_grad", in_specs=[VMEM_FULL, VMEM_FULL], out_specs=VMEM_FULL,
                          out_shape=jax.ShapeDtypeStruct((1, d), F32))(gathered, c_ctx.reshape(1, d))


def _cast_into_full(w, shard, full, axis, name):
    r, cdim = w.shape
    tr = _fit(r, ROW_TILE)
    nbr = r // tr

    def body(s_ref, w_ref, o_ref):
        o_ref[...] = w_ref[...].astype(BF16)

    if axis == 0:
        out_spec = pl.BlockSpec((tr, cdim), lambda i, s: (s[0] * nbr + i, 0))
    else:
        out_spec = pl.BlockSpec((tr, cdim), lambda i, s: (i, s[0]))
    return pl.pallas_call(
        body, name=name,
        grid_spec=pltpu.PrefetchScalarGridSpec(
            num_scalar_prefetch=1, grid=(nbr,), in_specs=[pl.BlockSpec((tr, cdim), lambda i, s: (i, 0))],
            out_specs=out_spec),
        out_shape=jax.ShapeDtypeStruct(full, BF16), compiler_params=_params(("parallel",)),
    )(shard.reshape(1).astype(jnp.int32), w)


def _adamw_halves(w, g_own, g_other, m, v, core, axis, name):
    r, cdim = w.shape
    hr, hc = (r // 2, cdim) if axis == 1 else (r, cdim // 2)
    assert g_own.shape == (hr, hc) and g_other.shape == (hr, hc)
    tr = _fit(hr, 256)
    nb = hr // tr
    c1 = 1.0 - ADAM_B1 ** ADAM_STEP
    c2 = 1.0 - ADAM_B2 ** ADAM_STEP

    def body(c_ref, w_ref, go_ref, gt_ref, m_ref, v_ref, g_ref, d_ref, nm_ref, nv_ref):
        gv = jnp.where(pl.program_id(0) == c_ref[0], go_ref[...], gt_ref[...])
        nm = ADAM_B1 * m_ref[...] + (1.0 - ADAM_B1) * gv
        nv = ADAM_B2 * v_ref[...] + (1.0 - ADAM_B2) * (gv * gv)
        g_ref[...] = gv
        nm_ref[...] = nm
        nv_ref[...] = nv
        d_ref[...] = -ADAM_LR * ((nm / c1) / (jnp.sqrt(nv / c2) + ADAM_EPS) + ADAM_WD * w_ref[...])

    if axis == 1:
        big = pl.BlockSpec((tr, hc), lambda p, i, c: (p * nb + i, 0))
    else:
        big = pl.BlockSpec((tr, hc), lambda p, i, c: (i, p))
    own = pl.BlockSpec((tr, hc), lambda p, i, c: (jnp.where(p == c[0], i, 0), 0))
    other = pl.BlockSpec((tr, hc), lambda p, i, c: (jnp.where(p == c[0], 0, i), 0))
    sh = jax.ShapeDtypeStruct((r, cdim), F32)
    return pl.pallas_call(
        body, name=name,
        grid_spec=pltpu.PrefetchScalarGridSpec(
            num_scalar_prefetch=1, grid=(2, nb), in_specs=[big, own, other, big, big], out_specs=[big] * 4),
        out_shape=[sh] * 4, compiler_params=_params(("parallel", "parallel")),
    )(core.reshape(1).astype(jnp.int32), w, g_own, g_other, m, v)


def _adamw(w, g, m, v, name):
    r, cdim = w.shape
    tr = _fit(r, 128) if r % (ROW_TILE // 4) == 0 else r
    c1 = 1.0 - ADAM_B1 ** ADAM_STEP
    c2 = 1.0 - ADAM_B2 ** ADAM_STEP

    def body(w_ref, g_ref, m_ref, v_ref, d_ref, nm_ref, nv_ref):
        gv = g_ref[...]
        nm = ADAM_B1 * m_ref[...] + (1.0 - ADAM_B1) * gv
        nv = ADAM_B2 * v_ref[...] + (1.0 - ADAM_B2) * (gv * gv)
        nm_ref[...] = nm
        nv_ref[...] = nv
        d_ref[...] = -ADAM_LR * ((nm / c1) / (jnp.sqrt(nv / c2) + ADAM_EPS) + ADAM_WD * w_ref[...])

    spec = pl.BlockSpec((tr, cdim), lambda i: (i, 0))
    sh = jax.ShapeDtypeStruct((r, cdim), F32)
    return pl.pallas_call(body, name=name, grid=(r // tr,), in_specs=[spec] * 4, out_specs=[spec] * 3,
                          out_shape=[sh, sh, sh], compiler_params=_params(("parallel",)))(w, g, m, v)


def _adamw_small(ws, gs, ms, vs):
    k = len(ws)
    c1 = 1.0 - ADAM_B1 ** ADAM_STEP
    c2 = 1.0 - ADAM_B2 ** ADAM_STEP

    def body(*refs):
        w_refs, g_refs, m_refs, v_refs = refs[0:k], refs[k:2 * k], refs[2 * k:3 * k], refs[3 * k:4 * k]
        d_refs, nm_refs, nv_refs = refs[4 * k:5 * k], refs[5 * k:6 * k], refs[6 * k:7 * k]
        for i in range(k):
            gv = g_refs[i][...]
            nm = ADAM_B1 * m_refs[i][...] + (1.0 - ADAM_B1) * gv
            nv = ADAM_B2 * v_refs[i][...] + (1.0 - ADAM_B2) * (gv * gv)
            nm_refs[i][...] = nm
            nv_refs[i][...] = nv
            d_refs[i][...] = -ADAM_LR * ((nm / c1) / (jnp.sqrt(nv / c2) + ADAM_EPS) + ADAM_WD * w_refs[i][...])

    shapes = [jax.ShapeDtypeStruct(w.shape, F32) for w in ws]
    outs = pl.pallas_call(body, name="adamw_small", in_specs=[VMEM_FULL] * (4 * k), out_specs=[VMEM_FULL] * (3 * k),
                          out_shape=shapes * 3)(*ws, *gs, *ms, *vs)
    return outs[0:k], outs[k:2 * k], outs[2 * k:3 * k]


SMALL = (("c_ctx", D_MODEL), ("b_ada", 6 * D_MODEL), ("q_norm_g", HEAD_DIM), ("k_norm_g", HEAD_DIM),
         ("sink_logit", HEADS_A), ("ln1_g", D_MODEL), ("ln1_b", D_MODEL), ("ln2_g", D_MODEL), ("ln2_b", D_MODEL))
WEIGHT_ORDER = ("c_ctx", "w_ada", "b_ada", "w_in", "q_norm_g", "k_norm_g", "sink_logit", "w_out", "ln1_g", "ln1_b",
                "w_gate", "w_up", "w_down", "ln2_g", "ln2_b")


def kernel(x, c, ctx, c_ctx, w_ada, b_ada, w_in, q_norm_g, k_norm_g, sink_logit, w_out, ln1_g, ln1_b, w_gate, w_up, w_down, ln2_g, ln2_b, loss_target, m_c_ctx, m_w_ada, m_b_ada, m_w_in, m_q_norm_g, m_k_norm_g, m_sink_logit, m_w_out, m_ln1_g, m_ln1_b, m_w_gate, m_w_up, m_w_down, m_ln2_g, m_ln2_b, v_c_ctx, v_w_ada, v_b_ada, v_w_in, v_q_norm_g, v_k_norm_g, v_sink_logit, v_w_out, v_ln1_g, v_ln1_b, v_w_gate, v_w_up, v_w_down, v_ln2_g, v_ln2_b):
    d = D_MODEL
    w = dict(c_ctx=c_ctx, w_ada=w_ada[0], b_ada=b_ada, w_in=w_in[0], q_norm_g=q_norm_g, k_norm_g=k_norm_g,
             sink_logit=sink_logit, w_out=w_out[0], ln1_g=ln1_g, ln1_b=ln1_b, w_gate=w_gate[0], w_up=w_up[0],
             w_down=w_down[0], ln2_g=ln2_g, ln2_b=ln2_b)
    m = dict(c_ctx=m_c_ctx, w_ada=m_w_ada[0], b_ada=m_b_ada, w_in=m_w_in[0], q_norm_g=m_q_norm_g, k_norm_g=m_k_norm_g,
             sink_logit=m_sink_logit, w_out=m_w_out[0], ln1_g=m_ln1_g, ln1_b=m_ln1_b, w_gate=m_w_gate[0],
             w_up=m_w_up[0], w_down=m_w_down[0], ln2_g=m_ln2_g, ln2_b=m_ln2_b)
    v = dict(c_ctx=v_c_ctx, w_ada=v_w_ada[0], b_ada=v_b_ada, w_in=v_w_in[0], q_norm_g=v_q_norm_g, k_norm_g=v_k_norm_g,
             sink_logit=v_sink_logit, w_out=v_w_out[0], ln1_g=v_ln1_g, ln1_b=v_ln1_b, w_gate=v_w_gate[0],
             w_up=v_w_up[0], w_down=v_w_down[0], ln2_g=v_ln2_g, ln2_b=v_ln2_b)
    mx, my, mc = _me()
    s_me = _shard_of((mx, my))
    me = _dev_index(mx, my, mc)
    pad8 = lambda row: jnp.concatenate([row.reshape(1, -1), jnp.zeros((7, row.size), F32)], axis=0)

    b_shard = lax.dynamic_slice(b_ada, (0, s_me * ADA_W), (1, ADA_W))
    act, mods4 = _ada_forward(pad8(c), pad8(c_ctx), w["w_ada"], b_shard)

    gathers = []
    prev, shard = mods4, s_me
    for k, names in enumerate(W_GROUPS):
        arrs = tuple(BIG_INDEX[name] for name in names)
        bufs = [_cast_into_full(w[name], shard, BIG[a][1], BIG[a][2], "cast_" + name) for name, a in zip(names, arrs)]
        send_sems, recv_sems, thru, prev = _gather_start("g%d" % k, arrs, bufs, prev)
        shard = s_me + prev[0, 0].astype(jnp.int32)
        gathers.append((arrs, send_sems, recv_sems, thru))

    forwards = {}

    def prefetch(k, after):
        arrs, send_sems, recv_sems, thru = gathers[k]
        landed = _gather_wait("g%d" % k, arrs, send_sems, recv_sems, thru, after)
        fwd_send, fwd_recv, landed, token = _forward_start("g%d" % k, arrs, landed)
        forwards[k] = (fwd_send, fwd_recv, landed)
        return token[0, 0]

    def weights(k, after):
        arrs, send_sems, recv_sems, thru = gathers[k]
        if k in forwards:
            return _forward_wait("g%d" % k, arrs, *forwards[k], after)
        landed = _gather_wait("g%d" % k, arrs, send_sems, recv_sems, thru, after)
        return _gather_forward("g%d" % k, arrs, landed)

    mod = jnp.transpose(mods4[:, 0:1, :], (1, 0, 2)).reshape(1, 6 * d) + prev[0, 0]
    mod_ctx = jnp.transpose(mods4[:, 8:9, :], (1, 0, 2)).reshape(1, 6 * d)

    scatters = {}

    def grads_out(k, dws):
        arrs = tuple(BIG_INDEX[name] for name in G_GROUPS[k])
        scatters[k], zero = _scatter_begin("g%d" % k, arrs, dws)
        return zero

    grad_x, partial = _layer_fwd_bwd(x[0], ctx[0], loss_target[0], mod, mod_ctx, weights, prefetch, grads_out,
                                     q_norm_g, k_norm_g, sink_logit, ln1_g, ln1_b, ln2_g, ln2_b)
    grads, delta, new_m, new_v = {}, {}, {}, {}

    p_send, p_recv, partial, p_land, after = _rows_start("partials", partial)
    joins = []
    for k, names in enumerate(G_GROUPS):
        arrs = tuple(BIG_INDEX[name] for name in names)
        state, after = _scatter_reduce("g%d" % k, arrs, scatters[k], after)
        joins.append(state)
    for k, names in enumerate(G_GROUPS):
        g_own, g_other = _scatter_end("g%d" % k, joins[k], after)
        for name, own, other in zip(names, g_own, g_other):
            grads[name], delta[name], new_m[name], new_v[name] = _adamw_halves(
                w[name], own, other, m[name], v[name], mc, BIG[BIG_INDEX[name]][2], "adamw_" + name)
            after = new_v[name]

    gathered = _rows_wait("partials", p_send, p_recv, partial, p_land, after)
    tot = _small_reduce(gathered)
    grads["b_ada"] = tot[0:6].reshape(1, 6 * d)
    grads["ln1_g"], grads["ln1_b"], grads["ln2_g"], grads["ln2_b"] = tot[8:9], tot[9:10], tot[10:11], tot[11:12]
    grads["q_norm_g"] = tot[13:14, 0:HEAD_DIM]
    grads["k_norm_g"] = tot[13:14, HEAD_DIM:2 * HEAD_DIM]
    grads["sink_logit"] = tot[13:14, 2 * HEAD_DIM:2 * HEAD_DIM + HEADS_A]
    loss = tot[12, 0]

    dm_all = gathered[:, 0:6, :].reshape(N_DEV, 6 * d)
    dmc_tot = jnp.concatenate([tot[6:8].reshape(1, 2 * d), jnp.zeros((1, 4 * d), F32)], axis=1)
    dm_rows = jnp.concatenate([pad8(dm_all[i]) for i in range(N_DEV)] + [pad8(dmc_tot), jnp.zeros((8, 6 * d), F32)], axis=0)
    dm_shard = lax.dynamic_slice(dm_rows, (0, s_me * ADA_W), (ADA_ROWS, ADA_W))
    dmc_shard = lax.dynamic_slice(pad8(dmc_tot), (0, s_me * ADA_W), (8, ADA_W))
    cc_part = _matmul(dmc_shard, w["w_ada"], name="d_cctx", tb=True, tm=8, tn=1024, tk=1536, out_dtype=F32)
    c_send, c_recv, cc_part, c_land, c_token = _rows_start("cctx", cc_part)
    grads["w_ada"] = _matmul(act, dm_shard, name="dw_ada", ta=True, tm=1024, tn=1024, tk=ADA_ROWS, out_dtype=F32,
                             after=c_token)
    delta["w_ada"], new_m["w_ada"], new_v["w_ada"] = _adamw(w["w_ada"], grads["w_ada"], m["w_ada"], v["w_ada"],
                                                            "adamw_w_ada")
    gathered_cc = _rows_wait("cctx", c_send, c_recv, cc_part, c_land, new_v["w_ada"])
    grads["c_ctx"] = _cctx_grad(gathered_cc, c_ctx).reshape(d)
    rows = lambda t: [t[name].reshape(1, size) for name, size in SMALL]
    small = _adamw_small(rows(w), rows(grads), rows(m), rows(v))
    for k, (name, size) in enumerate(SMALL):
        delta[name], new_m[name], new_v[name] = [t[k].reshape(w[name].shape) for t in small]
        grads[name] = grads[name].reshape(w[name].shape)

    lead = lambda name, t: t[None] if name in ("w_ada", "w_in", "w_out", "w_gate", "w_up", "w_down") else t
    outs = [loss, grad_x[None]]
    for group in (grads, delta, new_m, new_v):
        outs += [lead(name, group[name]) for name in WEIGHT_ORDER]
    return tuple(outs)
```

```python
import functools
import math

import jax
import jax.numpy as jnp
from jax import lax
from jax.experimental import pallas as pl
from jax.experimental.pallas import tpu as pltpu

F32 = jnp.float32
BF16 = jnp.bfloat16
MESH = pl.DeviceIdType.MESH

D_MODEL = 2048
HEAD_DIM = 128
HEADS_A = 8
HEADS_B = 8
KV_A = 2
KV_B = 2
GROUP = 4
GRID_W = 64
WINDOW = 128
BLOCK = 128
FFN = 5632
IN_WIDTH = 3072
MIX_WIDTH = 2048
ROPE_THETA = 10000.0
EPS = 1e-6
ATTN_SCALE = HEAD_DIM ** -0.5
DN_ALPHA = 2.0 ** 0.25
N_SHARD = 4
N_DEV = 8

ADAM_LR = 0.001
ADAM_B1 = 0.9
ADAM_B2 = 0.999
ADAM_EPS = 1e-08
ADAM_WD = 0.01
ADAM_STEP = 10

QA0, KA0, VA0, QB0, KB0, VB0 = 0, 1024, 1280, 1536, 2560, 2816

VMEM_LIMIT = 56 * 1024 * 1024
ROW_TILE = 256
NN = (((1,), (0,)), ((), ()))
NT = (((1,), (1,)), ((), ()))
TN = (((0,), (0,)), ((), ()))


def _fit(total, pref):
    step = ROW_TILE // 4
    best = step
    for cand in range(step, pref + 1, step):
        if total % cand == 0:
            best = cand
    return best


def _params(sem=None):
    return pltpu.CompilerParams(dimension_semantics=sem, vmem_limit_bytes=VMEM_LIMIT)


def _matmul(a, b, *, name, ta=False, tb=False, tm, tn, tk, out_dtype, after=None):
    m = a.shape[1] if ta else a.shape[0]
    k = a.shape[0] if ta else a.shape[1]
    n = b.shape[0] if tb else b.shape[1]
    assert (b.shape[1] if tb else b.shape[0]) == k
    tm, tn, tk = min(tm, m), min(tn, n), min(tk, k)
    assert m % tm == 0 and n % tn == 0 and k % tk == 0, (name, m, n, k, tm, tn, tk)
    nk = k // tk
    dn = (((0 if ta else 1,), (1 if tb else 0,)), ((), ()))

    def product(a_ref, b_ref):
        return lax.dot_general(a_ref[...].astype(BF16), b_ref[...].astype(BF16), dn, preferred_element_type=F32)

    def body_whole_k(a_ref, b_ref, *rest):
        o_ref = rest[-1]
        o_ref[...] = product(a_ref, b_ref).astype(o_ref.dtype)

    def body(a_ref, b_ref, *rest):
        o_ref, acc_ref = rest[-2:]
        kk = pl.program_id(2)
        part = product(a_ref, b_ref)

        @pl.when(kk == 0)
        def _():
            acc_ref[...] = part

        @pl.when(kk != 0)
        def _():
            acc_ref[...] += part

        @pl.when(kk == nk - 1)
        def _():
            o_ref[...] = acc_ref[...].astype(o_ref.dtype)

    a_spec = (pl.BlockSpec((tk, tm), lambda i, j, kk: (kk, i)) if ta
              else pl.BlockSpec((tm, tk), lambda i, j, kk: (i, kk)))
    b_spec = (pl.BlockSpec((tn, tk), lambda i, j, kk: (j, kk)) if tb
              else pl.BlockSpec((tk, tn), lambda i, j, kk: (kk, j)))
    return pl.pallas_call(
        body_whole_k if nk == 1 else body, name=name, grid=(m // tm, n // tn, nk),
        in_specs=[a_spec, b_spec] + ([] if after is None else [pl.BlockSpec(memory_space=pl.ANY)]),
        out_specs=pl.BlockSpec((tm, tn), lambda i, j, kk: (i, j)),
        out_shape=jax.ShapeDtypeStruct((m, n), out_dtype),
        scratch_shapes=[] if nk == 1 else [pltpu.VMEM((tm, tn), F32)],
        compiler_params=_params(("parallel", "parallel", "arbitrary")),
    )(a, b, *([] if after is None else [after]))


def _modulate_rows(x, ctx, mods):
    n, d = x.shape
    c = ctx.shape[0]
    nx = n // ROW_TILE
    assert c == ROW_TILE

    def body(x_ref, ctx_ref, mods_ref, o_ref):
        i = pl.program_id(0)

        @pl.when(i < nx)
        def _():
            o_ref[...] = (x_ref[...] * (1.0 + mods_ref[0:1, :]) + mods_ref[1:2, :]).astype(BF16)

        @pl.when(i >= nx)
        def _():
            o_ref[...] = (ctx_ref[...] * (1.0 + mods_ref[2:3, :]) + mods_ref[3:4, :]).astype(BF16)

    return pl.pallas_call(
        body, name="modulate_rows", grid=(nx + 1,),
        in_specs=[pl.BlockSpec((ROW_TILE, d), lambda i: (jnp.minimum(i, nx - 1), 0)),
                  pl.BlockSpec((ROW_TILE, d), lambda i: (0, 0)),
                  pl.BlockSpec((8, d), lambda i: (0, 0))],
        out_specs=pl.BlockSpec((ROW_TILE, d), lambda i: (i, 0)),
        out_shape=jax.ShapeDtypeStruct((n + c, d), BF16),
        compiler_params=_params(("parallel",)),
    )(x, ctx, mods)


def _rope_tables(n, c):
    rows = n // GRID_W
    row_ids = jnp.repeat(jnp.arange(rows, dtype=F32), GRID_W)
    col_ids = jnp.tile(jnp.arange(GRID_W, dtype=F32), rows)
    axis_dim = HEAD_DIM // 2
    inv_freq = jnp.power(ROPE_THETA, -jnp.arange(0, axis_dim, 2, dtype=F32) / axis_dim)
    ang_r = row_ids[:, None] * inv_freq
    ang_c = col_ids[:, None] * inv_freq
    ang = jnp.concatenate([ang_r, ang_r, ang_c, ang_c], axis=-1)
    cos, sin = jnp.cos(ang), jnp.sin(ang)
    quarter = (jnp.arange(HEAD_DIM) // (HEAD_DIM // 4)) % 2
    sin_a = jnp.where(quarter == 0, -sin, 0.0)
    sin_b = jnp.where(quarter == 1, sin, 0.0)
    pad = lambda t, v: jnp.concatenate([t, jnp.full((c, HEAD_DIM), v, F32)], axis=0)
    return pad(cos, 1.0), pad(sin_a, 0.0), pad(sin_b, 0.0)


def _rope(x, cos, sin_a, sin_b):
    return x * cos + pltpu.roll(x, 96, 1) * sin_a + pltpu.roll(x, 32, 1) * sin_b


def _rope_t(dy, cos, sin_a, sin_b):
    return dy * cos - pltpu.roll(dy, 96, 1) * sin_a - pltpu.roll(dy, 32, 1) * sin_b


def _rms(x):
    r = lax.rsqrt(jnp.mean(x * x, axis=-1, keepdims=True) + EPS)
    return x * r, r


def _qkv_post(h_all, cos, sin_a, sin_b, q_g, k_g):
    t = h_all.shape[0]
    nt = t // ROW_TILE

    def body(h_ref, cos_ref, sa_ref, sb_ref, qg_ref, kg_ref, qa_ref, ka_ref, va_ref, qb_ref, kb_ref, vb_ref):
        cos_, sa, sb = cos_ref[...], sa_ref[...], sb_ref[...]
        sl = lambda off, hh: h_ref[:, off + hh * HEAD_DIM: off + (hh + 1) * HEAD_DIM]
        for hh in range(HEADS_A):
            qa_ref[hh] = (_rope(sl(QA0, hh), cos_, sa, sb) * ATTN_SCALE).astype(BF16)
        for hh in range(KV_A):
            ka_ref[hh] = _rope(sl(KA0, hh), cos_, sa, sb).astype(BF16)
            va_ref[hh] = sl(VA0, hh).astype(BF16)
        for hh in range(HEADS_B):
            xn, _ = _rms(sl(QB0, hh))
            qb_ref[hh] = (_rope(xn * qg_ref[...], cos_, sa, sb) * ATTN_SCALE).astype(BF16)
        for hh in range(KV_B):
            xn, _ = _rms(sl(KB0, hh))
            kb_ref[hh] = _rope(xn * kg_ref[...], cos_, sa, sb).astype(BF16)
            vb_ref[hh] = sl(VB0, hh).astype(BF16)

    tab = pl.BlockSpec((ROW_TILE, HEAD_DIM), lambda i: (i, 0))
    gain = pl.BlockSpec((1, HEAD_DIM), lambda i: (0, 0))
    hs = lambda nh: pl.BlockSpec((nh, ROW_TILE, HEAD_DIM), lambda i: (0, i, 0))
    sh = lambda nh: jax.ShapeDtypeStruct((nh, t, HEAD_DIM), BF16)
    return pl.pallas_call(
        body, name="qkv_post", grid=(nt,),
        in_specs=[pl.BlockSpec((ROW_TILE, IN_WIDTH), lambda i: (i, 0)), tab, tab, tab, gain, gain],
        out_specs=[hs(HEADS_A), hs(KV_A), hs(KV_A), hs(HEADS_B), hs(KV_B), hs(KV_B)],
        out_shape=[sh(HEADS_A), sh(KV_A), sh(KV_A), sh(HEADS_B), sh(KV_B), sh(KV_B)],
        compiler_params=_params(("parallel",)),
    )(h_all, cos, sin_a, sin_b, q_g, k_g)


def _qkv_bwd_post(h_all, cos, sin_a, sin_b, q_g, k_g, dqa, dka, dva, dqb, dkb, dvb, n):
    t = h_all.shape[0]
    nt = t // ROW_TILE
    nx = n // ROW_TILE

    def body(h_ref, cos_ref, sa_ref, sb_ref, qg_ref, kg_ref,
             dqa_ref, dka_ref, dva_ref, dqb_ref, dkb_ref, dvb_ref, dh_ref, gs_ref):
        i = pl.program_id(0)
        cos_, sa, sb = cos_ref[...], sa_ref[...], sb_ref[...]
        latent = (i < nx).astype(F32)
        sl = lambda off, hh: h_ref[:, off + hh * HEAD_DIM: off + (hh + 1) * HEAD_DIM]

        def put(off, hh, val):
            dh_ref[:, off + hh * HEAD_DIM: off + (hh + 1) * HEAD_DIM] = val.astype(BF16)

        def norm_bwd(x, gain, dy):
            xn, r = _rms(x)
            dxh = dy * gain
            dx = r * (dxh - xn * jnp.mean(dxh * xn, axis=-1, keepdims=True))
            return dx, jnp.sum(dy * xn, axis=0, keepdims=True)

        for hh in range(HEADS_A):
            put(QA0, hh, _rope_t(dqa_ref[hh] * (ATTN_SCALE * latent), cos_, sa, sb))
        for hh in range(KV_A):
            put(KA0, hh, _rope_t(dka_ref[hh], cos_, sa, sb))
            put(VA0, hh, dva_ref[hh])
        gq = jnp.zeros((1, HEAD_DIM), F32)
        gk = jnp.zeros((1, HEAD_DIM), F32)
        for hh in range(HEADS_B):
            dq_t = dqb_ref[hh // GROUP, :, (hh % GROUP) * ROW_TILE:(hh % GROUP + 1) * ROW_TILE]
            dy = _rope_t(dq_t.T * (ATTN_SCALE * latent), cos_, sa, sb)
            dx, g = norm_bwd(sl(QB0, hh), qg_ref[...], dy)
            put(QB0, hh, dx)
            gq = gq + g
        for hh in range(KV_B):
            dy = _rope_t(dkb_ref[hh], cos_, sa, sb)
            dx, g = norm_bwd(sl(KB0, hh), kg_ref[...], dy)
            put(KB0, hh, dx)
            gk = gk + g
            put(VB0, hh, dvb_ref[hh])
        upd = jnp.concatenate([gq, gk, jnp.zeros((6, HEAD_DIM), F32)], axis=0)

        @pl.when(i == 0)
        def _():
            gs_ref[...] = upd

        @pl.when(i != 0)
        def _():
            gs_ref[...] += upd

    tab = pl.BlockSpec((ROW_TILE, HEAD_DIM), lambda i: (i, 0))
    gain = pl.BlockSpec((1, HEAD_DIM), lambda i: (0, 0))
    lat = lambda nh: pl.BlockSpec((nh, ROW_TILE, HEAD_DIM), lambda i: (0, jnp.minimum(i, nx - 1), 0))
    full = lambda nh: pl.BlockSpec((nh, ROW_TILE, HEAD_DIM), lambda i: (0, i, 0))
    return pl.pallas_call(
        body, name="qkv_bwd_post", grid=(nt,),
        in_specs=[pl.BlockSpec((ROW_TILE, IN_WIDTH), lambda i: (i, 0)), tab, tab, tab, gain, gain,
                  lat(HEADS_A), full(KV_A), full(KV_A),
                  pl.BlockSpec((KV_B, None, HEAD_DIM, GROUP * ROW_TILE), lambda i: (0, jnp.minimum(i, nx - 1), 0, 0)),
                  full(KV_B), full(KV_B)],
        out_specs=[pl.BlockSpec((ROW_TILE, IN_WIDTH), lambda i: (i, 0)),
                   pl.BlockSpec((8, HEAD_DIM), lambda i: (0, 0))],
        out_shape=[jax.ShapeDtypeStruct((t, IN_WIDTH), BF16), jax.ShapeDtypeStruct((8, HEAD_DIM), F32)],
        compiler_params=_params(("arbitrary",)),
    )(h_all, cos, sin_a, sin_b, q_g, k_g, dqa, dka, dva, dqb, dkb, dvb)


GB_TQ = 256
GB_TK = 256


def _heads_rows(ref2d, tq):
    return jnp.concatenate([ref2d[:, hh * HEAD_DIM:(hh + 1) * HEAD_DIM] for hh in range(GROUP)], axis=0)


def _attn_b_fwd(qb, kb, vb, n):
    t = kb.shape[1]
    nk = t // GB_TK
    tq = GB_TQ
    nq = n // tq
    qb_step = 8 if nq % 8 == 0 else (4 if nq % 4 == 0 else 1)
    rows = qb_step * GROUP * tq

    def body(q_ref, k_ref, v_ref, o_ref, lse_ref, m_s, l_s, acc_s):
        blocks = [(s, hh) for s in range(qb_step) for hh in range(GROUP)]
        q = jnp.concatenate([q_ref[hh, s * tq:(s + 1) * tq, :] for s, hh in blocks], axis=0)
        m_s[...] = jnp.full((1, rows), -jnp.inf, F32)
        l_s[...] = jnp.zeros((1, rows), F32)
        acc_s[...] = jnp.zeros((HEAD_DIM, rows), F32)

        def scores(j):
            start = pl.multiple_of(j * GB_TK, GB_TK)
            return lax.dot_general(k_ref[pl.ds(start, GB_TK), :], q, NT, preferred_element_type=F32)

        def step(j, carry):
            st = scores(j)
            vs = v_ref[pl.ds(pl.multiple_of(j * GB_TK, GB_TK), GB_TK), :]
            m_prev = m_s[...]
            m_new = jnp.maximum(m_prev, jnp.max(st, axis=0, keepdims=True))
            p = jnp.exp(st - m_new)
            alpha = jnp.exp(m_prev - m_new)
            l_s[...] = alpha * l_s[...] + jnp.sum(p, axis=0, keepdims=True)
            acc_s[...] = alpha * acc_s[...] + lax.dot_general(vs, p.astype(BF16), TN, preferred_element_type=F32)
            m_s[...] = m_new
            return carry

        lax.fori_loop(0, nk, step, 0, unroll=4)
        ot = acc_s[...] * (1.0 / l_s[...])
        lse = m_s[...] + jnp.log(l_s[...])
        width = GROUP * tq
        for s in range(qb_step):
            lse_ref[s] = lse[:, s * width:(s + 1) * width]
        for k, (s, hh) in enumerate(blocks):
            o_ref[s * tq:(s + 1) * tq, hh * HEAD_DIM:(hh + 1) * HEAD_DIM] = ot[:, k * tq:(k + 1) * tq].T.astype(BF16)

    return pl.pallas_call(
        body, name="attn_b_fwd", grid=(KV_B, nq // qb_step),
        in_specs=[pl.BlockSpec((GROUP, qb_step * tq, HEAD_DIM), lambda g, i: (g, i, 0)),
                  pl.BlockSpec((None, t, HEAD_DIM), lambda g, i: (g, 0, 0)),
                  pl.BlockSpec((None, t, HEAD_DIM), lambda g, i: (g, 0, 0))],
        out_specs=[pl.BlockSpec((qb_step * tq, GROUP * HEAD_DIM), lambda g, i: (i, KV_A + g)),
                   pl.BlockSpec((None, qb_step, 1, GROUP * tq), lambda g, i: (g, i, 0, 0))],
        out_shape=[jax.ShapeDtypeStruct((n, MIX_WIDTH), BF16),
                   jax.ShapeDtypeStruct((KV_B, nq, 1, GROUP * tq), F32)],
        scratch_shapes=[pltpu.VMEM((1, rows), F32), pltpu.VMEM((1, rows), F32), pltpu.VMEM((HEAD_DIM, rows), F32)],
        compiler_params=_params(("parallel", "parallel")),
    )(qb, kb, vb)


def _attn_b_bwd(qb, kb, vb, dheads, lse, delta, n):
    t = kb.shape[1]
    nk = t // GB_TK
    tq = GB_TQ
    nq = n // tq
    rows = GROUP * tq

    qb_step = 8 if nq % 8 == 0 else (4 if nq % 4 == 0 else 1)

    def body(q_ref, k_ref, v_ref, do_ref, lse_ref, dl_ref, dq_ref, dk_ref, dv_ref):
        j = pl.program_id(1)
        i = pl.program_id(2)

        @pl.when(i == 0)
        def _():
            dk_ref[...] = jnp.zeros_like(dk_ref)
            dv_ref[...] = jnp.zeros_like(dv_ref)

        @pl.when(j == 0)
        def _():
            for s in range(qb_step):
                dq_ref[qb_step * i + s] = jnp.zeros((HEAD_DIM, rows), F32)

        blocks = [(s, hh) for s in range(qb_step) for hh in range(GROUP)]
        q = jnp.concatenate([q_ref[hh, s * tq:(s + 1) * tq, :] for s, hh in blocks], axis=0)
        do = jnp.concatenate([do_ref[s * tq:(s + 1) * tq, hh * HEAD_DIM:(hh + 1) * HEAD_DIM] for s, hh in blocks], axis=0)
        lse_row = jnp.concatenate([lse_ref[s] for s in range(qb_step)], axis=1)
        dl_row = jnp.concatenate([dl_ref[s] for s in range(qb_step)], axis=1)
        ks, vs = k_ref[...], v_ref[...]
        st = lax.dot_general(ks, q, NT, preferred_element_type=F32)
        p = jnp.exp(st - lse_row)
        dpt = lax.dot_general(vs, do, NT, preferred_element_type=F32)
        ds = (p * (dpt - dl_row)).astype(BF16)
        dv_part = lax.dot_general(p.astype(BF16), do, NN, preferred_element_type=F32)
        dk_part = lax.dot_general(ds, q, NN, preferred_element_type=F32)
        dq_part = lax.dot_general(ks, ds, TN, preferred_element_type=F32)

        dk_ref[...] += dk_part
        dv_ref[...] += dv_part
        for s in range(qb_step):
            dq_ref[qb_step * i + s] += dq_part[:, s * rows:(s + 1) * rows]

    kv = pl.BlockSpec((None, GB_TK, HEAD_DIM), lambda g, j, i: (g, j, 0))
    row = pl.BlockSpec((None, qb_step, 1, rows), lambda g, j, i: (g, i, 0, 0))
    return pl.pallas_call(
        body, name="attn_b_bwd", grid=(KV_B, nk, nq // qb_step),
        in_specs=[pl.BlockSpec((GROUP, qb_step * tq, HEAD_DIM), lambda g, j, i: (g, i, 0)), kv, kv,
                  pl.BlockSpec((qb_step * tq, GROUP * HEAD_DIM), lambda g, j, i: (i, KV_A + g)), row, row],
        out_specs=[pl.BlockSpec((None, nq, HEAD_DIM, rows), lambda g, j, i: (g, 0, 0, 0)), kv, kv],
        out_shape=[jax.ShapeDtypeStruct((KV_B, nq, HEAD_DIM, rows), F32),
                   jax.ShapeDtypeStruct((KV_B, t, HEAD_DIM), F32),
                   jax.ShapeDtypeStruct((KV_B, t, HEAD_DIM), F32)],
        compiler_params=_params(("parallel", "arbitrary", "arbitrary")),
    )(qb, kb, vb, dheads, lse, delta)


def _delta_rows(dheads, heads):
    n = heads.shape[0]
    tq = GB_TQ
    w = GROUP * HEAD_DIM

    def body(a_ref, b_ref, o_ref):
        prod = a_ref[...].astype(F32) * b_ref[...].astype(F32)
        cols = [jnp.sum(prod[:, hh * HEAD_DIM:(hh + 1) * HEAD_DIM].T, axis=0, keepdims=True) for hh in range(GROUP)]
        o_ref[...] = jnp.concatenate(cols, axis=1)

    blk = pl.BlockSpec((tq, w), lambda g, i: (i, KV_A + g))
    return pl.pallas_call(
        body, name="delta_rows", grid=(KV_B, n // tq),
        in_specs=[blk, blk],
        out_specs=pl.BlockSpec((None, None, 1, GROUP * tq), lambda g, i: (g, i, 0, 0)),
        out_shape=jax.ShapeDtypeStruct((KV_B, n // tq, 1, GROUP * tq), F32),
        compiler_params=_params(("parallel", "parallel")),
    )(dheads, heads)


KWIN = 3 * BLOCK


def _window_scores(q, k_ref, j, n, nb, sink_row):
    c = k_ref.shape[0] - n
    start = pl.multiple_of(jnp.clip(j - 1, 0, nb - 3) * BLOCK, BLOCK)
    kw = k_ref[pl.ds(start, KWIN), :]
    kc = k_ref[pl.ds(n, c), :]
    s_loc = lax.dot_general(kw, q, NT, preferred_element_type=F32)
    s_ctx = lax.dot_general(kc, q, NT, preferred_element_type=F32)
    cols = GROUP * BLOCK
    qpos = j * BLOCK + lax.broadcasted_iota(jnp.int32, (KWIN, cols), 1) % BLOCK
    kpos = start + lax.broadcasted_iota(jnp.int32, (KWIN, cols), 0)
    s_loc = jnp.where(jnp.abs(qpos - kpos) <= WINDOW, s_loc, -jnp.inf)
    m = jnp.maximum(jnp.maximum(jnp.max(s_loc, axis=0, keepdims=True), jnp.max(s_ctx, axis=0, keepdims=True)),
                    sink_row)
    e_loc, e_ctx, e_sink = jnp.exp(s_loc - m), jnp.exp(s_ctx - m), jnp.exp(sink_row - m)
    inv = 1.0 / (jnp.sum(e_loc, axis=0, keepdims=True) + jnp.sum(e_ctx, axis=0, keepdims=True) + e_sink)
    return e_loc * inv, e_ctx * inv, e_sink * inv, start


def _sink_row(sink_ref, g):
    return jnp.concatenate([sink_ref[pl.ds(g * GROUP + hh, 1), :] for hh in range(GROUP)], axis=1)


def _attn_a_fwd(qa, ka, va, sink_b, heads_b, n):
    t = ka.shape[1]
    nb = n // BLOCK
    assert nb >= 3
    wb = 8 if nb % 8 == 0 else (4 if nb % 4 == 0 else 1)

    def body(q_ref, k_ref, v_ref, sink_ref, heads_b_ref, o_ref):
        g, jj = pl.program_id(0), pl.program_id(1)
        for s in range(wb):
            j = jj * wb + s
            rows = slice(s * BLOCK, (s + 1) * BLOCK)
            q = q_ref[:, rows, :].reshape(GROUP * BLOCK, HEAD_DIM)
            p_loc, p_ctx, _, start = _window_scores(q, k_ref, j, n, nb, _sink_row(sink_ref, g))
            vw = v_ref[pl.ds(start, KWIN), :]
            vc = v_ref[pl.ds(n, t - n), :]
            ot = (lax.dot_general(vw, p_loc.astype(BF16), TN, preferred_element_type=F32)
                  + lax.dot_general(vc, p_ctx.astype(BF16), TN, preferred_element_type=F32))
            for hh in range(GROUP):
                o_ref[rows, hh * HEAD_DIM:(hh + 1) * HEAD_DIM] = ot[:, hh * BLOCK:(hh + 1) * BLOCK].T.astype(BF16)

    return pl.pallas_call(
        body, name="attn_a_fwd", grid=(KV_A, nb // wb),
        in_specs=[pl.BlockSpec((GROUP, wb * BLOCK, HEAD_DIM), lambda g, j: (g, j, 0)),
                  pl.BlockSpec((None, t, HEAD_DIM), lambda g, j: (g, 0, 0)),
                  pl.BlockSpec((None, t, HEAD_DIM), lambda g, j: (g, 0, 0)),
                  pl.BlockSpec((HEADS_A, HEAD_DIM), lambda g, j: (0, 0)),
                  pl.BlockSpec(memory_space=pl.ANY)],
        out_specs=pl.BlockSpec((wb * BLOCK, GROUP * HEAD_DIM), lambda g, j: (j, g)),
        out_shape=jax.ShapeDtypeStruct((n, MIX_WIDTH), BF16),
        input_output_aliases={4: 0},
        compiler_params=_params(("parallel", "parallel")),
    )(qa, ka, va, sink_b, heads_b)


def _attn_a_bwd(qa, ka, va, sink_b, dheads, n):
    t = ka.shape[1]
    c = t - n
    nb = n // BLOCK
    wb = 8 if nb % 8 == 0 else (4 if nb % 4 == 0 else 1)

    def body(q_ref, k_ref, v_ref, sink_ref, do_ref, dq_ref, dk_ref, dv_ref, dsink_ref):
        g, jj = pl.program_id(0), pl.program_id(1)

        @pl.when(jj == 0)
        def _():
            dk_ref[...] = jnp.zeros_like(dk_ref)
            dv_ref[...] = jnp.zeros_like(dv_ref)
            dsink_ref[...] = jnp.zeros_like(dsink_ref)

        for s in range(wb):
            j = jj * wb + s
            rows = slice(s * BLOCK, (s + 1) * BLOCK)
            q = q_ref[:, rows, :].reshape(GROUP * BLOCK, HEAD_DIM)
            do = _heads_rows(do_ref[rows, :], BLOCK)
            p_loc, p_ctx, p_sink, start = _window_scores(q, k_ref, j, n, nb, _sink_row(sink_ref, g))
            kw, vw = k_ref[pl.ds(start, KWIN), :], v_ref[pl.ds(start, KWIN), :]
            kc, vc = k_ref[pl.ds(n, c), :], v_ref[pl.ds(n, c), :]
            dp_loc = lax.dot_general(vw, do, NT, preferred_element_type=F32)
            dp_ctx = lax.dot_general(vc, do, NT, preferred_element_type=F32)
            dl = jnp.sum(p_loc * dp_loc, axis=0, keepdims=True) + jnp.sum(p_ctx * dp_ctx, axis=0, keepdims=True)
            ds_loc = (p_loc * (dp_loc - dl)).astype(BF16)
            ds_ctx = (p_ctx * (dp_ctx - dl)).astype(BF16)
            dqt = (lax.dot_general(kw, ds_loc, TN, preferred_element_type=F32)
                   + lax.dot_general(kc, ds_ctx, TN, preferred_element_type=F32))
            for hh in range(GROUP):
                dq_ref[hh, rows, :] = dqt[:, hh * BLOCK:(hh + 1) * BLOCK].T
            dk_ref[pl.ds(start, KWIN), :] += lax.dot_general(ds_loc, q, NN, preferred_element_type=F32)
            dv_ref[pl.ds(start, KWIN), :] += lax.dot_general(p_loc.astype(BF16), do, NN, preferred_element_type=F32)
            dk_ref[pl.ds(n, c), :] += lax.dot_general(ds_ctx, q, NN, preferred_element_type=F32)
            dv_ref[pl.ds(n, c), :] += lax.dot_general(p_ctx.astype(BF16), do, NN, preferred_element_type=F32)
            dsk = -(p_sink * dl)
            upd = [jnp.broadcast_to(jnp.sum(dsk[:, hh * BLOCK:(hh + 1) * BLOCK], axis=1, keepdims=True), (1, HEAD_DIM))
                   for hh in range(GROUP)]
            dsink_ref[...] += jnp.concatenate(upd + [jnp.zeros((8 - GROUP, HEAD_DIM), F32)], axis=0)

    res = pl.BlockSpec((None, t, HEAD_DIM), lambda g, j: (g, 0, 0))
    return pl.pallas_call(
        body, name="attn_a_bwd", grid=(KV_A, nb // wb),
        in_specs=[pl.BlockSpec((GROUP, wb * BLOCK, HEAD_DIM), lambda g, j: (g, j, 0)), res, res,
                  pl.BlockSpec((HEADS_A, HEAD_DIM), lambda g, j: (0, 0)),
                  pl.BlockSpec((wb * BLOCK, GROUP * HEAD_DIM), lambda g, j: (j, g))],
        out_specs=[pl.BlockSpec((GROUP, wb * BLOCK, HEAD_DIM), lambda g, j: (g, j, 0)), res, res,
                   pl.BlockSpec((None, 8, HEAD_DIM), lambda g, j: (g, 0, 0))],
        out_shape=[jax.ShapeDtypeStruct((HEADS_A, n, HEAD_DIM), F32),
                   jax.ShapeDtypeStruct((KV_A, t, HEAD_DIM), F32),
                   jax.ShapeDtypeStruct((KV_A, t, HEAD_DIM), F32),
                   jax.ShapeDtypeStruct((KV_A, 8, HEAD_DIM), F32)],
        compiler_params=_params(("parallel", "arbitrary")),
    )(qa, ka, va, sink_b, dheads)


def _ln_stats(r):
    mu = jnp.mean(r, axis=-1, keepdims=True)
    cen = r - mu
    rstd = lax.rsqrt(jnp.mean(cen * cen, axis=-1, keepdims=True) + EPS)
    return cen * rstd, rstd


def _ln_bwd(dy, xhat, rstd, gain):
    dxh = dy * gain
    return rstd * (dxh - jnp.mean(dxh, axis=-1, keepdims=True)
                   - xhat * jnp.mean(dxh * xhat, axis=-1, keepdims=True))


def _accumulate_rows(ref, rows, i):
    pad = [jnp.zeros_like(rows[0])] * (8 - len(rows))
    upd = jnp.concatenate(rows + pad, axis=0)

    @pl.when(i == 0)
    def _():
        ref[...] = upd

    @pl.when(i != 0)
    def _():
        ref[...] += upd


def _colsum(v):
    return jnp.sum(v, axis=0, keepdims=True)


LN_TILE = 256


def _res_ln1(x, a, vec):
    n, d = x.shape

    def body(x_ref, a_ref, v_ref, xh_ref, rs_ref, u_ref):
        r1 = DN_ALPHA * x_ref[...] + v_ref[0:1, :] * a_ref[...]
        xhat, rstd = _ln_stats(r1)
        xh_ref[...] = xhat
        rs_ref[...] = rstd
        x1 = xhat * v_ref[1:2, :] + v_ref[2:3, :]
        u_ref[...] = (x1 * (1.0 + v_ref[3:4, :]) + v_ref[4:5, :]).astype(BF16)

    row = pl.BlockSpec((LN_TILE, d), lambda i: (i, 0))
    return pl.pallas_call(
        body, name="res_ln1", grid=(n // LN_TILE,),
        in_specs=[row, row, pl.BlockSpec((8, d), lambda i: (0, 0))],
        out_specs=[row, pl.BlockSpec((LN_TILE, 1), lambda i: (i, 0)), row],
        out_shape=[jax.ShapeDtypeStruct((n, d), F32), jax.ShapeDtypeStruct((n, 1), F32),
                   jax.ShapeDtypeStruct((n, d), BF16)],
        compiler_params=_params(("parallel",)),
    )(x, a, vec)


def _res_ln2_loss(xhat1, f, target, vec):
    n, d = f.shape

    def body(xh_ref, f_ref, t_ref, v_ref, dr_ref, df_ref, s_ref):
        i = pl.program_id(0)
        x1 = xh_ref[...] * v_ref[1:2, :] + v_ref[2:3, :]
        fv = f_ref[...]
        xhat, rstd = _ln_stats(DN_ALPHA * x1 + v_ref[0:1, :] * fv)
        err = xhat * v_ref[3:4, :] + v_ref[4:5, :] - t_ref[...]
        dy = err * (1.0 / d)
        dr2 = _ln_bwd(dy, xhat, rstd, v_ref[3:4, :])
        dr_ref[...] = dr2
        df_ref[...] = (dr2 * v_ref[0:1, :]).astype(BF16)
        _accumulate_rows(s_ref, [_colsum(dy * xhat), _colsum(dy), _colsum(dr2 * fv),
                                 _colsum(err * err) * (0.5 / d)], i)

    row = pl.BlockSpec((LN_TILE, d), lambda i: (i, 0))
    return pl.pallas_call(
        body, name="res_ln2_loss", grid=(n // LN_TILE,),
        in_specs=[row, row, row, pl.BlockSpec((8, d), lambda i: (0, 0))],
        out_specs=[row, row, pl.BlockSpec((8, d), lambda i: (0, 0))],
        out_shape=[jax.ShapeDtypeStruct((n, d), F32), jax.ShapeDtypeStruct((n, d), BF16),
                   jax.ShapeDtypeStruct((8, d), F32)],
        compiler_params=_params(("arbitrary",)),
    )(xhat1, f, target, vec)


def _ln1_bwd(du2, dr2, xhat1, rstd1, a, vec):
    n, d = du2.shape

    def body(du_ref, dr2_ref, xh_ref, rs_ref, a_ref, v_ref, dxp_ref, da_ref, s_ref):
        i = pl.program_id(0)
        du, xhat = du_ref[...], xh_ref[...]
        x1 = xhat * v_ref[1:2, :] + v_ref[2:3, :]
        dx1 = DN_ALPHA * dr2_ref[...] + du * (1.0 + v_ref[0:1, :])
        dr1 = _ln_bwd(dx1, xhat, rs_ref[...], v_ref[1:2, :])
        dxp_ref[...] = DN_ALPHA * dr1
        da_ref[...] = (dr1 * v_ref[3:4, :]).astype(BF16)
        _accumulate_rows(s_ref, [_colsum(du * x1), _colsum(du), _colsum(dx1 * xhat), _colsum(dx1),
                                 _colsum(dr1 * a_ref[...])], i)

    row = pl.BlockSpec((LN_TILE, d), lambda i: (i, 0))
    return pl.pallas_call(
        body, name="ln1_bwd", grid=(n // LN_TILE,),
        in_specs=[row, row, row, pl.BlockSpec((LN_TILE, 1), lambda i: (i, 0)), row,
                  pl.BlockSpec((8, d), lambda i: (0, 0))],
        out_specs=[row, row, pl.BlockSpec((8, d), lambda i: (0, 0))],
        out_shape=[jax.ShapeDtypeStruct((n, d), F32), jax.ShapeDtypeStruct((n, d), BF16),
                   jax.ShapeDtypeStruct((8, d), F32)],
        compiler_params=_params(("arbitrary",)),
    )(du2, dr2, xhat1, rstd1, a, vec)


def _mod1_bwd(du_all, dxp, x, ctx, mods):
    n, d = x.shape
    nx = n // ROW_TILE

    def body(du_ref, dxp_ref, x_ref, ctx_ref, m_ref, gx_ref, s_ref):
        i = pl.program_id(0)
        du = du_ref[...]
        zero = jnp.zeros((1, d), F32)

        @pl.when(i == 0)
        def _():
            s_ref[...] = jnp.zeros_like(s_ref)

        @pl.when(i < nx)
        def _():
            gx_ref[...] = dxp_ref[...] + du * (1.0 + m_ref[0:1, :])
            s_ref[...] += jnp.concatenate([_colsum(du * x_ref[...]), _colsum(du)] + [zero] * 6, axis=0)

        @pl.when(i >= nx)
        def _():
            s_ref[...] += jnp.concatenate([zero, zero, _colsum(du * ctx_ref[...]), _colsum(du)] + [zero] * 4, axis=0)

    lat = pl.BlockSpec((ROW_TILE, d), lambda i: (jnp.minimum(i, nx - 1), 0))
    return pl.pallas_call(
        body, name="mod1_bwd", grid=(nx + 1,),
        in_specs=[pl.BlockSpec((ROW_TILE, d), lambda i: (i, 0)), lat, lat,
                  pl.BlockSpec((ROW_TILE, d), lambda i: (0, 0)), pl.BlockSpec((8, d), lambda i: (0, 0))],
        out_specs=[lat, pl.BlockSpec((8, d), lambda i: (0, 0))],
        out_shape=[jax.ShapeDtypeStruct((n, d), F32), jax.ShapeDtypeStruct((8, d), F32)],
        compiler_params=_params(("arbitrary",)),
    )(du_all, dxp, x, ctx, mods)


FFN_TM = 1024
FFN_TN = 512


def _gate_up(u2, wg, wu, after):
    n, d = u2.shape
    f = wg.shape[1]

    def body(u_ref, wg_ref, wu_ref, after_ref, g_ref, up_ref, h_ref):
        u = u_ref[...]
        g = lax.dot_general(u, wg_ref[...], NN, preferred_element_type=F32)
        up = lax.dot_general(u, wu_ref[...], NN, preferred_element_type=F32)
        g_ref[...] = g.astype(BF16)
        up_ref[...] = up.astype(BF16)
        h_ref[...] = (g * jax.nn.sigmoid(g) * up).astype(BF16)

    tm = min(FFN_TM, n)
    wspec = pl.BlockSpec((d, FFN_TN), lambda j, i: (0, j))
    ospec = pl.BlockSpec((tm, FFN_TN), lambda j, i: (i, j))
    return pl.pallas_call(
        body, name="gate_up", grid=(f // FFN_TN, n // tm),
        in_specs=[pl.BlockSpec((tm, d), lambda j, i: (i, 0)), wspec, wspec, pl.BlockSpec(memory_space=pl.ANY)],
        out_specs=[ospec, ospec, ospec],
        out_shape=[jax.ShapeDtypeStruct((n, f), BF16)] * 3,
        compiler_params=_params(("parallel", "parallel")),
    )(u2, wg, wu, after)


def _glu_bwd(df, wd, g, u):
    n, d = df.shape
    f = wd.shape[0]

    def body(df_ref, wd_ref, g_ref, u_ref, dg_ref, du_ref):
        dh = lax.dot_general(df_ref[...], wd_ref[...], NT, preferred_element_type=F32)
        gv = g_ref[...].astype(F32)
        sig = jax.nn.sigmoid(gv)
        du_ref[...] = (dh * (gv * sig)).astype(BF16)
        dg_ref[...] = (dh * u_ref[...].astype(F32) * (sig * (1.0 + gv * (1.0 - sig)))).astype(BF16)

    tm = min(FFN_TM, n)
    ospec = pl.BlockSpec((tm, FFN_TN), lambda i, j: (i, j))
    return pl.pallas_call(
        body, name="glu_bwd", grid=(n // tm, f // FFN_TN),
        in_specs=[pl.BlockSpec((tm, d), lambda i, j: (i, 0)),
                  pl.BlockSpec((FFN_TN, d), lambda i, j: (j, 0)), ospec, ospec],
        out_specs=[ospec, ospec],
        out_shape=[jax.ShapeDtypeStruct((n, f), BF16), jax.ShapeDtypeStruct((n, f), BF16)],
        compiler_params=_params(("parallel", "parallel")),
    )(df, wd, g, u)


def _du2(dg, du, wg, wu):
    n, f = dg.shape
    d = wg.shape[0]
    tm, tn, tk = min(1024, n), 1024, 1408
    nk = f // tk

    def body(dg_ref, du_ref, wg_ref, wu_ref, o_ref):
        @pl.when(pl.program_id(2) == 0)
        def _():
            o_ref[...] = jnp.zeros_like(o_ref)

        o_ref[...] += (lax.dot_general(dg_ref[...], wg_ref[...], NT, preferred_element_type=F32)
                       + lax.dot_general(du_ref[...], wu_ref[...], NT, preferred_element_type=F32))

    aspec = pl.BlockSpec((tm, tk), lambda i, j, kk: (i, kk))
    wspec = pl.BlockSpec((tn, tk), lambda i, j, kk: (j, kk))
    return pl.pallas_call(
        body, name="du2", grid=(n // tm, d // tn, nk),
        in_specs=[aspec, aspec, wspec, wspec],
        out_specs=pl.BlockSpec((tm, tn), lambda i, j, kk: (i, j)),
        out_shape=jax.ShapeDtypeStruct((n, d), F32),
        compiler_params=_params(("parallel", "parallel", "arbitrary")),
    )(dg, du, wg, wu)


def _rows8(rows, d=D_MODEL):
    rows = [r.reshape(1, d).astype(F32) for r in rows]
    return jnp.concatenate(rows + [jnp.zeros((8 - len(rows), d), F32)], axis=0)


W_GROUPS = (("w_in",), ("w_out", "w_gate", "w_up"), ("w_down",))
G_GROUPS = (("w_down", "w_gate", "w_up"), ("w_out",), ("w_in",))


def _layer_fwd_bwd(x, ctx, target, mod, mod_ctx, weights, prefetch, grads_out,
                   q_g, k_g, sink, ln1_g, ln1_b, ln2_g, ln2_b):
    n, d = x.shape
    c = ctx.shape[0]
    sh1, sc1, g1, sh2, sc2, g2 = [mod[:, k * d:(k + 1) * d] for k in range(6)]
    csh1, csc1 = mod_ctx[:, 0:d], mod_ctx[:, d:2 * d]
    cos, sin_a, sin_b = _rope_tables(n, c)
    sink_b = jnp.broadcast_to(sink.reshape(HEADS_A, 1), (HEADS_A, HEAD_DIM)).astype(F32)

    u_all = _modulate_rows(x, ctx, _rows8([sc1, sh1, csc1, csh1]))
    (w_in,) = weights(0, u_all)
    h_all = _matmul(u_all, w_in, name="qkv_proj", tm=_fit(n + c, 1088), tn=1024, tk=2048, out_dtype=F32)
    qa, ka, va, qb, kb, vb = _qkv_post(h_all, cos, sin_a, sin_b, q_g, k_g)
    heads_b, lse = _attn_b_fwd(qb, kb, vb, n)
    zero = prefetch(1, heads_b)
    heads = _attn_a_fwd(qa, ka, va, sink_b + zero, heads_b, n)
    w_out, w_gate, w_up = weights(1, heads)
    a = _matmul(heads, w_out, name="out_proj", tm=1024, tn=1024, tk=2048, out_dtype=F32)
    xhat1, rstd1, u2 = _res_ln1(x, a, _rows8([g1, ln1_g, ln1_b, sc2, sh2]))
    zero = prefetch(2, u2)
    gg, uu, hh = _gate_up(u2, w_gate, w_up, zero.reshape(1, 1))
    (w_down,) = weights(2, hh)
    f = _matmul(hh, w_down, name="ffn_down", tm=1024, tn=512, tk=FFN, out_dtype=F32)
    dr2, df, s_ln2 = _res_ln2_loss(xhat1, f, target, _rows8([g2, ln1_g, ln1_b, ln2_g, ln2_b]))

    dgg, duu = _glu_bwd(df, w_down, gg, uu)
    dw_down = _matmul(hh, df, name="dw_down", ta=True, tm=512, tn=1024, tk=n, out_dtype=BF16)
    dw_gate = _matmul(u2, dgg, name="dw_gate", ta=True, tm=1024, tn=512, tk=n, out_dtype=BF16)
    dw_up = _matmul(u2, duu, name="dw_up", ta=True, tm=1024, tn=512, tk=n, out_dtype=BF16)
    zero = grads_out(0, [dw_down, dw_gate, dw_up])
    du2 = _du2(dgg, duu, w_gate, w_up)
    dxp, da, s_ln1 = _ln1_bwd(du2, dr2, xhat1, rstd1, a, _rows8([sc2, ln1_g, ln1_b, g1]) + zero)

    dheads = _matmul(da, w_out, name="d_heads", tb=True, tm=1024, tn=1024, tk=2048, out_dtype=BF16)
    dw_out = _matmul(heads, da, name="dw_out", ta=True, tm=1024, tn=1024, tk=n, out_dtype=BF16)
    zero = grads_out(1, [dw_out])
    delta = _delta_rows(dheads, heads)
    dqa, dka, dva, dsink = _attn_a_bwd(qa, ka, va, sink_b + zero, dheads, n)
    dqb, dkb, dvb = _attn_b_bwd(qb, kb, vb, dheads, lse, delta, n)
    dh_all, s_gain = _qkv_bwd_post(h_all, cos, sin_a, sin_b, q_g, k_g, dqa, dka, dva, dqb, dkb, dvb, n)
    dw_in = _matmul(u_all, dh_all, name="dw_in", ta=True, tm=1024, tn=1024, tk=n + c, out_dtype=BF16)
    zero = grads_out(2, [dw_in])
    du_all = _matmul(dh_all, w_in, name="d_u1", tb=True, tm=_fit(n + c, 1088), tn=1024, tk=IN_WIDTH, out_dtype=F32,
                     after=zero.reshape(1, 1))
    grad_x, s_mod1 = _mod1_bwd(du_all, dxp, x, ctx, _rows8([sc1]) + zero)

    dsink_row = jnp.concatenate([dsink[0, 0:GROUP, 0], dsink[1, 0:GROUP, 0]]).reshape(1, HEADS_A)
    misc = jnp.concatenate([s_gain[0:1], s_gain[1:2], dsink_row,
                            jnp.zeros((1, d - 2 * HEAD_DIM - HEADS_A), F32)], axis=1)
    partial = jnp.concatenate([
        s_mod1[1:2], s_mod1[0:1], s_ln1[4:5],
        s_ln1[1:2], s_ln1[0:1], s_ln2[2:3],
        s_mod1[3:4], s_mod1[2:3],
        s_ln1[2:3], s_ln1[3:4], s_ln2[0:1], s_ln2[1:2],
        s_ln2[3:4], misc, jnp.zeros((2, d), F32)], axis=0)
    return grad_x, partial


ANY = pl.BlockSpec(memory_space=pl.ANY)
VMEM_FULL = pl.BlockSpec(memory_space=pltpu.VMEM)
N_CHIP_PEERS = 3


def _me():
    return lax.axis_index("x"), lax.axis_index("y"), lax.axis_index("c")


def _other_chips(x, y):
    return [(1 - x, y), (x, 1 - y), (1 - x, 1 - y)]


def _shard_of(chip):
    return 2 * chip[0] + chip[1]


def _dev_index(x, y, c):
    return 4 * x + 2 * y + c


def _rcopy(src, dst, send_sems, recv_sems, k, dev):
    return pltpu.make_async_remote_copy(src_ref=src, dst_ref=dst, send_sem=send_sems.at[k], recv_sem=recv_sems.at[k],
                                        device_id=dev, device_id_type=MESH)


BIG = (("w_in", (D_MODEL, IN_WIDTH), 1), ("w_out", (MIX_WIDTH, D_MODEL), 0), ("w_gate", (D_MODEL, FFN), 1),
       ("w_up", (D_MODEL, FFN), 1), ("w_down", (FFN, D_MODEL), 0))


def _sub(ref, axis, idx, size):
    start = pl.multiple_of(idx * size, size)
    return ref.at[pl.ds(start, size), :] if axis == 0 else ref.at[:, pl.ds(start, size)]


def _shape_div(shape, axis, parts):
    return tuple(s // parts if a == axis else s for a, s in enumerate(shape))


def _piece(a, ref, shard, half):
    _, full, axis = BIG[a]
    view = _sub(ref, axis, shard, full[axis] // N_SHARD)
    return _sub(view, 1 - axis, half, full[1 - axis] // 2)


HBM = pl.BlockSpec(memory_space=pltpu.HBM)
SEM = pl.BlockSpec(memory_space=pltpu.SEMAPHORE)
EFFECT = pltpu.SideEffectType.DATAFLOW_SIDE_EFFECTING
BIG_INDEX = {name: a for a, (name, _, _) in enumerate(BIG)}


def _in_hbm(arr):
    return pltpu.with_memory_space_constraint(arr, pltpu.HBM)


def _gather_start(tag, arrs, bufs, prev):
    n_arr = len(arrs)

    def body(*refs):
        ins = refs[:n_arr]
        send_sems, recv_sems = refs[n_arr + 1], refs[n_arr + 2]
        token = refs[-1]
        x, y, c = _me()
        s_me = _shard_of((x, y))
        for i, a in enumerate(arrs):
            mine = _piece(a, ins[i], s_me, c)
            for j, chip in enumerate(_other_chips(x, y)):
                _rcopy(mine, mine, send_sems, recv_sems, N_CHIP_PEERS * i + j, (*chip, c)).start()
        token[...] = jnp.zeros_like(token)

    n_sem = N_CHIP_PEERS * n_arr
    outs = pl.pallas_call(
        body, name="gather_start_" + tag,
        in_specs=[HBM] * n_arr + [ANY],
        out_specs=[SEM, SEM] + [HBM] * n_arr + [VMEM_FULL],
        out_shape=[pltpu.SemaphoreType.DMA((n_sem,)), pltpu.SemaphoreType.DMA((n_sem,))]
        + [pltpu.HBM(BIG[a][1], BF16) for a in arrs] + [jax.ShapeDtypeStruct((8, HEAD_DIM), F32)],
        input_output_aliases={i: 2 + i for i in range(n_arr)},
        compiler_params=pltpu.CompilerParams(has_side_effects=EFFECT),
    )(*[_in_hbm(b) for b in bufs], prev)
    return outs[0], outs[1], list(outs[2:2 + n_arr]), outs[-1]


def _gather_wait(tag, arrs, send_sems, recv_sems, bufs, after):
    n_arr = len(arrs)

    def body(*refs):
        ins = refs[:n_arr]
        send_sems_, recv_sems_ = refs[n_arr], refs[n_arr + 1]
        x, y, c = _me()
        s_me = _shard_of((x, y))
        for i, a in enumerate(arrs):
            mine = _piece(a, ins[i], s_me, c)
            for j, chip in enumerate(_other_chips(x, y)):
                landed = _piece(a, ins[i], _shard_of(chip), c)
                cp = _rcopy(mine, landed, send_sems_, recv_sems_, N_CHIP_PEERS * i + j, (*chip, c))
                cp.wait_send()
                cp.wait_recv()

    outs = pl.pallas_call(
        body, name="gather_wait_" + tag,
        in_specs=[HBM] * n_arr + [SEM, SEM, ANY],
        out_specs=[HBM] * n_arr,
        out_shape=[pltpu.HBM(BIG[a][1], BF16) for a in arrs],
        input_output_aliases={i: i for i in range(n_arr)},
        compiler_params=pltpu.CompilerParams(has_side_effects=EFFECT),
    )(*bufs, send_sems, recv_sems, after)
    return list(outs)


def _gather_forward(tag, arrs, bufs):
    n_arr = len(arrs)

    def body(*refs):
        outs = refs[n_arr:2 * n_arr]
        send_sems, recv_sems = refs[2 * n_arr:]
        x, y, c = _me()
        sibling = (x, y, 1 - c)
        chips = _other_chips(x, y)
        copies = []
        for i, a in enumerate(arrs):
            for j, chip in enumerate(chips):
                landed = _piece(a, outs[i], _shard_of(chip), c)
                cp = _rcopy(landed, landed, send_sems, recv_sems, N_CHIP_PEERS * i + j, sibling)
                cp.start()
                copies.append(cp)
        for i, a in enumerate(arrs):
            for j, chip in enumerate(chips):
                other = _piece(a, outs[i], _shard_of(chip), 1 - c)
                _rcopy(other, other, send_sems, recv_sems, N_CHIP_PEERS * i + j, sibling).wait_recv()
        for cp in copies:
            cp.wait_send()

    n_sem = N_CHIP_PEERS * n_arr
    return list(pl.pallas_call(
        body, name="gather_forward_" + tag,
        in_specs=[ANY] * n_arr, out_specs=[ANY] * n_arr,
        out_shape=[jax.ShapeDtypeStruct(BIG[a][1], BF16) for a in arrs],
        input_output_aliases={i: i for i in range(n_arr)},
        scratch_shapes=[pltpu.SemaphoreType.DMA((n_sem,)), pltpu.SemaphoreType.DMA((n_sem,))],
    )(*bufs))


def _forward_start(tag, arrs, bufs):
    n_arr = len(arrs)

    def body(*refs):
        ins = refs[:n_arr]
        send_sems, recv_sems = refs[n_arr], refs[n_arr + 1]
        token = refs[-1]
        x, y, c = _me()
        for i, a in enumerate(arrs):
            for j, chip in enumerate(_other_chips(x, y)):
                landed = _piece(a, ins[i], _shard_of(chip), c)
                _rcopy(landed, landed, send_sems, recv_sems, N_CHIP_PEERS * i + j, (x, y, 1 - c)).start()
        token[...] = jnp.zeros_like(token)

    n_sem = N_CHIP_PEERS * n_arr
    outs = pl.pallas_call(
        body, name="gather_forward_start_" + tag,
        in_specs=[HBM] * n_arr,
        out_specs=[SEM, SEM] + [HBM] * n_arr + [VMEM_FULL],
        out_shape=[pltpu.SemaphoreType.DMA((n_sem,)), pltpu.SemaphoreType.DMA((n_sem,))]
        + [pltpu.HBM(BIG[a][1], BF16) for a in arrs] + [jax.ShapeDtypeStruct((8, HEAD_DIM), F32)],
        input_output_aliases={i: 2 + i for i in range(n_arr)},
        compiler_params=pltpu.CompilerParams(has_side_effects=EFFECT),
    )(*bufs)
    return outs[0], outs[1], list(outs[2:2 + n_arr]), outs[-1]


def _forward_wait(tag, arrs, send_sems, recv_sems, bufs, after):
    n_arr = len(arrs)

    def body(*refs):
        ins = refs[:n_arr]
        send_sems_, recv_sems_ = refs[n_arr], refs[n_arr + 1]
        x, y, c = _me()
        for i, a in enumerate(arrs):
            for j, chip in enumerate(_other_chips(x, y)):
                mine = _piece(a, ins[i], _shard_of(chip), c)
                other = _piece(a, ins[i], _shard_of(chip), 1 - c)
                cp = _rcopy(mine, other, send_sems_, recv_sems_, N_CHIP_PEERS * i + j, (x, y, 1 - c))
                cp.wait_send()
                cp.wait_recv()

    outs = pl.pallas_call(
        body, name="gather_forward_wait_" + tag,
        in_specs=[HBM] * n_arr + [SEM, SEM, ANY],
        out_specs=[HBM] * n_arr,
        out_shape=[pltpu.HBM(BIG[a][1], BF16) for a in arrs],
        input_output_aliases={i: i for i in range(n_arr)},
        compiler_params=pltpu.CompilerParams(has_side_effects=EFFECT),
    )(*bufs, send_sems, recv_sems, after)
    return list(outs)


def _peers(x, y, c):
    return [(x ^ (mask >> 2), y ^ ((mask >> 1) & 1), c ^ (mask & 1)) for mask in range(1, N_DEV)]


def _received_shape(a):
    _, full, axis = BIG[a]
    return (N_DEV - 1,) + _shape_div(_shape_div(full, 1 - axis, 2), axis, N_SHARD)


def _pieces_start(tag, arrs, dws):
    n_arr = len(arrs)

    def body(*refs):
        srcs, lands = refs[:n_arr], refs[n_arr:2 * n_arr]
        send_sems, recv_sems = refs[2 * n_arr], refs[2 * n_arr + 1]
        token = refs[-1]
        x, y, c = _me()
        for i, a in enumerate(arrs):
            for k, peer in enumerate(_peers(x, y, c)):
                src = _piece(a, srcs[i], _shard_of(peer[:2]), peer[2])
                _rcopy(src, lands[i].at[k], send_sems, recv_sems, (N_DEV - 1) * i + k, peer).start()
        token[...] = jnp.zeros_like(token)

    n_sem = (N_DEV - 1) * n_arr
    lands = [_in_hbm(lax.empty(_received_shape(a), BF16)) for a in arrs]
    outs = pl.pallas_call(
        body, name="grad_pieces_start_" + tag,
        in_specs=[HBM] * (2 * n_arr),
        out_specs=[SEM, SEM] + [HBM] * (2 * n_arr) + [VMEM_FULL],
        out_shape=[pltpu.SemaphoreType.DMA((n_sem,)), pltpu.SemaphoreType.DMA((n_sem,))]
        + [pltpu.HBM(BIG[a][1], BF16) for a in arrs] + [pltpu.HBM(_received_shape(a), BF16) for a in arrs]
        + [jax.ShapeDtypeStruct((8, HEAD_DIM), F32)],
        input_output_aliases={i: 2 + i for i in range(2 * n_arr)},
        compiler_params=pltpu.CompilerParams(has_side_effects=EFFECT),
    )(*[_in_hbm(dw) for dw in dws], *lands)
    return outs[0], outs[1], list(outs[2:2 + n_arr]), list(outs[2 + n_arr:2 + 2 * n_arr]), outs[-1]


def _pieces_wait(tag, arrs, send_sems, recv_sems, dws, lands, after):
    n_arr = len(arrs)

    def body(*refs):
        srcs, lands_ = refs[:n_arr], refs[n_arr:2 * n_arr]
        send_sems_, recv_sems_ = refs[2 * n_arr], refs[2 * n_arr + 1]
        x, y, c = _me()
        for i, a in enumerate(arrs):
            for k, peer in enumerate(_peers(x, y, c)):
                src = _piece(a, srcs[i], _shard_of(peer[:2]), peer[2])
                cp = _rcopy(src, lands_[i].at[k], send_sems_, recv_sems_, (N_DEV - 1) * i + k, peer)
                cp.wait_send()
                cp.wait_recv()

    outs = pl.pallas_call(
        body, name="grad_pieces_wait_" + tag,
        in_specs=[HBM] * (2 * n_arr) + [SEM, SEM, ANY],
        out_specs=[HBM] * (2 * n_arr),
        out_shape=[pltpu.HBM(BIG[a][1], BF16) for a in arrs] + [pltpu.HBM(_received_shape(a), BF16) for a in arrs],
        input_output_aliases={i: i for i in range(2 * n_arr)},
        compiler_params=pltpu.CompilerParams(has_side_effects=EFFECT),
    )(*dws, *lands, send_sems, recv_sems, after)
    return list(outs[:n_arr]), list(outs[n_arr:])


def _join_start(tag, g_halves):
    n_arr = len(g_halves)

    def body(*refs):
        srcs, lands = refs[:n_arr], refs[n_arr:2 * n_arr]
        send_sems, recv_sems = refs[2 * n_arr], refs[2 * n_arr + 1]
        token = refs[-1]
        x, y, c = _me()
        for i in range(n_arr):
            _rcopy(srcs[i], lands[i], send_sems, recv_sems, i, (x, y, 1 - c)).start()
        token[...] = jnp.zeros_like(token)

    shapes = [pltpu.HBM(g.shape, F32) for g in g_halves]
    outs = pl.pallas_call(
        body, name="grad_join_start_" + tag,
        in_specs=[HBM] * (2 * n_arr),
        out_specs=[SEM, SEM] + [HBM] * (2 * n_arr) + [VMEM_FULL],
        out_shape=[pltpu.SemaphoreType.DMA((n_arr,)), pltpu.SemaphoreType.DMA((n_arr,))] + shapes + shapes
        + [jax.ShapeDtypeStruct((8, HEAD_DIM), F32)],
        input_output_aliases={i: 2 + i for i in range(2 * n_arr)},
        compiler_params=pltpu.CompilerParams(has_side_effects=EFFECT),
    )(*[_in_hbm(g) for g in g_halves], *[_in_hbm(lax.empty(g.shape, F32)) for g in g_halves])
    return outs[0], outs[1], list(outs[2:2 + n_arr]), list(outs[2 + n_arr:2 + 2 * n_arr]), outs[-1]


def _join_wait(tag, send_sems, recv_sems, g_halves, lands, after):
    n_arr = len(g_halves)

    def body(*refs):
        srcs, lands_ = refs[:n_arr], refs[n_arr:2 * n_arr]
        send_sems_, recv_sems_ = refs[2 * n_arr], refs[2 * n_arr + 1]
        x, y, c = _me()
        for i in range(n_arr):
            cp = _rcopy(srcs[i], lands_[i], send_sems_, recv_sems_, i, (x, y, 1 - c))
            cp.wait_send()
            cp.wait_recv()

    shapes = [pltpu.HBM(g.shape, F32) for g in g_halves]
    outs = pl.pallas_call(
        body, name="grad_join_wait_" + tag,
        in_specs=[HBM] * (2 * n_arr) + [SEM, SEM, ANY],
        out_specs=[HBM] * (2 * n_arr),
        out_shape=shapes + shapes,
        input_output_aliases={i: i for i in range(2 * n_arr)},
        compiler_params=pltpu.CompilerParams(has_side_effects=EFFECT),
    )(*g_halves, *lands, send_sems, recv_sems, after)
    return list(outs[:n_arr]), list(outs[n_arr:])


def _piece_sum(a, dw, shard, core, received):
    name, full, axis = BIG[a]
    rows, cols = _received_shape(a)[1:]
    tr = _fit(rows, ROW_TILE)
    nbr = rows // tr

    def body(w_ref, dw_ref, rec_ref, o_ref):
        acc = dw_ref[...].astype(F32)
        for k in range(N_DEV - 1):
            acc = acc + rec_ref[k].astype(F32)
        o_ref[...] = acc

    if axis == 0:
        own = pl.BlockSpec((tr, cols), lambda i, w: (w[0] * nbr + i, w[1]))
    else:
        own = pl.BlockSpec((tr, cols), lambda i, w: (w[1] * nbr + i, w[0]))
    return pl.pallas_call(
        body, name="grad_sum_pieces_" + name,
        grid_spec=pltpu.PrefetchScalarGridSpec(
            num_scalar_prefetch=1, grid=(nbr,),
            in_specs=[own, pl.BlockSpec((N_DEV - 1, tr, cols), lambda i, w: (0, i, 0))],
            out_specs=pl.BlockSpec((tr, cols), lambda i, w: (i, 0))),
        out_shape=jax.ShapeDtypeStruct((rows, cols), F32),
        compiler_params=_params(("parallel",)),
    )(jnp.stack([shard, core]).astype(jnp.int32), dw, received)


def _scatter_begin(tag, arrs, dws):
    send_sems, recv_sems, dws, lands, token = _pieces_start(tag, arrs, dws)
    return (send_sems, recv_sems, dws, lands), token[0, 0]


def _scatter_reduce(tag, arrs, state, after):
    x, y, c = _me()
    send_sems, recv_sems, dws, lands = state
    dws, lands = _pieces_wait(tag, arrs, send_sems, recv_sems, dws, lands, after)
    g_own = [_piece_sum(a, dw, _shard_of((x, y)), c, r) for a, dw, r in zip(arrs, dws, lands)]
    send_sems, recv_sems, g_own, lands, token = _join_start(tag, g_own)
    return (send_sems, recv_sems, g_own, lands), token


def _scatter_end(tag, state, after):
    return _join_wait(tag, *state, after)


def _rows_start(tag, block):
    r, d = block.shape

    def body(src, land, send_sems, recv_sems, src_thru, land_thru, token):
        x, y, c = _me()
        for k, peer in enumerate(_peers(x, y, c)):
            _rcopy(src, land.at[_dev_index(x, y, c)], send_sems, recv_sems, k, peer).start()
        token[...] = jnp.zeros_like(token)

    outs = pl.pallas_call(
        body, name="rows_start_" + tag,
        in_specs=[HBM, HBM],
        out_specs=[SEM, SEM, HBM, HBM, VMEM_FULL],
        out_shape=[pltpu.SemaphoreType.DMA((N_DEV - 1,)), pltpu.SemaphoreType.DMA((N_DEV - 1,)),
                   pltpu.HBM((r, d), F32), pltpu.HBM((N_DEV, r, d), F32), jax.ShapeDtypeStruct((8, HEAD_DIM), F32)],
        input_output_aliases={0: 2, 1: 3},
        compiler_params=pltpu.CompilerParams(has_side_effects=EFFECT),
    )(_in_hbm(block), _in_hbm(lax.empty((N_DEV, r, d), F32)))
    return outs[0], outs[1], outs[2], outs[3], outs[4]


def _rows_wait(tag, send_sems, recv_sems, block, land, after):
    r, d = block.shape

    def body(src, land_, send_sems_, recv_sems_, after_ref, src_thru, land_thru):
        x, y, c = _me()
        for k, peer in enumerate(_peers(x, y, c)):
            cp = _rcopy(src, land_.at[_dev_index(*peer)], send_sems_, recv_sems_, k, peer)
            cp.wait_send()
            cp.wait_recv()

    outs = pl.pallas_call(
        body, name="rows_wait_" + tag,
        in_specs=[HBM, HBM, SEM, SEM, ANY],
        out_specs=[HBM, HBM],
        out_shape=[pltpu.HBM((r, d), F32), pltpu.HBM((N_DEV, r, d), F32)],
        input_output_aliases={0: 0, 1: 1},
        compiler_params=pltpu.CompilerParams(has_side_effects=EFFECT),
    )(block, land, send_sems, recv_sems, after)
    me = _dev_index(*_me())
    return lax.dynamic_update_slice(outs[1], outs[0][None], (me, 0, 0))


ADA_ROWS = 80
ADA_W = 6 * D_MODEL // N_SHARD


def _ada_forward(c_block, cctx_block, w_ada, b_shard):
    d = c_block.shape[1]

    def body(c_ref, cc_ref, w_ref, b_ref, act_ref, mods_ref, raw, mloc, send_sems, recv_sems):
        x, y, c = _me()
        me = _dev_index(x, y, c)
        s_me = _shard_of((x, y))
        raw[72:ADA_ROWS, :] = jnp.zeros((ADA_ROWS - 72, d), F32)
        raw[pl.ds(pl.multiple_of(me * 8, 8), 8), :] = c_ref[...]
        raw[64:72, :] = cc_ref[...]
        sends = []
        for mask in range(1, N_DEV):
            peer = (x ^ (mask >> 2), y ^ ((mask >> 1) & 1), c ^ (mask & 1))
            cp = _rcopy(c_ref, raw.at[pl.ds(pl.multiple_of(me * 8, 8), 8), :], send_sems, recv_sems, mask - 1, peer)
            cp.start()
            sends.append(cp)
        for mask in range(1, N_DEV):
            peer = (x ^ (mask >> 2), y ^ ((mask >> 1) & 1), c ^ (mask & 1))
            landed = raw.at[pl.ds(pl.multiple_of(_dev_index(*peer) * 8, 8), 8), :]
            _rcopy(landed, landed, send_sems, recv_sems, mask - 1, peer).wait_recv()
        v = raw[...]
        act = v * jax.nn.sigmoid(v)
        act_ref[...] = act
        mloc[...] = lax.dot_general(act.astype(BF16), w_ref[...].astype(BF16), NN,
                                    preferred_element_type=F32) + b_ref[...]
        mods_ref[s_me, 0:8, :] = mloc[pl.ds(pl.multiple_of(me * 8, 8), 8), :]
        mods_ref[s_me, 8:16, :] = mloc[64:72, :]
        base = N_DEV - 1
        for j, chip in enumerate(_other_chips(x, y)):
            peer = (*chip, c)
            rows = mloc.at[pl.ds(pl.multiple_of(_dev_index(*peer) * 8, 8), 8), :]
            cp = _rcopy(rows, mods_ref.at[s_me, 0:8, :], send_sems, recv_sems, base + 2 * j, peer)
            cp.start()
            sends.append(cp)
            cp = _rcopy(mloc.at[64:72, :], mods_ref.at[s_me, 8:16, :], send_sems, recv_sems, base + 2 * j + 1, peer)
            cp.start()
            sends.append(cp)
        for j, chip in enumerate(_other_chips(x, y)):
            for part in range(2):
                landed = mods_ref.at[_shard_of(chip), 8 * part:8 * part + 8, :]
                _rcopy(landed, landed, send_sems, recv_sems, base + 2 * j + part, (*chip, c)).wait_recv()
        for cp in sends:
            cp.wait_send()

    n_sem = N_DEV - 1 + 2 * N_CHIP_PEERS
    return pl.pallas_call(
        body, name="ada_forward",
        in_specs=[VMEM_FULL] * 4, out_specs=[VMEM_FULL, VMEM_FULL],
        out_shape=[jax.ShapeDtypeStruct((ADA_ROWS, d), F32), jax.ShapeDtypeStruct((N_SHARD, 16, ADA_W), F32)],
        scratch_shapes=[pltpu.VMEM((ADA_ROWS, d), F32), pltpu.VMEM((ADA_ROWS, ADA_W), F32),
                        pltpu.SemaphoreType.DMA((n_sem,)), pltpu.SemaphoreType.DMA((n_sem,))],
        compiler_params=pltpu.CompilerParams(vmem_limit_bytes=VMEM_LIMIT),
    )(c_block, cctx_block, w_ada, b_shard)


def _small_reduce(gathered):
    d = gathered.shape[2]

    def body(g_ref, o_ref):
        tot = g_ref[0]
        for i in range(1, N_DEV):
            tot = tot + g_ref[i]
        o_ref[...] = tot
        o_ref[0:2, :] = tot[0:2] + tot[6:8]
        o_ref[12:13, :] = jnp.broadcast_to(jnp.sum(tot[12:13], axis=1, keepdims=True), (1, d))

    return pl.pallas_call(body, name="small_reduce", in_specs=[VMEM_FULL], out_specs=VMEM_FULL,
                          out_shape=jax.ShapeDtypeStruct((16, d), F32))(gathered)


def _cctx_grad(gathered, c_ctx):
    d = gathered.shape[2]

    def body(g_ref, c_ref, o_ref):
        tot = g_ref[0, 0:1, :]
        for chip in range(1, N_SHARD):
            tot = tot + g_ref[2 * chip, 0:1, :]
        v = c_ref[...]
        sig = jax.nn.sigmoid(v)
        o_ref[...] = tot * (sig * (1.0 + v * (1.0 - sig)))

    return pl.pallas_call(body, name="cctx_grad", in_specs=[VMEM_FULL, VMEM_FULL], out_specs=VMEM_FULL,
                          out_shape=jax.ShapeDtypeStruct((1, d), F32))(gathered, c_ctx.reshape(1, d))


def _cast_into_full(w, shard, full, axis, name):
    r, cdim = w.shape
    tr = _fit(r, ROW_TILE)
    nbr = r // tr

    def body(s_ref, w_ref, o_ref):
        o_ref[...] = w_ref[...].astype(BF16)

    if axis == 0:
        out_spec = pl.BlockSpec((tr, cdim), lambda i, s: (s[0] * nbr + i, 0))
    else:
        out_spec = pl.BlockSpec((tr, cdim), lambda i, s: (i, s[0]))
    return pl.pallas_call(
        body, name=name,
        grid_spec=pltpu.PrefetchScalarGridSpec(
            num_scalar_prefetch=1, grid=(nbr,), in_specs=[pl.BlockSpec((tr, cdim), lambda i, s: (i, 0))],
            out_specs=out_spec),
        out_shape=jax.ShapeDtypeStruct(full, BF16), compiler_params=_params(("parallel",)),
    )(shard.reshape(1).astype(jnp.int32), w)


def _adamw_halves(w, g_own, g_other, m, v, core, axis, name):
    r, cdim = w.shape
    hr, hc = (r // 2, cdim) if axis == 1 else (r, cdim // 2)
    assert g_own.shape == (hr, hc) and g_other.shape == (hr, hc)
    tr = _fit(hr, 256)
    nb = hr // tr
    c1 = 1.0 - ADAM_B1 ** ADAM_STEP
    c2 = 1.0 - ADAM_B2 ** ADAM_STEP

    def body(c_ref, w_ref, go_ref, gt_ref, m_ref, v_ref, g_ref, d_ref, nm_ref, nv_ref):
        gv = jnp.where(pl.program_id(0) == c_ref[0], go_ref[...], gt_ref[...])
        nm = ADAM_B1 * m_ref[...] + (1.0 - ADAM_B1) * gv
        nv = ADAM_B2 * v_ref[...] + (1.0 - ADAM_B2) * (gv * gv)
        g_ref[...] = gv
        nm_ref[...] = nm
        nv_ref[...] = nv
        d_ref[...] = -ADAM_LR * ((nm / c1) / (jnp.sqrt(nv / c2) + ADAM_EPS) + ADAM_WD * w_ref[...])

    if axis == 1:
        big = pl.BlockSpec((tr, hc), lambda p, i, c: (p * nb + i, 0))
    else:
        big = pl.BlockSpec((tr, hc), lambda p, i, c: (i, p))
    own = pl.BlockSpec((tr, hc), lambda p, i, c: (jnp.where(p == c[0], i, 0), 0))
    other = pl.BlockSpec((tr, hc), lambda p, i, c: (jnp.where(p == c[0], 0, i), 0))
    sh = jax.ShapeDtypeStruct((r, cdim), F32)
    return pl.pallas_call(
        body, name=name,
        grid_spec=pltpu.PrefetchScalarGridSpec(
            num_scalar_prefetch=1, grid=(2, nb), in_specs=[big, own, other, big, big], out_specs=[big] * 4),
        out_shape=[sh] * 4, compiler_params=_params(("parallel", "parallel")),
    )(core.reshape(1).astype(jnp.int32), w, g_own, g_other, m, v)


def _adamw(w, g, m, v, name):
    r, cdim = w.shape
    tr = _fit(r, 128) if r % (ROW_TILE // 4) == 0 else r
    c1 = 1.0 - ADAM_B1 ** ADAM_STEP
    c2 = 1.0 - ADAM_B2 ** ADAM_STEP

    def body(w_ref, g_ref, m_ref, v_ref, d_ref, nm_ref, nv_ref):
        gv = g_ref[...]
        nm = ADAM_B1 * m_ref[...] + (1.0 - ADAM_B1) * gv
        nv = ADAM_B2 * v_ref[...] + (1.0 - ADAM_B2) * (gv * gv)
        nm_ref[...] = nm
        nv_ref[...] = nv
        d_ref[...] = -ADAM_LR * ((nm / c1) / (jnp.sqrt(nv / c2) + ADAM_EPS) + ADAM_WD * w_ref[...])

    spec = pl.BlockSpec((tr, cdim), lambda i: (i, 0))
    sh = jax.ShapeDtypeStruct((r, cdim), F32)
    return pl.pallas_call(body, name=name, grid=(r // tr,), in_specs=[spec] * 4, out_specs=[spec] * 3,
                          out_shape=[sh, sh, sh], compiler_params=_params(("parallel",)))(w, g, m, v)


def _adamw_small(ws, gs, ms, vs):
    k = len(ws)
    c1 = 1.0 - ADAM_B1 ** ADAM_STEP
    c2 = 1.0 - ADAM_B2 ** ADAM_STEP

    def body(*refs):
        w_refs, g_refs, m_refs, v_refs = refs[0:k], refs[k:2 * k], refs[2 * k:3 * k], refs[3 * k:4 * k]
        d_refs, nm_refs, nv_refs = refs[4 * k:5 * k], refs[5 * k:6 * k], refs[6 * k:7 * k]
        for i in range(k):
            gv = g_refs[i][...]
            nm = ADAM_B1 * m_refs[i][...] + (1.0 - ADAM_B1) * gv
            nv = ADAM_B2 * v_refs[i][...] + (1.0 - ADAM_B2) * (gv * gv)
            nm_refs[i][...] = nm
            nv_refs[i][...] = nv
            d_refs[i][...] = -ADAM_LR * ((nm / c1) / (jnp.sqrt(nv / c2) + ADAM_EPS) + ADAM_WD * w_refs[i][...])

    shapes = [jax.ShapeDtypeStruct(w.shape, F32) for w in ws]
    outs = pl.pallas_call(body, name="adamw_small", in_specs=[VMEM_FULL] * (4 * k), out_specs=[VMEM_FULL] * (3 * k),
                          out_shape=shapes * 3)(*ws, *gs, *ms, *vs)
    return outs[0:k], outs[k:2 * k], outs[2 * k:3 * k]


SMALL = (("c_ctx", D_MODEL), ("b_ada", 6 * D_MODEL), ("q_norm_g", HEAD_DIM), ("k_norm_g", HEAD_DIM),
         ("sink_logit", HEADS_A), ("ln1_g", D_MODEL), ("ln1_b", D_MODEL), ("ln2_g", D_MODEL), ("ln2_b", D_MODEL))
WEIGHT_ORDER = ("c_ctx", "w_ada", "b_ada", "w_in", "q_norm_g", "k_norm_g", "sink_logit", "w_out", "ln1_g", "ln1_b",
                "w_gate", "w_up", "w_down", "ln2_g", "ln2_b")


def kernel(x, c, ctx, c_ctx, w_ada, b_ada, w_in, q_norm_g, k_norm_g, sink_logit, w_out, ln1_g, ln1_b, w_gate, w_up, w_down, ln2_g, ln2_b, loss_target, m_c_ctx, m_w_ada, m_b_ada, m_w_in, m_q_norm_g, m_k_norm_g, m_sink_logit, m_w_out, m_ln1_g, m_ln1_b, m_w_gate, m_w_up, m_w_down, m_ln2_g, m_ln2_b, v_c_ctx, v_w_ada, v_b_ada, v_w_in, v_q_norm_g, v_k_norm_g, v_sink_logit, v_w_out, v_ln1_g, v_ln1_b, v_w_gate, v_w_up, v_w_down, v_ln2_g, v_ln2_b):
    d = D_MODEL
    w = dict(c_ctx=c_ctx, w_ada=w_ada[0], b_ada=b_ada, w_in=w_in[0], q_norm_g=q_norm_g, k_norm_g=k_norm_g,
             sink_logit=sink_logit, w_out=w_out[0], ln1_g=ln1_g, ln1_b=ln1_b, w_gate=w_gate[0], w_up=w_up[0],
             w_down=w_down[0], ln2_g=ln2_g, ln2_b=ln2_b)
    m = dict(c_ctx=m_c_ctx, w_ada=m_w_ada[0], b_ada=m_b_ada, w_in=m_w_in[0], q_norm_g=m_q_norm_g, k_norm_g=m_k_norm_g,
             sink_logit=m_sink_logit, w_out=m_w_out[0], ln1_g=m_ln1_g, ln1_b=m_ln1_b, w_gate=m_w_gate[0],
             w_up=m_w_up[0], w_down=m_w_down[0], ln2_g=m_ln2_g, ln2_b=m_ln2_b)
    v = dict(c_ctx=v_c_ctx, w_ada=v_w_ada[0], b_ada=v_b_ada, w_in=v_w_in[0], q_norm_g=v_q_norm_g, k_norm_g=v_k_norm_g,
             sink_logit=v_sink_logit, w_out=v_w_out[0], ln1_g=v_ln1_g, ln1_b=v_ln1_b, w_gate=v_w_gate[0],
             w_up=v_w_up[0], w_down=v_w_down[0], ln2_g=v_ln2_g, ln2_b=v_ln2_b)
    mx, my, mc = _me()
    s_me = _shard_of((mx, my))
    me = _dev_index(mx, my, mc)
    pad8 = lambda row: jnp.concatenate([row.reshape(1, -1), jnp.zeros((7, row.size), F32)], axis=0)

    b_shard = lax.dynamic_slice(b_ada, (0, s_me * ADA_W), (1, ADA_W))
    act, mods4 = _ada_forward(pad8(c), pad8(c_ctx), w["w_ada"], b_shard)

    gathers = []
    prev, shard = mods4, s_me
    for k, names in enumerate(W_GROUPS):
        arrs = tuple(BIG_INDEX[name] for name in names)
        bufs = [_cast_into_full(w[name], shard, BIG[a][1], BIG[a][2], "cast_" + name) for name, a in zip(names, arrs)]
        send_sems, recv_sems, thru, prev = _gather_start("g%d" % k, arrs, bufs, prev)
        shard = s_me + prev[0, 0].astype(jnp.int32)
        gathers.append((arrs, send_sems, recv_sems, thru))

    forwards = {}

    def prefetch(k, after):
        arrs, send_sems, recv_sems, thru = gathers[k]
        landed = _gather_wait("g%d" % k, arrs, send_sems, recv_sems, thru, after)
        fwd_send, fwd_recv, landed, token = _forward_start("g%d" % k, arrs, landed)
        forwards[k] = (fwd_send, fwd_recv, landed)
        return token[0, 0]

    def weights(k, after):
        arrs, send_sems, recv_sems, thru = gathers[k]
        if k in forwards:
            return _forward_wait("g%d" % k, arrs, *forwards[k], after)
        landed = _gather_wait("g%d" % k, arrs, send_sems, recv_sems, thru, after)
        return _gather_forward("g%d" % k, arrs, landed)

    mod = jnp.transpose(mods4[:, 0:1, :], (1, 0, 2)).reshape(1, 6 * d) + prev[0, 0]
    mod_ctx = jnp.transpose(mods4[:, 8:9, :], (1, 0, 2)).reshape(1, 6 * d)

    scatters = {}

    def grads_out(k, dws):
        arrs = tuple(BIG_INDEX[name] for name in G_GROUPS[k])
        scatters[k], zero = _scatter_begin("g%d" % k, arrs, dws)
        return zero

    grad_x, partial = _layer_fwd_bwd(x[0], ctx[0], loss_target[0], mod, mod_ctx, weights, prefetch, grads_out,
                                     q_norm_g, k_norm_g, sink_logit, ln1_g, ln1_b, ln2_g, ln2_b)
    grads, delta, new_m, new_v = {}, {}, {}, {}

    p_send, p_recv, partial, p_land, after = _rows_start("partials", partial)
    joins = []
    for k, names in enumerate(G_GROUPS):
        arrs = tuple(BIG_INDEX[name] for name in names)
        state, after = _scatter_reduce("g%d" % k, arrs, scatters[k], after)
        joins.append(state)
    for k, names in enumerate(G_GROUPS):
        g_own, g_other = _scatter_end("g%d" % k, joins[k], after)
        for name, own, other in zip(names, g_own, g_other):
            grads[name], delta[name], new_m[name], new_v[name] = _adamw_halves(
                w[name], own, other, m[name], v[name], mc, BIG[BIG_INDEX[name]][2], "adamw_" + name)
            after = new_v[name]

    gathered = _rows_wait("partials", p_send, p_recv, partial, p_land, after)
    tot = _small_reduce(gathered)
    grads["b_ada"] = tot[0:6].reshape(1, 6 * d)
    grads["ln1_g"], grads["ln1_b"], grads["ln2_g"], grads["ln2_b"] = tot[8:9], tot[9:10], tot[10:11], tot[11:12]
    grads["q_norm_g"] = tot[13:14, 0:HEAD_DIM]
    grads["k_norm_g"] = tot[13:14, HEAD_DIM:2 * HEAD_DIM]
    grads["sink_logit"] = tot[13:14, 2 * HEAD_DIM:2 * HEAD_DIM + HEADS_A]
    loss = tot[12, 0]

    dm_all = gathered[:, 0:6, :].reshape(N_DEV, 6 * d)
    dmc_tot = jnp.concatenate([tot[6:8].reshape(1, 2 * d), jnp.zeros((1, 4 * d), F32)], axis=1)
    dm_rows = jnp.concatenate([pad8(dm_all[i]) for i in range(N_DEV)] + [pad8(dmc_tot), jnp.zeros((8, 6 * d), F32)], axis=0)
    dm_shard = lax.dynamic_slice(dm_rows, (0, s_me * ADA_W), (ADA_ROWS, ADA_W))
    dmc_shard = lax.dynamic_slice(pad8(dmc_tot), (0, s_me * ADA_W), (8, ADA_W))
    cc_part = _matmul(dmc_shard, w["w_ada"], name="d_cctx", tb=True, tm=8, tn=1024, tk=1536, out_dtype=F32)
    c_send, c_recv, cc_part, c_land, c_token = _rows_start("cctx", cc_part)
    grads["w_ada"] = _matmul(act, dm_shard, name="dw_ada", ta=True, tm=1024, tn=1024, tk=ADA_ROWS, out_dtype=F32,
                             after=c_token)
    delta["w_ada"], new_m["w_ada"], new_v["w_ada"] = _adamw(w["w_ada"], grads["w_ada"], m["w_ada"], v["w_ada"],
                                                            "adamw_w_ada")
    gathered_cc = _rows_wait("cctx", c_send, c_recv, cc_part, c_land, new_v["w_ada"])
    grads["c_ctx"] = _cctx_grad(gathered_cc, c_ctx).reshape(d)
    rows = lambda t: [t[name].reshape(1, size) for name, size in SMALL]
    small = _adamw_small(rows(w), rows(grads), rows(m), rows(v))
    for k, (name, size) in enumerate(SMALL):
        delta[name], new_m[name], new_v[name] = [t[k].reshape(w[name].shape) for t in small]
        grads[name] = grads[name].reshape(w[name].shape)

    lead = lambda name, t: t[None] if name in ("w_ada", "w_in", "w_out", "w_gate", "w_up", "w_down") else t
    outs = [loss, grad_x[None]]
    for group in (grads, delta, new_m, new_v):
        outs += [lead(name, group[name]) for name in WEIGHT_ORDER]
    return tuple(outs)
```

```python
import functools
import math

import jax
import jax.numpy as jnp
from jax import lax
from jax.experimental import pallas as pl
from jax.experimental.pallas import tpu as pltpu

F32 = jnp.float32
BF16 = jnp.bfloat16
MESH = pl.DeviceIdType.MESH

D_MODEL = 2048
HEAD_DIM = 128
HEADS_A = 8
HEADS_B = 8
KV_A = 2
KV_B = 2
GROUP = 4
GRID_W = 64
WINDOW = 128
BLOCK = 128
FFN = 5632
IN_WIDTH = 3072
MIX_WIDTH = 2048
ROPE_THETA = 10000.0
EPS = 1e-6
ATTN_SCALE = HEAD_DIM ** -0.5
DN_ALPHA = 2.0 ** 0.25
N_SHARD = 4
N_DEV = 8

ADAM_LR = 0.001
ADAM_B1 = 0.9
ADAM_B2 = 0.999
ADAM_EPS = 1e-08
ADAM_WD = 0.01
ADAM_STEP = 10

QA0, KA0, VA0, QB0, KB0, VB0 = 0, 1024, 1280, 1536, 2560, 2816

VMEM_LIMIT = 56 * 1024 * 1024
ROW_TILE = 256
NN = (((1,), (0,)), ((), ()))
NT = (((1,), (1,)), ((), ()))
TN = (((0,), (0,)), ((), ()))


def _fit(total, pref):
    step = ROW_TILE // 4
    best = step
    for cand in range(step, pref + 1, step):
        if total % cand == 0:
            best = cand
    return best


def _params(sem=None):
    return pltpu.CompilerParams(dimension_semantics=sem, vmem_limit_bytes=VMEM_LIMIT)


def _matmul(a, b, *, name, ta=False, tb=False, tm, tn, tk, out_dtype, after=None):
    m = a.shape[1] if ta else a.shape[0]
    k = a.shape[0] if ta else a.shape[1]
    n = b.shape[0] if tb else b.shape[1]
    assert (b.shape[1] if tb else b.shape[0]) == k
    tm, tn, tk = min(tm, m), min(tn, n), min(tk, k)
    assert m % tm == 0 and n % tn == 0 and k % tk == 0, (name, m, n, k, tm, tn, tk)
    nk = k // tk
    dn = (((0 if ta else 1,), (1 if tb else 0,)), ((), ()))

    def product(a_ref, b_ref):
        return lax.dot_general(a_ref[...].astype(BF16), b_ref[...].astype(BF16), dn, preferred_element_type=F32)

    def body_whole_k(a_ref, b_ref, *rest):
        o_ref = rest[-1]
        o_ref[...] = product(a_ref, b_ref).astype(o_ref.dtype)

    def body(a_ref, b_ref, *rest):
        o_ref, acc_ref = rest[-2:]
        kk = pl.program_id(2)
        part = product(a_ref, b_ref)

        @pl.when(kk == 0)
        def _():
            acc_ref[...] = part

        @pl.when(kk != 0)
        def _():
            acc_ref[...] += part

        @pl.when(kk == nk - 1)
        def _():
            o_ref[...] = acc_ref[...].astype(o_ref.dtype)

    a_spec = (pl.BlockSpec((tk, tm), lambda i, j, kk: (kk, i)) if ta
              else pl.BlockSpec((tm, tk), lambda i, j, kk: (i, kk)))
    b_spec = (pl.BlockSpec((tn, tk), lambda i, j, kk: (j, kk)) if tb
              else pl.BlockSpec((tk, tn), lambda i, j, kk: (kk, j)))
    return pl.pallas_call(
        body_whole_k if nk == 1 else body, name=name, grid=(m // tm, n // tn, nk),
        in_specs=[a_spec, b_spec] + ([] if after is None else [pl.BlockSpec(memory_space=pl.ANY)]),
        out_specs=pl.BlockSpec((tm, tn), lambda i, j, kk: (i, j)),
        out_shape=jax.ShapeDtypeStruct((m, n), out_dtype),
        scratch_shapes=[] if nk == 1 else [pltpu.VMEM((tm, tn), F32)],
        compiler_params=_params(("parallel", "parallel", "arbitrary")),
    )(a, b, *([] if after is None else [after]))


def _modulate_rows(x, ctx, mods):
    n, d = x.shape
    c = ctx.shape[0]
    nx = n // ROW_TILE
    assert c == ROW_TILE

    def body(x_ref, ctx_ref, mods_ref, o_ref):
        i = pl.program_id(0)

        @pl.when(i < nx)
        def _():
            o_ref[...] = (x_ref[...] * (1.0 + mods_ref[0:1, :]) + mods_ref[1:2, :]).astype(BF16)

        @pl.when(i >= nx)
        def _():
            o_ref[...] = (ctx_ref[...] * (1.0 + mods_ref[2:3, :]) + mods_ref[3:4, :]).astype(BF16)

    return pl.pallas_call(
        body, name="modulate_rows", grid=(nx + 1,),
        in_specs=[pl.BlockSpec((ROW_TILE, d), lambda i: (jnp.minimum(i, nx - 1), 0)),
                  pl.BlockSpec((ROW_TILE, d), lambda i: (0, 0)),
                  pl.BlockSpec((8, d), lambda i: (0, 0))],
        out_specs=pl.BlockSpec((ROW_TILE, d), lambda i: (i, 0)),
        out_shape=jax.ShapeDtypeStruct((n + c, d), BF16),
        compiler_params=_params(("parallel",)),
    )(x, ctx, mods)


def _rope_tables(n, c):
    rows = n // GRID_W
    row_ids = jnp.repeat(jnp.arange(rows, dtype=F32), GRID_W)
    col_ids = jnp.tile(jnp.arange(GRID_W, dtype=F32), rows)
    axis_dim = HEAD_DIM // 2
    inv_freq = jnp.power(ROPE_THETA, -jnp.arange(0, axis_dim, 2, dtype=F32) / axis_dim)
    ang_r = row_ids[:, None] * inv_freq
    ang_c = col_ids[:, None] * inv_freq
    ang = jnp.concatenate([ang_r, ang_r, ang_c, ang_c], axis=-1)
    cos, sin = jnp.cos(ang), jnp.sin(ang)
    quarter = (jnp.arange(HEAD_DIM) // (HEAD_DIM // 4)) % 2
    sin_a = jnp.where(quarter == 0, -sin, 0.0)
    sin_b = jnp.where(quarter == 1, sin, 0.0)
    pad = lambda t, v: jnp.concatenate([t, jnp.full((c, HEAD_DIM), v, F32)], axis=0)
    return pad(cos, 1.0), pad(sin_a, 0.0), pad(sin_b, 0.0)


def _rope(x, cos, sin_a, sin_b):
    return x * cos + pltpu.roll(x, 96, 1) * sin_a + pltpu.roll(x, 32, 1) * sin_b


def _rope_t(dy, cos, sin_a, sin_b):
    return dy * cos - pltpu.roll(dy, 96, 1) * sin_a - pltpu.roll(dy, 32, 1) * sin_b


def _rms(x):
    r = lax.rsqrt(jnp.mean(x * x, axis=-1, keepdims=True) + EPS)
    return x * r, r


def _qkv_post(h_all, cos, sin_a, sin_b, q_g, k_g):
    t = h_all.shape[0]
    nt = t // ROW_TILE

    def body(h_ref, cos_ref, sa_ref, sb_ref, qg_ref, kg_ref, qa_ref, ka_ref, va_ref, qb_ref, kb_ref, vb_ref):
        cos_, sa, sb = cos_ref[...], sa_ref[...], sb_ref[...]
        sl = lambda off, hh: h_ref[:, off + hh * HEAD_DIM: off + (hh + 1) * HEAD_DIM]
        for hh in range(HEADS_A):
            qa_ref[hh] = (_rope(sl(QA0, hh), cos_, sa, sb) * ATTN_SCALE).astype(BF16)
        for hh in range(KV_A):
            ka_ref[hh] = _rope(sl(KA0, hh), cos_, sa, sb).astype(BF16)
            va_ref[hh] = sl(VA0, hh).astype(BF16)
        for hh in range(HEADS_B):
            xn, _ = _rms(sl(QB0, hh))
            qb_ref[hh] = (_rope(xn * qg_ref[...], cos_, sa, sb) * ATTN_SCALE).astype(BF16)
        for hh in range(KV_B):
            xn, _ = _rms(sl(KB0, hh))
            kb_ref[hh] = _rope(xn * kg_ref[...], cos_, sa, sb).astype(BF16)
            vb_ref[hh] = sl(VB0, hh).astype(BF16)

    tab = pl.BlockSpec((ROW_TILE, HEAD_DIM), lambda i: (i, 0))
    gain = pl.BlockSpec((1, HEAD_DIM), lambda i: (0, 0))
    hs = lambda nh: pl.BlockSpec((nh, ROW_TILE, HEAD_DIM), lambda i: (0, i, 0))
    sh = lambda nh: jax.ShapeDtypeStruct((nh, t, HEAD_DIM), BF16)
    return pl.pallas_call(
        body, name="qkv_post", grid=(nt,),
        in_specs=[pl.BlockSpec((ROW_TILE, IN_WIDTH), lambda i: (i, 0)), tab, tab, tab, gain, gain],
        out_specs=[hs(HEADS_A), hs(KV_A), hs(KV_A), hs(HEADS_B), hs(KV_B), hs(KV_B)],
        out_shape=[sh(HEADS_A), sh(KV_A), sh(KV_A), sh(HEADS_B), sh(KV_B), sh(KV_B)],
        compiler_params=_params(("parallel",)),
    )(h_all, cos, sin_a, sin_b, q_g, k_g)


def _qkv_bwd_post(h_all, cos, sin_a, sin_b, q_g, k_g, dqa, dka, dva, dqb, dkb, dvb, n):
    t = h_all.shape[0]
    nt = t // ROW_TILE
    nx = n // ROW_TILE

    def body(h_ref, cos_ref, sa_ref, sb_ref, qg_ref, kg_ref,
             dqa_ref, dka_ref, dva_ref, dqb_ref, dkb_ref, dvb_ref, dh_ref, gs_ref):
        i = pl.program_id(0)
        cos_, sa, sb = cos_ref[...], sa_ref[...], sb_ref[...]
        latent = (i < nx).astype(F32)
        sl = lambda off, hh: h_ref[:, off + hh * HEAD_DIM: off + (hh + 1) * HEAD_DIM]

        def put(off, hh, val):
            dh_ref[:, off + hh * HEAD_DIM: off + (hh + 1) * HEAD_DIM] = val.astype(BF16)

        def norm_bwd(x, gain, dy):
            xn, r = _rms(x)
            dxh = dy * gain
            dx = r * (dxh - xn * jnp.mean(dxh * xn, axis=-1, keepdims=True))
            return dx, jnp.sum(dy * xn, axis=0, keepdims=True)

        for hh in range(HEADS_A):
            put(QA0, hh, _rope_t(dqa_ref[hh] * (ATTN_SCALE * latent), cos_, sa, sb))
        for hh in range(KV_A):
            put(KA0, hh, _rope_t(dka_ref[hh], cos_, sa, sb))
            put(VA0, hh, dva_ref[hh])
        gq = jnp.zeros((1, HEAD_DIM), F32)
        gk = jnp.zeros((1, HEAD_DIM), F32)
        for hh in range(HEADS_B):
            dq_t = dqb_ref[hh // GROUP, :, (hh % GROUP) * ROW_TILE:(hh % GROUP + 1) * ROW_TILE]
            dy = _rope_t(dq_t.T * (ATTN_SCALE * latent), cos_, sa, sb)
            dx, g = norm_bwd(sl(QB0, hh), qg_ref[...], dy)
            put(QB0, hh, dx)
            gq = gq + g
        for hh in range(KV_B):
            dy = _rope_t(dkb_ref[hh], cos_, sa, sb)
            dx, g = norm_bwd(sl(KB0, hh), kg_ref[...], dy)
            put(KB0, hh, dx)
            gk = gk + g
            put(VB0, hh, dvb_ref[hh])
        upd = jnp.concatenate([gq, gk, jnp.zeros((6, HEAD_DIM), F32)], axis=0)

        @pl.when(i == 0)
        def _():
            gs_ref[...] = upd

        @pl.when(i != 0)
        def _():
            gs_ref[...] += upd

    tab = pl.BlockSpec((ROW_TILE, HEAD_DIM), lambda i: (i, 0))
    gain = pl.BlockSpec((1, HEAD_DIM), lambda i: (0, 0))
    lat = lambda nh: pl.BlockSpec((nh, ROW_TILE, HEAD_DIM), lambda i: (0, jnp.minimum(i, nx - 1), 0))
    full = lambda nh: pl.BlockSpec((nh, ROW_TILE, HEAD_DIM), lambda i: (0, i, 0))
    return pl.pallas_call(
        body, name="qkv_bwd_post", grid=(nt,),
        in_specs=[pl.BlockSpec((ROW_TILE, IN_WIDTH), lambda i: (i, 0)), tab, tab, tab, gain, gain,
                  lat(HEADS_A), full(KV_A), full(KV_A),
                  pl.BlockSpec((KV_B, None, HEAD_DIM, GROUP * ROW_TILE), lambda i: (0, jnp.minimum(i, nx - 1), 0, 0)),
                  full(KV_B), full(KV_B)],
        out_specs=[pl.BlockSpec((ROW_TILE, IN_WIDTH), lambda i: (i, 0)),
                   pl.BlockSpec((8, HEAD_DIM), lambda i: (0, 0))],
        out_shape=[jax.ShapeDtypeStruct((t, IN_WIDTH), BF16), jax.ShapeDtypeStruct((8, HEAD_DIM), F32)],
        compiler_params=_params(("arbitrary",)),
    )(h_all, cos, sin_a, sin_b, q_g, k_g, dqa, dka, dva, dqb, dkb, dvb)


GB_TQ = 256
GB_TK = 256


def _heads_rows(ref2d, tq):
    return jnp.concatenate([ref2d[:, hh * HEAD_DIM:(hh + 1) * HEAD_DIM] for hh in range(GROUP)], axis=0)


def _attn_b_fwd(qb, kb, vb, n):
    t = kb.shape[1]
    nk = t // GB_TK
    tq = GB_TQ
    nq = n // tq
    qb_step = 8 if nq % 8 == 0 else (4 if nq % 4 == 0 else 1)
    rows = qb_step * GROUP * tq

    def body(q_ref, k_ref, v_ref, o_ref, lse_ref, m_s, l_s, acc_s):
        blocks = [(s, hh) for s in range(qb_step) for hh in range(GROUP)]
        q = jnp.concatenate([q_ref[hh, s * tq:(s + 1) * tq, :] for s, hh in blocks], axis=0)
        m_s[...] = jnp.full((1, rows), -jnp.inf, F32)
        l_s[...] = jnp.zeros((1, rows), F32)
        acc_s[...] = jnp.zeros((HEAD_DIM, rows), F32)

        def scores(j):
            start = pl.multiple_of(j * GB_TK, GB_TK)
            return lax.dot_general(k_ref[pl.ds(start, GB_TK), :], q, NT, preferred_element_type=F32)

        def step(j, carry):
            st = scores(j)
            vs = v_ref[pl.ds(pl.multiple_of(j * GB_TK, GB_TK), GB_TK), :]
            m_prev = m_s[...]
            m_new = jnp.maximum(m_prev, jnp.max(st, axis=0, keepdims=True))
            p = jnp.exp(st - m_new)
            alpha = jnp.exp(m_prev - m_new)
            l_s[...] = alpha * l_s[...] + jnp.sum(p, axis=0, keepdims=True)
            acc_s[...] = alpha * acc_s[...] + lax.dot_general(vs, p.astype(BF16), TN, preferred_element_type=F32)
            m_s[...] = m_new
            return carry

        lax.fori_loop(0, nk, step, 0, unroll=4)
        ot = acc_s[...] * (1.0 / l_s[...])
        lse = m_s[...] + jnp.log(l_s[...])
        width = GROUP * tq
        for s in range(qb_step):
            lse_ref[s] = lse[:, s * width:(s + 1) * width]
        for k, (s, hh) in enumerate(blocks):
            o_ref[s * tq:(s + 1) * tq, hh * HEAD_DIM:(hh + 1) * HEAD_DIM] = ot[:, k * tq:(k + 1) * tq].T.astype(BF16)

    return pl.pallas_call(
        body, name="attn_b_fwd", grid=(KV_B, nq // qb_step),
        in_specs=[pl.BlockSpec((GROUP, qb_step * tq, HEAD_DIM), lambda g, i: (g, i, 0)),
                  pl.BlockSpec((None, t, HEAD_DIM), lambda g, i: (g, 0, 0)),
                  pl.BlockSpec((None, t, HEAD_DIM), lambda g, i: (g, 0, 0))],
        out_specs=[pl.BlockSpec((qb_step * tq, GROUP * HEAD_DIM), lambda g, i: (i, KV_A + g)),
                   pl.BlockSpec((None, qb_step, 1, GROUP * tq), lambda g, i: (g, i, 0, 0))],
        out_shape=[jax.ShapeDtypeStruct((n, MIX_WIDTH), BF16),
                   jax.ShapeDtypeStruct((KV_B, nq, 1, GROUP * tq), F32)],
        scratch_shapes=[pltpu.VMEM((1, rows), F32), pltpu.VMEM((1, rows), F32), pltpu.VMEM((HEAD_DIM, rows), F32)],
        compiler_params=_params(("parallel", "parallel")),
    )(qb, kb, vb)


def _attn_b_bwd(qb, kb, vb, dheads, lse, delta, n):
    t = kb.shape[1]
    nk = t // GB_TK
    tq = GB_TQ
    nq = n // tq
    rows = GROUP * tq

    qb_step = 8 if nq % 8 == 0 else (4 if nq % 4 == 0 else 1)

    def body(q_ref, k_ref, v_ref, do_ref, lse_ref, dl_ref, dq_ref, dk_ref, dv_ref):
        j = pl.program_id(1)
        i = pl.program_id(2)

        @pl.when(i == 0)
        def _():
            dk_ref[...] = jnp.zeros_like(dk_ref)
            dv_ref[...] = jnp.zeros_like(dv_ref)

        @pl.when(j == 0)
        def _():
            for s in range(qb_step):
                dq_ref[qb_step * i + s] = jnp.zeros((HEAD_DIM, rows), F32)

        blocks = [(s, hh) for s in range(qb_step) for hh in range(GROUP)]
        q = jnp.concatenate([q_ref[hh, s * tq:(s + 1) * tq, :] for s, hh in blocks], axis=0)
        do = jnp.concatenate([do_ref[s * tq:(s + 1) * tq, hh * HEAD_DIM:(hh + 1) * HEAD_DIM] for s, hh in blocks], axis=0)
        lse_row = jnp.concatenate([lse_ref[s] for s in range(qb_step)], axis=1)
        dl_row = jnp.concatenate([dl_ref[s] for s in range(qb_step)], axis=1)
        ks, vs = k_ref[...], v_ref[...]
        st = lax.dot_general(ks, q, NT, preferred_element_type=F32)
        p = jnp.exp(st - lse_row)
        dpt = lax.dot_general(vs, do, NT, preferred_element_type=F32)
        ds = (p * (dpt - dl_row)).astype(BF16)
        dv_part = lax.dot_general(p.astype(BF16), do, NN, preferred_element_type=F32)
        dk_part = lax.dot_general(ds, q, NN, preferred_element_type=F32)
        dq_part = lax.dot_general(ks, ds, TN, preferred_element_type=F32)

        dk_ref[...] += dk_part
        dv_ref[...] += dv_part
        for s in range(qb_step):
            dq_ref[qb_step * i + s] += dq_part[:, s * rows:(s + 1) * rows]

    kv = pl.BlockSpec((None, GB_TK, HEAD_DIM), lambda g, j, i: (g, j, 0))
    row = pl.BlockSpec((None, qb_step, 1, rows), lambda g, j, i: (g, i, 0, 0))
    return pl.pallas_call(
        body, name="attn_b_bwd", grid=(KV_B, nk, nq // qb_step),
        in_specs=[pl.BlockSpec((GROUP, qb_step * tq, HEAD_DIM), lambda g, j, i: (g, i, 0)), kv, kv,
                  pl.BlockSpec((qb_step * tq, GROUP * HEAD_DIM), lambda g, j, i: (i, KV_A + g)), row, row],
        out_specs=[pl.BlockSpec((None, nq, HEAD_DIM, rows), lambda g, j, i: (g, 0, 0, 0)), kv, kv],
        out_shape=[jax.ShapeDtypeStruct((KV_B, nq, HEAD_DIM, rows), F32),
                   jax.ShapeDtypeStruct((KV_B, t, HEAD_DIM), F32),
                   jax.ShapeDtypeStruct((KV_B, t, HEAD_DIM), F32)],
        compiler_params=_params(("parallel", "arbitrary", "arbitrary")),
    )(qb, kb, vb, dheads, lse, delta)


def _delta_rows(dheads, heads):
    n = heads.shape[0]
    tq = GB_TQ
    w = GROUP * HEAD_DIM

    def body(a_ref, b_ref, o_ref):
        prod = a_ref[...].astype(F32) * b_ref[...].astype(F32)
        cols = [jnp.sum(prod[:, hh * HEAD_DIM:(hh + 1) * HEAD_DIM].T, axis=0, keepdims=True) for hh in range(GROUP)]
        o_ref[...] = jnp.concatenate(cols, axis=1)

    blk = pl.BlockSpec((tq, w), lambda g, i: (i, KV_A + g))
    return pl.pallas_call(
        body, name="delta_rows", grid=(KV_B, n // tq),
        in_specs=[blk, blk],
        out_specs=pl.BlockSpec((None, None, 1, GROUP * tq), lambda g, i: (g, i, 0, 0)),
        out_shape=jax.ShapeDtypeStruct((KV_B, n // tq, 1, GROUP * tq), F32),
        compiler_params=_params(("parallel", "parallel")),
    )(dheads, heads)


KWIN = 3 * BLOCK


def _window_scores(q, k_ref, j, n, nb, sink_row):
    c = k_ref.shape[0] - n
    start = pl.multiple_of(jnp.clip(j - 1, 0, nb - 3) * BLOCK, BLOCK)
    kw = k_ref[pl.ds(start, KWIN), :]
    kc = k_ref[pl.ds(n, c), :]
    s_loc = lax.dot_general(kw, q, NT, preferred_element_type=F32)
    s_ctx = lax.dot_general(kc, q, NT, preferred_element_type=F32)
    cols = GROUP * BLOCK
    qpos = j * BLOCK + lax.broadcasted_iota(jnp.int32, (KWIN, cols), 1) % BLOCK
    kpos = start + lax.broadcasted_iota(jnp.int32, (KWIN, cols), 0)
    s_loc = jnp.where(jnp.abs(qpos - kpos) <= WINDOW, s_loc, -jnp.inf)
    m = jnp.maximum(jnp.maximum(jnp.max(s_loc, axis=0, keepdims=True), jnp.max(s_ctx, axis=0, keepdims=True)),
                    sink_row)
    e_loc, e_ctx, e_sink = jnp.exp(s_loc - m), jnp.exp(s_ctx - m), jnp.exp(sink_row - m)
    inv = 1.0 / (jnp.sum(e_loc, axis=0, keepdims=True) + jnp.sum(e_ctx, axis=0, keepdims=True) + e_sink)
    return e_loc * inv, e_ctx * inv, e_sink * inv, start


def _sink_row(sink_ref, g):
    return jnp.concatenate([sink_ref[pl.ds(g * GROUP + hh, 1), :] for hh in range(GROUP)], axis=1)


def _attn_a_fwd(qa, ka, va, sink_b, heads_b, n):
    t = ka.shape[1]
    nb = n // BLOCK
    assert nb >= 3
    wb = 4 if nb % 4 == 0 else 1

    def body(q_ref, k_ref, v_ref, sink_ref, heads_b_ref, o_ref):
        g, jj = pl.program_id(0), pl.program_id(1)
        for s in range(wb):
            j = jj * wb + s
            rows = slice(s * BLOCK, (s + 1) * BLOCK)
            q = q_ref[:, rows, :].reshape(GROUP * BLOCK, HEAD_DIM)
            p_loc, p_ctx, _, start = _window_scores(q, k_ref, j, n, nb, _sink_row(sink_ref, g))
            vw = v_ref[pl.ds(start, KWIN), :]
            vc = v_ref[pl.ds(n, t - n), :]
            ot = (lax.dot_general(vw, p_loc.astype(BF16), TN, preferred_element_type=F32)
                  + lax.dot_general(vc, p_ctx.astype(BF16), TN, preferred_element_type=F32))
            for hh in range(GROUP):
                o_ref[rows, hh * HEAD_DIM:(hh + 1) * HEAD_DIM] = ot[:, hh * BLOCK:(hh + 1) * BLOCK].T.astype(BF16)

    return pl.pallas_call(
        body, name="attn_a_fwd", grid=(KV_A, nb // wb),
        in_specs=[pl.BlockSpec((GROUP, wb * BLOCK, HEAD_DIM), lambda g, j: (g, j, 0)),
                  pl.BlockSpec((None, t, HEAD_DIM), lambda g, j: (g, 0, 0)),
                  pl.BlockSpec((None, t, HEAD_DIM), lambda g, j: (g, 0, 0)),
                  pl.BlockSpec((HEADS_A, HEAD_DIM), lambda g, j: (0, 0)),
                  pl.BlockSpec(memory_space=pl.ANY)],
        out_specs=pl.BlockSpec((wb * BLOCK, GROUP * HEAD_DIM), lambda g, j: (j, g)),
        out_shape=jax.ShapeDtypeStruct((n, MIX_WIDTH), BF16),
        input_output_aliases={4: 0},
        compiler_params=_params(("parallel", "parallel")),
    )(qa, ka, va, sink_b, heads_b)


def _attn_a_bwd(qa, ka, va, sink_b, dheads, n):
    t = ka.shape[1]
    c = t - n
    nb = n // BLOCK
    wb = 4 if nb % 4 == 0 else 1

    def body(q_ref, k_ref, v_ref, sink_ref, do_ref, dq_ref, dk_ref, dv_ref, dsink_ref):
        g, jj = pl.program_id(0), pl.program_id(1)

        @pl.when(jj == 0)
        def _():
            dk_ref[...] = jnp.zeros_like(dk_ref)
            dv_ref[...] = jnp.zeros_like(dv_ref)
            dsink_ref[...] = jnp.zeros_like(dsink_ref)

        for s in range(wb):
            j = jj * wb + s
            rows = slice(s * BLOCK, (s + 1) * BLOCK)
            q = q_ref[:, rows, :].reshape(GROUP * BLOCK, HEAD_DIM)
            do = _heads_rows(do_ref[rows, :], BLOCK)
            p_loc, p_ctx, p_sink, start = _window_scores(q, k_ref, j, n, nb, _sink_row(sink_ref, g))
            kw, vw = k_ref[pl.ds(start, KWIN), :], v_ref[pl.ds(start, KWIN), :]
            kc, vc = k_ref[pl.ds(n, c), :], v_ref[pl.ds(n, c), :]
            dp_loc = lax.dot_general(vw, do, NT, preferred_element_type=F32)
            dp_ctx = lax.dot_general(vc, do, NT, preferred_element_type=F32)
            dl = jnp.sum(p_loc * dp_loc, axis=0, keepdims=True) + jnp.sum(p_ctx * dp_ctx, axis=0, keepdims=True)
            ds_loc = (p_loc * (dp_loc - dl)).astype(BF16)
            ds_ctx = (p_ctx * (dp_ctx - dl)).astype(BF16)
            dqt = (lax.dot_general(kw, ds_loc, TN, preferred_element_type=F32)
                   + lax.dot_general(kc, ds_ctx, TN, preferred_element_type=F32))
            for hh in range(GROUP):
                dq_ref[hh, rows, :] = dqt[:, hh * BLOCK:(hh + 1) * BLOCK].T
            dk_ref[pl.ds(start, KWIN), :] += lax.dot_general(ds_loc, q, NN, preferred_element_type=F32)
            dv_ref[pl.ds(start, KWIN), :] += lax.dot_general(p_loc.astype(BF16), do, NN, preferred_element_type=F32)
            dk_ref[pl.ds(n, c), :] += lax.dot_general(ds_ctx, q, NN, preferred_element_type=F32)
            dv_ref[pl.ds(n, c), :] += lax.dot_general(p_ctx.astype(BF16), do, NN, preferred_element_type=F32)
            dsk = -(p_sink * dl)
            upd = [jnp.broadcast_to(jnp.sum(dsk[:, hh * BLOCK:(hh + 1) * BLOCK], axis=1, keepdims=True), (1, HEAD_DIM))
                   for hh in range(GROUP)]
            dsink_ref[...] += jnp.concatenate(upd + [jnp.zeros((8 - GROUP, HEAD_DIM), F32)], axis=0)

    res = pl.BlockSpec((None, t, HEAD_DIM), lambda g, j: (g, 0, 0))
    return pl.pallas_call(
        body, name="attn_a_bwd", grid=(KV_A, nb // wb),
        in_specs=[pl.BlockSpec((GROUP, wb * BLOCK, HEAD_DIM), lambda g, j: (g, j, 0)), res, res,
                  pl.BlockSpec((HEADS_A, HEAD_DIM), lambda g, j: (0, 0)),
                  pl.BlockSpec((wb * BLOCK, GROUP * HEAD_DIM), lambda g, j: (j, g))],
        out_specs=[pl.BlockSpec((GROUP, wb * BLOCK, HEAD_DIM), lambda g, j: (g, j, 0)), res, res,
                   pl.BlockSpec((None, 8, HEAD_DIM), lambda g, j: (g, 0, 0))],
        out_shape=[jax.ShapeDtypeStruct((HEADS_A, n, HEAD_DIM), F32),
                   jax.ShapeDtypeStruct((KV_A, t, HEAD_DIM), F32),
                   jax.ShapeDtypeStruct((KV_A, t, HEAD_DIM), F32),
                   jax.ShapeDtypeStruct((KV_A, 8, HEAD_DIM), F32)],
        compiler_params=_params(("parallel", "arbitrary")),
    )(qa, ka, va, sink_b, dheads)


def _ln_stats(r):
    mu = jnp.mean(r, axis=-1, keepdims=True)
    cen = r - mu
    rstd = lax.rsqrt(jnp.mean(cen * cen, axis=-1, keepdims=True) + EPS)
    return cen * rstd, rstd


def _ln_bwd(dy, xhat, rstd, gain):
    dxh = dy * gain
    return rstd * (dxh - jnp.mean(dxh, axis=-1, keepdims=True)
                   - xhat * jnp.mean(dxh * xhat, axis=-1, keepdims=True))


def _accumulate_rows(ref, rows, i):
    pad = [jnp.zeros_like(rows[0])] * (8 - len(rows))
    upd = jnp.concatenate(rows + pad, axis=0)

    @pl.when(i == 0)
    def _():
        ref[...] = upd

    @pl.when(i != 0)
    def _():
        ref[...] += upd


def _colsum(v):
    return jnp.sum(v, axis=0, keepdims=True)


LN_TILE = 256


LN_CHUNK = 32


def _row_in(d):
    return pl.BlockSpec((LN_TILE, d), lambda i: (i, 0))


def _res_ln1(x, a, vec):
    n, d = x.shape

    def body(x_ref, a_ref, v_ref, xh_ref, rs_ref, u_ref):
        r1 = DN_ALPHA * x_ref[...] + v_ref[0:1, :] * a_ref[...]
        xhat, rstd = _ln_stats(r1)
        xh_ref[...] = xhat
        rs_ref[...] = rstd
        x1 = xhat * v_ref[1:2, :] + v_ref[2:3, :]
        u_ref[...] = (x1 * (1.0 + v_ref[3:4, :]) + v_ref[4:5, :]).astype(BF16)

    row = pl.BlockSpec((LN_TILE, d), lambda i: (i, 0))
    return pl.pallas_call(
        body, name="res_ln1", grid=(n // LN_TILE,),
        in_specs=[_row_in(d), _row_in(d), pl.BlockSpec((8, d), lambda i: (0, 0))],
        out_specs=[row, pl.BlockSpec((LN_TILE, 1), lambda i: (i, 0)), row],
        out_shape=[jax.ShapeDtypeStruct((n, d), F32), jax.ShapeDtypeStruct((n, 1), F32),
                   jax.ShapeDtypeStruct((n, d), BF16)],
        compiler_params=_params(("parallel",)),
    )(x, a, vec)


def _res_ln2_loss(xhat1, f, target, vec):
    n, d = f.shape

    def body(xh_ref, f_ref, t_ref, v_ref, dr_ref, df_ref, s_ref):
        i = pl.program_id(0)

        def chunk(r, sums):
            rows = pl.ds(pl.multiple_of(r * LN_CHUNK, LN_CHUNK), LN_CHUNK)
            x1 = xh_ref[rows, :] * v_ref[1:2, :] + v_ref[2:3, :]
            fv = f_ref[rows, :]
            xhat, rstd = _ln_stats(DN_ALPHA * x1 + v_ref[0:1, :] * fv)
            err = xhat * v_ref[3:4, :] + v_ref[4:5, :] - t_ref[rows, :]
            dy = err * (1.0 / d)
            dr2 = _ln_bwd(dy, xhat, rstd, v_ref[3:4, :])
            dr_ref[rows, :] = dr2
            df_ref[rows, :] = (dr2 * v_ref[0:1, :]).astype(BF16)
            return tuple(s + t for s, t in zip(sums, (_colsum(dy * xhat), _colsum(dy), _colsum(dr2 * fv),
                                                       _colsum(err * err) * (0.5 / d))))

        zero = jnp.zeros((1, d), F32)
        sums = lax.fori_loop(0, LN_TILE // LN_CHUNK, chunk, (zero,) * 4)
        _accumulate_rows(s_ref, list(sums), i)

    row = pl.BlockSpec((LN_TILE, d), lambda i: (i, 0))
    return pl.pallas_call(
        body, name="res_ln2_loss", grid=(n // LN_TILE,),
        in_specs=[_row_in(d), _row_in(d), _row_in(d), pl.BlockSpec((8, d), lambda i: (0, 0))],
        out_specs=[row, row, pl.BlockSpec((8, d), lambda i: (0, 0))],
        out_shape=[jax.ShapeDtypeStruct((n, d), F32), jax.ShapeDtypeStruct((n, d), BF16),
                   jax.ShapeDtypeStruct((8, d), F32)],
        compiler_params=_params(("arbitrary",)),
    )(xhat1, f, target, vec)


def _ln1_bwd(du2, dr2, xhat1, rstd1, a, vec):
    n, d = du2.shape

    def body(du_ref, dr2_ref, xh_ref, rs_ref, a_ref, v_ref, dxp_ref, da_ref, s_ref):
        i = pl.program_id(0)

        def chunk(r, sums):
            rows = pl.ds(pl.multiple_of(r * LN_CHUNK, LN_CHUNK), LN_CHUNK)
            du, xhat = du_ref[rows, :], xh_ref[rows, :]
            x1 = xhat * v_ref[1:2, :] + v_ref[2:3, :]
            dx1 = DN_ALPHA * dr2_ref[rows, :] + du * (1.0 + v_ref[0:1, :])
            dr1 = _ln_bwd(dx1, xhat, rs_ref[rows, :], v_ref[1:2, :])
            dxp_ref[rows, :] = DN_ALPHA * dr1
            da_ref[rows, :] = (dr1 * v_ref[3:4, :]).astype(BF16)
            return tuple(s + t for s, t in zip(sums, (_colsum(du * x1), _colsum(du), _colsum(dx1 * xhat), _colsum(dx1),
                                                       _colsum(dr1 * a_ref[rows, :]))))

        zero = jnp.zeros((1, d), F32)
        sums = lax.fori_loop(0, LN_TILE // LN_CHUNK, chunk, (zero,) * 5)
        _accumulate_rows(s_ref, list(sums), i)

    row = pl.BlockSpec((LN_TILE, d), lambda i: (i, 0))
    return pl.pallas_call(
        body, name="ln1_bwd", grid=(n // LN_TILE,),
        in_specs=[_row_in(d), _row_in(d), _row_in(d), pl.BlockSpec((LN_TILE, 1), lambda i: (i, 0)), _row_in(d),
                  pl.BlockSpec((8, d), lambda i: (0, 0))],
        out_specs=[row, row, pl.BlockSpec((8, d), lambda i: (0, 0))],
        out_shape=[jax.ShapeDtypeStruct((n, d), F32), jax.ShapeDtypeStruct((n, d), BF16),
                   jax.ShapeDtypeStruct((8, d), F32)],
        compiler_params=_params(("arbitrary",)),
    )(du2, dr2, xhat1, rstd1, a, vec)


def _mod1_bwd(du_all, dxp, x, ctx, mods):
    n, d = x.shape
    nx = n // ROW_TILE

    def body(du_ref, dxp_ref, x_ref, ctx_ref, m_ref, gx_ref, s_ref):
        i = pl.program_id(0)
        du = du_ref[...]
        zero = jnp.zeros((1, d), F32)

        @pl.when(i == 0)
        def _():
            s_ref[...] = jnp.zeros_like(s_ref)

        @pl.when(i < nx)
        def _():
            gx_ref[...] = dxp_ref[...] + du * (1.0 + m_ref[0:1, :])
            s_ref[...] += jnp.concatenate([_colsum(du * x_ref[...]), _colsum(du)] + [zero] * 6, axis=0)

        @pl.when(i >= nx)
        def _():
            s_ref[...] += jnp.concatenate([zero, zero, _colsum(du * ctx_ref[...]), _colsum(du)] + [zero] * 4, axis=0)

    lat = pl.BlockSpec((ROW_TILE, d), lambda i: (jnp.minimum(i, nx - 1), 0))
    return pl.pallas_call(
        body, name="mod1_bwd", grid=(nx + 1,),
        in_specs=[pl.BlockSpec((ROW_TILE, d), lambda i: (i, 0)), lat, lat,
                  pl.BlockSpec((ROW_TILE, d), lambda i: (0, 0)), pl.BlockSpec((8, d), lambda i: (0, 0))],
        out_specs=[lat, pl.BlockSpec((8, d), lambda i: (0, 0))],
        out_shape=[jax.ShapeDtypeStruct((n, d), F32), jax.ShapeDtypeStruct((8, d), F32)],
        compiler_params=_params(("arbitrary",)),
    )(du_all, dxp, x, ctx, mods)


FFN_TM = 1024
FFN_TN = 512


def _gate_up(u2, wg, wu, after):
    n, d = u2.shape
    f = wg.shape[1]

    def body(u_ref, wg_ref, wu_ref, after_ref, g_ref, up_ref, h_ref):
        u = u_ref[...]
        g = lax.dot_general(u, wg_ref[...], NN, preferred_element_type=F32)
        up = lax.dot_general(u, wu_ref[...], NN, preferred_element_type=F32)
        g_ref[...] = g.astype(BF16)
        up_ref[...] = up.astype(BF16)
        h_ref[...] = (g * jax.nn.sigmoid(g) * up).astype(BF16)

    tm = min(FFN_TM, n)
    wspec = pl.BlockSpec((d, FFN_TN), lambda j, i: (0, j))
    ospec = pl.BlockSpec((tm, FFN_TN), lambda j, i: (i, j))
    return pl.pallas_call(
        body, name="gate_up", grid=(f // FFN_TN, n // tm),
        in_specs=[pl.BlockSpec((tm, d), lambda j, i: (i, 0)), wspec, wspec, pl.BlockSpec(memory_space=pl.ANY)],
        out_specs=[ospec, ospec, ospec],
        out_shape=[jax.ShapeDtypeStruct((n, f), BF16)] * 3,
        compiler_params=_params(("parallel", "parallel")),
    )(u2, wg, wu, after)


def _glu_bwd(df, wd, g, u):
    n, d = df.shape
    f = wd.shape[0]

    def body(df_ref, wd_ref, g_ref, u_ref, dg_ref, du_ref):
        dh = lax.dot_general(df_ref[...], wd_ref[...], NT, preferred_element_type=F32)
        gv = g_ref[...].astype(F32)
        sig = jax.nn.sigmoid(gv)
        du_ref[...] = (dh * (gv * sig)).astype(BF16)
        dg_ref[...] = (dh * u_ref[...].astype(F32) * (sig * (1.0 + gv * (1.0 - sig)))).astype(BF16)

    tm = min(FFN_TM, n)
    ospec = pl.BlockSpec((tm, FFN_TN), lambda i, j: (i, j))
    return pl.pallas_call(
        body, name="glu_bwd", grid=(n // tm, f // FFN_TN),
        in_specs=[pl.BlockSpec((tm, d), lambda i, j: (i, 0)),
                  pl.BlockSpec((FFN_TN, d), lambda i, j: (j, 0)), ospec, ospec],
        out_specs=[ospec, ospec],
        out_shape=[jax.ShapeDtypeStruct((n, f), BF16), jax.ShapeDtypeStruct((n, f), BF16)],
        compiler_params=_params(("parallel", "parallel")),
    )(df, wd, g, u)


def _du2(dg, du, wg, wu):
    n, f = dg.shape
    d = wg.shape[0]
    tm, tn, tk = min(1024, n), 1024, 1408
    nk = f // tk

    def body(dg_ref, du_ref, wg_ref, wu_ref, o_ref):
        @pl.when(pl.program_id(2) == 0)
        def _():
            o_ref[...] = jnp.zeros_like(o_ref)

        o_ref[...] += (lax.dot_general(dg_ref[...], wg_ref[...], NT, preferred_element_type=F32)
                       + lax.dot_general(du_ref[...], wu_ref[...], NT, preferred_element_type=F32))

    aspec = pl.BlockSpec((tm, tk), lambda i, j, kk: (i, kk))
    wspec = pl.BlockSpec((tn, tk), lambda i, j, kk: (j, kk))
    return pl.pallas_call(
        body, name="du2", grid=(n // tm, d // tn, nk),
        in_specs=[aspec, aspec, wspec, wspec],
        out_specs=pl.BlockSpec((tm, tn), lambda i, j, kk: (i, j)),
        out_shape=jax.ShapeDtypeStruct((n, d), F32),
        compiler_params=_params(("parallel", "parallel", "arbitrary")),
    )(dg, du, wg, wu)


def _rows8(rows, d=D_MODEL):
    rows = [r.reshape(1, d).astype(F32) for r in rows]
    return jnp.concatenate(rows + [jnp.zeros((8 - len(rows), d), F32)], axis=0)


W_GROUPS = (("w_in",), ("w_out", "w_gate", "w_up"), ("w_down",))
G_GROUPS = (("w_down", "w_gate", "w_up"), ("w_out",), ("w_in",))


def _layer_fwd_bwd(x, ctx, target, mod, mod_ctx, weights, prefetch, grads_out,
                   q_g, k_g, sink, ln1_g, ln1_b, ln2_g, ln2_b):
    n, d = x.shape
    c = ctx.shape[0]
    sh1, sc1, g1, sh2, sc2, g2 = [mod[:, k * d:(k + 1) * d] for k in range(6)]
    csh1, csc1 = mod_ctx[:, 0:d], mod_ctx[:, d:2 * d]
    cos, sin_a, sin_b = _rope_tables(n, c)
    sink_b = jnp.broadcast_to(sink.reshape(HEADS_A, 1), (HEADS_A, HEAD_DIM)).astype(F32)

    u_all = _modulate_rows(x, ctx, _rows8([sc1, sh1, csc1, csh1]))
    (w_in,) = weights(0, u_all)
    h_all = _matmul(u_all, w_in, name="qkv_proj", tm=_fit(n + c, 1088), tn=1024, tk=2048, out_dtype=F32)
    qa, ka, va, qb, kb, vb = _qkv_post(h_all, cos, sin_a, sin_b, q_g, k_g)
    heads_b, lse = _attn_b_fwd(qb, kb, vb, n)
    zero = prefetch(1, heads_b)
    heads = _attn_a_fwd(qa, ka, va, sink_b + zero, heads_b, n)
    w_out, w_gate, w_up = weights(1, heads)
    a = _matmul(heads, w_out, name="out_proj", tm=1024, tn=1024, tk=2048, out_dtype=F32)
    xhat1, rstd1, u2 = _res_ln1(x, a, _rows8([g1, ln1_g, ln1_b, sc2, sh2]))
    zero = prefetch(2, u2)
    gg, uu, hh = _gate_up(u2, w_gate, w_up, zero.reshape(1, 1))
    (w_down,) = weights(2, hh)
    f = _matmul(hh, w_down, name="ffn_down", tm=1024, tn=512, tk=FFN, out_dtype=F32)
    dr2, df, s_ln2 = _res_ln2_loss(xhat1, f, target, _rows8([g2, ln1_g, ln1_b, ln2_g, ln2_b]))

    dgg, duu = _glu_bwd(df, w_down, gg, uu)
    dw_down = _matmul(hh, df, name="dw_down", ta=True, tm=512, tn=1024, tk=n, out_dtype=BF16)
    dw_gate = _matmul(u2, dgg, name="dw_gate", ta=True, tm=1024, tn=512, tk=n, out_dtype=BF16)
    dw_up = _matmul(u2, duu, name="dw_up", ta=True, tm=1024, tn=512, tk=n, out_dtype=BF16)
    zero = grads_out(0, [dw_down, dw_gate, dw_up])
    du2 = _du2(dgg, duu, w_gate, w_up)
    dxp, da, s_ln1 = _ln1_bwd(du2, dr2, xhat1, rstd1, a, _rows8([sc2, ln1_g, ln1_b, g1]) + zero)

    dheads = _matmul(da, w_out, name="d_heads", tb=True, tm=1024, tn=1024, tk=2048, out_dtype=BF16)
    dw_out = _matmul(heads, da, name="dw_out", ta=True, tm=1024, tn=1024, tk=n, out_dtype=BF16)
    zero = grads_out(1, [dw_out])
    delta = _delta_rows(dheads, heads)
    dqa, dka, dva, dsink = _attn_a_bwd(qa, ka, va, sink_b + zero, dheads, n)
    dqb, dkb, dvb = _attn_b_bwd(qb, kb, vb, dheads, lse, delta, n)
    dh_all, s_gain = _qkv_bwd_post(h_all, cos, sin_a, sin_b, q_g, k_g, dqa, dka, dva, dqb, dkb, dvb, n)
    dw_in = _matmul(u_all, dh_all, name="dw_in", ta=True, tm=1024, tn=1024, tk=n + c, out_dtype=BF16)
    zero = grads_out(2, [dw_in])
    du_all = _matmul(dh_all, w_in, name="d_u1", tb=True, tm=_fit(n + c, 1088), tn=1024, tk=IN_WIDTH, out_dtype=F32,
                     after=zero.reshape(1, 1))
    grad_x, s_mod1 = _mod1_bwd(du_all, dxp, x, ctx, _rows8([sc1]) + zero)

    dsink_row = jnp.concatenate([dsink[0, 0:GROUP, 0], dsink[1, 0:GROUP, 0]]).reshape(1, HEADS_A)
    misc = jnp.concatenate([s_gain[0:1], s_gain[1:2], dsink_row,
                            jnp.zeros((1, d - 2 * HEAD_DIM - HEADS_A), F32)], axis=1)
    partial = jnp.concatenate([
        s_mod1[1:2], s_mod1[0:1], s_ln1[4:5],
        s_ln1[1:2], s_ln1[0:1], s_ln2[2:3],
        s_mod1[3:4], s_mod1[2:3],
        s_ln1[2:3], s_ln1[3:4], s_ln2[0:1], s_ln2[1:2],
        s_ln2[3:4], misc, jnp.zeros((2, d), F32)], axis=0)
    return grad_x, partial


ANY = pl.BlockSpec(memory_space=pl.ANY)
VMEM_FULL = pl.BlockSpec(memory_space=pltpu.VMEM)
N_CHIP_PEERS = 3


def _me():
    return lax.axis_index("x"), lax.axis_index("y"), lax.axis_index("c")


def _other_chips(x, y):
    return [(1 - x, y), (x, 1 - y), (1 - x, 1 - y)]


def _shard_of(chip):
    return 2 * chip[0] + chip[1]


def _dev_index(x, y, c):
    return 4 * x + 2 * y + c


def _rcopy(src, dst, send_sems, recv_sems, k, dev):
    return pltpu.make_async_remote_copy(src_ref=src, dst_ref=dst, send_sem=send_sems.at[k], recv_sem=recv_sems.at[k],
                                        device_id=dev, device_id_type=MESH)


BIG = (("w_in", (D_MODEL, IN_WIDTH), 1), ("w_out", (MIX_WIDTH, D_MODEL), 0), ("w_gate", (D_MODEL, FFN), 1),
       ("w_up", (D_MODEL, FFN), 1), ("w_down", (FFN, D_MODEL), 0))


def _sub(ref, axis, idx, size):
    start = pl.multiple_of(idx * size, size)
    return ref.at[pl.ds(start, size), :] if axis == 0 else ref.at[:, pl.ds(start, size)]


def _shape_div(shape, axis, parts):
    return tuple(s // parts if a == axis else s for a, s in enumerate(shape))


def _piece(a, ref, shard, half):
    _, full, axis = BIG[a]
    view = _sub(ref, axis, shard, full[axis] // N_SHARD)
    return _sub(view, 1 - axis, half, full[1 - axis] // 2)


HBM = pl.BlockSpec(memory_space=pltpu.HBM)
SEM = pl.BlockSpec(memory_space=pltpu.SEMAPHORE)
EFFECT = pltpu.SideEffectType.DATAFLOW_SIDE_EFFECTING
BIG_INDEX = {name: a for a, (name, _, _) in enumerate(BIG)}


def _in_hbm(arr):
    return pltpu.with_memory_space_constraint(arr, pltpu.HBM)


def _gather_start(tag, arrs, bufs, prev):
    n_arr = len(arrs)

    def body(*refs):
        ins = refs[:n_arr]
        send_sems, recv_sems = refs[n_arr + 1], refs[n_arr + 2]
        token = refs[-1]
        x, y, c = _me()
        s_me = _shard_of((x, y))
        for i, a in enumerate(arrs):
            mine = _piece(a, ins[i], s_me, c)
            for j, chip in enumerate(_other_chips(x, y)):
                _rcopy(mine, mine, send_sems, recv_sems, N_CHIP_PEERS * i + j, (*chip, c)).start()
        token[...] = jnp.zeros_like(token)

    n_sem = N_CHIP_PEERS * n_arr
    outs = pl.pallas_call(
        body, name="gather_start_" + tag,
        in_specs=[HBM] * n_arr + [ANY],
        out_specs=[SEM, SEM] + [HBM] * n_arr + [VMEM_FULL],
        out_shape=[pltpu.SemaphoreType.DMA((n_sem,)), pltpu.SemaphoreType.DMA((n_sem,))]
        + [pltpu.HBM(BIG[a][1], BF16) for a in arrs] + [jax.ShapeDtypeStruct((8, HEAD_DIM), F32)],
        input_output_aliases={i: 2 + i for i in range(n_arr)},
        compiler_params=pltpu.CompilerParams(has_side_effects=EFFECT),
    )(*[_in_hbm(b) for b in bufs], prev)
    return outs[0], outs[1], list(outs[2:2 + n_arr]), outs[-1]


def _gather_wait(tag, arrs, send_sems, recv_sems, bufs, after):
    n_arr = len(arrs)

    def body(*refs):
        ins = refs[:n_arr]
        send_sems_, recv_sems_ = refs[n_arr], refs[n_arr + 1]
        x, y, c = _me()
        s_me = _shard_of((x, y))
        for i, a in enumerate(arrs):
            mine = _piece(a, ins[i], s_me, c)
            for j, chip in enumerate(_other_chips(x, y)):
                landed = _piece(a, ins[i], _shard_of(chip), c)
                cp = _rcopy(mine, landed, send_sems_, recv_sems_, N_CHIP_PEERS * i + j, (*chip, c))
                cp.wait_send()
                cp.wait_recv()

    outs = pl.pallas_call(
        body, name="gather_wait_" + tag,
        in_specs=[HBM] * n_arr + [SEM, SEM, ANY],
        out_specs=[HBM] * n_arr,
        out_shape=[pltpu.HBM(BIG[a][1], BF16) for a in arrs],
        input_output_aliases={i: i for i in range(n_arr)},
        compiler_params=pltpu.CompilerParams(has_side_effects=EFFECT),
    )(*bufs, send_sems, recv_sems, after)
    return list(outs)


def _gather_forward(tag, arrs, bufs):
    n_arr = len(arrs)

    def body(*refs):
        outs = refs[n_arr:2 * n_arr]
        send_sems, recv_sems = refs[2 * n_arr:]
        x, y, c = _me()
        sibling = (x, y, 1 - c)
        chips = _other_chips(x, y)
        copies = []
        for i, a in enumerate(arrs):
            for j, chip in enumerate(chips):
                landed = _piece(a, outs[i], _shard_of(chip), c)
                cp = _rcopy(landed, landed, send_sems, recv_sems, N_CHIP_PEERS * i + j, sibling)
                cp.start()
                copies.append(cp)
        for i, a in enumerate(arrs):
            for j, chip in enumerate(chips):
                other = _piece(a, outs[i], _shard_of(chip), 1 - c)
                _rcopy(other, other, send_sems, recv_sems, N_CHIP_PEERS * i + j, sibling).wait_recv()
        for cp in copies:
            cp.wait_send()

    n_sem = N_CHIP_PEERS * n_arr
    return list(pl.pallas_call(
        body, name="gather_forward_" + tag,
        in_specs=[ANY] * n_arr, out_specs=[ANY] * n_arr,
        out_shape=[jax.ShapeDtypeStruct(BIG[a][1], BF16) for a in arrs],
        input_output_aliases={i: i for i in range(n_arr)},
        scratch_shapes=[pltpu.SemaphoreType.DMA((n_sem,)), pltpu.SemaphoreType.DMA((n_sem,))],
    )(*bufs))


def _forward_start(tag, arrs, bufs):
    n_arr = len(arrs)

    def body(*refs):
        ins = refs[:n_arr]
        send_sems, recv_sems = refs[n_arr], refs[n_arr + 1]
        token = refs[-1]
        x, y, c = _me()
        for i, a in enumerate(arrs):
            for j, chip in enumerate(_other_chips(x, y)):
                landed = _piece(a, ins[i], _shard_of(chip), c)
                _rcopy(landed, landed, send_sems, recv_sems, N_CHIP_PEERS * i + j, (x, y, 1 - c)).start()
        token[...] = jnp.zeros_like(token)

    n_sem = N_CHIP_PEERS * n_arr
    outs = pl.pallas_call(
        body, name="gather_forward_start_" + tag,
        in_specs=[HBM] * n_arr,
        out_specs=[SEM, SEM] + [HBM] * n_arr + [VMEM_FULL],
        out_shape=[pltpu.SemaphoreType.DMA((n_sem,)), pltpu.SemaphoreType.DMA((n_sem,))]
        + [pltpu.HBM(BIG[a][1], BF16) for a in arrs] + [jax.ShapeDtypeStruct((8, HEAD_DIM), F32)],
        input_output_aliases={i: 2 + i for i in range(n_arr)},
        compiler_params=pltpu.CompilerParams(has_side_effects=EFFECT),
    )(*bufs)
    return outs[0], outs[1], list(outs[2:2 + n_arr]), outs[-1]


def _forward_wait(tag, arrs, send_sems, recv_sems, bufs, after):
    n_arr = len(arrs)

    def body(*refs):
        ins = refs[:n_arr]
        send_sems_, recv_sems_ = refs[n_arr], refs[n_arr + 1]
        x, y, c = _me()
        for i, a in enumerate(arrs):
            for j, chip in enumerate(_other_chips(x, y)):
                mine = _piece(a, ins[i], _shard_of(chip), c)
                other = _piece(a, ins[i], _shard_of(chip), 1 - c)
                cp = _rcopy(mine, other, send_sems_, recv_sems_, N_CHIP_PEERS * i + j, (x, y, 1 - c))
                cp.wait_send()
                cp.wait_recv()

    outs = pl.pallas_call(
        body, name="gather_forward_wait_" + tag,
        in_specs=[HBM] * n_arr + [SEM, SEM, ANY],
        out_specs=[HBM] * n_arr,
        out_shape=[pltpu.HBM(BIG[a][1], BF16) for a in arrs],
        input_output_aliases={i: i for i in range(n_arr)},
        compiler_params=pltpu.CompilerParams(has_side_effects=EFFECT),
    )(*bufs, send_sems, recv_sems, after)
    return list(outs)


def _peers(x, y, c):
    return [(x ^ (mask >> 2), y ^ ((mask >> 1) & 1), c ^ (mask & 1)) for mask in range(1, N_DEV)]


def _received_shape(a):
    _, full, axis = BIG[a]
    return (N_DEV - 1,) + _shape_div(_shape_div(full, 1 - axis, 2), axis, N_SHARD)


def _pieces_start(tag, arrs, dws):
    n_arr = len(arrs)

    def body(*refs):
        srcs, lands = refs[:n_arr], refs[n_arr:2 * n_arr]
        send_sems, recv_sems = refs[2 * n_arr], refs[2 * n_arr + 1]
        token = refs[-1]
        x, y, c = _me()
        for i, a in enumerate(arrs):
            for k, peer in enumerate(_peers(x, y, c)):
                src = _piece(a, srcs[i], _shard_of(peer[:2]), peer[2])
                _rcopy(src, lands[i].at[k], send_sems, recv_sems, (N_DEV - 1) * i + k, peer).start()
        token[...] = jnp.zeros_like(token)

    n_sem = (N_DEV - 1) * n_arr
    lands = [_in_hbm(lax.empty(_received_shape(a), BF16)) for a in arrs]
    outs = pl.pallas_call(
        body, name="grad_pieces_start_" + tag,
        in_specs=[HBM] * (2 * n_arr),
        out_specs=[SEM, SEM] + [HBM] * (2 * n_arr) + [VMEM_FULL],
        out_shape=[pltpu.SemaphoreType.DMA((n_sem,)), pltpu.SemaphoreType.DMA((n_sem,))]
        + [pltpu.HBM(BIG[a][1], BF16) for a in arrs] + [pltpu.HBM(_received_shape(a), BF16) for a in arrs]
        + [jax.ShapeDtypeStruct((8, HEAD_DIM), F32)],
        input_output_aliases={i: 2 + i for i in range(2 * n_arr)},
        compiler_params=pltpu.CompilerParams(has_side_effects=EFFECT),
    )(*[_in_hbm(dw) for dw in dws], *lands)
    return outs[0], outs[1], list(outs[2:2 + n_arr]), list(outs[2 + n_arr:2 + 2 * n_arr]), outs[-1]


def _pieces_wait(tag, arrs, send_sems, recv_sems, dws, lands, after):
    n_arr = len(arrs)

    def body(*refs):
        srcs, lands_ = refs[:n_arr], refs[n_arr:2 * n_arr]
        send_sems_, recv_sems_ = refs[2 * n_arr], refs[2 * n_arr + 1]
        x, y, c = _me()
        for i, a in enumerate(arrs):
            for k, peer in enumerate(_peers(x, y, c)):
                src = _piece(a, srcs[i], _shard_of(peer[:2]), peer[2])
                cp = _rcopy(src, lands_[i].at[k], send_sems_, recv_sems_, (N_DEV - 1) * i + k, peer)
                cp.wait_send()
                cp.wait_recv()

    outs = pl.pallas_call(
        body, name="grad_pieces_wait_" + tag,
        in_specs=[HBM] * (2 * n_arr) + [SEM, SEM, ANY],
        out_specs=[HBM] * (2 * n_arr),
        out_shape=[pltpu.HBM(BIG[a][1], BF16) for a in arrs] + [pltpu.HBM(_received_shape(a), BF16) for a in arrs],
        input_output_aliases={i: i for i in range(2 * n_arr)},
        compiler_params=pltpu.CompilerParams(has_side_effects=EFFECT),
    )(*dws, *lands, send_sems, recv_sems, after)
    return list(outs[:n_arr]), list(outs[n_arr:])


def _join_start(tag, g_halves):
    n_arr = len(g_halves)

    def body(*refs):
        srcs, lands = refs[:n_arr], refs[n_arr:2 * n_arr]
        send_sems, recv_sems = refs[2 * n_arr], refs[2 * n_arr + 1]
        token = refs[-1]
        x, y, c = _me()
        for i in range(n_arr):
            _rcopy(srcs[i], lands[i], send_sems, recv_sems, i, (x, y, 1 - c)).start()
        token[...] = jnp.zeros_like(token)

    shapes = [pltpu.HBM(g.shape, F32) for g in g_halves]
    outs = pl.pallas_call(
        body, name="grad_join_start_" + tag,
        in_specs=[HBM] * (2 * n_arr),
        out_specs=[SEM, SEM] + [HBM] * (2 * n_arr) + [VMEM_FULL],
        out_shape=[pltpu.SemaphoreType.DMA((n_arr,)), pltpu.SemaphoreType.DMA((n_arr,))] + shapes + shapes
        + [jax.ShapeDtypeStruct((8, HEAD_DIM), F32)],
        input_output_aliases={i: 2 + i for i in range(2 * n_arr)},
        compiler_params=pltpu.CompilerParams(has_side_effects=EFFECT),
    )(*[_in_hbm(g) for g in g_halves], *[_in_hbm(lax.empty(g.shape, F32)) for g in g_halves])
    return outs[0], outs[1], list(outs[2:2 + n_arr]), list(outs[2 + n_arr:2 + 2 * n_arr]), outs[-1]


def _join_wait(tag, send_sems, recv_sems, g_halves, lands, after):
    n_arr = len(g_halves)

    def body(*refs):
        srcs, lands_ = refs[:n_arr], refs[n_arr:2 * n_arr]
        send_sems_, recv_sems_ = refs[2 * n_arr], refs[2 * n_arr + 1]
        x, y, c = _me()
        for i in range(n_arr):
            cp = _rcopy(srcs[i], lands_[i], send_sems_, recv_sems_, i, (x, y, 1 - c))
            cp.wait_send()
            cp.wait_recv()

    shapes = [pltpu.HBM(g.shape, F32) for g in g_halves]
    outs = pl.pallas_call(
        body, name="grad_join_wait_" + tag,
        in_specs=[HBM] * (2 * n_arr) + [SEM, SEM, ANY],
        out_specs=[HBM] * (2 * n_arr),
        out_shape=shapes + shapes,
        input_output_aliases={i: i for i in range(2 * n_arr)},
        compiler_params=pltpu.CompilerParams(has_side_effects=EFFECT),
    )(*g_halves, *lands, send_sems, recv_sems, after)
    return list(outs[:n_arr]), list(outs[n_arr:])


def _piece_sum(a, dw, shard, core, received):
    name, full, axis = BIG[a]
    rows, cols = _received_shape(a)[1:]
    tr = _fit(rows, ROW_TILE)
    nbr = rows // tr

    def body(w_ref, dw_ref, rec_ref, o_ref):
        acc = dw_ref[...].astype(F32)
        for k in range(N_DEV - 1):
            acc = acc + rec_ref[k].astype(F32)
        o_ref[...] = acc

    if axis == 0:
        own = pl.BlockSpec((tr, cols), lambda i, w: (w[0] * nbr + i, w[1]))
    else:
        own = pl.BlockSpec((tr, cols), lambda i, w: (w[1] * nbr + i, w[0]))
    return pl.pallas_call(
        body, name="grad_sum_pieces_" + name,
        grid_spec=pltpu.PrefetchScalarGridSpec(
            num_scalar_prefetch=1, grid=(nbr,),
            in_specs=[own, pl.BlockSpec((N_DEV - 1, tr, cols), lambda i, w: (0, i, 0))],
            out_specs=pl.BlockSpec((tr, cols), lambda i, w: (i, 0))),
        out_shape=jax.ShapeDtypeStruct((rows, cols), F32),
        compiler_params=_params(("parallel",)),
    )(jnp.stack([shard, core]).astype(jnp.int32), dw, received)


def _scatter_begin(tag, arrs, dws):
    send_sems, recv_sems, dws, lands, token = _pieces_start(tag, arrs, dws)
    return (send_sems, recv_sems, dws, lands), token[0, 0]


def _scatter_reduce(tag, arrs, state, after):
    x, y, c = _me()
    send_sems, recv_sems, dws, lands = state
    dws, lands = _pieces_wait(tag, arrs, send_sems, recv_sems, dws, lands, after)
    g_own = [_piece_sum(a, dw, _shard_of((x, y)), c, r) for a, dw, r in zip(arrs, dws, lands)]
    send_sems, recv_sems, g_own, lands, token = _join_start(tag, g_own)
    return (send_sems, recv_sems, g_own, lands), token


def _scatter_end(tag, state, after):
    return _join_wait(tag, *state, after)


def _rows_start(tag, block):
    r, d = block.shape

    def body(src, land, send_sems, recv_sems, src_thru, land_thru, token):
        x, y, c = _me()
        for k, peer in enumerate(_peers(x, y, c)):
            _rcopy(src, land.at[_dev_index(x, y, c)], send_sems, recv_sems, k, peer).start()
        token[...] = jnp.zeros_like(token)

    outs = pl.pallas_call(
        body, name="rows_start_" + tag,
        in_specs=[HBM, HBM],
        out_specs=[SEM, SEM, HBM, HBM, VMEM_FULL],
        out_shape=[pltpu.SemaphoreType.DMA((N_DEV - 1,)), pltpu.SemaphoreType.DMA((N_DEV - 1,)),
                   pltpu.HBM((r, d), F32), pltpu.HBM((N_DEV, r, d), F32), jax.ShapeDtypeStruct((8, HEAD_DIM), F32)],
        input_output_aliases={0: 2, 1: 3},
        compiler_params=pltpu.CompilerParams(has_side_effects=EFFECT),
    )(_in_hbm(block), _in_hbm(lax.empty((N_DEV, r, d), F32)))
    return outs[0], outs[1], outs[2], outs[3], outs[4]


def _rows_wait(tag, send_sems, recv_sems, block, land, after):
    r, d = block.shape

    def body(src, land_, send_sems_, recv_sems_, after_ref, src_thru, land_thru):
        x, y, c = _me()
        for k, peer in enumerate(_peers(x, y, c)):
            cp = _rcopy(src, land_.at[_dev_index(*peer)], send_sems_, recv_sems_, k, peer)
            cp.wait_send()
            cp.wait_recv()

    outs = pl.pallas_call(
        body, name="rows_wait_" + tag,
        in_specs=[HBM, HBM, SEM, SEM, ANY],
        out_specs=[HBM, HBM],
        out_shape=[pltpu.HBM((r, d), F32), pltpu.HBM((N_DEV, r, d), F32)],
        input_output_aliases={0: 0, 1: 1},
        compiler_params=pltpu.CompilerParams(has_side_effects=EFFECT),
    )(block, land, send_sems, recv_sems, after)
    me = _dev_index(*_me())
    return lax.dynamic_update_slice(outs[1], outs[0][None], (me, 0, 0))


ADA_ROWS = 80
ADA_W = 6 * D_MODEL // N_SHARD


def _ada_forward(c_block, cctx_block, w_ada, b_shard):
    d = c_block.shape[1]

    def body(c_ref, cc_ref, w_ref, b_ref, act_ref, mods_ref, raw, mloc, send_sems, recv_sems):
        x, y, c = _me()
        me = _dev_index(x, y, c)
        s_me = _shard_of((x, y))
        raw[72:ADA_ROWS, :] = jnp.zeros((ADA_ROWS - 72, d), F32)
        raw[pl.ds(pl.multiple_of(me * 8, 8), 8), :] = c_ref[...]
        raw[64:72, :] = cc_ref[...]
        sends = []
        for mask in range(1, N_DEV):
            peer = (x ^ (mask >> 2), y ^ ((mask >> 1) & 1), c ^ (mask & 1))
            cp = _rcopy(c_ref, raw.at[pl.ds(pl.multiple_of(me * 8, 8), 8), :], send_sems, recv_sems, mask - 1, peer)
            cp.start()
            sends.append(cp)
        for mask in range(1, N_DEV):
            peer = (x ^ (mask >> 2), y ^ ((mask >> 1) & 1), c ^ (mask & 1))
            landed = raw.at[pl.ds(pl.multiple_of(_dev_index(*peer) * 8, 8), 8), :]
            _rcopy(landed, landed, send_sems, recv_sems, mask - 1, peer).wait_recv()
        v = raw[...]
        act = v * jax.nn.sigmoid(v)
        act_ref[...] = act
        mloc[...] = lax.dot_general(act.astype(BF16), w_ref[...].astype(BF16), NN,
                                    preferred_element_type=F32) + b_ref[...]
        mods_ref[s_me, 0:8, :] = mloc[pl.ds(pl.multiple_of(me * 8, 8), 8), :]
        mods_ref[s_me, 8:16, :] = mloc[64:72, :]
        base = N_DEV - 1
        for j, chip in enumerate(_other_chips(x, y)):
            peer = (*chip, c)
            rows = mloc.at[pl.ds(pl.multiple_of(_dev_index(*peer) * 8, 8), 8), :]
            cp = _rcopy(rows, mods_ref.at[s_me, 0:8, :], send_sems, recv_sems, base + 2 * j, peer)
            cp.start()
            sends.append(cp)
            cp = _rcopy(mloc.at[64:72, :], mods_ref.at[s_me, 8:16, :], send_sems, recv_sems, base + 2 * j + 1, peer)
            cp.start()
            sends.append(cp)
        for j, chip in enumerate(_other_chips(x, y)):
            for part in range(2):
                landed = mods_ref.at[_shard_of(chip), 8 * part:8 * part + 8, :]
                _rcopy(landed, landed, send_sems, recv_sems, base + 2 * j + part, (*chip, c)).wait_recv()
        for cp in sends:
            cp.wait_send()

    n_sem = N_DEV - 1 + 2 * N_CHIP_PEERS
    return pl.pallas_call(
        body, name="ada_forward",
        in_specs=[VMEM_FULL] * 4, out_specs=[VMEM_FULL, VMEM_FULL],
        out_shape=[jax.ShapeDtypeStruct((ADA_ROWS, d), F32), jax.ShapeDtypeStruct((N_SHARD, 16, ADA_W), F32)],
        scratch_shapes=[pltpu.VMEM((ADA_ROWS, d), F32), pltpu.VMEM((ADA_ROWS, ADA_W), F32),
                        pltpu.SemaphoreType.DMA((n_sem,)), pltpu.SemaphoreType.DMA((n_sem,))],
        compiler_params=pltpu.CompilerParams(vmem_limit_bytes=VMEM_LIMIT),
    )(c_block, cctx_block, w_ada, b_shard)


def _small_reduce(gathered):
    d = gathered.shape[2]

    def body(g_ref, o_ref):
        tot = g_ref[0]
        for i in range(1, N_DEV):
            tot = tot + g_ref[i]
        o_ref[...] = tot
        o_ref[0:2, :] = tot[0:2] + tot[6:8]
        o_ref[12:13, :] = jnp.broadcast_to(jnp.sum(tot[12:13], axis=1, keepdims=True), (1, d))

    return pl.pallas_call(body, name="small_reduce", in_specs=[VMEM_FULL], out_specs=VMEM_FULL,
                          out_shape=jax.ShapeDtypeStruct((16, d), F32))(gathered)


def _cctx_grad(gathered, c_ctx):
    d = gathered.shape[2]

    def body(g_ref, c_ref, o_ref):
        tot = g_ref[0, 0:1, :]
        for chip in range(1, N_SHARD):
            tot = tot + g_ref[2 * chip, 0:1, :]
        v = c_ref[...]
        sig = jax.nn.sigmoid(v)
        o_ref[...] = tot * (sig * (1.0 + v * (1.0 - sig)))

    return pl.pallas_call(body, name="cctx_grad", in_specs=[VMEM_FULL, VMEM_FULL], out_specs=VMEM_FULL,
                          out_shape=jax.ShapeDtypeStruct((1, d), F32))(gathered, c_ctx.reshape(1, d))


def _cast_into_full(w, shard, full, axis, name):
    r, cdim = w.shape
    tr = _fit(r, ROW_TILE)
    nbr = r // tr

    def body(s_ref, w_ref, o_ref):
        o_ref[...] = w_ref[...].astype(BF16)

    if axis == 0:
        out_spec = pl.BlockSpec((tr, cdim), lambda i, s: (s[0] * nbr + i, 0))
    else:
        out_spec = pl.BlockSpec((tr, cdim), lambda i, s: (i, s[0]))
    return pl.pallas_call(
        body, name=name,
        grid_spec=pltpu.PrefetchScalarGridSpec(
            num_scalar_prefetch=1, grid=(nbr,), in_specs=[pl.BlockSpec((tr, cdim), lambda i, s: (i, 0))],
            out_specs=out_spec),
        out_shape=jax.ShapeDtypeStruct(full, BF16), compiler_params=_params(("parallel",)),
    )(shard.reshape(1).astype(jnp.int32), w)


def _adamw_halves(w, g_own, g_other, m, v, core, axis, name):
    r, cdim = w.shape
    hr, hc = (r // 2, cdim) if axis == 1 else (r, cdim // 2)
    assert g_own.shape == (hr, hc) and g_other.shape == (hr, hc)
    tr = _fit(hr, 256)
    nb = hr // tr
    c1 = 1.0 - ADAM_B1 ** ADAM_STEP
    c2 = 1.0 - ADAM_B2 ** ADAM_STEP

    def body(c_ref, w_ref, go_ref, gt_ref, m_ref, v_ref, g_ref, d_ref, nm_ref, nv_ref):
        gv = jnp.where(pl.program_id(0) == c_ref[0], go_ref[...], gt_ref[...])
        nm = ADAM_B1 * m_ref[...] + (1.0 - ADAM_B1) * gv
        nv = ADAM_B2 * v_ref[...] + (1.0 - ADAM_B2) * (gv * gv)
        g_ref[...] = gv
        nm_ref[...] = nm
        nv_ref[...] = nv
        d_ref[...] = -ADAM_LR * ((nm / c1) / (jnp.sqrt(nv / c2) + ADAM_EPS) + ADAM_WD * w_ref[...])

    if axis == 1:
        big = pl.BlockSpec((tr, hc), lambda p, i, c: (p * nb + i, 0))
    else:
        big = pl.BlockSpec((tr, hc), lambda p, i, c: (i, p))
    own = pl.BlockSpec((tr, hc), lambda p, i, c: (jnp.where(p == c[0], i, 0), 0))
    other = pl.BlockSpec((tr, hc), lambda p, i, c: (jnp.where(p == c[0], 0, i), 0))
    sh = jax.ShapeDtypeStruct((r, cdim), F32)
    return pl.pallas_call(
        body, name=name,
        grid_spec=pltpu.PrefetchScalarGridSpec(
            num_scalar_prefetch=1, grid=(2, nb), in_specs=[big, own, other, big, big], out_specs=[big] * 4),
        out_shape=[sh] * 4, compiler_params=_params(("parallel", "parallel")),
    )(core.reshape(1).astype(jnp.int32), w, g_own, g_other, m, v)


def _adamw(w, g, m, v, name):
    r, cdim = w.shape
    tr = _fit(r, 128) if r % (ROW_TILE // 4) == 0 else r
    c1 = 1.0 - ADAM_B1 ** ADAM_STEP
    c2 = 1.0 - ADAM_B2 ** ADAM_STEP

    def body(w_ref, g_ref, m_ref, v_ref, d_ref, nm_ref, nv_ref):
        gv = g_ref[...]
        nm = ADAM_B1 * m_ref[...] + (1.0 - ADAM_B1) * gv
        nv = ADAM_B2 * v_ref[...] + (1.0 - ADAM_B2) * (gv * gv)
        nm_ref[...] = nm
        nv_ref[...] = nv
        d_ref[...] = -ADAM_LR * ((nm / c1) / (jnp.sqrt(nv / c2) + ADAM_EPS) + ADAM_WD * w_ref[...])

    spec = pl.BlockSpec((tr, cdim), lambda i: (i, 0))
    sh = jax.ShapeDtypeStruct((r, cdim), F32)
    return pl.pallas_call(body, name=name, grid=(r // tr,), in_specs=[spec] * 4, out_specs=[spec] * 3,
                          out_shape=[sh, sh, sh], compiler_params=_params(("parallel",)))(w, g, m, v)


def _adamw_small(ws, gs, ms, vs):
    k = len(ws)
    c1 = 1.0 - ADAM_B1 ** ADAM_STEP
    c2 = 1.0 - ADAM_B2 ** ADAM_STEP

    def body(*refs):
        w_refs, g_refs, m_refs, v_refs = refs[0:k], refs[k:2 * k], refs[2 * k:3 * k], refs[3 * k:4 * k]
        d_refs, nm_refs, nv_refs = refs[4 * k:5 * k], refs[5 * k:6 * k], refs[6 * k:7 * k]
        for i in range(k):
            gv = g_refs[i][...]
            nm = ADAM_B1 * m_refs[i][...] + (1.0 - ADAM_B1) * gv
            nv = ADAM_B2 * v_refs[i][...] + (1.0 - ADAM_B2) * (gv * gv)
            nm_refs[i][...] = nm
            nv_refs[i][...] = nv
            d_refs[i][...] = -ADAM_LR * ((nm / c1) / (jnp.sqrt(nv / c2) + ADAM_EPS) + ADAM_WD * w_refs[i][...])

    shapes = [jax.ShapeDtypeStruct(w.shape, F32) for w in ws]
    outs = pl.pallas_call(body, name="adamw_small", in_specs=[VMEM_FULL] * (4 * k), out_specs=[VMEM_FULL] * (3 * k),
                          out_shape=shapes * 3)(*ws, *gs, *ms, *vs)
    return outs[0:k], outs[k:2 * k], outs[2 * k:3 * k]


SMALL = (("c_ctx", D_MODEL), ("b_ada", 6 * D_MODEL), ("q_norm_g", HEAD_DIM), ("k_norm_g", HEAD_DIM),
         ("sink_logit", HEADS_A), ("ln1_g", D_MODEL), ("ln1_b", D_MODEL), ("ln2_g", D_MODEL), ("ln2_b", D_MODEL))
WEIGHT_ORDER = ("c_ctx", "w_ada", "b_ada", "w_in", "q_norm_g", "k_norm_g", "sink_logit", "w_out", "ln1_g", "ln1_b",
                "w_gate", "w_up", "w_down", "ln2_g", "ln2_b")


def kernel(x, c, ctx, c_ctx, w_ada, b_ada, w_in, q_norm_g, k_norm_g, sink_logit, w_out, ln1_g, ln1_b, w_gate, w_up, w_down, ln2_g, ln2_b, loss_target, m_c_ctx, m_w_ada, m_b_ada, m_w_in, m_q_norm_g, m_k_norm_g, m_sink_logit, m_w_out, m_ln1_g, m_ln1_b, m_w_gate, m_w_up, m_w_down, m_ln2_g, m_ln2_b, v_c_ctx, v_w_ada, v_b_ada, v_w_in, v_q_norm_g, v_k_norm_g, v_sink_logit, v_w_out, v_ln1_g, v_ln1_b, v_w_gate, v_w_up, v_w_down, v_ln2_g, v_ln2_b):
    d = D_MODEL
    w = dict(c_ctx=c_ctx, w_ada=w_ada[0], b_ada=b_ada, w_in=w_in[0], q_norm_g=q_norm_g, k_norm_g=k_norm_g,
             sink_logit=sink_logit, w_out=w_out[0], ln1_g=ln1_g, ln1_b=ln1_b, w_gate=w_gate[0], w_up=w_up[0],
             w_down=w_down[0], ln2_g=ln2_g, ln2_b=ln2_b)
    m = dict(c_ctx=m_c_ctx, w_ada=m_w_ada[0], b_ada=m_b_ada, w_in=m_w_in[0], q_norm_g=m_q_norm_g, k_norm_g=m_k_norm_g,
             sink_logit=m_sink_logit, w_out=m_w_out[0], ln1_g=m_ln1_g, ln1_b=m_ln1_b, w_gate=m_w_gate[0],
             w_up=m_w_up[0], w_down=m_w_down[0], ln2_g=m_ln2_g, ln2_b=m_ln2_b)
    v = dict(c_ctx=v_c_ctx, w_ada=v_w_ada[0], b_ada=v_b_ada, w_in=v_w_in[0], q_norm_g=v_q_norm_g, k_norm_g=v_k_norm_g,
             sink_logit=v_sink_logit, w_out=v_w_out[0], ln1_g=v_ln1_g, ln1_b=v_ln1_b, w_gate=v_w_gate[0],
             w_up=v_w_up[0], w_down=v_w_down[0], ln2_g=v_ln2_g, ln2_b=v_ln2_b)
    mx, my, mc = _me()
    s_me = _shard_of((mx, my))
    me = _dev_index(mx, my, mc)
    pad8 = lambda row: jnp.concatenate([row.reshape(1, -1), jnp.zeros((7, row.size), F32)], axis=0)

    b_shard = lax.dynamic_slice(b_ada, (0, s_me * ADA_W), (1, ADA_W))
    act, mods4 = _ada_forward(pad8(c), pad8(c_ctx), w["w_ada"], b_shard)

    gathers = []
    prev, shard = mods4, s_me
    for k, names in enumerate(W_GROUPS):
        arrs = tuple(BIG_INDEX[name] for name in names)
        bufs = [_cast_into_full(w[name], shard, BIG[a][1], BIG[a][2], "cast_" + name) for name, a in zip(names, arrs)]
        send_sems, recv_sems, thru, prev = _gather_start("g%d" % k, arrs, bufs, prev)
        shard = s_me + prev[0, 0].astype(jnp.int32)
        gathers.append((arrs, send_sems, recv_sems, thru))

    forwards = {}

    def prefetch(k, after):
        arrs, send_sems, recv_sems, thru = gathers[k]
        landed = _gather_wait("g%d" % k, arrs, send_sems, recv_sems, thru, after)
        fwd_send, fwd_recv, landed, token = _forward_start("g%d" % k, arrs, landed)
        forwards[k] = (fwd_send, fwd_recv, landed)
        return token[0, 0]

    def weights(k, after):
        arrs, send_sems, recv_sems, thru = gathers[k]
        if k in forwards:
            return _forward_wait("g%d" % k, arrs, *forwards[k], after)
        landed = _gather_wait("g%d" % k, arrs, send_sems, recv_sems, thru, after)
        return _gather_forward("g%d" % k, arrs, landed)

    mod = jnp.transpose(mods4[:, 0:1, :], (1, 0, 2)).reshape(1, 6 * d) + prev[0, 0]
    mod_ctx = jnp.transpose(mods4[:, 8:9, :], (1, 0, 2)).reshape(1, 6 * d)

    scatters = {}

    def grads_out(k, dws):
        arrs = tuple(BIG_INDEX[name] for name in G_GROUPS[k])
        scatters[k], zero = _scatter_begin("g%d" % k, arrs, dws)
        return zero

    grad_x, partial = _layer_fwd_bwd(x[0], ctx[0], loss_target[0], mod, mod_ctx, weights, prefetch, grads_out,
                                     q_norm_g, k_norm_g, sink_logit, ln1_g, ln1_b, ln2_g, ln2_b)
    grads, delta, new_m, new_v = {}, {}, {}, {}

    p_send, p_recv, partial, p_land, after = _rows_start("partials", partial)
    joins = []
    for k, names in enumerate(G_GROUPS):
        arrs = tuple(BIG_INDEX[name] for name in names)
        state, after = _scatter_reduce("g%d" % k, arrs, scatters[k], after)
        joins.append(state)
    for k, names in enumerate(G_GROUPS):
        g_own, g_other = _scatter_end("g%d" % k, joins[k], after)
        for name, own, other in zip(names, g_own, g_other):
            grads[name], delta[name], new_m[name], new_v[name] = _adamw_halves(
                w[name], own, other, m[name], v[name], mc, BIG[BIG_INDEX[name]][2], "adamw_" + name)
            after = new_v[name]

    gathered = _rows_wait("partials", p_send, p_recv, partial, p_land, after)
    tot = _small_reduce(gathered)
    grads["b_ada"] = tot[0:6].reshape(1, 6 * d)
    grads["ln1_g"], grads["ln1_b"], grads["ln2_g"], grads["ln2_b"] = tot[8:9], tot[9:10], tot[10:11], tot[11:12]
    grads["q_norm_g"] = tot[13:14, 0:HEAD_DIM]
    grads["k_norm_g"] = tot[13:14, HEAD_DIM:2 * HEAD_DIM]
    grads["sink_logit"] = tot[13:14, 2 * HEAD_DIM:2 * HEAD_DIM + HEADS_A]
    loss = tot[12, 0]

    dm_all = gathered[:, 0:6, :].reshape(N_DEV, 6 * d)
    dmc_tot = jnp.concatenate([tot[6:8].reshape(1, 2 * d), jnp.zeros((1, 4 * d), F32)], axis=1)
    dm_rows = jnp.concatenate([pad8(dm_all[i]) for i in range(N_DEV)] + [pad8(dmc_tot), jnp.zeros((8, 6 * d), F32)], axis=0)
    dm_shard = lax.dynamic_slice(dm_rows, (0, s_me * ADA_W), (ADA_ROWS, ADA_W))
    dmc_shard = lax.dynamic_slice(pad8(dmc_tot), (0, s_me * ADA_W), (8, ADA_W))
    cc_part = _matmul(dmc_shard, w["w_ada"], name="d_cctx", tb=True, tm=8, tn=1024, tk=1536, out_dtype=F32)
    c_send, c_recv, cc_part, c_land, c_token = _rows_start("cctx", cc_part)
    grads["w_ada"] = _matmul(act, dm_shard, name="dw_ada", ta=True, tm=1024, tn=1024, tk=ADA_ROWS, out_dtype=F32,
                             after=c_token)
    delta["w_ada"], new_m["w_ada"], new_v["w_ada"] = _adamw(w["w_ada"], grads["w_ada"], m["w_ada"], v["w_ada"],
                                                            "adamw_w_ada")
    gathered_cc = _rows_wait("cctx", c_send, c_recv, cc_part, c_land, new_v["w_ada"])
    grads["c_ctx"] = _cctx_grad(gathered_cc, c_ctx).reshape(d)
    rows = lambda t: [t[name].reshape(1, size) for name, size in SMALL]
    small = _adamw_small(rows(w), rows(grads), rows(m), rows(v))
    for k, (name, size) in enumerate(SMALL):
        delta[name], new_m[name], new_v[name] = [t[k].reshape(w[name].shape) for t in small]
        grads[name] = grads[name].reshape(w[name].shape)

    lead = lambda name, t: t[None] if name in ("w_ada", "w_in", "w_out", "w_gate", "w_up", "w_down") else t
    outs = [loss, grad_x[None]]
    for group in (grads, delta, new_m, new_v):
        outs += [lead(name, group[name]) for name in WEIGHT_ORDER]
    return tuple(outs)
```
